```python
import jax, jax.numpy as jnp
from jax import lax
import numpy as np


D_MODEL = 1024
BATCH = 8
SEQ = 4096
DEPTH = 2

N_A_LAYERS = DEPTH // 2
N_B_LAYERS = DEPTH - N_A_LAYERS
HG_EXPAND = 128
HG_HEADS = D_MODEL // HG_EXPAND
HG_DV = D_MODEL // HG_HEADS
HG_CHUNK = 32
ATT_HEAD_DIM = 64
ATT_Q_HEADS = D_MODEL // ATT_HEAD_DIM
ATT_KV_HEADS = 2
ATT_GROUP = ATT_Q_HEADS // ATT_KV_HEADS
WINDOW = 128
D_FF = 2816
CONV_WIDTH = 3
EPS = 1e-6

kernel_name = 'yoco_hgrn2_swa_sink_alibi_convffn'

F32 = jnp.float32


def rms_norm(x, g):
    xf = x.astype(F32)
    xf = xf * lax.rsqrt(jnp.mean(xf * xf, axis=-1, keepdims=True) + EPS)
    return (xf * g.astype(F32)).astype(x.dtype)


def alibi_slopes(n_heads):
    return jnp.asarray(2.0 ** (-8.0 * np.arange(1, n_heads + 1) / n_heads), F32)


def hgrn2_chunked(q, k, v, logf):
    b_, s_, h_, dk = q.shape
    dv = v.shape[-1]
    n_chunks = s_ // HG_CHUNK

    def to_chunks(t):
        return t.reshape(b_, n_chunks, HG_CHUNK, h_, t.shape[-1]).transpose(1, 0, 3, 2, 4)

    qc, kc, vc, gc = to_chunks(q), to_chunks(k), to_chunks(v), to_chunks(logf)
    causal = jnp.tril(jnp.ones((HG_CHUNK, HG_CHUNK), bool))[:, :, None]

    def step(state, inp):
        qb, kb, vb, gb = inp
        cum = jnp.cumsum(gb, axis=2)
        o_inter = jnp.einsum('bhtk,bhkv->bhtv', qb * jnp.exp(cum), state)
        rel = cum[:, :, :, None, :] - cum[:, :, None, :, :]
        decay = jnp.exp(jnp.where(causal, rel, -jnp.inf))
        scores = jnp.einsum('bhtk,bhsk,bhtsk->bhts', qb, kb, decay)
        o_intra = jnp.einsum('bhts,bhsv->bhtv', scores, vb)
        last = cum[:, :, -1:, :]
        new_state = (jnp.exp(last[:, :, 0, :])[..., None] * state
                     + jnp.einsum('bhsk,bhsv->bhkv', kb * jnp.exp(last - cum), vb))
        return new_state, o_inter + o_intra

    s0 = jnp.zeros((b_, h_, dk, dv), F32)
    _, o = lax.scan(step, s0, (qc, kc, vc, gc))
    return o.transpose(1, 0, 3, 2, 4).reshape(b_, s_, h_, dv)


def hgrn2_mixer(x, w_in, lower_bound, out_norm, w_out):
    b_, s_, _ = x.shape
    q, f, i, g = jnp.split(x @ w_in, 4, axis=-1)
    q = jax.nn.silu(q.astype(F32)) * HG_EXPAND ** -0.5
    forget = lower_bound + (1.0 - lower_bound) * jax.nn.sigmoid(f.astype(F32))
    logf = jnp.log(forget)
    k = 1.0 - forget
    heads = lambda t: t.reshape(b_, s_, HG_HEADS, -1)
    o = hgrn2_chunked(heads(q), heads(k), heads(i.astype(F32)), heads(logf))
    o = rms_norm(o, out_norm) * jax.nn.silu(heads(g.astype(F32)))
    return o.reshape(b_, s_, D_MODEL).astype(x.dtype) @ w_out


def shared_kv(h, kv_norm, w_kv):
    b_, s_, _ = h.shape
    k, v = jnp.split(rms_norm(h, kv_norm) @ w_kv, 2, axis=-1)
    return (k.reshape(b_, s_, ATT_KV_HEADS, ATT_HEAD_DIM),
            v.reshape(b_, s_, ATT_KV_HEADS, ATT_HEAD_DIM))


def swa_sink_attention(x, k, v, w_q, sinks, w_o):
    b_, s_, _ = x.shape
    nb = s_ // WINDOW
    q = (x @ w_q).reshape(b_, nb, WINDOW, ATT_KV_HEADS, ATT_GROUP, ATT_HEAD_DIM)

    def band(t):
        tb = t.reshape(b_, nb, WINDOW, ATT_KV_HEADS, ATT_HEAD_DIM)
        prev = jnp.pad(tb[:, :-1], ((0, 0), (1, 0), (0, 0), (0, 0), (0, 0)))
        return jnp.concatenate([prev, tb], axis=2)

    kb, vb = band(k), band(v)
    scores = jnp.einsum('bnqkgd,bnskd->bnkgqs', q.astype(F32), kb.astype(F32)) * ATT_HEAD_DIM ** -0.5
    q_idx = jnp.arange(WINDOW)[:, None] + WINDOW
    k_idx = jnp.arange(2 * WINDOW)[None, :]
    dist = q_idx - k_idx
    key_abs = (jnp.arange(nb) * WINDOW)[:, None] + jnp.arange(2 * WINDOW)[None, :] - WINDOW
    valid = ((dist >= 0) & (dist < WINDOW))[None] & (key_abs >= 0)[:, None, :]
    slopes = alibi_slopes(ATT_Q_HEADS).reshape(ATT_KV_HEADS, ATT_GROUP)
    scores = scores - slopes[:, :, None, None] * dist.astype(F32)
    scores = jnp.where(valid[None, :, None, None], scores, -jnp.inf)
    sink = sinks.astype(F32).reshape(ATT_KV_HEADS, ATT_GROUP)[None, None, :, :, None, None]
    m = jnp.maximum(jnp.max(scores, axis=-1, keepdims=True), sink)
    e = jnp.exp(scores - m)
    probs = e / (jnp.sum(e, axis=-1, keepdims=True) + jnp.exp(sink - m))
    out = jnp.einsum('bnkgqs,bnskd->bnqkgd', probs, vb.astype(F32))
    return out.reshape(b_, s_, ATT_Q_HEADS * ATT_HEAD_DIM).astype(x.dtype) @ w_o


def conv_ffn(x, w_up, conv_w, conv_b, w_down):
    s_ = x.shape[1]
    gate, val = jnp.split(x @ w_up, 2, axis=-1)
    gp = jnp.pad(gate, ((0, 0), (CONV_WIDTH - 1, 0), (0, 0)))
    conv = conv_b
    for j in range(CONV_WIDTH):
        conv = conv + conv_w[j] * gp[:, j:j + s_]
    return (jax.nn.silu(conv) * val) @ w_down


def _fwd_setup_inputs(seed: int = 0) -> dict:
    key = jax.random.key(seed)
    ks = jax.random.split(key, 18)
    D = D_MODEL
    HQD = ATT_Q_HEADS * ATT_HEAD_DIM
    KVD = ATT_KV_HEADS * ATT_HEAD_DIM

    def w(k, shape, fan_in):
        return jax.random.normal(k, shape, F32) * fan_in ** -0.5

    def gain(k, shape):
        return 1.0 + 0.02 * jax.random.normal(k, shape, F32)

    return {
        'x': jax.random.normal(ks[0], (BATCH, SEQ, D), F32),
        'hg_norm': gain(ks[1], (N_A_LAYERS, D)),
        'hg_w_in': w(ks[2], (N_A_LAYERS, D, 4 * D), D),
        'hg_lb_logits': 0.1 * jax.random.normal(ks[3], (N_A_LAYERS + 1, D), F32),
        'hg_out_norm': gain(ks[4], (N_A_LAYERS, HG_DV)),
        'hg_w_out': w(ks[5], (N_A_LAYERS, D, D), D),
        'kv_norm': gain(ks[6], (D,)),
        'w_kv': w(ks[7], (D, 2 * KVD), D),
        'attn_norm': gain(ks[8], (N_B_LAYERS, D)),
        'attn_w_q': w(ks[9], (N_B_LAYERS, D, HQD), D),
        'attn_sinks': 0.5 * jax.random.normal(ks[10], (N_B_LAYERS, ATT_Q_HEADS), F32),
        'attn_w_o': w(ks[11], (N_B_LAYERS, HQD, D), HQD),
        'ffn_norm': gain(ks[12], (DEPTH, D)),
        'ffn_w_up': w(ks[13], (DEPTH, D, 2 * D_FF), D),
        'ffn_conv_w': w(ks[14], (DEPTH, CONV_WIDTH, D_FF), CONV_WIDTH),
        'ffn_conv_b': 0.02 * jax.random.normal(ks[15], (DEPTH, D_FF), F32),
        'ffn_w_down': w(ks[16], (DEPTH, D_FF, D), D_FF),
        'final_norm': gain(ks[17], (D,)),
    }


def _fwd_reference(x, hg_norm, hg_w_in, hg_lb_logits, hg_out_norm, hg_w_out, kv_norm, w_kv,
              attn_norm, attn_w_q, attn_sinks, attn_w_o, ffn_norm, ffn_w_up, ffn_conv_w,
              ffn_conv_b, ffn_w_down, final_norm):
    lower_bounds = jnp.cumsum(jax.nn.softmax(hg_lb_logits.astype(F32), axis=0), axis=0)
    h = x
    k_sh, v_sh = None, None
    for layer in range(DEPTH):
        if layer < N_A_LAYERS:
            a = layer
            h = h + hgrn2_mixer(rms_norm(h, hg_norm[a]), hg_w_in[a], lower_bounds[a],
                                hg_out_norm[a], hg_w_out[a])
        else:
            bi = layer - N_A_LAYERS
            if bi == 0:
                k_sh, v_sh = shared_kv(h, kv_norm, w_kv)
            h = h + swa_sink_attention(rms_norm(h, attn_norm[bi]), k_sh, v_sh,
                                       attn_w_q[bi], attn_sinks[bi], attn_w_o[bi])
        h = h + conv_ffn(rms_norm(h, ffn_norm[layer]), ffn_w_up[layer], ffn_conv_w[layer],
                         ffn_conv_b[layer], ffn_w_down[layer])
    return rms_norm(h, final_norm)


import jax as _jax
import jax.numpy as _jnp

TWIN_FORMAT = 'train_step'
FWD_PARAMS = ['x', 'hg_norm', 'hg_w_in', 'hg_lb_logits', 'hg_out_norm', 'hg_w_out', 'kv_norm', 'w_kv', 'attn_norm', 'attn_w_q', 'attn_sinks', 'attn_w_o', 'ffn_norm', 'ffn_w_up', 'ffn_conv_w', 'ffn_conv_b', 'ffn_w_down', 'final_norm']
TWIN_WEIGHTS = ['hg_norm', 'hg_w_in', 'hg_lb_logits', 'hg_out_norm', 'hg_w_out', 'kv_norm', 'w_kv', 'attn_norm', 'attn_w_q', 'attn_sinks', 'attn_w_o', 'ffn_norm', 'ffn_w_up', 'ffn_conv_w', 'ffn_conv_b', 'ffn_w_down', 'final_norm']
TWIN_DIFF_INPUT = 'x'
TWIN_INPUTS = ['x', 'hg_norm', 'hg_w_in', 'hg_lb_logits', 'hg_out_norm', 'hg_w_out', 'kv_norm', 'w_kv', 'attn_norm', 'attn_w_q', 'attn_sinks', 'attn_w_o', 'ffn_norm', 'ffn_w_up', 'ffn_conv_w', 'ffn_conv_b', 'ffn_w_down', 'final_norm', 'loss_target', 'm_hg_norm', 'm_hg_w_in', 'm_hg_lb_logits', 'm_hg_out_norm', 'm_hg_w_out', 'm_kv_norm', 'm_w_kv', 'm_attn_norm', 'm_attn_w_q', 'm_attn_sinks', 'm_attn_w_o', 'm_ffn_norm', 'm_ffn_w_up', 'm_ffn_conv_w', 'm_ffn_conv_b', 'm_ffn_w_down', 'm_final_norm', 'v_hg_norm', 'v_hg_w_in', 'v_hg_lb_logits', 'v_hg_out_norm', 'v_hg_w_out', 'v_kv_norm', 'v_w_kv', 'v_attn_norm', 'v_attn_w_q', 'v_attn_sinks', 'v_attn_w_o', 'v_ffn_norm', 'v_ffn_w_up', 'v_ffn_conv_w', 'v_ffn_conv_b', 'v_ffn_w_down', 'v_final_norm']
TWIN_OUTPUTS = ['loss', 'grad_x', 'grad_hg_norm', 'grad_hg_w_in', 'grad_hg_lb_logits', 'grad_hg_out_norm', 'grad_hg_w_out', 'grad_kv_norm', 'grad_w_kv', 'grad_attn_norm', 'grad_attn_w_q', 'grad_attn_sinks', 'grad_attn_w_o', 'grad_ffn_norm', 'grad_ffn_w_up', 'grad_ffn_conv_w', 'grad_ffn_conv_b', 'grad_ffn_w_down', 'grad_final_norm', 'delta_hg_norm', 'delta_hg_w_in', 'delta_hg_lb_logits', 'delta_hg_out_norm', 'delta_hg_w_out', 'delta_kv_norm', 'delta_w_kv', 'delta_attn_norm', 'delta_attn_w_q', 'delta_attn_sinks', 'delta_attn_w_o', 'delta_ffn_norm', 'delta_ffn_w_up', 'delta_ffn_conv_w', 'delta_ffn_conv_b', 'delta_ffn_w_down', 'delta_final_norm', 'new_m_hg_norm', 'new_m_hg_w_in', 'new_m_hg_lb_logits', 'new_m_hg_out_norm', 'new_m_hg_w_out', 'new_m_kv_norm', 'new_m_w_kv', 'new_m_attn_norm', 'new_m_attn_w_q', 'new_m_attn_sinks', 'new_m_attn_w_o', 'new_m_ffn_norm', 'new_m_ffn_w_up', 'new_m_ffn_conv_w', 'new_m_ffn_conv_b', 'new_m_ffn_w_down', 'new_m_final_norm', 'new_v_hg_norm', 'new_v_hg_w_in', 'new_v_hg_lb_logits', 'new_v_hg_out_norm', 'new_v_hg_w_out', 'new_v_kv_norm', 'new_v_w_kv', 'new_v_attn_norm', 'new_v_attn_w_q', 'new_v_attn_sinks', 'new_v_attn_w_o', 'new_v_ffn_norm', 'new_v_ffn_w_up', 'new_v_ffn_conv_w', 'new_v_ffn_conv_b', 'new_v_ffn_w_down', 'new_v_final_norm']
TWIN_LEAF_KINDS = {'loss': 'loss', 'grad_x': 'grad_x', 'grad_hg_norm': 'grad_w', 'grad_hg_w_in': 'grad_w', 'grad_hg_lb_logits': 'grad_w', 'grad_hg_out_norm': 'grad_w', 'grad_hg_w_out': 'grad_w', 'grad_kv_norm': 'grad_w', 'grad_w_kv': 'grad_w', 'grad_attn_norm': 'grad_w', 'grad_attn_w_q': 'grad_w', 'grad_attn_sinks': 'grad_w', 'grad_attn_w_o': 'grad_w', 'grad_ffn_norm': 'grad_w', 'grad_ffn_w_up': 'grad_w', 'grad_ffn_conv_w': 'grad_w', 'grad_ffn_conv_b': 'grad_w', 'grad_ffn_w_down': 'grad_w', 'grad_final_norm': 'grad_w', 'delta_hg_norm': 'delta_w', 'delta_hg_w_in': 'delta_w', 'delta_hg_lb_logits': 'delta_w', 'delta_hg_out_norm': 'delta_w', 'delta_hg_w_out': 'delta_w', 'delta_kv_norm': 'delta_w', 'delta_w_kv': 'delta_w', 'delta_attn_norm': 'delta_w', 'delta_attn_w_q': 'delta_w', 'delta_attn_sinks': 'delta_w', 'delta_attn_w_o': 'delta_w', 'delta_ffn_norm': 'delta_w', 'delta_ffn_w_up': 'delta_w', 'delta_ffn_conv_w': 'delta_w', 'delta_ffn_conv_b': 'delta_w', 'delta_ffn_w_down': 'delta_w', 'delta_final_norm': 'delta_w', 'new_m_hg_norm': 'new_m', 'new_m_hg_w_in': 'new_m', 'new_m_hg_lb_logits': 'new_m', 'new_m_hg_out_norm': 'new_m', 'new_m_hg_w_out': 'new_m', 'new_m_kv_norm': 'new_m', 'new_m_w_kv': 'new_m', 'new_m_attn_norm': 'new_m', 'new_m_attn_w_q': 'new_m', 'new_m_attn_sinks': 'new_m', 'new_m_attn_w_o': 'new_m', 'new_m_ffn_norm': 'new_m', 'new_m_ffn_w_up': 'new_m', 'new_m_ffn_conv_w': 'new_m', 'new_m_ffn_conv_b': 'new_m', 'new_m_ffn_w_down': 'new_m', 'new_m_final_norm': 'new_m', 'new_v_hg_norm': 'new_v', 'new_v_hg_w_in': 'new_v', 'new_v_hg_lb_logits': 'new_v', 'new_v_hg_out_norm': 'new_v', 'new_v_hg_w_out': 'new_v', 'new_v_kv_norm': 'new_v', 'new_v_w_kv': 'new_v', 'new_v_attn_norm': 'new_v', 'new_v_attn_w_q': 'new_v', 'new_v_attn_sinks': 'new_v', 'new_v_attn_w_o': 'new_v', 'new_v_ffn_norm': 'new_v', 'new_v_ffn_w_up': 'new_v', 'new_v_ffn_conv_w': 'new_v', 'new_v_ffn_conv_b': 'new_v', 'new_v_ffn_w_down': 'new_v', 'new_v_final_norm': 'new_v'}


def _forward(args):
    return _fwd_reference(*[args[k] for k in FWD_PARAMS])


def _output_shape():
    out = _jax.eval_shape(lambda: _forward(_fwd_setup_inputs(0)))
    return out.shape, out.dtype

N_MICROBATCH = 1
ADAM_LR = 0.001
ADAM_B1 = 0.9
ADAM_B2 = 0.999
ADAM_EPS = 1e-08
ADAM_WD = 0.01
ADAM_STEP = 10
PER_EXAMPLE_BATCH_AXIS = {'x': 0, 'loss_target': 0}
SHARED_INPUTS = []
_WEIGHT_DTYPES = {'hg_norm': _jnp.float32, 'hg_w_in': _jnp.float32, 'hg_lb_logits': _jnp.float32, 'hg_out_norm': _jnp.float32, 'hg_w_out': _jnp.float32, 'kv_norm': _jnp.float32, 'w_kv': _jnp.float32, 'attn_norm': _jnp.float32, 'attn_w_q': _jnp.float32, 'attn_sinks': _jnp.float32, 'attn_w_o': _jnp.float32, 'ffn_norm': _jnp.float32, 'ffn_w_up': _jnp.float32, 'ffn_conv_w': _jnp.float32, 'ffn_conv_b': _jnp.float32, 'ffn_w_down': _jnp.float32, 'final_norm': _jnp.float32}
MOMENT_SCALE = {'hg_norm': 1.659103e-01, 'hg_w_in': 8.175690e-02, 'hg_lb_logits': 1.110561e-02, 'hg_out_norm': 3.863548e-01, 'hg_w_out': 1.126677e-01, 'kv_norm': 5.650929e-02, 'w_kv': 1.130546e-01, 'attn_norm': 3.417247e-02, 'attn_w_q': 3.364065e-02, 'attn_sinks': 6.377624e-02, 'attn_w_o': 4.513145e-02, 'ffn_norm': 1.224084e-01, 'ffn_w_up': 5.119153e-02, 'ffn_conv_w': 5.307401e-02, 'ffn_conv_b': 5.085150e-02, 'ffn_w_down': 8.328182e-02, 'final_norm': 3.200870e+01}


def _to_microbatches(a, axis):
    t = _jnp.moveaxis(a, axis, 0)
    t = t.reshape((N_MICROBATCH, t.shape[0] // N_MICROBATCH) + t.shape[1:])
    return _jnp.moveaxis(t, 1, axis + 1)


def setup_inputs(seed: int = 0) -> dict:
    inp = _fwd_setup_inputs(seed)
    key = _jax.random.fold_in(_jax.random.key(seed), 7919)
    shape, _ = _output_shape()
    out = dict(inp)
    out["loss_target"] = _jax.random.normal(_jax.random.fold_in(key, 0), shape, _jnp.float32)
    for i, name in enumerate(TWIN_WEIGHTS):
        w = inp[name].astype(_jnp.float32)
        if MOMENT_SCALE is None:
            s = _jnp.sqrt(_jnp.mean(_jnp.square(w)) + 1e-30)
        else:
            s = MOMENT_SCALE[name]
        km, kv = _jax.random.split(_jax.random.fold_in(key, i + 1))
        out[name] = w
        out["m_" + name] = s * _jax.random.normal(km, w.shape, _jnp.float32)
        out["v_" + name] = (s * s) * _jax.random.uniform(kv, w.shape, _jnp.float32, 0.5, 1.5)
    if N_MICROBATCH > 1:
        for name, axis in PER_EXAMPLE_BATCH_AXIS.items():
            out[name] = _to_microbatches(out[name], axis)
    return {'x': out['x'], 'hg_norm': out['hg_norm'], 'hg_w_in': out['hg_w_in'], 'hg_lb_logits': out['hg_lb_logits'], 'hg_out_norm': out['hg_out_norm'], 'hg_w_out': out['hg_w_out'], 'kv_norm': out['kv_norm'], 'w_kv': out['w_kv'], 'attn_norm': out['attn_norm'], 'attn_w_q': out['attn_w_q'], 'attn_sinks': out['attn_sinks'], 'attn_w_o': out['attn_w_o'], 'ffn_norm': out['ffn_norm'], 'ffn_w_up': out['ffn_w_up'], 'ffn_conv_w': out['ffn_conv_w'], 'ffn_conv_b': out['ffn_conv_b'], 'ffn_w_down': out['ffn_w_down'], 'final_norm': out['final_norm'], 'loss_target': out['loss_target'], 'm_hg_norm': out['m_hg_norm'], 'm_hg_w_in': out['m_hg_w_in'], 'm_hg_lb_logits': out['m_hg_lb_logits'], 'm_hg_out_norm': out['m_hg_out_norm'], 'm_hg_w_out': out['m_hg_w_out'], 'm_kv_norm': out['m_kv_norm'], 'm_w_kv': out['m_w_kv'], 'm_attn_norm': out['m_attn_norm'], 'm_attn_w_q': out['m_attn_w_q'], 'm_attn_sinks': out['m_attn_sinks'], 'm_attn_w_o': out['m_attn_w_o'], 'm_ffn_norm': out['m_ffn_norm'], 'm_ffn_w_up': out['m_ffn_w_up'], 'm_ffn_conv_w': out['m_ffn_conv_w'], 'm_ffn_conv_b': out['m_ffn_conv_b'], 'm_ffn_w_down': out['m_ffn_w_down'], 'm_final_norm': out['m_final_norm'], 'v_hg_norm': out['v_hg_norm'], 'v_hg_w_in': out['v_hg_w_in'], 'v_hg_lb_logits': out['v_hg_lb_logits'], 'v_hg_out_norm': out['v_hg_out_norm'], 'v_hg_w_out': out['v_hg_w_out'], 'v_kv_norm': out['v_kv_norm'], 'v_w_kv': out['v_w_kv'], 'v_attn_norm': out['v_attn_norm'], 'v_attn_w_q': out['v_attn_w_q'], 'v_attn_sinks': out['v_attn_sinks'], 'v_attn_w_o': out['v_attn_w_o'], 'v_ffn_norm': out['v_ffn_norm'], 'v_ffn_w_up': out['v_ffn_w_up'], 'v_ffn_conv_w': out['v_ffn_conv_w'], 'v_ffn_conv_b': out['v_ffn_conv_b'], 'v_ffn_w_down': out['v_ffn_w_down'], 'v_final_norm': out['v_final_norm']}


def _loss(weights, diff, rest, loss_target):
    with _jax.named_scope("forward"):
        args = {**rest, TWIN_DIFF_INPUT: diff, **{k: w.astype(_WEIGHT_DTYPES[k]) for k, w in weights.items()}}
        y = _forward(args)
    with _jax.named_scope("loss_head"):
        err = _jnp.square(y.astype(_jnp.float32) - loss_target)
        return 0.5 * _jnp.sum(_jnp.mean(err, axis=-1)) if err.ndim else 0.5 * err


def _adamw(w, g, m, v):
    m = ADAM_B1 * m + (1.0 - ADAM_B1) * g
    v = ADAM_B2 * v + (1.0 - ADAM_B2) * _jnp.square(g)
    m_hat = m / (1.0 - ADAM_B1 ** ADAM_STEP)
    v_hat = v / (1.0 - ADAM_B2 ** ADAM_STEP)
    delta = -ADAM_LR * (m_hat / (_jnp.sqrt(v_hat) + ADAM_EPS) + ADAM_WD * w)
    return delta, m, v


def reference(x, hg_norm, hg_w_in, hg_lb_logits, hg_out_norm, hg_w_out, kv_norm, w_kv, attn_norm, attn_w_q, attn_sinks, attn_w_o, ffn_norm, ffn_w_up, ffn_conv_w, ffn_conv_b, ffn_w_down, final_norm, loss_target, m_hg_norm, m_hg_w_in, m_hg_lb_logits, m_hg_out_norm, m_hg_w_out, m_kv_norm, m_w_kv, m_attn_norm, m_attn_w_q, m_attn_sinks, m_attn_w_o, m_ffn_norm, m_ffn_w_up, m_ffn_conv_w, m_ffn_conv_b, m_ffn_w_down, m_final_norm, v_hg_norm, v_hg_w_in, v_hg_lb_logits, v_hg_out_norm, v_hg_w_out, v_kv_norm, v_w_kv, v_attn_norm, v_attn_w_q, v_attn_sinks, v_attn_w_o, v_ffn_norm, v_ffn_w_up, v_ffn_conv_w, v_ffn_conv_b, v_ffn_w_down, v_final_norm):
    given = dict(x=x, hg_norm=hg_norm, hg_w_in=hg_w_in, hg_lb_logits=hg_lb_logits, hg_out_norm=hg_out_norm, hg_w_out=hg_w_out, kv_norm=kv_norm, w_kv=w_kv, attn_norm=attn_norm, attn_w_q=attn_w_q, attn_sinks=attn_sinks, attn_w_o=attn_w_o, ffn_norm=ffn_norm, ffn_w_up=ffn_w_up, ffn_conv_w=ffn_conv_w, ffn_conv_b=ffn_conv_b, ffn_w_down=ffn_w_down, final_norm=final_norm, loss_target=loss_target, m_hg_norm=m_hg_norm, m_hg_w_in=m_hg_w_in, m_hg_lb_logits=m_hg_lb_logits, m_hg_out_norm=m_hg_out_norm, m_hg_w_out=m_hg_w_out, m_kv_norm=m_kv_norm, m_w_kv=m_w_kv, m_attn_norm=m_attn_norm, m_attn_w_q=m_attn_w_q, m_attn_sinks=m_attn_sinks, m_attn_w_o=m_attn_w_o, m_ffn_norm=m_ffn_norm, m_ffn_w_up=m_ffn_w_up, m_ffn_conv_w=m_ffn_conv_w, m_ffn_conv_b=m_ffn_conv_b, m_ffn_w_down=m_ffn_w_down, m_final_norm=m_final_norm, v_hg_norm=v_hg_norm, v_hg_w_in=v_hg_w_in, v_hg_lb_logits=v_hg_lb_logits, v_hg_out_norm=v_hg_out_norm, v_hg_w_out=v_hg_w_out, v_kv_norm=v_kv_norm, v_w_kv=v_w_kv, v_attn_norm=v_attn_norm, v_attn_w_q=v_attn_w_q, v_attn_sinks=v_attn_sinks, v_attn_w_o=v_attn_w_o, v_ffn_norm=v_ffn_norm, v_ffn_w_up=v_ffn_w_up, v_ffn_conv_w=v_ffn_conv_w, v_ffn_conv_b=v_ffn_conv_b, v_ffn_w_down=v_ffn_w_down, v_final_norm=v_final_norm)
    weights = {n: given[n] for n in TWIN_WEIGHTS}
    shared = {n: given[n] for n in SHARED_INPUTS}
    per_example = {n: given[n] for n in ['x']}
    grad_fn = _jax.value_and_grad(_loss, argnums=(0, 1))

    def one_microbatch(ex, loss_target):
        ex = dict(ex)
        diff = ex.pop(TWIN_DIFF_INPUT)
        return grad_fn(weights, diff, {**shared, **ex}, loss_target)

    if N_MICROBATCH == 1:
        loss, (grad_w, grad_x) = one_microbatch(per_example, given["loss_target"])
    else:
        def body(carry, xs):
            loss_sum, grad_sum = carry
            l_k, (gw_k, gx_k) = one_microbatch(xs[0], xs[1])
            with _jax.named_scope("update"):
                return (loss_sum + l_k, _jax.tree.map(_jnp.add, grad_sum, gw_k)), gx_k

        init = (_jnp.zeros((), _jnp.float32), _jax.tree.map(_jnp.zeros_like, weights))
        (loss, grad_w), grad_x = _jax.lax.scan(body, init, (per_example, given["loss_target"]))
    with _jax.named_scope("update"):
        delta_w, new_m, new_v = {}, {}, {}
        for n in TWIN_WEIGHTS:
            delta_w[n], new_m[n], new_v[n] = _adamw(weights[n], grad_w[n], given["m_" + n], given["v_" + n])
    return (loss, grad_x, *[grad_w[n] for n in TWIN_WEIGHTS], *[delta_w[n] for n in TWIN_WEIGHTS],
            *[new_m[n] for n in TWIN_WEIGHTS], *[new_v[n] for n in TWIN_WEIGHTS])
```

```python
import functools
import math

import jax
import jax.numpy as jnp
from jax import lax
from jax.experimental import pallas as pl
from jax.experimental.pallas import tpu as pltpu

F32 = jnp.float32
BF16 = jnp.bfloat16

EPS = 1e-6
HG_EXPAND = 128
HG_CHUNK = 32
ATT_HEAD_DIM = 64
ATT_KV_HEADS = 2
WINDOW = 128
CONV_WIDTH = 3
ADAM_LR = 0.001
ADAM_B1 = 0.9
ADAM_B2 = 0.999
ADAM_EPS = 1e-08
ADAM_WD = 0.01
ADAM_STEP = 10

N_DEV = 8
VMEM_LIMIT = 48 * 1024 * 1024
NEG = -1e30

NN = (((1,), (0,)), ((), ()))
NT = (((1,), (1,)), ((), ()))
TN = (((0,), (0,)), ((), ()))
MESH = pl.DeviceIdType.MESH


def _dot(a, b, dims=NN):
    return lax.dot_general(a.astype(BF16), b.astype(BF16), dims, preferred_element_type=F32)


def _dot_exact(a, b):
    return lax.dot_general(a, b, NN, precision=lax.Precision.HIGHEST, preferred_element_type=F32)


def _sigmoid(x):
    return 1.0 / (1.0 + jnp.exp(-x))


def _silu(x):
    return x * _sigmoid(x)


def _dsilu(x):
    s = _sigmoid(x)
    return s * (1.0 + x * (1.0 - s))


def _params(semantics):
    return pltpu.CompilerParams(dimension_semantics=semantics, vmem_limit_bytes=VMEM_LIMIT)


def _row_tile(rows, want=512):
    return min(rows, want)


def _matmul(name, a, b, *, dims, grid, a_spec, b_spec, o_spec, out_shape, acc_shape, add=None, add_spec=None):
    nk = grid[2]

    def body(*refs):
        if add is None:
            a_ref, b_ref, o_ref, acc = refs
        else:
            a_ref, b_ref, add_ref, o_ref, acc = refs
        k = pl.program_id(2)
        part = _dot(a_ref[...], b_ref[...], dims)

        def finish(total):
            if add is not None:
                total = total + add_ref[...]
            o_ref[...] = total.astype(o_ref.dtype)

        if nk == 1:
            finish(part)
        else:
            @pl.when(k == 0)
            def _():
                acc[...] = part

            @pl.when(k > 0)
            def _():
                acc[...] += part

            @pl.when(k == nk - 1)
            def _():
                finish(acc[...])

    in_specs = [a_spec, b_spec] + ([] if add is None else [add_spec])
    args = (a, b) + (() if add is None else (add,))
    return pl.pallas_call(
        body, name=name, grid=grid, in_specs=in_specs, out_specs=o_spec, out_shape=out_shape,
        scratch_shapes=[pltpu.VMEM(acc_shape, F32)],
        compiler_params=_params(("parallel", "parallel", "arbitrary")),
    )(*args)


def _mm_rows(name, a, w, *, out_dtype, add=None):
    s, kdim = a.shape
    n = w.shape[1]
    tm = _row_tile(s)
    return _matmul(
        name, a, w, dims=NN, grid=(s // tm, 1, 1),
        a_spec=pl.BlockSpec((tm, kdim), lambda i, j, k: (i, 0)),
        b_spec=pl.BlockSpec((kdim, n), lambda i, j, k: (0, 0)),
        o_spec=pl.BlockSpec((tm, n), lambda i, j, k: (i, 0)),
        out_shape=jax.ShapeDtypeStruct((s, n), out_dtype), acc_shape=(8, 128),
        add=add, add_spec=None if add is None else pl.BlockSpec((tm, n), lambda i, j, k: (i, 0)),
    )


def _mm_rows_nt(name, a, w, *, out_dtype):
    s, n = a.shape
    kdim = w.shape[0]
    tm = _row_tile(s)
    return _matmul(
        name, a, w, dims=NT, grid=(s // tm, 1, 1),
        a_spec=pl.BlockSpec((tm, n), lambda i, j, k: (i, 0)),
        b_spec=pl.BlockSpec((kdim, n), lambda i, j, k: (0, 0)),
        o_spec=pl.BlockSpec((tm, kdim), lambda i, j, k: (i, 0)),
        out_shape=jax.ShapeDtypeStruct((s, kdim), out_dtype), acc_shape=(8, 128),
    )


def _mm_tn(name, a, g):
    s, m = a.shape
    n = g.shape[1]
    ts = _row_tile(s)
    return _matmul(
        name, a, g, dims=TN, grid=(1, 1, s // ts),
        a_spec=pl.BlockSpec((ts, m), lambda i, j, k: (k, 0)),
        b_spec=pl.BlockSpec((ts, n), lambda i, j, k: (k, 0)),
        o_spec=pl.BlockSpec((m, n), lambda i, j, k: (0, 0)),
        out_shape=jax.ShapeDtypeStruct((m, n), F32), acc_shape=(m, n),
    )


def _rmsnorm_cast(name, h, gains):
    s, d = h.shape
    tm = _row_tile(s)
    n = len(gains)

    def body(*refs):
        h_ref, g_refs, o_refs = refs[0], refs[1:1 + n], refs[1 + n:]
        xv = h_ref[...]
        xhat = xv * lax.rsqrt(jnp.mean(xv * xv, axis=-1, keepdims=True) + EPS)
        for g_ref, o_ref in zip(g_refs, o_refs):
            o_ref[...] = (xhat * g_ref[...]).astype(BF16)

    row = pl.BlockSpec((tm, d), lambda i: (i, 0))
    vec = pl.BlockSpec((1, d), lambda i: (0, 0))
    return pl.pallas_call(
        body, name=name, grid=(s // tm,), in_specs=[row] + [vec] * n, out_specs=[row] * n,
        out_shape=[jax.ShapeDtypeStruct((s, d), BF16)] * n, compiler_params=_params(("parallel",)),
    )(h, *gains)


def _rmsnorm_bwd(name, h, dres, branches):
    s, d = h.shape
    tm = _row_tile(s)
    n = len(branches)

    def body(*refs):
        h_ref, dres_ref = refs[0], refs[1]
        da_refs, g_refs = refs[2:2 + n], refs[2 + n:2 + 2 * n]
        dh_ref, dg_refs = refs[2 + 2 * n], refs[3 + 2 * n:]
        i = pl.program_id(0)
        xv = h_ref[...]
        r = lax.rsqrt(jnp.mean(xv * xv, axis=-1, keepdims=True) + EPS)
        xhat = xv * r
        total = dres_ref[...]
        for da_ref, g_ref, dg_ref in zip(da_refs, g_refs, dg_refs):
            da = da_ref[...]
            dgain = jnp.sum(da * xhat, axis=0, keepdims=True)

            @pl.when(i == 0)
            def _():
                dg_ref[...] = dgain

            @pl.when(i > 0)
            def _():
                dg_ref[...] += dgain

            dxhat = da * g_ref[...]
            total = total + r * (dxhat - xhat * jnp.mean(dxhat * xhat, axis=-1, keepdims=True))
        dh_ref[...] = total

    row = pl.BlockSpec((tm, d), lambda i: (i, 0))
    vec = pl.BlockSpec((1, d), lambda i: (0, 0))
    outs = pl.pallas_call(
        body, name=name, grid=(s // tm,), in_specs=[row, row] + [row] * n + [vec] * n, out_specs=[row] + [vec] * n,
        out_shape=[jax.ShapeDtypeStruct((s, d), F32)] + [jax.ShapeDtypeStruct((1, d), F32)] * n,
        compiler_params=_params(("arbitrary",)),
    )(h, dres, *[b[0] for b in branches], *[b[1] for b in branches])
    return outs[0], outs[1:]


def _loss_head(h, gain, target):
    s, d = h.shape
    tm = _row_tile(s)

    def body(h_ref, g_ref, t_ref, dh_ref, dg_ref, loss_ref):
        i = pl.program_id(0)
        xv = h_ref[...]
        r = lax.rsqrt(jnp.mean(xv * xv, axis=-1, keepdims=True) + EPS)
        xhat = xv * r
        err = xhat * g_ref[...] - t_ref[...]
        dy = err * (1.0 / d)
        part = jnp.zeros((1, 128), F32) + 0.5 * jnp.sum(jnp.mean(err * err, axis=-1, keepdims=True))
        dgain = jnp.sum(dy * xhat, axis=0, keepdims=True)

        @pl.when(i == 0)
        def _():
            dg_ref[...] = dgain
            loss_ref[...] = part

        @pl.when(i > 0)
        def _():
            dg_ref[...] += dgain
            loss_ref[...] += part

        dxhat = dy * g_ref[...]
        dh_ref[...] = r * (dxhat - xhat * jnp.mean(dxhat * xhat, axis=-1, keepdims=True))

    row = pl.BlockSpec((tm, d), lambda i: (i, 0))
    vec = pl.BlockSpec((1, d), lambda i: (0, 0))
    return pl.pallas_call(
        body, name="loss_head", grid=(s // tm,), in_specs=[row, vec, row],
        out_specs=[row, vec, pl.BlockSpec((1, 128), lambda i: (0, 0))],
        out_shape=[jax.ShapeDtypeStruct((s, d), F32), jax.ShapeDtypeStruct((1, d), F32), jax.ShapeDtypeStruct((1, 128), F32)],
        compiler_params=_params(("arbitrary",)),
    )(h, gain, target)


def _hg_gates(p_ref, lbl_ref):
    pq = p_ref[0].astype(F32)
    pf = p_ref[1].astype(F32)
    v = p_ref[2].astype(F32)
    lb = _sigmoid(lbl_ref[0:1, :] - lbl_ref[1:2, :])
    sig = _sigmoid(pf)
    fg = lb + (1.0 - lb) * sig
    q = _silu(pq) * HG_EXPAND ** -0.5
    return q, 1.0 - fg, v, jnp.log(fg), pq, sig, fg, lb


def _hg_chunk_terms(q, k, g_s):
    gc = g_s[...]
    gm = g_s[HG_CHUNK // 2 - 1:HG_CHUNK // 2, :]
    gl = g_s[HG_CHUNK - 1:HG_CHUNK, :]
    return gl, q * jnp.exp(gc), q * jnp.exp(gc - gm), k * jnp.exp(gm - gc), k * jnp.exp(gl - gc)


def _tri(lower):
    r = lax.broadcasted_iota(jnp.int32, (HG_CHUNK, HG_CHUNK), 0)
    c = lax.broadcasted_iota(jnp.int32, (HG_CHUNK, HG_CHUNK), 1)
    return r >= c if lower else r <= c


def _hgrn2_fwd(p, lb_logits, out_gain):
    _, s, d = p.shape
    heads = d // HG_EXPAND
    t = _row_tile(s)
    nc = t // HG_CHUNK

    def body(p_ref, lbl_ref, gain_ref, o_ref, og_ref, st_ref, state, q_s, k_s, v_s, lf_s, g_s):
        @pl.when(pl.program_id(1) == 0)
        def _():
            state[...] = jnp.zeros_like(state)

        q, k, v, lf, _, _, _, _ = _hg_gates(p_ref, lbl_ref)
        q_s[...] = q
        k_s[...] = k
        v_s[...] = v
        lf_s[...] = lf
        tril = _tri(True)
        tril_f = tril.astype(F32)

        def chunk(c, carry):
            rows = pl.ds(pl.multiple_of(c * HG_CHUNK, HG_CHUNK), HG_CHUNK)
            g_s[...] = _dot_exact(tril_f, lf_s[rows, :])
            gl, qd, qt, kt, kd = _hg_chunk_terms(q_s[rows, :], k_s[rows, :], g_s)
            vc = v_s[rows, :]
            st = state[...]
            st_ref[c] = st
            a = jnp.where(tril, _dot(qt, kt, NT), 0.0)
            o_ref[rows, :] = _dot(qd, st, NT) + _dot(a, vc)
            state[...] = st * jnp.exp(gl) + _dot(vc, kd, TN)
            return carry

        lax.fori_loop(0, nc, chunk, 0)
        ov = o_ref[...]
        on = ov * lax.rsqrt(jnp.mean(ov * ov, axis=-1, keepdims=True) + EPS) * gain_ref[...]
        og_ref[...] = (on * _silu(p_ref[3].astype(F32))).astype(BF16)

    blk = pl.BlockSpec((t, HG_EXPAND), lambda h, b: (b, h))
    return pl.pallas_call(
        body, name="hgrn2_fwd", grid=(heads, s // t),
        in_specs=[pl.BlockSpec((4, t, HG_EXPAND), lambda h, b: (0, b, h)), pl.BlockSpec((2, HG_EXPAND), lambda h, b: (0, h)),
                  pl.BlockSpec((1, HG_EXPAND), lambda h, b: (0, 0))],
        out_specs=[blk, blk, pl.BlockSpec((None, nc, HG_EXPAND, HG_EXPAND), lambda h, b: (h, b, 0, 0))],
        out_shape=[jax.ShapeDtypeStruct((s, d), F32), jax.ShapeDtypeStruct((s, d), BF16),
                   jax.ShapeDtypeStruct((heads, s // HG_CHUNK, HG_EXPAND, HG_EXPAND), F32)],
        scratch_shapes=[pltpu.VMEM((HG_EXPAND, HG_EXPAND), F32)] + [pltpu.VMEM((t, HG_EXPAND), F32)] * 4
        + [pltpu.VMEM((HG_CHUNK, HG_EXPAND), F32)],
        compiler_params=_params(("parallel", "arbitrary")),
    )(p, lb_logits, out_gain)


def _hgrn2_bwd(p, lb_logits, out_gain, o, dog, states):
    _, s, d = p.shape
    heads = d // HG_EXPAND
    t = _row_tile(s)
    nc = t // HG_CHUNK
    nb = s // t

    def body(p_ref, lbl_ref, gain_ref, o_ref, dog_ref, st_ref, dp_ref, dlbl_ref, dgain_ref,
             dstate, q_s, k_s, v_s, lf_s, do_s, dq_s, dk_s, dv_s, dlf_s, g_s):
        h, b = pl.program_id(0), pl.program_id(1)

        @pl.when(b == 0)
        def _():
            dstate[...] = jnp.zeros_like(dstate)

        q, k, v, lf, pq, sig, fg, lb = _hg_gates(p_ref, lbl_ref)
        q_s[...] = q
        k_s[...] = k
        v_s[...] = v
        lf_s[...] = lf
        pg = p_ref[3].astype(F32)
        ov = o_ref[...]
        r = lax.rsqrt(jnp.mean(ov * ov, axis=-1, keepdims=True) + EPS)
        ohat = ov * r
        dogv = dog_ref[...]
        d_on = dogv * _silu(pg)
        dpg = dogv * ohat * gain_ref[...] * _dsilu(pg)
        dgain = jnp.sum(d_on * ohat, axis=0, keepdims=True)

        @pl.when((h == 0) & (b == 0))
        def _():
            dgain_ref[...] = dgain

        @pl.when((h > 0) | (b > 0))
        def _():
            dgain_ref[...] += dgain

        dohat = d_on * gain_ref[...]
        do_s[...] = r * (dohat - ohat * jnp.mean(dohat * ohat, axis=-1, keepdims=True))

        tril = _tri(True)
        tril_f = tril.astype(F32)
        triu_f = _tri(False).astype(F32)
        last_row = lax.broadcasted_iota(jnp.int32, (HG_CHUNK, HG_EXPAND), 0) == HG_CHUNK - 1

        def chunk(i, carry):
            c = nc - 1 - i
            rows = pl.ds(pl.multiple_of(c * HG_CHUNK, HG_CHUNK), HG_CHUNK)
            g_s[...] = _dot_exact(tril_f, lf_s[rows, :])
            gl, qd, qt, kt, kd = _hg_chunk_terms(q_s[rows, :], k_s[rows, :], g_s)
            gc = g_s[...]
            gm = g_s[HG_CHUNK // 2 - 1:HG_CHUNK // 2, :]
            vc = v_s[rows, :]
            doc = do_s[rows, :]
            st = st_ref[c]
            dst = dstate[...]
            a = jnp.where(tril, _dot(qt, kt, NT), 0.0)
            da = jnp.where(tril, _dot(doc, vc, NT), 0.0)
            dqt = _dot(da, kt)
            dkt = _dot(da, qt, TN)
            dqd = _dot(doc, st)
            dkd = _dot(vc, dst)
            dv_s[rows, :] = _dot(a, doc, TN) + _dot(kd, dst, NT)
            dq_s[rows, :] = dqt * jnp.exp(gc - gm) + dqd * jnp.exp(gc)
            dk_s[rows, :] = dkt * jnp.exp(gm - gc) + dkd * jnp.exp(gl - gc)
            dg = dqt * qt - dkt * kt + dqd * qd - dkd * kd
            dgl = jnp.sum(dkd * kd, axis=0, keepdims=True) + jnp.exp(gl) * jnp.sum(dst * st, axis=0, keepdims=True)
            dlf_s[rows, :] = _dot_exact(triu_f, dg + jnp.where(last_row, dgl, 0.0))
            dstate[...] = dst * jnp.exp(gl) + _dot(doc, qd, TN)
            return carry

        lax.fori_loop(0, nc, chunk, 0)
        dfg = dlf_s[...] / fg - dk_s[...]
        dlb = jnp.sum(dfg * (1.0 - sig), axis=0, keepdims=True)
        dl0 = dlb * lb * (1.0 - lb)
        dlbl = jnp.concatenate([dl0, -dl0], axis=0)

        @pl.when(b == 0)
        def _():
            dlbl_ref[...] = dlbl

        @pl.when(b > 0)
        def _():
            dlbl_ref[...] += dlbl

        dp_ref[0] = (dq_s[...] * HG_EXPAND ** -0.5 * _dsilu(pq)).astype(BF16)
        dp_ref[1] = (dfg * (1.0 - lb) * sig * (1.0 - sig)).astype(BF16)
        dp_ref[2] = dv_s[...].astype(BF16)
        dp_ref[3] = dpg.astype(BF16)

    blk = pl.BlockSpec((t, HG_EXPAND), lambda h, b: (nb - 1 - b, h))
    pblk = pl.BlockSpec((4, t, HG_EXPAND), lambda h, b: (0, nb - 1 - b, h))
    return pl.pallas_call(
        body, name="hgrn2_bwd", grid=(heads, nb),
        in_specs=[pblk, pl.BlockSpec((2, HG_EXPAND), lambda h, b: (0, h)), pl.BlockSpec((1, HG_EXPAND), lambda h, b: (0, 0)),
                  blk, blk, pl.BlockSpec((None, nc, HG_EXPAND, HG_EXPAND), lambda h, b: (h, nb - 1 - b, 0, 0))],
        out_specs=[pblk, pl.BlockSpec((2, HG_EXPAND), lambda h, b: (0, h)), pl.BlockSpec((1, HG_EXPAND), lambda h, b: (0, 0))],
        out_shape=[jax.ShapeDtypeStruct((4, s, d), BF16), jax.ShapeDtypeStruct((2, d), F32), jax.ShapeDtypeStruct((1, HG_EXPAND), F32)],
        scratch_shapes=[pltpu.VMEM((HG_EXPAND, HG_EXPAND), F32)] + [pltpu.VMEM((t, HG_EXPAND), F32)] * 9
        + [pltpu.VMEM((HG_CHUNK, HG_EXPAND), F32)],
        compiler_params=_params(("arbitrary", "arbitrary")),
    )(p, lb_logits, out_gain, o, dog, states)


HALO = 8


def _shift_down(xv, n):
    return pltpu.roll(xv, n, axis=0)


def _shift_up(xv, n):
    return pltpu.roll(xv, xv.shape[0] - n, axis=0)


def _ffn_hidden(u, conv_w, conv_b):
    _, nj, s, fb = u.shape
    tm = _row_tile(s)
    per = tm // HALO

    def body(gate_ref, prev_ref, val_ref, w_ref, b_ref, h_ref):
        i = pl.program_id(1)
        prev = jnp.where(i > 0, prev_ref[...].astype(F32), 0.0)
        ext = jnp.concatenate([prev, gate_ref[...].astype(F32)], axis=0)
        conv = b_ref[...] + w_ref[2:3, :] * ext[HALO:]
        conv = conv + w_ref[1:2, :] * _shift_down(ext, 1)[HALO:]
        conv = conv + w_ref[0:1, :] * _shift_down(ext, 2)[HALO:]
        h_ref[...] = (_silu(conv) * val_ref[...].astype(F32)).astype(BF16)

    return pl.pallas_call(
        body, name="ffn_hidden", grid=(nj, s // tm),
        in_specs=[pl.BlockSpec((None, None, tm, fb), lambda j, i: (0, j, i, 0)),
                  pl.BlockSpec((None, None, HALO, fb), lambda j, i: (0, j, jnp.maximum(i * per - 1, 0), 0)),
                  pl.BlockSpec((None, None, tm, fb), lambda j, i: (1, j, i, 0)),
                  pl.BlockSpec((None, CONV_WIDTH, fb), lambda j, i: (j, 0, 0)),
                  pl.BlockSpec((None, 1, fb), lambda j, i: (j, 0, 0))],
        out_specs=pl.BlockSpec((None, tm, fb), lambda j, i: (j, i, 0)),
        out_shape=jax.ShapeDtypeStruct((nj, s, fb), BF16), compiler_params=_params(("parallel", "parallel")),
    )(u, u, u, conv_w, conv_b)


def _ffn_hidden_bwd(u, dh, conv_w, conv_b):
    _, nj, s, fb = u.shape
    tm = _row_tile(s)
    per = tm // HALO
    nblk = s // HALO
    ni = s // tm

    def body(gate_ref, gprev_ref, gnext_ref, val_ref, vnext_ref, dh_ref, dhnext_ref, w_ref, b_ref, du_ref, dw_ref, db_ref):
        i = pl.program_id(1)
        has_next = i < ni - 1
        gprev = jnp.where(i > 0, gprev_ref[...].astype(F32), 0.0)
        gext = jnp.concatenate([gprev, gate_ref[...].astype(F32), gnext_ref[...].astype(F32)], axis=0)
        vext = jnp.concatenate([val_ref[...].astype(F32), vnext_ref[...].astype(F32)], axis=0)
        dhext = jnp.concatenate([dh_ref[...].astype(F32), jnp.where(has_next, dhnext_ref[...].astype(F32), 0.0)], axis=0)
        g0 = gext[HALO:]
        g1 = _shift_down(gext, 1)[HALO:]
        g2 = _shift_down(gext, 2)[HALO:]
        conv = b_ref[...] + w_ref[2:3, :] * g0 + w_ref[1:2, :] * g1 + w_ref[0:1, :] * g2
        dconv = dhext * vext * _dsilu(conv)
        dgate = w_ref[2:3, :] * dconv + w_ref[1:2, :] * _shift_up(dconv, 1) + w_ref[0:1, :] * _shift_up(dconv, 2)
        du_ref[0] = dgate[:tm].astype(BF16)
        du_ref[1] = (dhext * _silu(conv))[:tm].astype(BF16)
        own = dconv[:tm]
        dw = jnp.concatenate([jnp.sum(own * g2[:tm], axis=0, keepdims=True), jnp.sum(own * g1[:tm], axis=0, keepdims=True),
                              jnp.sum(own * g0[:tm], axis=0, keepdims=True)], axis=0)
        db = jnp.sum(own, axis=0, keepdims=True)

        @pl.when(i == 0)
        def _():
            dw_ref[...] = dw
            db_ref[...] = db

        @pl.when(i > 0)
        def _():
            dw_ref[...] += dw
            db_ref[...] += db

    def tile(part):
        return pl.BlockSpec((None, None, tm, fb), lambda j, i: (part, j, i, 0))

    def after(part):
        return pl.BlockSpec((None, None, HALO, fb), lambda j, i: (part, j, jnp.minimum((i + 1) * per, nblk - 1), 0))

    return pl.pallas_call(
        body, name="ffn_hidden_bwd", grid=(nj, ni),
        in_specs=[tile(0), pl.BlockSpec((None, None, HALO, fb), lambda j, i: (0, j, jnp.maximum(i * per - 1, 0), 0)), after(0),
                  tile(1), after(1),
                  pl.BlockSpec((None, tm, fb), lambda j, i: (j, i, 0)),
                  pl.BlockSpec((None, HALO, fb), lambda j, i: (j, jnp.minimum((i + 1) * per, nblk - 1), 0)),
                  pl.BlockSpec((None, CONV_WIDTH, fb), lambda j, i: (j, 0, 0)), pl.BlockSpec((None, 1, fb), lambda j, i: (j, 0, 0))],
        out_specs=[pl.BlockSpec((2, None, tm, fb), lambda j, i: (0, j, i, 0)),
                   pl.BlockSpec((None, CONV_WIDTH, fb), lambda j, i: (j, 0, 0)), pl.BlockSpec((None, 1, fb), lambda j, i: (j, 0, 0))],
        out_shape=[jax.ShapeDtypeStruct((2, nj, s, fb), BF16), jax.ShapeDtypeStruct((nj, CONV_WIDTH, fb), F32),
                   jax.ShapeDtypeStruct((nj, 1, fb), F32)],
        compiler_params=_params(("parallel", "arbitrary")),
    )(u, u, u, u, u, dh, dh, conv_w, conv_b)


def _attn_probs(q, kc, kp, sink, head, n, n_heads):
    iq = lax.broadcasted_iota(jnp.int32, (WINDOW, WINDOW), 0)
    ik = lax.broadcasted_iota(jnp.int32, (WINDOW, WINDOW), 1)
    slope = jnp.exp((jnp.zeros((1, 1), F32) + (head + 1).astype(F32)) * (-8.0 / n_heads * math.log(2.0)))
    dist = (iq - ik).astype(F32)
    scale = ATT_HEAD_DIM ** -0.5
    sc = jnp.where(iq >= ik, _dot(q, kc, NT) * scale - slope * dist, NEG)
    sp = jnp.where((ik > iq) & (n > 0), _dot(q, kp, NT) * scale - slope * (dist + WINDOW), NEG)
    m = jnp.maximum(jnp.maximum(jnp.max(sc, axis=-1, keepdims=True), jnp.max(sp, axis=-1, keepdims=True)), sink)
    ec = jnp.exp(sc - m)
    ep = jnp.exp(sp - m)
    es = jnp.exp(sink - m)
    inv = 1.0 / (jnp.sum(ec, axis=-1, keepdims=True) + jnp.sum(ep, axis=-1, keepdims=True) + es)
    return ec * inv, ep * inv, es * inv


def _attn_fwd(q, k, v, sinks):
    nq, s, hd = q.shape
    group = nq // ATT_KV_HEADS
    nb = s // WINDOW

    def body(q_ref, kc_ref, kp_ref, vc_ref, vp_ref, sink_ref, o_ref):
        h, n = pl.program_id(0), pl.program_id(1)
        pc, pp, _ = _attn_probs(q_ref[...], kc_ref[...], kp_ref[...], sink_ref[...], h, n, nq)
        o_ref[...] = (_dot(pc, vc_ref[...]) + _dot(pp, vp_ref[...])).astype(BF16)

    cur = pl.BlockSpec((None, WINDOW, hd), lambda h, n: (h // group, n, 0))
    prev = pl.BlockSpec((None, WINDOW, hd), lambda h, n: (h // group, jnp.maximum(n - 1, 0), 0))
    qblk = pl.BlockSpec((None, WINDOW, hd), lambda h, n: (h, n, 0))
    return pl.pallas_call(
        body, name="attn_fwd", grid=(nq, nb),
        in_specs=[qblk, cur, prev, cur, prev, pl.BlockSpec((None, 1, 1), lambda h, n: (h, 0, 0))],
        out_specs=qblk, out_shape=jax.ShapeDtypeStruct((nq, s, hd), BF16), compiler_params=_params(("parallel", "parallel")),
    )(q, k, k, v, v, sinks)


def _attn_bwd(q, k, v, o, do, sinks):
    nq, s, hd = q.shape
    group = nq // ATT_KV_HEADS
    nb = s // WINDOW

    def body(q_ref, kc_ref, kp_ref, vc_ref, vp_ref, o_ref, do_ref, sink_ref, dq_ref, dkc_ref, dkp_ref, dvc_ref, dvp_ref, ds_ref):
        g, n, j = pl.program_id(0), pl.program_id(1), pl.program_id(2)
        qv, kc, kp, vc, vp = q_ref[...], kc_ref[...], kp_ref[...], vc_ref[...], vp_ref[...]
        dov = do_ref[...]
        pc, pp, ps = _attn_probs(qv, kc, kp, sink_ref[...], g * group + j, n, nq)
        dsum = jnp.sum(dov.astype(F32) * o_ref[...].astype(F32), axis=-1, keepdims=True)
        scale = ATT_HEAD_DIM ** -0.5
        dsc = pc * (_dot(dov, vc, NT) - dsum) * scale
        dsp = pp * (_dot(dov, vp, NT) - dsum) * scale
        dq_ref[...] = (_dot(dsc, kc) + _dot(dsp, kp)).astype(BF16)
        ds_ref[...] = jnp.zeros((1, 128), F32) - jnp.sum(ps * dsum)
        parts = (_dot(dsc, qv, TN), _dot(dsp, qv, TN), _dot(pc, dov, TN), _dot(pp, dov, TN))
        for ref, part in zip((dkc_ref, dkp_ref, dvc_ref, dvp_ref), parts):
            @pl.when(j == 0)
            def _():
                ref[...] = part

            @pl.when(j > 0)
            def _():
                ref[...] += part

    cur = pl.BlockSpec((None, WINDOW, hd), lambda g, n, j: (g, n, 0))
    prev = pl.BlockSpec((None, WINDOW, hd), lambda g, n, j: (g, jnp.maximum(n - 1, 0), 0))
    qblk = pl.BlockSpec((None, WINDOW, hd), lambda g, n, j: (g * group + j, n, 0))
    kv_shape = jax.ShapeDtypeStruct((ATT_KV_HEADS, s, hd), F32)
    return pl.pallas_call(
        body, name="attn_bwd", grid=(ATT_KV_HEADS, nb, group),
        in_specs=[qblk, cur, prev, cur, prev, qblk, qblk, pl.BlockSpec((None, 1, 1), lambda g, n, j: (g * group + j, 0, 0))],
        out_specs=[qblk, cur, cur, cur, cur, pl.BlockSpec((None, None, 1, 128), lambda g, n, j: (g * group + j, n, 0, 0))],
        out_shape=[jax.ShapeDtypeStruct((nq, s, hd), BF16), kv_shape, kv_shape, kv_shape, kv_shape,
                   jax.ShapeDtypeStruct((nq, nb, 1, 128), F32)],
        compiler_params=_params(("parallel", "parallel", "arbitrary")),
    )(q, k, k, v, v, o, do, sinks)


HBM_SPEC = pl.BlockSpec(memory_space=pltpu.HBM)
VMEM_SPEC = pl.BlockSpec(memory_space=pltpu.VMEM)


def _place():
    return lax.axis_index("x"), lax.axis_index("y"), lax.axis_index("c")


def _flip(pos, r):
    return tuple(1 - p if (r >> (2 - a)) & 1 else p for a, p in enumerate(pos))


def _index(pos):
    return 4 * pos[0] + 2 * pos[1] + pos[2]


def _all_gather(name, shards, spec):
    n = len(shards)

    def body(*refs):
        x_refs, o_refs = refs[:n], refs[n:2 * n]
        send_sems, recv_sems, local_sems = refs[2 * n:]
        me = _place()
        sibling = _flip(me, 1)
        far = [_flip(me, r) for r in (4, 2, 6)]

        def copy(t, sem, block, to, src=None):
            rows = o_refs[t].at[_index(block)]
            return pltpu.make_async_remote_copy(
                src_ref=rows if src is None else src, dst_ref=rows, send_sem=send_sems.at[t, sem], recv_sem=recv_sems.at[t, sem],
                device_id=to, device_id_type=MESH)

        own = [pltpu.make_async_copy(x_refs[t], o_refs[t].at[_index(me)], local_sems.at[t]) for t in range(n)]
        for cp in own:
            cp.start()
        first = []
        for t in range(n):
            first.append(copy(t, 0, me, sibling, src=x_refs[t]))
            first += [copy(t, 1 + j, me, peer, src=x_refs[t]) for j, peer in enumerate(far)]
        for cp in first:
            cp.start()
        passed = []
        for j, peer in enumerate(far):
            for t in range(n):
                copy(t, 1 + j, peer, me).wait_recv()
                cp = copy(t, 4 + j, peer, sibling)
                cp.start()
                passed.append(cp)
        for t in range(n):
            copy(t, 0, sibling, me).wait_recv()
            for j, peer in enumerate(far):
                copy(t, 4 + j, _flip(peer, 1), me).wait_recv()
        for cp in first + passed:
            cp.wait_send()
        for cp in own:
            cp.wait()

    return pl.pallas_call(
        body, name=name, in_specs=[spec] * n, out_specs=[spec] * n,
        out_shape=[jax.ShapeDtypeStruct((N_DEV,) + sh.shape, sh.dtype) for sh in shards],
        scratch_shapes=[pltpu.SemaphoreType.DMA((n, 7)), pltpu.SemaphoreType.DMA((n, 7)), pltpu.SemaphoreType.DMA((n,))],
    )(*shards)


def _exchange(name, stacks):
    n = len(stacks)

    def body(*refs):
        x_refs, o_refs = refs[:n], refs[n:2 * n]
        send_sems, recv_sems, local_sems = refs[2 * n:]
        me = _place()
        mine = _index(me)
        own = [pltpu.make_async_copy(x_refs[t].at[mine], o_refs[t].at[mine], local_sems.at[t]) for t in range(n)]
        for cp in own:
            cp.start()

        def copy(t, r):
            peer = _flip(me, r)
            return pltpu.make_async_remote_copy(
                src_ref=x_refs[t].at[_index(peer)], dst_ref=o_refs[t].at[mine], send_sem=send_sems.at[t, r - 1],
                recv_sem=recv_sems.at[t, r - 1], device_id=peer, device_id_type=MESH)

        def arrival(t, r):
            peer = _flip(me, r)
            return pltpu.make_async_remote_copy(
                src_ref=x_refs[t].at[mine], dst_ref=o_refs[t].at[_index(peer)], send_sem=send_sems.at[t, r - 1],
                recv_sem=recv_sems.at[t, r - 1], device_id=peer, device_id_type=MESH)

        sent = [copy(t, r) for t in range(n) for r in range(1, N_DEV)]
        for cp in sent:
            cp.start()
        for t in range(n):
            for r in range(1, N_DEV):
                arrival(t, r).wait_recv()
        for cp in sent:
            cp.wait_send()
        for cp in own:
            cp.wait()

    return pl.pallas_call(
        body, name=name, in_specs=[HBM_SPEC] * n, out_specs=[HBM_SPEC] * n,
        out_shape=[jax.ShapeDtypeStruct(st.shape, st.dtype) for st in stacks],
        scratch_shapes=[pltpu.SemaphoreType.DMA((n, 7)), pltpu.SemaphoreType.DMA((n, 7)), pltpu.SemaphoreType.DMA((n,))],
    )(*stacks)


def _pack_rows(parts):
    offsets, row = [], 0
    for part in parts:
        offsets.append(row)
        row += part.shape[0]
    return offsets, -(-row // 8) * 8, -(-max(part.shape[1] for part in parts) // 128) * 128


def _pack(parts):
    offsets, rows, width = _pack_rows(parts)

    def body(*refs):
        o_ref = refs[-1]
        o_ref[...] = jnp.zeros_like(o_ref)
        for off, ref in zip(offsets, refs[:-1]):
            o_ref[off:off + ref.shape[0], 0:ref.shape[1]] = ref[...]

    return pl.pallas_call(body, name="pack_small_grads", in_specs=[VMEM_SPEC] * len(parts), out_specs=VMEM_SPEC,
                          out_shape=jax.ShapeDtypeStruct((rows, width), F32))(*parts)


def _adamw_math(w, g, m, v):
    m = ADAM_B1 * m + (1.0 - ADAM_B1) * g
    v = ADAM_B2 * v + (1.0 - ADAM_B2) * (g * g)
    m_hat = m / (1.0 - ADAM_B1 ** ADAM_STEP)
    v_hat = v / (1.0 - ADAM_B2 ** ADAM_STEP)
    return -ADAM_LR * (m_hat / (jnp.sqrt(v_hat) + ADAM_EPS) + ADAM_WD * w), m, v


def _adamw_shard(name, w, m, v, partials):
    rows, cols = w.shape
    tr = max(t for t in range(8, min(rows, 256) + 1, 8) if rows % t == 0)

    def body(w_ref, m_ref, v_ref, p_ref, g_ref, d_ref, nm_ref, nv_ref):
        g = p_ref[0].astype(F32)
        for dev in range(1, N_DEV):
            g = g + p_ref[dev].astype(F32)
        g_ref[...] = g
        d_ref[...], nm_ref[...], nv_ref[...] = _adamw_math(w_ref[...], g, m_ref[...], v_ref[...])

    blk = pl.BlockSpec((tr, cols), lambda i: (i, 0))
    return pl.pallas_call(
        body, name=name, grid=(rows // tr,), in_specs=[blk, blk, blk, pl.BlockSpec((N_DEV, tr, cols), lambda i: (0, i, 0))],
        out_specs=[blk] * 4, out_shape=[jax.ShapeDtypeStruct((rows, cols), F32)] * 4, compiler_params=_params(("parallel",)),
    )(w, m, v, partials)


def _adamw_small(gathered, offsets, entries):
    n = len(entries)

    def body(*refs):
        pack_ref = refs[0]
        w_refs, m_refs, v_refs = refs[1:1 + n], refs[1 + n:1 + 2 * n], refs[1 + 2 * n:1 + 3 * n]
        outs = refs[1 + 3 * n:]
        total = pack_ref[0]
        for dev in range(1, N_DEV):
            total = total + pack_ref[dev]
        mine = _index(_place())
        for e in range(n):
            rows, cols = w_refs[e].shape
            off = offsets[e]
            if entries[e][3]:
                g = jnp.zeros((rows, cols), F32)
                for dev in range(N_DEV):
                    g = g + jnp.where(mine == dev, total[off + dev * rows:off + (dev + 1) * rows, 0:cols], 0.0)
            else:
                g = total[off:off + rows, 0:cols]
            outs[4 * e][...] = g
            outs[4 * e + 1][...], outs[4 * e + 2][...], outs[4 * e + 3][...] = _adamw_math(w_refs[e][...], g, m_refs[e][...], v_refs[e][...])
        outs[4 * n][...] = total[offsets[n]:offsets[n] + 1, 0:128]

    shapes = []
    for w, _, _, _ in entries:
        shapes += [jax.ShapeDtypeStruct(w.shape, F32)] * 4
    shapes.append(jax.ShapeDtypeStruct((1, 128), F32))
    return pl.pallas_call(
        body, name="adamw_small", in_specs=[VMEM_SPEC] * (1 + 3 * n), out_specs=[VMEM_SPEC] * len(shapes), out_shape=shapes,
        compiler_params=pltpu.CompilerParams(vmem_limit_bytes=VMEM_LIMIT),
    )(gathered, *[e[0] for e in entries], *[e[1] for e in entries], *[e[2] for e in entries])


def _ffn_forward(tag, h, gain, w_up, w_down, conv_w, conv_b):
    s, d = h.shape
    fb = w_up.shape[2]
    tm = _row_tile(s)
    a, = _rmsnorm_cast(f"ffn_norm_{tag}", h, [gain])
    u = _matmul(
        f"ffn_up_{tag}", a, w_up, dims=NN, grid=(s // tm, N_DEV, 1),
        a_spec=pl.BlockSpec((tm, d), lambda i, j, k: (i, 0)),
        b_spec=pl.BlockSpec((None, d, fb), lambda i, j, k: (j, 0, 0)),
        o_spec=pl.BlockSpec((None, None, tm, fb), lambda i, j, k: (j // 4, j % 4, i, 0)),
        out_shape=jax.ShapeDtypeStruct((2, 4, s, fb), BF16), acc_shape=(8, 128))
    hidden = _ffn_hidden(u, conv_w, conv_b)
    out = _matmul(
        f"ffn_down_{tag}", hidden, w_down, dims=NN, grid=(s // tm, 1, 4),
        a_spec=pl.BlockSpec((None, tm, fb), lambda i, j, k: (k, i, 0)),
        b_spec=pl.BlockSpec((None, fb, d), lambda i, j, k: (k, 0, 0)),
        o_spec=pl.BlockSpec((tm, d), lambda i, j, k: (i, 0)),
        out_shape=jax.ShapeDtypeStruct((s, d), F32), acc_shape=(tm, d),
        add=h, add_spec=pl.BlockSpec((tm, d), lambda i, j, k: (i, 0)))
    return out, (a, u, hidden)


def _ffn_backward(tag, h, gain, w_up, w_down, conv_w, conv_b, saved, dout):
    a, u, hidden = saved
    s, d = h.shape
    fb = w_up.shape[2]
    tm = _row_tile(s)
    dhidden = _matmul(
        f"ffn_down_bwd_{tag}", dout, w_down, dims=NT, grid=(s // tm, 4, 1),
        a_spec=pl.BlockSpec((tm, d), lambda i, j, k: (i, 0)),
        b_spec=pl.BlockSpec((None, fb, d), lambda i, j, k: (j, 0, 0)),
        o_spec=pl.BlockSpec((None, tm, fb), lambda i, j, k: (j, i, 0)),
        out_shape=jax.ShapeDtypeStruct((4, s, fb), BF16), acc_shape=(8, 128))
    dw_down = _matmul(
        f"ffn_down_grad_{tag}", hidden, dout, dims=TN, grid=(4, 1, s // tm),
        a_spec=pl.BlockSpec((None, tm, fb), lambda i, j, k: (i, k, 0)),
        b_spec=pl.BlockSpec((tm, d), lambda i, j, k: (k, 0)),
        o_spec=pl.BlockSpec((None, fb, d), lambda i, j, k: (i, 0, 0)),
        out_shape=jax.ShapeDtypeStruct((4, fb, d), F32), acc_shape=(fb, d))
    du, dconv_w, dconv_b = _ffn_hidden_bwd(u, dhidden, conv_w, conv_b)
    da = _matmul(
        f"ffn_up_bwd_{tag}", du, w_up, dims=NT, grid=(s // tm, 1, N_DEV),
        a_spec=pl.BlockSpec((None, None, tm, fb), lambda i, j, k: (k // 4, k % 4, i, 0)),
        b_spec=pl.BlockSpec((None, d, fb), lambda i, j, k: (k, 0, 0)),
        o_spec=pl.BlockSpec((tm, d), lambda i, j, k: (i, 0)),
        out_shape=jax.ShapeDtypeStruct((s, d), F32), acc_shape=(tm, d))
    dw_up = _matmul(
        f"ffn_up_grad_{tag}", a, du, dims=TN, grid=(1, N_DEV, s // tm),
        a_spec=pl.BlockSpec((tm, d), lambda i, j, k: (k, 0)),
        b_spec=pl.BlockSpec((None, None, tm, fb), lambda i, j, k: (j // 4, j % 4, k, 0)),
        o_spec=pl.BlockSpec((None, d, fb), lambda i, j, k: (j, 0, 0)),
        out_shape=jax.ShapeDtypeStruct((N_DEV, d, fb), F32), acc_shape=(d, fb))
    dh, (dgain,) = _rmsnorm_bwd(f"ffn_norm_bwd_{tag}", h, dout, [(da, gain)])
    return dh, dgain, dw_up, dw_down, dconv_w, dconv_b


def _to_heads(a, n_heads):
    s = a.shape[0]
    return a.reshape(s, n_heads, ATT_HEAD_DIM).transpose(1, 0, 2)


def _from_heads(a):
    n_heads, s, hd = a.shape
    return a.transpose(1, 0, 2).reshape(s, n_heads * hd)


def kernel(x, hg_norm, hg_w_in, hg_lb_logits, hg_out_norm, hg_w_out, kv_norm, w_kv, attn_norm, attn_w_q, attn_sinks, attn_w_o, ffn_norm, ffn_w_up, ffn_conv_w, ffn_conv_b, ffn_w_down, final_norm, loss_target, m_hg_norm, m_hg_w_in, m_hg_lb_logits, m_hg_out_norm, m_hg_w_out, m_kv_norm, m_w_kv, m_attn_norm, m_attn_w_q, m_attn_sinks, m_attn_w_o, m_ffn_norm, m_ffn_w_up, m_ffn_conv_w, m_ffn_conv_b, m_ffn_w_down, m_final_norm, v_hg_norm, v_hg_w_in, v_hg_lb_logits, v_hg_out_norm, v_hg_w_out, v_kv_norm, v_w_kv, v_attn_norm, v_attn_w_q, v_attn_sinks, v_attn_w_o, v_ffn_norm, v_ffn_w_up, v_ffn_conv_w, v_ffn_conv_b, v_ffn_w_down, v_final_norm):
    _, s, d = x.shape
    x0, target = x[0], loss_target[0]
    half = hg_w_in.shape[2]
    fs = ffn_conv_w.shape[2]
    fb = 2 * fs
    kvd = w_kv.shape[1]
    nq = d // ATT_HEAD_DIM
    tm = _row_tile(s)

    (w_in, w_out, w_kvg, w_q, w_o, w_up0, w_up1, w_dn0, w_dn1, g_hgn, g_lbl, g_cw) = _all_gather(
        "gather_weights",
        [hg_w_in[0].astype(BF16), hg_w_out[0].astype(BF16), w_kv.astype(BF16), attn_w_q[0].astype(BF16), attn_w_o[0].astype(BF16),
         ffn_w_up[0].astype(BF16), ffn_w_up[1].astype(BF16), ffn_w_down[0].astype(BF16), ffn_w_down[1].astype(BF16),
         hg_norm, hg_lb_logits, ffn_conv_w], HBM_SPEC)
    w_out, w_kvg, w_q, w_o = w_out.reshape(d, d), w_kvg.reshape(d, kvd), w_q.reshape(d, d), w_o.reshape(d, d)
    w_dn = [w_dn0.reshape(4, fb, d), w_dn1.reshape(4, fb, d)]
    w_up = [w_up0, w_up1]
    hgn = g_hgn.reshape(1, d)
    lbl = g_lbl.transpose(1, 0, 2).reshape(2, d)
    conv_w = [g_cw[:, layer].reshape(4, 2, CONV_WIDTH, fs).transpose(0, 2, 1, 3).reshape(4, CONV_WIDTH, fb) for layer in range(2)]
    conv_b = [ffn_conv_b[layer].reshape(4, 1, fb) for layer in range(2)]
    gains = [ffn_norm[0:1], ffn_norm[1:2]]
    kvn, fin = kv_norm.reshape(1, d), final_norm.reshape(1, d)
    sinks = attn_sinks.reshape(nq, 1, 1)

    a0, = _rmsnorm_cast("hg_norm", x0, [hgn])
    p = _matmul(
        "hg_in", a0, w_in, dims=NN, grid=(s // tm, N_DEV, 1),
        a_spec=pl.BlockSpec((tm, d), lambda i, j, k: (i, 0)),
        b_spec=pl.BlockSpec((None, d, half), lambda i, j, k: (j, 0, 0)),
        o_spec=pl.BlockSpec((None, tm, half), lambda i, j, k: (j // 2, i, j % 2)),
        out_shape=jax.ShapeDtypeStruct((4, s, d), BF16), acc_shape=(8, 128))
    o, og, states = _hgrn2_fwd(p, lbl, hg_out_norm)
    x1 = _mm_rows("hg_out", og, w_out, out_dtype=F32, add=x0)
    x2, saved0 = _ffn_forward("0", x1, gains[0], w_up[0], w_dn[0], conv_w[0], conv_b[0])
    akv, a2 = _rmsnorm_cast("attn_norms", x2, [kvn, attn_norm])
    kv = _mm_rows("kv_proj", akv, w_kvg, out_dtype=BF16)
    qh = _to_heads(_mm_rows("q_proj", a2, w_q, out_dtype=BF16), nq)
    kh, vh = _to_heads(kv[:, :kvd // 2], ATT_KV_HEADS), _to_heads(kv[:, kvd // 2:], ATT_KV_HEADS)
    atth = _attn_fwd(qh, kh, vh, sinks)
    att = _from_heads(atth)
    x3 = _mm_rows("attn_out", att, w_o, out_dtype=F32, add=x2)
    x4, saved1 = _ffn_forward("1", x3, gains[1], w_up[1], w_dn[1], conv_w[1], conv_b[1])
    dx4, d_fin, loss_part = _loss_head(x4, fin, target)

    dx3, d_fn1, dw_up1, dw_dn1, dcw1, dcb1 = _ffn_backward("1", x3, gains[1], w_up[1], w_dn[1], conv_w[1], conv_b[1], saved1, dx4)
    datth = _to_heads(_mm_rows_nt("attn_out_bwd", dx3, w_o, out_dtype=BF16), nq)
    dw_o = _mm_tn("attn_out_grad", att, dx3)
    dqh, dkc, dkp, dvc, dvp, dsink = _attn_bwd(qh, kh, vh, atth, datth, sinks)
    shift = lambda cur, prv: cur + jnp.pad(prv[:, WINDOW:], ((0, 0), (0, WINDOW), (0, 0)))
    dq = _from_heads(dqh)
    dkv = jnp.concatenate([_from_heads(shift(dkc, dkp)), _from_heads(shift(dvc, dvp))], axis=1).astype(BF16)
    da2 = _mm_rows_nt("q_proj_bwd", dq, w_q, out_dtype=F32)
    dw_q = _mm_tn("q_proj_grad", a2, dq)
    dakv = _mm_rows_nt("kv_proj_bwd", dkv, w_kvg, out_dtype=F32)
    dw_kv = _mm_tn("kv_proj_grad", akv, dkv)
    dx2, (d_kvn, d_attn) = _rmsnorm_bwd("attn_norms_bwd", x2, dx3, [(dakv, kvn), (da2, attn_norm)])
    dx1, d_fn0, dw_up0, dw_dn0, dcw0, dcb0 = _ffn_backward("0", x1, gains[0], w_up[0], w_dn[0], conv_w[0], conv_b[0], saved0, dx2)
    dog = _mm_rows_nt("hg_out_bwd", dx1, w_out, out_dtype=F32)
    dw_out = _mm_tn("hg_out_grad", og, dx1)
    dp, d_lbl, d_ogain = _hgrn2_bwd(p, lbl, hg_out_norm, o, dog, states)
    da0 = _matmul(
        "hg_in_bwd", dp, w_in, dims=NT, grid=(s // tm, 1, N_DEV),
        a_spec=pl.BlockSpec((None, tm, half), lambda i, j, k: (k // 2, i, k % 2)),
        b_spec=pl.BlockSpec((None, d, half), lambda i, j, k: (k, 0, 0)),
        o_spec=pl.BlockSpec((tm, d), lambda i, j, k: (i, 0)),
        out_shape=jax.ShapeDtypeStruct((s, d), F32), acc_shape=(tm, d))
    dw_in = _matmul(
        "hg_in_grad", a0, dp, dims=TN, grid=(1, N_DEV, s // tm),
        a_spec=pl.BlockSpec((tm, d), lambda i, j, k: (k, 0)),
        b_spec=pl.BlockSpec((None, tm, half), lambda i, j, k: (j // 2, k, j % 2)),
        o_spec=pl.BlockSpec((None, d, half), lambda i, j, k: (j, 0, 0)),
        out_shape=jax.ShapeDtypeStruct((N_DEV, d, half), F32), acc_shape=(d, half))
    dx0, (d_hgn,) = _rmsnorm_bwd("hg_norm_bwd", x0, dx1, [(da0, hgn)])

    rows = d // N_DEV
    landed = _exchange("scatter_grads", [
        dw_in, dw_out.reshape(N_DEV, rows, d), dw_kv.reshape(N_DEV, rows, kvd), dw_q.reshape(N_DEV, rows, d), dw_o.reshape(N_DEV, rows, d),
        dw_up0, dw_up1, dw_dn0.reshape(N_DEV, fs, d), dw_dn1.reshape(N_DEV, fs, d)])
    big = {}
    for tag, w, m, v, part in [
            ("hg_w_in", hg_w_in[0], m_hg_w_in[0], v_hg_w_in[0], landed[0]), ("hg_w_out", hg_w_out[0], m_hg_w_out[0], v_hg_w_out[0], landed[1]),
            ("w_kv", w_kv, m_w_kv, v_w_kv, landed[2]), ("attn_w_q", attn_w_q[0], m_attn_w_q[0], v_attn_w_q[0], landed[3]),
            ("attn_w_o", attn_w_o[0], m_attn_w_o[0], v_attn_w_o[0], landed[4]),
            ("ffn_w_up0", ffn_w_up[0], m_ffn_w_up[0], v_ffn_w_up[0], landed[5]), ("ffn_w_up1", ffn_w_up[1], m_ffn_w_up[1], v_ffn_w_up[1], landed[6]),
            ("ffn_w_down0", ffn_w_down[0], m_ffn_w_down[0], v_ffn_w_down[0], landed[7]),
            ("ffn_w_down1", ffn_w_down[1], m_ffn_w_down[1], v_ffn_w_down[1], landed[8])]:
        big[tag] = _adamw_shard("adamw_" + tag, w, m, v, part)
    lead = lambda tag: [a[None] for a in big[tag]]
    pair = lambda tag: [jnp.stack([a, b]) for a, b in zip(big[tag + "0"], big[tag + "1"])]

    as_blocks = lambda a, r: a.reshape(r, N_DEV, -1).transpose(1, 0, 2).reshape(N_DEV * r, -1)
    d_cw = jnp.concatenate([g.transpose(1, 0, 2).reshape(CONV_WIDTH, 4 * fb) for g in (dcw0, dcw1)], axis=0)
    parts = [d_fin, jnp.concatenate([d_fn0, d_fn1], axis=0), jnp.concatenate([dcb0.reshape(1, 4 * fb), dcb1.reshape(1, 4 * fb)], axis=0),
             as_blocks(d_cw, 2 * CONV_WIDTH), d_attn, jnp.sum(dsink[:, :, 0, 0], axis=1).reshape(1, nq), d_kvn, d_ogain,
             as_blocks(d_hgn, 1), as_blocks(d_lbl, 2), loss_part]
    offsets, _, _ = _pack_rows(parts)
    gathered, = _all_gather("gather_small_grads", [_pack(parts)], VMEM_SPEC)
    two = lambda a: a.reshape(-1, a.shape[-1])
    small = [(fin, m_final_norm.reshape(1, d), v_final_norm.reshape(1, d), False), (ffn_norm, m_ffn_norm, v_ffn_norm, False),
             (ffn_conv_b, m_ffn_conv_b, v_ffn_conv_b, False), (two(ffn_conv_w), two(m_ffn_conv_w), two(v_ffn_conv_w), True),
             (attn_norm, m_attn_norm, v_attn_norm, False), (attn_sinks, m_attn_sinks, v_attn_sinks, False),
             (kvn, m_kv_norm.reshape(1, d), v_kv_norm.reshape(1, d), False), (hg_out_norm, m_hg_out_norm, v_hg_out_norm, False),
             (hg_norm, m_hg_norm, v_hg_norm, True), (hg_lb_logits, m_hg_lb_logits, v_hg_lb_logits, True)]
    res = _adamw_small(gathered, offsets, small)
    names = ["final_norm", "ffn_norm", "ffn_conv_b", "ffn_conv_w", "attn_norm", "attn_sinks", "kv_norm", "hg_out_norm", "hg_norm", "hg_lb_logits"]
    shapes = {"final_norm": final_norm.shape, "kv_norm": kv_norm.shape, "ffn_conv_w": ffn_conv_w.shape}
    out = {n: [a.reshape(shapes[n]) if n in shapes else a for a in res[4 * i:4 * i + 4]] for i, n in enumerate(names)}
    out.update(hg_w_in=lead("hg_w_in"), hg_w_out=lead("hg_w_out"), w_kv=big["w_kv"], attn_w_q=lead("attn_w_q"), attn_w_o=lead("attn_w_o"),
               ffn_w_up=pair("ffn_w_up"), ffn_w_down=pair("ffn_w_down"))
    order = ["hg_norm", "hg_w_in", "hg_lb_logits", "hg_out_norm", "hg_w_out", "kv_norm", "w_kv", "attn_norm", "attn_w_q", "attn_sinks",
             "attn_w_o", "ffn_norm", "ffn_w_up", "ffn_conv_w", "ffn_conv_b", "ffn_w_down", "final_norm"]
    loss = res[-1][0, 0]
    return (loss, dx0[None], *[out[n][0] for n in order], *[out[n][1] for n in order], *[out[n][2] for n in order], *[out[n][3] for n in order])
```

```python
import functools
import math

import jax
import jax.numpy as jnp
from jax import lax
from jax.experimental import pallas as pl
from jax.experimental.pallas import tpu as pltpu

F32 = jnp.float32
BF16 = jnp.bfloat16

EPS = 1e-6
HG_EXPAND = 128
HG_CHUNK = 32
ATT_HEAD_DIM = 64
ATT_KV_HEADS = 2
WINDOW = 128
CONV_WIDTH = 3
ADAM_LR = 0.001
ADAM_B1 = 0.9
ADAM_B2 = 0.999
ADAM_EPS = 1e-08
ADAM_WD = 0.01
ADAM_STEP = 10

N_DEV = 8
VMEM_LIMIT = 48 * 1024 * 1024
NEG = -1e30

NN = (((1,), (0,)), ((), ()))
NT = (((1,), (1,)), ((), ()))
TN = (((0,), (0,)), ((), ()))
MESH = pl.DeviceIdType.MESH


def _dot(a, b, dims=NN):
    return lax.dot_general(a.astype(BF16), b.astype(BF16), dims, preferred_element_type=F32)


def _sigmoid(x):
    return 1.0 / (1.0 + jnp.exp(-x))


def _silu(x):
    return x * _sigmoid(x)


def _dsilu(x):
    s = _sigmoid(x)
    return s * (1.0 + x * (1.0 - s))


def _params(semantics):
    return pltpu.CompilerParams(dimension_semantics=semantics, vmem_limit_bytes=VMEM_LIMIT)


def _row_tile(rows, want=512):
    return min(rows, want)


MM_ROWS = 1024


def _matmul(name, a, b, *, dims, grid, a_spec, b_spec, o_spec, out_shape, acc_shape, add=None, add_spec=None):
    nk = grid[2]

    def body(*refs):
        if add is None:
            a_ref, b_ref, o_ref, acc = refs
        else:
            a_ref, b_ref, add_ref, o_ref, acc = refs
        k = pl.program_id(2)
        part = _dot(a_ref[...], b_ref[...], dims)

        def finish(total):
            if add is not None:
                total = total + add_ref[...]
            o_ref[...] = total.astype(o_ref.dtype)

        if nk == 1:
            finish(part)
        else:
            @pl.when(k == 0)
            def _():
                acc[...] = part

            @pl.when(k > 0)
            def _():
                acc[...] += part

            @pl.when(k == nk - 1)
            def _():
                finish(acc[...])

    in_specs = [a_spec, b_spec] + ([] if add is None else [add_spec])
    args = (a, b) + (() if add is None else (add,))
    return pl.pallas_call(
        body, name=name, grid=grid, in_specs=in_specs, out_specs=o_spec, out_shape=out_shape,
        scratch_shapes=[pltpu.VMEM(acc_shape, F32)],
        compiler_params=_params(("parallel", "parallel", "arbitrary")),
    )(*args)


def _mm_rows(name, a, w, *, out_dtype, add=None):
    s, kdim = a.shape
    n = w.shape[1]
    tm = _row_tile(s, MM_ROWS)
    return _matmul(
        name, a, w, dims=NN, grid=(s // tm, 1, 1),
        a_spec=pl.BlockSpec((tm, kdim), lambda i, j, k: (i, 0)),
        b_spec=pl.BlockSpec((kdim, n), lambda i, j, k: (0, 0)),
        o_spec=pl.BlockSpec((tm, n), lambda i, j, k: (i, 0)),
        out_shape=jax.ShapeDtypeStruct((s, n), out_dtype), acc_shape=(8, 128),
        add=add, add_spec=None if add is None else pl.BlockSpec((tm, n), lambda i, j, k: (i, 0)),
    )


def _mm_rows_nt(name, a, w, *, out_dtype):
    s, n = a.shape
    kdim = w.shape[0]
    tm = _row_tile(s, MM_ROWS)
    return _matmul(
        name, a, w, dims=NT, grid=(s // tm, 1, 1),
        a_spec=pl.BlockSpec((tm, n), lambda i, j, k: (i, 0)),
        b_spec=pl.BlockSpec((kdim, n), lambda i, j, k: (0, 0)),
        o_spec=pl.BlockSpec((tm, kdim), lambda i, j, k: (i, 0)),
        out_shape=jax.ShapeDtypeStruct((s, kdim), out_dtype), acc_shape=(8, 128),
    )


def _mm_tn(name, a, g):
    s, m = a.shape
    n = g.shape[1]
    ts = _row_tile(s, MM_ROWS)
    return _matmul(
        name, a, g, dims=TN, grid=(1, 1, s // ts),
        a_spec=pl.BlockSpec((ts, m), lambda i, j, k: (k, 0)),
        b_spec=pl.BlockSpec((ts, n), lambda i, j, k: (k, 0)),
        o_spec=pl.BlockSpec((m, n), lambda i, j, k: (0, 0)),
        out_shape=jax.ShapeDtypeStruct((m, n), BF16), acc_shape=(m, n),
    )


def _rmsnorm_cast(name, h, gains):
    s, d = h.shape
    tm = _row_tile(s)
    n = len(gains)

    def body(*refs):
        h_ref, g_refs, o_refs = refs[0], refs[1:1 + n], refs[1 + n:]
        xv = h_ref[...]
        xhat = xv * lax.rsqrt(jnp.mean(xv * xv, axis=-1, keepdims=True) + EPS)
        for g_ref, o_ref in zip(g_refs, o_refs):
            o_ref[...] = (xhat * g_ref[...]).astype(BF16)

    row = pl.BlockSpec((tm, d), lambda i: (i, 0))
    vec = pl.BlockSpec((1, d), lambda i: (0, 0))
    return pl.pallas_call(
        body, name=name, grid=(s // tm,), in_specs=[row] + [vec] * n, out_specs=[row] * n,
        out_shape=[jax.ShapeDtypeStruct((s, d), BF16)] * n, compiler_params=_params(("parallel",)),
    )(h, *gains)


def _rmsnorm_bwd(name, h, dres, branches):
    s, d = h.shape
    tm = _row_tile(s)
    n = len(branches)

    def body(*refs):
        h_ref, dres_ref = refs[0], refs[1]
        da_refs, g_refs = refs[2:2 + n], refs[2 + n:2 + 2 * n]
        dh_ref, dg_refs = refs[2 + 2 * n], refs[3 + 2 * n:]
        i = pl.program_id(0)
        xv = h_ref[...]
        r = lax.rsqrt(jnp.mean(xv * xv, axis=-1, keepdims=True) + EPS)
        xhat = xv * r
        total = dres_ref[...]
        for da_ref, g_ref, dg_ref in zip(da_refs, g_refs, dg_refs):
            da = da_ref[...]
            dgain = jnp.sum(da * xhat, axis=0, keepdims=True)

            @pl.when(i == 0)
            def _():
                dg_ref[...] = dgain

            @pl.when(i > 0)
            def _():
                dg_ref[...] += dgain

            dxhat = da * g_ref[...]
            total = total + r * (dxhat - xhat * jnp.mean(dxhat * xhat, axis=-1, keepdims=True))
        dh_ref[...] = total

    row = pl.BlockSpec((tm, d), lambda i: (i, 0))
    vec = pl.BlockSpec((1, d), lambda i: (0, 0))
    outs = pl.pallas_call(
        body, name=name, grid=(s // tm,), in_specs=[row, row] + [row] * n + [vec] * n, out_specs=[row] + [vec] * n,
        out_shape=[jax.ShapeDtypeStruct((s, d), F32)] + [jax.ShapeDtypeStruct((1, d), F32)] * n,
        compiler_params=_params(("arbitrary",)),
    )(h, dres, *[b[0] for b in branches], *[b[1] for b in branches])
    return outs[0], outs[1:]


def _loss_head(h, gain, target):
    s, d = h.shape
    tm = _row_tile(s)

    def body(h_ref, g_ref, t_ref, dh_ref, dg_ref, loss_ref):
        i = pl.program_id(0)
        xv = h_ref[...]
        r = lax.rsqrt(jnp.mean(xv * xv, axis=-1, keepdims=True) + EPS)
        xhat = xv * r
        err = xhat * g_ref[...] - t_ref[...]
        dy = err * (1.0 / d)
        part = jnp.zeros((1, 128), F32) + 0.5 * jnp.sum(jnp.mean(err * err, axis=-1, keepdims=True))
        dgain = jnp.sum(dy * xhat, axis=0, keepdims=True)

        @pl.when(i == 0)
        def _():
            dg_ref[...] = dgain
            loss_ref[...] = part

        @pl.when(i > 0)
        def _():
            dg_ref[...] += dgain
            loss_ref[...] += part

        dxhat = dy * g_ref[...]
        dh_ref[...] = r * (dxhat - xhat * jnp.mean(dxhat * xhat, axis=-1, keepdims=True))

    row = pl.BlockSpec((tm, d), lambda i: (i, 0))
    vec = pl.BlockSpec((1, d), lambda i: (0, 0))
    return pl.pallas_call(
        body, name="loss_head", grid=(s // tm,), in_specs=[row, vec, row],
        out_specs=[row, vec, pl.BlockSpec((1, 128), lambda i: (0, 0))],
        out_shape=[jax.ShapeDtypeStruct((s, d), F32), jax.ShapeDtypeStruct((1, d), F32), jax.ShapeDtypeStruct((1, 128), F32)],
        compiler_params=_params(("arbitrary",)),
    )(h, gain, target)


def _bdot(a, b, ca, cb):
    return lax.dot_general(a.astype(BF16), b.astype(BF16), (((ca,), (cb,)), ((0,), (0,))), preferred_element_type=F32)


def _chunk_cumsum(xv, reverse=False):
    n = xv.shape[0]
    row = lax.broadcasted_iota(jnp.int32, xv.shape, 0) % HG_CHUNK
    step = 1
    while step < HG_CHUNK:
        if reverse:
            xv = xv + jnp.where(row < HG_CHUNK - step, pltpu.roll(xv, n - step, axis=0), 0.0)
        else:
            xv = xv + jnp.where(row >= step, pltpu.roll(xv, step, axis=0), 0.0)
        step *= 2
    return xv


def _hg_terms(p_ref, lbl_ref):
    pq = p_ref[0].astype(F32)
    pf = p_ref[1].astype(F32)
    lb = _sigmoid(lbl_ref[0:1, :] - lbl_ref[1:2, :])
    sig = _sigmoid(pf)
    fg = lb + (1.0 - lb) * sig
    nc = pq.shape[0] // HG_CHUNK
    chunks = lambda a: a.reshape(nc, HG_CHUNK, HG_EXPAND)
    q = chunks(_silu(pq) * HG_EXPAND ** -0.5)
    k = chunks(1.0 - fg)
    v = chunks(p_ref[2].astype(F32))
    g = chunks(_chunk_cumsum(jnp.log(fg)))
    gm = g[:, HG_CHUNK // 2 - 1:HG_CHUNK // 2, :]
    gl = g[:, HG_CHUNK - 1:HG_CHUNK, :]
    e_mid, e_inv, e_all, e_end = jnp.exp(g - gm), jnp.exp(gm - g), jnp.exp(g), jnp.exp(gl - g)
    terms = dict(q=q, k=k, v=v, qd=q * e_all, qt=q * e_mid, kt=k * e_inv, kd=k * e_end, e_last=jnp.exp(gl),
                 e_mid=e_mid, e_inv=e_inv, e_all=e_all, e_end=e_end)
    return terms, (pq, sig, fg, lb)


def _causal(nc):
    r = lax.broadcasted_iota(jnp.int32, (nc, HG_CHUNK, HG_CHUNK), 1)
    c = lax.broadcasted_iota(jnp.int32, (nc, HG_CHUNK, HG_CHUNK), 2)
    return r >= c


def _hgrn2_fwd(p, lb_logits, out_gain):
    _, s, d = p.shape
    heads = d // HG_EXPAND
    t = _row_tile(s)
    nc = t // HG_CHUNK

    def body(p_ref, lbl_ref, gain_ref, o_ref, og_ref, st_ref, state, decay):
        @pl.when(pl.program_id(1) == 0)
        def _():
            state[...] = jnp.zeros_like(state)

        tm, _ = _hg_terms(p_ref, lbl_ref)
        decay[...] = tm["e_last"]
        st_ref[...] = _bdot(tm["v"], tm["kd"], 1, 1)

        def chunk(c, carry):
            add = st_ref[c]
            st = state[...]
            st_ref[c] = st
            state[...] = st * decay[c] + add
            return carry

        lax.fori_loop(0, nc, chunk, 0)
        a = jnp.where(_causal(nc), _bdot(tm["qt"], tm["kt"], 2, 2), 0.0)
        ov = (_bdot(tm["qd"], st_ref[...], 2, 2) + _bdot(a, tm["v"], 2, 1)).reshape(t, HG_EXPAND)
        o_ref[...] = ov
        on = ov * lax.rsqrt(jnp.mean(ov * ov, axis=-1, keepdims=True) + EPS) * gain_ref[...]
        og_ref[...] = (on * _silu(p_ref[3].astype(F32))).astype(BF16)

    blk = pl.BlockSpec((t, HG_EXPAND), lambda h, b: (b, h))
    return pl.pallas_call(
        body, name="hgrn2_fwd", grid=(heads, s // t),
        in_specs=[pl.BlockSpec((4, t, HG_EXPAND), lambda h, b: (0, b, h)), pl.BlockSpec((2, HG_EXPAND), lambda h, b: (0, h)),
                  pl.BlockSpec((1, HG_EXPAND), lambda h, b: (0, 0))],
        out_specs=[blk, blk, pl.BlockSpec((None, nc, HG_EXPAND, HG_EXPAND), lambda h, b: (h, b, 0, 0))],
        out_shape=[jax.ShapeDtypeStruct((s, d), F32), jax.ShapeDtypeStruct((s, d), BF16),
                   jax.ShapeDtypeStruct((heads, s // HG_CHUNK, HG_EXPAND, HG_EXPAND), F32)],
        scratch_shapes=[pltpu.VMEM((HG_EXPAND, HG_EXPAND), F32), pltpu.VMEM((nc, 1, HG_EXPAND), F32)],
        compiler_params=_params(("parallel", "arbitrary")),
    )(p, lb_logits, out_gain)


def _hgrn2_bwd(p, lb_logits, out_gain, o, dog, states):
    _, s, d = p.shape
    heads = d // HG_EXPAND
    t = _row_tile(s)
    nc = t // HG_CHUNK
    nb = s // t

    def body(p_ref, lbl_ref, gain_ref, o_ref, dog_ref, st_ref, dp_ref, dlbl_ref, dgain_ref, dstate, decay, dst_s):
        h, b = pl.program_id(0), pl.program_id(1)

        @pl.when(b == 0)
        def _():
            dstate[...] = jnp.zeros_like(dstate)

        tm, (pq, sig, fg, lb) = _hg_terms(p_ref, lbl_ref)
        pg = p_ref[3].astype(F32)
        ov = o_ref[...]
        r = lax.rsqrt(jnp.mean(ov * ov, axis=-1, keepdims=True) + EPS)
        ohat = ov * r
        dogv = dog_ref[...]
        d_on = dogv * _silu(pg)
        dp_ref[3] = (dogv * ohat * gain_ref[...] * _dsilu(pg)).astype(BF16)
        dgain = jnp.sum(d_on * ohat, axis=0, keepdims=True)

        @pl.when((h == 0) & (b == 0))
        def _():
            dgain_ref[...] = dgain

        @pl.when((h > 0) | (b > 0))
        def _():
            dgain_ref[...] += dgain

        dohat = d_on * gain_ref[...]
        do = (r * (dohat - ohat * jnp.mean(dohat * ohat, axis=-1, keepdims=True))).reshape(nc, HG_CHUNK, HG_EXPAND)

        decay[...] = tm["e_last"]
        dst_s[...] = _bdot(do, tm["qd"], 1, 1)

        def chunk(i, carry):
            c = nc - 1 - i
            add = dst_s[c]
            dst = dstate[...]
            dst_s[c] = dst
            dstate[...] = dst * decay[c] + add
            return carry

        lax.fori_loop(0, nc, chunk, 0)
        st, dst = st_ref[...], dst_s[...]
        causal = _causal(nc)
        a = jnp.where(causal, _bdot(tm["qt"], tm["kt"], 2, 2), 0.0)
        da = jnp.where(causal, _bdot(do, tm["v"], 2, 2), 0.0)
        dqt = _bdot(da, tm["kt"], 2, 1)
        dkt = _bdot(da, tm["qt"], 1, 1)
        dqd = _bdot(do, st, 2, 1)
        dkd = _bdot(tm["v"], dst, 2, 1)
        dv = _bdot(a, do, 1, 1) + _bdot(tm["kd"], dst, 2, 2)
        dq = dqt * tm["e_mid"] + dqd * tm["e_all"]
        dk = dkt * tm["e_inv"] + dkd * tm["e_end"]
        dg = dqt * tm["qt"] - dkt * tm["kt"] + dqd * tm["qd"] - dkd * tm["kd"]
        dgl = jnp.sum(dkd * tm["kd"], axis=1, keepdims=True) + tm["e_last"] * jnp.sum(dst * st, axis=1, keepdims=True)
        last_row = lax.broadcasted_iota(jnp.int32, (nc, HG_CHUNK, HG_EXPAND), 1) == HG_CHUNK - 1
        flat = lambda a3: a3.reshape(t, HG_EXPAND)
        dlf = _chunk_cumsum(flat(dg + jnp.where(last_row, dgl, 0.0)), reverse=True)
        dfg = dlf / fg - flat(dk)
        dlb = jnp.sum(dfg * (1.0 - sig), axis=0, keepdims=True)
        dl0 = dlb * lb * (1.0 - lb)
        dlbl = jnp.concatenate([dl0, -dl0], axis=0)

        @pl.when(b == 0)
        def _():
            dlbl_ref[...] = dlbl

        @pl.when(b > 0)
        def _():
            dlbl_ref[...] += dlbl

        dp_ref[0] = (flat(dq) * HG_EXPAND ** -0.5 * _dsilu(pq)).astype(BF16)
        dp_ref[1] = (dfg * (1.0 - lb) * sig * (1.0 - sig)).astype(BF16)
        dp_ref[2] = flat(dv).astype(BF16)

    blk = pl.BlockSpec((t, HG_EXPAND), lambda h, b: (nb - 1 - b, h))
    pblk = pl.BlockSpec((4, t, HG_EXPAND), lambda h, b: (0, nb - 1 - b, h))
    return pl.pallas_call(
        body, name="hgrn2_bwd", grid=(heads, nb),
        in_specs=[pblk, pl.BlockSpec((2, HG_EXPAND), lambda h, b: (0, h)), pl.BlockSpec((1, HG_EXPAND), lambda h, b: (0, 0)),
                  blk, blk, pl.BlockSpec((None, nc, HG_EXPAND, HG_EXPAND), lambda h, b: (h, nb - 1 - b, 0, 0))],
        out_specs=[pblk, pl.BlockSpec((2, HG_EXPAND), lambda h, b: (0, h)), pl.BlockSpec((1, HG_EXPAND), lambda h, b: (0, 0))],
        out_shape=[jax.ShapeDtypeStruct((4, s, d), BF16), jax.ShapeDtypeStruct((2, d), F32), jax.ShapeDtypeStruct((1, HG_EXPAND), F32)],
        scratch_shapes=[pltpu.VMEM((HG_EXPAND, HG_EXPAND), F32), pltpu.VMEM((nc, 1, HG_EXPAND), F32),
                        pltpu.VMEM((nc, HG_EXPAND, HG_EXPAND), F32)],
        compiler_params=_params(("arbitrary", "arbitrary")),
    )(p, lb_logits, out_gain, o, dog, states)


HALO = 8


def _shift_down(xv, n):
    return pltpu.roll(xv, n, axis=0)


def _shift_up(xv, n):
    return pltpu.roll(xv, xv.shape[0] - n, axis=0)


def _ffn_hidden(u, conv_w, conv_b):
    _, nj, s, fb = u.shape
    tm = _row_tile(s)
    per = tm // HALO

    def body(gate_ref, prev_ref, val_ref, w_ref, b_ref, h_ref):
        i = pl.program_id(1)
        prev = jnp.where(i > 0, prev_ref[...].astype(F32), 0.0)
        ext = jnp.concatenate([prev, gate_ref[...].astype(F32)], axis=0)
        conv = b_ref[...] + w_ref[2:3, :] * ext[HALO:]
        conv = conv + w_ref[1:2, :] * _shift_down(ext, 1)[HALO:]
        conv = conv + w_ref[0:1, :] * _shift_down(ext, 2)[HALO:]
        h_ref[...] = (_silu(conv) * val_ref[...].astype(F32)).astype(BF16)

    return pl.pallas_call(
        body, name="ffn_hidden", grid=(nj, s // tm),
        in_specs=[pl.BlockSpec((None, None, tm, fb), lambda j, i: (0, j, i, 0)),
                  pl.BlockSpec((None, None, HALO, fb), lambda j, i: (0, j, jnp.maximum(i * per - 1, 0), 0)),
                  pl.BlockSpec((None, None, tm, fb), lambda j, i: (1, j, i, 0)),
                  pl.BlockSpec((None, CONV_WIDTH, fb), lambda j, i: (j, 0, 0)),
                  pl.BlockSpec((None, 1, fb), lambda j, i: (j, 0, 0))],
        out_specs=pl.BlockSpec((None, tm, fb), lambda j, i: (j, i, 0)),
        out_shape=jax.ShapeDtypeStruct((nj, s, fb), BF16), compiler_params=_params(("parallel", "parallel")),
    )(u, u, u, conv_w, conv_b)


def _ffn_hidden_bwd(u, dh, conv_w, conv_b):
    _, nj, s, fb = u.shape
    tm = _row_tile(s)
    per = tm // HALO
    nblk = s // HALO
    ni = s // tm

    def body(gate_ref, gprev_ref, gnext_ref, val_ref, vnext_ref, dh_ref, dhnext_ref, w_ref, b_ref, du_ref, dw_ref, db_ref):
        i = pl.program_id(1)
        has_next = i < ni - 1
        gprev = jnp.where(i > 0, gprev_ref[...].astype(F32), 0.0)
        gext = jnp.concatenate([gprev, gate_ref[...].astype(F32), gnext_ref[...].astype(F32)], axis=0)
        vext = jnp.concatenate([val_ref[...].astype(F32), vnext_ref[...].astype(F32)], axis=0)
        dhext = jnp.concatenate([dh_ref[...].astype(F32), jnp.where(has_next, dhnext_ref[...].astype(F32), 0.0)], axis=0)
        g0 = gext[HALO:]
        g1 = _shift_down(gext, 1)[HALO:]
        g2 = _shift_down(gext, 2)[HALO:]
        conv = b_ref[...] + w_ref[2:3, :] * g0 + w_ref[1:2, :] * g1 + w_ref[0:1, :] * g2
        dconv = dhext * vext * _dsilu(conv)
        dgate = w_ref[2:3, :] * dconv + w_ref[1:2, :] * _shift_up(dconv, 1) + w_ref[0:1, :] * _shift_up(dconv, 2)
        du_ref[0] = dgate[:tm].astype(BF16)
        du_ref[1] = (dhext * _silu(conv))[:tm].astype(BF16)
        own = dconv[:tm]
        dw = jnp.concatenate([jnp.sum(own * g2[:tm], axis=0, keepdims=True), jnp.sum(own * g1[:tm], axis=0, keepdims=True),
                              jnp.sum(own * g0[:tm], axis=0, keepdims=True)], axis=0)
        db = jnp.sum(own, axis=0, keepdims=True)

        @pl.when(i == 0)
        def _():
            dw_ref[...] = dw
            db_ref[...] = db

        @pl.when(i > 0)
        def _():
            dw_ref[...] += dw
            db_ref[...] += db

    def tile(part):
        return pl.BlockSpec((None, None, tm, fb), lambda j, i: (part, j, i, 0))

    def after(part):
        return pl.BlockSpec((None, None, HALO, fb), lambda j, i: (part, j, jnp.minimum((i + 1) * per, nblk - 1), 0))

    return pl.pallas_call(
        body, name="ffn_hidden_bwd", grid=(nj, ni),
        in_specs=[tile(0), pl.BlockSpec((None, None, HALO, fb), lambda j, i: (0, j, jnp.maximum(i * per - 1, 0), 0)), after(0),
                  tile(1), after(1),
                  pl.BlockSpec((None, tm, fb), lambda j, i: (j, i, 0)),
                  pl.BlockSpec((None, HALO, fb), lambda j, i: (j, jnp.minimum((i + 1) * per, nblk - 1), 0)),
                  pl.BlockSpec((None, CONV_WIDTH, fb), lambda j, i: (j, 0, 0)), pl.BlockSpec((None, 1, fb), lambda j, i: (j, 0, 0))],
        out_specs=[pl.BlockSpec((2, None, tm, fb), lambda j, i: (0, j, i, 0)),
                   pl.BlockSpec((None, CONV_WIDTH, fb), lambda j, i: (j, 0, 0)), pl.BlockSpec((None, 1, fb), lambda j, i: (j, 0, 0))],
        out_shape=[jax.ShapeDtypeStruct((2, nj, s, fb), BF16), jax.ShapeDtypeStruct((nj, CONV_WIDTH, fb), F32),
                   jax.ShapeDtypeStruct((nj, 1, fb), F32)],
        compiler_params=_params(("parallel", "arbitrary")),
    )(u, u, u, u, u, dh, dh, conv_w, conv_b)


ATT_TILE = 512


def _attn_probs(q, kb, sink, head, first, n_heads):
    iq = lax.broadcasted_iota(jnp.int32, (WINDOW, 2 * WINDOW), 0)
    ik = lax.broadcasted_iota(jnp.int32, (WINDOW, 2 * WINDOW), 1)
    dist = iq + WINDOW - ik
    valid = (dist >= 0) & (dist < WINDOW) & (ik >= jnp.where(first, WINDOW, 0))
    slope = 2.0 ** (-8.0 * (head + 1) / n_heads)
    sc = jnp.where(valid, _dot(q, kb, NT) * ATT_HEAD_DIM ** -0.5 - slope * dist.astype(F32), NEG)
    m = jnp.maximum(jnp.max(sc, axis=-1, keepdims=True), sink)
    e = jnp.exp(sc - m)
    es = jnp.exp(sink - m)
    inv = 1.0 / (jnp.sum(e, axis=-1, keepdims=True) + es)
    return e * inv, es * inv


def _attn_specs(s, d, kvd, tq):
    per = tq // WINDOW
    return [pl.BlockSpec((tq, d), lambda i: (i, 0)), pl.BlockSpec((tq, kvd), lambda i: (i, 0)),
            pl.BlockSpec((WINDOW, kvd), lambda i: (jnp.maximum(i * per - 1, 0), 0))]


def _attn_fwd(q, kv, sinks):
    s, d = q.shape
    kvd = kv.shape[1]
    half = kvd // 2
    hd = ATT_HEAD_DIM
    nq = d // hd
    group = nq // ATT_KV_HEADS
    tq = min(s, ATT_TILE)
    per = tq // WINDOW

    def body(q_ref, kvc_ref, kvp_ref, sink_ref, o_ref, band):
        i = pl.program_id(0)
        band[0:WINDOW, :] = kvp_ref[...]
        band[WINDOW:, :] = kvc_ref[...]

        def block(b, carry):
            rows = pl.ds(pl.multiple_of(b * WINDOW, WINDOW), WINDOW)
            keys = pl.ds(pl.multiple_of(b * WINDOW, WINDOW), 2 * WINDOW)
            first = (i * per + b) == 0
            for pair in range(nq // 2):
                outs = []
                for h in (2 * pair, 2 * pair + 1):
                    g = h // group
                    p, _ = _attn_probs(q_ref[rows, h * hd:(h + 1) * hd], band[keys, g * hd:(g + 1) * hd], sink_ref[0, h], h, first, nq)
                    outs.append(_dot(p, band[keys, half + g * hd:half + (g + 1) * hd]))
                o_ref[rows, pair * 2 * hd:(pair + 1) * 2 * hd] = jnp.concatenate(outs, axis=1).astype(BF16)
            return carry

        lax.fori_loop(0, per, block, 0)

    return pl.pallas_call(
        body, name="attn_fwd", grid=(s // tq,),
        in_specs=_attn_specs(s, d, kvd, tq) + [pl.BlockSpec(memory_space=pltpu.SMEM)],
        out_specs=pl.BlockSpec((tq, d), lambda i: (i, 0)), out_shape=jax.ShapeDtypeStruct((s, d), BF16),
        scratch_shapes=[pltpu.VMEM((tq + WINDOW, kvd), BF16)], compiler_params=_params(("parallel",)),
    )(q, kv, kv, sinks)


def _attn_bwd(q, kv, o, do, sinks):
    s, d = q.shape
    kvd = kv.shape[1]
    half = kvd // 2
    hd = ATT_HEAD_DIM
    nq = d // hd
    group = nq // ATT_KV_HEADS
    tq = min(s, ATT_TILE)
    per = tq // WINDOW
    nt = s // tq

    def body(q_ref, kvc_ref, kvp_ref, o_ref, do_ref, sink_ref, dq_ref, dkvc_ref, dkvp_ref, ds_ref, band, dband):
        i = pl.program_id(0)
        band[0:WINDOW, :] = kvp_ref[...]
        band[WINDOW:, :] = kvc_ref[...]
        dband[...] = jnp.zeros_like(dband)
        ds_ref[...] = jnp.zeros_like(ds_ref)

        def block(b, carry):
            rows = pl.ds(pl.multiple_of(b * WINDOW, WINDOW), WINDOW)
            keys = pl.ds(pl.multiple_of(b * WINDOW, WINDOW), 2 * WINDOW)
            first = (i * per + b) == 0
            dks, dvs = [], []
            for g in range(ATT_KV_HEADS):
                kb = band[keys, g * hd:(g + 1) * hd]
                vb = band[keys, half + g * hd:half + (g + 1) * hd]
                dk = jnp.zeros((2 * WINDOW, hd), F32)
                dv = jnp.zeros((2 * WINDOW, hd), F32)
                dqs = []
                for j in range(group):
                    h = g * group + j
                    cols = slice(h * hd, (h + 1) * hd)
                    qv, dov = q_ref[rows, cols], do_ref[rows, cols]
                    p, ps = _attn_probs(qv, kb, sink_ref[0, h], h, first, nq)
                    dsum = jnp.sum(dov.astype(F32) * o_ref[rows, cols].astype(F32), axis=-1, keepdims=True)
                    dsc = p * (_dot(dov, vb, NT) - dsum) * ATT_HEAD_DIM ** -0.5
                    dqs.append(_dot(dsc, kb))
                    dk = dk + _dot(dsc, qv, TN)
                    dv = dv + _dot(p, dov, TN)
                    ds_ref[h:h + 1, :] += jnp.zeros((1, 128), F32) - jnp.sum(ps * dsum)
                    if j % 2 == 1:
                        dq_ref[rows, (h - 1) * hd:(h + 1) * hd] = jnp.concatenate(dqs[-2:], axis=1).astype(BF16)
                dks.append(dk)
                dvs.append(dv)
            dband[keys, 0:half] += jnp.concatenate(dks, axis=1)
            dband[keys, half:] += jnp.concatenate(dvs, axis=1)
            return carry

        lax.fori_loop(0, per, block, 0)
        dkvp_ref[...] = dband[0:WINDOW, :]
        dkvc_ref[...] = dband[WINDOW:, :]

    big = pl.BlockSpec((tq, d), lambda i: (i, 0))
    return pl.pallas_call(
        body, name="attn_bwd", grid=(nt,),
        in_specs=_attn_specs(s, d, kvd, tq) + [big, big, pl.BlockSpec(memory_space=pltpu.SMEM)],
        out_specs=[big, pl.BlockSpec((tq, kvd), lambda i: (i, 0)), pl.BlockSpec((None, WINDOW, kvd), lambda i: (i, 0, 0)),
                   pl.BlockSpec((None, nq, 128), lambda i: (i, 0, 0))],
        out_shape=[jax.ShapeDtypeStruct((s, d), BF16), jax.ShapeDtypeStruct((s, kvd), F32), jax.ShapeDtypeStruct((nt, WINDOW, kvd), F32),
                   jax.ShapeDtypeStruct((nt, nq, 128), F32)],
        scratch_shapes=[pltpu.VMEM((tq + WINDOW, kvd), BF16), pltpu.VMEM((tq + WINDOW, kvd), F32)],
        compiler_params=_params(("parallel",)),
    )(q, kv, kv, o, do, sinks)


HBM_SPEC = pl.BlockSpec(memory_space=pltpu.HBM)
VMEM_SPEC = pl.BlockSpec(memory_space=pltpu.VMEM)


def _place():
    return lax.axis_index("x"), lax.axis_index("y"), lax.axis_index("c")


def _flip(pos, r):
    return tuple(1 - p if (r >> (2 - a)) & 1 else p for a, p in enumerate(pos))


def _index(pos):
    return 4 * pos[0] + 2 * pos[1] + pos[2]


def _all_gather(name, shards, spec):
    n = len(shards)

    def body(*refs):
        x_refs, o_refs = refs[:n], refs[n:2 * n]
        send_sems, recv_sems, local_sems = refs[2 * n:]
        me = _place()
        sibling = _flip(me, 1)
        far = [_flip(me, r) for r in (4, 2, 6)]

        def copy(t, sem, block, to, src=None):
            rows = o_refs[t].at[_index(block)]
            return pltpu.make_async_remote_copy(
                src_ref=rows if src is None else src, dst_ref=rows, send_sem=send_sems.at[t, sem], recv_sem=recv_sems.at[t, sem],
                device_id=to, device_id_type=MESH)

        own = [pltpu.make_async_copy(x_refs[t], o_refs[t].at[_index(me)], local_sems.at[t]) for t in range(n)]
        for cp in own:
            cp.start()
        first = []
        for t in range(n):
            first.append(copy(t, 0, me, sibling, src=x_refs[t]))
            first += [copy(t, 1 + j, me, peer, src=x_refs[t]) for j, peer in enumerate(far)]
        for cp in first:
            cp.start()
        passed = []
        for j, peer in enumerate(far):
            for t in range(n):
                copy(t, 1 + j, peer, me).wait_recv()
                cp = copy(t, 4 + j, peer, sibling)
                cp.start()
                passed.append(cp)
        for t in range(n):
            copy(t, 0, sibling, me).wait_recv()
            for j, peer in enumerate(far):
                copy(t, 4 + j, _flip(peer, 1), me).wait_recv()
        for cp in first + passed:
            cp.wait_send()
        for cp in own:
            cp.wait()

    return pl.pallas_call(
        body, name=name, in_specs=[spec] * n, out_specs=[spec] * n,
        out_shape=[jax.ShapeDtypeStruct((N_DEV,) + sh.shape, sh.dtype) for sh in shards],
        scratch_shapes=[pltpu.SemaphoreType.DMA((n, 7)), pltpu.SemaphoreType.DMA((n, 7)), pltpu.SemaphoreType.DMA((n,))],
    )(*shards)


def _exchange(name, stacks):
    n = len(stacks)

    def body(*refs):
        x_refs, o_refs = refs[:n], refs[n:2 * n]
        send_sems, recv_sems, local_sems = refs[2 * n:]
        me = _place()
        mine = _index(me)
        own = [pltpu.make_async_copy(x_refs[t].at[mine], o_refs[t].at[mine], local_sems.at[t]) for t in range(n)]
        for cp in own:
            cp.start()

        def copy(t, r):
            peer = _flip(me, r)
            return pltpu.make_async_remote_copy(
                src_ref=x_refs[t].at[_index(peer)], dst_ref=o_refs[t].at[mine], send_sem=send_sems.at[t, r - 1],
                recv_sem=recv_sems.at[t, r - 1], device_id=peer, device_id_type=MESH)

        def arrival(t, r):
            peer = _flip(me, r)
            return pltpu.make_async_remote_copy(
                src_ref=x_refs[t].at[mine], dst_ref=o_refs[t].at[_index(peer)], send_sem=send_sems.at[t, r - 1],
                recv_sem=recv_sems.at[t, r - 1], device_id=peer, device_id_type=MESH)

        sent = [copy(t, r) for t in range(n) for r in range(1, N_DEV)]
        for cp in sent:
            cp.start()
        for t in range(n):
            for r in range(1, N_DEV):
                arrival(t, r).wait_recv()
        for cp in sent:
            cp.wait_send()
        for cp in own:
            cp.wait()

    return pl.pallas_call(
        body, name=name, in_specs=[HBM_SPEC] * n, out_specs=[HBM_SPEC] * n,
        out_shape=[jax.ShapeDtypeStruct(st.shape, st.dtype) for st in stacks],
        scratch_shapes=[pltpu.SemaphoreType.DMA((n, 7)), pltpu.SemaphoreType.DMA((n, 7)), pltpu.SemaphoreType.DMA((n,))],
    )(*stacks)


def _pack_rows(parts):
    offsets, row = [], 0
    for part in parts:
        offsets.append(row)
        row += part.shape[0]
    return offsets, -(-row // 8) * 8, -(-max(part.shape[1] for part in parts) // 128) * 128


def _pack(parts):
    offsets, rows, width = _pack_rows(parts)

    def body(*refs):
        o_ref = refs[-1]
        o_ref[...] = jnp.zeros_like(o_ref)
        for off, ref in zip(offsets, refs[:-1]):
            o_ref[off:off + ref.shape[0], 0:ref.shape[1]] = ref[...]

    return pl.pallas_call(body, name="pack_small_grads", in_specs=[VMEM_SPEC] * len(parts), out_specs=VMEM_SPEC,
                          out_shape=jax.ShapeDtypeStruct((rows, width), F32))(*parts)


def _adamw_math(w, g, m, v):
    m = ADAM_B1 * m + (1.0 - ADAM_B1) * g
    v = ADAM_B2 * v + (1.0 - ADAM_B2) * (g * g)
    m_hat = m / (1.0 - ADAM_B1 ** ADAM_STEP)
    v_hat = v / (1.0 - ADAM_B2 ** ADAM_STEP)
    return -ADAM_LR * (m_hat / (jnp.sqrt(v_hat) + ADAM_EPS) + ADAM_WD * w), m, v


def _adamw_shard(name, w, m, v, partials):
    rows, cols = w.shape
    tr = max(t for t in range(8, min(rows, 256) + 1, 8) if rows % t == 0)

    def body(w_ref, m_ref, v_ref, p_ref, g_ref, d_ref, nm_ref, nv_ref):
        g = p_ref[0].astype(F32)
        for dev in range(1, N_DEV):
            g = g + p_ref[dev].astype(F32)
        g_ref[...] = g
        d_ref[...], nm_ref[...], nv_ref[...] = _adamw_math(w_ref[...], g, m_ref[...], v_ref[...])

    blk = pl.BlockSpec((tr, cols), lambda i: (i, 0))
    return pl.pallas_call(
        body, name=name, grid=(rows // tr,), in_specs=[blk, blk, blk, pl.BlockSpec((N_DEV, tr, cols), lambda i: (0, i, 0))],
        out_specs=[blk] * 4, out_shape=[jax.ShapeDtypeStruct((rows, cols), F32)] * 4, compiler_params=_params(("parallel",)),
    )(w, m, v, partials)


def _adamw_small(gathered, offsets, entries):
    n = len(entries)

    def body(*refs):
        pack_ref = refs[0]
        w_refs, m_refs, v_refs = refs[1:1 + n], refs[1 + n:1 + 2 * n], refs[1 + 2 * n:1 + 3 * n]
        outs = refs[1 + 3 * n:]
        total = pack_ref[0]
        for dev in range(1, N_DEV):
            total = total + pack_ref[dev]
        mine = _index(_place())
        for e in range(n):
            rows, cols = w_refs[e].shape
            off = offsets[e]
            if entries[e][3]:
                g = jnp.zeros((rows, cols), F32)
                for dev in range(N_DEV):
                    g = g + jnp.where(mine == dev, total[off + dev * rows:off + (dev + 1) * rows, 0:cols], 0.0)
            else:
                g = total[off:off + rows, 0:cols]
            outs[4 * e][...] = g
            outs[4 * e + 1][...], outs[4 * e + 2][...], outs[4 * e + 3][...] = _adamw_math(w_refs[e][...], g, m_refs[e][...], v_refs[e][...])
        outs[4 * n][...] = total[offsets[n]:offsets[n] + 1, 0:128]

    shapes = []
    for w, _, _, _ in entries:
        shapes += [jax.ShapeDtypeStruct(w.shape, F32)] * 4
    shapes.append(jax.ShapeDtypeStruct((1, 128), F32))
    return pl.pallas_call(
        body, name="adamw_small", in_specs=[VMEM_SPEC] * (1 + 3 * n), out_specs=[VMEM_SPEC] * len(shapes), out_shape=shapes,
        compiler_params=pltpu.CompilerParams(vmem_limit_bytes=VMEM_LIMIT),
    )(gathered, *[e[0] for e in entries], *[e[1] for e in entries], *[e[2] for e in entries])


def _ffn_forward(tag, h, gain, w_up, w_down, conv_w, conv_b):
    s, d = h.shape
    fb = w_up.shape[2]
    tm = _row_tile(s, MM_ROWS)
    a, = _rmsnorm_cast(f"ffn_norm_{tag}", h, [gain])
    u = _matmul(
        f"ffn_up_{tag}", a, w_up, dims=NN, grid=(s // tm, N_DEV, 1),
        a_spec=pl.BlockSpec((tm, d), lambda i, j, k: (i, 0)),
        b_spec=pl.BlockSpec((None, d, fb), lambda i, j, k: (j, 0, 0)),
        o_spec=pl.BlockSpec((None, None, tm, fb), lambda i, j, k: (j // 4, j % 4, i, 0)),
        out_shape=jax.ShapeDtypeStruct((2, 4, s, fb), BF16), acc_shape=(8, 128))
    hidden = _ffn_hidden(u, conv_w, conv_b)
    out = _matmul(
        f"ffn_down_{tag}", hidden, w_down, dims=NN, grid=(s // tm, 1, 4),
        a_spec=pl.BlockSpec((None, tm, fb), lambda i, j, k: (k, i, 0)),
        b_spec=pl.BlockSpec((None, fb, d), lambda i, j, k: (k, 0, 0)),
        o_spec=pl.BlockSpec((tm, d), lambda i, j, k: (i, 0)),
        out_shape=jax.ShapeDtypeStruct((s, d), F32), acc_shape=(tm, d),
        add=h, add_spec=pl.BlockSpec((tm, d), lambda i, j, k: (i, 0)))
    return out, (a, u, hidden)


def _ffn_backward(tag, h, gain, w_up, w_down, conv_w, conv_b, saved, dout):
    a, u, hidden = saved
    s, d = h.shape
    fb = w_up.shape[2]
    tm = _row_tile(s, MM_ROWS)
    dhidden = _matmul(
        f"ffn_down_bwd_{tag}", dout, w_down, dims=NT, grid=(s // tm, 4, 1),
        a_spec=pl.BlockSpec((tm, d), lambda i, j, k: (i, 0)),
        b_spec=pl.BlockSpec((None, fb, d), lambda i, j, k: (j, 0, 0)),
        o_spec=pl.BlockSpec((None, tm, fb), lambda i, j, k: (j, i, 0)),
        out_shape=jax.ShapeDtypeStruct((4, s, fb), BF16), acc_shape=(8, 128))
    dw_down = _matmul(
        f"ffn_down_grad_{tag}", hidden, dout, dims=TN, grid=(4, 1, s // tm),
        a_spec=pl.BlockSpec((None, tm, fb), lambda i, j, k: (i, k, 0)),
        b_spec=pl.BlockSpec((tm, d), lambda i, j, k: (k, 0)),
        o_spec=pl.BlockSpec((None, fb, d), lambda i, j, k: (i, 0, 0)),
        out_shape=jax.ShapeDtypeStruct((4, fb, d), BF16), acc_shape=(fb, d))
    du, dconv_w, dconv_b = _ffn_hidden_bwd(u, dhidden, conv_w, conv_b)
    da = _matmul(
        f"ffn_up_bwd_{tag}", du, w_up, dims=NT, grid=(s // tm, 1, N_DEV),
        a_spec=pl.BlockSpec((None, None, tm, fb), lambda i, j, k: (k // 4, k % 4, i, 0)),
        b_spec=pl.BlockSpec((None, d, fb), lambda i, j, k: (k, 0, 0)),
        o_spec=pl.BlockSpec((tm, d), lambda i, j, k: (i, 0)),
        out_shape=jax.ShapeDtypeStruct((s, d), F32), acc_shape=(tm, d))
    dw_up = _matmul(
        f"ffn_up_grad_{tag}", a, du, dims=TN, grid=(1, N_DEV, s // tm),
        a_spec=pl.BlockSpec((tm, d), lambda i, j, k: (k, 0)),
        b_spec=pl.BlockSpec((None, None, tm, fb), lambda i, j, k: (j // 4, j % 4, k, 0)),
        o_spec=pl.BlockSpec((None, d, fb), lambda i, j, k: (j, 0, 0)),
        out_shape=jax.ShapeDtypeStruct((N_DEV, d, fb), BF16), acc_shape=(d, fb))
    dh, (dgain,) = _rmsnorm_bwd(f"ffn_norm_bwd_{tag}", h, dout, [(da, gain)])
    return dh, dgain, dw_up, dw_down, dconv_w, dconv_b


def kernel(x, hg_norm, hg_w_in, hg_lb_logits, hg_out_norm, hg_w_out, kv_norm, w_kv, attn_norm, attn_w_q, attn_sinks, attn_w_o, ffn_norm, ffn_w_up, ffn_conv_w, ffn_conv_b, ffn_w_down, final_norm, loss_target, m_hg_norm, m_hg_w_in, m_hg_lb_logits, m_hg_out_norm, m_hg_w_out, m_kv_norm, m_w_kv, m_attn_norm, m_attn_w_q, m_attn_sinks, m_attn_w_o, m_ffn_norm, m_ffn_w_up, m_ffn_conv_w, m_ffn_conv_b, m_ffn_w_down, m_final_norm, v_hg_norm, v_hg_w_in, v_hg_lb_logits, v_hg_out_norm, v_hg_w_out, v_kv_norm, v_w_kv, v_attn_norm, v_attn_w_q, v_attn_sinks, v_attn_w_o, v_ffn_norm, v_ffn_w_up, v_ffn_conv_w, v_ffn_conv_b, v_ffn_w_down, v_final_norm):
    _, s, d = x.shape
    x0, target = x[0], loss_target[0]
    half = hg_w_in.shape[2]
    fs = ffn_conv_w.shape[2]
    fb = 2 * fs
    kvd = w_kv.shape[1]
    nq = d // ATT_HEAD_DIM
    tm = _row_tile(s, MM_ROWS)

    (w_in, w_out, w_kvg, w_q, w_o, w_up0, w_up1, w_dn0, w_dn1, g_hgn, g_lbl, g_cw) = _all_gather(
        "gather_weights",
        [hg_w_in[0].astype(BF16), hg_w_out[0].astype(BF16), w_kv.astype(BF16), attn_w_q[0].astype(BF16), attn_w_o[0].astype(BF16),
         ffn_w_up[0].astype(BF16), ffn_w_up[1].astype(BF16), ffn_w_down[0].astype(BF16), ffn_w_down[1].astype(BF16),
         hg_norm, hg_lb_logits, ffn_conv_w], HBM_SPEC)
    w_out, w_kvg, w_q, w_o = w_out.reshape(d, d), w_kvg.reshape(d, kvd), w_q.reshape(d, d), w_o.reshape(d, d)
    w_dn = [w_dn0.reshape(4, fb, d), w_dn1.reshape(4, fb, d)]
    w_up = [w_up0, w_up1]
    hgn = g_hgn.reshape(1, d)
    lbl = g_lbl.transpose(1, 0, 2).reshape(2, d)
    conv_w = [g_cw[:, layer].reshape(4, 2, CONV_WIDTH, fs).transpose(0, 2, 1, 3).reshape(4, CONV_WIDTH, fb) for layer in range(2)]
    conv_b = [ffn_conv_b[layer].reshape(4, 1, fb) for layer in range(2)]
    gains = [ffn_norm[0:1], ffn_norm[1:2]]
    kvn, fin = kv_norm.reshape(1, d), final_norm.reshape(1, d)

    a0, = _rmsnorm_cast("hg_norm", x0, [hgn])
    p = _matmul(
        "hg_in", a0, w_in, dims=NN, grid=(s // tm, N_DEV, 1),
        a_spec=pl.BlockSpec((tm, d), lambda i, j, k: (i, 0)),
        b_spec=pl.BlockSpec((None, d, half), lambda i, j, k: (j, 0, 0)),
        o_spec=pl.BlockSpec((None, tm, half), lambda i, j, k: (j // 2, i, j % 2)),
        out_shape=jax.ShapeDtypeStruct((4, s, d), BF16), acc_shape=(8, 128))
    o, og, states = _hgrn2_fwd(p, lbl, hg_out_norm)
    x1 = _mm_rows("hg_out", og, w_out, out_dtype=F32, add=x0)
    x2, saved0 = _ffn_forward("0", x1, gains[0], w_up[0], w_dn[0], conv_w[0], conv_b[0])
    akv, a2 = _rmsnorm_cast("attn_norms", x2, [kvn, attn_norm])
    kv = _mm_rows("kv_proj", akv, w_kvg, out_dtype=BF16)
    q = _mm_rows("q_proj", a2, w_q, out_dtype=BF16)
    att = _attn_fwd(q, kv, attn_sinks)
    x3 = _mm_rows("attn_out", att, w_o, out_dtype=F32, add=x2)
    x4, saved1 = _ffn_forward("1", x3, gains[1], w_up[1], w_dn[1], conv_w[1], conv_b[1])
    dx4, d_fin, loss_part = _loss_head(x4, fin, target)

    dx3, d_fn1, dw_up1, dw_dn1, dcw1, dcb1 = _ffn_backward("1", x3, gains[1], w_up[1], w_dn[1], conv_w[1], conv_b[1], saved1, dx4)
    datt = _mm_rows_nt("attn_out_bwd", dx3, w_o, out_dtype=BF16)
    dw_o = _mm_tn("attn_out_grad", att, dx3)
    dq, dkv_own, dkv_before, dsink = _attn_bwd(q, kv, att, datt, attn_sinks)
    tiles = dkv_before.shape[0]
    dkv = dkv_own.reshape(tiles, s // tiles, kvd)
    dkv = jnp.concatenate([dkv[:, :-WINDOW], dkv[:, -WINDOW:] + jnp.pad(dkv_before[1:], ((0, 1), (0, 0), (0, 0)))], axis=1).reshape(s, kvd)
    da2 = _mm_rows_nt("q_proj_bwd", dq, w_q, out_dtype=F32)
    dw_q = _mm_tn("q_proj_grad", a2, dq)
    dakv = _mm_rows_nt("kv_proj_bwd", dkv, w_kvg, out_dtype=F32)
    dw_kv = _mm_tn("kv_proj_grad", akv, dkv)
    dx2, (d_kvn, d_attn) = _rmsnorm_bwd("attn_norms_bwd", x2, dx3, [(dakv, kvn), (da2, attn_norm)])
    dx1, d_fn0, dw_up0, dw_dn0, dcw0, dcb0 = _ffn_backward("0", x1, gains[0], w_up[0], w_dn[0], conv_w[0], conv_b[0], saved0, dx2)
    dog = _mm_rows_nt("hg_out_bwd", dx1, w_out, out_dtype=F32)
    dw_out = _mm_tn("hg_out_grad", og, dx1)
    dp, d_lbl, d_ogain = _hgrn2_bwd(p, lbl, hg_out_norm, o, dog, states)
    da0 = _matmul(
        "hg_in_bwd", dp, w_in, dims=NT, grid=(s // tm, 1, N_DEV),
        a_spec=pl.BlockSpec((None, tm, half), lambda i, j, k: (k // 2, i, k % 2)),
        b_spec=pl.BlockSpec((None, d, half), lambda i, j, k: (k, 0, 0)),
        o_spec=pl.BlockSpec((tm, d), lambda i, j, k: (i, 0)),
        out_shape=jax.ShapeDtypeStruct((s, d), F32), acc_shape=(tm, d))
    dw_in = _matmul(
        "hg_in_grad", a0, dp, dims=TN, grid=(1, N_DEV, s // tm),
        a_spec=pl.BlockSpec((tm, d), lambda i, j, k: (k, 0)),
        b_spec=pl.BlockSpec((None, tm, half), lambda i, j, k: (j // 2, k, j % 2)),
        o_spec=pl.BlockSpec((None, d, half), lambda i, j, k: (j, 0, 0)),
        out_shape=jax.ShapeDtypeStruct((N_DEV, d, half), BF16), acc_shape=(d, half))
    dx0, (d_hgn,) = _rmsnorm_bwd("hg_norm_bwd", x0, dx1, [(da0, hgn)])

    rows = d // N_DEV
    landed = _exchange("scatter_grads", [
        dw_in, dw_out.reshape(N_DEV, rows, d), dw_kv.reshape(N_DEV, rows, kvd), dw_q.reshape(N_DEV, rows, d), dw_o.reshape(N_DEV, rows, d),
        dw_up0, dw_up1, dw_dn0.reshape(N_DEV, fs, d), dw_dn1.reshape(N_DEV, fs, d)])
    big = {}
    for tag, w, m, v, part in [
            ("hg_w_in", hg_w_in[0], m_hg_w_in[0], v_hg_w_in[0], landed[0]), ("hg_w_out", hg_w_out[0], m_hg_w_out[0], v_hg_w_out[0], landed[1]),
            ("w_kv", w_kv, m_w_kv, v_w_kv, landed[2]), ("attn_w_q", attn_w_q[0], m_attn_w_q[0], v_attn_w_q[0], landed[3]),
            ("attn_w_o", attn_w_o[0], m_attn_w_o[0], v_attn_w_o[0], landed[4]),
            ("ffn_w_up0", ffn_w_up[0], m_ffn_w_up[0], v_ffn_w_up[0], landed[5]), ("ffn_w_up1", ffn_w_up[1], m_ffn_w_up[1], v_ffn_w_up[1], landed[6]),
            ("ffn_w_down0", ffn_w_down[0], m_ffn_w_down[0], v_ffn_w_down[0], landed[7]),
            ("ffn_w_down1", ffn_w_down[1], m_ffn_w_down[1], v_ffn_w_down[1], landed[8])]:
        big[tag] = _adamw_shard("adamw_" + tag, w, m, v, part)
    lead = lambda tag: [a[None] for a in big[tag]]
    pair = lambda tag: [jnp.stack([a, b]) for a, b in zip(big[tag + "0"], big[tag + "1"])]

    as_blocks = lambda a, r: a.reshape(r, N_DEV, -1).transpose(1, 0, 2).reshape(N_DEV * r, -1)
    d_cw = jnp.concatenate([g.transpose(1, 0, 2).reshape(CONV_WIDTH, 4 * fb) for g in (dcw0, dcw1)], axis=0)
    parts = [d_fin, jnp.concatenate([d_fn0, d_fn1], axis=0), jnp.concatenate([dcb0.reshape(1, 4 * fb), dcb1.reshape(1, 4 * fb)], axis=0),
             as_blocks(d_cw, 2 * CONV_WIDTH), d_attn, jnp.sum(dsink[:, :, 0], axis=0).reshape(1, nq), d_kvn, d_ogain,
             as_blocks(d_hgn, 1), as_blocks(d_lbl, 2), loss_part]
    offsets, _, _ = _pack_rows(parts)
    gathered, = _all_gather("gather_small_grads", [_pack(parts)], VMEM_SPEC)
    two = lambda a: a.reshape(-1, a.shape[-1])
    small = [(fin, m_final_norm.reshape(1, d), v_final_norm.reshape(1, d), False), (ffn_norm, m_ffn_norm, v_ffn_norm, False),
             (ffn_conv_b, m_ffn_conv_b, v_ffn_conv_b, False), (two(ffn_conv_w), two(m_ffn_conv_w), two(v_ffn_conv_w), True),
             (attn_norm, m_attn_norm, v_attn_norm, False), (attn_sinks, m_attn_sinks, v_attn_sinks, False),
             (kvn, m_kv_norm.reshape(1, d), v_kv_norm.reshape(1, d), False), (hg_out_norm, m_hg_out_norm, v_hg_out_norm, False),
             (hg_norm, m_hg_norm, v_hg_norm, True), (hg_lb_logits, m_hg_lb_logits, v_hg_lb_logits, True)]
    res = _adamw_small(gathered, offsets, small)
    names = ["final_norm", "ffn_norm", "ffn_conv_b", "ffn_conv_w", "attn_norm", "attn_sinks", "kv_norm", "hg_out_norm", "hg_norm", "hg_lb_logits"]
    shapes = {"final_norm": final_norm.shape, "kv_norm": kv_norm.shape, "ffn_conv_w": ffn_conv_w.shape}
    out = {n: [a.reshape(shapes[n]) if n in shapes else a for a in res[4 * i:4 * i + 4]] for i, n in enumerate(names)}
    out.update(hg_w_in=lead("hg_w_in"), hg_w_out=lead("hg_w_out"), w_kv=big["w_kv"], attn_w_q=lead("attn_w_q"), attn_w_o=lead("attn_w_o"),
               ffn_w_up=pair("ffn_w_up"), ffn_w_down=pair("ffn_w_down"))
    order = ["hg_norm", "hg_w_in", "hg_lb_logits", "hg_out_norm", "hg_w_out", "kv_norm", "w_kv", "attn_norm", "attn_w_q", "attn_sinks",
             "attn_w_o", "ffn_norm", "ffn_w_up", "ffn_conv_w", "ffn_conv_b", "ffn_w_down", "final_norm"]
    loss = res[-1][0, 0]
    return (loss, dx0[None], *[out[n][0] for n in order], *[out[n][1] for n in order], *[out[n][2] for n in order], *[out[n][3] for n in order])
```

```python
import functools
import math

import jax
import jax.numpy as jnp
from jax import lax
from jax.experimental import pallas as pl
from jax.experimental.pallas import tpu as pltpu

F32 = jnp.float32
BF16 = jnp.bfloat16

EPS = 1e-6
HG_EXPAND = 128
HG_CHUNK = 32
ATT_HEAD_DIM = 64
ATT_KV_HEADS = 2
WINDOW = 128
CONV_WIDTH = 3
ADAM_LR = 0.001
ADAM_B1 = 0.9
ADAM_B2 = 0.999
ADAM_EPS = 1e-08
ADAM_WD = 0.01
ADAM_STEP = 10

N_DEV = 8
VMEM_LIMIT = 48 * 1024 * 1024
NEG = -1e30

NN = (((1,), (0,)), ((), ()))
NT = (((1,), (1,)), ((), ()))
TN = (((0,), (0,)), ((), ()))
MESH = pl.DeviceIdType.MESH


def _dot(a, b, dims=NN):
    return lax.dot_general(a.astype(BF16), b.astype(BF16), dims, preferred_element_type=F32)


def _sigmoid(x):
    return 1.0 / (1.0 + jnp.exp(-x))


def _silu(x):
    return x * _sigmoid(x)


def _dsilu(x):
    s = _sigmoid(x)
    return s * (1.0 + x * (1.0 - s))


def _params(semantics):
    return pltpu.CompilerParams(dimension_semantics=semantics, vmem_limit_bytes=VMEM_LIMIT)


def _row_tile(rows, want=512):
    return min(rows, want)


MM_ROWS = 1024


def _matmul(name, a, b, *, dims, grid, a_spec, b_spec, o_spec, out_shape, acc_shape, add=None, add_spec=None):
    nk = grid[2]

    def body(*refs):
        if add is None:
            a_ref, b_ref, o_ref, acc = refs
        else:
            a_ref, b_ref, add_ref, o_ref, acc = refs
        k = pl.program_id(2)
        part = _dot(a_ref[...], b_ref[...], dims)

        def finish(total):
            if add is not None:
                total = total + add_ref[...]
            o_ref[...] = total.astype(o_ref.dtype)

        if nk == 1:
            finish(part)
        else:
            @pl.when(k == 0)
            def _():
                acc[...] = part

            @pl.when(k > 0)
            def _():
                acc[...] += part

            @pl.when(k == nk - 1)
            def _():
                finish(acc[...])

    in_specs = [a_spec, b_spec] + ([] if add is None else [add_spec])
    args = (a, b) + (() if add is None else (add,))
    return pl.pallas_call(
        body, name=name, grid=grid, in_specs=in_specs, out_specs=o_spec, out_shape=out_shape,
        scratch_shapes=[pltpu.VMEM(acc_shape, F32)],
        compiler_params=_params(("parallel", "parallel", "arbitrary")),
    )(*args)


def _mm_rows(name, a, w, *, out_dtype, add=None):
    s, kdim = a.shape
    n = w.shape[1]
    tm = _row_tile(s, MM_ROWS)
    return _matmul(
        name, a, w, dims=NN, grid=(s // tm, 1, 1),
        a_spec=pl.BlockSpec((tm, kdim), lambda i, j, k: (i, 0)),
        b_spec=pl.BlockSpec((kdim, n), lambda i, j, k: (0, 0)),
        o_spec=pl.BlockSpec((tm, n), lambda i, j, k: (i, 0)),
        out_shape=jax.ShapeDtypeStruct((s, n), out_dtype), acc_shape=(8, 128),
        add=add, add_spec=None if add is None else pl.BlockSpec((tm, n), lambda i, j, k: (i, 0)),
    )


def _mm_rows_nt(name, a, w, *, out_dtype):
    s, n = a.shape
    kdim = w.shape[0]
    tm = _row_tile(s, MM_ROWS)
    return _matmul(
        name, a, w, dims=NT, grid=(s // tm, 1, 1),
        a_spec=pl.BlockSpec((tm, n), lambda i, j, k: (i, 0)),
        b_spec=pl.BlockSpec((kdim, n), lambda i, j, k: (0, 0)),
        o_spec=pl.BlockSpec((tm, kdim), lambda i, j, k: (i, 0)),
        out_shape=jax.ShapeDtypeStruct((s, kdim), out_dtype), acc_shape=(8, 128),
    )


def _mm_tn(name, a, g):
    s, m = a.shape
    n = g.shape[1]
    ts = _row_tile(s, MM_ROWS)
    return _matmul(
        name, a, g, dims=TN, grid=(1, 1, s // ts),
        a_spec=pl.BlockSpec((ts, m), lambda i, j, k: (k, 0)),
        b_spec=pl.BlockSpec((ts, n), lambda i, j, k: (k, 0)),
        o_spec=pl.BlockSpec((m, n), lambda i, j, k: (0, 0)),
        out_shape=jax.ShapeDtypeStruct((m, n), BF16), acc_shape=(m, n),
    )


def _rmsnorm_cast(name, h, gains):
    s, d = h.shape
    tm = _row_tile(s)
    n = len(gains)

    def body(*refs):
        h_ref, g_refs, o_refs = refs[0], refs[1:1 + n], refs[1 + n:]
        xv = h_ref[...]
        xhat = xv * lax.rsqrt(jnp.mean(xv * xv, axis=-1, keepdims=True) + EPS)
        for g_ref, o_ref in zip(g_refs, o_refs):
            o_ref[...] = (xhat * g_ref[...]).astype(BF16)

    row = pl.BlockSpec((tm, d), lambda i: (i, 0))
    vec = pl.BlockSpec((1, d), lambda i: (0, 0))
    return pl.pallas_call(
        body, name=name, grid=(s // tm,), in_specs=[row] + [vec] * n, out_specs=[row] * n,
        out_shape=[jax.ShapeDtypeStruct((s, d), BF16)] * n, compiler_params=_params(("parallel",)),
    )(h, *gains)


def _rmsnorm_bwd(name, h, dres, branches):
    s, d = h.shape
    tm = _row_tile(s)
    n = len(branches)

    def body(*refs):
        h_ref, dres_ref = refs[0], refs[1]
        da_refs, g_refs = refs[2:2 + n], refs[2 + n:2 + 2 * n]
        dh_ref, dg_refs = refs[2 + 2 * n], refs[3 + 2 * n:]
        i = pl.program_id(0)
        xv = h_ref[...]
        r = lax.rsqrt(jnp.mean(xv * xv, axis=-1, keepdims=True) + EPS)
        xhat = xv * r
        total = dres_ref[...]
        for da_ref, g_ref, dg_ref in zip(da_refs, g_refs, dg_refs):
            da = da_ref[...]
            dgain = jnp.sum(da * xhat, axis=0, keepdims=True)

            @pl.when(i == 0)
            def _():
                dg_ref[...] = dgain

            @pl.when(i > 0)
            def _():
                dg_ref[...] += dgain

            dxhat = da * g_ref[...]
            total = total + r * (dxhat - xhat * jnp.mean(dxhat * xhat, axis=-1, keepdims=True))
        dh_ref[...] = total

    row = pl.BlockSpec((tm, d), lambda i: (i, 0))
    vec = pl.BlockSpec((1, d), lambda i: (0, 0))
    outs = pl.pallas_call(
        body, name=name, grid=(s // tm,), in_specs=[row, row] + [row] * n + [vec] * n, out_specs=[row] + [vec] * n,
        out_shape=[jax.ShapeDtypeStruct((s, d), F32)] + [jax.ShapeDtypeStruct((1, d), F32)] * n,
        compiler_params=_params(("arbitrary",)),
    )(h, dres, *[b[0] for b in branches], *[b[1] for b in branches])
    return outs[0], outs[1:]


def _loss_head(h, gain, target):
    s, d = h.shape
    tm = _row_tile(s)

    def body(h_ref, g_ref, t_ref, dh_ref, dg_ref, loss_ref):
        i = pl.program_id(0)
        xv = h_ref[...]
        r = lax.rsqrt(jnp.mean(xv * xv, axis=-1, keepdims=True) + EPS)
        xhat = xv * r
        err = xhat * g_ref[...] - t_ref[...]
        dy = err * (1.0 / d)
        part = jnp.zeros((1, 128), F32) + 0.5 * jnp.sum(jnp.mean(err * err, axis=-1, keepdims=True))
        dgain = jnp.sum(dy * xhat, axis=0, keepdims=True)

        @pl.when(i == 0)
        def _():
            dg_ref[...] = dgain
            loss_ref[...] = part

        @pl.when(i > 0)
        def _():
            dg_ref[...] += dgain
            loss_ref[...] += part

        dxhat = dy * g_ref[...]
        dh_ref[...] = r * (dxhat - xhat * jnp.mean(dxhat * xhat, axis=-1, keepdims=True))

    row = pl.BlockSpec((tm, d), lambda i: (i, 0))
    vec = pl.BlockSpec((1, d), lambda i: (0, 0))
    return pl.pallas_call(
        body, name="loss_head", grid=(s // tm,), in_specs=[row, vec, row],
        out_specs=[row, vec, pl.BlockSpec((1, 128), lambda i: (0, 0))],
        out_shape=[jax.ShapeDtypeStruct((s, d), F32), jax.ShapeDtypeStruct((1, d), F32), jax.ShapeDtypeStruct((1, 128), F32)],
        compiler_params=_params(("arbitrary",)),
    )(h, gain, target)


def _bdot(a, b, ca, cb):
    return lax.dot_general(a.astype(BF16), b.astype(BF16), (((ca,), (cb,)), ((0,), (0,))), preferred_element_type=F32)


def _chunk_cumsum(xv, reverse=False):
    n = xv.shape[0]
    row = lax.broadcasted_iota(jnp.int32, xv.shape, 0) % HG_CHUNK
    step = 1
    while step < HG_CHUNK:
        if reverse:
            xv = xv + jnp.where(row < HG_CHUNK - step, pltpu.roll(xv, n - step, axis=0), 0.0)
        else:
            xv = xv + jnp.where(row >= step, pltpu.roll(xv, step, axis=0), 0.0)
        step *= 2
    return xv


def _hg_terms(p_ref, lbl_ref):
    pq = p_ref[0].astype(F32)
    pf = p_ref[1].astype(F32)
    lb = _sigmoid(lbl_ref[0:1, :] - lbl_ref[1:2, :])
    sig = _sigmoid(pf)
    fg = lb + (1.0 - lb) * sig
    nc = pq.shape[0] // HG_CHUNK
    chunks = lambda a: a.reshape(nc, HG_CHUNK, HG_EXPAND)
    q = chunks(_silu(pq) * HG_EXPAND ** -0.5)
    k = chunks(1.0 - fg)
    v = chunks(p_ref[2].astype(F32))
    g = chunks(_chunk_cumsum(jnp.log(fg)))
    gm = g[:, HG_CHUNK // 2 - 1:HG_CHUNK // 2, :]
    gl = g[:, HG_CHUNK - 1:HG_CHUNK, :]
    e_mid, e_inv, e_all, e_end = jnp.exp(g - gm), jnp.exp(gm - g), jnp.exp(g), jnp.exp(gl - g)
    terms = dict(q=q, k=k, v=v, qd=q * e_all, qt=q * e_mid, kt=k * e_inv, kd=k * e_end, e_last=jnp.exp(gl),
                 e_mid=e_mid, e_inv=e_inv, e_all=e_all, e_end=e_end)
    return terms, (pq, sig, fg, lb)


def _causal(nc):
    r = lax.broadcasted_iota(jnp.int32, (nc, HG_CHUNK, HG_CHUNK), 1)
    c = lax.broadcasted_iota(jnp.int32, (nc, HG_CHUNK, HG_CHUNK), 2)
    return r >= c


def _hgrn2_fwd(p, lb_logits, out_gain):
    _, s, d = p.shape
    heads = d // HG_EXPAND
    t = _row_tile(s)
    nc = t // HG_CHUNK

    def body(p_ref, lbl_ref, gain_ref, o_ref, og_ref, st_ref, state, decay):
        @pl.when(pl.program_id(1) == 0)
        def _():
            state[...] = jnp.zeros_like(state)

        tm, _ = _hg_terms(p_ref, lbl_ref)
        decay[...] = tm["e_last"]
        st_ref[...] = _bdot(tm["v"], tm["kd"], 1, 1)

        def chunk(c, carry):
            add = st_ref[c]
            st = state[...]
            st_ref[c] = st
            state[...] = st * decay[c] + add
            return carry

        lax.fori_loop(0, nc, chunk, 0)
        a = jnp.where(_causal(nc), _bdot(tm["qt"], tm["kt"], 2, 2), 0.0)
        ov = (_bdot(tm["qd"], st_ref[...], 2, 2) + _bdot(a, tm["v"], 2, 1)).reshape(t, HG_EXPAND)
        o_ref[...] = ov
        on = ov * lax.rsqrt(jnp.mean(ov * ov, axis=-1, keepdims=True) + EPS) * gain_ref[...]
        og_ref[...] = (on * _silu(p_ref[3].astype(F32))).astype(BF16)

    blk = pl.BlockSpec((t, HG_EXPAND), lambda h, b: (b, h))
    return pl.pallas_call(
        body, name="hgrn2_fwd", grid=(heads, s // t),
        in_specs=[pl.BlockSpec((4, t, HG_EXPAND), lambda h, b: (0, b, h)), pl.BlockSpec((2, HG_EXPAND), lambda h, b: (0, h)),
                  pl.BlockSpec((1, HG_EXPAND), lambda h, b: (0, 0))],
        out_specs=[blk, blk, pl.BlockSpec((None, nc, HG_EXPAND, HG_EXPAND), lambda h, b: (h, b, 0, 0))],
        out_shape=[jax.ShapeDtypeStruct((s, d), F32), jax.ShapeDtypeStruct((s, d), BF16),
                   jax.ShapeDtypeStruct((heads, s // HG_CHUNK, HG_EXPAND, HG_EXPAND), F32)],
        scratch_shapes=[pltpu.VMEM((HG_EXPAND, HG_EXPAND), F32), pltpu.VMEM((nc, 1, HG_EXPAND), F32)],
        compiler_params=_params(("parallel", "arbitrary")),
    )(p, lb_logits, out_gain)


def _hgrn2_bwd(p, lb_logits, out_gain, o, dog, states):
    _, s, d = p.shape
    heads = d // HG_EXPAND
    t = _row_tile(s)
    nc = t // HG_CHUNK
    nb = s // t

    def body(p_ref, lbl_ref, gain_ref, o_ref, dog_ref, st_ref, dp_ref, dlbl_ref, dgain_ref, dstate, decay, dst_s):
        h, b = pl.program_id(0), pl.program_id(1)

        @pl.when(b == 0)
        def _():
            dstate[...] = jnp.zeros_like(dstate)

        tm, (pq, sig, fg, lb) = _hg_terms(p_ref, lbl_ref)
        pg = p_ref[3].astype(F32)
        ov = o_ref[...]
        r = lax.rsqrt(jnp.mean(ov * ov, axis=-1, keepdims=True) + EPS)
        ohat = ov * r
        dogv = dog_ref[...]
        d_on = dogv * _silu(pg)
        dp_ref[3] = (dogv * ohat * gain_ref[...] * _dsilu(pg)).astype(BF16)
        dgain = jnp.sum(d_on * ohat, axis=0, keepdims=True)

        @pl.when((h == 0) & (b == 0))
        def _():
            dgain_ref[...] = dgain

        @pl.when((h > 0) | (b > 0))
        def _():
            dgain_ref[...] += dgain

        dohat = d_on * gain_ref[...]
        do = (r * (dohat - ohat * jnp.mean(dohat * ohat, axis=-1, keepdims=True))).reshape(nc, HG_CHUNK, HG_EXPAND)

        decay[...] = tm["e_last"]
        dst_s[...] = _bdot(do, tm["qd"], 1, 1)

        def chunk(i, carry):
            c = nc - 1 - i
            add = dst_s[c]
            dst = dstate[...]
            dst_s[c] = dst
            dstate[...] = dst * decay[c] + add
            return carry

        lax.fori_loop(0, nc, chunk, 0)
        st, dst = st_ref[...], dst_s[...]
        causal = _causal(nc)
        a = jnp.where(causal, _bdot(tm["qt"], tm["kt"], 2, 2), 0.0)
        da = jnp.where(causal, _bdot(do, tm["v"], 2, 2), 0.0)
        dqt = _bdot(da, tm["kt"], 2, 1)
        dkt = _bdot(da, tm["qt"], 1, 1)
        dqd = _bdot(do, st, 2, 1)
        dkd = _bdot(tm["v"], dst, 2, 1)
        dv = _bdot(a, do, 1, 1) + _bdot(tm["kd"], dst, 2, 2)
        dq = dqt * tm["e_mid"] + dqd * tm["e_all"]
        dk = dkt * tm["e_inv"] + dkd * tm["e_end"]
        dg = dqt * tm["qt"] - dkt * tm["kt"] + dqd * tm["qd"] - dkd * tm["kd"]
        dgl = jnp.sum(dkd * tm["kd"], axis=1, keepdims=True) + tm["e_last"] * jnp.sum(dst * st, axis=1, keepdims=True)
        last_row = lax.broadcasted_iota(jnp.int32, (nc, HG_CHUNK, HG_EXPAND), 1) == HG_CHUNK - 1
        flat = lambda a3: a3.reshape(t, HG_EXPAND)
        dlf = _chunk_cumsum(flat(dg + jnp.where(last_row, dgl, 0.0)), reverse=True)
        dfg = dlf / fg - flat(dk)
        dlb = jnp.sum(dfg * (1.0 - sig), axis=0, keepdims=True)
        dl0 = dlb * lb * (1.0 - lb)
        dlbl = jnp.concatenate([dl0, -dl0], axis=0)

        @pl.when(b == 0)
        def _():
            dlbl_ref[...] = dlbl

        @pl.when(b > 0)
        def _():
            dlbl_ref[...] += dlbl

        dp_ref[0] = (flat(dq) * HG_EXPAND ** -0.5 * _dsilu(pq)).astype(BF16)
        dp_ref[1] = (dfg * (1.0 - lb) * sig * (1.0 - sig)).astype(BF16)
        dp_ref[2] = flat(dv).astype(BF16)

    blk = pl.BlockSpec((t, HG_EXPAND), lambda h, b: (nb - 1 - b, h))
    pblk = pl.BlockSpec((4, t, HG_EXPAND), lambda h, b: (0, nb - 1 - b, h))
    return pl.pallas_call(
        body, name="hgrn2_bwd", grid=(heads, nb),
        in_specs=[pblk, pl.BlockSpec((2, HG_EXPAND), lambda h, b: (0, h)), pl.BlockSpec((1, HG_EXPAND), lambda h, b: (0, 0)),
                  blk, blk, pl.BlockSpec((None, nc, HG_EXPAND, HG_EXPAND), lambda h, b: (h, nb - 1 - b, 0, 0))],
        out_specs=[pblk, pl.BlockSpec((2, HG_EXPAND), lambda h, b: (0, h)), pl.BlockSpec((1, HG_EXPAND), lambda h, b: (0, 0))],
        out_shape=[jax.ShapeDtypeStruct((4, s, d), BF16), jax.ShapeDtypeStruct((2, d), F32), jax.ShapeDtypeStruct((1, HG_EXPAND), F32)],
        scratch_shapes=[pltpu.VMEM((HG_EXPAND, HG_EXPAND), F32), pltpu.VMEM((nc, 1, HG_EXPAND), F32),
                        pltpu.VMEM((nc, HG_EXPAND, HG_EXPAND), F32)],
        compiler_params=_params(("arbitrary", "arbitrary")),
    )(p, lb_logits, out_gain, o, dog, states)


HALO = 8


def _shift_down(xv, n):
    return pltpu.roll(xv, n, axis=0)


def _shift_up(xv, n):
    return pltpu.roll(xv, xv.shape[0] - n, axis=0)


def _ffn_hidden(u, conv_w, conv_b):
    _, nj, s, fb = u.shape
    tm = _row_tile(s)
    per = tm // HALO

    def body(gate_ref, prev_ref, val_ref, w_ref, b_ref, h_ref):
        i = pl.program_id(1)
        prev = jnp.where(i > 0, prev_ref[...].astype(F32), 0.0)
        ext = jnp.concatenate([prev, gate_ref[...].astype(F32)], axis=0)
        conv = b_ref[...] + w_ref[2:3, :] * ext[HALO:]
        conv = conv + w_ref[1:2, :] * _shift_down(ext, 1)[HALO:]
        conv = conv + w_ref[0:1, :] * _shift_down(ext, 2)[HALO:]
        h_ref[...] = (_silu(conv) * val_ref[...].astype(F32)).astype(BF16)

    return pl.pallas_call(
        body, name="ffn_hidden", grid=(nj, s // tm),
        in_specs=[pl.BlockSpec((None, None, tm, fb), lambda j, i: (0, j, i, 0)),
                  pl.BlockSpec((None, None, HALO, fb), lambda j, i: (0, j, jnp.maximum(i * per - 1, 0), 0)),
                  pl.BlockSpec((None, None, tm, fb), lambda j, i: (1, j, i, 0)),
                  pl.BlockSpec((None, CONV_WIDTH, fb), lambda j, i: (j, 0, 0)),
                  pl.BlockSpec((None, 1, fb), lambda j, i: (j, 0, 0))],
        out_specs=pl.BlockSpec((None, tm, fb), lambda j, i: (j, i, 0)),
        out_shape=jax.ShapeDtypeStruct((nj, s, fb), BF16), compiler_params=_params(("parallel", "parallel")),
    )(u, u, u, conv_w, conv_b)


def _ffn_hidden_bwd(u, dh, conv_w, conv_b):
    _, nj, s, fb = u.shape
    tm = _row_tile(s)
    per = tm // HALO
    nblk = s // HALO
    ni = s // tm

    def body(gate_ref, gprev_ref, gnext_ref, val_ref, vnext_ref, dh_ref, dhnext_ref, w_ref, b_ref, du_ref, dw_ref, db_ref):
        i = pl.program_id(1)
        has_next = i < ni - 1
        gprev = jnp.where(i > 0, gprev_ref[...].astype(F32), 0.0)
        gext = jnp.concatenate([gprev, gate_ref[...].astype(F32), gnext_ref[...].astype(F32)], axis=0)
        vext = jnp.concatenate([val_ref[...].astype(F32), vnext_ref[...].astype(F32)], axis=0)
        dhext = jnp.concatenate([dh_ref[...].astype(F32), jnp.where(has_next, dhnext_ref[...].astype(F32), 0.0)], axis=0)
        g0 = gext[HALO:]
        g1 = _shift_down(gext, 1)[HALO:]
        g2 = _shift_down(gext, 2)[HALO:]
        conv = b_ref[...] + w_ref[2:3, :] * g0 + w_ref[1:2, :] * g1 + w_ref[0:1, :] * g2
        dconv = dhext * vext * _dsilu(conv)
        dgate = w_ref[2:3, :] * dconv + w_ref[1:2, :] * _shift_up(dconv, 1) + w_ref[0:1, :] * _shift_up(dconv, 2)
        du_ref[0] = dgate[:tm].astype(BF16)
        du_ref[1] = (dhext * _silu(conv))[:tm].astype(BF16)
        own = dconv[:tm]
        dw = jnp.concatenate([jnp.sum(own * g2[:tm], axis=0, keepdims=True), jnp.sum(own * g1[:tm], axis=0, keepdims=True),
                              jnp.sum(own * g0[:tm], axis=0, keepdims=True)], axis=0)
        db = jnp.sum(own, axis=0, keepdims=True)

        @pl.when(i == 0)
        def _():
            dw_ref[...] = dw
            db_ref[...] = db

        @pl.when(i > 0)
        def _():
            dw_ref[...] += dw
            db_ref[...] += db

    def tile(part):
        return pl.BlockSpec((None, None, tm, fb), lambda j, i: (part, j, i, 0))

    def after(part):
        return pl.BlockSpec((None, None, HALO, fb), lambda j, i: (part, j, jnp.minimum((i + 1) * per, nblk - 1), 0))

    return pl.pallas_call(
        body, name="ffn_hidden_bwd", grid=(nj, ni),
        in_specs=[tile(0), pl.BlockSpec((None, None, HALO, fb), lambda j, i: (0, j, jnp.maximum(i * per - 1, 0), 0)), after(0),
                  tile(1), after(1),
                  pl.BlockSpec((None, tm, fb), lambda j, i: (j, i, 0)),
                  pl.BlockSpec((None, HALO, fb), lambda j, i: (j, jnp.minimum((i + 1) * per, nblk - 1), 0)),
                  pl.BlockSpec((None, CONV_WIDTH, fb), lambda j, i: (j, 0, 0)), pl.BlockSpec((None, 1, fb), lambda j, i: (j, 0, 0))],
        out_specs=[pl.BlockSpec((2, None, tm, fb), lambda j, i: (0, j, i, 0)),
                   pl.BlockSpec((None, CONV_WIDTH, fb), lambda j, i: (j, 0, 0)), pl.BlockSpec((None, 1, fb), lambda j, i: (j, 0, 0))],
        out_shape=[jax.ShapeDtypeStruct((2, nj, s, fb), BF16), jax.ShapeDtypeStruct((nj, CONV_WIDTH, fb), F32),
                   jax.ShapeDtypeStruct((nj, 1, fb), F32)],
        compiler_params=_params(("parallel", "arbitrary")),
    )(u, u, u, u, u, dh, dh, conv_w, conv_b)


ATT_TILE = 512


def _attn_probs(q, kb, sink, head, first, n_heads):
    iq = lax.broadcasted_iota(jnp.int32, (WINDOW, 2 * WINDOW), 0)
    ik = lax.broadcasted_iota(jnp.int32, (WINDOW, 2 * WINDOW), 1)
    dist = iq + WINDOW - ik
    valid = (dist >= 0) & (dist < WINDOW) & (ik >= jnp.where(first, WINDOW, 0))
    slope = 2.0 ** (-8.0 * (head + 1) / n_heads)
    sc = jnp.where(valid, _dot(q, kb, NT) * ATT_HEAD_DIM ** -0.5 - slope * dist.astype(F32), NEG)
    m = jnp.maximum(jnp.max(sc, axis=-1, keepdims=True), sink)
    e = jnp.exp(sc - m)
    es = jnp.exp(sink - m)
    inv = 1.0 / (jnp.sum(e, axis=-1, keepdims=True) + es)
    return e * inv, es * inv


def _attn_specs(s, d, kvd, tq):
    per = tq // WINDOW
    return [pl.BlockSpec((tq, d), lambda i: (i, 0)), pl.BlockSpec((tq, kvd), lambda i: (i, 0)),
            pl.BlockSpec((WINDOW, kvd), lambda i: (jnp.maximum(i * per - 1, 0), 0))]


def _attn_fwd(q, kv, sinks):
    s, d = q.shape
    kvd = kv.shape[1]
    half = kvd // 2
    hd = ATT_HEAD_DIM
    nq = d // hd
    group = nq // ATT_KV_HEADS
    tq = min(s, ATT_TILE)
    per = tq // WINDOW

    def body(q_ref, kvc_ref, kvp_ref, sink_ref, o_ref, band):
        i = pl.program_id(0)
        band[0:WINDOW, :] = kvp_ref[...]
        band[WINDOW:, :] = kvc_ref[...]

        def block(b, carry):
            rows = pl.ds(pl.multiple_of(b * WINDOW, WINDOW), WINDOW)
            keys = pl.ds(pl.multiple_of(b * WINDOW, WINDOW), 2 * WINDOW)
            first = (i * per + b) == 0
            for pair in range(nq // 2):
                outs = []
                for h in (2 * pair, 2 * pair + 1):
                    g = h // group
                    p, _ = _attn_probs(q_ref[rows, h * hd:(h + 1) * hd], band[keys, g * hd:(g + 1) * hd], sink_ref[0, h], h, first, nq)
                    outs.append(_dot(p, band[keys, half + g * hd:half + (g + 1) * hd]))
                o_ref[rows, pair * 2 * hd:(pair + 1) * 2 * hd] = jnp.concatenate(outs, axis=1).astype(BF16)
            return carry

        lax.fori_loop(0, per, block, 0)

    return pl.pallas_call(
        body, name="attn_fwd", grid=(s // tq,),
        in_specs=_attn_specs(s, d, kvd, tq) + [pl.BlockSpec(memory_space=pltpu.SMEM)],
        out_specs=pl.BlockSpec((tq, d), lambda i: (i, 0)), out_shape=jax.ShapeDtypeStruct((s, d), BF16),
        scratch_shapes=[pltpu.VMEM((tq + WINDOW, kvd), BF16)], compiler_params=_params(("parallel",)),
    )(q, kv, kv, sinks)


def _attn_bwd(q, kv, o, do, sinks):
    s, d = q.shape
    kvd = kv.shape[1]
    half = kvd // 2
    hd = ATT_HEAD_DIM
    nq = d // hd
    group = nq // ATT_KV_HEADS
    tq = min(s, ATT_TILE)
    per = tq // WINDOW
    nt = s // tq

    def body(q_ref, kvc_ref, kvp_ref, o_ref, do_ref, sink_ref, dq_ref, dkvc_ref, dkvp_ref, ds_ref, band, dband):
        i = pl.program_id(0)
        band[0:WINDOW, :] = kvp_ref[...]
        band[WINDOW:, :] = kvc_ref[...]
        dband[...] = jnp.zeros_like(dband)
        ds_ref[...] = jnp.zeros_like(ds_ref)

        def block(b, carry):
            rows = pl.ds(pl.multiple_of(b * WINDOW, WINDOW), WINDOW)
            keys = pl.ds(pl.multiple_of(b * WINDOW, WINDOW), 2 * WINDOW)
            first = (i * per + b) == 0
            dks, dvs = [], []
            for g in range(ATT_KV_HEADS):
                kb = band[keys, g * hd:(g + 1) * hd]
                vb = band[keys, half + g * hd:half + (g + 1) * hd]
                dk = jnp.zeros((2 * WINDOW, hd), F32)
                dv = jnp.zeros((2 * WINDOW, hd), F32)
                dqs = []
                for j in range(group):
                    h = g * group + j
                    cols = slice(h * hd, (h + 1) * hd)
                    qv, dov = q_ref[rows, cols], do_ref[rows, cols]
                    p, ps = _attn_probs(qv, kb, sink_ref[0, h], h, first, nq)
                    dsum = jnp.sum(dov.astype(F32) * o_ref[rows, cols].astype(F32), axis=-1, keepdims=True)
                    dsc = p * (_dot(dov, vb, NT) - dsum) * ATT_HEAD_DIM ** -0.5
                    dqs.append(_dot(dsc, kb))
                    dk = dk + _dot(dsc, qv, TN)
                    dv = dv + _dot(p, dov, TN)
                    ds_ref[h:h + 1, :] += jnp.zeros((1, 128), F32) - jnp.sum(ps * dsum)
                    if j % 2 == 1:
                        dq_ref[rows, (h - 1) * hd:(h + 1) * hd] = jnp.concatenate(dqs[-2:], axis=1).astype(BF16)
                dks.append(dk)
                dvs.append(dv)
            dband[keys, 0:half] += jnp.concatenate(dks, axis=1)
            dband[keys, half:] += jnp.concatenate(dvs, axis=1)
            return carry

        lax.fori_loop(0, per, block, 0)
        dkvp_ref[...] = dband[0:WINDOW, :]
        dkvc_ref[...] = dband[WINDOW:, :]

    big = pl.BlockSpec((tq, d), lambda i: (i, 0))
    return pl.pallas_call(
        body, name="attn_bwd", grid=(nt,),
        in_specs=_attn_specs(s, d, kvd, tq) + [big, big, pl.BlockSpec(memory_space=pltpu.SMEM)],
        out_specs=[big, pl.BlockSpec((tq, kvd), lambda i: (i, 0)), pl.BlockSpec((None, WINDOW, kvd), lambda i: (i, 0, 0)),
                   pl.BlockSpec((None, nq, 128), lambda i: (i, 0, 0))],
        out_shape=[jax.ShapeDtypeStruct((s, d), BF16), jax.ShapeDtypeStruct((s, kvd), F32), jax.ShapeDtypeStruct((nt, WINDOW, kvd), F32),
                   jax.ShapeDtypeStruct((nt, nq, 128), F32)],
        scratch_shapes=[pltpu.VMEM((tq + WINDOW, kvd), BF16), pltpu.VMEM((tq + WINDOW, kvd), F32)],
        compiler_params=_params(("parallel",)),
    )(q, kv, kv, o, do, sinks)


HBM_SPEC = pl.BlockSpec(memory_space=pltpu.HBM)
VMEM_SPEC = pl.BlockSpec(memory_space=pltpu.VMEM)


def _place():
    return lax.axis_index("x"), lax.axis_index("y"), lax.axis_index("c")


def _flip(pos, r):
    return tuple(1 - p if (r >> (2 - a)) & 1 else p for a, p in enumerate(pos))


def _index(pos):
    return 4 * pos[0] + 2 * pos[1] + pos[2]


def _all_gather(name, shards, spec):
    n = len(shards)

    def body(*refs):
        x_refs, o_refs = refs[:n], refs[n:2 * n]
        send_sems, recv_sems, local_sems = refs[2 * n:]
        me = _place()
        sibling = _flip(me, 1)
        far = [_flip(me, r) for r in (4, 2, 6)]

        def copy(t, sem, block, to, src=None):
            rows = o_refs[t].at[_index(block)]
            return pltpu.make_async_remote_copy(
                src_ref=rows if src is None else src, dst_ref=rows, send_sem=send_sems.at[t, sem], recv_sem=recv_sems.at[t, sem],
                device_id=to, device_id_type=MESH)

        own = [pltpu.make_async_copy(x_refs[t], o_refs[t].at[_index(me)], local_sems.at[t]) for t in range(n)]
        for cp in own:
            cp.start()
        first = []
        for t in range(n):
            first.append(copy(t, 0, me, sibling, src=x_refs[t]))
            first += [copy(t, 1 + j, me, peer, src=x_refs[t]) for j, peer in enumerate(far)]
        for cp in first:
            cp.start()
        passed = []
        for j, peer in enumerate(far):
            for t in range(n):
                copy(t, 1 + j, peer, me).wait_recv()
                cp = copy(t, 4 + j, peer, sibling)
                cp.start()
                passed.append(cp)
        for t in range(n):
            copy(t, 0, sibling, me).wait_recv()
            for j, peer in enumerate(far):
                copy(t, 4 + j, _flip(peer, 1), me).wait_recv()
        for cp in first + passed:
            cp.wait_send()
        for cp in own:
            cp.wait()

    return pl.pallas_call(
        body, name=name, in_specs=[spec] * n, out_specs=[spec] * n,
        out_shape=[jax.ShapeDtypeStruct((N_DEV,) + sh.shape, sh.dtype) for sh in shards],
        scratch_shapes=[pltpu.SemaphoreType.DMA((n, 7)), pltpu.SemaphoreType.DMA((n, 7)), pltpu.SemaphoreType.DMA((n,))],
    )(*shards)


SEM_SPEC = pl.BlockSpec(memory_space=pltpu.SEMAPHORE)
ANY_SPEC = pl.BlockSpec(memory_space=pl.ANY)


def _landing(own, mine):
    return lax.dynamic_update_slice(lax.empty((N_DEV,) + own.shape, own.dtype), own[None], (mine,) + (0,) * own.ndim)


def _peer_copies(src_refs, land_refs, send_sems, recv_sems, scatter, arrivals):
    me = _place()
    mine = _index(me)
    copies = []
    for t, (src, land) in enumerate(zip(src_refs, land_refs)):
        for r in range(1, N_DEV):
            peer = _flip(me, r)
            theirs = _index(peer)
            sem = t * (N_DEV - 1) + r - 1
            copies.append(pltpu.make_async_remote_copy(
                src_ref=src.at[theirs] if scatter else src, dst_ref=land.at[theirs if arrivals else mine],
                send_sem=send_sems.at[sem], recv_sem=recv_sems.at[sem], device_id=peer, device_id_type=MESH))
    return copies


def _send_start(name, sources, lands, scatter):
    n = len(sources)

    def body(*refs):
        for out in _peer_copies(refs[:n], refs[n:2 * n], refs[2 * n], refs[2 * n + 1], scatter, False):
            out.start()
        refs[-1][...] = jnp.zeros_like(refs[-1])

    outs = pl.pallas_call(
        body, name=name, in_specs=[HBM_SPEC] * (2 * n), out_specs=[SEM_SPEC, SEM_SPEC] + [HBM_SPEC] * (2 * n) + [VMEM_SPEC],
        out_shape=[pltpu.SemaphoreType.DMA((n * (N_DEV - 1),)), pltpu.SemaphoreType.DMA((n * (N_DEV - 1),))]
        + [pltpu.HBM(a.shape, a.dtype) for a in list(sources) + list(lands)] + [jax.ShapeDtypeStruct((8, 128), F32)],
        input_output_aliases={i: 2 + i for i in range(2 * n)},
        compiler_params=pltpu.CompilerParams(has_side_effects=pltpu.SideEffectType.DATAFLOW_SIDE_EFFECTING),
    )(*[pltpu.with_memory_space_constraint(a, pltpu.HBM) for a in list(sources) + list(lands)])
    return outs[0], outs[1], outs[2:2 + n], outs[2 + n:2 + 2 * n], outs[-1]


def _send_wait(name, started, after, scatter):
    send_sems, recv_sems, sources, lands, _ = started
    n = len(sources)

    def body(*refs):
        for out in _peer_copies(refs[:n], refs[n:2 * n], refs[2 * n], refs[2 * n + 1], scatter, False):
            out.wait_send()
        for arrival in _peer_copies(refs[:n], refs[n:2 * n], refs[2 * n], refs[2 * n + 1], scatter, True):
            arrival.wait_recv()

    outs = pl.pallas_call(
        body, name=name, in_specs=[HBM_SPEC] * (2 * n) + [SEM_SPEC, SEM_SPEC, ANY_SPEC], out_specs=[HBM_SPEC] * (2 * n),
        out_shape=[pltpu.HBM(a.shape, a.dtype) for a in list(sources) + list(lands)],
        input_output_aliases={i: i for i in range(2 * n)},
        compiler_params=pltpu.CompilerParams(has_side_effects=pltpu.SideEffectType.DATAFLOW_SIDE_EFFECTING),
    )(*sources, *lands, send_sems, recv_sems, after)
    return outs[n:]


def _pack_rows(parts):
    offsets, row = [], 0
    for part in parts:
        offsets.append(row)
        row += part.shape[0]
    return offsets, -(-row // 8) * 8, -(-max(part.shape[1] for part in parts) // 128) * 128


def _pack(parts):
    offsets, rows, width = _pack_rows(parts)

    def body(*refs):
        o_ref = refs[-1]
        o_ref[...] = jnp.zeros_like(o_ref)
        for off, ref in zip(offsets, refs[:-1]):
            o_ref[off:off + ref.shape[0], 0:ref.shape[1]] = ref[...]

    return pl.pallas_call(body, name="pack_small_grads", in_specs=[VMEM_SPEC] * len(parts), out_specs=VMEM_SPEC,
                          out_shape=jax.ShapeDtypeStruct((rows, width), F32))(*parts)


def _adamw_math(w, g, m, v):
    m = ADAM_B1 * m + (1.0 - ADAM_B1) * g
    v = ADAM_B2 * v + (1.0 - ADAM_B2) * (g * g)
    m_hat = m / (1.0 - ADAM_B1 ** ADAM_STEP)
    v_hat = v / (1.0 - ADAM_B2 ** ADAM_STEP)
    return -ADAM_LR * (m_hat / (jnp.sqrt(v_hat) + ADAM_EPS) + ADAM_WD * w), m, v


def _adamw_shard(name, w, m, v, partials):
    rows, cols = w.shape
    tr = max(t for t in range(8, min(rows, 256) + 1, 8) if rows % t == 0)

    def body(w_ref, m_ref, v_ref, p_ref, g_ref, d_ref, nm_ref, nv_ref):
        g = p_ref[0].astype(F32)
        for dev in range(1, N_DEV):
            g = g + p_ref[dev].astype(F32)
        g_ref[...] = g
        d_ref[...], nm_ref[...], nv_ref[...] = _adamw_math(w_ref[...], g, m_ref[...], v_ref[...])

    blk = pl.BlockSpec((tr, cols), lambda i: (i, 0))
    return pl.pallas_call(
        body, name=name, grid=(rows // tr,), in_specs=[blk, blk, blk, pl.BlockSpec((N_DEV, tr, cols), lambda i: (0, i, 0))],
        out_specs=[blk] * 4, out_shape=[jax.ShapeDtypeStruct((rows, cols), F32)] * 4, compiler_params=_params(("parallel",)),
    )(w, m, v, partials)


def _adamw_small(gathered, offsets, entries):
    n = len(entries)

    def body(*refs):
        pack_ref = refs[0]
        w_refs, m_refs, v_refs = refs[1:1 + n], refs[1 + n:1 + 2 * n], refs[1 + 2 * n:1 + 3 * n]
        outs = refs[1 + 3 * n:]
        total = pack_ref[0]
        for dev in range(1, N_DEV):
            total = total + pack_ref[dev]
        mine = _index(_place())
        for e in range(n):
            rows, cols = w_refs[e].shape
            off = offsets[e]
            if entries[e][3]:
                g = jnp.zeros((rows, cols), F32)
                for dev in range(N_DEV):
                    g = g + jnp.where(mine == dev, total[off + dev * rows:off + (dev + 1) * rows, 0:cols], 0.0)
            else:
                g = total[off:off + rows, 0:cols]
            outs[4 * e][...] = g
            outs[4 * e + 1][...], outs[4 * e + 2][...], outs[4 * e + 3][...] = _adamw_math(w_refs[e][...], g, m_refs[e][...], v_refs[e][...])
        outs[4 * n][...] = total[offsets[n]:offsets[n] + 1, 0:128]

    shapes = []
    for w, _, _, _ in entries:
        shapes += [jax.ShapeDtypeStruct(w.shape, F32)] * 4
    shapes.append(jax.ShapeDtypeStruct((1, 128), F32))
    return pl.pallas_call(
        body, name="adamw_small", in_specs=[VMEM_SPEC] * (1 + 3 * n), out_specs=[VMEM_SPEC] * len(shapes), out_shape=shapes,
        compiler_params=pltpu.CompilerParams(vmem_limit_bytes=VMEM_LIMIT),
    )(gathered, *[e[0] for e in entries], *[e[1] for e in entries], *[e[2] for e in entries])


def _ffn_forward(tag, h, gain, w_up, w_down, conv_w, conv_b):
    s, d = h.shape
    fb = w_up.shape[2]
    tm = _row_tile(s, MM_ROWS)
    a, = _rmsnorm_cast(f"ffn_norm_{tag}", h, [gain])
    u = _matmul(
        f"ffn_up_{tag}", a, w_up, dims=NN, grid=(s // tm, N_DEV, 1),
        a_spec=pl.BlockSpec((tm, d), lambda i, j, k: (i, 0)),
        b_spec=pl.BlockSpec((None, d, fb), lambda i, j, k: (j, 0, 0)),
        o_spec=pl.BlockSpec((None, None, tm, fb), lambda i, j, k: (j // 4, j % 4, i, 0)),
        out_shape=jax.ShapeDtypeStruct((2, 4, s, fb), BF16), acc_shape=(8, 128))
    hidden = _ffn_hidden(u, conv_w, conv_b)
    out = _matmul(
        f"ffn_down_{tag}", hidden, w_down, dims=NN, grid=(s // tm, 1, 4),
        a_spec=pl.BlockSpec((None, tm, fb), lambda i, j, k: (k, i, 0)),
        b_spec=pl.BlockSpec((None, fb, d), lambda i, j, k: (k, 0, 0)),
        o_spec=pl.BlockSpec((tm, d), lambda i, j, k: (i, 0)),
        out_shape=jax.ShapeDtypeStruct((s, d), F32), acc_shape=(tm, d),
        add=h, add_spec=pl.BlockSpec((tm, d), lambda i, j, k: (i, 0)))
    return out, (a, u, hidden)


def _ffn_backward(tag, h, gain, w_up, w_down, conv_w, conv_b, saved, dout):
    a, u, hidden = saved
    s, d = h.shape
    fb = w_up.shape[2]
    tm = _row_tile(s, MM_ROWS)
    dhidden = _matmul(
        f"ffn_down_bwd_{tag}", dout, w_down, dims=NT, grid=(s // tm, 4, 1),
        a_spec=pl.BlockSpec((tm, d), lambda i, j, k: (i, 0)),
        b_spec=pl.BlockSpec((None, fb, d), lambda i, j, k: (j, 0, 0)),
        o_spec=pl.BlockSpec((None, tm, fb), lambda i, j, k: (j, i, 0)),
        out_shape=jax.ShapeDtypeStruct((4, s, fb), BF16), acc_shape=(8, 128))
    dw_down = _matmul(
        f"ffn_down_grad_{tag}", hidden, dout, dims=TN, grid=(4, 1, s // tm),
        a_spec=pl.BlockSpec((None, tm, fb), lambda i, j, k: (i, k, 0)),
        b_spec=pl.BlockSpec((tm, d), lambda i, j, k: (k, 0)),
        o_spec=pl.BlockSpec((None, fb, d), lambda i, j, k: (i, 0, 0)),
        out_shape=jax.ShapeDtypeStruct((4, fb, d), BF16), acc_shape=(fb, d))
    du, dconv_w, dconv_b = _ffn_hidden_bwd(u, dhidden, conv_w, conv_b)
    da = _matmul(
        f"ffn_up_bwd_{tag}", du, w_up, dims=NT, grid=(s // tm, 1, N_DEV),
        a_spec=pl.BlockSpec((None, None, tm, fb), lambda i, j, k: (k // 4, k % 4, i, 0)),
        b_spec=pl.BlockSpec((None, d, fb), lambda i, j, k: (k, 0, 0)),
        o_spec=pl.BlockSpec((tm, d), lambda i, j, k: (i, 0)),
        out_shape=jax.ShapeDtypeStruct((s, d), F32), acc_shape=(tm, d))
    dw_up = _matmul(
        f"ffn_up_grad_{tag}", a, du, dims=TN, grid=(1, N_DEV, s // tm),
        a_spec=pl.BlockSpec((tm, d), lambda i, j, k: (k, 0)),
        b_spec=pl.BlockSpec((None, None, tm, fb), lambda i, j, k: (j // 4, j % 4, k, 0)),
        o_spec=pl.BlockSpec((None, d, fb), lambda i, j, k: (j, 0, 0)),
        out_shape=jax.ShapeDtypeStruct((N_DEV, d, fb), BF16), acc_shape=(d, fb))
    dh, (dgain,) = _rmsnorm_bwd(f"ffn_norm_bwd_{tag}", h, dout, [(da, gain)])
    return dh, dgain, dw_up, dw_down, dconv_w, dconv_b


def kernel(x, hg_norm, hg_w_in, hg_lb_logits, hg_out_norm, hg_w_out, kv_norm, w_kv, attn_norm, attn_w_q, attn_sinks, attn_w_o, ffn_norm, ffn_w_up, ffn_conv_w, ffn_conv_b, ffn_w_down, final_norm, loss_target, m_hg_norm, m_hg_w_in, m_hg_lb_logits, m_hg_out_norm, m_hg_w_out, m_kv_norm, m_w_kv, m_attn_norm, m_attn_w_q, m_attn_sinks, m_attn_w_o, m_ffn_norm, m_ffn_w_up, m_ffn_conv_w, m_ffn_conv_b, m_ffn_w_down, m_final_norm, v_hg_norm, v_hg_w_in, v_hg_lb_logits, v_hg_out_norm, v_hg_w_out, v_kv_norm, v_w_kv, v_attn_norm, v_attn_w_q, v_attn_sinks, v_attn_w_o, v_ffn_norm, v_ffn_w_up, v_ffn_conv_w, v_ffn_conv_b, v_ffn_w_down, v_final_norm):
    _, s, d = x.shape
    x0, target = x[0], loss_target[0]
    half = hg_w_in.shape[2]
    fs = ffn_conv_w.shape[2]
    fb = 2 * fs
    kvd = w_kv.shape[1]
    nq = d // ATT_HEAD_DIM
    tm = _row_tile(s, MM_ROWS)

    mine = _index(_place())
    gather = lambda tag, shards: _send_start("gather_start_" + tag, shards, [_landing(a, mine) for a in shards], False)
    coming_hg = gather("hg", [hg_w_in[0].astype(BF16), hg_norm, hg_lb_logits, hg_w_out[0].astype(BF16)])
    coming_ffn0 = gather("ffn0", [ffn_w_up[0].astype(BF16), ffn_conv_w, ffn_w_down[0].astype(BF16)])
    coming_attn = gather("attn", [w_kv.astype(BF16), attn_w_q[0].astype(BF16), attn_w_o[0].astype(BF16)])
    coming_ffn1 = gather("ffn1", [ffn_w_up[1].astype(BF16), ffn_w_down[1].astype(BF16)])
    w_in, g_hgn, g_lbl, w_out = _send_wait("gather_wait_hg", coming_hg, coming_hg[4], False)
    w_out = w_out.reshape(d, d)
    hgn = g_hgn.reshape(1, d)
    lbl = g_lbl.transpose(1, 0, 2).reshape(2, d)
    conv_b = [ffn_conv_b[layer].reshape(4, 1, fb) for layer in range(2)]
    gains = [ffn_norm[0:1], ffn_norm[1:2]]
    kvn, fin = kv_norm.reshape(1, d), final_norm.reshape(1, d)

    a0, = _rmsnorm_cast("hg_norm", x0, [hgn])
    p = _matmul(
        "hg_in", a0, w_in, dims=NN, grid=(s // tm, N_DEV, 1),
        a_spec=pl.BlockSpec((tm, d), lambda i, j, k: (i, 0)),
        b_spec=pl.BlockSpec((None, d, half), lambda i, j, k: (j, 0, 0)),
        o_spec=pl.BlockSpec((None, tm, half), lambda i, j, k: (j // 2, i, j % 2)),
        out_shape=jax.ShapeDtypeStruct((4, s, d), BF16), acc_shape=(8, 128))
    o, og, states = _hgrn2_fwd(p, lbl, hg_out_norm)
    x1 = _mm_rows("hg_out", og, w_out, out_dtype=F32, add=x0)
    w_up0, g_cw, w_dn0 = _send_wait("gather_wait_ffn0", coming_ffn0, x1, False)
    w_up, w_dn = [w_up0, None], [w_dn0.reshape(4, fb, d), None]
    conv_w = [g_cw[:, layer].reshape(4, 2, CONV_WIDTH, fs).transpose(0, 2, 1, 3).reshape(4, CONV_WIDTH, fb) for layer in range(2)]
    x2, saved0 = _ffn_forward("0", x1, gains[0], w_up[0], w_dn[0], conv_w[0], conv_b[0])
    w_kvg, w_q, w_o = _send_wait("gather_wait_attn", coming_attn, x2, False)
    w_kvg, w_q, w_o = w_kvg.reshape(d, kvd), w_q.reshape(d, d), w_o.reshape(d, d)
    akv, a2 = _rmsnorm_cast("attn_norms", x2, [kvn, attn_norm])
    kv = _mm_rows("kv_proj", akv, w_kvg, out_dtype=BF16)
    q = _mm_rows("q_proj", a2, w_q, out_dtype=BF16)
    att = _attn_fwd(q, kv, attn_sinks)
    x3 = _mm_rows("attn_out", att, w_o, out_dtype=F32, add=x2)
    w_up[1], w_dn1 = _send_wait("gather_wait_ffn1", coming_ffn1, x3, False)
    w_dn[1] = w_dn1.reshape(4, fb, d)
    x4, saved1 = _ffn_forward("1", x3, gains[1], w_up[1], w_dn[1], conv_w[1], conv_b[1])
    dx4, d_fin, loss_part = _loss_head(x4, fin, target)

    dx3, d_fn1, dw_up1, dw_dn1, dcw1, dcb1 = _ffn_backward("1", x3, gains[1], w_up[1], w_dn[1], conv_w[1], conv_b[1], saved1, dx4)
    rows = d // N_DEV
    scatter = lambda tag, stacks: _send_start("scatter_start_" + tag, stacks, [_landing(lax.dynamic_index_in_dim(a, mine, keepdims=False), mine) for a in stacks], True)
    going_ffn1 = scatter("ffn1", [dw_up1, dw_dn1.reshape(N_DEV, fs, d)])
    datt = _mm_rows_nt("attn_out_bwd", dx3, w_o, out_dtype=BF16)
    dw_o = _mm_tn("attn_out_grad", att, dx3)
    dq, dkv_own, dkv_before, dsink = _attn_bwd(q, kv, att, datt, attn_sinks)
    tiles = dkv_before.shape[0]
    dkv = dkv_own.reshape(tiles, s // tiles, kvd)
    dkv = jnp.concatenate([dkv[:, :-WINDOW], dkv[:, -WINDOW:] + jnp.pad(dkv_before[1:], ((0, 1), (0, 0), (0, 0)))], axis=1).reshape(s, kvd)
    da2 = _mm_rows_nt("q_proj_bwd", dq, w_q, out_dtype=F32)
    dw_q = _mm_tn("q_proj_grad", a2, dq)
    dakv = _mm_rows_nt("kv_proj_bwd", dkv, w_kvg, out_dtype=F32)
    dw_kv = _mm_tn("kv_proj_grad", akv, dkv)
    going_attn = scatter("attn", [dw_kv.reshape(N_DEV, rows, kvd), dw_q.reshape(N_DEV, rows, d), dw_o.reshape(N_DEV, rows, d)])
    dx2, (d_kvn, d_attn) = _rmsnorm_bwd("attn_norms_bwd", x2, dx3, [(dakv, kvn), (da2, attn_norm)])
    dx1, d_fn0, dw_up0, dw_dn0, dcw0, dcb0 = _ffn_backward("0", x1, gains[0], w_up[0], w_dn[0], conv_w[0], conv_b[0], saved0, dx2)
    going_ffn0 = scatter("ffn0", [dw_up0, dw_dn0.reshape(N_DEV, fs, d)])
    dog = _mm_rows_nt("hg_out_bwd", dx1, w_out, out_dtype=F32)
    dw_out = _mm_tn("hg_out_grad", og, dx1)
    dp, d_lbl, d_ogain = _hgrn2_bwd(p, lbl, hg_out_norm, o, dog, states)
    dw_in = _matmul(
        "hg_in_grad", a0, dp, dims=TN, grid=(1, N_DEV, s // tm),
        a_spec=pl.BlockSpec((tm, d), lambda i, j, k: (k, 0)),
        b_spec=pl.BlockSpec((None, tm, half), lambda i, j, k: (j // 2, k, j % 2)),
        o_spec=pl.BlockSpec((None, d, half), lambda i, j, k: (j, 0, 0)),
        out_shape=jax.ShapeDtypeStruct((N_DEV, d, half), BF16), acc_shape=(d, half))
    going_hg = scatter("hg", [dw_in, dw_out.reshape(N_DEV, rows, d)])
    da0 = _matmul(
        "hg_in_bwd", dp, w_in, dims=NT, grid=(s // tm, 1, N_DEV),
        a_spec=pl.BlockSpec((None, tm, half), lambda i, j, k: (k // 2, i, k % 2)),
        b_spec=pl.BlockSpec((None, d, half), lambda i, j, k: (k, 0, 0)),
        o_spec=pl.BlockSpec((tm, d), lambda i, j, k: (i, 0)),
        out_shape=jax.ShapeDtypeStruct((s, d), F32), acc_shape=(tm, d))
    dx0, (d_hgn,) = _rmsnorm_bwd("hg_norm_bwd", x0, dx1, [(da0, hgn)])

    arrive = lambda tag, going: _send_wait("scatter_wait_" + tag, going, dx0, True)
    (l_up1, l_dn1), (l_kv, l_q, l_o), (l_up0, l_dn0), (l_in, l_out) = (
        arrive("ffn1", going_ffn1), arrive("attn", going_attn), arrive("ffn0", going_ffn0), arrive("hg", going_hg))
    landed = [l_in, l_out, l_kv, l_q, l_o, l_up0, l_up1, l_dn0, l_dn1]
    big = {}
    for tag, w, m, v, part in [
            ("hg_w_in", hg_w_in[0], m_hg_w_in[0], v_hg_w_in[0], landed[0]), ("hg_w_out", hg_w_out[0], m_hg_w_out[0], v_hg_w_out[0], landed[1]),
            ("w_kv", w_kv, m_w_kv, v_w_kv, landed[2]), ("attn_w_q", attn_w_q[0], m_attn_w_q[0], v_attn_w_q[0], landed[3]),
            ("attn_w_o", attn_w_o[0], m_attn_w_o[0], v_attn_w_o[0], landed[4]),
            ("ffn_w_up0", ffn_w_up[0], m_ffn_w_up[0], v_ffn_w_up[0], landed[5]), ("ffn_w_up1", ffn_w_up[1], m_ffn_w_up[1], v_ffn_w_up[1], landed[6]),
            ("ffn_w_down0", ffn_w_down[0], m_ffn_w_down[0], v_ffn_w_down[0], landed[7]),
            ("ffn_w_down1", ffn_w_down[1], m_ffn_w_down[1], v_ffn_w_down[1], landed[8])]:
        big[tag] = _adamw_shard("adamw_" + tag, w, m, v, part)
    lead = lambda tag: [a[None] for a in big[tag]]
    pair = lambda tag: [jnp.stack([a, b]) for a, b in zip(big[tag + "0"], big[tag + "1"])]

    as_blocks = lambda a, r: a.reshape(r, N_DEV, -1).transpose(1, 0, 2).reshape(N_DEV * r, -1)
    d_cw = jnp.concatenate([g.transpose(1, 0, 2).reshape(CONV_WIDTH, 4 * fb) for g in (dcw0, dcw1)], axis=0)
    parts = [d_fin, jnp.concatenate([d_fn0, d_fn1], axis=0), jnp.concatenate([dcb0.reshape(1, 4 * fb), dcb1.reshape(1, 4 * fb)], axis=0),
             as_blocks(d_cw, 2 * CONV_WIDTH), d_attn, jnp.sum(dsink[:, :, 0], axis=0).reshape(1, nq), d_kvn, d_ogain,
             as_blocks(d_hgn, 1), as_blocks(d_lbl, 2), loss_part]
    offsets, _, _ = _pack_rows(parts)
    gathered, = _all_gather("gather_small_grads", [_pack(parts)], VMEM_SPEC)
    two = lambda a: a.reshape(-1, a.shape[-1])
    small = [(fin, m_final_norm.reshape(1, d), v_final_norm.reshape(1, d), False), (ffn_norm, m_ffn_norm, v_ffn_norm, False),
             (ffn_conv_b, m_ffn_conv_b, v_ffn_conv_b, False), (two(ffn_conv_w), two(m_ffn_conv_w), two(v_ffn_conv_w), True),
             (attn_norm, m_attn_norm, v_attn_norm, False), (attn_sinks, m_attn_sinks, v_attn_sinks, False),
             (kvn, m_kv_norm.reshape(1, d), v_kv_norm.reshape(1, d), False), (hg_out_norm, m_hg_out_norm, v_hg_out_norm, False),
             (hg_norm, m_hg_norm, v_hg_norm, True), (hg_lb_logits, m_hg_lb_logits, v_hg_lb_logits, True)]
    res = _adamw_small(gathered, offsets, small)
    names = ["final_norm", "ffn_norm", "ffn_conv_b", "ffn_conv_w", "attn_norm", "attn_sinks", "kv_norm", "hg_out_norm", "hg_norm", "hg_lb_logits"]
    shapes = {"final_norm": final_norm.shape, "kv_norm": kv_norm.shape, "ffn_conv_w": ffn_conv_w.shape}
    out = {n: [a.reshape(shapes[n]) if n in shapes else a for a in res[4 * i:4 * i + 4]] for i, n in enumerate(names)}
    out.update(hg_w_in=lead("hg_w_in"), hg_w_out=lead("hg_w_out"), w_kv=big["w_kv"], attn_w_q=lead("attn_w_q"), attn_w_o=lead("attn_w_o"),
               ffn_w_up=pair("ffn_w_up"), ffn_w_down=pair("ffn_w_down"))
    order = ["hg_norm", "hg_w_in", "hg_lb_logits", "hg_out_norm", "hg_w_out", "kv_norm", "w_kv", "attn_norm", "attn_w_q", "attn_sinks",
             "attn_w_o", "ffn_norm", "ffn_w_up", "ffn_conv_w", "ffn_conv_b", "ffn_w_down", "final_norm"]
    loss = res[-1][0, 0]
    return (loss, dx0[None], *[out[n][0] for n in order], *[out[n][1] for n in order], *[out[n][2] for n in order], *[out[n][3] for n in order])
```

```python
import functools
import math

import jax
import jax.numpy as jnp
from jax import lax
from jax.experimental import pallas as pl
from jax.experimental.pallas import tpu as pltpu

F32 = jnp.float32
BF16 = jnp.bfloat16

EPS = 1e-6
HG_EXPAND = 128
HG_CHUNK = 32
ATT_HEAD_DIM = 64
ATT_KV_HEADS = 2
WINDOW = 128
CONV_WIDTH = 3
ADAM_LR = 0.001
ADAM_B1 = 0.9
ADAM_B2 = 0.999
ADAM_EPS = 1e-08
ADAM_WD = 0.01
ADAM_STEP = 10

N_DEV = 8
VMEM_LIMIT = 48 * 1024 * 1024
NEG = -1e30

NN = (((1,), (0,)), ((), ()))
NT = (((1,), (1,)), ((), ()))
TN = (((0,), (0,)), ((), ()))
MESH = pl.DeviceIdType.MESH


def _dot(a, b, dims=NN):
    return lax.dot_general(a.astype(BF16), b.astype(BF16), dims, preferred_element_type=F32)


def _sigmoid(x):
    return 1.0 / (1.0 + jnp.exp(-x))


def _silu(x):
    return x * _sigmoid(x)


def _dsilu(x):
    s = _sigmoid(x)
    return s * (1.0 + x * (1.0 - s))


def _params(semantics):
    return pltpu.CompilerParams(dimension_semantics=semantics, vmem_limit_bytes=VMEM_LIMIT)


def _row_tile(rows, want=512):
    return min(rows, want)


MM_ROWS = 1024


def _matmul(name, a, b, *, dims, grid, a_spec, b_spec, o_spec, out_shape, acc_shape, add=None, add_spec=None):
    nk = grid[2]

    def body(*refs):
        if add is None:
            a_ref, b_ref, o_ref, acc = refs
        else:
            a_ref, b_ref, add_ref, o_ref, acc = refs
        k = pl.program_id(2)
        part = _dot(a_ref[...], b_ref[...], dims)

        def finish(total):
            if add is not None:
                total = total + add_ref[...]
            o_ref[...] = total.astype(o_ref.dtype)

        if nk == 1:
            finish(part)
        else:
            @pl.when(k == 0)
            def _():
                acc[...] = part

            @pl.when(k > 0)
            def _():
                acc[...] += part

            @pl.when(k == nk - 1)
            def _():
                finish(acc[...])

    in_specs = [a_spec, b_spec] + ([] if add is None else [add_spec])
    args = (a, b) + (() if add is None else (add,))
    return pl.pallas_call(
        body, name=name, grid=grid, in_specs=in_specs, out_specs=o_spec, out_shape=out_shape,
        scratch_shapes=[pltpu.VMEM(acc_shape, F32)],
        compiler_params=_params(("parallel", "parallel", "arbitrary")),
    )(*args)


def _mm_rows(name, a, w, *, out_dtype, add=None):
    s, kdim = a.shape
    n = w.shape[1]
    tm = _row_tile(s, MM_ROWS)
    return _matmul(
        name, a, w, dims=NN, grid=(s // tm, 1, 1),
        a_spec=pl.BlockSpec((tm, kdim), lambda i, j, k: (i, 0)),
        b_spec=pl.BlockSpec((kdim, n), lambda i, j, k: (0, 0)),
        o_spec=pl.BlockSpec((tm, n), lambda i, j, k: (i, 0)),
        out_shape=jax.ShapeDtypeStruct((s, n), out_dtype), acc_shape=(8, 128),
        add=add, add_spec=None if add is None else pl.BlockSpec((tm, n), lambda i, j, k: (i, 0)),
    )


def _mm_rows_nt(name, a, w, *, out_dtype):
    s, n = a.shape
    kdim = w.shape[0]
    tm = _row_tile(s, MM_ROWS)
    return _matmul(
        name, a, w, dims=NT, grid=(s // tm, 1, 1),
        a_spec=pl.BlockSpec((tm, n), lambda i, j, k: (i, 0)),
        b_spec=pl.BlockSpec((kdim, n), lambda i, j, k: (0, 0)),
        o_spec=pl.BlockSpec((tm, kdim), lambda i, j, k: (i, 0)),
        out_shape=jax.ShapeDtypeStruct((s, kdim), out_dtype), acc_shape=(8, 128),
    )


def _mm_tn(name, a, g):
    s, m = a.shape
    n = g.shape[1]
    ts = _row_tile(s, MM_ROWS)
    return _matmul(
        name, a, g, dims=TN, grid=(1, 1, s // ts),
        a_spec=pl.BlockSpec((ts, m), lambda i, j, k: (k, 0)),
        b_spec=pl.BlockSpec((ts, n), lambda i, j, k: (k, 0)),
        o_spec=pl.BlockSpec((m, n), lambda i, j, k: (0, 0)),
        out_shape=jax.ShapeDtypeStruct((m, n), BF16), acc_shape=(m, n),
    )


def _rmsnorm_cast(name, h, gains):
    s, d = h.shape
    tm = _row_tile(s)
    n = len(gains)

    def body(*refs):
        h_ref, g_refs, o_refs = refs[0], refs[1:1 + n], refs[1 + n:]
        xv = h_ref[...]
        xhat = xv * lax.rsqrt(jnp.mean(xv * xv, axis=-1, keepdims=True) + EPS)
        for g_ref, o_ref in zip(g_refs, o_refs):
            o_ref[...] = (xhat * g_ref[...]).astype(BF16)

    row = pl.BlockSpec((tm, d), lambda i: (i, 0))
    vec = pl.BlockSpec((1, d), lambda i: (0, 0))
    return pl.pallas_call(
        body, name=name, grid=(s // tm,), in_specs=[row] + [vec] * n, out_specs=[row] * n,
        out_shape=[jax.ShapeDtypeStruct((s, d), BF16)] * n, compiler_params=_params(("parallel",)),
    )(h, *gains)


def _rmsnorm_bwd(name, h, dres, branches):
    s, d = h.shape
    tm = _row_tile(s)
    n = len(branches)

    def body(*refs):
        h_ref, dres_ref = refs[0], refs[1]
        da_refs, g_refs = refs[2:2 + n], refs[2 + n:2 + 2 * n]
        dh_ref, dg_refs = refs[2 + 2 * n], refs[3 + 2 * n:]
        i = pl.program_id(0)
        xv = h_ref[...]
        r = lax.rsqrt(jnp.mean(xv * xv, axis=-1, keepdims=True) + EPS)
        xhat = xv * r
        total = dres_ref[...]
        for da_ref, g_ref, dg_ref in zip(da_refs, g_refs, dg_refs):
            da = da_ref[...]
            dgain = jnp.sum(da * xhat, axis=0, keepdims=True)

            @pl.when(i == 0)
            def _():
                dg_ref[...] = dgain

            @pl.when(i > 0)
            def _():
                dg_ref[...] += dgain

            dxhat = da * g_ref[...]
            total = total + r * (dxhat - xhat * jnp.mean(dxhat * xhat, axis=-1, keepdims=True))
        dh_ref[...] = total

    row = pl.BlockSpec((tm, d), lambda i: (i, 0))
    vec = pl.BlockSpec((1, d), lambda i: (0, 0))
    outs = pl.pallas_call(
        body, name=name, grid=(s // tm,), in_specs=[row, row] + [row] * n + [vec] * n, out_specs=[row] + [vec] * n,
        out_shape=[jax.ShapeDtypeStruct((s, d), F32)] + [jax.ShapeDtypeStruct((1, d), F32)] * n,
        compiler_params=_params(("arbitrary",)),
    )(h, dres, *[b[0] for b in branches], *[b[1] for b in branches])
    return outs[0], outs[1:]


def _loss_head(h, gain, target):
    s, d = h.shape
    tm = _row_tile(s)

    def body(h_ref, g_ref, t_ref, dh_ref, dg_ref, loss_ref):
        i = pl.program_id(0)
        xv = h_ref[...]
        r = lax.rsqrt(jnp.mean(xv * xv, axis=-1, keepdims=True) + EPS)
        xhat = xv * r
        err = xhat * g_ref[...] - t_ref[...]
        dy = err * (1.0 / d)
        part = jnp.zeros((1, 128), F32) + 0.5 * jnp.sum(jnp.mean(err * err, axis=-1, keepdims=True))
        dgain = jnp.sum(dy * xhat, axis=0, keepdims=True)

        @pl.when(i == 0)
        def _():
            dg_ref[...] = dgain
            loss_ref[...] = part

        @pl.when(i > 0)
        def _():
            dg_ref[...] += dgain
            loss_ref[...] += part

        dxhat = dy * g_ref[...]
        dh_ref[...] = r * (dxhat - xhat * jnp.mean(dxhat * xhat, axis=-1, keepdims=True))

    row = pl.BlockSpec((tm, d), lambda i: (i, 0))
    vec = pl.BlockSpec((1, d), lambda i: (0, 0))
    return pl.pallas_call(
        body, name="loss_head", grid=(s // tm,), in_specs=[row, vec, row],
        out_specs=[row, vec, pl.BlockSpec((1, 128), lambda i: (0, 0))],
        out_shape=[jax.ShapeDtypeStruct((s, d), F32), jax.ShapeDtypeStruct((1, d), F32), jax.ShapeDtypeStruct((1, 128), F32)],
        compiler_params=_params(("arbitrary",)),
    )(h, gain, target)


def _bdot(a, b, ca, cb):
    return lax.dot_general(a.astype(BF16), b.astype(BF16), (((ca,), (cb,)), ((0,), (0,))), preferred_element_type=F32)


def _chunk_cumsum(xv, reverse=False):
    n = xv.shape[0]
    row = lax.broadcasted_iota(jnp.int32, xv.shape, 0) % HG_CHUNK
    step = 1
    while step < HG_CHUNK:
        if reverse:
            xv = xv + jnp.where(row < HG_CHUNK - step, pltpu.roll(xv, n - step, axis=0), 0.0)
        else:
            xv = xv + jnp.where(row >= step, pltpu.roll(xv, step, axis=0), 0.0)
        step *= 2
    return xv


def _hg_terms(p_ref, lbl_ref):
    pq = p_ref[0].astype(F32)
    pf = p_ref[1].astype(F32)
    lb = _sigmoid(lbl_ref[0:1, :] - lbl_ref[1:2, :])
    sig = _sigmoid(pf)
    fg = lb + (1.0 - lb) * sig
    nc = pq.shape[0] // HG_CHUNK
    chunks = lambda a: a.reshape(nc, HG_CHUNK, HG_EXPAND)
    q = chunks(_silu(pq) * HG_EXPAND ** -0.5)
    k = chunks(1.0 - fg)
    v = chunks(p_ref[2].astype(F32))
    g = chunks(_chunk_cumsum(jnp.log(fg)))
    gm = g[:, HG_CHUNK // 2 - 1:HG_CHUNK // 2, :]
    gl = g[:, HG_CHUNK - 1:HG_CHUNK, :]
    e_mid, e_inv, e_all, e_end = jnp.exp(g - gm), jnp.exp(gm - g), jnp.exp(g), jnp.exp(gl - g)
    terms = dict(q=q, k=k, v=v, qd=q * e_all, qt=q * e_mid, kt=k * e_inv, kd=k * e_end, e_last=jnp.exp(gl),
                 e_mid=e_mid, e_inv=e_inv, e_all=e_all, e_end=e_end)
    return terms, (pq, sig, fg, lb)


def _causal(nc):
    r = lax.broadcasted_iota(jnp.int32, (nc, HG_CHUNK, HG_CHUNK), 1)
    c = lax.broadcasted_iota(jnp.int32, (nc, HG_CHUNK, HG_CHUNK), 2)
    return r >= c


def _hgrn2_fwd(p, lb_logits, out_gain):
    _, s, d = p.shape
    heads = d // HG_EXPAND
    t = _row_tile(s)
    nc = t // HG_CHUNK

    def body(p_ref, lbl_ref, gain_ref, o_ref, og_ref, st_ref, state, decay):
        @pl.when(pl.program_id(1) == 0)
        def _():
            state[...] = jnp.zeros_like(state)

        tm, _ = _hg_terms(p_ref, lbl_ref)
        decay[...] = tm["e_last"]
        st_ref[...] = _bdot(tm["v"], tm["kd"], 1, 1)

        def chunk(c, carry):
            add = st_ref[c]
            st = state[...]
            st_ref[c] = st
            state[...] = st * decay[c] + add
            return carry

        lax.fori_loop(0, nc, chunk, 0)
        a = jnp.where(_causal(nc), _bdot(tm["qt"], tm["kt"], 2, 2), 0.0)
        ov = (_bdot(tm["qd"], st_ref[...], 2, 2) + _bdot(a, tm["v"], 2, 1)).reshape(t, HG_EXPAND)
        o_ref[...] = ov
        on = ov * lax.rsqrt(jnp.mean(ov * ov, axis=-1, keepdims=True) + EPS) * gain_ref[...]
        og_ref[...] = (on * _silu(p_ref[3].astype(F32))).astype(BF16)

    blk = pl.BlockSpec((t, HG_EXPAND), lambda h, b: (b, h))
    return pl.pallas_call(
        body, name="hgrn2_fwd", grid=(heads, s // t),
        in_specs=[pl.BlockSpec((4, t, HG_EXPAND), lambda h, b: (0, b, h)), pl.BlockSpec((2, HG_EXPAND), lambda h, b: (0, h)),
                  pl.BlockSpec((1, HG_EXPAND), lambda h, b: (0, 0))],
        out_specs=[blk, blk, pl.BlockSpec((None, nc, HG_EXPAND, HG_EXPAND), lambda h, b: (h, b, 0, 0))],
        out_shape=[jax.ShapeDtypeStruct((s, d), F32), jax.ShapeDtypeStruct((s, d), BF16),
                   jax.ShapeDtypeStruct((heads, s // HG_CHUNK, HG_EXPAND, HG_EXPAND), F32)],
        scratch_shapes=[pltpu.VMEM((HG_EXPAND, HG_EXPAND), F32), pltpu.VMEM((nc, 1, HG_EXPAND), F32)],
        compiler_params=_params(("parallel", "arbitrary")),
    )(p, lb_logits, out_gain)


def _hgrn2_bwd(p, lb_logits, out_gain, o, dog, states):
    _, s, d = p.shape
    heads = d // HG_EXPAND
    t = _row_tile(s)
    nc = t // HG_CHUNK
    nb = s // t

    def body(p_ref, lbl_ref, gain_ref, o_ref, dog_ref, st_ref, dp_ref, dlbl_ref, dgain_ref, dstate, decay, dst_s):
        h, b = pl.program_id(0), pl.program_id(1)

        @pl.when(b == 0)
        def _():
            dstate[...] = jnp.zeros_like(dstate)

        tm, (pq, sig, fg, lb) = _hg_terms(p_ref, lbl_ref)
        pg = p_ref[3].astype(F32)
        ov = o_ref[...]
        r = lax.rsqrt(jnp.mean(ov * ov, axis=-1, keepdims=True) + EPS)
        ohat = ov * r
        dogv = dog_ref[...]
        d_on = dogv * _silu(pg)
        dp_ref[3] = (dogv * ohat * gain_ref[...] * _dsilu(pg)).astype(BF16)
        dgain = jnp.sum(d_on * ohat, axis=0, keepdims=True)

        @pl.when((h == 0) & (b == 0))
        def _():
            dgain_ref[...] = dgain

        @pl.when((h > 0) | (b > 0))
        def _():
            dgain_ref[...] += dgain

        dohat = d_on * gain_ref[...]
        do = (r * (dohat - ohat * jnp.mean(dohat * ohat, axis=-1, keepdims=True))).reshape(nc, HG_CHUNK, HG_EXPAND)

        decay[...] = tm["e_last"]
        dst_s[...] = _bdot(do, tm["qd"], 1, 1)

        def chunk(i, carry):
            c = nc - 1 - i
            add = dst_s[c]
            dst = dstate[...]
            dst_s[c] = dst
            dstate[...] = dst * decay[c] + add
            return carry

        lax.fori_loop(0, nc, chunk, 0)
        st, dst = st_ref[...], dst_s[...]
        causal = _causal(nc)
        a = jnp.where(causal, _bdot(tm["qt"], tm["kt"], 2, 2), 0.0)
        da = jnp.where(causal, _bdot(do, tm["v"], 2, 2), 0.0)
        dqt = _bdot(da, tm["kt"], 2, 1)
        dkt = _bdot(da, tm["qt"], 1, 1)
        dqd = _bdot(do, st, 2, 1)
        dkd = _bdot(tm["v"], dst, 2, 1)
        dv = _bdot(a, do, 1, 1) + _bdot(tm["kd"], dst, 2, 2)
        dq = dqt * tm["e_mid"] + dqd * tm["e_all"]
        dk = dkt * tm["e_inv"] + dkd * tm["e_end"]
        dg = dqt * tm["qt"] - dkt * tm["kt"] + dqd * tm["qd"] - dkd * tm["kd"]
        dgl = jnp.sum(dkd * tm["kd"], axis=1, keepdims=True) + tm["e_last"] * jnp.sum(dst * st, axis=1, keepdims=True)
        last_row = lax.broadcasted_iota(jnp.int32, (nc, HG_CHUNK, HG_EXPAND), 1) == HG_CHUNK - 1
        flat = lambda a3: a3.reshape(t, HG_EXPAND)
        dlf = _chunk_cumsum(flat(dg + jnp.where(last_row, dgl, 0.0)), reverse=True)
        dfg = dlf / fg - flat(dk)
        dlb = jnp.sum(dfg * (1.0 - sig), axis=0, keepdims=True)
        dl0 = dlb * lb * (1.0 - lb)
        dlbl = jnp.concatenate([dl0, -dl0], axis=0)

        @pl.when(b == 0)
        def _():
            dlbl_ref[...] = dlbl

        @pl.when(b > 0)
        def _():
            dlbl_ref[...] += dlbl

        dp_ref[0] = (flat(dq) * HG_EXPAND ** -0.5 * _dsilu(pq)).astype(BF16)
        dp_ref[1] = (dfg * (1.0 - lb) * sig * (1.0 - sig)).astype(BF16)
        dp_ref[2] = flat(dv).astype(BF16)

    blk = pl.BlockSpec((t, HG_EXPAND), lambda h, b: (nb - 1 - b, h))
    pblk = pl.BlockSpec((4, t, HG_EXPAND), lambda h, b: (0, nb - 1 - b, h))
    return pl.pallas_call(
        body, name="hgrn2_bwd", grid=(heads, nb),
        in_specs=[pblk, pl.BlockSpec((2, HG_EXPAND), lambda h, b: (0, h)), pl.BlockSpec((1, HG_EXPAND), lambda h, b: (0, 0)),
                  blk, blk, pl.BlockSpec((None, nc, HG_EXPAND, HG_EXPAND), lambda h, b: (h, nb - 1 - b, 0, 0))],
        out_specs=[pblk, pl.BlockSpec((2, HG_EXPAND), lambda h, b: (0, h)), pl.BlockSpec((1, HG_EXPAND), lambda h, b: (0, 0))],
        out_shape=[jax.ShapeDtypeStruct((4, s, d), BF16), jax.ShapeDtypeStruct((2, d), F32), jax.ShapeDtypeStruct((1, HG_EXPAND), F32)],
        scratch_shapes=[pltpu.VMEM((HG_EXPAND, HG_EXPAND), F32), pltpu.VMEM((nc, 1, HG_EXPAND), F32),
                        pltpu.VMEM((nc, HG_EXPAND, HG_EXPAND), F32)],
        compiler_params=_params(("arbitrary", "arbitrary")),
    )(p, lb_logits, out_gain, o, dog, states)


HALO = 8


def _shift_down(xv, n):
    return pltpu.roll(xv, n, axis=0)


def _shift_up(xv, n):
    return pltpu.roll(xv, xv.shape[0] - n, axis=0)


def _ffn_hidden(u, conv_w, conv_b):
    _, nj, s, fb = u.shape
    tm = _row_tile(s)
    per = tm // HALO

    def body(gate_ref, prev_ref, val_ref, w_ref, b_ref, h_ref):
        i = pl.program_id(1)
        prev = jnp.where(i > 0, prev_ref[...].astype(F32), 0.0)
        ext = jnp.concatenate([prev, gate_ref[...].astype(F32)], axis=0)
        conv = b_ref[...] + w_ref[2:3, :] * ext[HALO:]
        conv = conv + w_ref[1:2, :] * _shift_down(ext, 1)[HALO:]
        conv = conv + w_ref[0:1, :] * _shift_down(ext, 2)[HALO:]
        h_ref[...] = (_silu(conv) * val_ref[...].astype(F32)).astype(BF16)

    return pl.pallas_call(
        body, name="ffn_hidden", grid=(nj, s // tm),
        in_specs=[pl.BlockSpec((None, None, tm, fb), lambda j, i: (0, j, i, 0)),
                  pl.BlockSpec((None, None, HALO, fb), lambda j, i: (0, j, jnp.maximum(i * per - 1, 0), 0)),
                  pl.BlockSpec((None, None, tm, fb), lambda j, i: (1, j, i, 0)),
                  pl.BlockSpec((None, CONV_WIDTH, fb), lambda j, i: (j, 0, 0)),
                  pl.BlockSpec((None, 1, fb), lambda j, i: (j, 0, 0))],
        out_specs=pl.BlockSpec((None, tm, fb), lambda j, i: (j, i, 0)),
        out_shape=jax.ShapeDtypeStruct((nj, s, fb), BF16), compiler_params=_params(("parallel", "parallel")),
    )(u, u, u, conv_w, conv_b)


def _ffn_hidden_bwd(u, dh, conv_w, conv_b):
    _, nj, s, fb = u.shape
    tm = _row_tile(s)
    per = tm // HALO
    nblk = s // HALO
    ni = s // tm

    def body(gate_ref, gprev_ref, gnext_ref, val_ref, vnext_ref, dh_ref, dhnext_ref, w_ref, b_ref, du_ref, dw_ref, db_ref):
        i = pl.program_id(1)
        has_next = i < ni - 1
        gprev = jnp.where(i > 0, gprev_ref[...].astype(F32), 0.0)
        gext = jnp.concatenate([gprev, gate_ref[...].astype(F32), gnext_ref[...].astype(F32)], axis=0)
        vext = jnp.concatenate([val_ref[...].astype(F32), vnext_ref[...].astype(F32)], axis=0)
        dhext = jnp.concatenate([dh_ref[...].astype(F32), jnp.where(has_next, dhnext_ref[...].astype(F32), 0.0)], axis=0)
        g0 = gext[HALO:]
        g1 = _shift_down(gext, 1)[HALO:]
        g2 = _shift_down(gext, 2)[HALO:]
        conv = b_ref[...] + w_ref[2:3, :] * g0 + w_ref[1:2, :] * g1 + w_ref[0:1, :] * g2
        dconv = dhext * vext * _dsilu(conv)
        dgate = w_ref[2:3, :] * dconv + w_ref[1:2, :] * _shift_up(dconv, 1) + w_ref[0:1, :] * _shift_up(dconv, 2)
        du_ref[0] = dgate[:tm].astype(BF16)
        du_ref[1] = (dhext * _silu(conv))[:tm].astype(BF16)
        own = dconv[:tm]
        dw = jnp.concatenate([jnp.sum(own * g2[:tm], axis=0, keepdims=True), jnp.sum(own * g1[:tm], axis=0, keepdims=True),
                              jnp.sum(own * g0[:tm], axis=0, keepdims=True)], axis=0)
        db = jnp.sum(own, axis=0, keepdims=True)

        @pl.when(i == 0)
        def _():
            dw_ref[...] = dw
            db_ref[...] = db

        @pl.when(i > 0)
        def _():
            dw_ref[...] += dw
            db_ref[...] += db

    def tile(part):
        return pl.BlockSpec((None, None, tm, fb), lambda j, i: (part, j, i, 0))

    def after(part):
        return pl.BlockSpec((None, None, HALO, fb), lambda j, i: (part, j, jnp.minimum((i + 1) * per, nblk - 1), 0))

    return pl.pallas_call(
        body, name="ffn_hidden_bwd", grid=(nj, ni),
        in_specs=[tile(0), pl.BlockSpec((None, None, HALO, fb), lambda j, i: (0, j, jnp.maximum(i * per - 1, 0), 0)), after(0),
                  tile(1), after(1),
                  pl.BlockSpec((None, tm, fb), lambda j, i: (j, i, 0)),
                  pl.BlockSpec((None, HALO, fb), lambda j, i: (j, jnp.minimum((i + 1) * per, nblk - 1), 0)),
                  pl.BlockSpec((None, CONV_WIDTH, fb), lambda j, i: (j, 0, 0)), pl.BlockSpec((None, 1, fb), lambda j, i: (j, 0, 0))],
        out_specs=[pl.BlockSpec((2, None, tm, fb), lambda j, i: (0, j, i, 0)),
                   pl.BlockSpec((None, CONV_WIDTH, fb), lambda j, i: (j, 0, 0)), pl.BlockSpec((None, 1, fb), lambda j, i: (j, 0, 0))],
        out_shape=[jax.ShapeDtypeStruct((2, nj, s, fb), BF16), jax.ShapeDtypeStruct((nj, CONV_WIDTH, fb), F32),
                   jax.ShapeDtypeStruct((nj, 1, fb), F32)],
        compiler_params=_params(("parallel", "arbitrary")),
    )(u, u, u, u, u, dh, dh, conv_w, conv_b)


ATT_TILE = 512


def _attn_probs(q, kb, sink, head, first, n_heads):
    iq = lax.broadcasted_iota(jnp.int32, (WINDOW, 2 * WINDOW), 0)
    ik = lax.broadcasted_iota(jnp.int32, (WINDOW, 2 * WINDOW), 1)
    dist = iq + WINDOW - ik
    valid = (dist >= 0) & (dist < WINDOW) & (ik >= jnp.where(first, WINDOW, 0))
    slope = 2.0 ** (-8.0 * (head + 1) / n_heads)
    sc = jnp.where(valid, _dot(q, kb, NT) * ATT_HEAD_DIM ** -0.5 - slope * dist.astype(F32), NEG)
    m = jnp.maximum(jnp.max(sc, axis=-1, keepdims=True), sink)
    e = jnp.exp(sc - m)
    es = jnp.exp(sink - m)
    inv = 1.0 / (jnp.sum(e, axis=-1, keepdims=True) + es)
    return e * inv, es * inv


def _attn_specs(s, d, kvd, tq):
    per = tq // WINDOW
    return [pl.BlockSpec((tq, d), lambda i: (i, 0)), pl.BlockSpec((tq, kvd), lambda i: (i, 0)),
            pl.BlockSpec((WINDOW, kvd), lambda i: (jnp.maximum(i * per - 1, 0), 0))]


def _attn_fwd(q, kv, sinks):
    s, d = q.shape
    kvd = kv.shape[1]
    half = kvd // 2
    hd = ATT_HEAD_DIM
    nq = d // hd
    group = nq // ATT_KV_HEADS
    tq = min(s, ATT_TILE)
    per = tq // WINDOW

    def body(q_ref, kvc_ref, kvp_ref, sink_ref, o_ref, band):
        i = pl.program_id(0)
        band[0:WINDOW, :] = kvp_ref[...]
        band[WINDOW:, :] = kvc_ref[...]

        def block(b, carry):
            rows = pl.ds(pl.multiple_of(b * WINDOW, WINDOW), WINDOW)
            keys = pl.ds(pl.multiple_of(b * WINDOW, WINDOW), 2 * WINDOW)
            first = (i * per + b) == 0
            for pair in range(nq // 2):
                outs = []
                for h in (2 * pair, 2 * pair + 1):
                    g = h // group
                    p, _ = _attn_probs(q_ref[rows, h * hd:(h + 1) * hd], band[keys, g * hd:(g + 1) * hd], sink_ref[0, h], h, first, nq)
                    outs.append(_dot(p, band[keys, half + g * hd:half + (g + 1) * hd]))
                o_ref[rows, pair * 2 * hd:(pair + 1) * 2 * hd] = jnp.concatenate(outs, axis=1).astype(BF16)
            return carry

        lax.fori_loop(0, per, block, 0)

    return pl.pallas_call(
        body, name="attn_fwd", grid=(s // tq,),
        in_specs=_attn_specs(s, d, kvd, tq) + [pl.BlockSpec(memory_space=pltpu.SMEM)],
        out_specs=pl.BlockSpec((tq, d), lambda i: (i, 0)), out_shape=jax.ShapeDtypeStruct((s, d), BF16),
        scratch_shapes=[pltpu.VMEM((tq + WINDOW, kvd), BF16)], compiler_params=_params(("parallel",)),
    )(q, kv, kv, sinks)


def _attn_bwd(q, kv, o, do, sinks):
    s, d = q.shape
    kvd = kv.shape[1]
    half = kvd // 2
    hd = ATT_HEAD_DIM
    nq = d // hd
    group = nq // ATT_KV_HEADS
    tq = min(s, ATT_TILE)
    per = tq // WINDOW
    nt = s // tq

    def body(q_ref, kvc_ref, kvp_ref, o_ref, do_ref, sink_ref, dq_ref, dkvc_ref, dkvp_ref, ds_ref, band, dband):
        i = pl.program_id(0)
        band[0:WINDOW, :] = kvp_ref[...]
        band[WINDOW:, :] = kvc_ref[...]
        dband[...] = jnp.zeros_like(dband)
        ds_ref[...] = jnp.zeros_like(ds_ref)

        def block(b, carry):
            rows = pl.ds(pl.multiple_of(b * WINDOW, WINDOW), WINDOW)
            keys = pl.ds(pl.multiple_of(b * WINDOW, WINDOW), 2 * WINDOW)
            first = (i * per + b) == 0
            dks, dvs = [], []
            for g in range(ATT_KV_HEADS):
                kb = band[keys, g * hd:(g + 1) * hd]
                vb = band[keys, half + g * hd:half + (g + 1) * hd]
                dk = jnp.zeros((2 * WINDOW, hd), F32)
                dv = jnp.zeros((2 * WINDOW, hd), F32)
                dqs = []
                for j in range(group):
                    h = g * group + j
                    cols = slice(h * hd, (h + 1) * hd)
                    qv, dov = q_ref[rows, cols], do_ref[rows, cols]
                    p, ps = _attn_probs(qv, kb, sink_ref[0, h], h, first, nq)
                    dsum = jnp.sum(dov.astype(F32) * o_ref[rows, cols].astype(F32), axis=-1, keepdims=True)
                    dsc = p * (_dot(dov, vb, NT) - dsum) * ATT_HEAD_DIM ** -0.5
                    dqs.append(_dot(dsc, kb))
                    dk = dk + _dot(dsc, qv, TN)
                    dv = dv + _dot(p, dov, TN)
                    ds_ref[h:h + 1, :] += jnp.zeros((1, 128), F32) - jnp.sum(ps * dsum)
                    if j % 2 == 1:
                        dq_ref[rows, (h - 1) * hd:(h + 1) * hd] = jnp.concatenate(dqs[-2:], axis=1).astype(BF16)
                dks.append(dk)
                dvs.append(dv)
            dband[keys, 0:half] += jnp.concatenate(dks, axis=1)
            dband[keys, half:] += jnp.concatenate(dvs, axis=1)
            return carry

        lax.fori_loop(0, per, block, 0)
        dkvp_ref[...] = dband[0:WINDOW, :]
        dkvc_ref[...] = dband[WINDOW:, :]

    big = pl.BlockSpec((tq, d), lambda i: (i, 0))
    return pl.pallas_call(
        body, name="attn_bwd", grid=(nt,),
        in_specs=_attn_specs(s, d, kvd, tq) + [big, big, pl.BlockSpec(memory_space=pltpu.SMEM)],
        out_specs=[big, pl.BlockSpec((tq, kvd), lambda i: (i, 0)), pl.BlockSpec((None, WINDOW, kvd), lambda i: (i, 0, 0)),
                   pl.BlockSpec((None, nq, 128), lambda i: (i, 0, 0))],
        out_shape=[jax.ShapeDtypeStruct((s, d), BF16), jax.ShapeDtypeStruct((s, kvd), F32), jax.ShapeDtypeStruct((nt, WINDOW, kvd), F32),
                   jax.ShapeDtypeStruct((nt, nq, 128), F32)],
        scratch_shapes=[pltpu.VMEM((tq + WINDOW, kvd), BF16), pltpu.VMEM((tq + WINDOW, kvd), F32)],
        compiler_params=_params(("parallel",)),
    )(q, kv, kv, o, do, sinks)


HBM_SPEC = pl.BlockSpec(memory_space=pltpu.HBM)
VMEM_SPEC = pl.BlockSpec(memory_space=pltpu.VMEM)


def _place():
    return lax.axis_index("x"), lax.axis_index("y"), lax.axis_index("c")


def _flip(pos, r):
    return tuple(1 - p if (r >> (2 - a)) & 1 else p for a, p in enumerate(pos))


def _index(pos):
    return 4 * pos[0] + 2 * pos[1] + pos[2]


def _all_gather(name, shards, spec):
    n = len(shards)

    def body(*refs):
        x_refs, o_refs = refs[:n], refs[n:2 * n]
        send_sems, recv_sems, local_sems = refs[2 * n:]
        me = _place()
        sibling = _flip(me, 1)
        far = [_flip(me, r) for r in (4, 2, 6)]

        def copy(t, sem, block, to, src=None):
            rows = o_refs[t].at[_index(block)]
            return pltpu.make_async_remote_copy(
                src_ref=rows if src is None else src, dst_ref=rows, send_sem=send_sems.at[t, sem], recv_sem=recv_sems.at[t, sem],
                device_id=to, device_id_type=MESH)

        own = [pltpu.make_async_copy(x_refs[t], o_refs[t].at[_index(me)], local_sems.at[t]) for t in range(n)]
        for cp in own:
            cp.start()
        first = []
        for t in range(n):
            first.append(copy(t, 0, me, sibling, src=x_refs[t]))
            first += [copy(t, 1 + j, me, peer, src=x_refs[t]) for j, peer in enumerate(far)]
        for cp in first:
            cp.start()
        passed = []
        for j, peer in enumerate(far):
            for t in range(n):
                copy(t, 1 + j, peer, me).wait_recv()
                cp = copy(t, 4 + j, peer, sibling)
                cp.start()
                passed.append(cp)
        for t in range(n):
            copy(t, 0, sibling, me).wait_recv()
            for j, peer in enumerate(far):
                copy(t, 4 + j, _flip(peer, 1), me).wait_recv()
        for cp in first + passed:
            cp.wait_send()
        for cp in own:
            cp.wait()

    return pl.pallas_call(
        body, name=name, in_specs=[spec] * n, out_specs=[spec] * n,
        out_shape=[jax.ShapeDtypeStruct((N_DEV,) + sh.shape, sh.dtype) for sh in shards],
        scratch_shapes=[pltpu.SemaphoreType.DMA((n, 7)), pltpu.SemaphoreType.DMA((n, 7)), pltpu.SemaphoreType.DMA((n,))],
    )(*shards)


SEM_SPEC = pl.BlockSpec(memory_space=pltpu.SEMAPHORE)
ANY_SPEC = pl.BlockSpec(memory_space=pl.ANY)


def _landing(own, mine):
    return lax.dynamic_update_slice(lax.empty((N_DEV,) + own.shape, own.dtype), own[None], (mine,) + (0,) * own.ndim)


def _pinned(a, token):
    return a + token[0:1, 0:1].astype(a.dtype)


def _peer_copies(src_refs, land_refs, send_sems, recv_sems, scatter, arrivals):
    me = _place()
    mine = _index(me)
    copies = []
    for t, (src, land) in enumerate(zip(src_refs, land_refs)):
        for r in range(1, N_DEV):
            peer = _flip(me, r)
            theirs = _index(peer)
            sem = t * (N_DEV - 1) + r - 1
            copies.append(pltpu.make_async_remote_copy(
                src_ref=src.at[theirs] if scatter else src, dst_ref=land.at[theirs if arrivals else mine],
                send_sem=send_sems.at[sem], recv_sem=recv_sems.at[sem], device_id=peer, device_id_type=MESH))
    return copies


def _send_start(name, sources, lands, scatter):
    n = len(sources)

    def body(*refs):
        for out in _peer_copies(refs[:n], refs[n:2 * n], refs[2 * n], refs[2 * n + 1], scatter, False):
            out.start()
        refs[-1][...] = jnp.zeros_like(refs[-1])

    outs = pl.pallas_call(
        body, name=name, in_specs=[HBM_SPEC] * (2 * n), out_specs=[SEM_SPEC, SEM_SPEC] + [HBM_SPEC] * (2 * n) + [VMEM_SPEC],
        out_shape=[pltpu.SemaphoreType.DMA((n * (N_DEV - 1),)), pltpu.SemaphoreType.DMA((n * (N_DEV - 1),))]
        + [pltpu.HBM(a.shape, a.dtype) for a in list(sources) + list(lands)] + [jax.ShapeDtypeStruct((8, 128), F32)],
        input_output_aliases={i: 2 + i for i in range(2 * n)},
        compiler_params=pltpu.CompilerParams(has_side_effects=pltpu.SideEffectType.DATAFLOW_SIDE_EFFECTING),
    )(*[pltpu.with_memory_space_constraint(a, pltpu.HBM) for a in list(sources) + list(lands)])
    return outs[0], outs[1], outs[2:2 + n], outs[2 + n:2 + 2 * n], outs[-1]


def _send_wait(name, started, after, scatter):
    send_sems, recv_sems, sources, lands, _ = started
    n = len(sources)

    def body(*refs):
        for out in _peer_copies(refs[:n], refs[n:2 * n], refs[2 * n], refs[2 * n + 1], scatter, False):
            out.wait_send()
        for arrival in _peer_copies(refs[:n], refs[n:2 * n], refs[2 * n], refs[2 * n + 1], scatter, True):
            arrival.wait_recv()

    outs = pl.pallas_call(
        body, name=name, in_specs=[HBM_SPEC] * (2 * n) + [SEM_SPEC, SEM_SPEC, ANY_SPEC], out_specs=[HBM_SPEC] * (2 * n),
        out_shape=[pltpu.HBM(a.shape, a.dtype) for a in list(sources) + list(lands)],
        input_output_aliases={i: i for i in range(2 * n)},
        compiler_params=pltpu.CompilerParams(has_side_effects=pltpu.SideEffectType.DATAFLOW_SIDE_EFFECTING),
    )(*sources, *lands, send_sems, recv_sems, after)
    return outs[n:]


def _pack_rows(parts):
    offsets, row = [], 0
    for part in parts:
        offsets.append(row)
        row += part.shape[0]
    return offsets, -(-row // 8) * 8, -(-max(part.shape[1] for part in parts) // 128) * 128


def _pack(parts):
    offsets, rows, width = _pack_rows(parts)

    def body(*refs):
        o_ref = refs[-1]
        o_ref[...] = jnp.zeros_like(o_ref)
        for off, ref in zip(offsets, refs[:-1]):
            o_ref[off:off + ref.shape[0], 0:ref.shape[1]] = ref[...]

    return pl.pallas_call(body, name="pack_small_grads", in_specs=[VMEM_SPEC] * len(parts), out_specs=VMEM_SPEC,
                          out_shape=jax.ShapeDtypeStruct((rows, width), F32))(*parts)


def _adamw_math(w, g, m, v):
    m = ADAM_B1 * m + (1.0 - ADAM_B1) * g
    v = ADAM_B2 * v + (1.0 - ADAM_B2) * (g * g)
    m_hat = m / (1.0 - ADAM_B1 ** ADAM_STEP)
    v_hat = v / (1.0 - ADAM_B2 ** ADAM_STEP)
    return -ADAM_LR * (m_hat / (jnp.sqrt(v_hat) + ADAM_EPS) + ADAM_WD * w), m, v


def _adamw_shard(name, w, m, v, partials):
    rows, cols = w.shape
    tr = max(t for t in range(8, min(rows, 256) + 1, 8) if rows % t == 0)

    def body(w_ref, m_ref, v_ref, p_ref, g_ref, d_ref, nm_ref, nv_ref):
        g = p_ref[0].astype(F32)
        for dev in range(1, N_DEV):
            g = g + p_ref[dev].astype(F32)
        g_ref[...] = g
        d_ref[...], nm_ref[...], nv_ref[...] = _adamw_math(w_ref[...], g, m_ref[...], v_ref[...])

    blk = pl.BlockSpec((tr, cols), lambda i: (i, 0))
    return pl.pallas_call(
        body, name=name, grid=(rows // tr,), in_specs=[blk, blk, blk, pl.BlockSpec((N_DEV, tr, cols), lambda i: (0, i, 0))],
        out_specs=[blk] * 4, out_shape=[jax.ShapeDtypeStruct((rows, cols), F32)] * 4, compiler_params=_params(("parallel",)),
    )(w, m, v, partials)


def _adamw_small(gathered, offsets, entries):
    n = len(entries)

    def body(*refs):
        pack_ref = refs[0]
        w_refs, m_refs, v_refs = refs[1:1 + n], refs[1 + n:1 + 2 * n], refs[1 + 2 * n:1 + 3 * n]
        outs = refs[1 + 3 * n:]
        total = pack_ref[0]
        for dev in range(1, N_DEV):
            total = total + pack_ref[dev]
        mine = _index(_place())
        for e in range(n):
            rows, cols = w_refs[e].shape
            off = offsets[e]
            if entries[e][3]:
                g = jnp.zeros((rows, cols), F32)
                for dev in range(N_DEV):
                    g = g + jnp.where(mine == dev, total[off + dev * rows:off + (dev + 1) * rows, 0:cols], 0.0)
            else:
                g = total[off:off + rows, 0:cols]
            outs[4 * e][...] = g
            outs[4 * e + 1][...], outs[4 * e + 2][...], outs[4 * e + 3][...] = _adamw_math(w_refs[e][...], g, m_refs[e][...], v_refs[e][...])
        outs[4 * n][...] = total[offsets[n]:offsets[n] + 1, 0:128]

    shapes = []
    for w, _, _, _ in entries:
        shapes += [jax.ShapeDtypeStruct(w.shape, F32)] * 4
    shapes.append(jax.ShapeDtypeStruct((1, 128), F32))
    return pl.pallas_call(
        body, name="adamw_small", in_specs=[VMEM_SPEC] * (1 + 3 * n), out_specs=[VMEM_SPEC] * len(shapes), out_shape=shapes,
        compiler_params=pltpu.CompilerParams(vmem_limit_bytes=VMEM_LIMIT),
    )(gathered, *[e[0] for e in entries], *[e[1] for e in entries], *[e[2] for e in entries])


def _ffn_forward(tag, h, gain, w_up, w_down, conv_w, conv_b):
    s, d = h.shape
    fb = w_up.shape[2]
    tm = _row_tile(s, MM_ROWS)
    a, = _rmsnorm_cast(f"ffn_norm_{tag}", h, [gain])
    u = _matmul(
        f"ffn_up_{tag}", a, w_up, dims=NN, grid=(s // tm, N_DEV, 1),
        a_spec=pl.BlockSpec((tm, d), lambda i, j, k: (i, 0)),
        b_spec=pl.BlockSpec((None, d, fb), lambda i, j, k: (j, 0, 0)),
        o_spec=pl.BlockSpec((None, None, tm, fb), lambda i, j, k: (j // 4, j % 4, i, 0)),
        out_shape=jax.ShapeDtypeStruct((2, 4, s, fb), BF16), acc_shape=(8, 128))
    hidden = _ffn_hidden(u, conv_w, conv_b)
    out = _matmul(
        f"ffn_down_{tag}", hidden, w_down, dims=NN, grid=(s // tm, 1, 4),
        a_spec=pl.BlockSpec((None, tm, fb), lambda i, j, k: (k, i, 0)),
        b_spec=pl.BlockSpec((None, fb, d), lambda i, j, k: (k, 0, 0)),
        o_spec=pl.BlockSpec((tm, d), lambda i, j, k: (i, 0)),
        out_shape=jax.ShapeDtypeStruct((s, d), F32), acc_shape=(tm, d),
        add=h, add_spec=pl.BlockSpec((tm, d), lambda i, j, k: (i, 0)))
    return out, (a, u, hidden)


def _ffn_backward(tag, h, gain, w_up, w_down, conv_w, conv_b, saved, dout):
    a, u, hidden = saved
    s, d = h.shape
    fb = w_up.shape[2]
    tm = _row_tile(s, MM_ROWS)
    dhidden = _matmul(
        f"ffn_down_bwd_{tag}", dout, w_down, dims=NT, grid=(s // tm, 4, 1),
        a_spec=pl.BlockSpec((tm, d), lambda i, j, k: (i, 0)),
        b_spec=pl.BlockSpec((None, fb, d), lambda i, j, k: (j, 0, 0)),
        o_spec=pl.BlockSpec((None, tm, fb), lambda i, j, k: (j, i, 0)),
        out_shape=jax.ShapeDtypeStruct((4, s, fb), BF16), acc_shape=(8, 128))
    dw_down = _matmul(
        f"ffn_down_grad_{tag}", hidden, dout, dims=TN, grid=(4, 1, s // tm),
        a_spec=pl.BlockSpec((None, tm, fb), lambda i, j, k: (i, k, 0)),
        b_spec=pl.BlockSpec((tm, d), lambda i, j, k: (k, 0)),
        o_spec=pl.BlockSpec((None, fb, d), lambda i, j, k: (i, 0, 0)),
        out_shape=jax.ShapeDtypeStruct((4, fb, d), BF16), acc_shape=(fb, d))
    du, dconv_w, dconv_b = _ffn_hidden_bwd(u, dhidden, conv_w, conv_b)
    da = _matmul(
        f"ffn_up_bwd_{tag}", du, w_up, dims=NT, grid=(s // tm, 1, N_DEV),
        a_spec=pl.BlockSpec((None, None, tm, fb), lambda i, j, k: (k // 4, k % 4, i, 0)),
        b_spec=pl.BlockSpec((None, d, fb), lambda i, j, k: (k, 0, 0)),
        o_spec=pl.BlockSpec((tm, d), lambda i, j, k: (i, 0)),
        out_shape=jax.ShapeDtypeStruct((s, d), F32), acc_shape=(tm, d))
    dw_up = _matmul(
        f"ffn_up_grad_{tag}", a, du, dims=TN, grid=(1, N_DEV, s // tm),
        a_spec=pl.BlockSpec((tm, d), lambda i, j, k: (k, 0)),
        b_spec=pl.BlockSpec((None, None, tm, fb), lambda i, j, k: (j // 4, j % 4, k, 0)),
        o_spec=pl.BlockSpec((None, d, fb), lambda i, j, k: (j, 0, 0)),
        out_shape=jax.ShapeDtypeStruct((N_DEV, d, fb), BF16), acc_shape=(d, fb))
    dh, (dgain,) = _rmsnorm_bwd(f"ffn_norm_bwd_{tag}", h, dout, [(da, gain)])
    return dh, dgain, dw_up, dw_down, dconv_w, dconv_b


def kernel(x, hg_norm, hg_w_in, hg_lb_logits, hg_out_norm, hg_w_out, kv_norm, w_kv, attn_norm, attn_w_q, attn_sinks, attn_w_o, ffn_norm, ffn_w_up, ffn_conv_w, ffn_conv_b, ffn_w_down, final_norm, loss_target, m_hg_norm, m_hg_w_in, m_hg_lb_logits, m_hg_out_norm, m_hg_w_out, m_kv_norm, m_w_kv, m_attn_norm, m_attn_w_q, m_attn_sinks, m_attn_w_o, m_ffn_norm, m_ffn_w_up, m_ffn_conv_w, m_ffn_conv_b, m_ffn_w_down, m_final_norm, v_hg_norm, v_hg_w_in, v_hg_lb_logits, v_hg_out_norm, v_hg_w_out, v_kv_norm, v_w_kv, v_attn_norm, v_attn_w_q, v_attn_sinks, v_attn_w_o, v_ffn_norm, v_ffn_w_up, v_ffn_conv_w, v_ffn_conv_b, v_ffn_w_down, v_final_norm):
    _, s, d = x.shape
    x0, target = x[0], loss_target[0]
    half = hg_w_in.shape[2]
    fs = ffn_conv_w.shape[2]
    fb = 2 * fs
    kvd = w_kv.shape[1]
    nq = d // ATT_HEAD_DIM
    tm = _row_tile(s, MM_ROWS)

    mine = _index(_place())
    gather = lambda tag, shards: _send_start("gather_start_" + tag, shards, [_landing(a, mine) for a in shards], False)
    coming_hg = gather("hg", [hg_w_in[0].astype(BF16), hg_norm, hg_lb_logits])
    coming_out = gather("hg_out", [hg_w_out[0].astype(BF16)])
    coming_ffn0 = gather("ffn0", [ffn_w_up[0].astype(BF16), ffn_conv_w, ffn_w_down[0].astype(BF16)])
    coming_attn = gather("attn", [w_kv.astype(BF16), attn_w_q[0].astype(BF16), attn_w_o[0].astype(BF16)])
    coming_ffn1 = gather("ffn1", [ffn_w_up[1].astype(BF16), ffn_w_down[1].astype(BF16)])
    all_started = coming_out[4] + coming_ffn0[4] + coming_attn[4] + coming_ffn1[4]
    w_in, g_hgn, g_lbl = _send_wait("gather_wait_hg", coming_hg, all_started, False)
    hgn = g_hgn.reshape(1, d)
    lbl = g_lbl.transpose(1, 0, 2).reshape(2, d)
    conv_b = [ffn_conv_b[layer].reshape(4, 1, fb) for layer in range(2)]
    gains = [ffn_norm[0:1], ffn_norm[1:2]]
    kvn, fin = kv_norm.reshape(1, d), final_norm.reshape(1, d)

    a0, = _rmsnorm_cast("hg_norm", x0, [hgn])
    p = _matmul(
        "hg_in", a0, w_in, dims=NN, grid=(s // tm, N_DEV, 1),
        a_spec=pl.BlockSpec((tm, d), lambda i, j, k: (i, 0)),
        b_spec=pl.BlockSpec((None, d, half), lambda i, j, k: (j, 0, 0)),
        o_spec=pl.BlockSpec((None, tm, half), lambda i, j, k: (j // 2, i, j % 2)),
        out_shape=jax.ShapeDtypeStruct((4, s, d), BF16), acc_shape=(8, 128))
    o, og, states = _hgrn2_fwd(p, lbl, hg_out_norm)
    w_out = _send_wait("gather_wait_hg_out", coming_out, og, False)[0].reshape(d, d)
    x1 = _mm_rows("hg_out", og, w_out, out_dtype=F32, add=x0)
    w_up0, g_cw, w_dn0 = _send_wait("gather_wait_ffn0", coming_ffn0, x1, False)
    w_up, w_dn = [w_up0, None], [w_dn0.reshape(4, fb, d), None]
    conv_w = [g_cw[:, layer].reshape(4, 2, CONV_WIDTH, fs).transpose(0, 2, 1, 3).reshape(4, CONV_WIDTH, fb) for layer in range(2)]
    x2, saved0 = _ffn_forward("0", x1, gains[0], w_up[0], w_dn[0], conv_w[0], conv_b[0])
    w_kvg, w_q, w_o = _send_wait("gather_wait_attn", coming_attn, x2, False)
    w_kvg, w_q, w_o = w_kvg.reshape(d, kvd), w_q.reshape(d, d), w_o.reshape(d, d)
    akv, a2 = _rmsnorm_cast("attn_norms", x2, [kvn, attn_norm])
    kv = _mm_rows("kv_proj", akv, w_kvg, out_dtype=BF16)
    q = _mm_rows("q_proj", a2, w_q, out_dtype=BF16)
    att = _attn_fwd(q, kv, attn_sinks)
    x3 = _mm_rows("attn_out", att, w_o, out_dtype=F32, add=x2)
    w_up[1], w_dn1 = _send_wait("gather_wait_ffn1", coming_ffn1, x3, False)
    w_dn[1] = w_dn1.reshape(4, fb, d)
    x4, saved1 = _ffn_forward("1", x3, gains[1], w_up[1], w_dn[1], conv_w[1], conv_b[1])
    dx4, d_fin, loss_part = _loss_head(x4, fin, target)

    dx3, d_fn1, dw_up1, dw_dn1, dcw1, dcb1 = _ffn_backward("1", x3, gains[1], w_up[1], w_dn[1], conv_w[1], conv_b[1], saved1, dx4)
    rows = d // N_DEV
    scatter = lambda tag, stacks: _send_start("scatter_start_" + tag, stacks, [_landing(lax.dynamic_index_in_dim(a, mine, keepdims=False), mine) for a in stacks], True)
    going_ffn1 = scatter("ffn1", [dw_up1, dw_dn1.reshape(N_DEV, fs, d)])
    datt = _mm_rows_nt("attn_out_bwd", dx3, w_o, out_dtype=BF16)
    dw_o = _mm_tn("attn_out_grad", att, dx3)
    dq, dkv_own, dkv_before, dsink = _attn_bwd(q, kv, att, datt, _pinned(attn_sinks, going_ffn1[4]))
    tiles = dkv_before.shape[0]
    dkv = dkv_own.reshape(tiles, s // tiles, kvd)
    dkv = jnp.concatenate([dkv[:, :-WINDOW], dkv[:, -WINDOW:] + jnp.pad(dkv_before[1:], ((0, 1), (0, 0), (0, 0)))], axis=1).reshape(s, kvd)
    da2 = _mm_rows_nt("q_proj_bwd", dq, w_q, out_dtype=F32)
    dw_q = _mm_tn("q_proj_grad", a2, dq)
    dakv = _mm_rows_nt("kv_proj_bwd", dkv, w_kvg, out_dtype=F32)
    dw_kv = _mm_tn("kv_proj_grad", akv, dkv)
    going_attn = scatter("attn", [dw_kv.reshape(N_DEV, rows, kvd), dw_q.reshape(N_DEV, rows, d), dw_o.reshape(N_DEV, rows, d)])
    dx2, (d_kvn, d_attn) = _rmsnorm_bwd("attn_norms_bwd", x2, dx3, [(dakv, _pinned(kvn, going_attn[4])), (da2, attn_norm)])
    dx1, d_fn0, dw_up0, dw_dn0, dcw0, dcb0 = _ffn_backward("0", x1, gains[0], w_up[0], w_dn[0], conv_w[0], conv_b[0], saved0, dx2)
    going_ffn0 = scatter("ffn0", [dw_up0, dw_dn0.reshape(N_DEV, fs, d)])
    dog = _mm_rows_nt("hg_out_bwd", dx1, w_out, out_dtype=F32)
    dw_out = _mm_tn("hg_out_grad", og, dx1)
    dp, d_lbl, d_ogain = _hgrn2_bwd(p, lbl, _pinned(hg_out_norm, going_ffn0[4]), o, dog, states)
    dw_in = _matmul(
        "hg_in_grad", a0, dp, dims=TN, grid=(1, N_DEV, s // tm),
        a_spec=pl.BlockSpec((tm, d), lambda i, j, k: (k, 0)),
        b_spec=pl.BlockSpec((None, tm, half), lambda i, j, k: (j // 2, k, j % 2)),
        o_spec=pl.BlockSpec((None, d, half), lambda i, j, k: (j, 0, 0)),
        out_shape=jax.ShapeDtypeStruct((N_DEV, d, half), BF16), acc_shape=(d, half))
    going_hg = scatter("hg", [dw_in, dw_out.reshape(N_DEV, rows, d)])
    da0 = _matmul(
        "hg_in_bwd", dp, w_in, dims=NT, grid=(s // tm, 1, N_DEV),
        a_spec=pl.BlockSpec((None, tm, half), lambda i, j, k: (k // 2, i, k % 2)),
        b_spec=pl.BlockSpec((None, d, half), lambda i, j, k: (k, 0, 0)),
        o_spec=pl.BlockSpec((tm, d), lambda i, j, k: (i, 0)),
        out_shape=jax.ShapeDtypeStruct((s, d), F32), acc_shape=(tm, d))
    dx0, (d_hgn,) = _rmsnorm_bwd("hg_norm_bwd", x0, dx1, [(da0, _pinned(hgn, going_hg[4]))])

    arrive = lambda tag, going: _send_wait("scatter_wait_" + tag, going, dx0, True)
    (l_up1, l_dn1), (l_kv, l_q, l_o), (l_up0, l_dn0), (l_in, l_out) = (
        arrive("ffn1", going_ffn1), arrive("attn", going_attn), arrive("ffn0", going_ffn0), arrive("hg", going_hg))
    landed = [l_in, l_out, l_kv, l_q, l_o, l_up0, l_up1, l_dn0, l_dn1]
    big = {}
    for tag, w, m, v, part in [
            ("hg_w_in", hg_w_in[0], m_hg_w_in[0], v_hg_w_in[0], landed[0]), ("hg_w_out", hg_w_out[0], m_hg_w_out[0], v_hg_w_out[0], landed[1]),
            ("w_kv", w_kv, m_w_kv, v_w_kv, landed[2]), ("attn_w_q", attn_w_q[0], m_attn_w_q[0], v_attn_w_q[0], landed[3]),
            ("attn_w_o", attn_w_o[0], m_attn_w_o[0], v_attn_w_o[0], landed[4]),
            ("ffn_w_up0", ffn_w_up[0], m_ffn_w_up[0], v_ffn_w_up[0], landed[5]), ("ffn_w_up1", ffn_w_up[1], m_ffn_w_up[1], v_ffn_w_up[1], landed[6]),
            ("ffn_w_down0", ffn_w_down[0], m_ffn_w_down[0], v_ffn_w_down[0], landed[7]),
            ("ffn_w_down1", ffn_w_down[1], m_ffn_w_down[1], v_ffn_w_down[1], landed[8])]:
        big[tag] = _adamw_shard("adamw_" + tag, w, m, v, part)
    lead = lambda tag: [a[None] for a in big[tag]]
    pair = lambda tag: [jnp.stack([a, b]) for a, b in zip(big[tag + "0"], big[tag + "1"])]

    as_blocks = lambda a, r: a.reshape(r, N_DEV, -1).transpose(1, 0, 2).reshape(N_DEV * r, -1)
    d_cw = jnp.concatenate([g.transpose(1, 0, 2).reshape(CONV_WIDTH, 4 * fb) for g in (dcw0, dcw1)], axis=0)
    parts = [d_fin, jnp.concatenate([d_fn0, d_fn1], axis=0), jnp.concatenate([dcb0.reshape(1, 4 * fb), dcb1.reshape(1, 4 * fb)], axis=0),
             as_blocks(d_cw, 2 * CONV_WIDTH), d_attn, jnp.sum(dsink[:, :, 0], axis=0).reshape(1, nq), d_kvn, d_ogain,
             as_blocks(d_hgn, 1), as_blocks(d_lbl, 2), loss_part]
    offsets, _, _ = _pack_rows(parts)
    gathered, = _all_gather("gather_small_grads", [_pack(parts)], VMEM_SPEC)
    two = lambda a: a.reshape(-1, a.shape[-1])
    small = [(fin, m_final_norm.reshape(1, d), v_final_norm.reshape(1, d), False), (ffn_norm, m_ffn_norm, v_ffn_norm, False),
             (ffn_conv_b, m_ffn_conv_b, v_ffn_conv_b, False), (two(ffn_conv_w), two(m_ffn_conv_w), two(v_ffn_conv_w), True),
             (attn_norm, m_attn_norm, v_attn_norm, False), (attn_sinks, m_attn_sinks, v_attn_sinks, False),
             (kvn, m_kv_norm.reshape(1, d), v_kv_norm.reshape(1, d), False), (hg_out_norm, m_hg_out_norm, v_hg_out_norm, False),
             (hg_norm, m_hg_norm, v_hg_norm, True), (hg_lb_logits, m_hg_lb_logits, v_hg_lb_logits, True)]
    res = _adamw_small(gathered, offsets, small)
    names = ["final_norm", "ffn_norm", "ffn_conv_b", "ffn_conv_w", "attn_norm", "attn_sinks", "kv_norm", "hg_out_norm", "hg_norm", "hg_lb_logits"]
    shapes = {"final_norm": final_norm.shape, "kv_norm": kv_norm.shape, "ffn_conv_w": ffn_conv_w.shape}
    out = {n: [a.reshape(shapes[n]) if n in shapes else a for a in res[4 * i:4 * i + 4]] for i, n in enumerate(names)}
    out.update(hg_w_in=lead("hg_w_in"), hg_w_out=lead("hg_w_out"), w_kv=big["w_kv"], attn_w_q=lead("attn_w_q"), attn_w_o=lead("attn_w_o"),
               ffn_w_up=pair("ffn_w_up"), ffn_w_down=pair("ffn_w_down"))
    order = ["hg_norm", "hg_w_in", "hg_lb_logits", "hg_out_norm", "hg_w_out", "kv_norm", "w_kv", "attn_norm", "attn_w_q", "attn_sinks",
             "attn_w_o", "ffn_norm", "ffn_w_up", "ffn_conv_w", "ffn_conv_b", "ffn_w_down", "final_norm"]
    loss = res[-1][0, 0]
    return (loss, dx0[None], *[out[n][0] for n in order], *[out[n][1] for n in order], *[out[n][2] for n in order], *[out[n][3] for n in order])
```

```python
import functools
import math

import jax
import jax.numpy as jnp
from jax import lax
from jax.experimental import pallas as pl
from jax.experimental.pallas import tpu as pltpu

F32 = jnp.float32
BF16 = jnp.bfloat16

EPS = 1e-6
HG_EXPAND = 128
HG_CHUNK = 32
ATT_HEAD_DIM = 64
ATT_KV_HEADS = 2
WINDOW = 128
CONV_WIDTH = 3
ADAM_LR = 0.001
ADAM_B1 = 0.9
ADAM_B2 = 0.999
ADAM_EPS = 1e-08
ADAM_WD = 0.01
ADAM_STEP = 10

N_DEV = 8
VMEM_LIMIT = 48 * 1024 * 1024
NEG = -1e30

NN = (((1,), (0,)), ((), ()))
NT = (((1,), (1,)), ((), ()))
TN = (((0,), (0,)), ((), ()))
MESH = pl.DeviceIdType.MESH


def _dot(a, b, dims=NN):
    return lax.dot_general(a.astype(BF16), b.astype(BF16), dims, preferred_element_type=F32)


def _sigmoid(x):
    return 1.0 / (1.0 + jnp.exp(-x))


def _silu(x):
    return x * _sigmoid(x)


def _dsilu(x):
    s = _sigmoid(x)
    return s * (1.0 + x * (1.0 - s))


def _params(semantics):
    return pltpu.CompilerParams(dimension_semantics=semantics, vmem_limit_bytes=VMEM_LIMIT)


def _row_tile(rows, want=512):
    return min(rows, want)


MM_ROWS = 1024


def _matmul(name, a, b, *, dims, grid, a_spec, b_spec, o_spec, out_shape, acc_shape, add=None, add_spec=None):
    nk = grid[2]

    def body(*refs):
        if add is None:
            a_ref, b_ref, o_ref, acc = refs
        else:
            a_ref, b_ref, add_ref, o_ref, acc = refs
        k = pl.program_id(2)
        part = _dot(a_ref[...], b_ref[...], dims)

        def finish(total):
            if add is not None:
                total = total + add_ref[...]
            o_ref[...] = total.astype(o_ref.dtype)

        if nk == 1:
            finish(part)
        else:
            @pl.when(k == 0)
            def _():
                acc[...] = part

            @pl.when(k > 0)
            def _():
                acc[...] += part

            @pl.when(k == nk - 1)
            def _():
                finish(acc[...])

    in_specs = [a_spec, b_spec] + ([] if add is None else [add_spec])
    args = (a, b) + (() if add is None else (add,))
    return pl.pallas_call(
        body, name=name, grid=grid, in_specs=in_specs, out_specs=o_spec, out_shape=out_shape,
        scratch_shapes=[pltpu.VMEM(acc_shape, F32)],
        compiler_params=_params(("parallel", "parallel", "arbitrary")),
    )(*args)


def _mm_rows(name, a, w, *, out_dtype, add=None):
    s, kdim = a.shape
    n = w.shape[1]
    tm = _row_tile(s, MM_ROWS)
    return _matmul(
        name, a, w, dims=NN, grid=(s // tm, 1, 1),
        a_spec=pl.BlockSpec((tm, kdim), lambda i, j, k: (i, 0)),
        b_spec=pl.BlockSpec((kdim, n), lambda i, j, k: (0, 0)),
        o_spec=pl.BlockSpec((tm, n), lambda i, j, k: (i, 0)),
        out_shape=jax.ShapeDtypeStruct((s, n), out_dtype), acc_shape=(8, 128),
        add=add, add_spec=None if add is None else pl.BlockSpec((tm, n), lambda i, j, k: (i, 0)),
    )


def _mm_rows_nt(name, a, w, *, out_dtype):
    s, n = a.shape
    kdim = w.shape[0]
    tm = _row_tile(s, MM_ROWS)
    return _matmul(
        name, a, w, dims=NT, grid=(s // tm, 1, 1),
        a_spec=pl.BlockSpec((tm, n), lambda i, j, k: (i, 0)),
        b_spec=pl.BlockSpec((kdim, n), lambda i, j, k: (0, 0)),
        o_spec=pl.BlockSpec((tm, kdim), lambda i, j, k: (i, 0)),
        out_shape=jax.ShapeDtypeStruct((s, kdim), out_dtype), acc_shape=(8, 128),
    )


def _mm_tn(name, a, g):
    s, m = a.shape
    n = g.shape[1]
    ts = _row_tile(s, MM_ROWS)
    return _matmul(
        name, a, g, dims=TN, grid=(1, 1, s // ts),
        a_spec=pl.BlockSpec((ts, m), lambda i, j, k: (k, 0)),
        b_spec=pl.BlockSpec((ts, n), lambda i, j, k: (k, 0)),
        o_spec=pl.BlockSpec((m, n), lambda i, j, k: (0, 0)),
        out_shape=jax.ShapeDtypeStruct((m, n), BF16), acc_shape=(m, n),
    )


def _rmsnorm_cast(name, h, gains):
    s, d = h.shape
    tm = _row_tile(s)
    n = len(gains)

    def body(*refs):
        h_ref, g_refs, o_refs = refs[0], refs[1:1 + n], refs[1 + n:]
        xv = h_ref[...]
        xhat = xv * lax.rsqrt(jnp.mean(xv * xv, axis=-1, keepdims=True) + EPS)
        for g_ref, o_ref in zip(g_refs, o_refs):
            o_ref[...] = (xhat * g_ref[...]).astype(BF16)

    row = pl.BlockSpec((tm, d), lambda i: (i, 0))
    vec = pl.BlockSpec((1, d), lambda i: (0, 0))
    return pl.pallas_call(
        body, name=name, grid=(s // tm,), in_specs=[row] + [vec] * n, out_specs=[row] * n,
        out_shape=[jax.ShapeDtypeStruct((s, d), BF16)] * n, compiler_params=_params(("parallel",)),
    )(h, *gains)


def _rmsnorm_bwd(name, h, dres, branches):
    s, d = h.shape
    tm = _row_tile(s)
    n = len(branches)

    def body(*refs):
        h_ref, dres_ref = refs[0], refs[1]
        da_refs, g_refs = refs[2:2 + n], refs[2 + n:2 + 2 * n]
        dh_ref, dg_refs = refs[2 + 2 * n], refs[3 + 2 * n:]
        i = pl.program_id(0)
        xv = h_ref[...]
        r = lax.rsqrt(jnp.mean(xv * xv, axis=-1, keepdims=True) + EPS)
        xhat = xv * r
        total = dres_ref[...]
        for da_ref, g_ref, dg_ref in zip(da_refs, g_refs, dg_refs):
            da = da_ref[...]
            dgain = jnp.sum(da * xhat, axis=0, keepdims=True)

            @pl.when(i == 0)
            def _():
                dg_ref[...] = dgain

            @pl.when(i > 0)
            def _():
                dg_ref[...] += dgain

            dxhat = da * g_ref[...]
            total = total + r * (dxhat - xhat * jnp.mean(dxhat * xhat, axis=-1, keepdims=True))
        dh_ref[...] = total

    row = pl.BlockSpec((tm, d), lambda i: (i, 0))
    vec = pl.BlockSpec((1, d), lambda i: (0, 0))
    outs = pl.pallas_call(
        body, name=name, grid=(s // tm,), in_specs=[row, row] + [row] * n + [vec] * n, out_specs=[row] + [vec] * n,
        out_shape=[jax.ShapeDtypeStruct((s, d), F32)] + [jax.ShapeDtypeStruct((1, d), F32)] * n,
        compiler_params=_params(("arbitrary",)),
    )(h, dres, *[b[0] for b in branches], *[b[1] for b in branches])
    return outs[0], outs[1:]


def _loss_head(h, gain, target):
    s, d = h.shape
    tm = _row_tile(s)

    def body(h_ref, g_ref, t_ref, dh_ref, dg_ref, loss_ref):
        i = pl.program_id(0)
        xv = h_ref[...]
        r = lax.rsqrt(jnp.mean(xv * xv, axis=-1, keepdims=True) + EPS)
        xhat = xv * r
        err = xhat * g_ref[...] - t_ref[...]
        dy = err * (1.0 / d)
        part = jnp.zeros((1, 128), F32) + 0.5 * jnp.sum(jnp.mean(err * err, axis=-1, keepdims=True))
        dgain = jnp.sum(dy * xhat, axis=0, keepdims=True)

        @pl.when(i == 0)
        def _():
            dg_ref[...] = dgain
            loss_ref[...] = part

        @pl.when(i > 0)
        def _():
            dg_ref[...] += dgain
            loss_ref[...] += part

        dxhat = dy * g_ref[...]
        dh_ref[...] = r * (dxhat - xhat * jnp.mean(dxhat * xhat, axis=-1, keepdims=True))

    row = pl.BlockSpec((tm, d), lambda i: (i, 0))
    vec = pl.BlockSpec((1, d), lambda i: (0, 0))
    return pl.pallas_call(
        body, name="loss_head", grid=(s // tm,), in_specs=[row, vec, row],
        out_specs=[row, vec, pl.BlockSpec((1, 128), lambda i: (0, 0))],
        out_shape=[jax.ShapeDtypeStruct((s, d), F32), jax.ShapeDtypeStruct((1, d), F32), jax.ShapeDtypeStruct((1, 128), F32)],
        compiler_params=_params(("arbitrary",)),
    )(h, gain, target)


def _bdot(a, b, ca, cb):
    return lax.dot_general(a.astype(BF16), b.astype(BF16), (((ca,), (cb,)), ((0,), (0,))), preferred_element_type=F32)


def _chunk_cumsum(xv, reverse=False):
    n = xv.shape[0]
    row = lax.broadcasted_iota(jnp.int32, xv.shape, 0) % HG_CHUNK
    step = 1
    while step < HG_CHUNK:
        if reverse:
            xv = xv + jnp.where(row < HG_CHUNK - step, pltpu.roll(xv, n - step, axis=0), 0.0)
        else:
            xv = xv + jnp.where(row >= step, pltpu.roll(xv, step, axis=0), 0.0)
        step *= 2
    return xv


def _hg_terms(p_ref, lbl_ref):
    pq = p_ref[0].astype(F32)
    pf = p_ref[1].astype(F32)
    lb = _sigmoid(lbl_ref[0:1, :] - lbl_ref[1:2, :])
    sig = _sigmoid(pf)
    fg = lb + (1.0 - lb) * sig
    nc = pq.shape[0] // HG_CHUNK
    chunks = lambda a: a.reshape(nc, HG_CHUNK, HG_EXPAND)
    q = chunks(_silu(pq) * HG_EXPAND ** -0.5)
    k = chunks(1.0 - fg)
    v = chunks(p_ref[2].astype(F32))
    g = chunks(_chunk_cumsum(jnp.log(fg)))
    gm = g[:, HG_CHUNK // 2 - 1:HG_CHUNK // 2, :]
    gl = g[:, HG_CHUNK - 1:HG_CHUNK, :]
    e_mid, e_inv, e_all, e_end = jnp.exp(g - gm), jnp.exp(gm - g), jnp.exp(g), jnp.exp(gl - g)
    terms = dict(q=q, k=k, v=v, qd=q * e_all, qt=q * e_mid, kt=k * e_inv, kd=k * e_end, e_last=jnp.exp(gl),
                 e_mid=e_mid, e_inv=e_inv, e_all=e_all, e_end=e_end)
    return terms, (pq, sig, fg, lb)


def _causal(nc):
    r = lax.broadcasted_iota(jnp.int32, (nc, HG_CHUNK, HG_CHUNK), 1)
    c = lax.broadcasted_iota(jnp.int32, (nc, HG_CHUNK, HG_CHUNK), 2)
    return r >= c


def _hgrn2_fwd(p, lb_logits, out_gain):
    _, s, d = p.shape
    heads = d // HG_EXPAND
    t = _row_tile(s)
    nc = t // HG_CHUNK

    def body(p_ref, lbl_ref, gain_ref, o_ref, og_ref, st_ref, state, decay):
        @pl.when(pl.program_id(1) == 0)
        def _():
            state[...] = jnp.zeros_like(state)

        tm, _ = _hg_terms(p_ref, lbl_ref)
        decay[...] = tm["e_last"]
        st_ref[...] = _bdot(tm["v"], tm["kd"], 1, 1)

        def chunk(c, carry):
            add = st_ref[c]
            st = state[...]
            st_ref[c] = st
            state[...] = st * decay[c] + add
            return carry

        lax.fori_loop(0, nc, chunk, 0)
        a = jnp.where(_causal(nc), _bdot(tm["qt"], tm["kt"], 2, 2), 0.0)
        ov = (_bdot(tm["qd"], st_ref[...], 2, 2) + _bdot(a, tm["v"], 2, 1)).reshape(t, HG_EXPAND)
        o_ref[...] = ov
        on = ov * lax.rsqrt(jnp.mean(ov * ov, axis=-1, keepdims=True) + EPS) * gain_ref[...]
        og_ref[...] = (on * _silu(p_ref[3].astype(F32))).astype(BF16)

    blk = pl.BlockSpec((t, HG_EXPAND), lambda h, b: (b, h))
    return pl.pallas_call(
        body, name="hgrn2_fwd", grid=(heads, s // t),
        in_specs=[pl.BlockSpec((4, t, HG_EXPAND), lambda h, b: (0, b, h)), pl.BlockSpec((2, HG_EXPAND), lambda h, b: (0, h)),
                  pl.BlockSpec((1, HG_EXPAND), lambda h, b: (0, 0))],
        out_specs=[blk, blk, pl.BlockSpec((None, nc, HG_EXPAND, HG_EXPAND), lambda h, b: (h, b, 0, 0))],
        out_shape=[jax.ShapeDtypeStruct((s, d), F32), jax.ShapeDtypeStruct((s, d), BF16),
                   jax.ShapeDtypeStruct((heads, s // HG_CHUNK, HG_EXPAND, HG_EXPAND), F32)],
        scratch_shapes=[pltpu.VMEM((HG_EXPAND, HG_EXPAND), F32), pltpu.VMEM((nc, 1, HG_EXPAND), F32)],
        compiler_params=_params(("parallel", "arbitrary")),
    )(p, lb_logits, out_gain)


def _hgrn2_bwd(p, lb_logits, out_gain, o, dog, states):
    _, s, d = p.shape
    heads = d // HG_EXPAND
    t = _row_tile(s)
    nc = t // HG_CHUNK
    nb = s // t

    def body(p_ref, lbl_ref, gain_ref, o_ref, dog_ref, st_ref, dp_ref, dlbl_ref, dgain_ref, dstate, decay, dst_s):
        h, b = pl.program_id(0), pl.program_id(1)

        @pl.when(b == 0)
        def _():
            dstate[...] = jnp.zeros_like(dstate)

        tm, (pq, sig, fg, lb) = _hg_terms(p_ref, lbl_ref)
        pg = p_ref[3].astype(F32)
        ov = o_ref[...]
        r = lax.rsqrt(jnp.mean(ov * ov, axis=-1, keepdims=True) + EPS)
        ohat = ov * r
        dogv = dog_ref[...]
        d_on = dogv * _silu(pg)
        dp_ref[3] = (dogv * ohat * gain_ref[...] * _dsilu(pg)).astype(BF16)
        dgain = jnp.sum(d_on * ohat, axis=0, keepdims=True)

        @pl.when((h == 0) & (b == 0))
        def _():
            dgain_ref[...] = dgain

        @pl.when((h > 0) | (b > 0))
        def _():
            dgain_ref[...] += dgain

        dohat = d_on * gain_ref[...]
        do = (r * (dohat - ohat * jnp.mean(dohat * ohat, axis=-1, keepdims=True))).reshape(nc, HG_CHUNK, HG_EXPAND)

        decay[...] = tm["e_last"]
        dst_s[...] = _bdot(do, tm["qd"], 1, 1)

        def chunk(i, carry):
            c = nc - 1 - i
            add = dst_s[c]
            dst = dstate[...]
            dst_s[c] = dst
            dstate[...] = dst * decay[c] + add
            return carry

        lax.fori_loop(0, nc, chunk, 0)
        st, dst = st_ref[...], dst_s[...]
        causal = _causal(nc)
        a = jnp.where(causal, _bdot(tm["qt"], tm["kt"], 2, 2), 0.0)
        da = jnp.where(causal, _bdot(do, tm["v"], 2, 2), 0.0)
        dqt = _bdot(da, tm["kt"], 2, 1)
        dkt = _bdot(da, tm["qt"], 1, 1)
        dqd = _bdot(do, st, 2, 1)
        dkd = _bdot(tm["v"], dst, 2, 1)
        dv = _bdot(a, do, 1, 1) + _bdot(tm["kd"], dst, 2, 2)
        dq = dqt * tm["e_mid"] + dqd * tm["e_all"]
        dk = dkt * tm["e_inv"] + dkd * tm["e_end"]
        dg = dqt * tm["qt"] - dkt * tm["kt"] + dqd * tm["qd"] - dkd * tm["kd"]
        dgl = jnp.sum(dkd * tm["kd"], axis=1, keepdims=True) + tm["e_last"] * jnp.sum(dst * st, axis=1, keepdims=True)
        last_row = lax.broadcasted_iota(jnp.int32, (nc, HG_CHUNK, HG_EXPAND), 1) == HG_CHUNK - 1
        flat = lambda a3: a3.reshape(t, HG_EXPAND)
        dlf = _chunk_cumsum(flat(dg + jnp.where(last_row, dgl, 0.0)), reverse=True)
        dfg = dlf / fg - flat(dk)
        dlb = jnp.sum(dfg * (1.0 - sig), axis=0, keepdims=True)
        dl0 = dlb * lb * (1.0 - lb)
        dlbl = jnp.concatenate([dl0, -dl0], axis=0)

        @pl.when(b == 0)
        def _():
            dlbl_ref[...] = dlbl

        @pl.when(b > 0)
        def _():
            dlbl_ref[...] += dlbl

        dp_ref[0] = (flat(dq) * HG_EXPAND ** -0.5 * _dsilu(pq)).astype(BF16)
        dp_ref[1] = (dfg * (1.0 - lb) * sig * (1.0 - sig)).astype(BF16)
        dp_ref[2] = flat(dv).astype(BF16)

    blk = pl.BlockSpec((t, HG_EXPAND), lambda h, b: (nb - 1 - b, h))
    pblk = pl.BlockSpec((4, t, HG_EXPAND), lambda h, b: (0, nb - 1 - b, h))
    return pl.pallas_call(
        body, name="hgrn2_bwd", grid=(heads, nb),
        in_specs=[pblk, pl.BlockSpec((2, HG_EXPAND), lambda h, b: (0, h)), pl.BlockSpec((1, HG_EXPAND), lambda h, b: (0, 0)),
                  blk, blk, pl.BlockSpec((None, nc, HG_EXPAND, HG_EXPAND), lambda h, b: (h, nb - 1 - b, 0, 0))],
        out_specs=[pblk, pl.BlockSpec((2, HG_EXPAND), lambda h, b: (0, h)), pl.BlockSpec((1, HG_EXPAND), lambda h, b: (0, 0))],
        out_shape=[jax.ShapeDtypeStruct((4, s, d), BF16), jax.ShapeDtypeStruct((2, d), F32), jax.ShapeDtypeStruct((1, HG_EXPAND), F32)],
        scratch_shapes=[pltpu.VMEM((HG_EXPAND, HG_EXPAND), F32), pltpu.VMEM((nc, 1, HG_EXPAND), F32),
                        pltpu.VMEM((nc, HG_EXPAND, HG_EXPAND), F32)],
        compiler_params=_params(("arbitrary", "arbitrary")),
    )(p, lb_logits, out_gain, o, dog, states)


HALO = 8


def _shift_down(xv, n):
    return pltpu.roll(xv, n, axis=0)


def _shift_up(xv, n):
    return pltpu.roll(xv, xv.shape[0] - n, axis=0)


def _ffn_hidden(u, conv_w, conv_b):
    _, nj, s, fb = u.shape
    tm = _row_tile(s)
    per = tm // HALO

    def body(gate_ref, prev_ref, val_ref, w_ref, b_ref, h_ref):
        i = pl.program_id(1)
        prev = jnp.where(i > 0, prev_ref[...].astype(F32), 0.0)
        ext = jnp.concatenate([prev, gate_ref[...].astype(F32)], axis=0)
        conv = b_ref[...] + w_ref[2:3, :] * ext[HALO:]
        conv = conv + w_ref[1:2, :] * _shift_down(ext, 1)[HALO:]
        conv = conv + w_ref[0:1, :] * _shift_down(ext, 2)[HALO:]
        h_ref[...] = (_silu(conv) * val_ref[...].astype(F32)).astype(BF16)

    return pl.pallas_call(
        body, name="ffn_hidden", grid=(nj, s // tm),
        in_specs=[pl.BlockSpec((None, None, tm, fb), lambda j, i: (0, j, i, 0)),
                  pl.BlockSpec((None, None, HALO, fb), lambda j, i: (0, j, jnp.maximum(i * per - 1, 0), 0)),
                  pl.BlockSpec((None, None, tm, fb), lambda j, i: (1, j, i, 0)),
                  pl.BlockSpec((None, CONV_WIDTH, fb), lambda j, i: (j, 0, 0)),
                  pl.BlockSpec((None, 1, fb), lambda j, i: (j, 0, 0))],
        out_specs=pl.BlockSpec((None, tm, fb), lambda j, i: (j, i, 0)),
        out_shape=jax.ShapeDtypeStruct((nj, s, fb), BF16), compiler_params=_params(("parallel", "parallel")),
    )(u, u, u, conv_w, conv_b)


def _ffn_hidden_bwd(u, dh, conv_w, conv_b):
    _, nj, s, fb = u.shape
    tm = _row_tile(s)
    per = tm // HALO
    nblk = s // HALO
    ni = s // tm

    def body(gate_ref, gprev_ref, gnext_ref, val_ref, vnext_ref, dh_ref, dhnext_ref, w_ref, b_ref, du_ref, dw_ref, db_ref):
        i = pl.program_id(1)
        has_next = i < ni - 1
        gprev = jnp.where(i > 0, gprev_ref[...].astype(F32), 0.0)
        gext = jnp.concatenate([gprev, gate_ref[...].astype(F32), gnext_ref[...].astype(F32)], axis=0)
        vext = jnp.concatenate([val_ref[...].astype(F32), vnext_ref[...].astype(F32)], axis=0)
        dhext = jnp.concatenate([dh_ref[...].astype(F32), jnp.where(has_next, dhnext_ref[...].astype(F32), 0.0)], axis=0)
        g0 = gext[HALO:]
        g1 = _shift_down(gext, 1)[HALO:]
        g2 = _shift_down(gext, 2)[HALO:]
        conv = b_ref[...] + w_ref[2:3, :] * g0 + w_ref[1:2, :] * g1 + w_ref[0:1, :] * g2
        dconv = dhext * vext * _dsilu(conv)
        dgate = w_ref[2:3, :] * dconv + w_ref[1:2, :] * _shift_up(dconv, 1) + w_ref[0:1, :] * _shift_up(dconv, 2)
        du_ref[0] = dgate[:tm].astype(BF16)
        du_ref[1] = (dhext * _silu(conv))[:tm].astype(BF16)
        own = dconv[:tm]
        dw = jnp.concatenate([jnp.sum(own * g2[:tm], axis=0, keepdims=True), jnp.sum(own * g1[:tm], axis=0, keepdims=True),
                              jnp.sum(own * g0[:tm], axis=0, keepdims=True)], axis=0)
        db = jnp.sum(own, axis=0, keepdims=True)

        @pl.when(i == 0)
        def _():
            dw_ref[...] = dw
            db_ref[...] = db

        @pl.when(i > 0)
        def _():
            dw_ref[...] += dw
            db_ref[...] += db

    def tile(part):
        return pl.BlockSpec((None, None, tm, fb), lambda j, i: (part, j, i, 0))

    def after(part):
        return pl.BlockSpec((None, None, HALO, fb), lambda j, i: (part, j, jnp.minimum((i + 1) * per, nblk - 1), 0))

    return pl.pallas_call(
        body, name="ffn_hidden_bwd", grid=(nj, ni),
        in_specs=[tile(0), pl.BlockSpec((None, None, HALO, fb), lambda j, i: (0, j, jnp.maximum(i * per - 1, 0), 0)), after(0),
                  tile(1), after(1),
                  pl.BlockSpec((None, tm, fb), lambda j, i: (j, i, 0)),
                  pl.BlockSpec((None, HALO, fb), lambda j, i: (j, jnp.minimum((i + 1) * per, nblk - 1), 0)),
                  pl.BlockSpec((None, CONV_WIDTH, fb), lambda j, i: (j, 0, 0)), pl.BlockSpec((None, 1, fb), lambda j, i: (j, 0, 0))],
        out_specs=[pl.BlockSpec((2, None, tm, fb), lambda j, i: (0, j, i, 0)),
                   pl.BlockSpec((None, CONV_WIDTH, fb), lambda j, i: (j, 0, 0)), pl.BlockSpec((None, 1, fb), lambda j, i: (j, 0, 0))],
        out_shape=[jax.ShapeDtypeStruct((2, nj, s, fb), BF16), jax.ShapeDtypeStruct((nj, CONV_WIDTH, fb), F32),
                   jax.ShapeDtypeStruct((nj, 1, fb), F32)],
        compiler_params=_params(("parallel", "arbitrary")),
    )(u, u, u, u, u, dh, dh, conv_w, conv_b)


ATT_TILE = 512


def _attn_probs(q, kb, sink, head, first, n_heads):
    iq = lax.broadcasted_iota(jnp.int32, (WINDOW, 2 * WINDOW), 0)
    ik = lax.broadcasted_iota(jnp.int32, (WINDOW, 2 * WINDOW), 1)
    dist = iq + WINDOW - ik
    valid = (dist >= 0) & (dist < WINDOW) & (ik >= jnp.where(first, WINDOW, 0))
    slope = 2.0 ** (-8.0 * (head + 1) / n_heads)
    sc = jnp.where(valid, _dot(q, kb, NT) * ATT_HEAD_DIM ** -0.5 - slope * dist.astype(F32), NEG)
    m = jnp.maximum(jnp.max(sc, axis=-1, keepdims=True), sink)
    e = jnp.exp(sc - m)
    es = jnp.exp(sink - m)
    inv = 1.0 / (jnp.sum(e, axis=-1, keepdims=True) + es)
    return e * inv, es * inv


def _attn_specs(s, d, kvd, tq):
    per = tq // WINDOW
    return [pl.BlockSpec((tq, d), lambda i: (i, 0)), pl.BlockSpec((tq, kvd), lambda i: (i, 0)),
            pl.BlockSpec((WINDOW, kvd), lambda i: (jnp.maximum(i * per - 1, 0), 0))]


def _attn_fwd(q, kv, sinks):
    s, d = q.shape
    kvd = kv.shape[1]
    half = kvd // 2
    hd = ATT_HEAD_DIM
    nq = d // hd
    group = nq // ATT_KV_HEADS
    tq = min(s, ATT_TILE)
    per = tq // WINDOW

    def body(q_ref, kvc_ref, kvp_ref, sink_ref, o_ref, band):
        i = pl.program_id(0)
        band[0:WINDOW, :] = kvp_ref[...]
        band[WINDOW:, :] = kvc_ref[...]

        def block(b, carry):
            rows = pl.ds(pl.multiple_of(b * WINDOW, WINDOW), WINDOW)
            keys = pl.ds(pl.multiple_of(b * WINDOW, WINDOW), 2 * WINDOW)
            first = (i * per + b) == 0
            for pair in range(nq // 2):
                outs = []
                for h in (2 * pair, 2 * pair + 1):
                    g = h // group
                    p, _ = _attn_probs(q_ref[rows, h * hd:(h + 1) * hd], band[keys, g * hd:(g + 1) * hd], sink_ref[0, h], h, first, nq)
                    outs.append(_dot(p, band[keys, half + g * hd:half + (g + 1) * hd]))
                o_ref[rows, pair * 2 * hd:(pair + 1) * 2 * hd] = jnp.concatenate(outs, axis=1).astype(BF16)
            return carry

        lax.fori_loop(0, per, block, 0)

    return pl.pallas_call(
        body, name="attn_fwd", grid=(s // tq,),
        in_specs=_attn_specs(s, d, kvd, tq) + [pl.BlockSpec(memory_space=pltpu.SMEM)],
        out_specs=pl.BlockSpec((tq, d), lambda i: (i, 0)), out_shape=jax.ShapeDtypeStruct((s, d), BF16),
        scratch_shapes=[pltpu.VMEM((tq + WINDOW, kvd), BF16)], compiler_params=_params(("parallel",)),
    )(q, kv, kv, sinks)


def _attn_bwd(q, kv, o, do, sinks):
    s, d = q.shape
    kvd = kv.shape[1]
    half = kvd // 2
    hd = ATT_HEAD_DIM
    nq = d // hd
    group = nq // ATT_KV_HEADS
    tq = min(s, ATT_TILE)
    per = tq // WINDOW
    nt = s // tq

    def body(q_ref, kvc_ref, kvp_ref, o_ref, do_ref, sink_ref, dq_ref, dkvc_ref, dkvp_ref, ds_ref, band, dband):
        i = pl.program_id(0)
        band[0:WINDOW, :] = kvp_ref[...]
        band[WINDOW:, :] = kvc_ref[...]
        dband[...] = jnp.zeros_like(dband)
        ds_ref[...] = jnp.zeros_like(ds_ref)

        def block(b, carry):
            rows = pl.ds(pl.multiple_of(b * WINDOW, WINDOW), WINDOW)
            keys = pl.ds(pl.multiple_of(b * WINDOW, WINDOW), 2 * WINDOW)
            first = (i * per + b) == 0
            dks, dvs = [], []
            for g in range(ATT_KV_HEADS):
                kb = band[keys, g * hd:(g + 1) * hd]
                vb = band[keys, half + g * hd:half + (g + 1) * hd]
                dk = jnp.zeros((2 * WINDOW, hd), F32)
                dv = jnp.zeros((2 * WINDOW, hd), F32)
                dqs = []
                for j in range(group):
                    h = g * group + j
                    cols = slice(h * hd, (h + 1) * hd)
                    qv, dov = q_ref[rows, cols], do_ref[rows, cols]
                    p, ps = _attn_probs(qv, kb, sink_ref[0, h], h, first, nq)
                    dsum = jnp.sum(dov.astype(F32) * o_ref[rows, cols].astype(F32), axis=-1, keepdims=True)
                    dsc = p * (_dot(dov, vb, NT) - dsum) * ATT_HEAD_DIM ** -0.5
                    dqs.append(_dot(dsc, kb))
                    dk = dk + _dot(dsc, qv, TN)
                    dv = dv + _dot(p, dov, TN)
                    ds_ref[h:h + 1, :] += jnp.zeros((1, 128), F32) - jnp.sum(ps * dsum)
                    if j % 2 == 1:
                        dq_ref[rows, (h - 1) * hd:(h + 1) * hd] = jnp.concatenate(dqs[-2:], axis=1).astype(BF16)
                dks.append(dk)
                dvs.append(dv)
            dband[keys, 0:half] += jnp.concatenate(dks, axis=1)
            dband[keys, half:] += jnp.concatenate(dvs, axis=1)
            return carry

        lax.fori_loop(0, per, block, 0)
        dkvp_ref[...] = dband[0:WINDOW, :]
        dkvc_ref[...] = dband[WINDOW:, :]

    big = pl.BlockSpec((tq, d), lambda i: (i, 0))
    return pl.pallas_call(
        body, name="attn_bwd", grid=(nt,),
        in_specs=_attn_specs(s, d, kvd, tq) + [big, big, pl.BlockSpec(memory_space=pltpu.SMEM)],
        out_specs=[big, pl.BlockSpec((tq, kvd), lambda i: (i, 0)), pl.BlockSpec((None, WINDOW, kvd), lambda i: (i, 0, 0)),
                   pl.BlockSpec((None, nq, 128), lambda i: (i, 0, 0))],
        out_shape=[jax.ShapeDtypeStruct((s, d), BF16), jax.ShapeDtypeStruct((s, kvd), F32), jax.ShapeDtypeStruct((nt, WINDOW, kvd), F32),
                   jax.ShapeDtypeStruct((nt, nq, 128), F32)],
        scratch_shapes=[pltpu.VMEM((tq + WINDOW, kvd), BF16), pltpu.VMEM((tq + WINDOW, kvd), F32)],
        compiler_params=_params(("parallel",)),
    )(q, kv, kv, o, do, sinks)


HBM_SPEC = pl.BlockSpec(memory_space=pltpu.HBM)
VMEM_SPEC = pl.BlockSpec(memory_space=pltpu.VMEM)


def _place():
    return lax.axis_index("x"), lax.axis_index("y"), lax.axis_index("c")


def _flip(pos, r):
    return tuple(1 - p if (r >> (2 - a)) & 1 else p for a, p in enumerate(pos))


def _index(pos):
    return 4 * pos[0] + 2 * pos[1] + pos[2]


def _all_gather(name, shards, spec):
    n = len(shards)

    def body(*refs):
        x_refs, o_refs = refs[:n], refs[n:2 * n]
        send_sems, recv_sems, local_sems = refs[2 * n:]
        me = _place()
        sibling = _flip(me, 1)
        far = [_flip(me, r) for r in (4, 2, 6)]

        def copy(t, sem, block, to, src=None):
            rows = o_refs[t].at[_index(block)]
            return pltpu.make_async_remote_copy(
                src_ref=rows if src is None else src, dst_ref=rows, send_sem=send_sems.at[t, sem], recv_sem=recv_sems.at[t, sem],
                device_id=to, device_id_type=MESH)

        own = [pltpu.make_async_copy(x_refs[t], o_refs[t].at[_index(me)], local_sems.at[t]) for t in range(n)]
        for cp in own:
            cp.start()
        first = []
        for t in range(n):
            first.append(copy(t, 0, me, sibling, src=x_refs[t]))
            first += [copy(t, 1 + j, me, peer, src=x_refs[t]) for j, peer in enumerate(far)]
        for cp in first:
            cp.start()
        passed = []
        for j, peer in enumerate(far):
            for t in range(n):
                copy(t, 1 + j, peer, me).wait_recv()
                cp = copy(t, 4 + j, peer, sibling)
                cp.start()
                passed.append(cp)
        for t in range(n):
            copy(t, 0, sibling, me).wait_recv()
            for j, peer in enumerate(far):
                copy(t, 4 + j, _flip(peer, 1), me).wait_recv()
        for cp in first + passed:
            cp.wait_send()
        for cp in own:
            cp.wait()

    return pl.pallas_call(
        body, name=name, in_specs=[spec] * n, out_specs=[spec] * n,
        out_shape=[jax.ShapeDtypeStruct((N_DEV,) + sh.shape, sh.dtype) for sh in shards],
        scratch_shapes=[pltpu.SemaphoreType.DMA((n, 7)), pltpu.SemaphoreType.DMA((n, 7)), pltpu.SemaphoreType.DMA((n,))],
    )(*shards)


SEM_SPEC = pl.BlockSpec(memory_space=pltpu.SEMAPHORE)
ANY_SPEC = pl.BlockSpec(memory_space=pl.ANY)


def _landing(own, mine):
    return lax.dynamic_update_slice(lax.empty((N_DEV,) + own.shape, own.dtype), own[None], (mine,) + (0,) * own.ndim)


def _pinned(a, token):
    return a + token[0:1, 0:1].astype(a.dtype)


def _peer_copies(src_refs, land_refs, send_sems, recv_sems, scatter, arrivals):
    me = _place()
    mine = _index(me)
    copies = []
    for t, (src, land) in enumerate(zip(src_refs, land_refs)):
        for r in range(1, N_DEV):
            peer = _flip(me, r)
            theirs = _index(peer)
            sem = t * (N_DEV - 1) + r - 1
            copies.append(pltpu.make_async_remote_copy(
                src_ref=src.at[theirs] if scatter else src, dst_ref=land.at[theirs if arrivals else mine],
                send_sem=send_sems.at[sem], recv_sem=recv_sems.at[sem], device_id=peer, device_id_type=MESH))
    return copies


def _send_start(name, sources, lands, scatter, after=None):
    n = len(sources)
    extra = 0 if after is None else 1

    def body(*refs):
        outs = refs[2 * n + extra:]
        for out in _peer_copies(refs[:n], refs[n:2 * n], outs[0], outs[1], scatter, False):
            out.start()
        outs[-1][...] = jnp.zeros_like(outs[-1])

    outs = pl.pallas_call(
        body, name=name, in_specs=[HBM_SPEC] * (2 * n) + [ANY_SPEC] * extra,
        out_specs=[SEM_SPEC, SEM_SPEC] + [HBM_SPEC] * (2 * n) + [VMEM_SPEC],
        out_shape=[pltpu.SemaphoreType.DMA((n * (N_DEV - 1),)), pltpu.SemaphoreType.DMA((n * (N_DEV - 1),))]
        + [pltpu.HBM(a.shape, a.dtype) for a in list(sources) + list(lands)] + [jax.ShapeDtypeStruct((8, 128), F32)],
        input_output_aliases={i: 2 + i for i in range(2 * n)},
        compiler_params=pltpu.CompilerParams(has_side_effects=pltpu.SideEffectType.DATAFLOW_SIDE_EFFECTING),
    )(*[pltpu.with_memory_space_constraint(a, pltpu.HBM) for a in list(sources) + list(lands)], *([] if after is None else [after]))
    return outs[0], outs[1], outs[2:2 + n], outs[2 + n:2 + 2 * n], outs[-1]


def _send_wait(name, started, after, scatter):
    send_sems, recv_sems, sources, lands, _ = started
    n = len(sources)

    def body(*refs):
        for out in _peer_copies(refs[:n], refs[n:2 * n], refs[2 * n], refs[2 * n + 1], scatter, False):
            out.wait_send()
        for arrival in _peer_copies(refs[:n], refs[n:2 * n], refs[2 * n], refs[2 * n + 1], scatter, True):
            arrival.wait_recv()

    outs = pl.pallas_call(
        body, name=name, in_specs=[HBM_SPEC] * (2 * n) + [SEM_SPEC, SEM_SPEC, ANY_SPEC], out_specs=[HBM_SPEC] * (2 * n),
        out_shape=[pltpu.HBM(a.shape, a.dtype) for a in list(sources) + list(lands)],
        input_output_aliases={i: i for i in range(2 * n)},
        compiler_params=pltpu.CompilerParams(has_side_effects=pltpu.SideEffectType.DATAFLOW_SIDE_EFFECTING),
    )(*sources, *lands, send_sems, recv_sems, after)
    return outs[n:]


def _pack_rows(parts):
    offsets, row = [], 0
    for part in parts:
        offsets.append(row)
        row += part.shape[0]
    return offsets, -(-row // 8) * 8, -(-max(part.shape[1] for part in parts) // 128) * 128


def _pack(parts):
    offsets, rows, width = _pack_rows(parts)

    def body(*refs):
        o_ref = refs[-1]
        o_ref[...] = jnp.zeros_like(o_ref)
        for off, ref in zip(offsets, refs[:-1]):
            o_ref[off:off + ref.shape[0], 0:ref.shape[1]] = ref[...]

    return pl.pallas_call(body, name="pack_small_grads", in_specs=[VMEM_SPEC] * len(parts), out_specs=VMEM_SPEC,
                          out_shape=jax.ShapeDtypeStruct((rows, width), F32))(*parts)


def _adamw_math(w, g, m, v):
    m = ADAM_B1 * m + (1.0 - ADAM_B1) * g
    v = ADAM_B2 * v + (1.0 - ADAM_B2) * (g * g)
    m_hat = m / (1.0 - ADAM_B1 ** ADAM_STEP)
    v_hat = v / (1.0 - ADAM_B2 ** ADAM_STEP)
    return -ADAM_LR * (m_hat / (jnp.sqrt(v_hat) + ADAM_EPS) + ADAM_WD * w), m, v


def _adamw_shard(name, w, m, v, partials):
    rows, cols = w.shape
    tr = max(t for t in range(8, min(rows, 256) + 1, 8) if rows % t == 0)

    def body(w_ref, m_ref, v_ref, p_ref, g_ref, d_ref, nm_ref, nv_ref):
        g = p_ref[0].astype(F32)
        for dev in range(1, N_DEV):
            g = g + p_ref[dev].astype(F32)
        g_ref[...] = g
        d_ref[...], nm_ref[...], nv_ref[...] = _adamw_math(w_ref[...], g, m_ref[...], v_ref[...])

    blk = pl.BlockSpec((tr, cols), lambda i: (i, 0))
    return pl.pallas_call(
        body, name=name, grid=(rows // tr,), in_specs=[blk, blk, blk, pl.BlockSpec((N_DEV, tr, cols), lambda i: (0, i, 0))],
        out_specs=[blk] * 4, out_shape=[jax.ShapeDtypeStruct((rows, cols), F32)] * 4, compiler_params=_params(("parallel",)),
    )(w, m, v, partials)


def _adamw_small(gathered, offsets, entries):
    n = len(entries)

    def body(*refs):
        pack_ref = refs[0]
        w_refs, m_refs, v_refs = refs[1:1 + n], refs[1 + n:1 + 2 * n], refs[1 + 2 * n:1 + 3 * n]
        outs = refs[1 + 3 * n:]
        total = pack_ref[0]
        for dev in range(1, N_DEV):
            total = total + pack_ref[dev]
        mine = _index(_place())
        for e in range(n):
            rows, cols = w_refs[e].shape
            off = offsets[e]
            if entries[e][3]:
                g = jnp.zeros((rows, cols), F32)
                for dev in range(N_DEV):
                    g = g + jnp.where(mine == dev, total[off + dev * rows:off + (dev + 1) * rows, 0:cols], 0.0)
            else:
                g = total[off:off + rows, 0:cols]
            outs[4 * e][...] = g
            outs[4 * e + 1][...], outs[4 * e + 2][...], outs[4 * e + 3][...] = _adamw_math(w_refs[e][...], g, m_refs[e][...], v_refs[e][...])
        outs[4 * n][...] = total[offsets[n]:offsets[n] + 1, 0:128]

    shapes = []
    for w, _, _, _ in entries:
        shapes += [jax.ShapeDtypeStruct(w.shape, F32)] * 4
    shapes.append(jax.ShapeDtypeStruct((1, 128), F32))
    return pl.pallas_call(
        body, name="adamw_small", in_specs=[VMEM_SPEC] * (1 + 3 * n), out_specs=[VMEM_SPEC] * len(shapes), out_shape=shapes,
        compiler_params=pltpu.CompilerParams(vmem_limit_bytes=VMEM_LIMIT),
    )(gathered, *[e[0] for e in entries], *[e[1] for e in entries], *[e[2] for e in entries])


def _ffn_forward(tag, h, gain, w_up, w_down, conv_w, conv_b):
    s, d = h.shape
    fb = w_up.shape[2]
    tm = _row_tile(s, MM_ROWS)
    a, = _rmsnorm_cast(f"ffn_norm_{tag}", h, [gain])
    u = _matmul(
        f"ffn_up_{tag}", a, w_up, dims=NN, grid=(s // tm, N_DEV, 1),
        a_spec=pl.BlockSpec((tm, d), lambda i, j, k: (i, 0)),
        b_spec=pl.BlockSpec((None, d, fb), lambda i, j, k: (j, 0, 0)),
        o_spec=pl.BlockSpec((None, None, tm, fb), lambda i, j, k: (j // 4, j % 4, i, 0)),
        out_shape=jax.ShapeDtypeStruct((2, 4, s, fb), BF16), acc_shape=(8, 128))
    hidden = _ffn_hidden(u, conv_w, conv_b)
    out = _matmul(
        f"ffn_down_{tag}", hidden, w_down, dims=NN, grid=(s // tm, 1, 4),
        a_spec=pl.BlockSpec((None, tm, fb), lambda i, j, k: (k, i, 0)),
        b_spec=pl.BlockSpec((None, fb, d), lambda i, j, k: (k, 0, 0)),
        o_spec=pl.BlockSpec((tm, d), lambda i, j, k: (i, 0)),
        out_shape=jax.ShapeDtypeStruct((s, d), F32), acc_shape=(tm, d),
        add=h, add_spec=pl.BlockSpec((tm, d), lambda i, j, k: (i, 0)))
    return out, (a, u, hidden)


def _ffn_backward(tag, h, gain, w_up, w_down, conv_w, conv_b, saved, dout):
    a, u, hidden = saved
    s, d = h.shape
    fb = w_up.shape[2]
    tm = _row_tile(s, MM_ROWS)
    dhidden = _matmul(
        f"ffn_down_bwd_{tag}", dout, w_down, dims=NT, grid=(s // tm, 4, 1),
        a_spec=pl.BlockSpec((tm, d), lambda i, j, k: (i, 0)),
        b_spec=pl.BlockSpec((None, fb, d), lambda i, j, k: (j, 0, 0)),
        o_spec=pl.BlockSpec((None, tm, fb), lambda i, j, k: (j, i, 0)),
        out_shape=jax.ShapeDtypeStruct((4, s, fb), BF16), acc_shape=(8, 128))
    dw_down = _matmul(
        f"ffn_down_grad_{tag}", hidden, dout, dims=TN, grid=(4, 1, s // tm),
        a_spec=pl.BlockSpec((None, tm, fb), lambda i, j, k: (i, k, 0)),
        b_spec=pl.BlockSpec((tm, d), lambda i, j, k: (k, 0)),
        o_spec=pl.BlockSpec((None, fb, d), lambda i, j, k: (i, 0, 0)),
        out_shape=jax.ShapeDtypeStruct((4, fb, d), BF16), acc_shape=(fb, d))
    du, dconv_w, dconv_b = _ffn_hidden_bwd(u, dhidden, conv_w, conv_b)
    da = _matmul(
        f"ffn_up_bwd_{tag}", du, w_up, dims=NT, grid=(s // tm, 1, N_DEV),
        a_spec=pl.BlockSpec((None, None, tm, fb), lambda i, j, k: (k // 4, k % 4, i, 0)),
        b_spec=pl.BlockSpec((None, d, fb), lambda i, j, k: (k, 0, 0)),
        o_spec=pl.BlockSpec((tm, d), lambda i, j, k: (i, 0)),
        out_shape=jax.ShapeDtypeStruct((s, d), F32), acc_shape=(tm, d))
    dw_up = _matmul(
        f"ffn_up_grad_{tag}", a, du, dims=TN, grid=(1, N_DEV, s // tm),
        a_spec=pl.BlockSpec((tm, d), lambda i, j, k: (k, 0)),
        b_spec=pl.BlockSpec((None, None, tm, fb), lambda i, j, k: (j // 4, j % 4, k, 0)),
        o_spec=pl.BlockSpec((None, d, fb), lambda i, j, k: (j, 0, 0)),
        out_shape=jax.ShapeDtypeStruct((N_DEV, d, fb), BF16), acc_shape=(d, fb))
    dh, (dgain,) = _rmsnorm_bwd(f"ffn_norm_bwd_{tag}", h, dout, [(da, gain)])
    return dh, dgain, dw_up, dw_down, dconv_w, dconv_b


def kernel(x, hg_norm, hg_w_in, hg_lb_logits, hg_out_norm, hg_w_out, kv_norm, w_kv, attn_norm, attn_w_q, attn_sinks, attn_w_o, ffn_norm, ffn_w_up, ffn_conv_w, ffn_conv_b, ffn_w_down, final_norm, loss_target, m_hg_norm, m_hg_w_in, m_hg_lb_logits, m_hg_out_norm, m_hg_w_out, m_kv_norm, m_w_kv, m_attn_norm, m_attn_w_q, m_attn_sinks, m_attn_w_o, m_ffn_norm, m_ffn_w_up, m_ffn_conv_w, m_ffn_conv_b, m_ffn_w_down, m_final_norm, v_hg_norm, v_hg_w_in, v_hg_lb_logits, v_hg_out_norm, v_hg_w_out, v_kv_norm, v_w_kv, v_attn_norm, v_attn_w_q, v_attn_sinks, v_attn_w_o, v_ffn_norm, v_ffn_w_up, v_ffn_conv_w, v_ffn_conv_b, v_ffn_w_down, v_final_norm):
    _, s, d = x.shape
    x0, target = x[0], loss_target[0]
    half = hg_w_in.shape[2]
    fs = ffn_conv_w.shape[2]
    fb = 2 * fs
    kvd = w_kv.shape[1]
    nq = d // ATT_HEAD_DIM
    tm = _row_tile(s, MM_ROWS)

    mine = _index(_place())
    gather = lambda tag, shards, after: _send_start("gather_start_" + tag, shards, [_landing(a, mine) for a in shards], False, after)
    w_in, g_hgn, g_lbl = _all_gather("gather_hg", [hg_w_in[0].astype(BF16), hg_norm, hg_lb_logits], HBM_SPEC)
    coming_out = gather("hg_out", [hg_w_out[0].astype(BF16)], g_hgn)
    coming_ffn0 = gather("ffn0", [ffn_w_up[0].astype(BF16), ffn_conv_w, ffn_w_down[0].astype(BF16)], g_hgn)
    hgn = _pinned(_pinned(g_hgn.reshape(1, d), coming_out[4]), coming_ffn0[4])
    lbl = g_lbl.transpose(1, 0, 2).reshape(2, d)
    conv_b = [ffn_conv_b[layer].reshape(4, 1, fb) for layer in range(2)]
    gains = [ffn_norm[0:1], ffn_norm[1:2]]
    kvn, fin = kv_norm.reshape(1, d), final_norm.reshape(1, d)

    a0, = _rmsnorm_cast("hg_norm", x0, [hgn])
    p = _matmul(
        "hg_in", a0, w_in, dims=NN, grid=(s // tm, N_DEV, 1),
        a_spec=pl.BlockSpec((tm, d), lambda i, j, k: (i, 0)),
        b_spec=pl.BlockSpec((None, d, half), lambda i, j, k: (j, 0, 0)),
        o_spec=pl.BlockSpec((None, tm, half), lambda i, j, k: (j // 2, i, j % 2)),
        out_shape=jax.ShapeDtypeStruct((4, s, d), BF16), acc_shape=(8, 128))
    o, og, states = _hgrn2_fwd(p, lbl, hg_out_norm)
    w_out = _send_wait("gather_wait_hg_out", coming_out, og, False)[0].reshape(d, d)
    x1 = _mm_rows("hg_out", og, w_out, out_dtype=F32, add=x0)
    w_up0, g_cw, w_dn0 = _send_wait("gather_wait_ffn0", coming_ffn0, x1, False)
    w_up, w_dn = [w_up0, None], [w_dn0.reshape(4, fb, d), None]
    conv_w = [g_cw[:, layer].reshape(4, 2, CONV_WIDTH, fs).transpose(0, 2, 1, 3).reshape(4, CONV_WIDTH, fb) for layer in range(2)]
    coming_attn = gather("attn", [w_kv.astype(BF16), attn_w_q[0].astype(BF16), attn_w_o[0].astype(BF16)], w_dn0)
    coming_ffn1 = gather("ffn1", [ffn_w_up[1].astype(BF16), ffn_w_down[1].astype(BF16)], w_dn0)
    gains[0] = _pinned(_pinned(gains[0], coming_attn[4]), coming_ffn1[4])
    x2, saved0 = _ffn_forward("0", x1, gains[0], w_up[0], w_dn[0], conv_w[0], conv_b[0])
    w_kvg, w_q, w_o = _send_wait("gather_wait_attn", coming_attn, x2, False)
    w_kvg, w_q, w_o = w_kvg.reshape(d, kvd), w_q.reshape(d, d), w_o.reshape(d, d)
    akv, a2 = _rmsnorm_cast("attn_norms", x2, [kvn, attn_norm])
    kv = _mm_rows("kv_proj", akv, w_kvg, out_dtype=BF16)
    q = _mm_rows("q_proj", a2, w_q, out_dtype=BF16)
    att = _attn_fwd(q, kv, attn_sinks)
    x3 = _mm_rows("attn_out", att, w_o, out_dtype=F32, add=x2)
    w_up[1], w_dn1 = _send_wait("gather_wait_ffn1", coming_ffn1, x3, False)
    w_dn[1] = w_dn1.reshape(4, fb, d)
    x4, saved1 = _ffn_forward("1", x3, gains[1], w_up[1], w_dn[1], conv_w[1], conv_b[1])
    dx4, d_fin, loss_part = _loss_head(x4, fin, target)

    dx3, d_fn1, dw_up1, dw_dn1, dcw1, dcb1 = _ffn_backward("1", x3, gains[1], w_up[1], w_dn[1], conv_w[1], conv_b[1], saved1, dx4)
    rows = d // N_DEV
    scatter = lambda tag, stacks: _send_start("scatter_start_" + tag, stacks, [_landing(lax.dynamic_index_in_dim(a, mine, keepdims=False), mine) for a in stacks], True)
    going_ffn1 = scatter("ffn1", [dw_up1, dw_dn1.reshape(N_DEV, fs, d)])
    datt = _mm_rows_nt("attn_out_bwd", dx3, w_o, out_dtype=BF16)
    dw_o = _mm_tn("attn_out_grad", att, dx3)
    dq, dkv_own, dkv_before, dsink = _attn_bwd(q, kv, att, datt, _pinned(attn_sinks, going_ffn1[4]))
    tiles = dkv_before.shape[0]
    dkv = dkv_own.reshape(tiles, s // tiles, kvd)
    dkv = jnp.concatenate([dkv[:, :-WINDOW], dkv[:, -WINDOW:] + jnp.pad(dkv_before[1:], ((0, 1), (0, 0), (0, 0)))], axis=1).reshape(s, kvd)
    da2 = _mm_rows_nt("q_proj_bwd", dq, w_q, out_dtype=F32)
    dw_q = _mm_tn("q_proj_grad", a2, dq)
    dakv = _mm_rows_nt("kv_proj_bwd", dkv, w_kvg, out_dtype=F32)
    dw_kv = _mm_tn("kv_proj_grad", akv, dkv)
    going_attn = scatter("attn", [dw_kv.reshape(N_DEV, rows, kvd), dw_q.reshape(N_DEV, rows, d), dw_o.reshape(N_DEV, rows, d)])
    dx2, (d_kvn, d_attn) = _rmsnorm_bwd("attn_norms_bwd", x2, dx3, [(dakv, _pinned(kvn, going_attn[4])), (da2, attn_norm)])
    dx1, d_fn0, dw_up0, dw_dn0, dcw0, dcb0 = _ffn_backward("0", x1, gains[0], w_up[0], w_dn[0], conv_w[0], conv_b[0], saved0, dx2)
    going_ffn0 = scatter("ffn0", [dw_up0, dw_dn0.reshape(N_DEV, fs, d)])
    dog = _mm_rows_nt("hg_out_bwd", dx1, w_out, out_dtype=F32)
    dw_out = _mm_tn("hg_out_grad", og, dx1)
    dp, d_lbl, d_ogain = _hgrn2_bwd(p, lbl, _pinned(hg_out_norm, going_ffn0[4]), o, dog, states)
    dw_in = _matmul(
        "hg_in_grad", a0, dp, dims=TN, grid=(1, N_DEV, s // tm),
        a_spec=pl.BlockSpec((tm, d), lambda i, j, k: (k, 0)),
        b_spec=pl.BlockSpec((None, tm, half), lambda i, j, k: (j // 2, k, j % 2)),
        o_spec=pl.BlockSpec((None, d, half), lambda i, j, k: (j, 0, 0)),
        out_shape=jax.ShapeDtypeStruct((N_DEV, d, half), BF16), acc_shape=(d, half))
    going_hg = scatter("hg", [dw_in, dw_out.reshape(N_DEV, rows, d)])
    da0 = _matmul(
        "hg_in_bwd", dp, w_in, dims=NT, grid=(s // tm, 1, N_DEV),
        a_spec=pl.BlockSpec((None, tm, half), lambda i, j, k: (k // 2, i, k % 2)),
        b_spec=pl.BlockSpec((None, d, half), lambda i, j, k: (k, 0, 0)),
        o_spec=pl.BlockSpec((tm, d), lambda i, j, k: (i, 0)),
        out_shape=jax.ShapeDtypeStruct((s, d), F32), acc_shape=(tm, d))
    dx0, (d_hgn,) = _rmsnorm_bwd("hg_norm_bwd", x0, dx1, [(da0, _pinned(hgn, going_hg[4]))])

    arrive = lambda tag, going: _send_wait("scatter_wait_" + tag, going, dx0, True)
    (l_up1, l_dn1), (l_kv, l_q, l_o), (l_up0, l_dn0), (l_in, l_out) = (
        arrive("ffn1", going_ffn1), arrive("attn", going_attn), arrive("ffn0", going_ffn0), arrive("hg", going_hg))
    landed = [l_in, l_out, l_kv, l_q, l_o, l_up0, l_up1, l_dn0, l_dn1]
    big = {}
    for tag, w, m, v, part in [
            ("hg_w_in", hg_w_in[0], m_hg_w_in[0], v_hg_w_in[0], landed[0]), ("hg_w_out", hg_w_out[0], m_hg_w_out[0], v_hg_w_out[0], landed[1]),
            ("w_kv", w_kv, m_w_kv, v_w_kv, landed[2]), ("attn_w_q", attn_w_q[0], m_attn_w_q[0], v_attn_w_q[0], landed[3]),
            ("attn_w_o", attn_w_o[0], m_attn_w_o[0], v_attn_w_o[0], landed[4]),
            ("ffn_w_up0", ffn_w_up[0], m_ffn_w_up[0], v_ffn_w_up[0], landed[5]), ("ffn_w_up1", ffn_w_up[1], m_ffn_w_up[1], v_ffn_w_up[1], landed[6]),
            ("ffn_w_down0", ffn_w_down[0], m_ffn_w_down[0], v_ffn_w_down[0], landed[7]),
            ("ffn_w_down1", ffn_w_down[1], m_ffn_w_down[1], v_ffn_w_down[1], landed[8])]:
        big[tag] = _adamw_shard("adamw_" + tag, w, m, v, part)
    lead = lambda tag: [a[None] for a in big[tag]]
    pair = lambda tag: [jnp.stack([a, b]) for a, b in zip(big[tag + "0"], big[tag + "1"])]

    as_blocks = lambda a, r: a.reshape(r, N_DEV, -1).transpose(1, 0, 2).reshape(N_DEV * r, -1)
    d_cw = jnp.concatenate([g.transpose(1, 0, 2).reshape(CONV_WIDTH, 4 * fb) for g in (dcw0, dcw1)], axis=0)
    parts = [d_fin, jnp.concatenate([d_fn0, d_fn1], axis=0), jnp.concatenate([dcb0.reshape(1, 4 * fb), dcb1.reshape(1, 4 * fb)], axis=0),
             as_blocks(d_cw, 2 * CONV_WIDTH), d_attn, jnp.sum(dsink[:, :, 0], axis=0).reshape(1, nq), d_kvn, d_ogain,
             as_blocks(d_hgn, 1), as_blocks(d_lbl, 2), loss_part]
    offsets, _, _ = _pack_rows(parts)
    gathered, = _all_gather("gather_small_grads", [_pack(parts)], VMEM_SPEC)
    two = lambda a: a.reshape(-1, a.shape[-1])
    small = [(fin, m_final_norm.reshape(1, d), v_final_norm.reshape(1, d), False), (ffn_norm, m_ffn_norm, v_ffn_norm, False),
             (ffn_conv_b, m_ffn_conv_b, v_ffn_conv_b, False), (two(ffn_conv_w), two(m_ffn_conv_w), two(v_ffn_conv_w), True),
             (attn_norm, m_attn_norm, v_attn_norm, False), (attn_sinks, m_attn_sinks, v_attn_sinks, False),
             (kvn, m_kv_norm.reshape(1, d), v_kv_norm.reshape(1, d), False), (hg_out_norm, m_hg_out_norm, v_hg_out_norm, False),
             (hg_norm, m_hg_norm, v_hg_norm, True), (hg_lb_logits, m_hg_lb_logits, v_hg_lb_logits, True)]
    res = _adamw_small(gathered, offsets, small)
    names = ["final_norm", "ffn_norm", "ffn_conv_b", "ffn_conv_w", "attn_norm", "attn_sinks", "kv_norm", "hg_out_norm", "hg_norm", "hg_lb_logits"]
    shapes = {"final_norm": final_norm.shape, "kv_norm": kv_norm.shape, "ffn_conv_w": ffn_conv_w.shape}
    out = {n: [a.reshape(shapes[n]) if n in shapes else a for a in res[4 * i:4 * i + 4]] for i, n in enumerate(names)}
    out.update(hg_w_in=lead("hg_w_in"), hg_w_out=lead("hg_w_out"), w_kv=big["w_kv"], attn_w_q=lead("attn_w_q"), attn_w_o=lead("attn_w_o"),
               ffn_w_up=pair("ffn_w_up"), ffn_w_down=pair("ffn_w_down"))
    order = ["hg_norm", "hg_w_in", "hg_lb_logits", "hg_out_norm", "hg_w_out", "kv_norm", "w_kv", "attn_norm", "attn_w_q", "attn_sinks",
             "attn_w_o", "ffn_norm", "ffn_w_up", "ffn_conv_w", "ffn_conv_b", "ffn_w_down", "final_norm"]
    loss = res[-1][0, 0]
    return (loss, dx0[None], *[out[n][0] for n in order], *[out[n][1] for n in order], *[out[n][2] for n in order], *[out[n][3] for n in order])
```

```python
import functools
import math

import jax
import jax.numpy as jnp
from jax import lax
from jax.experimental import pallas as pl
from jax.experimental.pallas import tpu as pltpu

F32 = jnp.float32
BF16 = jnp.bfloat16

EPS = 1e-6
HG_EXPAND = 128
HG_CHUNK = 32
ATT_HEAD_DIM = 64
ATT_KV_HEADS = 2
WINDOW = 128
CONV_WIDTH = 3
ADAM_LR = 0.001
ADAM_B1 = 0.9
ADAM_B2 = 0.999
ADAM_EPS = 1e-08
ADAM_WD = 0.01
ADAM_STEP = 10

N_DEV = 8
VMEM_LIMIT = 48 * 1024 * 1024
NEG = -1e30

NN = (((1,), (0,)), ((), ()))
NT = (((1,), (1,)), ((), ()))
TN = (((0,), (0,)), ((), ()))
MESH = pl.DeviceIdType.MESH


def _dot(a, b, dims=NN):
    return lax.dot_general(a.astype(BF16), b.astype(BF16), dims, preferred_element_type=F32)


def _sigmoid(x):
    return 1.0 / (1.0 + jnp.exp(-x))


def _silu(x):
    return x * _sigmoid(x)


def _dsilu(x):
    s = _sigmoid(x)
    return s * (1.0 + x * (1.0 - s))


def _params(semantics):
    return pltpu.CompilerParams(dimension_semantics=semantics, vmem_limit_bytes=VMEM_LIMIT)


def _row_tile(rows, want=512):
    return min(rows, want)


MM_ROWS = 1024


def _matmul(name, a, b, *, dims, grid, a_spec, b_spec, o_spec, out_shape, acc_shape=(8, 128), add=None, add_spec=None, terms=None):
    nk = grid[2]

    def body(*refs):
        if add is None:
            a_ref, b_ref, o_ref, acc = refs
        else:
            a_ref, b_ref, add_ref, o_ref, acc = refs
        k = pl.program_id(2)
        pairs = [(a_ref[...], b_ref[...])] if terms is None else terms(a_ref, b_ref)
        part = _dot(*pairs[0], dims)
        for pair in pairs[1:]:
            part = part + _dot(*pair, dims)

        def finish(total):
            if add is not None:
                total = total + add_ref[...]
            o_ref[...] = total.astype(o_ref.dtype)

        if nk == 1:
            finish(part)
        else:
            @pl.when(k == 0)
            def _():
                acc[...] = part

            @pl.when(k > 0)
            def _():
                acc[...] += part

            @pl.when(k == nk - 1)
            def _():
                finish(acc[...])

    in_specs = [a_spec, b_spec] + ([] if add is None else [add_spec])
    args = (a, b) + (() if add is None else (add,))
    return pl.pallas_call(
        body, name=name, grid=grid, in_specs=in_specs, out_specs=o_spec, out_shape=out_shape,
        scratch_shapes=[pltpu.VMEM(acc_shape, F32)],
        compiler_params=_params(("parallel", "parallel", "arbitrary")),
    )(*args)


def _mm_rows(name, a, w, *, out_dtype, add=None):
    s, kdim = a.shape
    n = w.shape[1]
    tm = _row_tile(s, MM_ROWS)
    return _matmul(
        name, a, w, dims=NN, grid=(s // tm, 1, 1),
        a_spec=pl.BlockSpec((tm, kdim), lambda i, j, k: (i, 0)),
        b_spec=pl.BlockSpec((kdim, n), lambda i, j, k: (0, 0)),
        o_spec=pl.BlockSpec((tm, n), lambda i, j, k: (i, 0)),
        out_shape=jax.ShapeDtypeStruct((s, n), out_dtype), acc_shape=(8, 128),
        add=add, add_spec=None if add is None else pl.BlockSpec((tm, n), lambda i, j, k: (i, 0)),
    )


def _mm_rows_nt(name, a, w, *, out_dtype):
    s, n = a.shape
    kdim = w.shape[0]
    tm = _row_tile(s, MM_ROWS)
    return _matmul(
        name, a, w, dims=NT, grid=(s // tm, 1, 1),
        a_spec=pl.BlockSpec((tm, n), lambda i, j, k: (i, 0)),
        b_spec=pl.BlockSpec((kdim, n), lambda i, j, k: (0, 0)),
        o_spec=pl.BlockSpec((tm, kdim), lambda i, j, k: (i, 0)),
        out_shape=jax.ShapeDtypeStruct((s, kdim), out_dtype), acc_shape=(8, 128),
    )


def _mm_tn(name, a, g):
    s, m = a.shape
    n = g.shape[1]
    tn = min(n, 512)
    return _matmul(
        name, a, g, dims=TN, grid=(1, n // tn, 1),
        a_spec=pl.BlockSpec((s, m), lambda i, j, k: (0, 0)),
        b_spec=pl.BlockSpec((s, tn), lambda i, j, k: (0, j)),
        o_spec=pl.BlockSpec((m, tn), lambda i, j, k: (0, j)),
        out_shape=jax.ShapeDtypeStruct((m, n), BF16),
    )


def _rmsnorm_cast(name, h, gains):
    s, d = h.shape
    tm = _row_tile(s)
    n = len(gains)

    def body(*refs):
        h_ref, g_refs, o_refs = refs[0], refs[1:1 + n], refs[1 + n:]
        xv = h_ref[...]
        xhat = xv * lax.rsqrt(jnp.mean(xv * xv, axis=-1, keepdims=True) + EPS)
        for g_ref, o_ref in zip(g_refs, o_refs):
            o_ref[...] = (xhat * g_ref[...]).astype(BF16)

    row = pl.BlockSpec((tm, d), lambda i: (i, 0))
    vec = pl.BlockSpec((1, d), lambda i: (0, 0))
    return pl.pallas_call(
        body, name=name, grid=(s // tm,), in_specs=[row] + [vec] * n, out_specs=[row] * n,
        out_shape=[jax.ShapeDtypeStruct((s, d), BF16)] * n, compiler_params=_params(("parallel",)),
    )(h, *gains)


def _rmsnorm_bwd(name, h, dres, branches):
    s, d = h.shape
    tm = _row_tile(s)
    n = len(branches)

    def body(*refs):
        h_ref, dres_ref = refs[0], refs[1]
        da_refs, g_refs = refs[2:2 + n], refs[2 + n:2 + 2 * n]
        dh_ref, dhb_ref, dg_refs = refs[2 + 2 * n], refs[3 + 2 * n], refs[4 + 2 * n:]
        i = pl.program_id(0)
        xv = h_ref[...]
        r = lax.rsqrt(jnp.mean(xv * xv, axis=-1, keepdims=True) + EPS)
        xhat = xv * r
        total = dres_ref[...]
        for da_ref, g_ref, dg_ref in zip(da_refs, g_refs, dg_refs):
            da = da_ref[...]
            dgain = jnp.sum(da * xhat, axis=0, keepdims=True)

            @pl.when(i == 0)
            def _():
                dg_ref[...] = dgain

            @pl.when(i > 0)
            def _():
                dg_ref[...] += dgain

            dxhat = da * g_ref[...]
            total = total + r * (dxhat - xhat * jnp.mean(dxhat * xhat, axis=-1, keepdims=True))
        dh_ref[...] = total
        dhb_ref[...] = total.astype(BF16)

    row = pl.BlockSpec((tm, d), lambda i: (i, 0))
    vec = pl.BlockSpec((1, d), lambda i: (0, 0))
    outs = pl.pallas_call(
        body, name=name, grid=(s // tm,), in_specs=[row, row] + [row] * n + [vec] * n, out_specs=[row, row] + [vec] * n,
        out_shape=[jax.ShapeDtypeStruct((s, d), F32), jax.ShapeDtypeStruct((s, d), BF16)] + [jax.ShapeDtypeStruct((1, d), F32)] * n,
        compiler_params=_params(("arbitrary",)),
    )(h, dres, *[b[0] for b in branches], *[b[1] for b in branches])
    return (outs[0], outs[1]), outs[2:]


def _loss_head(h, gain, target):
    s, d = h.shape
    tm = _row_tile(s)

    def body(h_ref, g_ref, t_ref, dh_ref, dhb_ref, dg_ref, loss_ref):
        i = pl.program_id(0)
        xv = h_ref[...]
        r = lax.rsqrt(jnp.mean(xv * xv, axis=-1, keepdims=True) + EPS)
        xhat = xv * r
        err = xhat * g_ref[...] - t_ref[...]
        dy = err * (1.0 / d)
        part = jnp.zeros((1, 128), F32) + 0.5 * jnp.sum(jnp.mean(err * err, axis=-1, keepdims=True))
        dgain = jnp.sum(dy * xhat, axis=0, keepdims=True)

        @pl.when(i == 0)
        def _():
            dg_ref[...] = dgain
            loss_ref[...] = part

        @pl.when(i > 0)
        def _():
            dg_ref[...] += dgain
            loss_ref[...] += part

        dxhat = dy * g_ref[...]
        dh = r * (dxhat - xhat * jnp.mean(dxhat * xhat, axis=-1, keepdims=True))
        dh_ref[...] = dh
        dhb_ref[...] = dh.astype(BF16)

    row = pl.BlockSpec((tm, d), lambda i: (i, 0))
    vec = pl.BlockSpec((1, d), lambda i: (0, 0))
    return pl.pallas_call(
        body, name="loss_head", grid=(s // tm,), in_specs=[row, vec, row],
        out_specs=[row, row, vec, pl.BlockSpec((1, 128), lambda i: (0, 0))],
        out_shape=[jax.ShapeDtypeStruct((s, d), F32), jax.ShapeDtypeStruct((s, d), BF16), jax.ShapeDtypeStruct((1, d), F32),
                   jax.ShapeDtypeStruct((1, 128), F32)],
        compiler_params=_params(("arbitrary",)),
    )(h, gain, target)


def _bdot(a, b, ca, cb):
    return lax.dot_general(a.astype(BF16), b.astype(BF16), (((ca,), (cb,)), ((0,), (0,))), preferred_element_type=F32)


def _chunk_cumsum(xv, reverse=False):
    n = xv.shape[0]
    row = lax.broadcasted_iota(jnp.int32, xv.shape, 0) % HG_CHUNK
    step = 1
    while step < HG_CHUNK:
        if reverse:
            xv = xv + jnp.where(row < HG_CHUNK - step, pltpu.roll(xv, n - step, axis=0), 0.0)
        else:
            xv = xv + jnp.where(row >= step, pltpu.roll(xv, step, axis=0), 0.0)
        step *= 2
    return xv


def _hg_terms(p_ref, lbl_ref):
    pq = p_ref[0].astype(F32)
    pf = p_ref[1].astype(F32)
    lb = _sigmoid(lbl_ref[0:1, :] - lbl_ref[1:2, :])
    sig = _sigmoid(pf)
    fg = lb + (1.0 - lb) * sig
    nc = pq.shape[0] // HG_CHUNK
    chunks = lambda a: a.reshape(nc, HG_CHUNK, HG_EXPAND)
    q = chunks(_silu(pq) * HG_EXPAND ** -0.5)
    k = chunks(1.0 - fg)
    v = chunks(p_ref[2].astype(F32))
    g = chunks(_chunk_cumsum(jnp.log(fg)))
    gm = g[:, HG_CHUNK // 2 - 1:HG_CHUNK // 2, :]
    gl = g[:, HG_CHUNK - 1:HG_CHUNK, :]
    e_mid, e_inv, e_all, e_end = jnp.exp(g - gm), jnp.exp(gm - g), jnp.exp(g), jnp.exp(gl - g)
    terms = dict(q=q, k=k, v=v, qd=q * e_all, qt=q * e_mid, kt=k * e_inv, kd=k * e_end, e_last=jnp.exp(gl),
                 e_mid=e_mid, e_inv=e_inv, e_all=e_all, e_end=e_end)
    return terms, (pq, sig, fg, lb)


def _causal(nc):
    r = lax.broadcasted_iota(jnp.int32, (nc, HG_CHUNK, HG_CHUNK), 1)
    c = lax.broadcasted_iota(jnp.int32, (nc, HG_CHUNK, HG_CHUNK), 2)
    return r >= c


def _hgrn2_fwd(p, lb_logits, out_gain):
    _, s, d = p.shape
    heads = d // HG_EXPAND
    t = _row_tile(s)
    nc = t // HG_CHUNK

    def body(p_ref, lbl_ref, gain_ref, o_ref, og_ref, st_ref, state, decay):
        @pl.when(pl.program_id(1) == 0)
        def _():
            state[...] = jnp.zeros_like(state)

        tm, _ = _hg_terms(p_ref, lbl_ref)
        decay[...] = tm["e_last"]
        st_ref[...] = _bdot(tm["v"], tm["kd"], 1, 1)

        def chunk(c, carry):
            add = st_ref[c]
            st = state[...]
            st_ref[c] = st
            state[...] = st * decay[c] + add
            return carry

        lax.fori_loop(0, nc, chunk, 0)
        a = jnp.where(_causal(nc), _bdot(tm["qt"], tm["kt"], 2, 2), 0.0)
        ov = (_bdot(tm["qd"], st_ref[...], 2, 2) + _bdot(a, tm["v"], 2, 1)).reshape(t, HG_EXPAND)
        o_ref[...] = ov
        on = ov * lax.rsqrt(jnp.mean(ov * ov, axis=-1, keepdims=True) + EPS) * gain_ref[...]
        og_ref[...] = (on * _silu(p_ref[3].astype(F32))).astype(BF16)

    blk = pl.BlockSpec((t, HG_EXPAND), lambda h, b: (b, h))
    return pl.pallas_call(
        body, name="hgrn2_fwd", grid=(heads, s // t),
        in_specs=[pl.BlockSpec((4, t, HG_EXPAND), lambda h, b: (0, b, h)), pl.BlockSpec((2, HG_EXPAND), lambda h, b: (0, h)),
                  pl.BlockSpec((1, HG_EXPAND), lambda h, b: (0, 0))],
        out_specs=[blk, blk, pl.BlockSpec((None, nc, HG_EXPAND, HG_EXPAND), lambda h, b: (h, b, 0, 0))],
        out_shape=[jax.ShapeDtypeStruct((s, d), F32), jax.ShapeDtypeStruct((s, d), BF16),
                   jax.ShapeDtypeStruct((heads, s // HG_CHUNK, HG_EXPAND, HG_EXPAND), F32)],
        scratch_shapes=[pltpu.VMEM((HG_EXPAND, HG_EXPAND), F32), pltpu.VMEM((nc, 1, HG_EXPAND), F32)],
        compiler_params=_params(("parallel", "arbitrary")),
    )(p, lb_logits, out_gain)


def _hgrn2_bwd(p, lb_logits, out_gain, o, dog, states):
    _, s, d = p.shape
    heads = d // HG_EXPAND
    t = _row_tile(s)
    nc = t // HG_CHUNK
    nb = s // t

    def body(p_ref, lbl_ref, gain_ref, o_ref, dog_ref, st_ref, dp_ref, dlbl_ref, dgain_ref, dstate, decay, dst_s):
        h, b = pl.program_id(0), pl.program_id(1)

        @pl.when(b == 0)
        def _():
            dstate[...] = jnp.zeros_like(dstate)

        tm, (pq, sig, fg, lb) = _hg_terms(p_ref, lbl_ref)
        pg = p_ref[3].astype(F32)
        ov = o_ref[...]
        r = lax.rsqrt(jnp.mean(ov * ov, axis=-1, keepdims=True) + EPS)
        ohat = ov * r
        dogv = dog_ref[...]
        d_on = dogv * _silu(pg)
        dp_ref[3] = (dogv * ohat * gain_ref[...] * _dsilu(pg)).astype(BF16)
        dgain = jnp.sum(d_on * ohat, axis=0, keepdims=True)

        @pl.when((h == 0) & (b == 0))
        def _():
            dgain_ref[...] = dgain

        @pl.when((h > 0) | (b > 0))
        def _():
            dgain_ref[...] += dgain

        dohat = d_on * gain_ref[...]
        do = (r * (dohat - ohat * jnp.mean(dohat * ohat, axis=-1, keepdims=True))).reshape(nc, HG_CHUNK, HG_EXPAND)

        decay[...] = tm["e_last"]
        dst_s[...] = _bdot(do, tm["qd"], 1, 1)

        def chunk(i, carry):
            c = nc - 1 - i
            add = dst_s[c]
            dst = dstate[...]
            dst_s[c] = dst
            dstate[...] = dst * decay[c] + add
            return carry

        lax.fori_loop(0, nc, chunk, 0)
        st, dst = st_ref[...], dst_s[...]
        causal = _causal(nc)
        a = jnp.where(causal, _bdot(tm["qt"], tm["kt"], 2, 2), 0.0)
        da = jnp.where(causal, _bdot(do, tm["v"], 2, 2), 0.0)
        dqt = _bdot(da, tm["kt"], 2, 1)
        dkt = _bdot(da, tm["qt"], 1, 1)
        dqd = _bdot(do, st, 2, 1)
        dkd = _bdot(tm["v"], dst, 2, 1)
        dv = _bdot(a, do, 1, 1) + _bdot(tm["kd"], dst, 2, 2)
        dq = dqt * tm["e_mid"] + dqd * tm["e_all"]
        dk = dkt * tm["e_inv"] + dkd * tm["e_end"]
        dg = dqt * tm["qt"] - dkt * tm["kt"] + dqd * tm["qd"] - dkd * tm["kd"]
        dgl = jnp.sum(dkd * tm["kd"], axis=1, keepdims=True) + tm["e_last"] * jnp.sum(dst * st, axis=1, keepdims=True)
        last_row = lax.broadcasted_iota(jnp.int32, (nc, HG_CHUNK, HG_EXPAND), 1) == HG_CHUNK - 1
        flat = lambda a3: a3.reshape(t, HG_EXPAND)
        dlf = _chunk_cumsum(flat(dg + jnp.where(last_row, dgl, 0.0)), reverse=True)
        dfg = dlf / fg - flat(dk)
        dlb = jnp.sum(dfg * (1.0 - sig), axis=0, keepdims=True)
        dl0 = dlb * lb * (1.0 - lb)
        dlbl = jnp.concatenate([dl0, -dl0], axis=0)

        @pl.when(b == 0)
        def _():
            dlbl_ref[...] = dlbl

        @pl.when(b > 0)
        def _():
            dlbl_ref[...] += dlbl

        dp_ref[0] = (flat(dq) * HG_EXPAND ** -0.5 * _dsilu(pq)).astype(BF16)
        dp_ref[1] = (dfg * (1.0 - lb) * sig * (1.0 - sig)).astype(BF16)
        dp_ref[2] = flat(dv).astype(BF16)

    blk = pl.BlockSpec((t, HG_EXPAND), lambda h, b: (nb - 1 - b, h))
    pblk = pl.BlockSpec((4, t, HG_EXPAND), lambda h, b: (0, nb - 1 - b, h))
    return pl.pallas_call(
        body, name="hgrn2_bwd", grid=(heads, nb),
        in_specs=[pblk, pl.BlockSpec((2, HG_EXPAND), lambda h, b: (0, h)), pl.BlockSpec((1, HG_EXPAND), lambda h, b: (0, 0)),
                  blk, blk, pl.BlockSpec((None, nc, HG_EXPAND, HG_EXPAND), lambda h, b: (h, nb - 1 - b, 0, 0))],
        out_specs=[pblk, pl.BlockSpec((2, HG_EXPAND), lambda h, b: (0, h)), pl.BlockSpec((1, HG_EXPAND), lambda h, b: (0, 0))],
        out_shape=[jax.ShapeDtypeStruct((4, s, d), BF16), jax.ShapeDtypeStruct((2, d), F32), jax.ShapeDtypeStruct((1, HG_EXPAND), F32)],
        scratch_shapes=[pltpu.VMEM((HG_EXPAND, HG_EXPAND), F32), pltpu.VMEM((nc, 1, HG_EXPAND), F32),
                        pltpu.VMEM((nc, HG_EXPAND, HG_EXPAND), F32)],
        compiler_params=_params(("arbitrary", "arbitrary")),
    )(p, lb_logits, out_gain, o, dog, states)


HALO = 8


def _shift_down(xv, n):
    return pltpu.roll(xv, n, axis=0)


def _shift_up(xv, n):
    return pltpu.roll(xv, xv.shape[0] - n, axis=0)


def _ffn_hidden(u, conv_w, conv_b):
    _, nj, s, fb = u.shape
    tm = _row_tile(s)
    per = tm // HALO

    def body(gate_ref, prev_ref, val_ref, w_ref, b_ref, h_ref):
        i = pl.program_id(1)
        prev = jnp.where(i > 0, prev_ref[...].astype(F32), 0.0)
        ext = jnp.concatenate([prev, gate_ref[...].astype(F32)], axis=0)
        conv = b_ref[...] + w_ref[2:3, :] * ext[HALO:]
        conv = conv + w_ref[1:2, :] * _shift_down(ext, 1)[HALO:]
        conv = conv + w_ref[0:1, :] * _shift_down(ext, 2)[HALO:]
        h_ref[...] = (_silu(conv) * val_ref[...].astype(F32)).astype(BF16)

    return pl.pallas_call(
        body, name="ffn_hidden", grid=(nj, s // tm),
        in_specs=[pl.BlockSpec((None, None, tm, fb), lambda j, i: (0, j, i, 0)),
                  pl.BlockSpec((None, None, HALO, fb), lambda j, i: (0, j, jnp.maximum(i * per - 1, 0), 0)),
                  pl.BlockSpec((None, None, tm, fb), lambda j, i: (1, j, i, 0)),
                  pl.BlockSpec((None, CONV_WIDTH, fb), lambda j, i: (j, 0, 0)),
                  pl.BlockSpec((None, 1, fb), lambda j, i: (j, 0, 0))],
        out_specs=pl.BlockSpec((None, tm, fb), lambda j, i: (j, i, 0)),
        out_shape=jax.ShapeDtypeStruct((nj, s, fb), BF16), compiler_params=_params(("parallel", "parallel")),
    )(u, u, u, conv_w, conv_b)


def _ffn_hidden_bwd(u, dh, conv_w, conv_b):
    _, nj, s, fb = u.shape
    tm = _row_tile(s)
    per = tm // HALO
    nblk = s // HALO
    ni = s // tm

    def body(gate_ref, gprev_ref, gnext_ref, val_ref, vnext_ref, dh_ref, dhnext_ref, w_ref, b_ref, du_ref, dw_ref, db_ref):
        i = pl.program_id(1)
        has_next = i < ni - 1
        gprev = jnp.where(i > 0, gprev_ref[...].astype(F32), 0.0)
        gext = jnp.concatenate([gprev, gate_ref[...].astype(F32), gnext_ref[...].astype(F32)], axis=0)
        vext = jnp.concatenate([val_ref[...].astype(F32), vnext_ref[...].astype(F32)], axis=0)
        dhext = jnp.concatenate([dh_ref[...].astype(F32), jnp.where(has_next, dhnext_ref[...].astype(F32), 0.0)], axis=0)
        g0 = gext[HALO:]
        g1 = _shift_down(gext, 1)[HALO:]
        g2 = _shift_down(gext, 2)[HALO:]
        conv = b_ref[...] + w_ref[2:3, :] * g0 + w_ref[1:2, :] * g1 + w_ref[0:1, :] * g2
        dconv = dhext * vext * _dsilu(conv)
        dgate = w_ref[2:3, :] * dconv + w_ref[1:2, :] * _shift_up(dconv, 1) + w_ref[0:1, :] * _shift_up(dconv, 2)
        du_ref[0] = dgate[:tm].astype(BF16)
        du_ref[1] = (dhext * _silu(conv))[:tm].astype(BF16)
        own = dconv[:tm]
        dw = jnp.concatenate([jnp.sum(own * g2[:tm], axis=0, keepdims=True), jnp.sum(own * g1[:tm], axis=0, keepdims=True),
                              jnp.sum(own * g0[:tm], axis=0, keepdims=True)], axis=0)
        db = jnp.sum(own, axis=0, keepdims=True)

        @pl.when(i == 0)
        def _():
            dw_ref[...] = dw
            db_ref[...] = db

        @pl.when(i > 0)
        def _():
            dw_ref[...] += dw
            db_ref[...] += db

    def tile(part):
        return pl.BlockSpec((None, None, tm, fb), lambda j, i: (part, j, i, 0))

    def after(part):
        return pl.BlockSpec((None, None, HALO, fb), lambda j, i: (part, j, jnp.minimum((i + 1) * per, nblk - 1), 0))

    return pl.pallas_call(
        body, name="ffn_hidden_bwd", grid=(nj, ni),
        in_specs=[tile(0), pl.BlockSpec((None, None, HALO, fb), lambda j, i: (0, j, jnp.maximum(i * per - 1, 0), 0)), after(0),
                  tile(1), after(1),
                  pl.BlockSpec((None, tm, fb), lambda j, i: (j, i, 0)),
                  pl.BlockSpec((None, HALO, fb), lambda j, i: (j, jnp.minimum((i + 1) * per, nblk - 1), 0)),
                  pl.BlockSpec((None, CONV_WIDTH, fb), lambda j, i: (j, 0, 0)), pl.BlockSpec((None, 1, fb), lambda j, i: (j, 0, 0))],
        out_specs=[pl.BlockSpec((2, None, tm, fb), lambda j, i: (0, j, i, 0)),
                   pl.BlockSpec((None, CONV_WIDTH, fb), lambda j, i: (j, 0, 0)), pl.BlockSpec((None, 1, fb), lambda j, i: (j, 0, 0))],
        out_shape=[jax.ShapeDtypeStruct((2, nj, s, fb), BF16), jax.ShapeDtypeStruct((nj, CONV_WIDTH, fb), F32),
                   jax.ShapeDtypeStruct((nj, 1, fb), F32)],
        compiler_params=_params(("parallel", "arbitrary")),
    )(u, u, u, u, u, dh, dh, conv_w, conv_b)


ATT_TILE = 512


def _attn_probs(q, kb, sink, head, first, n_heads):
    iq = lax.broadcasted_iota(jnp.int32, (WINDOW, 2 * WINDOW), 0)
    ik = lax.broadcasted_iota(jnp.int32, (WINDOW, 2 * WINDOW), 1)
    dist = iq + WINDOW - ik
    valid = (dist >= 0) & (dist < WINDOW) & (ik >= jnp.where(first, WINDOW, 0))
    slope = 2.0 ** (-8.0 * (head + 1) / n_heads)
    sc = jnp.where(valid, _dot(q, kb, NT) * ATT_HEAD_DIM ** -0.5 - slope * dist.astype(F32), NEG)
    m = jnp.maximum(jnp.max(sc, axis=-1, keepdims=True), sink)
    e = jnp.exp(sc - m)
    es = jnp.exp(sink - m)
    inv = 1.0 / (jnp.sum(e, axis=-1, keepdims=True) + es)
    return e * inv, es * inv


def _attn_specs(s, d, kvd, tq):
    per = tq // WINDOW
    return [pl.BlockSpec((tq, d), lambda i: (i, 0)), pl.BlockSpec((tq, kvd), lambda i: (i, 0)),
            pl.BlockSpec((WINDOW, kvd), lambda i: (jnp.maximum(i * per - 1, 0), 0))]


def _attn_fwd(q, kv, sinks):
    s, d = q.shape
    kvd = kv.shape[1]
    half = kvd // 2
    hd = ATT_HEAD_DIM
    nq = d // hd
    group = nq // ATT_KV_HEADS
    tq = min(s, ATT_TILE)
    per = tq // WINDOW

    def body(q_ref, kvc_ref, kvp_ref, sink_ref, o_ref, band):
        i = pl.program_id(0)
        band[0:WINDOW, :] = kvp_ref[...]
        band[WINDOW:, :] = kvc_ref[...]

        def block(b, carry):
            rows = pl.ds(pl.multiple_of(b * WINDOW, WINDOW), WINDOW)
            keys = pl.ds(pl.multiple_of(b * WINDOW, WINDOW), 2 * WINDOW)
            first = (i * per + b) == 0
            for pair in range(nq // 2):
                outs = []
                for h in (2 * pair, 2 * pair + 1):
                    g = h // group
                    p, _ = _attn_probs(q_ref[rows, h * hd:(h + 1) * hd], band[keys, g * hd:(g + 1) * hd], sink_ref[0, h], h, first, nq)
                    outs.append(_dot(p, band[keys, half + g * hd:half + (g + 1) * hd]))
                o_ref[rows, pair * 2 * hd:(pair + 1) * 2 * hd] = jnp.concatenate(outs, axis=1).astype(BF16)
            return carry

        lax.fori_loop(0, per, block, 0)

    return pl.pallas_call(
        body, name="attn_fwd", grid=(s // tq,),
        in_specs=_attn_specs(s, d, kvd, tq) + [pl.BlockSpec(memory_space=pltpu.SMEM)],
        out_specs=pl.BlockSpec((tq, d), lambda i: (i, 0)), out_shape=jax.ShapeDtypeStruct((s, d), BF16),
        scratch_shapes=[pltpu.VMEM((tq + WINDOW, kvd), BF16)], compiler_params=_params(("parallel",)),
    )(q, kv, kv, sinks)


def _attn_bwd(q, kv, o, do, sinks):
    s, d = q.shape
    kvd = kv.shape[1]
    half = kvd // 2
    hd = ATT_HEAD_DIM
    nq = d // hd
    group = nq // ATT_KV_HEADS
    tq = min(s, ATT_TILE)
    per = tq // WINDOW
    nt = s // tq

    def body(q_ref, kvc_ref, kvp_ref, o_ref, do_ref, sink_ref, dq_ref, dkvc_ref, dkvp_ref, ds_ref, band, dband):
        i = pl.program_id(0)
        band[0:WINDOW, :] = kvp_ref[...]
        band[WINDOW:, :] = kvc_ref[...]
        dband[...] = jnp.zeros_like(dband)
        ds_ref[...] = jnp.zeros_like(ds_ref)

        def block(b, carry):
            rows = pl.ds(pl.multiple_of(b * WINDOW, WINDOW), WINDOW)
            keys = pl.ds(pl.multiple_of(b * WINDOW, WINDOW), 2 * WINDOW)
            first = (i * per + b) == 0
            dks, dvs = [], []
            for g in range(ATT_KV_HEADS):
                kb = band[keys, g * hd:(g + 1) * hd]
                vb = band[keys, half + g * hd:half + (g + 1) * hd]
                dk = jnp.zeros((2 * WINDOW, hd), F32)
                dv = jnp.zeros((2 * WINDOW, hd), F32)
                dqs = []
                for j in range(group):
                    h = g * group + j
                    cols = slice(h * hd, (h + 1) * hd)
                    qv, dov = q_ref[rows, cols], do_ref[rows, cols]
                    p, ps = _attn_probs(qv, kb, sink_ref[0, h], h, first, nq)
                    dsum = jnp.sum(dov.astype(F32) * o_ref[rows, cols].astype(F32), axis=-1, keepdims=True)
                    dsc = p * (_dot(dov, vb, NT) - dsum) * ATT_HEAD_DIM ** -0.5
                    dqs.append(_dot(dsc, kb))
                    dk = dk + _dot(dsc, qv, TN)
                    dv = dv + _dot(p, dov, TN)
                    ds_ref[h:h + 1, :] += jnp.zeros((1, 128), F32) - jnp.sum(ps * dsum)
                    if j % 2 == 1:
                        dq_ref[rows, (h - 1) * hd:(h + 1) * hd] = jnp.concatenate(dqs[-2:], axis=1).astype(BF16)
                dks.append(dk)
                dvs.append(dv)
            dband[keys, 0:half] += jnp.concatenate(dks, axis=1)
            dband[keys, half:] += jnp.concatenate(dvs, axis=1)
            return carry

        lax.fori_loop(0, per, block, 0)
        dkvp_ref[...] = dband[0:WINDOW, :]
        dkvc_ref[...] = dband[WINDOW:, :]

    big = pl.BlockSpec((tq, d), lambda i: (i, 0))
    return pl.pallas_call(
        body, name="attn_bwd", grid=(nt,),
        in_specs=_attn_specs(s, d, kvd, tq) + [big, big, pl.BlockSpec(memory_space=pltpu.SMEM)],
        out_specs=[big, pl.BlockSpec((tq, kvd), lambda i: (i, 0)), pl.BlockSpec((None, WINDOW, kvd), lambda i: (i, 0, 0)),
                   pl.BlockSpec((None, nq, 128), lambda i: (i, 0, 0))],
        out_shape=[jax.ShapeDtypeStruct((s, d), BF16), jax.ShapeDtypeStruct((s, kvd), F32), jax.ShapeDtypeStruct((nt, WINDOW, kvd), F32),
                   jax.ShapeDtypeStruct((nt, nq, 128), F32)],
        scratch_shapes=[pltpu.VMEM((tq + WINDOW, kvd), BF16), pltpu.VMEM((tq + WINDOW, kvd), F32)],
        compiler_params=_params(("parallel",)),
    )(q, kv, kv, o, do, sinks)


HBM_SPEC = pl.BlockSpec(memory_space=pltpu.HBM)
VMEM_SPEC = pl.BlockSpec(memory_space=pltpu.VMEM)


def _place():
    return lax.axis_index("x"), lax.axis_index("y"), lax.axis_index("c")


def _flip(pos, r):
    return tuple(1 - p if (r >> (2 - a)) & 1 else p for a, p in enumerate(pos))


def _index(pos):
    return 4 * pos[0] + 2 * pos[1] + pos[2]


def _all_gather(name, shards, spec):
    n = len(shards)

    def body(*refs):
        x_refs, o_refs = refs[:n], refs[n:2 * n]
        send_sems, recv_sems, local_sems = refs[2 * n:]
        me = _place()
        sibling = _flip(me, 1)
        far = [_flip(me, r) for r in (4, 2, 6)]

        def copy(t, sem, block, to, src=None):
            rows = o_refs[t].at[_index(block)]
            return pltpu.make_async_remote_copy(
                src_ref=rows if src is None else src, dst_ref=rows, send_sem=send_sems.at[t, sem], recv_sem=recv_sems.at[t, sem],
                device_id=to, device_id_type=MESH)

        own = [pltpu.make_async_copy(x_refs[t], o_refs[t].at[_index(me)], local_sems.at[t]) for t in range(n)]
        for cp in own:
            cp.start()
        first = []
        for t in range(n):
            first.append(copy(t, 0, me, sibling, src=x_refs[t]))
            first += [copy(t, 1 + j, me, peer, src=x_refs[t]) for j, peer in enumerate(far)]
        for cp in first:
            cp.start()
        passed = []
        for j, peer in enumerate(far):
            for t in range(n):
                copy(t, 1 + j, peer, me).wait_recv()
                cp = copy(t, 4 + j, peer, sibling)
                cp.start()
                passed.append(cp)
        for t in range(n):
            copy(t, 0, sibling, me).wait_recv()
            for j, peer in enumerate(far):
                copy(t, 4 + j, _flip(peer, 1), me).wait_recv()
        for cp in first + passed:
            cp.wait_send()
        for cp in own:
            cp.wait()

    return pl.pallas_call(
        body, name=name, in_specs=[spec] * n, out_specs=[spec] * n,
        out_shape=[jax.ShapeDtypeStruct((N_DEV,) + sh.shape, sh.dtype) for sh in shards],
        scratch_shapes=[pltpu.SemaphoreType.DMA((n, 7)), pltpu.SemaphoreType.DMA((n, 7)), pltpu.SemaphoreType.DMA((n,))],
    )(*shards)


SEM_SPEC = pl.BlockSpec(memory_space=pltpu.SEMAPHORE)
ANY_SPEC = pl.BlockSpec(memory_space=pl.ANY)


def _landing(own, mine):
    return lax.dynamic_update_slice(lax.empty((N_DEV,) + own.shape, own.dtype), own[None], (mine,) + (0,) * own.ndim)


def _pinned(a, token):
    return a + token[0:1, 0:1].astype(a.dtype)


def _peer_copies(src_refs, land_refs, send_sems, recv_sems, scatter, arrivals):
    me = _place()
    mine = _index(me)
    copies = []
    for t, (src, land) in enumerate(zip(src_refs, land_refs)):
        for r in range(1, N_DEV):
            peer = _flip(me, r)
            theirs = _index(peer)
            sem = t * (N_DEV - 1) + r - 1
            copies.append(pltpu.make_async_remote_copy(
                src_ref=src.at[theirs] if scatter else src, dst_ref=land.at[theirs if arrivals else mine],
                send_sem=send_sems.at[sem], recv_sem=recv_sems.at[sem], device_id=peer, device_id_type=MESH))
    return copies


def _send_start(name, sources, lands, scatter, after=None):
    n = len(sources)
    extra = 0 if after is None else 1

    def body(*refs):
        outs = refs[2 * n + extra:]
        for out in _peer_copies(refs[:n], refs[n:2 * n], outs[0], outs[1], scatter, False):
            out.start()
        outs[-1][...] = jnp.zeros_like(outs[-1])

    outs = pl.pallas_call(
        body, name=name, in_specs=[HBM_SPEC] * (2 * n) + [ANY_SPEC] * extra,
        out_specs=[SEM_SPEC, SEM_SPEC] + [HBM_SPEC] * (2 * n) + [VMEM_SPEC],
        out_shape=[pltpu.SemaphoreType.DMA((n * (N_DEV - 1),)), pltpu.SemaphoreType.DMA((n * (N_DEV - 1),))]
        + [pltpu.HBM(a.shape, a.dtype) for a in list(sources) + list(lands)] + [jax.ShapeDtypeStruct((8, 128), F32)],
        input_output_aliases={i: 2 + i for i in range(2 * n)},
        compiler_params=pltpu.CompilerParams(has_side_effects=pltpu.SideEffectType.DATAFLOW_SIDE_EFFECTING),
    )(*[pltpu.with_memory_space_constraint(a, pltpu.HBM) for a in list(sources) + list(lands)], *([] if after is None else [after]))
    return outs[0], outs[1], outs[2:2 + n], outs[2 + n:2 + 2 * n], outs[-1]


def _send_wait(name, started, after, scatter):
    send_sems, recv_sems, sources, lands, _ = started
    n = len(sources)

    def body(*refs):
        for out in _peer_copies(refs[:n], refs[n:2 * n], refs[2 * n], refs[2 * n + 1], scatter, False):
            out.wait_send()
        for arrival in _peer_copies(refs[:n], refs[n:2 * n], refs[2 * n], refs[2 * n + 1], scatter, True):
            arrival.wait_recv()

    outs = pl.pallas_call(
        body, name=name, in_specs=[HBM_SPEC] * (2 * n) + [SEM_SPEC, SEM_SPEC, ANY_SPEC], out_specs=[HBM_SPEC] * (2 * n),
        out_shape=[pltpu.HBM(a.shape, a.dtype) for a in list(sources) + list(lands)],
        input_output_aliases={i: i for i in range(2 * n)},
        compiler_params=pltpu.CompilerParams(has_side_effects=pltpu.SideEffectType.DATAFLOW_SIDE_EFFECTING),
    )(*sources, *lands, send_sems, recv_sems, after)
    return outs[n:]


def _pack_rows(parts):
    offsets, row = [], 0
    for part in parts:
        offsets.append(row)
        row += part.shape[0]
    return offsets, -(-row // 8) * 8, -(-max(part.shape[1] for part in parts) // 128) * 128


def _pack(name, parts):
    offsets, rows, width = _pack_rows(parts)

    def body(*refs):
        o_ref = refs[-1]
        o_ref[...] = jnp.zeros_like(o_ref)
        for off, ref in zip(offsets, refs[:-1]):
            o_ref[off:off + ref.shape[0], 0:ref.shape[1]] = ref[...]

    return pl.pallas_call(body, name=name, in_specs=[VMEM_SPEC] * len(parts), out_specs=VMEM_SPEC,
                          out_shape=jax.ShapeDtypeStruct((rows, width), F32))(*parts)


def _adamw_math(w, g, m, v):
    m = ADAM_B1 * m + (1.0 - ADAM_B1) * g
    v = ADAM_B2 * v + (1.0 - ADAM_B2) * (g * g)
    m_hat = m / (1.0 - ADAM_B1 ** ADAM_STEP)
    v_hat = v / (1.0 - ADAM_B2 ** ADAM_STEP)
    return -ADAM_LR * (m_hat / (jnp.sqrt(v_hat) + ADAM_EPS) + ADAM_WD * w), m, v


def _adamw_shard(name, w, m, v, partials, layer=None, into=None):
    rows, cols = w.shape[-2:]
    tr = max(t for t in range(8, min(rows, 256) + 1, 8) if rows % t == 0)

    def body(w_ref, m_ref, v_ref, p_ref, *rest):
        g_ref, d_ref, nm_ref, nv_ref = rest[-4:]
        g = p_ref[0].astype(F32)
        for dev in range(1, N_DEV):
            g = g + p_ref[dev].astype(F32)
        g_ref[...] = g
        d_ref[...], nm_ref[...], nv_ref[...] = _adamw_math(w_ref[...], g, m_ref[...], v_ref[...])

    blk = pl.BlockSpec((tr, cols), lambda i: (i, 0)) if layer is None else pl.BlockSpec((None, tr, cols), lambda i: (layer, i, 0))
    kept = [] if into is None else list(into)
    return pl.pallas_call(
        body, name=name, grid=(rows // tr,),
        in_specs=[blk, blk, blk, pl.BlockSpec((N_DEV, tr, cols), lambda i: (0, i, 0))] + [ANY_SPEC] * len(kept),
        out_specs=[blk] * 4, out_shape=[jax.ShapeDtypeStruct(w.shape, F32)] * 4,
        input_output_aliases={4 + i: i for i in range(len(kept))}, compiler_params=_params(("parallel",)),
    )(w, m, v, partials, *kept)


def _adamw_small(gathered, places, entries):
    n = len(entries)
    np_ = len(gathered)

    def body(*refs):
        pack_refs = refs[:np_]
        refs = refs[np_ - 1:]
        w_refs, m_refs, v_refs = refs[1:1 + n], refs[1 + n:1 + 2 * n], refs[1 + 2 * n:1 + 3 * n]
        outs = refs[1 + 3 * n:]
        totals = []
        for pack_ref in pack_refs:
            acc = pack_ref[0]
            for dev in range(1, N_DEV):
                acc = acc + pack_ref[dev]
            totals.append(acc)
        mine = _index(_place())
        for e in range(n):
            rows, cols = w_refs[e].shape
            total, off = totals[places[e][0]], places[e][1]
            if entries[e][3]:
                g = jnp.zeros((rows, cols), F32)
                for dev in range(N_DEV):
                    g = g + jnp.where(mine == dev, total[off + dev * rows:off + (dev + 1) * rows, 0:cols], 0.0)
            else:
                g = total[off:off + rows, 0:cols]
            outs[4 * e][...] = g
            outs[4 * e + 1][...], outs[4 * e + 2][...], outs[4 * e + 3][...] = _adamw_math(w_refs[e][...], g, m_refs[e][...], v_refs[e][...])
        outs[4 * n][...] = totals[places[n][0]][places[n][1]:places[n][1] + 1, 0:128]

    shapes = []
    for w, _, _, _ in entries:
        shapes += [jax.ShapeDtypeStruct(w.shape, F32)] * 4
    shapes.append(jax.ShapeDtypeStruct((1, 128), F32))
    return pl.pallas_call(
        body, name="adamw_small", in_specs=[VMEM_SPEC] * (np_ + 3 * n), out_specs=[VMEM_SPEC] * len(shapes), out_shape=shapes,
        compiler_params=pltpu.CompilerParams(vmem_limit_bytes=VMEM_LIMIT),
    )(*gathered, *[e[0] for e in entries], *[e[1] for e in entries], *[e[2] for e in entries])


def _ffn_forward(tag, h, gain, w_up, w_down, conv_w, conv_b, after_up=None):
    s, d = h.shape
    fb = w_up.shape[2]
    tm = _row_tile(s, MM_ROWS)
    a, = _rmsnorm_cast(f"ffn_norm_{tag}", h, [gain])
    u = _matmul(
        f"ffn_up_{tag}", a, w_up, dims=NN, grid=(s // tm, N_DEV, 1),
        a_spec=pl.BlockSpec((tm, d), lambda i, j, k: (i, 0)),
        b_spec=pl.BlockSpec((None, d, fb), lambda i, j, k: (j, 0, 0)),
        o_spec=pl.BlockSpec((None, None, tm, fb), lambda i, j, k: (j // 4, j % 4, i, 0)),
        out_shape=jax.ShapeDtypeStruct((2, 4, s, fb), BF16))
    if after_up is not None:
        conv_b = _pinned(conv_b, after_up(u))
    hidden = _ffn_hidden(u, conv_w, conv_b)
    th = _row_tile(s, MM_ROWS // 2)
    out = _matmul(
        f"ffn_down_{tag}", hidden, w_down, dims=NN, grid=(s // th, 1, 1),
        a_spec=pl.BlockSpec((4, th, fb), lambda i, j, k: (0, i, 0)),
        b_spec=pl.BlockSpec((4, fb, d), lambda i, j, k: (0, 0, 0)),
        o_spec=pl.BlockSpec((th, d), lambda i, j, k: (i, 0)),
        out_shape=jax.ShapeDtypeStruct((s, d), F32),
        add=h, add_spec=pl.BlockSpec((th, d), lambda i, j, k: (i, 0)),
        terms=lambda a_ref, b_ref: [(a_ref[k], b_ref[k]) for k in range(4)])
    return out, (a, u, hidden)


def _ffn_backward(tag, h, gain, w_up, w_down, conv_w, conv_b, saved, dout):
    a, u, hidden = saved
    dout, dout_bf = dout
    s, d = h.shape
    fb = w_up.shape[2]
    tm = _row_tile(s, MM_ROWS)
    dhidden = _matmul(
        f"ffn_down_bwd_{tag}", dout_bf, w_down, dims=NT, grid=(s // tm, 4, 1),
        a_spec=pl.BlockSpec((tm, d), lambda i, j, k: (i, 0)),
        b_spec=pl.BlockSpec((None, fb, d), lambda i, j, k: (j, 0, 0)),
        o_spec=pl.BlockSpec((None, tm, fb), lambda i, j, k: (j, i, 0)),
        out_shape=jax.ShapeDtypeStruct((4, s, fb), BF16))
    dw_down = _matmul(
        f"ffn_down_grad_{tag}", hidden, dout_bf, dims=TN, grid=(4, 1, 1),
        a_spec=pl.BlockSpec((None, s, fb), lambda i, j, k: (i, 0, 0)),
        b_spec=pl.BlockSpec((s, d), lambda i, j, k: (0, 0)),
        o_spec=pl.BlockSpec((None, fb, d), lambda i, j, k: (i, 0, 0)),
        out_shape=jax.ShapeDtypeStruct((4, fb, d), BF16))
    du, dconv_w, dconv_b = _ffn_hidden_bwd(u, dhidden, conv_w, conv_b)
    th = _row_tile(s, MM_ROWS // 2)
    da = _matmul(
        f"ffn_up_bwd_{tag}", du, w_up, dims=NT, grid=(s // th, 1, 1),
        a_spec=pl.BlockSpec((2, 4, th, fb), lambda i, j, k: (0, 0, i, 0)),
        b_spec=pl.BlockSpec((N_DEV, d, fb), lambda i, j, k: (0, 0, 0)),
        o_spec=pl.BlockSpec((th, d), lambda i, j, k: (i, 0)),
        out_shape=jax.ShapeDtypeStruct((s, d), F32),
        terms=lambda a_ref, b_ref: [(a_ref[k // 4, k % 4], b_ref[k]) for k in range(N_DEV)])
    dw_up = _matmul(
        f"ffn_up_grad_{tag}", a, du, dims=TN, grid=(1, N_DEV, 1),
        a_spec=pl.BlockSpec((s, d), lambda i, j, k: (0, 0)),
        b_spec=pl.BlockSpec((None, None, s, fb), lambda i, j, k: (j // 4, j % 4, 0, 0)),
        o_spec=pl.BlockSpec((None, d, fb), lambda i, j, k: (j, 0, 0)),
        out_shape=jax.ShapeDtypeStruct((N_DEV, d, fb), BF16))
    dh, (dgain,) = _rmsnorm_bwd(f"ffn_norm_bwd_{tag}", h, dout, [(da, gain)])
    return dh, dgain, dw_up, dw_down, dconv_w, dconv_b


def kernel(x, hg_norm, hg_w_in, hg_lb_logits, hg_out_norm, hg_w_out, kv_norm, w_kv, attn_norm, attn_w_q, attn_sinks, attn_w_o, ffn_norm, ffn_w_up, ffn_conv_w, ffn_conv_b, ffn_w_down, final_norm, loss_target, m_hg_norm, m_hg_w_in, m_hg_lb_logits, m_hg_out_norm, m_hg_w_out, m_kv_norm, m_w_kv, m_attn_norm, m_attn_w_q, m_attn_sinks, m_attn_w_o, m_ffn_norm, m_ffn_w_up, m_ffn_conv_w, m_ffn_conv_b, m_ffn_w_down, m_final_norm, v_hg_norm, v_hg_w_in, v_hg_lb_logits, v_hg_out_norm, v_hg_w_out, v_kv_norm, v_w_kv, v_attn_norm, v_attn_w_q, v_attn_sinks, v_attn_w_o, v_ffn_norm, v_ffn_w_up, v_ffn_conv_w, v_ffn_conv_b, v_ffn_w_down, v_final_norm):
    _, s, d = x.shape
    x0, target = x[0], loss_target[0]
    half = hg_w_in.shape[2]
    fs = ffn_conv_w.shape[2]
    fb = 2 * fs
    kvd = w_kv.shape[1]
    nq = d // ATT_HEAD_DIM
    tm = _row_tile(s, MM_ROWS)

    mine = _index(_place())
    gather = lambda tag, shards, after: _send_start("gather_start_" + tag, shards, [_landing(a, mine) for a in shards], False, after)
    w_in, g_hgn, g_lbl, w_out = _all_gather("gather_hg", [hg_w_in[0].astype(BF16), hg_norm, hg_lb_logits, hg_w_out[0].astype(BF16)], HBM_SPEC)
    w_out = w_out.reshape(d, d)
    coming_ffn0 = gather("ffn0", [ffn_w_up[0].astype(BF16), ffn_conv_w, ffn_w_down[0].astype(BF16)], g_hgn)
    hgn = _pinned(g_hgn.reshape(1, d), coming_ffn0[4])
    lbl = g_lbl.transpose(1, 0, 2).reshape(2, d)
    conv_b = [ffn_conv_b[layer].reshape(4, 1, fb) for layer in range(2)]
    gains = [ffn_norm[0:1], ffn_norm[1:2]]
    kvn, fin = kv_norm.reshape(1, d), final_norm.reshape(1, d)

    a0, = _rmsnorm_cast("hg_norm", x0, [hgn])
    p = _matmul(
        "hg_in", a0, w_in, dims=NN, grid=(s // tm, N_DEV, 1),
        a_spec=pl.BlockSpec((tm, d), lambda i, j, k: (i, 0)),
        b_spec=pl.BlockSpec((None, d, half), lambda i, j, k: (j, 0, 0)),
        o_spec=pl.BlockSpec((None, tm, half), lambda i, j, k: (j // 2, i, j % 2)),
        out_shape=jax.ShapeDtypeStruct((4, s, d), BF16), acc_shape=(8, 128))
    o, og, states = _hgrn2_fwd(p, lbl, hg_out_norm)
    x1 = _mm_rows("hg_out", og, w_out, out_dtype=F32, add=x0)
    w_up0, g_cw, w_dn0 = _send_wait("gather_wait_ffn0", coming_ffn0, x1, False)
    w_up, w_dn = [w_up0, None], [w_dn0.reshape(4, fb, d), None]
    conv_w = [g_cw[:, layer].reshape(4, 2, CONV_WIDTH, fs).transpose(0, 2, 1, 3).reshape(4, CONV_WIDTH, fb) for layer in range(2)]
    coming_attn = gather("attn", [w_kv.astype(BF16), attn_w_q[0].astype(BF16), attn_w_o[0].astype(BF16)], w_dn0)
    gains[0] = _pinned(gains[0], coming_attn[4])
    coming = []

    def start_ffn1(u):
        coming.append(gather("ffn1", [ffn_w_up[1].astype(BF16), ffn_w_down[1].astype(BF16)], u))
        return coming[0][4]

    x2, saved0 = _ffn_forward("0", x1, gains[0], w_up[0], w_dn[0], conv_w[0], conv_b[0], start_ffn1)
    coming_ffn1 = coming[0]
    w_kvg, w_q, w_o = _send_wait("gather_wait_attn", coming_attn, x2, False)
    w_kvg, w_q, w_o = w_kvg.reshape(d, kvd), w_q.reshape(d, d), w_o.reshape(d, d)
    akv, a2 = _rmsnorm_cast("attn_norms", x2, [kvn, attn_norm])
    kv = _mm_rows("kv_proj", akv, w_kvg, out_dtype=BF16)
    q = _mm_rows("q_proj", a2, w_q, out_dtype=BF16)
    att = _attn_fwd(q, kv, attn_sinks)
    x3 = _mm_rows("attn_out", att, w_o, out_dtype=F32, add=x2)
    w_up[1], w_dn1 = _send_wait("gather_wait_ffn1", coming_ffn1, x3, False)
    w_dn[1] = w_dn1.reshape(4, fb, d)
    x4, saved1 = _ffn_forward("1", x3, gains[1], w_up[1], w_dn[1], conv_w[1], conv_b[1])
    dx4, dx4_bf, d_fin, loss_part = _loss_head(x4, fin, target)

    dx3, d_fn1, dw_up1, dw_dn1, dcw1, dcb1 = _ffn_backward("1", x3, gains[1], w_up[1], w_dn[1], conv_w[1], conv_b[1], saved1, (dx4, dx4_bf))
    rows = d // N_DEV
    scatter = lambda tag, stacks: _send_start("scatter_start_" + tag, stacks, [_landing(lax.dynamic_index_in_dim(a, mine, keepdims=False), mine) for a in stacks], True)
    going_ffn1 = scatter("ffn1", [dw_up1, dw_dn1.reshape(N_DEV, fs, d)])
    datt = _mm_rows_nt("attn_out_bwd", dx3[1], w_o, out_dtype=BF16)
    dw_o = _mm_tn("attn_out_grad", att, dx3[1])
    dq, dkv_own, dkv_before, dsink = _attn_bwd(q, kv, att, datt, _pinned(attn_sinks, going_ffn1[4]))
    tiles = dkv_before.shape[0]
    dkv = dkv_own.reshape(tiles, s // tiles, kvd)
    dkv = jnp.concatenate([dkv[:, :-WINDOW], dkv[:, -WINDOW:] + jnp.pad(dkv_before[1:], ((0, 1), (0, 0), (0, 0)))], axis=1).reshape(s, kvd)
    da2 = _mm_rows_nt("q_proj_bwd", dq, w_q, out_dtype=F32)
    dw_q = _mm_tn("q_proj_grad", a2, dq)
    dakv = _mm_rows_nt("kv_proj_bwd", dkv, w_kvg, out_dtype=F32)
    dw_kv = _mm_tn("kv_proj_grad", akv, dkv)
    going_attn = scatter("attn", [dw_kv.reshape(N_DEV, rows, kvd), dw_q.reshape(N_DEV, rows, d), dw_o.reshape(N_DEV, rows, d)])
    dx2, (d_kvn, d_attn) = _rmsnorm_bwd("attn_norms_bwd", x2, dx3[0], [(dakv, _pinned(kvn, going_attn[4])), (da2, attn_norm)])
    dx1, d_fn0, dw_up0, dw_dn0, dcw0, dcb0 = _ffn_backward("0", x1, gains[0], w_up[0], w_dn[0], conv_w[0], conv_b[0], saved0, dx2)
    going_ffn0 = scatter("ffn0", [dw_up0, dw_dn0.reshape(N_DEV, fs, d)])
    dog = _mm_rows_nt("hg_out_bwd", dx1[1], w_out, out_dtype=F32)
    dw_out = _mm_tn("hg_out_grad", og, dx1[1])
    dp, d_lbl, d_ogain = _hgrn2_bwd(p, lbl, _pinned(hg_out_norm, going_ffn0[4]), o, dog, states)
    dw_in = _matmul(
        "hg_in_grad", a0, dp, dims=TN, grid=(1, N_DEV, 1),
        a_spec=pl.BlockSpec((s, d), lambda i, j, k: (0, 0)),
        b_spec=pl.BlockSpec((None, s, half), lambda i, j, k: (j // 2, 0, j % 2)),
        o_spec=pl.BlockSpec((None, d, half), lambda i, j, k: (j, 0, 0)),
        out_shape=jax.ShapeDtypeStruct((N_DEV, d, half), BF16))
    going_hg = scatter("hg", [dw_in, dw_out.reshape(N_DEV, rows, d)])
    th = _row_tile(s, MM_ROWS // 2)
    da0 = _matmul(
        "hg_in_bwd", dp, w_in, dims=NT, grid=(s // th, 1, 1),
        a_spec=pl.BlockSpec((4, th, d), lambda i, j, k: (0, i, 0)),
        b_spec=pl.BlockSpec((N_DEV, d, half), lambda i, j, k: (0, 0, 0)),
        o_spec=pl.BlockSpec((th, d), lambda i, j, k: (i, 0)),
        out_shape=jax.ShapeDtypeStruct((s, d), F32),
        terms=lambda a_ref, b_ref: [(a_ref[k // 2, :, (k % 2) * half:(k % 2 + 1) * half], b_ref[k]) for k in range(N_DEV)])
    (dx0, _), (d_hgn,) = _rmsnorm_bwd("hg_norm_bwd", x0, dx1[0], [(da0, _pinned(hgn, going_hg[4]))])

    arrive = lambda tag, going: _send_wait("scatter_wait_" + tag, going, dx0, True)
    (l_up1, l_dn1), (l_kv, l_q, l_o), (l_up0, l_dn0), (l_in, l_out) = (
        arrive("ffn1", going_ffn1), arrive("attn", going_attn), arrive("ffn0", going_ffn0), arrive("hg", going_hg))
    landed = [l_in, l_out, l_kv, l_q, l_o, l_up0, l_up1, l_dn0, l_dn1]
    big = {}
    for tag, w, m, v, part in [
            ("hg_w_in", hg_w_in[0], m_hg_w_in[0], v_hg_w_in[0], landed[0]), ("hg_w_out", hg_w_out[0], m_hg_w_out[0], v_hg_w_out[0], landed[1]),
            ("w_kv", w_kv, m_w_kv, v_w_kv, landed[2]), ("attn_w_q", attn_w_q[0], m_attn_w_q[0], v_attn_w_q[0], landed[3]),
            ("attn_w_o", attn_w_o[0], m_attn_w_o[0], v_attn_w_o[0], landed[4])]:
        big[tag] = _adamw_shard("adamw_" + tag, w, m, v, part)
    for tag, w, m, v, parts in [("ffn_w_up", ffn_w_up, m_ffn_w_up, v_ffn_w_up, landed[5:7]), ("ffn_w_down", ffn_w_down, m_ffn_w_down, v_ffn_w_down, landed[7:9])]:
        first = _adamw_shard("adamw_" + tag + "0", w, m, v, parts[0], layer=0)
        big[tag] = _adamw_shard("adamw_" + tag + "1", w, m, v, parts[1], layer=1, into=first)
    lead = lambda tag: [a[None] for a in big[tag]]

    as_blocks = lambda a, r: a.reshape(r, N_DEV, -1).transpose(1, 0, 2).reshape(N_DEV * r, -1)
    d_cw = jnp.concatenate([g.transpose(1, 0, 2).reshape(CONV_WIDTH, 4 * fb) for g in (dcw0, dcw1)], axis=0)
    parts = [d_fin, jnp.concatenate([d_fn0, d_fn1], axis=0), jnp.concatenate([dcb0.reshape(1, 4 * fb), dcb1.reshape(1, 4 * fb)], axis=0),
             as_blocks(d_cw, 2 * CONV_WIDTH), d_attn, jnp.sum(dsink[:, :, 0], axis=0).reshape(1, nq), d_kvn, d_ogain,
             as_blocks(d_hgn, 1), as_blocks(d_lbl, 2), loss_part]
    wide = [2]
    packs = [[parts[i] for i in wide], [part for i, part in enumerate(parts) if i not in wide]]
    places = [None] * len(parts)
    for which, members in enumerate([wide, [i for i in range(len(parts)) if i not in wide]]):
        for i, off in zip(members, _pack_rows(packs[which])[0]):
            places[i] = (which, off)
    gathered = _all_gather("gather_small_grads", [_pack("pack_wide_grads", packs[0]), _pack("pack_narrow_grads", packs[1])], VMEM_SPEC)
    two = lambda a: a.reshape(-1, a.shape[-1])
    small = [(fin, m_final_norm.reshape(1, d), v_final_norm.reshape(1, d), False), (ffn_norm, m_ffn_norm, v_ffn_norm, False),
             (ffn_conv_b, m_ffn_conv_b, v_ffn_conv_b, False), (two(ffn_conv_w), two(m_ffn_conv_w), two(v_ffn_conv_w), True),
             (attn_norm, m_attn_norm, v_attn_norm, False), (attn_sinks, m_attn_sinks, v_attn_sinks, False),
             (kvn, m_kv_norm.reshape(1, d), v_kv_norm.reshape(1, d), False), (hg_out_norm, m_hg_out_norm, v_hg_out_norm, False),
             (hg_norm, m_hg_norm, v_hg_norm, True), (hg_lb_logits, m_hg_lb_logits, v_hg_lb_logits, True)]
    res = _adamw_small(gathered, places, small)
    names = ["final_norm", "ffn_norm", "ffn_conv_b", "ffn_conv_w", "attn_norm", "attn_sinks", "kv_norm", "hg_out_norm", "hg_norm", "hg_lb_logits"]
    shapes = {"final_norm": final_norm.shape, "kv_norm": kv_norm.shape, "ffn_conv_w": ffn_conv_w.shape}
    out = {n: [a.reshape(shapes[n]) if n in shapes else a for a in res[4 * i:4 * i + 4]] for i, n in enumerate(names)}
    out.update(hg_w_in=lead("hg_w_in"), hg_w_out=lead("hg_w_out"), w_kv=big["w_kv"], attn_w_q=lead("attn_w_q"), attn_w_o=lead("attn_w_o"),
               ffn_w_up=big["ffn_w_up"], ffn_w_down=big["ffn_w_down"])
    order = ["hg_norm", "hg_w_in", "hg_lb_logits", "hg_out_norm", "hg_w_out", "kv_norm", "w_kv", "attn_norm", "attn_w_q", "attn_sinks",
             "attn_w_o", "ffn_norm", "ffn_w_up", "ffn_conv_w", "ffn_conv_b", "ffn_w_down", "final_norm"]
    loss = res[-1][0, 0]
    return (loss, dx0[None], *[out[n][0] for n in order], *[out[n][1] for n in order], *[out[n][2] for n in order], *[out[n][3] for n in order])
```

```python
import functools
import math

import jax
import jax.numpy as jnp
from jax import lax
from jax.experimental import pallas as pl
from jax.experimental.pallas import tpu as pltpu

F32 = jnp.float32
BF16 = jnp.bfloat16

EPS = 1e-6
HG_EXPAND = 128
HG_CHUNK = 32
ATT_HEAD_DIM = 64
ATT_KV_HEADS = 2
WINDOW = 128
CONV_WIDTH = 3
ADAM_LR = 0.001
ADAM_B1 = 0.9
ADAM_B2 = 0.999
ADAM_EPS = 1e-08
ADAM_WD = 0.01
ADAM_STEP = 10

N_DEV = 8
VMEM_LIMIT = 48 * 1024 * 1024
NEG = -1e30

NN = (((1,), (0,)), ((), ()))
NT = (((1,), (1,)), ((), ()))
TN = (((0,), (0,)), ((), ()))
MESH = pl.DeviceIdType.MESH


def _dot(a, b, dims=NN):
    return lax.dot_general(a.astype(BF16), b.astype(BF16), dims, preferred_element_type=F32)


def _sigmoid(x):
    return 0.5 * jnp.tanh(0.5 * x) + 0.5


def _silu(x):
    return x * _sigmoid(x)


def _silu_and_grad(x):
    s = _sigmoid(x)
    return x * s, s * (1.0 + x * (1.0 - s))


def _dsilu(x):
    return _silu_and_grad(x)[1]


def _params(semantics):
    return pltpu.CompilerParams(dimension_semantics=semantics, vmem_limit_bytes=VMEM_LIMIT)


def _row_tile(rows, want=512):
    return min(rows, want)


MM_ROWS = 1024


def _matmul(name, a, b, *, dims, grid, a_spec, b_spec, o_spec, out_shape, acc_shape=(8, 128), add=None, add_spec=None, terms=None):
    nk = grid[2]

    def body(*refs):
        if add is None:
            a_ref, b_ref, o_ref, acc = refs
        else:
            a_ref, b_ref, add_ref, o_ref, acc = refs
        k = pl.program_id(2)
        pairs = [(a_ref[...], b_ref[...])] if terms is None else terms(a_ref, b_ref)
        part = _dot(*pairs[0], dims)
        for pair in pairs[1:]:
            part = part + _dot(*pair, dims)

        def finish(total):
            if add is not None:
                total = total + add_ref[...]
            o_ref[...] = total.astype(o_ref.dtype)

        if nk == 1:
            finish(part)
        else:
            @pl.when(k == 0)
            def _():
                acc[...] = part

            @pl.when(k > 0)
            def _():
                acc[...] += part

            @pl.when(k == nk - 1)
            def _():
                finish(acc[...])

    in_specs = [a_spec, b_spec] + ([] if add is None else [add_spec])
    args = (a, b) + (() if add is None else (add,))
    return pl.pallas_call(
        body, name=name, grid=grid, in_specs=in_specs, out_specs=o_spec, out_shape=out_shape,
        scratch_shapes=[pltpu.VMEM(acc_shape, F32)],
        compiler_params=_params(("parallel", "parallel", "arbitrary")),
    )(*args)


def _mm_rows(name, a, w, *, out_dtype, add=None):
    s, kdim = a.shape
    n = w.shape[1]
    tm = _row_tile(s, MM_ROWS)
    return _matmul(
        name, a, w, dims=NN, grid=(s // tm, 1, 1),
        a_spec=pl.BlockSpec((tm, kdim), lambda i, j, k: (i, 0)),
        b_spec=pl.BlockSpec((kdim, n), lambda i, j, k: (0, 0)),
        o_spec=pl.BlockSpec((tm, n), lambda i, j, k: (i, 0)),
        out_shape=jax.ShapeDtypeStruct((s, n), out_dtype), acc_shape=(8, 128),
        add=add, add_spec=None if add is None else pl.BlockSpec((tm, n), lambda i, j, k: (i, 0)),
    )


def _mm_rows_nt(name, a, w, *, out_dtype):
    s, n = a.shape
    kdim = w.shape[0]
    tm = _row_tile(s, MM_ROWS)
    return _matmul(
        name, a, w, dims=NT, grid=(s // tm, 1, 1),
        a_spec=pl.BlockSpec((tm, n), lambda i, j, k: (i, 0)),
        b_spec=pl.BlockSpec((kdim, n), lambda i, j, k: (0, 0)),
        o_spec=pl.BlockSpec((tm, kdim), lambda i, j, k: (i, 0)),
        out_shape=jax.ShapeDtypeStruct((s, kdim), out_dtype), acc_shape=(8, 128),
    )


def _mm_tn(name, a, g):
    s, m = a.shape
    n = g.shape[1]
    tn = min(n, 512)
    return _matmul(
        name, a, g, dims=TN, grid=(1, n // tn, 1),
        a_spec=pl.BlockSpec((s, m), lambda i, j, k: (0, 0)),
        b_spec=pl.BlockSpec((s, tn), lambda i, j, k: (0, j)),
        o_spec=pl.BlockSpec((m, tn), lambda i, j, k: (0, j)),
        out_shape=jax.ShapeDtypeStruct((m, n), BF16),
    )


def _rmsnorm_cast(name, h, gains):
    s, d = h.shape
    tm = _row_tile(s)
    n = len(gains)

    def body(*refs):
        h_ref, g_refs, o_refs = refs[0], refs[1:1 + n], refs[1 + n:]
        xv = h_ref[...]
        xhat = xv * lax.rsqrt(jnp.mean(xv * xv, axis=-1, keepdims=True) + EPS)
        for g_ref, o_ref in zip(g_refs, o_refs):
            o_ref[...] = (xhat * g_ref[...]).astype(BF16)

    row = pl.BlockSpec((tm, d), lambda i: (i, 0))
    vec = pl.BlockSpec((1, d), lambda i: (0, 0))
    return pl.pallas_call(
        body, name=name, grid=(s // tm,), in_specs=[row] + [vec] * n, out_specs=[row] * n,
        out_shape=[jax.ShapeDtypeStruct((s, d), BF16)] * n, compiler_params=_params(("parallel",)),
    )(h, *gains)


def _rmsnorm_bwd(name, h, dres, branches):
    s, d = h.shape
    tm = _row_tile(s)
    n = len(branches)

    def body(*refs):
        h_ref, dres_ref = refs[0], refs[1]
        da_refs, g_refs = refs[2:2 + n], refs[2 + n:2 + 2 * n]
        dh_ref, dhb_ref, dg_refs = refs[2 + 2 * n], refs[3 + 2 * n], refs[4 + 2 * n:]
        i = pl.program_id(0)
        xv = h_ref[...]
        r = lax.rsqrt(jnp.mean(xv * xv, axis=-1, keepdims=True) + EPS)
        xhat = xv * r
        total = dres_ref[...]
        for da_ref, g_ref, dg_ref in zip(da_refs, g_refs, dg_refs):
            da = da_ref[...]
            dgain = jnp.sum(da * xhat, axis=0, keepdims=True)

            @pl.when(i == 0)
            def _():
                dg_ref[...] = dgain

            @pl.when(i > 0)
            def _():
                dg_ref[...] += dgain

            dxhat = da * g_ref[...]
            total = total + r * (dxhat - xhat * jnp.mean(dxhat * xhat, axis=-1, keepdims=True))
        dh_ref[...] = total
        dhb_ref[...] = total.astype(BF16)

    row = pl.BlockSpec((tm, d), lambda i: (i, 0))
    vec = pl.BlockSpec((1, d), lambda i: (0, 0))
    outs = pl.pallas_call(
        body, name=name, grid=(s // tm,), in_specs=[row, row] + [row] * n + [vec] * n, out_specs=[row, row] + [vec] * n,
        out_shape=[jax.ShapeDtypeStruct((s, d), F32), jax.ShapeDtypeStruct((s, d), BF16)] + [jax.ShapeDtypeStruct((1, d), F32)] * n,
        compiler_params=_params(("arbitrary",)),
    )(h, dres, *[b[0] for b in branches], *[b[1] for b in branches])
    return (outs[0], outs[1]), outs[2:]


def _loss_head(h, gain, target):
    s, d = h.shape
    tm = _row_tile(s)

    def body(h_ref, g_ref, t_ref, dh_ref, dhb_ref, dg_ref, loss_ref):
        i = pl.program_id(0)
        xv = h_ref[...]
        r = lax.rsqrt(jnp.mean(xv * xv, axis=-1, keepdims=True) + EPS)
        xhat = xv * r
        err = xhat * g_ref[...] - t_ref[...]
        dy = err * (1.0 / d)
        part = jnp.zeros((1, 128), F32) + 0.5 * jnp.sum(jnp.mean(err * err, axis=-1, keepdims=True))
        dgain = jnp.sum(dy * xhat, axis=0, keepdims=True)

        @pl.when(i == 0)
        def _():
            dg_ref[...] = dgain
            loss_ref[...] = part

        @pl.when(i > 0)
        def _():
            dg_ref[...] += dgain
            loss_ref[...] += part

        dxhat = dy * g_ref[...]
        dh = r * (dxhat - xhat * jnp.mean(dxhat * xhat, axis=-1, keepdims=True))
        dh_ref[...] = dh
        dhb_ref[...] = dh.astype(BF16)

    row = pl.BlockSpec((tm, d), lambda i: (i, 0))
    vec = pl.BlockSpec((1, d), lambda i: (0, 0))
    return pl.pallas_call(
        body, name="loss_head", grid=(s // tm,), in_specs=[row, vec, row],
        out_specs=[row, row, vec, pl.BlockSpec((1, 128), lambda i: (0, 0))],
        out_shape=[jax.ShapeDtypeStruct((s, d), F32), jax.ShapeDtypeStruct((s, d), BF16), jax.ShapeDtypeStruct((1, d), F32),
                   jax.ShapeDtypeStruct((1, 128), F32)],
        compiler_params=_params(("arbitrary",)),
    )(h, gain, target)


def _bdot(a, b, ca, cb):
    return lax.dot_general(a.astype(BF16), b.astype(BF16), (((ca,), (cb,)), ((0,), (0,))), preferred_element_type=F32)


def _chunk_cumsum(xv, reverse=False):
    n = xv.shape[0]
    row = lax.broadcasted_iota(jnp.int32, xv.shape, 0) % HG_CHUNK
    step = 1
    while step < HG_CHUNK:
        if reverse:
            xv = xv + jnp.where(row < HG_CHUNK - step, pltpu.roll(xv, n - step, axis=0), 0.0)
        else:
            xv = xv + jnp.where(row >= step, pltpu.roll(xv, step, axis=0), 0.0)
        step *= 2
    return xv


def _hg_terms(p_ref, lbl_ref):
    pq = p_ref[0].astype(F32)
    pf = p_ref[1].astype(F32)
    lb = _sigmoid(lbl_ref[0:1, :] - lbl_ref[1:2, :])
    sig = _sigmoid(pf)
    fg = lb + (1.0 - lb) * sig
    nc = pq.shape[0] // HG_CHUNK
    chunks = lambda a: a.reshape(nc, HG_CHUNK, HG_EXPAND)
    q = chunks(_silu(pq) * HG_EXPAND ** -0.5)
    k = chunks(1.0 - fg)
    v = chunks(p_ref[2].astype(F32))
    g = chunks(_chunk_cumsum(jnp.log(fg)))
    gm = g[:, HG_CHUNK // 2 - 1:HG_CHUNK // 2, :]
    gl = g[:, HG_CHUNK - 1:HG_CHUNK, :]
    e_mid, e_inv, e_all, e_end = jnp.exp(g - gm), jnp.exp(gm - g), jnp.exp(g), jnp.exp(gl - g)
    terms = dict(q=q, k=k, v=v, qd=q * e_all, qt=q * e_mid, kt=k * e_inv, kd=k * e_end, e_last=jnp.exp(gl),
                 e_mid=e_mid, e_inv=e_inv, e_all=e_all, e_end=e_end)
    return terms, (pq, sig, fg, lb)


def _causal(nc):
    r = lax.broadcasted_iota(jnp.int32, (nc, HG_CHUNK, HG_CHUNK), 1)
    c = lax.broadcasted_iota(jnp.int32, (nc, HG_CHUNK, HG_CHUNK), 2)
    return r >= c


def _hgrn2_fwd(p, lb_logits, out_gain):
    _, s, d = p.shape
    heads = d // HG_EXPAND
    t = _row_tile(s)
    nc = t // HG_CHUNK

    def body(p_ref, lbl_ref, gain_ref, o_ref, og_ref, st_ref, state, decay):
        @pl.when(pl.program_id(1) == 0)
        def _():
            state[...] = jnp.zeros_like(state)

        tm, _ = _hg_terms(p_ref, lbl_ref)
        decay[...] = tm["e_last"]
        st_ref[...] = _bdot(tm["v"], tm["kd"], 1, 1)

        def chunk(c, carry):
            add = st_ref[c]
            st = state[...]
            st_ref[c] = st
            state[...] = st * decay[c] + add
            return carry

        lax.fori_loop(0, nc, chunk, 0)
        a = jnp.where(_causal(nc), _bdot(tm["qt"], tm["kt"], 2, 2), 0.0)
        ov = (_bdot(tm["qd"], st_ref[...], 2, 2) + _bdot(a, tm["v"], 2, 1)).reshape(t, HG_EXPAND)
        o_ref[...] = ov
        on = ov * lax.rsqrt(jnp.mean(ov * ov, axis=-1, keepdims=True) + EPS) * gain_ref[...]
        og_ref[...] = (on * _silu(p_ref[3].astype(F32))).astype(BF16)

    blk = pl.BlockSpec((t, HG_EXPAND), lambda h, b: (b, h))
    return pl.pallas_call(
        body, name="hgrn2_fwd", grid=(heads, s // t),
        in_specs=[pl.BlockSpec((4, t, HG_EXPAND), lambda h, b: (0, b, h)), pl.BlockSpec((2, HG_EXPAND), lambda h, b: (0, h)),
                  pl.BlockSpec((1, HG_EXPAND), lambda h, b: (0, 0))],
        out_specs=[blk, blk, pl.BlockSpec((None, nc, HG_EXPAND, HG_EXPAND), lambda h, b: (h, b, 0, 0))],
        out_shape=[jax.ShapeDtypeStruct((s, d), F32), jax.ShapeDtypeStruct((s, d), BF16),
                   jax.ShapeDtypeStruct((heads, s // HG_CHUNK, HG_EXPAND, HG_EXPAND), F32)],
        scratch_shapes=[pltpu.VMEM((HG_EXPAND, HG_EXPAND), F32), pltpu.VMEM((nc, 1, HG_EXPAND), F32)],
        compiler_params=_params(("parallel", "arbitrary")),
    )(p, lb_logits, out_gain)


def _hgrn2_bwd(p, lb_logits, out_gain, o, dog, states):
    _, s, d = p.shape
    heads = d // HG_EXPAND
    t = _row_tile(s)
    nc = t // HG_CHUNK
    nb = s // t

    def body(p_ref, lbl_ref, gain_ref, o_ref, dog_ref, st_ref, dp_ref, dlbl_ref, dgain_ref, dstate, decay, dst_s):
        h, b = pl.program_id(0), pl.program_id(1)

        @pl.when(b == 0)
        def _():
            dstate[...] = jnp.zeros_like(dstate)

        tm, (pq, sig, fg, lb) = _hg_terms(p_ref, lbl_ref)
        pg = p_ref[3].astype(F32)
        ov = o_ref[...]
        r = lax.rsqrt(jnp.mean(ov * ov, axis=-1, keepdims=True) + EPS)
        ohat = ov * r
        dogv = dog_ref[...]
        d_on = dogv * _silu(pg)
        dp_ref[3] = (dogv * ohat * gain_ref[...] * _dsilu(pg)).astype(BF16)
        dgain = jnp.sum(d_on * ohat, axis=0, keepdims=True)

        @pl.when((h == 0) & (b == 0))
        def _():
            dgain_ref[...] = dgain

        @pl.when((h > 0) | (b > 0))
        def _():
            dgain_ref[...] += dgain

        dohat = d_on * gain_ref[...]
        do = (r * (dohat - ohat * jnp.mean(dohat * ohat, axis=-1, keepdims=True))).reshape(nc, HG_CHUNK, HG_EXPAND)

        decay[...] = tm["e_last"]
        dst_s[...] = _bdot(do, tm["qd"], 1, 1)

        def chunk(i, carry):
            c = nc - 1 - i
            add = dst_s[c]
            dst = dstate[...]
            dst_s[c] = dst
            dstate[...] = dst * decay[c] + add
            return carry

        lax.fori_loop(0, nc, chunk, 0)
        st, dst = st_ref[...], dst_s[...]
        causal = _causal(nc)
        a = jnp.where(causal, _bdot(tm["qt"], tm["kt"], 2, 2), 0.0)
        da = jnp.where(causal, _bdot(do, tm["v"], 2, 2), 0.0)
        dqt = _bdot(da, tm["kt"], 2, 1)
        dkt = _bdot(da, tm["qt"], 1, 1)
        dqd = _bdot(do, st, 2, 1)
        dkd = _bdot(tm["v"], dst, 2, 1)
        dv = _bdot(a, do, 1, 1) + _bdot(tm["kd"], dst, 2, 2)
        dq = dqt * tm["e_mid"] + dqd * tm["e_all"]
        dk = dkt * tm["e_inv"] + dkd * tm["e_end"]
        dg = dqt * tm["qt"] - dkt * tm["kt"] + dqd * tm["qd"] - dkd * tm["kd"]
        dgl = jnp.sum(dkd * tm["kd"], axis=1, keepdims=True) + tm["e_last"] * jnp.sum(dst * st, axis=1, keepdims=True)
        last_row = lax.broadcasted_iota(jnp.int32, (nc, HG_CHUNK, HG_EXPAND), 1) == HG_CHUNK - 1
        flat = lambda a3: a3.reshape(t, HG_EXPAND)
        dlf = _chunk_cumsum(flat(dg + jnp.where(last_row, dgl, 0.0)), reverse=True)
        dfg = dlf / fg - flat(dk)
        dlb = jnp.sum(dfg * (1.0 - sig), axis=0, keepdims=True)
        dl0 = dlb * lb * (1.0 - lb)
        dlbl = jnp.concatenate([dl0, -dl0], axis=0)

        @pl.when(b == 0)
        def _():
            dlbl_ref[...] = dlbl

        @pl.when(b > 0)
        def _():
            dlbl_ref[...] += dlbl

        dp_ref[0] = (flat(dq) * HG_EXPAND ** -0.5 * _dsilu(pq)).astype(BF16)
        dp_ref[1] = (dfg * (1.0 - lb) * sig * (1.0 - sig)).astype(BF16)
        dp_ref[2] = flat(dv).astype(BF16)

    blk = pl.BlockSpec((t, HG_EXPAND), lambda h, b: (nb - 1 - b, h))
    pblk = pl.BlockSpec((4, t, HG_EXPAND), lambda h, b: (0, nb - 1 - b, h))
    return pl.pallas_call(
        body, name="hgrn2_bwd", grid=(heads, nb),
        in_specs=[pblk, pl.BlockSpec((2, HG_EXPAND), lambda h, b: (0, h)), pl.BlockSpec((1, HG_EXPAND), lambda h, b: (0, 0)),
                  blk, blk, pl.BlockSpec((None, nc, HG_EXPAND, HG_EXPAND), lambda h, b: (h, nb - 1 - b, 0, 0))],
        out_specs=[pblk, pl.BlockSpec((2, HG_EXPAND), lambda h, b: (0, h)), pl.BlockSpec((1, HG_EXPAND), lambda h, b: (0, 0))],
        out_shape=[jax.ShapeDtypeStruct((4, s, d), BF16), jax.ShapeDtypeStruct((2, d), F32), jax.ShapeDtypeStruct((1, HG_EXPAND), F32)],
        scratch_shapes=[pltpu.VMEM((HG_EXPAND, HG_EXPAND), F32), pltpu.VMEM((nc, 1, HG_EXPAND), F32),
                        pltpu.VMEM((nc, HG_EXPAND, HG_EXPAND), F32)],
        compiler_params=_params(("arbitrary", "arbitrary")),
    )(p, lb_logits, out_gain, o, dog, states)


HALO = 8


def _shift_down(xv, n):
    return pltpu.roll(xv, n, axis=0)


def _shift_up(xv, n):
    return pltpu.roll(xv, xv.shape[0] - n, axis=0)


def _ffn_hidden(u, conv_w, conv_b):
    _, nj, s, fb = u.shape
    tm = _row_tile(s)
    per = tm // HALO

    def body(gate_ref, prev_ref, val_ref, w_ref, b_ref, h_ref):
        i = pl.program_id(1)
        prev = jnp.where(i > 0, prev_ref[...].astype(F32), 0.0)
        ext = jnp.concatenate([prev, gate_ref[...].astype(F32)], axis=0)
        conv = b_ref[...] + w_ref[2:3, :] * ext[HALO:]
        conv = conv + w_ref[1:2, :] * _shift_down(ext, 1)[HALO:]
        conv = conv + w_ref[0:1, :] * _shift_down(ext, 2)[HALO:]
        h_ref[...] = (_silu(conv) * val_ref[...].astype(F32)).astype(BF16)

    return pl.pallas_call(
        body, name="ffn_hidden", grid=(nj, s // tm),
        in_specs=[pl.BlockSpec((None, None, tm, fb), lambda j, i: (0, j, i, 0)),
                  pl.BlockSpec((None, None, HALO, fb), lambda j, i: (0, j, jnp.maximum(i * per - 1, 0), 0)),
                  pl.BlockSpec((None, None, tm, fb), lambda j, i: (1, j, i, 0)),
                  pl.BlockSpec((None, CONV_WIDTH, fb), lambda j, i: (j, 0, 0)),
                  pl.BlockSpec((None, 1, fb), lambda j, i: (j, 0, 0))],
        out_specs=pl.BlockSpec((None, tm, fb), lambda j, i: (j, i, 0)),
        out_shape=jax.ShapeDtypeStruct((nj, s, fb), BF16), compiler_params=_params(("parallel", "parallel")),
    )(u, u, u, conv_w, conv_b)


def _ffn_hidden_bwd(u, dh, conv_w, conv_b):
    _, nj, s, fb = u.shape
    tm = _row_tile(s)
    per = tm // HALO
    nblk = s // HALO
    ni = s // tm

    def body(gate_ref, gprev_ref, gnext_ref, val_ref, vnext_ref, dh_ref, dhnext_ref, w_ref, b_ref, du_ref, dw_ref, db_ref):
        i = pl.program_id(1)
        has_next = i < ni - 1
        gprev = jnp.where(i > 0, gprev_ref[...].astype(F32), 0.0)
        gext = jnp.concatenate([gprev, gate_ref[...].astype(F32), gnext_ref[...].astype(F32)], axis=0)
        vext = jnp.concatenate([val_ref[...].astype(F32), vnext_ref[...].astype(F32)], axis=0)
        dhext = jnp.concatenate([dh_ref[...].astype(F32), jnp.where(has_next, dhnext_ref[...].astype(F32), 0.0)], axis=0)
        g0 = gext[HALO:]
        g1 = _shift_down(gext, 1)[HALO:]
        g2 = _shift_down(gext, 2)[HALO:]
        conv = b_ref[...] + w_ref[2:3, :] * g0 + w_ref[1:2, :] * g1 + w_ref[0:1, :] * g2
        act, dact = _silu_and_grad(conv)
        dconv = dhext * vext * dact
        dgate = w_ref[2:3, :] * dconv + w_ref[1:2, :] * _shift_up(dconv, 1) + w_ref[0:1, :] * _shift_up(dconv, 2)
        du_ref[0] = dgate[:tm].astype(BF16)
        du_ref[1] = (dhext * act)[:tm].astype(BF16)
        own = dconv[:tm]
        dw = jnp.concatenate([jnp.sum(own * g2[:tm], axis=0, keepdims=True), jnp.sum(own * g1[:tm], axis=0, keepdims=True),
                              jnp.sum(own * g0[:tm], axis=0, keepdims=True)], axis=0)
        db = jnp.sum(own, axis=0, keepdims=True)

        @pl.when(i == 0)
        def _():
            dw_ref[...] = dw
            db_ref[...] = db

        @pl.when(i > 0)
        def _():
            dw_ref[...] += dw
            db_ref[...] += db

    def tile(part):
        return pl.BlockSpec((None, None, tm, fb), lambda j, i: (part, j, i, 0))

    def after(part):
        return pl.BlockSpec((None, None, HALO, fb), lambda j, i: (part, j, jnp.minimum((i + 1) * per, nblk - 1), 0))

    return pl.pallas_call(
        body, name="ffn_hidden_bwd", grid=(nj, ni),
        in_specs=[tile(0), pl.BlockSpec((None, None, HALO, fb), lambda j, i: (0, j, jnp.maximum(i * per - 1, 0), 0)), after(0),
                  tile(1), after(1),
                  pl.BlockSpec((None, tm, fb), lambda j, i: (j, i, 0)),
                  pl.BlockSpec((None, HALO, fb), lambda j, i: (j, jnp.minimum((i + 1) * per, nblk - 1), 0)),
                  pl.BlockSpec((None, CONV_WIDTH, fb), lambda j, i: (j, 0, 0)), pl.BlockSpec((None, 1, fb), lambda j, i: (j, 0, 0))],
        out_specs=[pl.BlockSpec((2, None, tm, fb), lambda j, i: (0, j, i, 0)),
                   pl.BlockSpec((None, CONV_WIDTH, fb), lambda j, i: (j, 0, 0)), pl.BlockSpec((None, 1, fb), lambda j, i: (j, 0, 0))],
        out_shape=[jax.ShapeDtypeStruct((2, nj, s, fb), BF16), jax.ShapeDtypeStruct((nj, CONV_WIDTH, fb), F32),
                   jax.ShapeDtypeStruct((nj, 1, fb), F32)],
        compiler_params=_params(("parallel", "arbitrary")),
    )(u, u, u, u, u, dh, dh, conv_w, conv_b)


ATT_TILE = 512


def _stack_heads(ref, rows, first_head, count):
    hd = ATT_HEAD_DIM
    return jnp.concatenate([ref[rows, (first_head + j) * hd:(first_head + j + 1) * hd] for j in range(count)], axis=0)


def _unstack_heads(stacked, ref, rows, first_head, count):
    hd = ATT_HEAD_DIM
    for pair in range(count // 2):
        both = [stacked[(2 * pair + j) * WINDOW:(2 * pair + j + 1) * WINDOW, :] for j in range(2)]
        ref[rows, (first_head + 2 * pair) * hd:(first_head + 2 * pair + 2) * hd] = jnp.concatenate(both, axis=1).astype(ref.dtype)


def _attn_probs_t(kb, qs, sink_ref, first_head, count, first, n_heads):
    lanes = count * WINDOW
    ik = lax.broadcasted_iota(jnp.int32, (2 * WINDOW, lanes), 0)
    iq = lax.broadcasted_iota(jnp.int32, (2 * WINDOW, lanes), 1) % WINDOW
    dist = iq + WINDOW - ik
    valid = (dist >= 0) & (dist < WINDOW) & (ik >= jnp.where(first, WINDOW, 0))
    per_head = lambda values: jnp.concatenate([jnp.zeros((1, WINDOW), F32) + v for v in values], axis=1)
    slope = per_head([2.0 ** (-8.0 * (first_head + j + 1) / n_heads) for j in range(count)])
    sink = per_head([sink_ref[0, first_head + j] for j in range(count)])
    sc = jnp.where(valid, _dot(kb, qs, NT) * ATT_HEAD_DIM ** -0.5 - slope * dist.astype(F32), NEG)
    m = jnp.maximum(jnp.max(sc, axis=0, keepdims=True), sink)
    e = jnp.exp(sc - m)
    es = jnp.exp(sink - m)
    inv = 1.0 / (jnp.sum(e, axis=0, keepdims=True) + es)
    return e * inv, es * inv


def _attn_specs(s, d, kvd, tq):
    per = tq // WINDOW
    return [pl.BlockSpec((tq, d), lambda i: (i, 0)), pl.BlockSpec((tq, kvd), lambda i: (i, 0)),
            pl.BlockSpec((WINDOW, kvd), lambda i: (jnp.maximum(i * per - 1, 0), 0))]


def _attn_fwd(q, kv, sinks):
    s, d = q.shape
    kvd = kv.shape[1]
    half = kvd // 2
    hd = ATT_HEAD_DIM
    nq = d // hd
    group = nq // ATT_KV_HEADS
    tq = min(s, ATT_TILE)
    per = tq // WINDOW

    def body(q_ref, kvc_ref, kvp_ref, sink_ref, o_ref, band):
        i = pl.program_id(0)
        band[0:WINDOW, :] = kvp_ref[...]
        band[WINDOW:, :] = kvc_ref[...]

        def block(b, carry):
            rows = pl.ds(pl.multiple_of(b * WINDOW, WINDOW), WINDOW)
            keys = pl.ds(pl.multiple_of(b * WINDOW, WINDOW), 2 * WINDOW)
            first = (i * per + b) == 0
            for g in range(ATT_KV_HEADS):
                p, _ = _attn_probs_t(band[keys, g * hd:(g + 1) * hd], _stack_heads(q_ref, rows, g * group, group), sink_ref,
                                     g * group, group, first, nq)
                out_t = _dot(band[keys, half + g * hd:half + (g + 1) * hd], p, TN)
                _unstack_heads(out_t.T, o_ref, rows, g * group, group)
            return carry

        lax.fori_loop(0, per, block, 0)

    return pl.pallas_call(
        body, name="attn_fwd", grid=(s // tq,),
        in_specs=_attn_specs(s, d, kvd, tq) + [pl.BlockSpec(memory_space=pltpu.SMEM)],
        out_specs=pl.BlockSpec((tq, d), lambda i: (i, 0)), out_shape=jax.ShapeDtypeStruct((s, d), BF16),
        scratch_shapes=[pltpu.VMEM((tq + WINDOW, kvd), BF16)], compiler_params=_params(("parallel",)),
    )(q, kv, kv, sinks)


def _attn_bwd(q, kv, o, do, sinks):
    s, d = q.shape
    kvd = kv.shape[1]
    half = kvd // 2
    hd = ATT_HEAD_DIM
    nq = d // hd
    group = nq // ATT_KV_HEADS
    tq = min(s, ATT_TILE)
    per = tq // WINDOW
    nt = s // tq

    def body(q_ref, kvc_ref, kvp_ref, o_ref, do_ref, sink_ref, dq_ref, dkvc_ref, dkvp_ref, ds_ref, band, dband):
        i = pl.program_id(0)
        band[0:WINDOW, :] = kvp_ref[...]
        band[WINDOW:, :] = kvc_ref[...]
        dband[...] = jnp.zeros_like(dband)
        ds_ref[...] = jnp.zeros_like(ds_ref)

        def block(b, carry):
            rows = pl.ds(pl.multiple_of(b * WINDOW, WINDOW), WINDOW)
            keys = pl.ds(pl.multiple_of(b * WINDOW, WINDOW), 2 * WINDOW)
            first = (i * per + b) == 0
            dks, dvs = [], []
            for g in range(ATT_KV_HEADS):
                kb = band[keys, g * hd:(g + 1) * hd]
                vb = band[keys, half + g * hd:half + (g + 1) * hd]
                qs = _stack_heads(q_ref, rows, g * group, group)
                dos = _stack_heads(do_ref, rows, g * group, group)
                p, ps = _attn_probs_t(kb, qs, sink_ref, g * group, group, first, nq)
                prod = dos.astype(F32) * _stack_heads(o_ref, rows, g * group, group).astype(F32)
                dsum = lax.dot_general(jnp.ones((8, hd), F32), prod, NT, precision=lax.Precision.HIGHEST,
                                       preferred_element_type=F32)[0:1, :]
                dsc = p * (_dot(vb, dos, NT) - dsum) * ATT_HEAD_DIM ** -0.5
                dvs.append(_dot(p, dos))
                dks.append(_dot(dsc, qs))
                _unstack_heads(_dot(kb, dsc, TN).T, dq_ref, rows, g * group, group)
                gone = ps * dsum
                for j in range(group):
                    ds_ref[g * group + j:g * group + j + 1, :] += jnp.zeros((1, 128), F32) - jnp.sum(gone[:, j * WINDOW:(j + 1) * WINDOW])
            dband[keys, 0:half] += jnp.concatenate(dks, axis=1)
            dband[keys, half:] += jnp.concatenate(dvs, axis=1)
            return carry

        lax.fori_loop(0, per, block, 0)
        dkvp_ref[...] = dband[0:WINDOW, :]
        dkvc_ref[...] = dband[WINDOW:, :]

    big = pl.BlockSpec((tq, d), lambda i: (i, 0))
    return pl.pallas_call(
        body, name="attn_bwd", grid=(nt,),
        in_specs=_attn_specs(s, d, kvd, tq) + [big, big, pl.BlockSpec(memory_space=pltpu.SMEM)],
        out_specs=[big, pl.BlockSpec((tq, kvd), lambda i: (i, 0)), pl.BlockSpec((None, WINDOW, kvd), lambda i: (i, 0, 0)),
                   pl.BlockSpec((None, nq, 128), lambda i: (i, 0, 0))],
        out_shape=[jax.ShapeDtypeStruct((s, d), BF16), jax.ShapeDtypeStruct((s, kvd), F32), jax.ShapeDtypeStruct((nt, WINDOW, kvd), F32),
                   jax.ShapeDtypeStruct((nt, nq, 128), F32)],
        scratch_shapes=[pltpu.VMEM((tq + WINDOW, kvd), BF16), pltpu.VMEM((tq + WINDOW, kvd), F32)],
        compiler_params=_params(("parallel",)),
    )(q, kv, kv, o, do, sinks)


HBM_SPEC = pl.BlockSpec(memory_space=pltpu.HBM)
VMEM_SPEC = pl.BlockSpec(memory_space=pltpu.VMEM)


def _place():
    return lax.axis_index("x"), lax.axis_index("y"), lax.axis_index("c")


def _flip(pos, r):
    return tuple(1 - p if (r >> (2 - a)) & 1 else p for a, p in enumerate(pos))


def _index(pos):
    return 4 * pos[0] + 2 * pos[1] + pos[2]


def _all_gather(name, shards, spec):
    n = len(shards)

    def body(*refs):
        x_refs, o_refs = refs[:n], refs[n:2 * n]
        send_sems, recv_sems, local_sems = refs[2 * n:]
        me = _place()
        sibling = _flip(me, 1)
        far = [_flip(me, r) for r in (4, 2, 6)]

        def copy(t, sem, block, to, src=None):
            rows = o_refs[t].at[_index(block)]
            return pltpu.make_async_remote_copy(
                src_ref=rows if src is None else src, dst_ref=rows, send_sem=send_sems.at[t, sem], recv_sem=recv_sems.at[t, sem],
                device_id=to, device_id_type=MESH)

        own = [pltpu.make_async_copy(x_refs[t], o_refs[t].at[_index(me)], local_sems.at[t]) for t in range(n)]
        for cp in own:
            cp.start()
        first = []
        for t in range(n):
            first.append(copy(t, 0, me, sibling, src=x_refs[t]))
            first += [copy(t, 1 + j, me, peer, src=x_refs[t]) for j, peer in enumerate(far)]
        for cp in first:
            cp.start()
        passed = []
        for j, peer in enumerate(far):
            for t in range(n):
                copy(t, 1 + j, peer, me).wait_recv()
                cp = copy(t, 4 + j, peer, sibling)
                cp.start()
                passed.append(cp)
        for t in range(n):
            copy(t, 0, sibling, me).wait_recv()
            for j, peer in enumerate(far):
                copy(t, 4 + j, _flip(peer, 1), me).wait_recv()
        for cp in first + passed:
            cp.wait_send()
        for cp in own:
            cp.wait()

    return pl.pallas_call(
        body, name=name, in_specs=[spec] * n, out_specs=[spec] * n,
        out_shape=[jax.ShapeDtypeStruct((N_DEV,) + sh.shape, sh.dtype) for sh in shards],
        scratch_shapes=[pltpu.SemaphoreType.DMA((n, 7)), pltpu.SemaphoreType.DMA((n, 7)), pltpu.SemaphoreType.DMA((n,))],
    )(*shards)


SEM_SPEC = pl.BlockSpec(memory_space=pltpu.SEMAPHORE)
ANY_SPEC = pl.BlockSpec(memory_space=pl.ANY)


def _landing(own, mine):
    return lax.dynamic_update_slice(lax.empty((N_DEV,) + own.shape, own.dtype), own[None], (mine,) + (0,) * own.ndim)


def _pinned(a, token):
    return a + token[0:1, 0:1].astype(a.dtype)


def _peer_copies(src_refs, land_refs, send_sems, recv_sems, scatter, arrivals):
    me = _place()
    mine = _index(me)
    copies = []
    for t, (src, land) in enumerate(zip(src_refs, land_refs)):
        for r in range(1, N_DEV):
            peer = _flip(me, r)
            theirs = _index(peer)
            sem = t * (N_DEV - 1) + r - 1
            copies.append(pltpu.make_async_remote_copy(
                src_ref=src.at[theirs] if scatter else src, dst_ref=land.at[theirs if arrivals else mine],
                send_sem=send_sems.at[sem], recv_sem=recv_sems.at[sem], device_id=peer, device_id_type=MESH))
    return copies


def _send_start(name, sources, lands, scatter, after=None):
    n = len(sources)
    extra = 0 if after is None else 1

    def body(*refs):
        outs = refs[2 * n + extra:]
        for out in _peer_copies(refs[:n], refs[n:2 * n], outs[0], outs[1], scatter, False):
            out.start()
        outs[-1][...] = jnp.zeros_like(outs[-1])

    outs = pl.pallas_call(
        body, name=name, in_specs=[HBM_SPEC] * (2 * n) + [ANY_SPEC] * extra,
        out_specs=[SEM_SPEC, SEM_SPEC] + [HBM_SPEC] * (2 * n) + [VMEM_SPEC],
        out_shape=[pltpu.SemaphoreType.DMA((n * (N_DEV - 1),)), pltpu.SemaphoreType.DMA((n * (N_DEV - 1),))]
        + [pltpu.HBM(a.shape, a.dtype) for a in list(sources) + list(lands)] + [jax.ShapeDtypeStruct((8, 128), F32)],
        input_output_aliases={i: 2 + i for i in range(2 * n)},
        compiler_params=pltpu.CompilerParams(has_side_effects=pltpu.SideEffectType.DATAFLOW_SIDE_EFFECTING),
    )(*[pltpu.with_memory_space_constraint(a, pltpu.HBM) for a in list(sources) + list(lands)], *([] if after is None else [after]))
    return outs[0], outs[1], outs[2:2 + n], outs[2 + n:2 + 2 * n], outs[-1]


def _send_wait(name, started, after, scatter):
    send_sems, recv_sems, sources, lands, _ = started
    n = len(sources)

    def body(*refs):
        for out in _peer_copies(refs[:n], refs[n:2 * n], refs[2 * n], refs[2 * n + 1], scatter, False):
            out.wait_send()
        for arrival in _peer_copies(refs[:n], refs[n:2 * n], refs[2 * n], refs[2 * n + 1], scatter, True):
            arrival.wait_recv()

    outs = pl.pallas_call(
        body, name=name, in_specs=[HBM_SPEC] * (2 * n) + [SEM_SPEC, SEM_SPEC, ANY_SPEC], out_specs=[HBM_SPEC] * (2 * n),
        out_shape=[pltpu.HBM(a.shape, a.dtype) for a in list(sources) + list(lands)],
        input_output_aliases={i: i for i in range(2 * n)},
        compiler_params=pltpu.CompilerParams(has_side_effects=pltpu.SideEffectType.DATAFLOW_SIDE_EFFECTING),
    )(*sources, *lands, send_sems, recv_sems, after)
    return outs[n:]


def _pack_rows(parts):
    offsets, row = [], 0
    for part in parts:
        offsets.append(row)
        row += part.shape[0]
    return offsets, -(-row // 8) * 8, -(-max(part.shape[1] for part in parts) // 128) * 128


def _pack(name, parts):
    offsets, rows, width = _pack_rows(parts)

    def body(*refs):
        o_ref = refs[-1]
        o_ref[...] = jnp.zeros_like(o_ref)
        for off, ref in zip(offsets, refs[:-1]):
            o_ref[off:off + ref.shape[0], 0:ref.shape[1]] = ref[...]

    return pl.pallas_call(body, name=name, in_specs=[VMEM_SPEC] * len(parts), out_specs=VMEM_SPEC,
                          out_shape=jax.ShapeDtypeStruct((rows, width), F32))(*parts)


def _adamw_math(w, g, m, v):
    m = ADAM_B1 * m + (1.0 - ADAM_B1) * g
    v = ADAM_B2 * v + (1.0 - ADAM_B2) * (g * g)
    m_hat = m / (1.0 - ADAM_B1 ** ADAM_STEP)
    v_hat = v / (1.0 - ADAM_B2 ** ADAM_STEP)
    return -ADAM_LR * (m_hat / (jnp.sqrt(v_hat) + ADAM_EPS) + ADAM_WD * w), m, v


def _adamw_shard(name, w, m, v, partials, layer=None, into=None):
    rows, cols = w.shape[-2:]
    tr = max(t for t in range(8, min(rows, 256) + 1, 8) if rows % t == 0)

    def body(w_ref, m_ref, v_ref, p_ref, *rest):
        g_ref, d_ref, nm_ref, nv_ref = rest[-4:]
        g = p_ref[0].astype(F32)
        for dev in range(1, N_DEV):
            g = g + p_ref[dev].astype(F32)
        g_ref[...] = g
        d_ref[...], nm_ref[...], nv_ref[...] = _adamw_math(w_ref[...], g, m_ref[...], v_ref[...])

    blk = pl.BlockSpec((tr, cols), lambda i: (i, 0)) if layer is None else pl.BlockSpec((None, tr, cols), lambda i: (layer, i, 0))
    kept = [] if into is None else list(into)
    return pl.pallas_call(
        body, name=name, grid=(rows // tr,),
        in_specs=[blk, blk, blk, pl.BlockSpec((N_DEV, tr, cols), lambda i: (0, i, 0))] + [ANY_SPEC] * len(kept),
        out_specs=[blk] * 4, out_shape=[jax.ShapeDtypeStruct(w.shape, F32)] * 4,
        input_output_aliases={4 + i: i for i in range(len(kept))}, compiler_params=_params(("parallel",)),
    )(w, m, v, partials, *kept)


def _adamw_small(gathered, places, entries):
    n = len(entries)
    np_ = len(gathered)

    def body(*refs):
        pack_refs = refs[:np_]
        refs = refs[np_ - 1:]
        w_refs, m_refs, v_refs = refs[1:1 + n], refs[1 + n:1 + 2 * n], refs[1 + 2 * n:1 + 3 * n]
        outs = refs[1 + 3 * n:]
        totals = []
        for pack_ref in pack_refs:
            acc = pack_ref[0]
            for dev in range(1, N_DEV):
                acc = acc + pack_ref[dev]
            totals.append(acc)
        mine = _index(_place())
        for e in range(n):
            rows, cols = w_refs[e].shape
            total, off = totals[places[e][0]], places[e][1]
            if entries[e][3]:
                g = jnp.zeros((rows, cols), F32)
                for dev in range(N_DEV):
                    g = g + jnp.where(mine == dev, total[off + dev * rows:off + (dev + 1) * rows, 0:cols], 0.0)
            else:
                g = total[off:off + rows, 0:cols]
            outs[4 * e][...] = g
            outs[4 * e + 1][...], outs[4 * e + 2][...], outs[4 * e + 3][...] = _adamw_math(w_refs[e][...], g, m_refs[e][...], v_refs[e][...])
        outs[4 * n][...] = totals[places[n][0]][places[n][1]:places[n][1] + 1, 0:128]

    shapes = []
    for w, _, _, _ in entries:
        shapes += [jax.ShapeDtypeStruct(w.shape, F32)] * 4
    shapes.append(jax.ShapeDtypeStruct((1, 128), F32))
    return pl.pallas_call(
        body, name="adamw_small", in_specs=[VMEM_SPEC] * (np_ + 3 * n), out_specs=[VMEM_SPEC] * len(shapes), out_shape=shapes,
        compiler_params=pltpu.CompilerParams(vmem_limit_bytes=VMEM_LIMIT),
    )(*gathered, *[e[0] for e in entries], *[e[1] for e in entries], *[e[2] for e in entries])


def _ffn_forward(tag, h, gain, w_up, w_down, conv_w, conv_b, after_up=None):
    s, d = h.shape
    fb = w_up.shape[2]
    tm = _row_tile(s, MM_ROWS)
    a, = _rmsnorm_cast(f"ffn_norm_{tag}", h, [gain])
    u = _matmul(
        f"ffn_up_{tag}", a, w_up, dims=NN, grid=(s // tm, N_DEV, 1),
        a_spec=pl.BlockSpec((tm, d), lambda i, j, k: (i, 0)),
        b_spec=pl.BlockSpec((None, d, fb), lambda i, j, k: (j, 0, 0)),
        o_spec=pl.BlockSpec((None, None, tm, fb), lambda i, j, k: (j // 4, j % 4, i, 0)),
        out_shape=jax.ShapeDtypeStruct((2, 4, s, fb), BF16))
    if after_up is not None:
        conv_b = _pinned(conv_b, after_up(u))
    hidden = _ffn_hidden(u, conv_w, conv_b)
    th = _row_tile(s, MM_ROWS // 2)
    out = _matmul(
        f"ffn_down_{tag}", hidden, w_down, dims=NN, grid=(s // th, 1, 1),
        a_spec=pl.BlockSpec((4, th, fb), lambda i, j, k: (0, i, 0)),
        b_spec=pl.BlockSpec((4, fb, d), lambda i, j, k: (0, 0, 0)),
        o_spec=pl.BlockSpec((th, d), lambda i, j, k: (i, 0)),
        out_shape=jax.ShapeDtypeStruct((s, d), F32),
        add=h, add_spec=pl.BlockSpec((th, d), lambda i, j, k: (i, 0)),
        terms=lambda a_ref, b_ref: [(a_ref[k], b_ref[k]) for k in range(4)])
    return out, (a, u, hidden)


def _ffn_backward(tag, h, gain, w_up, w_down, conv_w, conv_b, saved, dout):
    a, u, hidden = saved
    dout, dout_bf = dout
    s, d = h.shape
    fb = w_up.shape[2]
    tm = _row_tile(s, MM_ROWS)
    dhidden = _matmul(
        f"ffn_down_bwd_{tag}", dout_bf, w_down, dims=NT, grid=(s // tm, 4, 1),
        a_spec=pl.BlockSpec((tm, d), lambda i, j, k: (i, 0)),
        b_spec=pl.BlockSpec((None, fb, d), lambda i, j, k: (j, 0, 0)),
        o_spec=pl.BlockSpec((None, tm, fb), lambda i, j, k: (j, i, 0)),
        out_shape=jax.ShapeDtypeStruct((4, s, fb), BF16))
    dw_down = _matmul(
        f"ffn_down_grad_{tag}", hidden, dout_bf, dims=TN, grid=(4, 1, 1),
        a_spec=pl.BlockSpec((None, s, fb), lambda i, j, k: (i, 0, 0)),
        b_spec=pl.BlockSpec((s, d), lambda i, j, k: (0, 0)),
        o_spec=pl.BlockSpec((None, fb, d), lambda i, j, k: (i, 0, 0)),
        out_shape=jax.ShapeDtypeStruct((4, fb, d), BF16))
    du, dconv_w, dconv_b = _ffn_hidden_bwd(u, dhidden, conv_w, conv_b)
    th = _row_tile(s, MM_ROWS // 2)
    da = _matmul(
        f"ffn_up_bwd_{tag}", du, w_up, dims=NT, grid=(s // th, 1, 1),
        a_spec=pl.BlockSpec((2, 4, th, fb), lambda i, j, k: (0, 0, i, 0)),
        b_spec=pl.BlockSpec((N_DEV, d, fb), lambda i, j, k: (0, 0, 0)),
        o_spec=pl.BlockSpec((th, d), lambda i, j, k: (i, 0)),
        out_shape=jax.ShapeDtypeStruct((s, d), F32),
        terms=lambda a_ref, b_ref: [(a_ref[k // 4, k % 4], b_ref[k]) for k in range(N_DEV)])
    dw_up = _matmul(
        f"ffn_up_grad_{tag}", a, du, dims=TN, grid=(1, N_DEV, 1),
        a_spec=pl.BlockSpec((s, d), lambda i, j, k: (0, 0)),
        b_spec=pl.BlockSpec((None, None, s, fb), lambda i, j, k: (j // 4, j % 4, 0, 0)),
        o_spec=pl.BlockSpec((None, d, fb), lambda i, j, k: (j, 0, 0)),
        out_shape=jax.ShapeDtypeStruct((N_DEV, d, fb), BF16))
    dh, (dgain,) = _rmsnorm_bwd(f"ffn_norm_bwd_{tag}", h, dout, [(da, gain)])
    return dh, dgain, dw_up, dw_down, dconv_w, dconv_b


def kernel(x, hg_norm, hg_w_in, hg_lb_logits, hg_out_norm, hg_w_out, kv_norm, w_kv, attn_norm, attn_w_q, attn_sinks, attn_w_o, ffn_norm, ffn_w_up, ffn_conv_w, ffn_conv_b, ffn_w_down, final_norm, loss_target, m_hg_norm, m_hg_w_in, m_hg_lb_logits, m_hg_out_norm, m_hg_w_out, m_kv_norm, m_w_kv, m_attn_norm, m_attn_w_q, m_attn_sinks, m_attn_w_o, m_ffn_norm, m_ffn_w_up, m_ffn_conv_w, m_ffn_conv_b, m_ffn_w_down, m_final_norm, v_hg_norm, v_hg_w_in, v_hg_lb_logits, v_hg_out_norm, v_hg_w_out, v_kv_norm, v_w_kv, v_attn_norm, v_attn_w_q, v_attn_sinks, v_attn_w_o, v_ffn_norm, v_ffn_w_up, v_ffn_conv_w, v_ffn_conv_b, v_ffn_w_down, v_final_norm):
    _, s, d = x.shape
    x0, target = x[0], loss_target[0]
    half = hg_w_in.shape[2]
    fs = ffn_conv_w.shape[2]
    fb = 2 * fs
    kvd = w_kv.shape[1]
    nq = d // ATT_HEAD_DIM
    tm = _row_tile(s, MM_ROWS)

    mine = _index(_place())
    gather = lambda tag, shards, after: _send_start("gather_start_" + tag, shards, [_landing(a, mine) for a in shards], False, after)
    w_in, g_hgn, g_lbl, w_out = _all_gather("gather_hg", [hg_w_in[0].astype(BF16), hg_norm, hg_lb_logits, hg_w_out[0].astype(BF16)], HBM_SPEC)
    w_out = w_out.reshape(d, d)
    coming_ffn0 = gather("ffn0", [ffn_w_up[0].astype(BF16), ffn_conv_w, ffn_w_down[0].astype(BF16)], g_hgn)
    hgn = _pinned(g_hgn.reshape(1, d), coming_ffn0[4])
    lbl = g_lbl.transpose(1, 0, 2).reshape(2, d)
    conv_b = [ffn_conv_b[layer].reshape(4, 1, fb) for layer in range(2)]
    gains = [ffn_norm[0:1], ffn_norm[1:2]]
    kvn, fin = kv_norm.reshape(1, d), final_norm.reshape(1, d)

    a0, = _rmsnorm_cast("hg_norm", x0, [hgn])
    p = _matmul(
        "hg_in", a0, w_in, dims=NN, grid=(s // tm, N_DEV, 1),
        a_spec=pl.BlockSpec((tm, d), lambda i, j, k: (i, 0)),
        b_spec=pl.BlockSpec((None, d, half), lambda i, j, k: (j, 0, 0)),
        o_spec=pl.BlockSpec((None, tm, half), lambda i, j, k: (j // 2, i, j % 2)),
        out_shape=jax.ShapeDtypeStruct((4, s, d), BF16), acc_shape=(8, 128))
    o, og, states = _hgrn2_fwd(p, lbl, hg_out_norm)
    x1 = _mm_rows("hg_out", og, w_out, out_dtype=F32, add=x0)
    w_up0, g_cw, w_dn0 = _send_wait("gather_wait_ffn0", coming_ffn0, x1, False)
    w_up, w_dn = [w_up0, None], [w_dn0.reshape(4, fb, d), None]
    conv_w = [g_cw[:, layer].reshape(4, 2, CONV_WIDTH, fs).transpose(0, 2, 1, 3).reshape(4, CONV_WIDTH, fb) for layer in range(2)]
    coming_attn = gather("attn", [w_kv.astype(BF16), attn_w_q[0].astype(BF16), attn_w_o[0].astype(BF16)], w_dn0)
    gains[0] = _pinned(gains[0], coming_attn[4])
    coming = []

    def start_ffn1(u):
        coming.append(gather("ffn1", [ffn_w_up[1].astype(BF16), ffn_w_down[1].astype(BF16)], u))
        return coming[0][4]

    x2, saved0 = _ffn_forward("0", x1, gains[0], w_up[0], w_dn[0], conv_w[0], conv_b[0], start_ffn1)
    coming_ffn1 = coming[0]
    w_kvg, w_q, w_o = _send_wait("gather_wait_attn", coming_attn, x2, False)
    w_kvg, w_q, w_o = w_kvg.reshape(d, kvd), w_q.reshape(d, d), w_o.reshape(d, d)
    akv, a2 = _rmsnorm_cast("attn_norms", x2, [kvn, attn_norm])
    kv = _mm_rows("kv_proj", akv, w_kvg, out_dtype=BF16)
    q = _mm_rows("q_proj", a2, w_q, out_dtype=BF16)
    att = _attn_fwd(q, kv, attn_sinks)
    x3 = _mm_rows("attn_out", att, w_o, out_dtype=F32, add=x2)
    w_up[1], w_dn1 = _send_wait("gather_wait_ffn1", coming_ffn1, x3, False)
    w_dn[1] = w_dn1.reshape(4, fb, d)
    x4, saved1 = _ffn_forward("1", x3, gains[1], w_up[1], w_dn[1], conv_w[1], conv_b[1])
    dx4, dx4_bf, d_fin, loss_part = _loss_head(x4, fin, target)

    dx3, d_fn1, dw_up1, dw_dn1, dcw1, dcb1 = _ffn_backward("1", x3, gains[1], w_up[1], w_dn[1], conv_w[1], conv_b[1], saved1, (dx4, dx4_bf))
    rows = d // N_DEV
    scatter = lambda tag, stacks: _send_start("scatter_start_" + tag, stacks, [_landing(lax.dynamic_index_in_dim(a, mine, keepdims=False), mine) for a in stacks], True)
    going_ffn1 = scatter("ffn1", [dw_up1, dw_dn1.reshape(N_DEV, fs, d)])
    datt = _mm_rows_nt("attn_out_bwd", dx3[1], w_o, out_dtype=BF16)
    dw_o = _mm_tn("attn_out_grad", att, dx3[1])
    dq, dkv_own, dkv_before, dsink = _attn_bwd(q, kv, att, datt, _pinned(attn_sinks, going_ffn1[4]))
    tiles = dkv_before.shape[0]
    dkv = dkv_own.reshape(tiles, s // tiles, kvd)
    dkv = jnp.concatenate([dkv[:, :-WINDOW], dkv[:, -WINDOW:] + jnp.pad(dkv_before[1:], ((0, 1), (0, 0), (0, 0)))], axis=1).reshape(s, kvd)
    da2 = _mm_rows_nt("q_proj_bwd", dq, w_q, out_dtype=F32)
    dw_q = _mm_tn("q_proj_grad", a2, dq)
    dakv = _mm_rows_nt("kv_proj_bwd", dkv, w_kvg, out_dtype=F32)
    dw_kv = _mm_tn("kv_proj_grad", akv, dkv)
    going_attn = scatter("attn", [dw_kv.reshape(N_DEV, rows, kvd), dw_q.reshape(N_DEV, rows, d), dw_o.reshape(N_DEV, rows, d)])
    dx2, (d_kvn, d_attn) = _rmsnorm_bwd("attn_norms_bwd", x2, dx3[0], [(dakv, _pinned(kvn, going_attn[4])), (da2, attn_norm)])
    dx1, d_fn0, dw_up0, dw_dn0, dcw0, dcb0 = _ffn_backward("0", x1, gains[0], w_up[0], w_dn[0], conv_w[0], conv_b[0], saved0, dx2)
    going_ffn0 = scatter("ffn0", [dw_up0, dw_dn0.reshape(N_DEV, fs, d)])
    dog = _mm_rows_nt("hg_out_bwd", dx1[1], w_out, out_dtype=F32)
    dw_out = _mm_tn("hg_out_grad", og, dx1[1])
    dp, d_lbl, d_ogain = _hgrn2_bwd(p, lbl, _pinned(hg_out_norm, going_ffn0[4]), o, dog, states)
    dw_in = _matmul(
        "hg_in_grad", a0, dp, dims=TN, grid=(1, N_DEV, 1),
        a_spec=pl.BlockSpec((s, d), lambda i, j, k: (0, 0)),
        b_spec=pl.BlockSpec((None, s, half), lambda i, j, k: (j // 2, 0, j % 2)),
        o_spec=pl.BlockSpec((None, d, half), lambda i, j, k: (j, 0, 0)),
        out_shape=jax.ShapeDtypeStruct((N_DEV, d, half), BF16))
    going_hg = scatter("hg", [dw_in, dw_out.reshape(N_DEV, rows, d)])
    th = _row_tile(s, MM_ROWS // 2)
    da0 = _matmul(
        "hg_in_bwd", dp, w_in, dims=NT, grid=(s // th, 1, 1),
        a_spec=pl.BlockSpec((4, th, d), lambda i, j, k: (0, i, 0)),
        b_spec=pl.BlockSpec((N_DEV, d, half), lambda i, j, k: (0, 0, 0)),
        o_spec=pl.BlockSpec((th, d), lambda i, j, k: (i, 0)),
        out_shape=jax.ShapeDtypeStruct((s, d), F32),
        terms=lambda a_ref, b_ref: [(a_ref[k // 2, :, (k % 2) * half:(k % 2 + 1) * half], b_ref[k]) for k in range(N_DEV)])
    (dx0, _), (d_hgn,) = _rmsnorm_bwd("hg_norm_bwd", x0, dx1[0], [(da0, _pinned(hgn, going_hg[4]))])

    arrive = lambda tag, going, after: _send_wait("scatter_wait_" + tag, going, after, True)
    (l_up1, l_dn1), (l_kv, l_q, l_o), (l_up0, l_dn0) = arrive("ffn1", going_ffn1, dx0), arrive("attn", going_attn, dx0), arrive("ffn0", going_ffn0, dx0)
    big = {}
    for tag, w, m, v, part in [
            ("w_kv", w_kv, m_w_kv, v_w_kv, l_kv), ("attn_w_q", attn_w_q[0], m_attn_w_q[0], v_attn_w_q[0], l_q),
            ("attn_w_o", attn_w_o[0], m_attn_w_o[0], v_attn_w_o[0], l_o)]:
        big[tag] = _adamw_shard("adamw_" + tag, w, m, v, part)
    for tag, w, m, v, parts in [("ffn_w_up", ffn_w_up, m_ffn_w_up, v_ffn_w_up, (l_up0, l_up1)), ("ffn_w_down", ffn_w_down, m_ffn_w_down, v_ffn_w_down, (l_dn0, l_dn1))]:
        first = _adamw_shard("adamw_" + tag + "0", w, m, v, parts[0], layer=0)
        big[tag] = _adamw_shard("adamw_" + tag + "1", w, m, v, parts[1], layer=1, into=first)
    lead = lambda tag: [a[None] for a in big[tag]]

    as_blocks = lambda a, r: a.reshape(r, N_DEV, -1).transpose(1, 0, 2).reshape(N_DEV * r, -1)
    d_cw = jnp.concatenate([g.transpose(1, 0, 2).reshape(CONV_WIDTH, 4 * fb) for g in (dcw0, dcw1)], axis=0)
    parts = [d_fin, jnp.concatenate([d_fn0, d_fn1], axis=0), jnp.concatenate([dcb0.reshape(1, 4 * fb), dcb1.reshape(1, 4 * fb)], axis=0),
             as_blocks(d_cw, 2 * CONV_WIDTH), d_attn, jnp.sum(dsink[:, :, 0], axis=0).reshape(1, nq), d_kvn, d_ogain,
             as_blocks(d_hgn, 1), as_blocks(d_lbl, 2), loss_part]
    wide = [2]
    packs = [[parts[i] for i in wide], [part for i, part in enumerate(parts) if i not in wide]]
    places = [None] * len(parts)
    for which, members in enumerate([wide, [i for i in range(len(parts)) if i not in wide]]):
        for i, off in zip(members, _pack_rows(packs[which])[0]):
            places[i] = (which, off)
    gathered = _all_gather("gather_small_grads", [_pack("pack_wide_grads", packs[0]), _pack("pack_narrow_grads", packs[1])], VMEM_SPEC)
    two = lambda a: a.reshape(-1, a.shape[-1])
    small = [(fin, m_final_norm.reshape(1, d), v_final_norm.reshape(1, d), False), (ffn_norm, m_ffn_norm, v_ffn_norm, False),
             (ffn_conv_b, m_ffn_conv_b, v_ffn_conv_b, False), (two(ffn_conv_w), two(m_ffn_conv_w), two(v_ffn_conv_w), True),
             (attn_norm, m_attn_norm, v_attn_norm, False), (attn_sinks, m_attn_sinks, v_attn_sinks, False),
             (kvn, m_kv_norm.reshape(1, d), v_kv_norm.reshape(1, d), False), (hg_out_norm, m_hg_out_norm, v_hg_out_norm, False),
             (hg_norm, m_hg_norm, v_hg_norm, True), (hg_lb_logits, m_hg_lb_logits, v_hg_lb_logits, True)]
    res = _adamw_small(gathered, places, small)
    l_in, l_out = arrive("hg", going_hg, gathered[1])
    big["hg_w_in"] = _adamw_shard("adamw_hg_w_in", hg_w_in[0], m_hg_w_in[0], v_hg_w_in[0], l_in)
    big["hg_w_out"] = _adamw_shard("adamw_hg_w_out", hg_w_out[0], m_hg_w_out[0], v_hg_w_out[0], l_out)
    names = ["final_norm", "ffn_norm", "ffn_conv_b", "ffn_conv_w", "attn_norm", "attn_sinks", "kv_norm", "hg_out_norm", "hg_norm", "hg_lb_logits"]
    shapes = {"final_norm": final_norm.shape, "kv_norm": kv_norm.shape, "ffn_conv_w": ffn_conv_w.shape}
    out = {n: [a.reshape(shapes[n]) if n in shapes else a for a in res[4 * i:4 * i + 4]] for i, n in enumerate(names)}
    out.update(hg_w_in=lead("hg_w_in"), hg_w_out=lead("hg_w_out"), w_kv=big["w_kv"], attn_w_q=lead("attn_w_q"), attn_w_o=lead("attn_w_o"),
               ffn_w_up=big["ffn_w_up"], ffn_w_down=big["ffn_w_down"])
    order = ["hg_norm", "hg_w_in", "hg_lb_logits", "hg_out_norm", "hg_w_out", "kv_norm", "w_kv", "attn_norm", "attn_w_q", "attn_sinks",
             "attn_w_o", "ffn_norm", "ffn_w_up", "ffn_conv_w", "ffn_conv_b", "ffn_w_down", "final_norm"]
    loss = res[-1][0, 0]
    return (loss, dx0[None], *[out[n][0] for n in order], *[out[n][1] for n in order], *[out[n][2] for n in order], *[out[n][3] for n in order])
```

```python
import functools
import math

import jax
import jax.numpy as jnp
from jax import lax
from jax.experimental import pallas as pl
from jax.experimental.pallas import tpu as pltpu

F32 = jnp.float32
BF16 = jnp.bfloat16

EPS = 1e-6
HG_EXPAND = 128
HG_CHUNK = 32
ATT_HEAD_DIM = 64
ATT_KV_HEADS = 2
WINDOW = 128
CONV_WIDTH = 3
ADAM_LR = 0.001
ADAM_B1 = 0.9
ADAM_B2 = 0.999
ADAM_EPS = 1e-08
ADAM_WD = 0.01
ADAM_STEP = 10

N_DEV = 8
VMEM_LIMIT = 48 * 1024 * 1024
NEG = -1e30

NN = (((1,), (0,)), ((), ()))
NT = (((1,), (1,)), ((), ()))
TN = (((0,), (0,)), ((), ()))
MESH = pl.DeviceIdType.MESH


def _dot(a, b, dims=NN):
    return lax.dot_general(a.astype(BF16), b.astype(BF16), dims, preferred_element_type=F32)


def _sigmoid(x):
    return 0.5 * jnp.tanh(0.5 * x) + 0.5


def _silu(x):
    return x * _sigmoid(x)


def _silu_and_grad(x):
    s = _sigmoid(x)
    return x * s, s * (1.0 + x * (1.0 - s))


def _dsilu(x):
    return _silu_and_grad(x)[1]


def _params(semantics):
    return pltpu.CompilerParams(dimension_semantics=semantics, vmem_limit_bytes=VMEM_LIMIT)


def _row_tile(rows, want=512):
    return min(rows, want)


MM_ROWS = 1024


def _matmul(name, a, b, *, dims, grid, a_spec, b_spec, o_spec, out_shape, acc_shape=(8, 128), add=None, add_spec=None, terms=None):
    nk = grid[2]

    def body(*refs):
        if add is None:
            a_ref, b_ref, o_ref, acc = refs
        else:
            a_ref, b_ref, add_ref, o_ref, acc = refs
        k = pl.program_id(2)
        pairs = [(a_ref[...], b_ref[...])] if terms is None else terms(a_ref, b_ref)
        part = _dot(*pairs[0], dims)
        for pair in pairs[1:]:
            part = part + _dot(*pair, dims)

        def finish(total):
            if add is not None:
                total = total + add_ref[...]
            o_ref[...] = total.astype(o_ref.dtype)

        if nk == 1:
            finish(part)
        else:
            @pl.when(k == 0)
            def _():
                acc[...] = part

            @pl.when(k > 0)
            def _():
                acc[...] += part

            @pl.when(k == nk - 1)
            def _():
                finish(acc[...])

    in_specs = [a_spec, b_spec] + ([] if add is None else [add_spec])
    args = (a, b) + (() if add is None else (add,))
    return pl.pallas_call(
        body, name=name, grid=grid, in_specs=in_specs, out_specs=o_spec, out_shape=out_shape,
        scratch_shapes=[pltpu.VMEM(acc_shape, F32)],
        compiler_params=_params(("parallel", "parallel", "arbitrary")),
    )(*args)


def _mm_rows(name, a, w, *, out_dtype, add=None):
    s, kdim = a.shape
    n = w.shape[1]
    tm = _row_tile(s, MM_ROWS)
    return _matmul(
        name, a, w, dims=NN, grid=(s // tm, 1, 1),
        a_spec=pl.BlockSpec((tm, kdim), lambda i, j, k: (i, 0)),
        b_spec=pl.BlockSpec((kdim, n), lambda i, j, k: (0, 0)),
        o_spec=pl.BlockSpec((tm, n), lambda i, j, k: (i, 0)),
        out_shape=jax.ShapeDtypeStruct((s, n), out_dtype), acc_shape=(8, 128),
        add=add, add_spec=None if add is None else pl.BlockSpec((tm, n), lambda i, j, k: (i, 0)),
    )


def _mm_rows_nt(name, a, w, *, out_dtype):
    s, n = a.shape
    kdim = w.shape[0]
    tm = _row_tile(s, MM_ROWS)
    return _matmul(
        name, a, w, dims=NT, grid=(s // tm, 1, 1),
        a_spec=pl.BlockSpec((tm, n), lambda i, j, k: (i, 0)),
        b_spec=pl.BlockSpec((kdim, n), lambda i, j, k: (0, 0)),
        o_spec=pl.BlockSpec((tm, kdim), lambda i, j, k: (i, 0)),
        out_shape=jax.ShapeDtypeStruct((s, kdim), out_dtype), acc_shape=(8, 128),
    )


def _mm_tn(name, a, g):
    s, m = a.shape
    n = g.shape[1]
    tn = min(n, 512)
    return _matmul(
        name, a, g, dims=TN, grid=(1, n // tn, 1),
        a_spec=pl.BlockSpec((s, m), lambda i, j, k: (0, 0)),
        b_spec=pl.BlockSpec((s, tn), lambda i, j, k: (0, j)),
        o_spec=pl.BlockSpec((m, tn), lambda i, j, k: (0, j)),
        out_shape=jax.ShapeDtypeStruct((m, n), BF16),
    )


def _rmsnorm_cast(name, h, gains):
    s, d = h.shape
    tm = _row_tile(s)
    n = len(gains)

    def body(*refs):
        h_ref, g_refs, o_refs = refs[0], refs[1:1 + n], refs[1 + n:]
        xv = h_ref[...]
        xhat = xv * lax.rsqrt(jnp.mean(xv * xv, axis=-1, keepdims=True) + EPS)
        for g_ref, o_ref in zip(g_refs, o_refs):
            o_ref[...] = (xhat * g_ref[...]).astype(BF16)

    row = pl.BlockSpec((tm, d), lambda i: (i, 0))
    vec = pl.BlockSpec((1, d), lambda i: (0, 0))
    return pl.pallas_call(
        body, name=name, grid=(s // tm,), in_specs=[row] + [vec] * n, out_specs=[row] * n,
        out_shape=[jax.ShapeDtypeStruct((s, d), BF16)] * n, compiler_params=_params(("parallel",)),
    )(h, *gains)


def _rmsnorm_bwd(name, h, dres, branches):
    s, d = h.shape
    tm = _row_tile(s)
    n = len(branches)

    def body(*refs):
        h_ref, dres_ref = refs[0], refs[1]
        da_refs, g_refs = refs[2:2 + n], refs[2 + n:2 + 2 * n]
        dh_ref, dhb_ref, dg_refs = refs[2 + 2 * n], refs[3 + 2 * n], refs[4 + 2 * n:]
        i = pl.program_id(0)
        xv = h_ref[...]
        r = lax.rsqrt(jnp.mean(xv * xv, axis=-1, keepdims=True) + EPS)
        xhat = xv * r
        total = dres_ref[...]
        for da_ref, g_ref, dg_ref in zip(da_refs, g_refs, dg_refs):
            da = da_ref[...]
            dgain = jnp.sum(da * xhat, axis=0, keepdims=True)

            @pl.when(i == 0)
            def _():
                dg_ref[...] = dgain

            @pl.when(i > 0)
            def _():
                dg_ref[...] += dgain

            dxhat = da * g_ref[...]
            total = total + r * (dxhat - xhat * jnp.mean(dxhat * xhat, axis=-1, keepdims=True))
        dh_ref[...] = total
        dhb_ref[...] = total.astype(BF16)

    row = pl.BlockSpec((tm, d), lambda i: (i, 0))
    vec = pl.BlockSpec((1, d), lambda i: (0, 0))
    outs = pl.pallas_call(
        body, name=name, grid=(s // tm,), in_specs=[row, row] + [row] * n + [vec] * n, out_specs=[row, row] + [vec] * n,
        out_shape=[jax.ShapeDtypeStruct((s, d), F32), jax.ShapeDtypeStruct((s, d), BF16)] + [jax.ShapeDtypeStruct((1, d), F32)] * n,
        compiler_params=_params(("arbitrary",)),
    )(h, dres, *[b[0] for b in branches], *[b[1] for b in branches])
    return (outs[0], outs[1]), outs[2:]


def _loss_head(h, gain, target):
    s, d = h.shape
    tm = _row_tile(s)

    def body(h_ref, g_ref, t_ref, dh_ref, dhb_ref, dg_ref, loss_ref):
        i = pl.program_id(0)
        xv = h_ref[...]
        r = lax.rsqrt(jnp.mean(xv * xv, axis=-1, keepdims=True) + EPS)
        xhat = xv * r
        err = xhat * g_ref[...] - t_ref[...]
        dy = err * (1.0 / d)
        part = jnp.zeros((1, 128), F32) + 0.5 * jnp.sum(jnp.mean(err * err, axis=-1, keepdims=True))
        dgain = jnp.sum(dy * xhat, axis=0, keepdims=True)

        @pl.when(i == 0)
        def _():
            dg_ref[...] = dgain
            loss_ref[...] = part

        @pl.when(i > 0)
        def _():
            dg_ref[...] += dgain
            loss_ref[...] += part

        dxhat = dy * g_ref[...]
        dh = r * (dxhat - xhat * jnp.mean(dxhat * xhat, axis=-1, keepdims=True))
        dh_ref[...] = dh
        dhb_ref[...] = dh.astype(BF16)

    row = pl.BlockSpec((tm, d), lambda i: (i, 0))
    vec = pl.BlockSpec((1, d), lambda i: (0, 0))
    return pl.pallas_call(
        body, name="loss_head", grid=(s // tm,), in_specs=[row, vec, row],
        out_specs=[row, row, vec, pl.BlockSpec((1, 128), lambda i: (0, 0))],
        out_shape=[jax.ShapeDtypeStruct((s, d), F32), jax.ShapeDtypeStruct((s, d), BF16), jax.ShapeDtypeStruct((1, d), F32),
                   jax.ShapeDtypeStruct((1, 128), F32)],
        compiler_params=_params(("arbitrary",)),
    )(h, gain, target)


def _bdot(a, b, ca, cb):
    return lax.dot_general(a.astype(BF16), b.astype(BF16), (((ca,), (cb,)), ((0,), (0,))), preferred_element_type=F32)


def _chunk_cumsum(xv, reverse=False):
    n = xv.shape[0]
    row = lax.broadcasted_iota(jnp.int32, xv.shape, 0) % HG_CHUNK
    step = 1
    while step < HG_CHUNK:
        if reverse:
            xv = xv + jnp.where(row < HG_CHUNK - step, pltpu.roll(xv, n - step, axis=0), 0.0)
        else:
            xv = xv + jnp.where(row >= step, pltpu.roll(xv, step, axis=0), 0.0)
        step *= 2
    return xv


def _hg_terms(p_ref, lbl_ref):
    pq = p_ref[0].astype(F32)
    pf = p_ref[1].astype(F32)
    lb = _sigmoid(lbl_ref[0:1, :] - lbl_ref[1:2, :])
    sig = _sigmoid(pf)
    fg = lb + (1.0 - lb) * sig
    nc = pq.shape[0] // HG_CHUNK
    chunks = lambda a: a.reshape(nc, HG_CHUNK, HG_EXPAND)
    q = chunks(_silu(pq) * HG_EXPAND ** -0.5)
    k = chunks(1.0 - fg)
    v = chunks(p_ref[2].astype(F32))
    g = chunks(_chunk_cumsum(jnp.log(fg)))
    gm = g[:, HG_CHUNK // 2 - 1:HG_CHUNK // 2, :]
    gl = g[:, HG_CHUNK - 1:HG_CHUNK, :]
    e_mid, e_inv, e_all, e_end = jnp.exp(g - gm), jnp.exp(gm - g), jnp.exp(g), jnp.exp(gl - g)
    terms = dict(q=q, k=k, v=v, qd=q * e_all, qt=q * e_mid, kt=k * e_inv, kd=k * e_end, e_last=jnp.exp(gl),
                 e_mid=e_mid, e_inv=e_inv, e_all=e_all, e_end=e_end)
    return terms, (pq, sig, fg, lb)


def _causal(nc):
    r = lax.broadcasted_iota(jnp.int32, (nc, HG_CHUNK, HG_CHUNK), 1)
    c = lax.broadcasted_iota(jnp.int32, (nc, HG_CHUNK, HG_CHUNK), 2)
    return r >= c


def _hgrn2_fwd(p, lb_logits, out_gain):
    _, s, d = p.shape
    heads = d // HG_EXPAND
    t = _row_tile(s)
    nc = t // HG_CHUNK

    def body(p_ref, lbl_ref, gain_ref, o_ref, og_ref, st_ref, state, decay):
        @pl.when(pl.program_id(1) == 0)
        def _():
            state[...] = jnp.zeros_like(state)

        tm, _ = _hg_terms(p_ref, lbl_ref)
        decay[...] = tm["e_last"]
        st_ref[...] = _bdot(tm["v"], tm["kd"], 1, 1)

        def chunk(c, carry):
            add = st_ref[c]
            st = state[...]
            st_ref[c] = st
            state[...] = st * decay[c] + add
            return carry

        lax.fori_loop(0, nc, chunk, 0)
        a = jnp.where(_causal(nc), _bdot(tm["qt"], tm["kt"], 2, 2), 0.0)
        ov = (_bdot(tm["qd"], st_ref[...], 2, 2) + _bdot(a, tm["v"], 2, 1)).reshape(t, HG_EXPAND)
        o_ref[...] = ov
        on = ov * lax.rsqrt(jnp.mean(ov * ov, axis=-1, keepdims=True) + EPS) * gain_ref[...]
        og_ref[...] = (on * _silu(p_ref[3].astype(F32))).astype(BF16)

    blk = pl.BlockSpec((t, HG_EXPAND), lambda h, b: (b, h))
    return pl.pallas_call(
        body, name="hgrn2_fwd", grid=(heads, s // t),
        in_specs=[pl.BlockSpec((4, t, HG_EXPAND), lambda h, b: (0, b, h)), pl.BlockSpec((2, HG_EXPAND), lambda h, b: (0, h)),
                  pl.BlockSpec((1, HG_EXPAND), lambda h, b: (0, 0))],
        out_specs=[blk, blk, pl.BlockSpec((None, nc, HG_EXPAND, HG_EXPAND), lambda h, b: (h, b, 0, 0))],
        out_shape=[jax.ShapeDtypeStruct((s, d), F32), jax.ShapeDtypeStruct((s, d), BF16),
                   jax.ShapeDtypeStruct((heads, s // HG_CHUNK, HG_EXPAND, HG_EXPAND), F32)],
        scratch_shapes=[pltpu.VMEM((HG_EXPAND, HG_EXPAND), F32), pltpu.VMEM((nc, 1, HG_EXPAND), F32)],
        compiler_params=_params(("parallel", "arbitrary")),
    )(p, lb_logits, out_gain)


def _hgrn2_bwd(p, lb_logits, out_gain, o, dog, states):
    _, s, d = p.shape
    heads = d // HG_EXPAND
    t = _row_tile(s)
    nc = t // HG_CHUNK
    nb = s // t

    def body(p_ref, lbl_ref, gain_ref, o_ref, dog_ref, st_ref, dp_ref, dlbl_ref, dgain_ref, dstate, decay, dst_s):
        h, b = pl.program_id(0), pl.program_id(1)

        @pl.when(b == 0)
        def _():
            dstate[...] = jnp.zeros_like(dstate)

        tm, (pq, sig, fg, lb) = _hg_terms(p_ref, lbl_ref)
        pg = p_ref[3].astype(F32)
        ov = o_ref[...]
        r = lax.rsqrt(jnp.mean(ov * ov, axis=-1, keepdims=True) + EPS)
        ohat = ov * r
        dogv = dog_ref[...]
        d_on = dogv * _silu(pg)
        dp_ref[3] = (dogv * ohat * gain_ref[...] * _dsilu(pg)).astype(BF16)
        dgain = jnp.sum(d_on * ohat, axis=0, keepdims=True)

        @pl.when((h == 0) & (b == 0))
        def _():
            dgain_ref[...] = dgain

        @pl.when((h > 0) | (b > 0))
        def _():
            dgain_ref[...] += dgain

        dohat = d_on * gain_ref[...]
        do = (r * (dohat - ohat * jnp.mean(dohat * ohat, axis=-1, keepdims=True))).reshape(nc, HG_CHUNK, HG_EXPAND)

        decay[...] = tm["e_last"]
        dst_s[...] = _bdot(do, tm["qd"], 1, 1)

        def chunk(i, carry):
            c = nc - 1 - i
            add = dst_s[c]
            dst = dstate[...]
            dst_s[c] = dst
            dstate[...] = dst * decay[c] + add
            return carry

        lax.fori_loop(0, nc, chunk, 0)
        st, dst = st_ref[...], dst_s[...]
        causal = _causal(nc)
        a = jnp.where(causal, _bdot(tm["qt"], tm["kt"], 2, 2), 0.0)
        da = jnp.where(causal, _bdot(do, tm["v"], 2, 2), 0.0)
        dqt = _bdot(da, tm["kt"], 2, 1)
        dkt = _bdot(da, tm["qt"], 1, 1)
        dqd = _bdot(do, st, 2, 1)
        dkd = _bdot(tm["v"], dst, 2, 1)
        dv = _bdot(a, do, 1, 1) + _bdot(tm["kd"], dst, 2, 2)
        dq = dqt * tm["e_mid"] + dqd * tm["e_all"]
        dk = dkt * tm["e_inv"] + dkd * tm["e_end"]
        dg = dqt * tm["qt"] - dkt * tm["kt"] + dqd * tm["qd"] - dkd * tm["kd"]
        dgl = jnp.sum(dkd * tm["kd"], axis=1, keepdims=True) + tm["e_last"] * jnp.sum(dst * st, axis=1, keepdims=True)
        last_row = lax.broadcasted_iota(jnp.int32, (nc, HG_CHUNK, HG_EXPAND), 1) == HG_CHUNK - 1
        flat = lambda a3: a3.reshape(t, HG_EXPAND)
        dlf = _chunk_cumsum(flat(dg + jnp.where(last_row, dgl, 0.0)), reverse=True)
        dfg = dlf / fg - flat(dk)
        dlb = jnp.sum(dfg * (1.0 - sig), axis=0, keepdims=True)
        dl0 = dlb * lb * (1.0 - lb)
        dlbl = jnp.concatenate([dl0, -dl0], axis=0)

        @pl.when(b == 0)
        def _():
            dlbl_ref[...] = dlbl

        @pl.when(b > 0)
        def _():
            dlbl_ref[...] += dlbl

        dp_ref[0] = (flat(dq) * HG_EXPAND ** -0.5 * _dsilu(pq)).astype(BF16)
        dp_ref[1] = (dfg * (1.0 - lb) * sig * (1.0 - sig)).astype(BF16)
        dp_ref[2] = flat(dv).astype(BF16)

    blk = pl.BlockSpec((t, HG_EXPAND), lambda h, b: (nb - 1 - b, h))
    pblk = pl.BlockSpec((4, t, HG_EXPAND), lambda h, b: (0, nb - 1 - b, h))
    return pl.pallas_call(
        body, name="hgrn2_bwd", grid=(heads, nb),
        in_specs=[pblk, pl.BlockSpec((2, HG_EXPAND), lambda h, b: (0, h)), pl.BlockSpec((1, HG_EXPAND), lambda h, b: (0, 0)),
                  blk, blk, pl.BlockSpec((None, nc, HG_EXPAND, HG_EXPAND), lambda h, b: (h, nb - 1 - b, 0, 0))],
        out_specs=[pblk, pl.BlockSpec((2, HG_EXPAND), lambda h, b: (0, h)), pl.BlockSpec((1, HG_EXPAND), lambda h, b: (0, 0))],
        out_shape=[jax.ShapeDtypeStruct((4, s, d), BF16), jax.ShapeDtypeStruct((2, d), F32), jax.ShapeDtypeStruct((1, HG_EXPAND), F32)],
        scratch_shapes=[pltpu.VMEM((HG_EXPAND, HG_EXPAND), F32), pltpu.VMEM((nc, 1, HG_EXPAND), F32),
                        pltpu.VMEM((nc, HG_EXPAND, HG_EXPAND), F32)],
        compiler_params=_params(("arbitrary", "arbitrary")),
    )(p, lb_logits, out_gain, o, dog, states)


HALO = 8
FFN_FWD_ROWS = 512
FFN_BWD_ROWS = 256


def _shift_down(xv, n):
    return pltpu.roll(xv, n, axis=0)


def _shift_up(xv, n):
    return pltpu.roll(xv, xv.shape[0] - n, axis=0)


def _ffn_hidden_down(name, u, conv_w, conv_b, w_down, h):
    _, nj, s, fb = u.shape
    d = w_down.shape[2]
    tm = _row_tile(s, FFN_FWD_ROWS)
    per = tm // HALO

    def body(gate_ref, prev_ref, val_ref, w_ref, b_ref, wd_ref, h_ref, hid_ref, o_ref):
        i = pl.program_id(0)
        total = h_ref[...]
        for j in range(nj):
            prev = jnp.where(i > 0, prev_ref[j].astype(F32), 0.0)
            ext = jnp.concatenate([prev, gate_ref[j].astype(F32)], axis=0)
            conv = b_ref[j] + w_ref[j, 2:3, :] * ext[HALO:]
            conv = conv + w_ref[j, 1:2, :] * _shift_down(ext, 1)[HALO:]
            conv = conv + w_ref[j, 0:1, :] * _shift_down(ext, 2)[HALO:]
            hidden = (_silu(conv) * val_ref[j].astype(F32)).astype(BF16)
            hid_ref[j] = hidden
            total = total + _dot(hidden, wd_ref[j])
        o_ref[...] = total

    row = pl.BlockSpec((tm, d), lambda i: (i, 0))
    return pl.pallas_call(
        body, name=name, grid=(s // tm,),
        in_specs=[pl.BlockSpec((None, nj, tm, fb), lambda i: (0, 0, i, 0)),
                  pl.BlockSpec((None, nj, HALO, fb), lambda i: (0, 0, jnp.maximum(i * per - 1, 0), 0)),
                  pl.BlockSpec((None, nj, tm, fb), lambda i: (1, 0, i, 0)),
                  pl.BlockSpec((nj, CONV_WIDTH, fb), lambda i: (0, 0, 0)), pl.BlockSpec((nj, 1, fb), lambda i: (0, 0, 0)),
                  pl.BlockSpec((nj, fb, d), lambda i: (0, 0, 0)), row],
        out_specs=[pl.BlockSpec((nj, tm, fb), lambda i: (0, i, 0)), row],
        out_shape=[jax.ShapeDtypeStruct((nj, s, fb), BF16), jax.ShapeDtypeStruct((s, d), F32)],
        compiler_params=_params(("parallel",)),
    )(u, u, u, conv_w, conv_b, w_down, h)


def _ffn_hidden_up_bwd(name, u, dh, conv_w, conv_b, w_up):
    _, nj, s, fb = u.shape
    d = w_up.shape[1]
    tm = _row_tile(s, FFN_BWD_ROWS)
    per = tm // HALO
    nblk = s // HALO
    ni = s // tm

    def body(gate_ref, gprev_ref, gnext_ref, val_ref, vnext_ref, dh_ref, dhnext_ref, w_ref, b_ref, wu_ref, du_ref, dw_ref, db_ref, da_ref):
        i = pl.program_id(0)
        has_next = i < ni - 1
        total = None
        for j in range(nj):
            gprev = jnp.where(i > 0, gprev_ref[j].astype(F32), 0.0)
            gext = jnp.concatenate([gprev, gate_ref[j].astype(F32), gnext_ref[j].astype(F32)], axis=0)
            vext = jnp.concatenate([val_ref[j].astype(F32), vnext_ref[j].astype(F32)], axis=0)
            dhext = jnp.concatenate([dh_ref[j].astype(F32), jnp.where(has_next, dhnext_ref[j].astype(F32), 0.0)], axis=0)
            g0 = gext[HALO:]
            g1 = _shift_down(gext, 1)[HALO:]
            g2 = _shift_down(gext, 2)[HALO:]
            conv = b_ref[j] + w_ref[j, 2:3, :] * g0 + w_ref[j, 1:2, :] * g1 + w_ref[j, 0:1, :] * g2
            act, dact = _silu_and_grad(conv)
            dconv = dhext * vext * dact
            dgate = (w_ref[j, 2:3, :] * dconv + w_ref[j, 1:2, :] * _shift_up(dconv, 1) + w_ref[j, 0:1, :] * _shift_up(dconv, 2))[:tm].astype(BF16)
            dval = (dhext * act)[:tm].astype(BF16)
            du_ref[0, j] = dgate
            du_ref[1, j] = dval
            part = _dot(dgate, wu_ref[j], NT) + _dot(dval, wu_ref[nj + j], NT)
            total = part if total is None else total + part
            own = dconv[:tm]
            dw = jnp.concatenate([jnp.sum(own * g2[:tm], axis=0, keepdims=True), jnp.sum(own * g1[:tm], axis=0, keepdims=True),
                                  jnp.sum(own * g0[:tm], axis=0, keepdims=True)], axis=0)
            db = jnp.sum(own, axis=0, keepdims=True)

            @pl.when(i == 0)
            def _():
                dw_ref[j] = dw
                db_ref[j] = db

            @pl.when(i > 0)
            def _():
                dw_ref[j] += dw
                db_ref[j] += db

        da_ref[...] = total

    def tile(part):
        return pl.BlockSpec((None, nj, tm, fb), lambda i: (part, 0, i, 0))

    def after(part):
        return pl.BlockSpec((None, nj, HALO, fb), lambda i: (part, 0, jnp.minimum((i + 1) * per, nblk - 1), 0))

    return pl.pallas_call(
        body, name=name, grid=(ni,),
        in_specs=[tile(0), pl.BlockSpec((None, nj, HALO, fb), lambda i: (0, 0, jnp.maximum(i * per - 1, 0), 0)), after(0),
                  tile(1), after(1),
                  pl.BlockSpec((nj, tm, fb), lambda i: (0, i, 0)),
                  pl.BlockSpec((nj, HALO, fb), lambda i: (0, jnp.minimum((i + 1) * per, nblk - 1), 0)),
                  pl.BlockSpec((nj, CONV_WIDTH, fb), lambda i: (0, 0, 0)), pl.BlockSpec((nj, 1, fb), lambda i: (0, 0, 0)),
                  pl.BlockSpec((2 * nj, d, fb), lambda i: (0, 0, 0))],
        out_specs=[pl.BlockSpec((2, nj, tm, fb), lambda i: (0, 0, i, 0)),
                   pl.BlockSpec((nj, CONV_WIDTH, fb), lambda i: (0, 0, 0)), pl.BlockSpec((nj, 1, fb), lambda i: (0, 0, 0)),
                   pl.BlockSpec((tm, d), lambda i: (i, 0))],
        out_shape=[jax.ShapeDtypeStruct((2, nj, s, fb), BF16), jax.ShapeDtypeStruct((nj, CONV_WIDTH, fb), F32),
                   jax.ShapeDtypeStruct((nj, 1, fb), F32), jax.ShapeDtypeStruct((s, d), F32)],
        compiler_params=_params(("arbitrary",)),
    )(u, u, u, u, u, dh, dh, conv_w, conv_b, w_up)


ATT_TILE = 512


def _stack_heads(ref, rows, first_head, count):
    hd = ATT_HEAD_DIM
    return jnp.concatenate([ref[rows, (first_head + j) * hd:(first_head + j + 1) * hd] for j in range(count)], axis=0)


def _unstack_heads(stacked, ref, rows, first_head, count):
    hd = ATT_HEAD_DIM
    for pair in range(count // 2):
        both = [stacked[(2 * pair + j) * WINDOW:(2 * pair + j + 1) * WINDOW, :] for j in range(2)]
        ref[rows, (first_head + 2 * pair) * hd:(first_head + 2 * pair + 2) * hd] = jnp.concatenate(both, axis=1).astype(ref.dtype)


def _attn_probs_t(kb, qs, sink_ref, first_head, count, first, n_heads):
    lanes = count * WINDOW
    ik = lax.broadcasted_iota(jnp.int32, (2 * WINDOW, lanes), 0)
    iq = lax.broadcasted_iota(jnp.int32, (2 * WINDOW, lanes), 1) % WINDOW
    dist = iq + WINDOW - ik
    valid = (dist >= 0) & (dist < WINDOW) & (ik >= jnp.where(first, WINDOW, 0))
    per_head = lambda values: jnp.concatenate([jnp.zeros((1, WINDOW), F32) + v for v in values], axis=1)
    slope = per_head([2.0 ** (-8.0 * (first_head + j + 1) / n_heads) for j in range(count)])
    sink = per_head([sink_ref[0, first_head + j] for j in range(count)])
    sc = jnp.where(valid, _dot(kb, qs, NT) * ATT_HEAD_DIM ** -0.5 - slope * dist.astype(F32), NEG)
    m = jnp.maximum(jnp.max(sc, axis=0, keepdims=True), sink)
    e = jnp.exp(sc - m)
    es = jnp.exp(sink - m)
    inv = 1.0 / (jnp.sum(e, axis=0, keepdims=True) + es)
    return e * inv, es * inv


def _attn_specs(s, d, kvd, tq):
    per = tq // WINDOW
    return [pl.BlockSpec((tq, d), lambda i: (i, 0)), pl.BlockSpec((tq, kvd), lambda i: (i, 0)),
            pl.BlockSpec((WINDOW, kvd), lambda i: (jnp.maximum(i * per - 1, 0), 0))]


def _attn_fwd(q, kv, sinks):
    s, d = q.shape
    kvd = kv.shape[1]
    half = kvd // 2
    hd = ATT_HEAD_DIM
    nq = d // hd
    group = nq // ATT_KV_HEADS
    tq = min(s, ATT_TILE)
    per = tq // WINDOW

    def body(q_ref, kvc_ref, kvp_ref, sink_ref, o_ref, band):
        i = pl.program_id(0)
        band[0:WINDOW, :] = kvp_ref[...]
        band[WINDOW:, :] = kvc_ref[...]

        def block(b, carry):
            rows = pl.ds(pl.multiple_of(b * WINDOW, WINDOW), WINDOW)
            keys = pl.ds(pl.multiple_of(b * WINDOW, WINDOW), 2 * WINDOW)
            first = (i * per + b) == 0
            for g in range(ATT_KV_HEADS):
                p, _ = _attn_probs_t(band[keys, g * hd:(g + 1) * hd], _stack_heads(q_ref, rows, g * group, group), sink_ref,
                                     g * group, group, first, nq)
                out_t = _dot(band[keys, half + g * hd:half + (g + 1) * hd], p, TN)
                _unstack_heads(out_t.T, o_ref, rows, g * group, group)
            return carry

        lax.fori_loop(0, per, block, 0)

    return pl.pallas_call(
        body, name="attn_fwd", grid=(s // tq,),
        in_specs=_attn_specs(s, d, kvd, tq) + [pl.BlockSpec(memory_space=pltpu.SMEM)],
        out_specs=pl.BlockSpec((tq, d), lambda i: (i, 0)), out_shape=jax.ShapeDtypeStruct((s, d), BF16),
        scratch_shapes=[pltpu.VMEM((tq + WINDOW, kvd), BF16)], compiler_params=_params(("parallel",)),
    )(q, kv, kv, sinks)


def _attn_bwd(q, kv, o, do, sinks):
    s, d = q.shape
    kvd = kv.shape[1]
    half = kvd // 2
    hd = ATT_HEAD_DIM
    nq = d // hd
    group = nq // ATT_KV_HEADS
    tq = min(s, ATT_TILE)
    per = tq // WINDOW
    nt = s // tq

    def body(q_ref, kvc_ref, kvp_ref, o_ref, do_ref, sink_ref, dq_ref, dkvc_ref, dkvp_ref, ds_ref, band, dband):
        i = pl.program_id(0)
        band[0:WINDOW, :] = kvp_ref[...]
        band[WINDOW:, :] = kvc_ref[...]
        dband[...] = jnp.zeros_like(dband)
        ds_ref[...] = jnp.zeros_like(ds_ref)

        def block(b, carry):
            rows = pl.ds(pl.multiple_of(b * WINDOW, WINDOW), WINDOW)
            keys = pl.ds(pl.multiple_of(b * WINDOW, WINDOW), 2 * WINDOW)
            first = (i * per + b) == 0
            dks, dvs = [], []
            for g in range(ATT_KV_HEADS):
                kb = band[keys, g * hd:(g + 1) * hd]
                vb = band[keys, half + g * hd:half + (g + 1) * hd]
                qs = _stack_heads(q_ref, rows, g * group, group)
                dos = _stack_heads(do_ref, rows, g * group, group)
                p, ps = _attn_probs_t(kb, qs, sink_ref, g * group, group, first, nq)
                prod = dos.astype(F32) * _stack_heads(o_ref, rows, g * group, group).astype(F32)
                dsum = lax.dot_general(jnp.ones((8, hd), F32), prod, NT, precision=lax.Precision.HIGHEST,
                                       preferred_element_type=F32)[0:1, :]
                dsc = p * (_dot(vb, dos, NT) - dsum) * ATT_HEAD_DIM ** -0.5
                dvs.append(_dot(p, dos))
                dks.append(_dot(dsc, qs))
                _unstack_heads(_dot(kb, dsc, TN).T, dq_ref, rows, g * group, group)
                gone = ps * dsum
                for j in range(group):
                    ds_ref[g * group + j:g * group + j + 1, :] += jnp.zeros((1, 128), F32) - jnp.sum(gone[:, j * WINDOW:(j + 1) * WINDOW])
            dband[keys, 0:half] += jnp.concatenate(dks, axis=1)
            dband[keys, half:] += jnp.concatenate(dvs, axis=1)
            return carry

        lax.fori_loop(0, per, block, 0)
        dkvp_ref[...] = dband[0:WINDOW, :]
        dkvc_ref[...] = dband[WINDOW:, :]

    big = pl.BlockSpec((tq, d), lambda i: (i, 0))
    return pl.pallas_call(
        body, name="attn_bwd", grid=(nt,),
        in_specs=_attn_specs(s, d, kvd, tq) + [big, big, pl.BlockSpec(memory_space=pltpu.SMEM)],
        out_specs=[big, pl.BlockSpec((tq, kvd), lambda i: (i, 0)), pl.BlockSpec((None, WINDOW, kvd), lambda i: (i, 0, 0)),
                   pl.BlockSpec((None, nq, 128), lambda i: (i, 0, 0))],
        out_shape=[jax.ShapeDtypeStruct((s, d), BF16), jax.ShapeDtypeStruct((s, kvd), F32), jax.ShapeDtypeStruct((nt, WINDOW, kvd), F32),
                   jax.ShapeDtypeStruct((nt, nq, 128), F32)],
        scratch_shapes=[pltpu.VMEM((tq + WINDOW, kvd), BF16), pltpu.VMEM((tq + WINDOW, kvd), F32)],
        compiler_params=_params(("parallel",)),
    )(q, kv, kv, o, do, sinks)


HBM_SPEC = pl.BlockSpec(memory_space=pltpu.HBM)
VMEM_SPEC = pl.BlockSpec(memory_space=pltpu.VMEM)


def _place():
    return lax.axis_index("x"), lax.axis_index("y"), lax.axis_index("c")


def _flip(pos, r):
    return tuple(1 - p if (r >> (2 - a)) & 1 else p for a, p in enumerate(pos))


def _index(pos):
    return 4 * pos[0] + 2 * pos[1] + pos[2]


def _all_gather(name, shards, spec):
    n = len(shards)

    def body(*refs):
        x_refs, o_refs = refs[:n], refs[n:2 * n]
        send_sems, recv_sems, local_sems = refs[2 * n:]
        me = _place()
        sibling = _flip(me, 1)
        far = [_flip(me, r) for r in (4, 2, 6)]

        def copy(t, sem, block, to, src=None):
            rows = o_refs[t].at[_index(block)]
            return pltpu.make_async_remote_copy(
                src_ref=rows if src is None else src, dst_ref=rows, send_sem=send_sems.at[t, sem], recv_sem=recv_sems.at[t, sem],
                device_id=to, device_id_type=MESH)

        own = [pltpu.make_async_copy(x_refs[t], o_refs[t].at[_index(me)], local_sems.at[t]) for t in range(n)]
        for cp in own:
            cp.start()
        first = []
        for t in range(n):
            first.append(copy(t, 0, me, sibling, src=x_refs[t]))
            first += [copy(t, 1 + j, me, peer, src=x_refs[t]) for j, peer in enumerate(far)]
        for cp in first:
            cp.start()
        passed = []
        for j, peer in enumerate(far):
            for t in range(n):
                copy(t, 1 + j, peer, me).wait_recv()
                cp = copy(t, 4 + j, peer, sibling)
                cp.start()
                passed.append(cp)
        for t in range(n):
            copy(t, 0, sibling, me).wait_recv()
            for j, peer in enumerate(far):
                copy(t, 4 + j, _flip(peer, 1), me).wait_recv()
        for cp in first + passed:
            cp.wait_send()
        for cp in own:
            cp.wait()

    return pl.pallas_call(
        body, name=name, in_specs=[spec] * n, out_specs=[spec] * n,
        out_shape=[jax.ShapeDtypeStruct((N_DEV,) + sh.shape, sh.dtype) for sh in shards],
        scratch_shapes=[pltpu.SemaphoreType.DMA((n, 7)), pltpu.SemaphoreType.DMA((n, 7)), pltpu.SemaphoreType.DMA((n,))],
    )(*shards)


SEM_SPEC = pl.BlockSpec(memory_space=pltpu.SEMAPHORE)
ANY_SPEC = pl.BlockSpec(memory_space=pl.ANY)


def _landing(own, mine):
    return lax.dynamic_update_slice(lax.empty((N_DEV,) + own.shape, own.dtype), own[None], (mine,) + (0,) * own.ndim)


def _pinned(a, token):
    return a + token[0:1, 0:1].astype(a.dtype)


def _peer_copies(src_refs, land_refs, send_sems, recv_sems, scatter, arrivals):
    me = _place()
    mine = _index(me)
    copies = []
    for t, (src, land) in enumerate(zip(src_refs, land_refs)):
        for r in range(1, N_DEV):
            peer = _flip(me, r)
            theirs = _index(peer)
            sem = t * (N_DEV - 1) + r - 1
            copies.append(pltpu.make_async_remote_copy(
                src_ref=src.at[theirs] if scatter else src, dst_ref=land.at[theirs if arrivals else mine],
                send_sem=send_sems.at[sem], recv_sem=recv_sems.at[sem], device_id=peer, device_id_type=MESH))
    return copies


def _send_start(name, sources, lands, scatter, after=None):
    n = len(sources)
    extra = 0 if after is None else 1

    def body(*refs):
        outs = refs[2 * n + extra:]
        for out in _peer_copies(refs[:n], refs[n:2 * n], outs[0], outs[1], scatter, False):
            out.start()
        outs[-1][...] = jnp.zeros_like(outs[-1])

    outs = pl.pallas_call(
        body, name=name, in_specs=[HBM_SPEC] * (2 * n) + [ANY_SPEC] * extra,
        out_specs=[SEM_SPEC, SEM_SPEC] + [HBM_SPEC] * (2 * n) + [VMEM_SPEC],
        out_shape=[pltpu.SemaphoreType.DMA((n * (N_DEV - 1),)), pltpu.SemaphoreType.DMA((n * (N_DEV - 1),))]
        + [pltpu.HBM(a.shape, a.dtype) for a in list(sources) + list(lands)] + [jax.ShapeDtypeStruct((8, 128), F32)],
        input_output_aliases={i: 2 + i for i in range(2 * n)},
        compiler_params=pltpu.CompilerParams(has_side_effects=pltpu.SideEffectType.DATAFLOW_SIDE_EFFECTING),
    )(*[pltpu.with_memory_space_constraint(a, pltpu.HBM) for a in list(sources) + list(lands)], *([] if after is None else [after]))
    return outs[0], outs[1], outs[2:2 + n], outs[2 + n:2 + 2 * n], outs[-1]


def _send_wait(name, started, after, scatter):
    send_sems, recv_sems, sources, lands, _ = started
    n = len(sources)

    def body(*refs):
        for out in _peer_copies(refs[:n], refs[n:2 * n], refs[2 * n], refs[2 * n + 1], scatter, False):
            out.wait_send()
        for arrival in _peer_copies(refs[:n], refs[n:2 * n], refs[2 * n], refs[2 * n + 1], scatter, True):
            arrival.wait_recv()

    outs = pl.pallas_call(
        body, name=name, in_specs=[HBM_SPEC] * (2 * n) + [SEM_SPEC, SEM_SPEC, ANY_SPEC], out_specs=[HBM_SPEC] * (2 * n),
        out_shape=[pltpu.HBM(a.shape, a.dtype) for a in list(sources) + list(lands)],
        input_output_aliases={i: i for i in range(2 * n)},
        compiler_params=pltpu.CompilerParams(has_side_effects=pltpu.SideEffectType.DATAFLOW_SIDE_EFFECTING),
    )(*sources, *lands, send_sems, recv_sems, after)
    return outs[n:]


def _pack_rows(parts):
    offsets, row = [], 0
    for part in parts:
        offsets.append(row)
        row += part.shape[0]
    return offsets, -(-row // 8) * 8, -(-max(part.shape[1] for part in parts) // 128) * 128


def _pack(name, parts):
    offsets, rows, width = _pack_rows(parts)

    def body(*refs):
        o_ref = refs[-1]
        o_ref[...] = jnp.zeros_like(o_ref)
        for off, ref in zip(offsets, refs[:-1]):
            o_ref[off:off + ref.shape[0], 0:ref.shape[1]] = ref[...]

    return pl.pallas_call(body, name=name, in_specs=[VMEM_SPEC] * len(parts), out_specs=VMEM_SPEC,
                          out_shape=jax.ShapeDtypeStruct((rows, width), F32))(*parts)


def _adamw_math(w, g, m, v):
    m = ADAM_B1 * m + (1.0 - ADAM_B1) * g
    v = ADAM_B2 * v + (1.0 - ADAM_B2) * (g * g)
    m_hat = m * (1.0 / (1.0 - ADAM_B1 ** ADAM_STEP))
    denom = jnp.sqrt(v * (1.0 / (1.0 - ADAM_B2 ** ADAM_STEP))) + ADAM_EPS
    inv = pl.reciprocal(denom, approx=True)
    inv = inv * (2.0 - denom * inv)
    return -ADAM_LR * (m_hat * inv + ADAM_WD * w), m, v


def _adamw_step(w_ref, m_ref, v_ref, p_ref, g_ref, d_ref, nm_ref, nv_ref):
    g = p_ref[0].astype(F32)
    for dev in range(1, N_DEV):
        g = g + p_ref[dev].astype(F32)
    g_ref[...] = g
    d_ref[...], nm_ref[...], nv_ref[...] = _adamw_math(w_ref[...], g, m_ref[...], v_ref[...])


def _adamw_rows(rows):
    return max(t for t in range(8, min(rows, 256) + 1, 8) if rows % t == 0)


def _adamw_shard(name, w, m, v, partials):
    rows, cols = w.shape
    tr = _adamw_rows(rows)
    blk = pl.BlockSpec((tr, cols), lambda i: (i, 0))
    return pl.pallas_call(
        _adamw_step_fn(), name=name, grid=(rows // tr,), in_specs=[blk, blk, blk, pl.BlockSpec((N_DEV, tr, cols), lambda i: (0, i, 0))],
        out_specs=[blk] * 4, out_shape=[jax.ShapeDtypeStruct((rows, cols), F32)] * 4, compiler_params=_params(("parallel",)),
    )(w, m, v, partials)


def _adamw_step_fn():
    return functools.partial(_adamw_step)


def _adamw_layers(name, w, m, v, partials):
    layers, rows, cols = w.shape
    tr = _adamw_rows(rows)
    last = rows // tr - 1

    def body(w_ref, m_ref, v_ref, *rest):
        for layer in range(layers):
            @pl.when(pl.program_id(0) == layer)
            def _():
                _adamw_step(w_ref, m_ref, v_ref, rest[layer], *rest[layers:])

    blk = pl.BlockSpec((None, tr, cols), lambda l, i: (l, i, 0))
    part = lambda layer: pl.BlockSpec((N_DEV, tr, cols), lambda l, i: (0, jnp.where(l == layer, i, jnp.where(l < layer, 0, last)), 0))
    return pl.pallas_call(
        body, name=name, grid=(layers, rows // tr), in_specs=[blk, blk, blk] + [part(layer) for layer in range(layers)],
        out_specs=[blk] * 4, out_shape=[jax.ShapeDtypeStruct(w.shape, F32)] * 4, compiler_params=_params(("arbitrary", "arbitrary")),
    )(w, m, v, *partials)


def _adamw_small(gathered, places, entries):
    n = len(entries)
    np_ = len(gathered)

    def body(*refs):
        pack_refs = refs[:np_]
        refs = refs[np_ - 1:]
        w_refs, m_refs, v_refs = refs[1:1 + n], refs[1 + n:1 + 2 * n], refs[1 + 2 * n:1 + 3 * n]
        outs = refs[1 + 3 * n:]
        totals = []
        for pack_ref in pack_refs:
            acc = pack_ref[0]
            for dev in range(1, N_DEV):
                acc = acc + pack_ref[dev]
            totals.append(acc)
        mine = _index(_place())
        for e in range(n):
            rows, cols = w_refs[e].shape
            total, off = totals[places[e][0]], places[e][1]
            if entries[e][3]:
                g = jnp.zeros((rows, cols), F32)
                for dev in range(N_DEV):
                    g = g + jnp.where(mine == dev, total[off + dev * rows:off + (dev + 1) * rows, 0:cols], 0.0)
            else:
                g = total[off:off + rows, 0:cols]
            outs[4 * e][...] = g
            outs[4 * e + 1][...], outs[4 * e + 2][...], outs[4 * e + 3][...] = _adamw_math(w_refs[e][...], g, m_refs[e][...], v_refs[e][...])
        outs[4 * n][...] = totals[places[n][0]][places[n][1]:places[n][1] + 1, 0:128]

    shapes = []
    for w, _, _, _ in entries:
        shapes += [jax.ShapeDtypeStruct(w.shape, F32)] * 4
    shapes.append(jax.ShapeDtypeStruct((1, 128), F32))
    return pl.pallas_call(
        body, name="adamw_small", in_specs=[VMEM_SPEC] * (np_ + 3 * n), out_specs=[VMEM_SPEC] * len(shapes), out_shape=shapes,
        compiler_params=pltpu.CompilerParams(vmem_limit_bytes=VMEM_LIMIT),
    )(*gathered, *[e[0] for e in entries], *[e[1] for e in entries], *[e[2] for e in entries])


def _ffn_forward(tag, h, gain, w_up, w_down, conv_w, conv_b, after_up=None):
    s, d = h.shape
    fb = w_up.shape[2]
    tm = _row_tile(s, MM_ROWS)
    a, = _rmsnorm_cast(f"ffn_norm_{tag}", h, [gain])
    u = _matmul(
        f"ffn_up_{tag}", a, w_up, dims=NN, grid=(s // tm, N_DEV, 1),
        a_spec=pl.BlockSpec((tm, d), lambda i, j, k: (i, 0)),
        b_spec=pl.BlockSpec((None, d, fb), lambda i, j, k: (j, 0, 0)),
        o_spec=pl.BlockSpec((None, None, tm, fb), lambda i, j, k: (j // 4, j % 4, i, 0)),
        out_shape=jax.ShapeDtypeStruct((2, 4, s, fb), BF16))
    if after_up is not None:
        conv_b = _pinned(conv_b, after_up(u))
    hidden, out = _ffn_hidden_down(f"ffn_hidden_down_{tag}", u, conv_w, conv_b, w_down, h)
    return out, (a, u, hidden)


def _ffn_backward(tag, h, gain, w_up, w_down, conv_w, conv_b, saved, dout):
    a, u, hidden = saved
    dout, dout_bf = dout
    s, d = h.shape
    fb = w_up.shape[2]
    tm = _row_tile(s, MM_ROWS)
    dhidden = _matmul(
        f"ffn_down_bwd_{tag}", dout_bf, w_down, dims=NT, grid=(s // tm, 4, 1),
        a_spec=pl.BlockSpec((tm, d), lambda i, j, k: (i, 0)),
        b_spec=pl.BlockSpec((None, fb, d), lambda i, j, k: (j, 0, 0)),
        o_spec=pl.BlockSpec((None, tm, fb), lambda i, j, k: (j, i, 0)),
        out_shape=jax.ShapeDtypeStruct((4, s, fb), BF16))
    dw_down = _matmul(
        f"ffn_down_grad_{tag}", hidden, dout_bf, dims=TN, grid=(4, 1, 1),
        a_spec=pl.BlockSpec((None, s, fb), lambda i, j, k: (i, 0, 0)),
        b_spec=pl.BlockSpec((s, d), lambda i, j, k: (0, 0)),
        o_spec=pl.BlockSpec((None, fb, d), lambda i, j, k: (i, 0, 0)),
        out_shape=jax.ShapeDtypeStruct((4, fb, d), BF16))
    du, dconv_w, dconv_b, da = _ffn_hidden_up_bwd(f"ffn_hidden_up_bwd_{tag}", u, dhidden, conv_w, conv_b, w_up)
    dw_up = _matmul(
        f"ffn_up_grad_{tag}", a, du, dims=TN, grid=(1, N_DEV, 1),
        a_spec=pl.BlockSpec((s, d), lambda i, j, k: (0, 0)),
        b_spec=pl.BlockSpec((None, None, s, fb), lambda i, j, k: (j // 4, j % 4, 0, 0)),
        o_spec=pl.BlockSpec((None, d, fb), lambda i, j, k: (j, 0, 0)),
        out_shape=jax.ShapeDtypeStruct((N_DEV, d, fb), BF16))
    dh, (dgain,) = _rmsnorm_bwd(f"ffn_norm_bwd_{tag}", h, dout, [(da, gain)])
    return dh, dgain, dw_up, dw_down, dconv_w, dconv_b


def kernel(x, hg_norm, hg_w_in, hg_lb_logits, hg_out_norm, hg_w_out, kv_norm, w_kv, attn_norm, attn_w_q, attn_sinks, attn_w_o, ffn_norm, ffn_w_up, ffn_conv_w, ffn_conv_b, ffn_w_down, final_norm, loss_target, m_hg_norm, m_hg_w_in, m_hg_lb_logits, m_hg_out_norm, m_hg_w_out, m_kv_norm, m_w_kv, m_attn_norm, m_attn_w_q, m_attn_sinks, m_attn_w_o, m_ffn_norm, m_ffn_w_up, m_ffn_conv_w, m_ffn_conv_b, m_ffn_w_down, m_final_norm, v_hg_norm, v_hg_w_in, v_hg_lb_logits, v_hg_out_norm, v_hg_w_out, v_kv_norm, v_w_kv, v_attn_norm, v_attn_w_q, v_attn_sinks, v_attn_w_o, v_ffn_norm, v_ffn_w_up, v_ffn_conv_w, v_ffn_conv_b, v_ffn_w_down, v_final_norm):
    _, s, d = x.shape
    x0, target = x[0], loss_target[0]
    half = hg_w_in.shape[2]
    fs = ffn_conv_w.shape[2]
    fb = 2 * fs
    kvd = w_kv.shape[1]
    nq = d // ATT_HEAD_DIM
    tm = _row_tile(s, MM_ROWS)

    mine = _index(_place())
    gather = lambda tag, shards, after: _send_start("gather_start_" + tag, shards, [_landing(a, mine) for a in shards], False, after)
    w_in, g_hgn, g_lbl, w_out = _all_gather("gather_hg", [hg_w_in[0].astype(BF16), hg_norm, hg_lb_logits, hg_w_out[0].astype(BF16)], HBM_SPEC)
    w_out = w_out.reshape(d, d)
    coming_ffn0 = gather("ffn0", [ffn_w_up[0].astype(BF16), ffn_conv_w, ffn_w_down[0].astype(BF16)], g_hgn)
    hgn = _pinned(g_hgn.reshape(1, d), coming_ffn0[4])
    lbl = g_lbl.transpose(1, 0, 2).reshape(2, d)
    conv_b = [ffn_conv_b[layer].reshape(4, 1, fb) for layer in range(2)]
    gains = [ffn_norm[0:1], ffn_norm[1:2]]
    kvn, fin = kv_norm.reshape(1, d), final_norm.reshape(1, d)

    a0, = _rmsnorm_cast("hg_norm", x0, [hgn])
    p = _matmul(
        "hg_in", a0, w_in, dims=NN, grid=(s // tm, N_DEV, 1),
        a_spec=pl.BlockSpec((tm, d), lambda i, j, k: (i, 0)),
        b_spec=pl.BlockSpec((None, d, half), lambda i, j, k: (j, 0, 0)),
        o_spec=pl.BlockSpec((None, tm, half), lambda i, j, k: (j // 2, i, j % 2)),
        out_shape=jax.ShapeDtypeStruct((4, s, d), BF16), acc_shape=(8, 128))
    o, og, states = _hgrn2_fwd(p, lbl, hg_out_norm)
    x1 = _mm_rows("hg_out", og, w_out, out_dtype=F32, add=x0)
    w_up0, g_cw, w_dn0 = _send_wait("gather_wait_ffn0", coming_ffn0, x1, False)
    w_up, w_dn = [w_up0, None], [w_dn0.reshape(4, fb, d), None]
    conv_w = [g_cw[:, layer].reshape(4, 2, CONV_WIDTH, fs).transpose(0, 2, 1, 3).reshape(4, CONV_WIDTH, fb) for layer in range(2)]
    coming_attn = gather("attn", [w_kv.astype(BF16), attn_w_q[0].astype(BF16), attn_w_o[0].astype(BF16)], w_dn0)
    gains[0] = _pinned(gains[0], coming_attn[4])
    coming = []

    def start_ffn1(u):
        coming.append(gather("ffn1", [ffn_w_up[1].astype(BF16), ffn_w_down[1].astype(BF16)], u))
        return coming[0][4]

    x2, saved0 = _ffn_forward("0", x1, gains[0], w_up[0], w_dn[0], conv_w[0], conv_b[0], start_ffn1)
    coming_ffn1 = coming[0]
    w_kvg, w_q, w_o = _send_wait("gather_wait_attn", coming_attn, x2, False)
    w_kvg, w_q, w_o = w_kvg.reshape(d, kvd), w_q.reshape(d, d), w_o.reshape(d, d)
    akv, a2 = _rmsnorm_cast("attn_norms", x2, [kvn, attn_norm])
    kv = _mm_rows("kv_proj", akv, w_kvg, out_dtype=BF16)
    q = _mm_rows("q_proj", a2, w_q, out_dtype=BF16)
    att = _attn_fwd(q, kv, attn_sinks)
    x3 = _mm_rows("attn_out", att, w_o, out_dtype=F32, add=x2)
    w_up[1], w_dn1 = _send_wait("gather_wait_ffn1", coming_ffn1, x3, False)
    w_dn[1] = w_dn1.reshape(4, fb, d)
    x4, saved1 = _ffn_forward("1", x3, gains[1], w_up[1], w_dn[1], conv_w[1], conv_b[1])
    dx4, dx4_bf, d_fin, loss_part = _loss_head(x4, fin, target)

    dx3, d_fn1, dw_up1, dw_dn1, dcw1, dcb1 = _ffn_backward("1", x3, gains[1], w_up[1], w_dn[1], conv_w[1], conv_b[1], saved1, (dx4, dx4_bf))
    rows = d // N_DEV
    scatter = lambda tag, stacks: _send_start("scatter_start_" + tag, stacks, [_landing(lax.dynamic_index_in_dim(a, mine, keepdims=False), mine) for a in stacks], True)
    going_ffn1 = scatter("ffn1", [dw_up1, dw_dn1.reshape(N_DEV, fs, d)])
    datt = _mm_rows_nt("attn_out_bwd", dx3[1], w_o, out_dtype=BF16)
    dw_o = _mm_tn("attn_out_grad", att, dx3[1])
    dq, dkv_own, dkv_before, dsink = _attn_bwd(q, kv, att, datt, _pinned(attn_sinks, going_ffn1[4]))
    tiles = dkv_before.shape[0]
    dkv = dkv_own.reshape(tiles, s // tiles, kvd)
    dkv = jnp.concatenate([dkv[:, :-WINDOW], dkv[:, -WINDOW:] + jnp.pad(dkv_before[1:], ((0, 1), (0, 0), (0, 0)))], axis=1).reshape(s, kvd)
    da2 = _mm_rows_nt("q_proj_bwd", dq, w_q, out_dtype=F32)
    dw_q = _mm_tn("q_proj_grad", a2, dq)
    dakv = _mm_rows_nt("kv_proj_bwd", dkv, w_kvg, out_dtype=F32)
    dw_kv = _mm_tn("kv_proj_grad", akv, dkv)
    going_attn = scatter("attn", [dw_kv.reshape(N_DEV, rows, kvd), dw_q.reshape(N_DEV, rows, d), dw_o.reshape(N_DEV, rows, d)])
    dx2, (d_kvn, d_attn) = _rmsnorm_bwd("attn_norms_bwd", x2, dx3[0], [(dakv, _pinned(kvn, going_attn[4])), (da2, attn_norm)])
    dx1, d_fn0, dw_up0, dw_dn0, dcw0, dcb0 = _ffn_backward("0", x1, gains[0], w_up[0], w_dn[0], conv_w[0], conv_b[0], saved0, dx2)
    going_ffn0 = scatter("ffn0", [dw_up0, dw_dn0.reshape(N_DEV, fs, d)])
    dog = _mm_rows_nt("hg_out_bwd", dx1[1], w_out, out_dtype=F32)
    dw_out = _mm_tn("hg_out_grad", og, dx1[1])
    dp, d_lbl, d_ogain = _hgrn2_bwd(p, lbl, _pinned(hg_out_norm, going_ffn0[4]), o, dog, states)
    dw_in = _matmul(
        "hg_in_grad", a0, dp, dims=TN, grid=(1, N_DEV, 1),
        a_spec=pl.BlockSpec((s, d), lambda i, j, k: (0, 0)),
        b_spec=pl.BlockSpec((None, s, half), lambda i, j, k: (j // 2, 0, j % 2)),
        o_spec=pl.BlockSpec((None, d, half), lambda i, j, k: (j, 0, 0)),
        out_shape=jax.ShapeDtypeStruct((N_DEV, d, half), BF16))
    going_hg = scatter("hg", [dw_in, dw_out.reshape(N_DEV, rows, d)])
    th = _row_tile(s, MM_ROWS // 2)
    da0 = _matmul(
        "hg_in_bwd", dp, w_in, dims=NT, grid=(s // th, 1, 1),
        a_spec=pl.BlockSpec((4, th, d), lambda i, j, k: (0, i, 0)),
        b_spec=pl.BlockSpec((N_DEV, d, half), lambda i, j, k: (0, 0, 0)),
        o_spec=pl.BlockSpec((th, d), lambda i, j, k: (i, 0)),
        out_shape=jax.ShapeDtypeStruct((s, d), F32),
        terms=lambda a_ref, b_ref: [(a_ref[k // 2, :, (k % 2) * half:(k % 2 + 1) * half], b_ref[k]) for k in range(N_DEV)])
    (dx0, _), (d_hgn,) = _rmsnorm_bwd("hg_norm_bwd", x0, dx1[0], [(da0, _pinned(hgn, going_hg[4]))])

    arrive = lambda tag, going, after: _send_wait("scatter_wait_" + tag, going, after, True)
    (l_up1, l_dn1), (l_kv, l_q, l_o), (l_up0, l_dn0) = arrive("ffn1", going_ffn1, dx0), arrive("attn", going_attn, dx0), arrive("ffn0", going_ffn0, dx0)
    big = {}
    for tag, w, m, v, part in [
            ("w_kv", w_kv, m_w_kv, v_w_kv, l_kv), ("attn_w_q", attn_w_q[0], m_attn_w_q[0], v_attn_w_q[0], l_q),
            ("attn_w_o", attn_w_o[0], m_attn_w_o[0], v_attn_w_o[0], l_o)]:
        big[tag] = _adamw_shard("adamw_" + tag, w, m, v, part)
    big["ffn_w_up"] = _adamw_layers("adamw_ffn_w_up", ffn_w_up, m_ffn_w_up, v_ffn_w_up, (l_up0, l_up1))
    big["ffn_w_down"] = _adamw_layers("adamw_ffn_w_down", ffn_w_down, m_ffn_w_down, v_ffn_w_down, (l_dn0, l_dn1))
    lead = lambda tag: [a[None] for a in big[tag]]

    as_blocks = lambda a, r: a.reshape(r, N_DEV, -1).transpose(1, 0, 2).reshape(N_DEV * r, -1)
    d_cw = jnp.concatenate([g.transpose(1, 0, 2).reshape(CONV_WIDTH, 4 * fb) for g in (dcw0, dcw1)], axis=0)
    parts = [d_fin, jnp.concatenate([d_fn0, d_fn1], axis=0), jnp.concatenate([dcb0.reshape(1, 4 * fb), dcb1.reshape(1, 4 * fb)], axis=0),
             as_blocks(d_cw, 2 * CONV_WIDTH), d_attn, jnp.sum(dsink[:, :, 0], axis=0).reshape(1, nq), d_kvn, d_ogain,
             as_blocks(d_hgn, 1), as_blocks(d_lbl, 2), loss_part]
    wide = [2]
    packs = [[parts[i] for i in wide], [part for i, part in enumerate(parts) if i not in wide]]
    places = [None] * len(parts)
    for which, members in enumerate([wide, [i for i in range(len(parts)) if i not in wide]]):
        for i, off in zip(members, _pack_rows(packs[which])[0]):
            places[i] = (which, off)
    gathered = _all_gather("gather_small_grads", [_pack("pack_wide_grads", packs[0]), _pack("pack_narrow_grads", packs[1])], VMEM_SPEC)
    two = lambda a: a.reshape(-1, a.shape[-1])
    small = [(fin, m_final_norm.reshape(1, d), v_final_norm.reshape(1, d), False), (ffn_norm, m_ffn_norm, v_ffn_norm, False),
             (ffn_conv_b, m_ffn_conv_b, v_ffn_conv_b, False), (two(ffn_conv_w), two(m_ffn_conv_w), two(v_ffn_conv_w), True),
             (attn_norm, m_attn_norm, v_attn_norm, False), (attn_sinks, m_attn_sinks, v_attn_sinks, False),
             (kvn, m_kv_norm.reshape(1, d), v_kv_norm.reshape(1, d), False), (hg_out_norm, m_hg_out_norm, v_hg_out_norm, False),
             (hg_norm, m_hg_norm, v_hg_norm, True), (hg_lb_logits, m_hg_lb_logits, v_hg_lb_logits, True)]
    res = _adamw_small(gathered, places, small)
    l_in, l_out = arrive("hg", going_hg, gathered[1])
    big["hg_w_in"] = _adamw_shard("adamw_hg_w_in", hg_w_in[0], m_hg_w_in[0], v_hg_w_in[0], l_in)
    big["hg_w_out"] = _adamw_shard("adamw_hg_w_out", hg_w_out[0], m_hg_w_out[0], v_hg_w_out[0], l_out)
    names = ["final_norm", "ffn_norm", "ffn_conv_b", "ffn_conv_w", "attn_norm", "attn_sinks", "kv_norm", "hg_out_norm", "hg_norm", "hg_lb_logits"]
    shapes = {"final_norm": final_norm.shape, "kv_norm": kv_norm.shape, "ffn_conv_w": ffn_conv_w.shape}
    out = {n: [a.reshape(shapes[n]) if n in shapes else a for a in res[4 * i:4 * i + 4]] for i, n in enumerate(names)}
    out.update(hg_w_in=lead("hg_w_in"), hg_w_out=lead("hg_w_out"), w_kv=big["w_kv"], attn_w_q=lead("attn_w_q"), attn_w_o=lead("attn_w_o"),
               ffn_w_up=big["ffn_w_up"], ffn_w_down=big["ffn_w_down"])
    order = ["hg_norm", "hg_w_in", "hg_lb_logits", "hg_out_norm", "hg_w_out", "kv_norm", "w_kv", "attn_norm", "attn_w_q", "attn_sinks",
             "attn_w_o", "ffn_norm", "ffn_w_up", "ffn_conv_w", "ffn_conv_b", "ffn_w_down", "final_norm"]
    loss = res[-1][0, 0]
    return (loss, dx0[None], *[out[n][0] for n in order], *[out[n][1] for n in order], *[out[n][2] for n in order], *[out[n][3] for n in order])
```

```python
import functools
import math

import jax
import jax.numpy as jnp
from jax import lax
from jax.experimental import pallas as pl
from jax.experimental.pallas import tpu as pltpu

F32 = jnp.float32
BF16 = jnp.bfloat16

EPS = 1e-6
HG_EXPAND = 128
HG_CHUNK = 32
ATT_HEAD_DIM = 64
ATT_KV_HEADS = 2
WINDOW = 128
CONV_WIDTH = 3
ADAM_LR = 0.001
ADAM_B1 = 0.9
ADAM_B2 = 0.999
ADAM_EPS = 1e-08
ADAM_WD = 0.01
ADAM_STEP = 10

N_DEV = 8
VMEM_LIMIT = 48 * 1024 * 1024
NEG = -1e30

NN = (((1,), (0,)), ((), ()))
NT = (((1,), (1,)), ((), ()))
TN = (((0,), (0,)), ((), ()))
MESH = pl.DeviceIdType.MESH


def _dot(a, b, dims=NN):
    return lax.dot_general(a.astype(BF16), b.astype(BF16), dims, preferred_element_type=F32)


def _sigmoid(x):
    return 0.5 * jnp.tanh(0.5 * x) + 0.5


def _silu(x):
    return x * _sigmoid(x)


def _silu_and_grad(x):
    s = _sigmoid(x)
    return x * s, s * (1.0 + x * (1.0 - s))


def _dsilu(x):
    return _silu_and_grad(x)[1]


def _params(semantics):
    return pltpu.CompilerParams(dimension_semantics=semantics, vmem_limit_bytes=VMEM_LIMIT)


def _row_tile(rows, want=512):
    return min(rows, want)


MM_ROWS = 1024


def _matmul(name, a, b, *, dims, grid, a_spec, b_spec, o_spec, out_shape, acc_shape=(8, 128), add=None, add_spec=None, terms=None):
    nk = grid[2]

    def body(*refs):
        if add is None:
            a_ref, b_ref, o_ref, acc = refs
        else:
            a_ref, b_ref, add_ref, o_ref, acc = refs
        k = pl.program_id(2)
        pairs = [(a_ref[...], b_ref[...])] if terms is None else terms(a_ref, b_ref)
        part = _dot(*pairs[0], dims)
        for pair in pairs[1:]:
            part = part + _dot(*pair, dims)

        def finish(total):
            if add is not None:
                total = total + add_ref[...]
            o_ref[...] = total.astype(o_ref.dtype)

        if nk == 1:
            finish(part)
        else:
            @pl.when(k == 0)
            def _():
                acc[...] = part

            @pl.when(k > 0)
            def _():
                acc[...] += part

            @pl.when(k == nk - 1)
            def _():
                finish(acc[...])

    in_specs = [a_spec, b_spec] + ([] if add is None else [add_spec])
    args = (a, b) + (() if add is None else (add,))
    return pl.pallas_call(
        body, name=name, grid=grid, in_specs=in_specs, out_specs=o_spec, out_shape=out_shape,
        scratch_shapes=[pltpu.VMEM(acc_shape, F32)],
        compiler_params=_params(("parallel", "parallel", "arbitrary")),
    )(*args)


def _mm_rows(name, a, w, *, out_dtype, add=None):
    s, kdim = a.shape
    n = w.shape[1]
    tm = _row_tile(s, MM_ROWS)
    return _matmul(
        name, a, w, dims=NN, grid=(s // tm, 1, 1),
        a_spec=pl.BlockSpec((tm, kdim), lambda i, j, k: (i, 0)),
        b_spec=pl.BlockSpec((kdim, n), lambda i, j, k: (0, 0)),
        o_spec=pl.BlockSpec((tm, n), lambda i, j, k: (i, 0)),
        out_shape=jax.ShapeDtypeStruct((s, n), out_dtype), acc_shape=(8, 128),
        add=add, add_spec=None if add is None else pl.BlockSpec((tm, n), lambda i, j, k: (i, 0)),
    )


def _mm_rows_nt(name, a, w, *, out_dtype):
    s, n = a.shape
    kdim = w.shape[0]
    tm = _row_tile(s, MM_ROWS)
    return _matmul(
        name, a, w, dims=NT, grid=(s // tm, 1, 1),
        a_spec=pl.BlockSpec((tm, n), lambda i, j, k: (i, 0)),
        b_spec=pl.BlockSpec((kdim, n), lambda i, j, k: (0, 0)),
        o_spec=pl.BlockSpec((tm, kdim), lambda i, j, k: (i, 0)),
        out_shape=jax.ShapeDtypeStruct((s, kdim), out_dtype), acc_shape=(8, 128),
    )


def _mm_tn(name, a, g):
    s, m = a.shape
    n = g.shape[1]
    tn = min(n, 512)
    return _matmul(
        name, a, g, dims=TN, grid=(1, n // tn, 1),
        a_spec=pl.BlockSpec((s, m), lambda i, j, k: (0, 0)),
        b_spec=pl.BlockSpec((s, tn), lambda i, j, k: (0, j)),
        o_spec=pl.BlockSpec((m, tn), lambda i, j, k: (0, j)),
        out_shape=jax.ShapeDtypeStruct((m, n), BF16),
    )


def _rmsnorm_cast(name, h, gains):
    s, d = h.shape
    tm = _row_tile(s)
    n = len(gains)

    def body(*refs):
        h_ref, g_refs, o_refs = refs[0], refs[1:1 + n], refs[1 + n:]
        xv = h_ref[...]
        xhat = xv * lax.rsqrt(jnp.mean(xv * xv, axis=-1, keepdims=True) + EPS)
        for g_ref, o_ref in zip(g_refs, o_refs):
            o_ref[...] = (xhat * g_ref[...]).astype(BF16)

    row = pl.BlockSpec((tm, d), lambda i: (i, 0))
    vec = pl.BlockSpec((1, d), lambda i: (0, 0))
    return pl.pallas_call(
        body, name=name, grid=(s // tm,), in_specs=[row] + [vec] * n, out_specs=[row] * n,
        out_shape=[jax.ShapeDtypeStruct((s, d), BF16)] * n, compiler_params=_params(("parallel",)),
    )(h, *gains)


def _rmsnorm_bwd(name, h, dres, branches):
    s, d = h.shape
    tm = _row_tile(s)
    n = len(branches)

    def body(*refs):
        h_ref, dres_ref = refs[0], refs[1]
        da_refs, g_refs = refs[2:2 + n], refs[2 + n:2 + 2 * n]
        dh_ref, dhb_ref, dg_refs = refs[2 + 2 * n], refs[3 + 2 * n], refs[4 + 2 * n:]
        i = pl.program_id(0)
        xv = h_ref[...]
        r = lax.rsqrt(jnp.mean(xv * xv, axis=-1, keepdims=True) + EPS)
        xhat = xv * r
        total = dres_ref[...]
        for da_ref, g_ref, dg_ref in zip(da_refs, g_refs, dg_refs):
            da = da_ref[...]
            dgain = jnp.sum(da * xhat, axis=0, keepdims=True)

            @pl.when(i == 0)
            def _():
                dg_ref[...] = dgain

            @pl.when(i > 0)
            def _():
                dg_ref[...] += dgain

            dxhat = da * g_ref[...]
            total = total + r * (dxhat - xhat * jnp.mean(dxhat * xhat, axis=-1, keepdims=True))
        dh_ref[...] = total
        dhb_ref[...] = total.astype(BF16)

    row = pl.BlockSpec((tm, d), lambda i: (i, 0))
    vec = pl.BlockSpec((1, d), lambda i: (0, 0))
    outs = pl.pallas_call(
        body, name=name, grid=(s // tm,), in_specs=[row, row] + [row] * n + [vec] * n, out_specs=[row, row] + [vec] * n,
        out_shape=[jax.ShapeDtypeStruct((s, d), F32), jax.ShapeDtypeStruct((s, d), BF16)] + [jax.ShapeDtypeStruct((1, d), F32)] * n,
        compiler_params=_params(("arbitrary",)),
    )(h, dres, *[b[0] for b in branches], *[b[1] for b in branches])
    return (outs[0], outs[1]), outs[2:]


def _loss_head(h, gain, target):
    s, d = h.shape
    tm = _row_tile(s)

    def body(h_ref, g_ref, t_ref, dh_ref, dhb_ref, dg_ref, loss_ref):
        i = pl.program_id(0)
        xv = h_ref[...]
        r = lax.rsqrt(jnp.mean(xv * xv, axis=-1, keepdims=True) + EPS)
        xhat = xv * r
        err = xhat * g_ref[...] - t_ref[...]
        dy = err * (1.0 / d)
        part = jnp.zeros((1, 128), F32) + 0.5 * jnp.sum(jnp.mean(err * err, axis=-1, keepdims=True))
        dgain = jnp.sum(dy * xhat, axis=0, keepdims=True)

        @pl.when(i == 0)
        def _():
            dg_ref[...] = dgain
            loss_ref[...] = part

        @pl.when(i > 0)
        def _():
            dg_ref[...] += dgain
            loss_ref[...] += part

        dxhat = dy * g_ref[...]
        dh = r * (dxhat - xhat * jnp.mean(dxhat * xhat, axis=-1, keepdims=True))
        dh_ref[...] = dh
        dhb_ref[...] = dh.astype(BF16)

    row = pl.BlockSpec((tm, d), lambda i: (i, 0))
    vec = pl.BlockSpec((1, d), lambda i: (0, 0))
    return pl.pallas_call(
        body, name="loss_head", grid=(s // tm,), in_specs=[row, vec, row],
        out_specs=[row, row, vec, pl.BlockSpec((1, 128), lambda i: (0, 0))],
        out_shape=[jax.ShapeDtypeStruct((s, d), F32), jax.ShapeDtypeStruct((s, d), BF16), jax.ShapeDtypeStruct((1, d), F32),
                   jax.ShapeDtypeStruct((1, 128), F32)],
        compiler_params=_params(("arbitrary",)),
    )(h, gain, target)


def _bdot(a, b, ca, cb):
    return lax.dot_general(a.astype(BF16), b.astype(BF16), (((ca,), (cb,)), ((0,), (0,))), preferred_element_type=F32)


def _chunk_cumsum(xv, reverse=False):
    n = xv.shape[0]
    row = lax.broadcasted_iota(jnp.int32, xv.shape, 0) % HG_CHUNK
    step = 1
    while step < HG_CHUNK:
        if reverse:
            xv = xv + jnp.where(row < HG_CHUNK - step, pltpu.roll(xv, n - step, axis=0), 0.0)
        else:
            xv = xv + jnp.where(row >= step, pltpu.roll(xv, step, axis=0), 0.0)
        step *= 2
    return xv


def _hg_terms(p_ref, lbl_ref):
    pq = p_ref[0].astype(F32)
    pf = p_ref[1].astype(F32)
    lb = _sigmoid(lbl_ref[0:1, :] - lbl_ref[1:2, :])
    sig = _sigmoid(pf)
    fg = lb + (1.0 - lb) * sig
    nc = pq.shape[0] // HG_CHUNK
    chunks = lambda a: a.reshape(nc, HG_CHUNK, HG_EXPAND)
    q = chunks(_silu(pq) * HG_EXPAND ** -0.5)
    k = chunks(1.0 - fg)
    v = chunks(p_ref[2].astype(F32))
    g = chunks(_chunk_cumsum(jnp.log(fg)))
    gm = g[:, HG_CHUNK // 2 - 1:HG_CHUNK // 2, :]
    gl = g[:, HG_CHUNK - 1:HG_CHUNK, :]
    e_mid, e_inv, e_all, e_end = jnp.exp(g - gm), jnp.exp(gm - g), jnp.exp(g), jnp.exp(gl - g)
    terms = dict(q=q, k=k, v=v, qd=q * e_all, qt=q * e_mid, kt=k * e_inv, kd=k * e_end, e_last=jnp.exp(gl),
                 e_mid=e_mid, e_inv=e_inv, e_all=e_all, e_end=e_end)
    return terms, (pq, sig, fg, lb)


def _causal(nc):
    r = lax.broadcasted_iota(jnp.int32, (nc, HG_CHUNK, HG_CHUNK), 1)
    c = lax.broadcasted_iota(jnp.int32, (nc, HG_CHUNK, HG_CHUNK), 2)
    return r >= c


def _hgrn2_fwd(p, lb_logits, out_gain):
    _, s, d = p.shape
    heads = d // HG_EXPAND
    t = _row_tile(s)
    nc = t // HG_CHUNK

    def body(p_ref, lbl_ref, gain_ref, o_ref, og_ref, st_ref, state, decay):
        @pl.when(pl.program_id(1) == 0)
        def _():
            state[...] = jnp.zeros_like(state)

        tm, _ = _hg_terms(p_ref, lbl_ref)
        decay[...] = tm["e_last"]
        st_ref[...] = _bdot(tm["v"], tm["kd"], 1, 1)

        def chunk(c, carry):
            add = st_ref[c]
            st = state[...]
            st_ref[c] = st
            state[...] = st * decay[c] + add
            return carry

        lax.fori_loop(0, nc, chunk, 0)
        a = jnp.where(_causal(nc), _bdot(tm["qt"], tm["kt"], 2, 2), 0.0)
        ov = (_bdot(tm["qd"], st_ref[...], 2, 2) + _bdot(a, tm["v"], 2, 1)).reshape(t, HG_EXPAND)
        o_ref[...] = ov
        on = ov * lax.rsqrt(jnp.mean(ov * ov, axis=-1, keepdims=True) + EPS) * gain_ref[...]
        og_ref[...] = (on * _silu(p_ref[3].astype(F32))).astype(BF16)

    blk = pl.BlockSpec((t, HG_EXPAND), lambda h, b: (b, h))
    return pl.pallas_call(
        body, name="hgrn2_fwd", grid=(heads, s // t),
        in_specs=[pl.BlockSpec((4, t, HG_EXPAND), lambda h, b: (0, b, h)), pl.BlockSpec((2, HG_EXPAND), lambda h, b: (0, h)),
                  pl.BlockSpec((1, HG_EXPAND), lambda h, b: (0, 0))],
        out_specs=[blk, blk, pl.BlockSpec((None, nc, HG_EXPAND, HG_EXPAND), lambda h, b: (h, b, 0, 0))],
        out_shape=[jax.ShapeDtypeStruct((s, d), F32), jax.ShapeDtypeStruct((s, d), BF16),
                   jax.ShapeDtypeStruct((heads, s // HG_CHUNK, HG_EXPAND, HG_EXPAND), F32)],
        scratch_shapes=[pltpu.VMEM((HG_EXPAND, HG_EXPAND), F32), pltpu.VMEM((nc, 1, HG_EXPAND), F32)],
        compiler_params=_params(("parallel", "arbitrary")),
    )(p, lb_logits, out_gain)


def _hgrn2_bwd(p, lb_logits, out_gain, o, dog, states):
    _, s, d = p.shape
    heads = d // HG_EXPAND
    t = _row_tile(s)
    nc = t // HG_CHUNK
    nb = s // t

    def body(p_ref, lbl_ref, gain_ref, o_ref, dog_ref, st_ref, dp_ref, dlbl_ref, dgain_ref, dstate, decay, dst_s):
        h, b = pl.program_id(0), pl.program_id(1)

        @pl.when(b == 0)
        def _():
            dstate[...] = jnp.zeros_like(dstate)

        tm, (pq, sig, fg, lb) = _hg_terms(p_ref, lbl_ref)
        pg = p_ref[3].astype(F32)
        ov = o_ref[...]
        r = lax.rsqrt(jnp.mean(ov * ov, axis=-1, keepdims=True) + EPS)
        ohat = ov * r
        dogv = dog_ref[...]
        d_on = dogv * _silu(pg)
        dp_ref[3] = (dogv * ohat * gain_ref[...] * _dsilu(pg)).astype(BF16)
        dgain = jnp.sum(d_on * ohat, axis=0, keepdims=True)

        @pl.when((h == 0) & (b == 0))
        def _():
            dgain_ref[...] = dgain

        @pl.when((h > 0) | (b > 0))
        def _():
            dgain_ref[...] += dgain

        dohat = d_on * gain_ref[...]
        do = (r * (dohat - ohat * jnp.mean(dohat * ohat, axis=-1, keepdims=True))).reshape(nc, HG_CHUNK, HG_EXPAND)

        decay[...] = tm["e_last"]
        dst_s[...] = _bdot(do, tm["qd"], 1, 1)

        def chunk(i, carry):
            c = nc - 1 - i
            add = dst_s[c]
            dst = dstate[...]
            dst_s[c] = dst
            dstate[...] = dst * decay[c] + add
            return carry

        lax.fori_loop(0, nc, chunk, 0)
        st, dst = st_ref[...], dst_s[...]
        causal = _causal(nc)
        a = jnp.where(causal, _bdot(tm["qt"], tm["kt"], 2, 2), 0.0)
        da = jnp.where(causal, _bdot(do, tm["v"], 2, 2), 0.0)
        dqt = _bdot(da, tm["kt"], 2, 1)
        dkt = _bdot(da, tm["qt"], 1, 1)
        dqd = _bdot(do, st, 2, 1)
        dkd = _bdot(tm["v"], dst, 2, 1)
        dv = _bdot(a, do, 1, 1) + _bdot(tm["kd"], dst, 2, 2)
        dq = dqt * tm["e_mid"] + dqd * tm["e_all"]
        dk = dkt * tm["e_inv"] + dkd * tm["e_end"]
        dg = dqt * tm["qt"] - dkt * tm["kt"] + dqd * tm["qd"] - dkd * tm["kd"]
        dgl = jnp.sum(dkd * tm["kd"], axis=1, keepdims=True) + tm["e_last"] * jnp.sum(dst * st, axis=1, keepdims=True)
        last_row = lax.broadcasted_iota(jnp.int32, (nc, HG_CHUNK, HG_EXPAND), 1) == HG_CHUNK - 1
        flat = lambda a3: a3.reshape(t, HG_EXPAND)
        dlf = _chunk_cumsum(flat(dg + jnp.where(last_row, dgl, 0.0)), reverse=True)
        dfg = dlf / fg - flat(dk)
        dlb = jnp.sum(dfg * (1.0 - sig), axis=0, keepdims=True)
        dl0 = dlb * lb * (1.0 - lb)
        dlbl = jnp.concatenate([dl0, -dl0], axis=0)

        @pl.when(b == 0)
        def _():
            dlbl_ref[...] = dlbl

        @pl.when(b > 0)
        def _():
            dlbl_ref[...] += dlbl

        dp_ref[0] = (flat(dq) * HG_EXPAND ** -0.5 * _dsilu(pq)).astype(BF16)
        dp_ref[1] = (dfg * (1.0 - lb) * sig * (1.0 - sig)).astype(BF16)
        dp_ref[2] = flat(dv).astype(BF16)

    blk = pl.BlockSpec((t, HG_EXPAND), lambda h, b: (nb - 1 - b, h))
    pblk = pl.BlockSpec((4, t, HG_EXPAND), lambda h, b: (0, nb - 1 - b, h))
    return pl.pallas_call(
        body, name="hgrn2_bwd", grid=(heads, nb),
        in_specs=[pblk, pl.BlockSpec((2, HG_EXPAND), lambda h, b: (0, h)), pl.BlockSpec((1, HG_EXPAND), lambda h, b: (0, 0)),
                  blk, blk, pl.BlockSpec((None, nc, HG_EXPAND, HG_EXPAND), lambda h, b: (h, nb - 1 - b, 0, 0))],
        out_specs=[pblk, pl.BlockSpec((2, HG_EXPAND), lambda h, b: (0, h)), pl.BlockSpec((1, HG_EXPAND), lambda h, b: (0, 0))],
        out_shape=[jax.ShapeDtypeStruct((4, s, d), BF16), jax.ShapeDtypeStruct((2, d), F32), jax.ShapeDtypeStruct((1, HG_EXPAND), F32)],
        scratch_shapes=[pltpu.VMEM((HG_EXPAND, HG_EXPAND), F32), pltpu.VMEM((nc, 1, HG_EXPAND), F32),
                        pltpu.VMEM((nc, HG_EXPAND, HG_EXPAND), F32)],
        compiler_params=_params(("arbitrary", "arbitrary")),
    )(p, lb_logits, out_gain, o, dog, states)


HALO = 8
FFN_FWD_ROWS = 512
FFN_BWD_ROWS = 256


def _shift_down(xv, n):
    return pltpu.roll(xv, n, axis=0)


def _shift_up(xv, n):
    return pltpu.roll(xv, xv.shape[0] - n, axis=0)


def _ffn_hidden_down(name, u, conv_w, conv_b, w_down, h):
    _, nj, s, fb = u.shape
    d = w_down.shape[2]
    tm = _row_tile(s, FFN_FWD_ROWS)
    per = tm // HALO

    def body(gate_ref, prev_ref, val_ref, w_ref, b_ref, wd_ref, h_ref, hid_ref, o_ref):
        i = pl.program_id(0)
        total = h_ref[...]
        for j in range(nj):
            prev = jnp.where(i > 0, prev_ref[j].astype(F32), 0.0)
            ext = jnp.concatenate([prev, gate_ref[j].astype(F32)], axis=0)
            conv = b_ref[j] + w_ref[j, 2:3, :] * ext[HALO:]
            conv = conv + w_ref[j, 1:2, :] * _shift_down(ext, 1)[HALO:]
            conv = conv + w_ref[j, 0:1, :] * _shift_down(ext, 2)[HALO:]
            hidden = (_silu(conv) * val_ref[j].astype(F32)).astype(BF16)
            hid_ref[j] = hidden
            total = total + _dot(hidden, wd_ref[j])
        o_ref[...] = total

    row = pl.BlockSpec((tm, d), lambda i: (i, 0))
    return pl.pallas_call(
        body, name=name, grid=(s // tm,),
        in_specs=[pl.BlockSpec((None, nj, tm, fb), lambda i: (0, 0, i, 0)),
                  pl.BlockSpec((None, nj, HALO, fb), lambda i: (0, 0, jnp.maximum(i * per - 1, 0), 0)),
                  pl.BlockSpec((None, nj, tm, fb), lambda i: (1, 0, i, 0)),
                  pl.BlockSpec((nj, CONV_WIDTH, fb), lambda i: (0, 0, 0)), pl.BlockSpec((nj, 1, fb), lambda i: (0, 0, 0)),
                  pl.BlockSpec((nj, fb, d), lambda i: (0, 0, 0)), row],
        out_specs=[pl.BlockSpec((nj, tm, fb), lambda i: (0, i, 0)), row],
        out_shape=[jax.ShapeDtypeStruct((nj, s, fb), BF16), jax.ShapeDtypeStruct((s, d), F32)],
        compiler_params=_params(("parallel",)),
    )(u, u, u, conv_w, conv_b, w_down, h)


def _ffn_hidden_up_bwd(name, u, dh, conv_w, conv_b, w_up, h, gain, dres):
    _, nj, s, fb = u.shape
    d = w_up.shape[2]
    tm = _row_tile(s, FFN_BWD_ROWS)
    per = tm // HALO
    nblk = s // HALO
    ni = s // tm

    def body(gate_ref, gprev_ref, gnext_ref, val_ref, vnext_ref, dh_ref, dhnext_ref, w_ref, b_ref, wu_ref, h_ref, gain_ref, dres_ref,
             du_ref, dw_ref, db_ref, dx_ref, dxb_ref, dgain_ref):
        i = pl.program_id(0)
        has_next = i < ni - 1
        total = None
        for j in range(nj):
            gprev = jnp.where(i > 0, gprev_ref[j].astype(F32), 0.0)
            gext = jnp.concatenate([gprev, gate_ref[j].astype(F32), gnext_ref[j].astype(F32)], axis=0)
            vext = jnp.concatenate([val_ref[j].astype(F32), vnext_ref[j].astype(F32)], axis=0)
            dhext = jnp.concatenate([dh_ref[j].astype(F32), jnp.where(has_next, dhnext_ref[j].astype(F32), 0.0)], axis=0)
            g0 = gext[HALO:]
            g1 = _shift_down(gext, 1)[HALO:]
            g2 = _shift_down(gext, 2)[HALO:]
            conv = b_ref[j] + w_ref[j, 2:3, :] * g0 + w_ref[j, 1:2, :] * g1 + w_ref[j, 0:1, :] * g2
            act, dact = _silu_and_grad(conv)
            dconv = dhext * vext * dact
            dgate = (w_ref[j, 2:3, :] * dconv + w_ref[j, 1:2, :] * _shift_up(dconv, 1) + w_ref[j, 0:1, :] * _shift_up(dconv, 2))[:tm].astype(BF16)
            dval = (dhext * act)[:tm].astype(BF16)
            du_ref[0, j] = dgate
            du_ref[1, j] = dval
            part = _dot(dgate, wu_ref[j]) + _dot(dval, wu_ref[nj + j])
            total = part if total is None else total + part
            own = dconv[:tm]
            dw = jnp.concatenate([jnp.sum(own * g2[:tm], axis=0, keepdims=True), jnp.sum(own * g1[:tm], axis=0, keepdims=True),
                                  jnp.sum(own * g0[:tm], axis=0, keepdims=True)], axis=0)
            db = jnp.sum(own, axis=0, keepdims=True)

            @pl.when(i == 0)
            def _():
                dw_ref[j] = dw
                db_ref[j] = db

            @pl.when(i > 0)
            def _():
                dw_ref[j] += dw
                db_ref[j] += db

        xv = h_ref[...]
        r = lax.rsqrt(jnp.mean(xv * xv, axis=-1, keepdims=True) + EPS)
        xhat = xv * r
        dgain = jnp.sum(total * xhat, axis=0, keepdims=True)

        @pl.when(i == 0)
        def _():
            dgain_ref[...] = dgain

        @pl.when(i > 0)
        def _():
            dgain_ref[...] += dgain

        dxhat = total * gain_ref[...]
        dx = dres_ref[...] + r * (dxhat - xhat * jnp.mean(dxhat * xhat, axis=-1, keepdims=True))
        dx_ref[...] = dx
        dxb_ref[...] = dx.astype(BF16)

    def tile(part):
        return pl.BlockSpec((None, nj, tm, fb), lambda i: (part, 0, i, 0))

    def after(part):
        return pl.BlockSpec((None, nj, HALO, fb), lambda i: (part, 0, jnp.minimum((i + 1) * per, nblk - 1), 0))

    row = pl.BlockSpec((tm, d), lambda i: (i, 0))
    return pl.pallas_call(
        body, name=name, grid=(ni,),
        in_specs=[tile(0), pl.BlockSpec((None, nj, HALO, fb), lambda i: (0, 0, jnp.maximum(i * per - 1, 0), 0)), after(0),
                  tile(1), after(1),
                  pl.BlockSpec((nj, tm, fb), lambda i: (0, i, 0)),
                  pl.BlockSpec((nj, HALO, fb), lambda i: (0, jnp.minimum((i + 1) * per, nblk - 1), 0)),
                  pl.BlockSpec((nj, CONV_WIDTH, fb), lambda i: (0, 0, 0)), pl.BlockSpec((nj, 1, fb), lambda i: (0, 0, 0)),
                  pl.BlockSpec((2 * nj, fb, d), lambda i: (0, 0, 0)), row, pl.BlockSpec((1, d), lambda i: (0, 0)), row],
        out_specs=[pl.BlockSpec((2, nj, tm, fb), lambda i: (0, 0, i, 0)),
                   pl.BlockSpec((nj, CONV_WIDTH, fb), lambda i: (0, 0, 0)), pl.BlockSpec((nj, 1, fb), lambda i: (0, 0, 0)),
                   row, row, pl.BlockSpec((1, d), lambda i: (0, 0))],
        out_shape=[jax.ShapeDtypeStruct((2, nj, s, fb), BF16), jax.ShapeDtypeStruct((nj, CONV_WIDTH, fb), F32),
                   jax.ShapeDtypeStruct((nj, 1, fb), F32), jax.ShapeDtypeStruct((s, d), F32), jax.ShapeDtypeStruct((s, d), BF16),
                   jax.ShapeDtypeStruct((1, d), F32)],
        compiler_params=_params(("arbitrary",)),
    )(u, u, u, u, u, dh, dh, conv_w, conv_b, w_up, h, gain, dres)


ATT_TILE = 512


def _stack_heads(ref, rows, first_head, count):
    hd = ATT_HEAD_DIM
    return jnp.concatenate([ref[rows, (first_head + j) * hd:(first_head + j + 1) * hd] for j in range(count)], axis=0)


def _unstack_heads(stacked, ref, rows, first_head, count):
    hd = ATT_HEAD_DIM
    for pair in range(count // 2):
        both = [stacked[(2 * pair + j) * WINDOW:(2 * pair + j + 1) * WINDOW, :] for j in range(2)]
        ref[rows, (first_head + 2 * pair) * hd:(first_head + 2 * pair + 2) * hd] = jnp.concatenate(both, axis=1).astype(ref.dtype)


def _attn_probs_t(kb, qs, sink_ref, first_head, count, first, n_heads):
    lanes = count * WINDOW
    ik = lax.broadcasted_iota(jnp.int32, (2 * WINDOW, lanes), 0)
    iq = lax.broadcasted_iota(jnp.int32, (2 * WINDOW, lanes), 1) % WINDOW
    dist = iq + WINDOW - ik
    valid = (dist >= 0) & (dist < WINDOW) & (ik >= jnp.where(first, WINDOW, 0))
    per_head = lambda values: jnp.concatenate([jnp.zeros((1, WINDOW), F32) + v for v in values], axis=1)
    slope = per_head([2.0 ** (-8.0 * (first_head + j + 1) / n_heads) for j in range(count)])
    sink = per_head([sink_ref[0, first_head + j] for j in range(count)])
    sc = jnp.where(valid, _dot(kb, qs, NT) * ATT_HEAD_DIM ** -0.5 - slope * dist.astype(F32), NEG)
    m = jnp.maximum(jnp.max(sc, axis=0, keepdims=True), sink)
    e = jnp.exp(sc - m)
    es = jnp.exp(sink - m)
    inv = 1.0 / (jnp.sum(e, axis=0, keepdims=True) + es)
    return e * inv, es * inv


def _attn_specs(s, d, kvd, tq):
    per = tq // WINDOW
    return [pl.BlockSpec((tq, d), lambda i: (i, 0)), pl.BlockSpec((tq, kvd), lambda i: (i, 0)),
            pl.BlockSpec((WINDOW, kvd), lambda i: (jnp.maximum(i * per - 1, 0), 0))]


def _attn_fwd(q, kv, sinks):
    s, d = q.shape
    kvd = kv.shape[1]
    half = kvd // 2
    hd = ATT_HEAD_DIM
    nq = d // hd
    group = nq // ATT_KV_HEADS
    tq = min(s, ATT_TILE)
    per = tq // WINDOW

    def body(q_ref, kvc_ref, kvp_ref, sink_ref, o_ref, band):
        i = pl.program_id(0)
        band[0:WINDOW, :] = kvp_ref[...]
        band[WINDOW:, :] = kvc_ref[...]

        def block(b, carry):
            rows = pl.ds(pl.multiple_of(b * WINDOW, WINDOW), WINDOW)
            keys = pl.ds(pl.multiple_of(b * WINDOW, WINDOW), 2 * WINDOW)
            first = (i * per + b) == 0
            for g in range(ATT_KV_HEADS):
                p, _ = _attn_probs_t(band[keys, g * hd:(g + 1) * hd], _stack_heads(q_ref, rows, g * group, group), sink_ref,
                                     g * group, group, first, nq)
                out_t = _dot(band[keys, half + g * hd:half + (g + 1) * hd], p, TN)
                _unstack_heads(out_t.T, o_ref, rows, g * group, group)
            return carry

        lax.fori_loop(0, per, block, 0)

    return pl.pallas_call(
        body, name="attn_fwd", grid=(s // tq,),
        in_specs=_attn_specs(s, d, kvd, tq) + [pl.BlockSpec(memory_space=pltpu.SMEM)],
        out_specs=pl.BlockSpec((tq, d), lambda i: (i, 0)), out_shape=jax.ShapeDtypeStruct((s, d), BF16),
        scratch_shapes=[pltpu.VMEM((tq + WINDOW, kvd), BF16)], compiler_params=_params(("parallel",)),
    )(q, kv, kv, sinks)


def _attn_bwd(q, kv, o, do, sinks):
    s, d = q.shape
    kvd = kv.shape[1]
    half = kvd // 2
    hd = ATT_HEAD_DIM
    nq = d // hd
    group = nq // ATT_KV_HEADS
    tq = min(s, ATT_TILE)
    per = tq // WINDOW
    nt = s // tq

    def body(q_ref, kvc_ref, kvp_ref, o_ref, do_ref, sink_ref, dq_ref, dkvc_ref, dkvp_ref, ds_ref, band, dband):
        i = pl.program_id(0)
        band[0:WINDOW, :] = kvp_ref[...]
        band[WINDOW:, :] = kvc_ref[...]
        dband[...] = jnp.zeros_like(dband)
        ds_ref[...] = jnp.zeros_like(ds_ref)

        def block(b, carry):
            rows = pl.ds(pl.multiple_of(b * WINDOW, WINDOW), WINDOW)
            keys = pl.ds(pl.multiple_of(b * WINDOW, WINDOW), 2 * WINDOW)
            first = (i * per + b) == 0
            dks, dvs = [], []
            for g in range(ATT_KV_HEADS):
                kb = band[keys, g * hd:(g + 1) * hd]
                vb = band[keys, half + g * hd:half + (g + 1) * hd]
                qs = _stack_heads(q_ref, rows, g * group, group)
                dos = _stack_heads(do_ref, rows, g * group, group)
                p, ps = _attn_probs_t(kb, qs, sink_ref, g * group, group, first, nq)
                prod = dos.astype(F32) * _stack_heads(o_ref, rows, g * group, group).astype(F32)
                dsum = lax.dot_general(jnp.ones((8, hd), F32), prod, NT, precision=lax.Precision.HIGHEST,
                                       preferred_element_type=F32)[0:1, :]
                dsc = p * (_dot(vb, dos, NT) - dsum) * ATT_HEAD_DIM ** -0.5
                dvs.append(_dot(p, dos))
                dks.append(_dot(dsc, qs))
                _unstack_heads(_dot(kb, dsc, TN).T, dq_ref, rows, g * group, group)
                gone = ps * dsum
                for j in range(group):
                    ds_ref[g * group + j:g * group + j + 1, :] += jnp.zeros((1, 128), F32) - jnp.sum(gone[:, j * WINDOW:(j + 1) * WINDOW])
            dband[keys, 0:half] += jnp.concatenate(dks, axis=1)
            dband[keys, half:] += jnp.concatenate(dvs, axis=1)
            return carry

        lax.fori_loop(0, per, block, 0)
        dkvp_ref[...] = dband[0:WINDOW, :]
        dkvc_ref[...] = dband[WINDOW:, :]

    big = pl.BlockSpec((tq, d), lambda i: (i, 0))
    return pl.pallas_call(
        body, name="attn_bwd", grid=(nt,),
        in_specs=_attn_specs(s, d, kvd, tq) + [big, big, pl.BlockSpec(memory_space=pltpu.SMEM)],
        out_specs=[big, pl.BlockSpec((tq, kvd), lambda i: (i, 0)), pl.BlockSpec((None, WINDOW, kvd), lambda i: (i, 0, 0)),
                   pl.BlockSpec((None, nq, 128), lambda i: (i, 0, 0))],
        out_shape=[jax.ShapeDtypeStruct((s, d), BF16), jax.ShapeDtypeStruct((s, kvd), F32), jax.ShapeDtypeStruct((nt, WINDOW, kvd), F32),
                   jax.ShapeDtypeStruct((nt, nq, 128), F32)],
        scratch_shapes=[pltpu.VMEM((tq + WINDOW, kvd), BF16), pltpu.VMEM((tq + WINDOW, kvd), F32)],
        compiler_params=_params(("parallel",)),
    )(q, kv, kv, o, do, sinks)


HBM_SPEC = pl.BlockSpec(memory_space=pltpu.HBM)
VMEM_SPEC = pl.BlockSpec(memory_space=pltpu.VMEM)


def _place():
    return lax.axis_index("x"), lax.axis_index("y"), lax.axis_index("c")


def _flip(pos, r):
    return tuple(1 - p if (r >> (2 - a)) & 1 else p for a, p in enumerate(pos))


def _index(pos):
    return 4 * pos[0] + 2 * pos[1] + pos[2]


def _all_gather(name, shards, spec):
    n = len(shards)

    def body(*refs):
        x_refs, o_refs = refs[:n], refs[n:2 * n]
        send_sems, recv_sems, local_sems = refs[2 * n:]
        me = _place()
        sibling = _flip(me, 1)
        far = [_flip(me, r) for r in (4, 2, 6)]

        def copy(t, sem, block, to, src=None):
            rows = o_refs[t].at[_index(block)]
            return pltpu.make_async_remote_copy(
                src_ref=rows if src is None else src, dst_ref=rows, send_sem=send_sems.at[t, sem], recv_sem=recv_sems.at[t, sem],
                device_id=to, device_id_type=MESH)

        own = [pltpu.make_async_copy(x_refs[t], o_refs[t].at[_index(me)], local_sems.at[t]) for t in range(n)]
        for cp in own:
            cp.start()
        first = []
        for t in range(n):
            first.append(copy(t, 0, me, sibling, src=x_refs[t]))
            first += [copy(t, 1 + j, me, peer, src=x_refs[t]) for j, peer in enumerate(far)]
        for cp in first:
            cp.start()
        passed = []
        for j, peer in enumerate(far):
            for t in range(n):
                copy(t, 1 + j, peer, me).wait_recv()
                cp = copy(t, 4 + j, peer, sibling)
                cp.start()
                passed.append(cp)
        for t in range(n):
            copy(t, 0, sibling, me).wait_recv()
            for j, peer in enumerate(far):
                copy(t, 4 + j, _flip(peer, 1), me).wait_recv()
        for cp in first + passed:
            cp.wait_send()
        for cp in own:
            cp.wait()

    return pl.pallas_call(
        body, name=name, in_specs=[spec] * n, out_specs=[spec] * n,
        out_shape=[jax.ShapeDtypeStruct((N_DEV,) + sh.shape, sh.dtype) for sh in shards],
        scratch_shapes=[pltpu.SemaphoreType.DMA((n, 7)), pltpu.SemaphoreType.DMA((n, 7)), pltpu.SemaphoreType.DMA((n,))],
    )(*shards)


SEM_SPEC = pl.BlockSpec(memory_space=pltpu.SEMAPHORE)
ANY_SPEC = pl.BlockSpec(memory_space=pl.ANY)


def _landing(own, mine):
    return lax.dynamic_update_slice(lax.empty((N_DEV,) + own.shape, own.dtype), own[None], (mine,) + (0,) * own.ndim)


def _pinned(a, token):
    return a + token[0:1, 0:1].astype(a.dtype)


def _peer_copies(src_refs, land_refs, send_sems, recv_sems, scatter, arrivals):
    me = _place()
    mine = _index(me)
    copies = []
    for t, (src, land) in enumerate(zip(src_refs, land_refs)):
        for r in range(1, N_DEV):
            peer = _flip(me, r)
            theirs = _index(peer)
            sem = t * (N_DEV - 1) + r - 1
            copies.append(pltpu.make_async_remote_copy(
                src_ref=src.at[theirs] if scatter else src, dst_ref=land.at[theirs if arrivals else mine],
                send_sem=send_sems.at[sem], recv_sem=recv_sems.at[sem], device_id=peer, device_id_type=MESH))
    return copies


def _send_start(name, sources, lands, scatter, after=None):
    n = len(sources)
    extra = 0 if after is None else 1

    def body(*refs):
        outs = refs[2 * n + extra:]
        for out in _peer_copies(refs[:n], refs[n:2 * n], outs[0], outs[1], scatter, False):
            out.start()
        outs[-1][...] = jnp.zeros_like(outs[-1])

    outs = pl.pallas_call(
        body, name=name, in_specs=[HBM_SPEC] * (2 * n) + [ANY_SPEC] * extra,
        out_specs=[SEM_SPEC, SEM_SPEC] + [HBM_SPEC] * (2 * n) + [VMEM_SPEC],
        out_shape=[pltpu.SemaphoreType.DMA((n * (N_DEV - 1),)), pltpu.SemaphoreType.DMA((n * (N_DEV - 1),))]
        + [pltpu.HBM(a.shape, a.dtype) for a in list(sources) + list(lands)] + [jax.ShapeDtypeStruct((8, 128), F32)],
        input_output_aliases={i: 2 + i for i in range(2 * n)},
        compiler_params=pltpu.CompilerParams(has_side_effects=pltpu.SideEffectType.DATAFLOW_SIDE_EFFECTING),
    )(*[pltpu.with_memory_space_constraint(a, pltpu.HBM) for a in list(sources) + list(lands)], *([] if after is None else [after]))
    return outs[0], outs[1], outs[2:2 + n], outs[2 + n:2 + 2 * n], outs[-1]


def _send_wait(name, started, after, scatter):
    send_sems, recv_sems, sources, lands, _ = started
    n = len(sources)

    def body(*refs):
        for out in _peer_copies(refs[:n], refs[n:2 * n], refs[2 * n], refs[2 * n + 1], scatter, False):
            out.wait_send()
        for arrival in _peer_copies(refs[:n], refs[n:2 * n], refs[2 * n], refs[2 * n + 1], scatter, True):
            arrival.wait_recv()

    outs = pl.pallas_call(
        body, name=name, in_specs=[HBM_SPEC] * (2 * n) + [SEM_SPEC, SEM_SPEC, ANY_SPEC], out_specs=[HBM_SPEC] * (2 * n),
        out_shape=[pltpu.HBM(a.shape, a.dtype) for a in list(sources) + list(lands)],
        input_output_aliases={i: i for i in range(2 * n)},
        compiler_params=pltpu.CompilerParams(has_side_effects=pltpu.SideEffectType.DATAFLOW_SIDE_EFFECTING),
    )(*sources, *lands, send_sems, recv_sems, after)
    return outs[n:]


def _pack_rows(parts):
    offsets, row = [], 0
    for part in parts:
        offsets.append(row)
        row += part.shape[0]
    return offsets, -(-row // 8) * 8, -(-max(part.shape[1] for part in parts) // 128) * 128


def _pack(name, parts):
    offsets, rows, width = _pack_rows(parts)

    def body(*refs):
        o_ref = refs[-1]
        o_ref[...] = jnp.zeros_like(o_ref)
        for off, ref in zip(offsets, refs[:-1]):
            o_ref[off:off + ref.shape[0], 0:ref.shape[1]] = ref[...]

    return pl.pallas_call(body, name=name, in_specs=[VMEM_SPEC] * len(parts), out_specs=VMEM_SPEC,
                          out_shape=jax.ShapeDtypeStruct((rows, width), F32))(*parts)


def _adamw_math(w, g, m, v):
    m = ADAM_B1 * m + (1.0 - ADAM_B1) * g
    v = ADAM_B2 * v + (1.0 - ADAM_B2) * (g * g)
    m_hat = m * (1.0 / (1.0 - ADAM_B1 ** ADAM_STEP))
    denom = jnp.sqrt(v * (1.0 / (1.0 - ADAM_B2 ** ADAM_STEP))) + ADAM_EPS
    inv = pl.reciprocal(denom, approx=True)
    inv = inv * (2.0 - denom * inv)
    return -ADAM_LR * (m_hat * inv + ADAM_WD * w), m, v


def _adamw_step(w_ref, m_ref, v_ref, p_ref, g_ref, d_ref, nm_ref, nv_ref):
    g = p_ref[0].astype(F32)
    for dev in range(1, N_DEV):
        g = g + p_ref[dev].astype(F32)
    g_ref[...] = g
    d_ref[...], nm_ref[...], nv_ref[...] = _adamw_math(w_ref[...], g, m_ref[...], v_ref[...])


def _adamw_rows(rows):
    return max(t for t in range(8, min(rows, 256) + 1, 8) if rows % t == 0)


def _adamw_shard(name, w, m, v, partials):
    rows, cols = w.shape
    tr = _adamw_rows(rows)
    blk = pl.BlockSpec((tr, cols), lambda i: (i, 0))
    return pl.pallas_call(
        _adamw_step_fn(), name=name, grid=(rows // tr,), in_specs=[blk, blk, blk, pl.BlockSpec((N_DEV, tr, cols), lambda i: (0, i, 0))],
        out_specs=[blk] * 4, out_shape=[jax.ShapeDtypeStruct((rows, cols), F32)] * 4, compiler_params=_params(("parallel",)),
    )(w, m, v, partials)


def _adamw_step_fn():
    return functools.partial(_adamw_step)


def _adamw_layers(name, w, m, v, partials):
    layers, rows, cols = w.shape
    tr = _adamw_rows(rows)
    last = rows // tr - 1

    def body(w_ref, m_ref, v_ref, *rest):
        for layer in range(layers):
            @pl.when(pl.program_id(0) == layer)
            def _():
                _adamw_step(w_ref, m_ref, v_ref, rest[layer], *rest[layers:])

    blk = pl.BlockSpec((None, tr, cols), lambda l, i: (l, i, 0))
    part = lambda layer: pl.BlockSpec((N_DEV, tr, cols), lambda l, i: (0, jnp.where(l == layer, i, jnp.where(l < layer, 0, last)), 0))
    return pl.pallas_call(
        body, name=name, grid=(layers, rows // tr), in_specs=[blk, blk, blk] + [part(layer) for layer in range(layers)],
        out_specs=[blk] * 4, out_shape=[jax.ShapeDtypeStruct(w.shape, F32)] * 4, compiler_params=_params(("arbitrary", "arbitrary")),
    )(w, m, v, *partials)


def _adamw_small(gathered, places, entries):
    n = len(entries)
    np_ = len(gathered)

    def body(*refs):
        pack_refs = refs[:np_]
        refs = refs[np_ - 1:]
        w_refs, m_refs, v_refs = refs[1:1 + n], refs[1 + n:1 + 2 * n], refs[1 + 2 * n:1 + 3 * n]
        outs = refs[1 + 3 * n:]
        totals = []
        for pack_ref in pack_refs:
            acc = pack_ref[0]
            for dev in range(1, N_DEV):
                acc = acc + pack_ref[dev]
            totals.append(acc)
        mine = _index(_place())
        for e in range(n):
            rows, cols = w_refs[e].shape
            total, off = totals[places[e][0]], places[e][1]
            if entries[e][3]:
                g = jnp.zeros((rows, cols), F32)
                for dev in range(N_DEV):
                    g = g + jnp.where(mine == dev, total[off + dev * rows:off + (dev + 1) * rows, 0:cols], 0.0)
            else:
                g = total[off:off + rows, 0:cols]
            outs[4 * e][...] = g
            outs[4 * e + 1][...], outs[4 * e + 2][...], outs[4 * e + 3][...] = _adamw_math(w_refs[e][...], g, m_refs[e][...], v_refs[e][...])
        outs[4 * n][...] = totals[places[n][0]][places[n][1]:places[n][1] + 1, 0:128]

    shapes = []
    for w, _, _, _ in entries:
        shapes += [jax.ShapeDtypeStruct(w.shape, F32)] * 4
    shapes.append(jax.ShapeDtypeStruct((1, 128), F32))
    return pl.pallas_call(
        body, name="adamw_small", in_specs=[VMEM_SPEC] * (np_ + 3 * n), out_specs=[VMEM_SPEC] * len(shapes), out_shape=shapes,
        compiler_params=pltpu.CompilerParams(vmem_limit_bytes=VMEM_LIMIT),
    )(*gathered, *[e[0] for e in entries], *[e[1] for e in entries], *[e[2] for e in entries])


def _ffn_forward(tag, h, gain, w_up, w_down, conv_w, conv_b, after_up=None):
    s, d = h.shape
    fb = w_up.shape[1]
    tm = _row_tile(s, MM_ROWS)
    a, = _rmsnorm_cast(f"ffn_norm_{tag}", h, [gain])
    u = _matmul(
        f"ffn_up_{tag}", a, w_up, dims=NT, grid=(s // tm, N_DEV, 1),
        a_spec=pl.BlockSpec((tm, d), lambda i, j, k: (i, 0)),
        b_spec=pl.BlockSpec((None, fb, d), lambda i, j, k: (j, 0, 0)),
        o_spec=pl.BlockSpec((None, None, tm, fb), lambda i, j, k: (j // 4, j % 4, i, 0)),
        out_shape=jax.ShapeDtypeStruct((2, 4, s, fb), BF16))
    if after_up is not None:
        conv_b = _pinned(conv_b, after_up(u))
    hidden, out = _ffn_hidden_down(f"ffn_hidden_down_{tag}", u, conv_w, conv_b, w_down, h)
    return out, (a, u, hidden)


def _ffn_backward(tag, h, gain, w_up, w_down, conv_w, conv_b, saved, dout):
    a, u, hidden = saved
    dout, dout_bf = dout
    s, d = h.shape
    fb = w_up.shape[1]
    tm = _row_tile(s, MM_ROWS)
    dhidden = _matmul(
        f"ffn_down_bwd_{tag}", dout_bf, w_down, dims=NT, grid=(s // tm, 4, 1),
        a_spec=pl.BlockSpec((tm, d), lambda i, j, k: (i, 0)),
        b_spec=pl.BlockSpec((None, fb, d), lambda i, j, k: (j, 0, 0)),
        o_spec=pl.BlockSpec((None, tm, fb), lambda i, j, k: (j, i, 0)),
        out_shape=jax.ShapeDtypeStruct((4, s, fb), BF16))
    dw_down = _matmul(
        f"ffn_down_grad_{tag}", hidden, dout_bf, dims=TN, grid=(4, 1, 1),
        a_spec=pl.BlockSpec((None, s, fb), lambda i, j, k: (i, 0, 0)),
        b_spec=pl.BlockSpec((s, d), lambda i, j, k: (0, 0)),
        o_spec=pl.BlockSpec((None, fb, d), lambda i, j, k: (i, 0, 0)),
        out_shape=jax.ShapeDtypeStruct((4, fb, d), BF16))
    du, dconv_w, dconv_b, dh, dh_bf, dgain = _ffn_hidden_up_bwd(f"ffn_hidden_up_bwd_{tag}", u, dhidden, conv_w, conv_b, w_up, h, gain, dout)
    dw_up = _matmul(
        f"ffn_up_grad_{tag}", du, a, dims=TN, grid=(N_DEV, 1, 1),
        a_spec=pl.BlockSpec((None, None, s, fb), lambda i, j, k: (i // 4, i % 4, 0, 0)),
        b_spec=pl.BlockSpec((s, d), lambda i, j, k: (0, 0)),
        o_spec=pl.BlockSpec((None, fb, d), lambda i, j, k: (i, 0, 0)),
        out_shape=jax.ShapeDtypeStruct((N_DEV, fb, d), BF16))
    return (dh, dh_bf), dgain, dw_up, dw_down, dconv_w, dconv_b


def kernel(x, hg_norm, hg_w_in, hg_lb_logits, hg_out_norm, hg_w_out, kv_norm, w_kv, attn_norm, attn_w_q, attn_sinks, attn_w_o, ffn_norm, ffn_w_up, ffn_conv_w, ffn_conv_b, ffn_w_down, final_norm, loss_target, m_hg_norm, m_hg_w_in, m_hg_lb_logits, m_hg_out_norm, m_hg_w_out, m_kv_norm, m_w_kv, m_attn_norm, m_attn_w_q, m_attn_sinks, m_attn_w_o, m_ffn_norm, m_ffn_w_up, m_ffn_conv_w, m_ffn_conv_b, m_ffn_w_down, m_final_norm, v_hg_norm, v_hg_w_in, v_hg_lb_logits, v_hg_out_norm, v_hg_w_out, v_kv_norm, v_w_kv, v_attn_norm, v_attn_w_q, v_attn_sinks, v_attn_w_o, v_ffn_norm, v_ffn_w_up, v_ffn_conv_w, v_ffn_conv_b, v_ffn_w_down, v_final_norm):
    _, s, d = x.shape
    x0, target = x[0], loss_target[0]
    half = hg_w_in.shape[2]
    fs = ffn_conv_w.shape[2]
    fb = 2 * fs
    kvd = w_kv.shape[1]
    nq = d // ATT_HEAD_DIM
    tm = _row_tile(s, MM_ROWS)

    mine = _index(_place())
    gather = lambda tag, shards, after: _send_start("gather_start_" + tag, shards, [_landing(a, mine) for a in shards], False, after)
    w_in, g_hgn, g_lbl, w_out = _all_gather("gather_hg", [hg_w_in[0].astype(BF16), hg_norm, hg_lb_logits, hg_w_out[0].astype(BF16)], HBM_SPEC)
    w_out = w_out.reshape(d, d)
    up_t = lambda a: jnp.swapaxes(a, -1, -2)
    coming_ffn0 = gather("ffn0", [up_t(ffn_w_up[0]).astype(BF16), ffn_conv_w, ffn_w_down[0].astype(BF16)], g_hgn)
    hgn = _pinned(g_hgn.reshape(1, d), coming_ffn0[4])
    lbl = g_lbl.transpose(1, 0, 2).reshape(2, d)
    conv_b = [ffn_conv_b[layer].reshape(4, 1, fb) for layer in range(2)]
    gains = [ffn_norm[0:1], ffn_norm[1:2]]
    kvn, fin = kv_norm.reshape(1, d), final_norm.reshape(1, d)

    a0, = _rmsnorm_cast("hg_norm", x0, [hgn])
    p = _matmul(
        "hg_in", a0, w_in, dims=NN, grid=(s // tm, N_DEV, 1),
        a_spec=pl.BlockSpec((tm, d), lambda i, j, k: (i, 0)),
        b_spec=pl.BlockSpec((None, d, half), lambda i, j, k: (j, 0, 0)),
        o_spec=pl.BlockSpec((None, tm, half), lambda i, j, k: (j // 2, i, j % 2)),
        out_shape=jax.ShapeDtypeStruct((4, s, d), BF16), acc_shape=(8, 128))
    o, og, states = _hgrn2_fwd(p, lbl, hg_out_norm)
    x1 = _mm_rows("hg_out", og, w_out, out_dtype=F32, add=x0)
    w_up0, g_cw, w_dn0 = _send_wait("gather_wait_ffn0", coming_ffn0, x1, False)
    w_up, w_dn = [w_up0, None], [w_dn0.reshape(4, fb, d), None]
    conv_w = [g_cw[:, layer].reshape(4, 2, CONV_WIDTH, fs).transpose(0, 2, 1, 3).reshape(4, CONV_WIDTH, fb) for layer in range(2)]
    coming_attn = gather("attn", [w_kv.astype(BF16), attn_w_q[0].astype(BF16), attn_w_o[0].astype(BF16)], w_dn0)
    gains[0] = _pinned(gains[0], coming_attn[4])
    coming = []

    def start_ffn1(u):
        coming.append(gather("ffn1", [up_t(ffn_w_up[1]).astype(BF16), ffn_w_down[1].astype(BF16)], u))
        return coming[0][4]

    x2, saved0 = _ffn_forward("0", x1, gains[0], w_up[0], w_dn[0], conv_w[0], conv_b[0], start_ffn1)
    coming_ffn1 = coming[0]
    w_kvg, w_q, w_o = _send_wait("gather_wait_attn", coming_attn, x2, False)
    w_kvg, w_q, w_o = w_kvg.reshape(d, kvd), w_q.reshape(d, d), w_o.reshape(d, d)
    akv, a2 = _rmsnorm_cast("attn_norms", x2, [kvn, attn_norm])
    kv = _mm_rows("kv_proj", akv, w_kvg, out_dtype=BF16)
    q = _mm_rows("q_proj", a2, w_q, out_dtype=BF16)
    att = _attn_fwd(q, kv, attn_sinks)
    x3 = _mm_rows("attn_out", att, w_o, out_dtype=F32, add=x2)
    w_up[1], w_dn1 = _send_wait("gather_wait_ffn1", coming_ffn1, x3, False)
    w_dn[1] = w_dn1.reshape(4, fb, d)
    x4, saved1 = _ffn_forward("1", x3, gains[1], w_up[1], w_dn[1], conv_w[1], conv_b[1])
    dx4, dx4_bf, d_fin, loss_part = _loss_head(x4, fin, target)

    dx3, d_fn1, dw_up1, dw_dn1, dcw1, dcb1 = _ffn_backward("1", x3, gains[1], w_up[1], w_dn[1], conv_w[1], conv_b[1], saved1, (dx4, dx4_bf))
    rows = d // N_DEV
    scatter = lambda tag, stacks: _send_start("scatter_start_" + tag, stacks, [_landing(lax.dynamic_index_in_dim(a, mine, keepdims=False), mine) for a in stacks], True)
    going_ffn1 = scatter("ffn1", [dw_up1, dw_dn1.reshape(N_DEV, fs, d)])
    datt = _mm_rows_nt("attn_out_bwd", dx3[1], w_o, out_dtype=BF16)
    dw_o = _mm_tn("attn_out_grad", att, dx3[1])
    dq, dkv_own, dkv_before, dsink = _attn_bwd(q, kv, att, datt, _pinned(attn_sinks, going_ffn1[4]))
    tiles = dkv_before.shape[0]
    dkv = dkv_own.reshape(tiles, s // tiles, kvd)
    dkv = jnp.concatenate([dkv[:, :-WINDOW], dkv[:, -WINDOW:] + jnp.pad(dkv_before[1:], ((0, 1), (0, 0), (0, 0)))], axis=1).reshape(s, kvd)
    da2 = _mm_rows_nt("q_proj_bwd", dq, w_q, out_dtype=F32)
    dw_q = _mm_tn("q_proj_grad", a2, dq)
    dakv = _mm_rows_nt("kv_proj_bwd", dkv, w_kvg, out_dtype=F32)
    dw_kv = _mm_tn("kv_proj_grad", akv, dkv)
    going_attn = scatter("attn", [dw_kv.reshape(N_DEV, rows, kvd), dw_q.reshape(N_DEV, rows, d), dw_o.reshape(N_DEV, rows, d)])
    dx2, (d_kvn, d_attn) = _rmsnorm_bwd("attn_norms_bwd", x2, dx3[0], [(dakv, _pinned(kvn, going_attn[4])), (da2, attn_norm)])
    dx1, d_fn0, dw_up0, dw_dn0, dcw0, dcb0 = _ffn_backward("0", x1, gains[0], w_up[0], w_dn[0], conv_w[0], conv_b[0], saved0, dx2)
    going_ffn0 = scatter("ffn0", [dw_up0, dw_dn0.reshape(N_DEV, fs, d)])
    dog = _mm_rows_nt("hg_out_bwd", dx1[1], w_out, out_dtype=F32)
    dw_out = _mm_tn("hg_out_grad", og, dx1[1])
    dp, d_lbl, d_ogain = _hgrn2_bwd(p, lbl, _pinned(hg_out_norm, going_ffn0[4]), o, dog, states)
    dw_in = _matmul(
        "hg_in_grad", a0, dp, dims=TN, grid=(1, N_DEV, 1),
        a_spec=pl.BlockSpec((s, d), lambda i, j, k: (0, 0)),
        b_spec=pl.BlockSpec((None, s, half), lambda i, j, k: (j // 2, 0, j % 2)),
        o_spec=pl.BlockSpec((None, d, half), lambda i, j, k: (j, 0, 0)),
        out_shape=jax.ShapeDtypeStruct((N_DEV, d, half), BF16))
    going_hg = scatter("hg", [dw_in, dw_out.reshape(N_DEV, rows, d)])
    th = _row_tile(s, MM_ROWS // 2)
    da0 = _matmul(
        "hg_in_bwd", dp, w_in, dims=NT, grid=(s // th, 1, 1),
        a_spec=pl.BlockSpec((4, th, d), lambda i, j, k: (0, i, 0)),
        b_spec=pl.BlockSpec((N_DEV, d, half), lambda i, j, k: (0, 0, 0)),
        o_spec=pl.BlockSpec((th, d), lambda i, j, k: (i, 0)),
        out_shape=jax.ShapeDtypeStruct((s, d), F32),
        terms=lambda a_ref, b_ref: [(a_ref[k // 2, :, (k % 2) * half:(k % 2 + 1) * half], b_ref[k]) for k in range(N_DEV)])
    (dx0, _), (d_hgn,) = _rmsnorm_bwd("hg_norm_bwd", x0, dx1[0], [(da0, _pinned(hgn, going_hg[4]))])

    arrive = lambda tag, going, after: _send_wait("scatter_wait_" + tag, going, after, True)
    (l_up1, l_dn1), (l_kv, l_q, l_o), (l_up0, l_dn0) = arrive("ffn1", going_ffn1, dx0), arrive("attn", going_attn, dx0), arrive("ffn0", going_ffn0, dx0)
    big = {}
    for tag, w, m, v, part in [
            ("w_kv", w_kv, m_w_kv, v_w_kv, l_kv), ("attn_w_q", attn_w_q[0], m_attn_w_q[0], v_attn_w_q[0], l_q),
            ("attn_w_o", attn_w_o[0], m_attn_w_o[0], v_attn_w_o[0], l_o)]:
        big[tag] = _adamw_shard("adamw_" + tag, w, m, v, part)
    big["ffn_w_up"] = [up_t(a) for a in _adamw_layers("adamw_ffn_w_up", up_t(ffn_w_up), up_t(m_ffn_w_up), up_t(v_ffn_w_up), (l_up0, l_up1))]
    big["ffn_w_down"] = _adamw_layers("adamw_ffn_w_down", ffn_w_down, m_ffn_w_down, v_ffn_w_down, (l_dn0, l_dn1))
    lead = lambda tag: [a[None] for a in big[tag]]

    as_blocks = lambda a, r: a.reshape(r, N_DEV, -1).transpose(1, 0, 2).reshape(N_DEV * r, -1)
    d_cw = jnp.concatenate([g.transpose(1, 0, 2).reshape(CONV_WIDTH, 4 * fb) for g in (dcw0, dcw1)], axis=0)
    parts = [d_fin, jnp.concatenate([d_fn0, d_fn1], axis=0), jnp.concatenate([dcb0.reshape(1, 4 * fb), dcb1.reshape(1, 4 * fb)], axis=0),
             as_blocks(d_cw, 2 * CONV_WIDTH), d_attn, jnp.sum(dsink[:, :, 0], axis=0).reshape(1, nq), d_kvn, d_ogain,
             as_blocks(d_hgn, 1), as_blocks(d_lbl, 2), loss_part]
    wide = [2]
    packs = [[parts[i] for i in wide], [part for i, part in enumerate(parts) if i not in wide]]
    places = [None] * len(parts)
    for which, members in enumerate([wide, [i for i in range(len(parts)) if i not in wide]]):
        for i, off in zip(members, _pack_rows(packs[which])[0]):
            places[i] = (which, off)
    gathered = _all_gather("gather_small_grads", [_pack("pack_wide_grads", packs[0]), _pack("pack_narrow_grads", packs[1])], VMEM_SPEC)
    two = lambda a: a.reshape(-1, a.shape[-1])
    small = [(fin, m_final_norm.reshape(1, d), v_final_norm.reshape(1, d), False), (ffn_norm, m_ffn_norm, v_ffn_norm, False),
             (ffn_conv_b, m_ffn_conv_b, v_ffn_conv_b, False), (two(ffn_conv_w), two(m_ffn_conv_w), two(v_ffn_conv_w), True),
             (attn_norm, m_attn_norm, v_attn_norm, False), (attn_sinks, m_attn_sinks, v_attn_sinks, False),
             (kvn, m_kv_norm.reshape(1, d), v_kv_norm.reshape(1, d), False), (hg_out_norm, m_hg_out_norm, v_hg_out_norm, False),
             (hg_norm, m_hg_norm, v_hg_norm, True), (hg_lb_logits, m_hg_lb_logits, v_hg_lb_logits, True)]
    res = _adamw_small(gathered, places, small)
    l_in, l_out = arrive("hg", going_hg, gathered[1])
    big["hg_w_in"] = _adamw_shard("adamw_hg_w_in", hg_w_in[0], m_hg_w_in[0], v_hg_w_in[0], l_in)
    big["hg_w_out"] = _adamw_shard("adamw_hg_w_out", hg_w_out[0], m_hg_w_out[0], v_hg_w_out[0], l_out)
    names = ["final_norm", "ffn_norm", "ffn_conv_b", "ffn_conv_w", "attn_norm", "attn_sinks", "kv_norm", "hg_out_norm", "hg_norm", "hg_lb_logits"]
    shapes = {"final_norm": final_norm.shape, "kv_norm": kv_norm.shape, "ffn_conv_w": ffn_conv_w.shape}
    out = {n: [a.reshape(shapes[n]) if n in shapes else a for a in res[4 * i:4 * i + 4]] for i, n in enumerate(names)}
    out.update(hg_w_in=lead("hg_w_in"), hg_w_out=lead("hg_w_out"), w_kv=big["w_kv"], attn_w_q=lead("attn_w_q"), attn_w_o=lead("attn_w_o"),
               ffn_w_up=big["ffn_w_up"], ffn_w_down=big["ffn_w_down"])
    order = ["hg_norm", "hg_w_in", "hg_lb_logits", "hg_out_norm", "hg_w_out", "kv_norm", "w_kv", "attn_norm", "attn_w_q", "attn_sinks",
             "attn_w_o", "ffn_norm", "ffn_w_up", "ffn_conv_w", "ffn_conv_b", "ffn_w_down", "final_norm"]
    loss = res[-1][0, 0]
    return (loss, dx0[None], *[out[n][0] for n in order], *[out[n][1] for n in order], *[out[n][2] for n in order], *[out[n][3] for n in order])
```

```python
import functools
import math

import jax
import jax.numpy as jnp
from jax import lax
from jax.experimental import pallas as pl
from jax.experimental.pallas import tpu as pltpu

F32 = jnp.float32
BF16 = jnp.bfloat16

EPS = 1e-6
HG_EXPAND = 128
HG_CHUNK = 32
ATT_HEAD_DIM = 64
ATT_KV_HEADS = 2
WINDOW = 128
CONV_WIDTH = 3
ADAM_LR = 0.001
ADAM_B1 = 0.9
ADAM_B2 = 0.999
ADAM_EPS = 1e-08
ADAM_WD = 0.01
ADAM_STEP = 10

N_DEV = 8
VMEM_LIMIT = 48 * 1024 * 1024
NEG = -1e30

NN = (((1,), (0,)), ((), ()))
NT = (((1,), (1,)), ((), ()))
TN = (((0,), (0,)), ((), ()))
MESH = pl.DeviceIdType.MESH


def _dot(a, b, dims=NN):
    return lax.dot_general(a.astype(BF16), b.astype(BF16), dims, preferred_element_type=F32)


def _sigmoid(x):
    return 0.5 * jnp.tanh(0.5 * x) + 0.5


def _silu(x):
    return x * _sigmoid(x)


def _silu_and_grad(x):
    s = _sigmoid(x)
    return x * s, s * (1.0 + x * (1.0 - s))


def _dsilu(x):
    return _silu_and_grad(x)[1]


def _params(semantics):
    return pltpu.CompilerParams(dimension_semantics=semantics, vmem_limit_bytes=VMEM_LIMIT)


def _row_tile(rows, want=512):
    return min(rows, want)


MM_ROWS = 1024


def _matmul(name, a, b, *, dims, grid, a_spec, b_spec, o_spec, out_shape, acc_shape=(8, 128), add=None, add_spec=None, terms=None):
    nk = grid[2]

    def body(*refs):
        if add is None:
            a_ref, b_ref, o_ref, acc = refs
        else:
            a_ref, b_ref, add_ref, o_ref, acc = refs
        k = pl.program_id(2)
        pairs = [(a_ref[...], b_ref[...])] if terms is None else terms(a_ref, b_ref)
        part = _dot(*pairs[0], dims)
        for pair in pairs[1:]:
            part = part + _dot(*pair, dims)

        def finish(total):
            if add is not None:
                total = total + add_ref[...]
            o_ref[...] = total.astype(o_ref.dtype)

        if nk == 1:
            finish(part)
        else:
            @pl.when(k == 0)
            def _():
                acc[...] = part

            @pl.when(k > 0)
            def _():
                acc[...] += part

            @pl.when(k == nk - 1)
            def _():
                finish(acc[...])

    in_specs = [a_spec, b_spec] + ([] if add is None else [add_spec])
    args = (a, b) + (() if add is None else (add,))
    return pl.pallas_call(
        body, name=name, grid=grid, in_specs=in_specs, out_specs=o_spec, out_shape=out_shape,
        scratch_shapes=[pltpu.VMEM(acc_shape, F32)],
        compiler_params=_params(("parallel", "parallel", "arbitrary")),
    )(*args)


def _mm_rows(name, a, w, *, out_dtype, add=None):
    s, kdim = a.shape
    n = w.shape[1]
    tm = _row_tile(s, MM_ROWS)
    return _matmul(
        name, a, w, dims=NN, grid=(s // tm, 1, 1),
        a_spec=pl.BlockSpec((tm, kdim), lambda i, j, k: (i, 0)),
        b_spec=pl.BlockSpec((kdim, n), lambda i, j, k: (0, 0)),
        o_spec=pl.BlockSpec((tm, n), lambda i, j, k: (i, 0)),
        out_shape=jax.ShapeDtypeStruct((s, n), out_dtype), acc_shape=(8, 128),
        add=add, add_spec=None if add is None else pl.BlockSpec((tm, n), lambda i, j, k: (i, 0)),
    )


def _mm_rows_nt(name, a, w, *, out_dtype):
    s, n = a.shape
    kdim = w.shape[0]
    tm = _row_tile(s, MM_ROWS)
    return _matmul(
        name, a, w, dims=NT, grid=(s // tm, 1, 1),
        a_spec=pl.BlockSpec((tm, n), lambda i, j, k: (i, 0)),
        b_spec=pl.BlockSpec((kdim, n), lambda i, j, k: (0, 0)),
        o_spec=pl.BlockSpec((tm, kdim), lambda i, j, k: (i, 0)),
        out_shape=jax.ShapeDtypeStruct((s, kdim), out_dtype), acc_shape=(8, 128),
    )


def _mm_tn(name, a, g):
    s, m = a.shape
    n = g.shape[1]
    tn = min(n, 512)
    return _matmul(
        name, a, g, dims=TN, grid=(1, n // tn, 1),
        a_spec=pl.BlockSpec((s, m), lambda i, j, k: (0, 0)),
        b_spec=pl.BlockSpec((s, tn), lambda i, j, k: (0, j)),
        o_spec=pl.BlockSpec((m, tn), lambda i, j, k: (0, j)),
        out_shape=jax.ShapeDtypeStruct((m, n), BF16),
    )


def _rmsnorm_cast(name, h, gains):
    s, d = h.shape
    tm = _row_tile(s)
    n = len(gains)

    def body(*refs):
        h_ref, g_refs, o_refs = refs[0], refs[1:1 + n], refs[1 + n:]
        xv = h_ref[...]
        xhat = xv * lax.rsqrt(jnp.mean(xv * xv, axis=-1, keepdims=True) + EPS)
        for g_ref, o_ref in zip(g_refs, o_refs):
            o_ref[...] = (xhat * g_ref[...]).astype(BF16)

    row = pl.BlockSpec((tm, d), lambda i: (i, 0))
    vec = pl.BlockSpec((1, d), lambda i: (0, 0))
    return pl.pallas_call(
        body, name=name, grid=(s // tm,), in_specs=[row] + [vec] * n, out_specs=[row] * n,
        out_shape=[jax.ShapeDtypeStruct((s, d), BF16)] * n, compiler_params=_params(("parallel",)),
    )(h, *gains)


def _rmsnorm_bwd(name, h, dres, branches):
    s, d = h.shape
    tm = _row_tile(s)
    n = len(branches)

    def body(*refs):
        h_ref, dres_ref = refs[0], refs[1]
        da_refs, g_refs = refs[2:2 + n], refs[2 + n:2 + 2 * n]
        dh_ref, dhb_ref, dg_refs = refs[2 + 2 * n], refs[3 + 2 * n], refs[4 + 2 * n:]
        i = pl.program_id(0)
        xv = h_ref[...]
        r = lax.rsqrt(jnp.mean(xv * xv, axis=-1, keepdims=True) + EPS)
        xhat = xv * r
        total = dres_ref[...]
        for da_ref, g_ref, dg_ref in zip(da_refs, g_refs, dg_refs):
            da = da_ref[...]
            dgain = jnp.sum(da * xhat, axis=0, keepdims=True)

            @pl.when(i == 0)
            def _():
                dg_ref[...] = dgain

            @pl.when(i > 0)
            def _():
                dg_ref[...] += dgain

            dxhat = da * g_ref[...]
            total = total + r * (dxhat - xhat * jnp.mean(dxhat * xhat, axis=-1, keepdims=True))
        dh_ref[...] = total
        dhb_ref[...] = total.astype(BF16)

    row = pl.BlockSpec((tm, d), lambda i: (i, 0))
    vec = pl.BlockSpec((1, d), lambda i: (0, 0))
    outs = pl.pallas_call(
        body, name=name, grid=(s // tm,), in_specs=[row, row] + [row] * n + [vec] * n, out_specs=[row, row] + [vec] * n,
        out_shape=[jax.ShapeDtypeStruct((s, d), F32), jax.ShapeDtypeStruct((s, d), BF16)] + [jax.ShapeDtypeStruct((1, d), F32)] * n,
        compiler_params=_params(("arbitrary",)),
    )(h, dres, *[b[0] for b in branches], *[b[1] for b in branches])
    return (outs[0], outs[1]), outs[2:]


def _loss_head(h, gain, target):
    s, d = h.shape
    tm = _row_tile(s)

    def body(h_ref, g_ref, t_ref, dh_ref, dhb_ref, dg_ref, loss_ref):
        i = pl.program_id(0)
        xv = h_ref[...]
        r = lax.rsqrt(jnp.mean(xv * xv, axis=-1, keepdims=True) + EPS)
        xhat = xv * r
        err = xhat * g_ref[...] - t_ref[...]
        dy = err * (1.0 / d)
        part = jnp.zeros((1, 128), F32) + 0.5 * jnp.sum(jnp.mean(err * err, axis=-1, keepdims=True))
        dgain = jnp.sum(dy * xhat, axis=0, keepdims=True)

        @pl.when(i == 0)
        def _():
            dg_ref[...] = dgain
            loss_ref[...] = part

        @pl.when(i > 0)
        def _():
            dg_ref[...] += dgain
            loss_ref[...] += part

        dxhat = dy * g_ref[...]
        dh = r * (dxhat - xhat * jnp.mean(dxhat * xhat, axis=-1, keepdims=True))
        dh_ref[...] = dh
        dhb_ref[...] = dh.astype(BF16)

    row = pl.BlockSpec((tm, d), lambda i: (i, 0))
    vec = pl.BlockSpec((1, d), lambda i: (0, 0))
    return pl.pallas_call(
        body, name="loss_head", grid=(s // tm,), in_specs=[row, vec, row],
        out_specs=[row, row, vec, pl.BlockSpec((1, 128), lambda i: (0, 0))],
        out_shape=[jax.ShapeDtypeStruct((s, d), F32), jax.ShapeDtypeStruct((s, d), BF16), jax.ShapeDtypeStruct((1, d), F32),
                   jax.ShapeDtypeStruct((1, 128), F32)],
        compiler_params=_params(("arbitrary",)),
    )(h, gain, target)


def _bdot(a, b, ca, cb):
    return lax.dot_general(a.astype(BF16), b.astype(BF16), (((ca,), (cb,)), ((0,), (0,))), preferred_element_type=F32)


def _chunk_cumsum(xv, reverse=False):
    n = xv.shape[0]
    row = lax.broadcasted_iota(jnp.int32, xv.shape, 0) % HG_CHUNK
    step = 1
    while step < HG_CHUNK:
        if reverse:
            xv = xv + jnp.where(row < HG_CHUNK - step, pltpu.roll(xv, n - step, axis=0), 0.0)
        else:
            xv = xv + jnp.where(row >= step, pltpu.roll(xv, step, axis=0), 0.0)
        step *= 2
    return xv


def _hg_terms(p_ref, lbl_ref):
    pq = p_ref[0].astype(F32)
    pf = p_ref[1].astype(F32)
    lb = _sigmoid(lbl_ref[0:1, :] - lbl_ref[1:2, :])
    sig = _sigmoid(pf)
    fg = lb + (1.0 - lb) * sig
    nc = pq.shape[0] // HG_CHUNK
    chunks = lambda a: a.reshape(nc, HG_CHUNK, HG_EXPAND)
    q = chunks(_silu(pq) * HG_EXPAND ** -0.5)
    k = chunks(1.0 - fg)
    v = chunks(p_ref[2].astype(F32))
    g = chunks(_chunk_cumsum(jnp.log(fg)))
    gm = g[:, HG_CHUNK // 2 - 1:HG_CHUNK // 2, :]
    gl = g[:, HG_CHUNK - 1:HG_CHUNK, :]
    e_mid, e_inv, e_all, e_end = jnp.exp(g - gm), jnp.exp(gm - g), jnp.exp(g), jnp.exp(gl - g)
    terms = dict(q=q, k=k, v=v, qd=q * e_all, qt=q * e_mid, kt=k * e_inv, kd=k * e_end, e_last=jnp.exp(gl),
                 e_mid=e_mid, e_inv=e_inv, e_all=e_all, e_end=e_end)
    return terms, (pq, sig, fg, lb)


def _causal(nc):
    r = lax.broadcasted_iota(jnp.int32, (nc, HG_CHUNK, HG_CHUNK), 1)
    c = lax.broadcasted_iota(jnp.int32, (nc, HG_CHUNK, HG_CHUNK), 2)
    return r >= c


def _hgrn2_fwd(p, lb_logits, out_gain):
    _, s, d = p.shape
    heads = d // HG_EXPAND
    t = _row_tile(s, 2048)
    nc = t // HG_CHUNK

    def body(p_ref, lbl_ref, gain_ref, o_ref, og_ref, st_ref, state, decay):
        @pl.when(pl.program_id(1) == 0)
        def _():
            state[...] = jnp.zeros_like(state)

        tm, _ = _hg_terms(p_ref, lbl_ref)
        decay[...] = tm["e_last"]
        st_ref[...] = _bdot(tm["v"], tm["kd"], 1, 1)

        def chunk(c, carry):
            add = st_ref[c]
            st = state[...]
            st_ref[c] = st
            state[...] = st * decay[c] + add
            return carry

        lax.fori_loop(0, nc, chunk, 0)
        a = jnp.where(_causal(nc), _bdot(tm["qt"], tm["kt"], 2, 2), 0.0)
        ov = (_bdot(tm["qd"], st_ref[...], 2, 2) + _bdot(a, tm["v"], 2, 1)).reshape(t, HG_EXPAND)
        o_ref[...] = ov
        on = ov * lax.rsqrt(jnp.mean(ov * ov, axis=-1, keepdims=True) + EPS) * gain_ref[...]
        og_ref[...] = (on * _silu(p_ref[3].astype(F32))).astype(BF16)

    blk = pl.BlockSpec((t, HG_EXPAND), lambda h, b: (b, h))
    return pl.pallas_call(
        body, name="hgrn2_fwd", grid=(heads, s // t),
        in_specs=[pl.BlockSpec((4, t, HG_EXPAND), lambda h, b: (0, b, h)), pl.BlockSpec((2, HG_EXPAND), lambda h, b: (0, h)),
                  pl.BlockSpec((1, HG_EXPAND), lambda h, b: (0, 0))],
        out_specs=[blk, blk, pl.BlockSpec((None, nc, HG_EXPAND, HG_EXPAND), lambda h, b: (h, b, 0, 0))],
        out_shape=[jax.ShapeDtypeStruct((s, d), F32), jax.ShapeDtypeStruct((s, d), BF16),
                   jax.ShapeDtypeStruct((heads, s // HG_CHUNK, HG_EXPAND, HG_EXPAND), F32)],
        scratch_shapes=[pltpu.VMEM((HG_EXPAND, HG_EXPAND), F32), pltpu.VMEM((nc, 1, HG_EXPAND), F32)],
        compiler_params=_params(("parallel", "arbitrary")),
    )(p, lb_logits, out_gain)


def _hgrn2_bwd(p, lb_logits, out_gain, o, dog, states):
    _, s, d = p.shape
    heads = d // HG_EXPAND
    t = _row_tile(s, 1024)
    nc = t // HG_CHUNK
    nb = s // t

    def body(p_ref, lbl_ref, gain_ref, o_ref, dog_ref, st_ref, dp_ref, dlbl_ref, dgain_ref, dstate, decay, dst_s):
        h, b = pl.program_id(0), pl.program_id(1)

        @pl.when(b == 0)
        def _():
            dstate[...] = jnp.zeros_like(dstate)

        tm, (pq, sig, fg, lb) = _hg_terms(p_ref, lbl_ref)
        pg = p_ref[3].astype(F32)
        ov = o_ref[...]
        r = lax.rsqrt(jnp.mean(ov * ov, axis=-1, keepdims=True) + EPS)
        ohat = ov * r
        dogv = dog_ref[...]
        d_on = dogv * _silu(pg)
        dp_ref[3] = (dogv * ohat * gain_ref[...] * _dsilu(pg)).astype(BF16)
        dgain = jnp.sum(d_on * ohat, axis=0, keepdims=True)

        @pl.when((h == 0) & (b == 0))
        def _():
            dgain_ref[...] = dgain

        @pl.when((h > 0) | (b > 0))
        def _():
            dgain_ref[...] += dgain

        dohat = d_on * gain_ref[...]
        do = (r * (dohat - ohat * jnp.mean(dohat * ohat, axis=-1, keepdims=True))).reshape(nc, HG_CHUNK, HG_EXPAND)

        decay[...] = tm["e_last"]
        dst_s[...] = _bdot(do, tm["qd"], 1, 1)

        def chunk(i, carry):
            c = nc - 1 - i
            add = dst_s[c]
            dst = dstate[...]
            dst_s[c] = dst
            dstate[...] = dst * decay[c] + add
            return carry

        lax.fori_loop(0, nc, chunk, 0)
        st, dst = st_ref[...], dst_s[...]
        causal = _causal(nc)
        a = jnp.where(causal, _bdot(tm["qt"], tm["kt"], 2, 2), 0.0)
        da = jnp.where(causal, _bdot(do, tm["v"], 2, 2), 0.0)
        dqt = _bdot(da, tm["kt"], 2, 1)
        dkt = _bdot(da, tm["qt"], 1, 1)
        dqd = _bdot(do, st, 2, 1)
        dkd = _bdot(tm["v"], dst, 2, 1)
        dv = _bdot(a, do, 1, 1) + _bdot(tm["kd"], dst, 2, 2)
        dq = dqt * tm["e_mid"] + dqd * tm["e_all"]
        dk = dkt * tm["e_inv"] + dkd * tm["e_end"]
        dg = dqt * tm["qt"] - dkt * tm["kt"] + dqd * tm["qd"] - dkd * tm["kd"]
        dgl = jnp.sum(dkd * tm["kd"], axis=1, keepdims=True) + tm["e_last"] * jnp.sum(dst * st, axis=1, keepdims=True)
        last_row = lax.broadcasted_iota(jnp.int32, (nc, HG_CHUNK, HG_EXPAND), 1) == HG_CHUNK - 1
        flat = lambda a3: a3.reshape(t, HG_EXPAND)
        dlf = _chunk_cumsum(flat(dg + jnp.where(last_row, dgl, 0.0)), reverse=True)
        dfg = dlf / fg - flat(dk)
        dlb = jnp.sum(dfg * (1.0 - sig), axis=0, keepdims=True)
        dl0 = dlb * lb * (1.0 - lb)
        dlbl = jnp.concatenate([dl0, -dl0], axis=0)

        @pl.when(b == 0)
        def _():
            dlbl_ref[...] = dlbl

        @pl.when(b > 0)
        def _():
            dlbl_ref[...] += dlbl

        dp_ref[0] = (flat(dq) * HG_EXPAND ** -0.5 * _dsilu(pq)).astype(BF16)
        dp_ref[1] = (dfg * (1.0 - lb) * sig * (1.0 - sig)).astype(BF16)
        dp_ref[2] = flat(dv).astype(BF16)

    blk = pl.BlockSpec((t, HG_EXPAND), lambda h, b: (nb - 1 - b, h))
    pblk = pl.BlockSpec((4, t, HG_EXPAND), lambda h, b: (0, nb - 1 - b, h))
    return pl.pallas_call(
        body, name="hgrn2_bwd", grid=(heads, nb),
        in_specs=[pblk, pl.BlockSpec((2, HG_EXPAND), lambda h, b: (0, h)), pl.BlockSpec((1, HG_EXPAND), lambda h, b: (0, 0)),
                  blk, blk, pl.BlockSpec((None, nc, HG_EXPAND, HG_EXPAND), lambda h, b: (h, nb - 1 - b, 0, 0))],
        out_specs=[pblk, pl.BlockSpec((2, HG_EXPAND), lambda h, b: (0, h)), pl.BlockSpec((1, HG_EXPAND), lambda h, b: (0, 0))],
        out_shape=[jax.ShapeDtypeStruct((4, s, d), BF16), jax.ShapeDtypeStruct((2, d), F32), jax.ShapeDtypeStruct((1, HG_EXPAND), F32)],
        scratch_shapes=[pltpu.VMEM((HG_EXPAND, HG_EXPAND), F32), pltpu.VMEM((nc, 1, HG_EXPAND), F32),
                        pltpu.VMEM((nc, HG_EXPAND, HG_EXPAND), F32)],
        compiler_params=_params(("arbitrary", "arbitrary")),
    )(p, lb_logits, out_gain, o, dog, states)


HALO = 8
FFN_FWD_ROWS = 512
FFN_BWD_ROWS = 256


def _shift_down(xv, n):
    return pltpu.roll(xv, n, axis=0)


def _shift_up(xv, n):
    return pltpu.roll(xv, xv.shape[0] - n, axis=0)


def _ffn_hidden_down(name, u, conv_w, conv_b, w_down, h):
    _, nj, s, fb = u.shape
    d = w_down.shape[2]
    tm = _row_tile(s, FFN_FWD_ROWS)
    per = tm // HALO

    def body(gate_ref, prev_ref, val_ref, w_ref, b_ref, wd_ref, h_ref, hid_ref, o_ref):
        i = pl.program_id(0)
        total = h_ref[...]
        for j in range(nj):
            prev = jnp.where(i > 0, prev_ref[j].astype(F32), 0.0)
            ext = jnp.concatenate([prev, gate_ref[j].astype(F32)], axis=0)
            conv = b_ref[j] + w_ref[j, 2:3, :] * ext[HALO:]
            conv = conv + w_ref[j, 1:2, :] * _shift_down(ext, 1)[HALO:]
            conv = conv + w_ref[j, 0:1, :] * _shift_down(ext, 2)[HALO:]
            hidden = (_silu(conv) * val_ref[j].astype(F32)).astype(BF16)
            hid_ref[j] = hidden
            total = total + _dot(hidden, wd_ref[j])
        o_ref[...] = total

    row = pl.BlockSpec((tm, d), lambda i: (i, 0))
    return pl.pallas_call(
        body, name=name, grid=(s // tm,),
        in_specs=[pl.BlockSpec((None, nj, tm, fb), lambda i: (0, 0, i, 0)),
                  pl.BlockSpec((None, nj, HALO, fb), lambda i: (0, 0, jnp.maximum(i * per - 1, 0), 0)),
                  pl.BlockSpec((None, nj, tm, fb), lambda i: (1, 0, i, 0)),
                  pl.BlockSpec((nj, CONV_WIDTH, fb), lambda i: (0, 0, 0)), pl.BlockSpec((nj, 1, fb), lambda i: (0, 0, 0)),
                  pl.BlockSpec((nj, fb, d), lambda i: (0, 0, 0)), row],
        out_specs=[pl.BlockSpec((nj, tm, fb), lambda i: (0, i, 0)), row],
        out_shape=[jax.ShapeDtypeStruct((nj, s, fb), BF16), jax.ShapeDtypeStruct((s, d), F32)],
        compiler_params=_params(("parallel",)),
    )(u, u, u, conv_w, conv_b, w_down, h)


def _ffn_hidden_up_bwd(name, u, dh, conv_w, conv_b, w_up, h, gain, dres):
    _, nj, s, fb = u.shape
    d = w_up.shape[2]
    tm = _row_tile(s, FFN_BWD_ROWS)
    per = tm // HALO
    nblk = s // HALO
    ni = s // tm

    def body(gate_ref, gprev_ref, gnext_ref, val_ref, vnext_ref, dh_ref, dhnext_ref, w_ref, b_ref, wu_ref, h_ref, gain_ref, dres_ref,
             du_ref, dw_ref, db_ref, dx_ref, dxb_ref, dgain_ref):
        i = pl.program_id(0)
        has_next = i < ni - 1
        total = None
        for j in range(nj):
            gprev = jnp.where(i > 0, gprev_ref[j].astype(F32), 0.0)
            gext = jnp.concatenate([gprev, gate_ref[j].astype(F32), gnext_ref[j].astype(F32)], axis=0)
            vext = jnp.concatenate([val_ref[j].astype(F32), vnext_ref[j].astype(F32)], axis=0)
            dhext = jnp.concatenate([dh_ref[j].astype(F32), jnp.where(has_next, dhnext_ref[j].astype(F32), 0.0)], axis=0)
            g0 = gext[HALO:]
            g1 = _shift_down(gext, 1)[HALO:]
            g2 = _shift_down(gext, 2)[HALO:]
            conv = b_ref[j] + w_ref[j, 2:3, :] * g0 + w_ref[j, 1:2, :] * g1 + w_ref[j, 0:1, :] * g2
            act, dact = _silu_and_grad(conv)
            dconv = dhext * vext * dact
            dgate = (w_ref[j, 2:3, :] * dconv + w_ref[j, 1:2, :] * _shift_up(dconv, 1) + w_ref[j, 0:1, :] * _shift_up(dconv, 2))[:tm].astype(BF16)
            dval = (dhext * act)[:tm].astype(BF16)
            du_ref[0, j] = dgate
            du_ref[1, j] = dval
            part = _dot(dgate, wu_ref[j]) + _dot(dval, wu_ref[nj + j])
            total = part if total is None else total + part
            own = dconv[:tm]
            dw = jnp.concatenate([jnp.sum(own * g2[:tm], axis=0, keepdims=True), jnp.sum(own * g1[:tm], axis=0, keepdims=True),
                                  jnp.sum(own * g0[:tm], axis=0, keepdims=True)], axis=0)
            db = jnp.sum(own, axis=0, keepdims=True)

            @pl.when(i == 0)
            def _():
                dw_ref[j] = dw
                db_ref[j] = db

            @pl.when(i > 0)
            def _():
                dw_ref[j] += dw
                db_ref[j] += db

        xv = h_ref[...]
        r = lax.rsqrt(jnp.mean(xv * xv, axis=-1, keepdims=True) + EPS)
        xhat = xv * r
        dgain = jnp.sum(total * xhat, axis=0, keepdims=True)

        @pl.when(i == 0)
        def _():
            dgain_ref[...] = dgain

        @pl.when(i > 0)
        def _():
            dgain_ref[...] += dgain

        dxhat = total * gain_ref[...]
        dx = dres_ref[...] + r * (dxhat - xhat * jnp.mean(dxhat * xhat, axis=-1, keepdims=True))
        dx_ref[...] = dx
        dxb_ref[...] = dx.astype(BF16)

    def tile(part):
        return pl.BlockSpec((None, nj, tm, fb), lambda i: (part, 0, i, 0))

    def after(part):
        return pl.BlockSpec((None, nj, HALO, fb), lambda i: (part, 0, jnp.minimum((i + 1) * per, nblk - 1), 0))

    row = pl.BlockSpec((tm, d), lambda i: (i, 0))
    return pl.pallas_call(
        body, name=name, grid=(ni,),
        in_specs=[tile(0), pl.BlockSpec((None, nj, HALO, fb), lambda i: (0, 0, jnp.maximum(i * per - 1, 0), 0)), after(0),
                  tile(1), after(1),
                  pl.BlockSpec((nj, tm, fb), lambda i: (0, i, 0)),
                  pl.BlockSpec((nj, HALO, fb), lambda i: (0, jnp.minimum((i + 1) * per, nblk - 1), 0)),
                  pl.BlockSpec((nj, CONV_WIDTH, fb), lambda i: (0, 0, 0)), pl.BlockSpec((nj, 1, fb), lambda i: (0, 0, 0)),
                  pl.BlockSpec((2 * nj, fb, d), lambda i: (0, 0, 0)), row, pl.BlockSpec((1, d), lambda i: (0, 0)), row],
        out_specs=[pl.BlockSpec((2, nj, tm, fb), lambda i: (0, 0, i, 0)),
                   pl.BlockSpec((nj, CONV_WIDTH, fb), lambda i: (0, 0, 0)), pl.BlockSpec((nj, 1, fb), lambda i: (0, 0, 0)),
                   row, row, pl.BlockSpec((1, d), lambda i: (0, 0))],
        out_shape=[jax.ShapeDtypeStruct((2, nj, s, fb), BF16), jax.ShapeDtypeStruct((nj, CONV_WIDTH, fb), F32),
                   jax.ShapeDtypeStruct((nj, 1, fb), F32), jax.ShapeDtypeStruct((s, d), F32), jax.ShapeDtypeStruct((s, d), BF16),
                   jax.ShapeDtypeStruct((1, d), F32)],
        compiler_params=_params(("arbitrary",)),
    )(u, u, u, u, u, dh, dh, conv_w, conv_b, w_up, h, gain, dres)


ATT_TILE = 512


def _stack_heads(ref, rows, first_head, count):
    hd = ATT_HEAD_DIM
    return jnp.concatenate([ref[rows, (first_head + j) * hd:(first_head + j + 1) * hd] for j in range(count)], axis=0)


def _unstack_heads(stacked, ref, rows, first_head, count):
    hd = ATT_HEAD_DIM
    for pair in range(count // 2):
        both = [stacked[(2 * pair + j) * WINDOW:(2 * pair + j + 1) * WINDOW, :] for j in range(2)]
        ref[rows, (first_head + 2 * pair) * hd:(first_head + 2 * pair + 2) * hd] = jnp.concatenate(both, axis=1).astype(ref.dtype)


def _attn_probs_t(kb, qs, sink_ref, first_head, count, first, n_heads):
    lanes = count * WINDOW
    ik = lax.broadcasted_iota(jnp.int32, (2 * WINDOW, lanes), 0)
    iq = lax.broadcasted_iota(jnp.int32, (2 * WINDOW, lanes), 1) % WINDOW
    dist = iq + WINDOW - ik
    valid = (dist >= 0) & (dist < WINDOW) & (ik >= jnp.where(first, WINDOW, 0))
    per_head = lambda values: jnp.concatenate([jnp.zeros((1, WINDOW), F32) + v for v in values], axis=1)
    slope = per_head([2.0 ** (-8.0 * (first_head + j + 1) / n_heads) for j in range(count)])
    sink = per_head([sink_ref[0, first_head + j] for j in range(count)])
    sc = jnp.where(valid, _dot(kb, qs, NT) * ATT_HEAD_DIM ** -0.5 - slope * dist.astype(F32), NEG)
    m = jnp.maximum(jnp.max(sc, axis=0, keepdims=True), sink)
    e = jnp.exp(sc - m)
    es = jnp.exp(sink - m)
    inv = 1.0 / (jnp.sum(e, axis=0, keepdims=True) + es)
    return e * inv, es * inv


def _attn_specs(s, d, kvd, tq):
    per = tq // WINDOW
    return [pl.BlockSpec((tq, d), lambda i: (i, 0)), pl.BlockSpec((tq, kvd), lambda i: (i, 0)),
            pl.BlockSpec((WINDOW, kvd), lambda i: (jnp.maximum(i * per - 1, 0), 0))]


def _attn_fwd(q, kv, sinks):
    s, d = q.shape
    kvd = kv.shape[1]
    half = kvd // 2
    hd = ATT_HEAD_DIM
    nq = d // hd
    group = nq // ATT_KV_HEADS
    tq = min(s, ATT_TILE)
    per = tq // WINDOW

    def body(q_ref, kvc_ref, kvp_ref, sink_ref, o_ref, band):
        i = pl.program_id(0)
        band[0:WINDOW, :] = kvp_ref[...]
        band[WINDOW:, :] = kvc_ref[...]

        def block(b, carry):
            rows = pl.ds(pl.multiple_of(b * WINDOW, WINDOW), WINDOW)
            keys = pl.ds(pl.multiple_of(b * WINDOW, WINDOW), 2 * WINDOW)
            first = (i * per + b) == 0
            for g in range(ATT_KV_HEADS):
                p, _ = _attn_probs_t(band[keys, g * hd:(g + 1) * hd], _stack_heads(q_ref, rows, g * group, group), sink_ref,
                                     g * group, group, first, nq)
                out_t = _dot(band[keys, half + g * hd:half + (g + 1) * hd], p, TN)
                _unstack_heads(out_t.T, o_ref, rows, g * group, group)
            return carry

        lax.fori_loop(0, per, block, 0)

    return pl.pallas_call(
        body, name="attn_fwd", grid=(s // tq,),
        in_specs=_attn_specs(s, d, kvd, tq) + [pl.BlockSpec(memory_space=pltpu.SMEM)],
        out_specs=pl.BlockSpec((tq, d), lambda i: (i, 0)), out_shape=jax.ShapeDtypeStruct((s, d), BF16),
        scratch_shapes=[pltpu.VMEM((tq + WINDOW, kvd), BF16)], compiler_params=_params(("parallel",)),
    )(q, kv, kv, sinks)


def _attn_bwd(q, kv, o, do, sinks):
    s, d = q.shape
    kvd = kv.shape[1]
    half = kvd // 2
    hd = ATT_HEAD_DIM
    nq = d // hd
    group = nq // ATT_KV_HEADS
    tq = min(s, ATT_TILE)
    per = tq // WINDOW
    nt = s // tq

    def body(q_ref, kvc_ref, kvp_ref, o_ref, do_ref, sink_ref, dq_ref, dkvc_ref, dkvp_ref, ds_ref, band, dband):
        i = pl.program_id(0)
        band[0:WINDOW, :] = kvp_ref[...]
        band[WINDOW:, :] = kvc_ref[...]
        dband[...] = jnp.zeros_like(dband)
        ds_ref[...] = jnp.zeros_like(ds_ref)

        def block(b, carry):
            rows = pl.ds(pl.multiple_of(b * WINDOW, WINDOW), WINDOW)
            keys = pl.ds(pl.multiple_of(b * WINDOW, WINDOW), 2 * WINDOW)
            first = (i * per + b) == 0
            dks, dvs = [], []
            for g in range(ATT_KV_HEADS):
                kb = band[keys, g * hd:(g + 1) * hd]
                vb = band[keys, half + g * hd:half + (g + 1) * hd]
                qs = _stack_heads(q_ref, rows, g * group, group)
                dos = _stack_heads(do_ref, rows, g * group, group)
                p, ps = _attn_probs_t(kb, qs, sink_ref, g * group, group, first, nq)
                prod = dos.astype(F32) * _stack_heads(o_ref, rows, g * group, group).astype(F32)
                dsum = lax.dot_general(jnp.ones((8, hd), F32), prod, NT, precision=lax.Precision.HIGHEST,
                                       preferred_element_type=F32)[0:1, :]
                dsc = p * (_dot(vb, dos, NT) - dsum) * ATT_HEAD_DIM ** -0.5
                dvs.append(_dot(p, dos))
                dks.append(_dot(dsc, qs))
                _unstack_heads(_dot(kb, dsc, TN).T, dq_ref, rows, g * group, group)
                gone = ps * dsum
                for j in range(group):
                    ds_ref[g * group + j:g * group + j + 1, :] += jnp.zeros((1, 128), F32) - jnp.sum(gone[:, j * WINDOW:(j + 1) * WINDOW])
            dband[keys, 0:half] += jnp.concatenate(dks, axis=1)
            dband[keys, half:] += jnp.concatenate(dvs, axis=1)
            return carry

        lax.fori_loop(0, per, block, 0)
        dkvp_ref[...] = dband[0:WINDOW, :]
        dkvc_ref[...] = dband[WINDOW:, :]

    big = pl.BlockSpec((tq, d), lambda i: (i, 0))
    return pl.pallas_call(
        body, name="attn_bwd", grid=(nt,),
        in_specs=_attn_specs(s, d, kvd, tq) + [big, big, pl.BlockSpec(memory_space=pltpu.SMEM)],
        out_specs=[big, pl.BlockSpec((tq, kvd), lambda i: (i, 0)), pl.BlockSpec((None, WINDOW, kvd), lambda i: (i, 0, 0)),
                   pl.BlockSpec((None, nq, 128), lambda i: (i, 0, 0))],
        out_shape=[jax.ShapeDtypeStruct((s, d), BF16), jax.ShapeDtypeStruct((s, kvd), F32), jax.ShapeDtypeStruct((nt, WINDOW, kvd), F32),
                   jax.ShapeDtypeStruct((nt, nq, 128), F32)],
        scratch_shapes=[pltpu.VMEM((tq + WINDOW, kvd), BF16), pltpu.VMEM((tq + WINDOW, kvd), F32)],
        compiler_params=_params(("parallel",)),
    )(q, kv, kv, o, do, sinks)


HBM_SPEC = pl.BlockSpec(memory_space=pltpu.HBM)
VMEM_SPEC = pl.BlockSpec(memory_space=pltpu.VMEM)


def _place():
    return lax.axis_index("x"), lax.axis_index("y"), lax.axis_index("c")


def _flip(pos, r):
    return tuple(1 - p if (r >> (2 - a)) & 1 else p for a, p in enumerate(pos))


def _index(pos):
    return 4 * pos[0] + 2 * pos[1] + pos[2]


def _all_gather(name, shards, spec):
    n = len(shards)

    def body(*refs):
        x_refs, o_refs = refs[:n], refs[n:2 * n]
        send_sems, recv_sems, local_sems = refs[2 * n:]
        me = _place()
        sibling = _flip(me, 1)
        far = [_flip(me, r) for r in (4, 2, 6)]

        def copy(t, sem, block, to, src=None):
            rows = o_refs[t].at[_index(block)]
            return pltpu.make_async_remote_copy(
                src_ref=rows if src is None else src, dst_ref=rows, send_sem=send_sems.at[t, sem], recv_sem=recv_sems.at[t, sem],
                device_id=to, device_id_type=MESH)

        own = [pltpu.make_async_copy(x_refs[t], o_refs[t].at[_index(me)], local_sems.at[t]) for t in range(n)]
        for cp in own:
            cp.start()
        first = []
        for t in range(n):
            first.append(copy(t, 0, me, sibling, src=x_refs[t]))
            first += [copy(t, 1 + j, me, peer, src=x_refs[t]) for j, peer in enumerate(far)]
        for cp in first:
            cp.start()
        passed = []
        for j, peer in enumerate(far):
            for t in range(n):
                copy(t, 1 + j, peer, me).wait_recv()
                cp = copy(t, 4 + j, peer, sibling)
                cp.start()
                passed.append(cp)
        for t in range(n):
            copy(t, 0, sibling, me).wait_recv()
            for j, peer in enumerate(far):
                copy(t, 4 + j, _flip(peer, 1), me).wait_recv()
        for cp in first + passed:
            cp.wait_send()
        for cp in own:
            cp.wait()

    return pl.pallas_call(
        body, name=name, in_specs=[spec] * n, out_specs=[spec] * n,
        out_shape=[jax.ShapeDtypeStruct((N_DEV,) + sh.shape, sh.dtype) for sh in shards],
        scratch_shapes=[pltpu.SemaphoreType.DMA((n, 7)), pltpu.SemaphoreType.DMA((n, 7)), pltpu.SemaphoreType.DMA((n,))],
    )(*shards)


SEM_SPEC = pl.BlockSpec(memory_space=pltpu.SEMAPHORE)
ANY_SPEC = pl.BlockSpec(memory_space=pl.ANY)


def _landing(own, mine):
    return lax.dynamic_update_slice(lax.empty((N_DEV,) + own.shape, own.dtype), own[None], (mine,) + (0,) * own.ndim)


def _pinned(a, token):
    return a + token[0:1, 0:1].astype(a.dtype)


def _peer_copies(src_refs, land_refs, send_sems, recv_sems, scatter, arrivals):
    me = _place()
    mine = _index(me)
    copies = []
    for t, (src, land) in enumerate(zip(src_refs, land_refs)):
        for r in range(1, N_DEV):
            peer = _flip(me, r)
            theirs = _index(peer)
            sem = t * (N_DEV - 1) + r - 1
            copies.append(pltpu.make_async_remote_copy(
                src_ref=src.at[theirs] if scatter else src, dst_ref=land.at[theirs if arrivals else mine],
                send_sem=send_sems.at[sem], recv_sem=recv_sems.at[sem], device_id=peer, device_id_type=MESH))
    return copies


def _send_start(name, sources, lands, scatter, after=None):
    n = len(sources)
    extra = 0 if after is None else 1

    def body(*refs):
        outs = refs[2 * n + extra:]
        for out in _peer_copies(refs[:n], refs[n:2 * n], outs[0], outs[1], scatter, False):
            out.start()
        outs[-1][...] = jnp.zeros_like(outs[-1])

    outs = pl.pallas_call(
        body, name=name, in_specs=[HBM_SPEC] * (2 * n) + [ANY_SPEC] * extra,
        out_specs=[SEM_SPEC, SEM_SPEC] + [HBM_SPEC] * (2 * n) + [VMEM_SPEC],
        out_shape=[pltpu.SemaphoreType.DMA((n * (N_DEV - 1),)), pltpu.SemaphoreType.DMA((n * (N_DEV - 1),))]
        + [pltpu.HBM(a.shape, a.dtype) for a in list(sources) + list(lands)] + [jax.ShapeDtypeStruct((8, 128), F32)],
        input_output_aliases={i: 2 + i for i in range(2 * n)},
        compiler_params=pltpu.CompilerParams(has_side_effects=pltpu.SideEffectType.DATAFLOW_SIDE_EFFECTING),
    )(*[pltpu.with_memory_space_constraint(a, pltpu.HBM) for a in list(sources) + list(lands)], *([] if after is None else [after]))
    return outs[0], outs[1], outs[2:2 + n], outs[2 + n:2 + 2 * n], outs[-1]


def _send_wait(name, started, after, scatter):
    send_sems, recv_sems, sources, lands, _ = started
    n = len(sources)

    def body(*refs):
        for out in _peer_copies(refs[:n], refs[n:2 * n], refs[2 * n], refs[2 * n + 1], scatter, False):
            out.wait_send()
        for arrival in _peer_copies(refs[:n], refs[n:2 * n], refs[2 * n], refs[2 * n + 1], scatter, True):
            arrival.wait_recv()

    outs = pl.pallas_call(
        body, name=name, in_specs=[HBM_SPEC] * (2 * n) + [SEM_SPEC, SEM_SPEC, ANY_SPEC], out_specs=[HBM_SPEC] * (2 * n),
        out_shape=[pltpu.HBM(a.shape, a.dtype) for a in list(sources) + list(lands)],
        input_output_aliases={i: i for i in range(2 * n)},
        compiler_params=pltpu.CompilerParams(has_side_effects=pltpu.SideEffectType.DATAFLOW_SIDE_EFFECTING),
    )(*sources, *lands, send_sems, recv_sems, after)
    return outs[n:]


def _pack_rows(parts):
    offsets, row = [], 0
    for part in parts:
        offsets.append(row)
        row += part.shape[0]
    return offsets, -(-row // 8) * 8, -(-max(part.shape[1] for part in parts) // 128) * 128


def _pack(name, parts):
    offsets, rows, width = _pack_rows(parts)

    def body(*refs):
        o_ref = refs[-1]
        o_ref[...] = jnp.zeros_like(o_ref)
        for off, ref in zip(offsets, refs[:-1]):
            o_ref[off:off + ref.shape[0], 0:ref.shape[1]] = ref[...]

    return pl.pallas_call(body, name=name, in_specs=[VMEM_SPEC] * len(parts), out_specs=VMEM_SPEC,
                          out_shape=jax.ShapeDtypeStruct((rows, width), F32))(*parts)


def _adamw_math(w, g, m, v):
    m = ADAM_B1 * m + (1.0 - ADAM_B1) * g
    v = ADAM_B2 * v + (1.0 - ADAM_B2) * (g * g)
    m_hat = m * (1.0 / (1.0 - ADAM_B1 ** ADAM_STEP))
    denom = jnp.sqrt(v * (1.0 / (1.0 - ADAM_B2 ** ADAM_STEP))) + ADAM_EPS
    inv = pl.reciprocal(denom, approx=True)
    inv = inv * (2.0 - denom * inv)
    return -ADAM_LR * (m_hat * inv + ADAM_WD * w), m, v


def _adamw_step(w_ref, m_ref, v_ref, p_ref, g_ref, d_ref, nm_ref, nv_ref):
    g = p_ref[0].astype(F32)
    for dev in range(1, N_DEV):
        g = g + p_ref[dev].astype(F32)
    g_ref[...] = g
    d_ref[...], nm_ref[...], nv_ref[...] = _adamw_math(w_ref[...], g, m_ref[...], v_ref[...])


def _adamw_rows(rows):
    return max(t for t in range(8, min(rows, 256) + 1, 8) if rows % t == 0)


def _adamw_shard(name, w, m, v, partials):
    rows, cols = w.shape
    tr = _adamw_rows(rows)
    blk = pl.BlockSpec((tr, cols), lambda i: (i, 0))
    return pl.pallas_call(
        _adamw_step_fn(), name=name, grid=(rows // tr,), in_specs=[blk, blk, blk, pl.BlockSpec((N_DEV, tr, cols), lambda i: (0, i, 0))],
        out_specs=[blk] * 4, out_shape=[jax.ShapeDtypeStruct((rows, cols), F32)] * 4, compiler_params=_params(("parallel",)),
    )(w, m, v, partials)


def _adamw_step_fn():
    return functools.partial(_adamw_step)


def _adamw_layers(name, w, m, v, partials):
    layers, rows, cols = w.shape
    tr = _adamw_rows(rows)
    last = rows // tr - 1

    def body(w_ref, m_ref, v_ref, *rest):
        for layer in range(layers):
            @pl.when(pl.program_id(0) == layer)
            def _():
                _adamw_step(w_ref, m_ref, v_ref, rest[layer], *rest[layers:])

    blk = pl.BlockSpec((None, tr, cols), lambda l, i: (l, i, 0))
    part = lambda layer: pl.BlockSpec((N_DEV, tr, cols), lambda l, i: (0, jnp.where(l == layer, i, jnp.where(l < layer, 0, last)), 0))
    return pl.pallas_call(
        body, name=name, grid=(layers, rows // tr), in_specs=[blk, blk, blk] + [part(layer) for layer in range(layers)],
        out_specs=[blk] * 4, out_shape=[jax.ShapeDtypeStruct(w.shape, F32)] * 4, compiler_params=_params(("arbitrary", "arbitrary")),
    )(w, m, v, *partials)


def _adamw_small(gathered, places, entries):
    n = len(entries)
    np_ = len(gathered)

    def body(*refs):
        pack_refs = refs[:np_]
        refs = refs[np_ - 1:]
        w_refs, m_refs, v_refs = refs[1:1 + n], refs[1 + n:1 + 2 * n], refs[1 + 2 * n:1 + 3 * n]
        outs = refs[1 + 3 * n:]
        totals = []
        for pack_ref in pack_refs:
            acc = pack_ref[0]
            for dev in range(1, N_DEV):
                acc = acc + pack_ref[dev]
            totals.append(acc)
        mine = _index(_place())
        for e in range(n):
            rows, cols = w_refs[e].shape
            total, off = totals[places[e][0]], places[e][1]
            if entries[e][3]:
                g = jnp.zeros((rows, cols), F32)
                for dev in range(N_DEV):
                    g = g + jnp.where(mine == dev, total[off + dev * rows:off + (dev + 1) * rows, 0:cols], 0.0)
            else:
                g = total[off:off + rows, 0:cols]
            outs[4 * e][...] = g
            outs[4 * e + 1][...], outs[4 * e + 2][...], outs[4 * e + 3][...] = _adamw_math(w_refs[e][...], g, m_refs[e][...], v_refs[e][...])
        outs[4 * n][...] = totals[places[n][0]][places[n][1]:places[n][1] + 1, 0:128]

    shapes = []
    for w, _, _, _ in entries:
        shapes += [jax.ShapeDtypeStruct(w.shape, F32)] * 4
    shapes.append(jax.ShapeDtypeStruct((1, 128), F32))
    return pl.pallas_call(
        body, name="adamw_small", in_specs=[VMEM_SPEC] * (np_ + 3 * n), out_specs=[VMEM_SPEC] * len(shapes), out_shape=shapes,
        compiler_params=pltpu.CompilerParams(vmem_limit_bytes=VMEM_LIMIT),
    )(*gathered, *[e[0] for e in entries], *[e[1] for e in entries], *[e[2] for e in entries])


def _ffn_forward(tag, h, gain, w_up, w_down, conv_w, conv_b, after_up=None):
    s, d = h.shape
    fb = w_up.shape[1]
    tm = _row_tile(s, 2 * MM_ROWS)
    a, = _rmsnorm_cast(f"ffn_norm_{tag}", h, [gain])
    u = _matmul(
        f"ffn_up_{tag}", a, w_up, dims=NT, grid=(s // tm, N_DEV, 1),
        a_spec=pl.BlockSpec((tm, d), lambda i, j, k: (i, 0)),
        b_spec=pl.BlockSpec((None, fb, d), lambda i, j, k: (j, 0, 0)),
        o_spec=pl.BlockSpec((None, None, tm, fb), lambda i, j, k: (j // 4, j % 4, i, 0)),
        out_shape=jax.ShapeDtypeStruct((2, 4, s, fb), BF16))
    if after_up is not None:
        conv_b = _pinned(conv_b, after_up(u))
    hidden, out = _ffn_hidden_down(f"ffn_hidden_down_{tag}", u, conv_w, conv_b, w_down, h)
    return out, (a, u, hidden)


def _ffn_backward(tag, h, gain, w_up, w_down, conv_w, conv_b, saved, dout):
    a, u, hidden = saved
    dout, dout_bf = dout
    s, d = h.shape
    fb = w_up.shape[1]
    tm = _row_tile(s, MM_ROWS)
    dhidden = _matmul(
        f"ffn_down_bwd_{tag}", dout_bf, w_down, dims=NT, grid=(s // tm, 4, 1),
        a_spec=pl.BlockSpec((tm, d), lambda i, j, k: (i, 0)),
        b_spec=pl.BlockSpec((None, fb, d), lambda i, j, k: (j, 0, 0)),
        o_spec=pl.BlockSpec((None, tm, fb), lambda i, j, k: (j, i, 0)),
        out_shape=jax.ShapeDtypeStruct((4, s, fb), BF16))
    dw_down = _matmul(
        f"ffn_down_grad_{tag}", hidden, dout_bf, dims=TN, grid=(4, 1, 1),
        a_spec=pl.BlockSpec((None, s, fb), lambda i, j, k: (i, 0, 0)),
        b_spec=pl.BlockSpec((s, d), lambda i, j, k: (0, 0)),
        o_spec=pl.BlockSpec((None, fb, d), lambda i, j, k: (i, 0, 0)),
        out_shape=jax.ShapeDtypeStruct((4, fb, d), BF16))
    du, dconv_w, dconv_b, dh, dh_bf, dgain = _ffn_hidden_up_bwd(f"ffn_hidden_up_bwd_{tag}", u, dhidden, conv_w, conv_b, w_up, h, gain, dout)
    dw_up = _matmul(
        f"ffn_up_grad_{tag}", du, a, dims=TN, grid=(N_DEV, 1, 1),
        a_spec=pl.BlockSpec((None, None, s, fb), lambda i, j, k: (i // 4, i % 4, 0, 0)),
        b_spec=pl.BlockSpec((s, d), lambda i, j, k: (0, 0)),
        o_spec=pl.BlockSpec((None, fb, d), lambda i, j, k: (i, 0, 0)),
        out_shape=jax.ShapeDtypeStruct((N_DEV, fb, d), BF16))
    return (dh, dh_bf), dgain, dw_up, dw_down, dconv_w, dconv_b


def kernel(x, hg_norm, hg_w_in, hg_lb_logits, hg_out_norm, hg_w_out, kv_norm, w_kv, attn_norm, attn_w_q, attn_sinks, attn_w_o, ffn_norm, ffn_w_up, ffn_conv_w, ffn_conv_b, ffn_w_down, final_norm, loss_target, m_hg_norm, m_hg_w_in, m_hg_lb_logits, m_hg_out_norm, m_hg_w_out, m_kv_norm, m_w_kv, m_attn_norm, m_attn_w_q, m_attn_sinks, m_attn_w_o, m_ffn_norm, m_ffn_w_up, m_ffn_conv_w, m_ffn_conv_b, m_ffn_w_down, m_final_norm, v_hg_norm, v_hg_w_in, v_hg_lb_logits, v_hg_out_norm, v_hg_w_out, v_kv_norm, v_w_kv, v_attn_norm, v_attn_w_q, v_attn_sinks, v_attn_w_o, v_ffn_norm, v_ffn_w_up, v_ffn_conv_w, v_ffn_conv_b, v_ffn_w_down, v_final_norm):
    _, s, d = x.shape
    x0, target = x[0], loss_target[0]
    half = hg_w_in.shape[2]
    fs = ffn_conv_w.shape[2]
    fb = 2 * fs
    kvd = w_kv.shape[1]
    nq = d // ATT_HEAD_DIM
    tm = _row_tile(s, MM_ROWS)

    mine = _index(_place())
    gather = lambda tag, shards, after: _send_start("gather_start_" + tag, shards, [_landing(a, mine) for a in shards], False, after)
    w_in, g_hgn, g_lbl, w_out = _all_gather("gather_hg", [hg_w_in[0].astype(BF16), hg_norm, hg_lb_logits, hg_w_out[0].astype(BF16)], HBM_SPEC)
    w_out = w_out.reshape(d, d)
    up_t = lambda a: jnp.swapaxes(a, -1, -2)
    coming_ffn0 = gather("ffn0", [up_t(ffn_w_up[0]).astype(BF16), ffn_conv_w, ffn_w_down[0].astype(BF16)], g_hgn)
    hgn = _pinned(g_hgn.reshape(1, d), coming_ffn0[4])
    lbl = g_lbl.transpose(1, 0, 2).reshape(2, d)
    conv_b = [ffn_conv_b[layer].reshape(4, 1, fb) for layer in range(2)]
    gains = [ffn_norm[0:1], ffn_norm[1:2]]
    kvn, fin = kv_norm.reshape(1, d), final_norm.reshape(1, d)

    a0, = _rmsnorm_cast("hg_norm", x0, [hgn])
    t2 = _row_tile(s, 2 * MM_ROWS)
    p = _matmul(
        "hg_in", a0, w_in, dims=NN, grid=(s // t2, N_DEV, 1),
        a_spec=pl.BlockSpec((t2, d), lambda i, j, k: (i, 0)),
        b_spec=pl.BlockSpec((None, d, half), lambda i, j, k: (j, 0, 0)),
        o_spec=pl.BlockSpec((None, t2, half), lambda i, j, k: (j // 2, i, j % 2)),
        out_shape=jax.ShapeDtypeStruct((4, s, d), BF16), acc_shape=(8, 128))
    o, og, states = _hgrn2_fwd(p, lbl, hg_out_norm)
    x1 = _mm_rows("hg_out", og, w_out, out_dtype=F32, add=x0)
    w_up0, g_cw, w_dn0 = _send_wait("gather_wait_ffn0", coming_ffn0, x1, False)
    w_up, w_dn = [w_up0, None], [w_dn0.reshape(4, fb, d), None]
    conv_w = [g_cw[:, layer].reshape(4, 2, CONV_WIDTH, fs).transpose(0, 2, 1, 3).reshape(4, CONV_WIDTH, fb) for layer in range(2)]
    coming_attn = gather("attn", [w_kv.astype(BF16), attn_w_q[0].astype(BF16), attn_w_o[0].astype(BF16)], w_dn0)
    gains[0] = _pinned(gains[0], coming_attn[4])
    coming = []

    def start_ffn1(u):
        coming.append(gather("ffn1", [up_t(ffn_w_up[1]).astype(BF16), ffn_w_down[1].astype(BF16)], u))
        return coming[0][4]

    x2, saved0 = _ffn_forward("0", x1, gains[0], w_up[0], w_dn[0], conv_w[0], conv_b[0], start_ffn1)
    coming_ffn1 = coming[0]
    w_kvg, w_q, w_o = _send_wait("gather_wait_attn", coming_attn, x2, False)
    w_kvg, w_q, w_o = w_kvg.reshape(d, kvd), w_q.reshape(d, d), w_o.reshape(d, d)
    akv, a2 = _rmsnorm_cast("attn_norms", x2, [kvn, attn_norm])
    kv = _mm_rows("kv_proj", akv, w_kvg, out_dtype=BF16)
    q = _mm_rows("q_proj", a2, w_q, out_dtype=BF16)
    att = _attn_fwd(q, kv, attn_sinks)
    x3 = _mm_rows("attn_out", att, w_o, out_dtype=F32, add=x2)
    w_up[1], w_dn1 = _send_wait("gather_wait_ffn1", coming_ffn1, x3, False)
    w_dn[1] = w_dn1.reshape(4, fb, d)
    x4, saved1 = _ffn_forward("1", x3, gains[1], w_up[1], w_dn[1], conv_w[1], conv_b[1])
    dx4, dx4_bf, d_fin, loss_part = _loss_head(x4, fin, target)

    dx3, d_fn1, dw_up1, dw_dn1, dcw1, dcb1 = _ffn_backward("1", x3, gains[1], w_up[1], w_dn[1], conv_w[1], conv_b[1], saved1, (dx4, dx4_bf))
    rows = d // N_DEV
    scatter = lambda tag, stacks: _send_start("scatter_start_" + tag, stacks, [_landing(lax.dynamic_index_in_dim(a, mine, keepdims=False), mine) for a in stacks], True)
    going_ffn1 = scatter("ffn1", [dw_up1, dw_dn1.reshape(N_DEV, fs, d)])
    datt = _mm_rows_nt("attn_out_bwd", dx3[1], w_o, out_dtype=BF16)
    dw_o = _mm_tn("attn_out_grad", att, dx3[1])
    dq, dkv_own, dkv_before, dsink = _attn_bwd(q, kv, att, datt, _pinned(attn_sinks, going_ffn1[4]))
    tiles = dkv_before.shape[0]
    dkv = dkv_own.reshape(tiles, s // tiles, kvd)
    dkv = jnp.concatenate([dkv[:, :-WINDOW], dkv[:, -WINDOW:] + jnp.pad(dkv_before[1:], ((0, 1), (0, 0), (0, 0)))], axis=1).reshape(s, kvd)
    da2 = _mm_rows_nt("q_proj_bwd", dq, w_q, out_dtype=F32)
    dw_q = _mm_tn("q_proj_grad", a2, dq)
    dakv = _mm_rows_nt("kv_proj_bwd", dkv, w_kvg, out_dtype=F32)
    dw_kv = _mm_tn("kv_proj_grad", akv, dkv)
    going_attn = scatter("attn", [dw_kv.reshape(N_DEV, rows, kvd), dw_q.reshape(N_DEV, rows, d), dw_o.reshape(N_DEV, rows, d)])
    dx2, (d_kvn, d_attn) = _rmsnorm_bwd("attn_norms_bwd", x2, dx3[0], [(dakv, _pinned(kvn, going_attn[4])), (da2, attn_norm)])
    dx1, d_fn0, dw_up0, dw_dn0, dcw0, dcb0 = _ffn_backward("0", x1, gains[0], w_up[0], w_dn[0], conv_w[0], conv_b[0], saved0, dx2)
    going_ffn0 = scatter("ffn0", [dw_up0, dw_dn0.reshape(N_DEV, fs, d)])
    dog = _mm_rows_nt("hg_out_bwd", dx1[1], w_out, out_dtype=F32)
    dw_out = _mm_tn("hg_out_grad", og, dx1[1])
    dp, d_lbl, d_ogain = _hgrn2_bwd(p, lbl, _pinned(hg_out_norm, going_ffn0[4]), o, dog, states)
    dw_in = _matmul(
        "hg_in_grad", a0, dp, dims=TN, grid=(1, N_DEV, 1),
        a_spec=pl.BlockSpec((s, d), lambda i, j, k: (0, 0)),
        b_spec=pl.BlockSpec((None, s, half), lambda i, j, k: (j // 2, 0, j % 2)),
        o_spec=pl.BlockSpec((None, d, half), lambda i, j, k: (j, 0, 0)),
        out_shape=jax.ShapeDtypeStruct((N_DEV, d, half), BF16))
    going_hg = scatter("hg", [dw_in, dw_out.reshape(N_DEV, rows, d)])
    th = _row_tile(s, MM_ROWS // 2)
    da0 = _matmul(
        "hg_in_bwd", dp, w_in, dims=NT, grid=(s // th, 1, 1),
        a_spec=pl.BlockSpec((4, th, d), lambda i, j, k: (0, i, 0)),
        b_spec=pl.BlockSpec((N_DEV, d, half), lambda i, j, k: (0, 0, 0)),
        o_spec=pl.BlockSpec((th, d), lambda i, j, k: (i, 0)),
        out_shape=jax.ShapeDtypeStruct((s, d), F32),
        terms=lambda a_ref, b_ref: [(a_ref[k // 2, :, (k % 2) * half:(k % 2 + 1) * half], b_ref[k]) for k in range(N_DEV)])
    (dx0, _), (d_hgn,) = _rmsnorm_bwd("hg_norm_bwd", x0, dx1[0], [(da0, _pinned(hgn, going_hg[4]))])

    as_blocks = lambda a, r: a.reshape(r, N_DEV, -1).transpose(1, 0, 2).reshape(N_DEV * r, -1)
    d_cw = jnp.concatenate([g.transpose(1, 0, 2).reshape(CONV_WIDTH, 4 * fb) for g in (dcw0, dcw1)], axis=0)
    parts = [d_fin, jnp.concatenate([d_fn0, d_fn1], axis=0), jnp.concatenate([dcb0.reshape(1, 4 * fb), dcb1.reshape(1, 4 * fb)], axis=0),
             as_blocks(d_cw, 2 * CONV_WIDTH), d_attn, jnp.sum(dsink[:, :, 0], axis=0).reshape(1, nq), d_kvn, d_ogain,
             as_blocks(d_hgn, 1), as_blocks(d_lbl, 2), loss_part]
    wide = [2]
    packs = [[parts[i] for i in wide], [part for i, part in enumerate(parts) if i not in wide]]
    places = [None] * len(parts)
    for which, members in enumerate([wide, [i for i in range(len(parts)) if i not in wide]]):
        for i, off in zip(members, _pack_rows(packs[which])[0]):
            places[i] = (which, off)
    packed = [_pack("pack_wide_grads", packs[0]), _pack("pack_narrow_grads", packs[1])]
    going_small = _send_start("small_grads_start", packed, [_landing(a, mine) for a in packed], False)

    arrive = lambda tag, going, after: _send_wait("scatter_wait_" + tag, going, after, True)
    (l_up1, l_dn1), (l_kv, l_q, l_o), (l_up0, l_dn0) = (
        arrive("ffn1", going_ffn1, going_small[4]), arrive("attn", going_attn, going_small[4]), arrive("ffn0", going_ffn0, going_small[4]))
    big = {}
    for tag, w, m, v, part in [
            ("w_kv", w_kv, m_w_kv, v_w_kv, l_kv), ("attn_w_q", attn_w_q[0], m_attn_w_q[0], v_attn_w_q[0], l_q),
            ("attn_w_o", attn_w_o[0], m_attn_w_o[0], v_attn_w_o[0], l_o)]:
        big[tag] = _adamw_shard("adamw_" + tag, w, m, v, part)
    big["ffn_w_up"] = [up_t(a) for a in _adamw_layers("adamw_ffn_w_up", up_t(ffn_w_up), up_t(m_ffn_w_up), up_t(v_ffn_w_up), (l_up0, l_up1))]
    big["ffn_w_down"] = _adamw_layers("adamw_ffn_w_down", ffn_w_down, m_ffn_w_down, v_ffn_w_down, (l_dn0, l_dn1))
    lead = lambda tag: [a[None] for a in big[tag]]

    gathered = _send_wait("small_grads_wait", going_small, big["ffn_w_down"][0], False)
    two = lambda a: a.reshape(-1, a.shape[-1])
    small = [(fin, m_final_norm.reshape(1, d), v_final_norm.reshape(1, d), False), (ffn_norm, m_ffn_norm, v_ffn_norm, False),
             (ffn_conv_b, m_ffn_conv_b, v_ffn_conv_b, False), (two(ffn_conv_w), two(m_ffn_conv_w), two(v_ffn_conv_w), True),
             (attn_norm, m_attn_norm, v_attn_norm, False), (attn_sinks, m_attn_sinks, v_attn_sinks, False),
             (kvn, m_kv_norm.reshape(1, d), v_kv_norm.reshape(1, d), False), (hg_out_norm, m_hg_out_norm, v_hg_out_norm, False),
             (hg_norm, m_hg_norm, v_hg_norm, True), (hg_lb_logits, m_hg_lb_logits, v_hg_lb_logits, True)]
    res = _adamw_small(gathered, places, small)
    l_in, l_out = arrive("hg", going_hg, gathered[1])
    big["hg_w_in"] = _adamw_shard("adamw_hg_w_in", hg_w_in[0], m_hg_w_in[0], v_hg_w_in[0], l_in)
    big["hg_w_out"] = _adamw_shard("adamw_hg_w_out", hg_w_out[0], m_hg_w_out[0], v_hg_w_out[0], l_out)
    names = ["final_norm", "ffn_norm", "ffn_conv_b", "ffn_conv_w", "attn_norm", "attn_sinks", "kv_norm", "hg_out_norm", "hg_norm", "hg_lb_logits"]
    shapes = {"final_norm": final_norm.shape, "kv_norm": kv_norm.shape, "ffn_conv_w": ffn_conv_w.shape}
    out = {n: [a.reshape(shapes[n]) if n in shapes else a for a in res[4 * i:4 * i + 4]] for i, n in enumerate(names)}
    out.update(hg_w_in=lead("hg_w_in"), hg_w_out=lead("hg_w_out"), w_kv=big["w_kv"], attn_w_q=lead("attn_w_q"), attn_w_o=lead("attn_w_o"),
               ffn_w_up=big["ffn_w_up"], ffn_w_down=big["ffn_w_down"])
    order = ["hg_norm", "hg_w_in", "hg_lb_logits", "hg_out_norm", "hg_w_out", "kv_norm", "w_kv", "attn_norm", "attn_w_q", "attn_sinks",
             "attn_w_o", "ffn_norm", "ffn_w_up", "ffn_conv_w", "ffn_conv_b", "ffn_w_down", "final_norm"]
    loss = res[-1][0, 0]
    return (loss, dx0[None], *[out[n][0] for n in order], *[out[n][1] for n in order], *[out[n][2] for n in order], *[out[n][3] for n in order])
```

```python
import functools
import math

import jax
import jax.numpy as jnp
from jax import lax
from jax.experimental import pallas as pl
from jax.experimental.pallas import tpu as pltpu

F32 = jnp.float32
BF16 = jnp.bfloat16

EPS = 1e-6
HG_EXPAND = 128
HG_CHUNK = 32
ATT_HEAD_DIM = 64
ATT_KV_HEADS = 2
WINDOW = 128
CONV_WIDTH = 3
ADAM_LR = 0.001
ADAM_B1 = 0.9
ADAM_B2 = 0.999
ADAM_EPS = 1e-08
ADAM_WD = 0.01
ADAM_STEP = 10

N_DEV = 8
VMEM_LIMIT = 48 * 1024 * 1024
NEG = -1e30

NN = (((1,), (0,)), ((), ()))
NT = (((1,), (1,)), ((), ()))
TN = (((0,), (0,)), ((), ()))
MESH = pl.DeviceIdType.MESH


def _dot(a, b, dims=NN):
    return lax.dot_general(a.astype(BF16), b.astype(BF16), dims, preferred_element_type=F32)


def _sigmoid(x):
    return 0.5 * jnp.tanh(0.5 * x) + 0.5


def _silu(x):
    return x * _sigmoid(x)


def _silu_and_grad(x):
    s = _sigmoid(x)
    return x * s, s * (1.0 + x * (1.0 - s))


def _dsilu(x):
    return _silu_and_grad(x)[1]


def _params(semantics):
    return pltpu.CompilerParams(dimension_semantics=semantics, vmem_limit_bytes=VMEM_LIMIT)


def _row_tile(rows, want=512):
    return min(rows, want)


MM_ROWS = 1024


def _matmul(name, a, b, *, dims, grid, a_spec, b_spec, o_spec, out_shape, acc_shape=(8, 128), add=None, add_spec=None, terms=None):
    nk = grid[2]

    def body(*refs):
        if add is None:
            a_ref, b_ref, o_ref, acc = refs
        else:
            a_ref, b_ref, add_ref, o_ref, acc = refs
        k = pl.program_id(2)
        pairs = [(a_ref[...], b_ref[...])] if terms is None else terms(a_ref, b_ref)
        part = _dot(*pairs[0], dims)
        for pair in pairs[1:]:
            part = part + _dot(*pair, dims)

        def finish(total):
            if add is not None:
                total = total + add_ref[...]
            o_ref[...] = total.astype(o_ref.dtype)

        if nk == 1:
            finish(part)
        else:
            @pl.when(k == 0)
            def _():
                acc[...] = part

            @pl.when(k > 0)
            def _():
                acc[...] += part

            @pl.when(k == nk - 1)
            def _():
                finish(acc[...])

    in_specs = [a_spec, b_spec] + ([] if add is None else [add_spec])
    args = (a, b) + (() if add is None else (add,))
    return pl.pallas_call(
        body, name=name, grid=grid, in_specs=in_specs, out_specs=o_spec, out_shape=out_shape,
        scratch_shapes=[pltpu.VMEM(acc_shape, F32)],
        compiler_params=_params(("parallel", "parallel", "arbitrary")),
    )(*args)


def _mm_rows(name, a, w, *, out_dtype, add=None):
    s, kdim = a.shape
    n = w.shape[1]
    tm = _row_tile(s, MM_ROWS)
    return _matmul(
        name, a, w, dims=NN, grid=(s // tm, 1, 1),
        a_spec=pl.BlockSpec((tm, kdim), lambda i, j, k: (i, 0)),
        b_spec=pl.BlockSpec((kdim, n), lambda i, j, k: (0, 0)),
        o_spec=pl.BlockSpec((tm, n), lambda i, j, k: (i, 0)),
        out_shape=jax.ShapeDtypeStruct((s, n), out_dtype), acc_shape=(8, 128),
        add=add, add_spec=None if add is None else pl.BlockSpec((tm, n), lambda i, j, k: (i, 0)),
    )


def _mm_rows_nt(name, a, w, *, out_dtype):
    s, n = a.shape
    kdim = w.shape[0]
    tm = _row_tile(s, MM_ROWS)
    return _matmul(
        name, a, w, dims=NT, grid=(s // tm, 1, 1),
        a_spec=pl.BlockSpec((tm, n), lambda i, j, k: (i, 0)),
        b_spec=pl.BlockSpec((kdim, n), lambda i, j, k: (0, 0)),
        o_spec=pl.BlockSpec((tm, kdim), lambda i, j, k: (i, 0)),
        out_shape=jax.ShapeDtypeStruct((s, kdim), out_dtype), acc_shape=(8, 128),
    )


def _mm_tn(name, a, g):
    s, m = a.shape
    n = g.shape[1]
    tn = min(n, 512)
    return _matmul(
        name, a, g, dims=TN, grid=(1, n // tn, 1),
        a_spec=pl.BlockSpec((s, m), lambda i, j, k: (0, 0)),
        b_spec=pl.BlockSpec((s, tn), lambda i, j, k: (0, j)),
        o_spec=pl.BlockSpec((m, tn), lambda i, j, k: (0, j)),
        out_shape=jax.ShapeDtypeStruct((m, n), BF16),
    )


def _rmsnorm_cast(name, h, gains):
    s, d = h.shape
    tm = _row_tile(s)
    n = len(gains)

    def body(*refs):
        h_ref, g_refs, o_refs = refs[0], refs[1:1 + n], refs[1 + n:]
        xv = h_ref[...]
        xhat = xv * lax.rsqrt(jnp.mean(xv * xv, axis=-1, keepdims=True) + EPS)
        for g_ref, o_ref in zip(g_refs, o_refs):
            o_ref[...] = (xhat * g_ref[...]).astype(BF16)

    row = pl.BlockSpec((tm, d), lambda i: (i, 0))
    vec = pl.BlockSpec((1, d), lambda i: (0, 0))
    return pl.pallas_call(
        body, name=name, grid=(s // tm,), in_specs=[row] + [vec] * n, out_specs=[row] * n,
        out_shape=[jax.ShapeDtypeStruct((s, d), BF16)] * n, compiler_params=_params(("parallel",)),
    )(h, *gains)


def _rmsnorm_bwd(name, h, dres, branches):
    s, d = h.shape
    tm = _row_tile(s)
    n = len(branches)

    def body(*refs):
        h_ref, dres_ref = refs[0], refs[1]
        da_refs, g_refs = refs[2:2 + n], refs[2 + n:2 + 2 * n]
        dh_ref, dhb_ref, dg_refs = refs[2 + 2 * n], refs[3 + 2 * n], refs[4 + 2 * n:]
        i = pl.program_id(0)
        xv = h_ref[...]
        r = lax.rsqrt(jnp.mean(xv * xv, axis=-1, keepdims=True) + EPS)
        xhat = xv * r
        total = dres_ref[...]
        for da_ref, g_ref, dg_ref in zip(da_refs, g_refs, dg_refs):
            da = da_ref[...]
            dgain = jnp.sum(da * xhat, axis=0, keepdims=True)

            @pl.when(i == 0)
            def _():
                dg_ref[...] = dgain

            @pl.when(i > 0)
            def _():
                dg_ref[...] += dgain

            dxhat = da * g_ref[...]
            total = total + r * (dxhat - xhat * jnp.mean(dxhat * xhat, axis=-1, keepdims=True))
        dh_ref[...] = total
        dhb_ref[...] = total.astype(BF16)

    row = pl.BlockSpec((tm, d), lambda i: (i, 0))
    vec = pl.BlockSpec((1, d), lambda i: (0, 0))
    outs = pl.pallas_call(
        body, name=name, grid=(s // tm,), in_specs=[row, row] + [row] * n + [vec] * n, out_specs=[row, row] + [vec] * n,
        out_shape=[jax.ShapeDtypeStruct((s, d), F32), jax.ShapeDtypeStruct((s, d), BF16)] + [jax.ShapeDtypeStruct((1, d), F32)] * n,
        compiler_params=_params(("arbitrary",)),
    )(h, dres, *[b[0] for b in branches], *[b[1] for b in branches])
    return (outs[0], outs[1]), outs[2:]


def _loss_head(h, gain, target):
    s, d = h.shape
    tm = _row_tile(s)

    def body(h_ref, g_ref, t_ref, dh_ref, dhb_ref, dg_ref, loss_ref):
        i = pl.program_id(0)
        xv = h_ref[...]
        r = lax.rsqrt(jnp.mean(xv * xv, axis=-1, keepdims=True) + EPS)
        xhat = xv * r
        err = xhat * g_ref[...] - t_ref[...]
        dy = err * (1.0 / d)
        part = jnp.zeros((1, 128), F32) + 0.5 * jnp.sum(jnp.mean(err * err, axis=-1, keepdims=True))
        dgain = jnp.sum(dy * xhat, axis=0, keepdims=True)

        @pl.when(i == 0)
        def _():
            dg_ref[...] = dgain
            loss_ref[...] = part

        @pl.when(i > 0)
        def _():
            dg_ref[...] += dgain
            loss_ref[...] += part

        dxhat = dy * g_ref[...]
        dh = r * (dxhat - xhat * jnp.mean(dxhat * xhat, axis=-1, keepdims=True))
        dh_ref[...] = dh
        dhb_ref[...] = dh.astype(BF16)

    row = pl.BlockSpec((tm, d), lambda i: (i, 0))
    vec = pl.BlockSpec((1, d), lambda i: (0, 0))
    return pl.pallas_call(
        body, name="loss_head", grid=(s // tm,), in_specs=[row, vec, row],
        out_specs=[row, row, vec, pl.BlockSpec((1, 128), lambda i: (0, 0))],
        out_shape=[jax.ShapeDtypeStruct((s, d), F32), jax.ShapeDtypeStruct((s, d), BF16), jax.ShapeDtypeStruct((1, d), F32),
                   jax.ShapeDtypeStruct((1, 128), F32)],
        compiler_params=_params(("arbitrary",)),
    )(h, gain, target)


def _bdot(a, b, ca, cb):
    return lax.dot_general(a.astype(BF16), b.astype(BF16), (((ca,), (cb,)), ((0,), (0,))), preferred_element_type=F32)


def _chunk_cumsum(xv, reverse=False):
    n = xv.shape[0]
    row = lax.broadcasted_iota(jnp.int32, xv.shape, 0) % HG_CHUNK
    step = 1
    while step < HG_CHUNK:
        if reverse:
            xv = xv + jnp.where(row < HG_CHUNK - step, pltpu.roll(xv, n - step, axis=0), 0.0)
        else:
            xv = xv + jnp.where(row >= step, pltpu.roll(xv, step, axis=0), 0.0)
        step *= 2
    return xv


def _hg_terms(p_ref, lbl_ref):
    pq = p_ref[0].astype(F32)
    pf = p_ref[1].astype(F32)
    lb = _sigmoid(lbl_ref[0:1, :] - lbl_ref[1:2, :])
    sig = _sigmoid(pf)
    fg = lb + (1.0 - lb) * sig
    nc = pq.shape[0] // HG_CHUNK
    chunks = lambda a: a.reshape(nc, HG_CHUNK, HG_EXPAND)
    q = chunks(_silu(pq) * HG_EXPAND ** -0.5)
    k = chunks(1.0 - fg)
    v = chunks(p_ref[2].astype(F32))
    g = chunks(_chunk_cumsum(jnp.log(fg)))
    gm = g[:, HG_CHUNK // 2 - 1:HG_CHUNK // 2, :]
    gl = g[:, HG_CHUNK - 1:HG_CHUNK, :]
    e_mid, e_inv, e_all, e_end = jnp.exp(g - gm), jnp.exp(gm - g), jnp.exp(g), jnp.exp(gl - g)
    terms = dict(q=q, k=k, v=v, qd=q * e_all, qt=q * e_mid, kt=k * e_inv, kd=k * e_end, e_last=jnp.exp(gl),
                 e_mid=e_mid, e_inv=e_inv, e_all=e_all, e_end=e_end)
    return terms, (pq, sig, fg, lb)


def _causal(nc):
    r = lax.broadcasted_iota(jnp.int32, (nc, HG_CHUNK, HG_CHUNK), 1)
    c = lax.broadcasted_iota(jnp.int32, (nc, HG_CHUNK, HG_CHUNK), 2)
    return r >= c


def _hgrn2_fwd(p, lb_logits, out_gain):
    _, s, d = p.shape
    heads = d // HG_EXPAND
    t = _row_tile(s, 2048)
    nc = t // HG_CHUNK

    def body(p_ref, lbl_ref, gain_ref, o_ref, og_ref, st_ref, state, decay):
        @pl.when(pl.program_id(1) == 0)
        def _():
            state[...] = jnp.zeros_like(state)

        tm, _ = _hg_terms(p_ref, lbl_ref)
        decay[...] = tm["e_last"]
        st_ref[...] = _bdot(tm["v"], tm["kd"], 1, 1)

        def chunk(c, carry):
            add = st_ref[c]
            st = state[...]
            st_ref[c] = st
            state[...] = st * decay[c] + add
            return carry

        lax.fori_loop(0, nc, chunk, 0)
        a = jnp.where(_causal(nc), _bdot(tm["qt"], tm["kt"], 2, 2), 0.0)
        ov = (_bdot(tm["qd"], st_ref[...], 2, 2) + _bdot(a, tm["v"], 2, 1)).reshape(t, HG_EXPAND)
        o_ref[...] = ov
        on = ov * lax.rsqrt(jnp.mean(ov * ov, axis=-1, keepdims=True) + EPS) * gain_ref[...]
        og_ref[...] = (on * _silu(p_ref[3].astype(F32))).astype(BF16)

    blk = pl.BlockSpec((t, HG_EXPAND), lambda h, b: (b, h))
    return pl.pallas_call(
        body, name="hgrn2_fwd", grid=(heads, s // t),
        in_specs=[pl.BlockSpec((4, t, HG_EXPAND), lambda h, b: (0, b, h)), pl.BlockSpec((2, HG_EXPAND), lambda h, b: (0, h)),
                  pl.BlockSpec((1, HG_EXPAND), lambda h, b: (0, 0))],
        out_specs=[blk, blk, pl.BlockSpec((None, nc, HG_EXPAND, HG_EXPAND), lambda h, b: (h, b, 0, 0))],
        out_shape=[jax.ShapeDtypeStruct((s, d), F32), jax.ShapeDtypeStruct((s, d), BF16),
                   jax.ShapeDtypeStruct((heads, s // HG_CHUNK, HG_EXPAND, HG_EXPAND), F32)],
        scratch_shapes=[pltpu.VMEM((HG_EXPAND, HG_EXPAND), F32), pltpu.VMEM((nc, 1, HG_EXPAND), F32)],
        compiler_params=_params(("parallel", "arbitrary")),
    )(p, lb_logits, out_gain)


def _hgrn2_bwd(p, lb_logits, out_gain, o, dog, states):
    _, s, d = p.shape
    heads = d // HG_EXPAND
    t = _row_tile(s, 1024)
    nc = t // HG_CHUNK
    nb = s // t

    def body(p_ref, lbl_ref, gain_ref, o_ref, dog_ref, st_ref, dp_ref, dlbl_ref, dgain_ref, dstate, decay, dst_s):
        h, b = pl.program_id(0), pl.program_id(1)

        @pl.when(b == 0)
        def _():
            dstate[...] = jnp.zeros_like(dstate)

        tm, (pq, sig, fg, lb) = _hg_terms(p_ref, lbl_ref)
        pg = p_ref[3].astype(F32)
        ov = o_ref[...]
        r = lax.rsqrt(jnp.mean(ov * ov, axis=-1, keepdims=True) + EPS)
        ohat = ov * r
        dogv = dog_ref[...]
        d_on = dogv * _silu(pg)
        dp_ref[3] = (dogv * ohat * gain_ref[...] * _dsilu(pg)).astype(BF16)
        dgain = jnp.sum(d_on * ohat, axis=0, keepdims=True)

        @pl.when((h == 0) & (b == 0))
        def _():
            dgain_ref[...] = dgain

        @pl.when((h > 0) | (b > 0))
        def _():
            dgain_ref[...] += dgain

        dohat = d_on * gain_ref[...]
        do = (r * (dohat - ohat * jnp.mean(dohat * ohat, axis=-1, keepdims=True))).reshape(nc, HG_CHUNK, HG_EXPAND)

        decay[...] = tm["e_last"]
        dst_s[...] = _bdot(do, tm["qd"], 1, 1)

        def chunk(i, carry):
            c = nc - 1 - i
            add = dst_s[c]
            dst = dstate[...]
            dst_s[c] = dst
            dstate[...] = dst * decay[c] + add
            return carry

        lax.fori_loop(0, nc, chunk, 0)
        st, dst = st_ref[...], dst_s[...]
        causal = _causal(nc)
        a = jnp.where(causal, _bdot(tm["qt"], tm["kt"], 2, 2), 0.0)
        da = jnp.where(causal, _bdot(do, tm["v"], 2, 2), 0.0)
        dqt = _bdot(da, tm["kt"], 2, 1)
        dkt = _bdot(da, tm["qt"], 1, 1)
        dqd = _bdot(do, st, 2, 1)
        dkd = _bdot(tm["v"], dst, 2, 1)
        dv = _bdot(a, do, 1, 1) + _bdot(tm["kd"], dst, 2, 2)
        dq = dqt * tm["e_mid"] + dqd * tm["e_all"]
        dk = dkt * tm["e_inv"] + dkd * tm["e_end"]
        dg = dqt * tm["qt"] - dkt * tm["kt"] + dqd * tm["qd"] - dkd * tm["kd"]
        dgl = jnp.sum(dkd * tm["kd"], axis=1, keepdims=True) + tm["e_last"] * jnp.sum(dst * st, axis=1, keepdims=True)
        last_row = lax.broadcasted_iota(jnp.int32, (nc, HG_CHUNK, HG_EXPAND), 1) == HG_CHUNK - 1
        flat = lambda a3: a3.reshape(t, HG_EXPAND)
        dlf = _chunk_cumsum(flat(dg + jnp.where(last_row, dgl, 0.0)), reverse=True)
        dfg = dlf / fg - flat(dk)
        dlb = jnp.sum(dfg * (1.0 - sig), axis=0, keepdims=True)
        dl0 = dlb * lb * (1.0 - lb)
        dlbl = jnp.concatenate([dl0, -dl0], axis=0)

        @pl.when(b == 0)
        def _():
            dlbl_ref[...] = dlbl

        @pl.when(b > 0)
        def _():
            dlbl_ref[...] += dlbl

        dp_ref[0] = (flat(dq) * HG_EXPAND ** -0.5 * _dsilu(pq)).astype(BF16)
        dp_ref[1] = (dfg * (1.0 - lb) * sig * (1.0 - sig)).astype(BF16)
        dp_ref[2] = flat(dv).astype(BF16)

    blk = pl.BlockSpec((t, HG_EXPAND), lambda h, b: (nb - 1 - b, h))
    pblk = pl.BlockSpec((4, t, HG_EXPAND), lambda h, b: (0, nb - 1 - b, h))
    return pl.pallas_call(
        body, name="hgrn2_bwd", grid=(heads, nb),
        in_specs=[pblk, pl.BlockSpec((2, HG_EXPAND), lambda h, b: (0, h)), pl.BlockSpec((1, HG_EXPAND), lambda h, b: (0, 0)),
                  blk, blk, pl.BlockSpec((None, nc, HG_EXPAND, HG_EXPAND), lambda h, b: (h, nb - 1 - b, 0, 0))],
        out_specs=[pblk, pl.BlockSpec((2, HG_EXPAND), lambda h, b: (0, h)), pl.BlockSpec((1, HG_EXPAND), lambda h, b: (0, 0))],
        out_shape=[jax.ShapeDtypeStruct((4, s, d), BF16), jax.ShapeDtypeStruct((2, d), F32), jax.ShapeDtypeStruct((1, HG_EXPAND), F32)],
        scratch_shapes=[pltpu.VMEM((HG_EXPAND, HG_EXPAND), F32), pltpu.VMEM((nc, 1, HG_EXPAND), F32),
                        pltpu.VMEM((nc, HG_EXPAND, HG_EXPAND), F32)],
        compiler_params=_params(("arbitrary", "arbitrary")),
    )(p, lb_logits, out_gain, o, dog, states)


HALO = 8
FFN_FWD_ROWS = 512
FFN_BWD_ROWS = 256


def _shift_down(xv, n):
    return pltpu.roll(xv, n, axis=0)


def _shift_up(xv, n):
    return pltpu.roll(xv, xv.shape[0] - n, axis=0)


def _ffn_hidden_down(name, u, conv_w, conv_b, w_down, h):
    _, nj, s, fb = u.shape
    d = w_down.shape[2]
    tm = _row_tile(s, FFN_FWD_ROWS)
    per = tm // HALO

    def body(gate_ref, prev_ref, val_ref, w_ref, b_ref, wd_ref, h_ref, hid_ref, o_ref):
        i = pl.program_id(0)
        total = h_ref[...]
        for j in range(nj):
            prev = jnp.where(i > 0, prev_ref[j].astype(F32), 0.0)
            ext = jnp.concatenate([prev, gate_ref[j].astype(F32)], axis=0)
            conv = b_ref[j] + w_ref[j, 2:3, :] * ext[HALO:]
            conv = conv + w_ref[j, 1:2, :] * _shift_down(ext, 1)[HALO:]
            conv = conv + w_ref[j, 0:1, :] * _shift_down(ext, 2)[HALO:]
            hidden = (_silu(conv) * val_ref[j].astype(F32)).astype(BF16)
            hid_ref[j] = hidden
            total = total + _dot(hidden, wd_ref[j])
        o_ref[...] = total

    row = pl.BlockSpec((tm, d), lambda i: (i, 0))
    return pl.pallas_call(
        body, name=name, grid=(s // tm,),
        in_specs=[pl.BlockSpec((None, nj, tm, fb), lambda i: (0, 0, i, 0)),
                  pl.BlockSpec((None, nj, HALO, fb), lambda i: (0, 0, jnp.maximum(i * per - 1, 0), 0)),
                  pl.BlockSpec((None, nj, tm, fb), lambda i: (1, 0, i, 0)),
                  pl.BlockSpec((nj, CONV_WIDTH, fb), lambda i: (0, 0, 0)), pl.BlockSpec((nj, 1, fb), lambda i: (0, 0, 0)),
                  pl.BlockSpec((nj, fb, d), lambda i: (0, 0, 0)), row],
        out_specs=[pl.BlockSpec((nj, tm, fb), lambda i: (0, i, 0)), row],
        out_shape=[jax.ShapeDtypeStruct((nj, s, fb), BF16), jax.ShapeDtypeStruct((s, d), F32)],
        compiler_params=_params(("parallel",)),
    )(u, u, u, conv_w, conv_b, w_down, h)


def _ffn_hidden_up_bwd(name, u, dh, conv_w, conv_b, w_up, h, gain, dres):
    _, nj, s, fb = u.shape
    d = w_up.shape[2]
    tm = _row_tile(s, FFN_BWD_ROWS)
    per = tm // HALO
    nblk = s // HALO
    ni = s // tm

    def body(gate_ref, gprev_ref, gnext_ref, val_ref, vnext_ref, dh_ref, dhnext_ref, w_ref, b_ref, wu_ref, h_ref, gain_ref, dres_ref,
             du_ref, dw_ref, db_ref, dx_ref, dxb_ref, dgain_ref):
        i = pl.program_id(0)
        has_next = i < ni - 1
        total = None
        for j in range(nj):
            gprev = jnp.where(i > 0, gprev_ref[j].astype(F32), 0.0)
            gext = jnp.concatenate([gprev, gate_ref[j].astype(F32), gnext_ref[j].astype(F32)], axis=0)
            vext = jnp.concatenate([val_ref[j].astype(F32), vnext_ref[j].astype(F32)], axis=0)
            dhext = jnp.concatenate([dh_ref[j].astype(F32), jnp.where(has_next, dhnext_ref[j].astype(F32), 0.0)], axis=0)
            g0 = gext[HALO:]
            g1 = _shift_down(gext, 1)[HALO:]
            g2 = _shift_down(gext, 2)[HALO:]
            conv = b_ref[j] + w_ref[j, 2:3, :] * g0 + w_ref[j, 1:2, :] * g1 + w_ref[j, 0:1, :] * g2
            act, dact = _silu_and_grad(conv)
            dconv = dhext * vext * dact
            dgate = (w_ref[j, 2:3, :] * dconv + w_ref[j, 1:2, :] * _shift_up(dconv, 1) + w_ref[j, 0:1, :] * _shift_up(dconv, 2))[:tm].astype(BF16)
            dval = (dhext * act)[:tm].astype(BF16)
            du_ref[0, j] = dgate
            du_ref[1, j] = dval
            part = _dot(dgate, wu_ref[j]) + _dot(dval, wu_ref[nj + j])
            total = part if total is None else total + part
            own = dconv[:tm]
            dw = jnp.concatenate([jnp.sum(own * g2[:tm], axis=0, keepdims=True), jnp.sum(own * g1[:tm], axis=0, keepdims=True),
                                  jnp.sum(own * g0[:tm], axis=0, keepdims=True)], axis=0)
            db = jnp.sum(own, axis=0, keepdims=True)

            @pl.when(i == 0)
            def _():
                dw_ref[j] = dw
                db_ref[j] = db

            @pl.when(i > 0)
            def _():
                dw_ref[j] += dw
                db_ref[j] += db

        xv = h_ref[...]
        r = lax.rsqrt(jnp.mean(xv * xv, axis=-1, keepdims=True) + EPS)
        xhat = xv * r
        dgain = jnp.sum(total * xhat, axis=0, keepdims=True)

        @pl.when(i == 0)
        def _():
            dgain_ref[...] = dgain

        @pl.when(i > 0)
        def _():
            dgain_ref[...] += dgain

        dxhat = total * gain_ref[...]
        dx = dres_ref[...] + r * (dxhat - xhat * jnp.mean(dxhat * xhat, axis=-1, keepdims=True))
        dx_ref[...] = dx
        dxb_ref[...] = dx.astype(BF16)

    def tile(part):
        return pl.BlockSpec((None, nj, tm, fb), lambda i: (part, 0, i, 0))

    def after(part):
        return pl.BlockSpec((None, nj, HALO, fb), lambda i: (part, 0, jnp.minimum((i + 1) * per, nblk - 1), 0))

    row = pl.BlockSpec((tm, d), lambda i: (i, 0))
    return pl.pallas_call(
        body, name=name, grid=(ni,),
        in_specs=[tile(0), pl.BlockSpec((None, nj, HALO, fb), lambda i: (0, 0, jnp.maximum(i * per - 1, 0), 0)), after(0),
                  tile(1), after(1),
                  pl.BlockSpec((nj, tm, fb), lambda i: (0, i, 0)),
                  pl.BlockSpec((nj, HALO, fb), lambda i: (0, jnp.minimum((i + 1) * per, nblk - 1), 0)),
                  pl.BlockSpec((nj, CONV_WIDTH, fb), lambda i: (0, 0, 0)), pl.BlockSpec((nj, 1, fb), lambda i: (0, 0, 0)),
                  pl.BlockSpec((2 * nj, fb, d), lambda i: (0, 0, 0)), row, pl.BlockSpec((1, d), lambda i: (0, 0)), row],
        out_specs=[pl.BlockSpec((2, nj, tm, fb), lambda i: (0, 0, i, 0)),
                   pl.BlockSpec((nj, CONV_WIDTH, fb), lambda i: (0, 0, 0)), pl.BlockSpec((nj, 1, fb), lambda i: (0, 0, 0)),
                   row, row, pl.BlockSpec((1, d), lambda i: (0, 0))],
        out_shape=[jax.ShapeDtypeStruct((2, nj, s, fb), BF16), jax.ShapeDtypeStruct((nj, CONV_WIDTH, fb), F32),
                   jax.ShapeDtypeStruct((nj, 1, fb), F32), jax.ShapeDtypeStruct((s, d), F32), jax.ShapeDtypeStruct((s, d), BF16),
                   jax.ShapeDtypeStruct((1, d), F32)],
        compiler_params=_params(("arbitrary",)),
    )(u, u, u, u, u, dh, dh, conv_w, conv_b, w_up, h, gain, dres)


ATT_TILE = 512


def _stack_heads(ref, rows, first_head, count):
    hd = ATT_HEAD_DIM
    return jnp.concatenate([ref[rows, (first_head + j) * hd:(first_head + j + 1) * hd] for j in range(count)], axis=0)


def _unstack_heads(stacked, ref, rows, first_head, count):
    hd = ATT_HEAD_DIM
    for pair in range(count // 2):
        both = [stacked[(2 * pair + j) * WINDOW:(2 * pair + j + 1) * WINDOW, :] for j in range(2)]
        ref[rows, (first_head + 2 * pair) * hd:(first_head + 2 * pair + 2) * hd] = jnp.concatenate(both, axis=1).astype(ref.dtype)


def _attn_probs_t(kb, qs, sink_ref, first_head, count, first, n_heads):
    lanes = count * WINDOW
    ik = lax.broadcasted_iota(jnp.int32, (2 * WINDOW, lanes), 0)
    iq = lax.broadcasted_iota(jnp.int32, (2 * WINDOW, lanes), 1) % WINDOW
    dist = iq + WINDOW - ik
    valid = (dist >= 0) & (dist < WINDOW) & (ik >= jnp.where(first, WINDOW, 0))
    per_head = lambda values: jnp.concatenate([jnp.zeros((1, WINDOW), F32) + v for v in values], axis=1)
    slope = per_head([2.0 ** (-8.0 * (first_head + j + 1) / n_heads) for j in range(count)])
    sink = per_head([sink_ref[0, first_head + j] for j in range(count)])
    sc = jnp.where(valid, _dot(kb, qs, NT) * ATT_HEAD_DIM ** -0.5 - slope * dist.astype(F32), NEG)
    m = jnp.maximum(jnp.max(sc, axis=0, keepdims=True), sink)
    e = jnp.exp(sc - m)
    es = jnp.exp(sink - m)
    inv = 1.0 / (jnp.sum(e, axis=0, keepdims=True) + es)
    return e * inv, es * inv


def _attn_specs(s, d, kvd, tq):
    per = tq // WINDOW
    return [pl.BlockSpec((tq, d), lambda i: (i, 0)), pl.BlockSpec((tq, kvd), lambda i: (i, 0)),
            pl.BlockSpec((WINDOW, kvd), lambda i: (jnp.maximum(i * per - 1, 0), 0))]


def _attn_fwd(q, kv, sinks):
    s, d = q.shape
    kvd = kv.shape[1]
    half = kvd // 2
    hd = ATT_HEAD_DIM
    nq = d // hd
    group = nq // ATT_KV_HEADS
    tq = min(s, ATT_TILE)
    per = tq // WINDOW

    def body(q_ref, kvc_ref, kvp_ref, sink_ref, o_ref, band):
        i = pl.program_id(0)
        band[0:WINDOW, :] = kvp_ref[...]
        band[WINDOW:, :] = kvc_ref[...]

        def block(b, carry):
            rows = pl.ds(pl.multiple_of(b * WINDOW, WINDOW), WINDOW)
            keys = pl.ds(pl.multiple_of(b * WINDOW, WINDOW), 2 * WINDOW)
            first = (i * per + b) == 0
            for g in range(ATT_KV_HEADS):
                p, _ = _attn_probs_t(band[keys, g * hd:(g + 1) * hd], _stack_heads(q_ref, rows, g * group, group), sink_ref,
                                     g * group, group, first, nq)
                out_t = _dot(band[keys, half + g * hd:half + (g + 1) * hd], p, TN)
                _unstack_heads(out_t.T, o_ref, rows, g * group, group)
            return carry

        lax.fori_loop(0, per, block, 0)

    return pl.pallas_call(
        body, name="attn_fwd", grid=(s // tq,),
        in_specs=_attn_specs(s, d, kvd, tq) + [pl.BlockSpec(memory_space=pltpu.SMEM)],
        out_specs=pl.BlockSpec((tq, d), lambda i: (i, 0)), out_shape=jax.ShapeDtypeStruct((s, d), BF16),
        scratch_shapes=[pltpu.VMEM((tq + WINDOW, kvd), BF16)], compiler_params=_params(("parallel",)),
    )(q, kv, kv, sinks)


def _attn_bwd(q, kv, o, do, sinks):
    s, d = q.shape
    kvd = kv.shape[1]
    half = kvd // 2
    hd = ATT_HEAD_DIM
    nq = d // hd
    group = nq // ATT_KV_HEADS
    tq = min(s, ATT_TILE)
    per = tq // WINDOW
    nt = s // tq

    def body(q_ref, kvc_ref, kvp_ref, o_ref, do_ref, sink_ref, dq_ref, dkvc_ref, dkvp_ref, ds_ref, band, dband):
        i = pl.program_id(0)
        band[0:WINDOW, :] = kvp_ref[...]
        band[WINDOW:, :] = kvc_ref[...]
        dband[...] = jnp.zeros_like(dband)
        ds_ref[...] = jnp.zeros_like(ds_ref)

        def block(b, carry):
            rows = pl.ds(pl.multiple_of(b * WINDOW, WINDOW), WINDOW)
            keys = pl.ds(pl.multiple_of(b * WINDOW, WINDOW), 2 * WINDOW)
            first = (i * per + b) == 0
            dks, dvs = [], []
            for g in range(ATT_KV_HEADS):
                kb = band[keys, g * hd:(g + 1) * hd]
                vb = band[keys, half + g * hd:half + (g + 1) * hd]
                qs = _stack_heads(q_ref, rows, g * group, group)
                dos = _stack_heads(do_ref, rows, g * group, group)
                p, ps = _attn_probs_t(kb, qs, sink_ref, g * group, group, first, nq)
                prod = dos.astype(F32) * _stack_heads(o_ref, rows, g * group, group).astype(F32)
                dsum = lax.dot_general(jnp.ones((8, hd), F32), prod, NT, precision=lax.Precision.HIGHEST,
                                       preferred_element_type=F32)[0:1, :]
                dsc = p * (_dot(vb, dos, NT) - dsum) * ATT_HEAD_DIM ** -0.5
                dvs.append(_dot(p, dos))
                dks.append(_dot(dsc, qs))
                _unstack_heads(_dot(kb, dsc, TN).T, dq_ref, rows, g * group, group)
                gone = ps * dsum
                for j in range(group):
                    ds_ref[g * group + j:g * group + j + 1, :] += jnp.zeros((1, 128), F32) - jnp.sum(gone[:, j * WINDOW:(j + 1) * WINDOW])
            dband[keys, 0:half] += jnp.concatenate(dks, axis=1)
            dband[keys, half:] += jnp.concatenate(dvs, axis=1)
            return carry

        lax.fori_loop(0, per, block, 0)
        dkvp_ref[...] = dband[0:WINDOW, :]
        dkvc_ref[...] = dband[WINDOW:, :]

    big = pl.BlockSpec((tq, d), lambda i: (i, 0))
    return pl.pallas_call(
        body, name="attn_bwd", grid=(nt,),
        in_specs=_attn_specs(s, d, kvd, tq) + [big, big, pl.BlockSpec(memory_space=pltpu.SMEM)],
        out_specs=[big, pl.BlockSpec((tq, kvd), lambda i: (i, 0)), pl.BlockSpec((None, WINDOW, kvd), lambda i: (i, 0, 0)),
                   pl.BlockSpec((None, nq, 128), lambda i: (i, 0, 0))],
        out_shape=[jax.ShapeDtypeStruct((s, d), BF16), jax.ShapeDtypeStruct((s, kvd), F32), jax.ShapeDtypeStruct((nt, WINDOW, kvd), F32),
                   jax.ShapeDtypeStruct((nt, nq, 128), F32)],
        scratch_shapes=[pltpu.VMEM((tq + WINDOW, kvd), BF16), pltpu.VMEM((tq + WINDOW, kvd), F32)],
        compiler_params=_params(("parallel",)),
    )(q, kv, kv, o, do, sinks)


HBM_SPEC = pl.BlockSpec(memory_space=pltpu.HBM)
VMEM_SPEC = pl.BlockSpec(memory_space=pltpu.VMEM)


def _place():
    return lax.axis_index("x"), lax.axis_index("y"), lax.axis_index("c")


def _flip(pos, r):
    return tuple(1 - p if (r >> (2 - a)) & 1 else p for a, p in enumerate(pos))


def _index(pos):
    return 4 * pos[0] + 2 * pos[1] + pos[2]


def _all_gather(name, shards, spec):
    n = len(shards)

    def body(*refs):
        x_refs, o_refs = refs[:n], refs[n:2 * n]
        send_sems, recv_sems, local_sems = refs[2 * n:]
        me = _place()
        sibling = _flip(me, 1)
        far = [_flip(me, r) for r in (4, 2, 6)]

        def copy(t, sem, block, to, src=None):
            rows = o_refs[t].at[_index(block)]
            return pltpu.make_async_remote_copy(
                src_ref=rows if src is None else src, dst_ref=rows, send_sem=send_sems.at[t, sem], recv_sem=recv_sems.at[t, sem],
                device_id=to, device_id_type=MESH)

        own = [pltpu.make_async_copy(x_refs[t], o_refs[t].at[_index(me)], local_sems.at[t]) for t in range(n)]
        for cp in own:
            cp.start()
        first = []
        for t in range(n):
            first.append(copy(t, 0, me, sibling, src=x_refs[t]))
            first += [copy(t, 1 + j, me, peer, src=x_refs[t]) for j, peer in enumerate(far)]
        for cp in first:
            cp.start()
        passed = []
        for j, peer in enumerate(far):
            for t in range(n):
                copy(t, 1 + j, peer, me).wait_recv()
                cp = copy(t, 4 + j, peer, sibling)
                cp.start()
                passed.append(cp)
        for t in range(n):
            copy(t, 0, sibling, me).wait_recv()
            for j, peer in enumerate(far):
                copy(t, 4 + j, _flip(peer, 1), me).wait_recv()
        for cp in first + passed:
            cp.wait_send()
        for cp in own:
            cp.wait()

    return pl.pallas_call(
        body, name=name, in_specs=[spec] * n, out_specs=[spec] * n,
        out_shape=[jax.ShapeDtypeStruct((N_DEV,) + sh.shape, sh.dtype) for sh in shards],
        scratch_shapes=[pltpu.SemaphoreType.DMA((n, 7)), pltpu.SemaphoreType.DMA((n, 7)), pltpu.SemaphoreType.DMA((n,))],
    )(*shards)


SEM_SPEC = pl.BlockSpec(memory_space=pltpu.SEMAPHORE)
ANY_SPEC = pl.BlockSpec(memory_space=pl.ANY)


def _landing(own, mine):
    return lax.dynamic_update_slice(lax.empty((N_DEV,) + own.shape, own.dtype), own[None], (mine,) + (0,) * own.ndim)


def _pinned(a, token):
    return a + token[0:1, 0:1].astype(a.dtype)


def _peer_copies(src_refs, land_refs, send_sems, recv_sems, scatter, arrivals):
    me = _place()
    mine = _index(me)
    copies = []
    for t, (src, land) in enumerate(zip(src_refs, land_refs)):
        for r in range(1, N_DEV):
            peer = _flip(me, r)
            theirs = _index(peer)
            sem = t * (N_DEV - 1) + r - 1
            copies.append(pltpu.make_async_remote_copy(
                src_ref=src.at[theirs] if scatter else src, dst_ref=land.at[theirs if arrivals else mine],
                send_sem=send_sems.at[sem], recv_sem=recv_sems.at[sem], device_id=peer, device_id_type=MESH))
    return copies


def _send_start(name, sources, lands, scatter, after=None):
    n = len(sources)
    extra = 0 if after is None else 1

    def body(*refs):
        outs = refs[2 * n + extra:]
        for out in _peer_copies(refs[:n], refs[n:2 * n], outs[0], outs[1], scatter, False):
            out.start()
        outs[-1][...] = jnp.zeros_like(outs[-1])

    outs = pl.pallas_call(
        body, name=name, in_specs=[HBM_SPEC] * (2 * n) + [ANY_SPEC] * extra,
        out_specs=[SEM_SPEC, SEM_SPEC] + [HBM_SPEC] * (2 * n) + [VMEM_SPEC],
        out_shape=[pltpu.SemaphoreType.DMA((n * (N_DEV - 1),)), pltpu.SemaphoreType.DMA((n * (N_DEV - 1),))]
        + [pltpu.HBM(a.shape, a.dtype) for a in list(sources) + list(lands)] + [jax.ShapeDtypeStruct((8, 128), F32)],
        input_output_aliases={i: 2 + i for i in range(2 * n)},
        compiler_params=pltpu.CompilerParams(has_side_effects=pltpu.SideEffectType.DATAFLOW_SIDE_EFFECTING),
    )(*[pltpu.with_memory_space_constraint(a, pltpu.HBM) for a in list(sources) + list(lands)], *([] if after is None else [after]))
    return outs[0], outs[1], outs[2:2 + n], outs[2 + n:2 + 2 * n], outs[-1]


def _send_wait(name, started, after, scatter):
    send_sems, recv_sems, sources, lands, _ = started
    n = len(sources)

    def body(*refs):
        for out in _peer_copies(refs[:n], refs[n:2 * n], refs[2 * n], refs[2 * n + 1], scatter, False):
            out.wait_send()
        for arrival in _peer_copies(refs[:n], refs[n:2 * n], refs[2 * n], refs[2 * n + 1], scatter, True):
            arrival.wait_recv()

    outs = pl.pallas_call(
        body, name=name, in_specs=[HBM_SPEC] * (2 * n) + [SEM_SPEC, SEM_SPEC, ANY_SPEC], out_specs=[HBM_SPEC] * (2 * n),
        out_shape=[pltpu.HBM(a.shape, a.dtype) for a in list(sources) + list(lands)],
        input_output_aliases={i: i for i in range(2 * n)},
        compiler_params=pltpu.CompilerParams(has_side_effects=pltpu.SideEffectType.DATAFLOW_SIDE_EFFECTING),
    )(*sources, *lands, send_sems, recv_sems, after)
    return outs[n:]


def _pack_rows(parts):
    offsets, row = [], 0
    for part in parts:
        offsets.append(row)
        row += part.shape[0]
    return offsets, -(-row // 8) * 8, -(-max(part.shape[1] for part in parts) // 128) * 128


def _pack(name, parts):
    offsets, rows, width = _pack_rows(parts)

    def body(*refs):
        o_ref = refs[-1]
        o_ref[...] = jnp.zeros_like(o_ref)
        for off, ref in zip(offsets, refs[:-1]):
            o_ref[off:off + ref.shape[0], 0:ref.shape[1]] = ref[...]

    return pl.pallas_call(body, name=name, in_specs=[VMEM_SPEC] * len(parts), out_specs=VMEM_SPEC,
                          out_shape=jax.ShapeDtypeStruct((rows, width), F32))(*parts)


def _adamw_math(w, g, m, v):
    m = ADAM_B1 * m + (1.0 - ADAM_B1) * g
    v = ADAM_B2 * v + (1.0 - ADAM_B2) * (g * g)
    m_hat = m * (1.0 / (1.0 - ADAM_B1 ** ADAM_STEP))
    denom = jnp.sqrt(v * (1.0 / (1.0 - ADAM_B2 ** ADAM_STEP))) + ADAM_EPS
    inv = pl.reciprocal(denom, approx=True)
    inv = inv * (2.0 - denom * inv)
    return -ADAM_LR * (m_hat * inv + ADAM_WD * w), m, v


def _adamw_step(w_ref, m_ref, v_ref, p_ref, g_ref, d_ref, nm_ref, nv_ref):
    g = p_ref[0].astype(F32)
    for dev in range(1, N_DEV):
        g = g + p_ref[dev].astype(F32)
    g_ref[...] = g
    d_ref[...], nm_ref[...], nv_ref[...] = _adamw_math(w_ref[...], g, m_ref[...], v_ref[...])


def _adamw_rows(rows):
    return max(t for t in range(8, min(rows, 256) + 1, 8) if rows % t == 0)


def _adamw_shard(name, w, m, v, partials):
    rows, cols = w.shape
    tr = _adamw_rows(rows)
    blk = pl.BlockSpec((tr, cols), lambda i: (i, 0))
    return pl.pallas_call(
        _adamw_step_fn(), name=name, grid=(rows // tr,), in_specs=[blk, blk, blk, pl.BlockSpec((N_DEV, tr, cols), lambda i: (0, i, 0))],
        out_specs=[blk] * 4, out_shape=[jax.ShapeDtypeStruct((rows, cols), F32)] * 4, compiler_params=_params(("parallel",)),
    )(w, m, v, partials)


def _adamw_step_fn():
    return functools.partial(_adamw_step)


def _adamw_layers(name, w, m, v, partials):
    layers, rows, cols = w.shape
    tr = _adamw_rows(rows)
    last = rows // tr - 1

    def body(w_ref, m_ref, v_ref, *rest):
        for layer in range(layers):
            @pl.when(pl.program_id(0) == layer)
            def _():
                _adamw_step(w_ref, m_ref, v_ref, rest[layer], *rest[layers:])

    blk = pl.BlockSpec((None, tr, cols), lambda l, i: (l, i, 0))
    part = lambda layer: pl.BlockSpec((N_DEV, tr, cols), lambda l, i: (0, jnp.where(l == layer, i, jnp.where(l < layer, 0, last)), 0))
    return pl.pallas_call(
        body, name=name, grid=(layers, rows // tr), in_specs=[blk, blk, blk] + [part(layer) for layer in range(layers)],
        out_specs=[blk] * 4, out_shape=[jax.ShapeDtypeStruct(w.shape, F32)] * 4, compiler_params=_params(("arbitrary", "arbitrary")),
    )(w, m, v, *partials)


def _adamw_small(gathered, places, entries):
    n = len(entries)
    np_ = len(gathered)

    def body(*refs):
        pack_refs = refs[:np_]
        refs = refs[np_ - 1:]
        w_refs, m_refs, v_refs = refs[1:1 + n], refs[1 + n:1 + 2 * n], refs[1 + 2 * n:1 + 3 * n]
        outs = refs[1 + 3 * n:]
        totals = []
        for pack_ref in pack_refs:
            acc = pack_ref[0]
            for dev in range(1, N_DEV):
                acc = acc + pack_ref[dev]
            totals.append(acc)
        mine = _index(_place())
        for e in range(n):
            rows, cols = w_refs[e].shape
            total, off = totals[places[e][0]], places[e][1]
            if entries[e][3]:
                g = jnp.zeros((rows, cols), F32)
                for dev in range(N_DEV):
                    g = g + jnp.where(mine == dev, total[off + dev * rows:off + (dev + 1) * rows, 0:cols], 0.0)
            else:
                g = total[off:off + rows, 0:cols]
            outs[4 * e][...] = g
            outs[4 * e + 1][...], outs[4 * e + 2][...], outs[4 * e + 3][...] = _adamw_math(w_refs[e][...], g, m_refs[e][...], v_refs[e][...])
        outs[4 * n][...] = totals[places[n][0]][places[n][1]:places[n][1] + 1, 0:128]

    shapes = []
    for w, _, _, _ in entries:
        shapes += [jax.ShapeDtypeStruct(w.shape, F32)] * 4
    shapes.append(jax.ShapeDtypeStruct((1, 128), F32))
    return pl.pallas_call(
        body, name="adamw_small", in_specs=[VMEM_SPEC] * (np_ + 3 * n), out_specs=[VMEM_SPEC] * len(shapes), out_shape=shapes,
        compiler_params=pltpu.CompilerParams(vmem_limit_bytes=VMEM_LIMIT),
    )(*gathered, *[e[0] for e in entries], *[e[1] for e in entries], *[e[2] for e in entries])


def _ffn_forward(tag, h, gain, w_up, late):
    s, d = h.shape
    fb = w_up.shape[1]
    tm = _row_tile(s, 2 * MM_ROWS)
    a, = _rmsnorm_cast(f"ffn_norm_{tag}", h, [gain])
    u = _matmul(
        f"ffn_up_{tag}", a, w_up, dims=NT, grid=(s // tm, N_DEV, 1),
        a_spec=pl.BlockSpec((tm, d), lambda i, j, k: (i, 0)),
        b_spec=pl.BlockSpec((None, fb, d), lambda i, j, k: (j, 0, 0)),
        o_spec=pl.BlockSpec((None, None, tm, fb), lambda i, j, k: (j // 4, j % 4, i, 0)),
        out_shape=jax.ShapeDtypeStruct((2, 4, s, fb), BF16))
    w_down, conv_w, conv_b = late(u)
    hidden, out = _ffn_hidden_down(f"ffn_hidden_down_{tag}", u, conv_w, conv_b, w_down, h)
    return out, (a, u, hidden)


def _ffn_backward(tag, h, gain, w_up, w_down, conv_w, conv_b, saved, dout):
    a, u, hidden = saved
    dout, dout_bf = dout
    s, d = h.shape
    fb = w_up.shape[1]
    tm = _row_tile(s, MM_ROWS)
    dhidden = _matmul(
        f"ffn_down_bwd_{tag}", dout_bf, w_down, dims=NT, grid=(s // tm, 4, 1),
        a_spec=pl.BlockSpec((tm, d), lambda i, j, k: (i, 0)),
        b_spec=pl.BlockSpec((None, fb, d), lambda i, j, k: (j, 0, 0)),
        o_spec=pl.BlockSpec((None, tm, fb), lambda i, j, k: (j, i, 0)),
        out_shape=jax.ShapeDtypeStruct((4, s, fb), BF16))
    dw_down = _matmul(
        f"ffn_down_grad_{tag}", hidden, dout_bf, dims=TN, grid=(4, 1, 1),
        a_spec=pl.BlockSpec((None, s, fb), lambda i, j, k: (i, 0, 0)),
        b_spec=pl.BlockSpec((s, d), lambda i, j, k: (0, 0)),
        o_spec=pl.BlockSpec((None, fb, d), lambda i, j, k: (i, 0, 0)),
        out_shape=jax.ShapeDtypeStruct((4, fb, d), BF16))
    du, dconv_w, dconv_b, dh, dh_bf, dgain = _ffn_hidden_up_bwd(f"ffn_hidden_up_bwd_{tag}", u, dhidden, conv_w, conv_b, w_up, h, gain, dout)
    dw_up = _matmul(
        f"ffn_up_grad_{tag}", du, a, dims=TN, grid=(N_DEV, 1, 1),
        a_spec=pl.BlockSpec((None, None, s, fb), lambda i, j, k: (i // 4, i % 4, 0, 0)),
        b_spec=pl.BlockSpec((s, d), lambda i, j, k: (0, 0)),
        o_spec=pl.BlockSpec((None, fb, d), lambda i, j, k: (i, 0, 0)),
        out_shape=jax.ShapeDtypeStruct((N_DEV, fb, d), BF16))
    return (dh, dh_bf), dgain, dw_up, dw_down, dconv_w, dconv_b


def kernel(x, hg_norm, hg_w_in, hg_lb_logits, hg_out_norm, hg_w_out, kv_norm, w_kv, attn_norm, attn_w_q, attn_sinks, attn_w_o, ffn_norm, ffn_w_up, ffn_conv_w, ffn_conv_b, ffn_w_down, final_norm, loss_target, m_hg_norm, m_hg_w_in, m_hg_lb_logits, m_hg_out_norm, m_hg_w_out, m_kv_norm, m_w_kv, m_attn_norm, m_attn_w_q, m_attn_sinks, m_attn_w_o, m_ffn_norm, m_ffn_w_up, m_ffn_conv_w, m_ffn_conv_b, m_ffn_w_down, m_final_norm, v_hg_norm, v_hg_w_in, v_hg_lb_logits, v_hg_out_norm, v_hg_w_out, v_kv_norm, v_w_kv, v_attn_norm, v_attn_w_q, v_attn_sinks, v_attn_w_o, v_ffn_norm, v_ffn_w_up, v_ffn_conv_w, v_ffn_conv_b, v_ffn_w_down, v_final_norm):
    _, s, d = x.shape
    x0, target = x[0], loss_target[0]
    half = hg_w_in.shape[2]
    fs = ffn_conv_w.shape[2]
    fb = 2 * fs
    kvd = w_kv.shape[1]
    nq = d // ATT_HEAD_DIM
    tm = _row_tile(s, MM_ROWS)

    mine = _index(_place())
    gather = lambda tag, shards, after: _send_start("gather_start_" + tag, shards, [_landing(a, mine) for a in shards], False, after)
    w_in, g_hgn, g_lbl, w_out = _all_gather("gather_hg", [hg_w_in[0].astype(BF16), hg_norm, hg_lb_logits, hg_w_out[0].astype(BF16)], HBM_SPEC)
    w_out = w_out.reshape(d, d)
    up_t = lambda a: jnp.swapaxes(a, -1, -2)
    coming_up0 = gather("ffn_up0", [up_t(ffn_w_up[0]).astype(BF16)], g_hgn)
    hgn = _pinned(g_hgn.reshape(1, d), coming_up0[4])
    lbl = g_lbl.transpose(1, 0, 2).reshape(2, d)
    conv_b = [ffn_conv_b[layer].reshape(4, 1, fb) for layer in range(2)]
    gains = [ffn_norm[0:1], ffn_norm[1:2]]
    kvn, fin = kv_norm.reshape(1, d), final_norm.reshape(1, d)

    a0, = _rmsnorm_cast("hg_norm", x0, [hgn])
    t2 = _row_tile(s, 2 * MM_ROWS)
    p = _matmul(
        "hg_in", a0, w_in, dims=NN, grid=(s // t2, N_DEV, 1),
        a_spec=pl.BlockSpec((t2, d), lambda i, j, k: (i, 0)),
        b_spec=pl.BlockSpec((None, d, half), lambda i, j, k: (j, 0, 0)),
        o_spec=pl.BlockSpec((None, t2, half), lambda i, j, k: (j // 2, i, j % 2)),
        out_shape=jax.ShapeDtypeStruct((4, s, d), BF16), acc_shape=(8, 128))
    o, og, states = _hgrn2_fwd(p, lbl, hg_out_norm)
    coming_dn0 = gather("ffn_down0", [ffn_conv_w, ffn_w_down[0].astype(BF16)], o)
    x1 = _mm_rows("hg_out", og, _pinned(w_out, coming_dn0[4]), out_dtype=F32, add=x0)
    w_up0, = _send_wait("gather_wait_ffn_up0", coming_up0, x1, False)
    coming_attn = gather("attn", [w_kv.astype(BF16), attn_w_q[0].astype(BF16), attn_w_o[0].astype(BF16)], w_up0)
    gains[0] = _pinned(gains[0], coming_attn[4])
    w_up, w_dn, conv_w, coming = [w_up0, None], [None, None], [], {}

    def late0(u):
        g_cw, w_dn0 = _send_wait("gather_wait_ffn_down0", coming_dn0, u, False)
        w_dn[0] = w_dn0.reshape(4, fb, d)
        conv_w.extend(g_cw[:, layer].reshape(4, 2, CONV_WIDTH, fs).transpose(0, 2, 1, 3).reshape(4, CONV_WIDTH, fb) for layer in range(2))
        coming["up1"] = gather("ffn_up1", [up_t(ffn_w_up[1]).astype(BF16)], w_dn0)
        return w_dn[0], conv_w[0], _pinned(conv_b[0], coming["up1"][4])

    x2, saved0 = _ffn_forward("0", x1, gains[0], w_up[0], late0)
    w_kvg, w_q, w_o = _send_wait("gather_wait_attn", coming_attn, x2, False)
    w_kvg, w_q, w_o = w_kvg.reshape(d, kvd), w_q.reshape(d, d), w_o.reshape(d, d)
    akv, a2 = _rmsnorm_cast("attn_norms", x2, [kvn, attn_norm])
    kv = _mm_rows("kv_proj", akv, w_kvg, out_dtype=BF16)
    q = _mm_rows("q_proj", a2, w_q, out_dtype=BF16)
    att = _attn_fwd(q, kv, attn_sinks)
    coming_dn1 = gather("ffn_down1", [ffn_w_down[1].astype(BF16)], att)
    x3 = _mm_rows("attn_out", att, _pinned(w_o, coming_dn1[4]), out_dtype=F32, add=x2)
    w_up[1], = _send_wait("gather_wait_ffn_up1", coming["up1"], x3, False)

    def late1(u):
        w_dn[1] = _send_wait("gather_wait_ffn_down1", coming_dn1, u, False)[0].reshape(4, fb, d)
        return w_dn[1], conv_w[1], conv_b[1]

    x4, saved1 = _ffn_forward("1", x3, gains[1], w_up[1], late1)
    dx4, dx4_bf, d_fin, loss_part = _loss_head(x4, fin, target)

    dx3, d_fn1, dw_up1, dw_dn1, dcw1, dcb1 = _ffn_backward("1", x3, gains[1], w_up[1], w_dn[1], conv_w[1], conv_b[1], saved1, (dx4, dx4_bf))
    rows = d // N_DEV
    scatter = lambda tag, stacks: _send_start("scatter_start_" + tag, stacks, [_landing(lax.dynamic_index_in_dim(a, mine, keepdims=False), mine) for a in stacks], True)
    going_ffn1 = scatter("ffn1", [dw_up1, dw_dn1.reshape(N_DEV, fs, d)])
    datt = _mm_rows_nt("attn_out_bwd", dx3[1], w_o, out_dtype=BF16)
    dw_o = _mm_tn("attn_out_grad", att, dx3[1])
    dq, dkv_own, dkv_before, dsink = _attn_bwd(q, kv, att, datt, _pinned(attn_sinks, going_ffn1[4]))
    tiles = dkv_before.shape[0]
    dkv = dkv_own.reshape(tiles, s // tiles, kvd)
    dkv = jnp.concatenate([dkv[:, :-WINDOW], dkv[:, -WINDOW:] + jnp.pad(dkv_before[1:], ((0, 1), (0, 0), (0, 0)))], axis=1).reshape(s, kvd)
    da2 = _mm_rows_nt("q_proj_bwd", dq, w_q, out_dtype=F32)
    dw_q = _mm_tn("q_proj_grad", a2, dq)
    dakv = _mm_rows_nt("kv_proj_bwd", dkv, w_kvg, out_dtype=F32)
    dw_kv = _mm_tn("kv_proj_grad", akv, dkv)
    going_attn = scatter("attn", [dw_kv.reshape(N_DEV, rows, kvd), dw_q.reshape(N_DEV, rows, d), dw_o.reshape(N_DEV, rows, d)])
    dx2, (d_kvn, d_attn) = _rmsnorm_bwd("attn_norms_bwd", x2, dx3[0], [(dakv, _pinned(kvn, going_attn[4])), (da2, attn_norm)])
    dx1, d_fn0, dw_up0, dw_dn0, dcw0, dcb0 = _ffn_backward("0", x1, gains[0], w_up[0], w_dn[0], conv_w[0], conv_b[0], saved0, dx2)
    dw_out = _mm_tn("hg_out_grad", og, dx1[1])
    going_ffn0 = scatter("ffn0", [dw_up0, dw_dn0.reshape(N_DEV, fs, d), dw_out.reshape(N_DEV, rows, d)])
    dog = _mm_rows_nt("hg_out_bwd", dx1[1], w_out, out_dtype=F32)
    dp, d_lbl, d_ogain = _hgrn2_bwd(p, lbl, _pinned(hg_out_norm, going_ffn0[4]), o, dog, states)
    dw_in = _matmul(
        "hg_in_grad", a0, dp, dims=TN, grid=(1, N_DEV, 1),
        a_spec=pl.BlockSpec((s, d), lambda i, j, k: (0, 0)),
        b_spec=pl.BlockSpec((None, s, half), lambda i, j, k: (j // 2, 0, j % 2)),
        o_spec=pl.BlockSpec((None, d, half), lambda i, j, k: (j, 0, 0)),
        out_shape=jax.ShapeDtypeStruct((N_DEV, d, half), BF16))
    going_hg = scatter("hg", [dw_in])
    th = _row_tile(s, MM_ROWS // 2)
    da0 = _matmul(
        "hg_in_bwd", dp, w_in, dims=NT, grid=(s // th, 1, 1),
        a_spec=pl.BlockSpec((4, th, d), lambda i, j, k: (0, i, 0)),
        b_spec=pl.BlockSpec((N_DEV, d, half), lambda i, j, k: (0, 0, 0)),
        o_spec=pl.BlockSpec((th, d), lambda i, j, k: (i, 0)),
        out_shape=jax.ShapeDtypeStruct((s, d), F32),
        terms=lambda a_ref, b_ref: [(a_ref[k // 2, :, (k % 2) * half:(k % 2 + 1) * half], b_ref[k]) for k in range(N_DEV)])
    (dx0, _), (d_hgn,) = _rmsnorm_bwd("hg_norm_bwd", x0, dx1[0], [(da0, _pinned(hgn, going_hg[4]))])

    as_blocks = lambda a, r: a.reshape(r, N_DEV, -1).transpose(1, 0, 2).reshape(N_DEV * r, -1)
    d_cw = jnp.concatenate([g.transpose(1, 0, 2).reshape(CONV_WIDTH, 4 * fb) for g in (dcw0, dcw1)], axis=0)
    parts = [d_fin, jnp.concatenate([d_fn0, d_fn1], axis=0), jnp.concatenate([dcb0.reshape(1, 4 * fb), dcb1.reshape(1, 4 * fb)], axis=0),
             as_blocks(d_cw, 2 * CONV_WIDTH), d_attn, jnp.sum(dsink[:, :, 0], axis=0).reshape(1, nq), d_kvn, d_ogain,
             as_blocks(d_hgn, 1), as_blocks(d_lbl, 2), loss_part]
    wide = [2]
    packs = [[parts[i] for i in wide], [part for i, part in enumerate(parts) if i not in wide]]
    places = [None] * len(parts)
    for which, members in enumerate([wide, [i for i in range(len(parts)) if i not in wide]]):
        for i, off in zip(members, _pack_rows(packs[which])[0]):
            places[i] = (which, off)
    packed = [_pack("pack_wide_grads", packs[0]), _pack("pack_narrow_grads", packs[1])]
    going_small = _send_start("small_grads_start", packed, [_landing(a, mine) for a in packed], False)

    arrive = lambda tag, going, after: _send_wait("scatter_wait_" + tag, going, after, True)
    (l_up1, l_dn1), (l_kv, l_q, l_o), (l_up0, l_dn0, l_out) = (
        arrive("ffn1", going_ffn1, going_small[4]), arrive("attn", going_attn, going_small[4]), arrive("ffn0", going_ffn0, going_small[4]))
    big = {}
    for tag, w, m, v, part in [
            ("w_kv", w_kv, m_w_kv, v_w_kv, l_kv), ("attn_w_q", attn_w_q[0], m_attn_w_q[0], v_attn_w_q[0], l_q),
            ("attn_w_o", attn_w_o[0], m_attn_w_o[0], v_attn_w_o[0], l_o)]:
        big[tag] = _adamw_shard("adamw_" + tag, w, m, v, part)
    big["ffn_w_up"] = [up_t(a) for a in _adamw_layers("adamw_ffn_w_up", up_t(ffn_w_up), up_t(m_ffn_w_up), up_t(v_ffn_w_up), (l_up0, l_up1))]
    big["ffn_w_down"] = _adamw_layers("adamw_ffn_w_down", ffn_w_down, m_ffn_w_down, v_ffn_w_down, (l_dn0, l_dn1))
    lead = lambda tag: [a[None] for a in big[tag]]

    gathered = _send_wait("small_grads_wait", going_small, big["ffn_w_down"][0], False)
    two = lambda a: a.reshape(-1, a.shape[-1])
    small = [(fin, m_final_norm.reshape(1, d), v_final_norm.reshape(1, d), False), (ffn_norm, m_ffn_norm, v_ffn_norm, False),
             (ffn_conv_b, m_ffn_conv_b, v_ffn_conv_b, False), (two(ffn_conv_w), two(m_ffn_conv_w), two(v_ffn_conv_w), True),
             (attn_norm, m_attn_norm, v_attn_norm, False), (attn_sinks, m_attn_sinks, v_attn_sinks, False),
             (kvn, m_kv_norm.reshape(1, d), v_kv_norm.reshape(1, d), False), (hg_out_norm, m_hg_out_norm, v_hg_out_norm, False),
             (hg_norm, m_hg_norm, v_hg_norm, True), (hg_lb_logits, m_hg_lb_logits, v_hg_lb_logits, True)]
    res = _adamw_small(gathered, places, small)
    l_in, = arrive("hg", going_hg, gathered[1])
    big["hg_w_in"] = _adamw_shard("adamw_hg_w_in", hg_w_in[0], m_hg_w_in[0], v_hg_w_in[0], l_in)
    big["hg_w_out"] = _adamw_shard("adamw_hg_w_out", hg_w_out[0], m_hg_w_out[0], v_hg_w_out[0], l_out)
    names = ["final_norm", "ffn_norm", "ffn_conv_b", "ffn_conv_w", "attn_norm", "attn_sinks", "kv_norm", "hg_out_norm", "hg_norm", "hg_lb_logits"]
    shapes = {"final_norm": final_norm.shape, "kv_norm": kv_norm.shape, "ffn_conv_w": ffn_conv_w.shape}
    out = {n: [a.reshape(shapes[n]) if n in shapes else a for a in res[4 * i:4 * i + 4]] for i, n in enumerate(names)}
    out.update(hg_w_in=lead("hg_w_in"), hg_w_out=lead("hg_w_out"), w_kv=big["w_kv"], attn_w_q=lead("attn_w_q"), attn_w_o=lead("attn_w_o"),
               ffn_w_up=big["ffn_w_up"], ffn_w_down=big["ffn_w_down"])
    order = ["hg_norm", "hg_w_in", "hg_lb_logits", "hg_out_norm", "hg_w_out", "kv_norm", "w_kv", "attn_norm", "attn_w_q", "attn_sinks",
             "attn_w_o", "ffn_norm", "ffn_w_up", "ffn_conv_w", "ffn_conv_b", "ffn_w_down", "final_norm"]
    loss = res[-1][0, 0]
    return (loss, dx0[None], *[out[n][0] for n in order], *[out[n][1] for n in order], *[out[n][2] for n in order], *[out[n][3] for n in order])
```

```python
import functools

import jax
import jax.numpy as jnp
from jax import lax
from jax.experimental import pallas as pl
from jax.experimental.pallas import tpu as pltpu

F32 = jnp.float32
BF16 = jnp.bfloat16

EPS = 1e-6
HG_EXPAND = 128
HG_CHUNK = 32
ATT_HEAD_DIM = 64
ATT_KV_HEADS = 2
WINDOW = 128
CONV_WIDTH = 3
ADAM_LR = 0.001
ADAM_B1 = 0.9
ADAM_B2 = 0.999
ADAM_EPS = 1e-08
ADAM_WD = 0.01
ADAM_STEP = 10

N_DEV = 8
VMEM_LIMIT = 48 * 1024 * 1024
NEG = -1e30

NN = (((1,), (0,)), ((), ()))
NT = (((1,), (1,)), ((), ()))
TN = (((0,), (0,)), ((), ()))
MESH = pl.DeviceIdType.MESH


def _dot(a, b, dims=NN):
    return lax.dot_general(a.astype(BF16), b.astype(BF16), dims, preferred_element_type=F32)


def _sigmoid(x):
    return 0.5 * jnp.tanh(0.5 * x) + 0.5


def _silu(x):
    return x * _sigmoid(x)


def _silu_and_grad(x):
    s = _sigmoid(x)
    return x * s, s * (1.0 + x * (1.0 - s))


def _dsilu(x):
    return _silu_and_grad(x)[1]


def _params(semantics):
    return pltpu.CompilerParams(dimension_semantics=semantics, vmem_limit_bytes=VMEM_LIMIT)


def _row_tile(rows, want=512):
    return min(rows, want)


MM_ROWS = 1024


def _matmul(name, a, b, *, dims, grid, a_spec, b_spec, o_spec, out_shape, acc_shape=(8, 128), add=None, add_spec=None, terms=None):
    nk = grid[2]

    def body(*refs):
        if add is None:
            a_ref, b_ref, o_ref, acc = refs
        else:
            a_ref, b_ref, add_ref, o_ref, acc = refs
        k = pl.program_id(2)
        pairs = [(a_ref[...], b_ref[...])] if terms is None else terms(a_ref, b_ref)
        part = _dot(*pairs[0], dims)
        for pair in pairs[1:]:
            part = part + _dot(*pair, dims)

        def finish(total):
            if add is not None:
                total = total + add_ref[...]
            o_ref[...] = total.astype(o_ref.dtype)

        if nk == 1:
            finish(part)
        else:
            @pl.when(k == 0)
            def _():
                acc[...] = part

            @pl.when(k > 0)
            def _():
                acc[...] += part

            @pl.when(k == nk - 1)
            def _():
                finish(acc[...])

    in_specs = [a_spec, b_spec] + ([] if add is None else [add_spec])
    args = (a, b) + (() if add is None else (add,))
    return pl.pallas_call(
        body, name=name, grid=grid, in_specs=in_specs, out_specs=o_spec, out_shape=out_shape,
        scratch_shapes=[pltpu.VMEM(acc_shape, F32)],
        compiler_params=_params(("parallel", "parallel", "arbitrary")),
    )(*args)


def _mm_rows(name, a, w, *, out_dtype, add=None):
    s, kdim = a.shape
    n = w.shape[1]
    tm = _row_tile(s, MM_ROWS)
    return _matmul(
        name, a, w, dims=NN, grid=(s // tm, 1, 1),
        a_spec=pl.BlockSpec((tm, kdim), lambda i, j, k: (i, 0)),
        b_spec=pl.BlockSpec((kdim, n), lambda i, j, k: (0, 0)),
        o_spec=pl.BlockSpec((tm, n), lambda i, j, k: (i, 0)),
        out_shape=jax.ShapeDtypeStruct((s, n), out_dtype), acc_shape=(8, 128),
        add=add, add_spec=None if add is None else pl.BlockSpec((tm, n), lambda i, j, k: (i, 0)),
    )


def _mm_rows_nt(name, a, w, *, out_dtype):
    s, n = a.shape
    kdim = w.shape[0]
    tm = _row_tile(s, MM_ROWS)
    return _matmul(
        name, a, w, dims=NT, grid=(s // tm, 1, 1),
        a_spec=pl.BlockSpec((tm, n), lambda i, j, k: (i, 0)),
        b_spec=pl.BlockSpec((kdim, n), lambda i, j, k: (0, 0)),
        o_spec=pl.BlockSpec((tm, kdim), lambda i, j, k: (i, 0)),
        out_shape=jax.ShapeDtypeStruct((s, kdim), out_dtype), acc_shape=(8, 128),
    )


def _mm_tn(name, a, g):
    s, m = a.shape
    n = g.shape[1]
    tn = min(n, 512)
    return _matmul(
        name, a, g, dims=TN, grid=(1, n // tn, 1),
        a_spec=pl.BlockSpec((s, m), lambda i, j, k: (0, 0)),
        b_spec=pl.BlockSpec((s, tn), lambda i, j, k: (0, j)),
        o_spec=pl.BlockSpec((m, tn), lambda i, j, k: (0, j)),
        out_shape=jax.ShapeDtypeStruct((m, n), BF16),
    )


def _rmsnorm_cast(name, h, gains):
    s, d = h.shape
    tm = _row_tile(s)
    n = len(gains)

    def body(*refs):
        h_ref, g_refs, o_refs = refs[0], refs[1:1 + n], refs[1 + n:]
        xv = h_ref[...]
        xhat = xv * lax.rsqrt(jnp.mean(xv * xv, axis=-1, keepdims=True) + EPS)
        for g_ref, o_ref in zip(g_refs, o_refs):
            o_ref[...] = (xhat * g_ref[...]).astype(BF16)

    row = pl.BlockSpec((tm, d), lambda i: (i, 0))
    vec = pl.BlockSpec((1, d), lambda i: (0, 0))
    return pl.pallas_call(
        body, name=name, grid=(s // tm,), in_specs=[row] + [vec] * n, out_specs=[row] * n,
        out_shape=[jax.ShapeDtypeStruct((s, d), BF16)] * n, compiler_params=_params(("parallel",)),
    )(h, *gains)


def _proj_norm_bwd(name, h, dres, branches):
    s, d = h.shape
    tm = _row_tile(s)
    n = len(branches)

    def body(*refs):
        h_ref, dres_ref = refs[0], refs[1]
        g_refs, w_refs, gain_refs = refs[2:2 + n], refs[2 + n:2 + 2 * n], refs[2 + 2 * n:2 + 3 * n]
        dh_ref, dhb_ref, dg_refs = refs[2 + 3 * n], refs[3 + 3 * n], refs[4 + 3 * n:]
        i = pl.program_id(0)
        xv = h_ref[...]
        r = lax.rsqrt(jnp.mean(xv * xv, axis=-1, keepdims=True) + EPS)
        xhat = xv * r
        total = dres_ref[...]
        for branch, g_ref, w_ref, gain_ref, dg_ref in zip(branches, g_refs, w_refs, gain_refs, dg_refs):
            pairs = branch[4](g_ref, w_ref)
            da = _dot(*pairs[0], NT)
            for pair in pairs[1:]:
                da = da + _dot(*pair, NT)
            dgain = jnp.sum(da * xhat, axis=0, keepdims=True)

            @pl.when(i == 0)
            def _():
                dg_ref[...] = dgain

            @pl.when(i > 0)
            def _():
                dg_ref[...] += dgain

            dxhat = da * gain_ref[...]
            total = total + r * (dxhat - xhat * jnp.mean(dxhat * xhat, axis=-1, keepdims=True))
        dh_ref[...] = total
        dhb_ref[...] = total.astype(BF16)

    row = pl.BlockSpec((tm, d), lambda i: (i, 0))
    vec = pl.BlockSpec((1, d), lambda i: (0, 0))
    outs = pl.pallas_call(
        body, name=name, grid=(s // tm,),
        in_specs=[row, row] + [b[1](tm) for b in branches] + [b[3] for b in branches] + [vec] * n, out_specs=[row, row] + [vec] * n,
        out_shape=[jax.ShapeDtypeStruct((s, d), F32), jax.ShapeDtypeStruct((s, d), BF16)] + [jax.ShapeDtypeStruct((1, d), F32)] * n,
        compiler_params=_params(("arbitrary",)),
    )(h, dres, *[b[0] for b in branches], *[b[2] for b in branches], *[b[5] for b in branches])
    return (outs[0], outs[1]), outs[2:]


def _loss_head(h, gain, target):
    s, d = h.shape
    tm = _row_tile(s)

    def body(h_ref, g_ref, t_ref, dh_ref, dhb_ref, dg_ref, loss_ref):
        i = pl.program_id(0)
        xv = h_ref[...]
        r = lax.rsqrt(jnp.mean(xv * xv, axis=-1, keepdims=True) + EPS)
        xhat = xv * r
        err = xhat * g_ref[...] - t_ref[...]
        dy = err * (1.0 / d)
        part = jnp.zeros((1, 128), F32) + 0.5 * jnp.sum(jnp.mean(err * err, axis=-1, keepdims=True))
        dgain = jnp.sum(dy * xhat, axis=0, keepdims=True)

        @pl.when(i == 0)
        def _():
            dg_ref[...] = dgain
            loss_ref[...] = part

        @pl.when(i > 0)
        def _():
            dg_ref[...] += dgain
            loss_ref[...] += part

        dxhat = dy * g_ref[...]
        dh = r * (dxhat - xhat * jnp.mean(dxhat * xhat, axis=-1, keepdims=True))
        dh_ref[...] = dh
        dhb_ref[...] = dh.astype(BF16)

    row = pl.BlockSpec((tm, d), lambda i: (i, 0))
    vec = pl.BlockSpec((1, d), lambda i: (0, 0))
    return pl.pallas_call(
        body, name="loss_head", grid=(s // tm,), in_specs=[row, vec, row],
        out_specs=[row, row, vec, pl.BlockSpec((1, 128), lambda i: (0, 0))],
        out_shape=[jax.ShapeDtypeStruct((s, d), F32), jax.ShapeDtypeStruct((s, d), BF16), jax.ShapeDtypeStruct((1, d), F32),
                   jax.ShapeDtypeStruct((1, 128), F32)],
        compiler_params=_params(("arbitrary",)),
    )(h, gain, target)


def _bdot(a, b, ca, cb):
    return lax.dot_general(a.astype(BF16), b.astype(BF16), (((ca,), (cb,)), ((0,), (0,))), preferred_element_type=F32)


def _chunk_cumsum(xv, reverse=False):
    n = xv.shape[0]
    row = lax.broadcasted_iota(jnp.int32, xv.shape, 0) % HG_CHUNK
    step = 1
    while step < HG_CHUNK:
        if reverse:
            xv = xv + jnp.where(row < HG_CHUNK - step, pltpu.roll(xv, n - step, axis=0), 0.0)
        else:
            xv = xv + jnp.where(row >= step, pltpu.roll(xv, step, axis=0), 0.0)
        step *= 2
    return xv


def _hg_terms(p_ref, lbl_ref):
    pq = p_ref[0].astype(F32)
    pf = p_ref[1].astype(F32)
    lb = _sigmoid(lbl_ref[0:1, :] - lbl_ref[1:2, :])
    sig = _sigmoid(pf)
    fg = lb + (1.0 - lb) * sig
    nc = pq.shape[0] // HG_CHUNK
    chunks = lambda a: a.reshape(nc, HG_CHUNK, HG_EXPAND)
    q = chunks(_silu(pq) * HG_EXPAND ** -0.5)
    k = chunks(1.0 - fg)
    v = chunks(p_ref[2].astype(F32))
    g = chunks(_chunk_cumsum(jnp.log(fg)))
    gm = g[:, HG_CHUNK // 2 - 1:HG_CHUNK // 2, :]
    gl = g[:, HG_CHUNK - 1:HG_CHUNK, :]
    e_mid, e_inv, e_all, e_end = jnp.exp(g - gm), jnp.exp(gm - g), jnp.exp(g), jnp.exp(gl - g)
    terms = dict(q=q, k=k, v=v, qd=q * e_all, qt=q * e_mid, kt=k * e_inv, kd=k * e_end, e_last=jnp.exp(gl),
                 e_mid=e_mid, e_inv=e_inv, e_all=e_all, e_end=e_end)
    return terms, (pq, sig, fg, lb)


def _causal(nc):
    r = lax.broadcasted_iota(jnp.int32, (nc, HG_CHUNK, HG_CHUNK), 1)
    c = lax.broadcasted_iota(jnp.int32, (nc, HG_CHUNK, HG_CHUNK), 2)
    return r >= c


def _hgrn2_fwd(p, lb_logits, out_gain):
    _, s, d = p.shape
    heads = d // HG_EXPAND
    t = _row_tile(s, 2048)
    nc = t // HG_CHUNK

    def body(p_ref, lbl_ref, gain_ref, o_ref, og_ref, st_ref, state, decay):
        @pl.when(pl.program_id(1) == 0)
        def _():
            state[...] = jnp.zeros_like(state)

        tm, _ = _hg_terms(p_ref, lbl_ref)
        decay[...] = tm["e_last"]
        st_ref[...] = _bdot(tm["v"], tm["kd"], 1, 1)

        def chunk(c, carry):
            add = st_ref[c]
            st = state[...]
            st_ref[c] = st
            state[...] = st * decay[c] + add
            return carry

        lax.fori_loop(0, nc, chunk, 0)
        a = jnp.where(_causal(nc), _bdot(tm["qt"], tm["kt"], 2, 2), 0.0)
        ov = (_bdot(tm["qd"], st_ref[...], 2, 2) + _bdot(a, tm["v"], 2, 1)).reshape(t, HG_EXPAND)
        o_ref[...] = ov
        on = ov * lax.rsqrt(jnp.mean(ov * ov, axis=-1, keepdims=True) + EPS) * gain_ref[...]
        og_ref[...] = (on * _silu(p_ref[3].astype(F32))).astype(BF16)

    blk = pl.BlockSpec((t, HG_EXPAND), lambda h, b: (b, h))
    return pl.pallas_call(
        body, name="hgrn2_fwd", grid=(heads, s // t),
        in_specs=[pl.BlockSpec((4, t, HG_EXPAND), lambda h, b: (0, b, h)), pl.BlockSpec((2, HG_EXPAND), lambda h, b: (0, h)),
                  pl.BlockSpec((1, HG_EXPAND), lambda h, b: (0, 0))],
        out_specs=[blk, blk, pl.BlockSpec((None, nc, HG_EXPAND, HG_EXPAND), lambda h, b: (h, b, 0, 0))],
        out_shape=[jax.ShapeDtypeStruct((s, d), F32), jax.ShapeDtypeStruct((s, d), BF16),
                   jax.ShapeDtypeStruct((heads, s // HG_CHUNK, HG_EXPAND, HG_EXPAND), F32)],
        scratch_shapes=[pltpu.VMEM((HG_EXPAND, HG_EXPAND), F32), pltpu.VMEM((nc, 1, HG_EXPAND), F32)],
        compiler_params=_params(("parallel", "arbitrary")),
    )(p, lb_logits, out_gain)


def _hgrn2_bwd(p, lb_logits, out_gain, o, dog, states):
    _, s, d = p.shape
    heads = d // HG_EXPAND
    t = _row_tile(s, 1024)
    nc = t // HG_CHUNK
    nb = s // t

    def body(p_ref, lbl_ref, gain_ref, o_ref, dog_ref, st_ref, dp_ref, dlbl_ref, dgain_ref, dstate, decay, dst_s):
        h, b = pl.program_id(0), pl.program_id(1)

        @pl.when(b == 0)
        def _():
            dstate[...] = jnp.zeros_like(dstate)

        tm, (pq, sig, fg, lb) = _hg_terms(p_ref, lbl_ref)
        pg = p_ref[3].astype(F32)
        ov = o_ref[...]
        r = lax.rsqrt(jnp.mean(ov * ov, axis=-1, keepdims=True) + EPS)
        ohat = ov * r
        dogv = dog_ref[...]
        d_on = dogv * _silu(pg)
        dp_ref[3] = (dogv * ohat * gain_ref[...] * _dsilu(pg)).astype(BF16)
        dgain = jnp.sum(d_on * ohat, axis=0, keepdims=True)

        @pl.when((h == 0) & (b == 0))
        def _():
            dgain_ref[...] = dgain

        @pl.when((h > 0) | (b > 0))
        def _():
            dgain_ref[...] += dgain

        dohat = d_on * gain_ref[...]
        do = (r * (dohat - ohat * jnp.mean(dohat * ohat, axis=-1, keepdims=True))).reshape(nc, HG_CHUNK, HG_EXPAND)

        decay[...] = tm["e_last"]
        dst_s[...] = _bdot(do, tm["qd"], 1, 1)

        def chunk(i, carry):
            c = nc - 1 - i
            add = dst_s[c]
            dst = dstate[...]
            dst_s[c] = dst
            dstate[...] = dst * decay[c] + add
            return carry

        lax.fori_loop(0, nc, chunk, 0)
        st, dst = st_ref[...], dst_s[...]
        causal = _causal(nc)
        a = jnp.where(causal, _bdot(tm["qt"], tm["kt"], 2, 2), 0.0)
        da = jnp.where(causal, _bdot(do, tm["v"], 2, 2), 0.0)
        dqt = _bdot(da, tm["kt"], 2, 1)
        dkt = _bdot(da, tm["qt"], 1, 1)
        dqd = _bdot(do, st, 2, 1)
        dkd = _bdot(tm["v"], dst, 2, 1)
        dv = _bdot(a, do, 1, 1) + _bdot(tm["kd"], dst, 2, 2)
        dq = dqt * tm["e_mid"] + dqd * tm["e_all"]
        dk = dkt * tm["e_inv"] + dkd * tm["e_end"]
        dg = dqt * tm["qt"] - dkt * tm["kt"] + dqd * tm["qd"] - dkd * tm["kd"]
        dgl = jnp.sum(dkd * tm["kd"], axis=1, keepdims=True) + tm["e_last"] * jnp.sum(dst * st, axis=1, keepdims=True)
        last_row = lax.broadcasted_iota(jnp.int32, (nc, HG_CHUNK, HG_EXPAND), 1) == HG_CHUNK - 1
        flat = lambda a3: a3.reshape(t, HG_EXPAND)
        dlf = _chunk_cumsum(flat(dg + jnp.where(last_row, dgl, 0.0)), reverse=True)
        dfg = dlf / fg - flat(dk)
        dlb = jnp.sum(dfg * (1.0 - sig), axis=0, keepdims=True)
        dl0 = dlb * lb * (1.0 - lb)
        dlbl = jnp.concatenate([dl0, -dl0], axis=0)

        @pl.when(b == 0)
        def _():
            dlbl_ref[...] = dlbl

        @pl.when(b > 0)
        def _():
            dlbl_ref[...] += dlbl

        dp_ref[0] = (flat(dq) * HG_EXPAND ** -0.5 * _dsilu(pq)).astype(BF16)
        dp_ref[1] = (dfg * (1.0 - lb) * sig * (1.0 - sig)).astype(BF16)
        dp_ref[2] = flat(dv).astype(BF16)

    blk = pl.BlockSpec((t, HG_EXPAND), lambda h, b: (nb - 1 - b, h))
    pblk = pl.BlockSpec((4, t, HG_EXPAND), lambda h, b: (0, nb - 1 - b, h))
    return pl.pallas_call(
        body, name="hgrn2_bwd", grid=(heads, nb),
        in_specs=[pblk, pl.BlockSpec((2, HG_EXPAND), lambda h, b: (0, h)), pl.BlockSpec((1, HG_EXPAND), lambda h, b: (0, 0)),
                  blk, blk, pl.BlockSpec((None, nc, HG_EXPAND, HG_EXPAND), lambda h, b: (h, nb - 1 - b, 0, 0))],
        out_specs=[pblk, pl.BlockSpec((2, HG_EXPAND), lambda h, b: (0, h)), pl.BlockSpec((1, HG_EXPAND), lambda h, b: (0, 0))],
        out_shape=[jax.ShapeDtypeStruct((4, s, d), BF16), jax.ShapeDtypeStruct((2, d), F32), jax.ShapeDtypeStruct((1, HG_EXPAND), F32)],
        scratch_shapes=[pltpu.VMEM((HG_EXPAND, HG_EXPAND), F32), pltpu.VMEM((nc, 1, HG_EXPAND), F32),
                        pltpu.VMEM((nc, HG_EXPAND, HG_EXPAND), F32)],
        compiler_params=_params(("arbitrary", "arbitrary")),
    )(p, lb_logits, out_gain, o, dog, states)


HALO = 8
FFN_FWD_ROWS = 512
FFN_BWD_ROWS = 256


def _shift_down(xv, n):
    return pltpu.roll(xv, n, axis=0)


def _shift_up(xv, n):
    return pltpu.roll(xv, xv.shape[0] - n, axis=0)


def _ffn_hidden_down(name, u, conv_w, conv_b, w_down, h):
    _, nj, s, fb = u.shape
    d = w_down.shape[2]
    tm = _row_tile(s, FFN_FWD_ROWS)
    per = tm // HALO

    def body(gate_ref, prev_ref, val_ref, w_ref, b_ref, wd_ref, h_ref, hid_ref, o_ref):
        i = pl.program_id(0)
        total = h_ref[...]
        for j in range(nj):
            prev = jnp.where(i > 0, prev_ref[j].astype(F32), 0.0)
            ext = jnp.concatenate([prev, gate_ref[j].astype(F32)], axis=0)
            conv = b_ref[j] + w_ref[j, 2:3, :] * ext[HALO:]
            conv = conv + w_ref[j, 1:2, :] * _shift_down(ext, 1)[HALO:]
            conv = conv + w_ref[j, 0:1, :] * _shift_down(ext, 2)[HALO:]
            hidden = (_silu(conv) * val_ref[j].astype(F32)).astype(BF16)
            hid_ref[j] = hidden
            total = total + _dot(hidden, wd_ref[j])
        o_ref[...] = total

    row = pl.BlockSpec((tm, d), lambda i: (i, 0))
    return pl.pallas_call(
        body, name=name, grid=(s // tm,),
        in_specs=[pl.BlockSpec((None, nj, tm, fb), lambda i: (0, 0, i, 0)),
                  pl.BlockSpec((None, nj, HALO, fb), lambda i: (0, 0, jnp.maximum(i * per - 1, 0), 0)),
                  pl.BlockSpec((None, nj, tm, fb), lambda i: (1, 0, i, 0)),
                  pl.BlockSpec((nj, CONV_WIDTH, fb), lambda i: (0, 0, 0)), pl.BlockSpec((nj, 1, fb), lambda i: (0, 0, 0)),
                  pl.BlockSpec((nj, fb, d), lambda i: (0, 0, 0)), row],
        out_specs=[pl.BlockSpec((nj, tm, fb), lambda i: (0, i, 0)), row],
        out_shape=[jax.ShapeDtypeStruct((nj, s, fb), BF16), jax.ShapeDtypeStruct((s, d), F32)],
        compiler_params=_params(("parallel",)),
    )(u, u, u, conv_w, conv_b, w_down, h)


def _ffn_hidden_up_bwd(name, u, dh, conv_w, conv_b, w_up, h, gain, dres):
    _, nj, s, fb = u.shape
    d = w_up.shape[2]
    tm = _row_tile(s, FFN_BWD_ROWS)
    per = tm // HALO
    nblk = s // HALO
    ni = s // tm

    def body(gate_ref, gprev_ref, gnext_ref, val_ref, vnext_ref, dh_ref, dhnext_ref, w_ref, b_ref, wu_ref, h_ref, gain_ref, dres_ref,
             du_ref, dw_ref, db_ref, dx_ref, dxb_ref, dgain_ref):
        i = pl.program_id(0)
        has_next = i < ni - 1
        total = None
        for j in range(nj):
            gprev = jnp.where(i > 0, gprev_ref[j].astype(F32), 0.0)
            gext = jnp.concatenate([gprev, gate_ref[j].astype(F32), gnext_ref[j].astype(F32)], axis=0)
            vext = jnp.concatenate([val_ref[j].astype(F32), vnext_ref[j].astype(F32)], axis=0)
            dhext = jnp.concatenate([dh_ref[j].astype(F32), jnp.where(has_next, dhnext_ref[j].astype(F32), 0.0)], axis=0)
            g0 = gext[HALO:]
            g1 = _shift_down(gext, 1)[HALO:]
            g2 = _shift_down(gext, 2)[HALO:]
            conv = b_ref[j] + w_ref[j, 2:3, :] * g0 + w_ref[j, 1:2, :] * g1 + w_ref[j, 0:1, :] * g2
            act, dact = _silu_and_grad(conv)
            dconv = dhext * vext * dact
            dgate = (w_ref[j, 2:3, :] * dconv + w_ref[j, 1:2, :] * _shift_up(dconv, 1) + w_ref[j, 0:1, :] * _shift_up(dconv, 2))[:tm].astype(BF16)
            dval = (dhext * act)[:tm].astype(BF16)
            du_ref[0, j] = dgate
            du_ref[1, j] = dval
            part = _dot(dgate, wu_ref[j]) + _dot(dval, wu_ref[nj + j])
            total = part if total is None else total + part
            own = dconv[:tm]
            dw = jnp.concatenate([jnp.sum(own * g2[:tm], axis=0, keepdims=True), jnp.sum(own * g1[:tm], axis=0, keepdims=True),
                                  jnp.sum(own * g0[:tm], axis=0, keepdims=True)], axis=0)
            db = jnp.sum(own, axis=0, keepdims=True)

            @pl.when(i == 0)
            def _():
                dw_ref[j] = dw
                db_ref[j] = db

            @pl.when(i > 0)
            def _():
                dw_ref[j] += dw
                db_ref[j] += db

        xv = h_ref[...]
        r = lax.rsqrt(jnp.mean(xv * xv, axis=-1, keepdims=True) + EPS)
        xhat = xv * r
        dgain = jnp.sum(total * xhat, axis=0, keepdims=True)

        @pl.when(i == 0)
        def _():
            dgain_ref[...] = dgain

        @pl.when(i > 0)
        def _():
            dgain_ref[...] += dgain

        dxhat = total * gain_ref[...]
        dx = dres_ref[...] + r * (dxhat - xhat * jnp.mean(dxhat * xhat, axis=-1, keepdims=True))
        dx_ref[...] = dx
        dxb_ref[...] = dx.astype(BF16)

    def tile(part):
        return pl.BlockSpec((None, nj, tm, fb), lambda i: (part, 0, i, 0))

    def after(part):
        return pl.BlockSpec((None, nj, HALO, fb), lambda i: (part, 0, jnp.minimum((i + 1) * per, nblk - 1), 0))

    row = pl.BlockSpec((tm, d), lambda i: (i, 0))
    return pl.pallas_call(
        body, name=name, grid=(ni,),
        in_specs=[tile(0), pl.BlockSpec((None, nj, HALO, fb), lambda i: (0, 0, jnp.maximum(i * per - 1, 0), 0)), after(0),
                  tile(1), after(1),
                  pl.BlockSpec((nj, tm, fb), lambda i: (0, i, 0)),
                  pl.BlockSpec((nj, HALO, fb), lambda i: (0, jnp.minimum((i + 1) * per, nblk - 1), 0)),
                  pl.BlockSpec((nj, CONV_WIDTH, fb), lambda i: (0, 0, 0)), pl.BlockSpec((nj, 1, fb), lambda i: (0, 0, 0)),
                  pl.BlockSpec((2 * nj, fb, d), lambda i: (0, 0, 0)), row, pl.BlockSpec((1, d), lambda i: (0, 0)), row],
        out_specs=[pl.BlockSpec((2, nj, tm, fb), lambda i: (0, 0, i, 0)),
                   pl.BlockSpec((nj, CONV_WIDTH, fb), lambda i: (0, 0, 0)), pl.BlockSpec((nj, 1, fb), lambda i: (0, 0, 0)),
                   row, row, pl.BlockSpec((1, d), lambda i: (0, 0))],
        out_shape=[jax.ShapeDtypeStruct((2, nj, s, fb), BF16), jax.ShapeDtypeStruct((nj, CONV_WIDTH, fb), F32),
                   jax.ShapeDtypeStruct((nj, 1, fb), F32), jax.ShapeDtypeStruct((s, d), F32), jax.ShapeDtypeStruct((s, d), BF16),
                   jax.ShapeDtypeStruct((1, d), F32)],
        compiler_params=_params(("arbitrary",)),
    )(u, u, u, u, u, dh, dh, conv_w, conv_b, w_up, h, gain, dres)


ATT_TILE = 512


def _stack_heads(ref, rows, first_head, count):
    hd = ATT_HEAD_DIM
    return jnp.concatenate([ref[rows, (first_head + j) * hd:(first_head + j + 1) * hd] for j in range(count)], axis=0)


def _unstack_heads(stacked, ref, rows, first_head, count):
    hd = ATT_HEAD_DIM
    for pair in range(count // 2):
        both = [stacked[(2 * pair + j) * WINDOW:(2 * pair + j + 1) * WINDOW, :] for j in range(2)]
        ref[rows, (first_head + 2 * pair) * hd:(first_head + 2 * pair + 2) * hd] = jnp.concatenate(both, axis=1).astype(ref.dtype)


def _attn_probs_t(kb, qs, sink_ref, first_head, count, first, n_heads):
    lanes = count * WINDOW
    ik = lax.broadcasted_iota(jnp.int32, (2 * WINDOW, lanes), 0)
    iq = lax.broadcasted_iota(jnp.int32, (2 * WINDOW, lanes), 1) % WINDOW
    dist = iq + WINDOW - ik
    valid = (dist >= 0) & (dist < WINDOW) & (ik >= jnp.where(first, WINDOW, 0))
    per_head = lambda values: jnp.concatenate([jnp.zeros((1, WINDOW), F32) + v for v in values], axis=1)
    slope = per_head([2.0 ** (-8.0 * (first_head + j + 1) / n_heads) for j in range(count)])
    sink = per_head([sink_ref[0, first_head + j] for j in range(count)])
    sc = jnp.where(valid, _dot(kb, qs, NT) * ATT_HEAD_DIM ** -0.5 - slope * dist.astype(F32), NEG)
    m = jnp.maximum(jnp.max(sc, axis=0, keepdims=True), sink)
    e = jnp.exp(sc - m)
    es = jnp.exp(sink - m)
    inv = 1.0 / (jnp.sum(e, axis=0, keepdims=True) + es)
    return e * inv, es * inv


def _attn_specs(s, d, kvd, tq):
    per = tq // WINDOW
    return [pl.BlockSpec((tq, d), lambda i: (i, 0)), pl.BlockSpec((tq, kvd), lambda i: (i, 0)),
            pl.BlockSpec((WINDOW, kvd), lambda i: (jnp.maximum(i * per - 1, 0), 0))]


def _attn_fwd(q, kv, sinks):
    s, d = q.shape
    kvd = kv.shape[1]
    half = kvd // 2
    hd = ATT_HEAD_DIM
    nq = d // hd
    group = nq // ATT_KV_HEADS
    tq = min(s, ATT_TILE)
    per = tq // WINDOW

    def body(q_ref, kvc_ref, kvp_ref, sink_ref, o_ref, band):
        i = pl.program_id(0)
        band[0:WINDOW, :] = kvp_ref[...]
        band[WINDOW:, :] = kvc_ref[...]

        def block(b, carry):
            rows = pl.ds(pl.multiple_of(b * WINDOW, WINDOW), WINDOW)
            keys = pl.ds(pl.multiple_of(b * WINDOW, WINDOW), 2 * WINDOW)
            first = (i * per + b) == 0
            for g in range(ATT_KV_HEADS):
                p, _ = _attn_probs_t(band[keys, g * hd:(g + 1) * hd], _stack_heads(q_ref, rows, g * group, group), sink_ref,
                                     g * group, group, first, nq)
                out_t = _dot(band[keys, half + g * hd:half + (g + 1) * hd], p, TN)
                _unstack_heads(out_t.T, o_ref, rows, g * group, group)
            return carry

        lax.fori_loop(0, per, block, 0)

    return pl.pallas_call(
        body, name="attn_fwd", grid=(s // tq,),
        in_specs=_attn_specs(s, d, kvd, tq) + [pl.BlockSpec(memory_space=pltpu.SMEM)],
        out_specs=pl.BlockSpec((tq, d), lambda i: (i, 0)), out_shape=jax.ShapeDtypeStruct((s, d), BF16),
        scratch_shapes=[pltpu.VMEM((tq + WINDOW, kvd), BF16)], compiler_params=_params(("parallel",)),
    )(q, kv, kv, sinks)


def _attn_bwd(q, kv, o, do, sinks):
    s, d = q.shape
    kvd = kv.shape[1]
    half = kvd // 2
    hd = ATT_HEAD_DIM
    nq = d // hd
    group = nq // ATT_KV_HEADS
    tq = min(s, ATT_TILE)
    per = tq // WINDOW
    nt = s // tq

    def body(q_ref, kvc_ref, kvp_ref, o_ref, do_ref, sink_ref, dq_ref, dkvc_ref, dkvp_ref, ds_ref, band, dband):
        i = pl.program_id(0)
        band[0:WINDOW, :] = kvp_ref[...]
        band[WINDOW:, :] = kvc_ref[...]
        dband[...] = jnp.zeros_like(dband)
        ds_ref[...] = jnp.zeros_like(ds_ref)

        def block(b, carry):
            rows = pl.ds(pl.multiple_of(b * WINDOW, WINDOW), WINDOW)
            keys = pl.ds(pl.multiple_of(b * WINDOW, WINDOW), 2 * WINDOW)
            first = (i * per + b) == 0
            dks, dvs = [], []
            for g in range(ATT_KV_HEADS):
                kb = band[keys, g * hd:(g + 1) * hd]
                vb = band[keys, half + g * hd:half + (g + 1) * hd]
                qs = _stack_heads(q_ref, rows, g * group, group)
                dos = _stack_heads(do_ref, rows, g * group, group)
                p, ps = _attn_probs_t(kb, qs, sink_ref, g * group, group, first, nq)
                prod = dos.astype(F32) * _stack_heads(o_ref, rows, g * group, group).astype(F32)
                dsum = lax.dot_general(jnp.ones((8, hd), F32), prod, NT, precision=lax.Precision.HIGHEST,
                                       preferred_element_type=F32)[0:1, :]
                dsc = p * (_dot(vb, dos, NT) - dsum) * ATT_HEAD_DIM ** -0.5
                dvs.append(_dot(p, dos))
                dks.append(_dot(dsc, qs))
                _unstack_heads(_dot(kb, dsc, TN).T, dq_ref, rows, g * group, group)
                gone = ps * dsum
                for j in range(group):
                    ds_ref[g * group + j:g * group + j + 1, :] += jnp.zeros((1, 128), F32) - jnp.sum(gone[:, j * WINDOW:(j + 1) * WINDOW])
            dband[keys, 0:half] += jnp.concatenate(dks, axis=1)
            dband[keys, half:] += jnp.concatenate(dvs, axis=1)
            return carry

        lax.fori_loop(0, per, block, 0)
        dkvp_ref[...] = dband[0:WINDOW, :]
        dkvc_ref[...] = dband[WINDOW:, :]

    big = pl.BlockSpec((tq, d), lambda i: (i, 0))
    return pl.pallas_call(
        body, name="attn_bwd", grid=(nt,),
        in_specs=_attn_specs(s, d, kvd, tq) + [big, big, pl.BlockSpec(memory_space=pltpu.SMEM)],
        out_specs=[big, pl.BlockSpec((tq, kvd), lambda i: (i, 0)), pl.BlockSpec((None, WINDOW, kvd), lambda i: (i, 0, 0)),
                   pl.BlockSpec((None, nq, 128), lambda i: (i, 0, 0))],
        out_shape=[jax.ShapeDtypeStruct((s, d), BF16), jax.ShapeDtypeStruct((s, kvd), F32), jax.ShapeDtypeStruct((nt, WINDOW, kvd), F32),
                   jax.ShapeDtypeStruct((nt, nq, 128), F32)],
        scratch_shapes=[pltpu.VMEM((tq + WINDOW, kvd), BF16), pltpu.VMEM((tq + WINDOW, kvd), F32)],
        compiler_params=_params(("parallel",)),
    )(q, kv, kv, o, do, sinks)


HBM_SPEC = pl.BlockSpec(memory_space=pltpu.HBM)
VMEM_SPEC = pl.BlockSpec(memory_space=pltpu.VMEM)


def _place():
    return lax.axis_index("x"), lax.axis_index("y"), lax.axis_index("c")


def _flip(pos, r):
    return tuple(1 - p if (r >> (2 - a)) & 1 else p for a, p in enumerate(pos))


def _index(pos):
    return 4 * pos[0] + 2 * pos[1] + pos[2]


def _all_gather(name, shards, spec):
    n = len(shards)

    def body(*refs):
        x_refs, o_refs = refs[:n], refs[n:2 * n]
        send_sems, recv_sems, local_sems = refs[2 * n:]
        me = _place()
        sibling = _flip(me, 1)
        far = [_flip(me, r) for r in (4, 2, 6)]

        def copy(t, sem, block, to, src=None):
            rows = o_refs[t].at[_index(block)]
            return pltpu.make_async_remote_copy(
                src_ref=rows if src is None else src, dst_ref=rows, send_sem=send_sems.at[t, sem], recv_sem=recv_sems.at[t, sem],
                device_id=to, device_id_type=MESH)

        own = [pltpu.make_async_copy(x_refs[t], o_refs[t].at[_index(me)], local_sems.at[t]) for t in range(n)]
        for cp in own:
            cp.start()
        first = []
        for t in range(n):
            first.append(copy(t, 0, me, sibling, src=x_refs[t]))
            first += [copy(t, 1 + j, me, peer, src=x_refs[t]) for j, peer in enumerate(far)]
        for cp in first:
            cp.start()
        passed = []
        for j, peer in enumerate(far):
            for t in range(n):
                copy(t, 1 + j, peer, me).wait_recv()
                cp = copy(t, 4 + j, peer, sibling)
                cp.start()
                passed.append(cp)
        for t in range(n):
            copy(t, 0, sibling, me).wait_recv()
            for j, peer in enumerate(far):
                copy(t, 4 + j, _flip(peer, 1), me).wait_recv()
        for cp in first + passed:
            cp.wait_send()
        for cp in own:
            cp.wait()

    return pl.pallas_call(
        body, name=name, in_specs=[spec] * n, out_specs=[spec] * n,
        out_shape=[jax.ShapeDtypeStruct((N_DEV,) + sh.shape, sh.dtype) for sh in shards],
        scratch_shapes=[pltpu.SemaphoreType.DMA((n, 7)), pltpu.SemaphoreType.DMA((n, 7)), pltpu.SemaphoreType.DMA((n,))],
    )(*shards)


SEM_SPEC = pl.BlockSpec(memory_space=pltpu.SEMAPHORE)
ANY_SPEC = pl.BlockSpec(memory_space=pl.ANY)


def _landing(own, mine):
    return lax.dynamic_update_slice(lax.empty((N_DEV,) + own.shape, own.dtype), own[None], (mine,) + (0,) * own.ndim)


def _pinned(a, token):
    return a + token[0:1, 0:1].astype(a.dtype)


def _peer_copies(src_refs, land_refs, send_sems, recv_sems, scatter, arrivals):
    me = _place()
    mine = _index(me)
    copies = []
    for t, (src, land) in enumerate(zip(src_refs, land_refs)):
        for r in range(1, N_DEV):
            peer = _flip(me, r)
            theirs = _index(peer)
            sem = t * (N_DEV - 1) + r - 1
            copies.append(pltpu.make_async_remote_copy(
                src_ref=src.at[theirs] if scatter else src, dst_ref=land.at[theirs if arrivals else mine],
                send_sem=send_sems.at[sem], recv_sem=recv_sems.at[sem], device_id=peer, device_id_type=MESH))
    return copies


def _send_start(name, sources, lands, scatter, after=None):
    n = len(sources)
    extra = 0 if after is None else 1

    def body(*refs):
        outs = refs[2 * n + extra:]
        for out in _peer_copies(refs[:n], refs[n:2 * n], outs[0], outs[1], scatter, False):
            out.start()
        outs[-1][...] = jnp.zeros_like(outs[-1])

    outs = pl.pallas_call(
        body, name=name, in_specs=[HBM_SPEC] * (2 * n) + [ANY_SPEC] * extra,
        out_specs=[SEM_SPEC, SEM_SPEC] + [HBM_SPEC] * (2 * n) + [VMEM_SPEC],
        out_shape=[pltpu.SemaphoreType.DMA((n * (N_DEV - 1),)), pltpu.SemaphoreType.DMA((n * (N_DEV - 1),))]
        + [pltpu.HBM(a.shape, a.dtype) for a in list(sources) + list(lands)] + [jax.ShapeDtypeStruct((8, 128), F32)],
        input_output_aliases={i: 2 + i for i in range(2 * n)},
        compiler_params=pltpu.CompilerParams(has_side_effects=pltpu.SideEffectType.DATAFLOW_SIDE_EFFECTING),
    )(*[pltpu.with_memory_space_constraint(a, pltpu.HBM) for a in list(sources) + list(lands)], *([] if after is None else [after]))
    return outs[0], outs[1], outs[2:2 + n], outs[2 + n:2 + 2 * n], outs[-1]


def _send_wait(name, started, after, scatter):
    send_sems, recv_sems, sources, lands, _ = started
    n = len(sources)

    def body(*refs):
        for out in _peer_copies(refs[:n], refs[n:2 * n], refs[2 * n], refs[2 * n + 1], scatter, False):
            out.wait_send()
        for arrival in _peer_copies(refs[:n], refs[n:2 * n], refs[2 * n], refs[2 * n + 1], scatter, True):
            arrival.wait_recv()

    outs = pl.pallas_call(
        body, name=name, in_specs=[HBM_SPEC] * (2 * n) + [SEM_SPEC, SEM_SPEC, ANY_SPEC], out_specs=[HBM_SPEC] * (2 * n),
        out_shape=[pltpu.HBM(a.shape, a.dtype) for a in list(sources) + list(lands)],
        input_output_aliases={i: i for i in range(2 * n)},
        compiler_params=pltpu.CompilerParams(has_side_effects=pltpu.SideEffectType.DATAFLOW_SIDE_EFFECTING),
    )(*sources, *lands, send_sems, recv_sems, after)
    return outs[n:]


def _pack_rows(parts):
    offsets, row = [], 0
    for part in parts:
        offsets.append(row)
        row += part.shape[0]
    return offsets, -(-row // 8) * 8, -(-max(part.shape[1] for part in parts) // 128) * 128


def _pack(name, parts):
    offsets, rows, width = _pack_rows(parts)

    def body(*refs):
        o_ref = refs[-1]
        o_ref[...] = jnp.zeros_like(o_ref)
        for off, ref in zip(offsets, refs[:-1]):
            o_ref[off:off + ref.shape[0], 0:ref.shape[1]] = ref[...]

    return pl.pallas_call(body, name=name, in_specs=[VMEM_SPEC] * len(parts), out_specs=VMEM_SPEC,
                          out_shape=jax.ShapeDtypeStruct((rows, width), F32))(*parts)


def _adamw_math(w, g, m, v):
    m = ADAM_B1 * m + (1.0 - ADAM_B1) * g
    v = ADAM_B2 * v + (1.0 - ADAM_B2) * (g * g)
    m_hat = m * (1.0 / (1.0 - ADAM_B1 ** ADAM_STEP))
    denom = jnp.sqrt(v * (1.0 / (1.0 - ADAM_B2 ** ADAM_STEP))) + ADAM_EPS
    inv = pl.reciprocal(denom, approx=True)
    inv = inv * (2.0 - denom * inv)
    return -ADAM_LR * (m_hat * inv + ADAM_WD * w), m, v


def _adamw_step(w_ref, m_ref, v_ref, p_ref, g_ref, d_ref, nm_ref, nv_ref):
    g = p_ref[0].astype(F32)
    for dev in range(1, N_DEV):
        g = g + p_ref[dev].astype(F32)
    g_ref[...] = g
    d_ref[...], nm_ref[...], nv_ref[...] = _adamw_math(w_ref[...], g, m_ref[...], v_ref[...])


def _adamw_rows(rows):
    return max(t for t in range(8, min(rows, 256) + 1, 8) if rows % t == 0)


def _adamw_shard(name, w, m, v, partials):
    rows, cols = w.shape
    tr = _adamw_rows(rows)
    blk = pl.BlockSpec((tr, cols), lambda i: (i, 0))
    return pl.pallas_call(
        _adamw_step_fn(), name=name, grid=(rows // tr,), in_specs=[blk, blk, blk, pl.BlockSpec((N_DEV, tr, cols), lambda i: (0, i, 0))],
        out_specs=[blk] * 4, out_shape=[jax.ShapeDtypeStruct((rows, cols), F32)] * 4, compiler_params=_params(("parallel",)),
    )(w, m, v, partials)


def _adamw_step_fn():
    return functools.partial(_adamw_step)


def _adamw_layers(name, w, m, v, partials):
    layers, rows, cols = w.shape
    tr = _adamw_rows(rows)
    last = rows // tr - 1

    def body(w_ref, m_ref, v_ref, *rest):
        for layer in range(layers):
            @pl.when(pl.program_id(0) == layer)
            def _():
                _adamw_step(w_ref, m_ref, v_ref, rest[layer], *rest[layers:])

    blk = pl.BlockSpec((None, tr, cols), lambda l, i: (l, i, 0))
    part = lambda layer: pl.BlockSpec((N_DEV, tr, cols), lambda l, i: (0, jnp.where(l == layer, i, jnp.where(l < layer, 0, last)), 0))
    return pl.pallas_call(
        body, name=name, grid=(layers, rows // tr), in_specs=[blk, blk, blk] + [part(layer) for layer in range(layers)],
        out_specs=[blk] * 4, out_shape=[jax.ShapeDtypeStruct(w.shape, F32)] * 4, compiler_params=_params(("arbitrary", "arbitrary")),
    )(w, m, v, *partials)


def _adamw_small(gathered, places, entries):
    n = len(entries)
    np_ = len(gathered)

    def body(*refs):
        pack_refs = refs[:np_]
        refs = refs[np_ - 1:]
        w_refs, m_refs, v_refs = refs[1:1 + n], refs[1 + n:1 + 2 * n], refs[1 + 2 * n:1 + 3 * n]
        outs = refs[1 + 3 * n:]
        totals = []
        for pack_ref in pack_refs:
            acc = pack_ref[0]
            for dev in range(1, N_DEV):
                acc = acc + pack_ref[dev]
            totals.append(acc)
        mine = _index(_place())
        for e in range(n):
            rows, cols = w_refs[e].shape
            total, off = totals[places[e][0]], places[e][1]
            if entries[e][3]:
                g = jnp.zeros((rows, cols), F32)
                for dev in range(N_DEV):
                    g = g + jnp.where(mine == dev, total[off + dev * rows:off + (dev + 1) * rows, 0:cols], 0.0)
            else:
                g = total[off:off + rows, 0:cols]
            outs[4 * e][...] = g
            outs[4 * e + 1][...], outs[4 * e + 2][...], outs[4 * e + 3][...] = _adamw_math(w_refs[e][...], g, m_refs[e][...], v_refs[e][...])
        outs[4 * n][...] = totals[places[n][0]][places[n][1]:places[n][1] + 1, 0:128]

    shapes = []
    for w, _, _, _ in entries:
        shapes += [jax.ShapeDtypeStruct(w.shape, F32)] * 4
    shapes.append(jax.ShapeDtypeStruct((1, 128), F32))
    return pl.pallas_call(
        body, name="adamw_small", in_specs=[VMEM_SPEC] * (np_ + 3 * n), out_specs=[VMEM_SPEC] * len(shapes), out_shape=shapes,
        compiler_params=pltpu.CompilerParams(vmem_limit_bytes=VMEM_LIMIT),
    )(*gathered, *[e[0] for e in entries], *[e[1] for e in entries], *[e[2] for e in entries])


def _ffn_forward(tag, h, gain, w_up, late):
    s, d = h.shape
    fb = w_up.shape[1]
    tm = _row_tile(s, 2 * MM_ROWS)
    a, = _rmsnorm_cast(f"ffn_norm_{tag}", h, [gain])
    u = _matmul(
        f"ffn_up_{tag}", a, w_up, dims=NT, grid=(s // tm, N_DEV, 1),
        a_spec=pl.BlockSpec((tm, d), lambda i, j, k: (i, 0)),
        b_spec=pl.BlockSpec((None, fb, d), lambda i, j, k: (j, 0, 0)),
        o_spec=pl.BlockSpec((None, None, tm, fb), lambda i, j, k: (j // 4, j % 4, i, 0)),
        out_shape=jax.ShapeDtypeStruct((2, 4, s, fb), BF16))
    w_down, conv_w, conv_b = late(u)
    hidden, out = _ffn_hidden_down(f"ffn_hidden_down_{tag}", u, conv_w, conv_b, w_down, h)
    return out, (a, u, hidden)


def _ffn_backward(tag, h, gain, w_up, w_down, conv_w, conv_b, saved, dout):
    a, u, hidden = saved
    dout, dout_bf = dout
    s, d = h.shape
    fb = w_up.shape[1]
    tm = _row_tile(s, MM_ROWS)
    dhidden = _matmul(
        f"ffn_down_bwd_{tag}", dout_bf, w_down, dims=NT, grid=(s // tm, 4, 1),
        a_spec=pl.BlockSpec((tm, d), lambda i, j, k: (i, 0)),
        b_spec=pl.BlockSpec((None, fb, d), lambda i, j, k: (j, 0, 0)),
        o_spec=pl.BlockSpec((None, tm, fb), lambda i, j, k: (j, i, 0)),
        out_shape=jax.ShapeDtypeStruct((4, s, fb), BF16))
    dw_down = _matmul(
        f"ffn_down_grad_{tag}", hidden, dout_bf, dims=TN, grid=(4, 1, 1),
        a_spec=pl.BlockSpec((None, s, fb), lambda i, j, k: (i, 0, 0)),
        b_spec=pl.BlockSpec((s, d), lambda i, j, k: (0, 0)),
        o_spec=pl.BlockSpec((None, fb, d), lambda i, j, k: (i, 0, 0)),
        out_shape=jax.ShapeDtypeStruct((4, fb, d), BF16))
    du, dconv_w, dconv_b, dh, dh_bf, dgain = _ffn_hidden_up_bwd(f"ffn_hidden_up_bwd_{tag}", u, dhidden, conv_w, conv_b, w_up, h, gain, dout)
    dw_up = _matmul(
        f"ffn_up_grad_{tag}", du, a, dims=TN, grid=(N_DEV, 1, 1),
        a_spec=pl.BlockSpec((None, None, s, fb), lambda i, j, k: (i // 4, i % 4, 0, 0)),
        b_spec=pl.BlockSpec((s, d), lambda i, j, k: (0, 0)),
        o_spec=pl.BlockSpec((None, fb, d), lambda i, j, k: (i, 0, 0)),
        out_shape=jax.ShapeDtypeStruct((N_DEV, fb, d), BF16))
    return (dh, dh_bf), dgain, dw_up, dw_down, dconv_w, dconv_b


def kernel(x, hg_norm, hg_w_in, hg_lb_logits, hg_out_norm, hg_w_out, kv_norm, w_kv, attn_norm, attn_w_q, attn_sinks, attn_w_o, ffn_norm, ffn_w_up, ffn_conv_w, ffn_conv_b, ffn_w_down, final_norm, loss_target, m_hg_norm, m_hg_w_in, m_hg_lb_logits, m_hg_out_norm, m_hg_w_out, m_kv_norm, m_w_kv, m_attn_norm, m_attn_w_q, m_attn_sinks, m_attn_w_o, m_ffn_norm, m_ffn_w_up, m_ffn_conv_w, m_ffn_conv_b, m_ffn_w_down, m_final_norm, v_hg_norm, v_hg_w_in, v_hg_lb_logits, v_hg_out_norm, v_hg_w_out, v_kv_norm, v_w_kv, v_attn_norm, v_attn_w_q, v_attn_sinks, v_attn_w_o, v_ffn_norm, v_ffn_w_up, v_ffn_conv_w, v_ffn_conv_b, v_ffn_w_down, v_final_norm):
    _, s, d = x.shape
    x0, target = x[0], loss_target[0]
    half = hg_w_in.shape[2]
    fs = ffn_conv_w.shape[2]
    fb = 2 * fs
    kvd = w_kv.shape[1]
    nq = d // ATT_HEAD_DIM
    tm = _row_tile(s, MM_ROWS)

    mine = _index(_place())
    gather = lambda tag, shards, after: _send_start("gather_start_" + tag, shards, [_landing(a, mine) for a in shards], False, after)
    w_in, g_hgn, g_lbl, w_out = _all_gather("gather_hg", [hg_w_in[0].astype(BF16), hg_norm, hg_lb_logits, hg_w_out[0].astype(BF16)], HBM_SPEC)
    w_out = w_out.reshape(d, d)
    up_t = lambda a: jnp.swapaxes(a, -1, -2)
    coming_up0 = gather("ffn_up0", [up_t(ffn_w_up[0]).astype(BF16)], g_hgn)
    hgn = _pinned(g_hgn.reshape(1, d), coming_up0[4])
    lbl = g_lbl.transpose(1, 0, 2).reshape(2, d)
    conv_b = [ffn_conv_b[layer].reshape(4, 1, fb) for layer in range(2)]
    gains = [ffn_norm[0:1], ffn_norm[1:2]]
    kvn, fin = kv_norm.reshape(1, d), final_norm.reshape(1, d)

    a0, = _rmsnorm_cast("hg_norm", x0, [hgn])
    t2 = _row_tile(s, 2 * MM_ROWS)
    p = _matmul(
        "hg_in", a0, w_in, dims=NN, grid=(s // t2, N_DEV, 1),
        a_spec=pl.BlockSpec((t2, d), lambda i, j, k: (i, 0)),
        b_spec=pl.BlockSpec((None, d, half), lambda i, j, k: (j, 0, 0)),
        o_spec=pl.BlockSpec((None, t2, half), lambda i, j, k: (j // 2, i, j % 2)),
        out_shape=jax.ShapeDtypeStruct((4, s, d), BF16), acc_shape=(8, 128))
    o, og, states = _hgrn2_fwd(p, lbl, hg_out_norm)
    coming_dn0 = gather("ffn_down0", [ffn_conv_w, ffn_w_down[0].astype(BF16)], o)
    x1 = _mm_rows("hg_out", og, _pinned(w_out, coming_dn0[4]), out_dtype=F32, add=x0)
    w_up0, = _send_wait("gather_wait_ffn_up0", coming_up0, x1, False)
    coming_attn = gather("attn", [w_kv.astype(BF16), attn_w_q[0].astype(BF16), attn_w_o[0].astype(BF16)], w_up0)
    gains[0] = _pinned(gains[0], coming_attn[4])
    w_up, w_dn, conv_w, coming = [w_up0, None], [None, None], [], {}

    def late0(u):
        g_cw, w_dn0 = _send_wait("gather_wait_ffn_down0", coming_dn0, u, False)
        w_dn[0] = w_dn0.reshape(4, fb, d)
        conv_w.extend(g_cw[:, layer].reshape(4, 2, CONV_WIDTH, fs).transpose(0, 2, 1, 3).reshape(4, CONV_WIDTH, fb) for layer in range(2))
        coming["up1"] = gather("ffn_up1", [up_t(ffn_w_up[1]).astype(BF16)], w_dn0)
        return w_dn[0], conv_w[0], _pinned(conv_b[0], coming["up1"][4])

    x2, saved0 = _ffn_forward("0", x1, gains[0], w_up[0], late0)
    w_kvg, w_q, w_o = _send_wait("gather_wait_attn", coming_attn, x2, False)
    w_kvg, w_q, w_o = w_kvg.reshape(d, kvd), w_q.reshape(d, d), w_o.reshape(d, d)
    akv, a2 = _rmsnorm_cast("attn_norms", x2, [kvn, attn_norm])
    kv = _mm_rows("kv_proj", akv, w_kvg, out_dtype=BF16)
    q = _mm_rows("q_proj", a2, w_q, out_dtype=BF16)
    att = _attn_fwd(q, kv, attn_sinks)
    coming_dn1 = gather("ffn_down1", [ffn_w_down[1].astype(BF16)], att)
    x3 = _mm_rows("attn_out", att, _pinned(w_o, coming_dn1[4]), out_dtype=F32, add=x2)
    w_up[1], = _send_wait("gather_wait_ffn_up1", coming["up1"], x3, False)

    def late1(u):
        w_dn[1] = _send_wait("gather_wait_ffn_down1", coming_dn1, u, False)[0].reshape(4, fb, d)
        return w_dn[1], conv_w[1], conv_b[1]

    x4, saved1 = _ffn_forward("1", x3, gains[1], w_up[1], late1)
    dx4, dx4_bf, d_fin, loss_part = _loss_head(x4, fin, target)

    dx3, d_fn1, dw_up1, dw_dn1, dcw1, dcb1 = _ffn_backward("1", x3, gains[1], w_up[1], w_dn[1], conv_w[1], conv_b[1], saved1, (dx4, dx4_bf))
    rows = d // N_DEV
    scatter = lambda tag, stacks: _send_start("scatter_start_" + tag, stacks, [_landing(lax.dynamic_index_in_dim(a, mine, keepdims=False), mine) for a in stacks], True)
    going_ffn1 = scatter("ffn1", [dw_up1, dw_dn1.reshape(N_DEV, fs, d)])
    datt = _mm_rows_nt("attn_out_bwd", dx3[1], w_o, out_dtype=BF16)
    dw_o = _mm_tn("attn_out_grad", att, dx3[1])
    dq, dkv_own, dkv_before, dsink = _attn_bwd(q, kv, att, datt, _pinned(attn_sinks, going_ffn1[4]))
    tiles = dkv_before.shape[0]
    dkv = dkv_own.reshape(tiles, s // tiles, kvd)
    dkv = jnp.concatenate([dkv[:, :-WINDOW], dkv[:, -WINDOW:] + jnp.pad(dkv_before[1:], ((0, 1), (0, 0), (0, 0)))], axis=1).reshape(s, kvd)
    dw_q = _mm_tn("q_proj_grad", a2, dq)
    dw_kv = _mm_tn("kv_proj_grad", akv, dkv)
    going_attn = scatter("attn", [dw_kv.reshape(N_DEV, rows, kvd), dw_q.reshape(N_DEV, rows, d), dw_o.reshape(N_DEV, rows, d)])
    whole = lambda a_ref, b_ref: [(a_ref[...], b_ref[...])]
    rows_of = lambda width: (lambda tile: pl.BlockSpec((tile, width), lambda i: (i, 0)))
    dx2, (d_kvn, d_attn) = _proj_norm_bwd("attn_in_bwd", x2, dx3[0], [
        (dkv, rows_of(kvd), w_kvg, pl.BlockSpec((d, kvd), lambda i: (0, 0)), whole, _pinned(kvn, going_attn[4])),
        (dq, rows_of(d), w_q, pl.BlockSpec((d, d), lambda i: (0, 0)), whole, attn_norm)])
    dx1, d_fn0, dw_up0, dw_dn0, dcw0, dcb0 = _ffn_backward("0", x1, gains[0], w_up[0], w_dn[0], conv_w[0], conv_b[0], saved0, dx2)
    dw_out = _mm_tn("hg_out_grad", og, dx1[1])
    going_ffn0 = scatter("ffn0", [dw_up0, dw_dn0.reshape(N_DEV, fs, d), dw_out.reshape(N_DEV, rows, d)])
    dog = _mm_rows_nt("hg_out_bwd", dx1[1], w_out, out_dtype=F32)
    dp, d_lbl, d_ogain = _hgrn2_bwd(p, lbl, _pinned(hg_out_norm, going_ffn0[4]), o, dog, states)
    dw_in = _matmul(
        "hg_in_grad", a0, dp, dims=TN, grid=(1, N_DEV, 1),
        a_spec=pl.BlockSpec((s, d), lambda i, j, k: (0, 0)),
        b_spec=pl.BlockSpec((None, s, half), lambda i, j, k: (j // 2, 0, j % 2)),
        o_spec=pl.BlockSpec((None, d, half), lambda i, j, k: (j, 0, 0)),
        out_shape=jax.ShapeDtypeStruct((N_DEV, d, half), BF16))
    going_hg = scatter("hg", [dw_in])
    (dx0, _), (d_hgn,) = _proj_norm_bwd("hg_in_bwd", x0, dx1[0], [
        (dp, lambda tile: pl.BlockSpec((4, tile, d), lambda i: (0, i, 0)), w_in, pl.BlockSpec((N_DEV, d, half), lambda i: (0, 0, 0)),
         lambda g_ref, w_ref: [(g_ref[k // 2, :, (k % 2) * half:(k % 2 + 1) * half], w_ref[k]) for k in range(N_DEV)],
         _pinned(hgn, going_hg[4]))])

    as_blocks = lambda a, r: a.reshape(r, N_DEV, -1).transpose(1, 0, 2).reshape(N_DEV * r, -1)
    d_cw = jnp.concatenate([g.transpose(1, 0, 2).reshape(CONV_WIDTH, 4 * fb) for g in (dcw0, dcw1)], axis=0)
    parts = [d_fin, jnp.concatenate([d_fn0, d_fn1], axis=0), jnp.concatenate([dcb0.reshape(1, 4 * fb), dcb1.reshape(1, 4 * fb)], axis=0),
             as_blocks(d_cw, 2 * CONV_WIDTH), d_attn, jnp.sum(dsink[:, :, 0], axis=0).reshape(1, nq), d_kvn, d_ogain,
             as_blocks(d_hgn, 1), as_blocks(d_lbl, 2), loss_part]
    wide = [2]
    packs = [[parts[i] for i in wide], [part for i, part in enumerate(parts) if i not in wide]]
    places = [None] * len(parts)
    for which, members in enumerate([wide, [i for i in range(len(parts)) if i not in wide]]):
        for i, off in zip(members, _pack_rows(packs[which])[0]):
            places[i] = (which, off)
    packed = [_pack("pack_wide_grads", packs[0]), _pack("pack_narrow_grads", packs[1])]
    going_small = _send_start("small_grads_start", packed, [_landing(a, mine) for a in packed], False)

    arrive = lambda tag, going, after: _send_wait("scatter_wait_" + tag, going, after, True)
    (l_up1, l_dn1), (l_kv, l_q, l_o), (l_up0, l_dn0, l_out) = (
        arrive("ffn1", going_ffn1, going_small[4]), arrive("attn", going_attn, going_small[4]), arrive("ffn0", going_ffn0, going_small[4]))
    big = {}
    for tag, w, m, v, part in [
            ("w_kv", w_kv, m_w_kv, v_w_kv, l_kv), ("attn_w_q", attn_w_q[0], m_attn_w_q[0], v_attn_w_q[0], l_q),
            ("attn_w_o", attn_w_o[0], m_attn_w_o[0], v_attn_w_o[0], l_o)]:
        big[tag] = _adamw_shard("adamw_" + tag, w, m, v, part)
    big["ffn_w_up"] = [up_t(a) for a in _adamw_layers("adamw_ffn_w_up", up_t(ffn_w_up), up_t(m_ffn_w_up), up_t(v_ffn_w_up), (l_up0, l_up1))]
    big["ffn_w_down"] = _adamw_layers("adamw_ffn_w_down", ffn_w_down, m_ffn_w_down, v_ffn_w_down, (l_dn0, l_dn1))
    lead = lambda tag: [a[None] for a in big[tag]]

    both_done = big["ffn_w_up"][0][0, 0:1, 0:1] + big["ffn_w_down"][0][0, 0:1, 0:1]
    gathered = _send_wait("small_grads_wait", going_small, both_done, False)
    two = lambda a: a.reshape(-1, a.shape[-1])
    small = [(fin, m_final_norm.reshape(1, d), v_final_norm.reshape(1, d), False), (ffn_norm, m_ffn_norm, v_ffn_norm, False),
             (ffn_conv_b, m_ffn_conv_b, v_ffn_conv_b, False), (two(ffn_conv_w), two(m_ffn_conv_w), two(v_ffn_conv_w), True),
             (attn_norm, m_attn_norm, v_attn_norm, False), (attn_sinks, m_attn_sinks, v_attn_sinks, False),
             (kvn, m_kv_norm.reshape(1, d), v_kv_norm.reshape(1, d), False), (hg_out_norm, m_hg_out_norm, v_hg_out_norm, False),
             (hg_norm, m_hg_norm, v_hg_norm, True), (hg_lb_logits, m_hg_lb_logits, v_hg_lb_logits, True)]
    res = _adamw_small(gathered, places, small)
    l_in, = arrive("hg", going_hg, gathered[1])
    big["hg_w_in"] = _adamw_shard("adamw_hg_w_in", hg_w_in[0], m_hg_w_in[0], v_hg_w_in[0], l_in)
    big["hg_w_out"] = _adamw_shard("adamw_hg_w_out", hg_w_out[0], m_hg_w_out[0], v_hg_w_out[0], l_out)
    names = ["final_norm", "ffn_norm", "ffn_conv_b", "ffn_conv_w", "attn_norm", "attn_sinks", "kv_norm", "hg_out_norm", "hg_norm", "hg_lb_logits"]
    shapes = {"final_norm": final_norm.shape, "kv_norm": kv_norm.shape, "ffn_conv_w": ffn_conv_w.shape}
    out = {n: [a.reshape(shapes[n]) if n in shapes else a for a in res[4 * i:4 * i + 4]] for i, n in enumerate(names)}
    out.update(hg_w_in=lead("hg_w_in"), hg_w_out=lead("hg_w_out"), w_kv=big["w_kv"], attn_w_q=lead("attn_w_q"), attn_w_o=lead("attn_w_o"),
               ffn_w_up=big["ffn_w_up"], ffn_w_down=big["ffn_w_down"])
    order = ["hg_norm", "hg_w_in", "hg_lb_logits", "hg_out_norm", "hg_w_out", "kv_norm", "w_kv", "attn_norm", "attn_w_q", "attn_sinks",
             "attn_w_o", "ffn_norm", "ffn_w_up", "ffn_conv_w", "ffn_conv_b", "ffn_w_down", "final_norm"]
    loss = res[-1][0, 0]
    return (loss, dx0[None], *[out[n][0] for n in order], *[out[n][1] for n in order], *[out[n][2] for n in order], *[out[n][3] for n in order])
```

```python
import functools

import jax
import jax.numpy as jnp
from jax import lax
from jax.experimental import pallas as pl
from jax.experimental.pallas import tpu as pltpu

F32 = jnp.float32
BF16 = jnp.bfloat16

EPS = 1e-6
HG_EXPAND = 128
HG_CHUNK = 32
ATT_HEAD_DIM = 64
ATT_KV_HEADS = 2
WINDOW = 128
CONV_WIDTH = 3
ADAM_LR = 0.001
ADAM_B1 = 0.9
ADAM_B2 = 0.999
ADAM_EPS = 1e-08
ADAM_WD = 0.01
ADAM_STEP = 10

N_DEV = 8
VMEM_LIMIT = 48 * 1024 * 1024
NEG = -1e30

NN = (((1,), (0,)), ((), ()))
NT = (((1,), (1,)), ((), ()))
TN = (((0,), (0,)), ((), ()))
MESH = pl.DeviceIdType.MESH


def _dot(a, b, dims=NN):
    return lax.dot_general(a.astype(BF16), b.astype(BF16), dims, preferred_element_type=F32)


def _sigmoid(x):
    return 0.5 * jnp.tanh(0.5 * x) + 0.5


def _silu(x):
    return x * _sigmoid(x)


def _silu_and_grad(x):
    s = _sigmoid(x)
    return x * s, s * (1.0 + x * (1.0 - s))


def _dsilu(x):
    return _silu_and_grad(x)[1]


def _params(semantics):
    return pltpu.CompilerParams(dimension_semantics=semantics, vmem_limit_bytes=VMEM_LIMIT)


def _row_tile(rows, want=512):
    return min(rows, want)


MM_ROWS = 1024


def _matmul(name, a, b, *, dims, grid, a_spec, b_spec, o_spec, out_shape, acc_shape=(8, 128), add=None, add_spec=None, terms=None):
    nk = grid[2]

    def body(*refs):
        if add is None:
            a_ref, b_ref, o_ref, acc = refs
        else:
            a_ref, b_ref, add_ref, o_ref, acc = refs
        k = pl.program_id(2)
        pairs = [(a_ref[...], b_ref[...])] if terms is None else terms(a_ref, b_ref)
        part = _dot(*pairs[0], dims)
        for pair in pairs[1:]:
            part = part + _dot(*pair, dims)

        def finish(total):
            if add is not None:
                total = total + add_ref[...]
            o_ref[...] = total.astype(o_ref.dtype)

        if nk == 1:
            finish(part)
        else:
            @pl.when(k == 0)
            def _():
                acc[...] = part

            @pl.when(k > 0)
            def _():
                acc[...] += part

            @pl.when(k == nk - 1)
            def _():
                finish(acc[...])

    in_specs = [a_spec, b_spec] + ([] if add is None else [add_spec])
    args = (a, b) + (() if add is None else (add,))
    return pl.pallas_call(
        body, name=name, grid=grid, in_specs=in_specs, out_specs=o_spec, out_shape=out_shape,
        scratch_shapes=[pltpu.VMEM(acc_shape, F32)],
        compiler_params=_params(("parallel", "parallel", "arbitrary")),
    )(*args)


def _mm_rows(name, a, w, *, out_dtype, add=None):
    s, kdim = a.shape
    n = w.shape[1]
    tm = _row_tile(s, MM_ROWS)
    return _matmul(
        name, a, w, dims=NN, grid=(s // tm, 1, 1),
        a_spec=pl.BlockSpec((tm, kdim), lambda i, j, k: (i, 0)),
        b_spec=pl.BlockSpec((kdim, n), lambda i, j, k: (0, 0)),
        o_spec=pl.BlockSpec((tm, n), lambda i, j, k: (i, 0)),
        out_shape=jax.ShapeDtypeStruct((s, n), out_dtype), acc_shape=(8, 128),
        add=add, add_spec=None if add is None else pl.BlockSpec((tm, n), lambda i, j, k: (i, 0)),
    )


def _mm_rows_nt(name, a, w, *, out_dtype):
    s, n = a.shape
    kdim = w.shape[0]
    tm = _row_tile(s, MM_ROWS)
    return _matmul(
        name, a, w, dims=NT, grid=(s // tm, 1, 1),
        a_spec=pl.BlockSpec((tm, n), lambda i, j, k: (i, 0)),
        b_spec=pl.BlockSpec((kdim, n), lambda i, j, k: (0, 0)),
        o_spec=pl.BlockSpec((tm, kdim), lambda i, j, k: (i, 0)),
        out_shape=jax.ShapeDtypeStruct((s, kdim), out_dtype), acc_shape=(8, 128),
    )


def _mm_tn(name, a, g):
    s, m = a.shape
    n = g.shape[1]
    tn = min(n, 512)
    return _matmul(
        name, a, g, dims=TN, grid=(1, n // tn, 1),
        a_spec=pl.BlockSpec((s, m), lambda i, j, k: (0, 0)),
        b_spec=pl.BlockSpec((s, tn), lambda i, j, k: (0, j)),
        o_spec=pl.BlockSpec((m, tn), lambda i, j, k: (0, j)),
        out_shape=jax.ShapeDtypeStruct((m, n), BF16),
    )


def _rmsnorm_cast(name, h, gains):
    s, d = h.shape
    tm = _row_tile(s)
    n = len(gains)

    def body(*refs):
        h_ref, g_refs, o_refs = refs[0], refs[1:1 + n], refs[1 + n:]
        xv = h_ref[...]
        xhat = xv * lax.rsqrt(jnp.mean(xv * xv, axis=-1, keepdims=True) + EPS)
        for g_ref, o_ref in zip(g_refs, o_refs):
            o_ref[...] = (xhat * g_ref[...]).astype(BF16)

    row = pl.BlockSpec((tm, d), lambda i: (i, 0))
    vec = pl.BlockSpec((1, d), lambda i: (0, 0))
    return pl.pallas_call(
        body, name=name, grid=(s // tm,), in_specs=[row] + [vec] * n, out_specs=[row] * n,
        out_shape=[jax.ShapeDtypeStruct((s, d), BF16)] * n, compiler_params=_params(("parallel",)),
    )(h, *gains)


def _proj_norm_bwd(name, h, dres, branches):
    s, d = h.shape
    tm = _row_tile(s)
    n = len(branches)

    def body(*refs):
        h_ref, dres_ref = refs[0], refs[1]
        g_refs, w_refs, gain_refs = refs[2:2 + n], refs[2 + n:2 + 2 * n], refs[2 + 2 * n:2 + 3 * n]
        dh_ref, dhb_ref, dg_refs = refs[2 + 3 * n], refs[3 + 3 * n], refs[4 + 3 * n:]
        i = pl.program_id(0)
        xv = h_ref[...]
        r = lax.rsqrt(jnp.mean(xv * xv, axis=-1, keepdims=True) + EPS)
        xhat = xv * r
        total = dres_ref[...]
        for branch, g_ref, w_ref, gain_ref, dg_ref in zip(branches, g_refs, w_refs, gain_refs, dg_refs):
            pairs = branch[4](g_ref, w_ref)
            da = _dot(*pairs[0], NT)
            for pair in pairs[1:]:
                da = da + _dot(*pair, NT)
            dgain = jnp.sum(da * xhat, axis=0, keepdims=True)

            @pl.when(i == 0)
            def _():
                dg_ref[...] = dgain

            @pl.when(i > 0)
            def _():
                dg_ref[...] += dgain

            dxhat = da * gain_ref[...]
            total = total + r * (dxhat - xhat * jnp.mean(dxhat * xhat, axis=-1, keepdims=True))
        dh_ref[...] = total
        dhb_ref[...] = total.astype(BF16)

    row = pl.BlockSpec((tm, d), lambda i: (i, 0))
    vec = pl.BlockSpec((1, d), lambda i: (0, 0))
    outs = pl.pallas_call(
        body, name=name, grid=(s // tm,),
        in_specs=[row, row] + [b[1](tm) for b in branches] + [b[3] for b in branches] + [vec] * n, out_specs=[row, row] + [vec] * n,
        out_shape=[jax.ShapeDtypeStruct((s, d), F32), jax.ShapeDtypeStruct((s, d), BF16)] + [jax.ShapeDtypeStruct((1, d), F32)] * n,
        compiler_params=_params(("arbitrary",)),
    )(h, dres, *[b[0] for b in branches], *[b[2] for b in branches], *[b[5] for b in branches])
    return (outs[0], outs[1]), outs[2:]


def _loss_head(h, gain, target):
    s, d = h.shape
    tm = _row_tile(s)

    def body(h_ref, g_ref, t_ref, dh_ref, dhb_ref, dg_ref, loss_ref):
        i = pl.program_id(0)
        xv = h_ref[...]
        r = lax.rsqrt(jnp.mean(xv * xv, axis=-1, keepdims=True) + EPS)
        xhat = xv * r
        err = xhat * g_ref[...] - t_ref[...]
        dy = err * (1.0 / d)
        part = jnp.zeros((1, 128), F32) + 0.5 * jnp.sum(jnp.mean(err * err, axis=-1, keepdims=True))
        dgain = jnp.sum(dy * xhat, axis=0, keepdims=True)

        @pl.when(i == 0)
        def _():
            dg_ref[...] = dgain
            loss_ref[...] = part

        @pl.when(i > 0)
        def _():
            dg_ref[...] += dgain
            loss_ref[...] += part

        dxhat = dy * g_ref[...]
        dh = r * (dxhat - xhat * jnp.mean(dxhat * xhat, axis=-1, keepdims=True))
        dh_ref[...] = dh
        dhb_ref[...] = dh.astype(BF16)

    row = pl.BlockSpec((tm, d), lambda i: (i, 0))
    vec = pl.BlockSpec((1, d), lambda i: (0, 0))
    return pl.pallas_call(
        body, name="loss_head", grid=(s // tm,), in_specs=[row, vec, row],
        out_specs=[row, row, vec, pl.BlockSpec((1, 128), lambda i: (0, 0))],
        out_shape=[jax.ShapeDtypeStruct((s, d), F32), jax.ShapeDtypeStruct((s, d), BF16), jax.ShapeDtypeStruct((1, d), F32),
                   jax.ShapeDtypeStruct((1, 128), F32)],
        compiler_params=_params(("arbitrary",)),
    )(h, gain, target)


def _bdot(a, b, ca, cb):
    return lax.dot_general(a.astype(BF16), b.astype(BF16), (((ca,), (cb,)), ((0,), (0,))), preferred_element_type=F32)


def _chunk_cumsum(xv, reverse=False):
    n = xv.shape[0]
    row = lax.broadcasted_iota(jnp.int32, xv.shape, 0) % HG_CHUNK
    step = 1
    while step < HG_CHUNK:
        if reverse:
            xv = xv + jnp.where(row < HG_CHUNK - step, pltpu.roll(xv, n - step, axis=0), 0.0)
        else:
            xv = xv + jnp.where(row >= step, pltpu.roll(xv, step, axis=0), 0.0)
        step *= 2
    return xv


def _hg_terms(p_ref, lbl_ref):
    pq = p_ref[0].astype(F32)
    pf = p_ref[1].astype(F32)
    lb = _sigmoid(lbl_ref[0:1, :] - lbl_ref[1:2, :])
    sig = _sigmoid(pf)
    fg = lb + (1.0 - lb) * sig
    nc = pq.shape[0] // HG_CHUNK
    chunks = lambda a: a.reshape(nc, HG_CHUNK, HG_EXPAND)
    q = chunks(_silu(pq) * HG_EXPAND ** -0.5)
    k = chunks(1.0 - fg)
    v = chunks(p_ref[2].astype(F32))
    g = chunks(_chunk_cumsum(jnp.log(fg)))
    gm = g[:, HG_CHUNK // 2 - 1:HG_CHUNK // 2, :]
    gl = g[:, HG_CHUNK - 1:HG_CHUNK, :]
    e_mid, e_inv, e_all, e_end = jnp.exp(g - gm), jnp.exp(gm - g), jnp.exp(g), jnp.exp(gl - g)
    terms = dict(q=q, k=k, v=v, qd=q * e_all, qt=q * e_mid, kt=k * e_inv, kd=k * e_end, e_last=jnp.exp(gl),
                 e_mid=e_mid, e_inv=e_inv, e_all=e_all, e_end=e_end)
    return terms, (pq, sig, fg, lb)


def _causal(nc):
    r = lax.broadcasted_iota(jnp.int32, (nc, HG_CHUNK, HG_CHUNK), 1)
    c = lax.broadcasted_iota(jnp.int32, (nc, HG_CHUNK, HG_CHUNK), 2)
    return r >= c


def _hgrn2_fwd(p, lb_logits, out_gain):
    _, s, d = p.shape
    heads = d // HG_EXPAND
    t = _row_tile(s, 2048)
    nc = t // HG_CHUNK

    def body(p_ref, lbl_ref, gain_ref, o_ref, og_ref, st_ref, state, decay):
        @pl.when(pl.program_id(1) == 0)
        def _():
            state[...] = jnp.zeros_like(state)

        tm, _ = _hg_terms(p_ref, lbl_ref)
        decay[...] = tm["e_last"]
        st_ref[...] = _bdot(tm["v"], tm["kd"], 1, 1)

        def chunk(c, carry):
            add = st_ref[c]
            st = state[...]
            st_ref[c] = st
            state[...] = st * decay[c] + add
            return carry

        lax.fori_loop(0, nc, chunk, 0)
        a = jnp.where(_causal(nc), _bdot(tm["qt"], tm["kt"], 2, 2), 0.0)
        ov = (_bdot(tm["qd"], st_ref[...], 2, 2) + _bdot(a, tm["v"], 2, 1)).reshape(t, HG_EXPAND)
        o_ref[...] = ov
        on = ov * lax.rsqrt(jnp.mean(ov * ov, axis=-1, keepdims=True) + EPS) * gain_ref[...]
        og_ref[...] = (on * _silu(p_ref[3].astype(F32))).astype(BF16)

    blk = pl.BlockSpec((t, HG_EXPAND), lambda h, b: (b, h))
    return pl.pallas_call(
        body, name="hgrn2_fwd", grid=(heads, s // t),
        in_specs=[pl.BlockSpec((4, t, HG_EXPAND), lambda h, b: (0, b, h)), pl.BlockSpec((2, HG_EXPAND), lambda h, b: (0, h)),
                  pl.BlockSpec((1, HG_EXPAND), lambda h, b: (0, 0))],
        out_specs=[blk, blk, pl.BlockSpec((None, nc, HG_EXPAND, HG_EXPAND), lambda h, b: (h, b, 0, 0))],
        out_shape=[jax.ShapeDtypeStruct((s, d), F32), jax.ShapeDtypeStruct((s, d), BF16),
                   jax.ShapeDtypeStruct((heads, s // HG_CHUNK, HG_EXPAND, HG_EXPAND), F32)],
        scratch_shapes=[pltpu.VMEM((HG_EXPAND, HG_EXPAND), F32), pltpu.VMEM((nc, 1, HG_EXPAND), F32)],
        compiler_params=_params(("parallel", "arbitrary")),
    )(p, lb_logits, out_gain)


def _hgrn2_bwd(p, lb_logits, out_gain, o, dog, states):
    _, s, d = p.shape
    heads = d // HG_EXPAND
    t = _row_tile(s, 1024)
    nc = t // HG_CHUNK
    nb = s // t

    def body(p_ref, lbl_ref, gain_ref, o_ref, dog_ref, st_ref, dp_ref, dlbl_ref, dgain_ref, dstate, decay, dst_s):
        h, b = pl.program_id(0), pl.program_id(1)

        @pl.when(b == 0)
        def _():
            dstate[...] = jnp.zeros_like(dstate)

        tm, (pq, sig, fg, lb) = _hg_terms(p_ref, lbl_ref)
        pg = p_ref[3].astype(F32)
        ov = o_ref[...]
        r = lax.rsqrt(jnp.mean(ov * ov, axis=-1, keepdims=True) + EPS)
        ohat = ov * r
        dogv = dog_ref[...]
        d_on = dogv * _silu(pg)
        dp_ref[3] = (dogv * ohat * gain_ref[...] * _dsilu(pg)).astype(BF16)
        dgain = jnp.sum(d_on * ohat, axis=0, keepdims=True)

        @pl.when((h == 0) & (b == 0))
        def _():
            dgain_ref[...] = dgain

        @pl.when((h > 0) | (b > 0))
        def _():
            dgain_ref[...] += dgain

        dohat = d_on * gain_ref[...]
        do = (r * (dohat - ohat * jnp.mean(dohat * ohat, axis=-1, keepdims=True))).reshape(nc, HG_CHUNK, HG_EXPAND)

        decay[...] = tm["e_last"]
        dst_s[...] = _bdot(do, tm["qd"], 1, 1)

        def chunk(i, carry):
            c = nc - 1 - i
            add = dst_s[c]
            dst = dstate[...]
            dst_s[c] = dst
            dstate[...] = dst * decay[c] + add
            return carry

        lax.fori_loop(0, nc, chunk, 0)
        st, dst = st_ref[...], dst_s[...]
        causal = _causal(nc)
        a = jnp.where(causal, _bdot(tm["qt"], tm["kt"], 2, 2), 0.0)
        da = jnp.where(causal, _bdot(do, tm["v"], 2, 2), 0.0)
        dqt = _bdot(da, tm["kt"], 2, 1)
        dkt = _bdot(da, tm["qt"], 1, 1)
        dqd = _bdot(do, st, 2, 1)
        dkd = _bdot(tm["v"], dst, 2, 1)
        dv = _bdot(a, do, 1, 1) + _bdot(tm["kd"], dst, 2, 2)
        dq = dqt * tm["e_mid"] + dqd * tm["e_all"]
        dk = dkt * tm["e_inv"] + dkd * tm["e_end"]
        dg = dqt * tm["qt"] - dkt * tm["kt"] + dqd * tm["qd"] - dkd * tm["kd"]
        dgl = jnp.sum(dkd * tm["kd"], axis=1, keepdims=True) + tm["e_last"] * jnp.sum(dst * st, axis=1, keepdims=True)
        last_row = lax.broadcasted_iota(jnp.int32, (nc, HG_CHUNK, HG_EXPAND), 1) == HG_CHUNK - 1
        flat = lambda a3: a3.reshape(t, HG_EXPAND)
        dlf = _chunk_cumsum(flat(dg + jnp.where(last_row, dgl, 0.0)), reverse=True)
        dfg = dlf / fg - flat(dk)
        dlb = jnp.sum(dfg * (1.0 - sig), axis=0, keepdims=True)
        dl0 = dlb * lb * (1.0 - lb)
        dlbl = jnp.concatenate([dl0, -dl0], axis=0)

        @pl.when(b == 0)
        def _():
            dlbl_ref[...] = dlbl

        @pl.when(b > 0)
        def _():
            dlbl_ref[...] += dlbl

        dp_ref[0] = (flat(dq) * HG_EXPAND ** -0.5 * _dsilu(pq)).astype(BF16)
        dp_ref[1] = (dfg * (1.0 - lb) * sig * (1.0 - sig)).astype(BF16)
        dp_ref[2] = flat(dv).astype(BF16)

    blk = pl.BlockSpec((t, HG_EXPAND), lambda h, b: (nb - 1 - b, h))
    pblk = pl.BlockSpec((4, t, HG_EXPAND), lambda h, b: (0, nb - 1 - b, h))
    return pl.pallas_call(
        body, name="hgrn2_bwd", grid=(heads, nb),
        in_specs=[pblk, pl.BlockSpec((2, HG_EXPAND), lambda h, b: (0, h)), pl.BlockSpec((1, HG_EXPAND), lambda h, b: (0, 0)),
                  blk, blk, pl.BlockSpec((None, nc, HG_EXPAND, HG_EXPAND), lambda h, b: (h, nb - 1 - b, 0, 0))],
        out_specs=[pblk, pl.BlockSpec((2, HG_EXPAND), lambda h, b: (0, h)), pl.BlockSpec((1, HG_EXPAND), lambda h, b: (0, 0))],
        out_shape=[jax.ShapeDtypeStruct((4, s, d), BF16), jax.ShapeDtypeStruct((2, d), F32), jax.ShapeDtypeStruct((1, HG_EXPAND), F32)],
        scratch_shapes=[pltpu.VMEM((HG_EXPAND, HG_EXPAND), F32), pltpu.VMEM((nc, 1, HG_EXPAND), F32),
                        pltpu.VMEM((nc, HG_EXPAND, HG_EXPAND), F32)],
        compiler_params=_params(("arbitrary", "arbitrary")),
    )(p, lb_logits, out_gain, o, dog, states)


HALO = 8
FFN_FWD_ROWS = 512
FFN_BWD_ROWS = 256


def _shift_down(xv, n):
    return pltpu.roll(xv, n, axis=0)


def _shift_up(xv, n):
    return pltpu.roll(xv, xv.shape[0] - n, axis=0)


def _ffn_hidden_down(name, u, conv_w, conv_b, w_down, h):
    _, nj, s, fb = u.shape
    d = w_down.shape[2]
    tm = _row_tile(s, FFN_FWD_ROWS)
    per = tm // HALO

    def body(gate_ref, prev_ref, val_ref, w_ref, b_ref, wd_ref, h_ref, hid_ref, o_ref):
        i = pl.program_id(0)
        total = h_ref[...]
        for j in range(nj):
            prev = jnp.where(i > 0, prev_ref[j].astype(F32), 0.0)
            ext = jnp.concatenate([prev, gate_ref[j].astype(F32)], axis=0)
            conv = b_ref[j] + w_ref[j, 2:3, :] * ext[HALO:]
            conv = conv + w_ref[j, 1:2, :] * _shift_down(ext, 1)[HALO:]
            conv = conv + w_ref[j, 0:1, :] * _shift_down(ext, 2)[HALO:]
            hidden = (_silu(conv) * val_ref[j].astype(F32)).astype(BF16)
            hid_ref[j] = hidden
            total = total + _dot(hidden, wd_ref[j])
        o_ref[...] = total

    row = pl.BlockSpec((tm, d), lambda i: (i, 0))
    return pl.pallas_call(
        body, name=name, grid=(s // tm,),
        in_specs=[pl.BlockSpec((None, nj, tm, fb), lambda i: (0, 0, i, 0)),
                  pl.BlockSpec((None, nj, HALO, fb), lambda i: (0, 0, jnp.maximum(i * per - 1, 0), 0)),
                  pl.BlockSpec((None, nj, tm, fb), lambda i: (1, 0, i, 0)),
                  pl.BlockSpec((nj, CONV_WIDTH, fb), lambda i: (0, 0, 0)), pl.BlockSpec((nj, 1, fb), lambda i: (0, 0, 0)),
                  pl.BlockSpec((nj, fb, d), lambda i: (0, 0, 0)), row],
        out_specs=[pl.BlockSpec((nj, tm, fb), lambda i: (0, i, 0)), row],
        out_shape=[jax.ShapeDtypeStruct((nj, s, fb), BF16), jax.ShapeDtypeStruct((s, d), F32)],
        compiler_params=_params(("parallel",)),
    )(u, u, u, conv_w, conv_b, w_down, h)


def _ffn_hidden_up_bwd(name, u, dh, conv_w, conv_b, w_up, h, gain, dres):
    _, nj, s, fb = u.shape
    d = w_up.shape[2]
    tm = _row_tile(s, FFN_BWD_ROWS)
    per = tm // HALO
    nblk = s // HALO
    ni = s // tm

    def body(gate_ref, gprev_ref, gnext_ref, val_ref, vnext_ref, dh_ref, dhnext_ref, w_ref, b_ref, wu_ref, h_ref, gain_ref, dres_ref,
             du_ref, dw_ref, db_ref, dx_ref, dxb_ref, dgain_ref):
        i = pl.program_id(0)
        has_next = i < ni - 1
        total = None
        for j in range(nj):
            gprev = jnp.where(i > 0, gprev_ref[j].astype(F32), 0.0)
            gext = jnp.concatenate([gprev, gate_ref[j].astype(F32), gnext_ref[j].astype(F32)], axis=0)
            vext = jnp.concatenate([val_ref[j].astype(F32), vnext_ref[j].astype(F32)], axis=0)
            dhext = jnp.concatenate([dh_ref[j].astype(F32), jnp.where(has_next, dhnext_ref[j].astype(F32), 0.0)], axis=0)
            g0 = gext[HALO:]
            g1 = _shift_down(gext, 1)[HALO:]
            g2 = _shift_down(gext, 2)[HALO:]
            conv = b_ref[j] + w_ref[j, 2:3, :] * g0 + w_ref[j, 1:2, :] * g1 + w_ref[j, 0:1, :] * g2
            act, dact = _silu_and_grad(conv)
            dconv = dhext * vext * dact
            dgate = (w_ref[j, 2:3, :] * dconv + w_ref[j, 1:2, :] * _shift_up(dconv, 1) + w_ref[j, 0:1, :] * _shift_up(dconv, 2))[:tm].astype(BF16)
            dval = (dhext * act)[:tm].astype(BF16)
            du_ref[0, j] = dgate
            du_ref[1, j] = dval
            part = _dot(dgate, wu_ref[j]) + _dot(dval, wu_ref[nj + j])
            total = part if total is None else total + part
            own = dconv[:tm]
            dw = jnp.concatenate([jnp.sum(own * g2[:tm], axis=0, keepdims=True), jnp.sum(own * g1[:tm], axis=0, keepdims=True),
                                  jnp.sum(own * g0[:tm], axis=0, keepdims=True)], axis=0)
            db = jnp.sum(own, axis=0, keepdims=True)

            @pl.when(i == 0)
            def _():
                dw_ref[j] = dw
                db_ref[j] = db

            @pl.when(i > 0)
            def _():
                dw_ref[j] += dw
                db_ref[j] += db

        xv = h_ref[...]
        r = lax.rsqrt(jnp.mean(xv * xv, axis=-1, keepdims=True) + EPS)
        xhat = xv * r
        dgain = jnp.sum(total * xhat, axis=0, keepdims=True)

        @pl.when(i == 0)
        def _():
            dgain_ref[...] = dgain

        @pl.when(i > 0)
        def _():
            dgain_ref[...] += dgain

        dxhat = total * gain_ref[...]
        dx = dres_ref[...] + r * (dxhat - xhat * jnp.mean(dxhat * xhat, axis=-1, keepdims=True))
        dx_ref[...] = dx
        dxb_ref[...] = dx.astype(BF16)

    def tile(part):
        return pl.BlockSpec((None, nj, tm, fb), lambda i: (part, 0, i, 0))

    def after(part):
        return pl.BlockSpec((None, nj, HALO, fb), lambda i: (part, 0, jnp.minimum((i + 1) * per, nblk - 1), 0))

    row = pl.BlockSpec((tm, d), lambda i: (i, 0))
    return pl.pallas_call(
        body, name=name, grid=(ni,),
        in_specs=[tile(0), pl.BlockSpec((None, nj, HALO, fb), lambda i: (0, 0, jnp.maximum(i * per - 1, 0), 0)), after(0),
                  tile(1), after(1),
                  pl.BlockSpec((nj, tm, fb), lambda i: (0, i, 0)),
                  pl.BlockSpec((nj, HALO, fb), lambda i: (0, jnp.minimum((i + 1) * per, nblk - 1), 0)),
                  pl.BlockSpec((nj, CONV_WIDTH, fb), lambda i: (0, 0, 0)), pl.BlockSpec((nj, 1, fb), lambda i: (0, 0, 0)),
                  pl.BlockSpec((2 * nj, fb, d), lambda i: (0, 0, 0)), row, pl.BlockSpec((1, d), lambda i: (0, 0)), row],
        out_specs=[pl.BlockSpec((2, nj, tm, fb), lambda i: (0, 0, i, 0)),
                   pl.BlockSpec((nj, CONV_WIDTH, fb), lambda i: (0, 0, 0)), pl.BlockSpec((nj, 1, fb), lambda i: (0, 0, 0)),
                   row, row, pl.BlockSpec((1, d), lambda i: (0, 0))],
        out_shape=[jax.ShapeDtypeStruct((2, nj, s, fb), BF16), jax.ShapeDtypeStruct((nj, CONV_WIDTH, fb), F32),
                   jax.ShapeDtypeStruct((nj, 1, fb), F32), jax.ShapeDtypeStruct((s, d), F32), jax.ShapeDtypeStruct((s, d), BF16),
                   jax.ShapeDtypeStruct((1, d), F32)],
        compiler_params=_params(("arbitrary",)),
    )(u, u, u, u, u, dh, dh, conv_w, conv_b, w_up, h, gain, dres)


ATT_TILE = 512


def _stack_heads(ref, rows, first_head, count):
    hd = ATT_HEAD_DIM
    return jnp.concatenate([ref[rows, (first_head + j) * hd:(first_head + j + 1) * hd] for j in range(count)], axis=0)


def _unstack_heads(stacked, ref, rows, first_head, count):
    hd = ATT_HEAD_DIM
    for pair in range(count // 2):
        both = [stacked[(2 * pair + j) * WINDOW:(2 * pair + j + 1) * WINDOW, :] for j in range(2)]
        ref[rows, (first_head + 2 * pair) * hd:(first_head + 2 * pair + 2) * hd] = jnp.concatenate(both, axis=1).astype(ref.dtype)


def _attn_bias(first_head, count, n_heads, first):
    lanes = count * WINDOW
    ik = lax.broadcasted_iota(jnp.int32, (2 * WINDOW, lanes), 0)
    iq = lax.broadcasted_iota(jnp.int32, (2 * WINDOW, lanes), 1) % WINDOW
    dist = iq + WINDOW - ik
    valid = (dist >= 0) & (dist < WINDOW) & (ik >= (WINDOW if first else 0))
    slope = jnp.concatenate([jnp.zeros((1, WINDOW), F32) + 2.0 ** (-8.0 * (first_head + j + 1) / n_heads) for j in range(count)], axis=1)
    return jnp.where(valid, -slope * dist.astype(F32), NEG)


def _fill_attn_bias(bias_ref, group, n_heads):
    @pl.when(pl.program_id(0) == 0)
    def _():
        for g in range(ATT_KV_HEADS):
            bias_ref[0, g] = _attn_bias(g * group, group, n_heads, False)
            bias_ref[1, g] = _attn_bias(g * group, group, n_heads, True)


def _attn_probs_t(kb_scaled, qs, sink_ref, first_head, count, bias):
    sink = jnp.concatenate([jnp.zeros((1, WINDOW), F32) + sink_ref[0, first_head + j] for j in range(count)], axis=1)
    sc = _dot(kb_scaled, qs, NT) + bias
    m = jnp.maximum(jnp.max(sc, axis=0, keepdims=True), sink)
    e = jnp.exp(sc - m)
    es = jnp.exp(sink - m)
    inv = 1.0 / (jnp.sum(e, axis=0, keepdims=True) + es)
    return e * inv, es * inv


ATT_SCALE = ATT_HEAD_DIM ** -0.5


def _attn_specs(s, d, kvd, tq):
    per = tq // WINDOW
    return [pl.BlockSpec((tq, d), lambda i: (i, 0)), pl.BlockSpec((tq, kvd), lambda i: (i, 0)),
            pl.BlockSpec((WINDOW, kvd), lambda i: (jnp.maximum(i * per - 1, 0), 0))]


def _attn_fwd(q, kv, sinks):
    s, d = q.shape
    kvd = kv.shape[1]
    half = kvd // 2
    hd = ATT_HEAD_DIM
    nq = d // hd
    group = nq // ATT_KV_HEADS
    tq = min(s, ATT_TILE)
    per = tq // WINDOW

    def body(q_ref, kvc_ref, kvp_ref, sink_ref, o_ref, band, bias_ref):
        i = pl.program_id(0)
        _fill_attn_bias(bias_ref, group, nq)
        band[0:WINDOW, :] = kvp_ref[...]
        band[WINDOW:, :] = kvc_ref[...]

        def block(b, carry):
            rows = pl.ds(pl.multiple_of(b * WINDOW, WINDOW), WINDOW)
            keys = pl.ds(pl.multiple_of(b * WINDOW, WINDOW), 2 * WINDOW)
            first = (i * per + b) == 0
            for g in range(ATT_KV_HEADS):
                bias = jnp.where(first, bias_ref[1, g], bias_ref[0, g])
                p, _ = _attn_probs_t(band[keys, g * hd:(g + 1) * hd] * ATT_SCALE, _stack_heads(q_ref, rows, g * group, group), sink_ref,
                                     g * group, group, bias)
                out_t = _dot(band[keys, half + g * hd:half + (g + 1) * hd], p, TN)
                _unstack_heads(out_t.T, o_ref, rows, g * group, group)
            return carry

        lax.fori_loop(0, per, block, 0)

    return pl.pallas_call(
        body, name="attn_fwd", grid=(s // tq,),
        in_specs=_attn_specs(s, d, kvd, tq) + [pl.BlockSpec(memory_space=pltpu.SMEM)],
        out_specs=pl.BlockSpec((tq, d), lambda i: (i, 0)), out_shape=jax.ShapeDtypeStruct((s, d), BF16),
        scratch_shapes=[pltpu.VMEM((tq + WINDOW, kvd), BF16), pltpu.VMEM((2, ATT_KV_HEADS, 2 * WINDOW, group * WINDOW), F32)],
        compiler_params=_params(("arbitrary",)),
    )(q, kv, kv, sinks)


def _attn_bwd(q, kv, o, do, sinks):
    s, d = q.shape
    kvd = kv.shape[1]
    half = kvd // 2
    hd = ATT_HEAD_DIM
    nq = d // hd
    group = nq // ATT_KV_HEADS
    tq = min(s, ATT_TILE)
    per = tq // WINDOW
    nt = s // tq

    def body(q_ref, kvc_ref, kvp_ref, o_ref, do_ref, sink_ref, dq_ref, dkvc_ref, dkvp_ref, ds_ref, band, dband, bias_ref):
        i = pl.program_id(0)
        _fill_attn_bias(bias_ref, group, nq)
        band[0:WINDOW, :] = kvp_ref[...]
        band[WINDOW:, :] = kvc_ref[...]
        dband[...] = jnp.zeros_like(dband)
        ds_ref[...] = jnp.zeros_like(ds_ref)

        def block(b, carry):
            rows = pl.ds(pl.multiple_of(b * WINDOW, WINDOW), WINDOW)
            keys = pl.ds(pl.multiple_of(b * WINDOW, WINDOW), 2 * WINDOW)
            first = (i * per + b) == 0
            dks, dvs = [], []
            for g in range(ATT_KV_HEADS):
                kb = band[keys, g * hd:(g + 1) * hd] * ATT_SCALE
                vb = band[keys, half + g * hd:half + (g + 1) * hd]
                qs = _stack_heads(q_ref, rows, g * group, group)
                dos = _stack_heads(do_ref, rows, g * group, group)
                p, ps = _attn_probs_t(kb, qs, sink_ref, g * group, group, jnp.where(first, bias_ref[1, g], bias_ref[0, g]))
                prod = dos.astype(F32) * _stack_heads(o_ref, rows, g * group, group).astype(F32)
                dsum = lax.dot_general(jnp.ones((8, hd), F32), prod, NT, precision=lax.Precision.HIGHEST,
                                       preferred_element_type=F32)[0:1, :]
                dsc = p * (_dot(vb, dos, NT) - dsum)
                dvs.append(_dot(p, dos))
                dks.append(_dot(dsc, qs * ATT_SCALE))
                _unstack_heads(_dot(kb, dsc, TN).T, dq_ref, rows, g * group, group)
                gone = ps * dsum
                for j in range(group):
                    ds_ref[g * group + j:g * group + j + 1, :] += jnp.zeros((1, 128), F32) - jnp.sum(gone[:, j * WINDOW:(j + 1) * WINDOW])
            dband[keys, 0:half] += jnp.concatenate(dks, axis=1)
            dband[keys, half:] += jnp.concatenate(dvs, axis=1)
            return carry

        lax.fori_loop(0, per, block, 0)
        dkvp_ref[...] = dband[0:WINDOW, :]
        dkvc_ref[...] = dband[WINDOW:, :]

    big = pl.BlockSpec((tq, d), lambda i: (i, 0))
    return pl.pallas_call(
        body, name="attn_bwd", grid=(nt,),
        in_specs=_attn_specs(s, d, kvd, tq) + [big, big, pl.BlockSpec(memory_space=pltpu.SMEM)],
        out_specs=[big, pl.BlockSpec((tq, kvd), lambda i: (i, 0)), pl.BlockSpec((None, WINDOW, kvd), lambda i: (i, 0, 0)),
                   pl.BlockSpec((None, nq, 128), lambda i: (i, 0, 0))],
        out_shape=[jax.ShapeDtypeStruct((s, d), BF16), jax.ShapeDtypeStruct((s, kvd), F32), jax.ShapeDtypeStruct((nt, WINDOW, kvd), F32),
                   jax.ShapeDtypeStruct((nt, nq, 128), F32)],
        scratch_shapes=[pltpu.VMEM((tq + WINDOW, kvd), BF16), pltpu.VMEM((tq + WINDOW, kvd), F32),
                        pltpu.VMEM((2, ATT_KV_HEADS, 2 * WINDOW, group * WINDOW), F32)],
        compiler_params=_params(("arbitrary",)),
    )(q, kv, kv, o, do, sinks)


HBM_SPEC = pl.BlockSpec(memory_space=pltpu.HBM)
VMEM_SPEC = pl.BlockSpec(memory_space=pltpu.VMEM)


def _place():
    return lax.axis_index("x"), lax.axis_index("y"), lax.axis_index("c")


def _flip(pos, r):
    return tuple(1 - p if (r >> (2 - a)) & 1 else p for a, p in enumerate(pos))


def _index(pos):
    return 4 * pos[0] + 2 * pos[1] + pos[2]


def _all_gather(name, shards, spec):
    n = len(shards)

    def body(*refs):
        x_refs, o_refs = refs[:n], refs[n:2 * n]
        send_sems, recv_sems, local_sems = refs[2 * n:]
        me = _place()
        sibling = _flip(me, 1)
        far = [_flip(me, r) for r in (4, 2, 6)]

        def copy(t, sem, block, to, src=None):
            rows = o_refs[t].at[_index(block)]
            return pltpu.make_async_remote_copy(
                src_ref=rows if src is None else src, dst_ref=rows, send_sem=send_sems.at[t, sem], recv_sem=recv_sems.at[t, sem],
                device_id=to, device_id_type=MESH)

        own = [pltpu.make_async_copy(x_refs[t], o_refs[t].at[_index(me)], local_sems.at[t]) for t in range(n)]
        for cp in own:
            cp.start()
        first = []
        for t in range(n):
            first.append(copy(t, 0, me, sibling, src=x_refs[t]))
            first += [copy(t, 1 + j, me, peer, src=x_refs[t]) for j, peer in enumerate(far)]
        for cp in first:
            cp.start()
        passed = []
        for j, peer in enumerate(far):
            for t in range(n):
                copy(t, 1 + j, peer, me).wait_recv()
                cp = copy(t, 4 + j, peer, sibling)
                cp.start()
                passed.append(cp)
        for t in range(n):
            copy(t, 0, sibling, me).wait_recv()
            for j, peer in enumerate(far):
                copy(t, 4 + j, _flip(peer, 1), me).wait_recv()
        for cp in first + passed:
            cp.wait_send()
        for cp in own:
            cp.wait()

    return pl.pallas_call(
        body, name=name, in_specs=[spec] * n, out_specs=[spec] * n,
        out_shape=[jax.ShapeDtypeStruct((N_DEV,) + sh.shape, sh.dtype) for sh in shards],
        scratch_shapes=[pltpu.SemaphoreType.DMA((n, 7)), pltpu.SemaphoreType.DMA((n, 7)), pltpu.SemaphoreType.DMA((n,))],
    )(*shards)


SEM_SPEC = pl.BlockSpec(memory_space=pltpu.SEMAPHORE)
ANY_SPEC = pl.BlockSpec(memory_space=pl.ANY)


def _landing(own, mine):
    return lax.dynamic_update_slice(lax.empty((N_DEV,) + own.shape, own.dtype), own[None], (mine,) + (0,) * own.ndim)


def _pinned(a, token):
    return a + token[0:1, 0:1].astype(a.dtype)


def _peer_copies(src_refs, land_refs, send_sems, recv_sems, scatter, arrivals):
    me = _place()
    mine = _index(me)
    copies = []
    for t, (src, land) in enumerate(zip(src_refs, land_refs)):
        for r in range(1, N_DEV):
            peer = _flip(me, r)
            theirs = _index(peer)
            sem = t * (N_DEV - 1) + r - 1
            copies.append(pltpu.make_async_remote_copy(
                src_ref=src.at[theirs] if scatter else src, dst_ref=land.at[theirs if arrivals else mine],
                send_sem=send_sems.at[sem], recv_sem=recv_sems.at[sem], device_id=peer, device_id_type=MESH))
    return copies


def _send_start(name, sources, lands, scatter, after=None):
    n = len(sources)
    extra = 0 if after is None else 1

    def body(*refs):
        outs = refs[2 * n + extra:]
        for out in _peer_copies(refs[:n], refs[n:2 * n], outs[0], outs[1], scatter, False):
            out.start()
        outs[-1][...] = jnp.zeros_like(outs[-1])

    outs = pl.pallas_call(
        body, name=name, in_specs=[HBM_SPEC] * (2 * n) + [ANY_SPEC] * extra,
        out_specs=[SEM_SPEC, SEM_SPEC] + [HBM_SPEC] * (2 * n) + [VMEM_SPEC],
        out_shape=[pltpu.SemaphoreType.DMA((n * (N_DEV - 1),)), pltpu.SemaphoreType.DMA((n * (N_DEV - 1),))]
        + [pltpu.HBM(a.shape, a.dtype) for a in list(sources) + list(lands)] + [jax.ShapeDtypeStruct((8, 128), F32)],
        input_output_aliases={i: 2 + i for i in range(2 * n)},
        compiler_params=pltpu.CompilerParams(has_side_effects=pltpu.SideEffectType.DATAFLOW_SIDE_EFFECTING),
    )(*[pltpu.with_memory_space_constraint(a, pltpu.HBM) for a in list(sources) + list(lands)], *([] if after is None else [after]))
    return outs[0], outs[1], outs[2:2 + n], outs[2 + n:2 + 2 * n], outs[-1]


def _send_wait(name, started, after, scatter):
    send_sems, recv_sems, sources, lands, _ = started
    n = len(sources)

    def body(*refs):
        for out in _peer_copies(refs[:n], refs[n:2 * n], refs[2 * n], refs[2 * n + 1], scatter, False):
            out.wait_send()
        for arrival in _peer_copies(refs[:n], refs[n:2 * n], refs[2 * n], refs[2 * n + 1], scatter, True):
            arrival.wait_recv()

    outs = pl.pallas_call(
        body, name=name, in_specs=[HBM_SPEC] * (2 * n) + [SEM_SPEC, SEM_SPEC, ANY_SPEC], out_specs=[HBM_SPEC] * (2 * n),
        out_shape=[pltpu.HBM(a.shape, a.dtype) for a in list(sources) + list(lands)],
        input_output_aliases={i: i for i in range(2 * n)},
        compiler_params=pltpu.CompilerParams(has_side_effects=pltpu.SideEffectType.DATAFLOW_SIDE_EFFECTING),
    )(*sources, *lands, send_sems, recv_sems, after)
    return outs[n:]


def _pack_rows(parts):
    offsets, row = [], 0
    for part in parts:
        offsets.append(row)
        row += part.shape[0]
    return offsets, -(-row // 8) * 8, -(-max(part.shape[1] for part in parts) // 128) * 128


def _pack(name, parts):
    offsets, rows, width = _pack_rows(parts)

    def body(*refs):
        o_ref = refs[-1]
        o_ref[...] = jnp.zeros_like(o_ref)
        for off, ref in zip(offsets, refs[:-1]):
            o_ref[off:off + ref.shape[0], 0:ref.shape[1]] = ref[...]

    return pl.pallas_call(body, name=name, in_specs=[VMEM_SPEC] * len(parts), out_specs=VMEM_SPEC,
                          out_shape=jax.ShapeDtypeStruct((rows, width), F32))(*parts)


def _adamw_math(w, g, m, v):
    m = ADAM_B1 * m + (1.0 - ADAM_B1) * g
    v = ADAM_B2 * v + (1.0 - ADAM_B2) * (g * g)
    m_hat = m * (1.0 / (1.0 - ADAM_B1 ** ADAM_STEP))
    denom = jnp.sqrt(v * (1.0 / (1.0 - ADAM_B2 ** ADAM_STEP))) + ADAM_EPS
    inv = pl.reciprocal(denom, approx=True)
    inv = inv * (2.0 - denom * inv)
    return -ADAM_LR * (m_hat * inv + ADAM_WD * w), m, v


def _adamw_step(w_ref, m_ref, v_ref, p_ref, g_ref, d_ref, nm_ref, nv_ref):
    g = p_ref[0].astype(F32)
    for dev in range(1, N_DEV):
        g = g + p_ref[dev].astype(F32)
    g_ref[...] = g
    d_ref[...], nm_ref[...], nv_ref[...] = _adamw_math(w_ref[...], g, m_ref[...], v_ref[...])


def _adamw_rows(rows):
    return max(t for t in range(8, min(rows, 256) + 1, 8) if rows % t == 0)


def _adamw_shard(name, w, m, v, partials):
    rows, cols = w.shape
    tr = _adamw_rows(rows)
    blk = pl.BlockSpec((tr, cols), lambda i: (i, 0))
    return pl.pallas_call(
        _adamw_step_fn(), name=name, grid=(rows // tr,), in_specs=[blk, blk, blk, pl.BlockSpec((N_DEV, tr, cols), lambda i: (0, i, 0))],
        out_specs=[blk] * 4, out_shape=[jax.ShapeDtypeStruct((rows, cols), F32)] * 4, compiler_params=_params(("parallel",)),
    )(w, m, v, partials)


def _adamw_step_fn():
    return functools.partial(_adamw_step)


def _adamw_layers(name, w, m, v, partials):
    layers, rows, cols = w.shape
    tr = _adamw_rows(rows)
    last = rows // tr - 1

    def body(w_ref, m_ref, v_ref, *rest):
        for layer in range(layers):
            @pl.when(pl.program_id(0) == layer)
            def _():
                _adamw_step(w_ref, m_ref, v_ref, rest[layer], *rest[layers:])

    blk = pl.BlockSpec((None, tr, cols), lambda l, i: (l, i, 0))
    part = lambda layer: pl.BlockSpec((N_DEV, tr, cols), lambda l, i: (0, jnp.where(l == layer, i, jnp.where(l < layer, 0, last)), 0))
    return pl.pallas_call(
        body, name=name, grid=(layers, rows // tr), in_specs=[blk, blk, blk] + [part(layer) for layer in range(layers)],
        out_specs=[blk] * 4, out_shape=[jax.ShapeDtypeStruct(w.shape, F32)] * 4, compiler_params=_params(("arbitrary", "arbitrary")),
    )(w, m, v, *partials)


def _adamw_small(gathered, places, entries):
    n = len(entries)
    np_ = len(gathered)

    def body(*refs):
        pack_refs = refs[:np_]
        refs = refs[np_ - 1:]
        w_refs, m_refs, v_refs = refs[1:1 + n], refs[1 + n:1 + 2 * n], refs[1 + 2 * n:1 + 3 * n]
        outs = refs[1 + 3 * n:]
        totals = []
        for pack_ref in pack_refs:
            acc = pack_ref[0]
            for dev in range(1, N_DEV):
                acc = acc + pack_ref[dev]
            totals.append(acc)
        mine = _index(_place())
        for e in range(n):
            rows, cols = w_refs[e].shape
            total, off = totals[places[e][0]], places[e][1]
            if entries[e][3]:
                g = jnp.zeros((rows, cols), F32)
                for dev in range(N_DEV):
                    g = g + jnp.where(mine == dev, total[off + dev * rows:off + (dev + 1) * rows, 0:cols], 0.0)
            else:
                g = total[off:off + rows, 0:cols]
            outs[4 * e][...] = g
            outs[4 * e + 1][...], outs[4 * e + 2][...], outs[4 * e + 3][...] = _adamw_math(w_refs[e][...], g, m_refs[e][...], v_refs[e][...])
        outs[4 * n][...] = totals[places[n][0]][places[n][1]:places[n][1] + 1, 0:128]

    shapes = []
    for w, _, _, _ in entries:
        shapes += [jax.ShapeDtypeStruct(w.shape, F32)] * 4
    shapes.append(jax.ShapeDtypeStruct((1, 128), F32))
    return pl.pallas_call(
        body, name="adamw_small", in_specs=[VMEM_SPEC] * (np_ + 3 * n), out_specs=[VMEM_SPEC] * len(shapes), out_shape=shapes,
        compiler_params=pltpu.CompilerParams(vmem_limit_bytes=VMEM_LIMIT),
    )(*gathered, *[e[0] for e in entries], *[e[1] for e in entries], *[e[2] for e in entries])


def _ffn_forward(tag, h, gain, w_up, late):
    s, d = h.shape
    fb = w_up.shape[1]
    tm = _row_tile(s, 2 * MM_ROWS)
    a, = _rmsnorm_cast(f"ffn_norm_{tag}", h, [gain])
    u = _matmul(
        f"ffn_up_{tag}", a, w_up, dims=NT, grid=(s // tm, N_DEV, 1),
        a_spec=pl.BlockSpec((tm, d), lambda i, j, k: (i, 0)),
        b_spec=pl.BlockSpec((None, fb, d), lambda i, j, k: (j, 0, 0)),
        o_spec=pl.BlockSpec((None, None, tm, fb), lambda i, j, k: (j // 4, j % 4, i, 0)),
        out_shape=jax.ShapeDtypeStruct((2, 4, s, fb), BF16))
    w_down, conv_w, conv_b = late(u)
    hidden, out = _ffn_hidden_down(f"ffn_hidden_down_{tag}", u, conv_w, conv_b, w_down, h)
    return out, (a, u, hidden)


def _ffn_backward(tag, h, gain, w_up, w_down, conv_w, conv_b, saved, dout):
    a, u, hidden = saved
    dout, dout_bf = dout
    s, d = h.shape
    fb = w_up.shape[1]
    tm = _row_tile(s, MM_ROWS)
    dhidden = _matmul(
        f"ffn_down_bwd_{tag}", dout_bf, w_down, dims=NT, grid=(s // tm, 4, 1),
        a_spec=pl.BlockSpec((tm, d), lambda i, j, k: (i, 0)),
        b_spec=pl.BlockSpec((None, fb, d), lambda i, j, k: (j, 0, 0)),
        o_spec=pl.BlockSpec((None, tm, fb), lambda i, j, k: (j, i, 0)),
        out_shape=jax.ShapeDtypeStruct((4, s, fb), BF16))
    dw_down = _matmul(
        f"ffn_down_grad_{tag}", hidden, dout_bf, dims=TN, grid=(4, 1, 1),
        a_spec=pl.BlockSpec((None, s, fb), lambda i, j, k: (i, 0, 0)),
        b_spec=pl.BlockSpec((s, d), lambda i, j, k: (0, 0)),
        o_spec=pl.BlockSpec((None, fb, d), lambda i, j, k: (i, 0, 0)),
        out_shape=jax.ShapeDtypeStruct((4, fb, d), BF16))
    du, dconv_w, dconv_b, dh, dh_bf, dgain = _ffn_hidden_up_bwd(f"ffn_hidden_up_bwd_{tag}", u, dhidden, conv_w, conv_b, w_up, h, gain, dout)
    dw_up = _matmul(
        f"ffn_up_grad_{tag}", du, a, dims=TN, grid=(N_DEV, 1, 1),
        a_spec=pl.BlockSpec((None, None, s, fb), lambda i, j, k: (i // 4, i % 4, 0, 0)),
        b_spec=pl.BlockSpec((s, d), lambda i, j, k: (0, 0)),
        o_spec=pl.BlockSpec((None, fb, d), lambda i, j, k: (i, 0, 0)),
        out_shape=jax.ShapeDtypeStruct((N_DEV, fb, d), BF16))
    return (dh, dh_bf), dgain, dw_up, dw_down, dconv_w, dconv_b


def kernel(x, hg_norm, hg_w_in, hg_lb_logits, hg_out_norm, hg_w_out, kv_norm, w_kv, attn_norm, attn_w_q, attn_sinks, attn_w_o, ffn_norm, ffn_w_up, ffn_conv_w, ffn_conv_b, ffn_w_down, final_norm, loss_target, m_hg_norm, m_hg_w_in, m_hg_lb_logits, m_hg_out_norm, m_hg_w_out, m_kv_norm, m_w_kv, m_attn_norm, m_attn_w_q, m_attn_sinks, m_attn_w_o, m_ffn_norm, m_ffn_w_up, m_ffn_conv_w, m_ffn_conv_b, m_ffn_w_down, m_final_norm, v_hg_norm, v_hg_w_in, v_hg_lb_logits, v_hg_out_norm, v_hg_w_out, v_kv_norm, v_w_kv, v_attn_norm, v_attn_w_q, v_attn_sinks, v_attn_w_o, v_ffn_norm, v_ffn_w_up, v_ffn_conv_w, v_ffn_conv_b, v_ffn_w_down, v_final_norm):
    _, s, d = x.shape
    x0, target = x[0], loss_target[0]
    half = hg_w_in.shape[2]
    fs = ffn_conv_w.shape[2]
    fb = 2 * fs
    kvd = w_kv.shape[1]
    nq = d // ATT_HEAD_DIM
    tm = _row_tile(s, MM_ROWS)

    mine = _index(_place())
    gather = lambda tag, shards, after: _send_start("gather_start_" + tag, shards, [_landing(a, mine) for a in shards], False, after)
    w_in, g_hgn, g_lbl, w_out = _all_gather("gather_hg", [hg_w_in[0].astype(BF16), hg_norm, hg_lb_logits, hg_w_out[0].astype(BF16)], HBM_SPEC)
    w_out = w_out.reshape(d, d)
    up_t = lambda a: jnp.swapaxes(a, -1, -2)
    coming_up0 = gather("ffn_up0", [up_t(ffn_w_up[0]).astype(BF16)], g_hgn)
    hgn = _pinned(g_hgn.reshape(1, d), coming_up0[4])
    lbl = g_lbl.transpose(1, 0, 2).reshape(2, d)
    conv_b = [ffn_conv_b[layer].reshape(4, 1, fb) for layer in range(2)]
    gains = [ffn_norm[0:1], ffn_norm[1:2]]
    kvn, fin = kv_norm.reshape(1, d), final_norm.reshape(1, d)

    a0, = _rmsnorm_cast("hg_norm", x0, [hgn])
    t2 = _row_tile(s, 2 * MM_ROWS)
    p = _matmul(
        "hg_in", a0, w_in, dims=NN, grid=(s // t2, N_DEV, 1),
        a_spec=pl.BlockSpec((t2, d), lambda i, j, k: (i, 0)),
        b_spec=pl.BlockSpec((None, d, half), lambda i, j, k: (j, 0, 0)),
        o_spec=pl.BlockSpec((None, t2, half), lambda i, j, k: (j // 2, i, j % 2)),
        out_shape=jax.ShapeDtypeStruct((4, s, d), BF16), acc_shape=(8, 128))
    o, og, states = _hgrn2_fwd(p, lbl, hg_out_norm)
    coming_dn0 = gather("ffn_down0", [ffn_conv_w, ffn_w_down[0].astype(BF16)], o)
    x1 = _mm_rows("hg_out", og, _pinned(w_out, coming_dn0[4]), out_dtype=F32, add=x0)
    w_up0, = _send_wait("gather_wait_ffn_up0", coming_up0, x1, False)
    coming_attn = gather("attn", [w_kv.astype(BF16), attn_w_q[0].astype(BF16), attn_w_o[0].astype(BF16)], w_up0)
    gains[0] = _pinned(gains[0], coming_attn[4])
    w_up, w_dn, conv_w, coming = [w_up0, None], [None, None], [], {}

    def late0(u):
        g_cw, w_dn0 = _send_wait("gather_wait_ffn_down0", coming_dn0, u, False)
        w_dn[0] = w_dn0.reshape(4, fb, d)
        conv_w.extend(g_cw[:, layer].reshape(4, 2, CONV_WIDTH, fs).transpose(0, 2, 1, 3).reshape(4, CONV_WIDTH, fb) for layer in range(2))
        coming["up1"] = gather("ffn_up1", [up_t(ffn_w_up[1]).astype(BF16)], w_dn0)
        return w_dn[0], conv_w[0], _pinned(conv_b[0], coming["up1"][4])

    x2, saved0 = _ffn_forward("0", x1, gains[0], w_up[0], late0)
    w_kvg, w_q, w_o = _send_wait("gather_wait_attn", coming_attn, x2, False)
    w_kvg, w_q, w_o = w_kvg.reshape(d, kvd), w_q.reshape(d, d), w_o.reshape(d, d)
    akv, a2 = _rmsnorm_cast("attn_norms", x2, [kvn, attn_norm])
    kv = _mm_rows("kv_proj", akv, w_kvg, out_dtype=BF16)
    q = _mm_rows("q_proj", a2, w_q, out_dtype=BF16)
    att = _attn_fwd(q, kv, attn_sinks)
    coming_dn1 = gather("ffn_down1", [ffn_w_down[1].astype(BF16)], att)
    x3 = _mm_rows("attn_out", att, _pinned(w_o, coming_dn1[4]), out_dtype=F32, add=x2)
    w_up[1], = _send_wait("gather_wait_ffn_up1", coming["up1"], x3, False)

    def late1(u):
        w_dn[1] = _send_wait("gather_wait_ffn_down1", coming_dn1, u, False)[0].reshape(4, fb, d)
        return w_dn[1], conv_w[1], conv_b[1]

    x4, saved1 = _ffn_forward("1", x3, gains[1], w_up[1], late1)
    dx4, dx4_bf, d_fin, loss_part = _loss_head(x4, fin, target)

    dx3, d_fn1, dw_up1, dw_dn1, dcw1, dcb1 = _ffn_backward("1", x3, gains[1], w_up[1], w_dn[1], conv_w[1], conv_b[1], saved1, (dx4, dx4_bf))
    rows = d // N_DEV
    scatter = lambda tag, stacks: _send_start("scatter_start_" + tag, stacks, [_landing(lax.dynamic_index_in_dim(a, mine, keepdims=False), mine) for a in stacks], True)
    going_ffn1 = scatter("ffn1", [dw_up1, dw_dn1.reshape(N_DEV, fs, d)])
    datt = _mm_rows_nt("attn_out_bwd", dx3[1], w_o, out_dtype=BF16)
    dw_o = _mm_tn("attn_out_grad", att, dx3[1])
    dq, dkv_own, dkv_before, dsink = _attn_bwd(q, kv, att, datt, _pinned(attn_sinks, going_ffn1[4]))
    tiles = dkv_before.shape[0]
    dkv = dkv_own.reshape(tiles, s // tiles, kvd)
    dkv = jnp.concatenate([dkv[:, :-WINDOW], dkv[:, -WINDOW:] + jnp.pad(dkv_before[1:], ((0, 1), (0, 0), (0, 0)))], axis=1).reshape(s, kvd)
    dw_q = _mm_tn("q_proj_grad", a2, dq)
    dw_kv = _mm_tn("kv_proj_grad", akv, dkv)
    going_attn = scatter("attn", [dw_kv.reshape(N_DEV, rows, kvd), dw_q.reshape(N_DEV, rows, d), dw_o.reshape(N_DEV, rows, d)])
    whole = lambda a_ref, b_ref: [(a_ref[...], b_ref[...])]
    rows_of = lambda width: (lambda tile: pl.BlockSpec((tile, width), lambda i: (i, 0)))
    dx2, (d_kvn, d_attn) = _proj_norm_bwd("attn_in_bwd", x2, dx3[0], [
        (dkv, rows_of(kvd), w_kvg, pl.BlockSpec((d, kvd), lambda i: (0, 0)), whole, _pinned(kvn, going_attn[4])),
        (dq, rows_of(d), w_q, pl.BlockSpec((d, d), lambda i: (0, 0)), whole, attn_norm)])
    dx1, d_fn0, dw_up0, dw_dn0, dcw0, dcb0 = _ffn_backward("0", x1, gains[0], w_up[0], w_dn[0], conv_w[0], conv_b[0], saved0, dx2)
    dw_out = _mm_tn("hg_out_grad", og, dx1[1])
    going_ffn0 = scatter("ffn0", [dw_up0, dw_dn0.reshape(N_DEV, fs, d), dw_out.reshape(N_DEV, rows, d)])
    dog = _mm_rows_nt("hg_out_bwd", dx1[1], w_out, out_dtype=F32)
    dp, d_lbl, d_ogain = _hgrn2_bwd(p, lbl, _pinned(hg_out_norm, going_ffn0[4]), o, dog, states)
    dw_in = _matmul(
        "hg_in_grad", a0, dp, dims=TN, grid=(1, N_DEV, 1),
        a_spec=pl.BlockSpec((s, d), lambda i, j, k: (0, 0)),
        b_spec=pl.BlockSpec((None, s, half), lambda i, j, k: (j // 2, 0, j % 2)),
        o_spec=pl.BlockSpec((None, d, half), lambda i, j, k: (j, 0, 0)),
        out_shape=jax.ShapeDtypeStruct((N_DEV, d, half), BF16))
    going_hg = scatter("hg", [dw_in])
    (dx0, _), (d_hgn,) = _proj_norm_bwd("hg_in_bwd", x0, dx1[0], [
        (dp, lambda tile: pl.BlockSpec((4, tile, d), lambda i: (0, i, 0)), w_in, pl.BlockSpec((N_DEV, d, half), lambda i: (0, 0, 0)),
         lambda g_ref, w_ref: [(g_ref[k // 2, :, (k % 2) * half:(k % 2 + 1) * half], w_ref[k]) for k in range(N_DEV)],
         _pinned(hgn, going_hg[4]))])

    as_blocks = lambda a, r: a.reshape(r, N_DEV, -1).transpose(1, 0, 2).reshape(N_DEV * r, -1)
    d_cw = jnp.concatenate([g.transpose(1, 0, 2).reshape(CONV_WIDTH, 4 * fb) for g in (dcw0, dcw1)], axis=0)
    parts = [d_fin, jnp.concatenate([d_fn0, d_fn1], axis=0), jnp.concatenate([dcb0.reshape(1, 4 * fb), dcb1.reshape(1, 4 * fb)], axis=0),
             as_blocks(d_cw, 2 * CONV_WIDTH), d_attn, jnp.sum(dsink[:, :, 0], axis=0).reshape(1, nq), d_kvn, d_ogain,
             as_blocks(d_hgn, 1), as_blocks(d_lbl, 2), loss_part]
    wide = [2]
    packs = [[parts[i] for i in wide], [part for i, part in enumerate(parts) if i not in wide]]
    places = [None] * len(parts)
    for which, members in enumerate([wide, [i for i in range(len(parts)) if i not in wide]]):
        for i, off in zip(members, _pack_rows(packs[which])[0]):
            places[i] = (which, off)
    packed = [_pack("pack_wide_grads", packs[0]), _pack("pack_narrow_grads", packs[1])]
    going_small = _send_start("small_grads_start", packed, [_landing(a, mine) for a in packed], False)

    arrive = lambda tag, going, after: _send_wait("scatter_wait_" + tag, going, after, True)
    (l_up1, l_dn1), (l_kv, l_q, l_o), (l_up0, l_dn0, l_out) = (
        arrive("ffn1", going_ffn1, going_small[4]), arrive("attn", going_attn, going_small[4]), arrive("ffn0", going_ffn0, going_small[4]))
    big = {}
    for tag, w, m, v, part in [
            ("w_kv", w_kv, m_w_kv, v_w_kv, l_kv), ("attn_w_q", attn_w_q[0], m_attn_w_q[0], v_attn_w_q[0], l_q),
            ("attn_w_o", attn_w_o[0], m_attn_w_o[0], v_attn_w_o[0], l_o)]:
        big[tag] = _adamw_shard("adamw_" + tag, w, m, v, part)
    big["ffn_w_up"] = [up_t(a) for a in _adamw_layers("adamw_ffn_w_up", up_t(ffn_w_up), up_t(m_ffn_w_up), up_t(v_ffn_w_up), (l_up0, l_up1))]
    big["ffn_w_down"] = _adamw_layers("adamw_ffn_w_down", ffn_w_down, m_ffn_w_down, v_ffn_w_down, (l_dn0, l_dn1))
    lead = lambda tag: [a[None] for a in big[tag]]

    both_done = big["ffn_w_up"][0][0, 0:1, 0:1] + big["ffn_w_down"][0][0, 0:1, 0:1]
    gathered = _send_wait("small_grads_wait", going_small, both_done, False)
    two = lambda a: a.reshape(-1, a.shape[-1])
    small = [(fin, m_final_norm.reshape(1, d), v_final_norm.reshape(1, d), False), (ffn_norm, m_ffn_norm, v_ffn_norm, False),
             (ffn_conv_b, m_ffn_conv_b, v_ffn_conv_b, False), (two(ffn_conv_w), two(m_ffn_conv_w), two(v_ffn_conv_w), True),
             (attn_norm, m_attn_norm, v_attn_norm, False), (attn_sinks, m_attn_sinks, v_attn_sinks, False),
             (kvn, m_kv_norm.reshape(1, d), v_kv_norm.reshape(1, d), False), (hg_out_norm, m_hg_out_norm, v_hg_out_norm, False),
             (hg_norm, m_hg_norm, v_hg_norm, True), (hg_lb_logits, m_hg_lb_logits, v_hg_lb_logits, True)]
    res = _adamw_small(gathered, places, small)
    l_in, = arrive("hg", going_hg, gathered[1])
    big["hg_w_in"] = _adamw_shard("adamw_hg_w_in", hg_w_in[0], m_hg_w_in[0], v_hg_w_in[0], l_in)
    big["hg_w_out"] = _adamw_shard("adamw_hg_w_out", hg_w_out[0], m_hg_w_out[0], v_hg_w_out[0], l_out)
    names = ["final_norm", "ffn_norm", "ffn_conv_b", "ffn_conv_w", "attn_norm", "attn_sinks", "kv_norm", "hg_out_norm", "hg_norm", "hg_lb_logits"]
    shapes = {"final_norm": final_norm.shape, "kv_norm": kv_norm.shape, "ffn_conv_w": ffn_conv_w.shape}
    out = {n: [a.reshape(shapes[n]) if n in shapes else a for a in res[4 * i:4 * i + 4]] for i, n in enumerate(names)}
    out.update(hg_w_in=lead("hg_w_in"), hg_w_out=lead("hg_w_out"), w_kv=big["w_kv"], attn_w_q=lead("attn_w_q"), attn_w_o=lead("attn_w_o"),
               ffn_w_up=big["ffn_w_up"], ffn_w_down=big["ffn_w_down"])
    order = ["hg_norm", "hg_w_in", "hg_lb_logits", "hg_out_norm", "hg_w_out", "kv_norm", "w_kv", "attn_norm", "attn_w_q", "attn_sinks",
             "attn_w_o", "ffn_norm", "ffn_w_up", "ffn_conv_w", "ffn_conv_b", "ffn_w_down", "final_norm"]
    loss = res[-1][0, 0]
    return (loss, dx0[None], *[out[n][0] for n in order], *[out[n][1] for n in order], *[out[n][2] for n in order], *[out[n][3] for n in order])
```

```python
import functools

import jax
import jax.numpy as jnp
from jax import lax
from jax.experimental import pallas as pl
from jax.experimental.pallas import tpu as pltpu

F32 = jnp.float32
BF16 = jnp.bfloat16

EPS = 1e-6
HG_EXPAND = 128
HG_CHUNK = 32
ATT_HEAD_DIM = 64
ATT_KV_HEADS = 2
WINDOW = 128
CONV_WIDTH = 3
ADAM_LR = 0.001
ADAM_B1 = 0.9
ADAM_B2 = 0.999
ADAM_EPS = 1e-08
ADAM_WD = 0.01
ADAM_STEP = 10

N_DEV = 8
VMEM_LIMIT = 48 * 1024 * 1024
NEG = -1e30

NN = (((1,), (0,)), ((), ()))
NT = (((1,), (1,)), ((), ()))
TN = (((0,), (0,)), ((), ()))
MESH = pl.DeviceIdType.MESH


def _dot(a, b, dims=NN):
    return lax.dot_general(a.astype(BF16), b.astype(BF16), dims, preferred_element_type=F32)


def _sigmoid(x):
    return 0.5 * jnp.tanh(0.5 * x) + 0.5


def _silu(x):
    return x * _sigmoid(x)


def _silu_and_grad(x):
    s = _sigmoid(x)
    return x * s, s * (1.0 + x * (1.0 - s))


def _dsilu(x):
    return _silu_and_grad(x)[1]


def _params(semantics):
    return pltpu.CompilerParams(dimension_semantics=semantics, vmem_limit_bytes=VMEM_LIMIT)


def _row_tile(rows, want=512):
    return min(rows, want)


MM_ROWS = 1024


def _matmul(name, a, b, *, dims, grid, a_spec, b_spec, o_spec, out_shape, acc_shape=(8, 128), add=None, add_spec=None, terms=None):
    nk = grid[2]

    def body(*refs):
        if add is None:
            a_ref, b_ref, o_ref, acc = refs
        else:
            a_ref, b_ref, add_ref, o_ref, acc = refs
        k = pl.program_id(2)
        pairs = [(a_ref[...], b_ref[...])] if terms is None else terms(a_ref, b_ref)
        part = _dot(*pairs[0], dims)
        for pair in pairs[1:]:
            part = part + _dot(*pair, dims)

        def finish(total):
            if add is not None:
                total = total + add_ref[...]
            o_ref[...] = total.astype(o_ref.dtype)

        if nk == 1:
            finish(part)
        else:
            @pl.when(k == 0)
            def _():
                acc[...] = part

            @pl.when(k > 0)
            def _():
                acc[...] += part

            @pl.when(k == nk - 1)
            def _():
                finish(acc[...])

    in_specs = [a_spec, b_spec] + ([] if add is None else [add_spec])
    args = (a, b) + (() if add is None else (add,))
    return pl.pallas_call(
        body, name=name, grid=grid, in_specs=in_specs, out_specs=o_spec, out_shape=out_shape,
        scratch_shapes=[pltpu.VMEM(acc_shape, F32)],
        compiler_params=_params(("parallel", "parallel", "arbitrary")),
    )(*args)


def _mm_rows(name, a, w, *, out_dtype, add=None):
    s, kdim = a.shape
    n = w.shape[1]
    tm = _row_tile(s, MM_ROWS)
    return _matmul(
        name, a, w, dims=NN, grid=(s // tm, 1, 1),
        a_spec=pl.BlockSpec((tm, kdim), lambda i, j, k: (i, 0)),
        b_spec=pl.BlockSpec((kdim, n), lambda i, j, k: (0, 0)),
        o_spec=pl.BlockSpec((tm, n), lambda i, j, k: (i, 0)),
        out_shape=jax.ShapeDtypeStruct((s, n), out_dtype), acc_shape=(8, 128),
        add=add, add_spec=None if add is None else pl.BlockSpec((tm, n), lambda i, j, k: (i, 0)),
    )


def _mm_rows_nt(name, a, w, *, out_dtype):
    s, n = a.shape
    kdim = w.shape[0]
    tm = _row_tile(s, MM_ROWS)
    return _matmul(
        name, a, w, dims=NT, grid=(s // tm, 1, 1),
        a_spec=pl.BlockSpec((tm, n), lambda i, j, k: (i, 0)),
        b_spec=pl.BlockSpec((kdim, n), lambda i, j, k: (0, 0)),
        o_spec=pl.BlockSpec((tm, kdim), lambda i, j, k: (i, 0)),
        out_shape=jax.ShapeDtypeStruct((s, kdim), out_dtype), acc_shape=(8, 128),
    )


def _mm_tn(name, a, g):
    s, m = a.shape
    n = g.shape[1]
    tn = min(n, 512)
    return _matmul(
        name, a, g, dims=TN, grid=(1, n // tn, 1),
        a_spec=pl.BlockSpec((s, m), lambda i, j, k: (0, 0)),
        b_spec=pl.BlockSpec((s, tn), lambda i, j, k: (0, j)),
        o_spec=pl.BlockSpec((m, tn), lambda i, j, k: (0, j)),
        out_shape=jax.ShapeDtypeStruct((m, n), BF16),
    )


def _rmsnorm_cast(name, h, gains):
    s, d = h.shape
    tm = _row_tile(s)
    n = len(gains)

    def body(*refs):
        h_ref, g_refs, o_refs = refs[0], refs[1:1 + n], refs[1 + n:]
        xv = h_ref[...]
        xhat = xv * lax.rsqrt(jnp.mean(xv * xv, axis=-1, keepdims=True) + EPS)
        for g_ref, o_ref in zip(g_refs, o_refs):
            o_ref[...] = (xhat * g_ref[...]).astype(BF16)

    row = pl.BlockSpec((tm, d), lambda i: (i, 0))
    vec = pl.BlockSpec((1, d), lambda i: (0, 0))
    return pl.pallas_call(
        body, name=name, grid=(s // tm,), in_specs=[row] + [vec] * n, out_specs=[row] * n,
        out_shape=[jax.ShapeDtypeStruct((s, d), BF16)] * n, compiler_params=_params(("parallel",)),
    )(h, *gains)


def _proj_norm_bwd(name, h, dres, branches):
    s, d = h.shape
    tm = _row_tile(s)
    n = len(branches)

    def body(*refs):
        h_ref, dres_ref = refs[0], refs[1]
        g_refs, w_refs, gain_refs = refs[2:2 + n], refs[2 + n:2 + 2 * n], refs[2 + 2 * n:2 + 3 * n]
        dh_ref, dhb_ref, dg_refs = refs[2 + 3 * n], refs[3 + 3 * n], refs[4 + 3 * n:]
        i = pl.program_id(0)
        xv = h_ref[...]
        r = lax.rsqrt(jnp.mean(xv * xv, axis=-1, keepdims=True) + EPS)
        xhat = xv * r
        total = dres_ref[...]
        for branch, g_ref, w_ref, gain_ref, dg_ref in zip(branches, g_refs, w_refs, gain_refs, dg_refs):
            pairs = branch[4](g_ref, w_ref)
            da = _dot(*pairs[0], NT)
            for pair in pairs[1:]:
                da = da + _dot(*pair, NT)
            dgain = jnp.sum(da * xhat, axis=0, keepdims=True)

            @pl.when(i == 0)
            def _():
                dg_ref[...] = dgain

            @pl.when(i > 0)
            def _():
                dg_ref[...] += dgain

            dxhat = da * gain_ref[...]
            total = total + r * (dxhat - xhat * jnp.mean(dxhat * xhat, axis=-1, keepdims=True))
        dh_ref[...] = total
        dhb_ref[...] = total.astype(BF16)

    row = pl.BlockSpec((tm, d), lambda i: (i, 0))
    vec = pl.BlockSpec((1, d), lambda i: (0, 0))
    outs = pl.pallas_call(
        body, name=name, grid=(s // tm,),
        in_specs=[row, row] + [b[1](tm) for b in branches] + [b[3] for b in branches] + [vec] * n, out_specs=[row, row] + [vec] * n,
        out_shape=[jax.ShapeDtypeStruct((s, d), F32), jax.ShapeDtypeStruct((s, d), BF16)] + [jax.ShapeDtypeStruct((1, d), F32)] * n,
        compiler_params=_params(("arbitrary",)),
    )(h, dres, *[b[0] for b in branches], *[b[2] for b in branches], *[b[5] for b in branches])
    return (outs[0], outs[1]), outs[2:]


def _loss_head(h, gain, target):
    s, d = h.shape
    tm = _row_tile(s)

    def body(h_ref, g_ref, t_ref, dh_ref, dhb_ref, dg_ref, loss_ref):
        i = pl.program_id(0)
        xv = h_ref[...]
        r = lax.rsqrt(jnp.mean(xv * xv, axis=-1, keepdims=True) + EPS)
        xhat = xv * r
        err = xhat * g_ref[...] - t_ref[...]
        dy = err * (1.0 / d)
        part = jnp.zeros((1, 128), F32) + 0.5 * jnp.sum(jnp.mean(err * err, axis=-1, keepdims=True))
        dgain = jnp.sum(dy * xhat, axis=0, keepdims=True)

        @pl.when(i == 0)
        def _():
            dg_ref[...] = dgain
            loss_ref[...] = part

        @pl.when(i > 0)
        def _():
            dg_ref[...] += dgain
            loss_ref[...] += part

        dxhat = dy * g_ref[...]
        dh = r * (dxhat - xhat * jnp.mean(dxhat * xhat, axis=-1, keepdims=True))
        dh_ref[...] = dh
        dhb_ref[...] = dh.astype(BF16)

    row = pl.BlockSpec((tm, d), lambda i: (i, 0))
    vec = pl.BlockSpec((1, d), lambda i: (0, 0))
    return pl.pallas_call(
        body, name="loss_head", grid=(s // tm,), in_specs=[row, vec, row],
        out_specs=[row, row, vec, pl.BlockSpec((1, 128), lambda i: (0, 0))],
        out_shape=[jax.ShapeDtypeStruct((s, d), F32), jax.ShapeDtypeStruct((s, d), BF16), jax.ShapeDtypeStruct((1, d), F32),
                   jax.ShapeDtypeStruct((1, 128), F32)],
        compiler_params=_params(("arbitrary",)),
    )(h, gain, target)


def _bdot(a, b, ca, cb):
    return lax.dot_general(a.astype(BF16), b.astype(BF16), (((ca,), (cb,)), ((0,), (0,))), preferred_element_type=F32)


def _chunk_cumsum(xv, reverse=False):
    n = xv.shape[0]
    row = lax.broadcasted_iota(jnp.int32, xv.shape, 0) % HG_CHUNK
    step = 1
    while step < HG_CHUNK:
        if reverse:
            xv = xv + jnp.where(row < HG_CHUNK - step, pltpu.roll(xv, n - step, axis=0), 0.0)
        else:
            xv = xv + jnp.where(row >= step, pltpu.roll(xv, step, axis=0), 0.0)
        step *= 2
    return xv


def _hg_terms(p_ref, lbl_ref, g_ref=None):
    pq = p_ref[0].astype(F32)
    pf = p_ref[1].astype(F32)
    lb = _sigmoid(lbl_ref[0:1, :] - lbl_ref[1:2, :])
    sig = _sigmoid(pf)
    fg = lb + (1.0 - lb) * sig
    nc = pq.shape[0] // HG_CHUNK
    chunks = lambda a: a.reshape(nc, HG_CHUNK, HG_EXPAND)
    q = chunks(_silu(pq) * HG_EXPAND ** -0.5)
    k = chunks(1.0 - fg)
    v = chunks(p_ref[2].astype(F32))
    g = chunks(_chunk_cumsum(jnp.log(fg)) if g_ref is None else g_ref[...])
    gm = g[:, HG_CHUNK // 2 - 1:HG_CHUNK // 2, :]
    gl = g[:, HG_CHUNK - 1:HG_CHUNK, :]
    e_mid, e_inv, e_all, e_end = jnp.exp(g - gm), jnp.exp(gm - g), jnp.exp(g), jnp.exp(gl - g)
    terms = dict(q=q, k=k, v=v, g=g, qd=q * e_all, qt=q * e_mid, kt=k * e_inv, kd=k * e_end, e_last=jnp.exp(gl),
                 e_mid=e_mid, e_inv=e_inv, e_all=e_all, e_end=e_end)
    return terms, (pq, sig, fg, lb)


def _causal(nc):
    r = lax.broadcasted_iota(jnp.int32, (nc, HG_CHUNK, HG_CHUNK), 1)
    c = lax.broadcasted_iota(jnp.int32, (nc, HG_CHUNK, HG_CHUNK), 2)
    return r >= c


def _hgrn2_fwd(p, lb_logits, out_gain):
    _, s, d = p.shape
    heads = d // HG_EXPAND
    t = _row_tile(s, 2048)
    nc = t // HG_CHUNK

    def body(p_ref, lbl_ref, gain_ref, o_ref, og_ref, st_ref, g_ref, state, decay):
        @pl.when(pl.program_id(1) == 0)
        def _():
            state[...] = jnp.zeros_like(state)

        tm, _ = _hg_terms(p_ref, lbl_ref)
        g_ref[...] = tm["g"].reshape(t, HG_EXPAND)
        decay[...] = tm["e_last"]
        st_ref[...] = _bdot(tm["v"], tm["kd"], 1, 1)

        def chunk(c, carry):
            add = st_ref[c]
            st = state[...]
            st_ref[c] = st
            state[...] = st * decay[c] + add
            return carry

        lax.fori_loop(0, nc, chunk, 0)
        a = jnp.where(_causal(nc), _bdot(tm["qt"], tm["kt"], 2, 2), 0.0)
        ov = (_bdot(tm["qd"], st_ref[...], 2, 2) + _bdot(a, tm["v"], 2, 1)).reshape(t, HG_EXPAND)
        o_ref[...] = ov
        on = ov * lax.rsqrt(jnp.mean(ov * ov, axis=-1, keepdims=True) + EPS) * gain_ref[...]
        og_ref[...] = (on * _silu(p_ref[3].astype(F32))).astype(BF16)

    blk = pl.BlockSpec((t, HG_EXPAND), lambda h, b: (b, h))
    return pl.pallas_call(
        body, name="hgrn2_fwd", grid=(heads, s // t),
        in_specs=[pl.BlockSpec((4, t, HG_EXPAND), lambda h, b: (0, b, h)), pl.BlockSpec((2, HG_EXPAND), lambda h, b: (0, h)),
                  pl.BlockSpec((1, HG_EXPAND), lambda h, b: (0, 0))],
        out_specs=[blk, blk, pl.BlockSpec((None, nc, HG_EXPAND, HG_EXPAND), lambda h, b: (h, b, 0, 0)), blk],
        out_shape=[jax.ShapeDtypeStruct((s, d), F32), jax.ShapeDtypeStruct((s, d), BF16),
                   jax.ShapeDtypeStruct((heads, s // HG_CHUNK, HG_EXPAND, HG_EXPAND), F32), jax.ShapeDtypeStruct((s, d), F32)],
        scratch_shapes=[pltpu.VMEM((HG_EXPAND, HG_EXPAND), F32), pltpu.VMEM((nc, 1, HG_EXPAND), F32)],
        compiler_params=_params(("parallel", "arbitrary")),
    )(p, lb_logits, out_gain)


def _hgrn2_bwd(p, lb_logits, out_gain, o, dog, states, gsum):
    _, s, d = p.shape
    heads = d // HG_EXPAND
    t = _row_tile(s, 1024)
    nc = t // HG_CHUNK
    nb = s // t

    def body(p_ref, lbl_ref, gain_ref, o_ref, dog_ref, st_ref, g_ref, dp_ref, dlbl_ref, dgain_ref, dstate, decay, dst_s):
        h, b = pl.program_id(0), pl.program_id(1)

        @pl.when(b == 0)
        def _():
            dstate[...] = jnp.zeros_like(dstate)

        tm, (pq, sig, fg, lb) = _hg_terms(p_ref, lbl_ref, g_ref)
        pg = p_ref[3].astype(F32)
        ov = o_ref[...]
        r = lax.rsqrt(jnp.mean(ov * ov, axis=-1, keepdims=True) + EPS)
        ohat = ov * r
        dogv = dog_ref[...]
        d_on = dogv * _silu(pg)
        dp_ref[3] = (dogv * ohat * gain_ref[...] * _dsilu(pg)).astype(BF16)
        dgain = jnp.sum(d_on * ohat, axis=0, keepdims=True)

        @pl.when((h == 0) & (b == 0))
        def _():
            dgain_ref[...] = dgain

        @pl.when((h > 0) | (b > 0))
        def _():
            dgain_ref[...] += dgain

        dohat = d_on * gain_ref[...]
        do = (r * (dohat - ohat * jnp.mean(dohat * ohat, axis=-1, keepdims=True))).reshape(nc, HG_CHUNK, HG_EXPAND)

        decay[...] = tm["e_last"]
        dst_s[...] = _bdot(do, tm["qd"], 1, 1)

        def chunk(i, carry):
            c = nc - 1 - i
            add = dst_s[c]
            dst = dstate[...]
            dst_s[c] = dst
            dstate[...] = dst * decay[c] + add
            return carry

        lax.fori_loop(0, nc, chunk, 0)
        st, dst = st_ref[...], dst_s[...]
        causal = _causal(nc)
        a = jnp.where(causal, _bdot(tm["qt"], tm["kt"], 2, 2), 0.0)
        da = jnp.where(causal, _bdot(do, tm["v"], 2, 2), 0.0)
        dqt = _bdot(da, tm["kt"], 2, 1)
        dkt = _bdot(da, tm["qt"], 1, 1)
        dqd = _bdot(do, st, 2, 1)
        dkd = _bdot(tm["v"], dst, 2, 1)
        dv = _bdot(a, do, 1, 1) + _bdot(tm["kd"], dst, 2, 2)
        dq = dqt * tm["e_mid"] + dqd * tm["e_all"]
        dk = dkt * tm["e_inv"] + dkd * tm["e_end"]
        dg = dqt * tm["qt"] - dkt * tm["kt"] + dqd * tm["qd"] - dkd * tm["kd"]
        dgl = jnp.sum(dkd * tm["kd"], axis=1, keepdims=True) + tm["e_last"] * jnp.sum(dst * st, axis=1, keepdims=True)
        last_row = lax.broadcasted_iota(jnp.int32, (nc, HG_CHUNK, HG_EXPAND), 1) == HG_CHUNK - 1
        flat = lambda a3: a3.reshape(t, HG_EXPAND)
        dlf = _chunk_cumsum(flat(dg + jnp.where(last_row, dgl, 0.0)), reverse=True)
        dfg = dlf / fg - flat(dk)
        dlb = jnp.sum(dfg * (1.0 - sig), axis=0, keepdims=True)
        dl0 = dlb * lb * (1.0 - lb)
        dlbl = jnp.concatenate([dl0, -dl0], axis=0)

        @pl.when(b == 0)
        def _():
            dlbl_ref[...] = dlbl

        @pl.when(b > 0)
        def _():
            dlbl_ref[...] += dlbl

        dp_ref[0] = (flat(dq) * HG_EXPAND ** -0.5 * _dsilu(pq)).astype(BF16)
        dp_ref[1] = (dfg * (1.0 - lb) * sig * (1.0 - sig)).astype(BF16)
        dp_ref[2] = flat(dv).astype(BF16)

    blk = pl.BlockSpec((t, HG_EXPAND), lambda h, b: (nb - 1 - b, h))
    pblk = pl.BlockSpec((4, t, HG_EXPAND), lambda h, b: (0, nb - 1 - b, h))
    return pl.pallas_call(
        body, name="hgrn2_bwd", grid=(heads, nb),
        in_specs=[pblk, pl.BlockSpec((2, HG_EXPAND), lambda h, b: (0, h)), pl.BlockSpec((1, HG_EXPAND), lambda h, b: (0, 0)),
                  blk, blk, pl.BlockSpec((None, nc, HG_EXPAND, HG_EXPAND), lambda h, b: (h, nb - 1 - b, 0, 0)), blk],
        out_specs=[pblk, pl.BlockSpec((2, HG_EXPAND), lambda h, b: (0, h)), pl.BlockSpec((1, HG_EXPAND), lambda h, b: (0, 0))],
        out_shape=[jax.ShapeDtypeStruct((4, s, d), BF16), jax.ShapeDtypeStruct((2, d), F32), jax.ShapeDtypeStruct((1, HG_EXPAND), F32)],
        scratch_shapes=[pltpu.VMEM((HG_EXPAND, HG_EXPAND), F32), pltpu.VMEM((nc, 1, HG_EXPAND), F32),
                        pltpu.VMEM((nc, HG_EXPAND, HG_EXPAND), F32)],
        compiler_params=_params(("arbitrary", "arbitrary")),
    )(p, lb_logits, out_gain, o, dog, states, gsum)


HALO = 8
FFN_FWD_ROWS = 512
FFN_BWD_ROWS = 256


def _shift_down(xv, n):
    return pltpu.roll(xv, n, axis=0)


def _shift_up(xv, n):
    return pltpu.roll(xv, xv.shape[0] - n, axis=0)


def _ffn_hidden_down(name, u, conv_w, conv_b, w_down, h):
    _, nj, s, fb = u.shape
    d = w_down.shape[2]
    tm = _row_tile(s, FFN_FWD_ROWS)
    per = tm // HALO

    def body(gate_ref, prev_ref, val_ref, w_ref, b_ref, wd_ref, h_ref, hid_ref, conv_ref, o_ref):
        i = pl.program_id(0)
        total = h_ref[...]
        for j in range(nj):
            prev = jnp.where(i > 0, prev_ref[j].astype(F32), 0.0)
            ext = jnp.concatenate([prev, gate_ref[j].astype(F32)], axis=0)
            conv = b_ref[j] + w_ref[j, 2:3, :] * ext[HALO:]
            conv = conv + w_ref[j, 1:2, :] * _shift_down(ext, 1)[HALO:]
            conv = conv + w_ref[j, 0:1, :] * _shift_down(ext, 2)[HALO:]
            conv_ref[j] = conv.astype(BF16)
            hidden = (_silu(conv) * val_ref[j].astype(F32)).astype(BF16)
            hid_ref[j] = hidden
            total = total + _dot(hidden, wd_ref[j])
        o_ref[...] = total

    row = pl.BlockSpec((tm, d), lambda i: (i, 0))
    return pl.pallas_call(
        body, name=name, grid=(s // tm,),
        in_specs=[pl.BlockSpec((None, nj, tm, fb), lambda i: (0, 0, i, 0)),
                  pl.BlockSpec((None, nj, HALO, fb), lambda i: (0, 0, jnp.maximum(i * per - 1, 0), 0)),
                  pl.BlockSpec((None, nj, tm, fb), lambda i: (1, 0, i, 0)),
                  pl.BlockSpec((nj, CONV_WIDTH, fb), lambda i: (0, 0, 0)), pl.BlockSpec((nj, 1, fb), lambda i: (0, 0, 0)),
                  pl.BlockSpec((nj, fb, d), lambda i: (0, 0, 0)), row],
        out_specs=[pl.BlockSpec((nj, tm, fb), lambda i: (0, i, 0)), pl.BlockSpec((nj, tm, fb), lambda i: (0, i, 0)), row],
        out_shape=[jax.ShapeDtypeStruct((nj, s, fb), BF16), jax.ShapeDtypeStruct((nj, s, fb), BF16), jax.ShapeDtypeStruct((s, d), F32)],
        compiler_params=_params(("parallel",)),
    )(u, u, u, conv_w, conv_b, w_down, h)


def _ffn_hidden_up_bwd(name, u, conv, dh, conv_w, w_up, h, gain, dres):
    _, nj, s, fb = u.shape
    d = w_up.shape[2]
    tm = _row_tile(s, FFN_BWD_ROWS)
    per = tm // HALO
    nblk = s // HALO
    ni = s // tm

    def body(gate_ref, conv_ref, cnext_ref, val_ref, vnext_ref, dh_ref, dhnext_ref, w_ref, wu_ref, h_ref, gain_ref, dres_ref,
             du_ref, dw_ref, db_ref, dx_ref, dxb_ref, dgain_ref):
        i = pl.program_id(0)
        has_next = i < ni - 1
        total = None
        for j in range(nj):
            cext = jnp.concatenate([conv_ref[j].astype(F32), cnext_ref[j].astype(F32)], axis=0)
            vext = jnp.concatenate([val_ref[j].astype(F32), vnext_ref[j].astype(F32)], axis=0)
            dhext = jnp.concatenate([dh_ref[j].astype(F32), jnp.where(has_next, dhnext_ref[j].astype(F32), 0.0)], axis=0)
            act, dact = _silu_and_grad(cext)
            dconv = dhext * vext * dact
            taps = [_shift_up(dconv, 2)[:tm], _shift_up(dconv, 1)[:tm], dconv[:tm]]
            dgate = (w_ref[j, 0:1, :] * taps[0] + w_ref[j, 1:2, :] * taps[1] + w_ref[j, 2:3, :] * taps[2]).astype(BF16)
            dval = (dhext * act)[:tm].astype(BF16)
            du_ref[0, j] = dgate
            du_ref[1, j] = dval
            part = _dot(dgate, wu_ref[j]) + _dot(dval, wu_ref[nj + j])
            total = part if total is None else total + part
            gate = gate_ref[j].astype(F32)
            dw = jnp.concatenate([jnp.sum(tap * gate, axis=0, keepdims=True) for tap in taps], axis=0)
            db = jnp.sum(taps[2], axis=0, keepdims=True)

            @pl.when(i == 0)
            def _():
                dw_ref[j] = dw
                db_ref[j] = db

            @pl.when(i > 0)
            def _():
                dw_ref[j] += dw
                db_ref[j] += db

        xv = h_ref[...]
        r = lax.rsqrt(jnp.mean(xv * xv, axis=-1, keepdims=True) + EPS)
        xhat = xv * r
        dgain = jnp.sum(total * xhat, axis=0, keepdims=True)

        @pl.when(i == 0)
        def _():
            dgain_ref[...] = dgain

        @pl.when(i > 0)
        def _():
            dgain_ref[...] += dgain

        dxhat = total * gain_ref[...]
        dx = dres_ref[...] + r * (dxhat - xhat * jnp.mean(dxhat * xhat, axis=-1, keepdims=True))
        dx_ref[...] = dx
        dxb_ref[...] = dx.astype(BF16)

    def tile(part):
        return pl.BlockSpec((None, nj, tm, fb), lambda i: (part, 0, i, 0))

    def after(part):
        return pl.BlockSpec((None, nj, HALO, fb), lambda i: (part, 0, jnp.minimum((i + 1) * per, nblk - 1), 0))

    row = pl.BlockSpec((tm, d), lambda i: (i, 0))
    own = pl.BlockSpec((nj, tm, fb), lambda i: (0, i, 0))
    nxt = pl.BlockSpec((nj, HALO, fb), lambda i: (0, jnp.minimum((i + 1) * per, nblk - 1), 0))
    return pl.pallas_call(
        body, name=name, grid=(ni,),
        in_specs=[tile(0), own, nxt, tile(1), after(1), own, nxt,
                  pl.BlockSpec((nj, CONV_WIDTH, fb), lambda i: (0, 0, 0)),
                  pl.BlockSpec((2 * nj, fb, d), lambda i: (0, 0, 0)), row, pl.BlockSpec((1, d), lambda i: (0, 0)), row],
        out_specs=[pl.BlockSpec((2, nj, tm, fb), lambda i: (0, 0, i, 0)),
                   pl.BlockSpec((nj, CONV_WIDTH, fb), lambda i: (0, 0, 0)), pl.BlockSpec((nj, 1, fb), lambda i: (0, 0, 0)),
                   row, row, pl.BlockSpec((1, d), lambda i: (0, 0))],
        out_shape=[jax.ShapeDtypeStruct((2, nj, s, fb), BF16), jax.ShapeDtypeStruct((nj, CONV_WIDTH, fb), F32),
                   jax.ShapeDtypeStruct((nj, 1, fb), F32), jax.ShapeDtypeStruct((s, d), F32), jax.ShapeDtypeStruct((s, d), BF16),
                   jax.ShapeDtypeStruct((1, d), F32)],
        compiler_params=_params(("arbitrary",)),
    )(u, conv, conv, u, u, dh, dh, conv_w, w_up, h, gain, dres)


ATT_TILE = 512


def _stack_heads(ref, rows, first_head, count):
    hd = ATT_HEAD_DIM
    return jnp.concatenate([ref[rows, (first_head + j) * hd:(first_head + j + 1) * hd] for j in range(count)], axis=0)


def _unstack_heads(stacked, ref, rows, first_head, count):
    hd = ATT_HEAD_DIM
    for pair in range(count // 2):
        both = [stacked[(2 * pair + j) * WINDOW:(2 * pair + j + 1) * WINDOW, :] for j in range(2)]
        ref[rows, (first_head + 2 * pair) * hd:(first_head + 2 * pair + 2) * hd] = jnp.concatenate(both, axis=1).astype(ref.dtype)


def _attn_bias(first_head, count, n_heads, first):
    lanes = count * WINDOW
    ik = lax.broadcasted_iota(jnp.int32, (2 * WINDOW, lanes), 0)
    iq = lax.broadcasted_iota(jnp.int32, (2 * WINDOW, lanes), 1) % WINDOW
    dist = iq + WINDOW - ik
    valid = (dist >= 0) & (dist < WINDOW) & (ik >= (WINDOW if first else 0))
    slope = jnp.concatenate([jnp.zeros((1, WINDOW), F32) + 2.0 ** (-8.0 * (first_head + j + 1) / n_heads) for j in range(count)], axis=1)
    return jnp.where(valid, -slope * dist.astype(F32), NEG)


def _fill_attn_bias(bias_ref, group, n_heads):
    @pl.when(pl.program_id(0) == 0)
    def _():
        for g in range(ATT_KV_HEADS):
            bias_ref[0, g] = _attn_bias(g * group, group, n_heads, False)
            bias_ref[1, g] = _attn_bias(g * group, group, n_heads, True)


def _attn_probs_t(kb_scaled, qs, sink_ref, first_head, count, bias):
    sink = jnp.concatenate([jnp.zeros((1, WINDOW), F32) + sink_ref[0, first_head + j] for j in range(count)], axis=1)
    sc = _dot(kb_scaled, qs, NT) + bias
    m = jnp.maximum(jnp.max(sc, axis=0, keepdims=True), sink)
    e = jnp.exp(sc - m)
    es = jnp.exp(sink - m)
    inv = 1.0 / (jnp.sum(e, axis=0, keepdims=True) + es)
    return e * inv, es * inv


ATT_SCALE = ATT_HEAD_DIM ** -0.5


def _attn_specs(s, d, kvd, tq):
    per = tq // WINDOW
    return [pl.BlockSpec((tq, d), lambda i: (i, 0)), pl.BlockSpec((tq, kvd), lambda i: (i, 0)),
            pl.BlockSpec((WINDOW, kvd), lambda i: (jnp.maximum(i * per - 1, 0), 0))]


def _attn_fwd(q, kv, sinks):
    s, d = q.shape
    kvd = kv.shape[1]
    half = kvd // 2
    hd = ATT_HEAD_DIM
    nq = d // hd
    group = nq // ATT_KV_HEADS
    tq = min(s, ATT_TILE)
    per = tq // WINDOW

    def body(q_ref, kvc_ref, kvp_ref, sink_ref, o_ref, band, bias_ref):
        i = pl.program_id(0)
        _fill_attn_bias(bias_ref, group, nq)
        band[0:WINDOW, :] = kvp_ref[...]
        band[WINDOW:, :] = kvc_ref[...]

        def block(b, carry):
            rows = pl.ds(pl.multiple_of(b * WINDOW, WINDOW), WINDOW)
            keys = pl.ds(pl.multiple_of(b * WINDOW, WINDOW), 2 * WINDOW)
            first = (i * per + b) == 0
            for g in range(ATT_KV_HEADS):
                bias = jnp.where(first, bias_ref[1, g], bias_ref[0, g])
                p, _ = _attn_probs_t(band[keys, g * hd:(g + 1) * hd] * ATT_SCALE, _stack_heads(q_ref, rows, g * group, group), sink_ref,
                                     g * group, group, bias)
                out_t = _dot(band[keys, half + g * hd:half + (g + 1) * hd], p, TN)
                _unstack_heads(out_t.T, o_ref, rows, g * group, group)
            return carry

        lax.fori_loop(0, per, block, 0)

    return pl.pallas_call(
        body, name="attn_fwd", grid=(s // tq,),
        in_specs=_attn_specs(s, d, kvd, tq) + [pl.BlockSpec(memory_space=pltpu.SMEM)],
        out_specs=pl.BlockSpec((tq, d), lambda i: (i, 0)), out_shape=jax.ShapeDtypeStruct((s, d), BF16),
        scratch_shapes=[pltpu.VMEM((tq + WINDOW, kvd), BF16), pltpu.VMEM((2, ATT_KV_HEADS, 2 * WINDOW, group * WINDOW), F32)],
        compiler_params=_params(("arbitrary",)),
    )(q, kv, kv, sinks)


def _attn_bwd(q, kv, o, do, sinks):
    s, d = q.shape
    kvd = kv.shape[1]
    half = kvd // 2
    hd = ATT_HEAD_DIM
    nq = d // hd
    group = nq // ATT_KV_HEADS
    tq = min(s, ATT_TILE)
    per = tq // WINDOW
    nt = s // tq

    def body(q_ref, kvc_ref, kvp_ref, o_ref, do_ref, sink_ref, dq_ref, dkvc_ref, dkvp_ref, ds_ref, band, dband, bias_ref):
        i = pl.program_id(0)
        _fill_attn_bias(bias_ref, group, nq)
        band[0:WINDOW, :] = kvp_ref[...]
        band[WINDOW:, :] = kvc_ref[...]
        dband[...] = jnp.zeros_like(dband)
        ds_ref[...] = jnp.zeros_like(ds_ref)

        def block(b, carry):
            rows = pl.ds(pl.multiple_of(b * WINDOW, WINDOW), WINDOW)
            keys = pl.ds(pl.multiple_of(b * WINDOW, WINDOW), 2 * WINDOW)
            first = (i * per + b) == 0
            dks, dvs = [], []
            for g in range(ATT_KV_HEADS):
                kb = band[keys, g * hd:(g + 1) * hd] * ATT_SCALE
                vb = band[keys, half + g * hd:half + (g + 1) * hd]
                qs = _stack_heads(q_ref, rows, g * group, group)
                dos = _stack_heads(do_ref, rows, g * group, group)
                p, ps = _attn_probs_t(kb, qs, sink_ref, g * group, group, jnp.where(first, bias_ref[1, g], bias_ref[0, g]))
                prod = dos.astype(F32) * _stack_heads(o_ref, rows, g * group, group).astype(F32)
                dsum = lax.dot_general(jnp.ones((8, hd), F32), prod, NT, precision=lax.Precision.HIGHEST,
                                       preferred_element_type=F32)[0:1, :]
                dsc = p * (_dot(vb, dos, NT) - dsum)
                dvs.append(_dot(p, dos))
                dks.append(_dot(dsc, qs * ATT_SCALE))
                _unstack_heads(_dot(kb, dsc, TN).T, dq_ref, rows, g * group, group)
                gone = ps * dsum
                for j in range(group):
                    ds_ref[g * group + j:g * group + j + 1, :] += jnp.zeros((1, 128), F32) - jnp.sum(gone[:, j * WINDOW:(j + 1) * WINDOW])
            dband[keys, 0:half] += jnp.concatenate(dks, axis=1)
            dband[keys, half:] += jnp.concatenate(dvs, axis=1)
            return carry

        lax.fori_loop(0, per, block, 0)
        dkvp_ref[...] = dband[0:WINDOW, :]
        dkvc_ref[...] = dband[WINDOW:, :]

    big = pl.BlockSpec((tq, d), lambda i: (i, 0))
    return pl.pallas_call(
        body, name="attn_bwd", grid=(nt,),
        in_specs=_attn_specs(s, d, kvd, tq) + [big, big, pl.BlockSpec(memory_space=pltpu.SMEM)],
        out_specs=[big, pl.BlockSpec((tq, kvd), lambda i: (i, 0)), pl.BlockSpec((None, WINDOW, kvd), lambda i: (i, 0, 0)),
                   pl.BlockSpec((None, nq, 128), lambda i: (i, 0, 0))],
        out_shape=[jax.ShapeDtypeStruct((s, d), BF16), jax.ShapeDtypeStruct((s, kvd), F32), jax.ShapeDtypeStruct((nt, WINDOW, kvd), F32),
                   jax.ShapeDtypeStruct((nt, nq, 128), F32)],
        scratch_shapes=[pltpu.VMEM((tq + WINDOW, kvd), BF16), pltpu.VMEM((tq + WINDOW, kvd), F32),
                        pltpu.VMEM((2, ATT_KV_HEADS, 2 * WINDOW, group * WINDOW), F32)],
        compiler_params=_params(("arbitrary",)),
    )(q, kv, kv, o, do, sinks)


HBM_SPEC = pl.BlockSpec(memory_space=pltpu.HBM)
VMEM_SPEC = pl.BlockSpec(memory_space=pltpu.VMEM)


def _place():
    return lax.axis_index("x"), lax.axis_index("y"), lax.axis_index("c")


def _flip(pos, r):
    return tuple(1 - p if (r >> (2 - a)) & 1 else p for a, p in enumerate(pos))


def _index(pos):
    return 4 * pos[0] + 2 * pos[1] + pos[2]


def _all_gather(name, shards, spec):
    n = len(shards)

    def body(*refs):
        x_refs, o_refs = refs[:n], refs[n:2 * n]
        send_sems, recv_sems, local_sems = refs[2 * n:]
        me = _place()
        sibling = _flip(me, 1)
        far = [_flip(me, r) for r in (4, 2, 6)]

        def copy(t, sem, block, to, src=None):
            rows = o_refs[t].at[_index(block)]
            return pltpu.make_async_remote_copy(
                src_ref=rows if src is None else src, dst_ref=rows, send_sem=send_sems.at[t, sem], recv_sem=recv_sems.at[t, sem],
                device_id=to, device_id_type=MESH)

        own = [pltpu.make_async_copy(x_refs[t], o_refs[t].at[_index(me)], local_sems.at[t]) for t in range(n)]
        for cp in own:
            cp.start()
        first = []
        for t in range(n):
            first.append(copy(t, 0, me, sibling, src=x_refs[t]))
            first += [copy(t, 1 + j, me, peer, src=x_refs[t]) for j, peer in enumerate(far)]
        for cp in first:
            cp.start()
        passed = []
        for j, peer in enumerate(far):
            for t in range(n):
                copy(t, 1 + j, peer, me).wait_recv()
                cp = copy(t, 4 + j, peer, sibling)
                cp.start()
                passed.append(cp)
        for t in range(n):
            copy(t, 0, sibling, me).wait_recv()
            for j, peer in enumerate(far):
                copy(t, 4 + j, _flip(peer, 1), me).wait_recv()
        for cp in first + passed:
            cp.wait_send()
        for cp in own:
            cp.wait()

    return pl.pallas_call(
        body, name=name, in_specs=[spec] * n, out_specs=[spec] * n,
        out_shape=[jax.ShapeDtypeStruct((N_DEV,) + sh.shape, sh.dtype) for sh in shards],
        scratch_shapes=[pltpu.SemaphoreType.DMA((n, 7)), pltpu.SemaphoreType.DMA((n, 7)), pltpu.SemaphoreType.DMA((n,))],
    )(*shards)


SEM_SPEC = pl.BlockSpec(memory_space=pltpu.SEMAPHORE)
ANY_SPEC = pl.BlockSpec(memory_space=pl.ANY)


def _landing(own, mine):
    return lax.dynamic_update_slice(lax.empty((N_DEV,) + own.shape, own.dtype), own[None], (mine,) + (0,) * own.ndim)


def _pinned(a, token):
    return a + token[0:1, 0:1].astype(a.dtype)


def _peer_copies(src_refs, land_refs, send_sems, recv_sems, scatter, arrivals):
    me = _place()
    mine = _index(me)
    copies = []
    for t, (src, land) in enumerate(zip(src_refs, land_refs)):
        for r in range(1, N_DEV):
            peer = _flip(me, r)
            theirs = _index(peer)
            sem = t * (N_DEV - 1) + r - 1
            copies.append(pltpu.make_async_remote_copy(
                src_ref=src.at[theirs] if scatter else src, dst_ref=land.at[theirs if arrivals else mine],
                send_sem=send_sems.at[sem], recv_sem=recv_sems.at[sem], device_id=peer, device_id_type=MESH))
    return copies


def _send_start(name, sources, lands, scatter, after=None):
    n = len(sources)
    extra = 0 if after is None else 1

    def body(*refs):
        outs = refs[2 * n + extra:]
        for out in _peer_copies(refs[:n], refs[n:2 * n], outs[0], outs[1], scatter, False):
            out.start()
        outs[-1][...] = jnp.zeros_like(outs[-1])

    outs = pl.pallas_call(
        body, name=name, in_specs=[HBM_SPEC] * (2 * n) + [ANY_SPEC] * extra,
        out_specs=[SEM_SPEC, SEM_SPEC] + [HBM_SPEC] * (2 * n) + [VMEM_SPEC],
        out_shape=[pltpu.SemaphoreType.DMA((n * (N_DEV - 1),)), pltpu.SemaphoreType.DMA((n * (N_DEV - 1),))]
        + [pltpu.HBM(a.shape, a.dtype) for a in list(sources) + list(lands)] + [jax.ShapeDtypeStruct((8, 128), F32)],
        input_output_aliases={i: 2 + i for i in range(2 * n)},
        compiler_params=pltpu.CompilerParams(has_side_effects=pltpu.SideEffectType.DATAFLOW_SIDE_EFFECTING),
    )(*[pltpu.with_memory_space_constraint(a, pltpu.HBM) for a in list(sources) + list(lands)], *([] if after is None else [after]))
    return outs[0], outs[1], outs[2:2 + n], outs[2 + n:2 + 2 * n], outs[-1]


def _send_wait(name, started, after, scatter):
    send_sems, recv_sems, sources, lands, _ = started
    n = len(sources)

    def body(*refs):
        for out in _peer_copies(refs[:n], refs[n:2 * n], refs[2 * n], refs[2 * n + 1], scatter, False):
            out.wait_send()
        for arrival in _peer_copies(refs[:n], refs[n:2 * n], refs[2 * n], refs[2 * n + 1], scatter, True):
            arrival.wait_recv()

    outs = pl.pallas_call(
        body, name=name, in_specs=[HBM_SPEC] * (2 * n) + [SEM_SPEC, SEM_SPEC, ANY_SPEC], out_specs=[HBM_SPEC] * (2 * n),
        out_shape=[pltpu.HBM(a.shape, a.dtype) for a in list(sources) + list(lands)],
        input_output_aliases={i: i for i in range(2 * n)},
        compiler_params=pltpu.CompilerParams(has_side_effects=pltpu.SideEffectType.DATAFLOW_SIDE_EFFECTING),
    )(*sources, *lands, send_sems, recv_sems, after)
    return outs[n:]


def _pack_rows(parts):
    offsets, row = [], 0
    for part in parts:
        offsets.append(row)
        row += part.shape[0]
    return offsets, -(-row // 8) * 8, -(-max(part.shape[1] for part in parts) // 128) * 128


def _pack(name, parts):
    offsets, rows, width = _pack_rows(parts)

    def body(*refs):
        o_ref = refs[-1]
        o_ref[...] = jnp.zeros_like(o_ref)
        for off, ref in zip(offsets, refs[:-1]):
            o_ref[off:off + ref.shape[0], 0:ref.shape[1]] = ref[...]

    return pl.pallas_call(body, name=name, in_specs=[VMEM_SPEC] * len(parts), out_specs=VMEM_SPEC,
                          out_shape=jax.ShapeDtypeStruct((rows, width), F32))(*parts)


def _adamw_math(w, g, m, v):
    m = ADAM_B1 * m + (1.0 - ADAM_B1) * g
    v = ADAM_B2 * v + (1.0 - ADAM_B2) * (g * g)
    m_hat = m * (1.0 / (1.0 - ADAM_B1 ** ADAM_STEP))
    denom = jnp.sqrt(v * (1.0 / (1.0 - ADAM_B2 ** ADAM_STEP))) + ADAM_EPS
    inv = pl.reciprocal(denom, approx=True)
    inv = inv * (2.0 - denom * inv)
    return -ADAM_LR * (m_hat * inv + ADAM_WD * w), m, v


def _adamw_step(w_ref, m_ref, v_ref, p_ref, g_ref, d_ref, nm_ref, nv_ref):
    g = p_ref[0].astype(F32)
    for dev in range(1, N_DEV):
        g = g + p_ref[dev].astype(F32)
    g_ref[...] = g
    d_ref[...], nm_ref[...], nv_ref[...] = _adamw_math(w_ref[...], g, m_ref[...], v_ref[...])


def _adamw_rows(rows):
    return max(t for t in range(8, min(rows, 256) + 1, 8) if rows % t == 0)


def _adamw_shard(name, w, m, v, partials):
    rows, cols = w.shape
    tr = _adamw_rows(rows)
    blk = pl.BlockSpec((tr, cols), lambda i: (i, 0))
    return pl.pallas_call(
        _adamw_step_fn(), name=name, grid=(rows // tr,), in_specs=[blk, blk, blk, pl.BlockSpec((N_DEV, tr, cols), lambda i: (0, i, 0))],
        out_specs=[blk] * 4, out_shape=[jax.ShapeDtypeStruct((rows, cols), F32)] * 4, compiler_params=_params(("parallel",)),
    )(w, m, v, partials)


def _adamw_step_fn():
    return functools.partial(_adamw_step)


def _adamw_layers(name, w, m, v, partials):
    layers, rows, cols = w.shape
    tr = _adamw_rows(rows)
    last = rows // tr - 1

    def body(w_ref, m_ref, v_ref, *rest):
        for layer in range(layers):
            @pl.when(pl.program_id(0) == layer)
            def _():
                _adamw_step(w_ref, m_ref, v_ref, rest[layer], *rest[layers:])

    blk = pl.BlockSpec((None, tr, cols), lambda l, i: (l, i, 0))
    part = lambda layer: pl.BlockSpec((N_DEV, tr, cols), lambda l, i: (0, jnp.where(l == layer, i, jnp.where(l < layer, 0, last)), 0))
    return pl.pallas_call(
        body, name=name, grid=(layers, rows // tr), in_specs=[blk, blk, blk] + [part(layer) for layer in range(layers)],
        out_specs=[blk] * 4, out_shape=[jax.ShapeDtypeStruct(w.shape, F32)] * 4, compiler_params=_params(("arbitrary", "arbitrary")),
    )(w, m, v, *partials)


def _adamw_small(gathered, places, entries):
    n = len(entries)
    np_ = len(gathered)

    def body(*refs):
        pack_refs = refs[:np_]
        refs = refs[np_ - 1:]
        w_refs, m_refs, v_refs = refs[1:1 + n], refs[1 + n:1 + 2 * n], refs[1 + 2 * n:1 + 3 * n]
        outs = refs[1 + 3 * n:]
        totals = []
        for pack_ref in pack_refs:
            acc = pack_ref[0]
            for dev in range(1, N_DEV):
                acc = acc + pack_ref[dev]
            totals.append(acc)
        mine = _index(_place())
        for e in range(n):
            rows, cols = w_refs[e].shape
            total, off = totals[places[e][0]], places[e][1]
            if entries[e][3]:
                g = jnp.zeros((rows, cols), F32)
                for dev in range(N_DEV):
                    g = g + jnp.where(mine == dev, total[off + dev * rows:off + (dev + 1) * rows, 0:cols], 0.0)
            else:
                g = total[off:off + rows, 0:cols]
            outs[4 * e][...] = g
            outs[4 * e + 1][...], outs[4 * e + 2][...], outs[4 * e + 3][...] = _adamw_math(w_refs[e][...], g, m_refs[e][...], v_refs[e][...])
        outs[4 * n][...] = totals[places[n][0]][places[n][1]:places[n][1] + 1, 0:128]

    shapes = []
    for w, _, _, _ in entries:
        shapes += [jax.ShapeDtypeStruct(w.shape, F32)] * 4
    shapes.append(jax.ShapeDtypeStruct((1, 128), F32))
    return pl.pallas_call(
        body, name="adamw_small", in_specs=[VMEM_SPEC] * (np_ + 3 * n), out_specs=[VMEM_SPEC] * len(shapes), out_shape=shapes,
        compiler_params=pltpu.CompilerParams(vmem_limit_bytes=VMEM_LIMIT),
    )(*gathered, *[e[0] for e in entries], *[e[1] for e in entries], *[e[2] for e in entries])


def _ffn_forward(tag, h, gain, w_up, late):
    s, d = h.shape
    fb = w_up.shape[1]
    tm = _row_tile(s, 2 * MM_ROWS)
    a, = _rmsnorm_cast(f"ffn_norm_{tag}", h, [gain])
    u = _matmul(
        f"ffn_up_{tag}", a, w_up, dims=NT, grid=(s // tm, N_DEV, 1),
        a_spec=pl.BlockSpec((tm, d), lambda i, j, k: (i, 0)),
        b_spec=pl.BlockSpec((None, fb, d), lambda i, j, k: (j, 0, 0)),
        o_spec=pl.BlockSpec((None, None, tm, fb), lambda i, j, k: (j // 4, j % 4, i, 0)),
        out_shape=jax.ShapeDtypeStruct((2, 4, s, fb), BF16))
    w_down, conv_w, conv_b = late(u)
    hidden, conv, out = _ffn_hidden_down(f"ffn_hidden_down_{tag}", u, conv_w, conv_b, w_down, h)
    return out, (a, u, hidden, conv)


def _ffn_backward(tag, h, gain, w_up, w_down, conv_w, conv_b, saved, dout):
    a, u, hidden, conv = saved
    dout, dout_bf = dout
    s, d = h.shape
    fb = w_up.shape[1]
    tm = _row_tile(s, MM_ROWS)
    dhidden = _matmul(
        f"ffn_down_bwd_{tag}", dout_bf, w_down, dims=NT, grid=(s // tm, 4, 1),
        a_spec=pl.BlockSpec((tm, d), lambda i, j, k: (i, 0)),
        b_spec=pl.BlockSpec((None, fb, d), lambda i, j, k: (j, 0, 0)),
        o_spec=pl.BlockSpec((None, tm, fb), lambda i, j, k: (j, i, 0)),
        out_shape=jax.ShapeDtypeStruct((4, s, fb), BF16))
    dw_down = _matmul(
        f"ffn_down_grad_{tag}", hidden, dout_bf, dims=TN, grid=(4, 1, 1),
        a_spec=pl.BlockSpec((None, s, fb), lambda i, j, k: (i, 0, 0)),
        b_spec=pl.BlockSpec((s, d), lambda i, j, k: (0, 0)),
        o_spec=pl.BlockSpec((None, fb, d), lambda i, j, k: (i, 0, 0)),
        out_shape=jax.ShapeDtypeStruct((4, fb, d), BF16))
    du, dconv_w, dconv_b, dh, dh_bf, dgain = _ffn_hidden_up_bwd(f"ffn_hidden_up_bwd_{tag}", u, conv, dhidden, conv_w, w_up, h, gain, dout)
    dw_up = _matmul(
        f"ffn_up_grad_{tag}", du, a, dims=TN, grid=(N_DEV, 1, 1),
        a_spec=pl.BlockSpec((None, None, s, fb), lambda i, j, k: (i // 4, i % 4, 0, 0)),
        b_spec=pl.BlockSpec((s, d), lambda i, j, k: (0, 0)),
        o_spec=pl.BlockSpec((None, fb, d), lambda i, j, k: (i, 0, 0)),
        out_shape=jax.ShapeDtypeStruct((N_DEV, fb, d), BF16))
    return (dh, dh_bf), dgain, dw_up, dw_down, dconv_w, dconv_b


def kernel(x, hg_norm, hg_w_in, hg_lb_logits, hg_out_norm, hg_w_out, kv_norm, w_kv, attn_norm, attn_w_q, attn_sinks, attn_w_o, ffn_norm, ffn_w_up, ffn_conv_w, ffn_conv_b, ffn_w_down, final_norm, loss_target, m_hg_norm, m_hg_w_in, m_hg_lb_logits, m_hg_out_norm, m_hg_w_out, m_kv_norm, m_w_kv, m_attn_norm, m_attn_w_q, m_attn_sinks, m_attn_w_o, m_ffn_norm, m_ffn_w_up, m_ffn_conv_w, m_ffn_conv_b, m_ffn_w_down, m_final_norm, v_hg_norm, v_hg_w_in, v_hg_lb_logits, v_hg_out_norm, v_hg_w_out, v_kv_norm, v_w_kv, v_attn_norm, v_attn_w_q, v_attn_sinks, v_attn_w_o, v_ffn_norm, v_ffn_w_up, v_ffn_conv_w, v_ffn_conv_b, v_ffn_w_down, v_final_norm):
    _, s, d = x.shape
    x0, target = x[0], loss_target[0]
    half = hg_w_in.shape[2]
    fs = ffn_conv_w.shape[2]
    fb = 2 * fs
    kvd = w_kv.shape[1]
    nq = d // ATT_HEAD_DIM
    tm = _row_tile(s, MM_ROWS)

    mine = _index(_place())
    gather = lambda tag, shards, after: _send_start("gather_start_" + tag, shards, [_landing(a, mine) for a in shards], False, after)
    w_in, g_hgn, g_lbl, w_out = _all_gather("gather_hg", [hg_w_in[0].astype(BF16), hg_norm, hg_lb_logits, hg_w_out[0].astype(BF16)], HBM_SPEC)
    w_out = w_out.reshape(d, d)
    up_t = lambda a: jnp.swapaxes(a, -1, -2)
    coming_up0 = gather("ffn_up0", [up_t(ffn_w_up[0]).astype(BF16)], g_hgn)
    hgn = _pinned(g_hgn.reshape(1, d), coming_up0[4])
    lbl = g_lbl.transpose(1, 0, 2).reshape(2, d)
    conv_b = [ffn_conv_b[layer].reshape(4, 1, fb) for layer in range(2)]
    gains = [ffn_norm[0:1], ffn_norm[1:2]]
    kvn, fin = kv_norm.reshape(1, d), final_norm.reshape(1, d)

    a0, = _rmsnorm_cast("hg_norm", x0, [hgn])
    t2 = _row_tile(s, 2 * MM_ROWS)
    p = _matmul(
        "hg_in", a0, w_in, dims=NN, grid=(s // t2, N_DEV, 1),
        a_spec=pl.BlockSpec((t2, d), lambda i, j, k: (i, 0)),
        b_spec=pl.BlockSpec((None, d, half), lambda i, j, k: (j, 0, 0)),
        o_spec=pl.BlockSpec((None, t2, half), lambda i, j, k: (j // 2, i, j % 2)),
        out_shape=jax.ShapeDtypeStruct((4, s, d), BF16), acc_shape=(8, 128))
    o, og, states, gsum = _hgrn2_fwd(p, lbl, hg_out_norm)
    coming_dn0 = gather("ffn_down0", [ffn_conv_w, ffn_w_down[0].astype(BF16)], o)
    x1 = _mm_rows("hg_out", og, _pinned(w_out, coming_dn0[4]), out_dtype=F32, add=x0)
    w_up0, = _send_wait("gather_wait_ffn_up0", coming_up0, x1, False)
    coming_attn = gather("attn", [w_kv.astype(BF16), attn_w_q[0].astype(BF16), attn_w_o[0].astype(BF16)], w_up0)
    gains[0] = _pinned(gains[0], coming_attn[4])
    w_up, w_dn, conv_w, coming = [w_up0, None], [None, None], [], {}

    def late0(u):
        g_cw, w_dn0 = _send_wait("gather_wait_ffn_down0", coming_dn0, u, False)
        w_dn[0] = w_dn0.reshape(4, fb, d)
        conv_w.extend(g_cw[:, layer].reshape(4, 2, CONV_WIDTH, fs).transpose(0, 2, 1, 3).reshape(4, CONV_WIDTH, fb) for layer in range(2))
        coming["up1"] = gather("ffn_up1", [up_t(ffn_w_up[1]).astype(BF16)], w_dn0)
        return w_dn[0], conv_w[0], _pinned(conv_b[0], coming["up1"][4])

    x2, saved0 = _ffn_forward("0", x1, gains[0], w_up[0], late0)
    w_kvg, w_q, w_o = _send_wait("gather_wait_attn", coming_attn, x2, False)
    w_kvg, w_q, w_o = w_kvg.reshape(d, kvd), w_q.reshape(d, d), w_o.reshape(d, d)
    akv, a2 = _rmsnorm_cast("attn_norms", x2, [kvn, attn_norm])
    kv = _mm_rows("kv_proj", akv, w_kvg, out_dtype=BF16)
    q = _mm_rows("q_proj", a2, w_q, out_dtype=BF16)
    att = _attn_fwd(q, kv, attn_sinks)
    coming_dn1 = gather("ffn_down1", [ffn_w_down[1].astype(BF16)], att)
    x3 = _mm_rows("attn_out", att, _pinned(w_o, coming_dn1[4]), out_dtype=F32, add=x2)
    w_up[1], = _send_wait("gather_wait_ffn_up1", coming["up1"], x3, False)

    def late1(u):
        w_dn[1] = _send_wait("gather_wait_ffn_down1", coming_dn1, u, False)[0].reshape(4, fb, d)
        return w_dn[1], conv_w[1], conv_b[1]

    x4, saved1 = _ffn_forward("1", x3, gains[1], w_up[1], late1)
    dx4, dx4_bf, d_fin, loss_part = _loss_head(x4, fin, target)

    dx3, d_fn1, dw_up1, dw_dn1, dcw1, dcb1 = _ffn_backward("1", x3, gains[1], w_up[1], w_dn[1], conv_w[1], conv_b[1], saved1, (dx4, dx4_bf))
    rows = d // N_DEV
    scatter = lambda tag, stacks: _send_start("scatter_start_" + tag, stacks, [_landing(lax.dynamic_index_in_dim(a, mine, keepdims=False), mine) for a in stacks], True)
    going_ffn1 = scatter("ffn1", [dw_up1, dw_dn1.reshape(N_DEV, fs, d)])
    datt = _mm_rows_nt("attn_out_bwd", dx3[1], w_o, out_dtype=BF16)
    dw_o = _mm_tn("attn_out_grad", att, dx3[1])
    dq, dkv_own, dkv_before, dsink = _attn_bwd(q, kv, att, datt, _pinned(attn_sinks, going_ffn1[4]))
    tiles = dkv_before.shape[0]
    dkv = dkv_own.reshape(tiles, s // tiles, kvd)
    dkv = jnp.concatenate([dkv[:, :-WINDOW], dkv[:, -WINDOW:] + jnp.pad(dkv_before[1:], ((0, 1), (0, 0), (0, 0)))], axis=1).reshape(s, kvd)
    dw_q = _mm_tn("q_proj_grad", a2, dq)
    dw_kv = _mm_tn("kv_proj_grad", akv, dkv)
    going_attn = scatter("attn", [dw_kv.reshape(N_DEV, rows, kvd), dw_q.reshape(N_DEV, rows, d), dw_o.reshape(N_DEV, rows, d)])
    whole = lambda a_ref, b_ref: [(a_ref[...], b_ref[...])]
    rows_of = lambda width: (lambda tile: pl.BlockSpec((tile, width), lambda i: (i, 0)))
    dx2, (d_kvn, d_attn) = _proj_norm_bwd("attn_in_bwd", x2, dx3[0], [
        (dkv, rows_of(kvd), w_kvg, pl.BlockSpec((d, kvd), lambda i: (0, 0)), whole, _pinned(kvn, going_attn[4])),
        (dq, rows_of(d), w_q, pl.BlockSpec((d, d), lambda i: (0, 0)), whole, attn_norm)])
    dx1, d_fn0, dw_up0, dw_dn0, dcw0, dcb0 = _ffn_backward("0", x1, gains[0], w_up[0], w_dn[0], conv_w[0], conv_b[0], saved0, dx2)
    dw_out = _mm_tn("hg_out_grad", og, dx1[1])
    going_ffn0 = scatter("ffn0", [dw_up0, dw_dn0.reshape(N_DEV, fs, d), dw_out.reshape(N_DEV, rows, d)])
    dog = _mm_rows_nt("hg_out_bwd", dx1[1], w_out, out_dtype=F32)
    dp, d_lbl, d_ogain = _hgrn2_bwd(p, lbl, _pinned(hg_out_norm, going_ffn0[4]), o, dog, states, gsum)
    dw_in = _matmul(
        "hg_in_grad", a0, dp, dims=TN, grid=(1, N_DEV, 1),
        a_spec=pl.BlockSpec((s, d), lambda i, j, k: (0, 0)),
        b_spec=pl.BlockSpec((None, s, half), lambda i, j, k: (j // 2, 0, j % 2)),
        o_spec=pl.BlockSpec((None, d, half), lambda i, j, k: (j, 0, 0)),
        out_shape=jax.ShapeDtypeStruct((N_DEV, d, half), BF16))
    going_hg = scatter("hg", [dw_in])
    (dx0, _), (d_hgn,) = _proj_norm_bwd("hg_in_bwd", x0, dx1[0], [
        (dp, lambda tile: pl.BlockSpec((4, tile, d), lambda i: (0, i, 0)), w_in, pl.BlockSpec((N_DEV, d, half), lambda i: (0, 0, 0)),
         lambda g_ref, w_ref: [(g_ref[k // 2, :, (k % 2) * half:(k % 2 + 1) * half], w_ref[k]) for k in range(N_DEV)],
         _pinned(hgn, going_hg[4]))])

    as_blocks = lambda a, r: a.reshape(r, N_DEV, -1).transpose(1, 0, 2).reshape(N_DEV * r, -1)
    d_cw = jnp.concatenate([g.transpose(1, 0, 2).reshape(CONV_WIDTH, 4 * fb) for g in (dcw0, dcw1)], axis=0)
    parts = [d_fin, jnp.concatenate([d_fn0, d_fn1], axis=0), jnp.concatenate([dcb0.reshape(1, 4 * fb), dcb1.reshape(1, 4 * fb)], axis=0),
             as_blocks(d_cw, 2 * CONV_WIDTH), d_attn, jnp.sum(dsink[:, :, 0], axis=0).reshape(1, nq), d_kvn, d_ogain,
             as_blocks(d_hgn, 1), as_blocks(d_lbl, 2), loss_part]
    wide = [2]
    packs = [[parts[i] for i in wide], [part for i, part in enumerate(parts) if i not in wide]]
    places = [None] * len(parts)
    for which, members in enumerate([wide, [i for i in range(len(parts)) if i not in wide]]):
        for i, off in zip(members, _pack_rows(packs[which])[0]):
            places[i] = (which, off)
    packed = [_pack("pack_wide_grads", packs[0]), _pack("pack_narrow_grads", packs[1])]
    going_small = _send_start("small_grads_start", packed, [_landing(a, mine) for a in packed], False)

    arrive = lambda tag, going, after: _send_wait("scatter_wait_" + tag, going, after, True)
    (l_up1, l_dn1), (l_kv, l_q, l_o), (l_up0, l_dn0, l_out) = (
        arrive("ffn1", going_ffn1, going_small[4]), arrive("attn", going_attn, going_small[4]), arrive("ffn0", going_ffn0, going_small[4]))
    big = {}
    for tag, w, m, v, part in [
            ("w_kv", w_kv, m_w_kv, v_w_kv, l_kv), ("attn_w_q", attn_w_q[0], m_attn_w_q[0], v_attn_w_q[0], l_q),
            ("attn_w_o", attn_w_o[0], m_attn_w_o[0], v_attn_w_o[0], l_o)]:
        big[tag] = _adamw_shard("adamw_" + tag, w, m, v, part)
    big["ffn_w_up"] = [up_t(a) for a in _adamw_layers("adamw_ffn_w_up", up_t(ffn_w_up), up_t(m_ffn_w_up), up_t(v_ffn_w_up), (l_up0, l_up1))]
    big["ffn_w_down"] = _adamw_layers("adamw_ffn_w_down", ffn_w_down, m_ffn_w_down, v_ffn_w_down, (l_dn0, l_dn1))
    lead = lambda tag: [a[None] for a in big[tag]]

    both_done = big["ffn_w_up"][0][0, 0:1, 0:1] + big["ffn_w_down"][0][0, 0:1, 0:1]
    gathered = _send_wait("small_grads_wait", going_small, both_done, False)
    two = lambda a: a.reshape(-1, a.shape[-1])
    small = [(fin, m_final_norm.reshape(1, d), v_final_norm.reshape(1, d), False), (ffn_norm, m_ffn_norm, v_ffn_norm, False),
             (ffn_conv_b, m_ffn_conv_b, v_ffn_conv_b, False), (two(ffn_conv_w), two(m_ffn_conv_w), two(v_ffn_conv_w), True),
             (attn_norm, m_attn_norm, v_attn_norm, False), (attn_sinks, m_attn_sinks, v_attn_sinks, False),
             (kvn, m_kv_norm.reshape(1, d), v_kv_norm.reshape(1, d), False), (hg_out_norm, m_hg_out_norm, v_hg_out_norm, False),
             (hg_norm, m_hg_norm, v_hg_norm, True), (hg_lb_logits, m_hg_lb_logits, v_hg_lb_logits, True)]
    res = _adamw_small(gathered, places, small)
    l_in, = arrive("hg", going_hg, gathered[1])
    big["hg_w_in"] = _adamw_shard("adamw_hg_w_in", hg_w_in[0], m_hg_w_in[0], v_hg_w_in[0], l_in)
    big["hg_w_out"] = _adamw_shard("adamw_hg_w_out", hg_w_out[0], m_hg_w_out[0], v_hg_w_out[0], l_out)
    names = ["final_norm", "ffn_norm", "ffn_conv_b", "ffn_conv_w", "attn_norm", "attn_sinks", "kv_norm", "hg_out_norm", "hg_norm", "hg_lb_logits"]
    shapes = {"final_norm": final_norm.shape, "kv_norm": kv_norm.shape, "ffn_conv_w": ffn_conv_w.shape}
    out = {n: [a.reshape(shapes[n]) if n in shapes else a for a in res[4 * i:4 * i + 4]] for i, n in enumerate(names)}
    out.update(hg_w_in=lead("hg_w_in"), hg_w_out=lead("hg_w_out"), w_kv=big["w_kv"], attn_w_q=lead("attn_w_q"), attn_w_o=lead("attn_w_o"),
               ffn_w_up=big["ffn_w_up"], ffn_w_down=big["ffn_w_down"])
    order = ["hg_norm", "hg_w_in", "hg_lb_logits", "hg_out_norm", "hg_w_out", "kv_norm", "w_kv", "attn_norm", "attn_w_q", "attn_sinks",
             "attn_w_o", "ffn_norm", "ffn_w_up", "ffn_conv_w", "ffn_conv_b", "ffn_w_down", "final_norm"]
    loss = res[-1][0, 0]
    return (loss, dx0[None], *[out[n][0] for n in order], *[out[n][1] for n in order], *[out[n][2] for n in order], *[out[n][3] for n in order])
```

```python
import functools

import jax
import jax.numpy as jnp
from jax import lax
from jax.experimental import pallas as pl
from jax.experimental.pallas import tpu as pltpu

F32 = jnp.float32
BF16 = jnp.bfloat16

EPS = 1e-6
HG_EXPAND = 128
HG_CHUNK = 32
ATT_HEAD_DIM = 64
ATT_KV_HEADS = 2
WINDOW = 128
CONV_WIDTH = 3
ADAM_LR = 0.001
ADAM_B1 = 0.9
ADAM_B2 = 0.999
ADAM_EPS = 1e-08
ADAM_WD = 0.01
ADAM_STEP = 10

N_DEV = 8
VMEM_LIMIT = 48 * 1024 * 1024
NEG = -1e30

NN = (((1,), (0,)), ((), ()))
NT = (((1,), (1,)), ((), ()))
TN = (((0,), (0,)), ((), ()))
MESH = pl.DeviceIdType.MESH


def _dot(a, b, dims=NN):
    return lax.dot_general(a.astype(BF16), b.astype(BF16), dims, preferred_element_type=F32)


def _sigmoid(x):
    return 0.5 * jnp.tanh(0.5 * x) + 0.5


def _silu(x):
    return x * _sigmoid(x)


def _silu_and_grad(x):
    s = _sigmoid(x)
    return x * s, s * (1.0 + x * (1.0 - s))


def _dsilu(x):
    return _silu_and_grad(x)[1]


def _params(semantics):
    return pltpu.CompilerParams(dimension_semantics=semantics, vmem_limit_bytes=VMEM_LIMIT)


def _row_tile(rows, want=512):
    return min(rows, want)


MM_ROWS = 1024


def _matmul(name, a, b, *, dims, grid, a_spec, b_spec, o_spec, out_shape, acc_shape=(8, 128), add=None, add_spec=None, terms=None):
    nk = grid[2]

    def body(*refs):
        if add is None:
            a_ref, b_ref, o_ref, acc = refs
        else:
            a_ref, b_ref, add_ref, o_ref, acc = refs
        k = pl.program_id(2)
        pairs = [(a_ref[...], b_ref[...])] if terms is None else terms(a_ref, b_ref)
        part = _dot(*pairs[0], dims)
        for pair in pairs[1:]:
            part = part + _dot(*pair, dims)

        def finish(total):
            if add is not None:
                total = total + add_ref[...]
            o_ref[...] = total.astype(o_ref.dtype)

        if nk == 1:
            finish(part)
        else:
            @pl.when(k == 0)
            def _():
                acc[...] = part

            @pl.when(k > 0)
            def _():
                acc[...] += part

            @pl.when(k == nk - 1)
            def _():
                finish(acc[...])

    in_specs = [a_spec, b_spec] + ([] if add is None else [add_spec])
    args = (a, b) + (() if add is None else (add,))
    return pl.pallas_call(
        body, name=name, grid=grid, in_specs=in_specs, out_specs=o_spec, out_shape=out_shape,
        scratch_shapes=[pltpu.VMEM(acc_shape, F32)],
        compiler_params=_params(("parallel", "parallel", "arbitrary")),
    )(*args)


def _mm_rows(name, a, w, *, out_dtype, add=None):
    s, kdim = a.shape
    n = w.shape[1]
    tm = _row_tile(s, MM_ROWS)
    return _matmul(
        name, a, w, dims=NN, grid=(s // tm, 1, 1),
        a_spec=pl.BlockSpec((tm, kdim), lambda i, j, k: (i, 0)),
        b_spec=pl.BlockSpec((kdim, n), lambda i, j, k: (0, 0)),
        o_spec=pl.BlockSpec((tm, n), lambda i, j, k: (i, 0)),
        out_shape=jax.ShapeDtypeStruct((s, n), out_dtype), acc_shape=(8, 128),
        add=add, add_spec=None if add is None else pl.BlockSpec((tm, n), lambda i, j, k: (i, 0)),
    )


def _mm_rows_nt(name, a, w, *, out_dtype):
    s, n = a.shape
    kdim = w.shape[0]
    tm = _row_tile(s, MM_ROWS)
    return _matmul(
        name, a, w, dims=NT, grid=(s // tm, 1, 1),
        a_spec=pl.BlockSpec((tm, n), lambda i, j, k: (i, 0)),
        b_spec=pl.BlockSpec((kdim, n), lambda i, j, k: (0, 0)),
        o_spec=pl.BlockSpec((tm, kdim), lambda i, j, k: (i, 0)),
        out_shape=jax.ShapeDtypeStruct((s, kdim), out_dtype), acc_shape=(8, 128),
    )


def _mm_tn(name, a, g):
    s, m = a.shape
    n = g.shape[1]
    tn = min(n, 512)
    return _matmul(
        name, a, g, dims=TN, grid=(1, n // tn, 1),
        a_spec=pl.BlockSpec((s, m), lambda i, j, k: (0, 0)),
        b_spec=pl.BlockSpec((s, tn), lambda i, j, k: (0, j)),
        o_spec=pl.BlockSpec((m, tn), lambda i, j, k: (0, j)),
        out_shape=jax.ShapeDtypeStruct((m, n), BF16),
    )


def _rmsnorm_cast(name, h, gains):
    s, d = h.shape
    tm = _row_tile(s)
    n = len(gains)

    def body(*refs):
        h_ref, g_refs, o_refs = refs[0], refs[1:1 + n], refs[1 + n:]
        xv = h_ref[...]
        xhat = xv * lax.rsqrt(jnp.mean(xv * xv, axis=-1, keepdims=True) + EPS)
        for g_ref, o_ref in zip(g_refs, o_refs):
            o_ref[...] = (xhat * g_ref[...]).astype(BF16)

    row = pl.BlockSpec((tm, d), lambda i: (i, 0))
    vec = pl.BlockSpec((1, d), lambda i: (0, 0))
    return pl.pallas_call(
        body, name=name, grid=(s // tm,), in_specs=[row] + [vec] * n, out_specs=[row] * n,
        out_shape=[jax.ShapeDtypeStruct((s, d), BF16)] * n, compiler_params=_params(("parallel",)),
    )(h, *gains)


def _proj_norm_bwd(name, h, dres, branches):
    s, d = h.shape
    tm = _row_tile(s)
    n = len(branches)

    def body(*refs):
        h_ref, dres_ref = refs[0], refs[1]
        g_refs, w_refs, gain_refs = refs[2:2 + n], refs[2 + n:2 + 2 * n], refs[2 + 2 * n:2 + 3 * n]
        dh_ref, dhb_ref, dg_refs = refs[2 + 3 * n], refs[3 + 3 * n], refs[4 + 3 * n:]
        i = pl.program_id(0)
        xv = h_ref[...]
        r = lax.rsqrt(jnp.mean(xv * xv, axis=-1, keepdims=True) + EPS)
        xhat = xv * r
        total = dres_ref[...]
        for branch, g_ref, w_ref, gain_ref, dg_ref in zip(branches, g_refs, w_refs, gain_refs, dg_refs):
            pairs = branch[4](g_ref, w_ref)
            da = _dot(*pairs[0], NT)
            for pair in pairs[1:]:
                da = da + _dot(*pair, NT)
            dgain = jnp.sum(da * xhat, axis=0, keepdims=True)

            @pl.when(i == 0)
            def _():
                dg_ref[...] = dgain

            @pl.when(i > 0)
            def _():
                dg_ref[...] += dgain

            dxhat = da * gain_ref[...]
            total = total + r * (dxhat - xhat * jnp.mean(dxhat * xhat, axis=-1, keepdims=True))
        dh_ref[...] = total
        dhb_ref[...] = total.astype(BF16)

    row = pl.BlockSpec((tm, d), lambda i: (i, 0))
    vec = pl.BlockSpec((1, d), lambda i: (0, 0))
    outs = pl.pallas_call(
        body, name=name, grid=(s // tm,),
        in_specs=[row, row] + [b[1](tm) for b in branches] + [b[3] for b in branches] + [vec] * n, out_specs=[row, row] + [vec] * n,
        out_shape=[jax.ShapeDtypeStruct((s, d), F32), jax.ShapeDtypeStruct((s, d), BF16)] + [jax.ShapeDtypeStruct((1, d), F32)] * n,
        compiler_params=_params(("arbitrary",)),
    )(h, dres, *[b[0] for b in branches], *[b[2] for b in branches], *[b[5] for b in branches])
    return (outs[0], outs[1]), outs[2:]


def _loss_head(h, gain, target):
    s, d = h.shape
    tm = _row_tile(s)

    def body(h_ref, g_ref, t_ref, dh_ref, dhb_ref, dg_ref, loss_ref):
        i = pl.program_id(0)
        xv = h_ref[...]
        r = lax.rsqrt(jnp.mean(xv * xv, axis=-1, keepdims=True) + EPS)
        xhat = xv * r
        err = xhat * g_ref[...] - t_ref[...]
        dy = err * (1.0 / d)
        part = jnp.zeros((1, 128), F32) + 0.5 * jnp.sum(jnp.mean(err * err, axis=-1, keepdims=True))
        dgain = jnp.sum(dy * xhat, axis=0, keepdims=True)

        @pl.when(i == 0)
        def _():
            dg_ref[...] = dgain
            loss_ref[...] = part

        @pl.when(i > 0)
        def _():
            dg_ref[...] += dgain
            loss_ref[...] += part

        dxhat = dy * g_ref[...]
        dh = r * (dxhat - xhat * jnp.mean(dxhat * xhat, axis=-1, keepdims=True))
        dh_ref[...] = dh
        dhb_ref[...] = dh.astype(BF16)

    row = pl.BlockSpec((tm, d), lambda i: (i, 0))
    vec = pl.BlockSpec((1, d), lambda i: (0, 0))
    return pl.pallas_call(
        body, name="loss_head", grid=(s // tm,), in_specs=[row, vec, row],
        out_specs=[row, row, vec, pl.BlockSpec((1, 128), lambda i: (0, 0))],
        out_shape=[jax.ShapeDtypeStruct((s, d), F32), jax.ShapeDtypeStruct((s, d), BF16), jax.ShapeDtypeStruct((1, d), F32),
                   jax.ShapeDtypeStruct((1, 128), F32)],
        compiler_params=_params(("arbitrary",)),
    )(h, gain, target)


def _bdot(a, b, ca, cb):
    return lax.dot_general(a.astype(BF16), b.astype(BF16), (((ca,), (cb,)), ((0,), (0,))), preferred_element_type=F32)


def _chunk_cumsum(xv, reverse=False):
    n = xv.shape[0]
    row = lax.broadcasted_iota(jnp.int32, xv.shape, 0) % HG_CHUNK
    step = 1
    while step < HG_CHUNK:
        if reverse:
            xv = xv + jnp.where(row < HG_CHUNK - step, pltpu.roll(xv, n - step, axis=0), 0.0)
        else:
            xv = xv + jnp.where(row >= step, pltpu.roll(xv, step, axis=0), 0.0)
        step *= 2
    return xv


def _hg_terms(p_ref, lbl_ref, g_ref=None):
    pq = p_ref[0].astype(F32)
    pf = p_ref[1].astype(F32)
    lb = _sigmoid(lbl_ref[0:1, :] - lbl_ref[1:2, :])
    sig = _sigmoid(pf)
    fg = lb + (1.0 - lb) * sig
    nc = pq.shape[0] // HG_CHUNK
    chunks = lambda a: a.reshape(nc, HG_CHUNK, HG_EXPAND)
    q = chunks(_silu(pq) * HG_EXPAND ** -0.5)
    k = chunks(1.0 - fg)
    v = chunks(p_ref[2].astype(F32))
    g = chunks(_chunk_cumsum(jnp.log(fg)) if g_ref is None else g_ref[...])
    gm = g[:, HG_CHUNK // 2 - 1:HG_CHUNK // 2, :]
    gl = g[:, HG_CHUNK - 1:HG_CHUNK, :]
    e_mid, e_inv, e_all, e_end = jnp.exp(g - gm), jnp.exp(gm - g), jnp.exp(g), jnp.exp(gl - g)
    terms = dict(q=q, k=k, v=v, g=g, qd=q * e_all, qt=q * e_mid, kt=k * e_inv, kd=k * e_end, e_last=jnp.exp(gl),
                 e_mid=e_mid, e_inv=e_inv, e_all=e_all, e_end=e_end)
    return terms, (pq, sig, fg, lb)


def _causal(nc):
    r = lax.broadcasted_iota(jnp.int32, (nc, HG_CHUNK, HG_CHUNK), 1)
    c = lax.broadcasted_iota(jnp.int32, (nc, HG_CHUNK, HG_CHUNK), 2)
    return r >= c


def _hgrn2_fwd(p, lb_logits, out_gain):
    _, s, d = p.shape
    heads = d // HG_EXPAND
    t = _row_tile(s, 2048)
    nc = t // HG_CHUNK

    def body(p_ref, lbl_ref, gain_ref, o_ref, og_ref, st_ref, g_ref, state, decay):
        @pl.when(pl.program_id(1) == 0)
        def _():
            state[...] = jnp.zeros_like(state)

        tm, _ = _hg_terms(p_ref, lbl_ref)
        g_ref[...] = tm["g"].reshape(t, HG_EXPAND)
        decay[...] = tm["e_last"]
        st_ref[...] = _bdot(tm["v"], tm["kd"], 1, 1)

        def chunk(c, carry):
            add = st_ref[c]
            st = state[...]
            st_ref[c] = st
            state[...] = st * decay[c] + add
            return carry

        lax.fori_loop(0, nc, chunk, 0)
        a = jnp.where(_causal(nc), _bdot(tm["qt"], tm["kt"], 2, 2), 0.0)
        ov = (_bdot(tm["qd"], st_ref[...], 2, 2) + _bdot(a, tm["v"], 2, 1)).reshape(t, HG_EXPAND)
        o_ref[...] = ov
        on = ov * lax.rsqrt(jnp.mean(ov * ov, axis=-1, keepdims=True) + EPS) * gain_ref[...]
        og_ref[...] = (on * _silu(p_ref[3].astype(F32))).astype(BF16)

    blk = pl.BlockSpec((t, HG_EXPAND), lambda h, b: (b, h))
    return pl.pallas_call(
        body, name="hgrn2_fwd", grid=(heads, s // t),
        in_specs=[pl.BlockSpec((4, t, HG_EXPAND), lambda h, b: (0, b, h)), pl.BlockSpec((2, HG_EXPAND), lambda h, b: (0, h)),
                  pl.BlockSpec((1, HG_EXPAND), lambda h, b: (0, 0))],
        out_specs=[blk, blk, pl.BlockSpec((None, nc, HG_EXPAND, HG_EXPAND), lambda h, b: (h, b, 0, 0)), blk],
        out_shape=[jax.ShapeDtypeStruct((s, d), F32), jax.ShapeDtypeStruct((s, d), BF16),
                   jax.ShapeDtypeStruct((heads, s // HG_CHUNK, HG_EXPAND, HG_EXPAND), F32), jax.ShapeDtypeStruct((s, d), F32)],
        scratch_shapes=[pltpu.VMEM((HG_EXPAND, HG_EXPAND), F32), pltpu.VMEM((nc, 1, HG_EXPAND), F32)],
        compiler_params=_params(("parallel", "arbitrary")),
    )(p, lb_logits, out_gain)


def _hgrn2_bwd(p, lb_logits, out_gain, o, dog, states, gsum):
    _, s, d = p.shape
    heads = d // HG_EXPAND
    t = _row_tile(s, 1024)
    nc = t // HG_CHUNK
    nb = s // t

    def body(p_ref, lbl_ref, gain_ref, o_ref, dog_ref, st_ref, g_ref, dp_ref, dlbl_ref, dgain_ref, dstate, decay, dst_s):
        h, b = pl.program_id(0), pl.program_id(1)

        @pl.when(b == 0)
        def _():
            dstate[...] = jnp.zeros_like(dstate)

        tm, (pq, sig, fg, lb) = _hg_terms(p_ref, lbl_ref, g_ref)
        pg = p_ref[3].astype(F32)
        ov = o_ref[...]
        r = lax.rsqrt(jnp.mean(ov * ov, axis=-1, keepdims=True) + EPS)
        ohat = ov * r
        dogv = dog_ref[...]
        d_on = dogv * _silu(pg)
        dp_ref[3] = (dogv * ohat * gain_ref[...] * _dsilu(pg)).astype(BF16)
        dgain = jnp.sum(d_on * ohat, axis=0, keepdims=True)

        @pl.when((h == 0) & (b == 0))
        def _():
            dgain_ref[...] = dgain

        @pl.when((h > 0) | (b > 0))
        def _():
            dgain_ref[...] += dgain

        dohat = d_on * gain_ref[...]
        do = (r * (dohat - ohat * jnp.mean(dohat * ohat, axis=-1, keepdims=True))).reshape(nc, HG_CHUNK, HG_EXPAND)

        decay[...] = tm["e_last"]
        dst_s[...] = _bdot(do, tm["qd"], 1, 1)

        def chunk(i, carry):
            c = nc - 1 - i
            add = dst_s[c]
            dst = dstate[...]
            dst_s[c] = dst
            dstate[...] = dst * decay[c] + add
            return carry

        lax.fori_loop(0, nc, chunk, 0)
        st, dst = st_ref[...], dst_s[...]
        causal = _causal(nc)
        a = jnp.where(causal, _bdot(tm["qt"], tm["kt"], 2, 2), 0.0)
        da = jnp.where(causal, _bdot(do, tm["v"], 2, 2), 0.0)
        dqt = _bdot(da, tm["kt"], 2, 1)
        dkt = _bdot(da, tm["qt"], 1, 1)
        dqd = _bdot(do, st, 2, 1)
        dkd = _bdot(tm["v"], dst, 2, 1)
        dv = _bdot(a, do, 1, 1) + _bdot(tm["kd"], dst, 2, 2)
        dq = dqt * tm["e_mid"] + dqd * tm["e_all"]
        dk = dkt * tm["e_inv"] + dkd * tm["e_end"]
        dg = dqt * tm["qt"] - dkt * tm["kt"] + dqd * tm["qd"] - dkd * tm["kd"]
        dgl = jnp.sum(dkd * tm["kd"], axis=1, keepdims=True) + tm["e_last"] * jnp.sum(dst * st, axis=1, keepdims=True)
        last_row = lax.broadcasted_iota(jnp.int32, (nc, HG_CHUNK, HG_EXPAND), 1) == HG_CHUNK - 1
        flat = lambda a3: a3.reshape(t, HG_EXPAND)
        dlf = _chunk_cumsum(flat(dg + jnp.where(last_row, dgl, 0.0)), reverse=True)
        dfg = dlf / fg - flat(dk)
        dlb = jnp.sum(dfg * (1.0 - sig), axis=0, keepdims=True)
        dl0 = dlb * lb * (1.0 - lb)
        dlbl = jnp.concatenate([dl0, -dl0], axis=0)

        @pl.when(b == 0)
        def _():
            dlbl_ref[...] = dlbl

        @pl.when(b > 0)
        def _():
            dlbl_ref[...] += dlbl

        dp_ref[0] = (flat(dq) * HG_EXPAND ** -0.5 * _dsilu(pq)).astype(BF16)
        dp_ref[1] = (dfg * (1.0 - lb) * sig * (1.0 - sig)).astype(BF16)
        dp_ref[2] = flat(dv).astype(BF16)

    blk = pl.BlockSpec((t, HG_EXPAND), lambda h, b: (nb - 1 - b, h))
    pblk = pl.BlockSpec((4, t, HG_EXPAND), lambda h, b: (0, nb - 1 - b, h))
    return pl.pallas_call(
        body, name="hgrn2_bwd", grid=(heads, nb),
        in_specs=[pblk, pl.BlockSpec((2, HG_EXPAND), lambda h, b: (0, h)), pl.BlockSpec((1, HG_EXPAND), lambda h, b: (0, 0)),
                  blk, blk, pl.BlockSpec((None, nc, HG_EXPAND, HG_EXPAND), lambda h, b: (h, nb - 1 - b, 0, 0)), blk],
        out_specs=[pblk, pl.BlockSpec((2, HG_EXPAND), lambda h, b: (0, h)), pl.BlockSpec((1, HG_EXPAND), lambda h, b: (0, 0))],
        out_shape=[jax.ShapeDtypeStruct((4, s, d), BF16), jax.ShapeDtypeStruct((2, d), F32), jax.ShapeDtypeStruct((1, HG_EXPAND), F32)],
        scratch_shapes=[pltpu.VMEM((HG_EXPAND, HG_EXPAND), F32), pltpu.VMEM((nc, 1, HG_EXPAND), F32),
                        pltpu.VMEM((nc, HG_EXPAND, HG_EXPAND), F32)],
        compiler_params=_params(("arbitrary", "arbitrary")),
    )(p, lb_logits, out_gain, o, dog, states, gsum)


HALO = 8
FFN_FWD_ROWS = 512
FFN_BWD_ROWS = 256


def _shift_down(xv, n):
    return pltpu.roll(xv, n, axis=0)


def _shift_up(xv, n):
    return pltpu.roll(xv, xv.shape[0] - n, axis=0)


def _ffn_hidden_down(name, u, conv_w, conv_b, w_down, h):
    _, nj, s, fb = u.shape
    d = w_down.shape[2]
    tm = _row_tile(s, FFN_FWD_ROWS)
    per = tm // HALO

    def body(gate_ref, prev_ref, val_ref, w_ref, b_ref, wd_ref, h_ref, hid_ref, conv_ref, o_ref):
        i = pl.program_id(0)
        total = h_ref[...]
        for j in range(nj):
            prev = jnp.where(i > 0, prev_ref[j].astype(F32), 0.0)
            ext = jnp.concatenate([prev, gate_ref[j].astype(F32)], axis=0)
            conv = b_ref[j] + w_ref[j, 2:3, :] * ext[HALO:]
            conv = conv + w_ref[j, 1:2, :] * _shift_down(ext, 1)[HALO:]
            conv = conv + w_ref[j, 0:1, :] * _shift_down(ext, 2)[HALO:]
            conv_ref[j] = conv.astype(BF16)
            hidden = (_silu(conv) * val_ref[j].astype(F32)).astype(BF16)
            hid_ref[j] = hidden
            total = total + _dot(hidden, wd_ref[j])
        o_ref[...] = total

    row = pl.BlockSpec((tm, d), lambda i: (i, 0))
    return pl.pallas_call(
        body, name=name, grid=(s // tm,),
        in_specs=[pl.BlockSpec((None, nj, tm, fb), lambda i: (0, 0, i, 0)),
                  pl.BlockSpec((None, nj, HALO, fb), lambda i: (0, 0, jnp.maximum(i * per - 1, 0), 0)),
                  pl.BlockSpec((None, nj, tm, fb), lambda i: (1, 0, i, 0)),
                  pl.BlockSpec((nj, CONV_WIDTH, fb), lambda i: (0, 0, 0)), pl.BlockSpec((nj, 1, fb), lambda i: (0, 0, 0)),
                  pl.BlockSpec((nj, fb, d), lambda i: (0, 0, 0)), row],
        out_specs=[pl.BlockSpec((nj, tm, fb), lambda i: (0, i, 0)), pl.BlockSpec((nj, tm, fb), lambda i: (0, i, 0)), row],
        out_shape=[jax.ShapeDtypeStruct((nj, s, fb), BF16), jax.ShapeDtypeStruct((nj, s, fb), BF16), jax.ShapeDtypeStruct((s, d), F32)],
        compiler_params=_params(("parallel",)),
    )(u, u, u, conv_w, conv_b, w_down, h)


def _ffn_hidden_up_bwd(name, u, conv, dh, conv_w, w_up, h, gain, dres):
    _, nj, s, fb = u.shape
    d = w_up.shape[2]
    tm = _row_tile(s, FFN_BWD_ROWS)
    per = tm // HALO
    nblk = s // HALO
    ni = s // tm

    def body(gate_ref, conv_ref, cnext_ref, val_ref, vnext_ref, dh_ref, dhnext_ref, w_ref, wu_ref, h_ref, gain_ref, dres_ref,
             du_ref, dw_ref, db_ref, dx_ref, dxb_ref, dgain_ref):
        i = pl.program_id(0)
        has_next = i < ni - 1
        total = None
        for j in range(nj):
            cext = jnp.concatenate([conv_ref[j].astype(F32), cnext_ref[j].astype(F32)], axis=0)
            vext = jnp.concatenate([val_ref[j].astype(F32), vnext_ref[j].astype(F32)], axis=0)
            dhext = jnp.concatenate([dh_ref[j].astype(F32), jnp.where(has_next, dhnext_ref[j].astype(F32), 0.0)], axis=0)
            act, dact = _silu_and_grad(cext)
            dconv = dhext * vext * dact
            taps = [_shift_up(dconv, 2)[:tm], _shift_up(dconv, 1)[:tm], dconv[:tm]]
            dgate = (w_ref[j, 0:1, :] * taps[0] + w_ref[j, 1:2, :] * taps[1] + w_ref[j, 2:3, :] * taps[2]).astype(BF16)
            dval = (dhext * act)[:tm].astype(BF16)
            du_ref[0, j] = dgate
            du_ref[1, j] = dval
            part = _dot(dgate, wu_ref[j]) + _dot(dval, wu_ref[nj + j])
            total = part if total is None else total + part
            gate = gate_ref[j].astype(F32)
            dw = jnp.concatenate([jnp.sum(tap * gate, axis=0, keepdims=True) for tap in taps], axis=0)
            db = jnp.sum(taps[2], axis=0, keepdims=True)

            @pl.when(i == 0)
            def _():
                dw_ref[j] = dw
                db_ref[j] = db

            @pl.when(i > 0)
            def _():
                dw_ref[j] += dw
                db_ref[j] += db

        xv = h_ref[...]
        r = lax.rsqrt(jnp.mean(xv * xv, axis=-1, keepdims=True) + EPS)
        xhat = xv * r
        dgain = jnp.sum(total * xhat, axis=0, keepdims=True)

        @pl.when(i == 0)
        def _():
            dgain_ref[...] = dgain

        @pl.when(i > 0)
        def _():
            dgain_ref[...] += dgain

        dxhat = total * gain_ref[...]
        dx = dres_ref[...] + r * (dxhat - xhat * jnp.mean(dxhat * xhat, axis=-1, keepdims=True))
        dx_ref[...] = dx
        dxb_ref[...] = dx.astype(BF16)

    def tile(part):
        return pl.BlockSpec((None, nj, tm, fb), lambda i: (part, 0, i, 0))

    def after(part):
        return pl.BlockSpec((None, nj, HALO, fb), lambda i: (part, 0, jnp.minimum((i + 1) * per, nblk - 1), 0))

    row = pl.BlockSpec((tm, d), lambda i: (i, 0))
    own = pl.BlockSpec((nj, tm, fb), lambda i: (0, i, 0))
    nxt = pl.BlockSpec((nj, HALO, fb), lambda i: (0, jnp.minimum((i + 1) * per, nblk - 1), 0))
    return pl.pallas_call(
        body, name=name, grid=(ni,),
        in_specs=[tile(0), own, nxt, tile(1), after(1), own, nxt,
                  pl.BlockSpec((nj, CONV_WIDTH, fb), lambda i: (0, 0, 0)),
                  pl.BlockSpec((2 * nj, fb, d), lambda i: (0, 0, 0)), row, pl.BlockSpec((1, d), lambda i: (0, 0)), row],
        out_specs=[pl.BlockSpec((2, nj, tm, fb), lambda i: (0, 0, i, 0)),
                   pl.BlockSpec((nj, CONV_WIDTH, fb), lambda i: (0, 0, 0)), pl.BlockSpec((nj, 1, fb), lambda i: (0, 0, 0)),
                   row, row, pl.BlockSpec((1, d), lambda i: (0, 0))],
        out_shape=[jax.ShapeDtypeStruct((2, nj, s, fb), BF16), jax.ShapeDtypeStruct((nj, CONV_WIDTH, fb), F32),
                   jax.ShapeDtypeStruct((nj, 1, fb), F32), jax.ShapeDtypeStruct((s, d), F32), jax.ShapeDtypeStruct((s, d), BF16),
                   jax.ShapeDtypeStruct((1, d), F32)],
        compiler_params=_params(("arbitrary",)),
    )(u, conv, conv, u, u, dh, dh, conv_w, w_up, h, gain, dres)


ATT_TILE = 512


def _stack_heads(ref, rows, first_head, count):
    hd = ATT_HEAD_DIM
    return jnp.concatenate([ref[rows, (first_head + j) * hd:(first_head + j + 1) * hd] for j in range(count)], axis=0)


def _unstack_heads(stacked, ref, rows, first_head, count):
    hd = ATT_HEAD_DIM
    for pair in range(count // 2):
        both = [stacked[(2 * pair + j) * WINDOW:(2 * pair + j + 1) * WINDOW, :] for j in range(2)]
        ref[rows, (first_head + 2 * pair) * hd:(first_head + 2 * pair + 2) * hd] = jnp.concatenate(both, axis=1).astype(ref.dtype)


def _attn_bias(first_head, count, n_heads, first):
    lanes = count * WINDOW
    ik = lax.broadcasted_iota(jnp.int32, (2 * WINDOW, lanes), 0)
    iq = lax.broadcasted_iota(jnp.int32, (2 * WINDOW, lanes), 1) % WINDOW
    dist = iq + WINDOW - ik
    valid = (dist >= 0) & (dist < WINDOW) & (ik >= (WINDOW if first else 0))
    slope = jnp.concatenate([jnp.zeros((1, WINDOW), F32) + 2.0 ** (-8.0 * (first_head + j + 1) / n_heads) for j in range(count)], axis=1)
    return jnp.where(valid, -slope * dist.astype(F32), NEG)


def _fill_attn_bias(bias_ref, group, n_heads):
    @pl.when(pl.program_id(0) == 0)
    def _():
        for g in range(ATT_KV_HEADS):
            bias_ref[0, g] = _attn_bias(g * group, group, n_heads, False)
            bias_ref[1, g] = _attn_bias(g * group, group, n_heads, True)


def _attn_probs_t(kb_scaled, qs, sink_ref, first_head, count, bias):
    sink = jnp.concatenate([jnp.zeros((1, WINDOW), F32) + sink_ref[0, first_head + j] for j in range(count)], axis=1)
    sc = _dot(kb_scaled, qs, NT) + bias
    m = jnp.maximum(jnp.max(sc, axis=0, keepdims=True), sink)
    e = jnp.exp(sc - m)
    es = jnp.exp(sink - m)
    inv = 1.0 / (jnp.sum(e, axis=0, keepdims=True) + es)
    return e * inv, es * inv


ATT_SCALE = ATT_HEAD_DIM ** -0.5


def _attn_specs(s, d, kvd, tq):
    per = tq // WINDOW
    return [pl.BlockSpec((tq, d), lambda i: (i, 0)), pl.BlockSpec((tq, kvd), lambda i: (i, 0)),
            pl.BlockSpec((WINDOW, kvd), lambda i: (jnp.maximum(i * per - 1, 0), 0))]


def _attn_fwd(q, kv, sinks):
    s, d = q.shape
    kvd = kv.shape[1]
    half = kvd // 2
    hd = ATT_HEAD_DIM
    nq = d // hd
    group = nq // ATT_KV_HEADS
    tq = min(s, ATT_TILE)
    per = tq // WINDOW

    def body(q_ref, kvc_ref, kvp_ref, sink_ref, o_ref, band, bias_ref):
        i = pl.program_id(0)
        _fill_attn_bias(bias_ref, group, nq)
        band[0:WINDOW, :] = kvp_ref[...]
        band[WINDOW:, :] = kvc_ref[...]

        def block(b, carry):
            rows = pl.ds(pl.multiple_of(b * WINDOW, WINDOW), WINDOW)
            keys = pl.ds(pl.multiple_of(b * WINDOW, WINDOW), 2 * WINDOW)
            first = (i * per + b) == 0
            for g in range(ATT_KV_HEADS):
                bias = jnp.where(first, bias_ref[1, g], bias_ref[0, g])
                p, _ = _attn_probs_t(band[keys, g * hd:(g + 1) * hd] * ATT_SCALE, _stack_heads(q_ref, rows, g * group, group), sink_ref,
                                     g * group, group, bias)
                out_t = _dot(band[keys, half + g * hd:half + (g + 1) * hd], p, TN)
                _unstack_heads(out_t.T, o_ref, rows, g * group, group)
            return carry

        lax.fori_loop(0, per, block, 0)

    return pl.pallas_call(
        body, name="attn_fwd", grid=(s // tq,),
        in_specs=_attn_specs(s, d, kvd, tq) + [pl.BlockSpec(memory_space=pltpu.SMEM)],
        out_specs=pl.BlockSpec((tq, d), lambda i: (i, 0)), out_shape=jax.ShapeDtypeStruct((s, d), BF16),
        scratch_shapes=[pltpu.VMEM((tq + WINDOW, kvd), BF16), pltpu.VMEM((2, ATT_KV_HEADS, 2 * WINDOW, group * WINDOW), F32)],
        compiler_params=_params(("arbitrary",)),
    )(q, kv, kv, sinks)


def _attn_bwd(q, kv, o, do, sinks):
    s, d = q.shape
    kvd = kv.shape[1]
    half = kvd // 2
    hd = ATT_HEAD_DIM
    nq = d // hd
    group = nq // ATT_KV_HEADS
    tq = min(s, ATT_TILE)
    per = tq // WINDOW
    nt = s // tq

    def body(q_ref, kvc_ref, kvp_ref, o_ref, do_ref, sink_ref, dq_ref, dkvc_ref, dkvp_ref, ds_ref, band, dband, bias_ref):
        i = pl.program_id(0)
        _fill_attn_bias(bias_ref, group, nq)
        band[0:WINDOW, :] = kvp_ref[...]
        band[WINDOW:, :] = kvc_ref[...]
        dband[...] = jnp.zeros_like(dband)
        ds_ref[...] = jnp.zeros_like(ds_ref)

        def block(b, carry):
            rows = pl.ds(pl.multiple_of(b * WINDOW, WINDOW), WINDOW)
            keys = pl.ds(pl.multiple_of(b * WINDOW, WINDOW), 2 * WINDOW)
            first = (i * per + b) == 0
            dks, dvs = [], []
            for g in range(ATT_KV_HEADS):
                kb = band[keys, g * hd:(g + 1) * hd] * ATT_SCALE
                vb = band[keys, half + g * hd:half + (g + 1) * hd]
                qs = _stack_heads(q_ref, rows, g * group, group)
                dos = _stack_heads(do_ref, rows, g * group, group)
                p, ps = _attn_probs_t(kb, qs, sink_ref, g * group, group, jnp.where(first, bias_ref[1, g], bias_ref[0, g]))
                prod = dos.astype(F32) * _stack_heads(o_ref, rows, g * group, group).astype(F32)
                dsum = lax.dot_general(jnp.ones((8, hd), F32), prod, NT, precision=lax.Precision.HIGHEST,
                                       preferred_element_type=F32)[0:1, :]
                dsc = p * (_dot(vb, dos, NT) - dsum)
                dvs.append(_dot(p, dos))
                dks.append(_dot(dsc, qs * ATT_SCALE))
                _unstack_heads(_dot(kb, dsc, TN).T, dq_ref, rows, g * group, group)
                gone = ps * dsum
                for j in range(group):
                    ds_ref[g * group + j:g * group + j + 1, :] += jnp.zeros((1, 128), F32) - jnp.sum(gone[:, j * WINDOW:(j + 1) * WINDOW])
            dband[keys, 0:half] += jnp.concatenate(dks, axis=1)
            dband[keys, half:] += jnp.concatenate(dvs, axis=1)
            return carry

        lax.fori_loop(0, per, block, 0)
        dkvp_ref[...] = dband[0:WINDOW, :]
        dkvc_ref[...] = dband[WINDOW:, :]

    big = pl.BlockSpec((tq, d), lambda i: (i, 0))
    return pl.pallas_call(
        body, name="attn_bwd", grid=(nt,),
        in_specs=_attn_specs(s, d, kvd, tq) + [big, big, pl.BlockSpec(memory_space=pltpu.SMEM)],
        out_specs=[big, pl.BlockSpec((tq, kvd), lambda i: (i, 0)), pl.BlockSpec((None, WINDOW, kvd), lambda i: (i, 0, 0)),
                   pl.BlockSpec((None, nq, 128), lambda i: (i, 0, 0))],
        out_shape=[jax.ShapeDtypeStruct((s, d), BF16), jax.ShapeDtypeStruct((s, kvd), F32), jax.ShapeDtypeStruct((nt, WINDOW, kvd), F32),
                   jax.ShapeDtypeStruct((nt, nq, 128), F32)],
        scratch_shapes=[pltpu.VMEM((tq + WINDOW, kvd), BF16), pltpu.VMEM((tq + WINDOW, kvd), F32),
                        pltpu.VMEM((2, ATT_KV_HEADS, 2 * WINDOW, group * WINDOW), F32)],
        compiler_params=_params(("arbitrary",)),
    )(q, kv, kv, o, do, sinks)


HBM_SPEC = pl.BlockSpec(memory_space=pltpu.HBM)
VMEM_SPEC = pl.BlockSpec(memory_space=pltpu.VMEM)


def _place():
    return lax.axis_index("x"), lax.axis_index("y"), lax.axis_index("c")


def _flip(pos, r):
    return tuple(1 - p if (r >> (2 - a)) & 1 else p for a, p in enumerate(pos))


def _index(pos):
    return 4 * pos[0] + 2 * pos[1] + pos[2]


def _all_gather(name, shards, spec):
    n = len(shards)

    def body(*refs):
        x_refs, o_refs = refs[:n], refs[n:2 * n]
        send_sems, recv_sems, local_sems = refs[2 * n:]
        me = _place()
        sibling = _flip(me, 1)
        far = [_flip(me, r) for r in (4, 2, 6)]

        def copy(t, sem, block, to, src=None):
            rows = o_refs[t].at[_index(block)]
            return pltpu.make_async_remote_copy(
                src_ref=rows if src is None else src, dst_ref=rows, send_sem=send_sems.at[t, sem], recv_sem=recv_sems.at[t, sem],
                device_id=to, device_id_type=MESH)

        own = [pltpu.make_async_copy(x_refs[t], o_refs[t].at[_index(me)], local_sems.at[t]) for t in range(n)]
        for cp in own:
            cp.start()
        first = []
        for t in range(n):
            first.append(copy(t, 0, me, sibling, src=x_refs[t]))
            first += [copy(t, 1 + j, me, peer, src=x_refs[t]) for j, peer in enumerate(far)]
        for cp in first:
            cp.start()
        passed = []
        for j, peer in enumerate(far):
            for t in range(n):
                copy(t, 1 + j, peer, me).wait_recv()
                cp = copy(t, 4 + j, peer, sibling)
                cp.start()
                passed.append(cp)
        for t in range(n):
            copy(t, 0, sibling, me).wait_recv()
            for j, peer in enumerate(far):
                copy(t, 4 + j, _flip(peer, 1), me).wait_recv()
        for cp in first + passed:
            cp.wait_send()
        for cp in own:
            cp.wait()

    return pl.pallas_call(
        body, name=name, in_specs=[spec] * n, out_specs=[spec] * n,
        out_shape=[jax.ShapeDtypeStruct((N_DEV,) + sh.shape, sh.dtype) for sh in shards],
        scratch_shapes=[pltpu.SemaphoreType.DMA((n, 7)), pltpu.SemaphoreType.DMA((n, 7)), pltpu.SemaphoreType.DMA((n,))],
    )(*shards)


SEM_SPEC = pl.BlockSpec(memory_space=pltpu.SEMAPHORE)
ANY_SPEC = pl.BlockSpec(memory_space=pl.ANY)


def _landing(own, mine):
    return lax.dynamic_update_slice(lax.empty((N_DEV,) + own.shape, own.dtype), own[None], (mine,) + (0,) * own.ndim)


def _pinned(a, token):
    return a + token[0:1, 0:1].astype(a.dtype)


def _peer_copies(src_refs, land_refs, send_sems, recv_sems, scatter, arrivals):
    me = _place()
    mine = _index(me)
    copies = []
    for t, (src, land) in enumerate(zip(src_refs, land_refs)):
        for r in range(1, N_DEV):
            peer = _flip(me, r)
            theirs = _index(peer)
            sem = t * (N_DEV - 1) + r - 1
            copies.append(pltpu.make_async_remote_copy(
                src_ref=src.at[theirs] if scatter else src, dst_ref=land.at[theirs if arrivals else mine],
                send_sem=send_sems.at[sem], recv_sem=recv_sems.at[sem], device_id=peer, device_id_type=MESH))
    return copies


def _send_start(name, sources, lands, scatter, after=None):
    n = len(sources)
    extra = 0 if after is None else 1

    def body(*refs):
        outs = refs[2 * n + extra:]
        for out in _peer_copies(refs[:n], refs[n:2 * n], outs[0], outs[1], scatter, False):
            out.start()
        outs[-1][...] = jnp.zeros_like(outs[-1])

    outs = pl.pallas_call(
        body, name=name, in_specs=[HBM_SPEC] * (2 * n) + [ANY_SPEC] * extra,
        out_specs=[SEM_SPEC, SEM_SPEC] + [HBM_SPEC] * (2 * n) + [VMEM_SPEC],
        out_shape=[pltpu.SemaphoreType.DMA((n * (N_DEV - 1),)), pltpu.SemaphoreType.DMA((n * (N_DEV - 1),))]
        + [pltpu.HBM(a.shape, a.dtype) for a in list(sources) + list(lands)] + [jax.ShapeDtypeStruct((8, 128), F32)],
        input_output_aliases={i: 2 + i for i in range(2 * n)},
        compiler_params=pltpu.CompilerParams(has_side_effects=pltpu.SideEffectType.DATAFLOW_SIDE_EFFECTING),
    )(*[pltpu.with_memory_space_constraint(a, pltpu.HBM) for a in list(sources) + list(lands)], *([] if after is None else [after]))
    return outs[0], outs[1], outs[2:2 + n], outs[2 + n:2 + 2 * n], outs[-1]


def _send_wait(name, started, after, scatter):
    send_sems, recv_sems, sources, lands, _ = started
    n = len(sources)

    def body(*refs):
        for out in _peer_copies(refs[:n], refs[n:2 * n], refs[2 * n], refs[2 * n + 1], scatter, False):
            out.wait_send()
        for arrival in _peer_copies(refs[:n], refs[n:2 * n], refs[2 * n], refs[2 * n + 1], scatter, True):
            arrival.wait_recv()

    outs = pl.pallas_call(
        body, name=name, in_specs=[HBM_SPEC] * (2 * n) + [SEM_SPEC, SEM_SPEC, ANY_SPEC], out_specs=[HBM_SPEC] * (2 * n),
        out_shape=[pltpu.HBM(a.shape, a.dtype) for a in list(sources) + list(lands)],
        input_output_aliases={i: i for i in range(2 * n)},
        compiler_params=pltpu.CompilerParams(has_side_effects=pltpu.SideEffectType.DATAFLOW_SIDE_EFFECTING),
    )(*sources, *lands, send_sems, recv_sems, after)
    return outs[n:]


def _pack_rows(parts):
    offsets, row = [], 0
    for part in parts:
        offsets.append(row)
        row += part.shape[0]
    return offsets, -(-row // 8) * 8, -(-max(part.shape[1] for part in parts) // 128) * 128


def _pack(name, parts):
    offsets, rows, width = _pack_rows(parts)

    def body(*refs):
        o_ref = refs[-1]
        o_ref[...] = jnp.zeros_like(o_ref)
        for off, ref in zip(offsets, refs[:-1]):
            o_ref[off:off + ref.shape[0], 0:ref.shape[1]] = ref[...]

    return pl.pallas_call(body, name=name, in_specs=[VMEM_SPEC] * len(parts), out_specs=VMEM_SPEC,
                          out_shape=jax.ShapeDtypeStruct((rows, width), F32))(*parts)


def _adamw_math(w, g, m, v):
    m = ADAM_B1 * m + (1.0 - ADAM_B1) * g
    v = ADAM_B2 * v + (1.0 - ADAM_B2) * (g * g)
    m_hat = m * (1.0 / (1.0 - ADAM_B1 ** ADAM_STEP))
    denom = jnp.sqrt(v * (1.0 / (1.0 - ADAM_B2 ** ADAM_STEP))) + ADAM_EPS
    inv = pl.reciprocal(denom, approx=True)
    inv = inv * (2.0 - denom * inv)
    return -ADAM_LR * (m_hat * inv + ADAM_WD * w), m, v


def _adamw_step(w_ref, m_ref, v_ref, p_ref, g_ref, d_ref, nm_ref, nv_ref):
    g = p_ref[0].astype(F32)
    for dev in range(1, N_DEV):
        g = g + p_ref[dev].astype(F32)
    g_ref[...] = g
    d_ref[...], nm_ref[...], nv_ref[...] = _adamw_math(w_ref[...], g, m_ref[...], v_ref[...])


def _adamw_rows(rows):
    return max(t for t in range(8, min(rows, 256) + 1, 8) if rows % t == 0)


def _adamw_shard(name, w, m, v, partials):
    rows, cols = w.shape
    tr = _adamw_rows(rows)
    blk = pl.BlockSpec((tr, cols), lambda i: (i, 0))
    return pl.pallas_call(
        _adamw_step_fn(), name=name, grid=(rows // tr,), in_specs=[blk, blk, blk, pl.BlockSpec((N_DEV, tr, cols), lambda i: (0, i, 0))],
        out_specs=[blk] * 4, out_shape=[jax.ShapeDtypeStruct((rows, cols), F32)] * 4, compiler_params=_params(("parallel",)),
    )(w, m, v, partials)


def _adamw_step_fn():
    return functools.partial(_adamw_step)


def _adamw_layers(name, w, m, v, partials):
    layers, rows, cols = w.shape
    tr = _adamw_rows(rows)
    last = rows // tr - 1

    def body(w_ref, m_ref, v_ref, *rest):
        for layer in range(layers):
            @pl.when(pl.program_id(0) == layer)
            def _():
                _adamw_step(w_ref, m_ref, v_ref, rest[layer], *rest[layers:])

    blk = pl.BlockSpec((None, tr, cols), lambda l, i: (l, i, 0))
    part = lambda layer: pl.BlockSpec((N_DEV, tr, cols), lambda l, i: (0, jnp.where(l == layer, i, jnp.where(l < layer, 0, last)), 0))
    return pl.pallas_call(
        body, name=name, grid=(layers, rows // tr), in_specs=[blk, blk, blk] + [part(layer) for layer in range(layers)],
        out_specs=[blk] * 4, out_shape=[jax.ShapeDtypeStruct(w.shape, F32)] * 4, compiler_params=_params(("arbitrary", "arbitrary")),
    )(w, m, v, *partials)


def _adamw_small(gathered, places, entries):
    n = len(entries)
    np_ = len(gathered)

    def body(*refs):
        pack_refs = refs[:np_]
        refs = refs[np_ - 1:]
        w_refs, m_refs, v_refs = refs[1:1 + n], refs[1 + n:1 + 2 * n], refs[1 + 2 * n:1 + 3 * n]
        outs = refs[1 + 3 * n:]
        totals = []
        for pack_ref in pack_refs:
            acc = pack_ref[0]
            for dev in range(1, N_DEV):
                acc = acc + pack_ref[dev]
            totals.append(acc)
        mine = _index(_place())
        for e in range(n):
            rows, cols = w_refs[e].shape
            total, off = totals[places[e][0]], places[e][1]
            if entries[e][3]:
                g = jnp.zeros((rows, cols), F32)
                for dev in range(N_DEV):
                    g = g + jnp.where(mine == dev, total[off + dev * rows:off + (dev + 1) * rows, 0:cols], 0.0)
            else:
                g = total[off:off + rows, 0:cols]
            outs[4 * e][...] = g
            outs[4 * e + 1][...], outs[4 * e + 2][...], outs[4 * e + 3][...] = _adamw_math(w_refs[e][...], g, m_refs[e][...], v_refs[e][...])
        outs[4 * n][...] = totals[places[n][0]][places[n][1]:places[n][1] + 1, 0:128]

    shapes = []
    for w, _, _, _ in entries:
        shapes += [jax.ShapeDtypeStruct(w.shape, F32)] * 4
    shapes.append(jax.ShapeDtypeStruct((1, 128), F32))
    return pl.pallas_call(
        body, name="adamw_small", in_specs=[VMEM_SPEC] * (np_ + 3 * n), out_specs=[VMEM_SPEC] * len(shapes), out_shape=shapes,
        compiler_params=pltpu.CompilerParams(vmem_limit_bytes=VMEM_LIMIT),
    )(*gathered, *[e[0] for e in entries], *[e[1] for e in entries], *[e[2] for e in entries])


def _ffn_forward(tag, h, gain, w_up, late):
    s, d = h.shape
    fb = w_up.shape[1]
    tm = _row_tile(s, 2 * MM_ROWS)
    a, = _rmsnorm_cast(f"ffn_norm_{tag}", h, [gain])
    u = _matmul(
        f"ffn_up_{tag}", a, w_up, dims=NT, grid=(s // tm, N_DEV, 1),
        a_spec=pl.BlockSpec((tm, d), lambda i, j, k: (i, 0)),
        b_spec=pl.BlockSpec((None, fb, d), lambda i, j, k: (j, 0, 0)),
        o_spec=pl.BlockSpec((None, None, tm, fb), lambda i, j, k: (j // 4, j % 4, i, 0)),
        out_shape=jax.ShapeDtypeStruct((2, 4, s, fb), BF16))
    w_down, conv_w, conv_b = late(u)
    hidden, conv, out = _ffn_hidden_down(f"ffn_hidden_down_{tag}", u, conv_w, conv_b, w_down, h)
    return out, (a, u, hidden, conv)


def _ffn_backward(tag, h, gain, w_up, w_down, conv_w, conv_b, saved, dout):
    a, u, hidden, conv = saved
    dout, dout_bf = dout
    s, d = h.shape
    fb = w_up.shape[1]
    tm = _row_tile(s, MM_ROWS)
    dhidden = _matmul(
        f"ffn_down_bwd_{tag}", dout_bf, w_down, dims=NT, grid=(s // tm, 4, 1),
        a_spec=pl.BlockSpec((tm, d), lambda i, j, k: (i, 0)),
        b_spec=pl.BlockSpec((None, fb, d), lambda i, j, k: (j, 0, 0)),
        o_spec=pl.BlockSpec((None, tm, fb), lambda i, j, k: (j, i, 0)),
        out_shape=jax.ShapeDtypeStruct((4, s, fb), BF16))
    dw_down = _matmul(
        f"ffn_down_grad_{tag}", hidden, dout_bf, dims=TN, grid=(4, 1, 1),
        a_spec=pl.BlockSpec((None, s, fb), lambda i, j, k: (i, 0, 0)),
        b_spec=pl.BlockSpec((s, d), lambda i, j, k: (0, 0)),
        o_spec=pl.BlockSpec((None, fb, d), lambda i, j, k: (i, 0, 0)),
        out_shape=jax.ShapeDtypeStruct((4, fb, d), BF16))
    du, dconv_w, dconv_b, dh, dh_bf, dgain = _ffn_hidden_up_bwd(f"ffn_hidden_up_bwd_{tag}", u, conv, dhidden, conv_w, w_up, h, gain, dout)
    dw_up = _matmul(
        f"ffn_up_grad_{tag}", du, a, dims=TN, grid=(N_DEV, 1, 1),
        a_spec=pl.BlockSpec((None, None, s, fb), lambda i, j, k: (i // 4, i % 4, 0, 0)),
        b_spec=pl.BlockSpec((s, d), lambda i, j, k: (0, 0)),
        o_spec=pl.BlockSpec((None, fb, d), lambda i, j, k: (i, 0, 0)),
        out_shape=jax.ShapeDtypeStruct((N_DEV, fb, d), BF16))
    return (dh, dh_bf), dgain, dw_up, dw_down, dconv_w, dconv_b


def kernel(x, hg_norm, hg_w_in, hg_lb_logits, hg_out_norm, hg_w_out, kv_norm, w_kv, attn_norm, attn_w_q, attn_sinks, attn_w_o, ffn_norm, ffn_w_up, ffn_conv_w, ffn_conv_b, ffn_w_down, final_norm, loss_target, m_hg_norm, m_hg_w_in, m_hg_lb_logits, m_hg_out_norm, m_hg_w_out, m_kv_norm, m_w_kv, m_attn_norm, m_attn_w_q, m_attn_sinks, m_attn_w_o, m_ffn_norm, m_ffn_w_up, m_ffn_conv_w, m_ffn_conv_b, m_ffn_w_down, m_final_norm, v_hg_norm, v_hg_w_in, v_hg_lb_logits, v_hg_out_norm, v_hg_w_out, v_kv_norm, v_w_kv, v_attn_norm, v_attn_w_q, v_attn_sinks, v_attn_w_o, v_ffn_norm, v_ffn_w_up, v_ffn_conv_w, v_ffn_conv_b, v_ffn_w_down, v_final_norm):
    _, s, d = x.shape
    x0, target = x[0], loss_target[0]
    half = hg_w_in.shape[2]
    fs = ffn_conv_w.shape[2]
    fb = 2 * fs
    kvd = w_kv.shape[1]
    nq = d // ATT_HEAD_DIM
    tm = _row_tile(s, MM_ROWS)

    mine = _index(_place())
    gather = lambda tag, shards, after: _send_start("gather_start_" + tag, shards, [_landing(a, mine) for a in shards], False, after)
    w_in, g_hgn, g_lbl, w_out = _all_gather("gather_hg", [hg_w_in[0].astype(BF16), hg_norm, hg_lb_logits, hg_w_out[0].astype(BF16)], HBM_SPEC)
    w_out = w_out.reshape(d, d)
    up_t = lambda a: jnp.swapaxes(a, -1, -2)
    coming_up0 = gather("ffn_up0", [up_t(ffn_w_up[0]).astype(BF16)], g_hgn)
    hgn = _pinned(g_hgn.reshape(1, d), coming_up0[4])
    lbl = g_lbl.transpose(1, 0, 2).reshape(2, d)
    conv_b = [ffn_conv_b[layer].reshape(4, 1, fb) for layer in range(2)]
    gains = [ffn_norm[0:1], ffn_norm[1:2]]
    kvn, fin = kv_norm.reshape(1, d), final_norm.reshape(1, d)

    a0, = _rmsnorm_cast("hg_norm", x0, [hgn])
    t2 = _row_tile(s, 2 * MM_ROWS)
    p = _matmul(
        "hg_in", a0, w_in, dims=NN, grid=(s // t2, N_DEV, 1),
        a_spec=pl.BlockSpec((t2, d), lambda i, j, k: (i, 0)),
        b_spec=pl.BlockSpec((None, d, half), lambda i, j, k: (j, 0, 0)),
        o_spec=pl.BlockSpec((None, t2, half), lambda i, j, k: (j // 2, i, j % 2)),
        out_shape=jax.ShapeDtypeStruct((4, s, d), BF16), acc_shape=(8, 128))
    o, og, states, gsum = _hgrn2_fwd(p, lbl, hg_out_norm)
    coming_dn0 = gather("ffn_down0", [ffn_conv_w, ffn_w_down[0].astype(BF16)], o)
    x1 = _mm_rows("hg_out", og, _pinned(w_out, coming_dn0[4]), out_dtype=F32, add=x0)
    w_up0, = _send_wait("gather_wait_ffn_up0", coming_up0, x1, False)
    coming_attn = gather("attn", [w_kv.astype(BF16), attn_w_q[0].astype(BF16), attn_w_o[0].astype(BF16)], w_up0)
    gains[0] = _pinned(gains[0], coming_attn[4])
    w_up, w_dn, conv_w, coming = [w_up0, None], [None, None], [], {}

    def late0(u):
        g_cw, w_dn0 = _send_wait("gather_wait_ffn_down0", coming_dn0, u, False)
        w_dn[0] = w_dn0.reshape(4, fb, d)
        conv_w.extend(g_cw[:, layer].reshape(4, 2, CONV_WIDTH, fs).transpose(0, 2, 1, 3).reshape(4, CONV_WIDTH, fb) for layer in range(2))
        coming["up1"] = gather("ffn_up1", [up_t(ffn_w_up[1]).astype(BF16)], w_dn0)
        return w_dn[0], conv_w[0], _pinned(conv_b[0], coming["up1"][4])

    x2, saved0 = _ffn_forward("0", x1, gains[0], w_up[0], late0)
    w_kvg, w_q, w_o = _send_wait("gather_wait_attn", coming_attn, x2, False)
    w_kvg, w_q, w_o = w_kvg.reshape(d, kvd), w_q.reshape(d, d), w_o.reshape(d, d)
    akv, a2 = _rmsnorm_cast("attn_norms", x2, [kvn, attn_norm])
    kv = _mm_rows("kv_proj", akv, w_kvg, out_dtype=BF16)
    q = _mm_rows("q_proj", a2, w_q, out_dtype=BF16)
    coming_dn1 = gather("ffn_down1", [ffn_w_down[1].astype(BF16)], q)
    att = _attn_fwd(q, kv, _pinned(attn_sinks, coming_dn1[4]))
    x3 = _mm_rows("attn_out", att, w_o, out_dtype=F32, add=x2)
    w_up[1], = _send_wait("gather_wait_ffn_up1", coming["up1"], x3, False)

    def late1(u):
        w_dn[1] = _send_wait("gather_wait_ffn_down1", coming_dn1, u, False)[0].reshape(4, fb, d)
        return w_dn[1], conv_w[1], conv_b[1]

    x4, saved1 = _ffn_forward("1", x3, gains[1], w_up[1], late1)
    dx4, dx4_bf, d_fin, loss_part = _loss_head(x4, fin, target)

    dx3, d_fn1, dw_up1, dw_dn1, dcw1, dcb1 = _ffn_backward("1", x3, gains[1], w_up[1], w_dn[1], conv_w[1], conv_b[1], saved1, (dx4, dx4_bf))
    rows = d // N_DEV
    scatter = lambda tag, stacks: _send_start("scatter_start_" + tag, stacks, [_landing(lax.dynamic_index_in_dim(a, mine, keepdims=False), mine) for a in stacks], True)
    going_ffn1 = scatter("ffn1", [dw_up1, dw_dn1.reshape(N_DEV, fs, d)])
    datt = _mm_rows_nt("attn_out_bwd", dx3[1], w_o, out_dtype=BF16)
    dw_o = _mm_tn("attn_out_grad", att, dx3[1])
    dq, dkv_own, dkv_before, dsink = _attn_bwd(q, kv, att, datt, _pinned(attn_sinks, going_ffn1[4]))
    tiles = dkv_before.shape[0]
    dkv = dkv_own.reshape(tiles, s // tiles, kvd)
    dkv = jnp.concatenate([dkv[:, :-WINDOW], dkv[:, -WINDOW:] + jnp.pad(dkv_before[1:], ((0, 1), (0, 0), (0, 0)))], axis=1).reshape(s, kvd)
    dw_q = _mm_tn("q_proj_grad", a2, dq)
    dw_kv = _mm_tn("kv_proj_grad", akv, dkv)
    going_attn = scatter("attn", [dw_kv.reshape(N_DEV, rows, kvd), dw_q.reshape(N_DEV, rows, d), dw_o.reshape(N_DEV, rows, d)])
    whole = lambda a_ref, b_ref: [(a_ref[...], b_ref[...])]
    rows_of = lambda width: (lambda tile: pl.BlockSpec((tile, width), lambda i: (i, 0)))
    dx2, (d_kvn, d_attn) = _proj_norm_bwd("attn_in_bwd", x2, dx3[0], [
        (dkv, rows_of(kvd), w_kvg, pl.BlockSpec((d, kvd), lambda i: (0, 0)), whole, _pinned(kvn, going_attn[4])),
        (dq, rows_of(d), w_q, pl.BlockSpec((d, d), lambda i: (0, 0)), whole, attn_norm)])
    dx1, d_fn0, dw_up0, dw_dn0, dcw0, dcb0 = _ffn_backward("0", x1, gains[0], w_up[0], w_dn[0], conv_w[0], conv_b[0], saved0, dx2)
    dw_out = _mm_tn("hg_out_grad", og, dx1[1])
    going_ffn0 = scatter("ffn0", [dw_up0, dw_dn0.reshape(N_DEV, fs, d), dw_out.reshape(N_DEV, rows, d)])
    dog = _mm_rows_nt("hg_out_bwd", dx1[1], w_out, out_dtype=F32)
    dp, d_lbl, d_ogain = _hgrn2_bwd(p, lbl, _pinned(hg_out_norm, going_ffn0[4]), o, dog, states, gsum)
    dw_in = _matmul(
        "hg_in_grad", a0, dp, dims=TN, grid=(1, N_DEV, 1),
        a_spec=pl.BlockSpec((s, d), lambda i, j, k: (0, 0)),
        b_spec=pl.BlockSpec((None, s, half), lambda i, j, k: (j // 2, 0, j % 2)),
        o_spec=pl.BlockSpec((None, d, half), lambda i, j, k: (j, 0, 0)),
        out_shape=jax.ShapeDtypeStruct((N_DEV, d, half), BF16))
    going_hg = scatter("hg", [dw_in])
    (dx0, _), (d_hgn,) = _proj_norm_bwd("hg_in_bwd", x0, dx1[0], [
        (dp, lambda tile: pl.BlockSpec((4, tile, d), lambda i: (0, i, 0)), w_in, pl.BlockSpec((N_DEV, d, half), lambda i: (0, 0, 0)),
         lambda g_ref, w_ref: [(g_ref[k // 2, :, (k % 2) * half:(k % 2 + 1) * half], w_ref[k]) for k in range(N_DEV)],
         _pinned(hgn, going_hg[4]))])

    as_blocks = lambda a, r: a.reshape(r, N_DEV, -1).transpose(1, 0, 2).reshape(N_DEV * r, -1)
    d_cw = jnp.concatenate([g.transpose(1, 0, 2).reshape(CONV_WIDTH, 4 * fb) for g in (dcw0, dcw1)], axis=0)
    parts = [d_fin, jnp.concatenate([d_fn0, d_fn1], axis=0), jnp.concatenate([dcb0.reshape(1, 4 * fb), dcb1.reshape(1, 4 * fb)], axis=0),
             as_blocks(d_cw, 2 * CONV_WIDTH), d_attn, jnp.sum(dsink[:, :, 0], axis=0).reshape(1, nq), d_kvn, d_ogain,
             as_blocks(d_hgn, 1), as_blocks(d_lbl, 2), loss_part]
    wide = [2]
    packs = [[parts[i] for i in wide], [part for i, part in enumerate(parts) if i not in wide]]
    places = [None] * len(parts)
    for which, members in enumerate([wide, [i for i in range(len(parts)) if i not in wide]]):
        for i, off in zip(members, _pack_rows(packs[which])[0]):
            places[i] = (which, off)
    packed = [_pack("pack_wide_grads", packs[0]), _pack("pack_narrow_grads", packs[1])]
    going_small = _send_start("small_grads_start", packed, [_landing(a, mine) for a in packed], False)

    arrive = lambda tag, going, after: _send_wait("scatter_wait_" + tag, going, after, True)
    (l_up1, l_dn1), (l_kv, l_q, l_o), (l_up0, l_dn0, l_out) = (
        arrive("ffn1", going_ffn1, going_small[4]), arrive("attn", going_attn, going_small[4]), arrive("ffn0", going_ffn0, going_small[4]))
    big = {}
    for tag, w, m, v, part in [
            ("w_kv", w_kv, m_w_kv, v_w_kv, l_kv), ("attn_w_q", attn_w_q[0], m_attn_w_q[0], v_attn_w_q[0], l_q),
            ("attn_w_o", attn_w_o[0], m_attn_w_o[0], v_attn_w_o[0], l_o)]:
        big[tag] = _adamw_shard("adamw_" + tag, w, m, v, part)
    big["ffn_w_up"] = [up_t(a) for a in _adamw_layers("adamw_ffn_w_up", up_t(ffn_w_up), up_t(m_ffn_w_up), up_t(v_ffn_w_up), (l_up0, l_up1))]
    big["ffn_w_down"] = _adamw_layers("adamw_ffn_w_down", ffn_w_down, m_ffn_w_down, v_ffn_w_down, (l_dn0, l_dn1))
    lead = lambda tag: [a[None] for a in big[tag]]

    both_done = big["ffn_w_up"][0][0, 0:1, 0:1] + big["ffn_w_down"][0][0, 0:1, 0:1]
    gathered = _send_wait("small_grads_wait", going_small, both_done, False)
    two = lambda a: a.reshape(-1, a.shape[-1])
    small = [(fin, m_final_norm.reshape(1, d), v_final_norm.reshape(1, d), False), (ffn_norm, m_ffn_norm, v_ffn_norm, False),
             (ffn_conv_b, m_ffn_conv_b, v_ffn_conv_b, False), (two(ffn_conv_w), two(m_ffn_conv_w), two(v_ffn_conv_w), True),
             (attn_norm, m_attn_norm, v_attn_norm, False), (attn_sinks, m_attn_sinks, v_attn_sinks, False),
             (kvn, m_kv_norm.reshape(1, d), v_kv_norm.reshape(1, d), False), (hg_out_norm, m_hg_out_norm, v_hg_out_norm, False),
             (hg_norm, m_hg_norm, v_hg_norm, True), (hg_lb_logits, m_hg_lb_logits, v_hg_lb_logits, True)]
    res = _adamw_small(gathered, places, small)
    l_in, = arrive("hg", going_hg, gathered[1])
    big["hg_w_in"] = _adamw_shard("adamw_hg_w_in", hg_w_in[0], m_hg_w_in[0], v_hg_w_in[0], l_in)
    big["hg_w_out"] = _adamw_shard("adamw_hg_w_out", hg_w_out[0], m_hg_w_out[0], v_hg_w_out[0], l_out)
    names = ["final_norm", "ffn_norm", "ffn_conv_b", "ffn_conv_w", "attn_norm", "attn_sinks", "kv_norm", "hg_out_norm", "hg_norm", "hg_lb_logits"]
    shapes = {"final_norm": final_norm.shape, "kv_norm": kv_norm.shape, "ffn_conv_w": ffn_conv_w.shape}
    out = {n: [a.reshape(shapes[n]) if n in shapes else a for a in res[4 * i:4 * i + 4]] for i, n in enumerate(names)}
    out.update(hg_w_in=lead("hg_w_in"), hg_w_out=lead("hg_w_out"), w_kv=big["w_kv"], attn_w_q=lead("attn_w_q"), attn_w_o=lead("attn_w_o"),
               ffn_w_up=big["ffn_w_up"], ffn_w_down=big["ffn_w_down"])
    order = ["hg_norm", "hg_w_in", "hg_lb_logits", "hg_out_norm", "hg_w_out", "kv_norm", "w_kv", "attn_norm", "attn_w_q", "attn_sinks",
             "attn_w_o", "ffn_norm", "ffn_w_up", "ffn_conv_w", "ffn_conv_b", "ffn_w_down", "final_norm"]
    loss = res[-1][0, 0]
    return (loss, dx0[None], *[out[n][0] for n in order], *[out[n][1] for n in order], *[out[n][2] for n in order], *[out[n][3] for n in order])
```

```python
import functools

import jax
import jax.numpy as jnp
from jax import lax
from jax.experimental import pallas as pl
from jax.experimental.pallas import tpu as pltpu

F32 = jnp.float32
BF16 = jnp.bfloat16

EPS = 1e-6
HG_EXPAND = 128
HG_CHUNK = 32
ATT_HEAD_DIM = 64
ATT_KV_HEADS = 2
WINDOW = 128
CONV_WIDTH = 3
ADAM_LR = 0.001
ADAM_B1 = 0.9
ADAM_B2 = 0.999
ADAM_EPS = 1e-08
ADAM_WD = 0.01
ADAM_STEP = 10

N_DEV = 8
VMEM_LIMIT = 48 * 1024 * 1024
NEG = -1e30

NN = (((1,), (0,)), ((), ()))
NT = (((1,), (1,)), ((), ()))
TN = (((0,), (0,)), ((), ()))
MESH = pl.DeviceIdType.MESH


def _dot(a, b, dims=NN):
    return lax.dot_general(a.astype(BF16), b.astype(BF16), dims, preferred_element_type=F32)


def _sigmoid(x):
    return 0.5 * jnp.tanh(0.5 * x) + 0.5


def _silu(x):
    return x * _sigmoid(x)


def _silu_and_grad(x):
    s = _sigmoid(x)
    return x * s, s * (1.0 + x * (1.0 - s))


def _dsilu(x):
    return _silu_and_grad(x)[1]


def _params(semantics):
    return pltpu.CompilerParams(dimension_semantics=semantics, vmem_limit_bytes=VMEM_LIMIT)


def _row_tile(rows, want=512):
    return min(rows, want)


MM_ROWS = 1024


def _matmul(name, a, b, *, dims, grid, a_spec, b_spec, o_spec, out_shape, add=None, add_spec=None):
    assert grid[2] == 1

    def body(*refs):
        a_ref, b_ref, o_ref = refs[0], refs[1], refs[-1]
        total = _dot(a_ref[...], b_ref[...], dims)
        if add is not None:
            total = total + refs[2][...]
        o_ref[...] = total.astype(o_ref.dtype)

    in_specs = [a_spec, b_spec] + ([] if add is None else [add_spec])
    args = (a, b) + (() if add is None else (add,))
    return pl.pallas_call(
        body, name=name, grid=grid, in_specs=in_specs, out_specs=o_spec, out_shape=out_shape,
        compiler_params=_params(("parallel", "parallel", "arbitrary")),
    )(*args)


def _mm_rows(name, a, w, *, out_dtype, add=None):
    s, kdim = a.shape
    n = w.shape[1]
    tm = _row_tile(s, MM_ROWS)
    return _matmul(
        name, a, w, dims=NN, grid=(s // tm, 1, 1),
        a_spec=pl.BlockSpec((tm, kdim), lambda i, j, k: (i, 0)),
        b_spec=pl.BlockSpec((kdim, n), lambda i, j, k: (0, 0)),
        o_spec=pl.BlockSpec((tm, n), lambda i, j, k: (i, 0)),
        out_shape=jax.ShapeDtypeStruct((s, n), out_dtype),
        add=add, add_spec=None if add is None else pl.BlockSpec((tm, n), lambda i, j, k: (i, 0)),
    )


def _mm_rows_nt(name, a, w, *, out_dtype):
    s, n = a.shape
    kdim = w.shape[0]
    tm = _row_tile(s, MM_ROWS)
    return _matmul(
        name, a, w, dims=NT, grid=(s // tm, 1, 1),
        a_spec=pl.BlockSpec((tm, n), lambda i, j, k: (i, 0)),
        b_spec=pl.BlockSpec((kdim, n), lambda i, j, k: (0, 0)),
        o_spec=pl.BlockSpec((tm, kdim), lambda i, j, k: (i, 0)),
        out_shape=jax.ShapeDtypeStruct((s, kdim), out_dtype),
    )


def _mm_tn(name, a, g):
    s, m = a.shape
    n = g.shape[1]
    tn = min(n, 512)
    return _matmul(
        name, a, g, dims=TN, grid=(1, n // tn, 1),
        a_spec=pl.BlockSpec((s, m), lambda i, j, k: (0, 0)),
        b_spec=pl.BlockSpec((s, tn), lambda i, j, k: (0, j)),
        o_spec=pl.BlockSpec((m, tn), lambda i, j, k: (0, j)),
        out_shape=jax.ShapeDtypeStruct((m, n), BF16),
    )


def _rmsnorm_cast(name, h, gains):
    s, d = h.shape
    tm = _row_tile(s)
    n = len(gains)

    def body(*refs):
        h_ref, g_refs, o_refs = refs[0], refs[1:1 + n], refs[1 + n:]
        xv = h_ref[...]
        xhat = xv * lax.rsqrt(jnp.mean(xv * xv, axis=-1, keepdims=True) + EPS)
        for g_ref, o_ref in zip(g_refs, o_refs):
            o_ref[...] = (xhat * g_ref[...]).astype(BF16)

    row = pl.BlockSpec((tm, d), lambda i: (i, 0))
    vec = pl.BlockSpec((1, d), lambda i: (0, 0))
    return pl.pallas_call(
        body, name=name, grid=(s // tm,), in_specs=[row] + [vec] * n, out_specs=[row] * n,
        out_shape=[jax.ShapeDtypeStruct((s, d), BF16)] * n, compiler_params=_params(("parallel",)),
    )(h, *gains)


def _proj_norm_bwd(name, h, dres, branches):
    s, d = h.shape
    tm = _row_tile(s)
    n = len(branches)

    def body(*refs):
        h_ref, dres_ref = refs[0], refs[1]
        g_refs, w_refs, gain_refs = refs[2:2 + n], refs[2 + n:2 + 2 * n], refs[2 + 2 * n:2 + 3 * n]
        dh_ref, dhb_ref, dg_refs = refs[2 + 3 * n], refs[3 + 3 * n], refs[4 + 3 * n:]
        i = pl.program_id(0)
        xv = h_ref[...]
        r = lax.rsqrt(jnp.mean(xv * xv, axis=-1, keepdims=True) + EPS)
        xhat = xv * r
        total = dres_ref[...]
        for branch, g_ref, w_ref, gain_ref, dg_ref in zip(branches, g_refs, w_refs, gain_refs, dg_refs):
            pairs = branch[4](g_ref, w_ref)
            da = _dot(*pairs[0], NT)
            for pair in pairs[1:]:
                da = da + _dot(*pair, NT)
            dgain = jnp.sum(da * xhat, axis=0, keepdims=True)

            @pl.when(i == 0)
            def _():
                dg_ref[...] = dgain

            @pl.when(i > 0)
            def _():
                dg_ref[...] += dgain

            dxhat = da * gain_ref[...]
            total = total + r * (dxhat - xhat * jnp.mean(dxhat * xhat, axis=-1, keepdims=True))
        dh_ref[...] = total
        dhb_ref[...] = total.astype(BF16)

    row = pl.BlockSpec((tm, d), lambda i: (i, 0))
    vec = pl.BlockSpec((1, d), lambda i: (0, 0))
    outs = pl.pallas_call(
        body, name=name, grid=(s // tm,),
        in_specs=[row, row] + [b[1](tm) for b in branches] + [b[3] for b in branches] + [vec] * n, out_specs=[row, row] + [vec] * n,
        out_shape=[jax.ShapeDtypeStruct((s, d), F32), jax.ShapeDtypeStruct((s, d), BF16)] + [jax.ShapeDtypeStruct((1, d), F32)] * n,
        compiler_params=_params(("arbitrary",)),
    )(h, dres, *[b[0] for b in branches], *[b[2] for b in branches], *[b[5] for b in branches])
    return (outs[0], outs[1]), outs[2:]


def _loss_head(h, gain, target):
    s, d = h.shape
    tm = _row_tile(s)

    def body(h_ref, g_ref, t_ref, dh_ref, dhb_ref, dg_ref, loss_ref):
        i = pl.program_id(0)
        xv = h_ref[...]
        r = lax.rsqrt(jnp.mean(xv * xv, axis=-1, keepdims=True) + EPS)
        xhat = xv * r
        err = xhat * g_ref[...] - t_ref[...]
        dy = err * (1.0 / d)
        part = jnp.zeros((1, 128), F32) + 0.5 * jnp.sum(jnp.mean(err * err, axis=-1, keepdims=True))
        dgain = jnp.sum(dy * xhat, axis=0, keepdims=True)

        @pl.when(i == 0)
        def _():
            dg_ref[...] = dgain
            loss_ref[...] = part

        @pl.when(i > 0)
        def _():
            dg_ref[...] += dgain
            loss_ref[...] += part

        dxhat = dy * g_ref[...]
        dh = r * (dxhat - xhat * jnp.mean(dxhat * xhat, axis=-1, keepdims=True))
        dh_ref[...] = dh
        dhb_ref[...] = dh.astype(BF16)

    row = pl.BlockSpec((tm, d), lambda i: (i, 0))
    vec = pl.BlockSpec((1, d), lambda i: (0, 0))
    return pl.pallas_call(
        body, name="loss_head", grid=(s // tm,), in_specs=[row, vec, row],
        out_specs=[row, row, vec, pl.BlockSpec((1, 128), lambda i: (0, 0))],
        out_shape=[jax.ShapeDtypeStruct((s, d), F32), jax.ShapeDtypeStruct((s, d), BF16), jax.ShapeDtypeStruct((1, d), F32),
                   jax.ShapeDtypeStruct((1, 128), F32)],
        compiler_params=_params(("arbitrary",)),
    )(h, gain, target)


def _bdot(a, b, ca, cb):
    return lax.dot_general(a.astype(BF16), b.astype(BF16), (((ca,), (cb,)), ((0,), (0,))), preferred_element_type=F32)


def _chunk_cumsum(xv, reverse=False):
    n = xv.shape[0]
    row = lax.broadcasted_iota(jnp.int32, xv.shape, 0) % HG_CHUNK
    step = 1
    while step < HG_CHUNK:
        if reverse:
            xv = xv + jnp.where(row < HG_CHUNK - step, pltpu.roll(xv, n - step, axis=0), 0.0)
        else:
            xv = xv + jnp.where(row >= step, pltpu.roll(xv, step, axis=0), 0.0)
        step *= 2
    return xv


def _hg_terms(p_ref, lbl_ref, g_ref=None):
    pq = p_ref[0].astype(F32)
    pf = p_ref[1].astype(F32)
    lb = _sigmoid(lbl_ref[0:1, :] - lbl_ref[1:2, :])
    sig = _sigmoid(pf)
    fg = lb + (1.0 - lb) * sig
    nc = pq.shape[0] // HG_CHUNK
    chunks = lambda a: a.reshape(nc, HG_CHUNK, HG_EXPAND)
    q = chunks(_silu(pq) * HG_EXPAND ** -0.5)
    k = chunks(1.0 - fg)
    v = chunks(p_ref[2].astype(F32))
    g = chunks(_chunk_cumsum(jnp.log(fg)) if g_ref is None else g_ref[...])
    gm = g[:, HG_CHUNK // 2 - 1:HG_CHUNK // 2, :]
    gl = g[:, HG_CHUNK - 1:HG_CHUNK, :]
    e_mid, e_inv, e_all, e_end = jnp.exp(g - gm), jnp.exp(gm - g), jnp.exp(g), jnp.exp(gl - g)
    terms = dict(q=q, k=k, v=v, g=g, qd=q * e_all, qt=q * e_mid, kt=k * e_inv, kd=k * e_end, e_last=jnp.exp(gl),
                 e_mid=e_mid, e_inv=e_inv, e_all=e_all, e_end=e_end)
    return terms, (pq, sig, fg, lb)


def _causal(nc):
    r = lax.broadcasted_iota(jnp.int32, (nc, HG_CHUNK, HG_CHUNK), 1)
    c = lax.broadcasted_iota(jnp.int32, (nc, HG_CHUNK, HG_CHUNK), 2)
    return r >= c


def _hgrn2_fwd(p, lb_logits, out_gain):
    _, s, d = p.shape
    heads = d // HG_EXPAND
    t = _row_tile(s, 2048)
    nc = t // HG_CHUNK

    def body(p_ref, lbl_ref, gain_ref, o_ref, og_ref, st_ref, g_ref, state, decay):
        @pl.when(pl.program_id(1) == 0)
        def _():
            state[...] = jnp.zeros_like(state)

        tm, _ = _hg_terms(p_ref, lbl_ref)
        g_ref[...] = tm["g"].reshape(t, HG_EXPAND)
        decay[...] = tm["e_last"]
        st_ref[...] = _bdot(tm["v"], tm["kd"], 1, 1)

        def chunk(c, carry):
            add = st_ref[c]
            st = state[...]
            st_ref[c] = st
            state[...] = st * decay[c] + add
            return carry

        lax.fori_loop(0, nc, chunk, 0)
        a = jnp.where(_causal(nc), _bdot(tm["qt"], tm["kt"], 2, 2), 0.0)
        ov = (_bdot(tm["qd"], st_ref[...], 2, 2) + _bdot(a, tm["v"], 2, 1)).reshape(t, HG_EXPAND)
        o_ref[...] = ov
        on = ov * lax.rsqrt(jnp.mean(ov * ov, axis=-1, keepdims=True) + EPS) * gain_ref[...]
        og_ref[...] = (on * _silu(p_ref[3].astype(F32))).astype(BF16)

    blk = pl.BlockSpec((t, HG_EXPAND), lambda h, b: (b, h))
    return pl.pallas_call(
        body, name="hgrn2_fwd", grid=(heads, s // t),
        in_specs=[pl.BlockSpec((4, t, HG_EXPAND), lambda h, b: (0, b, h)), pl.BlockSpec((2, HG_EXPAND), lambda h, b: (0, h)),
                  pl.BlockSpec((1, HG_EXPAND), lambda h, b: (0, 0))],
        out_specs=[blk, blk, pl.BlockSpec((None, nc, HG_EXPAND, HG_EXPAND), lambda h, b: (h, b, 0, 0)), blk],
        out_shape=[jax.ShapeDtypeStruct((s, d), F32), jax.ShapeDtypeStruct((s, d), BF16),
                   jax.ShapeDtypeStruct((heads, s // HG_CHUNK, HG_EXPAND, HG_EXPAND), F32), jax.ShapeDtypeStruct((s, d), F32)],
        scratch_shapes=[pltpu.VMEM((HG_EXPAND, HG_EXPAND), F32), pltpu.VMEM((nc, 1, HG_EXPAND), F32)],
        compiler_params=_params(("parallel", "arbitrary")),
    )(p, lb_logits, out_gain)


def _hgrn2_bwd(p, lb_logits, out_gain, o, dog, states, gsum):
    _, s, d = p.shape
    heads = d // HG_EXPAND
    t = _row_tile(s, 1024)
    nc = t // HG_CHUNK
    nb = s // t

    def body(p_ref, lbl_ref, gain_ref, o_ref, dog_ref, st_ref, g_ref, dp_ref, dlbl_ref, dgain_ref, dstate, decay, dst_s):
        h, b = pl.program_id(0), pl.program_id(1)

        @pl.when(b == 0)
        def _():
            dstate[...] = jnp.zeros_like(dstate)

        tm, (pq, sig, fg, lb) = _hg_terms(p_ref, lbl_ref, g_ref)
        pg = p_ref[3].astype(F32)
        ov = o_ref[...]
        r = lax.rsqrt(jnp.mean(ov * ov, axis=-1, keepdims=True) + EPS)
        ohat = ov * r
        dogv = dog_ref[...]
        d_on = dogv * _silu(pg)
        dp_ref[3] = (dogv * ohat * gain_ref[...] * _dsilu(pg)).astype(BF16)
        dgain = jnp.sum(d_on * ohat, axis=0, keepdims=True)

        @pl.when((h == 0) & (b == 0))
        def _():
            dgain_ref[...] = dgain

        @pl.when((h > 0) | (b > 0))
        def _():
            dgain_ref[...] += dgain

        dohat = d_on * gain_ref[...]
        do = (r * (dohat - ohat * jnp.mean(dohat * ohat, axis=-1, keepdims=True))).reshape(nc, HG_CHUNK, HG_EXPAND)

        decay[...] = tm["e_last"]
        dst_s[...] = _bdot(do, tm["qd"], 1, 1)

        def chunk(i, carry):
            c = nc - 1 - i
            add = dst_s[c]
            dst = dstate[...]
            dst_s[c] = dst
            dstate[...] = dst * decay[c] + add
            return carry

        lax.fori_loop(0, nc, chunk, 0)
        st, dst = st_ref[...], dst_s[...]
        causal = _causal(nc)
        a = jnp.where(causal, _bdot(tm["qt"], tm["kt"], 2, 2), 0.0)
        da = jnp.where(causal, _bdot(do, tm["v"], 2, 2), 0.0)
        dqt = _bdot(da, tm["kt"], 2, 1)
        dkt = _bdot(da, tm["qt"], 1, 1)
        dqd = _bdot(do, st, 2, 1)
        dkd = _bdot(tm["v"], dst, 2, 1)
        dv = _bdot(a, do, 1, 1) + _bdot(tm["kd"], dst, 2, 2)
        dq = dqt * tm["e_mid"] + dqd * tm["e_all"]
        dk = dkt * tm["e_inv"] + dkd * tm["e_end"]
        dg = dqt * tm["qt"] - dkt * tm["kt"] + dqd * tm["qd"] - dkd * tm["kd"]
        dgl = jnp.sum(dkd * tm["kd"], axis=1, keepdims=True) + tm["e_last"] * jnp.sum(dst * st, axis=1, keepdims=True)
        last_row = lax.broadcasted_iota(jnp.int32, (nc, HG_CHUNK, HG_EXPAND), 1) == HG_CHUNK - 1
        flat = lambda a3: a3.reshape(t, HG_EXPAND)
        dlf = _chunk_cumsum(flat(dg + jnp.where(last_row, dgl, 0.0)), reverse=True)
        dfg = dlf / fg - flat(dk)
        dlb = jnp.sum(dfg * (1.0 - sig), axis=0, keepdims=True)
        dl0 = dlb * lb * (1.0 - lb)
        dlbl = jnp.concatenate([dl0, -dl0], axis=0)

        @pl.when(b == 0)
        def _():
            dlbl_ref[...] = dlbl

        @pl.when(b > 0)
        def _():
            dlbl_ref[...] += dlbl

        dp_ref[0] = (flat(dq) * HG_EXPAND ** -0.5 * _dsilu(pq)).astype(BF16)
        dp_ref[1] = (dfg * (1.0 - lb) * sig * (1.0 - sig)).astype(BF16)
        dp_ref[2] = flat(dv).astype(BF16)

    blk = pl.BlockSpec((t, HG_EXPAND), lambda h, b: (nb - 1 - b, h))
    pblk = pl.BlockSpec((4, t, HG_EXPAND), lambda h, b: (0, nb - 1 - b, h))
    return pl.pallas_call(
        body, name="hgrn2_bwd", grid=(heads, nb),
        in_specs=[pblk, pl.BlockSpec((2, HG_EXPAND), lambda h, b: (0, h)), pl.BlockSpec((1, HG_EXPAND), lambda h, b: (0, 0)),
                  blk, blk, pl.BlockSpec((None, nc, HG_EXPAND, HG_EXPAND), lambda h, b: (h, nb - 1 - b, 0, 0)), blk],
        out_specs=[pblk, pl.BlockSpec((2, HG_EXPAND), lambda h, b: (0, h)), pl.BlockSpec((1, HG_EXPAND), lambda h, b: (0, 0))],
        out_shape=[jax.ShapeDtypeStruct((4, s, d), BF16), jax.ShapeDtypeStruct((2, d), F32), jax.ShapeDtypeStruct((1, HG_EXPAND), F32)],
        scratch_shapes=[pltpu.VMEM((HG_EXPAND, HG_EXPAND), F32), pltpu.VMEM((nc, 1, HG_EXPAND), F32),
                        pltpu.VMEM((nc, HG_EXPAND, HG_EXPAND), F32)],
        compiler_params=_params(("arbitrary", "arbitrary")),
    )(p, lb_logits, out_gain, o, dog, states, gsum)


HALO = 8
FFN_FWD_ROWS = 512
FFN_BWD_ROWS = 256


def _shift_down(xv, n):
    return pltpu.roll(xv, n, axis=0)


def _shift_up(xv, n):
    return pltpu.roll(xv, xv.shape[0] - n, axis=0)


def _ffn_hidden_down(name, u, conv_w, conv_b, w_down, h):
    _, nj, s, fb = u.shape
    d = w_down.shape[2]
    tm = _row_tile(s, FFN_FWD_ROWS)
    per = tm // HALO

    def body(gate_ref, prev_ref, val_ref, w_ref, b_ref, wd_ref, h_ref, hid_ref, conv_ref, o_ref):
        i = pl.program_id(0)
        total = h_ref[...]
        for j in range(nj):
            prev = jnp.where(i > 0, prev_ref[j].astype(F32), 0.0)
            ext = jnp.concatenate([prev, gate_ref[j].astype(F32)], axis=0)
            conv = b_ref[j] + w_ref[j, 2:3, :] * ext[HALO:]
            conv = conv + w_ref[j, 1:2, :] * _shift_down(ext, 1)[HALO:]
            conv = conv + w_ref[j, 0:1, :] * _shift_down(ext, 2)[HALO:]
            conv_ref[j] = conv.astype(BF16)
            hidden = (_silu(conv) * val_ref[j].astype(F32)).astype(BF16)
            hid_ref[j] = hidden
            total = total + _dot(hidden, wd_ref[j])
        o_ref[...] = total

    row = pl.BlockSpec((tm, d), lambda i: (i, 0))
    return pl.pallas_call(
        body, name=name, grid=(s // tm,),
        in_specs=[pl.BlockSpec((None, nj, tm, fb), lambda i: (0, 0, i, 0)),
                  pl.BlockSpec((None, nj, HALO, fb), lambda i: (0, 0, jnp.maximum(i * per - 1, 0), 0)),
                  pl.BlockSpec((None, nj, tm, fb), lambda i: (1, 0, i, 0)),
                  pl.BlockSpec((nj, CONV_WIDTH, fb), lambda i: (0, 0, 0)), pl.BlockSpec((nj, 1, fb), lambda i: (0, 0, 0)),
                  pl.BlockSpec((nj, fb, d), lambda i: (0, 0, 0)), row],
        out_specs=[pl.BlockSpec((nj, tm, fb), lambda i: (0, i, 0)), pl.BlockSpec((nj, tm, fb), lambda i: (0, i, 0)), row],
        out_shape=[jax.ShapeDtypeStruct((nj, s, fb), BF16), jax.ShapeDtypeStruct((nj, s, fb), BF16), jax.ShapeDtypeStruct((s, d), F32)],
        compiler_params=_params(("parallel",)),
    )(u, u, u, conv_w, conv_b, w_down, h)


def _ffn_hidden_up_bwd(name, u, conv, dh, conv_w, w_up, h, gain, dres):
    _, nj, s, fb = u.shape
    d = w_up.shape[2]
    tm = _row_tile(s, FFN_BWD_ROWS)
    per = tm // HALO
    nblk = s // HALO
    ni = s // tm

    def body(gate_ref, conv_ref, cnext_ref, val_ref, vnext_ref, dh_ref, dhnext_ref, w_ref, wu_ref, h_ref, gain_ref, dres_ref,
             du_ref, dw_ref, db_ref, dx_ref, dxb_ref, dgain_ref):
        i = pl.program_id(0)
        has_next = i < ni - 1
        total = None
        for j in range(nj):
            cext = jnp.concatenate([conv_ref[j].astype(F32), cnext_ref[j].astype(F32)], axis=0)
            vext = jnp.concatenate([val_ref[j].astype(F32), vnext_ref[j].astype(F32)], axis=0)
            dhext = jnp.concatenate([dh_ref[j].astype(F32), jnp.where(has_next, dhnext_ref[j].astype(F32), 0.0)], axis=0)
            act, dact = _silu_and_grad(cext)
            dconv = dhext * vext * dact
            taps = [_shift_up(dconv, 2)[:tm], _shift_up(dconv, 1)[:tm], dconv[:tm]]
            dgate = (w_ref[j, 0:1, :] * taps[0] + w_ref[j, 1:2, :] * taps[1] + w_ref[j, 2:3, :] * taps[2]).astype(BF16)
            dval = (dhext * act)[:tm].astype(BF16)
            du_ref[0, j] = dgate
            du_ref[1, j] = dval
            part = _dot(dgate, wu_ref[j]) + _dot(dval, wu_ref[nj + j])
            total = part if total is None else total + part
            gate = gate_ref[j].astype(F32)
            dw = jnp.concatenate([jnp.sum(tap * gate, axis=0, keepdims=True) for tap in taps], axis=0)
            db = jnp.sum(taps[2], axis=0, keepdims=True)

            @pl.when(i == 0)
            def _():
                dw_ref[j] = dw
                db_ref[j] = db

            @pl.when(i > 0)
            def _():
                dw_ref[j] += dw
                db_ref[j] += db

        xv = h_ref[...]
        r = lax.rsqrt(jnp.mean(xv * xv, axis=-1, keepdims=True) + EPS)
        xhat = xv * r
        dgain = jnp.sum(total * xhat, axis=0, keepdims=True)

        @pl.when(i == 0)
        def _():
            dgain_ref[...] = dgain

        @pl.when(i > 0)
        def _():
            dgain_ref[...] += dgain

        dxhat = total * gain_ref[...]
        dx = dres_ref[...] + r * (dxhat - xhat * jnp.mean(dxhat * xhat, axis=-1, keepdims=True))
        dx_ref[...] = dx
        dxb_ref[...] = dx.astype(BF16)

    def tile(part):
        return pl.BlockSpec((None, nj, tm, fb), lambda i: (part, 0, i, 0))

    def after(part):
        return pl.BlockSpec((None, nj, HALO, fb), lambda i: (part, 0, jnp.minimum((i + 1) * per, nblk - 1), 0))

    row = pl.BlockSpec((tm, d), lambda i: (i, 0))
    own = pl.BlockSpec((nj, tm, fb), lambda i: (0, i, 0))
    nxt = pl.BlockSpec((nj, HALO, fb), lambda i: (0, jnp.minimum((i + 1) * per, nblk - 1), 0))
    return pl.pallas_call(
        body, name=name, grid=(ni,),
        in_specs=[tile(0), own, nxt, tile(1), after(1), own, nxt,
                  pl.BlockSpec((nj, CONV_WIDTH, fb), lambda i: (0, 0, 0)),
                  pl.BlockSpec((2 * nj, fb, d), lambda i: (0, 0, 0)), row, pl.BlockSpec((1, d), lambda i: (0, 0)), row],
        out_specs=[pl.BlockSpec((2, nj, tm, fb), lambda i: (0, 0, i, 0)),
                   pl.BlockSpec((nj, CONV_WIDTH, fb), lambda i: (0, 0, 0)), pl.BlockSpec((nj, 1, fb), lambda i: (0, 0, 0)),
                   row, row, pl.BlockSpec((1, d), lambda i: (0, 0))],
        out_shape=[jax.ShapeDtypeStruct((2, nj, s, fb), BF16), jax.ShapeDtypeStruct((nj, CONV_WIDTH, fb), F32),
                   jax.ShapeDtypeStruct((nj, 1, fb), F32), jax.ShapeDtypeStruct((s, d), F32), jax.ShapeDtypeStruct((s, d), BF16),
                   jax.ShapeDtypeStruct((1, d), F32)],
        compiler_params=_params(("arbitrary",)),
    )(u, conv, conv, u, u, dh, dh, conv_w, w_up, h, gain, dres)


ATT_TILE = 512


def _stack_heads(ref, rows, first_head, count):
    hd = ATT_HEAD_DIM
    return jnp.concatenate([ref[rows, (first_head + j) * hd:(first_head + j + 1) * hd] for j in range(count)], axis=0)


def _unstack_heads(stacked, ref, rows, first_head, count):
    hd = ATT_HEAD_DIM
    for pair in range(count // 2):
        both = [stacked[(2 * pair + j) * WINDOW:(2 * pair + j + 1) * WINDOW, :] for j in range(2)]
        ref[rows, (first_head + 2 * pair) * hd:(first_head + 2 * pair + 2) * hd] = jnp.concatenate(both, axis=1).astype(ref.dtype)


def _attn_bias(first_head, count, n_heads, first):
    lanes = count * WINDOW
    ik = lax.broadcasted_iota(jnp.int32, (2 * WINDOW, lanes), 0)
    iq = lax.broadcasted_iota(jnp.int32, (2 * WINDOW, lanes), 1) % WINDOW
    dist = iq + WINDOW - ik
    valid = (dist >= 0) & (dist < WINDOW) & (ik >= (WINDOW if first else 0))
    slope = jnp.concatenate([jnp.zeros((1, WINDOW), F32) + 2.0 ** (-8.0 * (first_head + j + 1) / n_heads) for j in range(count)], axis=1)
    return jnp.where(valid, -slope * dist.astype(F32), NEG)


def _fill_attn_bias(bias_ref, group, n_heads):
    @pl.when(pl.program_id(0) == 0)
    def _():
        for g in range(ATT_KV_HEADS):
            bias_ref[0, g] = _attn_bias(g * group, group, n_heads, False)
            bias_ref[1, g] = _attn_bias(g * group, group, n_heads, True)


def _attn_probs_t(kb_scaled, qs, sink_ref, first_head, count, bias):
    sink = jnp.concatenate([jnp.zeros((1, WINDOW), F32) + sink_ref[0, first_head + j] for j in range(count)], axis=1)
    sc = _dot(kb_scaled, qs, NT) + bias
    m = jnp.maximum(jnp.max(sc, axis=0, keepdims=True), sink)
    e = jnp.exp(sc - m)
    es = jnp.exp(sink - m)
    inv = 1.0 / (jnp.sum(e, axis=0, keepdims=True) + es)
    return e * inv, es * inv


ATT_SCALE = ATT_HEAD_DIM ** -0.5


def _attn_specs(s, d, kvd, tq):
    per = tq // WINDOW
    return [pl.BlockSpec((tq, d), lambda i: (i, 0)), pl.BlockSpec((tq, kvd), lambda i: (i, 0)),
            pl.BlockSpec((WINDOW, kvd), lambda i: (jnp.maximum(i * per - 1, 0), 0))]


def _attn_fwd(q, kv, sinks):
    s, d = q.shape
    kvd = kv.shape[1]
    half = kvd // 2
    hd = ATT_HEAD_DIM
    nq = d // hd
    group = nq // ATT_KV_HEADS
    tq = min(s, ATT_TILE)
    per = tq // WINDOW

    def body(q_ref, kvc_ref, kvp_ref, sink_ref, o_ref, band, bias_ref):
        i = pl.program_id(0)
        _fill_attn_bias(bias_ref, group, nq)
        band[0:WINDOW, :] = kvp_ref[...]
        band[WINDOW:, :] = kvc_ref[...]

        def block(b, carry):
            rows = pl.ds(pl.multiple_of(b * WINDOW, WINDOW), WINDOW)
            keys = pl.ds(pl.multiple_of(b * WINDOW, WINDOW), 2 * WINDOW)
            first = (i * per + b) == 0
            for g in range(ATT_KV_HEADS):
                bias = jnp.where(first, bias_ref[1, g], bias_ref[0, g])
                p, _ = _attn_probs_t(band[keys, g * hd:(g + 1) * hd] * ATT_SCALE, _stack_heads(q_ref, rows, g * group, group), sink_ref,
                                     g * group, group, bias)
                out_t = _dot(band[keys, half + g * hd:half + (g + 1) * hd], p, TN)
                _unstack_heads(out_t.T, o_ref, rows, g * group, group)
            return carry

        lax.fori_loop(0, per, block, 0)

    return pl.pallas_call(
        body, name="attn_fwd", grid=(s // tq,),
        in_specs=_attn_specs(s, d, kvd, tq) + [pl.BlockSpec(memory_space=pltpu.SMEM)],
        out_specs=pl.BlockSpec((tq, d), lambda i: (i, 0)), out_shape=jax.ShapeDtypeStruct((s, d), BF16),
        scratch_shapes=[pltpu.VMEM((tq + WINDOW, kvd), BF16), pltpu.VMEM((2, ATT_KV_HEADS, 2 * WINDOW, group * WINDOW), F32)],
        compiler_params=_params(("arbitrary",)),
    )(q, kv, kv, sinks)


def _attn_bwd(q, kv, o, do, sinks):
    s, d = q.shape
    kvd = kv.shape[1]
    half = kvd // 2
    hd = ATT_HEAD_DIM
    nq = d // hd
    group = nq // ATT_KV_HEADS
    tq = min(s, ATT_TILE)
    per = tq // WINDOW
    nt = s // tq

    def body(q_ref, kvc_ref, kvp_ref, o_ref, do_ref, sink_ref, dq_ref, dkvc_ref, dkvp_ref, ds_ref, band, dband, bias_ref):
        i = pl.program_id(0)
        _fill_attn_bias(bias_ref, group, nq)
        band[0:WINDOW, :] = kvp_ref[...]
        band[WINDOW:, :] = kvc_ref[...]
        dband[...] = jnp.zeros_like(dband)
        ds_ref[...] = jnp.zeros_like(ds_ref)

        def block(b, carry):
            rows = pl.ds(pl.multiple_of(b * WINDOW, WINDOW), WINDOW)
            keys = pl.ds(pl.multiple_of(b * WINDOW, WINDOW), 2 * WINDOW)
            first = (i * per + b) == 0
            dks, dvs = [], []
            for g in range(ATT_KV_HEADS):
                kb = band[keys, g * hd:(g + 1) * hd] * ATT_SCALE
                vb = band[keys, half + g * hd:half + (g + 1) * hd]
                qs = _stack_heads(q_ref, rows, g * group, group)
                dos = _stack_heads(do_ref, rows, g * group, group)
                p, ps = _attn_probs_t(kb, qs, sink_ref, g * group, group, jnp.where(first, bias_ref[1, g], bias_ref[0, g]))
                prod = dos.astype(F32) * _stack_heads(o_ref, rows, g * group, group).astype(F32)
                dsum = lax.dot_general(jnp.ones((8, hd), F32), prod, NT, precision=lax.Precision.HIGHEST,
                                       preferred_element_type=F32)[0:1, :]
                dsc = p * (_dot(vb, dos, NT) - dsum)
                dvs.append(_dot(p, dos))
                dks.append(_dot(dsc, qs * ATT_SCALE))
                _unstack_heads(_dot(kb, dsc, TN).T, dq_ref, rows, g * group, group)
                gone = ps * dsum
                for j in range(group):
                    ds_ref[g * group + j:g * group + j + 1, :] += jnp.zeros((1, 128), F32) - jnp.sum(gone[:, j * WINDOW:(j + 1) * WINDOW])
            dband[keys, 0:half] += jnp.concatenate(dks, axis=1)
            dband[keys, half:] += jnp.concatenate(dvs, axis=1)
            return carry

        lax.fori_loop(0, per, block, 0)
        dkvp_ref[...] = dband[0:WINDOW, :]
        dkvc_ref[...] = dband[WINDOW:, :]

    big = pl.BlockSpec((tq, d), lambda i: (i, 0))
    return pl.pallas_call(
        body, name="attn_bwd", grid=(nt,),
        in_specs=_attn_specs(s, d, kvd, tq) + [big, big, pl.BlockSpec(memory_space=pltpu.SMEM)],
        out_specs=[big, pl.BlockSpec((tq, kvd), lambda i: (i, 0)), pl.BlockSpec((None, WINDOW, kvd), lambda i: (i, 0, 0)),
                   pl.BlockSpec((None, nq, 128), lambda i: (i, 0, 0))],
        out_shape=[jax.ShapeDtypeStruct((s, d), BF16), jax.ShapeDtypeStruct((s, kvd), F32), jax.ShapeDtypeStruct((nt, WINDOW, kvd), F32),
                   jax.ShapeDtypeStruct((nt, nq, 128), F32)],
        scratch_shapes=[pltpu.VMEM((tq + WINDOW, kvd), BF16), pltpu.VMEM((tq + WINDOW, kvd), F32),
                        pltpu.VMEM((2, ATT_KV_HEADS, 2 * WINDOW, group * WINDOW), F32)],
        compiler_params=_params(("arbitrary",)),
    )(q, kv, kv, o, do, sinks)


HBM_SPEC = pl.BlockSpec(memory_space=pltpu.HBM)
VMEM_SPEC = pl.BlockSpec(memory_space=pltpu.VMEM)


def _place():
    return lax.axis_index("x"), lax.axis_index("y"), lax.axis_index("c")


def _flip(pos, r):
    return tuple(1 - p if (r >> (2 - a)) & 1 else p for a, p in enumerate(pos))


def _index(pos):
    return 4 * pos[0] + 2 * pos[1] + pos[2]


def _all_gather(name, shards, spec):
    n = len(shards)

    def body(*refs):
        x_refs, o_refs = refs[:n], refs[n:2 * n]
        send_sems, recv_sems, local_sems = refs[2 * n:]
        me = _place()
        sibling = _flip(me, 1)
        far = [_flip(me, r) for r in (4, 2, 6)]

        def copy(t, sem, block, to, src=None):
            rows = o_refs[t].at[_index(block)]
            return pltpu.make_async_remote_copy(
                src_ref=rows if src is None else src, dst_ref=rows, send_sem=send_sems.at[t, sem], recv_sem=recv_sems.at[t, sem],
                device_id=to, device_id_type=MESH)

        own = [pltpu.make_async_copy(x_refs[t], o_refs[t].at[_index(me)], local_sems.at[t]) for t in range(n)]
        for cp in own:
            cp.start()
        first = []
        for t in range(n):
            first.append(copy(t, 0, me, sibling, src=x_refs[t]))
            first += [copy(t, 1 + j, me, peer, src=x_refs[t]) for j, peer in enumerate(far)]
        for cp in first:
            cp.start()
        passed = []
        for j, peer in enumerate(far):
            for t in range(n):
                copy(t, 1 + j, peer, me).wait_recv()
                cp = copy(t, 4 + j, peer, sibling)
                cp.start()
                passed.append(cp)
        for t in range(n):
            copy(t, 0, sibling, me).wait_recv()
            for j, peer in enumerate(far):
                copy(t, 4 + j, _flip(peer, 1), me).wait_recv()
        for cp in first + passed:
            cp.wait_send()
        for cp in own:
            cp.wait()

    return pl.pallas_call(
        body, name=name, in_specs=[spec] * n, out_specs=[spec] * n,
        out_shape=[jax.ShapeDtypeStruct((N_DEV,) + sh.shape, sh.dtype) for sh in shards],
        scratch_shapes=[pltpu.SemaphoreType.DMA((n, 7)), pltpu.SemaphoreType.DMA((n, 7)), pltpu.SemaphoreType.DMA((n,))],
    )(*shards)


SEM_SPEC = pl.BlockSpec(memory_space=pltpu.SEMAPHORE)
ANY_SPEC = pl.BlockSpec(memory_space=pl.ANY)


def _landing(own, mine):
    return lax.dynamic_update_slice(lax.empty((N_DEV,) + own.shape, own.dtype), own[None], (mine,) + (0,) * own.ndim)


def _pinned(a, token):
    return a + token[0:1, 0:1].astype(a.dtype)


def _peer_copies(src_refs, land_refs, send_sems, recv_sems, scatter, arrivals):
    me = _place()
    mine = _index(me)
    copies = []
    for t, (src, land) in enumerate(zip(src_refs, land_refs)):
        for r in range(1, N_DEV):
            peer = _flip(me, r)
            theirs = _index(peer)
            sem = t * (N_DEV - 1) + r - 1
            copies.append(pltpu.make_async_remote_copy(
                src_ref=src.at[theirs] if scatter else src, dst_ref=land.at[theirs if arrivals else mine],
                send_sem=send_sems.at[sem], recv_sem=recv_sems.at[sem], device_id=peer, device_id_type=MESH))
    return copies


def _send_start(name, sources, lands, scatter, after=None):
    n = len(sources)
    extra = 0 if after is None else 1

    def body(*refs):
        outs = refs[2 * n + extra:]
        for out in _peer_copies(refs[:n], refs[n:2 * n], outs[0], outs[1], scatter, False):
            out.start()
        outs[-1][...] = jnp.zeros_like(outs[-1])

    outs = pl.pallas_call(
        body, name=name, in_specs=[HBM_SPEC] * (2 * n) + [ANY_SPEC] * extra,
        out_specs=[SEM_SPEC, SEM_SPEC] + [HBM_SPEC] * (2 * n) + [VMEM_SPEC],
        out_shape=[pltpu.SemaphoreType.DMA((n * (N_DEV - 1),)), pltpu.SemaphoreType.DMA((n * (N_DEV - 1),))]
        + [pltpu.HBM(a.shape, a.dtype) for a in list(sources) + list(lands)] + [jax.ShapeDtypeStruct((8, 128), F32)],
        input_output_aliases={i: 2 + i for i in range(2 * n)},
        compiler_params=pltpu.CompilerParams(has_side_effects=pltpu.SideEffectType.DATAFLOW_SIDE_EFFECTING),
    )(*[pltpu.with_memory_space_constraint(a, pltpu.HBM) for a in list(sources) + list(lands)], *([] if after is None else [after]))
    return outs[0], outs[1], outs[2:2 + n], outs[2 + n:2 + 2 * n], outs[-1]


def _send_wait(name, started, after, scatter):
    send_sems, recv_sems, sources, lands, _ = started
    n = len(sources)

    def body(*refs):
        for out in _peer_copies(refs[:n], refs[n:2 * n], refs[2 * n], refs[2 * n + 1], scatter, False):
            out.wait_send()
        for arrival in _peer_copies(refs[:n], refs[n:2 * n], refs[2 * n], refs[2 * n + 1], scatter, True):
            arrival.wait_recv()

    outs = pl.pallas_call(
        body, name=name, in_specs=[HBM_SPEC] * (2 * n) + [SEM_SPEC, SEM_SPEC, ANY_SPEC], out_specs=[HBM_SPEC] * (2 * n),
        out_shape=[pltpu.HBM(a.shape, a.dtype) for a in list(sources) + list(lands)],
        input_output_aliases={i: i for i in range(2 * n)},
        compiler_params=pltpu.CompilerParams(has_side_effects=pltpu.SideEffectType.DATAFLOW_SIDE_EFFECTING),
    )(*sources, *lands, send_sems, recv_sems, after)
    return outs[n:]


def _pack_rows(parts):
    offsets, row = [], 0
    for part in parts:
        offsets.append(row)
        row += part.shape[0]
    return offsets, -(-row // 8) * 8, -(-max(part.shape[1] for part in parts) // 128) * 128


def _pack(name, parts):
    offsets, rows, width = _pack_rows(parts)

    def body(*refs):
        o_ref = refs[-1]
        o_ref[...] = jnp.zeros_like(o_ref)
        for off, ref in zip(offsets, refs[:-1]):
            o_ref[off:off + ref.shape[0], 0:ref.shape[1]] = ref[...]

    return pl.pallas_call(body, name=name, in_specs=[VMEM_SPEC] * len(parts), out_specs=VMEM_SPEC,
                          out_shape=jax.ShapeDtypeStruct((rows, width), F32))(*parts)


def _adamw_math(w, g, m, v):
    m = ADAM_B1 * m + (1.0 - ADAM_B1) * g
    v = ADAM_B2 * v + (1.0 - ADAM_B2) * (g * g)
    m_hat = m * (1.0 / (1.0 - ADAM_B1 ** ADAM_STEP))
    denom = jnp.sqrt(v * (1.0 / (1.0 - ADAM_B2 ** ADAM_STEP))) + ADAM_EPS
    inv = pl.reciprocal(denom, approx=True)
    inv = inv * (2.0 - denom * inv)
    return -ADAM_LR * (m_hat * inv + ADAM_WD * w), m, v


def _adamw_step(w_ref, m_ref, v_ref, p_ref, g_ref, d_ref, nm_ref, nv_ref):
    g = p_ref[0].astype(F32)
    for dev in range(1, N_DEV):
        g = g + p_ref[dev].astype(F32)
    g_ref[...] = g
    d_ref[...], nm_ref[...], nv_ref[...] = _adamw_math(w_ref[...], g, m_ref[...], v_ref[...])


def _adamw_rows(rows):
    return max(t for t in range(8, min(rows, 256) + 1, 8) if rows % t == 0)


def _adamw_shard(name, w, m, v, partials):
    rows, cols = w.shape
    tr = _adamw_rows(rows)
    blk = pl.BlockSpec((tr, cols), lambda i: (i, 0))
    return pl.pallas_call(
        _adamw_step_fn(), name=name, grid=(rows // tr,), in_specs=[blk, blk, blk, pl.BlockSpec((N_DEV, tr, cols), lambda i: (0, i, 0))],
        out_specs=[blk] * 4, out_shape=[jax.ShapeDtypeStruct((rows, cols), F32)] * 4, compiler_params=_params(("parallel",)),
    )(w, m, v, partials)


def _adamw_step_fn():
    return functools.partial(_adamw_step)


def _adamw_layers(name, w, m, v, partials):
    layers, rows, cols = w.shape
    tr = _adamw_rows(rows)
    last = rows // tr - 1

    def body(w_ref, m_ref, v_ref, *rest):
        for layer in range(layers):
            @pl.when(pl.program_id(0) == layer)
            def _():
                _adamw_step(w_ref, m_ref, v_ref, rest[layer], *rest[layers:])

    blk = pl.BlockSpec((None, tr, cols), lambda l, i: (l, i, 0))
    part = lambda layer: pl.BlockSpec((N_DEV, tr, cols), lambda l, i: (0, jnp.where(l == layer, i, jnp.where(l < layer, 0, last)), 0))
    return pl.pallas_call(
        body, name=name, grid=(layers, rows // tr), in_specs=[blk, blk, blk] + [part(layer) for layer in range(layers)],
        out_specs=[blk] * 4, out_shape=[jax.ShapeDtypeStruct(w.shape, F32)] * 4, compiler_params=_params(("arbitrary", "arbitrary")),
    )(w, m, v, *partials)


def _adamw_small(gathered, places, entries):
    n = len(entries)
    np_ = len(gathered)

    def body(*refs):
        pack_refs = refs[:np_]
        refs = refs[np_ - 1:]
        w_refs, m_refs, v_refs = refs[1:1 + n], refs[1 + n:1 + 2 * n], refs[1 + 2 * n:1 + 3 * n]
        outs = refs[1 + 3 * n:]
        totals = []
        for pack_ref in pack_refs:
            acc = pack_ref[0]
            for dev in range(1, N_DEV):
                acc = acc + pack_ref[dev]
            totals.append(acc)
        mine = _index(_place())
        for e in range(n):
            rows, cols = w_refs[e].shape
            total, off = totals[places[e][0]], places[e][1]
            if entries[e][3]:
                g = jnp.zeros((rows, cols), F32)
                for dev in range(N_DEV):
                    g = g + jnp.where(mine == dev, total[off + dev * rows:off + (dev + 1) * rows, 0:cols], 0.0)
            else:
                g = total[off:off + rows, 0:cols]
            outs[4 * e][...] = g
            outs[4 * e + 1][...], outs[4 * e + 2][...], outs[4 * e + 3][...] = _adamw_math(w_refs[e][...], g, m_refs[e][...], v_refs[e][...])
        outs[4 * n][...] = totals[places[n][0]][places[n][1]:places[n][1] + 1, 0:128]

    shapes = []
    for w, _, _, _ in entries:
        shapes += [jax.ShapeDtypeStruct(w.shape, F32)] * 4
    shapes.append(jax.ShapeDtypeStruct((1, 128), F32))
    return pl.pallas_call(
        body, name="adamw_small", in_specs=[VMEM_SPEC] * (np_ + 3 * n), out_specs=[VMEM_SPEC] * len(shapes), out_shape=shapes,
        compiler_params=pltpu.CompilerParams(vmem_limit_bytes=VMEM_LIMIT),
    )(*gathered, *[e[0] for e in entries], *[e[1] for e in entries], *[e[2] for e in entries])


def _ffn_forward(tag, h, gain, w_up, late):
    s, d = h.shape
    fb = w_up.shape[1]
    tm = _row_tile(s, 2 * MM_ROWS)
    a, = _rmsnorm_cast(f"ffn_norm_{tag}", h, [gain])
    u = _matmul(
        f"ffn_up_{tag}", a, w_up, dims=NT, grid=(s // tm, N_DEV, 1),
        a_spec=pl.BlockSpec((tm, d), lambda i, j, k: (i, 0)),
        b_spec=pl.BlockSpec((None, fb, d), lambda i, j, k: (j, 0, 0)),
        o_spec=pl.BlockSpec((None, None, tm, fb), lambda i, j, k: (j // 4, j % 4, i, 0)),
        out_shape=jax.ShapeDtypeStruct((2, 4, s, fb), BF16))
    w_down, conv_w, conv_b = late(u)
    hidden, conv, out = _ffn_hidden_down(f"ffn_hidden_down_{tag}", u, conv_w, conv_b, w_down, h)
    return out, (a, u, hidden, conv)


def _ffn_backward(tag, h, gain, w_up, w_down, conv_w, conv_b, saved, dout):
    a, u, hidden, conv = saved
    dout, dout_bf = dout
    s, d = h.shape
    fb = w_up.shape[1]
    tm = _row_tile(s, MM_ROWS)
    dhidden = _matmul(
        f"ffn_down_bwd_{tag}", dout_bf, w_down, dims=NT, grid=(s // tm, 4, 1),
        a_spec=pl.BlockSpec((tm, d), lambda i, j, k: (i, 0)),
        b_spec=pl.BlockSpec((None, fb, d), lambda i, j, k: (j, 0, 0)),
        o_spec=pl.BlockSpec((None, tm, fb), lambda i, j, k: (j, i, 0)),
        out_shape=jax.ShapeDtypeStruct((4, s, fb), BF16))
    dw_down = _matmul(
        f"ffn_down_grad_{tag}", hidden, dout_bf, dims=TN, grid=(4, 1, 1),
        a_spec=pl.BlockSpec((None, s, fb), lambda i, j, k: (i, 0, 0)),
        b_spec=pl.BlockSpec((s, d), lambda i, j, k: (0, 0)),
        o_spec=pl.BlockSpec((None, fb, d), lambda i, j, k: (i, 0, 0)),
        out_shape=jax.ShapeDtypeStruct((4, fb, d), BF16))
    du, dconv_w, dconv_b, dh, dh_bf, dgain = _ffn_hidden_up_bwd(f"ffn_hidden_up_bwd_{tag}", u, conv, dhidden, conv_w, w_up, h, gain, dout)
    dw_up = _matmul(
        f"ffn_up_grad_{tag}", du, a, dims=TN, grid=(N_DEV, 1, 1),
        a_spec=pl.BlockSpec((None, None, s, fb), lambda i, j, k: (i // 4, i % 4, 0, 0)),
        b_spec=pl.BlockSpec((s, d), lambda i, j, k: (0, 0)),
        o_spec=pl.BlockSpec((None, fb, d), lambda i, j, k: (i, 0, 0)),
        out_shape=jax.ShapeDtypeStruct((N_DEV, fb, d), BF16))
    return (dh, dh_bf), dgain, dw_up, dw_down, dconv_w, dconv_b


def kernel(x, hg_norm, hg_w_in, hg_lb_logits, hg_out_norm, hg_w_out, kv_norm, w_kv, attn_norm, attn_w_q, attn_sinks, attn_w_o, ffn_norm, ffn_w_up, ffn_conv_w, ffn_conv_b, ffn_w_down, final_norm, loss_target, m_hg_norm, m_hg_w_in, m_hg_lb_logits, m_hg_out_norm, m_hg_w_out, m_kv_norm, m_w_kv, m_attn_norm, m_attn_w_q, m_attn_sinks, m_attn_w_o, m_ffn_norm, m_ffn_w_up, m_ffn_conv_w, m_ffn_conv_b, m_ffn_w_down, m_final_norm, v_hg_norm, v_hg_w_in, v_hg_lb_logits, v_hg_out_norm, v_hg_w_out, v_kv_norm, v_w_kv, v_attn_norm, v_attn_w_q, v_attn_sinks, v_attn_w_o, v_ffn_norm, v_ffn_w_up, v_ffn_conv_w, v_ffn_conv_b, v_ffn_w_down, v_final_norm):
    _, s, d = x.shape
    x0, target = x[0], loss_target[0]
    half = hg_w_in.shape[2]
    fs = ffn_conv_w.shape[2]
    fb = 2 * fs
    kvd = w_kv.shape[1]
    nq = d // ATT_HEAD_DIM
    tm = _row_tile(s, MM_ROWS)

    mine = _index(_place())
    gather = lambda tag, shards, after: _send_start("gather_start_" + tag, shards, [_landing(a, mine) for a in shards], False, after)
    w_in, g_hgn, g_lbl, w_out = _all_gather("gather_hg", [hg_w_in[0].astype(BF16), hg_norm, hg_lb_logits, hg_w_out[0].astype(BF16)], HBM_SPEC)
    w_out = w_out.reshape(d, d)
    up_t = lambda a: jnp.swapaxes(a, -1, -2)
    coming_up0 = gather("ffn_up0", [up_t(ffn_w_up[0]).astype(BF16)], g_hgn)
    hgn = _pinned(g_hgn.reshape(1, d), coming_up0[4])
    lbl = g_lbl.transpose(1, 0, 2).reshape(2, d)
    conv_b = [ffn_conv_b[layer].reshape(4, 1, fb) for layer in range(2)]
    gains = [ffn_norm[0:1], ffn_norm[1:2]]
    kvn, fin = kv_norm.reshape(1, d), final_norm.reshape(1, d)

    a0, = _rmsnorm_cast("hg_norm", x0, [hgn])
    t2 = _row_tile(s, 2 * MM_ROWS)
    p = _matmul(
        "hg_in", a0, w_in, dims=NN, grid=(s // t2, N_DEV, 1),
        a_spec=pl.BlockSpec((t2, d), lambda i, j, k: (i, 0)),
        b_spec=pl.BlockSpec((None, d, half), lambda i, j, k: (j, 0, 0)),
        o_spec=pl.BlockSpec((None, t2, half), lambda i, j, k: (j // 2, i, j % 2)),
        out_shape=jax.ShapeDtypeStruct((4, s, d), BF16))
    o, og, states, gsum = _hgrn2_fwd(p, lbl, hg_out_norm)
    coming_dn0 = gather("ffn_down0", [ffn_conv_w, ffn_w_down[0].astype(BF16)], o)
    x1 = _mm_rows("hg_out", og, _pinned(w_out, coming_dn0[4]), out_dtype=F32, add=x0)
    w_up0, = _send_wait("gather_wait_ffn_up0", coming_up0, x1, False)
    coming_attn = gather("attn", [w_kv.astype(BF16), attn_w_q[0].astype(BF16), attn_w_o[0].astype(BF16)], w_up0)
    gains[0] = _pinned(gains[0], coming_attn[4])
    w_up, w_dn, conv_w, coming = [w_up0, None], [None, None], [], {}

    def late0(u):
        g_cw, w_dn0 = _send_wait("gather_wait_ffn_down0", coming_dn0, u, False)
        w_dn[0] = w_dn0.reshape(4, fb, d)
        conv_w.extend(g_cw[:, layer].reshape(4, 2, CONV_WIDTH, fs).transpose(0, 2, 1, 3).reshape(4, CONV_WIDTH, fb) for layer in range(2))
        coming["up1"] = gather("ffn_up1", [up_t(ffn_w_up[1]).astype(BF16)], w_dn0)
        return w_dn[0], conv_w[0], _pinned(conv_b[0], coming["up1"][4])

    x2, saved0 = _ffn_forward("0", x1, gains[0], w_up[0], late0)
    w_kvg, w_q, w_o = _send_wait("gather_wait_attn", coming_attn, x2, False)
    w_kvg, w_q, w_o = w_kvg.reshape(d, kvd), w_q.reshape(d, d), w_o.reshape(d, d)
    akv, a2 = _rmsnorm_cast("attn_norms", x2, [kvn, attn_norm])
    kv = _mm_rows("kv_proj", akv, w_kvg, out_dtype=BF16)
    q = _mm_rows("q_proj", a2, w_q, out_dtype=BF16)
    coming_dn1 = gather("ffn_down1", [ffn_w_down[1].astype(BF16)], q)
    att = _attn_fwd(q, kv, _pinned(attn_sinks, coming_dn1[4]))
    x3 = _mm_rows("attn_out", att, w_o, out_dtype=F32, add=x2)
    w_up[1], = _send_wait("gather_wait_ffn_up1", coming["up1"], x3, False)

    def late1(u):
        w_dn[1] = _send_wait("gather_wait_ffn_down1", coming_dn1, u, False)[0].reshape(4, fb, d)
        return w_dn[1], conv_w[1], conv_b[1]

    x4, saved1 = _ffn_forward("1", x3, gains[1], w_up[1], late1)
    dx4, dx4_bf, d_fin, loss_part = _loss_head(x4, fin, target)

    dx3, d_fn1, dw_up1, dw_dn1, dcw1, dcb1 = _ffn_backward("1", x3, gains[1], w_up[1], w_dn[1], conv_w[1], conv_b[1], saved1, (dx4, dx4_bf))
    rows = d // N_DEV
    scatter = lambda tag, stacks: _send_start("scatter_start_" + tag, stacks, [_landing(lax.dynamic_index_in_dim(a, mine, keepdims=False), mine) for a in stacks], True)
    going_ffn1 = scatter("ffn1", [dw_up1, dw_dn1.reshape(N_DEV, fs, d)])
    datt = _mm_rows_nt("attn_out_bwd", dx3[1], w_o, out_dtype=BF16)
    dw_o = _mm_tn("attn_out_grad", att, dx3[1])
    dq, dkv_own, dkv_before, dsink = _attn_bwd(q, kv, att, datt, _pinned(attn_sinks, going_ffn1[4]))
    tiles = dkv_before.shape[0]
    dkv = dkv_own.reshape(tiles, s // tiles, kvd)
    dkv = jnp.concatenate([dkv[:, :-WINDOW], dkv[:, -WINDOW:] + jnp.pad(dkv_before[1:], ((0, 1), (0, 0), (0, 0)))], axis=1).reshape(s, kvd)
    dw_q = _mm_tn("q_proj_grad", a2, dq)
    dw_kv = _mm_tn("kv_proj_grad", akv, dkv)
    going_attn = scatter("attn", [dw_kv.reshape(N_DEV, rows, kvd), dw_q.reshape(N_DEV, rows, d), dw_o.reshape(N_DEV, rows, d)])
    whole = lambda a_ref, b_ref: [(a_ref[...], b_ref[...])]
    rows_of = lambda width: (lambda tile: pl.BlockSpec((tile, width), lambda i: (i, 0)))
    dx2, (d_kvn, d_attn) = _proj_norm_bwd("attn_in_bwd", x2, dx3[0], [
        (dkv, rows_of(kvd), w_kvg, pl.BlockSpec((d, kvd), lambda i: (0, 0)), whole, _pinned(kvn, going_attn[4])),
        (dq, rows_of(d), w_q, pl.BlockSpec((d, d), lambda i: (0, 0)), whole, attn_norm)])
    dx1, d_fn0, dw_up0, dw_dn0, dcw0, dcb0 = _ffn_backward("0", x1, gains[0], w_up[0], w_dn[0], conv_w[0], conv_b[0], saved0, dx2)
    dw_out = _mm_tn("hg_out_grad", og, dx1[1])
    going_ffn0 = scatter("ffn0", [dw_up0, dw_dn0.reshape(N_DEV, fs, d), dw_out.reshape(N_DEV, rows, d)])
    dog = _mm_rows_nt("hg_out_bwd", dx1[1], w_out, out_dtype=F32)
    dp, d_lbl, d_ogain = _hgrn2_bwd(p, lbl, _pinned(hg_out_norm, going_ffn0[4]), o, dog, states, gsum)
    dw_in = _matmul(
        "hg_in_grad", a0, dp, dims=TN, grid=(1, N_DEV, 1),
        a_spec=pl.BlockSpec((s, d), lambda i, j, k: (0, 0)),
        b_spec=pl.BlockSpec((None, s, half), lambda i, j, k: (j // 2, 0, j % 2)),
        o_spec=pl.BlockSpec((None, d, half), lambda i, j, k: (j, 0, 0)),
        out_shape=jax.ShapeDtypeStruct((N_DEV, d, half), BF16))
    going_hg = scatter("hg", [dw_in])
    (dx0, _), (d_hgn,) = _proj_norm_bwd("hg_in_bwd", x0, dx1[0], [
        (dp, lambda tile: pl.BlockSpec((4, tile, d), lambda i: (0, i, 0)), w_in, pl.BlockSpec((N_DEV, d, half), lambda i: (0, 0, 0)),
         lambda g_ref, w_ref: [(g_ref[k // 2, :, (k % 2) * half:(k % 2 + 1) * half], w_ref[k]) for k in range(N_DEV)],
         _pinned(hgn, going_hg[4]))])

    as_blocks = lambda a, r: a.reshape(r, N_DEV, -1).transpose(1, 0, 2).reshape(N_DEV * r, -1)
    d_cw = jnp.concatenate([g.transpose(1, 0, 2).reshape(CONV_WIDTH, 4 * fb) for g in (dcw0, dcw1)], axis=0)
    parts = [d_fin, jnp.concatenate([d_fn0, d_fn1], axis=0), jnp.concatenate([dcb0.reshape(1, 4 * fb), dcb1.reshape(1, 4 * fb)], axis=0),
             as_blocks(d_cw, 2 * CONV_WIDTH), d_attn, jnp.sum(dsink[:, :, 0], axis=0).reshape(1, nq), d_kvn, d_ogain,
             as_blocks(d_hgn, 1), as_blocks(d_lbl, 2), loss_part]
    wide = [2]
    packs = [[parts[i] for i in wide], [part for i, part in enumerate(parts) if i not in wide]]
    places = [None] * len(parts)
    for which, members in enumerate([wide, [i for i in range(len(parts)) if i not in wide]]):
        for i, off in zip(members, _pack_rows(packs[which])[0]):
            places[i] = (which, off)
    packed = [_pack("pack_wide_grads", packs[0]), _pack("pack_narrow_grads", packs[1])]
    going_small = _send_start("small_grads_start", packed, [_landing(a, mine) for a in packed], False)

    arrive = lambda tag, going, after: _send_wait("scatter_wait_" + tag, going, after, True)
    (l_up1, l_dn1), (l_kv, l_q, l_o), (l_up0, l_dn0, l_out) = (
        arrive("ffn1", going_ffn1, going_small[4]), arrive("attn", going_attn, going_small[4]), arrive("ffn0", going_ffn0, going_small[4]))
    big = {}
    for tag, w, m, v, part in [
            ("w_kv", w_kv, m_w_kv, v_w_kv, l_kv), ("attn_w_q", attn_w_q[0], m_attn_w_q[0], v_attn_w_q[0], l_q),
            ("attn_w_o", attn_w_o[0], m_attn_w_o[0], v_attn_w_o[0], l_o)]:
        big[tag] = _adamw_shard("adamw_" + tag, w, m, v, part)
    big["ffn_w_up"] = [up_t(a) for a in _adamw_layers("adamw_ffn_w_up", up_t(ffn_w_up), up_t(m_ffn_w_up), up_t(v_ffn_w_up), (l_up0, l_up1))]
    big["ffn_w_down"] = _adamw_layers("adamw_ffn_w_down", ffn_w_down, m_ffn_w_down, v_ffn_w_down, (l_dn0, l_dn1))
    lead = lambda tag: [a[None] for a in big[tag]]

    both_done = big["ffn_w_up"][0][0, 0:1, 0:1] + big["ffn_w_down"][0][0, 0:1, 0:1]
    gathered = _send_wait("small_grads_wait", going_small, both_done, False)
    two = lambda a: a.reshape(-1, a.shape[-1])
    small = [(fin, m_final_norm.reshape(1, d), v_final_norm.reshape(1, d), False), (ffn_norm, m_ffn_norm, v_ffn_norm, False),
             (ffn_conv_b, m_ffn_conv_b, v_ffn_conv_b, False), (two(ffn_conv_w), two(m_ffn_conv_w), two(v_ffn_conv_w), True),
             (attn_norm, m_attn_norm, v_attn_norm, False), (attn_sinks, m_attn_sinks, v_attn_sinks, False),
             (kvn, m_kv_norm.reshape(1, d), v_kv_norm.reshape(1, d), False), (hg_out_norm, m_hg_out_norm, v_hg_out_norm, False),
             (hg_norm, m_hg_norm, v_hg_norm, True), (hg_lb_logits, m_hg_lb_logits, v_hg_lb_logits, True)]
    res = _adamw_small(gathered, places, small)
    l_in, = arrive("hg", going_hg, gathered[1])
    big["hg_w_in"] = _adamw_shard("adamw_hg_w_in", hg_w_in[0], m_hg_w_in[0], v_hg_w_in[0], l_in)
    big["hg_w_out"] = _adamw_shard("adamw_hg_w_out", hg_w_out[0], m_hg_w_out[0], v_hg_w_out[0], l_out)
    names = ["final_norm", "ffn_norm", "ffn_conv_b", "ffn_conv_w", "attn_norm", "attn_sinks", "kv_norm", "hg_out_norm", "hg_norm", "hg_lb_logits"]
    shapes = {"final_norm": final_norm.shape, "kv_norm": kv_norm.shape, "ffn_conv_w": ffn_conv_w.shape}
    out = {n: [a.reshape(shapes[n]) if n in shapes else a for a in res[4 * i:4 * i + 4]] for i, n in enumerate(names)}
    out.update(hg_w_in=lead("hg_w_in"), hg_w_out=lead("hg_w_out"), w_kv=big["w_kv"], attn_w_q=lead("attn_w_q"), attn_w_o=lead("attn_w_o"),
               ffn_w_up=big["ffn_w_up"], ffn_w_down=big["ffn_w_down"])
    order = ["hg_norm", "hg_w_in", "hg_lb_logits", "hg_out_norm", "hg_w_out", "kv_norm", "w_kv", "attn_norm", "attn_w_q", "attn_sinks",
             "attn_w_o", "ffn_norm", "ffn_w_up", "ffn_conv_w", "ffn_conv_b", "ffn_w_down", "final_norm"]
    loss = res[-1][0, 0]
    return (loss, dx0[None], *[out[n][0] for n in order], *[out[n][1] for n in order], *[out[n][2] for n in order], *[out[n][3] for n in order])
```

```python
import functools

import jax
import jax.numpy as jnp
from jax import lax
from jax.experimental import pallas as pl
from jax.experimental.pallas import tpu as pltpu

F32 = jnp.float32
BF16 = jnp.bfloat16

EPS = 1e-6
HG_EXPAND = 128
HG_CHUNK = 32
ATT_HEAD_DIM = 64
ATT_KV_HEADS = 2
WINDOW = 128
CONV_WIDTH = 3
ADAM_LR = 0.001
ADAM_B1 = 0.9
ADAM_B2 = 0.999
ADAM_EPS = 1e-08
ADAM_WD = 0.01
ADAM_STEP = 10

N_DEV = 8
VMEM_LIMIT = 48 * 1024 * 1024
NEG = -1e30

NN = (((1,), (0,)), ((), ()))
NT = (((1,), (1,)), ((), ()))
TN = (((0,), (0,)), ((), ()))
MESH = pl.DeviceIdType.MESH


def _dot(a, b, dims=NN):
    return lax.dot_general(a.astype(BF16), b.astype(BF16), dims, preferred_element_type=F32)


def _sigmoid(x):
    return 0.5 * jnp.tanh(0.5 * x) + 0.5


def _silu(x):
    return x * _sigmoid(x)


def _silu_and_grad(x):
    s = _sigmoid(x)
    return x * s, s * (1.0 + x * (1.0 - s))


def _dsilu(x):
    return _silu_and_grad(x)[1]


def _params(semantics):
    return pltpu.CompilerParams(dimension_semantics=semantics, vmem_limit_bytes=VMEM_LIMIT)


def _row_tile(rows, want=512):
    return min(rows, want)


MM_ROWS = 1024


def _matmul(name, a, b, *, dims, grid, a_spec, b_spec, o_spec, out_shape, add=None, add_spec=None):
    assert grid[2] == 1

    def body(*refs):
        a_ref, b_ref, o_ref = refs[0], refs[1], refs[-1]
        total = _dot(a_ref[...], b_ref[...], dims)
        if add is not None:
            total = total + refs[2][...]
        o_ref[...] = total.astype(o_ref.dtype)

    in_specs = [a_spec, b_spec] + ([] if add is None else [add_spec])
    args = (a, b) + (() if add is None else (add,))
    return pl.pallas_call(
        body, name=name, grid=grid, in_specs=in_specs, out_specs=o_spec, out_shape=out_shape,
        compiler_params=_params(("parallel", "parallel", "arbitrary")),
    )(*args)


def _mm_rows(name, a, w, *, out_dtype, add=None):
    s, kdim = a.shape
    n = w.shape[1]
    tm = _row_tile(s, MM_ROWS)
    return _matmul(
        name, a, w, dims=NN, grid=(s // tm, 1, 1),
        a_spec=pl.BlockSpec((tm, kdim), lambda i, j, k: (i, 0)),
        b_spec=pl.BlockSpec((kdim, n), lambda i, j, k: (0, 0)),
        o_spec=pl.BlockSpec((tm, n), lambda i, j, k: (i, 0)),
        out_shape=jax.ShapeDtypeStruct((s, n), out_dtype),
        add=add, add_spec=None if add is None else pl.BlockSpec((tm, n), lambda i, j, k: (i, 0)),
    )


def _mm_rows_nt(name, a, w, *, out_dtype):
    s, n = a.shape
    kdim = w.shape[0]
    tm = _row_tile(s, MM_ROWS)
    return _matmul(
        name, a, w, dims=NT, grid=(s // tm, 1, 1),
        a_spec=pl.BlockSpec((tm, n), lambda i, j, k: (i, 0)),
        b_spec=pl.BlockSpec((kdim, n), lambda i, j, k: (0, 0)),
        o_spec=pl.BlockSpec((tm, kdim), lambda i, j, k: (i, 0)),
        out_shape=jax.ShapeDtypeStruct((s, kdim), out_dtype),
    )


def _mm_tn(name, a, g):
    s, m = a.shape
    n = g.shape[1]
    tn = min(n, 512)
    return _matmul(
        name, a, g, dims=TN, grid=(1, n // tn, 1),
        a_spec=pl.BlockSpec((s, m), lambda i, j, k: (0, 0)),
        b_spec=pl.BlockSpec((s, tn), lambda i, j, k: (0, j)),
        o_spec=pl.BlockSpec((m, tn), lambda i, j, k: (0, j)),
        out_shape=jax.ShapeDtypeStruct((m, n), BF16),
    )


def _rmsnorm_cast(name, h, gains):
    s, d = h.shape
    tm = _row_tile(s)
    n = len(gains)

    def body(*refs):
        h_ref, g_refs, o_refs = refs[0], refs[1:1 + n], refs[1 + n:]
        xv = h_ref[...]
        xhat = xv * lax.rsqrt(jnp.mean(xv * xv, axis=-1, keepdims=True) + EPS)
        for g_ref, o_ref in zip(g_refs, o_refs):
            o_ref[...] = (xhat * g_ref[...]).astype(BF16)

    row = pl.BlockSpec((tm, d), lambda i: (i, 0))
    vec = pl.BlockSpec((1, d), lambda i: (0, 0))
    return pl.pallas_call(
        body, name=name, grid=(s // tm,), in_specs=[row] + [vec] * n, out_specs=[row] * n,
        out_shape=[jax.ShapeDtypeStruct((s, d), BF16)] * n, compiler_params=_params(("parallel",)),
    )(h, *gains)


def _proj_norm_bwd(name, h, dres, branches):
    s, d = h.shape
    tm = _row_tile(s)
    n = len(branches)

    def body(*refs):
        h_ref, dres_ref = refs[0], refs[1]
        g_refs, w_refs, gain_refs = refs[2:2 + n], refs[2 + n:2 + 2 * n], refs[2 + 2 * n:2 + 3 * n]
        dh_ref, dhb_ref, dg_refs = refs[2 + 3 * n], refs[3 + 3 * n], refs[4 + 3 * n:]
        i = pl.program_id(0)
        xv = h_ref[...]
        r = lax.rsqrt(jnp.mean(xv * xv, axis=-1, keepdims=True) + EPS)
        xhat = xv * r
        total = dres_ref[...]
        for branch, g_ref, w_ref, gain_ref, dg_ref in zip(branches, g_refs, w_refs, gain_refs, dg_refs):
            pairs = branch[4](g_ref, w_ref)
            da = _dot(*pairs[0], NT)
            for pair in pairs[1:]:
                da = da + _dot(*pair, NT)
            dgain = jnp.sum(da * xhat, axis=0, keepdims=True)

            @pl.when(i == 0)
            def _():
                dg_ref[...] = dgain

            @pl.when(i > 0)
            def _():
                dg_ref[...] += dgain

            dxhat = da * gain_ref[...]
            total = total + r * (dxhat - xhat * jnp.mean(dxhat * xhat, axis=-1, keepdims=True))
        dh_ref[...] = total
        dhb_ref[...] = total.astype(BF16)

    row = pl.BlockSpec((tm, d), lambda i: (i, 0))
    vec = pl.BlockSpec((1, d), lambda i: (0, 0))
    outs = pl.pallas_call(
        body, name=name, grid=(s // tm,),
        in_specs=[row, row] + [b[1](tm) for b in branches] + [b[3] for b in branches] + [vec] * n, out_specs=[row, row] + [vec] * n,
        out_shape=[jax.ShapeDtypeStruct((s, d), F32), jax.ShapeDtypeStruct((s, d), BF16)] + [jax.ShapeDtypeStruct((1, d), F32)] * n,
        compiler_params=_params(("arbitrary",)),
    )(h, dres, *[b[0] for b in branches], *[b[2] for b in branches], *[b[5] for b in branches])
    return (outs[0], outs[1]), outs[2:]


def _loss_head(h, gain, target):
    s, d = h.shape
    tm = _row_tile(s)

    def body(h_ref, g_ref, t_ref, dh_ref, dhb_ref, dg_ref, loss_ref):
        i = pl.program_id(0)
        xv = h_ref[...]
        r = lax.rsqrt(jnp.mean(xv * xv, axis=-1, keepdims=True) + EPS)
        xhat = xv * r
        err = xhat * g_ref[...] - t_ref[...]
        dy = err * (1.0 / d)
        part = jnp.zeros((1, 128), F32) + 0.5 * jnp.sum(jnp.mean(err * err, axis=-1, keepdims=True))
        dgain = jnp.sum(dy * xhat, axis=0, keepdims=True)

        @pl.when(i == 0)
        def _():
            dg_ref[...] = dgain
            loss_ref[...] = part

        @pl.when(i > 0)
        def _():
            dg_ref[...] += dgain
            loss_ref[...] += part

        dxhat = dy * g_ref[...]
        dh = r * (dxhat - xhat * jnp.mean(dxhat * xhat, axis=-1, keepdims=True))
        dh_ref[...] = dh
        dhb_ref[...] = dh.astype(BF16)

    row = pl.BlockSpec((tm, d), lambda i: (i, 0))
    vec = pl.BlockSpec((1, d), lambda i: (0, 0))
    return pl.pallas_call(
        body, name="loss_head", grid=(s // tm,), in_specs=[row, vec, row],
        out_specs=[row, row, vec, pl.BlockSpec((1, 128), lambda i: (0, 0))],
        out_shape=[jax.ShapeDtypeStruct((s, d), F32), jax.ShapeDtypeStruct((s, d), BF16), jax.ShapeDtypeStruct((1, d), F32),
                   jax.ShapeDtypeStruct((1, 128), F32)],
        compiler_params=_params(("arbitrary",)),
    )(h, gain, target)


def _bdot(a, b, ca, cb):
    return lax.dot_general(a.astype(BF16), b.astype(BF16), (((ca,), (cb,)), ((0,), (0,))), preferred_element_type=F32)


def _chunk_cumsum(xv, reverse=False):
    n = xv.shape[0]
    row = lax.broadcasted_iota(jnp.int32, xv.shape, 0) % HG_CHUNK
    step = 1
    while step < HG_CHUNK:
        if reverse:
            xv = xv + jnp.where(row < HG_CHUNK - step, pltpu.roll(xv, n - step, axis=0), 0.0)
        else:
            xv = xv + jnp.where(row >= step, pltpu.roll(xv, step, axis=0), 0.0)
        step *= 2
    return xv


def _hg_terms(p_ref, lbl_ref, g_ref=None):
    pq = p_ref[0].astype(F32)
    pf = p_ref[1].astype(F32)
    lb = _sigmoid(lbl_ref[0:1, :] - lbl_ref[1:2, :])
    sig = _sigmoid(pf)
    fg = lb + (1.0 - lb) * sig
    nc = pq.shape[0] // HG_CHUNK
    chunks = lambda a: a.reshape(nc, HG_CHUNK, HG_EXPAND)
    q = chunks(_silu(pq) * HG_EXPAND ** -0.5)
    k = chunks(1.0 - fg)
    v = chunks(p_ref[2].astype(F32))
    g = chunks(_chunk_cumsum(jnp.log(fg)) if g_ref is None else g_ref[...])
    gm = g[:, HG_CHUNK // 2 - 1:HG_CHUNK // 2, :]
    gl = g[:, HG_CHUNK - 1:HG_CHUNK, :]
    e_mid, e_inv, e_all, e_end = jnp.exp(g - gm), jnp.exp(gm - g), jnp.exp(g), jnp.exp(gl - g)
    terms = dict(q=q, k=k, v=v, g=g, qd=q * e_all, qt=q * e_mid, kt=k * e_inv, kd=k * e_end, e_last=jnp.exp(gl),
                 e_mid=e_mid, e_inv=e_inv, e_all=e_all, e_end=e_end)
    return terms, (pq, sig, fg, lb)


def _causal(nc):
    r = lax.broadcasted_iota(jnp.int32, (nc, HG_CHUNK, HG_CHUNK), 1)
    c = lax.broadcasted_iota(jnp.int32, (nc, HG_CHUNK, HG_CHUNK), 2)
    return r >= c


def _hgrn2_fwd(p, lb_logits, out_gain):
    _, s, d = p.shape
    heads = d // HG_EXPAND
    t = _row_tile(s, 2048)
    nc = t // HG_CHUNK

    def body(p_ref, lbl_ref, gain_ref, o_ref, og_ref, st_ref, g_ref, state, decay):
        @pl.when(pl.program_id(1) == 0)
        def _():
            state[...] = jnp.zeros_like(state)

        tm, _ = _hg_terms(p_ref, lbl_ref)
        g_ref[...] = tm["g"].reshape(t, HG_EXPAND)
        decay[...] = tm["e_last"]
        st_ref[...] = _bdot(tm["v"], tm["kd"], 1, 1)

        def chunk(c, carry):
            add = st_ref[c]
            st = state[...]
            st_ref[c] = st
            state[...] = st * decay[c] + add
            return carry

        lax.fori_loop(0, nc, chunk, 0)
        a = jnp.where(_causal(nc), _bdot(tm["qt"], tm["kt"], 2, 2), 0.0)
        ov = (_bdot(tm["qd"], st_ref[...], 2, 2) + _bdot(a, tm["v"], 2, 1)).reshape(t, HG_EXPAND)
        o_ref[...] = ov
        on = ov * lax.rsqrt(jnp.mean(ov * ov, axis=-1, keepdims=True) + EPS) * gain_ref[...]
        og_ref[...] = (on * _silu(p_ref[3].astype(F32))).astype(BF16)

    blk = pl.BlockSpec((t, HG_EXPAND), lambda h, b: (b, h))
    return pl.pallas_call(
        body, name="hgrn2_fwd", grid=(heads, s // t),
        in_specs=[pl.BlockSpec((4, t, HG_EXPAND), lambda h, b: (0, b, h)), pl.BlockSpec((2, HG_EXPAND), lambda h, b: (0, h)),
                  pl.BlockSpec((1, HG_EXPAND), lambda h, b: (0, 0))],
        out_specs=[blk, blk, pl.BlockSpec((None, nc, HG_EXPAND, HG_EXPAND), lambda h, b: (h, b, 0, 0)), blk],
        out_shape=[jax.ShapeDtypeStruct((s, d), F32), jax.ShapeDtypeStruct((s, d), BF16),
                   jax.ShapeDtypeStruct((heads, s // HG_CHUNK, HG_EXPAND, HG_EXPAND), F32), jax.ShapeDtypeStruct((s, d), F32)],
        scratch_shapes=[pltpu.VMEM((HG_EXPAND, HG_EXPAND), F32), pltpu.VMEM((nc, 1, HG_EXPAND), F32)],
        compiler_params=_params(("parallel", "arbitrary")),
    )(p, lb_logits, out_gain)


def _hgrn2_bwd(p, lb_logits, out_gain, o, dog, states, gsum):
    _, s, d = p.shape
    heads = d // HG_EXPAND
    t = _row_tile(s, 1024)
    nc = t // HG_CHUNK
    nb = s // t

    def body(p_ref, lbl_ref, gain_ref, o_ref, dog_ref, st_ref, g_ref, dp_ref, dlbl_ref, dgain_ref, dstate, decay, dst_s):
        h, b = pl.program_id(0), pl.program_id(1)

        @pl.when(b == 0)
        def _():
            dstate[...] = jnp.zeros_like(dstate)

        tm, (pq, sig, fg, lb) = _hg_terms(p_ref, lbl_ref, g_ref)
        pg = p_ref[3].astype(F32)
        ov = o_ref[...]
        r = lax.rsqrt(jnp.mean(ov * ov, axis=-1, keepdims=True) + EPS)
        ohat = ov * r
        dogv = dog_ref[...]
        d_on = dogv * _silu(pg)
        dp_ref[3] = (dogv * ohat * gain_ref[...] * _dsilu(pg)).astype(BF16)
        dgain = jnp.sum(d_on * ohat, axis=0, keepdims=True)

        @pl.when((h == 0) & (b == 0))
        def _():
            dgain_ref[...] = dgain

        @pl.when((h > 0) | (b > 0))
        def _():
            dgain_ref[...] += dgain

        dohat = d_on * gain_ref[...]
        do = (r * (dohat - ohat * jnp.mean(dohat * ohat, axis=-1, keepdims=True))).reshape(nc, HG_CHUNK, HG_EXPAND)

        decay[...] = tm["e_last"]
        dst_s[...] = _bdot(do, tm["qd"], 1, 1)

        def chunk(i, carry):
            c = nc - 1 - i
            add = dst_s[c]
            dst = dstate[...]
            dst_s[c] = dst
            dstate[...] = dst * decay[c] + add
            return carry

        lax.fori_loop(0, nc, chunk, 0)
        st, dst = st_ref[...], dst_s[...]
        causal = _causal(nc)
        a = jnp.where(causal, _bdot(tm["qt"], tm["kt"], 2, 2), 0.0)
        da = jnp.where(causal, _bdot(do, tm["v"], 2, 2), 0.0)
        dqt = _bdot(da, tm["kt"], 2, 1)
        dkt = _bdot(da, tm["qt"], 1, 1)
        dqd = _bdot(do, st, 2, 1)
        dkd = _bdot(tm["v"], dst, 2, 1)
        dv = _bdot(a, do, 1, 1) + _bdot(tm["kd"], dst, 2, 2)
        dq = dqt * tm["e_mid"] + dqd * tm["e_all"]
        dk = dkt * tm["e_inv"] + dkd * tm["e_end"]
        dg = dqt * tm["qt"] - dkt * tm["kt"] + dqd * tm["qd"] - dkd * tm["kd"]
        dgl = jnp.sum(dkd * tm["kd"], axis=1, keepdims=True) + tm["e_last"] * jnp.sum(dst * st, axis=1, keepdims=True)
        last_row = lax.broadcasted_iota(jnp.int32, (nc, HG_CHUNK, HG_EXPAND), 1) == HG_CHUNK - 1
        flat = lambda a3: a3.reshape(t, HG_EXPAND)
        dlf = _chunk_cumsum(flat(dg + jnp.where(last_row, dgl, 0.0)), reverse=True)
        dfg = dlf / fg - flat(dk)
        dlb = jnp.sum(dfg * (1.0 - sig), axis=0, keepdims=True)
        dl0 = dlb * lb * (1.0 - lb)
        dlbl = jnp.concatenate([dl0, -dl0], axis=0)

        @pl.when(b == 0)
        def _():
            dlbl_ref[...] = dlbl

        @pl.when(b > 0)
        def _():
            dlbl_ref[...] += dlbl

        dp_ref[0] = (flat(dq) * HG_EXPAND ** -0.5 * _dsilu(pq)).astype(BF16)
        dp_ref[1] = (dfg * (1.0 - lb) * sig * (1.0 - sig)).astype(BF16)
        dp_ref[2] = flat(dv).astype(BF16)

    blk = pl.BlockSpec((t, HG_EXPAND), lambda h, b: (nb - 1 - b, h))
    pblk = pl.BlockSpec((4, t, HG_EXPAND), lambda h, b: (0, nb - 1 - b, h))
    return pl.pallas_call(
        body, name="hgrn2_bwd", grid=(heads, nb),
        in_specs=[pblk, pl.BlockSpec((2, HG_EXPAND), lambda h, b: (0, h)), pl.BlockSpec((1, HG_EXPAND), lambda h, b: (0, 0)),
                  blk, blk, pl.BlockSpec((None, nc, HG_EXPAND, HG_EXPAND), lambda h, b: (h, nb - 1 - b, 0, 0)), blk],
        out_specs=[pblk, pl.BlockSpec((2, HG_EXPAND), lambda h, b: (0, h)), pl.BlockSpec((1, HG_EXPAND), lambda h, b: (0, 0))],
        out_shape=[jax.ShapeDtypeStruct((4, s, d), BF16), jax.ShapeDtypeStruct((2, d), F32), jax.ShapeDtypeStruct((1, HG_EXPAND), F32)],
        scratch_shapes=[pltpu.VMEM((HG_EXPAND, HG_EXPAND), F32), pltpu.VMEM((nc, 1, HG_EXPAND), F32),
                        pltpu.VMEM((nc, HG_EXPAND, HG_EXPAND), F32)],
        compiler_params=_params(("arbitrary", "arbitrary")),
    )(p, lb_logits, out_gain, o, dog, states, gsum)


HALO = 8
FFN_FWD_ROWS = 512
FFN_BWD_ROWS = 256


def _shift_down(xv, n):
    return pltpu.roll(xv, n, axis=0)


def _shift_up(xv, n):
    return pltpu.roll(xv, xv.shape[0] - n, axis=0)


def _ffn_hidden_down(name, u, conv_w, conv_b, w_down, h):
    _, nj, s, fb = u.shape
    d = w_down.shape[2]
    tm = _row_tile(s, FFN_FWD_ROWS)
    per = tm // HALO

    def body(gate_ref, prev_ref, val_ref, w_ref, b_ref, wd_ref, h_ref, hid_ref, conv_ref, o_ref):
        i = pl.program_id(0)
        total = h_ref[...]
        for j in range(nj):
            prev = jnp.where(i > 0, prev_ref[j].astype(F32), 0.0)
            ext = jnp.concatenate([prev, gate_ref[j].astype(F32)], axis=0)
            conv = b_ref[j] + w_ref[j, 2:3, :] * ext[HALO:]
            conv = conv + w_ref[j, 1:2, :] * _shift_down(ext, 1)[HALO:]
            conv = conv + w_ref[j, 0:1, :] * _shift_down(ext, 2)[HALO:]
            conv = conv.astype(BF16)
            conv_ref[j] = conv
            hidden = _silu(conv) * val_ref[j]
            hid_ref[j] = hidden
            total = total + _dot(hidden, wd_ref[j])
        o_ref[...] = total

    row = pl.BlockSpec((tm, d), lambda i: (i, 0))
    return pl.pallas_call(
        body, name=name, grid=(s // tm,),
        in_specs=[pl.BlockSpec((None, nj, tm, fb), lambda i: (0, 0, i, 0)),
                  pl.BlockSpec((None, nj, HALO, fb), lambda i: (0, 0, jnp.maximum(i * per - 1, 0), 0)),
                  pl.BlockSpec((None, nj, tm, fb), lambda i: (1, 0, i, 0)),
                  pl.BlockSpec((nj, CONV_WIDTH, fb), lambda i: (0, 0, 0)), pl.BlockSpec((nj, 1, fb), lambda i: (0, 0, 0)),
                  pl.BlockSpec((nj, fb, d), lambda i: (0, 0, 0)), row],
        out_specs=[pl.BlockSpec((nj, tm, fb), lambda i: (0, i, 0)), pl.BlockSpec((nj, tm, fb), lambda i: (0, i, 0)), row],
        out_shape=[jax.ShapeDtypeStruct((nj, s, fb), BF16), jax.ShapeDtypeStruct((nj, s, fb), BF16), jax.ShapeDtypeStruct((s, d), F32)],
        compiler_params=_params(("parallel",)),
    )(u, u, u, conv_w, conv_b, w_down, h)


def _ffn_hidden_up_bwd(name, u, conv, dh, conv_w, w_up, h, gain, dres):
    _, nj, s, fb = u.shape
    d = w_up.shape[2]
    tm = _row_tile(s, FFN_BWD_ROWS)
    per = tm // HALO
    nblk = s // HALO
    ni = s // tm

    def body(gate_ref, conv_ref, cnext_ref, val_ref, vnext_ref, dh_ref, dhnext_ref, w_ref, wu_ref, h_ref, gain_ref, dres_ref,
             du_ref, dw_ref, db_ref, dx_ref, dxb_ref, dgain_ref):
        i = pl.program_id(0)
        has_next = i < ni - 1
        total = None
        for j in range(nj):
            act, dact = _silu_and_grad(conv_ref[j])
            dval = dh_ref[j] * act
            after = jnp.where(has_next, dhnext_ref[j].astype(F32), 0.0) * vnext_ref[j].astype(F32) * _dsilu(cnext_ref[j].astype(F32))
            dconv = jnp.concatenate([(dh_ref[j] * val_ref[j] * dact).astype(F32), after], axis=0)
            taps = [_shift_up(dconv, 2)[:tm], _shift_up(dconv, 1)[:tm], dconv[:tm]]
            dgate = (w_ref[j, 0:1, :] * taps[0] + w_ref[j, 1:2, :] * taps[1] + w_ref[j, 2:3, :] * taps[2]).astype(BF16)
            du_ref[0, j] = dgate
            du_ref[1, j] = dval
            part = _dot(dgate, wu_ref[j]) + _dot(dval, wu_ref[nj + j])
            total = part if total is None else total + part
            gate = gate_ref[j].astype(F32)
            dw = jnp.concatenate([jnp.sum(tap * gate, axis=0, keepdims=True) for tap in taps], axis=0)
            db = jnp.sum(taps[2], axis=0, keepdims=True)

            @pl.when(i == 0)
            def _():
                dw_ref[j] = dw
                db_ref[j] = db

            @pl.when(i > 0)
            def _():
                dw_ref[j] += dw
                db_ref[j] += db

        xv = h_ref[...]
        r = lax.rsqrt(jnp.mean(xv * xv, axis=-1, keepdims=True) + EPS)
        xhat = xv * r
        dgain = jnp.sum(total * xhat, axis=0, keepdims=True)

        @pl.when(i == 0)
        def _():
            dgain_ref[...] = dgain

        @pl.when(i > 0)
        def _():
            dgain_ref[...] += dgain

        dxhat = total * gain_ref[...]
        dx = dres_ref[...] + r * (dxhat - xhat * jnp.mean(dxhat * xhat, axis=-1, keepdims=True))
        dx_ref[...] = dx
        dxb_ref[...] = dx.astype(BF16)

    def tile(part):
        return pl.BlockSpec((None, nj, tm, fb), lambda i: (part, 0, i, 0))

    def after(part):
        return pl.BlockSpec((None, nj, HALO, fb), lambda i: (part, 0, jnp.minimum((i + 1) * per, nblk - 1), 0))

    row = pl.BlockSpec((tm, d), lambda i: (i, 0))
    own = pl.BlockSpec((nj, tm, fb), lambda i: (0, i, 0))
    nxt = pl.BlockSpec((nj, HALO, fb), lambda i: (0, jnp.minimum((i + 1) * per, nblk - 1), 0))
    return pl.pallas_call(
        body, name=name, grid=(ni,),
        in_specs=[tile(0), own, nxt, tile(1), after(1), own, nxt,
                  pl.BlockSpec((nj, CONV_WIDTH, fb), lambda i: (0, 0, 0)),
                  pl.BlockSpec((2 * nj, fb, d), lambda i: (0, 0, 0)), row, pl.BlockSpec((1, d), lambda i: (0, 0)), row],
        out_specs=[pl.BlockSpec((2, nj, tm, fb), lambda i: (0, 0, i, 0)),
                   pl.BlockSpec((nj, CONV_WIDTH, fb), lambda i: (0, 0, 0)), pl.BlockSpec((nj, 1, fb), lambda i: (0, 0, 0)),
                   row, row, pl.BlockSpec((1, d), lambda i: (0, 0))],
        out_shape=[jax.ShapeDtypeStruct((2, nj, s, fb), BF16), jax.ShapeDtypeStruct((nj, CONV_WIDTH, fb), F32),
                   jax.ShapeDtypeStruct((nj, 1, fb), F32), jax.ShapeDtypeStruct((s, d), F32), jax.ShapeDtypeStruct((s, d), BF16),
                   jax.ShapeDtypeStruct((1, d), F32)],
        compiler_params=_params(("arbitrary",)),
    )(u, conv, conv, u, u, dh, dh, conv_w, w_up, h, gain, dres)


ATT_TILE = 512


def _stack_heads(ref, rows, first_head, count):
    hd = ATT_HEAD_DIM
    return jnp.concatenate([ref[rows, (first_head + j) * hd:(first_head + j + 1) * hd] for j in range(count)], axis=0)


def _unstack_heads(stacked, ref, rows, first_head, count):
    hd = ATT_HEAD_DIM
    for pair in range(count // 2):
        both = [stacked[(2 * pair + j) * WINDOW:(2 * pair + j + 1) * WINDOW, :] for j in range(2)]
        ref[rows, (first_head + 2 * pair) * hd:(first_head + 2 * pair + 2) * hd] = jnp.concatenate(both, axis=1).astype(ref.dtype)


def _attn_bias(first_head, count, n_heads, first):
    lanes = count * WINDOW
    ik = lax.broadcasted_iota(jnp.int32, (2 * WINDOW, lanes), 0)
    iq = lax.broadcasted_iota(jnp.int32, (2 * WINDOW, lanes), 1) % WINDOW
    dist = iq + WINDOW - ik
    valid = (dist >= 0) & (dist < WINDOW) & (ik >= (WINDOW if first else 0))
    slope = jnp.concatenate([jnp.zeros((1, WINDOW), F32) + 2.0 ** (-8.0 * (first_head + j + 1) / n_heads) for j in range(count)], axis=1)
    return jnp.where(valid, -slope * dist.astype(F32), NEG)


def _fill_attn_bias(bias_ref, group, n_heads):
    @pl.when(pl.program_id(0) == 0)
    def _():
        for g in range(ATT_KV_HEADS):
            bias_ref[0, g] = _attn_bias(g * group, group, n_heads, False)
            bias_ref[1, g] = _attn_bias(g * group, group, n_heads, True)


def _attn_probs_t(kb_scaled, qs, sink_ref, first_head, count, bias):
    sink = jnp.concatenate([jnp.zeros((1, WINDOW), F32) + sink_ref[0, first_head + j] for j in range(count)], axis=1)
    sc = _dot(kb_scaled, qs, NT) + bias
    m = jnp.maximum(jnp.max(sc, axis=0, keepdims=True), sink)
    e = jnp.exp(sc - m)
    es = jnp.exp(sink - m)
    inv = 1.0 / (jnp.sum(e, axis=0, keepdims=True) + es)
    return e * inv, es * inv


ATT_SCALE = ATT_HEAD_DIM ** -0.5


def _attn_specs(s, d, kvd, tq):
    per = tq // WINDOW
    return [pl.BlockSpec((tq, d), lambda i: (i, 0)), pl.BlockSpec((tq, kvd), lambda i: (i, 0)),
            pl.BlockSpec((WINDOW, kvd), lambda i: (jnp.maximum(i * per - 1, 0), 0))]


def _attn_fwd(q, kv, sinks):
    s, d = q.shape
    kvd = kv.shape[1]
    half = kvd // 2
    hd = ATT_HEAD_DIM
    nq = d // hd
    group = nq // ATT_KV_HEADS
    tq = min(s, ATT_TILE)
    per = tq // WINDOW

    def body(q_ref, kvc_ref, kvp_ref, sink_ref, o_ref, band, bias_ref):
        i = pl.program_id(0)
        _fill_attn_bias(bias_ref, group, nq)
        band[0:WINDOW, :] = kvp_ref[...]
        band[WINDOW:, :] = kvc_ref[...]

        def block(b, carry):
            rows = pl.ds(pl.multiple_of(b * WINDOW, WINDOW), WINDOW)
            keys = pl.ds(pl.multiple_of(b * WINDOW, WINDOW), 2 * WINDOW)
            first = (i * per + b) == 0
            for g in range(ATT_KV_HEADS):
                bias = jnp.where(first, bias_ref[1, g], bias_ref[0, g])
                p, _ = _attn_probs_t(band[keys, g * hd:(g + 1) * hd] * ATT_SCALE, _stack_heads(q_ref, rows, g * group, group), sink_ref,
                                     g * group, group, bias)
                out_t = _dot(band[keys, half + g * hd:half + (g + 1) * hd], p, TN)
                _unstack_heads(out_t.T, o_ref, rows, g * group, group)
            return carry

        lax.fori_loop(0, per, block, 0)

    return pl.pallas_call(
        body, name="attn_fwd", grid=(s // tq,),
        in_specs=_attn_specs(s, d, kvd, tq) + [pl.BlockSpec(memory_space=pltpu.SMEM)],
        out_specs=pl.BlockSpec((tq, d), lambda i: (i, 0)), out_shape=jax.ShapeDtypeStruct((s, d), BF16),
        scratch_shapes=[pltpu.VMEM((tq + WINDOW, kvd), BF16), pltpu.VMEM((2, ATT_KV_HEADS, 2 * WINDOW, group * WINDOW), F32)],
        compiler_params=_params(("arbitrary",)),
    )(q, kv, kv, sinks)


def _attn_bwd(q, kv, o, do, sinks):
    s, d = q.shape
    kvd = kv.shape[1]
    half = kvd // 2
    hd = ATT_HEAD_DIM
    nq = d // hd
    group = nq // ATT_KV_HEADS
    tq = min(s, ATT_TILE)
    per = tq // WINDOW
    nt = s // tq

    def body(q_ref, kvc_ref, kvp_ref, o_ref, do_ref, sink_ref, dq_ref, dkvc_ref, dkvp_ref, ds_ref, band, dband, bias_ref):
        i = pl.program_id(0)
        _fill_attn_bias(bias_ref, group, nq)
        band[0:WINDOW, :] = kvp_ref[...]
        band[WINDOW:, :] = kvc_ref[...]
        dband[...] = jnp.zeros_like(dband)
        ds_ref[...] = jnp.zeros_like(ds_ref)

        def block(b, carry):
            rows = pl.ds(pl.multiple_of(b * WINDOW, WINDOW), WINDOW)
            keys = pl.ds(pl.multiple_of(b * WINDOW, WINDOW), 2 * WINDOW)
            first = (i * per + b) == 0
            dks, dvs = [], []
            for g in range(ATT_KV_HEADS):
                kb = band[keys, g * hd:(g + 1) * hd] * ATT_SCALE
                vb = band[keys, half + g * hd:half + (g + 1) * hd]
                qs = _stack_heads(q_ref, rows, g * group, group)
                dos = _stack_heads(do_ref, rows, g * group, group)
                p, ps = _attn_probs_t(kb, qs, sink_ref, g * group, group, jnp.where(first, bias_ref[1, g], bias_ref[0, g]))
                prod = dos.astype(F32) * _stack_heads(o_ref, rows, g * group, group).astype(F32)
                dsum = lax.dot_general(jnp.ones((8, hd), F32), prod, NT, precision=lax.Precision.HIGHEST,
                                       preferred_element_type=F32)[0:1, :]
                dsc = p * (_dot(vb, dos, NT) - dsum)
                dvs.append(_dot(p, dos))
                dks.append(_dot(dsc, qs * ATT_SCALE))
                _unstack_heads(_dot(kb, dsc, TN).T, dq_ref, rows, g * group, group)
                gone = ps * dsum
                for j in range(group):
                    ds_ref[g * group + j:g * group + j + 1, :] += jnp.zeros((1, 128), F32) - jnp.sum(gone[:, j * WINDOW:(j + 1) * WINDOW])
            dband[keys, 0:half] += jnp.concatenate(dks, axis=1)
            dband[keys, half:] += jnp.concatenate(dvs, axis=1)
            return carry

        lax.fori_loop(0, per, block, 0)
        dkvp_ref[...] = dband[0:WINDOW, :]
        dkvc_ref[...] = dband[WINDOW:, :]

    big = pl.BlockSpec((tq, d), lambda i: (i, 0))
    return pl.pallas_call(
        body, name="attn_bwd", grid=(nt,),
        in_specs=_attn_specs(s, d, kvd, tq) + [big, big, pl.BlockSpec(memory_space=pltpu.SMEM)],
        out_specs=[big, pl.BlockSpec((tq, kvd), lambda i: (i, 0)), pl.BlockSpec((None, WINDOW, kvd), lambda i: (i, 0, 0)),
                   pl.BlockSpec((None, nq, 128), lambda i: (i, 0, 0))],
        out_shape=[jax.ShapeDtypeStruct((s, d), BF16), jax.ShapeDtypeStruct((s, kvd), F32), jax.ShapeDtypeStruct((nt, WINDOW, kvd), F32),
                   jax.ShapeDtypeStruct((nt, nq, 128), F32)],
        scratch_shapes=[pltpu.VMEM((tq + WINDOW, kvd), BF16), pltpu.VMEM((tq + WINDOW, kvd), F32),
                        pltpu.VMEM((2, ATT_KV_HEADS, 2 * WINDOW, group * WINDOW), F32)],
        compiler_params=_params(("arbitrary",)),
    )(q, kv, kv, o, do, sinks)


HBM_SPEC = pl.BlockSpec(memory_space=pltpu.HBM)
VMEM_SPEC = pl.BlockSpec(memory_space=pltpu.VMEM)


def _place():
    return lax.axis_index("x"), lax.axis_index("y"), lax.axis_index("c")


def _flip(pos, r):
    return tuple(1 - p if (r >> (2 - a)) & 1 else p for a, p in enumerate(pos))


def _index(pos):
    return 4 * pos[0] + 2 * pos[1] + pos[2]


def _all_gather(name, shards, spec):
    n = len(shards)

    def body(*refs):
        x_refs, o_refs = refs[:n], refs[n:2 * n]
        send_sems, recv_sems, local_sems = refs[2 * n:]
        me = _place()
        sibling = _flip(me, 1)
        far = [_flip(me, r) for r in (4, 2, 6)]

        def copy(t, sem, block, to, src=None):
            rows = o_refs[t].at[_index(block)]
            return pltpu.make_async_remote_copy(
                src_ref=rows if src is None else src, dst_ref=rows, send_sem=send_sems.at[t, sem], recv_sem=recv_sems.at[t, sem],
                device_id=to, device_id_type=MESH)

        own = [pltpu.make_async_copy(x_refs[t], o_refs[t].at[_index(me)], local_sems.at[t]) for t in range(n)]
        for cp in own:
            cp.start()
        first = []
        for t in range(n):
            first.append(copy(t, 0, me, sibling, src=x_refs[t]))
            first += [copy(t, 1 + j, me, peer, src=x_refs[t]) for j, peer in enumerate(far)]
        for cp in first:
            cp.start()
        passed = []
        for j, peer in enumerate(far):
            for t in range(n):
                copy(t, 1 + j, peer, me).wait_recv()
                cp = copy(t, 4 + j, peer, sibling)
                cp.start()
                passed.append(cp)
        for t in range(n):
            copy(t, 0, sibling, me).wait_recv()
            for j, peer in enumerate(far):
                copy(t, 4 + j, _flip(peer, 1), me).wait_recv()
        for cp in first + passed:
            cp.wait_send()
        for cp in own:
            cp.wait()

    return pl.pallas_call(
        body, name=name, in_specs=[spec] * n, out_specs=[spec] * n,
        out_shape=[jax.ShapeDtypeStruct((N_DEV,) + sh.shape, sh.dtype) for sh in shards],
        scratch_shapes=[pltpu.SemaphoreType.DMA((n, 7)), pltpu.SemaphoreType.DMA((n, 7)), pltpu.SemaphoreType.DMA((n,))],
    )(*shards)


SEM_SPEC = pl.BlockSpec(memory_space=pltpu.SEMAPHORE)
ANY_SPEC = pl.BlockSpec(memory_space=pl.ANY)


def _landing(own, mine):
    return lax.dynamic_update_slice(lax.empty((N_DEV,) + own.shape, own.dtype), own[None], (mine,) + (0,) * own.ndim)


def _pinned(a, token):
    return a + token[0:1, 0:1].astype(a.dtype)


def _peer_copies(src_refs, land_refs, send_sems, recv_sems, scatter, arrivals):
    me = _place()
    mine = _index(me)
    copies = []
    for t, (src, land) in enumerate(zip(src_refs, land_refs)):
        for r in range(1, N_DEV):
            peer = _flip(me, r)
            theirs = _index(peer)
            sem = t * (N_DEV - 1) + r - 1
            copies.append(pltpu.make_async_remote_copy(
                src_ref=src.at[theirs] if scatter else src, dst_ref=land.at[theirs if arrivals else mine],
                send_sem=send_sems.at[sem], recv_sem=recv_sems.at[sem], device_id=peer, device_id_type=MESH))
    return copies


def _send_start(name, sources, lands, scatter, after=None):
    n = len(sources)
    extra = 0 if after is None else 1

    def body(*refs):
        outs = refs[2 * n + extra:]
        for out in _peer_copies(refs[:n], refs[n:2 * n], outs[0], outs[1], scatter, False):
            out.start()
        outs[-1][...] = jnp.zeros_like(outs[-1])

    outs = pl.pallas_call(
        body, name=name, in_specs=[HBM_SPEC] * (2 * n) + [ANY_SPEC] * extra,
        out_specs=[SEM_SPEC, SEM_SPEC] + [HBM_SPEC] * (2 * n) + [VMEM_SPEC],
        out_shape=[pltpu.SemaphoreType.DMA((n * (N_DEV - 1),)), pltpu.SemaphoreType.DMA((n * (N_DEV - 1),))]
        + [pltpu.HBM(a.shape, a.dtype) for a in list(sources) + list(lands)] + [jax.ShapeDtypeStruct((8, 128), F32)],
        input_output_aliases={i: 2 + i for i in range(2 * n)},
        compiler_params=pltpu.CompilerParams(has_side_effects=pltpu.SideEffectType.DATAFLOW_SIDE_EFFECTING),
    )(*[pltpu.with_memory_space_constraint(a, pltpu.HBM) for a in list(sources) + list(lands)], *([] if after is None else [after]))
    return outs[0], outs[1], outs[2:2 + n], outs[2 + n:2 + 2 * n], outs[-1]


def _send_wait(name, started, after, scatter):
    send_sems, recv_sems, sources, lands, _ = started
    n = len(sources)

    def body(*refs):
        for out in _peer_copies(refs[:n], refs[n:2 * n], refs[2 * n], refs[2 * n + 1], scatter, False):
            out.wait_send()
        for arrival in _peer_copies(refs[:n], refs[n:2 * n], refs[2 * n], refs[2 * n + 1], scatter, True):
            arrival.wait_recv()

    outs = pl.pallas_call(
        body, name=name, in_specs=[HBM_SPEC] * (2 * n) + [SEM_SPEC, SEM_SPEC, ANY_SPEC], out_specs=[HBM_SPEC] * (2 * n),
        out_shape=[pltpu.HBM(a.shape, a.dtype) for a in list(sources) + list(lands)],
        input_output_aliases={i: i for i in range(2 * n)},
        compiler_params=pltpu.CompilerParams(has_side_effects=pltpu.SideEffectType.DATAFLOW_SIDE_EFFECTING),
    )(*sources, *lands, send_sems, recv_sems, after)
    return outs[n:]


def _pack_rows(parts):
    offsets, row = [], 0
    for part in parts:
        offsets.append(row)
        row += part.shape[0]
    return offsets, -(-row // 8) * 8, -(-max(part.shape[1] for part in parts) // 128) * 128


def _pack(name, parts):
    offsets, rows, width = _pack_rows(parts)

    def body(*refs):
        o_ref = refs[-1]
        o_ref[...] = jnp.zeros_like(o_ref)
        for off, ref in zip(offsets, refs[:-1]):
            o_ref[off:off + ref.shape[0], 0:ref.shape[1]] = ref[...]

    return pl.pallas_call(body, name=name, in_specs=[VMEM_SPEC] * len(parts), out_specs=VMEM_SPEC,
                          out_shape=jax.ShapeDtypeStruct((rows, width), F32))(*parts)


def _adamw_math(w, g, m, v):
    m = ADAM_B1 * m + (1.0 - ADAM_B1) * g
    v = ADAM_B2 * v + (1.0 - ADAM_B2) * (g * g)
    m_hat = m * (1.0 / (1.0 - ADAM_B1 ** ADAM_STEP))
    denom = jnp.sqrt(v * (1.0 / (1.0 - ADAM_B2 ** ADAM_STEP))) + ADAM_EPS
    inv = pl.reciprocal(denom, approx=True)
    inv = inv * (2.0 - denom * inv)
    return -ADAM_LR * (m_hat * inv + ADAM_WD * w), m, v


def _adamw_step(w_ref, m_ref, v_ref, p_ref, g_ref, d_ref, nm_ref, nv_ref):
    g = p_ref[0].astype(F32)
    for dev in range(1, N_DEV):
        g = g + p_ref[dev].astype(F32)
    g_ref[...] = g
    d_ref[...], nm_ref[...], nv_ref[...] = _adamw_math(w_ref[...], g, m_ref[...], v_ref[...])


def _adamw_rows(rows):
    return max(t for t in range(8, min(rows, 256) + 1, 8) if rows % t == 0)


def _adamw_shard(name, w, m, v, partials):
    rows, cols = w.shape
    tr = _adamw_rows(rows)
    blk = pl.BlockSpec((tr, cols), lambda i: (i, 0))
    return pl.pallas_call(
        _adamw_step_fn(), name=name, grid=(rows // tr,), in_specs=[blk, blk, blk, pl.BlockSpec((N_DEV, tr, cols), lambda i: (0, i, 0))],
        out_specs=[blk] * 4, out_shape=[jax.ShapeDtypeStruct((rows, cols), F32)] * 4, compiler_params=_params(("parallel",)),
    )(w, m, v, partials)


def _adamw_step_fn():
    return functools.partial(_adamw_step)


def _adamw_layers(name, w, m, v, partials):
    layers, rows, cols = w.shape
    tr = _adamw_rows(rows)
    last = rows // tr - 1

    def body(w_ref, m_ref, v_ref, *rest):
        for layer in range(layers):
            @pl.when(pl.program_id(0) == layer)
            def _():
                _adamw_step(w_ref, m_ref, v_ref, rest[layer], *rest[layers:])

    blk = pl.BlockSpec((None, tr, cols), lambda l, i: (l, i, 0))
    part = lambda layer: pl.BlockSpec((N_DEV, tr, cols), lambda l, i: (0, jnp.where(l == layer, i, jnp.where(l < layer, 0, last)), 0))
    return pl.pallas_call(
        body, name=name, grid=(layers, rows // tr), in_specs=[blk, blk, blk] + [part(layer) for layer in range(layers)],
        out_specs=[blk] * 4, out_shape=[jax.ShapeDtypeStruct(w.shape, F32)] * 4, compiler_params=_params(("arbitrary", "arbitrary")),
    )(w, m, v, *partials)


def _adamw_small(gathered, places, entries):
    n = len(entries)
    np_ = len(gathered)

    def body(*refs):
        pack_refs = refs[:np_]
        refs = refs[np_ - 1:]
        w_refs, m_refs, v_refs = refs[1:1 + n], refs[1 + n:1 + 2 * n], refs[1 + 2 * n:1 + 3 * n]
        outs = refs[1 + 3 * n:]
        totals = []
        for pack_ref in pack_refs:
            acc = pack_ref[0]
            for dev in range(1, N_DEV):
                acc = acc + pack_ref[dev]
            totals.append(acc)
        mine = _index(_place())
        for e in range(n):
            rows, cols = w_refs[e].shape
            total, off = totals[places[e][0]], places[e][1]
            if entries[e][3]:
                g = jnp.zeros((rows, cols), F32)
                for dev in range(N_DEV):
                    g = g + jnp.where(mine == dev, total[off + dev * rows:off + (dev + 1) * rows, 0:cols], 0.0)
            else:
                g = total[off:off + rows, 0:cols]
            outs[4 * e][...] = g
            outs[4 * e + 1][...], outs[4 * e + 2][...], outs[4 * e + 3][...] = _adamw_math(w_refs[e][...], g, m_refs[e][...], v_refs[e][...])
        outs[4 * n][...] = totals[places[n][0]][places[n][1]:places[n][1] + 1, 0:128]

    shapes = []
    for w, _, _, _ in entries:
        shapes += [jax.ShapeDtypeStruct(w.shape, F32)] * 4
    shapes.append(jax.ShapeDtypeStruct((1, 128), F32))
    return pl.pallas_call(
        body, name="adamw_small", in_specs=[VMEM_SPEC] * (np_ + 3 * n), out_specs=[VMEM_SPEC] * len(shapes), out_shape=shapes,
        compiler_params=pltpu.CompilerParams(vmem_limit_bytes=VMEM_LIMIT),
    )(*gathered, *[e[0] for e in entries], *[e[1] for e in entries], *[e[2] for e in entries])


def _ffn_forward(tag, h, gain, w_up, late):
    s, d = h.shape
    fb = w_up.shape[1]
    tm = _row_tile(s, 2 * MM_ROWS)
    a, = _rmsnorm_cast(f"ffn_norm_{tag}", h, [gain])
    u = _matmul(
        f"ffn_up_{tag}", a, w_up, dims=NT, grid=(s // tm, N_DEV, 1),
        a_spec=pl.BlockSpec((tm, d), lambda i, j, k: (i, 0)),
        b_spec=pl.BlockSpec((None, fb, d), lambda i, j, k: (j, 0, 0)),
        o_spec=pl.BlockSpec((None, None, tm, fb), lambda i, j, k: (j // 4, j % 4, i, 0)),
        out_shape=jax.ShapeDtypeStruct((2, 4, s, fb), BF16))
    w_down, conv_w, conv_b = late(u)
    hidden, conv, out = _ffn_hidden_down(f"ffn_hidden_down_{tag}", u, conv_w, conv_b, w_down, h)
    return out, (a, u, hidden, conv)


def _ffn_backward(tag, h, gain, w_up, w_down, conv_w, conv_b, saved, dout):
    a, u, hidden, conv = saved
    dout, dout_bf = dout
    s, d = h.shape
    fb = w_up.shape[1]
    tm = _row_tile(s, MM_ROWS)
    dhidden = _matmul(
        f"ffn_down_bwd_{tag}", dout_bf, w_down, dims=NT, grid=(s // tm, 4, 1),
        a_spec=pl.BlockSpec((tm, d), lambda i, j, k: (i, 0)),
        b_spec=pl.BlockSpec((None, fb, d), lambda i, j, k: (j, 0, 0)),
        o_spec=pl.BlockSpec((None, tm, fb), lambda i, j, k: (j, i, 0)),
        out_shape=jax.ShapeDtypeStruct((4, s, fb), BF16))
    dw_down = _matmul(
        f"ffn_down_grad_{tag}", hidden, dout_bf, dims=TN, grid=(4, 1, 1),
        a_spec=pl.BlockSpec((None, s, fb), lambda i, j, k: (i, 0, 0)),
        b_spec=pl.BlockSpec((s, d), lambda i, j, k: (0, 0)),
        o_spec=pl.BlockSpec((None, fb, d), lambda i, j, k: (i, 0, 0)),
        out_shape=jax.ShapeDtypeStruct((4, fb, d), BF16))
    du, dconv_w, dconv_b, dh, dh_bf, dgain = _ffn_hidden_up_bwd(f"ffn_hidden_up_bwd_{tag}", u, conv, dhidden, conv_w, w_up, h, gain, dout)
    dw_up = _matmul(
        f"ffn_up_grad_{tag}", du, a, dims=TN, grid=(N_DEV, 1, 1),
        a_spec=pl.BlockSpec((None, None, s, fb), lambda i, j, k: (i // 4, i % 4, 0, 0)),
        b_spec=pl.BlockSpec((s, d), lambda i, j, k: (0, 0)),
        o_spec=pl.BlockSpec((None, fb, d), lambda i, j, k: (i, 0, 0)),
        out_shape=jax.ShapeDtypeStruct((N_DEV, fb, d), BF16))
    return (dh, dh_bf), dgain, dw_up, dw_down, dconv_w, dconv_b


def kernel(x, hg_norm, hg_w_in, hg_lb_logits, hg_out_norm, hg_w_out, kv_norm, w_kv, attn_norm, attn_w_q, attn_sinks, attn_w_o, ffn_norm, ffn_w_up, ffn_conv_w, ffn_conv_b, ffn_w_down, final_norm, loss_target, m_hg_norm, m_hg_w_in, m_hg_lb_logits, m_hg_out_norm, m_hg_w_out, m_kv_norm, m_w_kv, m_attn_norm, m_attn_w_q, m_attn_sinks, m_attn_w_o, m_ffn_norm, m_ffn_w_up, m_ffn_conv_w, m_ffn_conv_b, m_ffn_w_down, m_final_norm, v_hg_norm, v_hg_w_in, v_hg_lb_logits, v_hg_out_norm, v_hg_w_out, v_kv_norm, v_w_kv, v_attn_norm, v_attn_w_q, v_attn_sinks, v_attn_w_o, v_ffn_norm, v_ffn_w_up, v_ffn_conv_w, v_ffn_conv_b, v_ffn_w_down, v_final_norm):
    _, s, d = x.shape
    x0, target = x[0], loss_target[0]
    half = hg_w_in.shape[2]
    fs = ffn_conv_w.shape[2]
    fb = 2 * fs
    kvd = w_kv.shape[1]
    nq = d // ATT_HEAD_DIM
    tm = _row_tile(s, MM_ROWS)

    mine = _index(_place())
    gather = lambda tag, shards, after: _send_start("gather_start_" + tag, shards, [_landing(a, mine) for a in shards], False, after)
    w_in, g_hgn, g_lbl, w_out = _all_gather("gather_hg", [hg_w_in[0].astype(BF16), hg_norm, hg_lb_logits, hg_w_out[0].astype(BF16)], HBM_SPEC)
    w_out = w_out.reshape(d, d)
    up_t = lambda a: jnp.swapaxes(a, -1, -2)
    coming_up0 = gather("ffn_up0", [up_t(ffn_w_up[0]).astype(BF16)], g_hgn)
    hgn = _pinned(g_hgn.reshape(1, d), coming_up0[4])
    lbl = g_lbl.transpose(1, 0, 2).reshape(2, d)
    conv_b = [ffn_conv_b[layer].reshape(4, 1, fb) for layer in range(2)]
    gains = [ffn_norm[0:1], ffn_norm[1:2]]
    kvn, fin = kv_norm.reshape(1, d), final_norm.reshape(1, d)

    a0, = _rmsnorm_cast("hg_norm", x0, [hgn])
    t2 = _row_tile(s, 2 * MM_ROWS)
    p = _matmul(
        "hg_in", a0, w_in, dims=NN, grid=(s // t2, N_DEV, 1),
        a_spec=pl.BlockSpec((t2, d), lambda i, j, k: (i, 0)),
        b_spec=pl.BlockSpec((None, d, half), lambda i, j, k: (j, 0, 0)),
        o_spec=pl.BlockSpec((None, t2, half), lambda i, j, k: (j // 2, i, j % 2)),
        out_shape=jax.ShapeDtypeStruct((4, s, d), BF16))
    o, og, states, gsum = _hgrn2_fwd(p, lbl, hg_out_norm)
    coming_dn0 = gather("ffn_down0", [ffn_conv_w, ffn_w_down[0].astype(BF16)], o)
    x1 = _mm_rows("hg_out", og, _pinned(w_out, coming_dn0[4]), out_dtype=F32, add=x0)
    w_up0, = _send_wait("gather_wait_ffn_up0", coming_up0, x1, False)
    coming_attn = gather("attn", [w_kv.astype(BF16), attn_w_q[0].astype(BF16), attn_w_o[0].astype(BF16)], w_up0)
    gains[0] = _pinned(gains[0], coming_attn[4])
    w_up, w_dn, conv_w, coming = [w_up0, None], [None, None], [], {}

    def late0(u):
        g_cw, w_dn0 = _send_wait("gather_wait_ffn_down0", coming_dn0, u, False)
        w_dn[0] = w_dn0.reshape(4, fb, d)
        conv_w.extend(g_cw[:, layer].reshape(4, 2, CONV_WIDTH, fs).transpose(0, 2, 1, 3).reshape(4, CONV_WIDTH, fb) for layer in range(2))
        coming["up1"] = gather("ffn_up1", [up_t(ffn_w_up[1]).astype(BF16)], w_dn0)
        return w_dn[0], conv_w[0], _pinned(conv_b[0], coming["up1"][4])

    x2, saved0 = _ffn_forward("0", x1, gains[0], w_up[0], late0)
    w_kvg, w_q, w_o = _send_wait("gather_wait_attn", coming_attn, x2, False)
    w_kvg, w_q, w_o = w_kvg.reshape(d, kvd), w_q.reshape(d, d), w_o.reshape(d, d)
    akv, a2 = _rmsnorm_cast("attn_norms", x2, [kvn, attn_norm])
    kv = _mm_rows("kv_proj", akv, w_kvg, out_dtype=BF16)
    q = _mm_rows("q_proj", a2, w_q, out_dtype=BF16)
    coming_dn1 = gather("ffn_down1", [ffn_w_down[1].astype(BF16)], q)
    att = _attn_fwd(q, kv, _pinned(attn_sinks, coming_dn1[4]))
    x3 = _mm_rows("attn_out", att, w_o, out_dtype=F32, add=x2)
    w_up[1], = _send_wait("gather_wait_ffn_up1", coming["up1"], x3, False)

    def late1(u):
        w_dn[1] = _send_wait("gather_wait_ffn_down1", coming_dn1, u, False)[0].reshape(4, fb, d)
        return w_dn[1], conv_w[1], conv_b[1]

    x4, saved1 = _ffn_forward("1", x3, gains[1], w_up[1], late1)
    dx4, dx4_bf, d_fin, loss_part = _loss_head(x4, fin, target)

    dx3, d_fn1, dw_up1, dw_dn1, dcw1, dcb1 = _ffn_backward("1", x3, gains[1], w_up[1], w_dn[1], conv_w[1], conv_b[1], saved1, (dx4, dx4_bf))
    rows = d // N_DEV
    scatter = lambda tag, stacks: _send_start("scatter_start_" + tag, stacks, [_landing(lax.dynamic_index_in_dim(a, mine, keepdims=False), mine) for a in stacks], True)
    going_ffn1 = scatter("ffn1", [dw_up1, dw_dn1.reshape(N_DEV, fs, d)])
    datt = _mm_rows_nt("attn_out_bwd", dx3[1], w_o, out_dtype=BF16)
    dw_o = _mm_tn("attn_out_grad", att, dx3[1])
    dq, dkv_own, dkv_before, dsink = _attn_bwd(q, kv, att, datt, _pinned(attn_sinks, going_ffn1[4]))
    tiles = dkv_before.shape[0]
    dkv = dkv_own.reshape(tiles, s // tiles, kvd)
    dkv = jnp.concatenate([dkv[:, :-WINDOW], dkv[:, -WINDOW:] + jnp.pad(dkv_before[1:], ((0, 1), (0, 0), (0, 0)))], axis=1).reshape(s, kvd)
    dw_q = _mm_tn("q_proj_grad", a2, dq)
    dw_kv = _mm_tn("kv_proj_grad", akv, dkv)
    going_attn = scatter("attn", [dw_kv.reshape(N_DEV, rows, kvd), dw_q.reshape(N_DEV, rows, d), dw_o.reshape(N_DEV, rows, d)])
    whole = lambda a_ref, b_ref: [(a_ref[...], b_ref[...])]
    rows_of = lambda width: (lambda tile: pl.BlockSpec((tile, width), lambda i: (i, 0)))
    dx2, (d_kvn, d_attn) = _proj_norm_bwd("attn_in_bwd", x2, dx3[0], [
        (dkv, rows_of(kvd), w_kvg, pl.BlockSpec((d, kvd), lambda i: (0, 0)), whole, _pinned(kvn, going_attn[4])),
        (dq, rows_of(d), w_q, pl.BlockSpec((d, d), lambda i: (0, 0)), whole, attn_norm)])
    dx1, d_fn0, dw_up0, dw_dn0, dcw0, dcb0 = _ffn_backward("0", x1, gains[0], w_up[0], w_dn[0], conv_w[0], conv_b[0], saved0, dx2)
    dw_out = _mm_tn("hg_out_grad", og, dx1[1])
    going_ffn0 = scatter("ffn0", [dw_up0, dw_dn0.reshape(N_DEV, fs, d), dw_out.reshape(N_DEV, rows, d)])
    dog = _mm_rows_nt("hg_out_bwd", dx1[1], w_out, out_dtype=F32)
    dp, d_lbl, d_ogain = _hgrn2_bwd(p, lbl, _pinned(hg_out_norm, going_ffn0[4]), o, dog, states, gsum)
    dw_in = _matmul(
        "hg_in_grad", a0, dp, dims=TN, grid=(1, N_DEV, 1),
        a_spec=pl.BlockSpec((s, d), lambda i, j, k: (0, 0)),
        b_spec=pl.BlockSpec((None, s, half), lambda i, j, k: (j // 2, 0, j % 2)),
        o_spec=pl.BlockSpec((None, d, half), lambda i, j, k: (j, 0, 0)),
        out_shape=jax.ShapeDtypeStruct((N_DEV, d, half), BF16))
    going_hg = scatter("hg", [dw_in])
    (dx0, _), (d_hgn,) = _proj_norm_bwd("hg_in_bwd", x0, dx1[0], [
        (dp, lambda tile: pl.BlockSpec((4, tile, d), lambda i: (0, i, 0)), w_in, pl.BlockSpec((N_DEV, d, half), lambda i: (0, 0, 0)),
         lambda g_ref, w_ref: [(g_ref[k // 2, :, (k % 2) * half:(k % 2 + 1) * half], w_ref[k]) for k in range(N_DEV)],
         _pinned(hgn, going_hg[4]))])

    as_blocks = lambda a, r: a.reshape(r, N_DEV, -1).transpose(1, 0, 2).reshape(N_DEV * r, -1)
    d_cw = jnp.concatenate([g.transpose(1, 0, 2).reshape(CONV_WIDTH, 4 * fb) for g in (dcw0, dcw1)], axis=0)
    parts = [d_fin, jnp.concatenate([d_fn0, d_fn1], axis=0), jnp.concatenate([dcb0.reshape(1, 4 * fb), dcb1.reshape(1, 4 * fb)], axis=0),
             as_blocks(d_cw, 2 * CONV_WIDTH), d_attn, jnp.sum(dsink[:, :, 0], axis=0).reshape(1, nq), d_kvn, d_ogain,
             as_blocks(d_hgn, 1), as_blocks(d_lbl, 2), loss_part]
    wide = [2]
    packs = [[parts[i] for i in wide], [part for i, part in enumerate(parts) if i not in wide]]
    places = [None] * len(parts)
    for which, members in enumerate([wide, [i for i in range(len(parts)) if i not in wide]]):
        for i, off in zip(members, _pack_rows(packs[which])[0]):
            places[i] = (which, off)
    packed = [_pack("pack_wide_grads", packs[0]), _pack("pack_narrow_grads", packs[1])]
    going_small = _send_start("small_grads_start", packed, [_landing(a, mine) for a in packed], False)

    arrive = lambda tag, going, after: _send_wait("scatter_wait_" + tag, going, after, True)
    (l_up1, l_dn1), (l_kv, l_q, l_o), (l_up0, l_dn0, l_out) = (
        arrive("ffn1", going_ffn1, going_small[4]), arrive("attn", going_attn, going_small[4]), arrive("ffn0", going_ffn0, going_small[4]))
    big = {}
    for tag, w, m, v, part in [
            ("w_kv", w_kv, m_w_kv, v_w_kv, l_kv), ("attn_w_q", attn_w_q[0], m_attn_w_q[0], v_attn_w_q[0], l_q),
            ("attn_w_o", attn_w_o[0], m_attn_w_o[0], v_attn_w_o[0], l_o)]:
        big[tag] = _adamw_shard("adamw_" + tag, w, m, v, part)
    big["ffn_w_up"] = [up_t(a) for a in _adamw_layers("adamw_ffn_w_up", up_t(ffn_w_up), up_t(m_ffn_w_up), up_t(v_ffn_w_up), (l_up0, l_up1))]
    big["ffn_w_down"] = _adamw_layers("adamw_ffn_w_down", ffn_w_down, m_ffn_w_down, v_ffn_w_down, (l_dn0, l_dn1))
    lead = lambda tag: [a[None] for a in big[tag]]

    both_done = big["ffn_w_up"][0][0, 0:1, 0:1] + big["ffn_w_down"][0][0, 0:1, 0:1]
    gathered = _send_wait("small_grads_wait", going_small, both_done, False)
    two = lambda a: a.reshape(-1, a.shape[-1])
    small = [(fin, m_final_norm.reshape(1, d), v_final_norm.reshape(1, d), False), (ffn_norm, m_ffn_norm, v_ffn_norm, False),
             (ffn_conv_b, m_ffn_conv_b, v_ffn_conv_b, False), (two(ffn_conv_w), two(m_ffn_conv_w), two(v_ffn_conv_w), True),
             (attn_norm, m_attn_norm, v_attn_norm, False), (attn_sinks, m_attn_sinks, v_attn_sinks, False),
             (kvn, m_kv_norm.reshape(1, d), v_kv_norm.reshape(1, d), False), (hg_out_norm, m_hg_out_norm, v_hg_out_norm, False),
             (hg_norm, m_hg_norm, v_hg_norm, True), (hg_lb_logits, m_hg_lb_logits, v_hg_lb_logits, True)]
    res = _adamw_small(gathered, places, small)
    l_in, = arrive("hg", going_hg, gathered[1])
    big["hg_w_in"] = _adamw_shard("adamw_hg_w_in", hg_w_in[0], m_hg_w_in[0], v_hg_w_in[0], l_in)
    big["hg_w_out"] = _adamw_shard("adamw_hg_w_out", hg_w_out[0], m_hg_w_out[0], v_hg_w_out[0], l_out)
    names = ["final_norm", "ffn_norm", "ffn_conv_b", "ffn_conv_w", "attn_norm", "attn_sinks", "kv_norm", "hg_out_norm", "hg_norm", "hg_lb_logits"]
    shapes = {"final_norm": final_norm.shape, "kv_norm": kv_norm.shape, "ffn_conv_w": ffn_conv_w.shape}
    out = {n: [a.reshape(shapes[n]) if n in shapes else a for a in res[4 * i:4 * i + 4]] for i, n in enumerate(names)}
    out.update(hg_w_in=lead("hg_w_in"), hg_w_out=lead("hg_w_out"), w_kv=big["w_kv"], attn_w_q=lead("attn_w_q"), attn_w_o=lead("attn_w_o"),
               ffn_w_up=big["ffn_w_up"], ffn_w_down=big["ffn_w_down"])
    order = ["hg_norm", "hg_w_in", "hg_lb_logits", "hg_out_norm", "hg_w_out", "kv_norm", "w_kv", "attn_norm", "attn_w_q", "attn_sinks",
             "attn_w_o", "ffn_norm", "ffn_w_up", "ffn_conv_w", "ffn_conv_b", "ffn_w_down", "final_norm"]
    loss = res[-1][0, 0]
    return (loss, dx0[None], *[out[n][0] for n in order], *[out[n][1] for n in order], *[out[n][2] for n in order], *[out[n][3] for n in order])
```

```python
import functools

import jax
import jax.numpy as jnp
from jax import lax
from jax.experimental import pallas as pl
from jax.experimental.pallas import tpu as pltpu

F32 = jnp.float32
BF16 = jnp.bfloat16

EPS = 1e-6
HG_EXPAND = 128
HG_CHUNK = 32
ATT_HEAD_DIM = 64
ATT_KV_HEADS = 2
WINDOW = 128
CONV_WIDTH = 3
ADAM_LR = 0.001
ADAM_B1 = 0.9
ADAM_B2 = 0.999
ADAM_EPS = 1e-08
ADAM_WD = 0.01
ADAM_STEP = 10

N_DEV = 8
VMEM_LIMIT = 48 * 1024 * 1024
NEG = -1e30

NN = (((1,), (0,)), ((), ()))
NT = (((1,), (1,)), ((), ()))
TN = (((0,), (0,)), ((), ()))
MESH = pl.DeviceIdType.MESH


def _dot(a, b, dims=NN):
    return lax.dot_general(a.astype(BF16), b.astype(BF16), dims, preferred_element_type=F32)


def _sigmoid(x):
    return 0.5 * jnp.tanh(0.5 * x) + 0.5


def _silu(x):
    return x * _sigmoid(x)


def _silu_and_grad(x):
    s = _sigmoid(x)
    return x * s, s * (1.0 + x * (1.0 - s))


def _dsilu(x):
    return _silu_and_grad(x)[1]


def _params(semantics):
    return pltpu.CompilerParams(dimension_semantics=semantics, vmem_limit_bytes=VMEM_LIMIT)


def _row_tile(rows, want=512):
    return min(rows, want)


MM_ROWS = 1024


def _matmul(name, a, b, *, dims, grid, a_spec, b_spec, o_spec, out_shape, add=None, add_spec=None):
    assert grid[2] == 1

    def body(*refs):
        a_ref, b_ref, o_ref = refs[0], refs[1], refs[-1]
        total = _dot(a_ref[...], b_ref[...], dims)
        if add is not None:
            total = total + refs[2][...]
        o_ref[...] = total.astype(o_ref.dtype)

    in_specs = [a_spec, b_spec] + ([] if add is None else [add_spec])
    args = (a, b) + (() if add is None else (add,))
    return pl.pallas_call(
        body, name=name, grid=grid, in_specs=in_specs, out_specs=o_spec, out_shape=out_shape,
        compiler_params=_params(("parallel", "parallel", "arbitrary")),
    )(*args)


def _mm_rows(name, a, w, *, out_dtype, add=None):
    s, kdim = a.shape
    n = w.shape[1]
    tm = _row_tile(s, MM_ROWS)
    return _matmul(
        name, a, w, dims=NN, grid=(s // tm, 1, 1),
        a_spec=pl.BlockSpec((tm, kdim), lambda i, j, k: (i, 0)),
        b_spec=pl.BlockSpec((kdim, n), lambda i, j, k: (0, 0)),
        o_spec=pl.BlockSpec((tm, n), lambda i, j, k: (i, 0)),
        out_shape=jax.ShapeDtypeStruct((s, n), out_dtype),
        add=add, add_spec=None if add is None else pl.BlockSpec((tm, n), lambda i, j, k: (i, 0)),
    )


def _mm_rows_nt(name, a, w, *, out_dtype):
    s, n = a.shape
    kdim = w.shape[0]
    tm = _row_tile(s, MM_ROWS)
    return _matmul(
        name, a, w, dims=NT, grid=(s // tm, 1, 1),
        a_spec=pl.BlockSpec((tm, n), lambda i, j, k: (i, 0)),
        b_spec=pl.BlockSpec((kdim, n), lambda i, j, k: (0, 0)),
        o_spec=pl.BlockSpec((tm, kdim), lambda i, j, k: (i, 0)),
        out_shape=jax.ShapeDtypeStruct((s, kdim), out_dtype),
    )


def _mm_tn(name, a, g):
    s, m = a.shape
    n = g.shape[1]
    tn = min(n, 512)
    return _matmul(
        name, a, g, dims=TN, grid=(1, n // tn, 1),
        a_spec=pl.BlockSpec((s, m), lambda i, j, k: (0, 0)),
        b_spec=pl.BlockSpec((s, tn), lambda i, j, k: (0, j)),
        o_spec=pl.BlockSpec((m, tn), lambda i, j, k: (0, j)),
        out_shape=jax.ShapeDtypeStruct((m, n), BF16),
    )


NORM_ROWS = 256


def _norm_proj(name, h, branches, *, tm, nj):
    s, d = h.shape
    n = len(branches)
    rows = min(tm, NORM_ROWS)

    def body(*refs):
        h_ref, gain_refs, w_refs = refs[0], refs[1:1 + n], refs[1 + n:1 + 2 * n]
        a_refs, o_refs = refs[1 + 2 * n:1 + 3 * n], refs[1 + 3 * n:]

        @pl.when(pl.program_id(1) == 0)
        def _():
            def normalize(c, carry):
                at = pl.ds(pl.multiple_of(c * rows, rows), rows)
                xv = h_ref[at, :]
                xhat = xv * lax.rsqrt(jnp.mean(xv * xv, axis=-1, keepdims=True) + EPS)
                for gain_ref, a_ref in zip(gain_refs, a_refs):
                    a_ref[at, :] = (xhat * gain_ref[...]).astype(BF16)
                return carry

            lax.fori_loop(0, tm // rows, normalize, 0)

        for branch, w_ref, a_ref, o_ref in zip(branches, w_refs, a_refs, o_refs):
            o_ref[...] = _dot(a_ref[...], w_ref[...], branch[3]).astype(o_ref.dtype)

    row = pl.BlockSpec((tm, d), lambda i, j: (i, 0))
    vec = pl.BlockSpec((1, d), lambda i, j: (0, 0))
    outs = pl.pallas_call(
        body, name=name, grid=(s // tm, nj), in_specs=[row] + [vec] * n + [b[2] for b in branches],
        out_specs=[row] * n + [b[4] for b in branches],
        out_shape=[jax.ShapeDtypeStruct((s, d), BF16)] * n + [b[5] for b in branches],
        compiler_params=_params(("parallel", "arbitrary")),
    )(h, *[b[0] for b in branches], *[b[1] for b in branches])
    return outs[:n], outs[n:]


def _proj_norm_bwd(name, h, dres, branches):
    s, d = h.shape
    tm = _row_tile(s)
    n = len(branches)

    def body(*refs):
        h_ref, dres_ref = refs[0], refs[1]
        g_refs, w_refs, gain_refs = refs[2:2 + n], refs[2 + n:2 + 2 * n], refs[2 + 2 * n:2 + 3 * n]
        dh_ref, dhb_ref, dg_refs = refs[2 + 3 * n], refs[3 + 3 * n], refs[4 + 3 * n:]
        i = pl.program_id(0)
        xv = h_ref[...]
        r = lax.rsqrt(jnp.mean(xv * xv, axis=-1, keepdims=True) + EPS)
        xhat = xv * r
        total = dres_ref[...]
        for branch, g_ref, w_ref, gain_ref, dg_ref in zip(branches, g_refs, w_refs, gain_refs, dg_refs):
            pairs = branch[4](g_ref, w_ref)
            da = _dot(*pairs[0], NT)
            for pair in pairs[1:]:
                da = da + _dot(*pair, NT)
            dgain = jnp.sum(da * xhat, axis=0, keepdims=True)

            @pl.when(i == 0)
            def _():
                dg_ref[...] = dgain

            @pl.when(i > 0)
            def _():
                dg_ref[...] += dgain

            dxhat = da * gain_ref[...]
            total = total + r * (dxhat - xhat * jnp.mean(dxhat * xhat, axis=-1, keepdims=True))
        dh_ref[...] = total
        dhb_ref[...] = total.astype(BF16)

    row = pl.BlockSpec((tm, d), lambda i: (i, 0))
    vec = pl.BlockSpec((1, d), lambda i: (0, 0))
    outs = pl.pallas_call(
        body, name=name, grid=(s // tm,),
        in_specs=[row, row] + [b[1](tm) for b in branches] + [b[3] for b in branches] + [vec] * n, out_specs=[row, row] + [vec] * n,
        out_shape=[jax.ShapeDtypeStruct((s, d), F32), jax.ShapeDtypeStruct((s, d), BF16)] + [jax.ShapeDtypeStruct((1, d), F32)] * n,
        compiler_params=_params(("arbitrary",)),
    )(h, dres, *[b[0] for b in branches], *[b[2] for b in branches], *[b[5] for b in branches])
    return (outs[0], outs[1]), outs[2:]


def _loss_head(h, gain, target):
    s, d = h.shape
    tm = _row_tile(s)

    def body(h_ref, g_ref, t_ref, dh_ref, dhb_ref, dg_ref, loss_ref):
        i = pl.program_id(0)
        xv = h_ref[...]
        r = lax.rsqrt(jnp.mean(xv * xv, axis=-1, keepdims=True) + EPS)
        xhat = xv * r
        err = xhat * g_ref[...] - t_ref[...]
        dy = err * (1.0 / d)
        part = jnp.zeros((1, 128), F32) + 0.5 * jnp.sum(jnp.mean(err * err, axis=-1, keepdims=True))
        dgain = jnp.sum(dy * xhat, axis=0, keepdims=True)

        @pl.when(i == 0)
        def _():
            dg_ref[...] = dgain
            loss_ref[...] = part

        @pl.when(i > 0)
        def _():
            dg_ref[...] += dgain
            loss_ref[...] += part

        dxhat = dy * g_ref[...]
        dh = r * (dxhat - xhat * jnp.mean(dxhat * xhat, axis=-1, keepdims=True))
        dh_ref[...] = dh
        dhb_ref[...] = dh.astype(BF16)

    row = pl.BlockSpec((tm, d), lambda i: (i, 0))
    vec = pl.BlockSpec((1, d), lambda i: (0, 0))
    return pl.pallas_call(
        body, name="loss_head", grid=(s // tm,), in_specs=[row, vec, row],
        out_specs=[row, row, vec, pl.BlockSpec((1, 128), lambda i: (0, 0))],
        out_shape=[jax.ShapeDtypeStruct((s, d), F32), jax.ShapeDtypeStruct((s, d), BF16), jax.ShapeDtypeStruct((1, d), F32),
                   jax.ShapeDtypeStruct((1, 128), F32)],
        compiler_params=_params(("arbitrary",)),
    )(h, gain, target)


def _bdot(a, b, ca, cb):
    return lax.dot_general(a.astype(BF16), b.astype(BF16), (((ca,), (cb,)), ((0,), (0,))), preferred_element_type=F32)


def _chunk_cumsum(xv, reverse=False):
    n = xv.shape[0]
    row = lax.broadcasted_iota(jnp.int32, xv.shape, 0) % HG_CHUNK
    step = 1
    while step < HG_CHUNK:
        if reverse:
            xv = xv + jnp.where(row < HG_CHUNK - step, pltpu.roll(xv, n - step, axis=0), 0.0)
        else:
            xv = xv + jnp.where(row >= step, pltpu.roll(xv, step, axis=0), 0.0)
        step *= 2
    return xv


def _hg_terms(p_ref, lbl_ref, g_ref=None):
    pq = p_ref[0].astype(F32)
    pf = p_ref[1].astype(F32)
    lb = _sigmoid(lbl_ref[0:1, :] - lbl_ref[1:2, :])
    sig = _sigmoid(pf)
    fg = lb + (1.0 - lb) * sig
    nc = pq.shape[0] // HG_CHUNK
    chunks = lambda a: a.reshape(nc, HG_CHUNK, HG_EXPAND)
    q = chunks(_silu(pq) * HG_EXPAND ** -0.5)
    k = chunks(1.0 - fg)
    v = chunks(p_ref[2].astype(F32))
    g = chunks(_chunk_cumsum(jnp.log(fg)) if g_ref is None else g_ref[...])
    gm = g[:, HG_CHUNK // 2 - 1:HG_CHUNK // 2, :]
    gl = g[:, HG_CHUNK - 1:HG_CHUNK, :]
    e_mid, e_inv, e_all, e_end = jnp.exp(g - gm), jnp.exp(gm - g), jnp.exp(g), jnp.exp(gl - g)
    terms = dict(q=q, k=k, v=v, g=g, qd=q * e_all, qt=q * e_mid, kt=k * e_inv, kd=k * e_end, e_last=jnp.exp(gl),
                 e_mid=e_mid, e_inv=e_inv, e_all=e_all, e_end=e_end)
    return terms, (pq, sig, fg, lb)


def _causal(nc):
    r = lax.broadcasted_iota(jnp.int32, (nc, HG_CHUNK, HG_CHUNK), 1)
    c = lax.broadcasted_iota(jnp.int32, (nc, HG_CHUNK, HG_CHUNK), 2)
    return r >= c


def _hgrn2_fwd(p, lb_logits, out_gain):
    _, s, d = p.shape
    heads = d // HG_EXPAND
    t = _row_tile(s, 2048)
    nc = t // HG_CHUNK

    def body(p_ref, lbl_ref, gain_ref, o_ref, og_ref, st_ref, g_ref, state, decay):
        @pl.when(pl.program_id(1) == 0)
        def _():
            state[...] = jnp.zeros_like(state)

        tm, _ = _hg_terms(p_ref, lbl_ref)
        g_ref[...] = tm["g"].reshape(t, HG_EXPAND)
        decay[...] = tm["e_last"]
        st_ref[...] = _bdot(tm["v"], tm["kd"], 1, 1)

        def chunk(c, carry):
            add = st_ref[c]
            st = state[...]
            st_ref[c] = st
            state[...] = st * decay[c] + add
            return carry

        lax.fori_loop(0, nc, chunk, 0)
        a = jnp.where(_causal(nc), _bdot(tm["qt"], tm["kt"], 2, 2), 0.0)
        ov = (_bdot(tm["qd"], st_ref[...], 2, 2) + _bdot(a, tm["v"], 2, 1)).reshape(t, HG_EXPAND)
        o_ref[...] = ov
        on = ov * lax.rsqrt(jnp.mean(ov * ov, axis=-1, keepdims=True) + EPS) * gain_ref[...]
        og_ref[...] = (on * _silu(p_ref[3].astype(F32))).astype(BF16)

    blk = pl.BlockSpec((t, HG_EXPAND), lambda h, b: (b, h))
    return pl.pallas_call(
        body, name="hgrn2_fwd", grid=(heads, s // t),
        in_specs=[pl.BlockSpec((4, t, HG_EXPAND), lambda h, b: (0, b, h)), pl.BlockSpec((2, HG_EXPAND), lambda h, b: (0, h)),
                  pl.BlockSpec((1, HG_EXPAND), lambda h, b: (0, 0))],
        out_specs=[blk, blk, pl.BlockSpec((None, nc, HG_EXPAND, HG_EXPAND), lambda h, b: (h, b, 0, 0)), blk],
        out_shape=[jax.ShapeDtypeStruct((s, d), F32), jax.ShapeDtypeStruct((s, d), BF16),
                   jax.ShapeDtypeStruct((heads, s // HG_CHUNK, HG_EXPAND, HG_EXPAND), F32), jax.ShapeDtypeStruct((s, d), F32)],
        scratch_shapes=[pltpu.VMEM((HG_EXPAND, HG_EXPAND), F32), pltpu.VMEM((nc, 1, HG_EXPAND), F32)],
        compiler_params=_params(("parallel", "arbitrary")),
    )(p, lb_logits, out_gain)


def _hgrn2_bwd(p, lb_logits, out_gain, o, dog, states, gsum):
    _, s, d = p.shape
    heads = d // HG_EXPAND
    t = _row_tile(s, 1024)
    nc = t // HG_CHUNK
    nb = s // t

    def body(p_ref, lbl_ref, gain_ref, o_ref, dog_ref, st_ref, g_ref, dp_ref, dlbl_ref, dgain_ref, dstate, decay, dst_s):
        h, b = pl.program_id(0), pl.program_id(1)

        @pl.when(b == 0)
        def _():
            dstate[...] = jnp.zeros_like(dstate)

        tm, (pq, sig, fg, lb) = _hg_terms(p_ref, lbl_ref, g_ref)
        pg = p_ref[3].astype(F32)
        ov = o_ref[...]
        r = lax.rsqrt(jnp.mean(ov * ov, axis=-1, keepdims=True) + EPS)
        ohat = ov * r
        dogv = dog_ref[...]
        d_on = dogv * _silu(pg)
        dp_ref[3] = (dogv * ohat * gain_ref[...] * _dsilu(pg)).astype(BF16)
        dgain = jnp.sum(d_on * ohat, axis=0, keepdims=True)

        @pl.when((h == 0) & (b == 0))
        def _():
            dgain_ref[...] = dgain

        @pl.when((h > 0) | (b > 0))
        def _():
            dgain_ref[...] += dgain

        dohat = d_on * gain_ref[...]
        do = (r * (dohat - ohat * jnp.mean(dohat * ohat, axis=-1, keepdims=True))).reshape(nc, HG_CHUNK, HG_EXPAND)

        decay[...] = tm["e_last"]
        dst_s[...] = _bdot(do, tm["qd"], 1, 1)

        def chunk(i, carry):
            c = nc - 1 - i
            add = dst_s[c]
            dst = dstate[...]
            dst_s[c] = dst
            dstate[...] = dst * decay[c] + add
            return carry

        lax.fori_loop(0, nc, chunk, 0)
        st, dst = st_ref[...], dst_s[...]
        causal = _causal(nc)
        a = jnp.where(causal, _bdot(tm["qt"], tm["kt"], 2, 2), 0.0)
        da = jnp.where(causal, _bdot(do, tm["v"], 2, 2), 0.0)
        dqt = _bdot(da, tm["kt"], 2, 1)
        dkt = _bdot(da, tm["qt"], 1, 1)
        dqd = _bdot(do, st, 2, 1)
        dkd = _bdot(tm["v"], dst, 2, 1)
        dv = _bdot(a, do, 1, 1) + _bdot(tm["kd"], dst, 2, 2)
        dq = dqt * tm["e_mid"] + dqd * tm["e_all"]
        dk = dkt * tm["e_inv"] + dkd * tm["e_end"]
        dg = dqt * tm["qt"] - dkt * tm["kt"] + dqd * tm["qd"] - dkd * tm["kd"]
        dgl = jnp.sum(dkd * tm["kd"], axis=1, keepdims=True) + tm["e_last"] * jnp.sum(dst * st, axis=1, keepdims=True)
        last_row = lax.broadcasted_iota(jnp.int32, (nc, HG_CHUNK, HG_EXPAND), 1) == HG_CHUNK - 1
        flat = lambda a3: a3.reshape(t, HG_EXPAND)
        dlf = _chunk_cumsum(flat(dg + jnp.where(last_row, dgl, 0.0)), reverse=True)
        dfg = dlf / fg - flat(dk)
        dlb = jnp.sum(dfg * (1.0 - sig), axis=0, keepdims=True)
        dl0 = dlb * lb * (1.0 - lb)
        dlbl = jnp.concatenate([dl0, -dl0], axis=0)

        @pl.when(b == 0)
        def _():
            dlbl_ref[...] = dlbl

        @pl.when(b > 0)
        def _():
            dlbl_ref[...] += dlbl

        dp_ref[0] = (flat(dq) * HG_EXPAND ** -0.5 * _dsilu(pq)).astype(BF16)
        dp_ref[1] = (dfg * (1.0 - lb) * sig * (1.0 - sig)).astype(BF16)
        dp_ref[2] = flat(dv).astype(BF16)

    blk = pl.BlockSpec((t, HG_EXPAND), lambda h, b: (nb - 1 - b, h))
    pblk = pl.BlockSpec((4, t, HG_EXPAND), lambda h, b: (0, nb - 1 - b, h))
    return pl.pallas_call(
        body, name="hgrn2_bwd", grid=(heads, nb),
        in_specs=[pblk, pl.BlockSpec((2, HG_EXPAND), lambda h, b: (0, h)), pl.BlockSpec((1, HG_EXPAND), lambda h, b: (0, 0)),
                  blk, blk, pl.BlockSpec((None, nc, HG_EXPAND, HG_EXPAND), lambda h, b: (h, nb - 1 - b, 0, 0)), blk],
        out_specs=[pblk, pl.BlockSpec((2, HG_EXPAND), lambda h, b: (0, h)), pl.BlockSpec((1, HG_EXPAND), lambda h, b: (0, 0))],
        out_shape=[jax.ShapeDtypeStruct((4, s, d), BF16), jax.ShapeDtypeStruct((2, d), F32), jax.ShapeDtypeStruct((1, HG_EXPAND), F32)],
        scratch_shapes=[pltpu.VMEM((HG_EXPAND, HG_EXPAND), F32), pltpu.VMEM((nc, 1, HG_EXPAND), F32),
                        pltpu.VMEM((nc, HG_EXPAND, HG_EXPAND), F32)],
        compiler_params=_params(("arbitrary", "arbitrary")),
    )(p, lb_logits, out_gain, o, dog, states, gsum)


HALO = 8
FFN_FWD_ROWS = 512
FFN_BWD_ROWS = 256


def _shift_down(xv, n):
    return pltpu.roll(xv, n, axis=0)


def _shift_up(xv, n):
    return pltpu.roll(xv, xv.shape[0] - n, axis=0)


def _ffn_hidden_down(name, u, conv_w, conv_b, w_down, h):
    _, nj, s, fb = u.shape
    d = w_down.shape[2]
    tm = _row_tile(s, FFN_FWD_ROWS)
    per = tm // HALO

    def body(gate_ref, prev_ref, val_ref, w_ref, b_ref, wd_ref, h_ref, hid_ref, conv_ref, o_ref):
        i = pl.program_id(0)
        total = h_ref[...]
        for j in range(nj):
            prev = jnp.where(i > 0, prev_ref[j].astype(F32), 0.0)
            ext = jnp.concatenate([prev, gate_ref[j].astype(F32)], axis=0)
            conv = b_ref[j] + w_ref[j, 2:3, :] * ext[HALO:]
            conv = conv + w_ref[j, 1:2, :] * _shift_down(ext, 1)[HALO:]
            conv = conv + w_ref[j, 0:1, :] * _shift_down(ext, 2)[HALO:]
            conv = conv.astype(BF16)
            conv_ref[j] = conv
            hidden = _silu(conv) * val_ref[j]
            hid_ref[j] = hidden
            total = total + _dot(hidden, wd_ref[j])
        o_ref[...] = total

    row = pl.BlockSpec((tm, d), lambda i: (i, 0))
    return pl.pallas_call(
        body, name=name, grid=(s // tm,),
        in_specs=[pl.BlockSpec((None, nj, tm, fb), lambda i: (0, 0, i, 0)),
                  pl.BlockSpec((None, nj, HALO, fb), lambda i: (0, 0, jnp.maximum(i * per - 1, 0), 0)),
                  pl.BlockSpec((None, nj, tm, fb), lambda i: (1, 0, i, 0)),
                  pl.BlockSpec((nj, CONV_WIDTH, fb), lambda i: (0, 0, 0)), pl.BlockSpec((nj, 1, fb), lambda i: (0, 0, 0)),
                  pl.BlockSpec((nj, fb, d), lambda i: (0, 0, 0)), row],
        out_specs=[pl.BlockSpec((nj, tm, fb), lambda i: (0, i, 0)), pl.BlockSpec((nj, tm, fb), lambda i: (0, i, 0)), row],
        out_shape=[jax.ShapeDtypeStruct((nj, s, fb), BF16), jax.ShapeDtypeStruct((nj, s, fb), BF16), jax.ShapeDtypeStruct((s, d), F32)],
        compiler_params=_params(("parallel",)),
    )(u, u, u, conv_w, conv_b, w_down, h)


def _ffn_hidden_up_bwd(name, u, conv, dh, conv_w, w_up, h, gain, dres):
    _, nj, s, fb = u.shape
    d = w_up.shape[2]
    tm = _row_tile(s, FFN_BWD_ROWS)
    per = tm // HALO
    nblk = s // HALO
    ni = s // tm

    def body(gate_ref, conv_ref, cnext_ref, val_ref, vnext_ref, dh_ref, dhnext_ref, w_ref, wu_ref, h_ref, gain_ref, dres_ref,
             du_ref, dw_ref, db_ref, dx_ref, dxb_ref, dgain_ref):
        i = pl.program_id(0)
        has_next = i < ni - 1
        total = None
        for j in range(nj):
            act, dact = _silu_and_grad(conv_ref[j])
            dval = dh_ref[j] * act
            after = jnp.where(has_next, dhnext_ref[j].astype(F32), 0.0) * vnext_ref[j].astype(F32) * _dsilu(cnext_ref[j].astype(F32))
            dconv = jnp.concatenate([(dh_ref[j] * val_ref[j] * dact).astype(F32), after], axis=0)
            taps = [_shift_up(dconv, 2)[:tm], _shift_up(dconv, 1)[:tm], dconv[:tm]]
            dgate = (w_ref[j, 0:1, :] * taps[0] + w_ref[j, 1:2, :] * taps[1] + w_ref[j, 2:3, :] * taps[2]).astype(BF16)
            du_ref[0, j] = dgate
            du_ref[1, j] = dval
            part = _dot(dgate, wu_ref[j]) + _dot(dval, wu_ref[nj + j])
            total = part if total is None else total + part
            gate = gate_ref[j].astype(F32)
            dw = jnp.concatenate([jnp.sum(tap * gate, axis=0, keepdims=True) for tap in taps], axis=0)
            db = jnp.sum(taps[2], axis=0, keepdims=True)

            @pl.when(i == 0)
            def _():
                dw_ref[j] = dw
                db_ref[j] = db

            @pl.when(i > 0)
            def _():
                dw_ref[j] += dw
                db_ref[j] += db

        xv = h_ref[...]
        r = lax.rsqrt(jnp.mean(xv * xv, axis=-1, keepdims=True) + EPS)
        xhat = xv * r
        dgain = jnp.sum(total * xhat, axis=0, keepdims=True)

        @pl.when(i == 0)
        def _():
            dgain_ref[...] = dgain

        @pl.when(i > 0)
        def _():
            dgain_ref[...] += dgain

        dxhat = total * gain_ref[...]
        dx = dres_ref[...] + r * (dxhat - xhat * jnp.mean(dxhat * xhat, axis=-1, keepdims=True))
        dx_ref[...] = dx
        dxb_ref[...] = dx.astype(BF16)

    def tile(part):
        return pl.BlockSpec((None, nj, tm, fb), lambda i: (part, 0, i, 0))

    def after(part):
        return pl.BlockSpec((None, nj, HALO, fb), lambda i: (part, 0, jnp.minimum((i + 1) * per, nblk - 1), 0))

    row = pl.BlockSpec((tm, d), lambda i: (i, 0))
    own = pl.BlockSpec((nj, tm, fb), lambda i: (0, i, 0))
    nxt = pl.BlockSpec((nj, HALO, fb), lambda i: (0, jnp.minimum((i + 1) * per, nblk - 1), 0))
    return pl.pallas_call(
        body, name=name, grid=(ni,),
        in_specs=[tile(0), own, nxt, tile(1), after(1), own, nxt,
                  pl.BlockSpec((nj, CONV_WIDTH, fb), lambda i: (0, 0, 0)),
                  pl.BlockSpec((2 * nj, fb, d), lambda i: (0, 0, 0)), row, pl.BlockSpec((1, d), lambda i: (0, 0)), row],
        out_specs=[pl.BlockSpec((2, nj, tm, fb), lambda i: (0, 0, i, 0)),
                   pl.BlockSpec((nj, CONV_WIDTH, fb), lambda i: (0, 0, 0)), pl.BlockSpec((nj, 1, fb), lambda i: (0, 0, 0)),
                   row, row, pl.BlockSpec((1, d), lambda i: (0, 0))],
        out_shape=[jax.ShapeDtypeStruct((2, nj, s, fb), BF16), jax.ShapeDtypeStruct((nj, CONV_WIDTH, fb), F32),
                   jax.ShapeDtypeStruct((nj, 1, fb), F32), jax.ShapeDtypeStruct((s, d), F32), jax.ShapeDtypeStruct((s, d), BF16),
                   jax.ShapeDtypeStruct((1, d), F32)],
        compiler_params=_params(("arbitrary",)),
    )(u, conv, conv, u, u, dh, dh, conv_w, w_up, h, gain, dres)


ATT_TILE = 512


def _stack_heads(ref, rows, first_head, count):
    hd = ATT_HEAD_DIM
    return jnp.concatenate([ref[rows, (first_head + j) * hd:(first_head + j + 1) * hd] for j in range(count)], axis=0)


def _unstack_heads(stacked, ref, rows, first_head, count):
    hd = ATT_HEAD_DIM
    for pair in range(count // 2):
        both = [stacked[(2 * pair + j) * WINDOW:(2 * pair + j + 1) * WINDOW, :] for j in range(2)]
        ref[rows, (first_head + 2 * pair) * hd:(first_head + 2 * pair + 2) * hd] = jnp.concatenate(both, axis=1).astype(ref.dtype)


def _attn_bias(first_head, count, n_heads, first):
    lanes = count * WINDOW
    ik = lax.broadcasted_iota(jnp.int32, (2 * WINDOW, lanes), 0)
    iq = lax.broadcasted_iota(jnp.int32, (2 * WINDOW, lanes), 1) % WINDOW
    dist = iq + WINDOW - ik
    valid = (dist >= 0) & (dist < WINDOW) & (ik >= (WINDOW if first else 0))
    slope = jnp.concatenate([jnp.zeros((1, WINDOW), F32) + 2.0 ** (-8.0 * (first_head + j + 1) / n_heads) for j in range(count)], axis=1)
    return jnp.where(valid, -slope * dist.astype(F32), NEG)


def _fill_attn_bias(bias_ref, group, n_heads):
    @pl.when(pl.program_id(0) == 0)
    def _():
        for g in range(ATT_KV_HEADS):
            bias_ref[0, g] = _attn_bias(g * group, group, n_heads, False)
            bias_ref[1, g] = _attn_bias(g * group, group, n_heads, True)


def _attn_probs_t(kb_scaled, qs, sink_ref, first_head, count, bias):
    sink = jnp.concatenate([jnp.zeros((1, WINDOW), F32) + sink_ref[0, first_head + j] for j in range(count)], axis=1)
    sc = _dot(kb_scaled, qs, NT) + bias
    m = jnp.maximum(jnp.max(sc, axis=0, keepdims=True), sink)
    e = jnp.exp(sc - m)
    es = jnp.exp(sink - m)
    inv = 1.0 / (jnp.sum(e, axis=0, keepdims=True) + es)
    return e * inv, es * inv


ATT_SCALE = ATT_HEAD_DIM ** -0.5


def _attn_specs(s, d, kvd, tq):
    per = tq // WINDOW
    return [pl.BlockSpec((tq, d), lambda i: (i, 0)), pl.BlockSpec((tq, kvd), lambda i: (i, 0)),
            pl.BlockSpec((WINDOW, kvd), lambda i: (jnp.maximum(i * per - 1, 0), 0))]


def _attn_fwd(q, kv, sinks):
    s, d = q.shape
    kvd = kv.shape[1]
    half = kvd // 2
    hd = ATT_HEAD_DIM
    nq = d // hd
    group = nq // ATT_KV_HEADS
    tq = min(s, ATT_TILE)
    per = tq // WINDOW

    def body(q_ref, kvc_ref, kvp_ref, sink_ref, o_ref, band, bias_ref):
        i = pl.program_id(0)
        _fill_attn_bias(bias_ref, group, nq)
        band[0:WINDOW, :] = kvp_ref[...]
        band[WINDOW:, :] = kvc_ref[...]

        def block(b, carry):
            rows = pl.ds(pl.multiple_of(b * WINDOW, WINDOW), WINDOW)
            keys = pl.ds(pl.multiple_of(b * WINDOW, WINDOW), 2 * WINDOW)
            first = (i * per + b) == 0
            for g in range(ATT_KV_HEADS):
                bias = jnp.where(first, bias_ref[1, g], bias_ref[0, g])
                p, _ = _attn_probs_t(band[keys, g * hd:(g + 1) * hd] * ATT_SCALE, _stack_heads(q_ref, rows, g * group, group), sink_ref,
                                     g * group, group, bias)
                out_t = _dot(band[keys, half + g * hd:half + (g + 1) * hd], p, TN)
                _unstack_heads(out_t.T, o_ref, rows, g * group, group)
            return carry

        lax.fori_loop(0, per, block, 0)

    return pl.pallas_call(
        body, name="attn_fwd", grid=(s // tq,),
        in_specs=_attn_specs(s, d, kvd, tq) + [pl.BlockSpec(memory_space=pltpu.SMEM)],
        out_specs=pl.BlockSpec((tq, d), lambda i: (i, 0)), out_shape=jax.ShapeDtypeStruct((s, d), BF16),
        scratch_shapes=[pltpu.VMEM((tq + WINDOW, kvd), BF16), pltpu.VMEM((2, ATT_KV_HEADS, 2 * WINDOW, group * WINDOW), F32)],
        compiler_params=_params(("arbitrary",)),
    )(q, kv, kv, sinks)


def _attn_bwd(q, kv, o, do, sinks):
    s, d = q.shape
    kvd = kv.shape[1]
    half = kvd // 2
    hd = ATT_HEAD_DIM
    nq = d // hd
    group = nq // ATT_KV_HEADS
    tq = min(s, ATT_TILE)
    per = tq // WINDOW
    nt = s // tq

    def body(q_ref, kvc_ref, kvp_ref, o_ref, do_ref, sink_ref, dq_ref, dkvc_ref, dkvp_ref, ds_ref, band, dband, bias_ref):
        i = pl.program_id(0)
        _fill_attn_bias(bias_ref, group, nq)
        band[0:WINDOW, :] = kvp_ref[...]
        band[WINDOW:, :] = kvc_ref[...]
        dband[...] = jnp.zeros_like(dband)
        ds_ref[...] = jnp.zeros_like(ds_ref)

        def block(b, carry):
            rows = pl.ds(pl.multiple_of(b * WINDOW, WINDOW), WINDOW)
            keys = pl.ds(pl.multiple_of(b * WINDOW, WINDOW), 2 * WINDOW)
            first = (i * per + b) == 0
            dks, dvs = [], []
            for g in range(ATT_KV_HEADS):
                kb = band[keys, g * hd:(g + 1) * hd] * ATT_SCALE
                vb = band[keys, half + g * hd:half + (g + 1) * hd]
                qs = _stack_heads(q_ref, rows, g * group, group)
                dos = _stack_heads(do_ref, rows, g * group, group)
                p, ps = _attn_probs_t(kb, qs, sink_ref, g * group, group, jnp.where(first, bias_ref[1, g], bias_ref[0, g]))
                prod = dos.astype(F32) * _stack_heads(o_ref, rows, g * group, group).astype(F32)
                dsum = lax.dot_general(jnp.ones((8, hd), F32), prod, NT, precision=lax.Precision.HIGHEST,
                                       preferred_element_type=F32)[0:1, :]
                dsc = p * (_dot(vb, dos, NT) - dsum)
                dvs.append(_dot(p, dos))
                dks.append(_dot(dsc, qs * ATT_SCALE))
                _unstack_heads(_dot(kb, dsc, TN).T, dq_ref, rows, g * group, group)
                gone = ps * dsum
                for j in range(group):
                    ds_ref[g * group + j:g * group + j + 1, :] += jnp.zeros((1, 128), F32) - jnp.sum(gone[:, j * WINDOW:(j + 1) * WINDOW])
            dband[keys, 0:half] += jnp.concatenate(dks, axis=1)
            dband[keys, half:] += jnp.concatenate(dvs, axis=1)
            return carry

        lax.fori_loop(0, per, block, 0)
        dkvp_ref[...] = dband[0:WINDOW, :]
        dkvc_ref[...] = dband[WINDOW:, :]

    big = pl.BlockSpec((tq, d), lambda i: (i, 0))
    return pl.pallas_call(
        body, name="attn_bwd", grid=(nt,),
        in_specs=_attn_specs(s, d, kvd, tq) + [big, big, pl.BlockSpec(memory_space=pltpu.SMEM)],
        out_specs=[big, pl.BlockSpec((tq, kvd), lambda i: (i, 0)), pl.BlockSpec((None, WINDOW, kvd), lambda i: (i, 0, 0)),
                   pl.BlockSpec((None, nq, 128), lambda i: (i, 0, 0))],
        out_shape=[jax.ShapeDtypeStruct((s, d), BF16), jax.ShapeDtypeStruct((s, kvd), F32), jax.ShapeDtypeStruct((nt, WINDOW, kvd), F32),
                   jax.ShapeDtypeStruct((nt, nq, 128), F32)],
        scratch_shapes=[pltpu.VMEM((tq + WINDOW, kvd), BF16), pltpu.VMEM((tq + WINDOW, kvd), F32),
                        pltpu.VMEM((2, ATT_KV_HEADS, 2 * WINDOW, group * WINDOW), F32)],
        compiler_params=_params(("arbitrary",)),
    )(q, kv, kv, o, do, sinks)


HBM_SPEC = pl.BlockSpec(memory_space=pltpu.HBM)
VMEM_SPEC = pl.BlockSpec(memory_space=pltpu.VMEM)


def _place():
    return lax.axis_index("x"), lax.axis_index("y"), lax.axis_index("c")


def _flip(pos, r):
    return tuple(1 - p if (r >> (2 - a)) & 1 else p for a, p in enumerate(pos))


def _index(pos):
    return 4 * pos[0] + 2 * pos[1] + pos[2]


def _all_gather(name, shards, spec):
    n = len(shards)

    def body(*refs):
        x_refs, o_refs = refs[:n], refs[n:2 * n]
        send_sems, recv_sems, local_sems = refs[2 * n:]
        me = _place()
        sibling = _flip(me, 1)
        far = [_flip(me, r) for r in (4, 2, 6)]

        def copy(t, sem, block, to, src=None):
            rows = o_refs[t].at[_index(block)]
            return pltpu.make_async_remote_copy(
                src_ref=rows if src is None else src, dst_ref=rows, send_sem=send_sems.at[t, sem], recv_sem=recv_sems.at[t, sem],
                device_id=to, device_id_type=MESH)

        own = [pltpu.make_async_copy(x_refs[t], o_refs[t].at[_index(me)], local_sems.at[t]) for t in range(n)]
        for cp in own:
            cp.start()
        first = []
        for t in range(n):
            first.append(copy(t, 0, me, sibling, src=x_refs[t]))
            first += [copy(t, 1 + j, me, peer, src=x_refs[t]) for j, peer in enumerate(far)]
        for cp in first:
            cp.start()
        passed = []
        for j, peer in enumerate(far):
            for t in range(n):
                copy(t, 1 + j, peer, me).wait_recv()
                cp = copy(t, 4 + j, peer, sibling)
                cp.start()
                passed.append(cp)
        for t in range(n):
            copy(t, 0, sibling, me).wait_recv()
            for j, peer in enumerate(far):
                copy(t, 4 + j, _flip(peer, 1), me).wait_recv()
        for cp in first + passed:
            cp.wait_send()
        for cp in own:
            cp.wait()

    return pl.pallas_call(
        body, name=name, in_specs=[spec] * n, out_specs=[spec] * n,
        out_shape=[jax.ShapeDtypeStruct((N_DEV,) + sh.shape, sh.dtype) for sh in shards],
        scratch_shapes=[pltpu.SemaphoreType.DMA((n, 7)), pltpu.SemaphoreType.DMA((n, 7)), pltpu.SemaphoreType.DMA((n,))],
    )(*shards)


SEM_SPEC = pl.BlockSpec(memory_space=pltpu.SEMAPHORE)
ANY_SPEC = pl.BlockSpec(memory_space=pl.ANY)


def _landing(own, mine):
    return lax.dynamic_update_slice(lax.empty((N_DEV,) + own.shape, own.dtype), own[None], (mine,) + (0,) * own.ndim)


def _pinned(a, token):
    return a + token[0:1, 0:1].astype(a.dtype)


def _peer_copies(src_refs, land_refs, send_sems, recv_sems, scatter, arrivals):
    me = _place()
    mine = _index(me)
    copies = []
    for t, (src, land) in enumerate(zip(src_refs, land_refs)):
        for r in range(1, N_DEV):
            peer = _flip(me, r)
            theirs = _index(peer)
            sem = t * (N_DEV - 1) + r - 1
            copies.append(pltpu.make_async_remote_copy(
                src_ref=src.at[theirs] if scatter else src, dst_ref=land.at[theirs if arrivals else mine],
                send_sem=send_sems.at[sem], recv_sem=recv_sems.at[sem], device_id=peer, device_id_type=MESH))
    return copies


def _send_start(name, sources, lands, scatter, after=None):
    n = len(sources)
    extra = 0 if after is None else 1

    def body(*refs):
        outs = refs[2 * n + extra:]
        for out in _peer_copies(refs[:n], refs[n:2 * n], outs[0], outs[1], scatter, False):
            out.start()
        outs[-1][...] = jnp.zeros_like(outs[-1])

    outs = pl.pallas_call(
        body, name=name, in_specs=[HBM_SPEC] * (2 * n) + [ANY_SPEC] * extra,
        out_specs=[SEM_SPEC, SEM_SPEC] + [HBM_SPEC] * (2 * n) + [VMEM_SPEC],
        out_shape=[pltpu.SemaphoreType.DMA((n * (N_DEV - 1),)), pltpu.SemaphoreType.DMA((n * (N_DEV - 1),))]
        + [pltpu.HBM(a.shape, a.dtype) for a in list(sources) + list(lands)] + [jax.ShapeDtypeStruct((8, 128), F32)],
        input_output_aliases={i: 2 + i for i in range(2 * n)},
        compiler_params=pltpu.CompilerParams(has_side_effects=pltpu.SideEffectType.DATAFLOW_SIDE_EFFECTING),
    )(*[pltpu.with_memory_space_constraint(a, pltpu.HBM) for a in list(sources) + list(lands)], *([] if after is None else [after]))
    return outs[0], outs[1], outs[2:2 + n], outs[2 + n:2 + 2 * n], outs[-1]


def _send_wait(name, started, after, scatter):
    send_sems, recv_sems, sources, lands, _ = started
    n = len(sources)

    def body(*refs):
        for out in _peer_copies(refs[:n], refs[n:2 * n], refs[2 * n], refs[2 * n + 1], scatter, False):
            out.wait_send()
        for arrival in _peer_copies(refs[:n], refs[n:2 * n], refs[2 * n], refs[2 * n + 1], scatter, True):
            arrival.wait_recv()

    outs = pl.pallas_call(
        body, name=name, in_specs=[HBM_SPEC] * (2 * n) + [SEM_SPEC, SEM_SPEC, ANY_SPEC], out_specs=[HBM_SPEC] * (2 * n),
        out_shape=[pltpu.HBM(a.shape, a.dtype) for a in list(sources) + list(lands)],
        input_output_aliases={i: i for i in range(2 * n)},
        compiler_params=pltpu.CompilerParams(has_side_effects=pltpu.SideEffectType.DATAFLOW_SIDE_EFFECTING),
    )(*sources, *lands, send_sems, recv_sems, after)
    return outs[n:]


def _pack_rows(parts):
    offsets, row = [], 0
    for part in parts:
        offsets.append(row)
        row += part.shape[0]
    return offsets, -(-row // 8) * 8, -(-max(part.shape[1] for part in parts) // 128) * 128


def _pack(name, parts):
    offsets, rows, width = _pack_rows(parts)

    def body(*refs):
        o_ref = refs[-1]
        o_ref[...] = jnp.zeros_like(o_ref)
        for off, ref in zip(offsets, refs[:-1]):
            o_ref[off:off + ref.shape[0], 0:ref.shape[1]] = ref[...]

    return pl.pallas_call(body, name=name, in_specs=[VMEM_SPEC] * len(parts), out_specs=VMEM_SPEC,
                          out_shape=jax.ShapeDtypeStruct((rows, width), F32))(*parts)


def _adamw_math(w, g, m, v):
    m = ADAM_B1 * m + (1.0 - ADAM_B1) * g
    v = ADAM_B2 * v + (1.0 - ADAM_B2) * (g * g)
    m_hat = m * (1.0 / (1.0 - ADAM_B1 ** ADAM_STEP))
    denom = jnp.sqrt(v * (1.0 / (1.0 - ADAM_B2 ** ADAM_STEP))) + ADAM_EPS
    inv = pl.reciprocal(denom, approx=True)
    inv = inv * (2.0 - denom * inv)
    return -ADAM_LR * (m_hat * inv + ADAM_WD * w), m, v


def _adamw_step(w_ref, m_ref, v_ref, p_ref, g_ref, d_ref, nm_ref, nv_ref):
    g = p_ref[0].astype(F32)
    for dev in range(1, N_DEV):
        g = g + p_ref[dev].astype(F32)
    g_ref[...] = g
    d_ref[...], nm_ref[...], nv_ref[...] = _adamw_math(w_ref[...], g, m_ref[...], v_ref[...])


def _adamw_rows(rows):
    return max(t for t in range(8, min(rows, 256) + 1, 8) if rows % t == 0)


def _adamw_shard(name, w, m, v, partials):
    rows, cols = w.shape
    tr = _adamw_rows(rows)
    blk = pl.BlockSpec((tr, cols), lambda i: (i, 0))
    return pl.pallas_call(
        _adamw_step_fn(), name=name, grid=(rows // tr,), in_specs=[blk, blk, blk, pl.BlockSpec((N_DEV, tr, cols), lambda i: (0, i, 0))],
        out_specs=[blk] * 4, out_shape=[jax.ShapeDtypeStruct((rows, cols), F32)] * 4, compiler_params=_params(("parallel",)),
    )(w, m, v, partials)


def _adamw_step_fn():
    return functools.partial(_adamw_step)


def _adamw_layers(name, w, m, v, partials):
    layers, rows, cols = w.shape
    tr = _adamw_rows(rows)
    last = rows // tr - 1

    def body(w_ref, m_ref, v_ref, *rest):
        for layer in range(layers):
            @pl.when(pl.program_id(0) == layer)
            def _():
                _adamw_step(w_ref, m_ref, v_ref, rest[layer], *rest[layers:])

    blk = pl.BlockSpec((None, tr, cols), lambda l, i: (l, i, 0))
    part = lambda layer: pl.BlockSpec((N_DEV, tr, cols), lambda l, i: (0, jnp.where(l == layer, i, jnp.where(l < layer, 0, last)), 0))
    return pl.pallas_call(
        body, name=name, grid=(layers, rows // tr), in_specs=[blk, blk, blk] + [part(layer) for layer in range(layers)],
        out_specs=[blk] * 4, out_shape=[jax.ShapeDtypeStruct(w.shape, F32)] * 4, compiler_params=_params(("arbitrary", "arbitrary")),
    )(w, m, v, *partials)


def _adamw_small(gathered, places, entries):
    n = len(entries)
    np_ = len(gathered)

    def body(*refs):
        pack_refs = refs[:np_]
        refs = refs[np_ - 1:]
        w_refs, m_refs, v_refs = refs[1:1 + n], refs[1 + n:1 + 2 * n], refs[1 + 2 * n:1 + 3 * n]
        outs = refs[1 + 3 * n:]
        totals = []
        for pack_ref in pack_refs:
            acc = pack_ref[0]
            for dev in range(1, N_DEV):
                acc = acc + pack_ref[dev]
            totals.append(acc)
        mine = _index(_place())
        for e in range(n):
            rows, cols = w_refs[e].shape
            total, off = totals[places[e][0]], places[e][1]
            if entries[e][3]:
                g = jnp.zeros((rows, cols), F32)
                for dev in range(N_DEV):
                    g = g + jnp.where(mine == dev, total[off + dev * rows:off + (dev + 1) * rows, 0:cols], 0.0)
            else:
                g = total[off:off + rows, 0:cols]
            outs[4 * e][...] = g
            outs[4 * e + 1][...], outs[4 * e + 2][...], outs[4 * e + 3][...] = _adamw_math(w_refs[e][...], g, m_refs[e][...], v_refs[e][...])
        outs[4 * n][...] = totals[places[n][0]][places[n][1]:places[n][1] + 1, 0:128]

    shapes = []
    for w, _, _, _ in entries:
        shapes += [jax.ShapeDtypeStruct(w.shape, F32)] * 4
    shapes.append(jax.ShapeDtypeStruct((1, 128), F32))
    return pl.pallas_call(
        body, name="adamw_small", in_specs=[VMEM_SPEC] * (np_ + 3 * n), out_specs=[VMEM_SPEC] * len(shapes), out_shape=shapes,
        compiler_params=pltpu.CompilerParams(vmem_limit_bytes=VMEM_LIMIT),
    )(*gathered, *[e[0] for e in entries], *[e[1] for e in entries], *[e[2] for e in entries])


def _ffn_forward(tag, h, gain, w_up, late):
    s, d = h.shape
    fb = w_up.shape[1]
    tm = _row_tile(s, 2 * MM_ROWS)
    (a,), (u,) = _norm_proj(f"ffn_up_{tag}", h, [
        (gain, w_up, pl.BlockSpec((None, fb, d), lambda i, j: (j, 0, 0)), NT,
         pl.BlockSpec((None, None, tm, fb), lambda i, j: (j // 4, j % 4, i, 0)), jax.ShapeDtypeStruct((2, 4, s, fb), BF16))], tm=tm, nj=N_DEV)
    w_down, conv_w, conv_b = late(u)
    hidden, conv, out = _ffn_hidden_down(f"ffn_hidden_down_{tag}", u, conv_w, conv_b, w_down, h)
    return out, (a, u, hidden, conv)


def _ffn_backward(tag, h, gain, w_up, w_down, conv_w, conv_b, saved, dout):
    a, u, hidden, conv = saved
    dout, dout_bf = dout
    s, d = h.shape
    fb = w_up.shape[1]
    tm = _row_tile(s, MM_ROWS)
    dhidden = _matmul(
        f"ffn_down_bwd_{tag}", dout_bf, w_down, dims=NT, grid=(s // tm, 4, 1),
        a_spec=pl.BlockSpec((tm, d), lambda i, j, k: (i, 0)),
        b_spec=pl.BlockSpec((None, fb, d), lambda i, j, k: (j, 0, 0)),
        o_spec=pl.BlockSpec((None, tm, fb), lambda i, j, k: (j, i, 0)),
        out_shape=jax.ShapeDtypeStruct((4, s, fb), BF16))
    dw_down = _matmul(
        f"ffn_down_grad_{tag}", hidden, dout_bf, dims=TN, grid=(4, 1, 1),
        a_spec=pl.BlockSpec((None, s, fb), lambda i, j, k: (i, 0, 0)),
        b_spec=pl.BlockSpec((s, d), lambda i, j, k: (0, 0)),
        o_spec=pl.BlockSpec((None, fb, d), lambda i, j, k: (i, 0, 0)),
        out_shape=jax.ShapeDtypeStruct((4, fb, d), BF16))
    du, dconv_w, dconv_b, dh, dh_bf, dgain = _ffn_hidden_up_bwd(f"ffn_hidden_up_bwd_{tag}", u, conv, dhidden, conv_w, w_up, h, gain, dout)
    dw_up = _matmul(
        f"ffn_up_grad_{tag}", du, a, dims=TN, grid=(N_DEV, 1, 1),
        a_spec=pl.BlockSpec((None, None, s, fb), lambda i, j, k: (i // 4, i % 4, 0, 0)),
        b_spec=pl.BlockSpec((s, d), lambda i, j, k: (0, 0)),
        o_spec=pl.BlockSpec((None, fb, d), lambda i, j, k: (i, 0, 0)),
        out_shape=jax.ShapeDtypeStruct((N_DEV, fb, d), BF16))
    return (dh, dh_bf), dgain, dw_up, dw_down, dconv_w, dconv_b


def kernel(x, hg_norm, hg_w_in, hg_lb_logits, hg_out_norm, hg_w_out, kv_norm, w_kv, attn_norm, attn_w_q, attn_sinks, attn_w_o, ffn_norm, ffn_w_up, ffn_conv_w, ffn_conv_b, ffn_w_down, final_norm, loss_target, m_hg_norm, m_hg_w_in, m_hg_lb_logits, m_hg_out_norm, m_hg_w_out, m_kv_norm, m_w_kv, m_attn_norm, m_attn_w_q, m_attn_sinks, m_attn_w_o, m_ffn_norm, m_ffn_w_up, m_ffn_conv_w, m_ffn_conv_b, m_ffn_w_down, m_final_norm, v_hg_norm, v_hg_w_in, v_hg_lb_logits, v_hg_out_norm, v_hg_w_out, v_kv_norm, v_w_kv, v_attn_norm, v_attn_w_q, v_attn_sinks, v_attn_w_o, v_ffn_norm, v_ffn_w_up, v_ffn_conv_w, v_ffn_conv_b, v_ffn_w_down, v_final_norm):
    _, s, d = x.shape
    x0, target = x[0], loss_target[0]
    half = hg_w_in.shape[2]
    fs = ffn_conv_w.shape[2]
    fb = 2 * fs
    kvd = w_kv.shape[1]
    nq = d // ATT_HEAD_DIM
    tm = _row_tile(s, MM_ROWS)

    mine = _index(_place())
    gather = lambda tag, shards, after: _send_start("gather_start_" + tag, shards, [_landing(a, mine) for a in shards], False, after)
    w_in, g_hgn, g_lbl, w_out = _all_gather("gather_hg", [hg_w_in[0].astype(BF16), hg_norm, hg_lb_logits, hg_w_out[0].astype(BF16)], HBM_SPEC)
    w_out = w_out.reshape(d, d)
    up_t = lambda a: jnp.swapaxes(a, -1, -2)
    coming_up0 = gather("ffn_up0", [up_t(ffn_w_up[0]).astype(BF16)], g_hgn)
    hgn = _pinned(g_hgn.reshape(1, d), coming_up0[4])
    lbl = g_lbl.transpose(1, 0, 2).reshape(2, d)
    conv_b = [ffn_conv_b[layer].reshape(4, 1, fb) for layer in range(2)]
    gains = [ffn_norm[0:1], ffn_norm[1:2]]
    kvn, fin = kv_norm.reshape(1, d), final_norm.reshape(1, d)

    t2 = _row_tile(s, 2 * MM_ROWS)
    (a0,), (p,) = _norm_proj("hg_in", x0, [
        (hgn, w_in, pl.BlockSpec((None, d, half), lambda i, j: (j, 0, 0)), NN,
         pl.BlockSpec((None, t2, half), lambda i, j: (j // 2, i, j % 2)), jax.ShapeDtypeStruct((4, s, d), BF16))], tm=t2, nj=N_DEV)
    o, og, states, gsum = _hgrn2_fwd(p, lbl, hg_out_norm)
    coming_dn0 = gather("ffn_down0", [ffn_conv_w, ffn_w_down[0].astype(BF16)], o)
    x1 = _mm_rows("hg_out", og, _pinned(w_out, coming_dn0[4]), out_dtype=F32, add=x0)
    w_up0, = _send_wait("gather_wait_ffn_up0", coming_up0, x1, False)
    coming_attn = gather("attn", [w_kv.astype(BF16), attn_w_q[0].astype(BF16), attn_w_o[0].astype(BF16)], w_up0)
    gains[0] = _pinned(gains[0], coming_attn[4])
    w_up, w_dn, conv_w, coming = [w_up0, None], [None, None], [], {}

    def late0(u):
        g_cw, w_dn0 = _send_wait("gather_wait_ffn_down0", coming_dn0, u, False)
        w_dn[0] = w_dn0.reshape(4, fb, d)
        conv_w.extend(g_cw[:, layer].reshape(4, 2, CONV_WIDTH, fs).transpose(0, 2, 1, 3).reshape(4, CONV_WIDTH, fb) for layer in range(2))
        coming["up1"] = gather("ffn_up1", [up_t(ffn_w_up[1]).astype(BF16)], w_dn0)
        return w_dn[0], conv_w[0], _pinned(conv_b[0], coming["up1"][4])

    x2, saved0 = _ffn_forward("0", x1, gains[0], w_up[0], late0)
    w_kvg, w_q, w_o = _send_wait("gather_wait_attn", coming_attn, x2, False)
    w_kvg, w_q, w_o = w_kvg.reshape(d, kvd), w_q.reshape(d, d), w_o.reshape(d, d)
    (akv, a2), (kv, q) = _norm_proj("attn_in", x2, [
        (kvn, w_kvg, pl.BlockSpec((d, kvd), lambda i, j: (0, 0)), NN, pl.BlockSpec((tm, kvd), lambda i, j: (i, 0)), jax.ShapeDtypeStruct((s, kvd), BF16)),
        (attn_norm, w_q, pl.BlockSpec((d, d), lambda i, j: (0, 0)), NN, pl.BlockSpec((tm, d), lambda i, j: (i, 0)), jax.ShapeDtypeStruct((s, d), BF16))],
        tm=tm, nj=1)
    coming_dn1 = gather("ffn_down1", [ffn_w_down[1].astype(BF16)], q)
    att = _attn_fwd(q, kv, _pinned(attn_sinks, coming_dn1[4]))
    x3 = _mm_rows("attn_out", att, w_o, out_dtype=F32, add=x2)
    w_up[1], = _send_wait("gather_wait_ffn_up1", coming["up1"], x3, False)

    def late1(u):
        w_dn[1] = _send_wait("gather_wait_ffn_down1", coming_dn1, u, False)[0].reshape(4, fb, d)
        return w_dn[1], conv_w[1], conv_b[1]

    x4, saved1 = _ffn_forward("1", x3, gains[1], w_up[1], late1)
    dx4, dx4_bf, d_fin, loss_part = _loss_head(x4, fin, target)

    dx3, d_fn1, dw_up1, dw_dn1, dcw1, dcb1 = _ffn_backward("1", x3, gains[1], w_up[1], w_dn[1], conv_w[1], conv_b[1], saved1, (dx4, dx4_bf))
    rows = d // N_DEV
    scatter = lambda tag, stacks: _send_start("scatter_start_" + tag, stacks, [_landing(lax.dynamic_index_in_dim(a, mine, keepdims=False), mine) for a in stacks], True)
    going_ffn1 = scatter("ffn1", [dw_up1, dw_dn1.reshape(N_DEV, fs, d)])
    datt = _mm_rows_nt("attn_out_bwd", dx3[1], w_o, out_dtype=BF16)
    dw_o = _mm_tn("attn_out_grad", att, dx3[1])
    dq, dkv_own, dkv_before, dsink = _attn_bwd(q, kv, att, datt, _pinned(attn_sinks, going_ffn1[4]))
    tiles = dkv_before.shape[0]
    dkv = dkv_own.reshape(tiles, s // tiles, kvd)
    dkv = jnp.concatenate([dkv[:, :-WINDOW], dkv[:, -WINDOW:] + jnp.pad(dkv_before[1:], ((0, 1), (0, 0), (0, 0)))], axis=1).reshape(s, kvd)
    dw_q = _mm_tn("q_proj_grad", a2, dq)
    dw_kv = _mm_tn("kv_proj_grad", akv, dkv)
    going_attn = scatter("attn", [dw_kv.reshape(N_DEV, rows, kvd), dw_q.reshape(N_DEV, rows, d), dw_o.reshape(N_DEV, rows, d)])
    whole = lambda a_ref, b_ref: [(a_ref[...], b_ref[...])]
    rows_of = lambda width: (lambda tile: pl.BlockSpec((tile, width), lambda i: (i, 0)))
    dx2, (d_kvn, d_attn) = _proj_norm_bwd("attn_in_bwd", x2, dx3[0], [
        (dkv, rows_of(kvd), w_kvg, pl.BlockSpec((d, kvd), lambda i: (0, 0)), whole, _pinned(kvn, going_attn[4])),
        (dq, rows_of(d), w_q, pl.BlockSpec((d, d), lambda i: (0, 0)), whole, attn_norm)])
    dx1, d_fn0, dw_up0, dw_dn0, dcw0, dcb0 = _ffn_backward("0", x1, gains[0], w_up[0], w_dn[0], conv_w[0], conv_b[0], saved0, dx2)
    dw_out = _mm_tn("hg_out_grad", og, dx1[1])
    going_ffn0 = scatter("ffn0", [dw_up0, dw_dn0.reshape(N_DEV, fs, d), dw_out.reshape(N_DEV, rows, d)])
    dog = _mm_rows_nt("hg_out_bwd", dx1[1], w_out, out_dtype=F32)
    dp, d_lbl, d_ogain = _hgrn2_bwd(p, lbl, _pinned(hg_out_norm, going_ffn0[4]), o, dog, states, gsum)
    dw_in = _matmul(
        "hg_in_grad", a0, dp, dims=TN, grid=(1, N_DEV, 1),
        a_spec=pl.BlockSpec((s, d), lambda i, j, k: (0, 0)),
        b_spec=pl.BlockSpec((None, s, half), lambda i, j, k: (j // 2, 0, j % 2)),
        o_spec=pl.BlockSpec((None, d, half), lambda i, j, k: (j, 0, 0)),
        out_shape=jax.ShapeDtypeStruct((N_DEV, d, half), BF16))
    going_hg = scatter("hg", [dw_in])
    (dx0, _), (d_hgn,) = _proj_norm_bwd("hg_in_bwd", x0, dx1[0], [
        (dp, lambda tile: pl.BlockSpec((4, tile, d), lambda i: (0, i, 0)), w_in, pl.BlockSpec((N_DEV, d, half), lambda i: (0, 0, 0)),
         lambda g_ref, w_ref: [(g_ref[k // 2, :, (k % 2) * half:(k % 2 + 1) * half], w_ref[k]) for k in range(N_DEV)],
         _pinned(hgn, going_hg[4]))])

    as_blocks = lambda a, r: a.reshape(r, N_DEV, -1).transpose(1, 0, 2).reshape(N_DEV * r, -1)
    d_cw = jnp.concatenate([g.transpose(1, 0, 2).reshape(CONV_WIDTH, 4 * fb) for g in (dcw0, dcw1)], axis=0)
    parts = [d_fin, jnp.concatenate([d_fn0, d_fn1], axis=0), jnp.concatenate([dcb0.reshape(1, 4 * fb), dcb1.reshape(1, 4 * fb)], axis=0),
             as_blocks(d_cw, 2 * CONV_WIDTH), d_attn, jnp.sum(dsink[:, :, 0], axis=0).reshape(1, nq), d_kvn, d_ogain,
             as_blocks(d_hgn, 1), as_blocks(d_lbl, 2), loss_part]
    wide = [2]
    packs = [[parts[i] for i in wide], [part for i, part in enumerate(parts) if i not in wide]]
    places = [None] * len(parts)
    for which, members in enumerate([wide, [i for i in range(len(parts)) if i not in wide]]):
        for i, off in zip(members, _pack_rows(packs[which])[0]):
            places[i] = (which, off)
    packed = [_pack("pack_wide_grads", packs[0]), _pack("pack_narrow_grads", packs[1])]
    going_small = _send_start("small_grads_start", packed, [_landing(a, mine) for a in packed], False)

    arrive = lambda tag, going, after: _send_wait("scatter_wait_" + tag, going, after, True)
    (l_up1, l_dn1), (l_kv, l_q, l_o), (l_up0, l_dn0, l_out) = (
        arrive("ffn1", going_ffn1, going_small[4]), arrive("attn", going_attn, going_small[4]), arrive("ffn0", going_ffn0, going_small[4]))
    big = {}
    for tag, w, m, v, part in [
            ("w_kv", w_kv, m_w_kv, v_w_kv, l_kv), ("attn_w_q", attn_w_q[0], m_attn_w_q[0], v_attn_w_q[0], l_q),
            ("attn_w_o", attn_w_o[0], m_attn_w_o[0], v_attn_w_o[0], l_o)]:
        big[tag] = _adamw_shard("adamw_" + tag, w, m, v, part)
    big["ffn_w_up"] = [up_t(a) for a in _adamw_layers("adamw_ffn_w_up", up_t(ffn_w_up), up_t(m_ffn_w_up), up_t(v_ffn_w_up), (l_up0, l_up1))]
    big["ffn_w_down"] = _adamw_layers("adamw_ffn_w_down", ffn_w_down, m_ffn_w_down, v_ffn_w_down, (l_dn0, l_dn1))
    lead = lambda tag: [a[None] for a in big[tag]]

    both_done = big["ffn_w_up"][0][0, 0:1, 0:1] + big["ffn_w_down"][0][0, 0:1, 0:1]
    gathered = _send_wait("small_grads_wait", going_small, both_done, False)
    two = lambda a: a.reshape(-1, a.shape[-1])
    small = [(fin, m_final_norm.reshape(1, d), v_final_norm.reshape(1, d), False), (ffn_norm, m_ffn_norm, v_ffn_norm, False),
             (ffn_conv_b, m_ffn_conv_b, v_ffn_conv_b, False), (two(ffn_conv_w), two(m_ffn_conv_w), two(v_ffn_conv_w), True),
             (attn_norm, m_attn_norm, v_attn_norm, False), (attn_sinks, m_attn_sinks, v_attn_sinks, False),
             (kvn, m_kv_norm.reshape(1, d), v_kv_norm.reshape(1, d), False), (hg_out_norm, m_hg_out_norm, v_hg_out_norm, False),
             (hg_norm, m_hg_norm, v_hg_norm, True), (hg_lb_logits, m_hg_lb_logits, v_hg_lb_logits, True)]
    res = _adamw_small(gathered, places, small)
    l_in, = arrive("hg", going_hg, gathered[1])
    big["hg_w_in"] = _adamw_shard("adamw_hg_w_in", hg_w_in[0], m_hg_w_in[0], v_hg_w_in[0], l_in)
    big["hg_w_out"] = _adamw_shard("adamw_hg_w_out", hg_w_out[0], m_hg_w_out[0], v_hg_w_out[0], l_out)
    names = ["final_norm", "ffn_norm", "ffn_conv_b", "ffn_conv_w", "attn_norm", "attn_sinks", "kv_norm", "hg_out_norm", "hg_norm", "hg_lb_logits"]
    shapes = {"final_norm": final_norm.shape, "kv_norm": kv_norm.shape, "ffn_conv_w": ffn_conv_w.shape}
    out = {n: [a.reshape(shapes[n]) if n in shapes else a for a in res[4 * i:4 * i + 4]] for i, n in enumerate(names)}
    out.update(hg_w_in=lead("hg_w_in"), hg_w_out=lead("hg_w_out"), w_kv=big["w_kv"], attn_w_q=lead("attn_w_q"), attn_w_o=lead("attn_w_o"),
               ffn_w_up=big["ffn_w_up"], ffn_w_down=big["ffn_w_down"])
    order = ["hg_norm", "hg_w_in", "hg_lb_logits", "hg_out_norm", "hg_w_out", "kv_norm", "w_kv", "attn_norm", "attn_w_q", "attn_sinks",
             "attn_w_o", "ffn_norm", "ffn_w_up", "ffn_conv_w", "ffn_conv_b", "ffn_w_down", "final_norm"]
    loss = res[-1][0, 0]
    return (loss, dx0[None], *[out[n][0] for n in order], *[out[n][1] for n in order], *[out[n][2] for n in order], *[out[n][3] for n in order])
```

```python
import functools

import jax
import jax.numpy as jnp
from jax import lax
from jax.experimental import pallas as pl
from jax.experimental.pallas import tpu as pltpu

F32 = jnp.float32
BF16 = jnp.bfloat16

EPS = 1e-6
HG_EXPAND = 128
HG_CHUNK = 32
ATT_HEAD_DIM = 64
ATT_KV_HEADS = 2
WINDOW = 128
CONV_WIDTH = 3
ADAM_LR = 0.001
ADAM_B1 = 0.9
ADAM_B2 = 0.999
ADAM_EPS = 1e-08
ADAM_WD = 0.01
ADAM_STEP = 10

N_DEV = 8
VMEM_LIMIT = 48 * 1024 * 1024
NEG = -1e30

NN = (((1,), (0,)), ((), ()))
NT = (((1,), (1,)), ((), ()))
TN = (((0,), (0,)), ((), ()))
MESH = pl.DeviceIdType.MESH


def _dot(a, b, dims=NN):
    return lax.dot_general(a.astype(BF16), b.astype(BF16), dims, preferred_element_type=F32)


def _sigmoid(x):
    return 0.5 * jnp.tanh(0.5 * x) + 0.5


def _silu(x):
    return x * _sigmoid(x)


def _silu_and_grad(x):
    s = _sigmoid(x)
    return x * s, s * (1.0 + x * (1.0 - s))


def _dsilu(x):
    return _silu_and_grad(x)[1]


def _params(semantics):
    return pltpu.CompilerParams(dimension_semantics=semantics, vmem_limit_bytes=VMEM_LIMIT)


def _row_tile(rows, want=512):
    return min(rows, want)


MM_ROWS = 1024


def _matmul(name, a, b, *, dims, grid, a_spec, b_spec, o_spec, out_shape, add=None, add_spec=None, after=None):
    assert grid[2] == 1

    def body(*refs):
        a_ref, b_ref, o_ref = refs[0], refs[1], refs[-1]
        total = _dot(a_ref[...], b_ref[...], dims)
        if add is not None:
            total = total + refs[2][...]
        o_ref[...] = total.astype(o_ref.dtype)

    in_specs = [a_spec, b_spec] + ([] if add is None else [add_spec]) + ([] if after is None else [pl.BlockSpec(memory_space=pl.ANY)])
    args = (a, b) + (() if add is None else (add,)) + (() if after is None else (after,))
    return pl.pallas_call(
        body, name=name, grid=grid, in_specs=in_specs, out_specs=o_spec, out_shape=out_shape,
        compiler_params=_params(("parallel", "parallel", "arbitrary")),
    )(*args)


def _mm_rows(name, a, w, *, out_dtype, add=None, after=None):
    s, kdim = a.shape
    n = w.shape[1]
    tm = _row_tile(s, MM_ROWS)
    return _matmul(
        name, a, w, dims=NN, grid=(s // tm, 1, 1),
        a_spec=pl.BlockSpec((tm, kdim), lambda i, j, k: (i, 0)),
        b_spec=pl.BlockSpec((kdim, n), lambda i, j, k: (0, 0)),
        o_spec=pl.BlockSpec((tm, n), lambda i, j, k: (i, 0)),
        out_shape=jax.ShapeDtypeStruct((s, n), out_dtype),
        add=add, add_spec=None if add is None else pl.BlockSpec((tm, n), lambda i, j, k: (i, 0)), after=after,
    )


def _mm_rows_nt(name, a, w, *, out_dtype):
    s, n = a.shape
    kdim = w.shape[0]
    tm = _row_tile(s, MM_ROWS)
    return _matmul(
        name, a, w, dims=NT, grid=(s // tm, 1, 1),
        a_spec=pl.BlockSpec((tm, n), lambda i, j, k: (i, 0)),
        b_spec=pl.BlockSpec((kdim, n), lambda i, j, k: (0, 0)),
        o_spec=pl.BlockSpec((tm, kdim), lambda i, j, k: (i, 0)),
        out_shape=jax.ShapeDtypeStruct((s, kdim), out_dtype),
    )


def _mm_tn(name, a, g):
    s, m = a.shape
    n = g.shape[1]
    tn = min(n, 512)
    return _matmul(
        name, a, g, dims=TN, grid=(1, n // tn, 1),
        a_spec=pl.BlockSpec((s, m), lambda i, j, k: (0, 0)),
        b_spec=pl.BlockSpec((s, tn), lambda i, j, k: (0, j)),
        o_spec=pl.BlockSpec((m, tn), lambda i, j, k: (0, j)),
        out_shape=jax.ShapeDtypeStruct((m, n), BF16),
    )


NORM_ROWS = 256


def _norm_proj(name, h, branches, *, tm, nj):
    s, d = h.shape
    n = len(branches)
    rows = min(tm, NORM_ROWS)

    def body(*refs):
        h_ref, gain_refs, w_refs = refs[0], refs[1:1 + n], refs[1 + n:1 + 2 * n]
        a_refs, o_refs = refs[1 + 2 * n:1 + 3 * n], refs[1 + 3 * n:]

        @pl.when(pl.program_id(1) == 0)
        def _():
            def normalize(c, carry):
                at = pl.ds(pl.multiple_of(c * rows, rows), rows)
                xv = h_ref[at, :]
                xhat = xv * lax.rsqrt(jnp.mean(xv * xv, axis=-1, keepdims=True) + EPS)
                for gain_ref, a_ref in zip(gain_refs, a_refs):
                    a_ref[at, :] = (xhat * gain_ref[...]).astype(BF16)
                return carry

            lax.fori_loop(0, tm // rows, normalize, 0)

        for branch, w_ref, a_ref, o_ref in zip(branches, w_refs, a_refs, o_refs):
            o_ref[...] = _dot(a_ref[...], w_ref[...], branch[3]).astype(o_ref.dtype)

    row = pl.BlockSpec((tm, d), lambda i, j: (i, 0))
    vec = pl.BlockSpec((1, d), lambda i, j: (0, 0))
    outs = pl.pallas_call(
        body, name=name, grid=(s // tm, nj), in_specs=[row] + [vec] * n + [b[2] for b in branches],
        out_specs=[row] * n + [b[4] for b in branches],
        out_shape=[jax.ShapeDtypeStruct((s, d), BF16)] * n + [b[5] for b in branches],
        compiler_params=_params(("parallel", "arbitrary")),
    )(h, *[b[0] for b in branches], *[b[1] for b in branches])
    return outs[:n], outs[n:]


def _proj_norm_bwd(name, h, dres, branches):
    s, d = h.shape
    tm = _row_tile(s)
    n = len(branches)

    def body(*refs):
        h_ref, dres_ref = refs[0], refs[1]
        g_refs, w_refs, gain_refs = refs[2:2 + n], refs[2 + n:2 + 2 * n], refs[2 + 2 * n:2 + 3 * n]
        dh_ref, dhb_ref, dg_refs = refs[2 + 3 * n], refs[3 + 3 * n], refs[4 + 3 * n:]
        i = pl.program_id(0)
        xv = h_ref[...]
        r = lax.rsqrt(jnp.mean(xv * xv, axis=-1, keepdims=True) + EPS)
        xhat = xv * r
        total = dres_ref[...]
        for branch, g_ref, w_ref, gain_ref, dg_ref in zip(branches, g_refs, w_refs, gain_refs, dg_refs):
            pairs = branch[4](g_ref, w_ref)
            da = _dot(*pairs[0], NT)
            for pair in pairs[1:]:
                da = da + _dot(*pair, NT)
            dgain = jnp.sum(da * xhat, axis=0, keepdims=True)

            @pl.when(i == 0)
            def _():
                dg_ref[...] = dgain

            @pl.when(i > 0)
            def _():
                dg_ref[...] += dgain

            dxhat = da * gain_ref[...]
            total = total + r * (dxhat - xhat * jnp.mean(dxhat * xhat, axis=-1, keepdims=True))
        dh_ref[...] = total
        dhb_ref[...] = total.astype(BF16)

    row = pl.BlockSpec((tm, d), lambda i: (i, 0))
    vec = pl.BlockSpec((1, d), lambda i: (0, 0))
    outs = pl.pallas_call(
        body, name=name, grid=(s // tm,),
        in_specs=[row, row] + [b[1](tm) for b in branches] + [b[3] for b in branches] + [vec] * n, out_specs=[row, row] + [vec] * n,
        out_shape=[jax.ShapeDtypeStruct((s, d), F32), jax.ShapeDtypeStruct((s, d), BF16)] + [jax.ShapeDtypeStruct((1, d), F32)] * n,
        compiler_params=_params(("arbitrary",)),
    )(h, dres, *[b[0] for b in branches], *[b[2] for b in branches], *[b[5] for b in branches])
    return (outs[0], outs[1]), outs[2:]


def _loss_head(h, gain, target):
    s, d = h.shape
    tm = _row_tile(s)

    def body(h_ref, g_ref, t_ref, dh_ref, dhb_ref, dg_ref, loss_ref):
        i = pl.program_id(0)
        xv = h_ref[...]
        r = lax.rsqrt(jnp.mean(xv * xv, axis=-1, keepdims=True) + EPS)
        xhat = xv * r
        err = xhat * g_ref[...] - t_ref[...]
        dy = err * (1.0 / d)
        part = jnp.zeros((1, 128), F32) + 0.5 * jnp.sum(jnp.mean(err * err, axis=-1, keepdims=True))
        dgain = jnp.sum(dy * xhat, axis=0, keepdims=True)

        @pl.when(i == 0)
        def _():
            dg_ref[...] = dgain
            loss_ref[...] = part

        @pl.when(i > 0)
        def _():
            dg_ref[...] += dgain
            loss_ref[...] += part

        dxhat = dy * g_ref[...]
        dh = r * (dxhat - xhat * jnp.mean(dxhat * xhat, axis=-1, keepdims=True))
        dh_ref[...] = dh
        dhb_ref[...] = dh.astype(BF16)

    row = pl.BlockSpec((tm, d), lambda i: (i, 0))
    vec = pl.BlockSpec((1, d), lambda i: (0, 0))
    return pl.pallas_call(
        body, name="loss_head", grid=(s // tm,), in_specs=[row, vec, row],
        out_specs=[row, row, vec, pl.BlockSpec((1, 128), lambda i: (0, 0))],
        out_shape=[jax.ShapeDtypeStruct((s, d), F32), jax.ShapeDtypeStruct((s, d), BF16), jax.ShapeDtypeStruct((1, d), F32),
                   jax.ShapeDtypeStruct((1, 128), F32)],
        compiler_params=_params(("arbitrary",)),
    )(h, gain, target)


def _bdot(a, b, ca, cb):
    return lax.dot_general(a.astype(BF16), b.astype(BF16), (((ca,), (cb,)), ((0,), (0,))), preferred_element_type=F32)


def _chunk_cumsum(xv, reverse=False):
    n = xv.shape[0]
    row = lax.broadcasted_iota(jnp.int32, xv.shape, 0) % HG_CHUNK
    step = 1
    while step < HG_CHUNK:
        if reverse:
            xv = xv + jnp.where(row < HG_CHUNK - step, pltpu.roll(xv, n - step, axis=0), 0.0)
        else:
            xv = xv + jnp.where(row >= step, pltpu.roll(xv, step, axis=0), 0.0)
        step *= 2
    return xv


def _hg_terms(p_ref, lbl_ref, g_ref=None):
    pq = p_ref[0].astype(F32)
    pf = p_ref[1].astype(F32)
    lb = _sigmoid(lbl_ref[0:1, :] - lbl_ref[1:2, :])
    sig = _sigmoid(pf)
    fg = lb + (1.0 - lb) * sig
    nc = pq.shape[0] // HG_CHUNK
    chunks = lambda a: a.reshape(nc, HG_CHUNK, HG_EXPAND)
    q = chunks(_silu(pq) * HG_EXPAND ** -0.5)
    k = chunks(1.0 - fg)
    v = chunks(p_ref[2].astype(F32))
    g = chunks(_chunk_cumsum(jnp.log(fg)) if g_ref is None else g_ref[...])
    gm = g[:, HG_CHUNK // 2 - 1:HG_CHUNK // 2, :]
    gl = g[:, HG_CHUNK - 1:HG_CHUNK, :]
    e_mid, e_inv, e_all, e_end = jnp.exp(g - gm), jnp.exp(gm - g), jnp.exp(g), jnp.exp(gl - g)
    terms = dict(q=q, k=k, v=v, g=g, qd=q * e_all, qt=q * e_mid, kt=k * e_inv, kd=k * e_end, e_last=jnp.exp(gl),
                 e_mid=e_mid, e_inv=e_inv, e_all=e_all, e_end=e_end)
    return terms, (pq, sig, fg, lb)


def _causal(nc):
    r = lax.broadcasted_iota(jnp.int32, (nc, HG_CHUNK, HG_CHUNK), 1)
    c = lax.broadcasted_iota(jnp.int32, (nc, HG_CHUNK, HG_CHUNK), 2)
    return r >= c


def _hgrn2_fwd(p, lb_logits, out_gain):
    _, s, d = p.shape
    heads = d // HG_EXPAND
    t = _row_tile(s, 2048)
    nc = t // HG_CHUNK

    def body(p_ref, lbl_ref, gain_ref, o_ref, og_ref, st_ref, g_ref, state, decay):
        @pl.when(pl.program_id(1) == 0)
        def _():
            state[...] = jnp.zeros_like(state)

        tm, _ = _hg_terms(p_ref, lbl_ref)
        g_ref[...] = tm["g"].reshape(t, HG_EXPAND)
        decay[...] = tm["e_last"]
        st_ref[...] = _bdot(tm["v"], tm["kd"], 1, 1)

        def chunk(c, carry):
            add = st_ref[c]
            st = state[...]
            st_ref[c] = st
            state[...] = st * decay[c] + add
            return carry

        lax.fori_loop(0, nc, chunk, 0)
        a = jnp.where(_causal(nc), _bdot(tm["qt"], tm["kt"], 2, 2), 0.0)
        ov = (_bdot(tm["qd"], st_ref[...], 2, 2) + _bdot(a, tm["v"], 2, 1)).reshape(t, HG_EXPAND)
        o_ref[...] = ov
        on = ov * lax.rsqrt(jnp.mean(ov * ov, axis=-1, keepdims=True) + EPS) * gain_ref[...]
        og_ref[...] = (on * _silu(p_ref[3].astype(F32))).astype(BF16)

    blk = pl.BlockSpec((t, HG_EXPAND), lambda h, b: (b, h))
    return pl.pallas_call(
        body, name="hgrn2_fwd", grid=(heads, s // t),
        in_specs=[pl.BlockSpec((4, t, HG_EXPAND), lambda h, b: (0, b, h)), pl.BlockSpec((2, HG_EXPAND), lambda h, b: (0, h)),
                  pl.BlockSpec((1, HG_EXPAND), lambda h, b: (0, 0))],
        out_specs=[blk, blk, pl.BlockSpec((None, nc, HG_EXPAND, HG_EXPAND), lambda h, b: (h, b, 0, 0)), blk],
        out_shape=[jax.ShapeDtypeStruct((s, d), F32), jax.ShapeDtypeStruct((s, d), BF16),
                   jax.ShapeDtypeStruct((heads, s // HG_CHUNK, HG_EXPAND, HG_EXPAND), F32), jax.ShapeDtypeStruct((s, d), F32)],
        scratch_shapes=[pltpu.VMEM((HG_EXPAND, HG_EXPAND), F32), pltpu.VMEM((nc, 1, HG_EXPAND), F32)],
        compiler_params=_params(("parallel", "arbitrary")),
    )(p, lb_logits, out_gain)


def _hgrn2_bwd(p, lb_logits, out_gain, o, dog, states, gsum):
    _, s, d = p.shape
    heads = d // HG_EXPAND
    t = _row_tile(s, 1024)
    nc = t // HG_CHUNK
    nb = s // t

    def body(p_ref, lbl_ref, gain_ref, o_ref, dog_ref, st_ref, g_ref, dp_ref, dlbl_ref, dgain_ref, dstate, decay, dst_s):
        h, b = pl.program_id(0), pl.program_id(1)

        @pl.when(b == 0)
        def _():
            dstate[...] = jnp.zeros_like(dstate)

        tm, (pq, sig, fg, lb) = _hg_terms(p_ref, lbl_ref, g_ref)
        pg = p_ref[3].astype(F32)
        ov = o_ref[...]
        r = lax.rsqrt(jnp.mean(ov * ov, axis=-1, keepdims=True) + EPS)
        ohat = ov * r
        dogv = dog_ref[...]
        d_on = dogv * _silu(pg)
        dp_ref[3] = (dogv * ohat * gain_ref[...] * _dsilu(pg)).astype(BF16)
        dgain = jnp.sum(d_on * ohat, axis=0, keepdims=True)

        @pl.when((h == 0) & (b == 0))
        def _():
            dgain_ref[...] = dgain

        @pl.when((h > 0) | (b > 0))
        def _():
            dgain_ref[...] += dgain

        dohat = d_on * gain_ref[...]
        do = (r * (dohat - ohat * jnp.mean(dohat * ohat, axis=-1, keepdims=True))).reshape(nc, HG_CHUNK, HG_EXPAND)

        decay[...] = tm["e_last"]
        dst_s[...] = _bdot(do, tm["qd"], 1, 1)

        def chunk(i, carry):
            c = nc - 1 - i
            add = dst_s[c]
            dst = dstate[...]
            dst_s[c] = dst
            dstate[...] = dst * decay[c] + add
            return carry

        lax.fori_loop(0, nc, chunk, 0)
        st, dst = st_ref[...], dst_s[...]
        causal = _causal(nc)
        a = jnp.where(causal, _bdot(tm["qt"], tm["kt"], 2, 2), 0.0)
        da = jnp.where(causal, _bdot(do, tm["v"], 2, 2), 0.0)
        dqt = _bdot(da, tm["kt"], 2, 1)
        dkt = _bdot(da, tm["qt"], 1, 1)
        dqd = _bdot(do, st, 2, 1)
        dkd = _bdot(tm["v"], dst, 2, 1)
        dv = _bdot(a, do, 1, 1) + _bdot(tm["kd"], dst, 2, 2)
        dq = dqt * tm["e_mid"] + dqd * tm["e_all"]
        dk = dkt * tm["e_inv"] + dkd * tm["e_end"]
        dg = dqt * tm["qt"] - dkt * tm["kt"] + dqd * tm["qd"] - dkd * tm["kd"]
        dgl = jnp.sum(dkd * tm["kd"], axis=1, keepdims=True) + tm["e_last"] * jnp.sum(dst * st, axis=1, keepdims=True)
        last_row = lax.broadcasted_iota(jnp.int32, (nc, HG_CHUNK, HG_EXPAND), 1) == HG_CHUNK - 1
        flat = lambda a3: a3.reshape(t, HG_EXPAND)
        dlf = _chunk_cumsum(flat(dg + jnp.where(last_row, dgl, 0.0)), reverse=True)
        dfg = dlf / fg - flat(dk)
        dlb = jnp.sum(dfg * (1.0 - sig), axis=0, keepdims=True)
        dl0 = dlb * lb * (1.0 - lb)
        dlbl = jnp.concatenate([dl0, -dl0], axis=0)

        @pl.when(b == 0)
        def _():
            dlbl_ref[...] = dlbl

        @pl.when(b > 0)
        def _():
            dlbl_ref[...] += dlbl

        dp_ref[0] = (flat(dq) * HG_EXPAND ** -0.5 * _dsilu(pq)).astype(BF16)
        dp_ref[1] = (dfg * (1.0 - lb) * sig * (1.0 - sig)).astype(BF16)
        dp_ref[2] = flat(dv).astype(BF16)

    blk = pl.BlockSpec((t, HG_EXPAND), lambda h, b: (nb - 1 - b, h))
    pblk = pl.BlockSpec((4, t, HG_EXPAND), lambda h, b: (0, nb - 1 - b, h))
    return pl.pallas_call(
        body, name="hgrn2_bwd", grid=(heads, nb),
        in_specs=[pblk, pl.BlockSpec((2, HG_EXPAND), lambda h, b: (0, h)), pl.BlockSpec((1, HG_EXPAND), lambda h, b: (0, 0)),
                  blk, blk, pl.BlockSpec((None, nc, HG_EXPAND, HG_EXPAND), lambda h, b: (h, nb - 1 - b, 0, 0)), blk],
        out_specs=[pblk, pl.BlockSpec((2, HG_EXPAND), lambda h, b: (0, h)), pl.BlockSpec((1, HG_EXPAND), lambda h, b: (0, 0))],
        out_shape=[jax.ShapeDtypeStruct((4, s, d), BF16), jax.ShapeDtypeStruct((2, d), F32), jax.ShapeDtypeStruct((1, HG_EXPAND), F32)],
        scratch_shapes=[pltpu.VMEM((HG_EXPAND, HG_EXPAND), F32), pltpu.VMEM((nc, 1, HG_EXPAND), F32),
                        pltpu.VMEM((nc, HG_EXPAND, HG_EXPAND), F32)],
        compiler_params=_params(("arbitrary", "arbitrary")),
    )(p, lb_logits, out_gain, o, dog, states, gsum)


HALO = 8
FFN_FWD_ROWS = 512
FFN_BWD_ROWS = 256


def _shift_down(xv, n):
    return pltpu.roll(xv, n, axis=0)


def _shift_up(xv, n):
    return pltpu.roll(xv, xv.shape[0] - n, axis=0)


def _ffn_hidden_down(name, u, conv_w, conv_b, w_down, h):
    _, nj, s, fb = u.shape
    d = w_down.shape[2]
    tm = _row_tile(s, FFN_FWD_ROWS)
    per = tm // HALO

    def body(gate_ref, prev_ref, val_ref, w_ref, b_ref, wd_ref, h_ref, hid_ref, conv_ref, o_ref):
        i = pl.program_id(0)
        total = h_ref[...]
        for j in range(nj):
            prev = jnp.where(i > 0, prev_ref[j].astype(F32), 0.0)
            ext = jnp.concatenate([prev, gate_ref[j].astype(F32)], axis=0)
            conv = b_ref[j] + w_ref[j, 2:3, :] * ext[HALO:]
            conv = conv + w_ref[j, 1:2, :] * _shift_down(ext, 1)[HALO:]
            conv = conv + w_ref[j, 0:1, :] * _shift_down(ext, 2)[HALO:]
            conv = conv.astype(BF16)
            conv_ref[j] = conv
            hidden = _silu(conv) * val_ref[j]
            hid_ref[j] = hidden
            total = total + _dot(hidden, wd_ref[j])
        o_ref[...] = total

    row = pl.BlockSpec((tm, d), lambda i: (i, 0))
    return pl.pallas_call(
        body, name=name, grid=(s // tm,),
        in_specs=[pl.BlockSpec((None, nj, tm, fb), lambda i: (0, 0, i, 0)),
                  pl.BlockSpec((None, nj, HALO, fb), lambda i: (0, 0, jnp.maximum(i * per - 1, 0), 0)),
                  pl.BlockSpec((None, nj, tm, fb), lambda i: (1, 0, i, 0)),
                  pl.BlockSpec((nj, CONV_WIDTH, fb), lambda i: (0, 0, 0)), pl.BlockSpec((nj, 1, fb), lambda i: (0, 0, 0)),
                  pl.BlockSpec((nj, fb, d), lambda i: (0, 0, 0)), row],
        out_specs=[pl.BlockSpec((nj, tm, fb), lambda i: (0, i, 0)), pl.BlockSpec((nj, tm, fb), lambda i: (0, i, 0)), row],
        out_shape=[jax.ShapeDtypeStruct((nj, s, fb), BF16), jax.ShapeDtypeStruct((nj, s, fb), BF16), jax.ShapeDtypeStruct((s, d), F32)],
        compiler_params=_params(("parallel",)),
    )(u, u, u, conv_w, conv_b, w_down, h)


def _ffn_hidden_up_bwd(name, u, conv, dh, conv_w, w_up, h, gain, dres):
    _, nj, s, fb = u.shape
    d = w_up.shape[2]
    tm = _row_tile(s, FFN_BWD_ROWS)
    per = tm // HALO
    nblk = s // HALO
    ni = s // tm

    def body(gate_ref, conv_ref, cnext_ref, val_ref, vnext_ref, dh_ref, dhnext_ref, w_ref, wu_ref, h_ref, gain_ref, dres_ref,
             du_ref, dw_ref, db_ref, dx_ref, dxb_ref, dgain_ref):
        i = pl.program_id(0)
        has_next = i < ni - 1
        total = None
        for j in range(nj):
            act, dact = _silu_and_grad(conv_ref[j])
            dval = dh_ref[j] * act
            after = jnp.where(has_next, dhnext_ref[j].astype(F32), 0.0) * vnext_ref[j].astype(F32) * _dsilu(cnext_ref[j].astype(F32))
            dconv = jnp.concatenate([(dh_ref[j] * val_ref[j] * dact).astype(F32), after], axis=0)
            taps = [_shift_up(dconv, 2)[:tm], _shift_up(dconv, 1)[:tm], dconv[:tm]]
            dgate = (w_ref[j, 0:1, :] * taps[0] + w_ref[j, 1:2, :] * taps[1] + w_ref[j, 2:3, :] * taps[2]).astype(BF16)
            du_ref[0, j] = dgate
            du_ref[1, j] = dval
            part = _dot(dgate, wu_ref[j]) + _dot(dval, wu_ref[nj + j])
            total = part if total is None else total + part
            gate = gate_ref[j].astype(F32)
            dw = jnp.concatenate([jnp.sum(tap * gate, axis=0, keepdims=True) for tap in taps], axis=0)
            db = jnp.sum(taps[2], axis=0, keepdims=True)

            @pl.when(i == 0)
            def _():
                dw_ref[j] = dw
                db_ref[j] = db

            @pl.when(i > 0)
            def _():
                dw_ref[j] += dw
                db_ref[j] += db

        xv = h_ref[...]
        r = lax.rsqrt(jnp.mean(xv * xv, axis=-1, keepdims=True) + EPS)
        xhat = xv * r
        dgain = jnp.sum(total * xhat, axis=0, keepdims=True)

        @pl.when(i == 0)
        def _():
            dgain_ref[...] = dgain

        @pl.when(i > 0)
        def _():
            dgain_ref[...] += dgain

        dxhat = total * gain_ref[...]
        dx = dres_ref[...] + r * (dxhat - xhat * jnp.mean(dxhat * xhat, axis=-1, keepdims=True))
        dx_ref[...] = dx
        dxb_ref[...] = dx.astype(BF16)

    def tile(part):
        return pl.BlockSpec((None, nj, tm, fb), lambda i: (part, 0, i, 0))

    def after(part):
        return pl.BlockSpec((None, nj, HALO, fb), lambda i: (part, 0, jnp.minimum((i + 1) * per, nblk - 1), 0))

    row = pl.BlockSpec((tm, d), lambda i: (i, 0))
    own = pl.BlockSpec((nj, tm, fb), lambda i: (0, i, 0))
    nxt = pl.BlockSpec((nj, HALO, fb), lambda i: (0, jnp.minimum((i + 1) * per, nblk - 1), 0))
    return pl.pallas_call(
        body, name=name, grid=(ni,),
        in_specs=[tile(0), own, nxt, tile(1), after(1), own, nxt,
                  pl.BlockSpec((nj, CONV_WIDTH, fb), lambda i: (0, 0, 0)),
                  pl.BlockSpec((2 * nj, fb, d), lambda i: (0, 0, 0)), row, pl.BlockSpec((1, d), lambda i: (0, 0)), row],
        out_specs=[pl.BlockSpec((2, nj, tm, fb), lambda i: (0, 0, i, 0)),
                   pl.BlockSpec((nj, CONV_WIDTH, fb), lambda i: (0, 0, 0)), pl.BlockSpec((nj, 1, fb), lambda i: (0, 0, 0)),
                   row, row, pl.BlockSpec((1, d), lambda i: (0, 0))],
        out_shape=[jax.ShapeDtypeStruct((2, nj, s, fb), BF16), jax.ShapeDtypeStruct((nj, CONV_WIDTH, fb), F32),
                   jax.ShapeDtypeStruct((nj, 1, fb), F32), jax.ShapeDtypeStruct((s, d), F32), jax.ShapeDtypeStruct((s, d), BF16),
                   jax.ShapeDtypeStruct((1, d), F32)],
        compiler_params=_params(("arbitrary",)),
    )(u, conv, conv, u, u, dh, dh, conv_w, w_up, h, gain, dres)


ATT_TILE = 512


def _stack_heads(ref, rows, first_head, count):
    hd = ATT_HEAD_DIM
    return jnp.concatenate([ref[rows, (first_head + j) * hd:(first_head + j + 1) * hd] for j in range(count)], axis=0)


def _unstack_heads(stacked, ref, rows, first_head, count):
    hd = ATT_HEAD_DIM
    for pair in range(count // 2):
        both = [stacked[(2 * pair + j) * WINDOW:(2 * pair + j + 1) * WINDOW, :] for j in range(2)]
        ref[rows, (first_head + 2 * pair) * hd:(first_head + 2 * pair + 2) * hd] = jnp.concatenate(both, axis=1).astype(ref.dtype)


def _attn_bias(first_head, count, n_heads, first):
    lanes = count * WINDOW
    ik = lax.broadcasted_iota(jnp.int32, (2 * WINDOW, lanes), 0)
    iq = lax.broadcasted_iota(jnp.int32, (2 * WINDOW, lanes), 1) % WINDOW
    dist = iq + WINDOW - ik
    valid = (dist >= 0) & (dist < WINDOW) & (ik >= (WINDOW if first else 0))
    slope = jnp.concatenate([jnp.zeros((1, WINDOW), F32) + 2.0 ** (-8.0 * (first_head + j + 1) / n_heads) for j in range(count)], axis=1)
    return jnp.where(valid, -slope * dist.astype(F32), NEG)


def _fill_attn_bias(bias_ref, group, n_heads):
    @pl.when(pl.program_id(0) == 0)
    def _():
        for g in range(ATT_KV_HEADS):
            bias_ref[0, g] = _attn_bias(g * group, group, n_heads, False)
            bias_ref[1, g] = _attn_bias(g * group, group, n_heads, True)


def _attn_probs_t(kb_scaled, qs, sink_ref, first_head, count, bias):
    sink = jnp.concatenate([jnp.zeros((1, WINDOW), F32) + sink_ref[0, first_head + j] for j in range(count)], axis=1)
    sc = _dot(kb_scaled, qs, NT) + bias
    m = jnp.maximum(jnp.max(sc, axis=0, keepdims=True), sink)
    e = jnp.exp(sc - m)
    es = jnp.exp(sink - m)
    inv = 1.0 / (jnp.sum(e, axis=0, keepdims=True) + es)
    return e * inv, es * inv


ATT_SCALE = ATT_HEAD_DIM ** -0.5


def _attn_specs(s, d, kvd, tq):
    per = tq // WINDOW
    return [pl.BlockSpec((tq, d), lambda i: (i, 0)), pl.BlockSpec((tq, kvd), lambda i: (i, 0)),
            pl.BlockSpec((WINDOW, kvd), lambda i: (jnp.maximum(i * per - 1, 0), 0))]


def _attn_fwd(q, kv, sinks):
    s, d = q.shape
    kvd = kv.shape[1]
    half = kvd // 2
    hd = ATT_HEAD_DIM
    nq = d // hd
    group = nq // ATT_KV_HEADS
    tq = min(s, ATT_TILE)
    per = tq // WINDOW

    def body(q_ref, kvc_ref, kvp_ref, sink_ref, o_ref, band, bias_ref):
        i = pl.program_id(0)
        _fill_attn_bias(bias_ref, group, nq)
        band[0:WINDOW, :] = kvp_ref[...]
        band[WINDOW:, :] = kvc_ref[...]

        def block(b, carry):
            rows = pl.ds(pl.multiple_of(b * WINDOW, WINDOW), WINDOW)
            keys = pl.ds(pl.multiple_of(b * WINDOW, WINDOW), 2 * WINDOW)
            first = (i * per + b) == 0
            for g in range(ATT_KV_HEADS):
                bias = jnp.where(first, bias_ref[1, g], bias_ref[0, g])
                p, _ = _attn_probs_t(band[keys, g * hd:(g + 1) * hd] * ATT_SCALE, _stack_heads(q_ref, rows, g * group, group), sink_ref,
                                     g * group, group, bias)
                out_t = _dot(band[keys, half + g * hd:half + (g + 1) * hd], p, TN)
                _unstack_heads(out_t.T, o_ref, rows, g * group, group)
            return carry

        lax.fori_loop(0, per, block, 0)

    return pl.pallas_call(
        body, name="attn_fwd", grid=(s // tq,),
        in_specs=_attn_specs(s, d, kvd, tq) + [pl.BlockSpec(memory_space=pltpu.SMEM)],
        out_specs=pl.BlockSpec((tq, d), lambda i: (i, 0)), out_shape=jax.ShapeDtypeStruct((s, d), BF16),
        scratch_shapes=[pltpu.VMEM((tq + WINDOW, kvd), BF16), pltpu.VMEM((2, ATT_KV_HEADS, 2 * WINDOW, group * WINDOW), F32)],
        compiler_params=_params(("arbitrary",)),
    )(q, kv, kv, sinks)


def _attn_bwd(q, kv, o, do, sinks):
    s, d = q.shape
    kvd = kv.shape[1]
    half = kvd // 2
    hd = ATT_HEAD_DIM
    nq = d // hd
    group = nq // ATT_KV_HEADS
    tq = min(s, ATT_TILE)
    per = tq // WINDOW
    nt = s // tq

    def body(q_ref, kvc_ref, kvp_ref, o_ref, do_ref, sink_ref, dq_ref, dkvc_ref, dkvp_ref, ds_ref, band, dband, bias_ref):
        i = pl.program_id(0)
        _fill_attn_bias(bias_ref, group, nq)
        band[0:WINDOW, :] = kvp_ref[...]
        band[WINDOW:, :] = kvc_ref[...]
        dband[...] = jnp.zeros_like(dband)
        ds_ref[...] = jnp.zeros_like(ds_ref)

        def block(b, carry):
            rows = pl.ds(pl.multiple_of(b * WINDOW, WINDOW), WINDOW)
            keys = pl.ds(pl.multiple_of(b * WINDOW, WINDOW), 2 * WINDOW)
            first = (i * per + b) == 0
            dks, dvs = [], []
            for g in range(ATT_KV_HEADS):
                kb = band[keys, g * hd:(g + 1) * hd] * ATT_SCALE
                vb = band[keys, half + g * hd:half + (g + 1) * hd]
                qs = _stack_heads(q_ref, rows, g * group, group)
                dos = _stack_heads(do_ref, rows, g * group, group)
                p, ps = _attn_probs_t(kb, qs, sink_ref, g * group, group, jnp.where(first, bias_ref[1, g], bias_ref[0, g]))
                prod = dos.astype(F32) * _stack_heads(o_ref, rows, g * group, group).astype(F32)
                dsum = lax.dot_general(jnp.ones((8, hd), F32), prod, NT, precision=lax.Precision.HIGHEST,
                                       preferred_element_type=F32)[0:1, :]
                dsc = p * (_dot(vb, dos, NT) - dsum)
                dvs.append(_dot(p, dos))
                dks.append(_dot(dsc, qs * ATT_SCALE))
                _unstack_heads(_dot(kb, dsc, TN).T, dq_ref, rows, g * group, group)
                gone = ps * dsum
                for j in range(group):
                    ds_ref[g * group + j:g * group + j + 1, :] += jnp.zeros((1, 128), F32) - jnp.sum(gone[:, j * WINDOW:(j + 1) * WINDOW])
            dband[keys, 0:half] += jnp.concatenate(dks, axis=1)
            dband[keys, half:] += jnp.concatenate(dvs, axis=1)
            return carry

        lax.fori_loop(0, per, block, 0)
        dkvp_ref[...] = dband[0:WINDOW, :]
        dkvc_ref[...] = dband[WINDOW:, :]

    big = pl.BlockSpec((tq, d), lambda i: (i, 0))
    return pl.pallas_call(
        body, name="attn_bwd", grid=(nt,),
        in_specs=_attn_specs(s, d, kvd, tq) + [big, big, pl.BlockSpec(memory_space=pltpu.SMEM)],
        out_specs=[big, pl.BlockSpec((tq, kvd), lambda i: (i, 0)), pl.BlockSpec((None, WINDOW, kvd), lambda i: (i, 0, 0)),
                   pl.BlockSpec((None, nq, 128), lambda i: (i, 0, 0))],
        out_shape=[jax.ShapeDtypeStruct((s, d), BF16), jax.ShapeDtypeStruct((s, kvd), F32), jax.ShapeDtypeStruct((nt, WINDOW, kvd), F32),
                   jax.ShapeDtypeStruct((nt, nq, 128), F32)],
        scratch_shapes=[pltpu.VMEM((tq + WINDOW, kvd), BF16), pltpu.VMEM((tq + WINDOW, kvd), F32),
                        pltpu.VMEM((2, ATT_KV_HEADS, 2 * WINDOW, group * WINDOW), F32)],
        compiler_params=_params(("arbitrary",)),
    )(q, kv, kv, o, do, sinks)


HBM_SPEC = pl.BlockSpec(memory_space=pltpu.HBM)
VMEM_SPEC = pl.BlockSpec(memory_space=pltpu.VMEM)


def _place():
    return lax.axis_index("x"), lax.axis_index("y"), lax.axis_index("c")


def _flip(pos, r):
    return tuple(1 - p if (r >> (2 - a)) & 1 else p for a, p in enumerate(pos))


def _index(pos):
    return 4 * pos[0] + 2 * pos[1] + pos[2]


def _all_gather(name, shards, spec):
    n = len(shards)

    def body(*refs):
        x_refs, o_refs = refs[:n], refs[n:2 * n]
        send_sems, recv_sems, local_sems = refs[2 * n:]
        me = _place()
        sibling = _flip(me, 1)
        far = [_flip(me, r) for r in (4, 2, 6)]

        def copy(t, sem, block, to, src=None):
            rows = o_refs[t].at[_index(block)]
            return pltpu.make_async_remote_copy(
                src_ref=rows if src is None else src, dst_ref=rows, send_sem=send_sems.at[t, sem], recv_sem=recv_sems.at[t, sem],
                device_id=to, device_id_type=MESH)

        own = [pltpu.make_async_copy(x_refs[t], o_refs[t].at[_index(me)], local_sems.at[t]) for t in range(n)]
        for cp in own:
            cp.start()
        first = []
        for t in range(n):
            first.append(copy(t, 0, me, sibling, src=x_refs[t]))
            first += [copy(t, 1 + j, me, peer, src=x_refs[t]) for j, peer in enumerate(far)]
        for cp in first:
            cp.start()
        passed = []
        for j, peer in enumerate(far):
            for t in range(n):
                copy(t, 1 + j, peer, me).wait_recv()
                cp = copy(t, 4 + j, peer, sibling)
                cp.start()
                passed.append(cp)
        for t in range(n):
            copy(t, 0, sibling, me).wait_recv()
            for j, peer in enumerate(far):
                copy(t, 4 + j, _flip(peer, 1), me).wait_recv()
        for cp in first + passed:
            cp.wait_send()
        for cp in own:
            cp.wait()

    return pl.pallas_call(
        body, name=name, in_specs=[spec] * n, out_specs=[spec] * n,
        out_shape=[jax.ShapeDtypeStruct((N_DEV,) + sh.shape, sh.dtype) for sh in shards],
        scratch_shapes=[pltpu.SemaphoreType.DMA((n, 7)), pltpu.SemaphoreType.DMA((n, 7)), pltpu.SemaphoreType.DMA((n,))],
    )(*shards)


SEM_SPEC = pl.BlockSpec(memory_space=pltpu.SEMAPHORE)
ANY_SPEC = pl.BlockSpec(memory_space=pl.ANY)


def _landing(own, mine):
    return lax.dynamic_update_slice(lax.empty((N_DEV,) + own.shape, own.dtype), own[None], (mine,) + (0,) * own.ndim)


def _peer_copies(src_refs, land_refs, send_sems, recv_sems, scatter, arrivals):
    me = _place()
    mine = _index(me)
    copies = []
    for t, (src, land) in enumerate(zip(src_refs, land_refs)):
        for r in range(1, N_DEV):
            peer = _flip(me, r)
            theirs = _index(peer)
            sem = t * (N_DEV - 1) + r - 1
            copies.append(pltpu.make_async_remote_copy(
                src_ref=src.at[theirs] if scatter else src, dst_ref=land.at[theirs if arrivals else mine],
                send_sem=send_sems.at[sem], recv_sem=recv_sems.at[sem], device_id=peer, device_id_type=MESH))
    return copies


def _send_start(name, sources, lands, scatter, after=None, carry=None):
    n = len(sources)
    extra = [a for a in (after, carry) if a is not None]
    token = jax.ShapeDtypeStruct((8, 128), F32) if carry is None else jax.ShapeDtypeStruct(carry.shape, carry.dtype)

    def body(*refs):
        outs = refs[2 * n + len(extra):]
        for out in _peer_copies(refs[:n], refs[n:2 * n], outs[0], outs[1], scatter, False):
            out.start()
        outs[-1][...] = jnp.zeros_like(outs[-1]) if carry is None else refs[2 * n + len(extra) - 1][...]

    outs = pl.pallas_call(
        body, name=name, in_specs=[HBM_SPEC] * (2 * n) + [ANY_SPEC] * (after is not None) + [VMEM_SPEC] * (carry is not None),
        out_specs=[SEM_SPEC, SEM_SPEC] + [HBM_SPEC] * (2 * n) + [VMEM_SPEC],
        out_shape=[pltpu.SemaphoreType.DMA((n * (N_DEV - 1),)), pltpu.SemaphoreType.DMA((n * (N_DEV - 1),))]
        + [pltpu.HBM(a.shape, a.dtype) for a in list(sources) + list(lands)] + [token],
        input_output_aliases={i: 2 + i for i in range(2 * n)},
        compiler_params=pltpu.CompilerParams(has_side_effects=pltpu.SideEffectType.DATAFLOW_SIDE_EFFECTING),
    )(*[pltpu.with_memory_space_constraint(a, pltpu.HBM) for a in list(sources) + list(lands)], *extra)
    return outs[0], outs[1], outs[2:2 + n], outs[2 + n:2 + 2 * n], outs[-1]


def _send_wait(name, started, after, scatter):
    send_sems, recv_sems, sources, lands, _ = started
    n = len(sources)

    def body(*refs):
        for out in _peer_copies(refs[:n], refs[n:2 * n], refs[2 * n], refs[2 * n + 1], scatter, False):
            out.wait_send()
        for arrival in _peer_copies(refs[:n], refs[n:2 * n], refs[2 * n], refs[2 * n + 1], scatter, True):
            arrival.wait_recv()

    outs = pl.pallas_call(
        body, name=name, in_specs=[HBM_SPEC] * (2 * n) + [SEM_SPEC, SEM_SPEC, ANY_SPEC], out_specs=[HBM_SPEC] * (2 * n),
        out_shape=[pltpu.HBM(a.shape, a.dtype) for a in list(sources) + list(lands)],
        input_output_aliases={i: i for i in range(2 * n)},
        compiler_params=pltpu.CompilerParams(has_side_effects=pltpu.SideEffectType.DATAFLOW_SIDE_EFFECTING),
    )(*sources, *lands, send_sems, recv_sems, after)
    return outs[n:]


def _pack_rows(parts):
    offsets, row = [], 0
    for part in parts:
        offsets.append(row)
        row += part.shape[0]
    return offsets, -(-row // 8) * 8, -(-max(part.shape[1] for part in parts) // 128) * 128


def _pack(name, parts):
    offsets, rows, width = _pack_rows(parts)

    def body(*refs):
        o_ref = refs[-1]
        o_ref[...] = jnp.zeros_like(o_ref)
        for off, ref in zip(offsets, refs[:-1]):
            o_ref[off:off + ref.shape[0], 0:ref.shape[1]] = ref[...]

    return pl.pallas_call(body, name=name, in_specs=[VMEM_SPEC] * len(parts), out_specs=VMEM_SPEC,
                          out_shape=jax.ShapeDtypeStruct((rows, width), F32))(*parts)


def _adamw_math(w, g, m, v):
    m = ADAM_B1 * m + (1.0 - ADAM_B1) * g
    v = ADAM_B2 * v + (1.0 - ADAM_B2) * (g * g)
    m_hat = m * (1.0 / (1.0 - ADAM_B1 ** ADAM_STEP))
    denom = jnp.sqrt(v * (1.0 / (1.0 - ADAM_B2 ** ADAM_STEP))) + ADAM_EPS
    inv = pl.reciprocal(denom, approx=True)
    inv = inv * (2.0 - denom * inv)
    return -ADAM_LR * (m_hat * inv + ADAM_WD * w), m, v


def _adamw_step(w_ref, m_ref, v_ref, p_ref, g_ref, d_ref, nm_ref, nv_ref):
    g = p_ref[0].astype(F32)
    for dev in range(1, N_DEV):
        g = g + p_ref[dev].astype(F32)
    g_ref[...] = g
    d_ref[...], nm_ref[...], nv_ref[...] = _adamw_math(w_ref[...], g, m_ref[...], v_ref[...])


def _adamw_rows(rows):
    return max(t for t in range(8, min(rows, 256) + 1, 8) if rows % t == 0)


def _adamw_shard(name, w, m, v, partials):
    rows, cols = w.shape
    tr = _adamw_rows(rows)
    blk = pl.BlockSpec((tr, cols), lambda i: (i, 0))
    return pl.pallas_call(
        _adamw_step_fn(), name=name, grid=(rows // tr,), in_specs=[blk, blk, blk, pl.BlockSpec((N_DEV, tr, cols), lambda i: (0, i, 0))],
        out_specs=[blk] * 4, out_shape=[jax.ShapeDtypeStruct((rows, cols), F32)] * 4, compiler_params=_params(("parallel",)),
    )(w, m, v, partials)


def _adamw_step_fn():
    return functools.partial(_adamw_step)


def _adamw_layers(name, w, m, v, partials):
    layers, rows, cols = w.shape
    tr = _adamw_rows(rows)
    last = rows // tr - 1

    def body(w_ref, m_ref, v_ref, *rest):
        for layer in range(layers):
            @pl.when(pl.program_id(0) == layer)
            def _():
                _adamw_step(w_ref, m_ref, v_ref, rest[layer], *rest[layers:])

    blk = pl.BlockSpec((None, tr, cols), lambda l, i: (l, i, 0))
    part = lambda layer: pl.BlockSpec((N_DEV, tr, cols), lambda l, i: (0, jnp.where(l == layer, i, jnp.where(l < layer, 0, last)), 0))
    return pl.pallas_call(
        body, name=name, grid=(layers, rows // tr), in_specs=[blk, blk, blk] + [part(layer) for layer in range(layers)],
        out_specs=[blk] * 4, out_shape=[jax.ShapeDtypeStruct(w.shape, F32)] * 4, compiler_params=_params(("arbitrary", "arbitrary")),
    )(w, m, v, *partials)


def _adamw_small(gathered, places, entries):
    n = len(entries)
    np_ = len(gathered)

    def body(*refs):
        pack_refs = refs[:np_]
        refs = refs[np_ - 1:]
        w_refs, m_refs, v_refs = refs[1:1 + n], refs[1 + n:1 + 2 * n], refs[1 + 2 * n:1 + 3 * n]
        outs = refs[1 + 3 * n:]
        totals = []
        for pack_ref in pack_refs:
            acc = pack_ref[0]
            for dev in range(1, N_DEV):
                acc = acc + pack_ref[dev]
            totals.append(acc)
        mine = _index(_place())
        for e in range(n):
            rows, cols = w_refs[e].shape
            total, off = totals[places[e][0]], places[e][1]
            if entries[e][3]:
                g = jnp.zeros((rows, cols), F32)
                for dev in range(N_DEV):
                    g = g + jnp.where(mine == dev, total[off + dev * rows:off + (dev + 1) * rows, 0:cols], 0.0)
            else:
                g = total[off:off + rows, 0:cols]
            outs[4 * e][...] = g
            outs[4 * e + 1][...], outs[4 * e + 2][...], outs[4 * e + 3][...] = _adamw_math(w_refs[e][...], g, m_refs[e][...], v_refs[e][...])
        outs[4 * n][...] = totals[places[n][0]][places[n][1]:places[n][1] + 1, 0:128]

    shapes = []
    for w, _, _, _ in entries:
        shapes += [jax.ShapeDtypeStruct(w.shape, F32)] * 4
    shapes.append(jax.ShapeDtypeStruct((1, 128), F32))
    return pl.pallas_call(
        body, name="adamw_small", in_specs=[VMEM_SPEC] * (np_ + 3 * n), out_specs=[VMEM_SPEC] * len(shapes), out_shape=shapes,
        compiler_params=pltpu.CompilerParams(vmem_limit_bytes=VMEM_LIMIT),
    )(*gathered, *[e[0] for e in entries], *[e[1] for e in entries], *[e[2] for e in entries])


def _ffn_forward(tag, h, gain, w_up, late):
    s, d = h.shape
    fb = w_up.shape[1]
    tm = _row_tile(s, 2 * MM_ROWS)
    (a,), (u,) = _norm_proj(f"ffn_up_{tag}", h, [
        (gain, w_up, pl.BlockSpec((None, fb, d), lambda i, j: (j, 0, 0)), NT,
         pl.BlockSpec((None, None, tm, fb), lambda i, j: (j // 4, j % 4, i, 0)), jax.ShapeDtypeStruct((2, 4, s, fb), BF16))], tm=tm, nj=N_DEV)
    w_down, conv_w, conv_b = late(u)
    hidden, conv, out = _ffn_hidden_down(f"ffn_hidden_down_{tag}", u, conv_w, conv_b, w_down, h)
    return out, (a, u, hidden, conv)


def _ffn_backward(tag, h, gain, w_up, w_down, conv_w, conv_b, saved, dout):
    a, u, hidden, conv = saved
    dout, dout_bf = dout
    s, d = h.shape
    fb = w_up.shape[1]
    tm = _row_tile(s, MM_ROWS)
    dhidden = _matmul(
        f"ffn_down_bwd_{tag}", dout_bf, w_down, dims=NT, grid=(s // tm, 4, 1),
        a_spec=pl.BlockSpec((tm, d), lambda i, j, k: (i, 0)),
        b_spec=pl.BlockSpec((None, fb, d), lambda i, j, k: (j, 0, 0)),
        o_spec=pl.BlockSpec((None, tm, fb), lambda i, j, k: (j, i, 0)),
        out_shape=jax.ShapeDtypeStruct((4, s, fb), BF16))
    dw_down = _matmul(
        f"ffn_down_grad_{tag}", hidden, dout_bf, dims=TN, grid=(4, 1, 1),
        a_spec=pl.BlockSpec((None, s, fb), lambda i, j, k: (i, 0, 0)),
        b_spec=pl.BlockSpec((s, d), lambda i, j, k: (0, 0)),
        o_spec=pl.BlockSpec((None, fb, d), lambda i, j, k: (i, 0, 0)),
        out_shape=jax.ShapeDtypeStruct((4, fb, d), BF16))
    du, dconv_w, dconv_b, dh, dh_bf, dgain = _ffn_hidden_up_bwd(f"ffn_hidden_up_bwd_{tag}", u, conv, dhidden, conv_w, w_up, h, gain, dout)
    dw_up = _matmul(
        f"ffn_up_grad_{tag}", du, a, dims=TN, grid=(N_DEV, 1, 1),
        a_spec=pl.BlockSpec((None, None, s, fb), lambda i, j, k: (i // 4, i % 4, 0, 0)),
        b_spec=pl.BlockSpec((s, d), lambda i, j, k: (0, 0)),
        o_spec=pl.BlockSpec((None, fb, d), lambda i, j, k: (i, 0, 0)),
        out_shape=jax.ShapeDtypeStruct((N_DEV, fb, d), BF16))
    return (dh, dh_bf), dgain, dw_up, dw_down, dconv_w, dconv_b


def kernel(x, hg_norm, hg_w_in, hg_lb_logits, hg_out_norm, hg_w_out, kv_norm, w_kv, attn_norm, attn_w_q, attn_sinks, attn_w_o, ffn_norm, ffn_w_up, ffn_conv_w, ffn_conv_b, ffn_w_down, final_norm, loss_target, m_hg_norm, m_hg_w_in, m_hg_lb_logits, m_hg_out_norm, m_hg_w_out, m_kv_norm, m_w_kv, m_attn_norm, m_attn_w_q, m_attn_sinks, m_attn_w_o, m_ffn_norm, m_ffn_w_up, m_ffn_conv_w, m_ffn_conv_b, m_ffn_w_down, m_final_norm, v_hg_norm, v_hg_w_in, v_hg_lb_logits, v_hg_out_norm, v_hg_w_out, v_kv_norm, v_w_kv, v_attn_norm, v_attn_w_q, v_attn_sinks, v_attn_w_o, v_ffn_norm, v_ffn_w_up, v_ffn_conv_w, v_ffn_conv_b, v_ffn_w_down, v_final_norm):
    _, s, d = x.shape
    x0, target = x[0], loss_target[0]
    half = hg_w_in.shape[2]
    fs = ffn_conv_w.shape[2]
    fb = 2 * fs
    kvd = w_kv.shape[1]
    nq = d // ATT_HEAD_DIM
    tm = _row_tile(s, MM_ROWS)

    mine = _index(_place())
    gather = lambda tag, shards, after, carry=None: _send_start("gather_start_" + tag, shards, [_landing(a, mine) for a in shards], False, after, carry)
    w_in, g_hgn, g_lbl, w_out = _all_gather("gather_hg", [hg_w_in[0].astype(BF16), hg_norm, hg_lb_logits, hg_w_out[0].astype(BF16)], HBM_SPEC)
    w_out = w_out.reshape(d, d)
    up_t = lambda a: jnp.swapaxes(a, -1, -2)
    coming_up0 = gather("ffn_up0", [up_t(ffn_w_up[0]).astype(BF16)], None, g_hgn.reshape(1, d))
    hgn = coming_up0[4]
    lbl = g_lbl.transpose(1, 0, 2).reshape(2, d)
    conv_b = [ffn_conv_b[layer].reshape(4, 1, fb) for layer in range(2)]
    gains = [ffn_norm[0:1], ffn_norm[1:2]]
    kvn, fin = kv_norm.reshape(1, d), final_norm.reshape(1, d)

    t2 = _row_tile(s, 2 * MM_ROWS)
    (a0,), (p,) = _norm_proj("hg_in", x0, [
        (hgn, w_in, pl.BlockSpec((None, d, half), lambda i, j: (j, 0, 0)), NN,
         pl.BlockSpec((None, t2, half), lambda i, j: (j // 2, i, j % 2)), jax.ShapeDtypeStruct((4, s, d), BF16))], tm=t2, nj=N_DEV)
    o, og, states, gsum = _hgrn2_fwd(p, lbl, hg_out_norm)
    coming_dn0 = gather("ffn_down0", [ffn_conv_w, ffn_w_down[0].astype(BF16)], o)
    x1 = _mm_rows("hg_out", og, w_out, out_dtype=F32, add=x0, after=coming_dn0[4])
    w_up0, = _send_wait("gather_wait_ffn_up0", coming_up0, x1, False)
    coming_attn = gather("attn", [w_kv.astype(BF16), attn_w_q[0].astype(BF16), attn_w_o[0].astype(BF16)], w_up0, gains[0])
    gains[0] = coming_attn[4]
    w_up, w_dn, conv_w, coming = [w_up0, None], [None, None], [], {}

    def late0(u):
        g_cw, w_dn0 = _send_wait("gather_wait_ffn_down0", coming_dn0, u, False)
        w_dn[0] = w_dn0.reshape(4, fb, d)
        conv_w.extend(g_cw[:, layer].reshape(4, 2, CONV_WIDTH, fs).transpose(0, 2, 1, 3).reshape(4, CONV_WIDTH, fb) for layer in range(2))
        coming["up1"] = gather("ffn_up1", [up_t(ffn_w_up[1]).astype(BF16)], w_dn0, conv_b[0])
        return w_dn[0], conv_w[0], coming["up1"][4]

    x2, saved0 = _ffn_forward("0", x1, gains[0], w_up[0], late0)
    w_kvg, w_q, w_o = _send_wait("gather_wait_attn", coming_attn, x2, False)
    w_kvg, w_q, w_o = w_kvg.reshape(d, kvd), w_q.reshape(d, d), w_o.reshape(d, d)
    (akv, a2), (kv, q) = _norm_proj("attn_in", x2, [
        (kvn, w_kvg, pl.BlockSpec((d, kvd), lambda i, j: (0, 0)), NN, pl.BlockSpec((tm, kvd), lambda i, j: (i, 0)), jax.ShapeDtypeStruct((s, kvd), BF16)),
        (attn_norm, w_q, pl.BlockSpec((d, d), lambda i, j: (0, 0)), NN, pl.BlockSpec((tm, d), lambda i, j: (i, 0)), jax.ShapeDtypeStruct((s, d), BF16))],
        tm=tm, nj=1)
    coming_dn1 = gather("ffn_down1", [ffn_w_down[1].astype(BF16)], q, attn_sinks)
    att = _attn_fwd(q, kv, coming_dn1[4])
    x3 = _mm_rows("attn_out", att, w_o, out_dtype=F32, add=x2)
    w_up[1], = _send_wait("gather_wait_ffn_up1", coming["up1"], x3, False)

    def late1(u):
        w_dn[1] = _send_wait("gather_wait_ffn_down1", coming_dn1, u, False)[0].reshape(4, fb, d)
        return w_dn[1], conv_w[1], conv_b[1]

    x4, saved1 = _ffn_forward("1", x3, gains[1], w_up[1], late1)
    dx4, dx4_bf, d_fin, loss_part = _loss_head(x4, fin, target)

    dx3, d_fn1, dw_up1, dw_dn1, dcw1, dcb1 = _ffn_backward("1", x3, gains[1], w_up[1], w_dn[1], conv_w[1], conv_b[1], saved1, (dx4, dx4_bf))
    rows = d // N_DEV
    scatter = lambda tag, stacks, carry: _send_start("scatter_start_" + tag, stacks, [_landing(lax.dynamic_index_in_dim(a, mine, keepdims=False), mine) for a in stacks], True, None, carry)
    going_ffn1 = scatter("ffn1", [dw_up1, dw_dn1.reshape(N_DEV, fs, d)], attn_sinks)
    datt = _mm_rows_nt("attn_out_bwd", dx3[1], w_o, out_dtype=BF16)
    dw_o = _mm_tn("attn_out_grad", att, dx3[1])
    dq, dkv_own, dkv_before, dsink = _attn_bwd(q, kv, att, datt, going_ffn1[4])
    tiles = dkv_before.shape[0]
    dkv = dkv_own.reshape(tiles, s // tiles, kvd)
    dkv = jnp.concatenate([dkv[:, :-WINDOW], dkv[:, -WINDOW:] + jnp.pad(dkv_before[1:], ((0, 1), (0, 0), (0, 0)))], axis=1).reshape(s, kvd)
    dw_q = _mm_tn("q_proj_grad", a2, dq)
    dw_kv = _mm_tn("kv_proj_grad", akv, dkv)
    going_attn = scatter("attn", [dw_kv.reshape(N_DEV, rows, kvd), dw_q.reshape(N_DEV, rows, d), dw_o.reshape(N_DEV, rows, d)], kvn)
    whole = lambda a_ref, b_ref: [(a_ref[...], b_ref[...])]
    rows_of = lambda width: (lambda tile: pl.BlockSpec((tile, width), lambda i: (i, 0)))
    dx2, (d_kvn, d_attn) = _proj_norm_bwd("attn_in_bwd", x2, dx3[0], [
        (dkv, rows_of(kvd), w_kvg, pl.BlockSpec((d, kvd), lambda i: (0, 0)), whole, going_attn[4]),
        (dq, rows_of(d), w_q, pl.BlockSpec((d, d), lambda i: (0, 0)), whole, attn_norm)])
    dx1, d_fn0, dw_up0, dw_dn0, dcw0, dcb0 = _ffn_backward("0", x1, gains[0], w_up[0], w_dn[0], conv_w[0], conv_b[0], saved0, dx2)
    dw_out = _mm_tn("hg_out_grad", og, dx1[1])
    going_ffn0 = scatter("ffn0", [dw_up0, dw_dn0.reshape(N_DEV, fs, d), dw_out.reshape(N_DEV, rows, d)], hg_out_norm)
    dog = _mm_rows_nt("hg_out_bwd", dx1[1], w_out, out_dtype=F32)
    dp, d_lbl, d_ogain = _hgrn2_bwd(p, lbl, going_ffn0[4], o, dog, states, gsum)
    dw_in = _matmul(
        "hg_in_grad", a0, dp, dims=TN, grid=(1, N_DEV, 1),
        a_spec=pl.BlockSpec((s, d), lambda i, j, k: (0, 0)),
        b_spec=pl.BlockSpec((None, s, half), lambda i, j, k: (j // 2, 0, j % 2)),
        o_spec=pl.BlockSpec((None, d, half), lambda i, j, k: (j, 0, 0)),
        out_shape=jax.ShapeDtypeStruct((N_DEV, d, half), BF16))
    going_hg = scatter("hg", [dw_in], hgn)
    (dx0, _), (d_hgn,) = _proj_norm_bwd("hg_in_bwd", x0, dx1[0], [
        (dp, lambda tile: pl.BlockSpec((4, tile, d), lambda i: (0, i, 0)), w_in, pl.BlockSpec((N_DEV, d, half), lambda i: (0, 0, 0)),
         lambda g_ref, w_ref: [(g_ref[k // 2, :, (k % 2) * half:(k % 2 + 1) * half], w_ref[k]) for k in range(N_DEV)],
         going_hg[4])])

    as_blocks = lambda a, r: a.reshape(r, N_DEV, -1).transpose(1, 0, 2).reshape(N_DEV * r, -1)
    d_cw = jnp.concatenate([g.transpose(1, 0, 2).reshape(CONV_WIDTH, 4 * fb) for g in (dcw0, dcw1)], axis=0)
    parts = [d_fin, jnp.concatenate([d_fn0, d_fn1], axis=0), jnp.concatenate([dcb0.reshape(1, 4 * fb), dcb1.reshape(1, 4 * fb)], axis=0),
             as_blocks(d_cw, 2 * CONV_WIDTH), d_attn, jnp.sum(dsink[:, :, 0], axis=0).reshape(1, nq), d_kvn, d_ogain,
             as_blocks(d_hgn, 1), as_blocks(d_lbl, 2), loss_part]
    wide = [2]
    packs = [[parts[i] for i in wide], [part for i, part in enumerate(parts) if i not in wide]]
    places = [None] * len(parts)
    for which, members in enumerate([wide, [i for i in range(len(parts)) if i not in wide]]):
        for i, off in zip(members, _pack_rows(packs[which])[0]):
            places[i] = (which, off)
    packed = [_pack("pack_wide_grads", packs[0]), _pack("pack_narrow_grads", packs[1])]
    going_small = _send_start("small_grads_start", packed, [_landing(a, mine) for a in packed], False)

    arrive = lambda tag, going, after: _send_wait("scatter_wait_" + tag, going, after, True)
    (l_up1, l_dn1), (l_kv, l_q, l_o), (l_up0, l_dn0, l_out) = (
        arrive("ffn1", going_ffn1, going_small[4]), arrive("attn", going_attn, going_small[4]), arrive("ffn0", going_ffn0, going_small[4]))
    big = {}
    for tag, w, m, v, part in [
            ("w_kv", w_kv, m_w_kv, v_w_kv, l_kv), ("attn_w_q", attn_w_q[0], m_attn_w_q[0], v_attn_w_q[0], l_q),
            ("attn_w_o", attn_w_o[0], m_attn_w_o[0], v_attn_w_o[0], l_o)]:
        big[tag] = _adamw_shard("adamw_" + tag, w, m, v, part)
    up_done = _adamw_layers("adamw_ffn_w_up", up_t(ffn_w_up), up_t(m_ffn_w_up), up_t(v_ffn_w_up), (l_up0, l_up1))
    big["ffn_w_up"] = [up_t(a) for a in up_done]
    big["ffn_w_down"] = _adamw_layers("adamw_ffn_w_down", ffn_w_down, m_ffn_w_down, v_ffn_w_down, (l_dn0, l_dn1))
    lead = lambda tag: [a[None] for a in big[tag]]

    both_done = up_done[0][0, 0:1, 0:1] + big["ffn_w_down"][0][0, 0:1, 0:1]
    gathered = _send_wait("small_grads_wait", going_small, both_done, False)
    two = lambda a: a.reshape(-1, a.shape[-1])
    small = [(fin, m_final_norm.reshape(1, d), v_final_norm.reshape(1, d), False), (ffn_norm, m_ffn_norm, v_ffn_norm, False),
             (ffn_conv_b, m_ffn_conv_b, v_ffn_conv_b, False), (two(ffn_conv_w), two(m_ffn_conv_w), two(v_ffn_conv_w), True),
             (attn_norm, m_attn_norm, v_attn_norm, False), (attn_sinks, m_attn_sinks, v_attn_sinks, False),
             (kvn, m_kv_norm.reshape(1, d), v_kv_norm.reshape(1, d), False), (hg_out_norm, m_hg_out_norm, v_hg_out_norm, False),
             (hg_norm, m_hg_norm, v_hg_norm, True), (hg_lb_logits, m_hg_lb_logits, v_hg_lb_logits, True)]
    res = _adamw_small(gathered, places, small)
    l_in, = arrive("hg", going_hg, gathered[1])
    big["hg_w_in"] = _adamw_shard("adamw_hg_w_in", hg_w_in[0], m_hg_w_in[0], v_hg_w_in[0], l_in)
    big["hg_w_out"] = _adamw_shard("adamw_hg_w_out", hg_w_out[0], m_hg_w_out[0], v_hg_w_out[0], l_out)
    names = ["final_norm", "ffn_norm", "ffn_conv_b", "ffn_conv_w", "attn_norm", "attn_sinks", "kv_norm", "hg_out_norm", "hg_norm", "hg_lb_logits"]
    shapes = {"final_norm": final_norm.shape, "kv_norm": kv_norm.shape, "ffn_conv_w": ffn_conv_w.shape}
    out = {n: [a.reshape(shapes[n]) if n in shapes else a for a in res[4 * i:4 * i + 4]] for i, n in enumerate(names)}
    out.update(hg_w_in=lead("hg_w_in"), hg_w_out=lead("hg_w_out"), w_kv=big["w_kv"], attn_w_q=lead("attn_w_q"), attn_w_o=lead("attn_w_o"),
               ffn_w_up=big["ffn_w_up"], ffn_w_down=big["ffn_w_down"])
    order = ["hg_norm", "hg_w_in", "hg_lb_logits", "hg_out_norm", "hg_w_out", "kv_norm", "w_kv", "attn_norm", "attn_w_q", "attn_sinks",
             "attn_w_o", "ffn_norm", "ffn_w_up", "ffn_conv_w", "ffn_conv_b", "ffn_w_down", "final_norm"]
    loss = res[-1][0, 0]
    return (loss, dx0[None], *[out[n][0] for n in order], *[out[n][1] for n in order], *[out[n][2] for n in order], *[out[n][3] for n in order])
```

```python
import functools

import jax
import jax.numpy as jnp
from jax import lax
from jax.experimental import pallas as pl
from jax.experimental.pallas import tpu as pltpu

F32 = jnp.float32
BF16 = jnp.bfloat16

EPS = 1e-6
HG_EXPAND = 128
HG_CHUNK = 32
ATT_HEAD_DIM = 64
ATT_KV_HEADS = 2
WINDOW = 128
CONV_WIDTH = 3
ADAM_LR = 0.001
ADAM_B1 = 0.9
ADAM_B2 = 0.999
ADAM_EPS = 1e-08
ADAM_WD = 0.01
ADAM_STEP = 10

N_DEV = 8
VMEM_LIMIT = 48 * 1024 * 1024
NEG = -1e30

NN = (((1,), (0,)), ((), ()))
NT = (((1,), (1,)), ((), ()))
TN = (((0,), (0,)), ((), ()))
MESH = pl.DeviceIdType.MESH


def _dot(a, b, dims=NN):
    return lax.dot_general(a.astype(BF16), b.astype(BF16), dims, preferred_element_type=F32)


def _sigmoid(x):
    return 0.5 * jnp.tanh(0.5 * x) + 0.5


def _silu(x):
    return x * _sigmoid(x)


def _silu_and_grad(x):
    s = _sigmoid(x)
    return x * s, s * (1.0 + x * (1.0 - s))


def _dsilu(x):
    return _silu_and_grad(x)[1]


def _params(semantics):
    return pltpu.CompilerParams(dimension_semantics=semantics, vmem_limit_bytes=VMEM_LIMIT)


def _row_tile(rows, want=512):
    return min(rows, want)


MM_ROWS = 1024


def _matmul(name, a, b, *, dims, grid, a_spec, b_spec, o_spec, out_shape, add=None, add_spec=None, after=None):
    assert grid[2] == 1

    def body(*refs):
        a_ref, b_ref, o_ref = refs[0], refs[1], refs[-1]
        total = _dot(a_ref[...], b_ref[...], dims)
        if add is not None:
            total = total + refs[2][...]
        o_ref[...] = total.astype(o_ref.dtype)

    in_specs = [a_spec, b_spec] + ([] if add is None else [add_spec]) + ([] if after is None else [pl.BlockSpec(memory_space=pl.ANY)])
    args = (a, b) + (() if add is None else (add,)) + (() if after is None else (after,))
    return pl.pallas_call(
        body, name=name, grid=grid, in_specs=in_specs, out_specs=o_spec, out_shape=out_shape,
        compiler_params=_params(("parallel", "parallel", "arbitrary")),
    )(*args)


def _mm_rows(name, a, w, *, out_dtype, add=None, after=None):
    s, kdim = a.shape
    n = w.shape[1]
    tm = _row_tile(s, MM_ROWS)
    return _matmul(
        name, a, w, dims=NN, grid=(s // tm, 1, 1),
        a_spec=pl.BlockSpec((tm, kdim), lambda i, j, k: (i, 0)),
        b_spec=pl.BlockSpec((kdim, n), lambda i, j, k: (0, 0)),
        o_spec=pl.BlockSpec((tm, n), lambda i, j, k: (i, 0)),
        out_shape=jax.ShapeDtypeStruct((s, n), out_dtype),
        add=add, add_spec=None if add is None else pl.BlockSpec((tm, n), lambda i, j, k: (i, 0)), after=after,
    )


def _mm_rows_nt(name, a, w, *, out_dtype):
    s, n = a.shape
    kdim = w.shape[0]
    tm = _row_tile(s, MM_ROWS)
    return _matmul(
        name, a, w, dims=NT, grid=(s // tm, 1, 1),
        a_spec=pl.BlockSpec((tm, n), lambda i, j, k: (i, 0)),
        b_spec=pl.BlockSpec((kdim, n), lambda i, j, k: (0, 0)),
        o_spec=pl.BlockSpec((tm, kdim), lambda i, j, k: (i, 0)),
        out_shape=jax.ShapeDtypeStruct((s, kdim), out_dtype),
    )


def _mm_tn(name, a, g):
    s, m = a.shape
    n = g.shape[1]
    tn = min(n, 512)
    return _matmul(
        name, a, g, dims=TN, grid=(1, n // tn, 1),
        a_spec=pl.BlockSpec((s, m), lambda i, j, k: (0, 0)),
        b_spec=pl.BlockSpec((s, tn), lambda i, j, k: (0, j)),
        o_spec=pl.BlockSpec((m, tn), lambda i, j, k: (0, j)),
        out_shape=jax.ShapeDtypeStruct((m, n), BF16),
    )


NORM_ROWS = 256


def _norm_proj(name, h, branches, *, tm, nj):
    s, d = h.shape
    n = len(branches)
    rows = min(tm, NORM_ROWS)

    def body(*refs):
        h_ref, gain_refs, w_refs = refs[0], refs[1:1 + n], refs[1 + n:1 + 2 * n]
        a_refs, o_refs = refs[1 + 2 * n:1 + 3 * n], refs[1 + 3 * n:]

        @pl.when(pl.program_id(1) == 0)
        def _():
            def normalize(c, carry):
                at = pl.ds(pl.multiple_of(c * rows, rows), rows)
                xv = h_ref[at, :]
                xhat = xv * lax.rsqrt(jnp.mean(xv * xv, axis=-1, keepdims=True) + EPS)
                for gain_ref, a_ref in zip(gain_refs, a_refs):
                    a_ref[at, :] = (xhat * gain_ref[...]).astype(BF16)
                return carry

            lax.fori_loop(0, tm // rows, normalize, 0)

        for branch, w_ref, a_ref, o_ref in zip(branches, w_refs, a_refs, o_refs):
            o_ref[...] = _dot(a_ref[...], w_ref[...], branch[3]).astype(o_ref.dtype)

    row = pl.BlockSpec((tm, d), lambda i, j: (i, 0))
    vec = pl.BlockSpec((1, d), lambda i, j: (0, 0))
    outs = pl.pallas_call(
        body, name=name, grid=(s // tm, nj), in_specs=[row] + [vec] * n + [b[2] for b in branches],
        out_specs=[row] * n + [b[4] for b in branches],
        out_shape=[jax.ShapeDtypeStruct((s, d), BF16)] * n + [b[5] for b in branches],
        compiler_params=_params(("parallel", "arbitrary")),
    )(h, *[b[0] for b in branches], *[b[1] for b in branches])
    return outs[:n], outs[n:]


def _proj_norm_bwd(name, h, dres, branches):
    s, d = h.shape
    tm = _row_tile(s)
    n = len(branches)

    def body(*refs):
        h_ref, dres_ref = refs[0], refs[1]
        g_refs, w_refs, gain_refs = refs[2:2 + n], refs[2 + n:2 + 2 * n], refs[2 + 2 * n:2 + 3 * n]
        dh_ref, dhb_ref, dg_refs = refs[2 + 3 * n], refs[3 + 3 * n], refs[4 + 3 * n:]
        i = pl.program_id(0)
        xv = h_ref[...]
        r = lax.rsqrt(jnp.mean(xv * xv, axis=-1, keepdims=True) + EPS)
        xhat = xv * r
        total = dres_ref[...]
        for branch, g_ref, w_ref, gain_ref, dg_ref in zip(branches, g_refs, w_refs, gain_refs, dg_refs):
            pairs = branch[4](g_ref, w_ref)
            da = _dot(*pairs[0], NT)
            for pair in pairs[1:]:
                da = da + _dot(*pair, NT)
            dgain = jnp.sum(da * xhat, axis=0, keepdims=True)

            @pl.when(i == 0)
            def _():
                dg_ref[...] = dgain

            @pl.when(i > 0)
            def _():
                dg_ref[...] += dgain

            dxhat = da * gain_ref[...]
            total = total + r * (dxhat - xhat * jnp.mean(dxhat * xhat, axis=-1, keepdims=True))
        dh_ref[...] = total
        dhb_ref[...] = total.astype(BF16)

    row = pl.BlockSpec((tm, d), lambda i: (i, 0))
    vec = pl.BlockSpec((1, d), lambda i: (0, 0))
    outs = pl.pallas_call(
        body, name=name, grid=(s // tm,),
        in_specs=[row, row] + [b[1](tm) for b in branches] + [b[3] for b in branches] + [vec] * n, out_specs=[row, row] + [vec] * n,
        out_shape=[jax.ShapeDtypeStruct((s, d), F32), jax.ShapeDtypeStruct((s, d), BF16)] + [jax.ShapeDtypeStruct((1, d), F32)] * n,
        compiler_params=_params(("arbitrary",)),
    )(h, dres, *[b[0] for b in branches], *[b[2] for b in branches], *[b[5] for b in branches])
    return (outs[0], outs[1]), outs[2:]


def _loss_head(h, gain, target):
    s, d = h.shape
    tm = _row_tile(s)

    def body(h_ref, g_ref, t_ref, dh_ref, dhb_ref, dg_ref, loss_ref):
        i = pl.program_id(0)
        xv = h_ref[...]
        r = lax.rsqrt(jnp.mean(xv * xv, axis=-1, keepdims=True) + EPS)
        xhat = xv * r
        err = xhat * g_ref[...] - t_ref[...]
        dy = err * (1.0 / d)
        part = jnp.zeros((1, 128), F32) + 0.5 * jnp.sum(jnp.mean(err * err, axis=-1, keepdims=True))
        dgain = jnp.sum(dy * xhat, axis=0, keepdims=True)

        @pl.when(i == 0)
        def _():
            dg_ref[...] = dgain
            loss_ref[...] = part

        @pl.when(i > 0)
        def _():
            dg_ref[...] += dgain
            loss_ref[...] += part

        dxhat = dy * g_ref[...]
        dh = r * (dxhat - xhat * jnp.mean(dxhat * xhat, axis=-1, keepdims=True))
        dh_ref[...] = dh
        dhb_ref[...] = dh.astype(BF16)

    row = pl.BlockSpec((tm, d), lambda i: (i, 0))
    vec = pl.BlockSpec((1, d), lambda i: (0, 0))
    return pl.pallas_call(
        body, name="loss_head", grid=(s // tm,), in_specs=[row, vec, row],
        out_specs=[row, row, vec, pl.BlockSpec((1, 128), lambda i: (0, 0))],
        out_shape=[jax.ShapeDtypeStruct((s, d), F32), jax.ShapeDtypeStruct((s, d), BF16), jax.ShapeDtypeStruct((1, d), F32),
                   jax.ShapeDtypeStruct((1, 128), F32)],
        compiler_params=_params(("arbitrary",)),
    )(h, gain, target)


def _bdot(a, b, ca, cb):
    return lax.dot_general(a.astype(BF16), b.astype(BF16), (((ca,), (cb,)), ((0,), (0,))), preferred_element_type=F32)


def _chunk_cumsum(xv, reverse=False):
    n = xv.shape[0]
    row = lax.broadcasted_iota(jnp.int32, xv.shape, 0) % HG_CHUNK
    step = 1
    while step < HG_CHUNK:
        if reverse:
            xv = xv + jnp.where(row < HG_CHUNK - step, pltpu.roll(xv, n - step, axis=0), 0.0)
        else:
            xv = xv + jnp.where(row >= step, pltpu.roll(xv, step, axis=0), 0.0)
        step *= 2
    return xv


def _hg_terms(p_ref, lbl_ref, g_ref=None):
    pq = p_ref[0].astype(F32)
    pf = p_ref[1].astype(F32)
    lb = _sigmoid(lbl_ref[0:1, :] - lbl_ref[1:2, :])
    sig = _sigmoid(pf)
    fg = lb + (1.0 - lb) * sig
    nc = pq.shape[0] // HG_CHUNK
    chunks = lambda a: a.reshape(nc, HG_CHUNK, HG_EXPAND)
    q = chunks(_silu(pq) * HG_EXPAND ** -0.5)
    k = chunks(1.0 - fg)
    v = chunks(p_ref[2].astype(F32))
    g = chunks(_chunk_cumsum(jnp.log(fg)) if g_ref is None else g_ref[...])
    gm = g[:, HG_CHUNK // 2 - 1:HG_CHUNK // 2, :]
    gl = g[:, HG_CHUNK - 1:HG_CHUNK, :]
    e_mid, e_inv, e_all, e_end = jnp.exp(g - gm), jnp.exp(gm - g), jnp.exp(g), jnp.exp(gl - g)
    terms = dict(q=q, k=k, v=v, g=g, qd=q * e_all, qt=q * e_mid, kt=k * e_inv, kd=k * e_end, e_last=jnp.exp(gl),
                 e_mid=e_mid, e_inv=e_inv, e_all=e_all, e_end=e_end)
    return terms, (pq, sig, fg, lb)


def _causal(nc):
    r = lax.broadcasted_iota(jnp.int32, (nc, HG_CHUNK, HG_CHUNK), 1)
    c = lax.broadcasted_iota(jnp.int32, (nc, HG_CHUNK, HG_CHUNK), 2)
    return r >= c


def _hgrn2_fwd(p, lb_logits, out_gain):
    _, s, d = p.shape
    heads = d // HG_EXPAND
    t = _row_tile(s, 2048)
    nc = t // HG_CHUNK

    def body(p_ref, lbl_ref, gain_ref, o_ref, og_ref, st_ref, g_ref, state, decay):
        @pl.when(pl.program_id(1) == 0)
        def _():
            state[...] = jnp.zeros_like(state)

        tm, _ = _hg_terms(p_ref, lbl_ref)
        g_ref[...] = tm["g"].reshape(t, HG_EXPAND)
        decay[...] = tm["e_last"]
        st_ref[...] = _bdot(tm["v"], tm["kd"], 1, 1)

        def chunk(c, carry):
            add = st_ref[c]
            st = state[...]
            st_ref[c] = st
            state[...] = st * decay[c] + add
            return carry

        lax.fori_loop(0, nc, chunk, 0)
        a = jnp.where(_causal(nc), _bdot(tm["qt"], tm["kt"], 2, 2), 0.0)
        ov = (_bdot(tm["qd"], st_ref[...], 2, 2) + _bdot(a, tm["v"], 2, 1)).reshape(t, HG_EXPAND)
        o_ref[...] = ov
        on = ov * lax.rsqrt(jnp.mean(ov * ov, axis=-1, keepdims=True) + EPS) * gain_ref[...]
        og_ref[...] = (on * _silu(p_ref[3].astype(F32))).astype(BF16)

    blk = pl.BlockSpec((t, HG_EXPAND), lambda h, b: (b, h))
    return pl.pallas_call(
        body, name="hgrn2_fwd", grid=(heads, s // t),
        in_specs=[pl.BlockSpec((4, t, HG_EXPAND), lambda h, b: (0, b, h)), pl.BlockSpec((2, HG_EXPAND), lambda h, b: (0, h)),
                  pl.BlockSpec((1, HG_EXPAND), lambda h, b: (0, 0))],
        out_specs=[blk, blk, pl.BlockSpec((None, nc, HG_EXPAND, HG_EXPAND), lambda h, b: (h, b, 0, 0)), blk],
        out_shape=[jax.ShapeDtypeStruct((s, d), F32), jax.ShapeDtypeStruct((s, d), BF16),
                   jax.ShapeDtypeStruct((heads, s // HG_CHUNK, HG_EXPAND, HG_EXPAND), F32), jax.ShapeDtypeStruct((s, d), F32)],
        scratch_shapes=[pltpu.VMEM((HG_EXPAND, HG_EXPAND), F32), pltpu.VMEM((nc, 1, HG_EXPAND), F32)],
        compiler_params=_params(("parallel", "arbitrary")),
    )(p, lb_logits, out_gain)


def _hgrn2_bwd(p, lb_logits, out_gain, o, dog, states, gsum):
    _, s, d = p.shape
    heads = d // HG_EXPAND
    t = _row_tile(s, 1024)
    nc = t // HG_CHUNK
    nb = s // t

    def body(p_ref, lbl_ref, gain_ref, o_ref, dog_ref, st_ref, g_ref, dp_ref, dlbl_ref, dgain_ref, dstate, decay, dst_s):
        h, b = pl.program_id(0), pl.program_id(1)

        @pl.when(b == 0)
        def _():
            dstate[...] = jnp.zeros_like(dstate)

        tm, (pq, sig, fg, lb) = _hg_terms(p_ref, lbl_ref, g_ref)
        pg = p_ref[3].astype(F32)
        ov = o_ref[...]
        r = lax.rsqrt(jnp.mean(ov * ov, axis=-1, keepdims=True) + EPS)
        ohat = ov * r
        dogv = dog_ref[...]
        d_on = dogv * _silu(pg)
        dp_ref[3] = (dogv * ohat * gain_ref[...] * _dsilu(pg)).astype(BF16)
        dgain = jnp.sum(d_on * ohat, axis=0, keepdims=True)

        @pl.when((h == 0) & (b == 0))
        def _():
            dgain_ref[...] = dgain

        @pl.when((h > 0) | (b > 0))
        def _():
            dgain_ref[...] += dgain

        dohat = d_on * gain_ref[...]
        do = (r * (dohat - ohat * jnp.mean(dohat * ohat, axis=-1, keepdims=True))).reshape(nc, HG_CHUNK, HG_EXPAND)

        decay[...] = tm["e_last"]
        dst_s[...] = _bdot(do, tm["qd"], 1, 1)

        def chunk(i, carry):
            c = nc - 1 - i
            add = dst_s[c]
            dst = dstate[...]
            dst_s[c] = dst
            dstate[...] = dst * decay[c] + add
            return carry

        lax.fori_loop(0, nc, chunk, 0)
        st, dst = st_ref[...], dst_s[...]
        causal = _causal(nc)
        a = jnp.where(causal, _bdot(tm["qt"], tm["kt"], 2, 2), 0.0)
        da = jnp.where(causal, _bdot(do, tm["v"], 2, 2), 0.0)
        dqt = _bdot(da, tm["kt"], 2, 1)
        dkt = _bdot(da, tm["qt"], 1, 1)
        dqd = _bdot(do, st, 2, 1)
        dkd = _bdot(tm["v"], dst, 2, 1)
        dv = _bdot(a, do, 1, 1) + _bdot(tm["kd"], dst, 2, 2)
        dq = dqt * tm["e_mid"] + dqd * tm["e_all"]
        dk = dkt * tm["e_inv"] + dkd * tm["e_end"]
        dg = dqt * tm["qt"] - dkt * tm["kt"] + dqd * tm["qd"] - dkd * tm["kd"]
        dgl = jnp.sum(dkd * tm["kd"], axis=1, keepdims=True) + tm["e_last"] * jnp.sum(dst * st, axis=1, keepdims=True)
        last_row = lax.broadcasted_iota(jnp.int32, (nc, HG_CHUNK, HG_EXPAND), 1) == HG_CHUNK - 1
        flat = lambda a3: a3.reshape(t, HG_EXPAND)
        dlf = _chunk_cumsum(flat(dg + jnp.where(last_row, dgl, 0.0)), reverse=True)
        dfg = dlf / fg - flat(dk)
        dlb = jnp.sum(dfg * (1.0 - sig), axis=0, keepdims=True)
        dl0 = dlb * lb * (1.0 - lb)
        dlbl = jnp.concatenate([dl0, -dl0], axis=0)

        @pl.when(b == 0)
        def _():
            dlbl_ref[...] = dlbl

        @pl.when(b > 0)
        def _():
            dlbl_ref[...] += dlbl

        dp_ref[0] = (flat(dq) * HG_EXPAND ** -0.5 * _dsilu(pq)).astype(BF16)
        dp_ref[1] = (dfg * (1.0 - lb) * sig * (1.0 - sig)).astype(BF16)
        dp_ref[2] = flat(dv).astype(BF16)

    blk = pl.BlockSpec((t, HG_EXPAND), lambda h, b: (nb - 1 - b, h))
    pblk = pl.BlockSpec((4, t, HG_EXPAND), lambda h, b: (0, nb - 1 - b, h))
    return pl.pallas_call(
        body, name="hgrn2_bwd", grid=(heads, nb),
        in_specs=[pblk, pl.BlockSpec((2, HG_EXPAND), lambda h, b: (0, h)), pl.BlockSpec((1, HG_EXPAND), lambda h, b: (0, 0)),
                  blk, blk, pl.BlockSpec((None, nc, HG_EXPAND, HG_EXPAND), lambda h, b: (h, nb - 1 - b, 0, 0)), blk],
        out_specs=[pblk, pl.BlockSpec((2, HG_EXPAND), lambda h, b: (0, h)), pl.BlockSpec((1, HG_EXPAND), lambda h, b: (0, 0))],
        out_shape=[jax.ShapeDtypeStruct((4, s, d), BF16), jax.ShapeDtypeStruct((2, d), F32), jax.ShapeDtypeStruct((1, HG_EXPAND), F32)],
        scratch_shapes=[pltpu.VMEM((HG_EXPAND, HG_EXPAND), F32), pltpu.VMEM((nc, 1, HG_EXPAND), F32),
                        pltpu.VMEM((nc, HG_EXPAND, HG_EXPAND), F32)],
        compiler_params=_params(("arbitrary", "arbitrary")),
    )(p, lb_logits, out_gain, o, dog, states, gsum)


HALO = 8
FFN_FWD_ROWS = 512
FFN_BWD_ROWS = 256


def _shift_down(xv, n):
    return pltpu.roll(xv, n, axis=0)


def _shift_up(xv, n):
    return pltpu.roll(xv, xv.shape[0] - n, axis=0)


def _ffn_hidden_down(name, u, conv_w, conv_b, w_down, h):
    _, nj, s, fb = u.shape
    d = w_down.shape[2]
    tm = _row_tile(s, FFN_FWD_ROWS)
    per = tm // HALO

    def body(gate_ref, prev_ref, val_ref, w_ref, b_ref, wd_ref, h_ref, hid_ref, conv_ref, o_ref):
        i = pl.program_id(0)
        total = h_ref[...]
        for j in range(nj):
            prev = jnp.where(i > 0, prev_ref[j].astype(F32), 0.0)
            ext = jnp.concatenate([prev, gate_ref[j].astype(F32)], axis=0)
            conv = b_ref[j] + w_ref[j, 2:3, :] * ext[HALO:]
            conv = conv + w_ref[j, 1:2, :] * _shift_down(ext, 1)[HALO:]
            conv = conv + w_ref[j, 0:1, :] * _shift_down(ext, 2)[HALO:]
            conv = conv.astype(BF16)
            conv_ref[j] = conv
            hidden = _silu(conv) * val_ref[j]
            hid_ref[j] = hidden
            total = total + _dot(hidden, wd_ref[j])
        o_ref[...] = total

    row = pl.BlockSpec((tm, d), lambda i: (i, 0))
    return pl.pallas_call(
        body, name=name, grid=(s // tm,),
        in_specs=[pl.BlockSpec((None, nj, tm, fb), lambda i: (0, 0, i, 0)),
                  pl.BlockSpec((None, nj, HALO, fb), lambda i: (0, 0, jnp.maximum(i * per - 1, 0), 0)),
                  pl.BlockSpec((None, nj, tm, fb), lambda i: (1, 0, i, 0)),
                  pl.BlockSpec((nj, CONV_WIDTH, fb), lambda i: (0, 0, 0)), pl.BlockSpec((nj, 1, fb), lambda i: (0, 0, 0)),
                  pl.BlockSpec((nj, fb, d), lambda i: (0, 0, 0)), row],
        out_specs=[pl.BlockSpec((nj, tm, fb), lambda i: (0, i, 0)), pl.BlockSpec((nj, tm, fb), lambda i: (0, i, 0)), row],
        out_shape=[jax.ShapeDtypeStruct((nj, s, fb), BF16), jax.ShapeDtypeStruct((nj, s, fb), BF16), jax.ShapeDtypeStruct((s, d), F32)],
        compiler_params=_params(("parallel",)),
    )(u, u, u, conv_w, conv_b, w_down, h)


def _ffn_hidden_up_bwd(name, u, conv, dh, conv_w, w_up, h, gain, dres):
    _, nj, s, fb = u.shape
    d = w_up.shape[2]
    tm = _row_tile(s, FFN_BWD_ROWS)
    per = tm // HALO
    nblk = s // HALO
    ni = s // tm

    def body(gate_ref, conv_ref, cnext_ref, val_ref, vnext_ref, dh_ref, dhnext_ref, w_ref, wu_ref, h_ref, gain_ref, dres_ref,
             du_ref, dw_ref, db_ref, dx_ref, dxb_ref, dgain_ref):
        i = pl.program_id(0)
        has_next = i < ni - 1
        total = None
        for j in range(nj):
            act, dact = _silu_and_grad(conv_ref[j])
            dval = dh_ref[j] * act
            after = jnp.where(has_next, dhnext_ref[j].astype(F32), 0.0) * vnext_ref[j].astype(F32) * _dsilu(cnext_ref[j].astype(F32))
            dconv = jnp.concatenate([(dh_ref[j] * val_ref[j] * dact).astype(F32), after], axis=0)
            taps = [_shift_up(dconv, 2)[:tm], _shift_up(dconv, 1)[:tm], dconv[:tm]]
            dgate = (w_ref[j, 0:1, :] * taps[0] + w_ref[j, 1:2, :] * taps[1] + w_ref[j, 2:3, :] * taps[2]).astype(BF16)
            du_ref[0, j] = dgate
            du_ref[1, j] = dval
            part = _dot(dgate, wu_ref[j]) + _dot(dval, wu_ref[nj + j])
            total = part if total is None else total + part
            gate = gate_ref[j].astype(F32)
            dw = jnp.concatenate([jnp.sum(tap * gate, axis=0, keepdims=True) for tap in taps], axis=0)
            db = jnp.sum(taps[2], axis=0, keepdims=True)

            @pl.when(i == 0)
            def _():
                dw_ref[j] = dw
                db_ref[j] = db

            @pl.when(i > 0)
            def _():
                dw_ref[j] += dw
                db_ref[j] += db

        xv = h_ref[...]
        r = lax.rsqrt(jnp.mean(xv * xv, axis=-1, keepdims=True) + EPS)
        xhat = xv * r
        dgain = jnp.sum(total * xhat, axis=0, keepdims=True)

        @pl.when(i == 0)
        def _():
            dgain_ref[...] = dgain

        @pl.when(i > 0)
        def _():
            dgain_ref[...] += dgain

        dxhat = total * gain_ref[...]
        dx = dres_ref[...] + r * (dxhat - xhat * jnp.mean(dxhat * xhat, axis=-1, keepdims=True))
        dx_ref[...] = dx
        dxb_ref[...] = dx.astype(BF16)

    def tile(part):
        return pl.BlockSpec((None, nj, tm, fb), lambda i: (part, 0, i, 0))

    def after(part):
        return pl.BlockSpec((None, nj, HALO, fb), lambda i: (part, 0, jnp.minimum((i + 1) * per, nblk - 1), 0))

    row = pl.BlockSpec((tm, d), lambda i: (i, 0))
    own = pl.BlockSpec((nj, tm, fb), lambda i: (0, i, 0))
    nxt = pl.BlockSpec((nj, HALO, fb), lambda i: (0, jnp.minimum((i + 1) * per, nblk - 1), 0))
    return pl.pallas_call(
        body, name=name, grid=(ni,),
        in_specs=[tile(0), own, nxt, tile(1), after(1), own, nxt,
                  pl.BlockSpec((nj, CONV_WIDTH, fb), lambda i: (0, 0, 0)),
                  pl.BlockSpec((2 * nj, fb, d), lambda i: (0, 0, 0)), row, pl.BlockSpec((1, d), lambda i: (0, 0)), row],
        out_specs=[pl.BlockSpec((2, nj, tm, fb), lambda i: (0, 0, i, 0)),
                   pl.BlockSpec((nj, CONV_WIDTH, fb), lambda i: (0, 0, 0)), pl.BlockSpec((nj, 1, fb), lambda i: (0, 0, 0)),
                   row, row, pl.BlockSpec((1, d), lambda i: (0, 0))],
        out_shape=[jax.ShapeDtypeStruct((2, nj, s, fb), BF16), jax.ShapeDtypeStruct((nj, CONV_WIDTH, fb), F32),
                   jax.ShapeDtypeStruct((nj, 1, fb), F32), jax.ShapeDtypeStruct((s, d), F32), jax.ShapeDtypeStruct((s, d), BF16),
                   jax.ShapeDtypeStruct((1, d), F32)],
        compiler_params=_params(("arbitrary",)),
    )(u, conv, conv, u, u, dh, dh, conv_w, w_up, h, gain, dres)


ATT_TILE = 512


def _stack_heads(ref, rows, first_head, count):
    hd = ATT_HEAD_DIM
    return jnp.concatenate([ref[rows, (first_head + j) * hd:(first_head + j + 1) * hd] for j in range(count)], axis=0)


def _unstack_heads(stacked, ref, rows, first_head, count):
    hd = ATT_HEAD_DIM
    for pair in range(count // 2):
        both = [stacked[(2 * pair + j) * WINDOW:(2 * pair + j + 1) * WINDOW, :] for j in range(2)]
        ref[rows, (first_head + 2 * pair) * hd:(first_head + 2 * pair + 2) * hd] = jnp.concatenate(both, axis=1).astype(ref.dtype)


def _attn_bias(first_head, count, n_heads, first):
    lanes = count * WINDOW
    ik = lax.broadcasted_iota(jnp.int32, (2 * WINDOW, lanes), 0)
    iq = lax.broadcasted_iota(jnp.int32, (2 * WINDOW, lanes), 1) % WINDOW
    dist = iq + WINDOW - ik
    valid = (dist >= 0) & (dist < WINDOW) & (ik >= (WINDOW if first else 0))
    slope = jnp.concatenate([jnp.zeros((1, WINDOW), F32) + 2.0 ** (-8.0 * (first_head + j + 1) / n_heads) for j in range(count)], axis=1)
    return jnp.where(valid, -slope * dist.astype(F32), NEG)


def _fill_attn_bias(bias_ref, group, n_heads):
    @pl.when(pl.program_id(0) == 0)
    def _():
        for g in range(ATT_KV_HEADS):
            bias_ref[0, g] = _attn_bias(g * group, group, n_heads, False)
            bias_ref[1, g] = _attn_bias(g * group, group, n_heads, True)


def _attn_probs_t(kb_scaled, qs, sink_ref, first_head, count, bias):
    sink = jnp.concatenate([jnp.zeros((1, WINDOW), F32) + sink_ref[0, first_head + j] for j in range(count)], axis=1)
    sc = _dot(kb_scaled, qs, NT) + bias
    m = jnp.maximum(jnp.max(sc, axis=0, keepdims=True), sink)
    e = jnp.exp(sc - m)
    es = jnp.exp(sink - m)
    inv = 1.0 / (jnp.sum(e, axis=0, keepdims=True) + es)
    return e * inv, es * inv


ATT_SCALE = ATT_HEAD_DIM ** -0.5


def _attn_specs(s, d, kvd, tq):
    per = tq // WINDOW
    return [pl.BlockSpec((tq, d), lambda i: (i, 0)), pl.BlockSpec((tq, kvd), lambda i: (i, 0)),
            pl.BlockSpec((WINDOW, kvd), lambda i: (jnp.maximum(i * per - 1, 0), 0))]


def _attn_fwd(q, kv, sinks):
    s, d = q.shape
    kvd = kv.shape[1]
    half = kvd // 2
    hd = ATT_HEAD_DIM
    nq = d // hd
    group = nq // ATT_KV_HEADS
    tq = min(s, ATT_TILE)
    per = tq // WINDOW

    def body(q_ref, kvc_ref, kvp_ref, sink_ref, o_ref, band, bias_ref):
        i = pl.program_id(0)
        _fill_attn_bias(bias_ref, group, nq)
        band[0:WINDOW, :] = kvp_ref[...]
        band[WINDOW:, :] = kvc_ref[...]

        def block(b, carry):
            rows = pl.ds(pl.multiple_of(b * WINDOW, WINDOW), WINDOW)
            keys = pl.ds(pl.multiple_of(b * WINDOW, WINDOW), 2 * WINDOW)
            first = (i * per + b) == 0
            for g in range(ATT_KV_HEADS):
                bias = jnp.where(first, bias_ref[1, g], bias_ref[0, g])
                p, _ = _attn_probs_t(band[keys, g * hd:(g + 1) * hd] * ATT_SCALE, _stack_heads(q_ref, rows, g * group, group), sink_ref,
                                     g * group, group, bias)
                out_t = _dot(band[keys, half + g * hd:half + (g + 1) * hd], p, TN)
                _unstack_heads(out_t.T, o_ref, rows, g * group, group)
            return carry

        lax.fori_loop(0, per, block, 0)

    return pl.pallas_call(
        body, name="attn_fwd", grid=(s // tq,),
        in_specs=_attn_specs(s, d, kvd, tq) + [pl.BlockSpec(memory_space=pltpu.SMEM)],
        out_specs=pl.BlockSpec((tq, d), lambda i: (i, 0)), out_shape=jax.ShapeDtypeStruct((s, d), BF16),
        scratch_shapes=[pltpu.VMEM((tq + WINDOW, kvd), BF16), pltpu.VMEM((2, ATT_KV_HEADS, 2 * WINDOW, group * WINDOW), F32)],
        compiler_params=_params(("arbitrary",)),
    )(q, kv, kv, sinks)


def _attn_bwd(q, kv, o, do, sinks):
    s, d = q.shape
    kvd = kv.shape[1]
    half = kvd // 2
    hd = ATT_HEAD_DIM
    nq = d // hd
    group = nq // ATT_KV_HEADS
    tq = min(s, ATT_TILE)
    per = tq // WINDOW
    nt = s // tq

    def body(q_ref, kvc_ref, kvp_ref, o_ref, do_ref, sink_ref, dq_ref, dkvc_ref, dkvp_ref, ds_ref, band, dband, bias_ref):
        i = pl.program_id(0)
        _fill_attn_bias(bias_ref, group, nq)
        band[0:WINDOW, :] = kvp_ref[...]
        band[WINDOW:, :] = kvc_ref[...]
        dband[...] = jnp.zeros_like(dband)
        ds_ref[...] = jnp.zeros_like(ds_ref)

        def block(b, carry):
            rows = pl.ds(pl.multiple_of(b * WINDOW, WINDOW), WINDOW)
            keys = pl.ds(pl.multiple_of(b * WINDOW, WINDOW), 2 * WINDOW)
            first = (i * per + b) == 0
            dks, dvs = [], []
            for g in range(ATT_KV_HEADS):
                kb = band[keys, g * hd:(g + 1) * hd] * ATT_SCALE
                vb = band[keys, half + g * hd:half + (g + 1) * hd]
                qs = _stack_heads(q_ref, rows, g * group, group)
                dos = _stack_heads(do_ref, rows, g * group, group)
                p, ps = _attn_probs_t(kb, qs, sink_ref, g * group, group, jnp.where(first, bias_ref[1, g], bias_ref[0, g]))
                prod = dos.astype(F32) * _stack_heads(o_ref, rows, g * group, group).astype(F32)
                dsum = lax.dot_general(jnp.ones((8, hd), F32), prod, NT, precision=lax.Precision.HIGHEST,
                                       preferred_element_type=F32)[0:1, :]
                dsc = p * (_dot(vb, dos, NT) - dsum)
                dvs.append(_dot(p, dos))
                dks.append(_dot(dsc, qs * ATT_SCALE))
                _unstack_heads(_dot(kb, dsc, TN).T, dq_ref, rows, g * group, group)
                gone = ps * dsum
                for j in range(group):
                    ds_ref[g * group + j:g * group + j + 1, :] += jnp.zeros((1, 128), F32) - jnp.sum(gone[:, j * WINDOW:(j + 1) * WINDOW])
            dband[keys, 0:half] += jnp.concatenate(dks, axis=1)
            dband[keys, half:] += jnp.concatenate(dvs, axis=1)
            return carry

        lax.fori_loop(0, per, block, 0)
        dkvp_ref[...] = dband[0:WINDOW, :]
        dkvc_ref[...] = dband[WINDOW:, :]

    big = pl.BlockSpec((tq, d), lambda i: (i, 0))
    return pl.pallas_call(
        body, name="attn_bwd", grid=(nt,),
        in_specs=_attn_specs(s, d, kvd, tq) + [big, big, pl.BlockSpec(memory_space=pltpu.SMEM)],
        out_specs=[big, pl.BlockSpec((tq, kvd), lambda i: (i, 0)), pl.BlockSpec((None, WINDOW, kvd), lambda i: (i, 0, 0)),
                   pl.BlockSpec((None, nq, 128), lambda i: (i, 0, 0))],
        out_shape=[jax.ShapeDtypeStruct((s, d), BF16), jax.ShapeDtypeStruct((s, kvd), F32), jax.ShapeDtypeStruct((nt, WINDOW, kvd), F32),
                   jax.ShapeDtypeStruct((nt, nq, 128), F32)],
        scratch_shapes=[pltpu.VMEM((tq + WINDOW, kvd), BF16), pltpu.VMEM((tq + WINDOW, kvd), F32),
                        pltpu.VMEM((2, ATT_KV_HEADS, 2 * WINDOW, group * WINDOW), F32)],
        compiler_params=_params(("arbitrary",)),
    )(q, kv, kv, o, do, sinks)


HBM_SPEC = pl.BlockSpec(memory_space=pltpu.HBM)
VMEM_SPEC = pl.BlockSpec(memory_space=pltpu.VMEM)


def _place():
    return lax.axis_index("x"), lax.axis_index("y"), lax.axis_index("c")


def _flip(pos, r):
    return tuple(1 - p if (r >> (2 - a)) & 1 else p for a, p in enumerate(pos))


def _index(pos):
    return 4 * pos[0] + 2 * pos[1] + pos[2]


def _all_gather(name, shards, spec):
    n = len(shards)

    def body(*refs):
        x_refs, o_refs = refs[:n], refs[n:2 * n]
        send_sems, recv_sems, local_sems = refs[2 * n:]
        me = _place()
        sibling = _flip(me, 1)
        far = [_flip(me, r) for r in (4, 2, 6)]

        def copy(t, sem, block, to, src=None):
            rows = o_refs[t].at[_index(block)]
            return pltpu.make_async_remote_copy(
                src_ref=rows if src is None else src, dst_ref=rows, send_sem=send_sems.at[t, sem], recv_sem=recv_sems.at[t, sem],
                device_id=to, device_id_type=MESH)

        own = [pltpu.make_async_copy(x_refs[t], o_refs[t].at[_index(me)], local_sems.at[t]) for t in range(n)]
        for cp in own:
            cp.start()
        first = []
        for t in range(n):
            first.append(copy(t, 0, me, sibling, src=x_refs[t]))
            first += [copy(t, 1 + j, me, peer, src=x_refs[t]) for j, peer in enumerate(far)]
        for cp in first:
            cp.start()
        passed = []
        for j, peer in enumerate(far):
            for t in range(n):
                copy(t, 1 + j, peer, me).wait_recv()
                cp = copy(t, 4 + j, peer, sibling)
                cp.start()
                passed.append(cp)
        for t in range(n):
            copy(t, 0, sibling, me).wait_recv()
            for j, peer in enumerate(far):
                copy(t, 4 + j, _flip(peer, 1), me).wait_recv()
        for cp in first + passed:
            cp.wait_send()
        for cp in own:
            cp.wait()

    return pl.pallas_call(
        body, name=name, in_specs=[spec] * n, out_specs=[spec] * n,
        out_shape=[jax.ShapeDtypeStruct((N_DEV,) + sh.shape, sh.dtype) for sh in shards],
        scratch_shapes=[pltpu.SemaphoreType.DMA((n, 7)), pltpu.SemaphoreType.DMA((n, 7)), pltpu.SemaphoreType.DMA((n,))],
    )(*shards)


SEM_SPEC = pl.BlockSpec(memory_space=pltpu.SEMAPHORE)
ANY_SPEC = pl.BlockSpec(memory_space=pl.ANY)


def _landing(own, mine):
    return lax.dynamic_update_slice(lax.empty((N_DEV,) + own.shape, own.dtype), own[None], (mine,) + (0,) * own.ndim)


def _peer_copies(src_refs, land_refs, send_sems, recv_sems, scatter, arrivals):
    me = _place()
    mine = _index(me)
    copies = []
    for t, (src, land) in enumerate(zip(src_refs, land_refs)):
        for r in range(1, N_DEV):
            peer = _flip(me, r)
            theirs = _index(peer)
            sem = t * N_DEV + r - 1
            copies.append(pltpu.make_async_remote_copy(
                src_ref=src.at[theirs] if scatter else src, dst_ref=land.at[theirs if arrivals else mine],
                send_sem=send_sems.at[sem], recv_sem=recv_sems.at[sem], device_id=peer, device_id_type=MESH))
    return copies


def _own_copies(src_refs, land_refs, send_sems):
    mine = _index(_place())
    return [pltpu.make_async_copy(src.at[mine], land.at[mine], send_sems.at[t * N_DEV + N_DEV - 1])
            for t, (src, land) in enumerate(zip(src_refs, land_refs))]


def _send_start(name, sources, lands, scatter, after=None, carry=None):
    n = len(sources)
    extra = [a for a in (after, carry) if a is not None]
    token = jax.ShapeDtypeStruct((8, 128), F32) if carry is None else jax.ShapeDtypeStruct(carry.shape, carry.dtype)

    def body(*refs):
        outs = refs[2 * n + len(extra):]
        for out in _peer_copies(refs[:n], refs[n:2 * n], outs[0], outs[1], scatter, False) + (_own_copies(refs[:n], refs[n:2 * n], outs[0]) if scatter else []):
            out.start()
        outs[-1][...] = jnp.zeros_like(outs[-1]) if carry is None else refs[2 * n + len(extra) - 1][...]

    outs = pl.pallas_call(
        body, name=name, in_specs=[HBM_SPEC] * (2 * n) + [ANY_SPEC] * (after is not None) + [VMEM_SPEC] * (carry is not None),
        out_specs=[SEM_SPEC, SEM_SPEC] + [HBM_SPEC] * (2 * n) + [VMEM_SPEC],
        out_shape=[pltpu.SemaphoreType.DMA((n * N_DEV,)), pltpu.SemaphoreType.DMA((n * N_DEV,))]
        + [pltpu.HBM(a.shape, a.dtype) for a in list(sources) + list(lands)] + [token],
        input_output_aliases={i: 2 + i for i in range(2 * n)},
        compiler_params=pltpu.CompilerParams(has_side_effects=pltpu.SideEffectType.DATAFLOW_SIDE_EFFECTING),
    )(*[pltpu.with_memory_space_constraint(a, pltpu.HBM) for a in list(sources) + list(lands)], *extra)
    return outs[0], outs[1], outs[2:2 + n], outs[2 + n:2 + 2 * n], outs[-1]


def _send_wait(name, started, after, scatter):
    send_sems, recv_sems, sources, lands, _ = started
    n = len(sources)

    def body(*refs):
        for out in _peer_copies(refs[:n], refs[n:2 * n], refs[2 * n], refs[2 * n + 1], scatter, False):
            out.wait_send()
        for own in _own_copies(refs[:n], refs[n:2 * n], refs[2 * n]) if scatter else []:
            own.wait()
        for arrival in _peer_copies(refs[:n], refs[n:2 * n], refs[2 * n], refs[2 * n + 1], scatter, True):
            arrival.wait_recv()

    outs = pl.pallas_call(
        body, name=name, in_specs=[HBM_SPEC] * (2 * n) + [SEM_SPEC, SEM_SPEC, ANY_SPEC], out_specs=[HBM_SPEC] * (2 * n),
        out_shape=[pltpu.HBM(a.shape, a.dtype) for a in list(sources) + list(lands)],
        input_output_aliases={i: i for i in range(2 * n)},
        compiler_params=pltpu.CompilerParams(has_side_effects=pltpu.SideEffectType.DATAFLOW_SIDE_EFFECTING),
    )(*sources, *lands, send_sems, recv_sems, after)
    return outs[n:]


def _pack_rows(parts):
    offsets, row = [], 0
    for part in parts:
        offsets.append(row)
        row += part.shape[0]
    return offsets, -(-row // 8) * 8, -(-max(part.shape[1] for part in parts) // 128) * 128


def _pack(name, parts):
    offsets, rows, width = _pack_rows(parts)

    def body(*refs):
        o_ref = refs[-1]
        o_ref[...] = jnp.zeros_like(o_ref)
        for off, ref in zip(offsets, refs[:-1]):
            o_ref[off:off + ref.shape[0], 0:ref.shape[1]] = ref[...]

    return pl.pallas_call(body, name=name, in_specs=[VMEM_SPEC] * len(parts), out_specs=VMEM_SPEC,
                          out_shape=jax.ShapeDtypeStruct((rows, width), F32))(*parts)


def _adamw_math(w, g, m, v):
    m = ADAM_B1 * m + (1.0 - ADAM_B1) * g
    v = ADAM_B2 * v + (1.0 - ADAM_B2) * (g * g)
    m_hat = m * (1.0 / (1.0 - ADAM_B1 ** ADAM_STEP))
    denom = jnp.sqrt(v * (1.0 / (1.0 - ADAM_B2 ** ADAM_STEP))) + ADAM_EPS
    inv = pl.reciprocal(denom, approx=True)
    inv = inv * (2.0 - denom * inv)
    return -ADAM_LR * (m_hat * inv + ADAM_WD * w), m, v


def _adamw_step(w_ref, m_ref, v_ref, p_ref, g_ref, d_ref, nm_ref, nv_ref):
    g = p_ref[0].astype(F32)
    for dev in range(1, N_DEV):
        g = g + p_ref[dev].astype(F32)
    g_ref[...] = g
    d_ref[...], nm_ref[...], nv_ref[...] = _adamw_math(w_ref[...], g, m_ref[...], v_ref[...])


def _adamw_rows(rows):
    return max(t for t in range(8, min(rows, 256) + 1, 8) if rows % t == 0)


def _adamw_shard(name, w, m, v, partials):
    rows, cols = w.shape
    tr = _adamw_rows(rows)
    blk = pl.BlockSpec((tr, cols), lambda i: (i, 0))
    return pl.pallas_call(
        _adamw_step_fn(), name=name, grid=(rows // tr,), in_specs=[blk, blk, blk, pl.BlockSpec((N_DEV, tr, cols), lambda i: (0, i, 0))],
        out_specs=[blk] * 4, out_shape=[jax.ShapeDtypeStruct((rows, cols), F32)] * 4, compiler_params=_params(("parallel",)),
    )(w, m, v, partials)


def _adamw_step_fn():
    return functools.partial(_adamw_step)


def _adamw_layers(name, w, m, v, partials):
    layers, rows, cols = w.shape
    tr = _adamw_rows(rows)
    last = rows // tr - 1

    def body(w_ref, m_ref, v_ref, *rest):
        for layer in range(layers):
            @pl.when(pl.program_id(0) == layer)
            def _():
                _adamw_step(w_ref, m_ref, v_ref, rest[layer], *rest[layers:])

    blk = pl.BlockSpec((None, tr, cols), lambda l, i: (l, i, 0))
    part = lambda layer: pl.BlockSpec((N_DEV, tr, cols), lambda l, i: (0, jnp.where(l == layer, i, jnp.where(l < layer, 0, last)), 0))
    return pl.pallas_call(
        body, name=name, grid=(layers, rows // tr), in_specs=[blk, blk, blk] + [part(layer) for layer in range(layers)],
        out_specs=[blk] * 4, out_shape=[jax.ShapeDtypeStruct(w.shape, F32)] * 4, compiler_params=_params(("arbitrary", "arbitrary")),
    )(w, m, v, *partials)


def _adamw_small(gathered, places, entries):
    n = len(entries)
    np_ = len(gathered)

    def body(*refs):
        pack_refs = refs[:np_]
        refs = refs[np_ - 1:]
        w_refs, m_refs, v_refs = refs[1:1 + n], refs[1 + n:1 + 2 * n], refs[1 + 2 * n:1 + 3 * n]
        outs = refs[1 + 3 * n:]
        totals = []
        for pack_ref in pack_refs:
            acc = pack_ref[0]
            for dev in range(1, N_DEV):
                acc = acc + pack_ref[dev]
            totals.append(acc)
        mine = _index(_place())
        for e in range(n):
            rows, cols = w_refs[e].shape
            total, off = totals[places[e][0]], places[e][1]
            if entries[e][3]:
                g = jnp.zeros((rows, cols), F32)
                for dev in range(N_DEV):
                    g = g + jnp.where(mine == dev, total[off + dev * rows:off + (dev + 1) * rows, 0:cols], 0.0)
            else:
                g = total[off:off + rows, 0:cols]
            outs[4 * e][...] = g
            outs[4 * e + 1][...], outs[4 * e + 2][...], outs[4 * e + 3][...] = _adamw_math(w_refs[e][...], g, m_refs[e][...], v_refs[e][...])
        outs[4 * n][...] = totals[places[n][0]][places[n][1]:places[n][1] + 1, 0:128]

    shapes = []
    for w, _, _, _ in entries:
        shapes += [jax.ShapeDtypeStruct(w.shape, F32)] * 4
    shapes.append(jax.ShapeDtypeStruct((1, 128), F32))
    return pl.pallas_call(
        body, name="adamw_small", in_specs=[VMEM_SPEC] * (np_ + 3 * n), out_specs=[VMEM_SPEC] * len(shapes), out_shape=shapes,
        compiler_params=pltpu.CompilerParams(vmem_limit_bytes=VMEM_LIMIT),
    )(*gathered, *[e[0] for e in entries], *[e[1] for e in entries], *[e[2] for e in entries])


def _ffn_forward(tag, h, gain, w_up, late):
    s, d = h.shape
    fb = w_up.shape[1]
    tm = _row_tile(s, 2 * MM_ROWS)
    (a,), (u,) = _norm_proj(f"ffn_up_{tag}", h, [
        (gain, w_up, pl.BlockSpec((None, fb, d), lambda i, j: (j, 0, 0)), NT,
         pl.BlockSpec((None, None, tm, fb), lambda i, j: (j // 4, j % 4, i, 0)), jax.ShapeDtypeStruct((2, 4, s, fb), BF16))], tm=tm, nj=N_DEV)
    w_down, conv_w, conv_b = late(u)
    hidden, conv, out = _ffn_hidden_down(f"ffn_hidden_down_{tag}", u, conv_w, conv_b, w_down, h)
    return out, (a, u, hidden, conv)


def _ffn_backward(tag, h, gain, w_up, w_down, conv_w, conv_b, saved, dout, sent=None):
    a, u, hidden, conv = saved
    dout, dout_bf = dout
    s, d = h.shape
    fb = w_up.shape[1]
    tm = _row_tile(s, MM_ROWS)
    dhidden = _matmul(
        f"ffn_down_bwd_{tag}", dout_bf, w_down, dims=NT, grid=(s // tm, 4, 1),
        a_spec=pl.BlockSpec((tm, d), lambda i, j, k: (i, 0)),
        b_spec=pl.BlockSpec((None, fb, d), lambda i, j, k: (j, 0, 0)),
        o_spec=pl.BlockSpec((None, tm, fb), lambda i, j, k: (j, i, 0)),
        out_shape=jax.ShapeDtypeStruct((4, s, fb), BF16))
    dw_down = _matmul(
        f"ffn_down_grad_{tag}", hidden, dout_bf, dims=TN, grid=(4, 1, 1),
        a_spec=pl.BlockSpec((None, s, fb), lambda i, j, k: (i, 0, 0)),
        b_spec=pl.BlockSpec((s, d), lambda i, j, k: (0, 0)),
        o_spec=pl.BlockSpec((None, fb, d), lambda i, j, k: (i, 0, 0)),
        out_shape=jax.ShapeDtypeStruct((4, fb, d), BF16))
    if sent is not None:
        gain = sent(dw_down, gain)
    du, dconv_w, dconv_b, dh, dh_bf, dgain = _ffn_hidden_up_bwd(f"ffn_hidden_up_bwd_{tag}", u, conv, dhidden, conv_w, w_up, h, gain, dout)
    dw_up = _matmul(
        f"ffn_up_grad_{tag}", du, a, dims=TN, grid=(N_DEV, 1, 1),
        a_spec=pl.BlockSpec((None, None, s, fb), lambda i, j, k: (i // 4, i % 4, 0, 0)),
        b_spec=pl.BlockSpec((s, d), lambda i, j, k: (0, 0)),
        o_spec=pl.BlockSpec((None, fb, d), lambda i, j, k: (i, 0, 0)),
        out_shape=jax.ShapeDtypeStruct((N_DEV, fb, d), BF16))
    return (dh, dh_bf), dgain, dw_up, dw_down, dconv_w, dconv_b


def kernel(x, hg_norm, hg_w_in, hg_lb_logits, hg_out_norm, hg_w_out, kv_norm, w_kv, attn_norm, attn_w_q, attn_sinks, attn_w_o, ffn_norm, ffn_w_up, ffn_conv_w, ffn_conv_b, ffn_w_down, final_norm, loss_target, m_hg_norm, m_hg_w_in, m_hg_lb_logits, m_hg_out_norm, m_hg_w_out, m_kv_norm, m_w_kv, m_attn_norm, m_attn_w_q, m_attn_sinks, m_attn_w_o, m_ffn_norm, m_ffn_w_up, m_ffn_conv_w, m_ffn_conv_b, m_ffn_w_down, m_final_norm, v_hg_norm, v_hg_w_in, v_hg_lb_logits, v_hg_out_norm, v_hg_w_out, v_kv_norm, v_w_kv, v_attn_norm, v_attn_w_q, v_attn_sinks, v_attn_w_o, v_ffn_norm, v_ffn_w_up, v_ffn_conv_w, v_ffn_conv_b, v_ffn_w_down, v_final_norm):
    _, s, d = x.shape
    x0, target = x[0], loss_target[0]
    half = hg_w_in.shape[2]
    fs = ffn_conv_w.shape[2]
    fb = 2 * fs
    kvd = w_kv.shape[1]
    nq = d // ATT_HEAD_DIM
    tm = _row_tile(s, MM_ROWS)

    mine = _index(_place())
    gather = lambda tag, shards, after, carry=None: _send_start("gather_start_" + tag, shards, [_landing(a, mine) for a in shards], False, after, carry)
    w_in, g_hgn, g_lbl, w_out = _all_gather("gather_hg", [hg_w_in[0].astype(BF16), hg_norm, hg_lb_logits, hg_w_out[0].astype(BF16)], HBM_SPEC)
    w_out = w_out.reshape(d, d)
    up_t = lambda a: jnp.swapaxes(a, -1, -2)
    coming_up0 = gather("ffn_up0", [up_t(ffn_w_up[0]).astype(BF16)], None, g_hgn.reshape(1, d))
    hgn = coming_up0[4]
    lbl = g_lbl.transpose(1, 0, 2).reshape(2, d)
    conv_b = [ffn_conv_b[layer].reshape(4, 1, fb) for layer in range(2)]
    gains = [ffn_norm[0:1], ffn_norm[1:2]]
    kvn, fin = kv_norm.reshape(1, d), final_norm.reshape(1, d)

    t2 = _row_tile(s, 2 * MM_ROWS)
    (a0,), (p,) = _norm_proj("hg_in", x0, [
        (hgn, w_in, pl.BlockSpec((None, d, half), lambda i, j: (j, 0, 0)), NN,
         pl.BlockSpec((None, t2, half), lambda i, j: (j // 2, i, j % 2)), jax.ShapeDtypeStruct((4, s, d), BF16))], tm=t2, nj=N_DEV)
    o, og, states, gsum = _hgrn2_fwd(p, lbl, hg_out_norm)
    coming_dn0 = gather("ffn_down0", [ffn_conv_w, ffn_w_down[0].astype(BF16)], o)
    x1 = _mm_rows("hg_out", og, w_out, out_dtype=F32, add=x0, after=coming_dn0[4])
    w_up0, = _send_wait("gather_wait_ffn_up0", coming_up0, x1, False)
    coming_attn = gather("attn", [w_kv.astype(BF16), attn_w_q[0].astype(BF16), attn_w_o[0].astype(BF16)], w_up0, gains[0])
    gains[0] = coming_attn[4]
    w_up, w_dn, conv_w, coming = [w_up0, None], [None, None], [], {}

    def late0(u):
        g_cw, w_dn0 = _send_wait("gather_wait_ffn_down0", coming_dn0, u, False)
        w_dn[0] = w_dn0.reshape(4, fb, d)
        conv_w.extend(g_cw[:, layer].reshape(4, 2, CONV_WIDTH, fs).transpose(0, 2, 1, 3).reshape(4, CONV_WIDTH, fb) for layer in range(2))
        coming["up1"] = gather("ffn_up1", [up_t(ffn_w_up[1]).astype(BF16)], w_dn0, conv_b[0])
        return w_dn[0], conv_w[0], coming["up1"][4]

    x2, saved0 = _ffn_forward("0", x1, gains[0], w_up[0], late0)
    w_kvg, w_q, w_o = _send_wait("gather_wait_attn", coming_attn, x2, False)
    w_kvg, w_q, w_o = w_kvg.reshape(d, kvd), w_q.reshape(d, d), w_o.reshape(d, d)
    (akv, a2), (kv, q) = _norm_proj("attn_in", x2, [
        (kvn, w_kvg, pl.BlockSpec((d, kvd), lambda i, j: (0, 0)), NN, pl.BlockSpec((tm, kvd), lambda i, j: (i, 0)), jax.ShapeDtypeStruct((s, kvd), BF16)),
        (attn_norm, w_q, pl.BlockSpec((d, d), lambda i, j: (0, 0)), NN, pl.BlockSpec((tm, d), lambda i, j: (i, 0)), jax.ShapeDtypeStruct((s, d), BF16))],
        tm=tm, nj=1)
    coming_dn1 = gather("ffn_down1", [ffn_w_down[1].astype(BF16)], q, attn_sinks)
    att = _attn_fwd(q, kv, coming_dn1[4])
    x3 = _mm_rows("attn_out", att, w_o, out_dtype=F32, add=x2)
    w_up[1], = _send_wait("gather_wait_ffn_up1", coming["up1"], x3, False)

    def late1(u):
        w_dn[1] = _send_wait("gather_wait_ffn_down1", coming_dn1, u, False)[0].reshape(4, fb, d)
        return w_dn[1], conv_w[1], conv_b[1]

    x4, saved1 = _ffn_forward("1", x3, gains[1], w_up[1], late1)
    dx4, dx4_bf, d_fin, loss_part = _loss_head(x4, fin, target)

    dx3, d_fn1, dw_up1, dw_dn1, dcw1, dcb1 = _ffn_backward("1", x3, gains[1], w_up[1], w_dn[1], conv_w[1], conv_b[1], saved1, (dx4, dx4_bf))
    rows = d // N_DEV
    scatter = lambda tag, stacks, carry: _send_start("scatter_start_" + tag, stacks, [lax.empty(a.shape, a.dtype) for a in stacks], True, None, carry)
    going_ffn1 = scatter("ffn1", [dw_up1, dw_dn1.reshape(N_DEV, fs, d)], attn_sinks)
    datt = _mm_rows_nt("attn_out_bwd", dx3[1], w_o, out_dtype=BF16)
    dw_o = _mm_tn("attn_out_grad", att, dx3[1])
    dq, dkv_own, dkv_before, dsink = _attn_bwd(q, kv, att, datt, going_ffn1[4])
    tiles = dkv_before.shape[0]
    dkv = dkv_own.reshape(tiles, s // tiles, kvd)
    dkv = jnp.concatenate([dkv[:, :-WINDOW], dkv[:, -WINDOW:] + jnp.pad(dkv_before[1:], ((0, 1), (0, 0), (0, 0)))], axis=1).reshape(s, kvd)
    dw_q = _mm_tn("q_proj_grad", a2, dq)
    dw_kv = _mm_tn("kv_proj_grad", akv, dkv)
    going_attn = scatter("attn", [dw_kv.reshape(N_DEV, rows, kvd), dw_q.reshape(N_DEV, rows, d), dw_o.reshape(N_DEV, rows, d)], kvn)
    whole = lambda a_ref, b_ref: [(a_ref[...], b_ref[...])]
    rows_of = lambda width: (lambda tile: pl.BlockSpec((tile, width), lambda i: (i, 0)))
    dx2, (d_kvn, d_attn) = _proj_norm_bwd("attn_in_bwd", x2, dx3[0], [
        (dkv, rows_of(kvd), w_kvg, pl.BlockSpec((d, kvd), lambda i: (0, 0)), whole, going_attn[4]),
        (dq, rows_of(d), w_q, pl.BlockSpec((d, d), lambda i: (0, 0)), whole, attn_norm)])
    going = {}

    def sent0(dw_dn0, gain):
        going["ffn_dn0"] = scatter("ffn_dn0", [dw_dn0.reshape(N_DEV, fs, d)], gain)
        return going["ffn_dn0"][4]

    dx1, d_fn0, dw_up0, _, dcw0, dcb0 = _ffn_backward("0", x1, gains[0], w_up[0], w_dn[0], conv_w[0], conv_b[0], saved0, dx2, sent0)
    dw_out = _mm_tn("hg_out_grad", og, dx1[1])
    going_ffn0 = scatter("ffn0", [dw_up0, dw_out.reshape(N_DEV, rows, d)], hg_out_norm)
    dog = _mm_rows_nt("hg_out_bwd", dx1[1], w_out, out_dtype=F32)
    dp, d_lbl, d_ogain = _hgrn2_bwd(p, lbl, going_ffn0[4], o, dog, states, gsum)
    dw_in = _matmul(
        "hg_in_grad", a0, dp, dims=TN, grid=(1, N_DEV, 1),
        a_spec=pl.BlockSpec((s, d), lambda i, j, k: (0, 0)),
        b_spec=pl.BlockSpec((None, s, half), lambda i, j, k: (j // 2, 0, j % 2)),
        o_spec=pl.BlockSpec((None, d, half), lambda i, j, k: (j, 0, 0)),
        out_shape=jax.ShapeDtypeStruct((N_DEV, d, half), BF16))
    going_hg = scatter("hg", [dw_in], hgn)
    (dx0, _), (d_hgn,) = _proj_norm_bwd("hg_in_bwd", x0, dx1[0], [
        (dp, lambda tile: pl.BlockSpec((4, tile, d), lambda i: (0, i, 0)), w_in, pl.BlockSpec((N_DEV, d, half), lambda i: (0, 0, 0)),
         lambda g_ref, w_ref: [(g_ref[k // 2, :, (k % 2) * half:(k % 2 + 1) * half], w_ref[k]) for k in range(N_DEV)],
         going_hg[4])])

    as_blocks = lambda a, r: a.reshape(r, N_DEV, -1).transpose(1, 0, 2).reshape(N_DEV * r, -1)
    d_cw = jnp.concatenate([g.transpose(1, 0, 2).reshape(CONV_WIDTH, 4 * fb) for g in (dcw0, dcw1)], axis=0)
    parts = [d_fin, jnp.concatenate([d_fn0, d_fn1], axis=0), jnp.concatenate([dcb0.reshape(1, 4 * fb), dcb1.reshape(1, 4 * fb)], axis=0),
             as_blocks(d_cw, 2 * CONV_WIDTH), d_attn, jnp.sum(dsink[:, :, 0], axis=0).reshape(1, nq), d_kvn, d_ogain,
             as_blocks(d_hgn, 1), as_blocks(d_lbl, 2), loss_part]
    wide = [2]
    packs = [[parts[i] for i in wide], [part for i, part in enumerate(parts) if i not in wide]]
    places = [None] * len(parts)
    for which, members in enumerate([wide, [i for i in range(len(parts)) if i not in wide]]):
        for i, off in zip(members, _pack_rows(packs[which])[0]):
            places[i] = (which, off)
    packed = [_pack("pack_wide_grads", packs[0]), _pack("pack_narrow_grads", packs[1])]
    going_small = _send_start("small_grads_start", packed, [_landing(a, mine) for a in packed], False)

    arrive = lambda tag, going, after: _send_wait("scatter_wait_" + tag, going, after, True)
    (l_up1, l_dn1), (l_kv, l_q, l_o), (l_dn0,), (l_up0, l_out) = (
        arrive("ffn1", going_ffn1, going_small[4]), arrive("attn", going_attn, going_small[4]),
        arrive("ffn_dn0", going["ffn_dn0"], going_small[4]), arrive("ffn0", going_ffn0, going_small[4]))
    big = {}
    for tag, w, m, v, part in [
            ("w_kv", w_kv, m_w_kv, v_w_kv, l_kv), ("attn_w_q", attn_w_q[0], m_attn_w_q[0], v_attn_w_q[0], l_q),
            ("attn_w_o", attn_w_o[0], m_attn_w_o[0], v_attn_w_o[0], l_o)]:
        big[tag] = _adamw_shard("adamw_" + tag, w, m, v, part)
    up_done = _adamw_layers("adamw_ffn_w_up", up_t(ffn_w_up), up_t(m_ffn_w_up), up_t(v_ffn_w_up), (l_up0, l_up1))
    big["ffn_w_up"] = [up_t(a) for a in up_done]
    big["ffn_w_down"] = _adamw_layers("adamw_ffn_w_down", ffn_w_down, m_ffn_w_down, v_ffn_w_down, (l_dn0, l_dn1))
    lead = lambda tag: [a[None] for a in big[tag]]

    both_done = up_done[0][0, 0:1, 0:1] + big["ffn_w_down"][0][0, 0:1, 0:1]
    gathered = _send_wait("small_grads_wait", going_small, both_done, False)
    two = lambda a: a.reshape(-1, a.shape[-1])
    small = [(fin, m_final_norm.reshape(1, d), v_final_norm.reshape(1, d), False), (ffn_norm, m_ffn_norm, v_ffn_norm, False),
             (ffn_conv_b, m_ffn_conv_b, v_ffn_conv_b, False), (two(ffn_conv_w), two(m_ffn_conv_w), two(v_ffn_conv_w), True),
             (attn_norm, m_attn_norm, v_attn_norm, False), (attn_sinks, m_attn_sinks, v_attn_sinks, False),
             (kvn, m_kv_norm.reshape(1, d), v_kv_norm.reshape(1, d), False), (hg_out_norm, m_hg_out_norm, v_hg_out_norm, False),
             (hg_norm, m_hg_norm, v_hg_norm, True), (hg_lb_logits, m_hg_lb_logits, v_hg_lb_logits, True)]
    res = _adamw_small(gathered, places, small)
    l_in, = arrive("hg", going_hg, gathered[1])
    big["hg_w_in"] = _adamw_shard("adamw_hg_w_in", hg_w_in[0], m_hg_w_in[0], v_hg_w_in[0], l_in)
    big["hg_w_out"] = _adamw_shard("adamw_hg_w_out", hg_w_out[0], m_hg_w_out[0], v_hg_w_out[0], l_out)
    names = ["final_norm", "ffn_norm", "ffn_conv_b", "ffn_conv_w", "attn_norm", "attn_sinks", "kv_norm", "hg_out_norm", "hg_norm", "hg_lb_logits"]
    shapes = {"final_norm": final_norm.shape, "kv_norm": kv_norm.shape, "ffn_conv_w": ffn_conv_w.shape}
    out = {n: [a.reshape(shapes[n]) if n in shapes else a for a in res[4 * i:4 * i + 4]] for i, n in enumerate(names)}
    out.update(hg_w_in=lead("hg_w_in"), hg_w_out=lead("hg_w_out"), w_kv=big["w_kv"], attn_w_q=lead("attn_w_q"), attn_w_o=lead("attn_w_o"),
               ffn_w_up=big["ffn_w_up"], ffn_w_down=big["ffn_w_down"])
    order = ["hg_norm", "hg_w_in", "hg_lb_logits", "hg_out_norm", "hg_w_out", "kv_norm", "w_kv", "attn_norm", "attn_w_q", "attn_sinks",
             "attn_w_o", "ffn_norm", "ffn_w_up", "ffn_conv_w", "ffn_conv_b", "ffn_w_down", "final_norm"]
    loss = res[-1][0, 0]
    return (loss, dx0[None], *[out[n][0] for n in order], *[out[n][1] for n in order], *[out[n][2] for n in order], *[out[n][3] for n in order])
```

```python
import functools

import jax
import jax.numpy as jnp
from jax import lax
from jax.experimental import pallas as pl
from jax.experimental.pallas import tpu as pltpu

F32 = jnp.float32
BF16 = jnp.bfloat16

EPS = 1e-6
HG_EXPAND = 128
HG_CHUNK = 32
ATT_HEAD_DIM = 64
ATT_KV_HEADS = 2
WINDOW = 128
CONV_WIDTH = 3
ADAM_LR = 0.001
ADAM_B1 = 0.9
ADAM_B2 = 0.999
ADAM_EPS = 1e-08
ADAM_WD = 0.01
ADAM_STEP = 10

N_DEV = 8
VMEM_LIMIT = 60 * 1024 * 1024
NEG = -1e30

NN = (((1,), (0,)), ((), ()))
NT = (((1,), (1,)), ((), ()))
TN = (((0,), (0,)), ((), ()))
MESH = pl.DeviceIdType.MESH


def _dot(a, b, dims=NN):
    return lax.dot_general(a.astype(BF16), b.astype(BF16), dims, preferred_element_type=F32)


def _sigmoid(x):
    return 0.5 * jnp.tanh(0.5 * x) + 0.5


def _silu(x):
    return x * _sigmoid(x)


def _silu_and_grad(x):
    s = _sigmoid(x)
    return x * s, s * (1.0 + x * (1.0 - s))


def _dsilu(x):
    return _silu_and_grad(x)[1]


def _params(semantics):
    return pltpu.CompilerParams(dimension_semantics=semantics, vmem_limit_bytes=VMEM_LIMIT)


def _row_tile(rows, want=512):
    return min(rows, want)


MM_ROWS = 1024


def _matmul(name, a, b, *, dims, grid, a_spec, b_spec, o_spec, out_shape, add=None, add_spec=None, after=None):
    assert grid[2] == 1

    def body(*refs):
        a_ref, b_ref, o_ref = refs[0], refs[1], refs[-1]
        total = _dot(a_ref[...], b_ref[...], dims)
        if add is not None:
            total = total + refs[2][...]
        o_ref[...] = total.astype(o_ref.dtype)

    in_specs = [a_spec, b_spec] + ([] if add is None else [add_spec]) + ([] if after is None else [pl.BlockSpec(memory_space=pl.ANY)])
    args = (a, b) + (() if add is None else (add,)) + (() if after is None else (after,))
    return pl.pallas_call(
        body, name=name, grid=grid, in_specs=in_specs, out_specs=o_spec, out_shape=out_shape,
        compiler_params=_params(("parallel", "parallel", "arbitrary")),
    )(*args)


def _mm_rows(name, a, w, *, out_dtype, add=None, after=None):
    s, kdim = a.shape
    n = w.shape[1]
    tm = _row_tile(s, MM_ROWS)
    return _matmul(
        name, a, w, dims=NN, grid=(s // tm, 1, 1),
        a_spec=pl.BlockSpec((tm, kdim), lambda i, j, k: (i, 0)),
        b_spec=pl.BlockSpec((kdim, n), lambda i, j, k: (0, 0)),
        o_spec=pl.BlockSpec((tm, n), lambda i, j, k: (i, 0)),
        out_shape=jax.ShapeDtypeStruct((s, n), out_dtype),
        add=add, add_spec=None if add is None else pl.BlockSpec((tm, n), lambda i, j, k: (i, 0)), after=after,
    )


def _mm_rows_nt(name, a, w, *, out_dtype):
    s, n = a.shape
    kdim = w.shape[0]
    tm = _row_tile(s, MM_ROWS)
    return _matmul(
        name, a, w, dims=NT, grid=(s // tm, 1, 1),
        a_spec=pl.BlockSpec((tm, n), lambda i, j, k: (i, 0)),
        b_spec=pl.BlockSpec((kdim, n), lambda i, j, k: (0, 0)),
        o_spec=pl.BlockSpec((tm, kdim), lambda i, j, k: (i, 0)),
        out_shape=jax.ShapeDtypeStruct((s, kdim), out_dtype),
    )


def _mm_tn(name, a, g):
    s, m = a.shape
    n = g.shape[1]
    tn = min(n, 512)
    return _matmul(
        name, a, g, dims=TN, grid=(1, n // tn, 1),
        a_spec=pl.BlockSpec((s, m), lambda i, j, k: (0, 0)),
        b_spec=pl.BlockSpec((s, tn), lambda i, j, k: (0, j)),
        o_spec=pl.BlockSpec((m, tn), lambda i, j, k: (0, j)),
        out_shape=jax.ShapeDtypeStruct((m, n), BF16),
    )


NORM_ROWS = 256


def _norm_proj(name, h, branches, *, tm, nj):
    s, d = h.shape
    n = len(branches)
    rows = min(tm, NORM_ROWS)

    def body(*refs):
        h_ref, gain_refs, w_refs = refs[0], refs[1:1 + n], refs[1 + n:1 + 2 * n]
        a_refs, o_refs = refs[1 + 2 * n:1 + 3 * n], refs[1 + 3 * n:]

        @pl.when(pl.program_id(1) == 0)
        def _():
            def normalize(c, carry):
                at = pl.ds(pl.multiple_of(c * rows, rows), rows)
                xv = h_ref[at, :]
                xhat = xv * lax.rsqrt(jnp.mean(xv * xv, axis=-1, keepdims=True) + EPS)
                for gain_ref, a_ref in zip(gain_refs, a_refs):
                    a_ref[at, :] = (xhat * gain_ref[...]).astype(BF16)
                return carry

            lax.fori_loop(0, tm // rows, normalize, 0)

        for branch, w_ref, a_ref, o_ref in zip(branches, w_refs, a_refs, o_refs):
            o_ref[...] = _dot(a_ref[...], w_ref[...], branch[3]).astype(o_ref.dtype)

    row = pl.BlockSpec((tm, d), lambda i, j: (i, 0))
    vec = pl.BlockSpec((1, d), lambda i, j: (0, 0))
    outs = pl.pallas_call(
        body, name=name, grid=(s // tm, nj), in_specs=[row] + [vec] * n + [b[2] for b in branches],
        out_specs=[row] * n + [b[4] for b in branches],
        out_shape=[jax.ShapeDtypeStruct((s, d), BF16)] * n + [b[5] for b in branches],
        compiler_params=_params(("parallel", "arbitrary")),
    )(h, *[b[0] for b in branches], *[b[1] for b in branches])
    return outs[:n], outs[n:]


def _proj_norm_bwd(name, h, dres, branches):
    s, d = h.shape
    tm = _row_tile(s)
    n = len(branches)

    def body(*refs):
        h_ref, dres_ref = refs[0], refs[1]
        g_refs, w_refs, gain_refs = refs[2:2 + n], refs[2 + n:2 + 2 * n], refs[2 + 2 * n:2 + 3 * n]
        dh_ref, dhb_ref, dg_refs = refs[2 + 3 * n], refs[3 + 3 * n], refs[4 + 3 * n:]
        i = pl.program_id(0)
        xv = h_ref[...]
        r = lax.rsqrt(jnp.mean(xv * xv, axis=-1, keepdims=True) + EPS)
        xhat = xv * r
        total = dres_ref[...]
        for branch, g_ref, w_ref, gain_ref, dg_ref in zip(branches, g_refs, w_refs, gain_refs, dg_refs):
            pairs = branch[4](g_ref, w_ref)
            da = _dot(*pairs[0], NT)
            for pair in pairs[1:]:
                da = da + _dot(*pair, NT)
            dgain = jnp.sum(da * xhat, axis=0, keepdims=True)

            @pl.when(i == 0)
            def _():
                dg_ref[...] = dgain

            @pl.when(i > 0)
            def _():
                dg_ref[...] += dgain

            dxhat = da * gain_ref[...]
            total = total + r * (dxhat - xhat * jnp.mean(dxhat * xhat, axis=-1, keepdims=True))
        dh_ref[...] = total
        dhb_ref[...] = total.astype(BF16)

    row = pl.BlockSpec((tm, d), lambda i: (i, 0))
    vec = pl.BlockSpec((1, d), lambda i: (0, 0))
    outs = pl.pallas_call(
        body, name=name, grid=(s // tm,),
        in_specs=[row, row] + [b[1](tm) for b in branches] + [b[3] for b in branches] + [vec] * n, out_specs=[row, row] + [vec] * n,
        out_shape=[jax.ShapeDtypeStruct((s, d), F32), jax.ShapeDtypeStruct((s, d), BF16)] + [jax.ShapeDtypeStruct((1, d), F32)] * n,
        compiler_params=_params(("arbitrary",)),
    )(h, dres, *[b[0] for b in branches], *[b[2] for b in branches], *[b[5] for b in branches])
    return (outs[0], outs[1]), outs[2:]


def _loss_head(h, gain, target):
    s, d = h.shape
    tm = _row_tile(s)

    def body(h_ref, g_ref, t_ref, dh_ref, dhb_ref, dg_ref, loss_ref):
        i = pl.program_id(0)
        xv = h_ref[...]
        r = lax.rsqrt(jnp.mean(xv * xv, axis=-1, keepdims=True) + EPS)
        xhat = xv * r
        err = xhat * g_ref[...] - t_ref[...]
        dy = err * (1.0 / d)
        part = jnp.zeros((1, 128), F32) + 0.5 * jnp.sum(jnp.mean(err * err, axis=-1, keepdims=True))
        dgain = jnp.sum(dy * xhat, axis=0, keepdims=True)

        @pl.when(i == 0)
        def _():
            dg_ref[...] = dgain
            loss_ref[...] = part

        @pl.when(i > 0)
        def _():
            dg_ref[...] += dgain
            loss_ref[...] += part

        dxhat = dy * g_ref[...]
        dh = r * (dxhat - xhat * jnp.mean(dxhat * xhat, axis=-1, keepdims=True))
        dh_ref[...] = dh
        dhb_ref[...] = dh.astype(BF16)

    row = pl.BlockSpec((tm, d), lambda i: (i, 0))
    vec = pl.BlockSpec((1, d), lambda i: (0, 0))
    return pl.pallas_call(
        body, name="loss_head", grid=(s // tm,), in_specs=[row, vec, row],
        out_specs=[row, row, vec, pl.BlockSpec((1, 128), lambda i: (0, 0))],
        out_shape=[jax.ShapeDtypeStruct((s, d), F32), jax.ShapeDtypeStruct((s, d), BF16), jax.ShapeDtypeStruct((1, d), F32),
                   jax.ShapeDtypeStruct((1, 128), F32)],
        compiler_params=_params(("arbitrary",)),
    )(h, gain, target)


def _bdot(a, b, ca, cb):
    return lax.dot_general(a.astype(BF16), b.astype(BF16), (((ca,), (cb,)), ((0,), (0,))), preferred_element_type=F32)


def _chunk_cumsum(xv, reverse=False):
    n = xv.shape[0]
    row = lax.broadcasted_iota(jnp.int32, xv.shape, 0) % HG_CHUNK
    step = 1
    while step < HG_CHUNK:
        if reverse:
            xv = xv + jnp.where(row < HG_CHUNK - step, pltpu.roll(xv, n - step, axis=0), 0.0)
        else:
            xv = xv + jnp.where(row >= step, pltpu.roll(xv, step, axis=0), 0.0)
        step *= 2
    return xv


def _hg_terms(p_ref, lbl_ref, g_ref=None):
    pq = p_ref[0].astype(F32)
    pf = p_ref[1].astype(F32)
    lb = _sigmoid(lbl_ref[0:1, :] - lbl_ref[1:2, :])
    sig = _sigmoid(pf)
    fg = lb + (1.0 - lb) * sig
    nc = pq.shape[0] // HG_CHUNK
    chunks = lambda a: a.reshape(nc, HG_CHUNK, HG_EXPAND)
    q = chunks(_silu(pq) * HG_EXPAND ** -0.5)
    k = chunks(1.0 - fg)
    v = chunks(p_ref[2].astype(F32))
    g = chunks(_chunk_cumsum(jnp.log(fg)) if g_ref is None else g_ref[...])
    gm = g[:, HG_CHUNK // 2 - 1:HG_CHUNK // 2, :]
    gl = g[:, HG_CHUNK - 1:HG_CHUNK, :]
    e_mid, e_inv, e_all, e_end = jnp.exp(g - gm), jnp.exp(gm - g), jnp.exp(g), jnp.exp(gl - g)
    terms = dict(q=q, k=k, v=v, g=g, qd=q * e_all, qt=q * e_mid, kt=k * e_inv, kd=k * e_end, e_last=jnp.exp(gl),
                 e_mid=e_mid, e_inv=e_inv, e_all=e_all, e_end=e_end)
    return terms, (pq, sig, fg, lb)


def _causal(nc):
    r = lax.broadcasted_iota(jnp.int32, (nc, HG_CHUNK, HG_CHUNK), 1)
    c = lax.broadcasted_iota(jnp.int32, (nc, HG_CHUNK, HG_CHUNK), 2)
    return r >= c


def _hgrn2_fwd(p, lb_logits, out_gain):
    _, s, d = p.shape
    heads = d // HG_EXPAND
    t = _row_tile(s, 2048)
    nc = t // HG_CHUNK

    def body(p_ref, lbl_ref, gain_ref, o_ref, og_ref, st_ref, g_ref, state, decay):
        @pl.when(pl.program_id(1) == 0)
        def _():
            state[...] = jnp.zeros_like(state)

        tm, _ = _hg_terms(p_ref, lbl_ref)
        g_ref[...] = tm["g"].reshape(t, HG_EXPAND)
        decay[...] = tm["e_last"]
        st_ref[...] = _bdot(tm["v"], tm["kd"], 1, 1)

        def chunk(c, carry):
            add = st_ref[c]
            st = state[...]
            st_ref[c] = st
            state[...] = st * decay[c] + add
            return carry

        lax.fori_loop(0, nc, chunk, 0)
        a = jnp.where(_causal(nc), _bdot(tm["qt"], tm["kt"], 2, 2), 0.0)
        ov = (_bdot(tm["qd"], st_ref[...], 2, 2) + _bdot(a, tm["v"], 2, 1)).reshape(t, HG_EXPAND)
        o_ref[...] = ov
        on = ov * lax.rsqrt(jnp.mean(ov * ov, axis=-1, keepdims=True) + EPS) * gain_ref[...]
        og_ref[...] = (on * _silu(p_ref[3].astype(F32))).astype(BF16)

    blk = pl.BlockSpec((t, HG_EXPAND), lambda h, b: (b, h))
    return pl.pallas_call(
        body, name="hgrn2_fwd", grid=(heads, s // t),
        in_specs=[pl.BlockSpec((4, t, HG_EXPAND), lambda h, b: (0, b, h)), pl.BlockSpec((2, HG_EXPAND), lambda h, b: (0, h)),
                  pl.BlockSpec((1, HG_EXPAND), lambda h, b: (0, 0))],
        out_specs=[blk, blk, pl.BlockSpec((None, nc, HG_EXPAND, HG_EXPAND), lambda h, b: (h, b, 0, 0)), blk],
        out_shape=[jax.ShapeDtypeStruct((s, d), F32), jax.ShapeDtypeStruct((s, d), BF16),
                   jax.ShapeDtypeStruct((heads, s // HG_CHUNK, HG_EXPAND, HG_EXPAND), F32), jax.ShapeDtypeStruct((s, d), F32)],
        scratch_shapes=[pltpu.VMEM((HG_EXPAND, HG_EXPAND), F32), pltpu.VMEM((nc, 1, HG_EXPAND), F32)],
        compiler_params=_params(("parallel", "arbitrary")),
    )(p, lb_logits, out_gain)


def _hgrn2_bwd(p, lb_logits, out_gain, o, dog, states, gsum):
    _, s, d = p.shape
    heads = d // HG_EXPAND
    t = _row_tile(s, 1024)
    nc = t // HG_CHUNK
    nb = s // t

    def body(p_ref, lbl_ref, gain_ref, o_ref, dog_ref, st_ref, g_ref, dp_ref, dlbl_ref, dgain_ref, dstate, decay, dst_s):
        h, b = pl.program_id(0), pl.program_id(1)

        @pl.when(b == 0)
        def _():
            dstate[...] = jnp.zeros_like(dstate)

        tm, (pq, sig, fg, lb) = _hg_terms(p_ref, lbl_ref, g_ref)
        pg = p_ref[3].astype(F32)
        ov = o_ref[...]
        r = lax.rsqrt(jnp.mean(ov * ov, axis=-1, keepdims=True) + EPS)
        ohat = ov * r
        dogv = dog_ref[...]
        d_on = dogv * _silu(pg)
        dp_ref[3] = (dogv * ohat * gain_ref[...] * _dsilu(pg)).astype(BF16)
        dgain = jnp.sum(d_on * ohat, axis=0, keepdims=True)

        @pl.when((h == 0) & (b == 0))
        def _():
            dgain_ref[...] = dgain

        @pl.when((h > 0) | (b > 0))
        def _():
            dgain_ref[...] += dgain

        dohat = d_on * gain_ref[...]
        do = (r * (dohat - ohat * jnp.mean(dohat * ohat, axis=-1, keepdims=True))).reshape(nc, HG_CHUNK, HG_EXPAND)

        decay[...] = tm["e_last"]
        dst_s[...] = _bdot(do, tm["qd"], 1, 1)

        def chunk(i, carry):
            c = nc - 1 - i
            add = dst_s[c]
            dst = dstate[...]
            dst_s[c] = dst
            dstate[...] = dst * decay[c] + add
            return carry

        lax.fori_loop(0, nc, chunk, 0)
        st, dst = st_ref[...], dst_s[...]
        causal = _causal(nc)
        a = jnp.where(causal, _bdot(tm["qt"], tm["kt"], 2, 2), 0.0)
        da = jnp.where(causal, _bdot(do, tm["v"], 2, 2), 0.0)
        dqt = _bdot(da, tm["kt"], 2, 1)
        dkt = _bdot(da, tm["qt"], 1, 1)
        dqd = _bdot(do, st, 2, 1)
        dkd = _bdot(tm["v"], dst, 2, 1)
        dv = _bdot(a, do, 1, 1) + _bdot(tm["kd"], dst, 2, 2)
        dq = dqt * tm["e_mid"] + dqd * tm["e_all"]
        dk = dkt * tm["e_inv"] + dkd * tm["e_end"]
        dg = dqt * tm["qt"] - dkt * tm["kt"] + dqd * tm["qd"] - dkd * tm["kd"]
        dgl = jnp.sum(dkd * tm["kd"], axis=1, keepdims=True) + tm["e_last"] * jnp.sum(dst * st, axis=1, keepdims=True)
        last_row = lax.broadcasted_iota(jnp.int32, (nc, HG_CHUNK, HG_EXPAND), 1) == HG_CHUNK - 1
        flat = lambda a3: a3.reshape(t, HG_EXPAND)
        dlf = _chunk_cumsum(flat(dg + jnp.where(last_row, dgl, 0.0)), reverse=True)
        dfg = dlf / fg - flat(dk)
        dlb = jnp.sum(dfg * (1.0 - sig), axis=0, keepdims=True)
        dl0 = dlb * lb * (1.0 - lb)
        dlbl = jnp.concatenate([dl0, -dl0], axis=0)

        @pl.when(b == 0)
        def _():
            dlbl_ref[...] = dlbl

        @pl.when(b > 0)
        def _():
            dlbl_ref[...] += dlbl

        dp_ref[0] = (flat(dq) * HG_EXPAND ** -0.5 * _dsilu(pq)).astype(BF16)
        dp_ref[1] = (dfg * (1.0 - lb) * sig * (1.0 - sig)).astype(BF16)
        dp_ref[2] = flat(dv).astype(BF16)

    blk = pl.BlockSpec((t, HG_EXPAND), lambda h, b: (nb - 1 - b, h))
    pblk = pl.BlockSpec((4, t, HG_EXPAND), lambda h, b: (0, nb - 1 - b, h))
    return pl.pallas_call(
        body, name="hgrn2_bwd", grid=(heads, nb),
        in_specs=[pblk, pl.BlockSpec((2, HG_EXPAND), lambda h, b: (0, h)), pl.BlockSpec((1, HG_EXPAND), lambda h, b: (0, 0)),
                  blk, blk, pl.BlockSpec((None, nc, HG_EXPAND, HG_EXPAND), lambda h, b: (h, nb - 1 - b, 0, 0)), blk],
        out_specs=[pblk, pl.BlockSpec((2, HG_EXPAND), lambda h, b: (0, h)), pl.BlockSpec((1, HG_EXPAND), lambda h, b: (0, 0))],
        out_shape=[jax.ShapeDtypeStruct((4, s, d), BF16), jax.ShapeDtypeStruct((2, d), F32), jax.ShapeDtypeStruct((1, HG_EXPAND), F32)],
        scratch_shapes=[pltpu.VMEM((HG_EXPAND, HG_EXPAND), F32), pltpu.VMEM((nc, 1, HG_EXPAND), F32),
                        pltpu.VMEM((nc, HG_EXPAND, HG_EXPAND), F32)],
        compiler_params=_params(("arbitrary", "arbitrary")),
    )(p, lb_logits, out_gain, o, dog, states, gsum)


HALO = 8
FFN_FWD_ROWS = 512
FFN_BWD_ROWS = 256


def _shift_down(xv, n):
    return pltpu.roll(xv, n, axis=0)


def _shift_up(xv, n):
    return pltpu.roll(xv, xv.shape[0] - n, axis=0)


def _ffn_hidden_down(name, u, conv_w, conv_b, w_down, h):
    _, nj, s, fb = u.shape
    d = w_down.shape[2]
    tm = _row_tile(s, FFN_FWD_ROWS)
    per = tm // HALO

    def body(gate_ref, prev_ref, val_ref, w_ref, b_ref, wd_ref, h_ref, hid_ref, conv_ref, o_ref):
        i = pl.program_id(0)
        total = h_ref[...]
        for j in range(nj):
            prev = jnp.where(i > 0, prev_ref[j].astype(F32), 0.0)
            ext = jnp.concatenate([prev, gate_ref[j].astype(F32)], axis=0)
            conv = b_ref[j] + w_ref[j, 2:3, :] * ext[HALO:]
            conv = conv + w_ref[j, 1:2, :] * _shift_down(ext, 1)[HALO:]
            conv = conv + w_ref[j, 0:1, :] * _shift_down(ext, 2)[HALO:]
            conv = conv.astype(BF16)
            conv_ref[j] = conv
            hidden = _silu(conv) * val_ref[j]
            hid_ref[j] = hidden
            total = total + _dot(hidden, wd_ref[j])
        o_ref[...] = total

    row = pl.BlockSpec((tm, d), lambda i: (i, 0))
    return pl.pallas_call(
        body, name=name, grid=(s // tm,),
        in_specs=[pl.BlockSpec((None, nj, tm, fb), lambda i: (0, 0, i, 0)),
                  pl.BlockSpec((None, nj, HALO, fb), lambda i: (0, 0, jnp.maximum(i * per - 1, 0), 0)),
                  pl.BlockSpec((None, nj, tm, fb), lambda i: (1, 0, i, 0)),
                  pl.BlockSpec((nj, CONV_WIDTH, fb), lambda i: (0, 0, 0)), pl.BlockSpec((nj, 1, fb), lambda i: (0, 0, 0)),
                  pl.BlockSpec((nj, fb, d), lambda i: (0, 0, 0)), row],
        out_specs=[pl.BlockSpec((nj, tm, fb), lambda i: (0, i, 0)), pl.BlockSpec((nj, tm, fb), lambda i: (0, i, 0)), row],
        out_shape=[jax.ShapeDtypeStruct((nj, s, fb), BF16), jax.ShapeDtypeStruct((nj, s, fb), BF16), jax.ShapeDtypeStruct((s, d), F32)],
        compiler_params=_params(("parallel",)),
    )(u, u, u, conv_w, conv_b, w_down, h)


def _ffn_hidden_up_bwd(name, u, conv, dh, conv_w, w_up, h, gain, dres):
    _, nj, s, fb = u.shape
    d = w_up.shape[2]
    tm = _row_tile(s, FFN_BWD_ROWS)
    per = tm // HALO
    nblk = s // HALO
    ni = s // tm

    def body(gate_ref, conv_ref, cnext_ref, val_ref, vnext_ref, dh_ref, dhnext_ref, w_ref, wu_ref, h_ref, gain_ref, dres_ref,
             du_ref, dw_ref, db_ref, dx_ref, dxb_ref, dgain_ref):
        i = pl.program_id(0)
        has_next = i < ni - 1
        total = None
        for j in range(nj):
            act, dact = _silu_and_grad(conv_ref[j])
            dval = dh_ref[j] * act
            after = jnp.where(has_next, dhnext_ref[j].astype(F32), 0.0) * vnext_ref[j].astype(F32) * _dsilu(cnext_ref[j].astype(F32))
            dconv = jnp.concatenate([(dh_ref[j] * val_ref[j] * dact).astype(F32), after], axis=0)
            taps = [_shift_up(dconv, 2)[:tm], _shift_up(dconv, 1)[:tm], dconv[:tm]]
            dgate = (w_ref[j, 0:1, :] * taps[0] + w_ref[j, 1:2, :] * taps[1] + w_ref[j, 2:3, :] * taps[2]).astype(BF16)
            du_ref[0, j] = dgate
            du_ref[1, j] = dval
            part = _dot(dgate, wu_ref[j]) + _dot(dval, wu_ref[nj + j])
            total = part if total is None else total + part
            gate = gate_ref[j].astype(F32)
            dw = jnp.concatenate([jnp.sum(tap * gate, axis=0, keepdims=True) for tap in taps], axis=0)
            db = jnp.sum(taps[2], axis=0, keepdims=True)

            @pl.when(i == 0)
            def _():
                dw_ref[j] = dw
                db_ref[j] = db

            @pl.when(i > 0)
            def _():
                dw_ref[j] += dw
                db_ref[j] += db

        xv = h_ref[...]
        r = lax.rsqrt(jnp.mean(xv * xv, axis=-1, keepdims=True) + EPS)
        xhat = xv * r
        dgain = jnp.sum(total * xhat, axis=0, keepdims=True)

        @pl.when(i == 0)
        def _():
            dgain_ref[...] = dgain

        @pl.when(i > 0)
        def _():
            dgain_ref[...] += dgain

        dxhat = total * gain_ref[...]
        dx = dres_ref[...] + r * (dxhat - xhat * jnp.mean(dxhat * xhat, axis=-1, keepdims=True))
        dx_ref[...] = dx
        dxb_ref[...] = dx.astype(BF16)

    def tile(part):
        return pl.BlockSpec((None, nj, tm, fb), lambda i: (part, 0, i, 0))

    def after(part):
        return pl.BlockSpec((None, nj, HALO, fb), lambda i: (part, 0, jnp.minimum((i + 1) * per, nblk - 1), 0))

    row = pl.BlockSpec((tm, d), lambda i: (i, 0))
    own = pl.BlockSpec((nj, tm, fb), lambda i: (0, i, 0))
    nxt = pl.BlockSpec((nj, HALO, fb), lambda i: (0, jnp.minimum((i + 1) * per, nblk - 1), 0))
    return pl.pallas_call(
        body, name=name, grid=(ni,),
        in_specs=[tile(0), own, nxt, tile(1), after(1), own, nxt,
                  pl.BlockSpec((nj, CONV_WIDTH, fb), lambda i: (0, 0, 0)),
                  pl.BlockSpec((2 * nj, fb, d), lambda i: (0, 0, 0)), row, pl.BlockSpec((1, d), lambda i: (0, 0)), row],
        out_specs=[pl.BlockSpec((2, nj, tm, fb), lambda i: (0, 0, i, 0)),
                   pl.BlockSpec((nj, CONV_WIDTH, fb), lambda i: (0, 0, 0)), pl.BlockSpec((nj, 1, fb), lambda i: (0, 0, 0)),
                   row, row, pl.BlockSpec((1, d), lambda i: (0, 0))],
        out_shape=[jax.ShapeDtypeStruct((2, nj, s, fb), BF16), jax.ShapeDtypeStruct((nj, CONV_WIDTH, fb), F32),
                   jax.ShapeDtypeStruct((nj, 1, fb), F32), jax.ShapeDtypeStruct((s, d), F32), jax.ShapeDtypeStruct((s, d), BF16),
                   jax.ShapeDtypeStruct((1, d), F32)],
        compiler_params=_params(("arbitrary",)),
    )(u, conv, conv, u, u, dh, dh, conv_w, w_up, h, gain, dres)


ATT_TILE = 512


def _stack_heads(ref, rows, first_head, count):
    hd = ATT_HEAD_DIM
    return jnp.concatenate([ref[rows, (first_head + j) * hd:(first_head + j + 1) * hd] for j in range(count)], axis=0)


def _unstack_heads(stacked, ref, rows, first_head, count):
    hd = ATT_HEAD_DIM
    for pair in range(count // 2):
        both = [stacked[(2 * pair + j) * WINDOW:(2 * pair + j + 1) * WINDOW, :] for j in range(2)]
        ref[rows, (first_head + 2 * pair) * hd:(first_head + 2 * pair + 2) * hd] = jnp.concatenate(both, axis=1).astype(ref.dtype)


def _attn_bias(first_head, count, n_heads, first):
    lanes = count * WINDOW
    ik = lax.broadcasted_iota(jnp.int32, (2 * WINDOW, lanes), 0)
    iq = lax.broadcasted_iota(jnp.int32, (2 * WINDOW, lanes), 1) % WINDOW
    dist = iq + WINDOW - ik
    valid = (dist >= 0) & (dist < WINDOW) & (ik >= (WINDOW if first else 0))
    slope = jnp.concatenate([jnp.zeros((1, WINDOW), F32) + 2.0 ** (-8.0 * (first_head + j + 1) / n_heads) for j in range(count)], axis=1)
    return jnp.where(valid, -slope * dist.astype(F32), NEG)


def _fill_attn_bias(bias_ref, group, n_heads):
    @pl.when(pl.program_id(0) == 0)
    def _():
        for g in range(ATT_KV_HEADS):
            bias_ref[0, g] = _attn_bias(g * group, group, n_heads, False)
            bias_ref[1, g] = _attn_bias(g * group, group, n_heads, True)


def _attn_probs_t(kb_scaled, qs, sink_ref, first_head, count, bias):
    sink = jnp.concatenate([jnp.zeros((1, WINDOW), F32) + sink_ref[0, first_head + j] for j in range(count)], axis=1)
    sc = _dot(kb_scaled, qs, NT) + bias
    m = jnp.maximum(jnp.max(sc, axis=0, keepdims=True), sink)
    e = jnp.exp(sc - m)
    es = jnp.exp(sink - m)
    inv = 1.0 / (jnp.sum(e, axis=0, keepdims=True) + es)
    return e * inv, es * inv


ATT_SCALE = ATT_HEAD_DIM ** -0.5


def _attn_specs(s, d, kvd, tq):
    per = tq // WINDOW
    return [pl.BlockSpec((tq, d), lambda i: (i, 0)), pl.BlockSpec((tq, kvd), lambda i: (i, 0)),
            pl.BlockSpec((WINDOW, kvd), lambda i: (jnp.maximum(i * per - 1, 0), 0))]


def _attn_fwd(q, kv, sinks):
    s, d = q.shape
    kvd = kv.shape[1]
    half = kvd // 2
    hd = ATT_HEAD_DIM
    nq = d // hd
    group = nq // ATT_KV_HEADS
    tq = min(s, ATT_TILE)
    per = tq // WINDOW

    def body(q_ref, kvc_ref, kvp_ref, sink_ref, o_ref, band, bias_ref):
        i = pl.program_id(0)
        _fill_attn_bias(bias_ref, group, nq)
        band[0:WINDOW, :] = kvp_ref[...]
        band[WINDOW:, :] = kvc_ref[...]

        def block(b, carry):
            rows = pl.ds(pl.multiple_of(b * WINDOW, WINDOW), WINDOW)
            keys = pl.ds(pl.multiple_of(b * WINDOW, WINDOW), 2 * WINDOW)
            first = (i * per + b) == 0
            for g in range(ATT_KV_HEADS):
                bias = jnp.where(first, bias_ref[1, g], bias_ref[0, g])
                p, _ = _attn_probs_t(band[keys, g * hd:(g + 1) * hd] * ATT_SCALE, _stack_heads(q_ref, rows, g * group, group), sink_ref,
                                     g * group, group, bias)
                out_t = _dot(band[keys, half + g * hd:half + (g + 1) * hd], p, TN)
                _unstack_heads(out_t.T, o_ref, rows, g * group, group)
            return carry

        lax.fori_loop(0, per, block, 0)

    return pl.pallas_call(
        body, name="attn_fwd", grid=(s // tq,),
        in_specs=_attn_specs(s, d, kvd, tq) + [pl.BlockSpec(memory_space=pltpu.SMEM)],
        out_specs=pl.BlockSpec((tq, d), lambda i: (i, 0)), out_shape=jax.ShapeDtypeStruct((s, d), BF16),
        scratch_shapes=[pltpu.VMEM((tq + WINDOW, kvd), BF16), pltpu.VMEM((2, ATT_KV_HEADS, 2 * WINDOW, group * WINDOW), F32)],
        compiler_params=_params(("arbitrary",)),
    )(q, kv, kv, sinks)


def _attn_bwd(q, kv, o, do, sinks):
    s, d = q.shape
    kvd = kv.shape[1]
    half = kvd // 2
    hd = ATT_HEAD_DIM
    nq = d // hd
    group = nq // ATT_KV_HEADS
    tq = min(s, ATT_TILE)
    per = tq // WINDOW
    nt = s // tq

    def body(q_ref, kvc_ref, kvp_ref, o_ref, do_ref, sink_ref, dq_ref, dkvc_ref, dkvp_ref, ds_ref, band, dband, bias_ref):
        i = pl.program_id(0)
        _fill_attn_bias(bias_ref, group, nq)
        band[0:WINDOW, :] = kvp_ref[...]
        band[WINDOW:, :] = kvc_ref[...]
        dband[...] = jnp.zeros_like(dband)
        ds_ref[...] = jnp.zeros_like(ds_ref)

        def block(b, carry):
            rows = pl.ds(pl.multiple_of(b * WINDOW, WINDOW), WINDOW)
            keys = pl.ds(pl.multiple_of(b * WINDOW, WINDOW), 2 * WINDOW)
            first = (i * per + b) == 0
            dks, dvs = [], []
            for g in range(ATT_KV_HEADS):
                kb = band[keys, g * hd:(g + 1) * hd] * ATT_SCALE
                vb = band[keys, half + g * hd:half + (g + 1) * hd]
                qs = _stack_heads(q_ref, rows, g * group, group)
                dos = _stack_heads(do_ref, rows, g * group, group)
                p, ps = _attn_probs_t(kb, qs, sink_ref, g * group, group, jnp.where(first, bias_ref[1, g], bias_ref[0, g]))
                prod = dos.astype(F32) * _stack_heads(o_ref, rows, g * group, group).astype(F32)
                dsum = lax.dot_general(jnp.ones((8, hd), F32), prod, NT, precision=lax.Precision.HIGHEST,
                                       preferred_element_type=F32)[0:1, :]
                dsc = p * (_dot(vb, dos, NT) - dsum)
                dvs.append(_dot(p, dos))
                dks.append(_dot(dsc, qs * ATT_SCALE))
                _unstack_heads(_dot(kb, dsc, TN).T, dq_ref, rows, g * group, group)
                gone = ps * dsum
                for j in range(group):
                    ds_ref[g * group + j:g * group + j + 1, :] += jnp.zeros((1, 128), F32) - jnp.sum(gone[:, j * WINDOW:(j + 1) * WINDOW])
            dband[keys, 0:half] += jnp.concatenate(dks, axis=1)
            dband[keys, half:] += jnp.concatenate(dvs, axis=1)
            return carry

        lax.fori_loop(0, per, block, 0)
        dkvp_ref[...] = dband[0:WINDOW, :]
        dkvc_ref[...] = dband[WINDOW:, :]

    big = pl.BlockSpec((tq, d), lambda i: (i, 0))
    return pl.pallas_call(
        body, name="attn_bwd", grid=(nt,),
        in_specs=_attn_specs(s, d, kvd, tq) + [big, big, pl.BlockSpec(memory_space=pltpu.SMEM)],
        out_specs=[big, pl.BlockSpec((tq, kvd), lambda i: (i, 0)), pl.BlockSpec((None, WINDOW, kvd), lambda i: (i, 0, 0)),
                   pl.BlockSpec((None, nq, 128), lambda i: (i, 0, 0))],
        out_shape=[jax.ShapeDtypeStruct((s, d), BF16), jax.ShapeDtypeStruct((s, kvd), F32), jax.ShapeDtypeStruct((nt, WINDOW, kvd), F32),
                   jax.ShapeDtypeStruct((nt, nq, 128), F32)],
        scratch_shapes=[pltpu.VMEM((tq + WINDOW, kvd), BF16), pltpu.VMEM((tq + WINDOW, kvd), F32),
                        pltpu.VMEM((2, ATT_KV_HEADS, 2 * WINDOW, group * WINDOW), F32)],
        compiler_params=_params(("arbitrary",)),
    )(q, kv, kv, o, do, sinks)


HBM_SPEC = pl.BlockSpec(memory_space=pltpu.HBM)
VMEM_SPEC = pl.BlockSpec(memory_space=pltpu.VMEM)


def _place():
    return lax.axis_index("x"), lax.axis_index("y"), lax.axis_index("c")


def _flip(pos, r):
    return tuple(1 - p if (r >> (2 - a)) & 1 else p for a, p in enumerate(pos))


def _index(pos):
    return 4 * pos[0] + 2 * pos[1] + pos[2]


def _all_gather(name, shards, spec):
    n = len(shards)

    def body(*refs):
        x_refs, o_refs = refs[:n], refs[n:2 * n]
        send_sems, recv_sems, local_sems = refs[2 * n:]
        me = _place()
        sibling = _flip(me, 1)
        far = [_flip(me, r) for r in (4, 2, 6)]

        def copy(t, sem, block, to, src=None):
            rows = o_refs[t].at[_index(block)]
            return pltpu.make_async_remote_copy(
                src_ref=rows if src is None else src, dst_ref=rows, send_sem=send_sems.at[t, sem], recv_sem=recv_sems.at[t, sem],
                device_id=to, device_id_type=MESH)

        own = [pltpu.make_async_copy(x_refs[t], o_refs[t].at[_index(me)], local_sems.at[t]) for t in range(n)]
        for cp in own:
            cp.start()
        first = []
        for t in range(n):
            first.append(copy(t, 0, me, sibling, src=x_refs[t]))
            first += [copy(t, 1 + j, me, peer, src=x_refs[t]) for j, peer in enumerate(far)]
        for cp in first:
            cp.start()
        passed = []
        for j, peer in enumerate(far):
            for t in range(n):
                copy(t, 1 + j, peer, me).wait_recv()
                cp = copy(t, 4 + j, peer, sibling)
                cp.start()
                passed.append(cp)
        for t in range(n):
            copy(t, 0, sibling, me).wait_recv()
            for j, peer in enumerate(far):
                copy(t, 4 + j, _flip(peer, 1), me).wait_recv()
        for cp in first + passed:
            cp.wait_send()
        for cp in own:
            cp.wait()

    return pl.pallas_call(
        body, name=name, in_specs=[spec] * n, out_specs=[spec] * n,
        out_shape=[jax.ShapeDtypeStruct((N_DEV,) + sh.shape, sh.dtype) for sh in shards],
        scratch_shapes=[pltpu.SemaphoreType.DMA((n, 7)), pltpu.SemaphoreType.DMA((n, 7)), pltpu.SemaphoreType.DMA((n,))],
    )(*shards)


SEM_SPEC = pl.BlockSpec(memory_space=pltpu.SEMAPHORE)
ANY_SPEC = pl.BlockSpec(memory_space=pl.ANY)


def _landing(own, mine):
    return lax.dynamic_update_slice(lax.empty((N_DEV,) + own.shape, own.dtype), own[None], (mine,) + (0,) * own.ndim)


def _peer_copies(src_refs, land_refs, send_sems, recv_sems, scatter, arrivals):
    me = _place()
    mine = _index(me)
    copies = []
    for t, (src, land) in enumerate(zip(src_refs, land_refs)):
        for r in range(1, N_DEV):
            peer = _flip(me, r)
            theirs = _index(peer)
            sem = t * N_DEV + r - 1
            copies.append(pltpu.make_async_remote_copy(
                src_ref=src.at[theirs] if scatter else src, dst_ref=land.at[theirs if arrivals else mine],
                send_sem=send_sems.at[sem], recv_sem=recv_sems.at[sem], device_id=peer, device_id_type=MESH))
    return copies


def _own_copies(src_refs, land_refs, send_sems):
    mine = _index(_place())
    return [pltpu.make_async_copy(src.at[mine], land.at[mine], send_sems.at[t * N_DEV + N_DEV - 1])
            for t, (src, land) in enumerate(zip(src_refs, land_refs))]


def _send_start(name, sources, lands, scatter, after=None, carry=None):
    n = len(sources)
    extra = [a for a in (after, carry) if a is not None]
    token = jax.ShapeDtypeStruct((8, 128), F32) if carry is None else jax.ShapeDtypeStruct(carry.shape, carry.dtype)

    def body(*refs):
        outs = refs[2 * n + len(extra):]
        for out in _peer_copies(refs[:n], refs[n:2 * n], outs[0], outs[1], scatter, False) + (_own_copies(refs[:n], refs[n:2 * n], outs[0]) if scatter else []):
            out.start()
        outs[-1][...] = jnp.zeros_like(outs[-1]) if carry is None else refs[2 * n + len(extra) - 1][...]

    outs = pl.pallas_call(
        body, name=name, in_specs=[HBM_SPEC] * (2 * n) + [ANY_SPEC] * (after is not None) + [VMEM_SPEC] * (carry is not None),
        out_specs=[SEM_SPEC, SEM_SPEC] + [HBM_SPEC] * (2 * n) + [VMEM_SPEC],
        out_shape=[pltpu.SemaphoreType.DMA((n * N_DEV,)), pltpu.SemaphoreType.DMA((n * N_DEV,))]
        + [pltpu.HBM(a.shape, a.dtype) for a in list(sources) + list(lands)] + [token],
        input_output_aliases={i: 2 + i for i in range(2 * n)},
        compiler_params=pltpu.CompilerParams(has_side_effects=pltpu.SideEffectType.DATAFLOW_SIDE_EFFECTING),
    )(*[pltpu.with_memory_space_constraint(a, pltpu.HBM) for a in list(sources) + list(lands)], *extra)
    return outs[0], outs[1], outs[2:2 + n], outs[2 + n:2 + 2 * n], outs[-1]


def _send_wait(name, started, after, scatter):
    send_sems, recv_sems, sources, lands, _ = started
    n = len(sources)

    def body(*refs):
        for out in _peer_copies(refs[:n], refs[n:2 * n], refs[2 * n], refs[2 * n + 1], scatter, False):
            out.wait_send()
        for own in _own_copies(refs[:n], refs[n:2 * n], refs[2 * n]) if scatter else []:
            own.wait()
        for arrival in _peer_copies(refs[:n], refs[n:2 * n], refs[2 * n], refs[2 * n + 1], scatter, True):
            arrival.wait_recv()

    outs = pl.pallas_call(
        body, name=name, in_specs=[HBM_SPEC] * (2 * n) + [SEM_SPEC, SEM_SPEC, ANY_SPEC], out_specs=[HBM_SPEC] * (2 * n),
        out_shape=[pltpu.HBM(a.shape, a.dtype) for a in list(sources) + list(lands)],
        input_output_aliases={i: i for i in range(2 * n)},
        compiler_params=pltpu.CompilerParams(has_side_effects=pltpu.SideEffectType.DATAFLOW_SIDE_EFFECTING),
    )(*sources, *lands, send_sems, recv_sems, after)
    return outs[n:]


def _pack_rows(parts):
    offsets, row = [], 0
    for part in parts:
        offsets.append(row)
        row += part.shape[0]
    return offsets, -(-row // 8) * 8, -(-max(part.shape[1] for part in parts) // 128) * 128


def _pack(name, parts):
    offsets, rows, width = _pack_rows(parts)

    def body(*refs):
        o_ref = refs[-1]
        o_ref[...] = jnp.zeros_like(o_ref)
        for off, ref in zip(offsets, refs[:-1]):
            o_ref[off:off + ref.shape[0], 0:ref.shape[1]] = ref[...]

    return pl.pallas_call(body, name=name, in_specs=[VMEM_SPEC] * len(parts), out_specs=VMEM_SPEC,
                          out_shape=jax.ShapeDtypeStruct((rows, width), F32))(*parts)


def _adamw_math(w, g, m, v):
    m = ADAM_B1 * m + (1.0 - ADAM_B1) * g
    v = ADAM_B2 * v + (1.0 - ADAM_B2) * (g * g)
    m_hat = m * (1.0 / (1.0 - ADAM_B1 ** ADAM_STEP))
    denom = jnp.sqrt(v * (1.0 / (1.0 - ADAM_B2 ** ADAM_STEP))) + ADAM_EPS
    inv = pl.reciprocal(denom, approx=True)
    inv = inv * (2.0 - denom * inv)
    return -ADAM_LR * (m_hat * inv + ADAM_WD * w), m, v


def _adamw_step(w_ref, m_ref, v_ref, p_ref, g_ref, d_ref, nm_ref, nv_ref):
    g = p_ref[0].astype(F32)
    for dev in range(1, N_DEV):
        g = g + p_ref[dev].astype(F32)
    g_ref[...] = g
    d_ref[...], nm_ref[...], nv_ref[...] = _adamw_math(w_ref[...], g, m_ref[...], v_ref[...])


def _adamw_rows(rows):
    return max(t for t in range(8, min(rows, 256) + 1, 8) if rows % t == 0)


def _adamw_shard(name, w, m, v, partials):
    rows, cols = w.shape
    tr = _adamw_rows(rows)
    blk = pl.BlockSpec((tr, cols), lambda i: (i, 0))
    return pl.pallas_call(
        _adamw_step_fn(), name=name, grid=(rows // tr,), in_specs=[blk, blk, blk, pl.BlockSpec((N_DEV, tr, cols), lambda i: (0, i, 0))],
        out_specs=[blk] * 4, out_shape=[jax.ShapeDtypeStruct((rows, cols), F32)] * 4, compiler_params=_params(("parallel",)),
    )(w, m, v, partials)


def _adamw_step_fn():
    return functools.partial(_adamw_step)


def _adamw_layers(name, w, m, v, partials):
    layers, rows, cols = w.shape
    tr = _adamw_rows(rows)
    last = rows // tr - 1

    def body(w_ref, m_ref, v_ref, *rest):
        for layer in range(layers):
            @pl.when(pl.program_id(0) == layer)
            def _():
                _adamw_step(w_ref, m_ref, v_ref, rest[layer], *rest[layers:])

    blk = pl.BlockSpec((None, tr, cols), lambda l, i: (l, i, 0))
    part = lambda layer: pl.BlockSpec((N_DEV, tr, cols), lambda l, i: (0, jnp.where(l == layer, i, jnp.where(l < layer, 0, last)), 0))
    return pl.pallas_call(
        body, name=name, grid=(layers, rows // tr), in_specs=[blk, blk, blk] + [part(layer) for layer in range(layers)],
        out_specs=[blk] * 4, out_shape=[jax.ShapeDtypeStruct(w.shape, F32)] * 4, compiler_params=_params(("arbitrary", "arbitrary")),
    )(w, m, v, *partials)


def _adamw_small(gathered, places, entries):
    n = len(entries)
    np_ = len(gathered)

    def body(*refs):
        pack_refs = refs[:np_]
        refs = refs[np_ - 1:]
        w_refs, m_refs, v_refs = refs[1:1 + n], refs[1 + n:1 + 2 * n], refs[1 + 2 * n:1 + 3 * n]
        outs = refs[1 + 3 * n:]
        totals = []
        for pack_ref in pack_refs:
            acc = pack_ref[0]
            for dev in range(1, N_DEV):
                acc = acc + pack_ref[dev]
            totals.append(acc)
        mine = _index(_place())
        for e in range(n):
            rows, cols = w_refs[e].shape
            total, off = totals[places[e][0]], places[e][1]
            if entries[e][3]:
                g = jnp.zeros((rows, cols), F32)
                for dev in range(N_DEV):
                    g = g + jnp.where(mine == dev, total[off + dev * rows:off + (dev + 1) * rows, 0:cols], 0.0)
            else:
                g = total[off:off + rows, 0:cols]
            outs[4 * e][...] = g
            outs[4 * e + 1][...], outs[4 * e + 2][...], outs[4 * e + 3][...] = _adamw_math(w_refs[e][...], g, m_refs[e][...], v_refs[e][...])
        outs[4 * n][...] = totals[places[n][0]][places[n][1]:places[n][1] + 1, 0:128]

    shapes = []
    for w, _, _, _ in entries:
        shapes += [jax.ShapeDtypeStruct(w.shape, F32)] * 4
    shapes.append(jax.ShapeDtypeStruct((1, 128), F32))
    return pl.pallas_call(
        body, name="adamw_small", in_specs=[VMEM_SPEC] * (np_ + 3 * n), out_specs=[VMEM_SPEC] * len(shapes), out_shape=shapes,
        compiler_params=pltpu.CompilerParams(vmem_limit_bytes=VMEM_LIMIT),
    )(*gathered, *[e[0] for e in entries], *[e[1] for e in entries], *[e[2] for e in entries])


def _ffn_forward(tag, h, gain, w_up, late):
    s, d = h.shape
    fb = w_up.shape[1]
    tm = _row_tile(s, 2 * MM_ROWS)
    (a,), (u,) = _norm_proj(f"ffn_up_{tag}", h, [
        (gain, w_up, pl.BlockSpec((None, fb, d), lambda i, j: (j, 0, 0)), NT,
         pl.BlockSpec((None, None, tm, fb), lambda i, j: (j // 4, j % 4, i, 0)), jax.ShapeDtypeStruct((2, 4, s, fb), BF16))], tm=tm, nj=N_DEV)
    w_down, conv_w, conv_b = late(u)
    hidden, conv, out = _ffn_hidden_down(f"ffn_hidden_down_{tag}", u, conv_w, conv_b, w_down, h)
    return out, (a, u, hidden, conv)


def _ffn_backward(tag, h, gain, w_up, w_down, conv_w, conv_b, saved, dout, sent=None):
    a, u, hidden, conv = saved
    dout, dout_bf = dout
    s, d = h.shape
    fb = w_up.shape[1]
    tm = _row_tile(s, MM_ROWS)
    dhidden = _matmul(
        f"ffn_down_bwd_{tag}", dout_bf, w_down, dims=NT, grid=(s // tm, 4, 1),
        a_spec=pl.BlockSpec((tm, d), lambda i, j, k: (i, 0)),
        b_spec=pl.BlockSpec((None, fb, d), lambda i, j, k: (j, 0, 0)),
        o_spec=pl.BlockSpec((None, tm, fb), lambda i, j, k: (j, i, 0)),
        out_shape=jax.ShapeDtypeStruct((4, s, fb), BF16))
    dw_down = _matmul(
        f"ffn_down_grad_{tag}", hidden, dout_bf, dims=TN, grid=(4, 1, 1),
        a_spec=pl.BlockSpec((None, s, fb), lambda i, j, k: (i, 0, 0)),
        b_spec=pl.BlockSpec((s, d), lambda i, j, k: (0, 0)),
        o_spec=pl.BlockSpec((None, fb, d), lambda i, j, k: (i, 0, 0)),
        out_shape=jax.ShapeDtypeStruct((4, fb, d), BF16))
    if sent is not None:
        gain = sent(dw_down, gain)
    du, dconv_w, dconv_b, dh, dh_bf, dgain = _ffn_hidden_up_bwd(f"ffn_hidden_up_bwd_{tag}", u, conv, dhidden, conv_w, w_up, h, gain, dout)
    dw_up = _matmul(
        f"ffn_up_grad_{tag}", du, a, dims=TN, grid=(N_DEV, 1, 1),
        a_spec=pl.BlockSpec((None, None, s, fb), lambda i, j, k: (i // 4, i % 4, 0, 0)),
        b_spec=pl.BlockSpec((s, d), lambda i, j, k: (0, 0)),
        o_spec=pl.BlockSpec((None, fb, d), lambda i, j, k: (i, 0, 0)),
        out_shape=jax.ShapeDtypeStruct((N_DEV, fb, d), BF16))
    return (dh, dh_bf), dgain, dw_up, dw_down, dconv_w, dconv_b


def kernel(x, hg_norm, hg_w_in, hg_lb_logits, hg_out_norm, hg_w_out, kv_norm, w_kv, attn_norm, attn_w_q, attn_sinks, attn_w_o, ffn_norm, ffn_w_up, ffn_conv_w, ffn_conv_b, ffn_w_down, final_norm, loss_target, m_hg_norm, m_hg_w_in, m_hg_lb_logits, m_hg_out_norm, m_hg_w_out, m_kv_norm, m_w_kv, m_attn_norm, m_attn_w_q, m_attn_sinks, m_attn_w_o, m_ffn_norm, m_ffn_w_up, m_ffn_conv_w, m_ffn_conv_b, m_ffn_w_down, m_final_norm, v_hg_norm, v_hg_w_in, v_hg_lb_logits, v_hg_out_norm, v_hg_w_out, v_kv_norm, v_w_kv, v_attn_norm, v_attn_w_q, v_attn_sinks, v_attn_w_o, v_ffn_norm, v_ffn_w_up, v_ffn_conv_w, v_ffn_conv_b, v_ffn_w_down, v_final_norm):
    _, s, d = x.shape
    x0, target = x[0], loss_target[0]
    half = hg_w_in.shape[2]
    fs = ffn_conv_w.shape[2]
    fb = 2 * fs
    kvd = w_kv.shape[1]
    nq = d // ATT_HEAD_DIM
    tm = _row_tile(s, MM_ROWS)

    mine = _index(_place())
    gather = lambda tag, shards, after, carry=None: _send_start("gather_start_" + tag, shards, [_landing(a, mine) for a in shards], False, after, carry)
    w_in, g_hgn, g_lbl, w_out = _all_gather("gather_hg", [hg_w_in[0].astype(BF16), hg_norm, hg_lb_logits, hg_w_out[0].astype(BF16)], HBM_SPEC)
    w_out = w_out.reshape(d, d)
    up_t = lambda a: jnp.swapaxes(a, -1, -2)
    coming_up0 = gather("ffn_up0", [up_t(ffn_w_up[0]).astype(BF16)], None, g_hgn.reshape(1, d))
    hgn = coming_up0[4]
    lbl = g_lbl.transpose(1, 0, 2).reshape(2, d)
    conv_b = [ffn_conv_b[layer].reshape(4, 1, fb) for layer in range(2)]
    gains = [ffn_norm[0:1], ffn_norm[1:2]]
    kvn, fin = kv_norm.reshape(1, d), final_norm.reshape(1, d)

    t2 = _row_tile(s, 2 * MM_ROWS)
    (a0,), (p,) = _norm_proj("hg_in", x0, [
        (hgn, w_in, pl.BlockSpec((None, d, half), lambda i, j: (j, 0, 0)), NN,
         pl.BlockSpec((None, t2, half), lambda i, j: (j // 2, i, j % 2)), jax.ShapeDtypeStruct((4, s, d), BF16))], tm=t2, nj=N_DEV)
    o, og, states, gsum = _hgrn2_fwd(p, lbl, hg_out_norm)
    coming_dn0 = gather("ffn_down0", [ffn_conv_w, ffn_w_down[0].astype(BF16)], o)
    x1 = _mm_rows("hg_out", og, w_out, out_dtype=F32, add=x0, after=coming_dn0[4])
    w_up0, = _send_wait("gather_wait_ffn_up0", coming_up0, x1, False)
    coming_attn = gather("attn", [w_kv.astype(BF16), attn_w_q[0].astype(BF16), attn_w_o[0].astype(BF16)], w_up0, gains[0])
    gains[0] = coming_attn[4]
    w_up, w_dn, conv_w, coming = [w_up0, None], [None, None], [], {}

    def late0(u):
        g_cw, w_dn0 = _send_wait("gather_wait_ffn_down0", coming_dn0, u, False)
        w_dn[0] = w_dn0.reshape(4, fb, d)
        conv_w.extend(g_cw[:, layer].reshape(4, 2, CONV_WIDTH, fs).transpose(0, 2, 1, 3).reshape(4, CONV_WIDTH, fb) for layer in range(2))
        coming["up1"] = gather("ffn_up1", [up_t(ffn_w_up[1]).astype(BF16)], w_dn0, conv_b[0])
        return w_dn[0], conv_w[0], coming["up1"][4]

    x2, saved0 = _ffn_forward("0", x1, gains[0], w_up[0], late0)
    w_kvg, w_q, w_o = _send_wait("gather_wait_attn", coming_attn, x2, False)
    w_kvg, w_q, w_o = w_kvg.reshape(d, kvd), w_q.reshape(d, d), w_o.reshape(d, d)
    (akv, a2), (kv, q) = _norm_proj("attn_in", x2, [
        (kvn, w_kvg, pl.BlockSpec((d, kvd), lambda i, j: (0, 0)), NN, pl.BlockSpec((tm, kvd), lambda i, j: (i, 0)), jax.ShapeDtypeStruct((s, kvd), BF16)),
        (attn_norm, w_q, pl.BlockSpec((d, d), lambda i, j: (0, 0)), NN, pl.BlockSpec((tm, d), lambda i, j: (i, 0)), jax.ShapeDtypeStruct((s, d), BF16))],
        tm=tm, nj=1)
    coming_dn1 = gather("ffn_down1", [ffn_w_down[1].astype(BF16)], q, attn_sinks)
    att = _attn_fwd(q, kv, coming_dn1[4])
    x3 = _mm_rows("attn_out", att, w_o, out_dtype=F32, add=x2)
    w_up[1], = _send_wait("gather_wait_ffn_up1", coming["up1"], x3, False)

    def late1(u):
        w_dn[1] = _send_wait("gather_wait_ffn_down1", coming_dn1, u, False)[0].reshape(4, fb, d)
        return w_dn[1], conv_w[1], conv_b[1]

    x4, saved1 = _ffn_forward("1", x3, gains[1], w_up[1], late1)
    dx4, dx4_bf, d_fin, loss_part = _loss_head(x4, fin, target)

    dx3, d_fn1, dw_up1, dw_dn1, dcw1, dcb1 = _ffn_backward("1", x3, gains[1], w_up[1], w_dn[1], conv_w[1], conv_b[1], saved1, (dx4, dx4_bf))
    rows = d // N_DEV
    scatter = lambda tag, stacks, carry: _send_start("scatter_start_" + tag, stacks, [lax.empty(a.shape, a.dtype) for a in stacks], True, None, carry)
    going_ffn1 = scatter("ffn1", [dw_up1, dw_dn1.reshape(N_DEV, fs, d)], attn_sinks)
    datt = _mm_rows_nt("attn_out_bwd", dx3[1], w_o, out_dtype=BF16)
    dw_o = _mm_tn("attn_out_grad", att, dx3[1])
    dq, dkv_own, dkv_before, dsink = _attn_bwd(q, kv, att, datt, going_ffn1[4])
    tiles = dkv_before.shape[0]
    dkv = dkv_own.reshape(tiles, s // tiles, kvd)
    dkv = jnp.concatenate([dkv[:, :-WINDOW], dkv[:, -WINDOW:] + jnp.pad(dkv_before[1:], ((0, 1), (0, 0), (0, 0)))], axis=1).reshape(s, kvd)
    dw_q = _mm_tn("q_proj_grad", a2, dq)
    dw_kv = _mm_tn("kv_proj_grad", akv, dkv)
    going_attn = scatter("attn", [dw_kv.reshape(N_DEV, rows, kvd), dw_q.reshape(N_DEV, rows, d), dw_o.reshape(N_DEV, rows, d)], kvn)
    whole = lambda a_ref, b_ref: [(a_ref[...], b_ref[...])]
    rows_of = lambda width: (lambda tile: pl.BlockSpec((tile, width), lambda i: (i, 0)))
    dx2, (d_kvn, d_attn) = _proj_norm_bwd("attn_in_bwd", x2, dx3[0], [
        (dkv, rows_of(kvd), w_kvg, pl.BlockSpec((d, kvd), lambda i: (0, 0)), whole, going_attn[4]),
        (dq, rows_of(d), w_q, pl.BlockSpec((d, d), lambda i: (0, 0)), whole, attn_norm)])
    going = {}

    def sent0(dw_dn0, gain):
        going["ffn_dn0"] = scatter("ffn_dn0", [dw_dn0.reshape(N_DEV, fs, d)], gain)
        return going["ffn_dn0"][4]

    dx1, d_fn0, dw_up0, _, dcw0, dcb0 = _ffn_backward("0", x1, gains[0], w_up[0], w_dn[0], conv_w[0], conv_b[0], saved0, dx2, sent0)
    dw_out = _mm_tn("hg_out_grad", og, dx1[1])
    going_ffn0 = scatter("ffn0", [dw_up0, dw_out.reshape(N_DEV, rows, d)], hg_out_norm)
    dog = _mm_rows_nt("hg_out_bwd", dx1[1], w_out, out_dtype=F32)
    dp, d_lbl, d_ogain = _hgrn2_bwd(p, lbl, going_ffn0[4], o, dog, states, gsum)
    dw_in = _matmul(
        "hg_in_grad", a0, dp, dims=TN, grid=(1, N_DEV, 1),
        a_spec=pl.BlockSpec((s, d), lambda i, j, k: (0, 0)),
        b_spec=pl.BlockSpec((None, s, half), lambda i, j, k: (j // 2, 0, j % 2)),
        o_spec=pl.BlockSpec((None, d, half), lambda i, j, k: (j, 0, 0)),
        out_shape=jax.ShapeDtypeStruct((N_DEV, d, half), BF16))
    going_hg = scatter("hg", [dw_in], hgn)
    (dx0, _), (d_hgn,) = _proj_norm_bwd("hg_in_bwd", x0, dx1[0], [
        (dp, lambda tile: pl.BlockSpec((4, tile, d), lambda i: (0, i, 0)), w_in, pl.BlockSpec((N_DEV, d, half), lambda i: (0, 0, 0)),
         lambda g_ref, w_ref: [(g_ref[k // 2, :, (k % 2) * half:(k % 2 + 1) * half], w_ref[k]) for k in range(N_DEV)],
         going_hg[4])])

    as_blocks = lambda a, r: a.reshape(r, N_DEV, -1).transpose(1, 0, 2).reshape(N_DEV * r, -1)
    d_cw = jnp.concatenate([g.transpose(1, 0, 2).reshape(CONV_WIDTH, 4 * fb) for g in (dcw0, dcw1)], axis=0)
    parts = [d_fin, jnp.concatenate([d_fn0, d_fn1], axis=0), jnp.concatenate([dcb0.reshape(1, 4 * fb), dcb1.reshape(1, 4 * fb)], axis=0),
             as_blocks(d_cw, 2 * CONV_WIDTH), d_attn, jnp.sum(dsink[:, :, 0], axis=0).reshape(1, nq), d_kvn, d_ogain,
             as_blocks(d_hgn, 1), as_blocks(d_lbl, 2), loss_part]
    wide = [2]
    packs = [[parts[i] for i in wide], [part for i, part in enumerate(parts) if i not in wide]]
    places = [None] * len(parts)
    for which, members in enumerate([wide, [i for i in range(len(parts)) if i not in wide]]):
        for i, off in zip(members, _pack_rows(packs[which])[0]):
            places[i] = (which, off)
    packed = [_pack("pack_wide_grads", packs[0]), _pack("pack_narrow_grads", packs[1])]
    going_small = _send_start("small_grads_start", packed, [_landing(a, mine) for a in packed], False)

    arrive = lambda tag, going, after: _send_wait("scatter_wait_" + tag, going, after, True)
    (l_up1, l_dn1), (l_kv, l_q, l_o), (l_dn0,), (l_up0, l_out) = (
        arrive("ffn1", going_ffn1, going_small[4]), arrive("attn", going_attn, going_small[4]),
        arrive("ffn_dn0", going["ffn_dn0"], going_small[4]), arrive("ffn0", going_ffn0, going_small[4]))
    big = {}
    for tag, w, m, v, part in [
            ("w_kv", w_kv, m_w_kv, v_w_kv, l_kv), ("attn_w_q", attn_w_q[0], m_attn_w_q[0], v_attn_w_q[0], l_q),
            ("attn_w_o", attn_w_o[0], m_attn_w_o[0], v_attn_w_o[0], l_o)]:
        big[tag] = _adamw_shard("adamw_" + tag, w, m, v, part)
    up_done = _adamw_layers("adamw_ffn_w_up", up_t(ffn_w_up), up_t(m_ffn_w_up), up_t(v_ffn_w_up), (l_up0, l_up1))
    big["ffn_w_up"] = [up_t(a) for a in up_done]
    big["ffn_w_down"] = _adamw_layers("adamw_ffn_w_down", ffn_w_down, m_ffn_w_down, v_ffn_w_down, (l_dn0, l_dn1))
    lead = lambda tag: [a[None] for a in big[tag]]

    both_done = up_done[0][0, 0:1, 0:1] + big["ffn_w_down"][0][0, 0:1, 0:1]
    gathered = _send_wait("small_grads_wait", going_small, both_done, False)
    two = lambda a: a.reshape(-1, a.shape[-1])
    small = [(fin, m_final_norm.reshape(1, d), v_final_norm.reshape(1, d), False), (ffn_norm, m_ffn_norm, v_ffn_norm, False),
             (ffn_conv_b, m_ffn_conv_b, v_ffn_conv_b, False), (two(ffn_conv_w), two(m_ffn_conv_w), two(v_ffn_conv_w), True),
             (attn_norm, m_attn_norm, v_attn_norm, False), (attn_sinks, m_attn_sinks, v_attn_sinks, False),
             (kvn, m_kv_norm.reshape(1, d), v_kv_norm.reshape(1, d), False), (hg_out_norm, m_hg_out_norm, v_hg_out_norm, False),
             (hg_norm, m_hg_norm, v_hg_norm, True), (hg_lb_logits, m_hg_lb_logits, v_hg_lb_logits, True)]
    res = _adamw_small(gathered, places, small)
    l_in, = arrive("hg", going_hg, gathered[1])
    big["hg_w_in"] = _adamw_shard("adamw_hg_w_in", hg_w_in[0], m_hg_w_in[0], v_hg_w_in[0], l_in)
    big["hg_w_out"] = _adamw_shard("adamw_hg_w_out", hg_w_out[0], m_hg_w_out[0], v_hg_w_out[0], l_out)
    names = ["final_norm", "ffn_norm", "ffn_conv_b", "ffn_conv_w", "attn_norm", "attn_sinks", "kv_norm", "hg_out_norm", "hg_norm", "hg_lb_logits"]
    shapes = {"final_norm": final_norm.shape, "kv_norm": kv_norm.shape, "ffn_conv_w": ffn_conv_w.shape}
    out = {n: [a.reshape(shapes[n]) if n in shapes else a for a in res[4 * i:4 * i + 4]] for i, n in enumerate(names)}
    out.update(hg_w_in=lead("hg_w_in"), hg_w_out=lead("hg_w_out"), w_kv=big["w_kv"], attn_w_q=lead("attn_w_q"), attn_w_o=lead("attn_w_o"),
               ffn_w_up=big["ffn_w_up"], ffn_w_down=big["ffn_w_down"])
    order = ["hg_norm", "hg_w_in", "hg_lb_logits", "hg_out_norm", "hg_w_out", "kv_norm", "w_kv", "attn_norm", "attn_w_q", "attn_sinks",
             "attn_w_o", "ffn_norm", "ffn_w_up", "ffn_conv_w", "ffn_conv_b", "ffn_w_down", "final_norm"]
    loss = res[-1][0, 0]
    return (loss, dx0[None], *[out[n][0] for n in order], *[out[n][1] for n in order], *[out[n][2] for n in order], *[out[n][3] for n in order])
```

```python
import functools

import jax
import jax.numpy as jnp
from jax import lax
from jax.experimental import pallas as pl
from jax.experimental.pallas import tpu as pltpu

F32 = jnp.float32
BF16 = jnp.bfloat16

EPS = 1e-6
HG_EXPAND = 128
HG_CHUNK = 32
ATT_HEAD_DIM = 64
ATT_KV_HEADS = 2
WINDOW = 128
CONV_WIDTH = 3
ADAM_LR = 0.001
ADAM_B1 = 0.9
ADAM_B2 = 0.999
ADAM_EPS = 1e-08
ADAM_WD = 0.01
ADAM_STEP = 10

N_DEV = 8
VMEM_LIMIT = 60 * 1024 * 1024
NEG = -1e30

NN = (((1,), (0,)), ((), ()))
NT = (((1,), (1,)), ((), ()))
TN = (((0,), (0,)), ((), ()))
MESH = pl.DeviceIdType.MESH


def _dot(a, b, dims=NN):
    return lax.dot_general(a.astype(BF16), b.astype(BF16), dims, preferred_element_type=F32)


def _sigmoid(x):
    return 0.5 * jnp.tanh(0.5 * x) + 0.5


def _silu(x):
    return x * _sigmoid(x)


def _silu_and_grad(x):
    s = _sigmoid(x)
    return x * s, s * (1.0 + x * (1.0 - s))


def _dsilu(x):
    return _silu_and_grad(x)[1]


def _params(semantics):
    return pltpu.CompilerParams(dimension_semantics=semantics, vmem_limit_bytes=VMEM_LIMIT)


def _row_tile(rows, want=512):
    return min(rows, want)


MM_ROWS = 1024


def _matmul(name, a, b, *, dims, grid, a_spec, b_spec, o_spec, out_shape, add=None, add_spec=None, after=None):
    assert grid[2] == 1

    def body(*refs):
        a_ref, b_ref, o_ref = refs[0], refs[1], refs[-1]
        total = _dot(a_ref[...], b_ref[...], dims)
        if add is not None:
            total = total + refs[2][...]
        o_ref[...] = total.astype(o_ref.dtype)

    in_specs = [a_spec, b_spec] + ([] if add is None else [add_spec]) + ([] if after is None else [pl.BlockSpec(memory_space=pl.ANY)])
    args = (a, b) + (() if add is None else (add,)) + (() if after is None else (after,))
    return pl.pallas_call(
        body, name=name, grid=grid, in_specs=in_specs, out_specs=o_spec, out_shape=out_shape,
        compiler_params=_params(("parallel", "parallel", "arbitrary")),
    )(*args)


def _mm_rows(name, a, w, *, out_dtype, add=None, after=None):
    s, kdim = a.shape
    n = w.shape[1]
    tm = _row_tile(s, MM_ROWS)
    return _matmul(
        name, a, w, dims=NN, grid=(s // tm, 1, 1),
        a_spec=pl.BlockSpec((tm, kdim), lambda i, j, k: (i, 0)),
        b_spec=pl.BlockSpec((kdim, n), lambda i, j, k: (0, 0)),
        o_spec=pl.BlockSpec((tm, n), lambda i, j, k: (i, 0)),
        out_shape=jax.ShapeDtypeStruct((s, n), out_dtype),
        add=add, add_spec=None if add is None else pl.BlockSpec((tm, n), lambda i, j, k: (i, 0)), after=after,
    )


def _mm_rows_nt(name, a, w, *, out_dtype):
    s, n = a.shape
    kdim = w.shape[0]
    tm = _row_tile(s, MM_ROWS)
    return _matmul(
        name, a, w, dims=NT, grid=(s // tm, 1, 1),
        a_spec=pl.BlockSpec((tm, n), lambda i, j, k: (i, 0)),
        b_spec=pl.BlockSpec((kdim, n), lambda i, j, k: (0, 0)),
        o_spec=pl.BlockSpec((tm, kdim), lambda i, j, k: (i, 0)),
        out_shape=jax.ShapeDtypeStruct((s, kdim), out_dtype),
    )


def _mm_tn(name, a, g):
    s, m = a.shape
    n = g.shape[1]
    tn = min(n, 512)
    return _matmul(
        name, a, g, dims=TN, grid=(1, n // tn, 1),
        a_spec=pl.BlockSpec((s, m), lambda i, j, k: (0, 0)),
        b_spec=pl.BlockSpec((s, tn), lambda i, j, k: (0, j)),
        o_spec=pl.BlockSpec((m, tn), lambda i, j, k: (0, j)),
        out_shape=jax.ShapeDtypeStruct((m, n), BF16),
    )


NORM_ROWS = 256


def _norm_proj(name, h, branches, *, tm, nj):
    s, d = h.shape
    n = len(branches)
    rows = min(tm, NORM_ROWS)

    def body(*refs):
        h_ref, gain_refs, w_refs = refs[0], refs[1:1 + n], refs[1 + n:1 + 2 * n]
        a_refs, o_refs = refs[1 + 2 * n:1 + 3 * n], refs[1 + 3 * n:]

        @pl.when(pl.program_id(1) == 0)
        def _():
            def normalize(c, carry):
                at = pl.ds(pl.multiple_of(c * rows, rows), rows)
                xv = h_ref[at, :]
                xhat = xv * lax.rsqrt(jnp.mean(xv * xv, axis=-1, keepdims=True) + EPS)
                for gain_ref, a_ref in zip(gain_refs, a_refs):
                    a_ref[at, :] = (xhat * gain_ref[...]).astype(BF16)
                return carry

            lax.fori_loop(0, tm // rows, normalize, 0)

        for branch, w_ref, a_ref, o_ref in zip(branches, w_refs, a_refs, o_refs):
            o_ref[...] = _dot(a_ref[...], w_ref[...], branch[3]).astype(o_ref.dtype)

    row = pl.BlockSpec((tm, d), lambda i, j: (i, 0))
    vec = pl.BlockSpec((1, d), lambda i, j: (0, 0))
    outs = pl.pallas_call(
        body, name=name, grid=(s // tm, nj), in_specs=[row] + [vec] * n + [b[2] for b in branches],
        out_specs=[row] * n + [b[4] for b in branches],
        out_shape=[jax.ShapeDtypeStruct((s, d), BF16)] * n + [b[5] for b in branches],
        compiler_params=_params(("parallel", "arbitrary")),
    )(h, *[b[0] for b in branches], *[b[1] for b in branches])
    return outs[:n], outs[n:]


def _proj_norm_bwd(name, h, dres, branches):
    s, d = h.shape
    tm = _row_tile(s)
    n = len(branches)

    def body(*refs):
        h_ref, dres_ref = refs[0], refs[1]
        g_refs, w_refs, gain_refs = refs[2:2 + n], refs[2 + n:2 + 2 * n], refs[2 + 2 * n:2 + 3 * n]
        dh_ref, dhb_ref, dg_refs = refs[2 + 3 * n], refs[3 + 3 * n], refs[4 + 3 * n:]
        i = pl.program_id(0)
        xv = h_ref[...]
        r = lax.rsqrt(jnp.mean(xv * xv, axis=-1, keepdims=True) + EPS)
        xhat = xv * r
        total = dres_ref[...]
        for branch, g_ref, w_ref, gain_ref, dg_ref in zip(branches, g_refs, w_refs, gain_refs, dg_refs):
            pairs = branch[4](g_ref, w_ref)
            da = _dot(*pairs[0], NT)
            for pair in pairs[1:]:
                da = da + _dot(*pair, NT)
            dgain = jnp.sum(da * xhat, axis=0, keepdims=True)

            @pl.when(i == 0)
            def _():
                dg_ref[...] = dgain

            @pl.when(i > 0)
            def _():
                dg_ref[...] += dgain

            dxhat = da * gain_ref[...]
            total = total + r * (dxhat - xhat * jnp.mean(dxhat * xhat, axis=-1, keepdims=True))
        dh_ref[...] = total
        dhb_ref[...] = total.astype(BF16)

    row = pl.BlockSpec((tm, d), lambda i: (i, 0))
    vec = pl.BlockSpec((1, d), lambda i: (0, 0))
    outs = pl.pallas_call(
        body, name=name, grid=(s // tm,),
        in_specs=[row, row] + [b[1](tm) for b in branches] + [b[3] for b in branches] + [vec] * n, out_specs=[row, row] + [vec] * n,
        out_shape=[jax.ShapeDtypeStruct((s, d), F32), jax.ShapeDtypeStruct((s, d), BF16)] + [jax.ShapeDtypeStruct((1, d), F32)] * n,
        compiler_params=_params(("arbitrary",)),
    )(h, dres, *[b[0] for b in branches], *[b[2] for b in branches], *[b[5] for b in branches])
    return (outs[0], outs[1]), outs[2:]


def _loss_head(h, gain, target):
    s, d = h.shape
    tm = _row_tile(s)

    def body(h_ref, g_ref, t_ref, dh_ref, dhb_ref, dg_ref, loss_ref):
        i = pl.program_id(0)
        xv = h_ref[...]
        r = lax.rsqrt(jnp.mean(xv * xv, axis=-1, keepdims=True) + EPS)
        xhat = xv * r
        err = xhat * g_ref[...] - t_ref[...]
        dy = err * (1.0 / d)
        part = jnp.zeros((1, 128), F32) + 0.5 * jnp.sum(jnp.mean(err * err, axis=-1, keepdims=True))
        dgain = jnp.sum(dy * xhat, axis=0, keepdims=True)

        @pl.when(i == 0)
        def _():
            dg_ref[...] = dgain
            loss_ref[...] = part

        @pl.when(i > 0)
        def _():
            dg_ref[...] += dgain
            loss_ref[...] += part

        dxhat = dy * g_ref[...]
        dh = r * (dxhat - xhat * jnp.mean(dxhat * xhat, axis=-1, keepdims=True))
        dh_ref[...] = dh
        dhb_ref[...] = dh.astype(BF16)

    row = pl.BlockSpec((tm, d), lambda i: (i, 0))
    vec = pl.BlockSpec((1, d), lambda i: (0, 0))
    return pl.pallas_call(
        body, name="loss_head", grid=(s // tm,), in_specs=[row, vec, row],
        out_specs=[row, row, vec, pl.BlockSpec((1, 128), lambda i: (0, 0))],
        out_shape=[jax.ShapeDtypeStruct((s, d), F32), jax.ShapeDtypeStruct((s, d), BF16), jax.ShapeDtypeStruct((1, d), F32),
                   jax.ShapeDtypeStruct((1, 128), F32)],
        compiler_params=_params(("arbitrary",)),
    )(h, gain, target)


def _bdot(a, b, ca, cb):
    return lax.dot_general(a.astype(BF16), b.astype(BF16), (((ca,), (cb,)), ((0,), (0,))), preferred_element_type=F32)


def _chunk_cumsum(xv, reverse=False):
    n = xv.shape[0]
    row = lax.broadcasted_iota(jnp.int32, xv.shape, 0) % HG_CHUNK
    step = 1
    while step < HG_CHUNK:
        if reverse:
            xv = xv + jnp.where(row < HG_CHUNK - step, pltpu.roll(xv, n - step, axis=0), 0.0)
        else:
            xv = xv + jnp.where(row >= step, pltpu.roll(xv, step, axis=0), 0.0)
        step *= 2
    return xv


def _hg_terms(p_ref, lbl_ref, g_ref=None):
    pq = p_ref[0].astype(F32)
    pf = p_ref[1].astype(F32)
    lb = _sigmoid(lbl_ref[0:1, :] - lbl_ref[1:2, :])
    sig = _sigmoid(pf)
    fg = lb + (1.0 - lb) * sig
    nc = pq.shape[0] // HG_CHUNK
    chunks = lambda a: a.reshape(nc, HG_CHUNK, HG_EXPAND)
    q = chunks(_silu(pq) * HG_EXPAND ** -0.5)
    k = chunks(1.0 - fg)
    v = chunks(p_ref[2].astype(F32))
    g = chunks(_chunk_cumsum(jnp.log(fg)) if g_ref is None else g_ref[...])
    gm = g[:, HG_CHUNK // 2 - 1:HG_CHUNK // 2, :]
    gl = g[:, HG_CHUNK - 1:HG_CHUNK, :]
    e_mid, e_inv, e_all, e_end = jnp.exp(g - gm), jnp.exp(gm - g), jnp.exp(g), jnp.exp(gl - g)
    terms = dict(q=q, k=k, v=v, g=g, qd=q * e_all, qt=q * e_mid, kt=k * e_inv, kd=k * e_end, e_last=jnp.exp(gl),
                 e_mid=e_mid, e_inv=e_inv, e_all=e_all, e_end=e_end)
    return terms, (pq, sig, fg, lb)


def _causal(nc):
    r = lax.broadcasted_iota(jnp.int32, (nc, HG_CHUNK, HG_CHUNK), 1)
    c = lax.broadcasted_iota(jnp.int32, (nc, HG_CHUNK, HG_CHUNK), 2)
    return r >= c


def _hgrn2_fwd(p, lb_logits, out_gain):
    _, s, d = p.shape
    heads = d // HG_EXPAND
    t = _row_tile(s, 4096)
    nc = t // HG_CHUNK

    def body(p_ref, lbl_ref, gain_ref, o_ref, og_ref, st_ref, g_ref, state, decay):
        @pl.when(pl.program_id(1) == 0)
        def _():
            state[...] = jnp.zeros_like(state)

        tm, _ = _hg_terms(p_ref, lbl_ref)
        g_ref[...] = tm["g"].reshape(t, HG_EXPAND)
        decay[...] = tm["e_last"]
        st_ref[...] = _bdot(tm["v"], tm["kd"], 1, 1)

        def chunk(c, carry):
            add = st_ref[c]
            st = state[...]
            st_ref[c] = st
            state[...] = st * decay[c] + add
            return carry

        lax.fori_loop(0, nc, chunk, 0)
        a = jnp.where(_causal(nc), _bdot(tm["qt"], tm["kt"], 2, 2), 0.0)
        ov = (_bdot(tm["qd"], st_ref[...], 2, 2) + _bdot(a, tm["v"], 2, 1)).reshape(t, HG_EXPAND)
        o_ref[...] = ov
        on = ov * lax.rsqrt(jnp.mean(ov * ov, axis=-1, keepdims=True) + EPS) * gain_ref[...]
        og_ref[...] = (on * _silu(p_ref[3].astype(F32))).astype(BF16)

    blk = pl.BlockSpec((t, HG_EXPAND), lambda h, b: (b, h))
    return pl.pallas_call(
        body, name="hgrn2_fwd", grid=(heads, s // t),
        in_specs=[pl.BlockSpec((4, t, HG_EXPAND), lambda h, b: (0, b, h)), pl.BlockSpec((2, HG_EXPAND), lambda h, b: (0, h)),
                  pl.BlockSpec((1, HG_EXPAND), lambda h, b: (0, 0))],
        out_specs=[blk, blk, pl.BlockSpec((None, nc, HG_EXPAND, HG_EXPAND), lambda h, b: (h, b, 0, 0)), blk],
        out_shape=[jax.ShapeDtypeStruct((s, d), F32), jax.ShapeDtypeStruct((s, d), BF16),
                   jax.ShapeDtypeStruct((heads, s // HG_CHUNK, HG_EXPAND, HG_EXPAND), F32), jax.ShapeDtypeStruct((s, d), F32)],
        scratch_shapes=[pltpu.VMEM((HG_EXPAND, HG_EXPAND), F32), pltpu.VMEM((nc, 1, HG_EXPAND), F32)],
        compiler_params=_params(("parallel", "arbitrary")),
    )(p, lb_logits, out_gain)


def _hgrn2_bwd(p, lb_logits, out_gain, o, dog, states, gsum):
    _, s, d = p.shape
    heads = d // HG_EXPAND
    t = _row_tile(s, 1024)
    nc = t // HG_CHUNK
    nb = s // t

    def body(p_ref, lbl_ref, gain_ref, o_ref, dog_ref, st_ref, g_ref, dp_ref, dlbl_ref, dgain_ref, dstate, decay, dst_s):
        h, b = pl.program_id(0), pl.program_id(1)

        @pl.when(b == 0)
        def _():
            dstate[...] = jnp.zeros_like(dstate)

        tm, (pq, sig, fg, lb) = _hg_terms(p_ref, lbl_ref, g_ref)
        pg = p_ref[3].astype(F32)
        ov = o_ref[...]
        r = lax.rsqrt(jnp.mean(ov * ov, axis=-1, keepdims=True) + EPS)
        ohat = ov * r
        dogv = dog_ref[...]
        d_on = dogv * _silu(pg)
        dp_ref[3] = (dogv * ohat * gain_ref[...] * _dsilu(pg)).astype(BF16)
        dgain = jnp.sum(d_on * ohat, axis=0, keepdims=True)

        @pl.when((h == 0) & (b == 0))
        def _():
            dgain_ref[...] = dgain

        @pl.when((h > 0) | (b > 0))
        def _():
            dgain_ref[...] += dgain

        dohat = d_on * gain_ref[...]
        do = (r * (dohat - ohat * jnp.mean(dohat * ohat, axis=-1, keepdims=True))).reshape(nc, HG_CHUNK, HG_EXPAND)

        decay[...] = tm["e_last"]
        dst_s[...] = _bdot(do, tm["qd"], 1, 1)

        def chunk(i, carry):
            c = nc - 1 - i
            add = dst_s[c]
            dst = dstate[...]
            dst_s[c] = dst
            dstate[...] = dst * decay[c] + add
            return carry

        lax.fori_loop(0, nc, chunk, 0)
        st, dst = st_ref[...], dst_s[...]
        causal = _causal(nc)
        a = jnp.where(causal, _bdot(tm["qt"], tm["kt"], 2, 2), 0.0)
        da = jnp.where(causal, _bdot(do, tm["v"], 2, 2), 0.0)
        dqt = _bdot(da, tm["kt"], 2, 1)
        dkt = _bdot(da, tm["qt"], 1, 1)
        dqd = _bdot(do, st, 2, 1)
        dkd = _bdot(tm["v"], dst, 2, 1)
        dv = _bdot(a, do, 1, 1) + _bdot(tm["kd"], dst, 2, 2)
        dq = dqt * tm["e_mid"] + dqd * tm["e_all"]
        dk = dkt * tm["e_inv"] + dkd * tm["e_end"]
        dg = dqt * tm["qt"] - dkt * tm["kt"] + dqd * tm["qd"] - dkd * tm["kd"]
        dgl = jnp.sum(dkd * tm["kd"], axis=1, keepdims=True) + tm["e_last"] * jnp.sum(dst * st, axis=1, keepdims=True)
        last_row = lax.broadcasted_iota(jnp.int32, (nc, HG_CHUNK, HG_EXPAND), 1) == HG_CHUNK - 1
        flat = lambda a3: a3.reshape(t, HG_EXPAND)
        dlf = _chunk_cumsum(flat(dg + jnp.where(last_row, dgl, 0.0)), reverse=True)
        dfg = dlf / fg - flat(dk)
        dlb = jnp.sum(dfg * (1.0 - sig), axis=0, keepdims=True)
        dl0 = dlb * lb * (1.0 - lb)
        dlbl = jnp.concatenate([dl0, -dl0], axis=0)

        @pl.when(b == 0)
        def _():
            dlbl_ref[...] = dlbl

        @pl.when(b > 0)
        def _():
            dlbl_ref[...] += dlbl

        dp_ref[0] = (flat(dq) * HG_EXPAND ** -0.5 * _dsilu(pq)).astype(BF16)
        dp_ref[1] = (dfg * (1.0 - lb) * sig * (1.0 - sig)).astype(BF16)
        dp_ref[2] = flat(dv).astype(BF16)

    blk = pl.BlockSpec((t, HG_EXPAND), lambda h, b: (nb - 1 - b, h))
    pblk = pl.BlockSpec((4, t, HG_EXPAND), lambda h, b: (0, nb - 1 - b, h))
    return pl.pallas_call(
        body, name="hgrn2_bwd", grid=(heads, nb),
        in_specs=[pblk, pl.BlockSpec((2, HG_EXPAND), lambda h, b: (0, h)), pl.BlockSpec((1, HG_EXPAND), lambda h, b: (0, 0)),
                  blk, blk, pl.BlockSpec((None, nc, HG_EXPAND, HG_EXPAND), lambda h, b: (h, nb - 1 - b, 0, 0)), blk],
        out_specs=[pblk, pl.BlockSpec((2, HG_EXPAND), lambda h, b: (0, h)), pl.BlockSpec((1, HG_EXPAND), lambda h, b: (0, 0))],
        out_shape=[jax.ShapeDtypeStruct((4, s, d), BF16), jax.ShapeDtypeStruct((2, d), F32), jax.ShapeDtypeStruct((1, HG_EXPAND), F32)],
        scratch_shapes=[pltpu.VMEM((HG_EXPAND, HG_EXPAND), F32), pltpu.VMEM((nc, 1, HG_EXPAND), F32),
                        pltpu.VMEM((nc, HG_EXPAND, HG_EXPAND), F32)],
        compiler_params=_params(("arbitrary", "arbitrary")),
    )(p, lb_logits, out_gain, o, dog, states, gsum)


HALO = 8
FFN_FWD_ROWS = 512
FFN_BWD_ROWS = 256


def _shift_down(xv, n):
    return pltpu.roll(xv, n, axis=0)


def _shift_up(xv, n):
    return pltpu.roll(xv, xv.shape[0] - n, axis=0)


def _ffn_hidden_down(name, u, conv_w, conv_b, w_down, h):
    _, nj, s, fb = u.shape
    d = w_down.shape[2]
    tm = _row_tile(s, FFN_FWD_ROWS)
    per = tm // HALO

    def body(gate_ref, prev_ref, val_ref, w_ref, b_ref, wd_ref, h_ref, hid_ref, conv_ref, o_ref):
        i = pl.program_id(0)
        total = h_ref[...]
        for j in range(nj):
            prev = jnp.where(i > 0, prev_ref[j].astype(F32), 0.0)
            ext = jnp.concatenate([prev, gate_ref[j].astype(F32)], axis=0)
            conv = b_ref[j] + w_ref[j, 2:3, :] * ext[HALO:]
            conv = conv + w_ref[j, 1:2, :] * _shift_down(ext, 1)[HALO:]
            conv = conv + w_ref[j, 0:1, :] * _shift_down(ext, 2)[HALO:]
            conv = conv.astype(BF16)
            conv_ref[j] = conv
            hidden = _silu(conv) * val_ref[j]
            hid_ref[j] = hidden
            total = total + _dot(hidden, wd_ref[j])
        o_ref[...] = total

    row = pl.BlockSpec((tm, d), lambda i: (i, 0))
    return pl.pallas_call(
        body, name=name, grid=(s // tm,),
        in_specs=[pl.BlockSpec((None, nj, tm, fb), lambda i: (0, 0, i, 0)),
                  pl.BlockSpec((None, nj, HALO, fb), lambda i: (0, 0, jnp.maximum(i * per - 1, 0), 0)),
                  pl.BlockSpec((None, nj, tm, fb), lambda i: (1, 0, i, 0)),
                  pl.BlockSpec((nj, CONV_WIDTH, fb), lambda i: (0, 0, 0)), pl.BlockSpec((nj, 1, fb), lambda i: (0, 0, 0)),
                  pl.BlockSpec((nj, fb, d), lambda i: (0, 0, 0)), row],
        out_specs=[pl.BlockSpec((nj, tm, fb), lambda i: (0, i, 0)), pl.BlockSpec((nj, tm, fb), lambda i: (0, i, 0)), row],
        out_shape=[jax.ShapeDtypeStruct((nj, s, fb), BF16), jax.ShapeDtypeStruct((nj, s, fb), BF16), jax.ShapeDtypeStruct((s, d), F32)],
        compiler_params=_params(("parallel",)),
    )(u, u, u, conv_w, conv_b, w_down, h)


def _ffn_hidden_up_bwd(name, u, conv, dh, conv_w, w_up, h, gain, dres):
    _, nj, s, fb = u.shape
    d = w_up.shape[2]
    tm = _row_tile(s, FFN_BWD_ROWS)
    per = tm // HALO
    nblk = s // HALO
    ni = s // tm

    def body(gate_ref, conv_ref, cnext_ref, val_ref, vnext_ref, dh_ref, dhnext_ref, w_ref, wu_ref, h_ref, gain_ref, dres_ref,
             du_ref, dw_ref, db_ref, dx_ref, dxb_ref, dgain_ref):
        i = pl.program_id(0)
        has_next = i < ni - 1
        total = None
        for j in range(nj):
            act, dact = _silu_and_grad(conv_ref[j])
            dval = dh_ref[j] * act
            after = jnp.where(has_next, dhnext_ref[j].astype(F32), 0.0) * vnext_ref[j].astype(F32) * _dsilu(cnext_ref[j].astype(F32))
            dconv = jnp.concatenate([(dh_ref[j] * val_ref[j] * dact).astype(F32), after], axis=0)
            taps = [_shift_up(dconv, 2)[:tm], _shift_up(dconv, 1)[:tm], dconv[:tm]]
            dgate = (w_ref[j, 0:1, :] * taps[0] + w_ref[j, 1:2, :] * taps[1] + w_ref[j, 2:3, :] * taps[2]).astype(BF16)
            du_ref[0, j] = dgate
            du_ref[1, j] = dval
            part = _dot(dgate, wu_ref[j]) + _dot(dval, wu_ref[nj + j])
            total = part if total is None else total + part
            gate = gate_ref[j].astype(F32)
            dw = jnp.concatenate([jnp.sum(tap * gate, axis=0, keepdims=True) for tap in taps], axis=0)
            db = jnp.sum(taps[2], axis=0, keepdims=True)

            @pl.when(i == 0)
            def _():
                dw_ref[j] = dw
                db_ref[j] = db

            @pl.when(i > 0)
            def _():
                dw_ref[j] += dw
                db_ref[j] += db

        xv = h_ref[...]
        r = lax.rsqrt(jnp.mean(xv * xv, axis=-1, keepdims=True) + EPS)
        xhat = xv * r
        dgain = jnp.sum(total * xhat, axis=0, keepdims=True)

        @pl.when(i == 0)
        def _():
            dgain_ref[...] = dgain

        @pl.when(i > 0)
        def _():
            dgain_ref[...] += dgain

        dxhat = total * gain_ref[...]
        dx = dres_ref[...] + r * (dxhat - xhat * jnp.mean(dxhat * xhat, axis=-1, keepdims=True))
        dx_ref[...] = dx
        dxb_ref[...] = dx.astype(BF16)

    def tile(part):
        return pl.BlockSpec((None, nj, tm, fb), lambda i: (part, 0, i, 0))

    def after(part):
        return pl.BlockSpec((None, nj, HALO, fb), lambda i: (part, 0, jnp.minimum((i + 1) * per, nblk - 1), 0))

    row = pl.BlockSpec((tm, d), lambda i: (i, 0))
    own = pl.BlockSpec((nj, tm, fb), lambda i: (0, i, 0))
    nxt = pl.BlockSpec((nj, HALO, fb), lambda i: (0, jnp.minimum((i + 1) * per, nblk - 1), 0))
    return pl.pallas_call(
        body, name=name, grid=(ni,),
        in_specs=[tile(0), own, nxt, tile(1), after(1), own, nxt,
                  pl.BlockSpec((nj, CONV_WIDTH, fb), lambda i: (0, 0, 0)),
                  pl.BlockSpec((2 * nj, fb, d), lambda i: (0, 0, 0)), row, pl.BlockSpec((1, d), lambda i: (0, 0)), row],
        out_specs=[pl.BlockSpec((2, nj, tm, fb), lambda i: (0, 0, i, 0)),
                   pl.BlockSpec((nj, CONV_WIDTH, fb), lambda i: (0, 0, 0)), pl.BlockSpec((nj, 1, fb), lambda i: (0, 0, 0)),
                   row, row, pl.BlockSpec((1, d), lambda i: (0, 0))],
        out_shape=[jax.ShapeDtypeStruct((2, nj, s, fb), BF16), jax.ShapeDtypeStruct((nj, CONV_WIDTH, fb), F32),
                   jax.ShapeDtypeStruct((nj, 1, fb), F32), jax.ShapeDtypeStruct((s, d), F32), jax.ShapeDtypeStruct((s, d), BF16),
                   jax.ShapeDtypeStruct((1, d), F32)],
        compiler_params=_params(("arbitrary",)),
    )(u, conv, conv, u, u, dh, dh, conv_w, w_up, h, gain, dres)


ATT_TILE = 512


def _stack_heads(ref, rows, first_head, count):
    hd = ATT_HEAD_DIM
    return jnp.concatenate([ref[rows, (first_head + j) * hd:(first_head + j + 1) * hd] for j in range(count)], axis=0)


def _unstack_heads(stacked, ref, rows, first_head, count):
    hd = ATT_HEAD_DIM
    for pair in range(count // 2):
        both = [stacked[(2 * pair + j) * WINDOW:(2 * pair + j + 1) * WINDOW, :] for j in range(2)]
        ref[rows, (first_head + 2 * pair) * hd:(first_head + 2 * pair + 2) * hd] = jnp.concatenate(both, axis=1).astype(ref.dtype)


def _attn_bias(first_head, count, n_heads, first):
    lanes = count * WINDOW
    ik = lax.broadcasted_iota(jnp.int32, (2 * WINDOW, lanes), 0)
    iq = lax.broadcasted_iota(jnp.int32, (2 * WINDOW, lanes), 1) % WINDOW
    dist = iq + WINDOW - ik
    valid = (dist >= 0) & (dist < WINDOW) & (ik >= (WINDOW if first else 0))
    slope = jnp.concatenate([jnp.zeros((1, WINDOW), F32) + 2.0 ** (-8.0 * (first_head + j + 1) / n_heads) for j in range(count)], axis=1)
    return jnp.where(valid, -slope * dist.astype(F32), NEG)


def _fill_attn_bias(bias_ref, group, n_heads):
    @pl.when(pl.program_id(0) == 0)
    def _():
        for g in range(ATT_KV_HEADS):
            bias_ref[0, g] = _attn_bias(g * group, group, n_heads, False)
            bias_ref[1, g] = _attn_bias(g * group, group, n_heads, True)


def _attn_probs_t(kb_scaled, qs, sink_ref, first_head, count, bias):
    sink = jnp.concatenate([jnp.zeros((1, WINDOW), F32) + sink_ref[0, first_head + j] for j in range(count)], axis=1)
    sc = _dot(kb_scaled, qs, NT) + bias
    m = jnp.maximum(jnp.max(sc, axis=0, keepdims=True), sink)
    e = jnp.exp(sc - m)
    es = jnp.exp(sink - m)
    inv = 1.0 / (jnp.sum(e, axis=0, keepdims=True) + es)
    return e * inv, es * inv


ATT_SCALE = ATT_HEAD_DIM ** -0.5


def _attn_specs(s, d, kvd, tq):
    per = tq // WINDOW
    return [pl.BlockSpec((tq, d), lambda i: (i, 0)), pl.BlockSpec((tq, kvd), lambda i: (i, 0)),
            pl.BlockSpec((WINDOW, kvd), lambda i: (jnp.maximum(i * per - 1, 0), 0))]


def _attn_fwd(q, kv, sinks):
    s, d = q.shape
    kvd = kv.shape[1]
    half = kvd // 2
    hd = ATT_HEAD_DIM
    nq = d // hd
    group = nq // ATT_KV_HEADS
    tq = min(s, ATT_TILE)
    per = tq // WINDOW

    def body(q_ref, kvc_ref, kvp_ref, sink_ref, o_ref, band, bias_ref):
        i = pl.program_id(0)
        _fill_attn_bias(bias_ref, group, nq)
        band[0:WINDOW, :] = kvp_ref[...]
        band[WINDOW:, :] = kvc_ref[...]

        def block(b, carry):
            rows = pl.ds(pl.multiple_of(b * WINDOW, WINDOW), WINDOW)
            keys = pl.ds(pl.multiple_of(b * WINDOW, WINDOW), 2 * WINDOW)
            first = (i * per + b) == 0
            for g in range(ATT_KV_HEADS):
                bias = jnp.where(first, bias_ref[1, g], bias_ref[0, g])
                p, _ = _attn_probs_t(band[keys, g * hd:(g + 1) * hd] * ATT_SCALE, _stack_heads(q_ref, rows, g * group, group), sink_ref,
                                     g * group, group, bias)
                out_t = _dot(band[keys, half + g * hd:half + (g + 1) * hd], p, TN)
                _unstack_heads(out_t.T, o_ref, rows, g * group, group)
            return carry

        lax.fori_loop(0, per, block, 0)

    return pl.pallas_call(
        body, name="attn_fwd", grid=(s // tq,),
        in_specs=_attn_specs(s, d, kvd, tq) + [pl.BlockSpec(memory_space=pltpu.SMEM)],
        out_specs=pl.BlockSpec((tq, d), lambda i: (i, 0)), out_shape=jax.ShapeDtypeStruct((s, d), BF16),
        scratch_shapes=[pltpu.VMEM((tq + WINDOW, kvd), BF16), pltpu.VMEM((2, ATT_KV_HEADS, 2 * WINDOW, group * WINDOW), F32)],
        compiler_params=_params(("arbitrary",)),
    )(q, kv, kv, sinks)


def _attn_bwd(q, kv, o, do, sinks):
    s, d = q.shape
    kvd = kv.shape[1]
    half = kvd // 2
    hd = ATT_HEAD_DIM
    nq = d // hd
    group = nq // ATT_KV_HEADS
    tq = min(s, ATT_TILE)
    per = tq // WINDOW
    nt = s // tq

    def body(q_ref, kvc_ref, kvp_ref, o_ref, do_ref, sink_ref, dq_ref, dkvc_ref, dkvp_ref, ds_ref, band, dband, bias_ref):
        i = pl.program_id(0)
        _fill_attn_bias(bias_ref, group, nq)
        band[0:WINDOW, :] = kvp_ref[...]
        band[WINDOW:, :] = kvc_ref[...]
        dband[...] = jnp.zeros_like(dband)
        ds_ref[...] = jnp.zeros_like(ds_ref)

        def block(b, carry):
            rows = pl.ds(pl.multiple_of(b * WINDOW, WINDOW), WINDOW)
            keys = pl.ds(pl.multiple_of(b * WINDOW, WINDOW), 2 * WINDOW)
            first = (i * per + b) == 0
            dks, dvs = [], []
            for g in range(ATT_KV_HEADS):
                kb = band[keys, g * hd:(g + 1) * hd] * ATT_SCALE
                vb = band[keys, half + g * hd:half + (g + 1) * hd]
                qs = _stack_heads(q_ref, rows, g * group, group)
                dos = _stack_heads(do_ref, rows, g * group, group)
                p, ps = _attn_probs_t(kb, qs, sink_ref, g * group, group, jnp.where(first, bias_ref[1, g], bias_ref[0, g]))
                prod = dos.astype(F32) * _stack_heads(o_ref, rows, g * group, group).astype(F32)
                dsum = lax.dot_general(jnp.ones((8, hd), F32), prod, NT, precision=lax.Precision.HIGHEST,
                                       preferred_element_type=F32)[0:1, :]
                dsc = p * (_dot(vb, dos, NT) - dsum)
                dvs.append(_dot(p, dos))
                dks.append(_dot(dsc, qs * ATT_SCALE))
                _unstack_heads(_dot(kb, dsc, TN).T, dq_ref, rows, g * group, group)
                gone = ps * dsum
                for j in range(group):
                    ds_ref[g * group + j:g * group + j + 1, :] += jnp.zeros((1, 128), F32) - jnp.sum(gone[:, j * WINDOW:(j + 1) * WINDOW])
            dband[keys, 0:half] += jnp.concatenate(dks, axis=1)
            dband[keys, half:] += jnp.concatenate(dvs, axis=1)
            return carry

        lax.fori_loop(0, per, block, 0)
        dkvp_ref[...] = dband[0:WINDOW, :]
        dkvc_ref[...] = dband[WINDOW:, :]

    big = pl.BlockSpec((tq, d), lambda i: (i, 0))
    return pl.pallas_call(
        body, name="attn_bwd", grid=(nt,),
        in_specs=_attn_specs(s, d, kvd, tq) + [big, big, pl.BlockSpec(memory_space=pltpu.SMEM)],
        out_specs=[big, pl.BlockSpec((tq, kvd), lambda i: (i, 0)), pl.BlockSpec((None, WINDOW, kvd), lambda i: (i, 0, 0)),
                   pl.BlockSpec((None, nq, 128), lambda i: (i, 0, 0))],
        out_shape=[jax.ShapeDtypeStruct((s, d), BF16), jax.ShapeDtypeStruct((s, kvd), F32), jax.ShapeDtypeStruct((nt, WINDOW, kvd), F32),
                   jax.ShapeDtypeStruct((nt, nq, 128), F32)],
        scratch_shapes=[pltpu.VMEM((tq + WINDOW, kvd), BF16), pltpu.VMEM((tq + WINDOW, kvd), F32),
                        pltpu.VMEM((2, ATT_KV_HEADS, 2 * WINDOW, group * WINDOW), F32)],
        compiler_params=_params(("arbitrary",)),
    )(q, kv, kv, o, do, sinks)


HBM_SPEC = pl.BlockSpec(memory_space=pltpu.HBM)
VMEM_SPEC = pl.BlockSpec(memory_space=pltpu.VMEM)


def _place():
    return lax.axis_index("x"), lax.axis_index("y"), lax.axis_index("c")


def _flip(pos, r):
    return tuple(1 - p if (r >> (2 - a)) & 1 else p for a, p in enumerate(pos))


def _index(pos):
    return 4 * pos[0] + 2 * pos[1] + pos[2]


def _all_gather(name, shards, spec):
    n = len(shards)

    def body(*refs):
        x_refs, o_refs = refs[:n], refs[n:2 * n]
        send_sems, recv_sems, local_sems = refs[2 * n:]
        me = _place()
        sibling = _flip(me, 1)
        far = [_flip(me, r) for r in (4, 2, 6)]

        def copy(t, sem, block, to, src=None):
            rows = o_refs[t].at[_index(block)]
            return pltpu.make_async_remote_copy(
                src_ref=rows if src is None else src, dst_ref=rows, send_sem=send_sems.at[t, sem], recv_sem=recv_sems.at[t, sem],
                device_id=to, device_id_type=MESH)

        own = [pltpu.make_async_copy(x_refs[t], o_refs[t].at[_index(me)], local_sems.at[t]) for t in range(n)]
        for cp in own:
            cp.start()
        first = []
        for t in range(n):
            first.append(copy(t, 0, me, sibling, src=x_refs[t]))
            first += [copy(t, 1 + j, me, peer, src=x_refs[t]) for j, peer in enumerate(far)]
        for cp in first:
            cp.start()
        passed = []
        for j, peer in enumerate(far):
            for t in range(n):
                copy(t, 1 + j, peer, me).wait_recv()
                cp = copy(t, 4 + j, peer, sibling)
                cp.start()
                passed.append(cp)
        for t in range(n):
            copy(t, 0, sibling, me).wait_recv()
            for j, peer in enumerate(far):
                copy(t, 4 + j, _flip(peer, 1), me).wait_recv()
        for cp in first + passed:
            cp.wait_send()
        for cp in own:
            cp.wait()

    return pl.pallas_call(
        body, name=name, in_specs=[spec] * n, out_specs=[spec] * n,
        out_shape=[jax.ShapeDtypeStruct((N_DEV,) + sh.shape, sh.dtype) for sh in shards],
        scratch_shapes=[pltpu.SemaphoreType.DMA((n, 7)), pltpu.SemaphoreType.DMA((n, 7)), pltpu.SemaphoreType.DMA((n,))],
    )(*shards)


SEM_SPEC = pl.BlockSpec(memory_space=pltpu.SEMAPHORE)
ANY_SPEC = pl.BlockSpec(memory_space=pl.ANY)


def _landing(own, mine):
    return lax.dynamic_update_slice(lax.empty((N_DEV,) + own.shape, own.dtype), own[None], (mine,) + (0,) * own.ndim)


def _peer_copies(src_refs, land_refs, send_sems, recv_sems, scatter, arrivals):
    me = _place()
    mine = _index(me)
    copies = []
    for t, (src, land) in enumerate(zip(src_refs, land_refs)):
        for r in range(1, N_DEV):
            peer = _flip(me, r)
            theirs = _index(peer)
            sem = t * N_DEV + r - 1
            copies.append(pltpu.make_async_remote_copy(
                src_ref=src.at[theirs] if scatter else src, dst_ref=land.at[theirs if arrivals else mine],
                send_sem=send_sems.at[sem], recv_sem=recv_sems.at[sem], device_id=peer, device_id_type=MESH))
    return copies


def _own_copies(src_refs, land_refs, send_sems):
    mine = _index(_place())
    return [pltpu.make_async_copy(src.at[mine], land.at[mine], send_sems.at[t * N_DEV + N_DEV - 1])
            for t, (src, land) in enumerate(zip(src_refs, land_refs))]


def _send_start(name, sources, lands, scatter, after=None, carry=None):
    n = len(sources)
    extra = [a for a in (after, carry) if a is not None]
    token = jax.ShapeDtypeStruct((8, 128), F32) if carry is None else jax.ShapeDtypeStruct(carry.shape, carry.dtype)

    def body(*refs):
        outs = refs[2 * n + len(extra):]
        for out in _peer_copies(refs[:n], refs[n:2 * n], outs[0], outs[1], scatter, False) + (_own_copies(refs[:n], refs[n:2 * n], outs[0]) if scatter else []):
            out.start()
        outs[-1][...] = jnp.zeros_like(outs[-1]) if carry is None else refs[2 * n + len(extra) - 1][...]

    outs = pl.pallas_call(
        body, name=name, in_specs=[HBM_SPEC] * (2 * n) + [ANY_SPEC] * (after is not None) + [VMEM_SPEC] * (carry is not None),
        out_specs=[SEM_SPEC, SEM_SPEC] + [HBM_SPEC] * (2 * n) + [VMEM_SPEC],
        out_shape=[pltpu.SemaphoreType.DMA((n * N_DEV,)), pltpu.SemaphoreType.DMA((n * N_DEV,))]
        + [pltpu.HBM(a.shape, a.dtype) for a in list(sources) + list(lands)] + [token],
        input_output_aliases={i: 2 + i for i in range(2 * n)},
        compiler_params=pltpu.CompilerParams(has_side_effects=pltpu.SideEffectType.DATAFLOW_SIDE_EFFECTING),
    )(*[pltpu.with_memory_space_constraint(a, pltpu.HBM) for a in list(sources) + list(lands)], *extra)
    return outs[0], outs[1], outs[2:2 + n], outs[2 + n:2 + 2 * n], outs[-1]


def _send_wait(name, started, after, scatter):
    send_sems, recv_sems, sources, lands, _ = started
    n = len(sources)

    def body(*refs):
        for out in _peer_copies(refs[:n], refs[n:2 * n], refs[2 * n], refs[2 * n + 1], scatter, False):
            out.wait_send()
        for own in _own_copies(refs[:n], refs[n:2 * n], refs[2 * n]) if scatter else []:
            own.wait()
        for arrival in _peer_copies(refs[:n], refs[n:2 * n], refs[2 * n], refs[2 * n + 1], scatter, True):
            arrival.wait_recv()

    outs = pl.pallas_call(
        body, name=name, in_specs=[HBM_SPEC] * (2 * n) + [SEM_SPEC, SEM_SPEC, ANY_SPEC], out_specs=[HBM_SPEC] * (2 * n),
        out_shape=[pltpu.HBM(a.shape, a.dtype) for a in list(sources) + list(lands)],
        input_output_aliases={i: i for i in range(2 * n)},
        compiler_params=pltpu.CompilerParams(has_side_effects=pltpu.SideEffectType.DATAFLOW_SIDE_EFFECTING),
    )(*sources, *lands, send_sems, recv_sems, after)
    return outs[n:]


def _pack_rows(parts):
    offsets, row = [], 0
    for part in parts:
        offsets.append(row)
        row += part.shape[0]
    return offsets, -(-row // 8) * 8, -(-max(part.shape[1] for part in parts) // 128) * 128


def _pack(name, parts):
    offsets, rows, width = _pack_rows(parts)

    def body(*refs):
        o_ref = refs[-1]
        o_ref[...] = jnp.zeros_like(o_ref)
        for off, ref in zip(offsets, refs[:-1]):
            o_ref[off:off + ref.shape[0], 0:ref.shape[1]] = ref[...]

    return pl.pallas_call(body, name=name, in_specs=[VMEM_SPEC] * len(parts), out_specs=VMEM_SPEC,
                          out_shape=jax.ShapeDtypeStruct((rows, width), F32))(*parts)


def _adamw_math(w, g, m, v):
    m = ADAM_B1 * m + (1.0 - ADAM_B1) * g
    v = ADAM_B2 * v + (1.0 - ADAM_B2) * (g * g)
    m_hat = m * (1.0 / (1.0 - ADAM_B1 ** ADAM_STEP))
    denom = jnp.sqrt(v * (1.0 / (1.0 - ADAM_B2 ** ADAM_STEP))) + ADAM_EPS
    inv = pl.reciprocal(denom, approx=True)
    inv = inv * (2.0 - denom * inv)
    return -ADAM_LR * (m_hat * inv + ADAM_WD * w), m, v


def _adamw_step(w_ref, m_ref, v_ref, p_ref, g_ref, d_ref, nm_ref, nv_ref):
    g = p_ref[0].astype(F32)
    for dev in range(1, N_DEV):
        g = g + p_ref[dev].astype(F32)
    g_ref[...] = g
    d_ref[...], nm_ref[...], nv_ref[...] = _adamw_math(w_ref[...], g, m_ref[...], v_ref[...])


def _adamw_rows(rows):
    return max(t for t in range(8, min(rows, 256) + 1, 8) if rows % t == 0)


def _adamw_shard(name, w, m, v, partials):
    rows, cols = w.shape
    tr = _adamw_rows(rows)
    blk = pl.BlockSpec((tr, cols), lambda i: (i, 0))
    return pl.pallas_call(
        _adamw_step_fn(), name=name, grid=(rows // tr,), in_specs=[blk, blk, blk, pl.BlockSpec((N_DEV, tr, cols), lambda i: (0, i, 0))],
        out_specs=[blk] * 4, out_shape=[jax.ShapeDtypeStruct((rows, cols), F32)] * 4, compiler_params=_params(("parallel",)),
    )(w, m, v, partials)


def _adamw_step_fn():
    return functools.partial(_adamw_step)


def _adamw_layers(name, w, m, v, partials):
    layers, rows, cols = w.shape
    tr = _adamw_rows(rows)
    last = rows // tr - 1

    def body(w_ref, m_ref, v_ref, *rest):
        for layer in range(layers):
            @pl.when(pl.program_id(0) == layer)
            def _():
                _adamw_step(w_ref, m_ref, v_ref, rest[layer], *rest[layers:])

    blk = pl.BlockSpec((None, tr, cols), lambda l, i: (l, i, 0))
    part = lambda layer: pl.BlockSpec((N_DEV, tr, cols), lambda l, i: (0, jnp.where(l == layer, i, jnp.where(l < layer, 0, last)), 0))
    return pl.pallas_call(
        body, name=name, grid=(layers, rows // tr), in_specs=[blk, blk, blk] + [part(layer) for layer in range(layers)],
        out_specs=[blk] * 4, out_shape=[jax.ShapeDtypeStruct(w.shape, F32)] * 4, compiler_params=_params(("arbitrary", "arbitrary")),
    )(w, m, v, *partials)


def _adamw_small(gathered, places, entries):
    n = len(entries)
    np_ = len(gathered)

    def body(*refs):
        pack_refs = refs[:np_]
        refs = refs[np_ - 1:]
        w_refs, m_refs, v_refs = refs[1:1 + n], refs[1 + n:1 + 2 * n], refs[1 + 2 * n:1 + 3 * n]
        outs = refs[1 + 3 * n:]
        totals = []
        for pack_ref in pack_refs:
            acc = pack_ref[0]
            for dev in range(1, N_DEV):
                acc = acc + pack_ref[dev]
            totals.append(acc)
        mine = _index(_place())
        for e in range(n):
            rows, cols = w_refs[e].shape
            total, off = totals[places[e][0]], places[e][1]
            if entries[e][3]:
                g = jnp.zeros((rows, cols), F32)
                for dev in range(N_DEV):
                    g = g + jnp.where(mine == dev, total[off + dev * rows:off + (dev + 1) * rows, 0:cols], 0.0)
            else:
                g = total[off:off + rows, 0:cols]
            outs[4 * e][...] = g
            outs[4 * e + 1][...], outs[4 * e + 2][...], outs[4 * e + 3][...] = _adamw_math(w_refs[e][...], g, m_refs[e][...], v_refs[e][...])
        outs[4 * n][...] = totals[places[n][0]][places[n][1]:places[n][1] + 1, 0:128]

    shapes = []
    for w, _, _, _ in entries:
        shapes += [jax.ShapeDtypeStruct(w.shape, F32)] * 4
    shapes.append(jax.ShapeDtypeStruct((1, 128), F32))
    return pl.pallas_call(
        body, name="adamw_small", in_specs=[VMEM_SPEC] * (np_ + 3 * n), out_specs=[VMEM_SPEC] * len(shapes), out_shape=shapes,
        compiler_params=pltpu.CompilerParams(vmem_limit_bytes=VMEM_LIMIT),
    )(*gathered, *[e[0] for e in entries], *[e[1] for e in entries], *[e[2] for e in entries])


def _ffn_forward(tag, h, gain, w_up, late):
    s, d = h.shape
    fb = w_up.shape[1]
    tm = _row_tile(s, 2 * MM_ROWS)
    (a,), (u,) = _norm_proj(f"ffn_up_{tag}", h, [
        (gain, w_up, pl.BlockSpec((None, fb, d), lambda i, j: (j, 0, 0)), NT,
         pl.BlockSpec((None, None, tm, fb), lambda i, j: (j // 4, j % 4, i, 0)), jax.ShapeDtypeStruct((2, 4, s, fb), BF16))], tm=tm, nj=N_DEV)
    w_down, conv_w, conv_b = late(u)
    hidden, conv, out = _ffn_hidden_down(f"ffn_hidden_down_{tag}", u, conv_w, conv_b, w_down, h)
    return out, (a, u, hidden, conv)


def _ffn_backward(tag, h, gain, w_up, w_down, conv_w, conv_b, saved, dout, sent=None):
    a, u, hidden, conv = saved
    dout, dout_bf = dout
    s, d = h.shape
    fb = w_up.shape[1]
    tm = _row_tile(s, MM_ROWS)
    dhidden = _matmul(
        f"ffn_down_bwd_{tag}", dout_bf, w_down, dims=NT, grid=(s // tm, 4, 1),
        a_spec=pl.BlockSpec((tm, d), lambda i, j, k: (i, 0)),
        b_spec=pl.BlockSpec((None, fb, d), lambda i, j, k: (j, 0, 0)),
        o_spec=pl.BlockSpec((None, tm, fb), lambda i, j, k: (j, i, 0)),
        out_shape=jax.ShapeDtypeStruct((4, s, fb), BF16))
    dw_down = _matmul(
        f"ffn_down_grad_{tag}", hidden, dout_bf, dims=TN, grid=(4, 1, 1),
        a_spec=pl.BlockSpec((None, s, fb), lambda i, j, k: (i, 0, 0)),
        b_spec=pl.BlockSpec((s, d), lambda i, j, k: (0, 0)),
        o_spec=pl.BlockSpec((None, fb, d), lambda i, j, k: (i, 0, 0)),
        out_shape=jax.ShapeDtypeStruct((4, fb, d), BF16))
    if sent is not None:
        gain = sent(dw_down, gain)
    du, dconv_w, dconv_b, dh, dh_bf, dgain = _ffn_hidden_up_bwd(f"ffn_hidden_up_bwd_{tag}", u, conv, dhidden, conv_w, w_up, h, gain, dout)
    dw_up = _matmul(
        f"ffn_up_grad_{tag}", du, a, dims=TN, grid=(N_DEV, 1, 1),
        a_spec=pl.BlockSpec((None, None, s, fb), lambda i, j, k: (i // 4, i % 4, 0, 0)),
        b_spec=pl.BlockSpec((s, d), lambda i, j, k: (0, 0)),
        o_spec=pl.BlockSpec((None, fb, d), lambda i, j, k: (i, 0, 0)),
        out_shape=jax.ShapeDtypeStruct((N_DEV, fb, d), BF16))
    return (dh, dh_bf), dgain, dw_up, dw_down, dconv_w, dconv_b


def kernel(x, hg_norm, hg_w_in, hg_lb_logits, hg_out_norm, hg_w_out, kv_norm, w_kv, attn_norm, attn_w_q, attn_sinks, attn_w_o, ffn_norm, ffn_w_up, ffn_conv_w, ffn_conv_b, ffn_w_down, final_norm, loss_target, m_hg_norm, m_hg_w_in, m_hg_lb_logits, m_hg_out_norm, m_hg_w_out, m_kv_norm, m_w_kv, m_attn_norm, m_attn_w_q, m_attn_sinks, m_attn_w_o, m_ffn_norm, m_ffn_w_up, m_ffn_conv_w, m_ffn_conv_b, m_ffn_w_down, m_final_norm, v_hg_norm, v_hg_w_in, v_hg_lb_logits, v_hg_out_norm, v_hg_w_out, v_kv_norm, v_w_kv, v_attn_norm, v_attn_w_q, v_attn_sinks, v_attn_w_o, v_ffn_norm, v_ffn_w_up, v_ffn_conv_w, v_ffn_conv_b, v_ffn_w_down, v_final_norm):
    _, s, d = x.shape
    x0, target = x[0], loss_target[0]
    half = hg_w_in.shape[2]
    fs = ffn_conv_w.shape[2]
    fb = 2 * fs
    kvd = w_kv.shape[1]
    nq = d // ATT_HEAD_DIM
    tm = _row_tile(s, MM_ROWS)

    mine = _index(_place())
    gather = lambda tag, shards, after, carry=None: _send_start("gather_start_" + tag, shards, [_landing(a, mine) for a in shards], False, after, carry)
    w_in, g_hgn, g_lbl, w_out = _all_gather("gather_hg", [hg_w_in[0].astype(BF16), hg_norm, hg_lb_logits, hg_w_out[0].astype(BF16)], HBM_SPEC)
    w_out = w_out.reshape(d, d)
    up_t = lambda a: jnp.swapaxes(a, -1, -2)
    coming_up0 = gather("ffn_up0", [up_t(ffn_w_up[0]).astype(BF16)], None, g_hgn.reshape(1, d))
    hgn = coming_up0[4]
    lbl = g_lbl.transpose(1, 0, 2).reshape(2, d)
    conv_b = [ffn_conv_b[layer].reshape(4, 1, fb) for layer in range(2)]
    gains = [ffn_norm[0:1], ffn_norm[1:2]]
    kvn, fin = kv_norm.reshape(1, d), final_norm.reshape(1, d)

    t2 = _row_tile(s, 2 * MM_ROWS)
    (a0,), (p,) = _norm_proj("hg_in", x0, [
        (hgn, w_in, pl.BlockSpec((None, d, half), lambda i, j: (j, 0, 0)), NN,
         pl.BlockSpec((None, t2, half), lambda i, j: (j // 2, i, j % 2)), jax.ShapeDtypeStruct((4, s, d), BF16))], tm=t2, nj=N_DEV)
    o, og, states, gsum = _hgrn2_fwd(p, lbl, hg_out_norm)
    coming_dn0 = gather("ffn_down0", [ffn_conv_w, ffn_w_down[0].astype(BF16)], o)
    x1 = _mm_rows("hg_out", og, w_out, out_dtype=F32, add=x0, after=coming_dn0[4])
    w_up0, = _send_wait("gather_wait_ffn_up0", coming_up0, x1, False)
    coming_attn = gather("attn", [w_kv.astype(BF16), attn_w_q[0].astype(BF16), attn_w_o[0].astype(BF16)], w_up0, gains[0])
    gains[0] = coming_attn[4]
    w_up, w_dn, conv_w, coming = [w_up0, None], [None, None], [], {}

    def late0(u):
        g_cw, w_dn0 = _send_wait("gather_wait_ffn_down0", coming_dn0, u, False)
        w_dn[0] = w_dn0.reshape(4, fb, d)
        conv_w.extend(g_cw[:, layer].reshape(4, 2, CONV_WIDTH, fs).transpose(0, 2, 1, 3).reshape(4, CONV_WIDTH, fb) for layer in range(2))
        coming["up1"] = gather("ffn_up1", [up_t(ffn_w_up[1]).astype(BF16)], w_dn0, conv_b[0])
        return w_dn[0], conv_w[0], coming["up1"][4]

    x2, saved0 = _ffn_forward("0", x1, gains[0], w_up[0], late0)
    w_kvg, w_q, w_o = _send_wait("gather_wait_attn", coming_attn, x2, False)
    w_kvg, w_q, w_o = w_kvg.reshape(d, kvd), w_q.reshape(d, d), w_o.reshape(d, d)
    (akv, a2), (kv, q) = _norm_proj("attn_in", x2, [
        (kvn, w_kvg, pl.BlockSpec((d, kvd), lambda i, j: (0, 0)), NN, pl.BlockSpec((tm, kvd), lambda i, j: (i, 0)), jax.ShapeDtypeStruct((s, kvd), BF16)),
        (attn_norm, w_q, pl.BlockSpec((d, d), lambda i, j: (0, 0)), NN, pl.BlockSpec((tm, d), lambda i, j: (i, 0)), jax.ShapeDtypeStruct((s, d), BF16))],
        tm=tm, nj=1)
    coming_dn1 = gather("ffn_down1", [ffn_w_down[1].astype(BF16)], q, attn_sinks)
    att = _attn_fwd(q, kv, coming_dn1[4])
    x3 = _mm_rows("attn_out", att, w_o, out_dtype=F32, add=x2)
    w_up[1], = _send_wait("gather_wait_ffn_up1", coming["up1"], x3, False)

    def late1(u):
        w_dn[1] = _send_wait("gather_wait_ffn_down1", coming_dn1, u, False)[0].reshape(4, fb, d)
        return w_dn[1], conv_w[1], conv_b[1]

    x4, saved1 = _ffn_forward("1", x3, gains[1], w_up[1], late1)
    dx4, dx4_bf, d_fin, loss_part = _loss_head(x4, fin, target)

    dx3, d_fn1, dw_up1, dw_dn1, dcw1, dcb1 = _ffn_backward("1", x3, gains[1], w_up[1], w_dn[1], conv_w[1], conv_b[1], saved1, (dx4, dx4_bf))
    rows = d // N_DEV
    scatter = lambda tag, stacks, carry: _send_start("scatter_start_" + tag, stacks, [lax.empty(a.shape, a.dtype) for a in stacks], True, None, carry)
    going_ffn1 = scatter("ffn1", [dw_up1, dw_dn1.reshape(N_DEV, fs, d)], attn_sinks)
    datt = _mm_rows_nt("attn_out_bwd", dx3[1], w_o, out_dtype=BF16)
    dw_o = _mm_tn("attn_out_grad", att, dx3[1])
    dq, dkv_own, dkv_before, dsink = _attn_bwd(q, kv, att, datt, going_ffn1[4])
    tiles = dkv_before.shape[0]
    dkv = dkv_own.reshape(tiles, s // tiles, kvd)
    dkv = jnp.concatenate([dkv[:, :-WINDOW], dkv[:, -WINDOW:] + jnp.pad(dkv_before[1:], ((0, 1), (0, 0), (0, 0)))], axis=1).reshape(s, kvd)
    dw_q = _mm_tn("q_proj_grad", a2, dq)
    dw_kv = _mm_tn("kv_proj_grad", akv, dkv)
    going_attn = scatter("attn", [dw_kv.reshape(N_DEV, rows, kvd), dw_q.reshape(N_DEV, rows, d), dw_o.reshape(N_DEV, rows, d)], kvn)
    whole = lambda a_ref, b_ref: [(a_ref[...], b_ref[...])]
    rows_of = lambda width: (lambda tile: pl.BlockSpec((tile, width), lambda i: (i, 0)))
    dx2, (d_kvn, d_attn) = _proj_norm_bwd("attn_in_bwd", x2, dx3[0], [
        (dkv, rows_of(kvd), w_kvg, pl.BlockSpec((d, kvd), lambda i: (0, 0)), whole, going_attn[4]),
        (dq, rows_of(d), w_q, pl.BlockSpec((d, d), lambda i: (0, 0)), whole, attn_norm)])
    going = {}

    def sent0(dw_dn0, gain):
        going["ffn_dn0"] = scatter("ffn_dn0", [dw_dn0.reshape(N_DEV, fs, d)], gain)
        return going["ffn_dn0"][4]

    dx1, d_fn0, dw_up0, _, dcw0, dcb0 = _ffn_backward("0", x1, gains[0], w_up[0], w_dn[0], conv_w[0], conv_b[0], saved0, dx2, sent0)
    dw_out = _mm_tn("hg_out_grad", og, dx1[1])
    going_ffn0 = scatter("ffn0", [dw_up0, dw_out.reshape(N_DEV, rows, d)], hg_out_norm)
    dog = _mm_rows_nt("hg_out_bwd", dx1[1], w_out, out_dtype=F32)
    dp, d_lbl, d_ogain = _hgrn2_bwd(p, lbl, going_ffn0[4], o, dog, states, gsum)
    dw_in = _matmul(
        "hg_in_grad", a0, dp, dims=TN, grid=(1, N_DEV, 1),
        a_spec=pl.BlockSpec((s, d), lambda i, j, k: (0, 0)),
        b_spec=pl.BlockSpec((None, s, half), lambda i, j, k: (j // 2, 0, j % 2)),
        o_spec=pl.BlockSpec((None, d, half), lambda i, j, k: (j, 0, 0)),
        out_shape=jax.ShapeDtypeStruct((N_DEV, d, half), BF16))
    going_hg = scatter("hg", [dw_in], hgn)
    (dx0, _), (d_hgn,) = _proj_norm_bwd("hg_in_bwd", x0, dx1[0], [
        (dp, lambda tile: pl.BlockSpec((4, tile, d), lambda i: (0, i, 0)), w_in, pl.BlockSpec((N_DEV, d, half), lambda i: (0, 0, 0)),
         lambda g_ref, w_ref: [(g_ref[k // 2, :, (k % 2) * half:(k % 2 + 1) * half], w_ref[k]) for k in range(N_DEV)],
         going_hg[4])])

    as_blocks = lambda a, r: a.reshape(r, N_DEV, -1).transpose(1, 0, 2).reshape(N_DEV * r, -1)
    d_cw = jnp.concatenate([g.transpose(1, 0, 2).reshape(CONV_WIDTH, 4 * fb) for g in (dcw0, dcw1)], axis=0)
    parts = [d_fin, jnp.concatenate([d_fn0, d_fn1], axis=0), jnp.concatenate([dcb0.reshape(1, 4 * fb), dcb1.reshape(1, 4 * fb)], axis=0),
             as_blocks(d_cw, 2 * CONV_WIDTH), d_attn, jnp.sum(dsink[:, :, 0], axis=0).reshape(1, nq), d_kvn, d_ogain,
             as_blocks(d_hgn, 1), as_blocks(d_lbl, 2), loss_part]
    wide = [2]
    packs = [[parts[i] for i in wide], [part for i, part in enumerate(parts) if i not in wide]]
    places = [None] * len(parts)
    for which, members in enumerate([wide, [i for i in range(len(parts)) if i not in wide]]):
        for i, off in zip(members, _pack_rows(packs[which])[0]):
            places[i] = (which, off)
    packed = [_pack("pack_wide_grads", packs[0]), _pack("pack_narrow_grads", packs[1])]
    going_small = _send_start("small_grads_start", packed, [_landing(a, mine) for a in packed], False)

    arrive = lambda tag, going, after: _send_wait("scatter_wait_" + tag, going, after, True)
    (l_up1, l_dn1), (l_kv, l_q, l_o), (l_dn0,), (l_up0, l_out) = (
        arrive("ffn1", going_ffn1, going_small[4]), arrive("attn", going_attn, going_small[4]),
        arrive("ffn_dn0", going["ffn_dn0"], going_small[4]), arrive("ffn0", going_ffn0, going_small[4]))
    big = {}
    for tag, w, m, v, part in [
            ("w_kv", w_kv, m_w_kv, v_w_kv, l_kv), ("attn_w_q", attn_w_q[0], m_attn_w_q[0], v_attn_w_q[0], l_q),
            ("attn_w_o", attn_w_o[0], m_attn_w_o[0], v_attn_w_o[0], l_o)]:
        big[tag] = _adamw_shard("adamw_" + tag, w, m, v, part)
    up_done = _adamw_layers("adamw_ffn_w_up", up_t(ffn_w_up), up_t(m_ffn_w_up), up_t(v_ffn_w_up), (l_up0, l_up1))
    big["ffn_w_up"] = [up_t(a) for a in up_done]
    big["ffn_w_down"] = _adamw_layers("adamw_ffn_w_down", ffn_w_down, m_ffn_w_down, v_ffn_w_down, (l_dn0, l_dn1))
    lead = lambda tag: [a[None] for a in big[tag]]

    both_done = up_done[0][0, 0:1, 0:1] + big["ffn_w_down"][0][0, 0:1, 0:1]
    gathered = _send_wait("small_grads_wait", going_small, both_done, False)
    two = lambda a: a.reshape(-1, a.shape[-1])
    small = [(fin, m_final_norm.reshape(1, d), v_final_norm.reshape(1, d), False), (ffn_norm, m_ffn_norm, v_ffn_norm, False),
             (ffn_conv_b, m_ffn_conv_b, v_ffn_conv_b, False), (two(ffn_conv_w), two(m_ffn_conv_w), two(v_ffn_conv_w), True),
             (attn_norm, m_attn_norm, v_attn_norm, False), (attn_sinks, m_attn_sinks, v_attn_sinks, False),
             (kvn, m_kv_norm.reshape(1, d), v_kv_norm.reshape(1, d), False), (hg_out_norm, m_hg_out_norm, v_hg_out_norm, False),
             (hg_norm, m_hg_norm, v_hg_norm, True), (hg_lb_logits, m_hg_lb_logits, v_hg_lb_logits, True)]
    res = _adamw_small(gathered, places, small)
    l_in, = arrive("hg", going_hg, gathered[1])
    big["hg_w_in"] = _adamw_shard("adamw_hg_w_in", hg_w_in[0], m_hg_w_in[0], v_hg_w_in[0], l_in)
    big["hg_w_out"] = _adamw_shard("adamw_hg_w_out", hg_w_out[0], m_hg_w_out[0], v_hg_w_out[0], l_out)
    names = ["final_norm", "ffn_norm", "ffn_conv_b", "ffn_conv_w", "attn_norm", "attn_sinks", "kv_norm", "hg_out_norm", "hg_norm", "hg_lb_logits"]
    shapes = {"final_norm": final_norm.shape, "kv_norm": kv_norm.shape, "ffn_conv_w": ffn_conv_w.shape}
    out = {n: [a.reshape(shapes[n]) if n in shapes else a for a in res[4 * i:4 * i + 4]] for i, n in enumerate(names)}
    out.update(hg_w_in=lead("hg_w_in"), hg_w_out=lead("hg_w_out"), w_kv=big["w_kv"], attn_w_q=lead("attn_w_q"), attn_w_o=lead("attn_w_o"),
               ffn_w_up=big["ffn_w_up"], ffn_w_down=big["ffn_w_down"])
    order = ["hg_norm", "hg_w_in", "hg_lb_logits", "hg_out_norm", "hg_w_out", "kv_norm", "w_kv", "attn_norm", "attn_w_q", "attn_sinks",
             "attn_w_o", "ffn_norm", "ffn_w_up", "ffn_conv_w", "ffn_conv_b", "ffn_w_down", "final_norm"]
    loss = res[-1][0, 0]
    return (loss, dx0[None], *[out[n][0] for n in order], *[out[n][1] for n in order], *[out[n][2] for n in order], *[out[n][3] for n in order])
```

```python
import functools

import jax
import jax.numpy as jnp
from jax import lax
from jax.experimental import pallas as pl
from jax.experimental.pallas import tpu as pltpu

F32 = jnp.float32
BF16 = jnp.bfloat16

EPS = 1e-6
HG_EXPAND = 128
HG_CHUNK = 32
ATT_HEAD_DIM = 64
ATT_KV_HEADS = 2
WINDOW = 128
CONV_WIDTH = 3
ADAM_LR = 0.001
ADAM_B1 = 0.9
ADAM_B2 = 0.999
ADAM_EPS = 1e-08
ADAM_WD = 0.01
ADAM_STEP = 10

N_DEV = 8
VMEM_LIMIT = 60 * 1024 * 1024
NEG = -1e30

NN = (((1,), (0,)), ((), ()))
NT = (((1,), (1,)), ((), ()))
TN = (((0,), (0,)), ((), ()))
MESH = pl.DeviceIdType.MESH


def _dot(a, b, dims=NN):
    return lax.dot_general(a.astype(BF16), b.astype(BF16), dims, preferred_element_type=F32)


def _sigmoid(x):
    return 0.5 * jnp.tanh(0.5 * x) + 0.5


def _silu(x):
    return x * _sigmoid(x)


def _silu_and_grad(x):
    s = _sigmoid(x)
    return x * s, s * (1.0 + x * (1.0 - s))


def _dsilu(x):
    return _silu_and_grad(x)[1]


def _params(semantics):
    return pltpu.CompilerParams(dimension_semantics=semantics, vmem_limit_bytes=VMEM_LIMIT)


def _row_tile(rows, want=512):
    return min(rows, want)


MM_ROWS = 1024


def _matmul(name, a, b, *, dims, grid, a_spec, b_spec, o_spec, out_shape, add=None, add_spec=None, after=None):
    assert grid[2] == 1

    def body(*refs):
        a_ref, b_ref, o_ref = refs[0], refs[1], refs[-1]
        total = _dot(a_ref[...], b_ref[...], dims)
        if add is not None:
            total = total + refs[2][...]
        o_ref[...] = total.astype(o_ref.dtype)

    in_specs = [a_spec, b_spec] + ([] if add is None else [add_spec]) + ([] if after is None else [pl.BlockSpec(memory_space=pl.ANY)])
    args = (a, b) + (() if add is None else (add,)) + (() if after is None else (after,))
    return pl.pallas_call(
        body, name=name, grid=grid, in_specs=in_specs, out_specs=o_spec, out_shape=out_shape,
        compiler_params=_params(("parallel", "parallel", "arbitrary")),
    )(*args)


def _mm_rows(name, a, w, *, out_dtype, add=None, after=None):
    s, kdim = a.shape
    n = w.shape[1]
    tm = _row_tile(s, MM_ROWS)
    return _matmul(
        name, a, w, dims=NN, grid=(s // tm, 1, 1),
        a_spec=pl.BlockSpec((tm, kdim), lambda i, j, k: (i, 0)),
        b_spec=pl.BlockSpec((kdim, n), lambda i, j, k: (0, 0)),
        o_spec=pl.BlockSpec((tm, n), lambda i, j, k: (i, 0)),
        out_shape=jax.ShapeDtypeStruct((s, n), out_dtype),
        add=add, add_spec=None if add is None else pl.BlockSpec((tm, n), lambda i, j, k: (i, 0)), after=after,
    )


def _mm_rows_nt(name, a, w, *, out_dtype):
    s, n = a.shape
    kdim = w.shape[0]
    tm = _row_tile(s, MM_ROWS)
    return _matmul(
        name, a, w, dims=NT, grid=(s // tm, 1, 1),
        a_spec=pl.BlockSpec((tm, n), lambda i, j, k: (i, 0)),
        b_spec=pl.BlockSpec((kdim, n), lambda i, j, k: (0, 0)),
        o_spec=pl.BlockSpec((tm, kdim), lambda i, j, k: (i, 0)),
        out_shape=jax.ShapeDtypeStruct((s, kdim), out_dtype),
    )


def _mm_tn(name, a, g):
    s, m = a.shape
    n = g.shape[1]
    tn = min(n, 512)
    return _matmul(
        name, a, g, dims=TN, grid=(1, n // tn, 1),
        a_spec=pl.BlockSpec((s, m), lambda i, j, k: (0, 0)),
        b_spec=pl.BlockSpec((s, tn), lambda i, j, k: (0, j)),
        o_spec=pl.BlockSpec((m, tn), lambda i, j, k: (0, j)),
        out_shape=jax.ShapeDtypeStruct((m, n), BF16),
    )


NORM_ROWS = 256


def _norm_proj(name, h, branches, *, tm, nj):
    s, d = h.shape
    n = len(branches)
    rows = min(tm, NORM_ROWS)

    def body(*refs):
        h_ref, gain_refs, w_refs = refs[0], refs[1:1 + n], refs[1 + n:1 + 2 * n]
        a_refs, o_refs = refs[1 + 2 * n:1 + 3 * n], refs[1 + 3 * n:]

        @pl.when(pl.program_id(1) == 0)
        def _():
            def normalize(c, carry):
                at = pl.ds(pl.multiple_of(c * rows, rows), rows)
                xv = h_ref[at, :]
                xhat = xv * lax.rsqrt(jnp.mean(xv * xv, axis=-1, keepdims=True) + EPS)
                for gain_ref, a_ref in zip(gain_refs, a_refs):
                    a_ref[at, :] = (xhat * gain_ref[...]).astype(BF16)
                return carry

            lax.fori_loop(0, tm // rows, normalize, 0)

        for branch, w_ref, a_ref, o_ref in zip(branches, w_refs, a_refs, o_refs):
            o_ref[...] = _dot(a_ref[...], w_ref[...], branch[3]).astype(o_ref.dtype)

    row = pl.BlockSpec((tm, d), lambda i, j: (i, 0))
    vec = pl.BlockSpec((1, d), lambda i, j: (0, 0))
    outs = pl.pallas_call(
        body, name=name, grid=(s // tm, nj), in_specs=[row] + [vec] * n + [b[2] for b in branches],
        out_specs=[row] * n + [b[4] for b in branches],
        out_shape=[jax.ShapeDtypeStruct((s, d), BF16)] * n + [b[5] for b in branches],
        compiler_params=_params(("parallel", "arbitrary")),
    )(h, *[b[0] for b in branches], *[b[1] for b in branches])
    return outs[:n], outs[n:]


def _proj_norm_bwd(name, h, dres, branches):
    s, d = h.shape
    tm = _row_tile(s)
    n = len(branches)

    def body(*refs):
        h_ref, dres_ref = refs[0], refs[1]
        g_refs, w_refs, gain_refs = refs[2:2 + n], refs[2 + n:2 + 2 * n], refs[2 + 2 * n:2 + 3 * n]
        dh_ref, dhb_ref, dg_refs = refs[2 + 3 * n], refs[3 + 3 * n], refs[4 + 3 * n:]
        i = pl.program_id(0)
        xv = h_ref[...]
        r = lax.rsqrt(jnp.mean(xv * xv, axis=-1, keepdims=True) + EPS)
        xhat = xv * r
        total = dres_ref[...]
        for branch, g_ref, w_ref, gain_ref, dg_ref in zip(branches, g_refs, w_refs, gain_refs, dg_refs):
            pairs = branch[4](g_ref, w_ref)
            da = _dot(*pairs[0], NT)
            for pair in pairs[1:]:
                da = da + _dot(*pair, NT)
            dgain = jnp.sum(da * xhat, axis=0, keepdims=True)

            @pl.when(i == 0)
            def _():
                dg_ref[...] = dgain

            @pl.when(i > 0)
            def _():
                dg_ref[...] += dgain

            dxhat = da * gain_ref[...]
            total = total + r * (dxhat - xhat * jnp.mean(dxhat * xhat, axis=-1, keepdims=True))
        dh_ref[...] = total
        dhb_ref[...] = total.astype(BF16)

    row = pl.BlockSpec((tm, d), lambda i: (i, 0))
    vec = pl.BlockSpec((1, d), lambda i: (0, 0))
    outs = pl.pallas_call(
        body, name=name, grid=(s // tm,),
        in_specs=[row, row] + [b[1](tm) for b in branches] + [b[3] for b in branches] + [vec] * n, out_specs=[row, row] + [vec] * n,
        out_shape=[jax.ShapeDtypeStruct((s, d), F32), jax.ShapeDtypeStruct((s, d), BF16)] + [jax.ShapeDtypeStruct((1, d), F32)] * n,
        compiler_params=_params(("arbitrary",)),
    )(h, dres, *[b[0] for b in branches], *[b[2] for b in branches], *[b[5] for b in branches])
    return (outs[0], outs[1]), outs[2:]


def _loss_head(h, gain, target):
    s, d = h.shape
    tm = _row_tile(s)

    def body(h_ref, g_ref, t_ref, dh_ref, dhb_ref, dg_ref, loss_ref):
        i = pl.program_id(0)
        xv = h_ref[...]
        r = lax.rsqrt(jnp.mean(xv * xv, axis=-1, keepdims=True) + EPS)
        xhat = xv * r
        err = xhat * g_ref[...] - t_ref[...]
        dy = err * (1.0 / d)
        part = jnp.zeros((1, 128), F32) + 0.5 * jnp.sum(jnp.mean(err * err, axis=-1, keepdims=True))
        dgain = jnp.sum(dy * xhat, axis=0, keepdims=True)

        @pl.when(i == 0)
        def _():
            dg_ref[...] = dgain
            loss_ref[...] = part

        @pl.when(i > 0)
        def _():
            dg_ref[...] += dgain
            loss_ref[...] += part

        dxhat = dy * g_ref[...]
        dh = r * (dxhat - xhat * jnp.mean(dxhat * xhat, axis=-1, keepdims=True))
        dh_ref[...] = dh
        dhb_ref[...] = dh.astype(BF16)

    row = pl.BlockSpec((tm, d), lambda i: (i, 0))
    vec = pl.BlockSpec((1, d), lambda i: (0, 0))
    return pl.pallas_call(
        body, name="loss_head", grid=(s // tm,), in_specs=[row, vec, row],
        out_specs=[row, row, vec, pl.BlockSpec((1, 128), lambda i: (0, 0))],
        out_shape=[jax.ShapeDtypeStruct((s, d), F32), jax.ShapeDtypeStruct((s, d), BF16), jax.ShapeDtypeStruct((1, d), F32),
                   jax.ShapeDtypeStruct((1, 128), F32)],
        compiler_params=_params(("arbitrary",)),
    )(h, gain, target)


def _bdot(a, b, ca, cb):
    return lax.dot_general(a.astype(BF16), b.astype(BF16), (((ca,), (cb,)), ((0,), (0,))), preferred_element_type=F32)


def _chunk_cumsum(xv, reverse=False):
    n = xv.shape[0]
    row = lax.broadcasted_iota(jnp.int32, xv.shape, 0) % HG_CHUNK
    step = 1
    while step < HG_CHUNK:
        if reverse:
            xv = xv + jnp.where(row < HG_CHUNK - step, pltpu.roll(xv, n - step, axis=0), 0.0)
        else:
            xv = xv + jnp.where(row >= step, pltpu.roll(xv, step, axis=0), 0.0)
        step *= 2
    return xv


def _hg_terms(p_ref, lbl_ref, g_ref=None):
    pq = p_ref[0].astype(F32)
    pf = p_ref[1].astype(F32)
    lb = _sigmoid(lbl_ref[0:1, :] - lbl_ref[1:2, :])
    sig = _sigmoid(pf)
    fg = lb + (1.0 - lb) * sig
    nc = pq.shape[0] // HG_CHUNK
    chunks = lambda a: a.reshape(nc, HG_CHUNK, HG_EXPAND)
    q = chunks(_silu(pq) * HG_EXPAND ** -0.5)
    k = chunks(1.0 - fg)
    v = chunks(p_ref[2].astype(F32))
    g = chunks(_chunk_cumsum(jnp.log(fg)) if g_ref is None else g_ref[...])
    gm = g[:, HG_CHUNK // 2 - 1:HG_CHUNK // 2, :]
    gl = g[:, HG_CHUNK - 1:HG_CHUNK, :]
    e_mid, e_inv, e_all, e_end = jnp.exp(g - gm), jnp.exp(gm - g), jnp.exp(g), jnp.exp(gl - g)
    terms = dict(q=q, k=k, v=v, g=g, qd=q * e_all, qt=q * e_mid, kt=k * e_inv, kd=k * e_end, e_last=jnp.exp(gl),
                 e_mid=e_mid, e_inv=e_inv, e_all=e_all, e_end=e_end)
    return terms, (pq, sig, fg, lb)


def _causal(nc):
    r = lax.broadcasted_iota(jnp.int32, (nc, HG_CHUNK, HG_CHUNK), 1)
    c = lax.broadcasted_iota(jnp.int32, (nc, HG_CHUNK, HG_CHUNK), 2)
    return r >= c


def _hgrn2_fwd(p, lb_logits, out_gain):
    _, s, d = p.shape
    heads = d // HG_EXPAND
    t = _row_tile(s, 2048)
    nc = t // HG_CHUNK

    def body(p_ref, lbl_ref, gain_ref, o_ref, og_ref, st_ref, g_ref, state, decay):
        @pl.when(pl.program_id(1) == 0)
        def _():
            state[...] = jnp.zeros_like(state)

        tm, _ = _hg_terms(p_ref, lbl_ref)
        g_ref[...] = tm["g"].reshape(t, HG_EXPAND)
        decay[...] = tm["e_last"]
        st_ref[...] = _bdot(tm["v"], tm["kd"], 1, 1)

        def chunk(c, carry):
            add = st_ref[c]
            st = state[...]
            st_ref[c] = st
            state[...] = st * decay[c] + add
            return carry

        lax.fori_loop(0, nc, chunk, 0)
        a = jnp.where(_causal(nc), _bdot(tm["qt"], tm["kt"], 2, 2), 0.0)
        ov = (_bdot(tm["qd"], st_ref[...], 2, 2) + _bdot(a, tm["v"], 2, 1)).reshape(t, HG_EXPAND)
        o_ref[...] = ov
        on = ov * lax.rsqrt(jnp.mean(ov * ov, axis=-1, keepdims=True) + EPS) * gain_ref[...]
        og_ref[...] = (on * _silu(p_ref[3].astype(F32))).astype(BF16)

    blk = pl.BlockSpec((t, HG_EXPAND), lambda h, b: (b, h))
    return pl.pallas_call(
        body, name="hgrn2_fwd", grid=(heads, s // t),
        in_specs=[pl.BlockSpec((4, t, HG_EXPAND), lambda h, b: (0, b, h)), pl.BlockSpec((2, HG_EXPAND), lambda h, b: (0, h)),
                  pl.BlockSpec((1, HG_EXPAND), lambda h, b: (0, 0))],
        out_specs=[blk, blk, pl.BlockSpec((None, nc, HG_EXPAND, HG_EXPAND), lambda h, b: (h, b, 0, 0)), blk],
        out_shape=[jax.ShapeDtypeStruct((s, d), F32), jax.ShapeDtypeStruct((s, d), BF16),
                   jax.ShapeDtypeStruct((heads, s // HG_CHUNK, HG_EXPAND, HG_EXPAND), F32), jax.ShapeDtypeStruct((s, d), F32)],
        scratch_shapes=[pltpu.VMEM((HG_EXPAND, HG_EXPAND), F32), pltpu.VMEM((nc, 1, HG_EXPAND), F32)],
        compiler_params=_params(("parallel", "arbitrary")),
    )(p, lb_logits, out_gain)


def _hgrn2_bwd(p, lb_logits, out_gain, o, dog, states, gsum):
    _, s, d = p.shape
    heads = d // HG_EXPAND
    t = _row_tile(s, 1024)
    nc = t // HG_CHUNK
    nb = s // t

    def body(p_ref, lbl_ref, gain_ref, o_ref, dog_ref, st_ref, g_ref, dp_ref, dlbl_ref, dgain_ref, dstate, decay, dst_s):
        h, b = pl.program_id(0), pl.program_id(1)

        @pl.when(b == 0)
        def _():
            dstate[...] = jnp.zeros_like(dstate)

        tm, (pq, sig, fg, lb) = _hg_terms(p_ref, lbl_ref, g_ref)
        pg = p_ref[3].astype(F32)
        ov = o_ref[...]
        r = lax.rsqrt(jnp.mean(ov * ov, axis=-1, keepdims=True) + EPS)
        ohat = ov * r
        dogv = dog_ref[...]
        d_on = dogv * _silu(pg)
        dp_ref[3] = (dogv * ohat * gain_ref[...] * _dsilu(pg)).astype(BF16)
        dgain = jnp.sum(d_on * ohat, axis=0, keepdims=True)

        @pl.when((h == 0) & (b == 0))
        def _():
            dgain_ref[...] = dgain

        @pl.when((h > 0) | (b > 0))
        def _():
            dgain_ref[...] += dgain

        dohat = d_on * gain_ref[...]
        do = (r * (dohat - ohat * jnp.mean(dohat * ohat, axis=-1, keepdims=True))).reshape(nc, HG_CHUNK, HG_EXPAND)

        decay[...] = tm["e_last"]
        dst_s[...] = _bdot(do, tm["qd"], 1, 1)

        def chunk(i, carry):
            c = nc - 1 - i
            add = dst_s[c]
            dst = dstate[...]
            dst_s[c] = dst
            dstate[...] = dst * decay[c] + add
            return carry

        lax.fori_loop(0, nc, chunk, 0)
        st, dst = st_ref[...], dst_s[...]
        causal = _causal(nc)
        a = jnp.where(causal, _bdot(tm["qt"], tm["kt"], 2, 2), 0.0)
        da = jnp.where(causal, _bdot(do, tm["v"], 2, 2), 0.0)
        dqt = _bdot(da, tm["kt"], 2, 1)
        dkt = _bdot(da, tm["qt"], 1, 1)
        dqd = _bdot(do, st, 2, 1)
        dkd = _bdot(tm["v"], dst, 2, 1)
        dv = _bdot(a, do, 1, 1) + _bdot(tm["kd"], dst, 2, 2)
        dq = dqt * tm["e_mid"] + dqd * tm["e_all"]
        dk = dkt * tm["e_inv"] + dkd * tm["e_end"]
        dg = dqt * tm["qt"] - dkt * tm["kt"] + dqd * tm["qd"] - dkd * tm["kd"]
        dgl = jnp.sum(dkd * tm["kd"], axis=1, keepdims=True) + tm["e_last"] * jnp.sum(dst * st, axis=1, keepdims=True)
        last_row = lax.broadcasted_iota(jnp.int32, (nc, HG_CHUNK, HG_EXPAND), 1) == HG_CHUNK - 1
        flat = lambda a3: a3.reshape(t, HG_EXPAND)
        dlf = _chunk_cumsum(flat(dg + jnp.where(last_row, dgl, 0.0)), reverse=True)
        dfg = dlf / fg - flat(dk)
        dlb = jnp.sum(dfg * (1.0 - sig), axis=0, keepdims=True)
        dl0 = dlb * lb * (1.0 - lb)
        dlbl = jnp.concatenate([dl0, -dl0], axis=0)

        @pl.when(b == 0)
        def _():
            dlbl_ref[...] = dlbl

        @pl.when(b > 0)
        def _():
            dlbl_ref[...] += dlbl

        dp_ref[0] = (flat(dq) * HG_EXPAND ** -0.5 * _dsilu(pq)).astype(BF16)
        dp_ref[1] = (dfg * (1.0 - lb) * sig * (1.0 - sig)).astype(BF16)
        dp_ref[2] = flat(dv).astype(BF16)

    blk = pl.BlockSpec((t, HG_EXPAND), lambda h, b: (nb - 1 - b, h))
    pblk = pl.BlockSpec((4, t, HG_EXPAND), lambda h, b: (0, nb - 1 - b, h))
    return pl.pallas_call(
        body, name="hgrn2_bwd", grid=(heads, nb),
        in_specs=[pblk, pl.BlockSpec((2, HG_EXPAND), lambda h, b: (0, h)), pl.BlockSpec((1, HG_EXPAND), lambda h, b: (0, 0)),
                  blk, blk, pl.BlockSpec((None, nc, HG_EXPAND, HG_EXPAND), lambda h, b: (h, nb - 1 - b, 0, 0)), blk],
        out_specs=[pblk, pl.BlockSpec((2, HG_EXPAND), lambda h, b: (0, h)), pl.BlockSpec((1, HG_EXPAND), lambda h, b: (0, 0))],
        out_shape=[jax.ShapeDtypeStruct((4, s, d), BF16), jax.ShapeDtypeStruct((2, d), F32), jax.ShapeDtypeStruct((1, HG_EXPAND), F32)],
        scratch_shapes=[pltpu.VMEM((HG_EXPAND, HG_EXPAND), F32), pltpu.VMEM((nc, 1, HG_EXPAND), F32),
                        pltpu.VMEM((nc, HG_EXPAND, HG_EXPAND), F32)],
        compiler_params=_params(("arbitrary", "arbitrary")),
    )(p, lb_logits, out_gain, o, dog, states, gsum)


HALO = 8
FFN_FWD_ROWS = 512
FFN_BWD_ROWS = 256


def _shift_down(xv, n):
    return pltpu.roll(xv, n, axis=0)


def _shift_up(xv, n):
    return pltpu.roll(xv, xv.shape[0] - n, axis=0)


def _ffn_hidden_down(name, u, conv_w, conv_b, w_down, h):
    _, nj, s, fb = u.shape
    d = w_down.shape[2]
    tm = _row_tile(s, FFN_FWD_ROWS)
    per = tm // HALO

    def body(gate_ref, prev_ref, val_ref, w_ref, b_ref, wd_ref, h_ref, hid_ref, conv_ref, o_ref):
        i = pl.program_id(0)
        total = h_ref[...]
        for j in range(nj):
            prev = jnp.where(i > 0, prev_ref[j].astype(F32), 0.0)
            ext = jnp.concatenate([prev, gate_ref[j].astype(F32)], axis=0)
            conv = b_ref[j] + w_ref[j, 2:3, :] * ext[HALO:]
            conv = conv + w_ref[j, 1:2, :] * _shift_down(ext, 1)[HALO:]
            conv = conv + w_ref[j, 0:1, :] * _shift_down(ext, 2)[HALO:]
            conv = conv.astype(BF16)
            conv_ref[j] = conv
            hidden = _silu(conv) * val_ref[j]
            hid_ref[j] = hidden
            total = total + _dot(hidden, wd_ref[j])
        o_ref[...] = total

    row = pl.BlockSpec((tm, d), lambda i: (i, 0))
    return pl.pallas_call(
        body, name=name, grid=(s // tm,),
        in_specs=[pl.BlockSpec((None, nj, tm, fb), lambda i: (0, 0, i, 0)),
                  pl.BlockSpec((None, nj, HALO, fb), lambda i: (0, 0, jnp.maximum(i * per - 1, 0), 0)),
                  pl.BlockSpec((None, nj, tm, fb), lambda i: (1, 0, i, 0)),
                  pl.BlockSpec((nj, CONV_WIDTH, fb), lambda i: (0, 0, 0)), pl.BlockSpec((nj, 1, fb), lambda i: (0, 0, 0)),
                  pl.BlockSpec((nj, fb, d), lambda i: (0, 0, 0)), row],
        out_specs=[pl.BlockSpec((nj, tm, fb), lambda i: (0, i, 0)), pl.BlockSpec((nj, tm, fb), lambda i: (0, i, 0)), row],
        out_shape=[jax.ShapeDtypeStruct((nj, s, fb), BF16), jax.ShapeDtypeStruct((nj, s, fb), BF16), jax.ShapeDtypeStruct((s, d), F32)],
        compiler_params=_params(("parallel",)),
    )(u, u, u, conv_w, conv_b, w_down, h)


def _ffn_hidden_up_bwd(name, u, conv, dh, conv_w, w_up, h, gain, dres):
    _, nj, s, fb = u.shape
    d = w_up.shape[2]
    tm = _row_tile(s, FFN_BWD_ROWS)
    per = tm // HALO
    nblk = s // HALO
    ni = s // tm

    def body(gate_ref, conv_ref, cnext_ref, val_ref, vnext_ref, dh_ref, dhnext_ref, w_ref, wu_ref, h_ref, gain_ref, dres_ref,
             du_ref, dw_ref, db_ref, dx_ref, dxb_ref, dgain_ref):
        i = pl.program_id(0)
        has_next = i < ni - 1
        total = None
        for j in range(nj):
            act, dact = _silu_and_grad(conv_ref[j])
            dval = dh_ref[j] * act
            after = jnp.where(has_next, dhnext_ref[j].astype(F32), 0.0) * vnext_ref[j].astype(F32) * _dsilu(cnext_ref[j].astype(F32))
            dconv = jnp.concatenate([(dh_ref[j] * val_ref[j] * dact).astype(F32), after], axis=0)
            taps = [_shift_up(dconv, 2)[:tm], _shift_up(dconv, 1)[:tm], dconv[:tm]]
            dgate = (w_ref[j, 0:1, :] * taps[0] + w_ref[j, 1:2, :] * taps[1] + w_ref[j, 2:3, :] * taps[2]).astype(BF16)
            du_ref[0, j] = dgate
            du_ref[1, j] = dval
            part = _dot(dgate, wu_ref[j]) + _dot(dval, wu_ref[nj + j])
            total = part if total is None else total + part
            gate = gate_ref[j].astype(F32)
            dw = jnp.concatenate([jnp.sum(tap * gate, axis=0, keepdims=True) for tap in taps], axis=0)
            db = jnp.sum(taps[2], axis=0, keepdims=True)

            @pl.when(i == 0)
            def _():
                dw_ref[j] = dw
                db_ref[j] = db

            @pl.when(i > 0)
            def _():
                dw_ref[j] += dw
                db_ref[j] += db

        xv = h_ref[...]
        r = lax.rsqrt(jnp.mean(xv * xv, axis=-1, keepdims=True) + EPS)
        xhat = xv * r
        dgain = jnp.sum(total * xhat, axis=0, keepdims=True)

        @pl.when(i == 0)
        def _():
            dgain_ref[...] = dgain

        @pl.when(i > 0)
        def _():
            dgain_ref[...] += dgain

        dxhat = total * gain_ref[...]
        dx = dres_ref[...] + r * (dxhat - xhat * jnp.mean(dxhat * xhat, axis=-1, keepdims=True))
        dx_ref[...] = dx
        dxb_ref[...] = dx.astype(BF16)

    def tile(part):
        return pl.BlockSpec((None, nj, tm, fb), lambda i: (part, 0, i, 0))

    def after(part):
        return pl.BlockSpec((None, nj, HALO, fb), lambda i: (part, 0, jnp.minimum((i + 1) * per, nblk - 1), 0))

    row = pl.BlockSpec((tm, d), lambda i: (i, 0))
    own = pl.BlockSpec((nj, tm, fb), lambda i: (0, i, 0))
    nxt = pl.BlockSpec((nj, HALO, fb), lambda i: (0, jnp.minimum((i + 1) * per, nblk - 1), 0))
    return pl.pallas_call(
        body, name=name, grid=(ni,),
        in_specs=[tile(0), own, nxt, tile(1), after(1), own, nxt,
                  pl.BlockSpec((nj, CONV_WIDTH, fb), lambda i: (0, 0, 0)),
                  pl.BlockSpec((2 * nj, fb, d), lambda i: (0, 0, 0)), row, pl.BlockSpec((1, d), lambda i: (0, 0)), row],
        out_specs=[pl.BlockSpec((2, nj, tm, fb), lambda i: (0, 0, i, 0)),
                   pl.BlockSpec((nj, CONV_WIDTH, fb), lambda i: (0, 0, 0)), pl.BlockSpec((nj, 1, fb), lambda i: (0, 0, 0)),
                   row, row, pl.BlockSpec((1, d), lambda i: (0, 0))],
        out_shape=[jax.ShapeDtypeStruct((2, nj, s, fb), BF16), jax.ShapeDtypeStruct((nj, CONV_WIDTH, fb), F32),
                   jax.ShapeDtypeStruct((nj, 1, fb), F32), jax.ShapeDtypeStruct((s, d), F32), jax.ShapeDtypeStruct((s, d), BF16),
                   jax.ShapeDtypeStruct((1, d), F32)],
        compiler_params=_params(("arbitrary",)),
    )(u, conv, conv, u, u, dh, dh, conv_w, w_up, h, gain, dres)


ATT_TILE = 512


def _stack_heads(ref, rows, first_head, count):
    hd = ATT_HEAD_DIM
    return jnp.concatenate([ref[rows, (first_head + j) * hd:(first_head + j + 1) * hd] for j in range(count)], axis=0)


def _unstack_heads(stacked, ref, rows, first_head, count):
    hd = ATT_HEAD_DIM
    for pair in range(count // 2):
        both = [stacked[(2 * pair + j) * WINDOW:(2 * pair + j + 1) * WINDOW, :] for j in range(2)]
        ref[rows, (first_head + 2 * pair) * hd:(first_head + 2 * pair + 2) * hd] = jnp.concatenate(both, axis=1).astype(ref.dtype)


def _attn_bias(first_head, count, n_heads, first):
    lanes = count * WINDOW
    ik = lax.broadcasted_iota(jnp.int32, (2 * WINDOW, lanes), 0)
    iq = lax.broadcasted_iota(jnp.int32, (2 * WINDOW, lanes), 1) % WINDOW
    dist = iq + WINDOW - ik
    valid = (dist >= 0) & (dist < WINDOW) & (ik >= (WINDOW if first else 0))
    slope = jnp.concatenate([jnp.zeros((1, WINDOW), F32) + 2.0 ** (-8.0 * (first_head + j + 1) / n_heads) for j in range(count)], axis=1)
    return jnp.where(valid, -slope * dist.astype(F32), NEG)


def _fill_attn_bias(bias_ref, group, n_heads):
    @pl.when(pl.program_id(0) == 0)
    def _():
        for g in range(ATT_KV_HEADS):
            bias_ref[0, g] = _attn_bias(g * group, group, n_heads, False)
            bias_ref[1, g] = _attn_bias(g * group, group, n_heads, True)


def _attn_probs_t(kb_scaled, qs, sink_ref, first_head, count, bias):
    sink = jnp.concatenate([jnp.zeros((1, WINDOW), F32) + sink_ref[0, first_head + j] for j in range(count)], axis=1)
    sc = _dot(kb_scaled, qs, NT) + bias
    m = jnp.maximum(jnp.max(sc, axis=0, keepdims=True), sink)
    e = jnp.exp(sc - m)
    es = jnp.exp(sink - m)
    inv = 1.0 / (jnp.sum(e, axis=0, keepdims=True) + es)
    return e * inv, es * inv


ATT_SCALE = ATT_HEAD_DIM ** -0.5


def _attn_specs(s, d, kvd, tq):
    per = tq // WINDOW
    return [pl.BlockSpec((tq, d), lambda i: (i, 0)), pl.BlockSpec((tq, kvd), lambda i: (i, 0)),
            pl.BlockSpec((WINDOW, kvd), lambda i: (jnp.maximum(i * per - 1, 0), 0))]


def _attn_fwd(q, kv, sinks):
    s, d = q.shape
    kvd = kv.shape[1]
    half = kvd // 2
    hd = ATT_HEAD_DIM
    nq = d // hd
    group = nq // ATT_KV_HEADS
    tq = min(s, ATT_TILE)
    per = tq // WINDOW

    def body(q_ref, kvc_ref, kvp_ref, sink_ref, o_ref, band, bias_ref):
        i = pl.program_id(0)
        _fill_attn_bias(bias_ref, group, nq)
        band[0:WINDOW, :] = kvp_ref[...]
        band[WINDOW:, :] = kvc_ref[...]

        def block(b, carry):
            rows = pl.ds(pl.multiple_of(b * WINDOW, WINDOW), WINDOW)
            keys = pl.ds(pl.multiple_of(b * WINDOW, WINDOW), 2 * WINDOW)
            first = (i * per + b) == 0
            for g in range(ATT_KV_HEADS):
                bias = jnp.where(first, bias_ref[1, g], bias_ref[0, g])
                p, _ = _attn_probs_t(band[keys, g * hd:(g + 1) * hd] * ATT_SCALE, _stack_heads(q_ref, rows, g * group, group), sink_ref,
                                     g * group, group, bias)
                out_t = _dot(band[keys, half + g * hd:half + (g + 1) * hd], p, TN)
                _unstack_heads(out_t.T, o_ref, rows, g * group, group)
            return carry

        lax.fori_loop(0, per, block, 0)

    return pl.pallas_call(
        body, name="attn_fwd", grid=(s // tq,),
        in_specs=_attn_specs(s, d, kvd, tq) + [pl.BlockSpec(memory_space=pltpu.SMEM)],
        out_specs=pl.BlockSpec((tq, d), lambda i: (i, 0)), out_shape=jax.ShapeDtypeStruct((s, d), BF16),
        scratch_shapes=[pltpu.VMEM((tq + WINDOW, kvd), BF16), pltpu.VMEM((2, ATT_KV_HEADS, 2 * WINDOW, group * WINDOW), F32)],
        compiler_params=_params(("arbitrary",)),
    )(q, kv, kv, sinks)


def _attn_bwd(q, kv, o, do, sinks):
    s, d = q.shape
    kvd = kv.shape[1]
    half = kvd // 2
    hd = ATT_HEAD_DIM
    nq = d // hd
    group = nq // ATT_KV_HEADS
    tq = min(s, ATT_TILE)
    per = tq // WINDOW
    nt = s // tq

    def body(q_ref, kvc_ref, kvp_ref, o_ref, do_ref, sink_ref, dq_ref, dkvc_ref, dkvp_ref, ds_ref, band, dband, bias_ref):
        i = pl.program_id(0)
        _fill_attn_bias(bias_ref, group, nq)
        band[0:WINDOW, :] = kvp_ref[...]
        band[WINDOW:, :] = kvc_ref[...]
        dband[...] = jnp.zeros_like(dband)
        ds_ref[...] = jnp.zeros_like(ds_ref)

        def block(b, carry):
            rows = pl.ds(pl.multiple_of(b * WINDOW, WINDOW), WINDOW)
            keys = pl.ds(pl.multiple_of(b * WINDOW, WINDOW), 2 * WINDOW)
            first = (i * per + b) == 0
            dks, dvs = [], []
            for g in range(ATT_KV_HEADS):
                kb = band[keys, g * hd:(g + 1) * hd] * ATT_SCALE
                vb = band[keys, half + g * hd:half + (g + 1) * hd]
                qs = _stack_heads(q_ref, rows, g * group, group)
                dos = _stack_heads(do_ref, rows, g * group, group)
                p, ps = _attn_probs_t(kb, qs, sink_ref, g * group, group, jnp.where(first, bias_ref[1, g], bias_ref[0, g]))
                prod = dos.astype(F32) * _stack_heads(o_ref, rows, g * group, group).astype(F32)
                dsum = lax.dot_general(jnp.ones((8, hd), F32), prod, NT, precision=lax.Precision.HIGHEST,
                                       preferred_element_type=F32)[0:1, :]
                dsc = p * (_dot(vb, dos, NT) - dsum)
                dvs.append(_dot(p, dos))
                dks.append(_dot(dsc, qs * ATT_SCALE))
                _unstack_heads(_dot(kb, dsc, TN).T, dq_ref, rows, g * group, group)
                gone = ps * dsum
                for j in range(group):
                    ds_ref[g * group + j:g * group + j + 1, :] += jnp.zeros((1, 128), F32) - jnp.sum(gone[:, j * WINDOW:(j + 1) * WINDOW])
            dband[keys, 0:half] += jnp.concatenate(dks, axis=1)
            dband[keys, half:] += jnp.concatenate(dvs, axis=1)
            return carry

        lax.fori_loop(0, per, block, 0)
        dkvp_ref[...] = dband[0:WINDOW, :]
        dkvc_ref[...] = dband[WINDOW:, :]

    big = pl.BlockSpec((tq, d), lambda i: (i, 0))
    return pl.pallas_call(
        body, name="attn_bwd", grid=(nt,),
        in_specs=_attn_specs(s, d, kvd, tq) + [big, big, pl.BlockSpec(memory_space=pltpu.SMEM)],
        out_specs=[big, pl.BlockSpec((tq, kvd), lambda i: (i, 0)), pl.BlockSpec((None, WINDOW, kvd), lambda i: (i, 0, 0)),
                   pl.BlockSpec((None, nq, 128), lambda i: (i, 0, 0))],
        out_shape=[jax.ShapeDtypeStruct((s, d), BF16), jax.ShapeDtypeStruct((s, kvd), F32), jax.ShapeDtypeStruct((nt, WINDOW, kvd), F32),
                   jax.ShapeDtypeStruct((nt, nq, 128), F32)],
        scratch_shapes=[pltpu.VMEM((tq + WINDOW, kvd), BF16), pltpu.VMEM((tq + WINDOW, kvd), F32),
                        pltpu.VMEM((2, ATT_KV_HEADS, 2 * WINDOW, group * WINDOW), F32)],
        compiler_params=_params(("arbitrary",)),
    )(q, kv, kv, o, do, sinks)


HBM_SPEC = pl.BlockSpec(memory_space=pltpu.HBM)
VMEM_SPEC = pl.BlockSpec(memory_space=pltpu.VMEM)


def _place():
    return lax.axis_index("x"), lax.axis_index("y"), lax.axis_index("c")


def _flip(pos, r):
    return tuple(1 - p if (r >> (2 - a)) & 1 else p for a, p in enumerate(pos))


def _index(pos):
    return 4 * pos[0] + 2 * pos[1] + pos[2]


def _all_gather(name, shards, spec):
    n = len(shards)

    def body(*refs):
        x_refs, o_refs = refs[:n], refs[n:2 * n]
        send_sems, recv_sems, local_sems = refs[2 * n:]
        me = _place()
        sibling = _flip(me, 1)
        far = [_flip(me, r) for r in (4, 2, 6)]

        def copy(t, sem, block, to, src=None):
            rows = o_refs[t].at[_index(block)]
            return pltpu.make_async_remote_copy(
                src_ref=rows if src is None else src, dst_ref=rows, send_sem=send_sems.at[t, sem], recv_sem=recv_sems.at[t, sem],
                device_id=to, device_id_type=MESH)

        own = [pltpu.make_async_copy(x_refs[t], o_refs[t].at[_index(me)], local_sems.at[t]) for t in range(n)]
        for cp in own:
            cp.start()
        first = []
        for t in range(n):
            first.append(copy(t, 0, me, sibling, src=x_refs[t]))
            first += [copy(t, 1 + j, me, peer, src=x_refs[t]) for j, peer in enumerate(far)]
        for cp in first:
            cp.start()
        passed = []
        for j, peer in enumerate(far):
            for t in range(n):
                copy(t, 1 + j, peer, me).wait_recv()
                cp = copy(t, 4 + j, peer, sibling)
                cp.start()
                passed.append(cp)
        for t in range(n):
            copy(t, 0, sibling, me).wait_recv()
            for j, peer in enumerate(far):
                copy(t, 4 + j, _flip(peer, 1), me).wait_recv()
        for cp in first + passed:
            cp.wait_send()
        for cp in own:
            cp.wait()

    return pl.pallas_call(
        body, name=name, in_specs=[spec] * n, out_specs=[spec] * n,
        out_shape=[jax.ShapeDtypeStruct((N_DEV,) + sh.shape, sh.dtype) for sh in shards],
        scratch_shapes=[pltpu.SemaphoreType.DMA((n, 7)), pltpu.SemaphoreType.DMA((n, 7)), pltpu.SemaphoreType.DMA((n,))],
    )(*shards)


SEM_SPEC = pl.BlockSpec(memory_space=pltpu.SEMAPHORE)
ANY_SPEC = pl.BlockSpec(memory_space=pl.ANY)


def _landing(own, mine):
    return lax.dynamic_update_slice(lax.empty((N_DEV,) + own.shape, own.dtype), own[None], (mine,) + (0,) * own.ndim)


def _peer_copies(src_refs, land_refs, send_sems, recv_sems, scatter, arrivals):
    me = _place()
    mine = _index(me)
    copies = []
    for t, (src, land) in enumerate(zip(src_refs, land_refs)):
        for r in range(1, N_DEV):
            peer = _flip(me, r)
            theirs = _index(peer)
            sem = t * N_DEV + r - 1
            copies.append(pltpu.make_async_remote_copy(
                src_ref=src.at[theirs] if scatter else src, dst_ref=land.at[theirs if arrivals else mine],
                send_sem=send_sems.at[sem], recv_sem=recv_sems.at[sem], device_id=peer, device_id_type=MESH))
    return copies


def _own_copies(src_refs, land_refs, send_sems):
    mine = _index(_place())
    return [pltpu.make_async_copy(src.at[mine], land.at[mine], send_sems.at[t * N_DEV + N_DEV - 1])
            for t, (src, land) in enumerate(zip(src_refs, land_refs))]


def _send_start(name, sources, lands, scatter, after=None, carry=None):
    n = len(sources)
    extra = [a for a in (after, carry) if a is not None]
    token = jax.ShapeDtypeStruct((8, 128), F32) if carry is None else jax.ShapeDtypeStruct(carry.shape, carry.dtype)

    def body(*refs):
        outs = refs[2 * n + len(extra):]
        for out in _peer_copies(refs[:n], refs[n:2 * n], outs[0], outs[1], scatter, False) + (_own_copies(refs[:n], refs[n:2 * n], outs[0]) if scatter else []):
            out.start()
        outs[-1][...] = jnp.zeros_like(outs[-1]) if carry is None else refs[2 * n + len(extra) - 1][...]

    outs = pl.pallas_call(
        body, name=name, in_specs=[HBM_SPEC] * (2 * n) + [ANY_SPEC] * (after is not None) + [VMEM_SPEC] * (carry is not None),
        out_specs=[SEM_SPEC, SEM_SPEC] + [HBM_SPEC] * (2 * n) + [VMEM_SPEC],
        out_shape=[pltpu.SemaphoreType.DMA((n * N_DEV,)), pltpu.SemaphoreType.DMA((n * N_DEV,))]
        + [pltpu.HBM(a.shape, a.dtype) for a in list(sources) + list(lands)] + [token],
        input_output_aliases={i: 2 + i for i in range(2 * n)},
        compiler_params=pltpu.CompilerParams(has_side_effects=pltpu.SideEffectType.DATAFLOW_SIDE_EFFECTING),
    )(*[pltpu.with_memory_space_constraint(a, pltpu.HBM) for a in list(sources) + list(lands)], *extra)
    return outs[0], outs[1], outs[2:2 + n], outs[2 + n:2 + 2 * n], outs[-1]


def _send_wait(name, started, after, scatter):
    send_sems, recv_sems, sources, lands, _ = started
    n = len(sources)

    def body(*refs):
        for out in _peer_copies(refs[:n], refs[n:2 * n], refs[2 * n], refs[2 * n + 1], scatter, False):
            out.wait_send()
        for own in _own_copies(refs[:n], refs[n:2 * n], refs[2 * n]) if scatter else []:
            own.wait()
        for arrival in _peer_copies(refs[:n], refs[n:2 * n], refs[2 * n], refs[2 * n + 1], scatter, True):
            arrival.wait_recv()

    outs = pl.pallas_call(
        body, name=name, in_specs=[HBM_SPEC] * (2 * n) + [SEM_SPEC, SEM_SPEC, ANY_SPEC], out_specs=[HBM_SPEC] * (2 * n),
        out_shape=[pltpu.HBM(a.shape, a.dtype) for a in list(sources) + list(lands)],
        input_output_aliases={i: i for i in range(2 * n)},
        compiler_params=pltpu.CompilerParams(has_side_effects=pltpu.SideEffectType.DATAFLOW_SIDE_EFFECTING),
    )(*sources, *lands, send_sems, recv_sems, after)
    return outs[n:]


def _pack_rows(parts):
    offsets, row = [], 0
    for part in parts:
        offsets.append(row)
        row += part.shape[0]
    return offsets, -(-row // 8) * 8, -(-max(part.shape[1] for part in parts) // 128) * 128


def _pack(name, parts):
    offsets, rows, width = _pack_rows(parts)

    def body(*refs):
        o_ref = refs[-1]
        o_ref[...] = jnp.zeros_like(o_ref)
        for off, ref in zip(offsets, refs[:-1]):
            o_ref[off:off + ref.shape[0], 0:ref.shape[1]] = ref[...]

    return pl.pallas_call(body, name=name, in_specs=[VMEM_SPEC] * len(parts), out_specs=VMEM_SPEC,
                          out_shape=jax.ShapeDtypeStruct((rows, width), F32))(*parts)


def _adamw_math(w, g, m, v):
    m = ADAM_B1 * m + (1.0 - ADAM_B1) * g
    v = ADAM_B2 * v + (1.0 - ADAM_B2) * (g * g)
    m_hat = m * (1.0 / (1.0 - ADAM_B1 ** ADAM_STEP))
    denom = jnp.sqrt(v * (1.0 / (1.0 - ADAM_B2 ** ADAM_STEP))) + ADAM_EPS
    inv = pl.reciprocal(denom, approx=True)
    inv = inv * (2.0 - denom * inv)
    return -ADAM_LR * (m_hat * inv + ADAM_WD * w), m, v


def _adamw_step(w_ref, m_ref, v_ref, p_ref, g_ref, d_ref, nm_ref, nv_ref):
    g = p_ref[0].astype(F32)
    for dev in range(1, N_DEV):
        g = g + p_ref[dev].astype(F32)
    g_ref[...] = g
    d_ref[...], nm_ref[...], nv_ref[...] = _adamw_math(w_ref[...], g, m_ref[...], v_ref[...])


def _adamw_rows(rows):
    return max(t for t in range(8, min(rows // 4, 256) + 1, 8) if rows % t == 0)


def _adamw_shard(name, w, m, v, partials):
    rows, cols = w.shape
    tr = _adamw_rows(rows)
    blk = pl.BlockSpec((tr, cols), lambda i: (i, 0))
    return pl.pallas_call(
        _adamw_step_fn(), name=name, grid=(rows // tr,), in_specs=[blk, blk, blk, pl.BlockSpec((N_DEV, tr, cols), lambda i: (0, i, 0))],
        out_specs=[blk] * 4, out_shape=[jax.ShapeDtypeStruct((rows, cols), F32)] * 4, compiler_params=_params(("parallel",)),
    )(w, m, v, partials)


def _adamw_step_fn():
    return functools.partial(_adamw_step)


def _adamw_layers(name, w, m, v, partials):
    layers, rows, cols = w.shape
    tr = _adamw_rows(rows)
    last = rows // tr - 1

    def body(w_ref, m_ref, v_ref, *rest):
        for layer in range(layers):
            @pl.when(pl.program_id(0) == layer)
            def _():
                _adamw_step(w_ref, m_ref, v_ref, rest[layer], *rest[layers:])

    blk = pl.BlockSpec((None, tr, cols), lambda l, i: (l, i, 0))
    part = lambda layer: pl.BlockSpec((N_DEV, tr, cols), lambda l, i: (0, jnp.where(l == layer, i, jnp.where(l < layer, 0, last)), 0))
    return pl.pallas_call(
        body, name=name, grid=(layers, rows // tr), in_specs=[blk, blk, blk] + [part(layer) for layer in range(layers)],
        out_specs=[blk] * 4, out_shape=[jax.ShapeDtypeStruct(w.shape, F32)] * 4, compiler_params=_params(("arbitrary", "arbitrary")),
    )(w, m, v, *partials)


def _adamw_small(gathered, places, entries):
    n = len(entries)
    np_ = len(gathered)

    def body(*refs):
        pack_refs = refs[:np_]
        refs = refs[np_ - 1:]
        w_refs, m_refs, v_refs = refs[1:1 + n], refs[1 + n:1 + 2 * n], refs[1 + 2 * n:1 + 3 * n]
        outs = refs[1 + 3 * n:]
        totals = []
        for pack_ref in pack_refs:
            acc = pack_ref[0]
            for dev in range(1, N_DEV):
                acc = acc + pack_ref[dev]
            totals.append(acc)
        mine = _index(_place())
        for e in range(n):
            rows, cols = w_refs[e].shape
            total, off = totals[places[e][0]], places[e][1]
            if entries[e][3]:
                g = jnp.zeros((rows, cols), F32)
                for dev in range(N_DEV):
                    g = g + jnp.where(mine == dev, total[off + dev * rows:off + (dev + 1) * rows, 0:cols], 0.0)
            else:
                g = total[off:off + rows, 0:cols]
            outs[4 * e][...] = g
            outs[4 * e + 1][...], outs[4 * e + 2][...], outs[4 * e + 3][...] = _adamw_math(w_refs[e][...], g, m_refs[e][...], v_refs[e][...])
        outs[4 * n][...] = totals[places[n][0]][places[n][1]:places[n][1] + 1, 0:128]

    shapes = []
    for w, _, _, _ in entries:
        shapes += [jax.ShapeDtypeStruct(w.shape, F32)] * 4
    shapes.append(jax.ShapeDtypeStruct((1, 128), F32))
    return pl.pallas_call(
        body, name="adamw_small", in_specs=[VMEM_SPEC] * (np_ + 3 * n), out_specs=[VMEM_SPEC] * len(shapes), out_shape=shapes,
        compiler_params=pltpu.CompilerParams(vmem_limit_bytes=VMEM_LIMIT),
    )(*gathered, *[e[0] for e in entries], *[e[1] for e in entries], *[e[2] for e in entries])


def _ffn_forward(tag, h, gain, w_up, late):
    s, d = h.shape
    fb = w_up.shape[1]
    tm = _row_tile(s, 2 * MM_ROWS)
    (a,), (u,) = _norm_proj(f"ffn_up_{tag}", h, [
        (gain, w_up, pl.BlockSpec((None, fb, d), lambda i, j: (j, 0, 0)), NT,
         pl.BlockSpec((None, None, tm, fb), lambda i, j: (j // 4, j % 4, i, 0)), jax.ShapeDtypeStruct((2, 4, s, fb), BF16))], tm=tm, nj=N_DEV)
    w_down, conv_w, conv_b = late(u)
    hidden, conv, out = _ffn_hidden_down(f"ffn_hidden_down_{tag}", u, conv_w, conv_b, w_down, h)
    return out, (a, u, hidden, conv)


def _ffn_backward(tag, h, gain, w_up, w_down, conv_w, conv_b, saved, dout, sent=None):
    a, u, hidden, conv = saved
    dout, dout_bf = dout
    s, d = h.shape
    fb = w_up.shape[1]
    tm = _row_tile(s, MM_ROWS)
    dhidden = _matmul(
        f"ffn_down_bwd_{tag}", dout_bf, w_down, dims=NT, grid=(s // tm, 4, 1),
        a_spec=pl.BlockSpec((tm, d), lambda i, j, k: (i, 0)),
        b_spec=pl.BlockSpec((None, fb, d), lambda i, j, k: (j, 0, 0)),
        o_spec=pl.BlockSpec((None, tm, fb), lambda i, j, k: (j, i, 0)),
        out_shape=jax.ShapeDtypeStruct((4, s, fb), BF16))
    dw_down = _matmul(
        f"ffn_down_grad_{tag}", hidden, dout_bf, dims=TN, grid=(4, 1, 1),
        a_spec=pl.BlockSpec((None, s, fb), lambda i, j, k: (i, 0, 0)),
        b_spec=pl.BlockSpec((s, d), lambda i, j, k: (0, 0)),
        o_spec=pl.BlockSpec((None, fb, d), lambda i, j, k: (i, 0, 0)),
        out_shape=jax.ShapeDtypeStruct((4, fb, d), BF16))
    if sent is not None:
        gain = sent(dw_down, gain)
    du, dconv_w, dconv_b, dh, dh_bf, dgain = _ffn_hidden_up_bwd(f"ffn_hidden_up_bwd_{tag}", u, conv, dhidden, conv_w, w_up, h, gain, dout)
    dw_up = _matmul(
        f"ffn_up_grad_{tag}", du, a, dims=TN, grid=(N_DEV, 1, 1),
        a_spec=pl.BlockSpec((None, None, s, fb), lambda i, j, k: (i // 4, i % 4, 0, 0)),
        b_spec=pl.BlockSpec((s, d), lambda i, j, k: (0, 0)),
        o_spec=pl.BlockSpec((None, fb, d), lambda i, j, k: (i, 0, 0)),
        out_shape=jax.ShapeDtypeStruct((N_DEV, fb, d), BF16))
    return (dh, dh_bf), dgain, dw_up, dw_down, dconv_w, dconv_b


def kernel(x, hg_norm, hg_w_in, hg_lb_logits, hg_out_norm, hg_w_out, kv_norm, w_kv, attn_norm, attn_w_q, attn_sinks, attn_w_o, ffn_norm, ffn_w_up, ffn_conv_w, ffn_conv_b, ffn_w_down, final_norm, loss_target, m_hg_norm, m_hg_w_in, m_hg_lb_logits, m_hg_out_norm, m_hg_w_out, m_kv_norm, m_w_kv, m_attn_norm, m_attn_w_q, m_attn_sinks, m_attn_w_o, m_ffn_norm, m_ffn_w_up, m_ffn_conv_w, m_ffn_conv_b, m_ffn_w_down, m_final_norm, v_hg_norm, v_hg_w_in, v_hg_lb_logits, v_hg_out_norm, v_hg_w_out, v_kv_norm, v_w_kv, v_attn_norm, v_attn_w_q, v_attn_sinks, v_attn_w_o, v_ffn_norm, v_ffn_w_up, v_ffn_conv_w, v_ffn_conv_b, v_ffn_w_down, v_final_norm):
    _, s, d = x.shape
    x0, target = x[0], loss_target[0]
    half = hg_w_in.shape[2]
    fs = ffn_conv_w.shape[2]
    fb = 2 * fs
    kvd = w_kv.shape[1]
    nq = d // ATT_HEAD_DIM
    tm = _row_tile(s, MM_ROWS)

    mine = _index(_place())
    gather = lambda tag, shards, after, carry=None: _send_start("gather_start_" + tag, shards, [_landing(a, mine) for a in shards], False, after, carry)
    w_in, g_hgn, g_lbl, w_out = _all_gather("gather_hg", [hg_w_in[0].astype(BF16), hg_norm, hg_lb_logits, hg_w_out[0].astype(BF16)], HBM_SPEC)
    w_out = w_out.reshape(d, d)
    up_t = lambda a: jnp.swapaxes(a, -1, -2)
    coming_up0 = gather("ffn_up0", [up_t(ffn_w_up[0]).astype(BF16)], None, g_hgn.reshape(1, d))
    hgn = coming_up0[4]
    lbl = g_lbl.transpose(1, 0, 2).reshape(2, d)
    conv_b = [ffn_conv_b[layer].reshape(4, 1, fb) for layer in range(2)]
    gains = [ffn_norm[0:1], ffn_norm[1:2]]
    kvn, fin = kv_norm.reshape(1, d), final_norm.reshape(1, d)

    t2 = _row_tile(s, 2 * MM_ROWS)
    (a0,), (p,) = _norm_proj("hg_in", x0, [
        (hgn, w_in, pl.BlockSpec((None, d, half), lambda i, j: (j, 0, 0)), NN,
         pl.BlockSpec((None, t2, half), lambda i, j: (j // 2, i, j % 2)), jax.ShapeDtypeStruct((4, s, d), BF16))], tm=t2, nj=N_DEV)
    o, og, states, gsum = _hgrn2_fwd(p, lbl, hg_out_norm)
    coming_dn0 = gather("ffn_down0", [ffn_conv_w, ffn_w_down[0].astype(BF16)], o)
    x1 = _mm_rows("hg_out", og, w_out, out_dtype=F32, add=x0, after=coming_dn0[4])
    w_up0, = _send_wait("gather_wait_ffn_up0", coming_up0, x1, False)
    coming_attn = gather("attn", [w_kv.astype(BF16), attn_w_q[0].astype(BF16), attn_w_o[0].astype(BF16)], w_up0, gains[0])
    gains[0] = coming_attn[4]
    w_up, w_dn, conv_w, coming = [w_up0, None], [None, None], [], {}

    def late0(u):
        g_cw, w_dn0 = _send_wait("gather_wait_ffn_down0", coming_dn0, u, False)
        w_dn[0] = w_dn0.reshape(4, fb, d)
        conv_w.extend(g_cw[:, layer].reshape(4, 2, CONV_WIDTH, fs).transpose(0, 2, 1, 3).reshape(4, CONV_WIDTH, fb) for layer in range(2))
        coming["up1"] = gather("ffn_up1", [up_t(ffn_w_up[1]).astype(BF16)], w_dn0, conv_b[0])
        return w_dn[0], conv_w[0], coming["up1"][4]

    x2, saved0 = _ffn_forward("0", x1, gains[0], w_up[0], late0)
    w_kvg, w_q, w_o = _send_wait("gather_wait_attn", coming_attn, x2, False)
    w_kvg, w_q, w_o = w_kvg.reshape(d, kvd), w_q.reshape(d, d), w_o.reshape(d, d)
    (akv, a2), (kv, q) = _norm_proj("attn_in", x2, [
        (kvn, w_kvg, pl.BlockSpec((d, kvd), lambda i, j: (0, 0)), NN, pl.BlockSpec((tm, kvd), lambda i, j: (i, 0)), jax.ShapeDtypeStruct((s, kvd), BF16)),
        (attn_norm, w_q, pl.BlockSpec((d, d), lambda i, j: (0, 0)), NN, pl.BlockSpec((tm, d), lambda i, j: (i, 0)), jax.ShapeDtypeStruct((s, d), BF16))],
        tm=tm, nj=1)
    coming_dn1 = gather("ffn_down1", [ffn_w_down[1].astype(BF16)], q, attn_sinks)
    att = _attn_fwd(q, kv, coming_dn1[4])
    x3 = _mm_rows("attn_out", att, w_o, out_dtype=F32, add=x2)
    w_up[1], = _send_wait("gather_wait_ffn_up1", coming["up1"], x3, False)

    def late1(u):
        w_dn[1] = _send_wait("gather_wait_ffn_down1", coming_dn1, u, False)[0].reshape(4, fb, d)
        return w_dn[1], conv_w[1], conv_b[1]

    x4, saved1 = _ffn_forward("1", x3, gains[1], w_up[1], late1)
    dx4, dx4_bf, d_fin, loss_part = _loss_head(x4, fin, target)

    dx3, d_fn1, dw_up1, dw_dn1, dcw1, dcb1 = _ffn_backward("1", x3, gains[1], w_up[1], w_dn[1], conv_w[1], conv_b[1], saved1, (dx4, dx4_bf))
    rows = d // N_DEV
    scatter = lambda tag, stacks, carry: _send_start("scatter_start_" + tag, stacks, [lax.empty(a.shape, a.dtype) for a in stacks], True, None, carry)
    going_ffn1 = scatter("ffn1", [dw_up1, dw_dn1.reshape(N_DEV, fs, d)], attn_sinks)
    datt = _mm_rows_nt("attn_out_bwd", dx3[1], w_o, out_dtype=BF16)
    dw_o = _mm_tn("attn_out_grad", att, dx3[1])
    dq, dkv_own, dkv_before, dsink = _attn_bwd(q, kv, att, datt, going_ffn1[4])
    tiles = dkv_before.shape[0]
    dkv = dkv_own.reshape(tiles, s // tiles, kvd)
    dkv = jnp.concatenate([dkv[:, :-WINDOW], dkv[:, -WINDOW:] + jnp.pad(dkv_before[1:], ((0, 1), (0, 0), (0, 0)))], axis=1).reshape(s, kvd)
    dw_q = _mm_tn("q_proj_grad", a2, dq)
    dw_kv = _mm_tn("kv_proj_grad", akv, dkv)
    going_attn = scatter("attn", [dw_kv.reshape(N_DEV, rows, kvd), dw_q.reshape(N_DEV, rows, d), dw_o.reshape(N_DEV, rows, d)], kvn)
    whole = lambda a_ref, b_ref: [(a_ref[...], b_ref[...])]
    rows_of = lambda width: (lambda tile: pl.BlockSpec((tile, width), lambda i: (i, 0)))
    dx2, (d_kvn, d_attn) = _proj_norm_bwd("attn_in_bwd", x2, dx3[0], [
        (dkv, rows_of(kvd), w_kvg, pl.BlockSpec((d, kvd), lambda i: (0, 0)), whole, going_attn[4]),
        (dq, rows_of(d), w_q, pl.BlockSpec((d, d), lambda i: (0, 0)), whole, attn_norm)])
    going = {}

    def sent0(dw_dn0, gain):
        going["ffn_dn0"] = scatter("ffn_dn0", [dw_dn0.reshape(N_DEV, fs, d)], gain)
        return going["ffn_dn0"][4]

    dx1, d_fn0, dw_up0, _, dcw0, dcb0 = _ffn_backward("0", x1, gains[0], w_up[0], w_dn[0], conv_w[0], conv_b[0], saved0, dx2, sent0)
    dw_out = _mm_tn("hg_out_grad", og, dx1[1])
    going_ffn0 = scatter("ffn0", [dw_up0, dw_out.reshape(N_DEV, rows, d)], hg_out_norm)
    dog = _mm_rows_nt("hg_out_bwd", dx1[1], w_out, out_dtype=F32)
    dp, d_lbl, d_ogain = _hgrn2_bwd(p, lbl, going_ffn0[4], o, dog, states, gsum)
    dw_in = _matmul(
        "hg_in_grad", a0, dp, dims=TN, grid=(1, N_DEV, 1),
        a_spec=pl.BlockSpec((s, d), lambda i, j, k: (0, 0)),
        b_spec=pl.BlockSpec((None, s, half), lambda i, j, k: (j // 2, 0, j % 2)),
        o_spec=pl.BlockSpec((None, d, half), lambda i, j, k: (j, 0, 0)),
        out_shape=jax.ShapeDtypeStruct((N_DEV, d, half), BF16))
    going_hg = scatter("hg", [dw_in], hgn)
    (dx0, _), (d_hgn,) = _proj_norm_bwd("hg_in_bwd", x0, dx1[0], [
        (dp, lambda tile: pl.BlockSpec((4, tile, d), lambda i: (0, i, 0)), w_in, pl.BlockSpec((N_DEV, d, half), lambda i: (0, 0, 0)),
         lambda g_ref, w_ref: [(g_ref[k // 2, :, (k % 2) * half:(k % 2 + 1) * half], w_ref[k]) for k in range(N_DEV)],
         going_hg[4])])

    as_blocks = lambda a, r: a.reshape(r, N_DEV, -1).transpose(1, 0, 2).reshape(N_DEV * r, -1)
    d_cw = jnp.concatenate([g.transpose(1, 0, 2).reshape(CONV_WIDTH, 4 * fb) for g in (dcw0, dcw1)], axis=0)
    parts = [d_fin, jnp.concatenate([d_fn0, d_fn1], axis=0), jnp.concatenate([dcb0.reshape(1, 4 * fb), dcb1.reshape(1, 4 * fb)], axis=0),
             as_blocks(d_cw, 2 * CONV_WIDTH), d_attn, jnp.sum(dsink[:, :, 0], axis=0).reshape(1, nq), d_kvn, d_ogain,
             as_blocks(d_hgn, 1), as_blocks(d_lbl, 2), loss_part]
    wide = [2]
    packs = [[parts[i] for i in wide], [part for i, part in enumerate(parts) if i not in wide]]
    places = [None] * len(parts)
    for which, members in enumerate([wide, [i for i in range(len(parts)) if i not in wide]]):
        for i, off in zip(members, _pack_rows(packs[which])[0]):
            places[i] = (which, off)
    packed = [_pack("pack_wide_grads", packs[0]), _pack("pack_narrow_grads", packs[1])]
    going_small = _send_start("small_grads_start", packed, [_landing(a, mine) for a in packed], False)

    arrive = lambda tag, going, after: _send_wait("scatter_wait_" + tag, going, after, True)
    (l_up1, l_dn1), (l_kv, l_q, l_o), (l_dn0,), (l_up0, l_out) = (
        arrive("ffn1", going_ffn1, going_small[4]), arrive("attn", going_attn, going_small[4]),
        arrive("ffn_dn0", going["ffn_dn0"], going_small[4]), arrive("ffn0", going_ffn0, going_small[4]))
    big = {}
    for tag, w, m, v, part in [
            ("w_kv", w_kv, m_w_kv, v_w_kv, l_kv), ("attn_w_q", attn_w_q[0], m_attn_w_q[0], v_attn_w_q[0], l_q),
            ("attn_w_o", attn_w_o[0], m_attn_w_o[0], v_attn_w_o[0], l_o)]:
        big[tag] = _adamw_shard("adamw_" + tag, w, m, v, part)
    up_done = _adamw_layers("adamw_ffn_w_up", up_t(ffn_w_up), up_t(m_ffn_w_up), up_t(v_ffn_w_up), (l_up0, l_up1))
    big["ffn_w_up"] = [up_t(a) for a in up_done]
    big["ffn_w_down"] = _adamw_layers("adamw_ffn_w_down", ffn_w_down, m_ffn_w_down, v_ffn_w_down, (l_dn0, l_dn1))
    lead = lambda tag: [a[None] for a in big[tag]]

    both_done = up_done[0][0, 0:1, 0:1] + big["ffn_w_down"][0][0, 0:1, 0:1]
    gathered = _send_wait("small_grads_wait", going_small, both_done, False)
    two = lambda a: a.reshape(-1, a.shape[-1])
    small = [(fin, m_final_norm.reshape(1, d), v_final_norm.reshape(1, d), False), (ffn_norm, m_ffn_norm, v_ffn_norm, False),
             (ffn_conv_b, m_ffn_conv_b, v_ffn_conv_b, False), (two(ffn_conv_w), two(m_ffn_conv_w), two(v_ffn_conv_w), True),
             (attn_norm, m_attn_norm, v_attn_norm, False), (attn_sinks, m_attn_sinks, v_attn_sinks, False),
             (kvn, m_kv_norm.reshape(1, d), v_kv_norm.reshape(1, d), False), (hg_out_norm, m_hg_out_norm, v_hg_out_norm, False),
             (hg_norm, m_hg_norm, v_hg_norm, True), (hg_lb_logits, m_hg_lb_logits, v_hg_lb_logits, True)]
    res = _adamw_small(gathered, places, small)
    l_in, = arrive("hg", going_hg, gathered[1])
    big["hg_w_in"] = _adamw_shard("adamw_hg_w_in", hg_w_in[0], m_hg_w_in[0], v_hg_w_in[0], l_in)
    big["hg_w_out"] = _adamw_shard("adamw_hg_w_out", hg_w_out[0], m_hg_w_out[0], v_hg_w_out[0], l_out)
    names = ["final_norm", "ffn_norm", "ffn_conv_b", "ffn_conv_w", "attn_norm", "attn_sinks", "kv_norm", "hg_out_norm", "hg_norm", "hg_lb_logits"]
    shapes = {"final_norm": final_norm.shape, "kv_norm": kv_norm.shape, "ffn_conv_w": ffn_conv_w.shape}
    out = {n: [a.reshape(shapes[n]) if n in shapes else a for a in res[4 * i:4 * i + 4]] for i, n in enumerate(names)}
    out.update(hg_w_in=lead("hg_w_in"), hg_w_out=lead("hg_w_out"), w_kv=big["w_kv"], attn_w_q=lead("attn_w_q"), attn_w_o=lead("attn_w_o"),
               ffn_w_up=big["ffn_w_up"], ffn_w_down=big["ffn_w_down"])
    order = ["hg_norm", "hg_w_in", "hg_lb_logits", "hg_out_norm", "hg_w_out", "kv_norm", "w_kv", "attn_norm", "attn_w_q", "attn_sinks",
             "attn_w_o", "ffn_norm", "ffn_w_up", "ffn_conv_w", "ffn_conv_b", "ffn_w_down", "final_norm"]
    loss = res[-1][0, 0]
    return (loss, dx0[None], *[out[n][0] for n in order], *[out[n][1] for n in order], *[out[n][2] for n in order], *[out[n][3] for n in order])
```

```python
import functools

import jax
import jax.numpy as jnp
from jax import lax
from jax.experimental import pallas as pl
from jax.experimental.pallas import tpu as pltpu

F32 = jnp.float32
BF16 = jnp.bfloat16

EPS = 1e-6
HG_EXPAND = 128
HG_CHUNK = 32
ATT_HEAD_DIM = 64
ATT_KV_HEADS = 2
WINDOW = 128
CONV_WIDTH = 3
ADAM_LR = 0.001
ADAM_B1 = 0.9
ADAM_B2 = 0.999
ADAM_EPS = 1e-08
ADAM_WD = 0.01
ADAM_STEP = 10

N_DEV = 8
VMEM_LIMIT = 60 * 1024 * 1024
NEG = -1e30

NN = (((1,), (0,)), ((), ()))
NT = (((1,), (1,)), ((), ()))
TN = (((0,), (0,)), ((), ()))
MESH = pl.DeviceIdType.MESH


def _dot(a, b, dims=NN):
    return lax.dot_general(a.astype(BF16), b.astype(BF16), dims, preferred_element_type=F32)


def _sigmoid(x):
    return 0.5 * jnp.tanh(0.5 * x) + 0.5


def _silu(x):
    return x * _sigmoid(x)


def _silu_and_grad(x):
    s = _sigmoid(x)
    return x * s, s * (1.0 + x * (1.0 - s))


def _dsilu(x):
    return _silu_and_grad(x)[1]


def _params(semantics):
    return pltpu.CompilerParams(dimension_semantics=semantics, vmem_limit_bytes=VMEM_LIMIT)


def _row_tile(rows, want=512):
    return min(rows, want)


MM_ROWS = 1024


def _matmul(name, a, b, *, dims, grid, a_spec, b_spec, o_spec, out_shape, add=None, add_spec=None, after=None):
    assert grid[2] == 1

    def body(*refs):
        a_ref, b_ref, o_ref = refs[0], refs[1], refs[-1]
        total = _dot(a_ref[...], b_ref[...], dims)
        if add is not None:
            total = total + refs[2][...]
        o_ref[...] = total.astype(o_ref.dtype)

    in_specs = [a_spec, b_spec] + ([] if add is None else [add_spec]) + ([] if after is None else [pl.BlockSpec(memory_space=pl.ANY)])
    args = (a, b) + (() if add is None else (add,)) + (() if after is None else (after,))
    return pl.pallas_call(
        body, name=name, grid=grid, in_specs=in_specs, out_specs=o_spec, out_shape=out_shape,
        compiler_params=_params(("parallel", "parallel", "arbitrary")),
    )(*args)


def _mm_rows(name, a, w, *, out_dtype, add=None, after=None):
    s, kdim = a.shape
    n = w.shape[1]
    tm = _row_tile(s, MM_ROWS)
    return _matmul(
        name, a, w, dims=NN, grid=(s // tm, 1, 1),
        a_spec=pl.BlockSpec((tm, kdim), lambda i, j, k: (i, 0)),
        b_spec=pl.BlockSpec((kdim, n), lambda i, j, k: (0, 0)),
        o_spec=pl.BlockSpec((tm, n), lambda i, j, k: (i, 0)),
        out_shape=jax.ShapeDtypeStruct((s, n), out_dtype),
        add=add, add_spec=None if add is None else pl.BlockSpec((tm, n), lambda i, j, k: (i, 0)), after=after,
    )


def _mm_rows_nt(name, a, w, *, out_dtype):
    s, n = a.shape
    kdim = w.shape[0]
    tm = _row_tile(s, MM_ROWS)
    return _matmul(
        name, a, w, dims=NT, grid=(s // tm, 1, 1),
        a_spec=pl.BlockSpec((tm, n), lambda i, j, k: (i, 0)),
        b_spec=pl.BlockSpec((kdim, n), lambda i, j, k: (0, 0)),
        o_spec=pl.BlockSpec((tm, kdim), lambda i, j, k: (i, 0)),
        out_shape=jax.ShapeDtypeStruct((s, kdim), out_dtype),
    )


def _mm_tn(name, a, g):
    s, m = a.shape
    n = g.shape[1]
    tn = min(n, 512)
    return _matmul(
        name, a, g, dims=TN, grid=(1, n // tn, 1),
        a_spec=pl.BlockSpec((s, m), lambda i, j, k: (0, 0)),
        b_spec=pl.BlockSpec((s, tn), lambda i, j, k: (0, j)),
        o_spec=pl.BlockSpec((m, tn), lambda i, j, k: (0, j)),
        out_shape=jax.ShapeDtypeStruct((m, n), BF16),
    )


NORM_ROWS = 256


def _norm_proj(name, h, branches, *, tm, nj):
    s, d = h.shape
    n = len(branches)
    rows = min(tm, NORM_ROWS)

    def body(*refs):
        h_ref, gain_refs, w_refs = refs[0], refs[1:1 + n], refs[1 + n:1 + 2 * n]
        a_refs, o_refs = refs[1 + 2 * n:1 + 3 * n], refs[1 + 3 * n:]

        @pl.when(pl.program_id(1) == 0)
        def _():
            def normalize(c, carry):
                at = pl.ds(pl.multiple_of(c * rows, rows), rows)
                xv = h_ref[at, :]
                xhat = xv * lax.rsqrt(jnp.mean(xv * xv, axis=-1, keepdims=True) + EPS)
                for gain_ref, a_ref in zip(gain_refs, a_refs):
                    a_ref[at, :] = (xhat * gain_ref[...]).astype(BF16)
                return carry

            lax.fori_loop(0, tm // rows, normalize, 0)

        for branch, w_ref, a_ref, o_ref in zip(branches, w_refs, a_refs, o_refs):
            o_ref[...] = _dot(a_ref[...], w_ref[...], branch[3]).astype(o_ref.dtype)

    row = pl.BlockSpec((tm, d), lambda i, j: (i, 0))
    vec = pl.BlockSpec((1, d), lambda i, j: (0, 0))
    outs = pl.pallas_call(
        body, name=name, grid=(s // tm, nj), in_specs=[row] + [vec] * n + [b[2] for b in branches],
        out_specs=[row] * n + [b[4] for b in branches],
        out_shape=[jax.ShapeDtypeStruct((s, d), BF16)] * n + [b[5] for b in branches],
        compiler_params=_params(("parallel", "arbitrary")),
    )(h, *[b[0] for b in branches], *[b[1] for b in branches])
    return outs[:n], outs[n:]


def _proj_norm_bwd(name, h, dres, branches):
    s, d = h.shape
    tm = _row_tile(s)
    n = len(branches)

    def body(*refs):
        h_ref, dres_ref = refs[0], refs[1]
        g_refs, w_refs, gain_refs = refs[2:2 + n], refs[2 + n:2 + 2 * n], refs[2 + 2 * n:2 + 3 * n]
        dh_ref, dhb_ref, dg_refs = refs[2 + 3 * n], refs[3 + 3 * n], refs[4 + 3 * n:]
        i = pl.program_id(0)
        xv = h_ref[...]
        r = lax.rsqrt(jnp.mean(xv * xv, axis=-1, keepdims=True) + EPS)
        xhat = xv * r
        total = dres_ref[...]
        for branch, g_ref, w_ref, gain_ref, dg_ref in zip(branches, g_refs, w_refs, gain_refs, dg_refs):
            pairs = branch[4](g_ref, w_ref)
            da = _dot(*pairs[0], NT)
            for pair in pairs[1:]:
                da = da + _dot(*pair, NT)
            dgain = jnp.sum(da * xhat, axis=0, keepdims=True)

            @pl.when(i == 0)
            def _():
                dg_ref[...] = dgain

            @pl.when(i > 0)
            def _():
                dg_ref[...] += dgain

            dxhat = da * gain_ref[...]
            total = total + r * (dxhat - xhat * jnp.mean(dxhat * xhat, axis=-1, keepdims=True))
        dh_ref[...] = total
        dhb_ref[...] = total.astype(BF16)

    row = pl.BlockSpec((tm, d), lambda i: (i, 0))
    vec = pl.BlockSpec((1, d), lambda i: (0, 0))
    outs = pl.pallas_call(
        body, name=name, grid=(s // tm,),
        in_specs=[row, row] + [b[1](tm) for b in branches] + [b[3] for b in branches] + [vec] * n, out_specs=[row, row] + [vec] * n,
        out_shape=[jax.ShapeDtypeStruct((s, d), F32), jax.ShapeDtypeStruct((s, d), BF16)] + [jax.ShapeDtypeStruct((1, d), F32)] * n,
        compiler_params=_params(("arbitrary",)),
    )(h, dres, *[b[0] for b in branches], *[b[2] for b in branches], *[b[5] for b in branches])
    return (outs[0], outs[1]), outs[2:]


def _loss_head(h, gain, target):
    s, d = h.shape
    tm = _row_tile(s)

    def body(h_ref, g_ref, t_ref, dh_ref, dhb_ref, dg_ref, loss_ref):
        i = pl.program_id(0)
        xv = h_ref[...]
        r = lax.rsqrt(jnp.mean(xv * xv, axis=-1, keepdims=True) + EPS)
        xhat = xv * r
        err = xhat * g_ref[...] - t_ref[...]
        dy = err * (1.0 / d)
        part = jnp.zeros((1, 128), F32) + 0.5 * jnp.sum(jnp.mean(err * err, axis=-1, keepdims=True))
        dgain = jnp.sum(dy * xhat, axis=0, keepdims=True)

        @pl.when(i == 0)
        def _():
            dg_ref[...] = dgain
            loss_ref[...] = part

        @pl.when(i > 0)
        def _():
            dg_ref[...] += dgain
            loss_ref[...] += part

        dxhat = dy * g_ref[...]
        dh = r * (dxhat - xhat * jnp.mean(dxhat * xhat, axis=-1, keepdims=True))
        dh_ref[...] = dh
        dhb_ref[...] = dh.astype(BF16)

    row = pl.BlockSpec((tm, d), lambda i: (i, 0))
    vec = pl.BlockSpec((1, d), lambda i: (0, 0))
    return pl.pallas_call(
        body, name="loss_head", grid=(s // tm,), in_specs=[row, vec, row],
        out_specs=[row, row, vec, pl.BlockSpec((1, 128), lambda i: (0, 0))],
        out_shape=[jax.ShapeDtypeStruct((s, d), F32), jax.ShapeDtypeStruct((s, d), BF16), jax.ShapeDtypeStruct((1, d), F32),
                   jax.ShapeDtypeStruct((1, 128), F32)],
        compiler_params=_params(("arbitrary",)),
    )(h, gain, target)


def _bdot(a, b, ca, cb):
    return lax.dot_general(a.astype(BF16), b.astype(BF16), (((ca,), (cb,)), ((0,), (0,))), preferred_element_type=F32)


def _chunk_cumsum(xv, reverse=False):
    n = xv.shape[0]
    row = lax.broadcasted_iota(jnp.int32, xv.shape, 0) % HG_CHUNK
    step = 1
    while step < HG_CHUNK:
        if reverse:
            xv = xv + jnp.where(row < HG_CHUNK - step, pltpu.roll(xv, n - step, axis=0), 0.0)
        else:
            xv = xv + jnp.where(row >= step, pltpu.roll(xv, step, axis=0), 0.0)
        step *= 2
    return xv


def _hg_terms(p_ref, lbl_ref, g_ref=None):
    pq = p_ref[0].astype(F32)
    pf = p_ref[1].astype(F32)
    lb = _sigmoid(lbl_ref[0:1, :] - lbl_ref[1:2, :])
    sig = _sigmoid(pf)
    fg = lb + (1.0 - lb) * sig
    nc = pq.shape[0] // HG_CHUNK
    chunks = lambda a: a.reshape(nc, HG_CHUNK, HG_EXPAND)
    q = chunks(_silu(pq) * HG_EXPAND ** -0.5)
    k = chunks(1.0 - fg)
    v = chunks(p_ref[2].astype(F32))
    g = chunks(_chunk_cumsum(jnp.log(fg)) if g_ref is None else g_ref[...])
    gm = g[:, HG_CHUNK // 2 - 1:HG_CHUNK // 2, :]
    gl = g[:, HG_CHUNK - 1:HG_CHUNK, :]
    e_mid, e_inv, e_all, e_end = jnp.exp(g - gm), jnp.exp(gm - g), jnp.exp(g), jnp.exp(gl - g)
    terms = dict(q=q, k=k, v=v, g=g, qd=q * e_all, qt=q * e_mid, kt=k * e_inv, kd=k * e_end, e_last=jnp.exp(gl),
                 e_mid=e_mid, e_inv=e_inv, e_all=e_all, e_end=e_end)
    return terms, (pq, sig, fg, lb)


def _causal(nc):
    r = lax.broadcasted_iota(jnp.int32, (nc, HG_CHUNK, HG_CHUNK), 1)
    c = lax.broadcasted_iota(jnp.int32, (nc, HG_CHUNK, HG_CHUNK), 2)
    return r >= c


def _hgrn2_fwd(p, lb_logits, out_gain):
    _, s, d = p.shape
    heads = d // HG_EXPAND
    t = _row_tile(s, 2048)
    nc = t // HG_CHUNK

    def body(p_ref, lbl_ref, gain_ref, o_ref, og_ref, st_ref, g_ref, state, decay):
        @pl.when(pl.program_id(1) == 0)
        def _():
            state[...] = jnp.zeros_like(state)

        tm, _ = _hg_terms(p_ref, lbl_ref)
        g_ref[...] = tm["g"].reshape(t, HG_EXPAND)
        decay[...] = tm["e_last"]
        st_ref[...] = _bdot(tm["v"], tm["kd"], 1, 1)

        def chunk(c, carry):
            add = st_ref[c]
            st = state[...]
            st_ref[c] = st
            state[...] = st * decay[c] + add
            return carry

        lax.fori_loop(0, nc, chunk, 0)
        a = jnp.where(_causal(nc), _bdot(tm["qt"], tm["kt"], 2, 2), 0.0)
        ov = (_bdot(tm["qd"], st_ref[...], 2, 2) + _bdot(a, tm["v"], 2, 1)).reshape(t, HG_EXPAND)
        o_ref[...] = ov
        on = ov * lax.rsqrt(jnp.mean(ov * ov, axis=-1, keepdims=True) + EPS) * gain_ref[...]
        og_ref[...] = (on * _silu(p_ref[3].astype(F32))).astype(BF16)

    blk = pl.BlockSpec((t, HG_EXPAND), lambda h, b: (b, h))
    return pl.pallas_call(
        body, name="hgrn2_fwd", grid=(heads, s // t),
        in_specs=[pl.BlockSpec((4, t, HG_EXPAND), lambda h, b: (0, b, h)), pl.BlockSpec((2, HG_EXPAND), lambda h, b: (0, h)),
                  pl.BlockSpec((1, HG_EXPAND), lambda h, b: (0, 0))],
        out_specs=[blk, blk, pl.BlockSpec((None, nc, HG_EXPAND, HG_EXPAND), lambda h, b: (h, b, 0, 0)), blk],
        out_shape=[jax.ShapeDtypeStruct((s, d), F32), jax.ShapeDtypeStruct((s, d), BF16),
                   jax.ShapeDtypeStruct((heads, s // HG_CHUNK, HG_EXPAND, HG_EXPAND), F32), jax.ShapeDtypeStruct((s, d), F32)],
        scratch_shapes=[pltpu.VMEM((HG_EXPAND, HG_EXPAND), F32), pltpu.VMEM((nc, 1, HG_EXPAND), F32)],
        compiler_params=_params(("parallel", "arbitrary")),
    )(p, lb_logits, out_gain)


def _hgrn2_bwd(p, lb_logits, out_gain, o, dog, states, gsum):
    _, s, d = p.shape
    heads = d // HG_EXPAND
    t = _row_tile(s, 1024)
    nc = t // HG_CHUNK
    nb = s // t

    def body(p_ref, lbl_ref, gain_ref, o_ref, dog_ref, st_ref, g_ref, dp_ref, dlbl_ref, dgain_ref, dstate, decay, dst_s):
        h, b = pl.program_id(0), pl.program_id(1)

        @pl.when(b == 0)
        def _():
            dstate[...] = jnp.zeros_like(dstate)

        tm, (pq, sig, fg, lb) = _hg_terms(p_ref, lbl_ref, g_ref)
        pg = p_ref[3].astype(F32)
        ov = o_ref[...]
        r = lax.rsqrt(jnp.mean(ov * ov, axis=-1, keepdims=True) + EPS)
        ohat = ov * r
        dogv = dog_ref[...]
        d_on = dogv * _silu(pg)
        dp_ref[3] = (dogv * ohat * gain_ref[...] * _dsilu(pg)).astype(BF16)
        dgain = jnp.sum(d_on * ohat, axis=0, keepdims=True)

        @pl.when((h == 0) & (b == 0))
        def _():
            dgain_ref[...] = dgain

        @pl.when((h > 0) | (b > 0))
        def _():
            dgain_ref[...] += dgain

        dohat = d_on * gain_ref[...]
        do = (r * (dohat - ohat * jnp.mean(dohat * ohat, axis=-1, keepdims=True))).reshape(nc, HG_CHUNK, HG_EXPAND)

        decay[...] = tm["e_last"]
        dst_s[...] = _bdot(do, tm["qd"], 1, 1)

        def chunk(i, carry):
            c = nc - 1 - i
            add = dst_s[c]
            dst = dstate[...]
            dst_s[c] = dst
            dstate[...] = dst * decay[c] + add
            return carry

        lax.fori_loop(0, nc, chunk, 0)
        st, dst = st_ref[...], dst_s[...]
        causal = _causal(nc)
        a = jnp.where(causal, _bdot(tm["qt"], tm["kt"], 2, 2), 0.0)
        da = jnp.where(causal, _bdot(do, tm["v"], 2, 2), 0.0)
        dqt = _bdot(da, tm["kt"], 2, 1)
        dkt = _bdot(da, tm["qt"], 1, 1)
        dqd = _bdot(do, st, 2, 1)
        dkd = _bdot(tm["v"], dst, 2, 1)
        dv = _bdot(a, do, 1, 1) + _bdot(tm["kd"], dst, 2, 2)
        dq = dqt * tm["e_mid"] + dqd * tm["e_all"]
        dk = dkt * tm["e_inv"] + dkd * tm["e_end"]
        dg = dqt * tm["qt"] - dkt * tm["kt"] + dqd * tm["qd"] - dkd * tm["kd"]
        dgl = jnp.sum(dkd * tm["kd"], axis=1, keepdims=True) + tm["e_last"] * jnp.sum(dst * st, axis=1, keepdims=True)
        last_row = lax.broadcasted_iota(jnp.int32, (nc, HG_CHUNK, HG_EXPAND), 1) == HG_CHUNK - 1
        flat = lambda a3: a3.reshape(t, HG_EXPAND)
        dlf = _chunk_cumsum(flat(dg + jnp.where(last_row, dgl, 0.0)), reverse=True)
        dfg = dlf / fg - flat(dk)
        dlb = jnp.sum(dfg * (1.0 - sig), axis=0, keepdims=True)
        dl0 = dlb * lb * (1.0 - lb)
        dlbl = jnp.concatenate([dl0, -dl0], axis=0)

        @pl.when(b == 0)
        def _():
            dlbl_ref[...] = dlbl

        @pl.when(b > 0)
        def _():
            dlbl_ref[...] += dlbl

        dp_ref[0] = (flat(dq) * HG_EXPAND ** -0.5 * _dsilu(pq)).astype(BF16)
        dp_ref[1] = (dfg * (1.0 - lb) * sig * (1.0 - sig)).astype(BF16)
        dp_ref[2] = flat(dv).astype(BF16)

    blk = pl.BlockSpec((t, HG_EXPAND), lambda h, b: (nb - 1 - b, h))
    pblk = pl.BlockSpec((4, t, HG_EXPAND), lambda h, b: (0, nb - 1 - b, h))
    return pl.pallas_call(
        body, name="hgrn2_bwd", grid=(heads, nb),
        in_specs=[pblk, pl.BlockSpec((2, HG_EXPAND), lambda h, b: (0, h)), pl.BlockSpec((1, HG_EXPAND), lambda h, b: (0, 0)),
                  blk, blk, pl.BlockSpec((None, nc, HG_EXPAND, HG_EXPAND), lambda h, b: (h, nb - 1 - b, 0, 0)), blk],
        out_specs=[pblk, pl.BlockSpec((2, HG_EXPAND), lambda h, b: (0, h)), pl.BlockSpec((1, HG_EXPAND), lambda h, b: (0, 0))],
        out_shape=[jax.ShapeDtypeStruct((4, s, d), BF16), jax.ShapeDtypeStruct((2, d), F32), jax.ShapeDtypeStruct((1, HG_EXPAND), F32)],
        scratch_shapes=[pltpu.VMEM((HG_EXPAND, HG_EXPAND), F32), pltpu.VMEM((nc, 1, HG_EXPAND), F32),
                        pltpu.VMEM((nc, HG_EXPAND, HG_EXPAND), F32)],
        compiler_params=_params(("arbitrary", "arbitrary")),
    )(p, lb_logits, out_gain, o, dog, states, gsum)


HALO = 8
FFN_FWD_ROWS = 512
FFN_BWD_ROWS = 256


def _shift_down(xv, n):
    return pltpu.roll(xv, n, axis=0)


def _shift_up(xv, n):
    return pltpu.roll(xv, xv.shape[0] - n, axis=0)


def _ffn_hidden_down(name, u, conv_w, conv_b, w_down, h):
    _, nj, s, fb = u.shape
    d = w_down.shape[2]
    tm = _row_tile(s, FFN_FWD_ROWS)
    per = tm // HALO

    def body(gate_ref, prev_ref, val_ref, w_ref, b_ref, wd_ref, h_ref, hid_ref, conv_ref, o_ref):
        i = pl.program_id(0)
        total = h_ref[...]
        for j in range(nj):
            prev = jnp.where(i > 0, prev_ref[j].astype(F32), 0.0)
            ext = jnp.concatenate([prev, gate_ref[j].astype(F32)], axis=0)
            conv = b_ref[j] + w_ref[j, 2:3, :] * ext[HALO:]
            conv = conv + w_ref[j, 1:2, :] * _shift_down(ext, 1)[HALO:]
            conv = conv + w_ref[j, 0:1, :] * _shift_down(ext, 2)[HALO:]
            conv = conv.astype(BF16)
            conv_ref[j] = conv
            hidden = _silu(conv) * val_ref[j]
            hid_ref[j] = hidden
            total = total + _dot(hidden, wd_ref[j])
        o_ref[...] = total

    row = pl.BlockSpec((tm, d), lambda i: (i, 0))
    return pl.pallas_call(
        body, name=name, grid=(s // tm,),
        in_specs=[pl.BlockSpec((None, nj, tm, fb), lambda i: (0, 0, i, 0)),
                  pl.BlockSpec((None, nj, HALO, fb), lambda i: (0, 0, jnp.maximum(i * per - 1, 0), 0)),
                  pl.BlockSpec((None, nj, tm, fb), lambda i: (1, 0, i, 0)),
                  pl.BlockSpec((nj, CONV_WIDTH, fb), lambda i: (0, 0, 0)), pl.BlockSpec((nj, 1, fb), lambda i: (0, 0, 0)),
                  pl.BlockSpec((nj, fb, d), lambda i: (0, 0, 0)), row],
        out_specs=[pl.BlockSpec((nj, tm, fb), lambda i: (0, i, 0)), pl.BlockSpec((nj, tm, fb), lambda i: (0, i, 0)), row],
        out_shape=[jax.ShapeDtypeStruct((nj, s, fb), BF16), jax.ShapeDtypeStruct((nj, s, fb), BF16), jax.ShapeDtypeStruct((s, d), F32)],
        compiler_params=_params(("parallel",)),
    )(u, u, u, conv_w, conv_b, w_down, h)


def _ffn_hidden_up_bwd(name, u, conv, dh, conv_w, w_up, h, gain, dres):
    _, nj, s, fb = u.shape
    d = w_up.shape[2]
    tm = _row_tile(s, FFN_BWD_ROWS)
    per = tm // HALO
    nblk = s // HALO
    ni = s // tm

    def body(gate_ref, conv_ref, cnext_ref, val_ref, vnext_ref, dh_ref, dhnext_ref, w_ref, wu_ref, h_ref, gain_ref, dres_ref,
             du_ref, dw_ref, db_ref, dx_ref, dxb_ref, dgain_ref):
        i = pl.program_id(0)
        has_next = i < ni - 1
        total = None
        for j in range(nj):
            act, dact = _silu_and_grad(conv_ref[j])
            dval = dh_ref[j] * act
            after = jnp.where(has_next, dhnext_ref[j].astype(F32), 0.0) * vnext_ref[j].astype(F32) * _dsilu(cnext_ref[j].astype(F32))
            dconv = jnp.concatenate([(dh_ref[j] * val_ref[j] * dact).astype(F32), after], axis=0)
            taps = [_shift_up(dconv, 2)[:tm], _shift_up(dconv, 1)[:tm], dconv[:tm]]
            dgate = (w_ref[j, 0:1, :] * taps[0] + w_ref[j, 1:2, :] * taps[1] + w_ref[j, 2:3, :] * taps[2]).astype(BF16)
            du_ref[0, j] = dgate
            du_ref[1, j] = dval
            part = _dot(dgate, wu_ref[j]) + _dot(dval, wu_ref[nj + j])
            total = part if total is None else total + part
            gate = gate_ref[j].astype(F32)
            dw = jnp.concatenate([jnp.sum(tap * gate, axis=0, keepdims=True) for tap in taps], axis=0)
            db = jnp.sum(taps[2], axis=0, keepdims=True)

            @pl.when(i == 0)
            def _():
                dw_ref[j] = dw
                db_ref[j] = db

            @pl.when(i > 0)
            def _():
                dw_ref[j] += dw
                db_ref[j] += db

        xv = h_ref[...]
        r = lax.rsqrt(jnp.mean(xv * xv, axis=-1, keepdims=True) + EPS)
        xhat = xv * r
        dgain = jnp.sum(total * xhat, axis=0, keepdims=True)

        @pl.when(i == 0)
        def _():
            dgain_ref[...] = dgain

        @pl.when(i > 0)
        def _():
            dgain_ref[...] += dgain

        dxhat = total * gain_ref[...]
        dx = dres_ref[...] + r * (dxhat - xhat * jnp.mean(dxhat * xhat, axis=-1, keepdims=True))
        dx_ref[...] = dx
        dxb_ref[...] = dx.astype(BF16)

    def tile(part):
        return pl.BlockSpec((None, nj, tm, fb), lambda i: (part, 0, i, 0))

    def after(part):
        return pl.BlockSpec((None, nj, HALO, fb), lambda i: (part, 0, jnp.minimum((i + 1) * per, nblk - 1), 0))

    row = pl.BlockSpec((tm, d), lambda i: (i, 0))
    own = pl.BlockSpec((nj, tm, fb), lambda i: (0, i, 0))
    nxt = pl.BlockSpec((nj, HALO, fb), lambda i: (0, jnp.minimum((i + 1) * per, nblk - 1), 0))
    return pl.pallas_call(
        body, name=name, grid=(ni,),
        in_specs=[tile(0), own, nxt, tile(1), after(1), own, nxt,
                  pl.BlockSpec((nj, CONV_WIDTH, fb), lambda i: (0, 0, 0)),
                  pl.BlockSpec((2 * nj, fb, d), lambda i: (0, 0, 0)), row, pl.BlockSpec((1, d), lambda i: (0, 0)), row],
        out_specs=[pl.BlockSpec((2, nj, tm, fb), lambda i: (0, 0, i, 0)),
                   pl.BlockSpec((nj, CONV_WIDTH, fb), lambda i: (0, 0, 0)), pl.BlockSpec((nj, 1, fb), lambda i: (0, 0, 0)),
                   row, row, pl.BlockSpec((1, d), lambda i: (0, 0))],
        out_shape=[jax.ShapeDtypeStruct((2, nj, s, fb), BF16), jax.ShapeDtypeStruct((nj, CONV_WIDTH, fb), F32),
                   jax.ShapeDtypeStruct((nj, 1, fb), F32), jax.ShapeDtypeStruct((s, d), F32), jax.ShapeDtypeStruct((s, d), BF16),
                   jax.ShapeDtypeStruct((1, d), F32)],
        compiler_params=_params(("arbitrary",)),
    )(u, conv, conv, u, u, dh, dh, conv_w, w_up, h, gain, dres)


ATT_TILE = 512


def _stack_heads(ref, rows, first_head, count):
    hd = ATT_HEAD_DIM
    return jnp.concatenate([ref[rows, (first_head + j) * hd:(first_head + j + 1) * hd] for j in range(count)], axis=0)


def _unstack_heads(stacked, ref, rows, first_head, count):
    hd = ATT_HEAD_DIM
    for pair in range(count // 2):
        both = [stacked[(2 * pair + j) * WINDOW:(2 * pair + j + 1) * WINDOW, :] for j in range(2)]
        ref[rows, (first_head + 2 * pair) * hd:(first_head + 2 * pair + 2) * hd] = jnp.concatenate(both, axis=1).astype(ref.dtype)


def _attn_bias(first_head, count, n_heads, first):
    lanes = count * WINDOW
    ik = lax.broadcasted_iota(jnp.int32, (2 * WINDOW, lanes), 0)
    iq = lax.broadcasted_iota(jnp.int32, (2 * WINDOW, lanes), 1) % WINDOW
    dist = iq + WINDOW - ik
    valid = (dist >= 0) & (dist < WINDOW) & (ik >= (WINDOW if first else 0))
    slope = jnp.concatenate([jnp.zeros((1, WINDOW), F32) + 2.0 ** (-8.0 * (first_head + j + 1) / n_heads) for j in range(count)], axis=1)
    return jnp.where(valid, -slope * dist.astype(F32), NEG)


def _fill_attn_bias(bias_ref, group, n_heads):
    @pl.when(pl.program_id(0) == 0)
    def _():
        for g in range(ATT_KV_HEADS):
            bias_ref[0, g] = _attn_bias(g * group, group, n_heads, False)
            bias_ref[1, g] = _attn_bias(g * group, group, n_heads, True)


def _attn_probs_t(kb_scaled, qs, sink_ref, first_head, count, bias):
    sink = jnp.concatenate([jnp.zeros((1, WINDOW), F32) + sink_ref[0, first_head + j] for j in range(count)], axis=1)
    sc = _dot(kb_scaled, qs, NT) + bias
    m = jnp.maximum(jnp.max(sc, axis=0, keepdims=True), sink)
    e = jnp.exp(sc - m)
    es = jnp.exp(sink - m)
    inv = 1.0 / (jnp.sum(e, axis=0, keepdims=True) + es)
    return e * inv, es * inv


ATT_SCALE = ATT_HEAD_DIM ** -0.5


def _attn_specs(s, d, kvd, tq):
    per = tq // WINDOW
    return [pl.BlockSpec((tq, d), lambda i: (i, 0)), pl.BlockSpec((tq, kvd), lambda i: (i, 0)),
            pl.BlockSpec((WINDOW, kvd), lambda i: (jnp.maximum(i * per - 1, 0), 0))]


def _attn_fwd(q, kv, sinks):
    s, d = q.shape
    kvd = kv.shape[1]
    half = kvd // 2
    hd = ATT_HEAD_DIM
    nq = d // hd
    group = nq // ATT_KV_HEADS
    tq = min(s, ATT_TILE)
    per = tq // WINDOW

    def body(q_ref, kvc_ref, kvp_ref, sink_ref, o_ref, band, bias_ref):
        i = pl.program_id(0)
        _fill_attn_bias(bias_ref, group, nq)
        band[0:WINDOW, :] = kvp_ref[...]
        band[WINDOW:, :] = kvc_ref[...]

        def block(b, carry):
            rows = pl.ds(pl.multiple_of(b * WINDOW, WINDOW), WINDOW)
            keys = pl.ds(pl.multiple_of(b * WINDOW, WINDOW), 2 * WINDOW)
            first = (i * per + b) == 0
            for g in range(ATT_KV_HEADS):
                bias = jnp.where(first, bias_ref[1, g], bias_ref[0, g])
                p, _ = _attn_probs_t(band[keys, g * hd:(g + 1) * hd] * ATT_SCALE, _stack_heads(q_ref, rows, g * group, group), sink_ref,
                                     g * group, group, bias)
                out_t = _dot(band[keys, half + g * hd:half + (g + 1) * hd], p, TN)
                _unstack_heads(out_t.T, o_ref, rows, g * group, group)
            return carry

        lax.fori_loop(0, per, block, 0)

    return pl.pallas_call(
        body, name="attn_fwd", grid=(s // tq,),
        in_specs=_attn_specs(s, d, kvd, tq) + [pl.BlockSpec(memory_space=pltpu.SMEM)],
        out_specs=pl.BlockSpec((tq, d), lambda i: (i, 0)), out_shape=jax.ShapeDtypeStruct((s, d), BF16),
        scratch_shapes=[pltpu.VMEM((tq + WINDOW, kvd), BF16), pltpu.VMEM((2, ATT_KV_HEADS, 2 * WINDOW, group * WINDOW), F32)],
        compiler_params=_params(("arbitrary",)),
    )(q, kv, kv, sinks)


def _attn_bwd(q, kv, o, do, sinks):
    s, d = q.shape
    kvd = kv.shape[1]
    half = kvd // 2
    hd = ATT_HEAD_DIM
    nq = d // hd
    group = nq // ATT_KV_HEADS
    tq = min(s, ATT_TILE)
    per = tq // WINDOW
    nt = s // tq

    def body(q_ref, kvc_ref, kvp_ref, o_ref, do_ref, sink_ref, dq_ref, dkv_ref, ds_ref, band, dband, bias_ref, later):
        step = pl.program_id(0)
        i = nt - 1 - step
        _fill_attn_bias(bias_ref, group, nq)
        band[0:WINDOW, :] = kvp_ref[...]
        band[WINDOW:, :] = kvc_ref[...]
        dband[...] = jnp.zeros_like(dband)
        ds_ref[...] = jnp.zeros_like(ds_ref)

        def block(b, carry):
            rows = pl.ds(pl.multiple_of(b * WINDOW, WINDOW), WINDOW)
            keys = pl.ds(pl.multiple_of(b * WINDOW, WINDOW), 2 * WINDOW)
            first = (i * per + b) == 0
            dks, dvs = [], []
            for g in range(ATT_KV_HEADS):
                kb = band[keys, g * hd:(g + 1) * hd] * ATT_SCALE
                vb = band[keys, half + g * hd:half + (g + 1) * hd]
                qs = _stack_heads(q_ref, rows, g * group, group)
                dos = _stack_heads(do_ref, rows, g * group, group)
                p, ps = _attn_probs_t(kb, qs, sink_ref, g * group, group, jnp.where(first, bias_ref[1, g], bias_ref[0, g]))
                prod = dos.astype(F32) * _stack_heads(o_ref, rows, g * group, group).astype(F32)
                dsum = lax.dot_general(jnp.ones((8, hd), F32), prod, NT, precision=lax.Precision.HIGHEST,
                                       preferred_element_type=F32)[0:1, :]
                dsc = p * (_dot(vb, dos, NT) - dsum)
                dvs.append(_dot(p, dos))
                dks.append(_dot(dsc, qs * ATT_SCALE))
                _unstack_heads(_dot(kb, dsc, TN).T, dq_ref, rows, g * group, group)
                gone = ps * dsum
                for j in range(group):
                    ds_ref[g * group + j:g * group + j + 1, :] += jnp.zeros((1, 128), F32) - jnp.sum(gone[:, j * WINDOW:(j + 1) * WINDOW])
            dband[keys, 0:half] += jnp.concatenate(dks, axis=1)
            dband[keys, half:] += jnp.concatenate(dvs, axis=1)
            return carry

        lax.fori_loop(0, per, block, 0)

        @pl.when(step > 0)
        def _():
            dband[tq:, :] += later[...]

        dkv_ref[...] = dband[WINDOW:, :]
        later[...] = dband[0:WINDOW, :]

    big = pl.BlockSpec((tq, d), lambda i: (nt - 1 - i, 0))
    own = pl.BlockSpec((tq, kvd), lambda i: (nt - 1 - i, 0))
    return pl.pallas_call(
        body, name="attn_bwd", grid=(nt,),
        in_specs=[big, own, pl.BlockSpec((WINDOW, kvd), lambda i: (jnp.maximum((nt - 1 - i) * per - 1, 0), 0)), big, big,
                  pl.BlockSpec(memory_space=pltpu.SMEM)],
        out_specs=[big, own, pl.BlockSpec((None, nq, 128), lambda i: (nt - 1 - i, 0, 0))],
        out_shape=[jax.ShapeDtypeStruct((s, d), BF16), jax.ShapeDtypeStruct((s, kvd), F32), jax.ShapeDtypeStruct((nt, nq, 128), F32)],
        scratch_shapes=[pltpu.VMEM((tq + WINDOW, kvd), BF16), pltpu.VMEM((tq + WINDOW, kvd), F32),
                        pltpu.VMEM((2, ATT_KV_HEADS, 2 * WINDOW, group * WINDOW), F32), pltpu.VMEM((WINDOW, kvd), F32)],
        compiler_params=_params(("arbitrary",)),
    )(q, kv, kv, o, do, sinks)


HBM_SPEC = pl.BlockSpec(memory_space=pltpu.HBM)
VMEM_SPEC = pl.BlockSpec(memory_space=pltpu.VMEM)


def _place():
    return lax.axis_index("x"), lax.axis_index("y"), lax.axis_index("c")


def _flip(pos, r):
    return tuple(1 - p if (r >> (2 - a)) & 1 else p for a, p in enumerate(pos))


def _index(pos):
    return 4 * pos[0] + 2 * pos[1] + pos[2]


def _all_gather(name, shards, spec):
    n = len(shards)

    def body(*refs):
        x_refs, o_refs = refs[:n], refs[n:2 * n]
        send_sems, recv_sems, local_sems = refs[2 * n:]
        me = _place()
        sibling = _flip(me, 1)
        far = [_flip(me, r) for r in (4, 2, 6)]

        def copy(t, sem, block, to, src=None):
            rows = o_refs[t].at[_index(block)]
            return pltpu.make_async_remote_copy(
                src_ref=rows if src is None else src, dst_ref=rows, send_sem=send_sems.at[t, sem], recv_sem=recv_sems.at[t, sem],
                device_id=to, device_id_type=MESH)

        own = [pltpu.make_async_copy(x_refs[t], o_refs[t].at[_index(me)], local_sems.at[t]) for t in range(n)]
        for cp in own:
            cp.start()
        first = []
        for t in range(n):
            first.append(copy(t, 0, me, sibling, src=x_refs[t]))
            first += [copy(t, 1 + j, me, peer, src=x_refs[t]) for j, peer in enumerate(far)]
        for cp in first:
            cp.start()
        passed = []
        for j, peer in enumerate(far):
            for t in range(n):
                copy(t, 1 + j, peer, me).wait_recv()
                cp = copy(t, 4 + j, peer, sibling)
                cp.start()
                passed.append(cp)
        for t in range(n):
            copy(t, 0, sibling, me).wait_recv()
            for j, peer in enumerate(far):
                copy(t, 4 + j, _flip(peer, 1), me).wait_recv()
        for cp in first + passed:
            cp.wait_send()
        for cp in own:
            cp.wait()

    return pl.pallas_call(
        body, name=name, in_specs=[spec] * n, out_specs=[spec] * n,
        out_shape=[jax.ShapeDtypeStruct((N_DEV,) + sh.shape, sh.dtype) for sh in shards],
        scratch_shapes=[pltpu.SemaphoreType.DMA((n, 7)), pltpu.SemaphoreType.DMA((n, 7)), pltpu.SemaphoreType.DMA((n,))],
    )(*shards)


SEM_SPEC = pl.BlockSpec(memory_space=pltpu.SEMAPHORE)
ANY_SPEC = pl.BlockSpec(memory_space=pl.ANY)


def _landing(own, mine):
    return lax.dynamic_update_slice(lax.empty((N_DEV,) + own.shape, own.dtype), own[None], (mine,) + (0,) * own.ndim)


def _peer_copies(src_refs, land_refs, send_sems, recv_sems, scatter, arrivals):
    me = _place()
    mine = _index(me)
    copies = []
    for t, (src, land) in enumerate(zip(src_refs, land_refs)):
        for r in range(1, N_DEV):
            peer = _flip(me, r)
            theirs = _index(peer)
            sem = t * N_DEV + r - 1
            copies.append(pltpu.make_async_remote_copy(
                src_ref=src.at[theirs] if scatter else src, dst_ref=land.at[theirs if arrivals else mine],
                send_sem=send_sems.at[sem], recv_sem=recv_sems.at[sem], device_id=peer, device_id_type=MESH))
    return copies


def _own_copies(src_refs, land_refs, send_sems):
    mine = _index(_place())
    return [pltpu.make_async_copy(src.at[mine], land.at[mine], send_sems.at[t * N_DEV + N_DEV - 1])
            for t, (src, land) in enumerate(zip(src_refs, land_refs))]


def _send_start(name, sources, lands, scatter, after=None, carry=None):
    n = len(sources)
    extra = [a for a in (after, carry) if a is not None]
    token = jax.ShapeDtypeStruct((8, 128), F32) if carry is None else jax.ShapeDtypeStruct(carry.shape, carry.dtype)

    def body(*refs):
        outs = refs[2 * n + len(extra):]
        for out in _peer_copies(refs[:n], refs[n:2 * n], outs[0], outs[1], scatter, False) + (_own_copies(refs[:n], refs[n:2 * n], outs[0]) if scatter else []):
            out.start()
        outs[-1][...] = jnp.zeros_like(outs[-1]) if carry is None else refs[2 * n + len(extra) - 1][...]

    outs = pl.pallas_call(
        body, name=name, in_specs=[HBM_SPEC] * (2 * n) + [ANY_SPEC] * (after is not None) + [VMEM_SPEC] * (carry is not None),
        out_specs=[SEM_SPEC, SEM_SPEC] + [HBM_SPEC] * (2 * n) + [VMEM_SPEC],
        out_shape=[pltpu.SemaphoreType.DMA((n * N_DEV,)), pltpu.SemaphoreType.DMA((n * N_DEV,))]
        + [pltpu.HBM(a.shape, a.dtype) for a in list(sources) + list(lands)] + [token],
        input_output_aliases={i: 2 + i for i in range(2 * n)},
        compiler_params=pltpu.CompilerParams(has_side_effects=pltpu.SideEffectType.DATAFLOW_SIDE_EFFECTING),
    )(*[pltpu.with_memory_space_constraint(a, pltpu.HBM) for a in list(sources) + list(lands)], *extra)
    return outs[0], outs[1], outs[2:2 + n], outs[2 + n:2 + 2 * n], outs[-1]


def _send_wait(name, started, after, scatter):
    send_sems, recv_sems, sources, lands, _ = started
    n = len(sources)

    def body(*refs):
        for out in _peer_copies(refs[:n], refs[n:2 * n], refs[2 * n], refs[2 * n + 1], scatter, False):
            out.wait_send()
        for own in _own_copies(refs[:n], refs[n:2 * n], refs[2 * n]) if scatter else []:
            own.wait()
        for arrival in _peer_copies(refs[:n], refs[n:2 * n], refs[2 * n], refs[2 * n + 1], scatter, True):
            arrival.wait_recv()

    outs = pl.pallas_call(
        body, name=name, in_specs=[HBM_SPEC] * (2 * n) + [SEM_SPEC, SEM_SPEC, ANY_SPEC], out_specs=[HBM_SPEC] * (2 * n),
        out_shape=[pltpu.HBM(a.shape, a.dtype) for a in list(sources) + list(lands)],
        input_output_aliases={i: i for i in range(2 * n)},
        compiler_params=pltpu.CompilerParams(has_side_effects=pltpu.SideEffectType.DATAFLOW_SIDE_EFFECTING),
    )(*sources, *lands, send_sems, recv_sems, after)
    return outs[n:]


def _pack_rows(parts):
    offsets, row = [], 0
    for part in parts:
        offsets.append(row)
        row += part.shape[0]
    return offsets, -(-row // 8) * 8, -(-max(part.shape[1] for part in parts) // 128) * 128


def _pack(name, parts):
    offsets, rows, width = _pack_rows(parts)

    def body(*refs):
        o_ref = refs[-1]
        o_ref[...] = jnp.zeros_like(o_ref)
        for off, ref in zip(offsets, refs[:-1]):
            o_ref[off:off + ref.shape[0], 0:ref.shape[1]] = ref[...]

    return pl.pallas_call(body, name=name, in_specs=[VMEM_SPEC] * len(parts), out_specs=VMEM_SPEC,
                          out_shape=jax.ShapeDtypeStruct((rows, width), F32))(*parts)


def _adamw_math(w, g, m, v):
    m = ADAM_B1 * m + (1.0 - ADAM_B1) * g
    v = ADAM_B2 * v + (1.0 - ADAM_B2) * (g * g)
    m_hat = m * (1.0 / (1.0 - ADAM_B1 ** ADAM_STEP))
    denom = jnp.sqrt(v * (1.0 / (1.0 - ADAM_B2 ** ADAM_STEP))) + ADAM_EPS
    inv = pl.reciprocal(denom, approx=True)
    inv = inv * (2.0 - denom * inv)
    return -ADAM_LR * (m_hat * inv + ADAM_WD * w), m, v


def _adamw_step(w_ref, m_ref, v_ref, p_ref, g_ref, d_ref, nm_ref, nv_ref):
    g = p_ref[0].astype(F32)
    for dev in range(1, N_DEV):
        g = g + p_ref[dev].astype(F32)
    g_ref[...] = g
    d_ref[...], nm_ref[...], nv_ref[...] = _adamw_math(w_ref[...], g, m_ref[...], v_ref[...])


def _adamw_rows(rows):
    return max(t for t in range(8, min(rows, 256) + 1, 8) if rows % t == 0)


def _adamw_shard(name, w, m, v, partials):
    rows, cols = w.shape
    tr = _adamw_rows(rows)
    blk = pl.BlockSpec((tr, cols), lambda i: (i, 0))
    return pl.pallas_call(
        _adamw_step_fn(), name=name, grid=(rows // tr,), in_specs=[blk, blk, blk, pl.BlockSpec((N_DEV, tr, cols), lambda i: (0, i, 0))],
        out_specs=[blk] * 4, out_shape=[jax.ShapeDtypeStruct((rows, cols), F32)] * 4, compiler_params=_params(("parallel",)),
    )(w, m, v, partials)


def _adamw_step_fn():
    return functools.partial(_adamw_step)


def _adamw_layers(name, w, m, v, partials):
    layers, rows, cols = w.shape
    tr = _adamw_rows(rows)
    last = rows // tr - 1

    def body(w_ref, m_ref, v_ref, *rest):
        for layer in range(layers):
            @pl.when(pl.program_id(0) == layer)
            def _():
                _adamw_step(w_ref, m_ref, v_ref, rest[layer], *rest[layers:])

    blk = pl.BlockSpec((None, tr, cols), lambda l, i: (l, i, 0))
    part = lambda layer: pl.BlockSpec((N_DEV, tr, cols), lambda l, i: (0, jnp.where(l == layer, i, jnp.where(l < layer, 0, last)), 0))
    return pl.pallas_call(
        body, name=name, grid=(layers, rows // tr), in_specs=[blk, blk, blk] + [part(layer) for layer in range(layers)],
        out_specs=[blk] * 4, out_shape=[jax.ShapeDtypeStruct(w.shape, F32)] * 4, compiler_params=_params(("arbitrary", "arbitrary")),
    )(w, m, v, *partials)


def _adamw_small(gathered, places, entries):
    n = len(entries)
    np_ = len(gathered)

    def body(*refs):
        pack_refs = refs[:np_]
        refs = refs[np_ - 1:]
        w_refs, m_refs, v_refs = refs[1:1 + n], refs[1 + n:1 + 2 * n], refs[1 + 2 * n:1 + 3 * n]
        outs = refs[1 + 3 * n:]
        totals = []
        for pack_ref in pack_refs:
            acc = pack_ref[0]
            for dev in range(1, N_DEV):
                acc = acc + pack_ref[dev]
            totals.append(acc)
        mine = _index(_place())
        for e in range(n):
            rows, cols = w_refs[e].shape
            total, off = totals[places[e][0]], places[e][1]
            if entries[e][3]:
                g = jnp.zeros((rows, cols), F32)
                for dev in range(N_DEV):
                    g = g + jnp.where(mine == dev, total[off + dev * rows:off + (dev + 1) * rows, 0:cols], 0.0)
            else:
                g = total[off:off + rows, 0:cols]
            outs[4 * e][...] = g
            outs[4 * e + 1][...], outs[4 * e + 2][...], outs[4 * e + 3][...] = _adamw_math(w_refs[e][...], g, m_refs[e][...], v_refs[e][...])
        outs[4 * n][...] = totals[places[n][0]][places[n][1]:places[n][1] + 1, 0:128]

    shapes = []
    for w, _, _, _ in entries:
        shapes += [jax.ShapeDtypeStruct(w.shape, F32)] * 4
    shapes.append(jax.ShapeDtypeStruct((1, 128), F32))
    return pl.pallas_call(
        body, name="adamw_small", in_specs=[VMEM_SPEC] * (np_ + 3 * n), out_specs=[VMEM_SPEC] * len(shapes), out_shape=shapes,
        compiler_params=pltpu.CompilerParams(vmem_limit_bytes=VMEM_LIMIT),
    )(*gathered, *[e[0] for e in entries], *[e[1] for e in entries], *[e[2] for e in entries])


def _ffn_forward(tag, h, gain, w_up, late):
    s, d = h.shape
    fb = w_up.shape[1]
    tm = _row_tile(s, 2 * MM_ROWS)
    (a,), (u,) = _norm_proj(f"ffn_up_{tag}", h, [
        (gain, w_up, pl.BlockSpec((None, fb, d), lambda i, j: (j, 0, 0)), NT,
         pl.BlockSpec((None, None, tm, fb), lambda i, j: (j // 4, j % 4, i, 0)), jax.ShapeDtypeStruct((2, 4, s, fb), BF16))], tm=tm, nj=N_DEV)
    w_down, conv_w, conv_b = late(u)
    hidden, conv, out = _ffn_hidden_down(f"ffn_hidden_down_{tag}", u, conv_w, conv_b, w_down, h)
    return out, (a, u, hidden, conv)


def _ffn_backward(tag, h, gain, w_up, w_down, conv_w, conv_b, saved, dout, sent=None):
    a, u, hidden, conv = saved
    dout, dout_bf = dout
    s, d = h.shape
    fb = w_up.shape[1]
    tm = _row_tile(s, MM_ROWS)
    dhidden = _matmul(
        f"ffn_down_bwd_{tag}", dout_bf, w_down, dims=NT, grid=(s // tm, 4, 1),
        a_spec=pl.BlockSpec((tm, d), lambda i, j, k: (i, 0)),
        b_spec=pl.BlockSpec((None, fb, d), lambda i, j, k: (j, 0, 0)),
        o_spec=pl.BlockSpec((None, tm, fb), lambda i, j, k: (j, i, 0)),
        out_shape=jax.ShapeDtypeStruct((4, s, fb), BF16))
    dw_down = _matmul(
        f"ffn_down_grad_{tag}", hidden, dout_bf, dims=TN, grid=(4, 1, 1),
        a_spec=pl.BlockSpec((None, s, fb), lambda i, j, k: (i, 0, 0)),
        b_spec=pl.BlockSpec((s, d), lambda i, j, k: (0, 0)),
        o_spec=pl.BlockSpec((None, fb, d), lambda i, j, k: (i, 0, 0)),
        out_shape=jax.ShapeDtypeStruct((4, fb, d), BF16))
    if sent is not None:
        gain = sent(dw_down, gain)
    du, dconv_w, dconv_b, dh, dh_bf, dgain = _ffn_hidden_up_bwd(f"ffn_hidden_up_bwd_{tag}", u, conv, dhidden, conv_w, w_up, h, gain, dout)
    dw_up = _matmul(
        f"ffn_up_grad_{tag}", du, a, dims=TN, grid=(N_DEV, 1, 1),
        a_spec=pl.BlockSpec((None, None, s, fb), lambda i, j, k: (i // 4, i % 4, 0, 0)),
        b_spec=pl.BlockSpec((s, d), lambda i, j, k: (0, 0)),
        o_spec=pl.BlockSpec((None, fb, d), lambda i, j, k: (i, 0, 0)),
        out_shape=jax.ShapeDtypeStruct((N_DEV, fb, d), BF16))
    return (dh, dh_bf), dgain, dw_up, dw_down, dconv_w, dconv_b


def kernel(x, hg_norm, hg_w_in, hg_lb_logits, hg_out_norm, hg_w_out, kv_norm, w_kv, attn_norm, attn_w_q, attn_sinks, attn_w_o, ffn_norm, ffn_w_up, ffn_conv_w, ffn_conv_b, ffn_w_down, final_norm, loss_target, m_hg_norm, m_hg_w_in, m_hg_lb_logits, m_hg_out_norm, m_hg_w_out, m_kv_norm, m_w_kv, m_attn_norm, m_attn_w_q, m_attn_sinks, m_attn_w_o, m_ffn_norm, m_ffn_w_up, m_ffn_conv_w, m_ffn_conv_b, m_ffn_w_down, m_final_norm, v_hg_norm, v_hg_w_in, v_hg_lb_logits, v_hg_out_norm, v_hg_w_out, v_kv_norm, v_w_kv, v_attn_norm, v_attn_w_q, v_attn_sinks, v_attn_w_o, v_ffn_norm, v_ffn_w_up, v_ffn_conv_w, v_ffn_conv_b, v_ffn_w_down, v_final_norm):
    _, s, d = x.shape
    x0, target = x[0], loss_target[0]
    half = hg_w_in.shape[2]
    fs = ffn_conv_w.shape[2]
    fb = 2 * fs
    kvd = w_kv.shape[1]
    nq = d // ATT_HEAD_DIM
    tm = _row_tile(s, MM_ROWS)

    mine = _index(_place())
    gather = lambda tag, shards, after, carry=None: _send_start("gather_start_" + tag, shards, [_landing(a, mine) for a in shards], False, after, carry)
    w_in, g_hgn, g_lbl, w_out = _all_gather("gather_hg", [hg_w_in[0].astype(BF16), hg_norm, hg_lb_logits, hg_w_out[0].astype(BF16)], HBM_SPEC)
    w_out = w_out.reshape(d, d)
    up_t = lambda a: jnp.swapaxes(a, -1, -2)
    coming_up0 = gather("ffn_up0", [up_t(ffn_w_up[0]).astype(BF16)], None, g_hgn.reshape(1, d))
    hgn = coming_up0[4]
    lbl = g_lbl.transpose(1, 0, 2).reshape(2, d)
    conv_b = [ffn_conv_b[layer].reshape(4, 1, fb) for layer in range(2)]
    gains = [ffn_norm[0:1], ffn_norm[1:2]]
    kvn, fin = kv_norm.reshape(1, d), final_norm.reshape(1, d)

    t2 = _row_tile(s, 2 * MM_ROWS)
    (a0,), (p,) = _norm_proj("hg_in", x0, [
        (hgn, w_in, pl.BlockSpec((None, d, half), lambda i, j: (j, 0, 0)), NN,
         pl.BlockSpec((None, t2, half), lambda i, j: (j // 2, i, j % 2)), jax.ShapeDtypeStruct((4, s, d), BF16))], tm=t2, nj=N_DEV)
    o, og, states, gsum = _hgrn2_fwd(p, lbl, hg_out_norm)
    coming_dn0 = gather("ffn_down0", [ffn_conv_w, ffn_w_down[0].astype(BF16)], o)
    x1 = _mm_rows("hg_out", og, w_out, out_dtype=F32, add=x0, after=coming_dn0[4])
    w_up0, = _send_wait("gather_wait_ffn_up0", coming_up0, x1, False)
    coming_attn = gather("attn", [w_kv.astype(BF16), attn_w_q[0].astype(BF16), attn_w_o[0].astype(BF16)], w_up0, gains[0])
    gains[0] = coming_attn[4]
    w_up, w_dn, conv_w, coming = [w_up0, None], [None, None], [], {}

    def late0(u):
        g_cw, w_dn0 = _send_wait("gather_wait_ffn_down0", coming_dn0, u, False)
        w_dn[0] = w_dn0.reshape(4, fb, d)
        conv_w.extend(g_cw[:, layer].reshape(4, 2, CONV_WIDTH, fs).transpose(0, 2, 1, 3).reshape(4, CONV_WIDTH, fb) for layer in range(2))
        coming["up1"] = gather("ffn_up1", [up_t(ffn_w_up[1]).astype(BF16)], w_dn0, conv_b[0])
        return w_dn[0], conv_w[0], coming["up1"][4]

    x2, saved0 = _ffn_forward("0", x1, gains[0], w_up[0], late0)
    w_kvg, w_q, w_o = _send_wait("gather_wait_attn", coming_attn, x2, False)
    w_kvg, w_q, w_o = w_kvg.reshape(d, kvd), w_q.reshape(d, d), w_o.reshape(d, d)
    (akv, a2), (kv, q) = _norm_proj("attn_in", x2, [
        (kvn, w_kvg, pl.BlockSpec((d, kvd), lambda i, j: (0, 0)), NN, pl.BlockSpec((tm, kvd), lambda i, j: (i, 0)), jax.ShapeDtypeStruct((s, kvd), BF16)),
        (attn_norm, w_q, pl.BlockSpec((d, d), lambda i, j: (0, 0)), NN, pl.BlockSpec((tm, d), lambda i, j: (i, 0)), jax.ShapeDtypeStruct((s, d), BF16))],
        tm=tm, nj=1)
    coming_dn1 = gather("ffn_down1", [ffn_w_down[1].astype(BF16)], q, attn_sinks)
    att = _attn_fwd(q, kv, coming_dn1[4])
    x3 = _mm_rows("attn_out", att, w_o, out_dtype=F32, add=x2)
    w_up[1], = _send_wait("gather_wait_ffn_up1", coming["up1"], x3, False)

    def late1(u):
        w_dn[1] = _send_wait("gather_wait_ffn_down1", coming_dn1, u, False)[0].reshape(4, fb, d)
        return w_dn[1], conv_w[1], conv_b[1]

    x4, saved1 = _ffn_forward("1", x3, gains[1], w_up[1], late1)
    dx4, dx4_bf, d_fin, loss_part = _loss_head(x4, fin, target)

    dx3, d_fn1, dw_up1, dw_dn1, dcw1, dcb1 = _ffn_backward("1", x3, gains[1], w_up[1], w_dn[1], conv_w[1], conv_b[1], saved1, (dx4, dx4_bf))
    rows = d // N_DEV
    scatter = lambda tag, stacks, carry: _send_start("scatter_start_" + tag, stacks, [lax.empty(a.shape, a.dtype) for a in stacks], True, None, carry)
    going_ffn1 = scatter("ffn1", [dw_up1, dw_dn1.reshape(N_DEV, fs, d)], attn_sinks)
    datt = _mm_rows_nt("attn_out_bwd", dx3[1], w_o, out_dtype=BF16)
    dw_o = _mm_tn("attn_out_grad", att, dx3[1])
    dq, dkv, dsink = _attn_bwd(q, kv, att, datt, going_ffn1[4])
    dw_q = _mm_tn("q_proj_grad", a2, dq)
    dw_kv = _mm_tn("kv_proj_grad", akv, dkv)
    going_attn = scatter("attn", [dw_kv.reshape(N_DEV, rows, kvd), dw_q.reshape(N_DEV, rows, d), dw_o.reshape(N_DEV, rows, d)], kvn)
    whole = lambda a_ref, b_ref: [(a_ref[...], b_ref[...])]
    rows_of = lambda width: (lambda tile: pl.BlockSpec((tile, width), lambda i: (i, 0)))
    dx2, (d_kvn, d_attn) = _proj_norm_bwd("attn_in_bwd", x2, dx3[0], [
        (dkv, rows_of(kvd), w_kvg, pl.BlockSpec((d, kvd), lambda i: (0, 0)), whole, going_attn[4]),
        (dq, rows_of(d), w_q, pl.BlockSpec((d, d), lambda i: (0, 0)), whole, attn_norm)])
    going = {}

    def sent0(dw_dn0, gain):
        going["ffn_dn0"] = scatter("ffn_dn0", [dw_dn0.reshape(N_DEV, fs, d)], gain)
        return going["ffn_dn0"][4]

    dx1, d_fn0, dw_up0, _, dcw0, dcb0 = _ffn_backward("0", x1, gains[0], w_up[0], w_dn[0], conv_w[0], conv_b[0], saved0, dx2, sent0)
    dw_out = _mm_tn("hg_out_grad", og, dx1[1])
    going_ffn0 = scatter("ffn0", [dw_up0, dw_out.reshape(N_DEV, rows, d)], hg_out_norm)
    dog = _mm_rows_nt("hg_out_bwd", dx1[1], w_out, out_dtype=F32)
    dp, d_lbl, d_ogain = _hgrn2_bwd(p, lbl, going_ffn0[4], o, dog, states, gsum)
    dw_in = _matmul(
        "hg_in_grad", a0, dp, dims=TN, grid=(1, N_DEV, 1),
        a_spec=pl.BlockSpec((s, d), lambda i, j, k: (0, 0)),
        b_spec=pl.BlockSpec((None, s, half), lambda i, j, k: (j // 2, 0, j % 2)),
        o_spec=pl.BlockSpec((None, d, half), lambda i, j, k: (j, 0, 0)),
        out_shape=jax.ShapeDtypeStruct((N_DEV, d, half), BF16))
    going_hg = scatter("hg", [dw_in], hgn)
    (dx0, _), (d_hgn,) = _proj_norm_bwd("hg_in_bwd", x0, dx1[0], [
        (dp, lambda tile: pl.BlockSpec((4, tile, d), lambda i: (0, i, 0)), w_in, pl.BlockSpec((N_DEV, d, half), lambda i: (0, 0, 0)),
         lambda g_ref, w_ref: [(g_ref[k // 2, :, (k % 2) * half:(k % 2 + 1) * half], w_ref[k]) for k in range(N_DEV)],
         going_hg[4])])

    as_blocks = lambda a, r: a.reshape(r, N_DEV, -1).transpose(1, 0, 2).reshape(N_DEV * r, -1)
    d_cw = jnp.concatenate([g.transpose(1, 0, 2).reshape(CONV_WIDTH, 4 * fb) for g in (dcw0, dcw1)], axis=0)
    parts = [d_fin, jnp.concatenate([d_fn0, d_fn1], axis=0), jnp.concatenate([dcb0.reshape(1, 4 * fb), dcb1.reshape(1, 4 * fb)], axis=0),
             as_blocks(d_cw, 2 * CONV_WIDTH), d_attn, jnp.sum(dsink[:, :, 0], axis=0).reshape(1, nq), d_kvn, d_ogain,
             as_blocks(d_hgn, 1), as_blocks(d_lbl, 2), loss_part]
    wide = [2]
    packs = [[parts[i] for i in wide], [part for i, part in enumerate(parts) if i not in wide]]
    places = [None] * len(parts)
    for which, members in enumerate([wide, [i for i in range(len(parts)) if i not in wide]]):
        for i, off in zip(members, _pack_rows(packs[which])[0]):
            places[i] = (which, off)
    packed = [_pack("pack_wide_grads", packs[0]), _pack("pack_narrow_grads", packs[1])]
    going_small = _send_start("small_grads_start", packed, [_landing(a, mine) for a in packed], False)

    arrive = lambda tag, going, after: _send_wait("scatter_wait_" + tag, going, after, True)
    (l_up1, l_dn1), (l_kv, l_q, l_o), (l_dn0,), (l_up0, l_out) = (
        arrive("ffn1", going_ffn1, going_small[4]), arrive("attn", going_attn, going_small[4]),
        arrive("ffn_dn0", going["ffn_dn0"], going_small[4]), arrive("ffn0", going_ffn0, going_small[4]))
    big = {}
    for tag, w, m, v, part in [
            ("w_kv", w_kv, m_w_kv, v_w_kv, l_kv), ("attn_w_q", attn_w_q[0], m_attn_w_q[0], v_attn_w_q[0], l_q),
            ("attn_w_o", attn_w_o[0], m_attn_w_o[0], v_attn_w_o[0], l_o)]:
        big[tag] = _adamw_shard("adamw_" + tag, w, m, v, part)
    up_done = _adamw_layers("adamw_ffn_w_up", up_t(ffn_w_up), up_t(m_ffn_w_up), up_t(v_ffn_w_up), (l_up0, l_up1))
    big["ffn_w_up"] = [up_t(a) for a in up_done]
    big["ffn_w_down"] = _adamw_layers("adamw_ffn_w_down", ffn_w_down, m_ffn_w_down, v_ffn_w_down, (l_dn0, l_dn1))
    lead = lambda tag: [a[None] for a in big[tag]]

    both_done = up_done[0][0, 0:1, 0:1] + big["ffn_w_down"][0][0, 0:1, 0:1]
    gathered = _send_wait("small_grads_wait", going_small, both_done, False)
    two = lambda a: a.reshape(-1, a.shape[-1])
    small = [(fin, m_final_norm.reshape(1, d), v_final_norm.reshape(1, d), False), (ffn_norm, m_ffn_norm, v_ffn_norm, False),
             (ffn_conv_b, m_ffn_conv_b, v_ffn_conv_b, False), (two(ffn_conv_w), two(m_ffn_conv_w), two(v_ffn_conv_w), True),
             (attn_norm, m_attn_norm, v_attn_norm, False), (attn_sinks, m_attn_sinks, v_attn_sinks, False),
             (kvn, m_kv_norm.reshape(1, d), v_kv_norm.reshape(1, d), False), (hg_out_norm, m_hg_out_norm, v_hg_out_norm, False),
             (hg_norm, m_hg_norm, v_hg_norm, True), (hg_lb_logits, m_hg_lb_logits, v_hg_lb_logits, True)]
    res = _adamw_small(gathered, places, small)
    l_in, = arrive("hg", going_hg, gathered[1])
    big["hg_w_in"] = _adamw_shard("adamw_hg_w_in", hg_w_in[0], m_hg_w_in[0], v_hg_w_in[0], l_in)
    big["hg_w_out"] = _adamw_shard("adamw_hg_w_out", hg_w_out[0], m_hg_w_out[0], v_hg_w_out[0], l_out)
    names = ["final_norm", "ffn_norm", "ffn_conv_b", "ffn_conv_w", "attn_norm", "attn_sinks", "kv_norm", "hg_out_norm", "hg_norm", "hg_lb_logits"]
    shapes = {"final_norm": final_norm.shape, "kv_norm": kv_norm.shape, "ffn_conv_w": ffn_conv_w.shape}
    out = {n: [a.reshape(shapes[n]) if n in shapes else a for a in res[4 * i:4 * i + 4]] for i, n in enumerate(names)}
    out.update(hg_w_in=lead("hg_w_in"), hg_w_out=lead("hg_w_out"), w_kv=big["w_kv"], attn_w_q=lead("attn_w_q"), attn_w_o=lead("attn_w_o"),
               ffn_w_up=big["ffn_w_up"], ffn_w_down=big["ffn_w_down"])
    order = ["hg_norm", "hg_w_in", "hg_lb_logits", "hg_out_norm", "hg_w_out", "kv_norm", "w_kv", "attn_norm", "attn_w_q", "attn_sinks",
             "attn_w_o", "ffn_norm", "ffn_w_up", "ffn_conv_w", "ffn_conv_b", "ffn_w_down", "final_norm"]
    loss = res[-1][0, 0]
    return (loss, dx0[None], *[out[n][0] for n in order], *[out[n][1] for n in order], *[out[n][2] for n in order], *[out[n][3] for n in order])
```

```python
import functools

import jax
import jax.numpy as jnp
from jax import lax
from jax.experimental import pallas as pl
from jax.experimental.pallas import tpu as pltpu

F32 = jnp.float32
BF16 = jnp.bfloat16

EPS = 1e-6
HG_EXPAND = 128
HG_CHUNK = 32
ATT_HEAD_DIM = 64
ATT_KV_HEADS = 2
WINDOW = 128
CONV_WIDTH = 3
ADAM_LR = 0.001
ADAM_B1 = 0.9
ADAM_B2 = 0.999
ADAM_EPS = 1e-08
ADAM_WD = 0.01
ADAM_STEP = 10

N_DEV = 8
VMEM_LIMIT = 60 * 1024 * 1024
NEG = -1e30

NN = (((1,), (0,)), ((), ()))
NT = (((1,), (1,)), ((), ()))
TN = (((0,), (0,)), ((), ()))
MESH = pl.DeviceIdType.MESH


def _dot(a, b, dims=NN):
    return lax.dot_general(a.astype(BF16), b.astype(BF16), dims, preferred_element_type=F32)


def _sigmoid(x):
    return 0.5 * jnp.tanh(0.5 * x) + 0.5


def _silu(x):
    return x * _sigmoid(x)


def _silu_and_grad(x):
    s = _sigmoid(x)
    return x * s, s * (1.0 + x * (1.0 - s))


def _dsilu(x):
    return _silu_and_grad(x)[1]


def _params(semantics):
    return pltpu.CompilerParams(dimension_semantics=semantics, vmem_limit_bytes=VMEM_LIMIT)


def _row_tile(rows, want=512):
    return min(rows, want)


MM_ROWS = 1024


def _matmul(name, a, b, *, dims, grid, a_spec, b_spec, o_spec, out_shape, add=None, add_spec=None, after=None):
    assert grid[2] == 1

    def body(*refs):
        a_ref, b_ref, o_ref = refs[0], refs[1], refs[-1]
        total = _dot(a_ref[...], b_ref[...], dims)
        if add is not None:
            total = total + refs[2][...]
        o_ref[...] = total.astype(o_ref.dtype)

    in_specs = [a_spec, b_spec] + ([] if add is None else [add_spec]) + ([] if after is None else [pl.BlockSpec(memory_space=pl.ANY)])
    args = (a, b) + (() if add is None else (add,)) + (() if after is None else (after,))
    return pl.pallas_call(
        body, name=name, grid=grid, in_specs=in_specs, out_specs=o_spec, out_shape=out_shape,
        compiler_params=_params(("parallel", "parallel", "arbitrary")),
    )(*args)


def _mm_rows(name, a, w, *, out_dtype, add=None, after=None):
    s, kdim = a.shape
    n = w.shape[1]
    tm = _row_tile(s, MM_ROWS)
    return _matmul(
        name, a, w, dims=NN, grid=(s // tm, 1, 1),
        a_spec=pl.BlockSpec((tm, kdim), lambda i, j, k: (i, 0)),
        b_spec=pl.BlockSpec((kdim, n), lambda i, j, k: (0, 0)),
        o_spec=pl.BlockSpec((tm, n), lambda i, j, k: (i, 0)),
        out_shape=jax.ShapeDtypeStruct((s, n), out_dtype),
        add=add, add_spec=None if add is None else pl.BlockSpec((tm, n), lambda i, j, k: (i, 0)), after=after,
    )


def _mm_rows_nt(name, a, w, *, out_dtype):
    s, n = a.shape
    kdim = w.shape[0]
    tm = _row_tile(s, MM_ROWS)
    return _matmul(
        name, a, w, dims=NT, grid=(s // tm, 1, 1),
        a_spec=pl.BlockSpec((tm, n), lambda i, j, k: (i, 0)),
        b_spec=pl.BlockSpec((kdim, n), lambda i, j, k: (0, 0)),
        o_spec=pl.BlockSpec((tm, kdim), lambda i, j, k: (i, 0)),
        out_shape=jax.ShapeDtypeStruct((s, kdim), out_dtype),
    )


def _mm_tn(name, a, g):
    s, m = a.shape
    n = g.shape[1]
    tn = min(n, 512)
    parts = 4
    rows = s // parts

    def body(a_hbm, g_ref, o_ref, a_vmem, sems):
        first = pl.program_id(0) == 0
        copies = [pltpu.make_async_copy(a_hbm.at[pl.ds(c * rows, rows)], a_vmem.at[pl.ds(c * rows, rows)], sems.at[c]) for c in range(parts)]

        @pl.when(first)
        def _():
            for copy in copies:
                copy.start()

        total = None
        for c in range(parts):
            pl.when(first)(copies[c].wait)
            part = _dot(a_vmem[c * rows:(c + 1) * rows, :], g_ref[c * rows:(c + 1) * rows, :], TN)
            total = part if total is None else total + part
        o_ref[...] = total.astype(o_ref.dtype)

    return pl.pallas_call(
        body, name=name, grid=(n // tn,),
        in_specs=[pl.BlockSpec(memory_space=pl.ANY), pl.BlockSpec((s, tn), lambda j: (0, j))],
        out_specs=pl.BlockSpec((m, tn), lambda j: (0, j)), out_shape=jax.ShapeDtypeStruct((m, n), BF16),
        scratch_shapes=[pltpu.VMEM((s, m), a.dtype), pltpu.SemaphoreType.DMA((parts,))],
        compiler_params=_params(("arbitrary",)),
    )(a, g)


NORM_ROWS = 256


def _norm_proj(name, h, branches, *, tm, nj):
    s, d = h.shape
    n = len(branches)
    rows = min(tm, NORM_ROWS)

    def body(*refs):
        h_ref, gain_refs, w_refs = refs[0], refs[1:1 + n], refs[1 + n:1 + 2 * n]
        a_refs, o_refs = refs[1 + 2 * n:1 + 3 * n], refs[1 + 3 * n:]

        @pl.when(pl.program_id(1) == 0)
        def _():
            def normalize(c, carry):
                at = pl.ds(pl.multiple_of(c * rows, rows), rows)
                xv = h_ref[at, :]
                xhat = xv * lax.rsqrt(jnp.mean(xv * xv, axis=-1, keepdims=True) + EPS)
                for gain_ref, a_ref in zip(gain_refs, a_refs):
                    a_ref[at, :] = (xhat * gain_ref[...]).astype(BF16)
                return carry

            lax.fori_loop(0, tm // rows, normalize, 0)

        for branch, w_ref, a_ref, o_ref in zip(branches, w_refs, a_refs, o_refs):
            o_ref[...] = _dot(a_ref[...], w_ref[...], branch[3]).astype(o_ref.dtype)

    row = pl.BlockSpec((tm, d), lambda i, j: (i, 0))
    vec = pl.BlockSpec((1, d), lambda i, j: (0, 0))
    outs = pl.pallas_call(
        body, name=name, grid=(s // tm, nj), in_specs=[row] + [vec] * n + [b[2] for b in branches],
        out_specs=[row] * n + [b[4] for b in branches],
        out_shape=[jax.ShapeDtypeStruct((s, d), BF16)] * n + [b[5] for b in branches],
        compiler_params=_params(("parallel", "arbitrary")),
    )(h, *[b[0] for b in branches], *[b[1] for b in branches])
    return outs[:n], outs[n:]


def _proj_norm_bwd(name, h, dres, branches):
    s, d = h.shape
    tm = _row_tile(s)
    n = len(branches)

    def body(*refs):
        h_ref, dres_ref = refs[0], refs[1]
        g_refs, w_refs, gain_refs = refs[2:2 + n], refs[2 + n:2 + 2 * n], refs[2 + 2 * n:2 + 3 * n]
        dh_ref, dhb_ref, dg_refs = refs[2 + 3 * n], refs[3 + 3 * n], refs[4 + 3 * n:]
        i = pl.program_id(0)
        xv = h_ref[...]
        r = lax.rsqrt(jnp.mean(xv * xv, axis=-1, keepdims=True) + EPS)
        xhat = xv * r
        total = dres_ref[...]
        for branch, g_ref, w_ref, gain_ref, dg_ref in zip(branches, g_refs, w_refs, gain_refs, dg_refs):
            pairs = branch[4](g_ref, w_ref)
            da = _dot(*pairs[0], NT)
            for pair in pairs[1:]:
                da = da + _dot(*pair, NT)
            dgain = jnp.sum(da * xhat, axis=0, keepdims=True)

            @pl.when(i == 0)
            def _():
                dg_ref[...] = dgain

            @pl.when(i > 0)
            def _():
                dg_ref[...] += dgain

            dxhat = da * gain_ref[...]
            total = total + r * (dxhat - xhat * jnp.mean(dxhat * xhat, axis=-1, keepdims=True))
        dh_ref[...] = total
        dhb_ref[...] = total.astype(BF16)

    row = pl.BlockSpec((tm, d), lambda i: (i, 0))
    vec = pl.BlockSpec((1, d), lambda i: (0, 0))
    outs = pl.pallas_call(
        body, name=name, grid=(s // tm,),
        in_specs=[row, row] + [b[1](tm) for b in branches] + [b[3] for b in branches] + [vec] * n, out_specs=[row, row] + [vec] * n,
        out_shape=[jax.ShapeDtypeStruct((s, d), F32), jax.ShapeDtypeStruct((s, d), BF16)] + [jax.ShapeDtypeStruct((1, d), F32)] * n,
        compiler_params=_params(("arbitrary",)),
    )(h, dres, *[b[0] for b in branches], *[b[2] for b in branches], *[b[5] for b in branches])
    return (outs[0], outs[1]), outs[2:]


def _loss_head(h, gain, target):
    s, d = h.shape
    tm = _row_tile(s)

    def body(h_ref, g_ref, t_ref, dh_ref, dhb_ref, dg_ref, loss_ref):
        i = pl.program_id(0)
        xv = h_ref[...]
        r = lax.rsqrt(jnp.mean(xv * xv, axis=-1, keepdims=True) + EPS)
        xhat = xv * r
        err = xhat * g_ref[...] - t_ref[...]
        dy = err * (1.0 / d)
        part = jnp.zeros((1, 128), F32) + 0.5 * jnp.sum(jnp.mean(err * err, axis=-1, keepdims=True))
        dgain = jnp.sum(dy * xhat, axis=0, keepdims=True)

        @pl.when(i == 0)
        def _():
            dg_ref[...] = dgain
            loss_ref[...] = part

        @pl.when(i > 0)
        def _():
            dg_ref[...] += dgain
            loss_ref[...] += part

        dxhat = dy * g_ref[...]
        dh = r * (dxhat - xhat * jnp.mean(dxhat * xhat, axis=-1, keepdims=True))
        dh_ref[...] = dh
        dhb_ref[...] = dh.astype(BF16)

    row = pl.BlockSpec((tm, d), lambda i: (i, 0))
    vec = pl.BlockSpec((1, d), lambda i: (0, 0))
    return pl.pallas_call(
        body, name="loss_head", grid=(s // tm,), in_specs=[row, vec, row],
        out_specs=[row, row, vec, pl.BlockSpec((1, 128), lambda i: (0, 0))],
        out_shape=[jax.ShapeDtypeStruct((s, d), F32), jax.ShapeDtypeStruct((s, d), BF16), jax.ShapeDtypeStruct((1, d), F32),
                   jax.ShapeDtypeStruct((1, 128), F32)],
        compiler_params=_params(("arbitrary",)),
    )(h, gain, target)


def _bdot(a, b, ca, cb):
    return lax.dot_general(a.astype(BF16), b.astype(BF16), (((ca,), (cb,)), ((0,), (0,))), preferred_element_type=F32)


def _chunk_cumsum(xv, reverse=False):
    n = xv.shape[0]
    row = lax.broadcasted_iota(jnp.int32, xv.shape, 0) % HG_CHUNK
    step = 1
    while step < HG_CHUNK:
        if reverse:
            xv = xv + jnp.where(row < HG_CHUNK - step, pltpu.roll(xv, n - step, axis=0), 0.0)
        else:
            xv = xv + jnp.where(row >= step, pltpu.roll(xv, step, axis=0), 0.0)
        step *= 2
    return xv


def _hg_terms(p_ref, lbl_ref, g_ref=None):
    pq = p_ref[0].astype(F32)
    pf = p_ref[1].astype(F32)
    lb = _sigmoid(lbl_ref[0:1, :] - lbl_ref[1:2, :])
    sig = _sigmoid(pf)
    fg = lb + (1.0 - lb) * sig
    nc = pq.shape[0] // HG_CHUNK
    chunks = lambda a: a.reshape(nc, HG_CHUNK, HG_EXPAND)
    q = chunks(_silu(pq) * HG_EXPAND ** -0.5)
    k = chunks(1.0 - fg)
    v = chunks(p_ref[2].astype(F32))
    g = chunks(_chunk_cumsum(jnp.log(fg)) if g_ref is None else g_ref[...])
    gm = g[:, HG_CHUNK // 2 - 1:HG_CHUNK // 2, :]
    gl = g[:, HG_CHUNK - 1:HG_CHUNK, :]
    e_mid, e_inv, e_all, e_end = jnp.exp(g - gm), jnp.exp(gm - g), jnp.exp(g), jnp.exp(gl - g)
    terms = dict(q=q, k=k, v=v, g=g, qd=q * e_all, qt=q * e_mid, kt=k * e_inv, kd=k * e_end, e_last=jnp.exp(gl),
                 e_mid=e_mid, e_inv=e_inv, e_all=e_all, e_end=e_end)
    return terms, (pq, sig, fg, lb)


def _causal(nc):
    r = lax.broadcasted_iota(jnp.int32, (nc, HG_CHUNK, HG_CHUNK), 1)
    c = lax.broadcasted_iota(jnp.int32, (nc, HG_CHUNK, HG_CHUNK), 2)
    return r >= c


def _hgrn2_fwd(p, lb_logits, out_gain):
    _, s, d = p.shape
    heads = d // HG_EXPAND
    t = _row_tile(s, 2048)
    nc = t // HG_CHUNK

    def body(p_ref, lbl_ref, gain_ref, o_ref, og_ref, st_ref, g_ref, state, decay):
        @pl.when(pl.program_id(1) == 0)
        def _():
            state[...] = jnp.zeros_like(state)

        tm, _ = _hg_terms(p_ref, lbl_ref)
        g_ref[...] = tm["g"].reshape(t, HG_EXPAND)
        decay[...] = tm["e_last"]
        st_ref[...] = _bdot(tm["v"], tm["kd"], 1, 1)

        def chunk(c, carry):
            add = st_ref[c]
            st = state[...]
            st_ref[c] = st
            state[...] = st * decay[c] + add
            return carry

        lax.fori_loop(0, nc, chunk, 0)
        a = jnp.where(_causal(nc), _bdot(tm["qt"], tm["kt"], 2, 2), 0.0)
        ov = (_bdot(tm["qd"], st_ref[...], 2, 2) + _bdot(a, tm["v"], 2, 1)).reshape(t, HG_EXPAND)
        o_ref[...] = ov
        on = ov * lax.rsqrt(jnp.mean(ov * ov, axis=-1, keepdims=True) + EPS) * gain_ref[...]
        og_ref[...] = (on * _silu(p_ref[3].astype(F32))).astype(BF16)

    blk = pl.BlockSpec((t, HG_EXPAND), lambda h, b: (b, h))
    return pl.pallas_call(
        body, name="hgrn2_fwd", grid=(heads, s // t),
        in_specs=[pl.BlockSpec((4, t, HG_EXPAND), lambda h, b: (0, b, h)), pl.BlockSpec((2, HG_EXPAND), lambda h, b: (0, h)),
                  pl.BlockSpec((1, HG_EXPAND), lambda h, b: (0, 0))],
        out_specs=[blk, blk, pl.BlockSpec((None, nc, HG_EXPAND, HG_EXPAND), lambda h, b: (h, b, 0, 0)), blk],
        out_shape=[jax.ShapeDtypeStruct((s, d), F32), jax.ShapeDtypeStruct((s, d), BF16),
                   jax.ShapeDtypeStruct((heads, s // HG_CHUNK, HG_EXPAND, HG_EXPAND), F32), jax.ShapeDtypeStruct((s, d), F32)],
        scratch_shapes=[pltpu.VMEM((HG_EXPAND, HG_EXPAND), F32), pltpu.VMEM((nc, 1, HG_EXPAND), F32)],
        compiler_params=_params(("parallel", "arbitrary")),
    )(p, lb_logits, out_gain)


def _hgrn2_bwd(p, lb_logits, out_gain, o, dog, states, gsum):
    _, s, d = p.shape
    heads = d // HG_EXPAND
    t = _row_tile(s, 1024)
    nc = t // HG_CHUNK
    nb = s // t

    def body(p_ref, lbl_ref, gain_ref, o_ref, dog_ref, st_ref, g_ref, dp_ref, dlbl_ref, dgain_ref, dstate, decay, dst_s):
        h, b = pl.program_id(0), pl.program_id(1)

        @pl.when(b == 0)
        def _():
            dstate[...] = jnp.zeros_like(dstate)

        tm, (pq, sig, fg, lb) = _hg_terms(p_ref, lbl_ref, g_ref)
        pg = p_ref[3].astype(F32)
        ov = o_ref[...]
        r = lax.rsqrt(jnp.mean(ov * ov, axis=-1, keepdims=True) + EPS)
        ohat = ov * r
        dogv = dog_ref[...]
        d_on = dogv * _silu(pg)
        dp_ref[3] = (dogv * ohat * gain_ref[...] * _dsilu(pg)).astype(BF16)
        dgain = jnp.sum(d_on * ohat, axis=0, keepdims=True)

        @pl.when((h == 0) & (b == 0))
        def _():
            dgain_ref[...] = dgain

        @pl.when((h > 0) | (b > 0))
        def _():
            dgain_ref[...] += dgain

        dohat = d_on * gain_ref[...]
        do = (r * (dohat - ohat * jnp.mean(dohat * ohat, axis=-1, keepdims=True))).reshape(nc, HG_CHUNK, HG_EXPAND)

        decay[...] = tm["e_last"]
        dst_s[...] = _bdot(do, tm["qd"], 1, 1)

        def chunk(i, carry):
            c = nc - 1 - i
            add = dst_s[c]
            dst = dstate[...]
            dst_s[c] = dst
            dstate[...] = dst * decay[c] + add
            return carry

        lax.fori_loop(0, nc, chunk, 0)
        st, dst = st_ref[...], dst_s[...]
        causal = _causal(nc)
        a = jnp.where(causal, _bdot(tm["qt"], tm["kt"], 2, 2), 0.0)
        da = jnp.where(causal, _bdot(do, tm["v"], 2, 2), 0.0)
        dqt = _bdot(da, tm["kt"], 2, 1)
        dkt = _bdot(da, tm["qt"], 1, 1)
        dqd = _bdot(do, st, 2, 1)
        dkd = _bdot(tm["v"], dst, 2, 1)
        dv = _bdot(a, do, 1, 1) + _bdot(tm["kd"], dst, 2, 2)
        dq = dqt * tm["e_mid"] + dqd * tm["e_all"]
        dk = dkt * tm["e_inv"] + dkd * tm["e_end"]
        dg = dqt * tm["qt"] - dkt * tm["kt"] + dqd * tm["qd"] - dkd * tm["kd"]
        dgl = jnp.sum(dkd * tm["kd"], axis=1, keepdims=True) + tm["e_last"] * jnp.sum(dst * st, axis=1, keepdims=True)
        last_row = lax.broadcasted_iota(jnp.int32, (nc, HG_CHUNK, HG_EXPAND), 1) == HG_CHUNK - 1
        flat = lambda a3: a3.reshape(t, HG_EXPAND)
        dlf = _chunk_cumsum(flat(dg + jnp.where(last_row, dgl, 0.0)), reverse=True)
        dfg = dlf / fg - flat(dk)
        dlb = jnp.sum(dfg * (1.0 - sig), axis=0, keepdims=True)
        dl0 = dlb * lb * (1.0 - lb)
        dlbl = jnp.concatenate([dl0, -dl0], axis=0)

        @pl.when(b == 0)
        def _():
            dlbl_ref[...] = dlbl

        @pl.when(b > 0)
        def _():
            dlbl_ref[...] += dlbl

        dp_ref[0] = (flat(dq) * HG_EXPAND ** -0.5 * _dsilu(pq)).astype(BF16)
        dp_ref[1] = (dfg * (1.0 - lb) * sig * (1.0 - sig)).astype(BF16)
        dp_ref[2] = flat(dv).astype(BF16)

    blk = pl.BlockSpec((t, HG_EXPAND), lambda h, b: (nb - 1 - b, h))
    pblk = pl.BlockSpec((4, t, HG_EXPAND), lambda h, b: (0, nb - 1 - b, h))
    return pl.pallas_call(
        body, name="hgrn2_bwd", grid=(heads, nb),
        in_specs=[pblk, pl.BlockSpec((2, HG_EXPAND), lambda h, b: (0, h)), pl.BlockSpec((1, HG_EXPAND), lambda h, b: (0, 0)),
                  blk, blk, pl.BlockSpec((None, nc, HG_EXPAND, HG_EXPAND), lambda h, b: (h, nb - 1 - b, 0, 0)), blk],
        out_specs=[pblk, pl.BlockSpec((2, HG_EXPAND), lambda h, b: (0, h)), pl.BlockSpec((1, HG_EXPAND), lambda h, b: (0, 0))],
        out_shape=[jax.ShapeDtypeStruct((4, s, d), BF16), jax.ShapeDtypeStruct((2, d), F32), jax.ShapeDtypeStruct((1, HG_EXPAND), F32)],
        scratch_shapes=[pltpu.VMEM((HG_EXPAND, HG_EXPAND), F32), pltpu.VMEM((nc, 1, HG_EXPAND), F32),
                        pltpu.VMEM((nc, HG_EXPAND, HG_EXPAND), F32)],
        compiler_params=_params(("arbitrary", "arbitrary")),
    )(p, lb_logits, out_gain, o, dog, states, gsum)


HALO = 8
FFN_FWD_ROWS = 512
FFN_BWD_ROWS = 256


def _shift_down(xv, n):
    return pltpu.roll(xv, n, axis=0)


def _shift_up(xv, n):
    return pltpu.roll(xv, xv.shape[0] - n, axis=0)


def _ffn_hidden_down(name, u, conv_w, conv_b, w_down, h):
    _, nj, s, fb = u.shape
    d = w_down.shape[2]
    tm = _row_tile(s, FFN_FWD_ROWS)
    per = tm // HALO

    def body(gate_ref, prev_ref, val_ref, w_ref, b_ref, wd_ref, h_ref, hid_ref, conv_ref, o_ref):
        i = pl.program_id(0)
        total = h_ref[...]
        for j in range(nj):
            prev = jnp.where(i > 0, prev_ref[j].astype(F32), 0.0)
            ext = jnp.concatenate([prev, gate_ref[j].astype(F32)], axis=0)
            conv = b_ref[j] + w_ref[j, 2:3, :] * ext[HALO:]
            conv = conv + w_ref[j, 1:2, :] * _shift_down(ext, 1)[HALO:]
            conv = conv + w_ref[j, 0:1, :] * _shift_down(ext, 2)[HALO:]
            conv = conv.astype(BF16)
            conv_ref[j] = conv
            hidden = _silu(conv) * val_ref[j]
            hid_ref[j] = hidden
            total = total + _dot(hidden, wd_ref[j])
        o_ref[...] = total

    row = pl.BlockSpec((tm, d), lambda i: (i, 0))
    return pl.pallas_call(
        body, name=name, grid=(s // tm,),
        in_specs=[pl.BlockSpec((None, nj, tm, fb), lambda i: (0, 0, i, 0)),
                  pl.BlockSpec((None, nj, HALO, fb), lambda i: (0, 0, jnp.maximum(i * per - 1, 0), 0)),
                  pl.BlockSpec((None, nj, tm, fb), lambda i: (1, 0, i, 0)),
                  pl.BlockSpec((nj, CONV_WIDTH, fb), lambda i: (0, 0, 0)), pl.BlockSpec((nj, 1, fb), lambda i: (0, 0, 0)),
                  pl.BlockSpec((nj, fb, d), lambda i: (0, 0, 0)), row],
        out_specs=[pl.BlockSpec((nj, tm, fb), lambda i: (0, i, 0)), pl.BlockSpec((nj, tm, fb), lambda i: (0, i, 0)), row],
        out_shape=[jax.ShapeDtypeStruct((nj, s, fb), BF16), jax.ShapeDtypeStruct((nj, s, fb), BF16), jax.ShapeDtypeStruct((s, d), F32)],
        compiler_params=_params(("parallel",)),
    )(u, u, u, conv_w, conv_b, w_down, h)


def _ffn_hidden_up_bwd(name, u, conv, dh, conv_w, w_up, h, gain, dres):
    _, nj, s, fb = u.shape
    d = w_up.shape[2]
    tm = _row_tile(s, FFN_BWD_ROWS)
    per = tm // HALO
    nblk = s // HALO
    ni = s // tm

    def body(gate_ref, conv_ref, cnext_ref, val_ref, vnext_ref, dh_ref, dhnext_ref, w_ref, wu_ref, h_ref, gain_ref, dres_ref,
             du_ref, dw_ref, db_ref, dx_ref, dxb_ref, dgain_ref):
        i = pl.program_id(0)
        has_next = i < ni - 1
        total = None
        for j in range(nj):
            act, dact = _silu_and_grad(conv_ref[j])
            dval = dh_ref[j] * act
            after = jnp.where(has_next, dhnext_ref[j].astype(F32), 0.0) * vnext_ref[j].astype(F32) * _dsilu(cnext_ref[j].astype(F32))
            dconv = jnp.concatenate([(dh_ref[j] * val_ref[j] * dact).astype(F32), after], axis=0)
            taps = [_shift_up(dconv, 2)[:tm], _shift_up(dconv, 1)[:tm], dconv[:tm]]
            dgate = (w_ref[j, 0:1, :] * taps[0] + w_ref[j, 1:2, :] * taps[1] + w_ref[j, 2:3, :] * taps[2]).astype(BF16)
            du_ref[0, j] = dgate
            du_ref[1, j] = dval
            part = _dot(dgate, wu_ref[j]) + _dot(dval, wu_ref[nj + j])
            total = part if total is None else total + part
            gate = gate_ref[j].astype(F32)
            dw = jnp.concatenate([jnp.sum(tap * gate, axis=0, keepdims=True) for tap in taps], axis=0)
            db = jnp.sum(taps[2], axis=0, keepdims=True)

            @pl.when(i == 0)
            def _():
                dw_ref[j] = dw
                db_ref[j] = db

            @pl.when(i > 0)
            def _():
                dw_ref[j] += dw
                db_ref[j] += db

        xv = h_ref[...]
        r = lax.rsqrt(jnp.mean(xv * xv, axis=-1, keepdims=True) + EPS)
        xhat = xv * r
        dgain = jnp.sum(total * xhat, axis=0, keepdims=True)

        @pl.when(i == 0)
        def _():
            dgain_ref[...] = dgain

        @pl.when(i > 0)
        def _():
            dgain_ref[...] += dgain

        dxhat = total * gain_ref[...]
        dx = dres_ref[...] + r * (dxhat - xhat * jnp.mean(dxhat * xhat, axis=-1, keepdims=True))
        dx_ref[...] = dx
        dxb_ref[...] = dx.astype(BF16)

    def tile(part):
        return pl.BlockSpec((None, nj, tm, fb), lambda i: (part, 0, i, 0))

    def after(part):
        return pl.BlockSpec((None, nj, HALO, fb), lambda i: (part, 0, jnp.minimum((i + 1) * per, nblk - 1), 0))

    row = pl.BlockSpec((tm, d), lambda i: (i, 0))
    own = pl.BlockSpec((nj, tm, fb), lambda i: (0, i, 0))
    nxt = pl.BlockSpec((nj, HALO, fb), lambda i: (0, jnp.minimum((i + 1) * per, nblk - 1), 0))
    return pl.pallas_call(
        body, name=name, grid=(ni,),
        in_specs=[tile(0), own, nxt, tile(1), after(1), own, nxt,
                  pl.BlockSpec((nj, CONV_WIDTH, fb), lambda i: (0, 0, 0)),
                  pl.BlockSpec((2 * nj, fb, d), lambda i: (0, 0, 0)), row, pl.BlockSpec((1, d), lambda i: (0, 0)), row],
        out_specs=[pl.BlockSpec((2, nj, tm, fb), lambda i: (0, 0, i, 0)),
                   pl.BlockSpec((nj, CONV_WIDTH, fb), lambda i: (0, 0, 0)), pl.BlockSpec((nj, 1, fb), lambda i: (0, 0, 0)),
                   row, row, pl.BlockSpec((1, d), lambda i: (0, 0))],
        out_shape=[jax.ShapeDtypeStruct((2, nj, s, fb), BF16), jax.ShapeDtypeStruct((nj, CONV_WIDTH, fb), F32),
                   jax.ShapeDtypeStruct((nj, 1, fb), F32), jax.ShapeDtypeStruct((s, d), F32), jax.ShapeDtypeStruct((s, d), BF16),
                   jax.ShapeDtypeStruct((1, d), F32)],
        compiler_params=_params(("arbitrary",)),
    )(u, conv, conv, u, u, dh, dh, conv_w, w_up, h, gain, dres)


ATT_TILE = 512


def _stack_heads(ref, rows, first_head, count):
    hd = ATT_HEAD_DIM
    return jnp.concatenate([ref[rows, (first_head + j) * hd:(first_head + j + 1) * hd] for j in range(count)], axis=0)


def _unstack_heads(stacked, ref, rows, first_head, count):
    hd = ATT_HEAD_DIM
    for pair in range(count // 2):
        both = [stacked[(2 * pair + j) * WINDOW:(2 * pair + j + 1) * WINDOW, :] for j in range(2)]
        ref[rows, (first_head + 2 * pair) * hd:(first_head + 2 * pair + 2) * hd] = jnp.concatenate(both, axis=1).astype(ref.dtype)


def _attn_bias(first_head, count, n_heads, first):
    lanes = count * WINDOW
    ik = lax.broadcasted_iota(jnp.int32, (2 * WINDOW, lanes), 0)
    iq = lax.broadcasted_iota(jnp.int32, (2 * WINDOW, lanes), 1) % WINDOW
    dist = iq + WINDOW - ik
    valid = (dist >= 0) & (dist < WINDOW) & (ik >= (WINDOW if first else 0))
    slope = jnp.concatenate([jnp.zeros((1, WINDOW), F32) + 2.0 ** (-8.0 * (first_head + j + 1) / n_heads) for j in range(count)], axis=1)
    return jnp.where(valid, -slope * dist.astype(F32), NEG)


def _fill_attn_bias(bias_ref, group, n_heads):
    @pl.when(pl.program_id(0) == 0)
    def _():
        for g in range(ATT_KV_HEADS):
            bias_ref[0, g] = _attn_bias(g * group, group, n_heads, False)
            bias_ref[1, g] = _attn_bias(g * group, group, n_heads, True)


def _attn_probs_t(kb_scaled, qs, sink_ref, first_head, count, bias):
    sink = jnp.concatenate([jnp.zeros((1, WINDOW), F32) + sink_ref[0, first_head + j] for j in range(count)], axis=1)
    sc = _dot(kb_scaled, qs, NT) + bias
    m = jnp.maximum(jnp.max(sc, axis=0, keepdims=True), sink)
    e = jnp.exp(sc - m)
    es = jnp.exp(sink - m)
    inv = 1.0 / (jnp.sum(e, axis=0, keepdims=True) + es)
    return e * inv, es * inv


ATT_SCALE = ATT_HEAD_DIM ** -0.5


def _attn_specs(s, d, kvd, tq):
    per = tq // WINDOW
    return [pl.BlockSpec((tq, d), lambda i: (i, 0)), pl.BlockSpec((tq, kvd), lambda i: (i, 0)),
            pl.BlockSpec((WINDOW, kvd), lambda i: (jnp.maximum(i * per - 1, 0), 0))]


def _attn_fwd(q, kv, sinks):
    s, d = q.shape
    kvd = kv.shape[1]
    half = kvd // 2
    hd = ATT_HEAD_DIM
    nq = d // hd
    group = nq // ATT_KV_HEADS
    tq = min(s, ATT_TILE)
    per = tq // WINDOW

    def body(q_ref, kvc_ref, kvp_ref, sink_ref, o_ref, band, bias_ref):
        i = pl.program_id(0)
        _fill_attn_bias(bias_ref, group, nq)
        band[0:WINDOW, :] = kvp_ref[...]
        band[WINDOW:, :] = kvc_ref[...]

        def block(b, carry):
            rows = pl.ds(pl.multiple_of(b * WINDOW, WINDOW), WINDOW)
            keys = pl.ds(pl.multiple_of(b * WINDOW, WINDOW), 2 * WINDOW)
            first = (i * per + b) == 0
            for g in range(ATT_KV_HEADS):
                bias = jnp.where(first, bias_ref[1, g], bias_ref[0, g])
                p, _ = _attn_probs_t(band[keys, g * hd:(g + 1) * hd] * ATT_SCALE, _stack_heads(q_ref, rows, g * group, group), sink_ref,
                                     g * group, group, bias)
                out_t = _dot(band[keys, half + g * hd:half + (g + 1) * hd], p, TN)
                _unstack_heads(out_t.T, o_ref, rows, g * group, group)
            return carry

        lax.fori_loop(0, per, block, 0)

    return pl.pallas_call(
        body, name="attn_fwd", grid=(s // tq,),
        in_specs=_attn_specs(s, d, kvd, tq) + [pl.BlockSpec(memory_space=pltpu.SMEM)],
        out_specs=pl.BlockSpec((tq, d), lambda i: (i, 0)), out_shape=jax.ShapeDtypeStruct((s, d), BF16),
        scratch_shapes=[pltpu.VMEM((tq + WINDOW, kvd), BF16), pltpu.VMEM((2, ATT_KV_HEADS, 2 * WINDOW, group * WINDOW), F32)],
        compiler_params=_params(("arbitrary",)),
    )(q, kv, kv, sinks)


def _attn_bwd(q, kv, o, do, sinks):
    s, d = q.shape
    kvd = kv.shape[1]
    half = kvd // 2
    hd = ATT_HEAD_DIM
    nq = d // hd
    group = nq // ATT_KV_HEADS
    tq = min(s, ATT_TILE)
    per = tq // WINDOW
    nt = s // tq

    def body(q_ref, kvc_ref, kvp_ref, o_ref, do_ref, sink_ref, dq_ref, dkv_ref, ds_ref, band, dband, bias_ref, later):
        step = pl.program_id(0)
        i = nt - 1 - step
        _fill_attn_bias(bias_ref, group, nq)
        band[0:WINDOW, :] = kvp_ref[...]
        band[WINDOW:, :] = kvc_ref[...]
        dband[...] = jnp.zeros_like(dband)
        ds_ref[...] = jnp.zeros_like(ds_ref)

        def block(b, carry):
            rows = pl.ds(pl.multiple_of(b * WINDOW, WINDOW), WINDOW)
            keys = pl.ds(pl.multiple_of(b * WINDOW, WINDOW), 2 * WINDOW)
            first = (i * per + b) == 0
            dks, dvs = [], []
            for g in range(ATT_KV_HEADS):
                kb = band[keys, g * hd:(g + 1) * hd] * ATT_SCALE
                vb = band[keys, half + g * hd:half + (g + 1) * hd]
                qs = _stack_heads(q_ref, rows, g * group, group)
                dos = _stack_heads(do_ref, rows, g * group, group)
                p, ps = _attn_probs_t(kb, qs, sink_ref, g * group, group, jnp.where(first, bias_ref[1, g], bias_ref[0, g]))
                prod = dos.astype(F32) * _stack_heads(o_ref, rows, g * group, group).astype(F32)
                dsum = lax.dot_general(jnp.ones((8, hd), F32), prod, NT, precision=lax.Precision.HIGHEST,
                                       preferred_element_type=F32)[0:1, :]
                dsc = p * (_dot(vb, dos, NT) - dsum)
                dvs.append(_dot(p, dos))
                dks.append(_dot(dsc, qs * ATT_SCALE))
                _unstack_heads(_dot(kb, dsc, TN).T, dq_ref, rows, g * group, group)
                gone = ps * dsum
                for j in range(group):
                    ds_ref[g * group + j:g * group + j + 1, :] += jnp.zeros((1, 128), F32) - jnp.sum(gone[:, j * WINDOW:(j + 1) * WINDOW])
            dband[keys, 0:half] += jnp.concatenate(dks, axis=1)
            dband[keys, half:] += jnp.concatenate(dvs, axis=1)
            return carry

        lax.fori_loop(0, per, block, 0)

        @pl.when(step > 0)
        def _():
            dband[tq:, :] += later[...]

        dkv_ref[...] = dband[WINDOW:, :]
        later[...] = dband[0:WINDOW, :]

    big = pl.BlockSpec((tq, d), lambda i: (nt - 1 - i, 0))
    own = pl.BlockSpec((tq, kvd), lambda i: (nt - 1 - i, 0))
    return pl.pallas_call(
        body, name="attn_bwd", grid=(nt,),
        in_specs=[big, own, pl.BlockSpec((WINDOW, kvd), lambda i: (jnp.maximum((nt - 1 - i) * per - 1, 0), 0)), big, big,
                  pl.BlockSpec(memory_space=pltpu.SMEM)],
        out_specs=[big, own, pl.BlockSpec((None, nq, 128), lambda i: (nt - 1 - i, 0, 0))],
        out_shape=[jax.ShapeDtypeStruct((s, d), BF16), jax.ShapeDtypeStruct((s, kvd), F32), jax.ShapeDtypeStruct((nt, nq, 128), F32)],
        scratch_shapes=[pltpu.VMEM((tq + WINDOW, kvd), BF16), pltpu.VMEM((tq + WINDOW, kvd), F32),
                        pltpu.VMEM((2, ATT_KV_HEADS, 2 * WINDOW, group * WINDOW), F32), pltpu.VMEM((WINDOW, kvd), F32)],
        compiler_params=_params(("arbitrary",)),
    )(q, kv, kv, o, do, sinks)


HBM_SPEC = pl.BlockSpec(memory_space=pltpu.HBM)
VMEM_SPEC = pl.BlockSpec(memory_space=pltpu.VMEM)


def _place():
    return lax.axis_index("x"), lax.axis_index("y"), lax.axis_index("c")


def _flip(pos, r):
    return tuple(1 - p if (r >> (2 - a)) & 1 else p for a, p in enumerate(pos))


def _index(pos):
    return 4 * pos[0] + 2 * pos[1] + pos[2]


def _all_gather(name, shards, spec):
    n = len(shards)

    def body(*refs):
        x_refs, o_refs = refs[:n], refs[n:2 * n]
        send_sems, recv_sems, local_sems = refs[2 * n:]
        me = _place()
        sibling = _flip(me, 1)
        far = [_flip(me, r) for r in (4, 2, 6)]

        def copy(t, sem, block, to, src=None):
            rows = o_refs[t].at[_index(block)]
            return pltpu.make_async_remote_copy(
                src_ref=rows if src is None else src, dst_ref=rows, send_sem=send_sems.at[t, sem], recv_sem=recv_sems.at[t, sem],
                device_id=to, device_id_type=MESH)

        own = [pltpu.make_async_copy(x_refs[t], o_refs[t].at[_index(me)], local_sems.at[t]) for t in range(n)]
        for cp in own:
            cp.start()
        first = []
        for t in range(n):
            first.append(copy(t, 0, me, sibling, src=x_refs[t]))
            first += [copy(t, 1 + j, me, peer, src=x_refs[t]) for j, peer in enumerate(far)]
        for cp in first:
            cp.start()
        passed = []
        for j, peer in enumerate(far):
            for t in range(n):
                copy(t, 1 + j, peer, me).wait_recv()
                cp = copy(t, 4 + j, peer, sibling)
                cp.start()
                passed.append(cp)
        for t in range(n):
            copy(t, 0, sibling, me).wait_recv()
            for j, peer in enumerate(far):
                copy(t, 4 + j, _flip(peer, 1), me).wait_recv()
        for cp in first + passed:
            cp.wait_send()
        for cp in own:
            cp.wait()

    return pl.pallas_call(
        body, name=name, in_specs=[spec] * n, out_specs=[spec] * n,
        out_shape=[jax.ShapeDtypeStruct((N_DEV,) + sh.shape, sh.dtype) for sh in shards],
        scratch_shapes=[pltpu.SemaphoreType.DMA((n, 7)), pltpu.SemaphoreType.DMA((n, 7)), pltpu.SemaphoreType.DMA((n,))],
    )(*shards)


SEM_SPEC = pl.BlockSpec(memory_space=pltpu.SEMAPHORE)
ANY_SPEC = pl.BlockSpec(memory_space=pl.ANY)


def _landing(own, mine):
    return lax.dynamic_update_slice(lax.empty((N_DEV,) + own.shape, own.dtype), own[None], (mine,) + (0,) * own.ndim)


def _peer_copies(src_refs, land_refs, send_sems, recv_sems, scatter, arrivals):
    me = _place()
    mine = _index(me)
    copies = []
    for t, (src, land) in enumerate(zip(src_refs, land_refs)):
        for r in range(1, N_DEV):
            peer = _flip(me, r)
            theirs = _index(peer)
            sem = t * N_DEV + r - 1
            copies.append(pltpu.make_async_remote_copy(
                src_ref=src.at[theirs] if scatter else src, dst_ref=land.at[theirs if arrivals else mine],
                send_sem=send_sems.at[sem], recv_sem=recv_sems.at[sem], device_id=peer, device_id_type=MESH))
    return copies


def _own_copies(src_refs, land_refs, send_sems):
    mine = _index(_place())
    return [pltpu.make_async_copy(src.at[mine], land.at[mine], send_sems.at[t * N_DEV + N_DEV - 1])
            for t, (src, land) in enumerate(zip(src_refs, land_refs))]


def _send_start(name, sources, lands, scatter, after=None, carry=None):
    n = len(sources)
    extra = [a for a in (after, carry) if a is not None]
    token = jax.ShapeDtypeStruct((8, 128), F32) if carry is None else jax.ShapeDtypeStruct(carry.shape, carry.dtype)

    def body(*refs):
        outs = refs[2 * n + len(extra):]
        for out in _peer_copies(refs[:n], refs[n:2 * n], outs[0], outs[1], scatter, False) + (_own_copies(refs[:n], refs[n:2 * n], outs[0]) if scatter else []):
            out.start()
        outs[-1][...] = jnp.zeros_like(outs[-1]) if carry is None else refs[2 * n + len(extra) - 1][...]

    outs = pl.pallas_call(
        body, name=name, in_specs=[HBM_SPEC] * (2 * n) + [ANY_SPEC] * (after is not None) + [VMEM_SPEC] * (carry is not None),
        out_specs=[SEM_SPEC, SEM_SPEC] + [HBM_SPEC] * (2 * n) + [VMEM_SPEC],
        out_shape=[pltpu.SemaphoreType.DMA((n * N_DEV,)), pltpu.SemaphoreType.DMA((n * N_DEV,))]
        + [pltpu.HBM(a.shape, a.dtype) for a in list(sources) + list(lands)] + [token],
        input_output_aliases={i: 2 + i for i in range(2 * n)},
        compiler_params=pltpu.CompilerParams(has_side_effects=pltpu.SideEffectType.DATAFLOW_SIDE_EFFECTING),
    )(*[pltpu.with_memory_space_constraint(a, pltpu.HBM) for a in list(sources) + list(lands)], *extra)
    return outs[0], outs[1], outs[2:2 + n], outs[2 + n:2 + 2 * n], outs[-1]


def _send_wait(name, started, after, scatter):
    send_sems, recv_sems, sources, lands, _ = started
    n = len(sources)

    def body(*refs):
        for out in _peer_copies(refs[:n], refs[n:2 * n], refs[2 * n], refs[2 * n + 1], scatter, False):
            out.wait_send()
        for own in _own_copies(refs[:n], refs[n:2 * n], refs[2 * n]) if scatter else []:
            own.wait()
        for arrival in _peer_copies(refs[:n], refs[n:2 * n], refs[2 * n], refs[2 * n + 1], scatter, True):
            arrival.wait_recv()

    outs = pl.pallas_call(
        body, name=name, in_specs=[HBM_SPEC] * (2 * n) + [SEM_SPEC, SEM_SPEC, ANY_SPEC], out_specs=[HBM_SPEC] * (2 * n),
        out_shape=[pltpu.HBM(a.shape, a.dtype) for a in list(sources) + list(lands)],
        input_output_aliases={i: i for i in range(2 * n)},
        compiler_params=pltpu.CompilerParams(has_side_effects=pltpu.SideEffectType.DATAFLOW_SIDE_EFFECTING),
    )(*sources, *lands, send_sems, recv_sems, after)
    return outs[n:]


def _pack_rows(parts):
    offsets, row = [], 0
    for part in parts:
        offsets.append(row)
        row += part.shape[0]
    return offsets, -(-row // 8) * 8, -(-max(part.shape[1] for part in parts) // 128) * 128


def _pack(name, parts):
    offsets, rows, width = _pack_rows(parts)

    def body(*refs):
        o_ref = refs[-1]
        o_ref[...] = jnp.zeros_like(o_ref)
        for off, ref in zip(offsets, refs[:-1]):
            o_ref[off:off + ref.shape[0], 0:ref.shape[1]] = ref[...]

    return pl.pallas_call(body, name=name, in_specs=[VMEM_SPEC] * len(parts), out_specs=VMEM_SPEC,
                          out_shape=jax.ShapeDtypeStruct((rows, width), F32))(*parts)


def _adamw_math(w, g, m, v):
    m = ADAM_B1 * m + (1.0 - ADAM_B1) * g
    v = ADAM_B2 * v + (1.0 - ADAM_B2) * (g * g)
    m_hat = m * (1.0 / (1.0 - ADAM_B1 ** ADAM_STEP))
    denom = jnp.sqrt(v * (1.0 / (1.0 - ADAM_B2 ** ADAM_STEP))) + ADAM_EPS
    inv = pl.reciprocal(denom, approx=True)
    inv = inv * (2.0 - denom * inv)
    return -ADAM_LR * (m_hat * inv + ADAM_WD * w), m, v


def _adamw_step(w_ref, m_ref, v_ref, p_ref, g_ref, d_ref, nm_ref, nv_ref):
    g = p_ref[0].astype(F32)
    for dev in range(1, N_DEV):
        g = g + p_ref[dev].astype(F32)
    g_ref[...] = g
    d_ref[...], nm_ref[...], nv_ref[...] = _adamw_math(w_ref[...], g, m_ref[...], v_ref[...])


def _adamw_rows(rows):
    return max(t for t in range(8, min(rows, 256) + 1, 8) if rows % t == 0)


def _adamw_shard(name, w, m, v, partials):
    rows, cols = w.shape
    tr = _adamw_rows(rows)
    blk = pl.BlockSpec((tr, cols), lambda i: (i, 0))
    return pl.pallas_call(
        _adamw_step_fn(), name=name, grid=(rows // tr,), in_specs=[blk, blk, blk, pl.BlockSpec((N_DEV, tr, cols), lambda i: (0, i, 0))],
        out_specs=[blk] * 4, out_shape=[jax.ShapeDtypeStruct((rows, cols), F32)] * 4, compiler_params=_params(("parallel",)),
    )(w, m, v, partials)


def _adamw_step_fn():
    return functools.partial(_adamw_step)


def _adamw_layers(name, w, m, v, partials):
    layers, rows, cols = w.shape
    tr = _adamw_rows(rows)
    last = rows // tr - 1

    def body(w_ref, m_ref, v_ref, *rest):
        for layer in range(layers):
            @pl.when(pl.program_id(0) == layer)
            def _():
                _adamw_step(w_ref, m_ref, v_ref, rest[layer], *rest[layers:])

    blk = pl.BlockSpec((None, tr, cols), lambda l, i: (l, i, 0))
    part = lambda layer: pl.BlockSpec((N_DEV, tr, cols), lambda l, i: (0, jnp.where(l == layer, i, jnp.where(l < layer, 0, last)), 0))
    return pl.pallas_call(
        body, name=name, grid=(layers, rows // tr), in_specs=[blk, blk, blk] + [part(layer) for layer in range(layers)],
        out_specs=[blk] * 4, out_shape=[jax.ShapeDtypeStruct(w.shape, F32)] * 4, compiler_params=_params(("arbitrary", "arbitrary")),
    )(w, m, v, *partials)


def _adamw_small(gathered, places, entries):
    n = len(entries)
    np_ = len(gathered)

    def body(*refs):
        pack_refs = refs[:np_]
        refs = refs[np_ - 1:]
        w_refs, m_refs, v_refs = refs[1:1 + n], refs[1 + n:1 + 2 * n], refs[1 + 2 * n:1 + 3 * n]
        outs = refs[1 + 3 * n:]
        totals = []
        for pack_ref in pack_refs:
            acc = pack_ref[0]
            for dev in range(1, N_DEV):
                acc = acc + pack_ref[dev]
            totals.append(acc)
        mine = _index(_place())
        for e in range(n):
            rows, cols = w_refs[e].shape
            total, off = totals[places[e][0]], places[e][1]
            if entries[e][3]:
                g = jnp.zeros((rows, cols), F32)
                for dev in range(N_DEV):
                    g = g + jnp.where(mine == dev, total[off + dev * rows:off + (dev + 1) * rows, 0:cols], 0.0)
            else:
                g = total[off:off + rows, 0:cols]
            outs[4 * e][...] = g
            outs[4 * e + 1][...], outs[4 * e + 2][...], outs[4 * e + 3][...] = _adamw_math(w_refs[e][...], g, m_refs[e][...], v_refs[e][...])
        outs[4 * n][...] = totals[places[n][0]][places[n][1]:places[n][1] + 1, 0:128]

    shapes = []
    for w, _, _, _ in entries:
        shapes += [jax.ShapeDtypeStruct(w.shape, F32)] * 4
    shapes.append(jax.ShapeDtypeStruct((1, 128), F32))
    return pl.pallas_call(
        body, name="adamw_small", in_specs=[VMEM_SPEC] * (np_ + 3 * n), out_specs=[VMEM_SPEC] * len(shapes), out_shape=shapes,
        compiler_params=pltpu.CompilerParams(vmem_limit_bytes=VMEM_LIMIT),
    )(*gathered, *[e[0] for e in entries], *[e[1] for e in entries], *[e[2] for e in entries])


def _ffn_forward(tag, h, gain, w_up, late):
    s, d = h.shape
    fb = w_up.shape[1]
    tm = _row_tile(s, 2 * MM_ROWS)
    (a,), (u,) = _norm_proj(f"ffn_up_{tag}", h, [
        (gain, w_up, pl.BlockSpec((None, fb, d), lambda i, j: (j, 0, 0)), NT,
         pl.BlockSpec((None, None, tm, fb), lambda i, j: (j // 4, j % 4, i, 0)), jax.ShapeDtypeStruct((2, 4, s, fb), BF16))], tm=tm, nj=N_DEV)
    w_down, conv_w, conv_b = late(u)
    hidden, conv, out = _ffn_hidden_down(f"ffn_hidden_down_{tag}", u, conv_w, conv_b, w_down, h)
    return out, (a, u, hidden, conv)


def _ffn_backward(tag, h, gain, w_up, w_down, conv_w, conv_b, saved, dout, sent=None):
    a, u, hidden, conv = saved
    dout, dout_bf = dout
    s, d = h.shape
    fb = w_up.shape[1]
    tm = _row_tile(s, MM_ROWS)
    dhidden = _matmul(
        f"ffn_down_bwd_{tag}", dout_bf, w_down, dims=NT, grid=(s // tm, 4, 1),
        a_spec=pl.BlockSpec((tm, d), lambda i, j, k: (i, 0)),
        b_spec=pl.BlockSpec((None, fb, d), lambda i, j, k: (j, 0, 0)),
        o_spec=pl.BlockSpec((None, tm, fb), lambda i, j, k: (j, i, 0)),
        out_shape=jax.ShapeDtypeStruct((4, s, fb), BF16))
    dw_down = _matmul(
        f"ffn_down_grad_{tag}", hidden, dout_bf, dims=TN, grid=(4, 1, 1),
        a_spec=pl.BlockSpec((None, s, fb), lambda i, j, k: (i, 0, 0)),
        b_spec=pl.BlockSpec((s, d), lambda i, j, k: (0, 0)),
        o_spec=pl.BlockSpec((None, fb, d), lambda i, j, k: (i, 0, 0)),
        out_shape=jax.ShapeDtypeStruct((4, fb, d), BF16))
    if sent is not None:
        gain = sent(dw_down, gain)
    du, dconv_w, dconv_b, dh, dh_bf, dgain = _ffn_hidden_up_bwd(f"ffn_hidden_up_bwd_{tag}", u, conv, dhidden, conv_w, w_up, h, gain, dout)
    dw_up = _matmul(
        f"ffn_up_grad_{tag}", du, a, dims=TN, grid=(N_DEV, 1, 1),
        a_spec=pl.BlockSpec((None, None, s, fb), lambda i, j, k: (i // 4, i % 4, 0, 0)),
        b_spec=pl.BlockSpec((s, d), lambda i, j, k: (0, 0)),
        o_spec=pl.BlockSpec((None, fb, d), lambda i, j, k: (i, 0, 0)),
        out_shape=jax.ShapeDtypeStruct((N_DEV, fb, d), BF16))
    return (dh, dh_bf), dgain, dw_up, dw_down, dconv_w, dconv_b


def kernel(x, hg_norm, hg_w_in, hg_lb_logits, hg_out_norm, hg_w_out, kv_norm, w_kv, attn_norm, attn_w_q, attn_sinks, attn_w_o, ffn_norm, ffn_w_up, ffn_conv_w, ffn_conv_b, ffn_w_down, final_norm, loss_target, m_hg_norm, m_hg_w_in, m_hg_lb_logits, m_hg_out_norm, m_hg_w_out, m_kv_norm, m_w_kv, m_attn_norm, m_attn_w_q, m_attn_sinks, m_attn_w_o, m_ffn_norm, m_ffn_w_up, m_ffn_conv_w, m_ffn_conv_b, m_ffn_w_down, m_final_norm, v_hg_norm, v_hg_w_in, v_hg_lb_logits, v_hg_out_norm, v_hg_w_out, v_kv_norm, v_w_kv, v_attn_norm, v_attn_w_q, v_attn_sinks, v_attn_w_o, v_ffn_norm, v_ffn_w_up, v_ffn_conv_w, v_ffn_conv_b, v_ffn_w_down, v_final_norm):
    _, s, d = x.shape
    x0, target = x[0], loss_target[0]
    half = hg_w_in.shape[2]
    fs = ffn_conv_w.shape[2]
    fb = 2 * fs
    kvd = w_kv.shape[1]
    nq = d // ATT_HEAD_DIM
    tm = _row_tile(s, MM_ROWS)

    mine = _index(_place())
    gather = lambda tag, shards, after, carry=None: _send_start("gather_start_" + tag, shards, [_landing(a, mine) for a in shards], False, after, carry)
    w_in, g_hgn, g_lbl, w_out = _all_gather("gather_hg", [hg_w_in[0].astype(BF16), hg_norm, hg_lb_logits, hg_w_out[0].astype(BF16)], HBM_SPEC)
    w_out = w_out.reshape(d, d)
    up_t = lambda a: jnp.swapaxes(a, -1, -2)
    coming_up0 = gather("ffn_up0", [up_t(ffn_w_up[0]).astype(BF16)], None, g_hgn.reshape(1, d))
    hgn = coming_up0[4]
    lbl = g_lbl.transpose(1, 0, 2).reshape(2, d)
    conv_b = [ffn_conv_b[layer].reshape(4, 1, fb) for layer in range(2)]
    gains = [ffn_norm[0:1], ffn_norm[1:2]]
    kvn, fin = kv_norm.reshape(1, d), final_norm.reshape(1, d)

    t2 = _row_tile(s, 2 * MM_ROWS)
    (a0,), (p,) = _norm_proj("hg_in", x0, [
        (hgn, w_in, pl.BlockSpec((None, d, half), lambda i, j: (j, 0, 0)), NN,
         pl.BlockSpec((None, t2, half), lambda i, j: (j // 2, i, j % 2)), jax.ShapeDtypeStruct((4, s, d), BF16))], tm=t2, nj=N_DEV)
    o, og, states, gsum = _hgrn2_fwd(p, lbl, hg_out_norm)
    coming_dn0 = gather("ffn_down0", [ffn_conv_w, ffn_w_down[0].astype(BF16)], o)
    x1 = _mm_rows("hg_out", og, w_out, out_dtype=F32, add=x0, after=coming_dn0[4])
    w_up0, = _send_wait("gather_wait_ffn_up0", coming_up0, x1, False)
    coming_attn = gather("attn", [w_kv.astype(BF16), attn_w_q[0].astype(BF16), attn_w_o[0].astype(BF16)], w_up0, gains[0])
    gains[0] = coming_attn[4]
    w_up, w_dn, conv_w, coming = [w_up0, None], [None, None], [], {}

    def late0(u):
        g_cw, w_dn0 = _send_wait("gather_wait_ffn_down0", coming_dn0, u, False)
        w_dn[0] = w_dn0.reshape(4, fb, d)
        conv_w.extend(g_cw[:, layer].reshape(4, 2, CONV_WIDTH, fs).transpose(0, 2, 1, 3).reshape(4, CONV_WIDTH, fb) for layer in range(2))
        coming["up1"] = gather("ffn_up1", [up_t(ffn_w_up[1]).astype(BF16)], w_dn0, conv_b[0])
        return w_dn[0], conv_w[0], coming["up1"][4]

    x2, saved0 = _ffn_forward("0", x1, gains[0], w_up[0], late0)
    w_kvg, w_q, w_o = _send_wait("gather_wait_attn", coming_attn, x2, False)
    w_kvg, w_q, w_o = w_kvg.reshape(d, kvd), w_q.reshape(d, d), w_o.reshape(d, d)
    (akv, a2), (kv, q) = _norm_proj("attn_in", x2, [
        (kvn, w_kvg, pl.BlockSpec((d, kvd), lambda i, j: (0, 0)), NN, pl.BlockSpec((tm, kvd), lambda i, j: (i, 0)), jax.ShapeDtypeStruct((s, kvd), BF16)),
        (attn_norm, w_q, pl.BlockSpec((d, d), lambda i, j: (0, 0)), NN, pl.BlockSpec((tm, d), lambda i, j: (i, 0)), jax.ShapeDtypeStruct((s, d), BF16))],
        tm=tm, nj=1)
    coming_dn1 = gather("ffn_down1", [ffn_w_down[1].astype(BF16)], q, attn_sinks)
    att = _attn_fwd(q, kv, coming_dn1[4])
    x3 = _mm_rows("attn_out", att, w_o, out_dtype=F32, add=x2)
    w_up[1], = _send_wait("gather_wait_ffn_up1", coming["up1"], x3, False)

    def late1(u):
        w_dn[1] = _send_wait("gather_wait_ffn_down1", coming_dn1, u, False)[0].reshape(4, fb, d)
        return w_dn[1], conv_w[1], conv_b[1]

    x4, saved1 = _ffn_forward("1", x3, gains[1], w_up[1], late1)
    dx4, dx4_bf, d_fin, loss_part = _loss_head(x4, fin, target)

    dx3, d_fn1, dw_up1, dw_dn1, dcw1, dcb1 = _ffn_backward("1", x3, gains[1], w_up[1], w_dn[1], conv_w[1], conv_b[1], saved1, (dx4, dx4_bf))
    rows = d // N_DEV
    scatter = lambda tag, stacks, carry: _send_start("scatter_start_" + tag, stacks, [lax.empty(a.shape, a.dtype) for a in stacks], True, None, carry)
    going_ffn1 = scatter("ffn1", [dw_up1, dw_dn1.reshape(N_DEV, fs, d)], attn_sinks)
    datt = _mm_rows_nt("attn_out_bwd", dx3[1], w_o, out_dtype=BF16)
    dw_o = _mm_tn("attn_out_grad", att, dx3[1])
    dq, dkv, dsink = _attn_bwd(q, kv, att, datt, going_ffn1[4])
    dw_q = _mm_tn("q_proj_grad", a2, dq)
    dw_kv = _mm_tn("kv_proj_grad", akv, dkv)
    going_attn = scatter("attn", [dw_kv.reshape(N_DEV, rows, kvd), dw_q.reshape(N_DEV, rows, d), dw_o.reshape(N_DEV, rows, d)], kvn)
    whole = lambda a_ref, b_ref: [(a_ref[...], b_ref[...])]
    rows_of = lambda width: (lambda tile: pl.BlockSpec((tile, width), lambda i: (i, 0)))
    dx2, (d_kvn, d_attn) = _proj_norm_bwd("attn_in_bwd", x2, dx3[0], [
        (dkv, rows_of(kvd), w_kvg, pl.BlockSpec((d, kvd), lambda i: (0, 0)), whole, going_attn[4]),
        (dq, rows_of(d), w_q, pl.BlockSpec((d, d), lambda i: (0, 0)), whole, attn_norm)])
    going = {}

    def sent0(dw_dn0, gain):
        going["ffn_dn0"] = scatter("ffn_dn0", [dw_dn0.reshape(N_DEV, fs, d)], gain)
        return going["ffn_dn0"][4]

    dx1, d_fn0, dw_up0, _, dcw0, dcb0 = _ffn_backward("0", x1, gains[0], w_up[0], w_dn[0], conv_w[0], conv_b[0], saved0, dx2, sent0)
    dw_out = _mm_tn("hg_out_grad", og, dx1[1])
    going_ffn0 = scatter("ffn0", [dw_up0, dw_out.reshape(N_DEV, rows, d)], hg_out_norm)
    dog = _mm_rows_nt("hg_out_bwd", dx1[1], w_out, out_dtype=F32)
    dp, d_lbl, d_ogain = _hgrn2_bwd(p, lbl, going_ffn0[4], o, dog, states, gsum)
    dw_in = _matmul(
        "hg_in_grad", a0, dp, dims=TN, grid=(1, N_DEV, 1),
        a_spec=pl.BlockSpec((s, d), lambda i, j, k: (0, 0)),
        b_spec=pl.BlockSpec((None, s, half), lambda i, j, k: (j // 2, 0, j % 2)),
        o_spec=pl.BlockSpec((None, d, half), lambda i, j, k: (j, 0, 0)),
        out_shape=jax.ShapeDtypeStruct((N_DEV, d, half), BF16))
    going_hg = scatter("hg", [dw_in], hgn)
    (dx0, _), (d_hgn,) = _proj_norm_bwd("hg_in_bwd", x0, dx1[0], [
        (dp, lambda tile: pl.BlockSpec((4, tile, d), lambda i: (0, i, 0)), w_in, pl.BlockSpec((N_DEV, d, half), lambda i: (0, 0, 0)),
         lambda g_ref, w_ref: [(g_ref[k // 2, :, (k % 2) * half:(k % 2 + 1) * half], w_ref[k]) for k in range(N_DEV)],
         going_hg[4])])

    as_blocks = lambda a, r: a.reshape(r, N_DEV, -1).transpose(1, 0, 2).reshape(N_DEV * r, -1)
    d_cw = jnp.concatenate([g.transpose(1, 0, 2).reshape(CONV_WIDTH, 4 * fb) for g in (dcw0, dcw1)], axis=0)
    parts = [d_fin, jnp.concatenate([d_fn0, d_fn1], axis=0), jnp.concatenate([dcb0.reshape(1, 4 * fb), dcb1.reshape(1, 4 * fb)], axis=0),
             as_blocks(d_cw, 2 * CONV_WIDTH), d_attn, jnp.sum(dsink[:, :, 0], axis=0).reshape(1, nq), d_kvn, d_ogain,
             as_blocks(d_hgn, 1), as_blocks(d_lbl, 2), loss_part]
    wide = [2]
    packs = [[parts[i] for i in wide], [part for i, part in enumerate(parts) if i not in wide]]
    places = [None] * len(parts)
    for which, members in enumerate([wide, [i for i in range(len(parts)) if i not in wide]]):
        for i, off in zip(members, _pack_rows(packs[which])[0]):
            places[i] = (which, off)
    packed = [_pack("pack_wide_grads", packs[0]), _pack("pack_narrow_grads", packs[1])]
    going_small = _send_start("small_grads_start", packed, [_landing(a, mine) for a in packed], False)

    arrive = lambda tag, going, after: _send_wait("scatter_wait_" + tag, going, after, True)
    (l_up1, l_dn1), (l_kv, l_q, l_o), (l_dn0,), (l_up0, l_out) = (
        arrive("ffn1", going_ffn1, going_small[4]), arrive("attn", going_attn, going_small[4]),
        arrive("ffn_dn0", going["ffn_dn0"], going_small[4]), arrive("ffn0", going_ffn0, going_small[4]))
    big = {}
    for tag, w, m, v, part in [
            ("w_kv", w_kv, m_w_kv, v_w_kv, l_kv), ("attn_w_q", attn_w_q[0], m_attn_w_q[0], v_attn_w_q[0], l_q),
            ("attn_w_o", attn_w_o[0], m_attn_w_o[0], v_attn_w_o[0], l_o)]:
        big[tag] = _adamw_shard("adamw_" + tag, w, m, v, part)
    up_done = _adamw_layers("adamw_ffn_w_up", up_t(ffn_w_up), up_t(m_ffn_w_up), up_t(v_ffn_w_up), (l_up0, l_up1))
    big["ffn_w_up"] = [up_t(a) for a in up_done]
    big["ffn_w_down"] = _adamw_layers("adamw_ffn_w_down", ffn_w_down, m_ffn_w_down, v_ffn_w_down, (l_dn0, l_dn1))
    lead = lambda tag: [a[None] for a in big[tag]]

    both_done = up_done[0][0, 0:1, 0:1] + big["ffn_w_down"][0][0, 0:1, 0:1]
    gathered = _send_wait("small_grads_wait", going_small, both_done, False)
    two = lambda a: a.reshape(-1, a.shape[-1])
    small = [(fin, m_final_norm.reshape(1, d), v_final_norm.reshape(1, d), False), (ffn_norm, m_ffn_norm, v_ffn_norm, False),
             (ffn_conv_b, m_ffn_conv_b, v_ffn_conv_b, False), (two(ffn_conv_w), two(m_ffn_conv_w), two(v_ffn_conv_w), True),
             (attn_norm, m_attn_norm, v_attn_norm, False), (attn_sinks, m_attn_sinks, v_attn_sinks, False),
             (kvn, m_kv_norm.reshape(1, d), v_kv_norm.reshape(1, d), False), (hg_out_norm, m_hg_out_norm, v_hg_out_norm, False),
             (hg_norm, m_hg_norm, v_hg_norm, True), (hg_lb_logits, m_hg_lb_logits, v_hg_lb_logits, True)]
    res = _adamw_small(gathered, places, small)
    l_in, = arrive("hg", going_hg, gathered[1])
    big["hg_w_in"] = _adamw_shard("adamw_hg_w_in", hg_w_in[0], m_hg_w_in[0], v_hg_w_in[0], l_in)
    big["hg_w_out"] = _adamw_shard("adamw_hg_w_out", hg_w_out[0], m_hg_w_out[0], v_hg_w_out[0], l_out)
    names = ["final_norm", "ffn_norm", "ffn_conv_b", "ffn_conv_w", "attn_norm", "attn_sinks", "kv_norm", "hg_out_norm", "hg_norm", "hg_lb_logits"]
    shapes = {"final_norm": final_norm.shape, "kv_norm": kv_norm.shape, "ffn_conv_w": ffn_conv_w.shape}
    out = {n: [a.reshape(shapes[n]) if n in shapes else a for a in res[4 * i:4 * i + 4]] for i, n in enumerate(names)}
    out.update(hg_w_in=lead("hg_w_in"), hg_w_out=lead("hg_w_out"), w_kv=big["w_kv"], attn_w_q=lead("attn_w_q"), attn_w_o=lead("attn_w_o"),
               ffn_w_up=big["ffn_w_up"], ffn_w_down=big["ffn_w_down"])
    order = ["hg_norm", "hg_w_in", "hg_lb_logits", "hg_out_norm", "hg_w_out", "kv_norm", "w_kv", "attn_norm", "attn_w_q", "attn_sinks",
             "attn_w_o", "ffn_norm", "ffn_w_up", "ffn_conv_w", "ffn_conv_b", "ffn_w_down", "final_norm"]
    loss = res[-1][0, 0]
    return (loss, dx0[None], *[out[n][0] for n in order], *[out[n][1] for n in order], *[out[n][2] for n in order], *[out[n][3] for n in order])
```
